```python
import jax, jax.numpy as jnp
from jax import lax
import numpy as np

D_MODEL = 1024
BATCH = 8
SEQ = 8192
DEPTH = 2

HEAD_DIM = 64
ATTN_HEADS = (D_MODEL // 2) // HEAD_DIM
ATTN_KV_HEADS = ATTN_HEADS // 4
WINDOW = 128
ATTN_BLOCK = 128
ROPE_THETA = 10000.0
CONV_CH = D_MODEL // 2
CONV_WIDTH = 3
Q_W = ATTN_HEADS * HEAD_DIM
KV_W = ATTN_KV_HEADS * HEAD_DIM
EVEN_IN_W = Q_W + 2 * KV_W + 3 * CONV_CH
EVEN_MIX_W = Q_W + CONV_CH
DN_HEAD_DIM = 128
DN_HEADS = D_MODEL // DN_HEAD_DIM
DN_W = DN_HEADS * DN_HEAD_DIM
DN_CONV_WIDTH = 4
DN_CHUNK = 64
ODD_IN_W = 4 * DN_W + 2 * DN_HEADS
D_FF = -(-8 * D_MODEL // (3 * 256)) * 256
EPS = 1e-6
N_EVEN = (DEPTH + 1) // 2
N_ODD = DEPTH // 2

kernel_name = 'hybrid_swa_shortconv_gdn_block'


def rms_norm(x, g):
    xf = x.astype(jnp.float32)
    y = xf * lax.rsqrt(jnp.mean(xf * xf, axis=-1, keepdims=True) + EPS) * g.astype(jnp.float32)
    return y.astype(x.dtype)


def l2_norm(x):
    return x * lax.rsqrt(jnp.sum(x * x, axis=-1, keepdims=True) + EPS)


def rope_tables(seq_len):
    inv_freq = ROPE_THETA ** (-jnp.arange(0, HEAD_DIM, 2, dtype=jnp.float32) / HEAD_DIM)
    ang = jnp.arange(seq_len, dtype=jnp.float32)[:, None] * inv_freq[None, :]
    return jnp.cos(ang), jnp.sin(ang)


def apply_rope(x, cos, sin):
    xf = x.astype(jnp.float32)
    half = HEAD_DIM // 2
    x1, x2 = xf[..., :half], xf[..., half:]
    c, s = cos[None, :, None, :], sin[None, :, None, :]
    return jnp.concatenate([x1 * c - x2 * s, x2 * c + x1 * s], axis=-1).astype(x.dtype)


def causal_depthwise_conv(u, w):
    width, ch = w.shape
    return lax.conv_general_dilated(
        u, w[:, None, :].astype(u.dtype), window_strides=(1,), padding=[(width - 1, 0)],
        dimension_numbers=('NWC', 'WIO', 'NWC'), feature_group_count=ch)


def sliding_window_attention(q, k, v, sinks):
    bsz, t, h, d = q.shape
    kvh = k.shape[2]
    grp = h // kvh
    nb = t // ATTN_BLOCK
    qb = q.reshape(bsz, nb, ATTN_BLOCK, kvh, grp, d)

    def band(z):
        pad = jnp.zeros((bsz, ATTN_BLOCK, kvh, d), z.dtype)
        zp = jnp.concatenate([pad, z], axis=1).reshape(bsz, nb + 1, ATTN_BLOCK, kvh, d)
        return jnp.concatenate([zp[:, :-1], zp[:, 1:]], axis=2)

    kb, vb = band(k), band(v)
    s = jnp.einsum('bnqkgd,bnskd->bnkgqs', qb, kb,
                   preferred_element_type=jnp.float32) * (d ** -0.5)
    blk = jnp.arange(nb)[:, None] * ATTN_BLOCK
    q_pos = blk + jnp.arange(ATTN_BLOCK)[None, :]
    k_pos = blk - ATTN_BLOCK + jnp.arange(2 * ATTN_BLOCK)[None, :]
    diff = q_pos[:, :, None] - k_pos[:, None, :]
    valid = (diff >= 0) & (diff < WINDOW) & (k_pos[:, None, :] >= 0)
    s = jnp.where(valid[None, :, None, None], s, -jnp.inf)
    sink = jnp.broadcast_to(sinks.astype(jnp.float32).reshape(kvh, grp)[None, None, :, :, None, None],
                            s.shape[:-1] + (1,))
    p = jax.nn.softmax(jnp.concatenate([s, sink], axis=-1), axis=-1)[..., :-1]
    o = jnp.einsum('bnkgqs,bnskd->bnqkgd', p.astype(v.dtype), vb)
    return o.reshape(bsz, t, h * d)


def attn_conv_mixer(h, w_in, q_gain, k_gain, sinks, conv_w, w_out, cos, sin):
    bsz, t, _ = h.shape
    proj = h @ w_in
    q, k, v, gate_b, gate_c, xin = jnp.split(
        proj, np.cumsum([Q_W, KV_W, KV_W, CONV_CH, CONV_CH]).tolist(), axis=-1)
    q = apply_rope(rms_norm(q.reshape(bsz, t, ATTN_HEADS, HEAD_DIM), q_gain), cos, sin)
    k = apply_rope(rms_norm(k.reshape(bsz, t, ATTN_KV_HEADS, HEAD_DIM), k_gain), cos, sin)
    v = v.reshape(bsz, t, ATTN_KV_HEADS, HEAD_DIM)
    y_attn = sliding_window_attention(q, k, v, sinks)
    y_conv = gate_b * causal_depthwise_conv(gate_c * xin, conv_w)
    return jnp.concatenate([y_attn, y_conv], axis=-1) @ w_out


def chunk_gated_delta_rule(q, k, v, beta, g):
    bsz, t, h, dk = q.shape
    dv = v.shape[-1]
    n = t // DN_CHUNK

    def chunks(z):
        return z.reshape(bsz, n, DN_CHUNK, h, -1).transpose(0, 1, 3, 2, 4)

    q, k, v = chunks(q), chunks(k), chunks(v)
    beta = chunks(beta[..., None])[..., 0]
    gc = jnp.cumsum(chunks(g[..., None])[..., 0], axis=-1)
    idx = jnp.arange(DN_CHUNK)
    lower = idx[:, None] >= idx[None, :]
    strict = idx[:, None] > idx[None, :]
    gamma = jnp.exp(jnp.where(lower, gc[..., :, None] - gc[..., None, :], -jnp.inf))
    kb = k * beta[..., None]
    a_mat = jnp.where(strict, jnp.einsum('bnhcd,bnhsd->bnhcs', kb, k) * gamma, 0.0)
    m_mat = a_mat + jnp.eye(DN_CHUNK, dtype=a_mat.dtype)
    rhs = jnp.concatenate([v * beta[..., None], kb * jnp.exp(gc)[..., None]], axis=-1)
    sol = lax.linalg.triangular_solve(m_mat, rhs, left_side=True, lower=True, unit_diagonal=True)
    u, w = sol[..., :dv], sol[..., dv:]
    attn_intra = jnp.einsum('bnhcd,bnhsd->bnhcs', q, k) * gamma
    q_dec = q * jnp.exp(gc)[..., None]
    k_dec = k * jnp.exp(gc[..., -1:] - gc)[..., None]
    chunk_decay = jnp.exp(gc[..., -1])

    def step(state, xs):
        qd, kd, u_i, w_i, a_i, dec = xs
        v_new = u_i - jnp.einsum('bhcd,bhde->bhce', w_i, state)
        o = jnp.einsum('bhcd,bhde->bhce', qd, state) + jnp.einsum('bhcs,bhse->bhce', a_i, v_new)
        state = state * dec[..., None, None] + jnp.einsum('bhcd,bhce->bhde', kd, v_new)
        return state, o

    xs = tuple(jnp.moveaxis(z, 1, 0) for z in (q_dec, k_dec, u, w, attn_intra, chunk_decay))
    s0 = jnp.zeros((bsz, h, dk, dv), jnp.float32)
    _, o = lax.scan(step, s0, xs)
    return o.transpose(1, 0, 3, 2, 4).reshape(bsz, t, h, dv)


def gated_deltanet_mixer(h, w_in, conv_w, a_log, dt_bias, o_gain, w_out):
    bsz, t, _ = h.shape
    proj = h @ w_in
    qkv, z, b_logit, a_logit = jnp.split(
        proj, [3 * DN_W, 4 * DN_W, 4 * DN_W + DN_HEADS], axis=-1)
    qkv = jax.nn.silu(causal_depthwise_conv(qkv, conv_w)).astype(jnp.float32)
    q, k, v = jnp.split(qkv, 3, axis=-1)
    q = l2_norm(q.reshape(bsz, t, DN_HEADS, DN_HEAD_DIM)) * (DN_HEAD_DIM ** -0.5)
    k = l2_norm(k.reshape(bsz, t, DN_HEADS, DN_HEAD_DIM))
    v = v.reshape(bsz, t, DN_HEADS, DN_HEAD_DIM)
    beta = jax.nn.sigmoid(b_logit.astype(jnp.float32))
    g = -jnp.exp(a_log.astype(jnp.float32)) * jax.nn.softplus(
        a_logit.astype(jnp.float32) + dt_bias.astype(jnp.float32))
    o = chunk_gated_delta_rule(q, k, v, beta, g)
    zf = z.reshape(bsz, t, DN_HEADS, DN_HEAD_DIM).astype(jnp.float32)
    o = rms_norm(o, o_gain) * jax.nn.silu(zf)
    return o.reshape(bsz, t, DN_W).astype(h.dtype) @ w_out


def swiglu_ffn(h, w_gate_up, w_down):
    gate, up = jnp.split(h @ w_gate_up, 2, axis=-1)
    return (jax.nn.silu(gate) * up) @ w_down


def _fwd_setup_inputs(seed: int = 0) -> dict:
    key = jax.random.key(seed)
    ks = jax.random.split(key, 20)
    nrm = lambda k_, shape, scale: jax.random.normal(k_, shape, jnp.float32) * scale
    dt = jnp.exp(jax.random.uniform(ks[12], (N_ODD, DN_HEADS), jnp.float32,
                                    np.log(1e-3), np.log(1e-1)))
    return {
        'x': nrm(ks[0], (BATCH, SEQ, D_MODEL), 1.0),
        'even_norm': 1.0 + nrm(ks[1], (N_EVEN, D_MODEL), 0.02),
        'even_w_in': nrm(ks[2], (N_EVEN, D_MODEL, EVEN_IN_W), D_MODEL ** -0.5),
        'even_q_gain': 1.0 + nrm(ks[3], (N_EVEN, HEAD_DIM), 0.02),
        'even_k_gain': 1.0 + nrm(ks[4], (N_EVEN, HEAD_DIM), 0.02),
        'even_sinks': nrm(ks[5], (N_EVEN, ATTN_HEADS), 0.5),
        'even_conv_w': nrm(ks[6], (N_EVEN, CONV_WIDTH, CONV_CH), CONV_WIDTH ** -0.5),
        'even_w_out': nrm(ks[7], (N_EVEN, EVEN_MIX_W, D_MODEL), EVEN_MIX_W ** -0.5),
        'odd_norm': 1.0 + nrm(ks[8], (N_ODD, D_MODEL), 0.02),
        'odd_w_in': nrm(ks[9], (N_ODD, D_MODEL, ODD_IN_W), D_MODEL ** -0.5),
        'odd_conv_w': nrm(ks[10], (N_ODD, DN_CONV_WIDTH, 3 * DN_W), DN_CONV_WIDTH ** -0.5),
        'odd_a_log': jnp.log(jax.random.uniform(ks[11], (N_ODD, DN_HEADS), jnp.float32, 1.0, 16.0)),
        'odd_dt_bias': dt + jnp.log(-jnp.expm1(-dt)),
        'odd_o_gain': 1.0 + nrm(ks[13], (N_ODD, DN_HEAD_DIM), 0.02),
        'odd_w_out': nrm(ks[14], (N_ODD, DN_W, D_MODEL), DN_W ** -0.5),
        'ffn_norm': 1.0 + nrm(ks[15], (DEPTH, D_MODEL), 0.02),
        'ffn_w_gate_up': nrm(ks[16], (DEPTH, D_MODEL, 2 * D_FF), D_MODEL ** -0.5),
        'ffn_w_down': nrm(ks[17], (DEPTH, D_FF, D_MODEL), D_FF ** -0.5),
    }


def _fwd_reference(x, even_norm, even_w_in, even_q_gain, even_k_gain, even_sinks, even_conv_w,
              even_w_out, odd_norm, odd_w_in, odd_conv_w, odd_a_log, odd_dt_bias, odd_o_gain,
              odd_w_out, ffn_norm, ffn_w_gate_up, ffn_w_down):
    cos, sin = rope_tables(x.shape[1])
    for layer in range(DEPTH):
        i = layer // 2
        if layer % 2 == 0:
            x = x + attn_conv_mixer(rms_norm(x, even_norm[i]), even_w_in[i], even_q_gain[i],
                                    even_k_gain[i], even_sinks[i], even_conv_w[i],
                                    even_w_out[i], cos, sin)
        else:
            x = x + gated_deltanet_mixer(rms_norm(x, odd_norm[i]), odd_w_in[i], odd_conv_w[i],
                                         odd_a_log[i], odd_dt_bias[i], odd_o_gain[i],
                                         odd_w_out[i])
        x = x + swiglu_ffn(rms_norm(x, ffn_norm[layer]), ffn_w_gate_up[layer], ffn_w_down[layer])
    return x


import jax as _jax
import jax.numpy as _jnp

TWIN_FORMAT = 'train_step'
FWD_PARAMS = ['x', 'even_norm', 'even_w_in', 'even_q_gain', 'even_k_gain', 'even_sinks', 'even_conv_w', 'even_w_out', 'odd_norm', 'odd_w_in', 'odd_conv_w', 'odd_a_log', 'odd_dt_bias', 'odd_o_gain', 'odd_w_out', 'ffn_norm', 'ffn_w_gate_up', 'ffn_w_down']
TWIN_WEIGHTS = ['even_norm', 'even_w_in', 'even_q_gain', 'even_k_gain', 'even_sinks', 'even_conv_w', 'even_w_out', 'odd_norm', 'odd_w_in', 'odd_conv_w', 'odd_a_log', 'odd_dt_bias', 'odd_o_gain', 'odd_w_out', 'ffn_norm', 'ffn_w_gate_up', 'ffn_w_down']
TWIN_DIFF_INPUT = 'x'
TWIN_INPUTS = ['x', 'even_norm', 'even_w_in', 'even_q_gain', 'even_k_gain', 'even_sinks', 'even_conv_w', 'even_w_out', 'odd_norm', 'odd_w_in', 'odd_conv_w', 'odd_a_log', 'odd_dt_bias', 'odd_o_gain', 'odd_w_out', 'ffn_norm', 'ffn_w_gate_up', 'ffn_w_down', 'loss_target', 'm_even_norm', 'm_even_w_in', 'm_even_q_gain', 'm_even_k_gain', 'm_even_sinks', 'm_even_conv_w', 'm_even_w_out', 'm_odd_norm', 'm_odd_w_in', 'm_odd_conv_w', 'm_odd_a_log', 'm_odd_dt_bias', 'm_odd_o_gain', 'm_odd_w_out', 'm_ffn_norm', 'm_ffn_w_gate_up', 'm_ffn_w_down', 'v_even_norm', 'v_even_w_in', 'v_even_q_gain', 'v_even_k_gain', 'v_even_sinks', 'v_even_conv_w', 'v_even_w_out', 'v_odd_norm', 'v_odd_w_in', 'v_odd_conv_w', 'v_odd_a_log', 'v_odd_dt_bias', 'v_odd_o_gain', 'v_odd_w_out', 'v_ffn_norm', 'v_ffn_w_gate_up', 'v_ffn_w_down']
TWIN_OUTPUTS = ['loss', 'grad_x', 'grad_even_norm', 'grad_even_w_in', 'grad_even_q_gain', 'grad_even_k_gain', 'grad_even_sinks', 'grad_even_conv_w', 'grad_even_w_out', 'grad_odd_norm', 'grad_odd_w_in', 'grad_odd_conv_w', 'grad_odd_a_log', 'grad_odd_dt_bias', 'grad_odd_o_gain', 'grad_odd_w_out', 'grad_ffn_norm', 'grad_ffn_w_gate_up', 'grad_ffn_w_down', 'delta_even_norm', 'delta_even_w_in', 'delta_even_q_gain', 'delta_even_k_gain', 'delta_even_sinks', 'delta_even_conv_w', 'delta_even_w_out', 'delta_odd_norm', 'delta_odd_w_in', 'delta_odd_conv_w', 'delta_odd_a_log', 'delta_odd_dt_bias', 'delta_odd_o_gain', 'delta_odd_w_out', 'delta_ffn_norm', 'delta_ffn_w_gate_up', 'delta_ffn_w_down', 'new_m_even_norm', 'new_m_even_w_in', 'new_m_even_q_gain', 'new_m_even_k_gain', 'new_m_even_sinks', 'new_m_even_conv_w', 'new_m_even_w_out', 'new_m_odd_norm', 'new_m_odd_w_in', 'new_m_odd_conv_w', 'new_m_odd_a_log', 'new_m_odd_dt_bias', 'new_m_odd_o_gain', 'new_m_odd_w_out', 'new_m_ffn_norm', 'new_m_ffn_w_gate_up', 'new_m_ffn_w_down', 'new_v_even_norm', 'new_v_even_w_in', 'new_v_even_q_gain', 'new_v_even_k_gain', 'new_v_even_sinks', 'new_v_even_conv_w', 'new_v_even_w_out', 'new_v_odd_norm', 'new_v_odd_w_in', 'new_v_odd_conv_w', 'new_v_odd_a_log', 'new_v_odd_dt_bias', 'new_v_odd_o_gain', 'new_v_odd_w_out', 'new_v_ffn_norm', 'new_v_ffn_w_gate_up', 'new_v_ffn_w_down']
TWIN_LEAF_KINDS = {'loss': 'loss', 'grad_x': 'grad_x', 'grad_even_norm': 'grad_w', 'grad_even_w_in': 'grad_w', 'grad_even_q_gain': 'grad_w', 'grad_even_k_gain': 'grad_w', 'grad_even_sinks': 'grad_w', 'grad_even_conv_w': 'grad_w', 'grad_even_w_out': 'grad_w', 'grad_odd_norm': 'grad_w', 'grad_odd_w_in': 'grad_w', 'grad_odd_conv_w': 'grad_w', 'grad_odd_a_log': 'grad_w', 'grad_odd_dt_bias': 'grad_w', 'grad_odd_o_gain': 'grad_w', 'grad_odd_w_out': 'grad_w', 'grad_ffn_norm': 'grad_w', 'grad_ffn_w_gate_up': 'grad_w', 'grad_ffn_w_down': 'grad_w', 'delta_even_norm': 'delta_w', 'delta_even_w_in': 'delta_w', 'delta_even_q_gain': 'delta_w', 'delta_even_k_gain': 'delta_w', 'delta_even_sinks': 'delta_w', 'delta_even_conv_w': 'delta_w', 'delta_even_w_out': 'delta_w', 'delta_odd_norm': 'delta_w', 'delta_odd_w_in': 'delta_w', 'delta_odd_conv_w': 'delta_w', 'delta_odd_a_log': 'delta_w', 'delta_odd_dt_bias': 'delta_w', 'delta_odd_o_gain': 'delta_w', 'delta_odd_w_out': 'delta_w', 'delta_ffn_norm': 'delta_w', 'delta_ffn_w_gate_up': 'delta_w', 'delta_ffn_w_down': 'delta_w', 'new_m_even_norm': 'new_m', 'new_m_even_w_in': 'new_m', 'new_m_even_q_gain': 'new_m', 'new_m_even_k_gain': 'new_m', 'new_m_even_sinks': 'new_m', 'new_m_even_conv_w': 'new_m', 'new_m_even_w_out': 'new_m', 'new_m_odd_norm': 'new_m', 'new_m_odd_w_in': 'new_m', 'new_m_odd_conv_w': 'new_m', 'new_m_odd_a_log': 'new_m', 'new_m_odd_dt_bias': 'new_m', 'new_m_odd_o_gain': 'new_m', 'new_m_odd_w_out': 'new_m', 'new_m_ffn_norm': 'new_m', 'new_m_ffn_w_gate_up': 'new_m', 'new_m_ffn_w_down': 'new_m', 'new_v_even_norm': 'new_v', 'new_v_even_w_in': 'new_v', 'new_v_even_q_gain': 'new_v', 'new_v_even_k_gain': 'new_v', 'new_v_even_sinks': 'new_v', 'new_v_even_conv_w': 'new_v', 'new_v_even_w_out': 'new_v', 'new_v_odd_norm': 'new_v', 'new_v_odd_w_in': 'new_v', 'new_v_odd_conv_w': 'new_v', 'new_v_odd_a_log': 'new_v', 'new_v_odd_dt_bias': 'new_v', 'new_v_odd_o_gain': 'new_v', 'new_v_odd_w_out': 'new_v', 'new_v_ffn_norm': 'new_v', 'new_v_ffn_w_gate_up': 'new_v', 'new_v_ffn_w_down': 'new_v'}


def _forward(args):
    return _fwd_reference(*[args[k] for k in FWD_PARAMS])


def _output_shape():
    def fwd():
        inp = _fwd_setup_inputs(0)
        return _fwd_reference(*[inp[k] for k in FWD_PARAMS])
    out = _jax.eval_shape(fwd)
    return out.shape, out.dtype

N_MICROBATCH = 1
ADAM_LR = 0.001
ADAM_B1 = 0.9
ADAM_B2 = 0.999
ADAM_EPS = 1e-08
ADAM_WD = 0.01
ADAM_STEP = 10
PER_EXAMPLE_BATCH_AXIS = {'x': 0, 'loss_target': 0}
SHARED_INPUTS = []
_WEIGHT_DTYPES = {'even_norm': _jnp.float32, 'even_w_in': _jnp.float32, 'even_q_gain': _jnp.float32, 'even_k_gain': _jnp.float32, 'even_sinks': _jnp.float32, 'even_conv_w': _jnp.float32, 'even_w_out': _jnp.float32, 'odd_norm': _jnp.float32, 'odd_w_in': _jnp.float32, 'odd_conv_w': _jnp.float32, 'odd_a_log': _jnp.float32, 'odd_dt_bias': _jnp.float32, 'odd_o_gain': _jnp.float32, 'odd_w_out': _jnp.float32, 'ffn_norm': _jnp.float32, 'ffn_w_gate_up': _jnp.float32, 'ffn_w_down': _jnp.float32}
MOMENT_SCALE = {'even_norm': 9.796308e+01, 'even_w_in': 1.654948e+00, 'even_q_gain': 6.065159e+00, 'even_k_gain': 6.023919e+00, 'even_sinks': 1.718592e+00, 'even_conv_w': 3.701775e+01, 'even_w_out': 1.545613e+00, 'odd_norm': 2.665493e+01, 'odd_w_in': 6.249017e-01, 'odd_conv_w': 8.004740e-01, 'odd_a_log': 5.999511e+01, 'odd_dt_bias': 5.716587e+01, 'odd_o_gain': 1.866924e+02, 'odd_w_out': 1.668877e+00, 'ffn_norm': 4.937462e+01, 'ffn_w_gate_up': 4.262878e-01, 'ffn_w_down': 6.644298e-01}


def _to_microbatches(a, axis):
    t = _jnp.moveaxis(a, axis, 0)
    t = t.reshape((N_MICROBATCH, t.shape[0] // N_MICROBATCH) + t.shape[1:])
    return _jnp.moveaxis(t, 1, axis + 1)


def setup_inputs(seed: int = 0) -> dict:
    inp = _fwd_setup_inputs(seed)
    key = _jax.random.fold_in(_jax.random.key(seed), 7919)
    shape, _ = _output_shape()
    out = dict(inp)
    out["loss_target"] = _jax.random.normal(_jax.random.fold_in(key, 0), shape, _jnp.float32)
    for i, name in enumerate(TWIN_WEIGHTS):
        w = inp[name].astype(_jnp.float32)
        if MOMENT_SCALE is None:
            s = _jnp.sqrt(_jnp.mean(_jnp.square(w)) + 1e-30)
        else:
            s = MOMENT_SCALE[name]
        km, kv = _jax.random.split(_jax.random.fold_in(key, i + 1))
        out[name] = w
        out["m_" + name] = s * _jax.random.normal(km, w.shape, _jnp.float32)
        out["v_" + name] = (s * s) * _jax.random.uniform(kv, w.shape, _jnp.float32, 0.5, 1.5)
    if N_MICROBATCH > 1:
        for name, axis in PER_EXAMPLE_BATCH_AXIS.items():
            out[name] = _to_microbatches(out[name], axis)
    return {'x': out['x'], 'even_norm': out['even_norm'], 'even_w_in': out['even_w_in'], 'even_q_gain': out['even_q_gain'], 'even_k_gain': out['even_k_gain'], 'even_sinks': out['even_sinks'], 'even_conv_w': out['even_conv_w'], 'even_w_out': out['even_w_out'], 'odd_norm': out['odd_norm'], 'odd_w_in': out['odd_w_in'], 'odd_conv_w': out['odd_conv_w'], 'odd_a_log': out['odd_a_log'], 'odd_dt_bias': out['odd_dt_bias'], 'odd_o_gain': out['odd_o_gain'], 'odd_w_out': out['odd_w_out'], 'ffn_norm': out['ffn_norm'], 'ffn_w_gate_up': out['ffn_w_gate_up'], 'ffn_w_down': out['ffn_w_down'], 'loss_target': out['loss_target'], 'm_even_norm': out['m_even_norm'], 'm_even_w_in': out['m_even_w_in'], 'm_even_q_gain': out['m_even_q_gain'], 'm_even_k_gain': out['m_even_k_gain'], 'm_even_sinks': out['m_even_sinks'], 'm_even_conv_w': out['m_even_conv_w'], 'm_even_w_out': out['m_even_w_out'], 'm_odd_norm': out['m_odd_norm'], 'm_odd_w_in': out['m_odd_w_in'], 'm_odd_conv_w': out['m_odd_conv_w'], 'm_odd_a_log': out['m_odd_a_log'], 'm_odd_dt_bias': out['m_odd_dt_bias'], 'm_odd_o_gain': out['m_odd_o_gain'], 'm_odd_w_out': out['m_odd_w_out'], 'm_ffn_norm': out['m_ffn_norm'], 'm_ffn_w_gate_up': out['m_ffn_w_gate_up'], 'm_ffn_w_down': out['m_ffn_w_down'], 'v_even_norm': out['v_even_norm'], 'v_even_w_in': out['v_even_w_in'], 'v_even_q_gain': out['v_even_q_gain'], 'v_even_k_gain': out['v_even_k_gain'], 'v_even_sinks': out['v_even_sinks'], 'v_even_conv_w': out['v_even_conv_w'], 'v_even_w_out': out['v_even_w_out'], 'v_odd_norm': out['v_odd_norm'], 'v_odd_w_in': out['v_odd_w_in'], 'v_odd_conv_w': out['v_odd_conv_w'], 'v_odd_a_log': out['v_odd_a_log'], 'v_odd_dt_bias': out['v_odd_dt_bias'], 'v_odd_o_gain': out['v_odd_o_gain'], 'v_odd_w_out': out['v_odd_w_out'], 'v_ffn_norm': out['v_ffn_norm'], 'v_ffn_w_gate_up': out['v_ffn_w_gate_up'], 'v_ffn_w_down': out['v_ffn_w_down']}


def _loss(weights, diff, rest, loss_target):
    with _jax.named_scope("forward"):
        args = {**rest, TWIN_DIFF_INPUT: diff, **{k: w.astype(_WEIGHT_DTYPES[k]) for k, w in weights.items()}}
        y = _forward(args)
    with _jax.named_scope("loss_head"):
        err = _jnp.square(y.astype(_jnp.float32) - loss_target)
        return 0.5 * _jnp.sum(_jnp.mean(err, axis=-1)) if err.ndim else 0.5 * err


def _adamw(w, g, m, v):
    m = ADAM_B1 * m + (1.0 - ADAM_B1) * g
    v = ADAM_B2 * v + (1.0 - ADAM_B2) * _jnp.square(g)
    m_hat = m / (1.0 - ADAM_B1 ** ADAM_STEP)
    v_hat = v / (1.0 - ADAM_B2 ** ADAM_STEP)
    delta = -ADAM_LR * (m_hat / (_jnp.sqrt(v_hat) + ADAM_EPS) + ADAM_WD * w)
    return delta, m, v


def reference(x, even_norm, even_w_in, even_q_gain, even_k_gain, even_sinks, even_conv_w, even_w_out, odd_norm, odd_w_in, odd_conv_w, odd_a_log, odd_dt_bias, odd_o_gain, odd_w_out, ffn_norm, ffn_w_gate_up, ffn_w_down, loss_target, m_even_norm, m_even_w_in, m_even_q_gain, m_even_k_gain, m_even_sinks, m_even_conv_w, m_even_w_out, m_odd_norm, m_odd_w_in, m_odd_conv_w, m_odd_a_log, m_odd_dt_bias, m_odd_o_gain, m_odd_w_out, m_ffn_norm, m_ffn_w_gate_up, m_ffn_w_down, v_even_norm, v_even_w_in, v_even_q_gain, v_even_k_gain, v_even_sinks, v_even_conv_w, v_even_w_out, v_odd_norm, v_odd_w_in, v_odd_conv_w, v_odd_a_log, v_odd_dt_bias, v_odd_o_gain, v_odd_w_out, v_ffn_norm, v_ffn_w_gate_up, v_ffn_w_down):
    given = dict(x=x, even_norm=even_norm, even_w_in=even_w_in, even_q_gain=even_q_gain, even_k_gain=even_k_gain, even_sinks=even_sinks, even_conv_w=even_conv_w, even_w_out=even_w_out, odd_norm=odd_norm, odd_w_in=odd_w_in, odd_conv_w=odd_conv_w, odd_a_log=odd_a_log, odd_dt_bias=odd_dt_bias, odd_o_gain=odd_o_gain, odd_w_out=odd_w_out, ffn_norm=ffn_norm, ffn_w_gate_up=ffn_w_gate_up, ffn_w_down=ffn_w_down, loss_target=loss_target, m_even_norm=m_even_norm, m_even_w_in=m_even_w_in, m_even_q_gain=m_even_q_gain, m_even_k_gain=m_even_k_gain, m_even_sinks=m_even_sinks, m_even_conv_w=m_even_conv_w, m_even_w_out=m_even_w_out, m_odd_norm=m_odd_norm, m_odd_w_in=m_odd_w_in, m_odd_conv_w=m_odd_conv_w, m_odd_a_log=m_odd_a_log, m_odd_dt_bias=m_odd_dt_bias, m_odd_o_gain=m_odd_o_gain, m_odd_w_out=m_odd_w_out, m_ffn_norm=m_ffn_norm, m_ffn_w_gate_up=m_ffn_w_gate_up, m_ffn_w_down=m_ffn_w_down, v_even_norm=v_even_norm, v_even_w_in=v_even_w_in, v_even_q_gain=v_even_q_gain, v_even_k_gain=v_even_k_gain, v_even_sinks=v_even_sinks, v_even_conv_w=v_even_conv_w, v_even_w_out=v_even_w_out, v_odd_norm=v_odd_norm, v_odd_w_in=v_odd_w_in, v_odd_conv_w=v_odd_conv_w, v_odd_a_log=v_odd_a_log, v_odd_dt_bias=v_odd_dt_bias, v_odd_o_gain=v_odd_o_gain, v_odd_w_out=v_odd_w_out, v_ffn_norm=v_ffn_norm, v_ffn_w_gate_up=v_ffn_w_gate_up, v_ffn_w_down=v_ffn_w_down)
    weights = {n: given[n] for n in TWIN_WEIGHTS}
    shared = {n: given[n] for n in SHARED_INPUTS}
    per_example = {n: given[n] for n in ['x']}
    grad_fn = _jax.value_and_grad(_loss, argnums=(0, 1))

    def one_microbatch(ex, loss_target):
        ex = dict(ex)
        diff = ex.pop(TWIN_DIFF_INPUT)
        return grad_fn(weights, diff, {**shared, **ex}, loss_target)

    if N_MICROBATCH == 1:
        loss, (grad_w, grad_x) = one_microbatch(per_example, given["loss_target"])
    else:
        def body(carry, xs):
            loss_sum, grad_sum = carry
            l_k, (gw_k, gx_k) = one_microbatch(xs[0], xs[1])
            with _jax.named_scope("update"):
                return (loss_sum + l_k, _jax.tree.map(_jnp.add, grad_sum, gw_k)), gx_k

        init = (_jnp.zeros((), _jnp.float32), _jax.tree.map(_jnp.zeros_like, weights))
        (loss, grad_w), grad_x = _jax.lax.scan(body, init, (per_example, given["loss_target"]))
    with _jax.named_scope("update"):
        delta_w, new_m, new_v = {}, {}, {}
        for n in TWIN_WEIGHTS:
            delta_w[n], new_m[n], new_v[n] = _adamw(weights[n], grad_w[n], given["m_" + n], given["v_" + n])
    return (loss, grad_x, *[grad_w[n] for n in TWIN_WEIGHTS], *[delta_w[n] for n in TWIN_WEIGHTS],
            *[new_m[n] for n in TWIN_WEIGHTS], *[new_v[n] for n in TWIN_WEIGHTS])
```

```python
import functools

import jax
import jax.numpy as jnp
from jax import lax
from jax.experimental import pallas as pl
from jax.experimental.pallas import tpu as pltpu

F32 = jnp.float32
MXU_DTYPE = jnp.bfloat16
HI = lax.Precision.HIGHEST
EPS = 1e-6
N_DEV = 8
D_MODEL = 1024
HEAD_DIM = 64
ATTN_HEADS = 8
KV_HEADS = 2
ATTN_BLOCK = 128
Q_W = 512
KV_W = 128
CONV_CH = 512
EVEN_IN_W = 2304
DN_HEADS = 8
DN_DIM = 128
DN_W = 1024
DN_CHUNK = 64
ODD_IN_W = 4112
ODD_IN_PAD = 4224
D_FF = 2816
NEG = -1e30
VMEM_LIMIT = 56 * 1024 * 1024
ADAM_LR, ADAM_B1, ADAM_B2, ADAM_EPS, ADAM_WD, ADAM_STEP = 0.001, 0.9, 0.999, 1e-08, 0.01, 10
MESH = pl.DeviceIdType.MESH


def _cp(*sem):
    return pltpu.CompilerParams(dimension_semantics=sem, vmem_limit_bytes=VMEM_LIMIT)


def _pick(n, cap):
    best = 128
    for t in range(128, cap + 1, 128):
        if n % t == 0:
            best = t
    return best


def _mx(a, b):
    return jnp.dot(a.astype(MXU_DTYPE), b.astype(MXU_DTYPE), preferred_element_type=F32)


def _mx_nt(a, b):
    return lax.dot_general(a.astype(MXU_DTYPE), b.astype(MXU_DTYPE), (((1,), (1,)), ((), ())),
                           preferred_element_type=F32)


def _mx_tn(a, b):
    return lax.dot_general(a.astype(MXU_DTYPE), b.astype(MXU_DTYPE), (((0,), (0,)), ((), ())),
                           preferred_element_type=F32)


def _hi(a, b):
    return jnp.dot(a, b, precision=HI, preferred_element_type=F32)


def _hi_nt(a, b):
    return lax.dot_general(a, b, (((1,), (1,)), ((), ())), precision=HI, preferred_element_type=F32)


def _hi_tn(a, b):
    return lax.dot_general(a, b, (((0,), (0,)), ((), ())), precision=HI, preferred_element_type=F32)


def _sigmoid(x):
    return 1.0 / (1.0 + jnp.exp(-x))


def _softplus(x):
    return jnp.maximum(x, 0.0) + jnp.log(1.0 + jnp.exp(-jnp.abs(x)))


def mm_nn(a, b, name, res=None, out_dtype=F32, tm=512):
    m, k = a.shape
    _, n = b.shape
    tn = _pick(n, 1536)

    def body(*refs):
        a_ref, b_ref = refs[0], refs[1]
        o_ref = refs[-1]
        acc = _mx(a_ref[...], b_ref[...])
        if res is not None:
            acc = acc + refs[2][...]
        o_ref[...] = acc.astype(o_ref.dtype)

    in_specs = [pl.BlockSpec((tm, k), lambda j, i: (i, 0)), pl.BlockSpec((k, tn), lambda j, i: (0, j))]
    args = [a, b]
    if res is not None:
        in_specs.append(pl.BlockSpec((tm, tn), lambda j, i: (i, j)))
        args.append(res)
    return pl.pallas_call(
        body, name=name, grid=(n // tn, m // tm), in_specs=in_specs,
        out_specs=pl.BlockSpec((tm, tn), lambda j, i: (i, j)),
        out_shape=jax.ShapeDtypeStruct((m, n), out_dtype), compiler_params=_cp("parallel", "parallel"))(*args)


def mm_nt(a, b, name, out_dtype=F32, tm=512):
    m, k = a.shape
    n, _ = b.shape
    tn = _pick(n, 512 if k > 3000 else 1536)

    def body(a_ref, b_ref, o_ref):
        o_ref[...] = _mx_nt(a_ref[...], b_ref[...]).astype(o_ref.dtype)

    return pl.pallas_call(
        body, name=name, grid=(n // tn, m // tm),
        in_specs=[pl.BlockSpec((tm, k), lambda j, i: (i, 0)), pl.BlockSpec((tn, k), lambda j, i: (j, 0))],
        out_specs=pl.BlockSpec((tm, tn), lambda j, i: (i, j)),
        out_shape=jax.ShapeDtypeStruct((m, n), out_dtype), compiler_params=_cp("parallel", "parallel"))(a, b)


def mm_tn(a, b, name, tk=512):
    kk, m = a.shape
    _, n = b.shape
    tm = _pick(m, 1408)
    tn = _pick(n, 1408)

    def body(a_ref, b_ref, o_ref):
        k = pl.program_id(2)
        p = _mx_tn(a_ref[...], b_ref[...])

        @pl.when(k == 0)
        def _():
            o_ref[...] = p

        @pl.when(k > 0)
        def _():
            o_ref[...] += p

    return pl.pallas_call(
        body, name=name, grid=(m // tm, n // tn, kk // tk),
        in_specs=[pl.BlockSpec((tk, tm), lambda i, j, k: (k, i)), pl.BlockSpec((tk, tn), lambda i, j, k: (k, j))],
        out_specs=pl.BlockSpec((tm, tn), lambda i, j, k: (i, j)),
        out_shape=jax.ShapeDtypeStruct((m, n), F32), compiler_params=_cp("parallel", "parallel", "arbitrary"))(a, b)


def rms_fwd(x, g, name, tm=512):
    t, d = x.shape

    def body(x_ref, g_ref, o_ref):
        xv = x_ref[...]
        r = lax.rsqrt(jnp.mean(xv * xv, axis=-1, keepdims=True) + EPS)
        o_ref[...] = (xv * r * g_ref[...]).astype(o_ref.dtype)

    return pl.pallas_call(
        body, name=name, grid=(t // tm,),
        in_specs=[pl.BlockSpec((tm, d), lambda i: (i, 0)), pl.BlockSpec((1, d), lambda i: (0, 0))],
        out_specs=pl.BlockSpec((tm, d), lambda i: (i, 0)),
        out_shape=jax.ShapeDtypeStruct((t, d), MXU_DTYPE), compiler_params=_cp("parallel"))(x, g)


def rms_bwd(x, g, dh, dres, name, tm=512):
    t, d = x.shape

    def body(x_ref, g_ref, dh_ref, dres_ref, dx_ref, dg_ref):
        i = pl.program_id(0)
        xv = x_ref[...]
        r = lax.rsqrt(jnp.mean(xv * xv, axis=-1, keepdims=True) + EPS)
        xh = xv * r
        dhv = dh_ref[...]
        dxh = dhv * g_ref[...]
        dx_ref[...] = dres_ref[...] + r * (dxh - xh * jnp.mean(dxh * xh, axis=-1, keepdims=True))
        part = jnp.sum(dhv * xh, axis=0, keepdims=True)

        @pl.when(i == 0)
        def _():
            dg_ref[...] = part

        @pl.when(i > 0)
        def _():
            dg_ref[...] += part

    row = pl.BlockSpec((tm, d), lambda i: (i, 0))
    one = pl.BlockSpec((1, d), lambda i: (0, 0))
    return pl.pallas_call(
        body, name=name, grid=(t // tm,), in_specs=[row, one, row, row], out_specs=(row, one),
        out_shape=(jax.ShapeDtypeStruct((t, d), F32), jax.ShapeDtypeStruct((1, d), F32)),
        compiler_params=_cp("arbitrary"))(x, g, dh, dres)


def swiglu_fwd(gu, name, tm=256):
    t, n2 = gu.shape
    f = n2 // 2

    def body(g_ref, u_ref, o_ref):
        gv = g_ref[...]
        o_ref[...] = (gv * _sigmoid(gv) * u_ref[...]).astype(o_ref.dtype)

    return pl.pallas_call(
        body, name=name, grid=(t // tm,),
        in_specs=[pl.BlockSpec((tm, f), lambda i: (i, 0)), pl.BlockSpec((tm, f), lambda i: (i, 1))],
        out_specs=pl.BlockSpec((tm, f), lambda i: (i, 0)),
        out_shape=jax.ShapeDtypeStruct((t, f), MXU_DTYPE), compiler_params=_cp("parallel"))(gu, gu)


def swiglu_bwd(gu, da, name, tm=256):
    t, n2 = gu.shape
    f = n2 // 2

    def body(g_ref, u_ref, da_ref, dg_ref, du_ref):
        gv = g_ref[...]
        sg = _sigmoid(gv)
        dav = da_ref[...]
        dg_ref[...] = (dav * u_ref[...] * sg * (1.0 + gv * (1.0 - sg))).astype(dg_ref.dtype)
        du_ref[...] = (dav * gv * sg).astype(du_ref.dtype)

    lo = pl.BlockSpec((tm, f), lambda i: (i, 0))
    hi = pl.BlockSpec((tm, f), lambda i: (i, 1))
    dgate, dup = pl.pallas_call(
        body, name=name, grid=(t // tm,), in_specs=[lo, hi, lo], out_specs=(lo, lo),
        out_shape=(jax.ShapeDtypeStruct((t, f), MXU_DTYPE), jax.ShapeDtypeStruct((t, f), MXU_DTYPE)),
        compiler_params=_cp("parallel"))(gu, gu, da)
    return jnp.concatenate([dgate, dup], axis=-1)


def loss_head(y, target, name, tm=512):
    t, d = y.shape

    def body(y_ref, t_ref, dy_ref, l_ref):
        i = pl.program_id(0)
        e = y_ref[...] - t_ref[...]
        dy_ref[...] = e * (1.0 / d)
        part = jnp.zeros((1, 128), F32) + 0.5 * jnp.sum(jnp.mean(e * e, axis=-1, keepdims=True), axis=0, keepdims=True)

        @pl.when(i == 0)
        def _():
            l_ref[...] = part

        @pl.when(i > 0)
        def _():
            l_ref[...] += part

    row = pl.BlockSpec((tm, d), lambda i: (i, 0))
    return pl.pallas_call(
        body, name=name, grid=(t // tm,), in_specs=[row, row],
        out_specs=(row, pl.BlockSpec((1, 128), lambda i: (0, 0))),
        out_shape=(jax.ShapeDtypeStruct((t, d), F32), jax.ShapeDtypeStruct((1, 128), F32)),
        compiler_params=_cp("arbitrary"))(y, target)


def _rot(x):
    return jnp.concatenate([-x[:, HEAD_DIM // 2:], x[:, :HEAD_DIM // 2]], axis=-1)


def _rot_t(y):
    return jnp.concatenate([y[:, HEAD_DIM // 2:], -y[:, :HEAD_DIM // 2]], axis=-1)


def qk_prep_fwd(proj, q_gain, k_gain, cosf, sinf, name, tm=256):
    t = proj.shape[0]
    nh = ATTN_HEADS + KV_HEADS

    def body(p_ref, qg_ref, kg_ref, c_ref, s_ref, q_ref, k_ref):
        c, s = c_ref[...], s_ref[...]
        outs = []
        for h in range(nh):
            xh = p_ref[:, h * HEAD_DIM:(h + 1) * HEAD_DIM]
            gain = qg_ref[...] if h < ATTN_HEADS else kg_ref[...]
            r = lax.rsqrt(jnp.mean(xh * xh, axis=-1, keepdims=True) + EPS)
            xn = xh * r * gain
            outs.append(xn * c + _rot(xn) * s)
        q_ref[...] = jnp.concatenate(outs[:ATTN_HEADS], axis=-1)
        k_ref[...] = jnp.concatenate(outs[ATTN_HEADS:], axis=-1)

    gspec = pl.BlockSpec((1, HEAD_DIM), lambda i: (0, 0))
    tspec = pl.BlockSpec((tm, HEAD_DIM), lambda i: (i, 0))
    return pl.pallas_call(
        body, name=name, grid=(t // tm,),
        in_specs=[pl.BlockSpec((tm, Q_W + KV_W), lambda i: (i, 0)), gspec, gspec, tspec, tspec],
        out_specs=(pl.BlockSpec((tm, Q_W), lambda i: (i, 0)), pl.BlockSpec((tm, KV_W), lambda i: (i, 0))),
        out_shape=(jax.ShapeDtypeStruct((t, Q_W), F32), jax.ShapeDtypeStruct((t, KV_W), F32)),
        compiler_params=_cp("parallel"))(proj, q_gain, k_gain, cosf, sinf)


def qk_prep_bwd(proj, q_gain, k_gain, cosf, sinf, dq, dk, name, tm=256):
    t = proj.shape[0]
    nh = ATTN_HEADS + KV_HEADS

    def body(p_ref, qg_ref, kg_ref, c_ref, s_ref, dq_ref, dk_ref, o_ref, dqg_ref, dkg_ref):
        i = pl.program_id(0)
        c, s = c_ref[...], s_ref[...]
        outs = []
        dqg = jnp.zeros((1, HEAD_DIM), F32)
        dkg = jnp.zeros((1, HEAD_DIM), F32)
        for h in range(nh):
            xh = p_ref[:, h * HEAD_DIM:(h + 1) * HEAD_DIM]
            if h < ATTN_HEADS:
                gain = qg_ref[...]
                dout = dq_ref[:, h * HEAD_DIM:(h + 1) * HEAD_DIM]
            else:
                gain = kg_ref[...]
                dout = dk_ref[:, (h - ATTN_HEADS) * HEAD_DIM:(h - ATTN_HEADS + 1) * HEAD_DIM]
            r = lax.rsqrt(jnp.mean(xh * xh, axis=-1, keepdims=True) + EPS)
            xhat = xh * r
            dxn = dout * c + _rot_t(dout * s)
            part = jnp.sum(dxn * xhat, axis=0, keepdims=True)
            if h < ATTN_HEADS:
                dqg = dqg + part
            else:
                dkg = dkg + part
            dxh = dxn * gain
            outs.append(r * (dxh - xhat * jnp.mean(dxh * xhat, axis=-1, keepdims=True)))
        o_ref[...] = jnp.concatenate(outs, axis=-1).astype(o_ref.dtype)

        @pl.when(i == 0)
        def _():
            dqg_ref[...] = dqg
            dkg_ref[...] = dkg

        @pl.when(i > 0)
        def _():
            dqg_ref[...] += dqg
            dkg_ref[...] += dkg

    gspec = pl.BlockSpec((1, HEAD_DIM), lambda i: (0, 0))
    tspec = pl.BlockSpec((tm, HEAD_DIM), lambda i: (i, 0))
    return pl.pallas_call(
        body, name=name, grid=(t // tm,),
        in_specs=[pl.BlockSpec((tm, Q_W + KV_W), lambda i: (i, 0)), gspec, gspec, tspec, tspec,
                  pl.BlockSpec((tm, Q_W), lambda i: (i, 0)), pl.BlockSpec((tm, KV_W), lambda i: (i, 0))],
        out_specs=(pl.BlockSpec((tm, Q_W + KV_W), lambda i: (i, 0)), gspec, gspec),
        out_shape=(jax.ShapeDtypeStruct((t, Q_W + KV_W), MXU_DTYPE), jax.ShapeDtypeStruct((1, HEAD_DIM), F32),
                   jax.ShapeDtypeStruct((1, HEAD_DIM), F32)),
        compiler_params=_cp("arbitrary"))(proj, q_gain, k_gain, cosf, sinf, dq, dk)


def _swa_valid(n):
    qi = lax.broadcasted_iota(jnp.int32, (ATTN_BLOCK, 2 * ATTN_BLOCK), 0)
    kj = lax.broadcasted_iota(jnp.int32, (ATTN_BLOCK, 2 * ATTN_BLOCK), 1)
    diff = qi + ATTN_BLOCK - kj
    return (diff >= 0) & (diff < ATTN_BLOCK) & (n * ATTN_BLOCK - ATTN_BLOCK + kj >= 0)


def swa_fwd(q, k, proj, sinks, name):
    t = q.shape[0]
    nb = t // ATTN_BLOCK
    scale = HEAD_DIM ** -0.5
    grp = ATTN_HEADS // KV_HEADS

    def body(q_ref, kc_ref, kp_ref, vc_ref, vp_ref, s_ref, y_ref, lse_ref):
        n = pl.program_id(0)
        valid = _swa_valid(n)
        kk = jnp.concatenate([kp_ref[...], kc_ref[...]], axis=0)
        vv = jnp.concatenate([vp_ref[...], vc_ref[...]], axis=0)
        lane = lax.broadcasted_iota(jnp.int32, (ATTN_BLOCK, ATTN_HEADS), 1)
        lse = jnp.zeros((ATTN_BLOCK, ATTN_HEADS), F32)
        outs = []
        for h in range(ATTN_HEADS):
            kv = h // grp
            qh = q_ref[:, h * HEAD_DIM:(h + 1) * HEAD_DIM]
            kh = kk[:, kv * HEAD_DIM:(kv + 1) * HEAD_DIM]
            vh = vv[:, kv * HEAD_DIM:(kv + 1) * HEAD_DIM]
            sc = jnp.where(valid, _mx_nt(qh, kh) * scale, NEG)
            sink = s_ref[0:1, h:h + 1]
            m = jnp.maximum(jnp.max(sc, axis=-1, keepdims=True), sink)
            e = jnp.exp(sc - m)
            den = jnp.sum(e, axis=-1, keepdims=True) + jnp.exp(sink - m)
            outs.append(_mx(e / den, vh))
            lse = jnp.where(lane == h, m + jnp.log(den), lse)
        y_ref[...] = jnp.concatenate(outs, axis=-1)
        lse_ref[...] = lse

    cur = lambda n: (n, 0)
    prev = lambda n: (jnp.maximum(n - 1, 0), 0)
    vcol = (Q_W + KV_W) // KV_W
    return pl.pallas_call(
        body, name=name, grid=(nb,),
        in_specs=[pl.BlockSpec((ATTN_BLOCK, Q_W), cur), pl.BlockSpec((ATTN_BLOCK, KV_W), cur),
                  pl.BlockSpec((ATTN_BLOCK, KV_W), prev),
                  pl.BlockSpec((ATTN_BLOCK, KV_W), lambda n: (n, vcol)),
                  pl.BlockSpec((ATTN_BLOCK, KV_W), lambda n: (jnp.maximum(n - 1, 0), vcol)),
                  pl.BlockSpec((1, ATTN_HEADS), lambda n: (0, 0))],
        out_specs=(pl.BlockSpec((ATTN_BLOCK, Q_W), cur), pl.BlockSpec((ATTN_BLOCK, ATTN_HEADS), cur)),
        out_shape=(jax.ShapeDtypeStruct((t, Q_W), F32), jax.ShapeDtypeStruct((t, ATTN_HEADS), F32)),
        compiler_params=_cp("parallel"))(q, k, k, proj, proj, sinks)


def swa_bwd(q, k, proj, sinks, y, lse, dmix, name):
    t = q.shape[0]
    nb = t // ATTN_BLOCK
    scale = HEAD_DIM ** -0.5
    grp = ATTN_HEADS // KV_HEADS

    def body(q_ref, kc_ref, kp_ref, vc_ref, vp_ref, s_ref, y_ref, lse_ref, dy_ref,
             dq_ref, dk_ref, dv_ref, ds_ref, dkc, dvc):
        n = pl.program_id(0)

        @pl.when(n == 0)
        def _():
            dkc[...] = jnp.zeros_like(dkc)
            dvc[...] = jnp.zeros_like(dvc)
            ds_ref[...] = jnp.zeros_like(ds_ref)

        @pl.when(n < nb)
        def _():
            valid = _swa_valid(n)
            kk = jnp.concatenate([kp_ref[...], kc_ref[...]], axis=0)
            vv = jnp.concatenate([vp_ref[...], vc_ref[...]], axis=0)
            lane = lax.broadcasted_iota(jnp.int32, (1, ATTN_HEADS), 1)
            dsink = jnp.zeros((1, ATTN_HEADS), F32)
            dqs, dks, dvs = [], [], []
            for h in range(ATTN_HEADS):
                kv = h // grp
                qh = q_ref[:, h * HEAD_DIM:(h + 1) * HEAD_DIM]
                kh = kk[:, kv * HEAD_DIM:(kv + 1) * HEAD_DIM]
                vh = vv[:, kv * HEAD_DIM:(kv + 1) * HEAD_DIM]
                do = dy_ref[:, h * HEAD_DIM:(h + 1) * HEAD_DIM]
                oh = y_ref[:, h * HEAD_DIM:(h + 1) * HEAD_DIM]
                lse_h = lse_ref[:, h:h + 1]
                sc = jnp.where(valid, _mx_nt(qh, kh) * scale, NEG)
                p = jnp.exp(sc - lse_h)
                delta = jnp.sum(do * oh, axis=-1, keepdims=True)
                ds = p * (_mx_nt(do, vh) - delta)
                dqs.append(_mx(ds, kh) * scale)
                dkh = _mx_tn(ds, qh) * scale
                dvh = _mx_tn(p, do)
                if h % grp == 0:
                    dks.append(dkh)
                    dvs.append(dvh)
                else:
                    dks[kv] = dks[kv] + dkh
                    dvs[kv] = dvs[kv] + dvh
                dsh = -jnp.sum(jnp.exp(s_ref[0:1, h:h + 1] - lse_h) * delta, axis=0, keepdims=True)
                dsink = jnp.where(lane == h, dsh, dsink)
            dq_ref[...] = jnp.concatenate(dqs, axis=-1)
            dkf = jnp.concatenate(dks, axis=-1)
            dvf = jnp.concatenate(dvs, axis=-1)
            dk_ref[...] = dkc[...] + dkf[:ATTN_BLOCK]
            dv_ref[...] = (dvc[...] + dvf[:ATTN_BLOCK]).astype(dv_ref.dtype)
            dkc[...] = dkf[ATTN_BLOCK:]
            dvc[...] = dvf[ATTN_BLOCK:]
            ds_ref[...] += dsink

        @pl.when(n == nb)
        def _():
            dk_ref[...] = dkc[...]
            dv_ref[...] = dvc[...].astype(dv_ref.dtype)

    cur = lambda n: (jnp.minimum(n, nb - 1), 0)
    prev = lambda n: (jnp.clip(n - 1, 0, nb - 1), 0)
    vcol = (Q_W + KV_W) // KV_W
    return pl.pallas_call(
        body, name=name, grid=(nb + 1,),
        in_specs=[pl.BlockSpec((ATTN_BLOCK, Q_W), cur), pl.BlockSpec((ATTN_BLOCK, KV_W), cur),
                  pl.BlockSpec((ATTN_BLOCK, KV_W), prev),
                  pl.BlockSpec((ATTN_BLOCK, KV_W), lambda n: (jnp.minimum(n, nb - 1), vcol)),
                  pl.BlockSpec((ATTN_BLOCK, KV_W), lambda n: (jnp.clip(n - 1, 0, nb - 1), vcol)),
                  pl.BlockSpec((1, ATTN_HEADS), lambda n: (0, 0)),
                  pl.BlockSpec((ATTN_BLOCK, Q_W), cur), pl.BlockSpec((ATTN_BLOCK, ATTN_HEADS), cur),
                  pl.BlockSpec((ATTN_BLOCK, Q_W), cur)],
        out_specs=(pl.BlockSpec((ATTN_BLOCK, Q_W), cur), pl.BlockSpec((ATTN_BLOCK, KV_W), prev),
                   pl.BlockSpec((ATTN_BLOCK, KV_W), prev), pl.BlockSpec((1, ATTN_HEADS), lambda n: (0, 0))),
        out_shape=(jax.ShapeDtypeStruct((t, Q_W), F32), jax.ShapeDtypeStruct((t, KV_W), F32),
                   jax.ShapeDtypeStruct((t, KV_W), MXU_DTYPE), jax.ShapeDtypeStruct((1, ATTN_HEADS), F32)),
        scratch_shapes=[pltpu.VMEM((ATTN_BLOCK, KV_W), F32), pltpu.VMEM((ATTN_BLOCK, KV_W), F32)],
        compiler_params=_cp("arbitrary"))(q, k, k, proj, proj, sinks, y, lse, dmix)


GC_W = 256
_GB0, _GC0, _XI0 = 768 // GC_W, 1280 // GC_W, 1792 // GC_W
HALO = 8


def gconv_fwd(proj, conv_w, name, tm=512):
    t = proj.shape[0]
    hb = tm // HALO

    def body(gb_ref, gc_ref, xi_ref, gch_ref, xih_ref, w_ref, y_ref):
        i = pl.program_id(1)
        u = gc_ref[...] * xi_ref[...]
        uh = jnp.where(i == 0, 0.0, gch_ref[...] * xih_ref[...])
        up = jnp.concatenate([uh, u], axis=0)
        cv = w_ref[0:1, :] * up[HALO - 2:HALO - 2 + tm]
        cv = cv + w_ref[1:2, :] * up[HALO - 1:HALO - 1 + tm]
        cv = cv + w_ref[2:3, :] * u
        y_ref[...] = (gb_ref[...] * cv).astype(y_ref.dtype)

    def col(c0):
        return pl.BlockSpec((tm, GC_W), lambda cj, i: (i, c0 + cj))

    def halo(c0):
        return pl.BlockSpec((HALO, GC_W), lambda cj, i: (jnp.maximum(i * hb - 1, 0), c0 + cj))

    return pl.pallas_call(
        body, name=name, grid=(CONV_CH // GC_W, t // tm),
        in_specs=[col(_GB0), col(_GC0), col(_XI0), halo(_GC0), halo(_XI0),
                  pl.BlockSpec((3, GC_W), lambda cj, i: (0, cj))],
        out_specs=pl.BlockSpec((tm, GC_W), lambda cj, i: (i, cj)),
        out_shape=jax.ShapeDtypeStruct((t, CONV_CH), MXU_DTYPE),
        compiler_params=_cp("parallel", "parallel"))(proj, proj, proj, proj, proj, conv_w)


def gconv_bwd(proj, conv_w, dmix, name, tm=512):
    t = proj.shape[0]
    hb = tm // HALO
    nt = t // tm
    dy0 = Q_W // GC_W

    def body(gb_ref, gc_ref, xi_ref, gch_ref, xih_ref, gbn_ref, dyn_ref, dy_ref, w_ref,
             dgb_ref, dgc_ref, dxi_ref, dw_ref):
        i = pl.program_id(1)
        gc, xi, gb, dy = gc_ref[...], xi_ref[...], gb_ref[...], dy_ref[...]
        u = gc * xi
        uh = jnp.where(i == 0, 0.0, gch_ref[...] * xih_ref[...])
        up = jnp.concatenate([uh, u], axis=0)
        u2 = up[HALO - 2:HALO - 2 + tm]
        u1 = up[HALO - 1:HALO - 1 + tm]
        cv = w_ref[0:1, :] * u2 + w_ref[1:2, :] * u1 + w_ref[2:3, :] * u
        dgb_ref[...] = (dy * cv).astype(dgb_ref.dtype)
        dcv = dy * gb
        dcvn = jnp.where(i == nt - 1, 0.0, dyn_ref[...] * gbn_ref[...])
        dcvp = jnp.concatenate([dcv, dcvn], axis=0)
        du = w_ref[0:1, :] * dcvp[2:2 + tm] + w_ref[1:2, :] * dcvp[1:1 + tm] + w_ref[2:3, :] * dcv
        dgc_ref[...] = (du * xi).astype(dgc_ref.dtype)
        dxi_ref[...] = (du * gc).astype(dxi_ref.dtype)
        dw = jnp.concatenate([jnp.sum(dcv * u2, axis=0, keepdims=True), jnp.sum(dcv * u1, axis=0, keepdims=True),
                              jnp.sum(dcv * u, axis=0, keepdims=True)], axis=0)

        @pl.when(i == 0)
        def _():
            dw_ref[...] = dw

        @pl.when(i > 0)
        def _():
            dw_ref[...] += dw

    def col(c0):
        return pl.BlockSpec((tm, GC_W), lambda cj, i: (i, c0 + cj))

    def halo(c0):
        return pl.BlockSpec((HALO, GC_W), lambda cj, i: (jnp.maximum(i * hb - 1, 0), c0 + cj))

    def nxt(c0):
        return pl.BlockSpec((HALO, GC_W), lambda cj, i: (jnp.minimum((i + 1) * hb, t // HALO - 1), c0 + cj))

    out = pl.BlockSpec((tm, GC_W), lambda cj, i: (i, cj))
    return pl.pallas_call(
        body, name=name, grid=(CONV_CH // GC_W, nt),
        in_specs=[col(_GB0), col(_GC0), col(_XI0), halo(_GC0), halo(_XI0), nxt(_GB0), nxt(dy0), col(dy0),
                  pl.BlockSpec((3, GC_W), lambda cj, i: (0, cj))],
        out_specs=(out, out, out, pl.BlockSpec((3, GC_W), lambda cj, i: (0, cj))),
        out_shape=(jax.ShapeDtypeStruct((t, CONV_CH), MXU_DTYPE),) * 3 + (jax.ShapeDtypeStruct((3, CONV_CH), F32),),
        compiler_params=_cp("parallel", "arbitrary"))(proj, proj, proj, proj, proj, proj, dmix, dmix, conv_w)


_QKV_W = 3 * DN_W
_BA_COL = (4 * DN_W) // 128
_Z_COL = _QKV_W // DN_W


def gdn_prep_fwd(proj, conv_w, alog_row, dtb_row, name, tm=256):
    t = proj.shape[0]
    hb = tm // HALO
    qscale = DN_DIM ** -0.5

    def body(x_ref, xh_ref, w_ref, ba_ref, al_ref, dt_ref, q_ref, k_ref, v_ref, bg_ref):
        i = pl.program_id(0)
        for gi in range(3 * DN_HEADS):
            sl = slice(gi * DN_DIM, (gi + 1) * DN_DIM)
            xp = jnp.concatenate([jnp.where(i == 0, 0.0, xh_ref[:, sl]), x_ref[:, sl]], axis=0)
            c = w_ref[0:1, sl] * xp[HALO - 3:HALO - 3 + tm]
            for j in range(1, 4):
                c = c + w_ref[j:j + 1, sl] * xp[HALO - 3 + j:HALO - 3 + j + tm]
            s = c * _sigmoid(c)
            osl = slice((gi % DN_HEADS) * DN_DIM, (gi % DN_HEADS + 1) * DN_DIM)
            if gi < DN_HEADS:
                q_ref[:, osl] = s * lax.rsqrt(jnp.sum(s * s, axis=-1, keepdims=True) + EPS) * qscale
            elif gi < 2 * DN_HEADS:
                k_ref[:, osl] = s * lax.rsqrt(jnp.sum(s * s, axis=-1, keepdims=True) + EPS)
            else:
                v_ref[:, osl] = s
        ba = ba_ref[...]
        lane = lax.broadcasted_iota(jnp.int32, ba.shape, 1)
        gval = -jnp.exp(al_ref[...]) * _softplus(ba + dt_ref[...])
        bg_ref[...] = jnp.where(lane < DN_HEADS, _sigmoid(ba), jnp.where(lane < 2 * DN_HEADS, gval, 0.0))

    row = pl.BlockSpec((tm, DN_W), lambda i: (i, 0))
    one = pl.BlockSpec((1, 128), lambda i: (0, 0))
    return pl.pallas_call(
        body, name=name, grid=(t // tm,),
        in_specs=[pl.BlockSpec((tm, _QKV_W), lambda i: (i, 0)),
                  pl.BlockSpec((HALO, _QKV_W), lambda i: (jnp.maximum(i * hb - 1, 0), 0)),
                  pl.BlockSpec((4, _QKV_W), lambda i: (0, 0)),
                  pl.BlockSpec((tm, 128), lambda i: (i, _BA_COL)), one, one],
        out_specs=(row, row, row, pl.BlockSpec((tm, 128), lambda i: (i, 0))),
        out_shape=(jax.ShapeDtypeStruct((t, DN_W), F32),) * 3 + (jax.ShapeDtypeStruct((t, 128), F32),),
        compiler_params=_cp("parallel"))(proj, proj, conv_w, proj, alog_row, dtb_row)


def gdn_prep_bwd(proj, conv_w, alog_row, dtb_row, dq, dk, dv, dbg, name, tm=256):
    t = proj.shape[0]
    hb = tm // HALO
    nt = t // tm
    qscale = DN_DIM ** -0.5
    te = tm + HALO

    def body(x_ref, xh_ref, xn_ref, w_ref, ba_ref, al_ref, dt_ref, dq_ref, dk_ref, dv_ref,
             dqn_ref, dkn_ref, dvn_ref, dbg_ref, dx_ref, dba_ref, dw_ref, ddt_ref, dal_ref):
        i = pl.program_id(0)
        first = i == 0
        last = i == nt - 1
        dws = []
        for gi in range(3 * DN_HEADS):
            sl = slice(gi * DN_DIM, (gi + 1) * DN_DIM)
            osl = slice((gi % DN_HEADS) * DN_DIM, (gi % DN_HEADS + 1) * DN_DIM)
            xe = jnp.concatenate([jnp.where(first, 0.0, xh_ref[:, sl]), x_ref[:, sl], xn_ref[:, sl]], axis=0)
            c = w_ref[0:1, sl] * xe[HALO - 3:HALO - 3 + te]
            for j in range(1, 4):
                c = c + w_ref[j:j + 1, sl] * xe[HALO - 3 + j:HALO - 3 + j + te]
            sg = _sigmoid(c)
            s = c * sg
            d_ref, dn_ref = ((dq_ref, dqn_ref), (dk_ref, dkn_ref), (dv_ref, dvn_ref))[gi // DN_HEADS]
            dy = jnp.concatenate([d_ref[:, osl], jnp.where(last, 0.0, dn_ref[:, osl])], axis=0)
            if gi < 2 * DN_HEADS:
                r = lax.rsqrt(jnp.sum(s * s, axis=-1, keepdims=True) + EPS)
                sh = s * r
                ds = r * (dy - sh * jnp.sum(sh * dy, axis=-1, keepdims=True))
                if gi < DN_HEADS:
                    ds = ds * qscale
            else:
                ds = dy
            dc = ds * sg * (1.0 + c * (1.0 - sg))
            dx = w_ref[0:1, sl] * dc[3:3 + tm]
            for j in range(1, 4):
                dx = dx + w_ref[j:j + 1, sl] * dc[3 - j:3 - j + tm]
            dx_ref[:, sl] = dx.astype(dx_ref.dtype)
            dc0 = dc[:tm]
            dws.append(jnp.concatenate(
                [jnp.sum(dc0 * xe[HALO - 3 + j:HALO - 3 + j + tm], axis=0, keepdims=True) for j in range(4)], axis=0))
        dw = jnp.concatenate(dws, axis=-1)
        ba = ba_ref[...]
        dbgv = dbg_ref[...]
        lane = lax.broadcasted_iota(jnp.int32, ba.shape, 1)
        beta = _sigmoid(ba)
        ea = -jnp.exp(al_ref[...])
        zin = ba + dt_ref[...]
        is_b = lane < DN_HEADS
        is_a = (lane >= DN_HEADS) & (lane < 2 * DN_HEADS)
        da = jnp.where(is_a, dbgv * ea * _sigmoid(zin), 0.0)
        dba_ref[...] = jnp.where(is_b, dbgv * beta * (1.0 - beta), da).astype(dba_ref.dtype)
        ddt = jnp.sum(da, axis=0, keepdims=True)
        dal = jnp.sum(jnp.where(is_a, dbgv * ea * _softplus(zin), 0.0), axis=0, keepdims=True)

        @pl.when(first)
        def _():
            dw_ref[...] = dw
            ddt_ref[...] = ddt
            dal_ref[...] = dal

        @pl.when(i > 0)
        def _():
            dw_ref[...] += dw
            ddt_ref[...] += ddt
            dal_ref[...] += dal

    row = pl.BlockSpec((tm, DN_W), lambda i: (i, 0))
    nrow = pl.BlockSpec((HALO, DN_W), lambda i: (jnp.minimum((i + 1) * hb, t // HALO - 1), 0))
    one = pl.BlockSpec((1, 128), lambda i: (0, 0))
    return pl.pallas_call(
        body, name=name, grid=(nt,),
        in_specs=[pl.BlockSpec((tm, _QKV_W), lambda i: (i, 0)),
                  pl.BlockSpec((HALO, _QKV_W), lambda i: (jnp.maximum(i * hb - 1, 0), 0)),
                  pl.BlockSpec((HALO, _QKV_W), lambda i: (jnp.minimum((i + 1) * hb, t // HALO - 1), 0)),
                  pl.BlockSpec((4, _QKV_W), lambda i: (0, 0)),
                  pl.BlockSpec((tm, 128), lambda i: (i, _BA_COL)), one, one,
                  row, row, row, nrow, nrow, nrow, pl.BlockSpec((tm, 128), lambda i: (i, 0))],
        out_specs=(pl.BlockSpec((tm, _QKV_W), lambda i: (i, 0)), pl.BlockSpec((tm, 128), lambda i: (i, 0)),
                   pl.BlockSpec((4, _QKV_W), lambda i: (0, 0)), one, one),
        out_shape=(jax.ShapeDtypeStruct((t, _QKV_W), MXU_DTYPE), jax.ShapeDtypeStruct((t, 128), MXU_DTYPE),
                   jax.ShapeDtypeStruct((4, _QKV_W), F32), jax.ShapeDtypeStruct((1, 128), F32),
                   jax.ShapeDtypeStruct((1, 128), F32)),
        compiler_params=_cp("arbitrary"))(proj, proj, proj, conv_w, proj, alog_row, dtb_row, dq, dk, dv, dq, dk, dv, dbg)


def _chunk_masks():
    r = lax.broadcasted_iota(jnp.int32, (DN_CHUNK, DN_CHUNK), 0)
    c = lax.broadcasted_iota(jnp.int32, (DN_CHUNK, DN_CHUNK), 1)
    return r >= c, r > c


def _inv_unit_lower(a):
    r = lax.broadcasted_iota(jnp.int32, a.shape, 0)
    c = lax.broadcasted_iota(jnp.int32, a.shape, 1)
    x = jnp.where(r == c, 1.0, 0.0) - a
    pw = _hi(a, a)
    for step in range(5):
        x = x + _hi(x, pw)
        if step < 4:
            pw = _hi(pw, pw)
    return x


def _chunk_common(q, k, v, beta, gc, gcr, lower, strict):
    gam = jnp.exp(jnp.where(lower, gc - gcr, NEG))
    eg = jnp.exp(gc)
    gl = gc[DN_CHUNK - 1:DN_CHUNK, :]
    kdf = jnp.exp(gl - gc)
    kb = k * beta
    bmat = _mx_nt(kb, k)
    qmat = _mx_nt(q, k)
    return gam, eg, jnp.exp(gl), kdf, kb, bmat, qmat


def gdn_fwd(q, k, v, bg, name):
    t = q.shape[0]
    n_chunks = t // DN_CHUNK

    def body(q_ref, k_ref, v_ref, bg_ref, o_ref, sall_ref, tall_ref, s_ref):
        n = pl.program_id(0)

        @pl.when(n == 0)
        def _():
            s_ref[...] = jnp.zeros_like(s_ref)

        lower, strict = _chunk_masks()
        bgv = bg_ref[...]
        gcs = _hi(jnp.where(lower, 1.0, 0.0), bgv)
        gcs_t = gcs.T
        for h in range(DN_HEADS):
            sl = slice(h * DN_DIM, (h + 1) * DN_DIM)
            st = s_ref[h]
            sall_ref[0, h] = st
            qh, kh, vh = q_ref[:, sl], k_ref[:, sl], v_ref[:, sl]
            beta = bgv[:, h:h + 1]
            gc = gcs[:, DN_HEADS + h:DN_HEADS + h + 1]
            gcr = gcs_t[DN_HEADS + h:DN_HEADS + h + 1, :]
            gam, eg, dec, kdf, kb, bmat, qmat = _chunk_common(qh, kh, vh, beta, gc, gcr, lower, strict)
            tm_ = _inv_unit_lower(jnp.where(strict, bmat * gam, 0.0))
            tall_ref[0, h] = tm_
            u = _hi(tm_, vh * beta)
            w = _hi(tm_, kb * eg)
            v_new = u - _mx(w, st)
            o_ref[:, sl] = _mx(qh * eg, st) + _mx(qmat * gam, v_new)
            s_ref[h] = st * dec + _mx_tn(kh * kdf, v_new)

    row = pl.BlockSpec((DN_CHUNK, DN_W), lambda n: (n, 0))
    return pl.pallas_call(
        body, name=name, grid=(n_chunks,),
        in_specs=[row, row, row, pl.BlockSpec((DN_CHUNK, 128), lambda n: (n, 0))],
        out_specs=(row, pl.BlockSpec((1, DN_HEADS, DN_DIM, DN_DIM), lambda n: (n, 0, 0, 0)),
                   pl.BlockSpec((1, DN_HEADS, DN_CHUNK, DN_CHUNK), lambda n: (n, 0, 0, 0))),
        out_shape=(jax.ShapeDtypeStruct((t, DN_W), F32),
                   jax.ShapeDtypeStruct((n_chunks, DN_HEADS, DN_DIM, DN_DIM), F32),
                   jax.ShapeDtypeStruct((n_chunks, DN_HEADS, DN_CHUNK, DN_CHUNK), F32)),
        scratch_shapes=[pltpu.VMEM((DN_HEADS, DN_DIM, DN_DIM), F32)],
        compiler_params=_cp("arbitrary"))(q, k, v, bg)


def gdn_bwd(q, k, v, bg, sall, tall, do, name):
    t = q.shape[0]
    n_chunks = t // DN_CHUNK

    def body(q_ref, k_ref, v_ref, bg_ref, sall_ref, tall_ref, do_ref, dq_ref, dk_ref, dv_ref, dbg_ref, ds_ref):
        n = pl.program_id(0)

        @pl.when(n == 0)
        def _():
            ds_ref[...] = jnp.zeros_like(ds_ref)

        lower, strict = _chunk_masks()
        ltri = jnp.where(lower, 1.0, 0.0)
        bgv = bg_ref[...]
        gcs = _hi(ltri, bgv)
        gcs_t = gcs.T
        lane = lax.broadcasted_iota(jnp.int32, (DN_CHUNK, 128), 1)
        rowi = lax.broadcasted_iota(jnp.int32, (DN_CHUNK, 1), 0)
        dbeta_all = jnp.zeros((DN_CHUNK, 128), F32)
        dgc_all = jnp.zeros((DN_CHUNK, 128), F32)
        for h in range(DN_HEADS):
            sl = slice(h * DN_DIM, (h + 1) * DN_DIM)
            st = sall_ref[0, h]
            tm_ = tall_ref[0, h]
            dsn = ds_ref[h]
            qh, kh, vh, doh = q_ref[:, sl], k_ref[:, sl], v_ref[:, sl], do_ref[:, sl]
            beta = bgv[:, h:h + 1]
            gc = gcs[:, DN_HEADS + h:DN_HEADS + h + 1]
            gcr = gcs_t[DN_HEADS + h:DN_HEADS + h + 1, :]
            gam, eg, dec, kdf, kb, bmat, qmat = _chunk_common(qh, kh, vh, beta, gc, gcr, lower, strict)
            rhs_w = kb * eg
            u = _hi(tm_, vh * beta)
            w = _hi(tm_, rhs_w)
            qd = qh * eg
            kd = kh * kdf
            pmat = qmat * gam
            v_new = u - _mx(w, st)
            dqd = _mx_nt(doh, st)
            ds_acc = _mx_tn(qd, doh)
            dp = jnp.where(lower, _mx_nt(doh, v_new), 0.0)
            dvn = _mx_tn(pmat, doh)
            ddec = jnp.sum(jnp.sum(dsn * st, axis=1, keepdims=True), axis=0, keepdims=True)
            ds_acc = ds_acc + dec * dsn
            dkd = _mx_nt(v_new, dsn)
            dvn = dvn + _mx(kd, dsn)
            dw = -_mx_nt(dvn, st)
            ds_acc = ds_acc - _mx_tn(w, dvn)
            dru = _hi_tn(tm_, dvn)
            drw = _hi_tn(tm_, dw)
            da = jnp.where(strict, -(_hi_nt(dru, u) + _hi_nt(drw, w)), 0.0)
            db = da * gam
            dq_m = dp * gam
            dgam = da * bmat + dp * qmat
            dkb = _mx(db, kh) + drw * eg
            dk = _mx_tn(db, kb) + _mx_tn(dq_m, qh) + dkd * kdf
            dq = _mx(dq_m, kh) + dqd * eg
            e = dgam * gam
            tk = jnp.sum(dkd * kd, axis=-1, keepdims=True)
            dgc = (jnp.sum(e, axis=1, keepdims=True) - jnp.sum(e.T, axis=1, keepdims=True)
                   + jnp.sum(dqd * qd, axis=-1, keepdims=True) - tk + jnp.sum(drw * rhs_w, axis=-1, keepdims=True))
            dgl = jnp.sum(tk, axis=0, keepdims=True) + ddec * dec
            dgc = dgc + jnp.where(rowi == DN_CHUNK - 1, dgl, 0.0)
            dbeta = jnp.sum(dru * vh, axis=-1, keepdims=True) + jnp.sum(dkb * kh, axis=-1, keepdims=True)
            dq_ref[:, sl] = dq
            dk_ref[:, sl] = dk + dkb * beta
            dv_ref[:, sl] = dru * beta
            ds_ref[h] = ds_acc
            dbeta_all = jnp.where(lane == h, dbeta, dbeta_all)
            dgc_all = jnp.where(lane == DN_HEADS + h, dgc, dgc_all)
        dbg_ref[...] = dbeta_all + _hi_tn(ltri, dgc_all)

    rev = lambda n: (n_chunks - 1 - n, 0)
    row = pl.BlockSpec((DN_CHUNK, DN_W), rev)
    small = pl.BlockSpec((DN_CHUNK, 128), rev)
    return pl.pallas_call(
        body, name=name, grid=(n_chunks,),
        in_specs=[row, row, row, small,
                  pl.BlockSpec((1, DN_HEADS, DN_DIM, DN_DIM), lambda n: (n_chunks - 1 - n, 0, 0, 0)),
                  pl.BlockSpec((1, DN_HEADS, DN_CHUNK, DN_CHUNK), lambda n: (n_chunks - 1 - n, 0, 0, 0)), row],
        out_specs=(row, row, row, small),
        out_shape=(jax.ShapeDtypeStruct((t, DN_W), F32),) * 3 + (jax.ShapeDtypeStruct((t, 128), F32),),
        scratch_shapes=[pltpu.VMEM((DN_HEADS, DN_DIM, DN_DIM), F32)],
        compiler_params=_cp("arbitrary"))(q, k, v, bg, sall, tall, do)


def gdn_out_fwd(o, proj, o_gain, name, tm=256):
    t = o.shape[0]

    def body(o_ref, z_ref, g_ref, y_ref):
        for h in range(DN_HEADS):
            sl = slice(h * DN_DIM, (h + 1) * DN_DIM)
            ov, zv = o_ref[:, sl], z_ref[:, sl]
            r = lax.rsqrt(jnp.mean(ov * ov, axis=-1, keepdims=True) + EPS)
            y_ref[:, sl] = (ov * r * g_ref[...] * (zv * _sigmoid(zv))).astype(y_ref.dtype)

    row = pl.BlockSpec((tm, DN_W), lambda i: (i, 0))
    return pl.pallas_call(
        body, name=name, grid=(t // tm,),
        in_specs=[row, pl.BlockSpec((tm, DN_W), lambda i: (i, _Z_COL)), pl.BlockSpec((1, DN_DIM), lambda i: (0, 0))],
        out_specs=row, out_shape=jax.ShapeDtypeStruct((t, DN_W), MXU_DTYPE),
        compiler_params=_cp("parallel"))(o, proj, o_gain)


def gdn_out_bwd(o, proj, o_gain, dy, name, tm=256):
    t = o.shape[0]

    def body(o_ref, z_ref, g_ref, dy_ref, do_ref, dz_ref, dg_ref):
        i = pl.program_id(0)
        dg = jnp.zeros((1, DN_DIM), F32)
        for h in range(DN_HEADS):
            sl = slice(h * DN_DIM, (h + 1) * DN_DIM)
            ov, zv, dyv = o_ref[:, sl], z_ref[:, sl], dy_ref[:, sl]
            r = lax.rsqrt(jnp.mean(ov * ov, axis=-1, keepdims=True) + EPS)
            oh = ov * r
            sg = _sigmoid(zv)
            dz_ref[:, sl] = (dyv * oh * g_ref[...] * sg * (1.0 + zv * (1.0 - sg))).astype(dz_ref.dtype)
            don = dyv * (zv * sg)
            dg = dg + jnp.sum(don * oh, axis=0, keepdims=True)
            doh = don * g_ref[...]
            do_ref[:, sl] = r * (doh - oh * jnp.mean(doh * oh, axis=-1, keepdims=True))

        @pl.when(i == 0)
        def _():
            dg_ref[...] = dg

        @pl.when(i > 0)
        def _():
            dg_ref[...] += dg

    row = pl.BlockSpec((tm, DN_W), lambda i: (i, 0))
    one = pl.BlockSpec((1, DN_DIM), lambda i: (0, 0))
    return pl.pallas_call(
        body, name=name, grid=(t // tm,),
        in_specs=[row, pl.BlockSpec((tm, DN_W), lambda i: (i, _Z_COL)), one, row],
        out_specs=(row, row, one),
        out_shape=(jax.ShapeDtypeStruct((t, DN_W), F32), jax.ShapeDtypeStruct((t, DN_W), MXU_DTYPE),
                   jax.ShapeDtypeStruct((1, DN_DIM), F32)),
        compiler_params=_cp("arbitrary"))(o, proj, o_gain, dy)


def _peer(k):
    x, y, c = lax.axis_index("x"), lax.axis_index("y"), lax.axis_index("c")
    px = 1 - x if k & 4 else x
    py = 1 - y if k & 2 else y
    pc = 1 - c if k & 1 else c
    return (px, py, pc), 4 * px + 2 * py + pc


def all_gather(shards, name):
    na = len(shards)

    def body(*refs):
        ins, outs = refs[:na], refs[na:2 * na]
        send_sems, recv_sems, local_sems = refs[2 * na:]
        _, me = _peer(0)
        local = [pltpu.make_async_copy(ins[a], outs[a].at[me], local_sems.at[a]) for a in range(na)]
        for cp in local:
            cp.start()
        sends = []
        for k in range(1, N_DEV):
            peer, _ = _peer(k)
            for a in range(na):
                cp = pltpu.make_async_remote_copy(
                    src_ref=ins[a], dst_ref=outs[a].at[me], send_sem=send_sems.at[a, k - 1],
                    recv_sem=recv_sems.at[a, k - 1], device_id=peer, device_id_type=MESH)
                cp.start()
                sends.append(cp)
        for k in range(1, N_DEV):
            peer, pid = _peer(k)
            for a in range(na):
                pltpu.make_async_remote_copy(
                    src_ref=ins[a], dst_ref=outs[a].at[pid], send_sem=send_sems.at[a, k - 1],
                    recv_sem=recv_sems.at[a, k - 1], device_id=peer, device_id_type=MESH).wait_recv()
        for cp in sends:
            cp.wait_send()
        for cp in local:
            cp.wait()

    anyspec = pl.BlockSpec(memory_space=pl.ANY)
    return pl.pallas_call(
        body, name=name, in_specs=[anyspec] * na, out_specs=tuple([anyspec] * na),
        out_shape=tuple(jax.ShapeDtypeStruct((N_DEV,) + s.shape, s.dtype) for s in shards),
        scratch_shapes=[pltpu.SemaphoreType.DMA((na, N_DEV - 1)), pltpu.SemaphoreType.DMA((na, N_DEV - 1)),
                        pltpu.SemaphoreType.DMA((na,))],
        compiler_params=pltpu.CompilerParams(has_side_effects=True))(*shards)


def all_to_all(pieces, name):
    na = len(pieces)

    def body(*refs):
        ins, outs = refs[:na], refs[na:2 * na]
        send_sems, recv_sems, local_sems = refs[2 * na:]
        _, me = _peer(0)
        local = [pltpu.make_async_copy(ins[a].at[me], outs[a].at[me], local_sems.at[a]) for a in range(na)]
        for cp in local:
            cp.start()
        sends = []
        for k in range(1, N_DEV):
            peer, pid = _peer(k)
            for a in range(na):
                cp = pltpu.make_async_remote_copy(
                    src_ref=ins[a].at[pid], dst_ref=outs[a].at[me], send_sem=send_sems.at[a, k - 1],
                    recv_sem=recv_sems.at[a, k - 1], device_id=peer, device_id_type=MESH)
                cp.start()
                sends.append(cp)
        for k in range(1, N_DEV):
            peer, pid = _peer(k)
            for a in range(na):
                pltpu.make_async_remote_copy(
                    src_ref=ins[a].at[me], dst_ref=outs[a].at[pid], send_sem=send_sems.at[a, k - 1],
                    recv_sem=recv_sems.at[a, k - 1], device_id=peer, device_id_type=MESH).wait_recv()
        for cp in sends:
            cp.wait_send()
        for cp in local:
            cp.wait()

    anyspec = pl.BlockSpec(memory_space=pl.ANY)
    return pl.pallas_call(
        body, name=name, in_specs=[anyspec] * na, out_specs=tuple([anyspec] * na),
        out_shape=tuple(jax.ShapeDtypeStruct(p.shape, p.dtype) for p in pieces),
        scratch_shapes=[pltpu.SemaphoreType.DMA((na, N_DEV - 1)), pltpu.SemaphoreType.DMA((na, N_DEV - 1)),
                        pltpu.SemaphoreType.DMA((na,))],
        compiler_params=pltpu.CompilerParams(has_side_effects=True))(*pieces)


def _adamw(w, g, m, v):
    m = ADAM_B1 * m + (1.0 - ADAM_B1) * g
    v = ADAM_B2 * v + (1.0 - ADAM_B2) * (g * g)
    m_hat = m / (1.0 - ADAM_B1 ** ADAM_STEP)
    v_hat = v / (1.0 - ADAM_B2 ** ADAM_STEP)
    return -ADAM_LR * (m_hat / (jnp.sqrt(v_hat) + ADAM_EPS) + ADAM_WD * w), m, v


def adam_sum(w, pieces, m, v, name):
    r, c = w.shape
    tr = r
    for cand in (256, 128, 64, 32, 16, 8):
        if r % cand == 0:
            tr = cand
            break

    def body(w_ref, p_ref, m_ref, v_ref, g_ref, d_ref, nm_ref, nv_ref):
        g = p_ref[0].astype(F32)
        for s in range(1, N_DEV):
            g = g + p_ref[s].astype(F32)
        g_ref[...] = g
        d_ref[...], nm_ref[...], nv_ref[...] = _adamw(w_ref[...], g, m_ref[...], v_ref[...])

    row = pl.BlockSpec((tr, c), lambda i: (i, 0))
    out = jax.ShapeDtypeStruct((r, c), F32)
    return pl.pallas_call(
        body, name=name, grid=(r // tr,),
        in_specs=[row, pl.BlockSpec((N_DEV, tr, c), lambda i: (0, i, 0)), row, row],
        out_specs=(row,) * 4, out_shape=(out,) * 4, compiler_params=_cp("parallel"))(w, pieces, m, v)


def sum_rows(gathered, name):
    _, r, c = gathered.shape

    def body(p_ref, o_ref):
        g = p_ref[0]
        for s in range(1, N_DEV):
            g = g + p_ref[s]
        o_ref[...] = g

    return pl.pallas_call(body, name=name, out_shape=jax.ShapeDtypeStruct((r, c), F32))(gathered)


def adam_small(w, g, m, v, name):
    def body(w_ref, g_ref, m_ref, v_ref, d_ref, nm_ref, nv_ref):
        d_ref[...], nm_ref[...], nv_ref[...] = _adamw(w_ref[...], g_ref[...], m_ref[...], v_ref[...])

    out = jax.ShapeDtypeStruct(w.shape, F32)
    return pl.pallas_call(body, name=name, out_shape=(out,) * 3)(w, g, m, v)


def _rope_tables(t):
    inv_freq = 10000.0 ** (-jnp.arange(0, HEAD_DIM, 2, dtype=F32) / HEAD_DIM)
    ang = jnp.arange(t, dtype=F32)[:, None] * inv_freq[None, :]
    cos, sin = jnp.cos(ang), jnp.sin(ang)
    return jnp.concatenate([cos, cos], axis=-1), jnp.concatenate([sin, sin], axis=-1)


def _lane_row(vec8):
    return jnp.pad(vec8.reshape(1, DN_HEADS), ((0, 0), (DN_HEADS, 128 - 2 * DN_HEADS)))


def _ffn_fwd(x, norm_g, w_gu, w_d, tag):
    f = rms_fwd(x, norm_g, f"{tag}_norm")
    gu = mm_nn(f, w_gu, f"{tag}_gate_up")
    a = swiglu_fwd(gu, f"{tag}_act")
    return mm_nn(a, w_d, f"{tag}_down", res=x), (f, gu, a)


def _ffn_bwd(x, norm_g, w_gu, w_d, saved, dy, tag):
    f, gu, a = saved
    da = mm_nt(dy, w_d, f"{tag}_d_act")
    dwd = mm_tn(a, dy, f"{tag}_dw_down")
    dgu = swiglu_bwd(gu, da, f"{tag}_d_gate_up")
    df = mm_nt(dgu, w_gu, f"{tag}_d_normed")
    dwgu = mm_tn(f, dgu, f"{tag}_dw_gate_up")
    dx, dg = rms_bwd(x, norm_g, df, dy, f"{tag}_d_norm")
    return dx, dwgu, dwd, dg


def local_step(x, target, wts, small):
    t = x.shape[0]
    cosf, sinf = _rope_tables(t)
    alog_row, dtb_row = _lane_row(small["odd_a_log"]), _lane_row(small["odd_dt_bias"])

    h0 = rms_fwd(x, small["even_norm"], "even_norm")
    proj0 = mm_nn(h0, wts["even_w_in"], "even_in_proj")
    qr, kr = qk_prep_fwd(proj0, small["even_q_gain"], small["even_k_gain"], cosf, sinf, "even_qk_prep")
    y_attn, lse = swa_fwd(qr, kr, proj0, small["even_sinks"], "even_swa")
    y_conv = gconv_fwd(proj0, small["even_conv_w"], "even_gconv")
    mix0 = jnp.concatenate([y_attn.astype(MXU_DTYPE), y_conv], axis=-1)
    x1 = mm_nn(mix0, wts["even_w_out"], "even_out_proj", res=x)
    x2, ffn0 = _ffn_fwd(x1, small["ffn_norm0"], wts["ffn_w_gate_up0"], wts["ffn_w_down0"], "ffn0")

    h1 = rms_fwd(x2, small["odd_norm"], "odd_norm")
    proj1 = mm_nn(h1, wts["odd_w_in"], "odd_in_proj")
    qn, kn, vs, bg = gdn_prep_fwd(proj1, small["odd_conv_w"], alog_row, dtb_row, "odd_prep")
    o, sall, tall = gdn_fwd(qn, kn, vs, bg, "odd_delta_rule")
    og = gdn_out_fwd(o, proj1, small["odd_o_gain"], "odd_gate_norm")
    x3 = mm_nn(og, wts["odd_w_out"], "odd_out_proj", res=x2)
    x4, ffn1 = _ffn_fwd(x3, small["ffn_norm1"], wts["ffn_w_gate_up1"], wts["ffn_w_down1"], "ffn1")

    dy, loss_row = loss_head(x4, target, "loss_head")

    gw, gs = {}, {}
    dx3, gw["ffn_w_gate_up1"], gw["ffn_w_down1"], gs["ffn_norm1"] = _ffn_bwd(
        x3, small["ffn_norm1"], wts["ffn_w_gate_up1"], wts["ffn_w_down1"], ffn1, dy, "ffn1")

    dog = mm_nt(dx3, wts["odd_w_out"], "odd_d_gated")
    gw["odd_w_out"] = mm_tn(og, dx3, "odd_dw_out")
    do, dz, gs["odd_o_gain"] = gdn_out_bwd(o, proj1, small["odd_o_gain"], dog, "odd_d_gate_norm")
    dqn, dkn, dvs, dbg = gdn_bwd(qn, kn, vs, bg, sall, tall, do, "odd_d_delta_rule")
    dqkv, dba, gs["odd_conv_w"], ddt_row, dal_row = gdn_prep_bwd(
        proj1, small["odd_conv_w"], alog_row, dtb_row, dqn, dkn, dvs, dbg, "odd_d_prep")
    gs["odd_dt_bias"] = ddt_row[:, DN_HEADS:2 * DN_HEADS]
    gs["odd_a_log"] = dal_row[:, DN_HEADS:2 * DN_HEADS]
    dproj1 = jnp.concatenate([dqkv, dz, dba], axis=-1)
    dh1 = mm_nt(dproj1, wts["odd_w_in"], "odd_d_normed")
    gw["odd_w_in"] = mm_tn(h1, dproj1, "odd_dw_in")
    dx2, gs["odd_norm"] = rms_bwd(x2, small["odd_norm"], dh1, dx3, "odd_d_norm")

    dx1, gw["ffn_w_gate_up0"], gw["ffn_w_down0"], gs["ffn_norm0"] = _ffn_bwd(
        x1, small["ffn_norm0"], wts["ffn_w_gate_up0"], wts["ffn_w_down0"], ffn0, dx2, "ffn0")

    dmix = mm_nt(dx1, wts["even_w_out"], "even_d_mix")
    gw["even_w_out"] = mm_tn(mix0, dx1, "even_dw_out")
    dqr, dkr, dv, gs["even_sinks"] = swa_bwd(qr, kr, proj0, small["even_sinks"], y_attn, lse, dmix, "even_d_swa")
    dqk, gs["even_q_gain"], gs["even_k_gain"] = qk_prep_bwd(
        proj0, small["even_q_gain"], small["even_k_gain"], cosf, sinf, dqr, dkr, "even_d_qk_prep")
    dgb, dgc, dxi, gs["even_conv_w"] = gconv_bwd(proj0, small["even_conv_w"], dmix, "even_d_gconv")
    dproj0 = jnp.concatenate([dqk, dv, dgb, dgc, dxi], axis=-1)
    dh0 = mm_nt(dproj0, wts["even_w_in"], "even_d_normed")
    gw["even_w_in"] = mm_tn(h0, dproj0, "even_dw_in")
    grad_x, gs["even_norm"] = rms_bwd(x, small["even_norm"], dh0, dx1, "even_d_norm")
    return loss_row, grad_x, gw, gs


_SMALL_ORDER = ("even_norm", "even_q_gain", "even_k_gain", "even_sinks", "odd_a_log", "odd_dt_bias", "odd_o_gain",
                "ffn_norm0", "ffn_norm1", "odd_norm", "even_conv_w", "odd_conv_w")
_SMALL_SIZE = {"even_norm": 1024, "even_q_gain": 64, "even_k_gain": 64, "even_sinks": 8, "odd_a_log": 8,
               "odd_dt_bias": 8, "odd_o_gain": 128, "ffn_norm0": 1024, "ffn_norm1": 1024, "odd_norm": 1024,
               "even_conv_w": 3 * 512, "odd_conv_w": 4 * 3072}
_N_REPL = 9


def _pack_rows(vals):
    flat = jnp.concatenate([v.reshape(-1) for v in vals])
    pad = (-flat.shape[0]) % 1024
    return jnp.pad(flat, (0, pad)).reshape(-1, 128)


def _my_block(full, size, axis):
    me = 4 * lax.axis_index("x") + 2 * lax.axis_index("y") + lax.axis_index("c")
    return lax.dynamic_slice_in_dim(full, me * size, size, axis=axis)


def _col_gathered(g):
    return g.transpose(1, 0, 2).reshape(g.shape[1], N_DEV * g.shape[2])


def _col_pieces(dw):
    k, n8 = dw.shape
    return dw.reshape(k, N_DEV, n8 // N_DEV).transpose(1, 0, 2)


def kernel(x, even_norm, even_w_in, even_q_gain, even_k_gain, even_sinks, even_conv_w, even_w_out, odd_norm, odd_w_in, odd_conv_w, odd_a_log, odd_dt_bias, odd_o_gain, odd_w_out, ffn_norm, ffn_w_gate_up, ffn_w_down, loss_target, m_even_norm, m_even_w_in, m_even_q_gain, m_even_k_gain, m_even_sinks, m_even_conv_w, m_even_w_out, m_odd_norm, m_odd_w_in, m_odd_conv_w, m_odd_a_log, m_odd_dt_bias, m_odd_o_gain, m_odd_w_out, m_ffn_norm, m_ffn_w_gate_up, m_ffn_w_down, v_even_norm, v_even_w_in, v_even_q_gain, v_even_k_gain, v_even_sinks, v_even_conv_w, v_even_w_out, v_odd_norm, v_odd_w_in, v_odd_conv_w, v_odd_a_log, v_odd_dt_bias, v_odd_o_gain, v_odd_w_out, v_ffn_norm, v_ffn_w_gate_up, v_ffn_w_down):
    t = x.shape[1]
    d = D_MODEL

    big_shards = {
        "even_w_in": even_w_in.reshape(d, EVEN_IN_W // N_DEV),
        "even_w_out": even_w_out.reshape(d // N_DEV, d),
        "odd_w_in": odd_w_in.reshape(d, ODD_IN_W // N_DEV),
        "odd_w_out": odd_w_out.reshape(d // N_DEV, d),
        "ffn_w_gate_up": ffn_w_gate_up.reshape(2 * d, 2 * D_FF // N_DEV),
        "ffn_w_down": ffn_w_down.reshape(2 * D_FF // N_DEV, d),
    }
    big_names = tuple(big_shards)
    gathered = dict(zip(big_names, all_gather([big_shards[n].astype(MXU_DTYPE) for n in big_names], "gather_weights")))
    shard_rows = _pack_rows([odd_norm, even_conv_w, odd_conv_w])
    (small_g,) = all_gather([shard_rows], "gather_small_weights")

    fpd = D_FF // N_DEV
    wts = {
        "even_w_in": _col_gathered(gathered["even_w_in"]),
        "even_w_out": gathered["even_w_out"].reshape(d, d),
        "odd_w_in": jnp.pad(_col_gathered(gathered["odd_w_in"]), ((0, 0), (0, ODD_IN_PAD - ODD_IN_W))),
        "odd_w_out": gathered["odd_w_out"].reshape(d, d),
    }
    for l in range(2):
        wts[f"ffn_w_gate_up{l}"] = _col_gathered(gathered["ffn_w_gate_up"][:, l * d:(l + 1) * d, :])
        wts[f"ffn_w_down{l}"] = gathered["ffn_w_down"][:, l * fpd:(l + 1) * fpd, :].reshape(D_FF, d)
    sg = small_g.reshape(N_DEV, -1)
    o1 = d // N_DEV
    o2 = o1 + 3 * CONV_CH // N_DEV
    small = {
        "even_norm": even_norm, "even_q_gain": even_q_gain, "even_k_gain": even_k_gain, "even_sinks": even_sinks,
        "odd_a_log": odd_a_log.reshape(-1), "odd_dt_bias": odd_dt_bias.reshape(-1), "odd_o_gain": odd_o_gain,
        "ffn_norm0": ffn_norm[0:1], "ffn_norm1": ffn_norm[1:2],
        "odd_norm": sg[:, :o1].reshape(1, d),
        "even_conv_w": sg[:, o1:o2].reshape(N_DEV, 3, CONV_CH // N_DEV).transpose(1, 0, 2).reshape(3, CONV_CH),
        "odd_conv_w": sg[:, o2:o2 + 4 * _QKV_W // N_DEV].reshape(N_DEV, 4, _QKV_W // N_DEV).transpose(1, 0, 2).reshape(4, _QKV_W),
    }

    loss_row, grad_x, gw, gs = local_step(x.reshape(t, d), loss_target.reshape(t, d), wts, small)

    rows = _pack_rows([gs[n] for n in _SMALL_ORDER] + [loss_row[:, 0:1]])
    (rows_g,) = all_gather([rows], "gather_small_grads")
    tot = sum_rows(rows_g, "sum_small_grads").reshape(-1)
    off, sgrad = 0, {}
    for n in _SMALL_ORDER:
        sgrad[n] = tot[off:off + _SMALL_SIZE[n]]
        off += _SMALL_SIZE[n]
    loss = tot[off]

    pieces = {
        "even_w_in": _col_pieces(gw["even_w_in"]),
        "even_w_out": gw["even_w_out"].reshape(N_DEV, d // N_DEV, d),
        "odd_w_in": _col_pieces(gw["odd_w_in"][:, :ODD_IN_W]),
        "odd_w_out": gw["odd_w_out"].reshape(N_DEV, d // N_DEV, d),
        "ffn_w_gate_up": jnp.concatenate([_col_pieces(gw[f"ffn_w_gate_up{l}"]) for l in range(2)], axis=1),
        "ffn_w_down": jnp.concatenate([gw[f"ffn_w_down{l}"].reshape(N_DEV, fpd, d) for l in range(2)], axis=1),
    }
    landed = dict(zip(big_names, all_to_all([pieces[n].astype(MXU_DTYPE) for n in big_names], "exchange_grads")))
    big_m = {"even_w_in": m_even_w_in, "even_w_out": m_even_w_out, "odd_w_in": m_odd_w_in, "odd_w_out": m_odd_w_out,
             "ffn_w_gate_up": m_ffn_w_gate_up, "ffn_w_down": m_ffn_w_down}
    big_v = {"even_w_in": v_even_w_in, "even_w_out": v_even_w_out, "odd_w_in": v_odd_w_in, "odd_w_out": v_odd_w_out,
             "ffn_w_gate_up": v_ffn_w_gate_up, "ffn_w_down": v_ffn_w_down}
    big_w = {"even_w_in": even_w_in, "even_w_out": even_w_out, "odd_w_in": odd_w_in, "odd_w_out": odd_w_out,
             "ffn_w_gate_up": ffn_w_gate_up, "ffn_w_down": ffn_w_down}
    res = {}
    for n in big_names:
        s2 = big_shards[n].shape
        out = adam_sum(big_shards[n], landed[n], big_m[n].reshape(s2), big_v[n].reshape(s2), f"adamw_{n}")
        res[n] = tuple(o.reshape(big_w[n].shape) for o in out)

    repl = _SMALL_ORDER[:_N_REPL]
    repl_w = {"even_norm": even_norm, "even_q_gain": even_q_gain, "even_k_gain": even_k_gain, "even_sinks": even_sinks,
              "odd_a_log": odd_a_log, "odd_dt_bias": odd_dt_bias, "odd_o_gain": odd_o_gain,
              "ffn_norm0": ffn_norm[0], "ffn_norm1": ffn_norm[1]}
    repl_m = {"even_norm": m_even_norm, "even_q_gain": m_even_q_gain, "even_k_gain": m_even_k_gain,
              "even_sinks": m_even_sinks, "odd_a_log": m_odd_a_log, "odd_dt_bias": m_odd_dt_bias,
              "odd_o_gain": m_odd_o_gain, "ffn_norm0": m_ffn_norm[0], "ffn_norm1": m_ffn_norm[1]}
    repl_v = {"even_norm": v_even_norm, "even_q_gain": v_even_q_gain, "even_k_gain": v_even_k_gain,
              "even_sinks": v_even_sinks, "odd_a_log": v_odd_a_log, "odd_dt_bias": v_odd_dt_bias,
              "odd_o_gain": v_odd_o_gain, "ffn_norm0": v_ffn_norm[0], "ffn_norm1": v_ffn_norm[1]}
    pk = lambda dct: _pack_rows([dct[n] for n in repl])
    pd_, pm_, pv_ = adam_small(pk(repl_w), pk(sgrad), pk(repl_m), pk(repl_v), "adamw_replicated")
    sres = {}
    off = 0
    for n in repl:
        sz = _SMALL_SIZE[n]
        sres[n] = (sgrad[n], pd_.reshape(-1)[off:off + sz], pm_.reshape(-1)[off:off + sz], pv_.reshape(-1)[off:off + sz])
        off += sz
    g_on = _my_block(sgrad["odd_norm"].reshape(1, d), d // N_DEV, 1)
    g_ec = _my_block(sgrad["even_conv_w"].reshape(3, CONV_CH), CONV_CH // N_DEV, 1)
    g_oc = _my_block(sgrad["odd_conv_w"].reshape(4, _QKV_W), _QKV_W // N_DEV, 1)
    shard_w = _pack_rows([odd_norm, even_conv_w, odd_conv_w])
    sd_, sm_, sv_ = adam_small(shard_w, _pack_rows([g_on, g_ec, g_oc]),
                               _pack_rows([m_odd_norm, m_even_conv_w, m_odd_conv_w]),
                               _pack_rows([v_odd_norm, v_even_conv_w, v_odd_conv_w]), "adamw_sharded_small")
    off = 0
    for n, gfull, like in (("odd_norm", g_on, odd_norm), ("even_conv_w", g_ec, even_conv_w), ("odd_conv_w", g_oc, odd_conv_w)):
        sz = like.size
        sres[n] = (gfull, sd_.reshape(-1)[off:off + sz], sm_.reshape(-1)[off:off + sz], sv_.reshape(-1)[off:off + sz])
        off += sz

    def small_out(name, like, kind):
        if name == "ffn_norm":
            return jnp.stack([sres["ffn_norm0"][kind], sres["ffn_norm1"][kind]]).reshape(like.shape)
        return sres[name][kind].reshape(like.shape)

    order = (("even_norm", even_norm), ("even_w_in", even_w_in), ("even_q_gain", even_q_gain),
             ("even_k_gain", even_k_gain), ("even_sinks", even_sinks), ("even_conv_w", even_conv_w),
             ("even_w_out", even_w_out), ("odd_norm", odd_norm), ("odd_w_in", odd_w_in), ("odd_conv_w", odd_conv_w),
             ("odd_a_log", odd_a_log), ("odd_dt_bias", odd_dt_bias), ("odd_o_gain", odd_o_gain),
             ("odd_w_out", odd_w_out), ("ffn_norm", ffn_norm), ("ffn_w_gate_up", ffn_w_gate_up),
             ("ffn_w_down", ffn_w_down))
    outs = [loss, grad_x.reshape(x.shape)]
    for kind in range(4):
        for name, like in order:
            outs.append(res[name][kind] if name in res else small_out(name, like, kind))
    return tuple(outs)
```

```python
import functools

import jax
import jax.numpy as jnp
from jax import lax
from jax.experimental import pallas as pl
from jax.experimental.pallas import tpu as pltpu

F32 = jnp.float32
MXU_DTYPE = jnp.bfloat16
HI = lax.Precision.HIGH
EPS = 1e-6
N_DEV = 8
D_MODEL = 1024
HEAD_DIM = 64
ATTN_HEADS = 8
KV_HEADS = 2
ATTN_BLOCK = 128
Q_W = 512
KV_W = 128
CONV_CH = 512
EVEN_IN_W = 2304
DN_HEADS = 8
DN_DIM = 128
DN_W = 1024
DN_CHUNK = 64
ODD_IN_W = 4112
ODD_IN_PAD = 4224
D_FF = 2816
NEG = -1e30
VMEM_LIMIT = 56 * 1024 * 1024
ADAM_LR, ADAM_B1, ADAM_B2, ADAM_EPS, ADAM_WD, ADAM_STEP = 0.001, 0.9, 0.999, 1e-08, 0.01, 10
MESH = pl.DeviceIdType.MESH


def _cp(*sem):
    return pltpu.CompilerParams(dimension_semantics=sem, vmem_limit_bytes=VMEM_LIMIT)


def _pick(n, cap):
    best = 128
    for t in range(128, cap + 1, 128):
        if n % t == 0:
            best = t
    return best


def _mx(a, b):
    return jnp.dot(a.astype(MXU_DTYPE), b.astype(MXU_DTYPE), preferred_element_type=F32)


def _mx_nt(a, b):
    return lax.dot_general(a.astype(MXU_DTYPE), b.astype(MXU_DTYPE), (((1,), (1,)), ((), ())),
                           preferred_element_type=F32)


def _mx_tn(a, b):
    return lax.dot_general(a.astype(MXU_DTYPE), b.astype(MXU_DTYPE), (((0,), (0,)), ((), ())),
                           preferred_element_type=F32)


def _hi(a, b):
    return jnp.dot(a, b, precision=HI, preferred_element_type=F32)


def _hi_nt(a, b):
    return lax.dot_general(a, b, (((1,), (1,)), ((), ())), precision=HI, preferred_element_type=F32)


def _hi_tn(a, b):
    return lax.dot_general(a, b, (((0,), (0,)), ((), ())), precision=HI, preferred_element_type=F32)


def _sigmoid(x):
    return 1.0 / (1.0 + jnp.exp(-x))


def _softplus(x):
    return jnp.maximum(x, 0.0) + jnp.log(1.0 + jnp.exp(-jnp.abs(x)))


def mm_nn(a, b, name, res=None, out_dtype=F32, tm=512):
    m, k = a.shape
    _, n = b.shape
    tn = _pick(n, 1536)

    def body(*refs):
        a_ref, b_ref = refs[0], refs[1]
        o_ref = refs[-1]
        acc = _mx(a_ref[...], b_ref[...])
        if res is not None:
            acc = acc + refs[2][...]
        o_ref[...] = acc.astype(o_ref.dtype)

    in_specs = [pl.BlockSpec((tm, k), lambda j, i: (i, 0)), pl.BlockSpec((k, tn), lambda j, i: (0, j))]
    args = [a, b]
    if res is not None:
        in_specs.append(pl.BlockSpec((tm, tn), lambda j, i: (i, j)))
        args.append(res)
    return pl.pallas_call(
        body, name=name, grid=(n // tn, m // tm), in_specs=in_specs,
        out_specs=pl.BlockSpec((tm, tn), lambda j, i: (i, j)),
        out_shape=jax.ShapeDtypeStruct((m, n), out_dtype), compiler_params=_cp("parallel", "parallel"))(*args)


def mm_nt(a, b, name, out_dtype=F32, tm=512, after=None):
    m, k = a.shape
    n, _ = b.shape
    tn = _pick(n, 512 if k > 3000 else 1536)

    def body(a_ref, b_ref, *rest):
        o_ref = rest[-1]
        o_ref[...] = _mx_nt(a_ref[...], b_ref[...]).astype(o_ref.dtype)

    in_specs = [pl.BlockSpec((tm, k), lambda j, i: (i, 0)), pl.BlockSpec((tn, k), lambda j, i: (j, 0))]
    args = [a, b]
    if after is not None:
        in_specs.append(pl.BlockSpec(memory_space=pl.ANY))
        args.append(after)
    return pl.pallas_call(
        body, name=name, grid=(n // tn, m // tm), in_specs=in_specs,
        out_specs=pl.BlockSpec((tm, tn), lambda j, i: (i, j)),
        out_shape=jax.ShapeDtypeStruct((m, n), out_dtype), compiler_params=_cp("parallel", "parallel"))(*args)


def mm_tn(a, b, name, tk=512):
    kk, m = a.shape
    _, n = b.shape
    tm = _pick(m, 1408)
    tn = _pick(n, 1408)

    def body(a_ref, b_ref, o_ref):
        k = pl.program_id(2)
        p = _mx_tn(a_ref[...], b_ref[...])

        @pl.when(k == 0)
        def _():
            o_ref[...] = p

        @pl.when(k > 0)
        def _():
            o_ref[...] += p

    return pl.pallas_call(
        body, name=name, grid=(m // tm, n // tn, kk // tk),
        in_specs=[pl.BlockSpec((tk, tm), lambda i, j, k: (k, i)), pl.BlockSpec((tk, tn), lambda i, j, k: (k, j))],
        out_specs=pl.BlockSpec((tm, tn), lambda i, j, k: (i, j)),
        out_shape=jax.ShapeDtypeStruct((m, n), F32), compiler_params=_cp("parallel", "parallel", "arbitrary"))(a, b)


def rms_fwd(x, g, name, tm=512, after=None):
    t, d = x.shape

    def body(x_ref, g_ref, *rest):
        o_ref = rest[-1]
        xv = x_ref[...]
        r = lax.rsqrt(jnp.mean(xv * xv, axis=-1, keepdims=True) + EPS)
        o_ref[...] = (xv * r * g_ref[...]).astype(o_ref.dtype)

    in_specs = [pl.BlockSpec((tm, d), lambda i: (i, 0)), pl.BlockSpec((1, d), lambda i: (0, 0))]
    args = [x, g]
    if after is not None:
        in_specs.append(pl.BlockSpec(memory_space=pl.ANY))
        args.append(after)
    return pl.pallas_call(
        body, name=name, grid=(t // tm,), in_specs=in_specs,
        out_specs=pl.BlockSpec((tm, d), lambda i: (i, 0)),
        out_shape=jax.ShapeDtypeStruct((t, d), MXU_DTYPE), compiler_params=_cp("parallel"))(*args)


def rms_bwd(x, g, dh, dres, name, tm=512):
    t, d = x.shape

    def body(x_ref, g_ref, dh_ref, dres_ref, dx_ref, dg_ref):
        i = pl.program_id(0)
        xv = x_ref[...]
        r = lax.rsqrt(jnp.mean(xv * xv, axis=-1, keepdims=True) + EPS)
        xh = xv * r
        dhv = dh_ref[...]
        dxh = dhv * g_ref[...]
        dx_ref[...] = dres_ref[...] + r * (dxh - xh * jnp.mean(dxh * xh, axis=-1, keepdims=True))
        part = jnp.sum(dhv * xh, axis=0, keepdims=True)

        @pl.when(i == 0)
        def _():
            dg_ref[...] = part

        @pl.when(i > 0)
        def _():
            dg_ref[...] += part

    row = pl.BlockSpec((tm, d), lambda i: (i, 0))
    one = pl.BlockSpec((1, d), lambda i: (0, 0))
    return pl.pallas_call(
        body, name=name, grid=(t // tm,), in_specs=[row, one, row, row], out_specs=(row, one),
        out_shape=(jax.ShapeDtypeStruct((t, d), F32), jax.ShapeDtypeStruct((1, d), F32)),
        compiler_params=_cp("arbitrary"))(x, g, dh, dres)


def swiglu_fwd(gu, name, tm=256):
    t, n2 = gu.shape
    f = n2 // 2

    def body(g_ref, u_ref, o_ref):
        gv = g_ref[...]
        o_ref[...] = (gv * _sigmoid(gv) * u_ref[...]).astype(o_ref.dtype)

    return pl.pallas_call(
        body, name=name, grid=(t // tm,),
        in_specs=[pl.BlockSpec((tm, f), lambda i: (i, 0)), pl.BlockSpec((tm, f), lambda i: (i, 1))],
        out_specs=pl.BlockSpec((tm, f), lambda i: (i, 0)),
        out_shape=jax.ShapeDtypeStruct((t, f), MXU_DTYPE), compiler_params=_cp("parallel"))(gu, gu)


def swiglu_bwd(gu, da, name, tm=256):
    t, n2 = gu.shape
    f = n2 // 2

    def body(g_ref, u_ref, da_ref, dg_ref, du_ref):
        gv = g_ref[...]
        sg = _sigmoid(gv)
        dav = da_ref[...]
        dg_ref[...] = (dav * u_ref[...] * sg * (1.0 + gv * (1.0 - sg))).astype(dg_ref.dtype)
        du_ref[...] = (dav * gv * sg).astype(du_ref.dtype)

    lo = pl.BlockSpec((tm, f), lambda i: (i, 0))
    hi = pl.BlockSpec((tm, f), lambda i: (i, 1))
    dgate, dup = pl.pallas_call(
        body, name=name, grid=(t // tm,), in_specs=[lo, hi, lo], out_specs=(lo, lo),
        out_shape=(jax.ShapeDtypeStruct((t, f), MXU_DTYPE), jax.ShapeDtypeStruct((t, f), MXU_DTYPE)),
        compiler_params=_cp("parallel"))(gu, gu, da)
    return jnp.concatenate([dgate, dup], axis=-1)


def loss_head(y, target, name, tm=512):
    t, d = y.shape

    def body(y_ref, t_ref, dy_ref, l_ref):
        i = pl.program_id(0)
        e = y_ref[...] - t_ref[...]
        dy_ref[...] = e * (1.0 / d)
        part = jnp.zeros((1, 128), F32) + 0.5 * jnp.sum(jnp.mean(e * e, axis=-1, keepdims=True), axis=0, keepdims=True)

        @pl.when(i == 0)
        def _():
            l_ref[...] = part

        @pl.when(i > 0)
        def _():
            l_ref[...] += part

    row = pl.BlockSpec((tm, d), lambda i: (i, 0))
    return pl.pallas_call(
        body, name=name, grid=(t // tm,), in_specs=[row, row],
        out_specs=(row, pl.BlockSpec((1, 128), lambda i: (0, 0))),
        out_shape=(jax.ShapeDtypeStruct((t, d), F32), jax.ShapeDtypeStruct((1, 128), F32)),
        compiler_params=_cp("arbitrary"))(y, target)


def _rot(x):
    return jnp.concatenate([-x[:, HEAD_DIM // 2:], x[:, :HEAD_DIM // 2]], axis=-1)


def _rot_t(y):
    return jnp.concatenate([y[:, HEAD_DIM // 2:], -y[:, :HEAD_DIM // 2]], axis=-1)


def qk_prep_fwd(proj, q_gain, k_gain, cosf, sinf, name, tm=256):
    t = proj.shape[0]
    nh = ATTN_HEADS + KV_HEADS

    def body(p_ref, qg_ref, kg_ref, c_ref, s_ref, q_ref, k_ref):
        c, s = c_ref[...], s_ref[...]
        outs = []
        for h in range(nh):
            xh = p_ref[:, h * HEAD_DIM:(h + 1) * HEAD_DIM]
            gain = qg_ref[...] if h < ATTN_HEADS else kg_ref[...]
            r = lax.rsqrt(jnp.mean(xh * xh, axis=-1, keepdims=True) + EPS)
            xn = xh * r * gain
            outs.append(xn * c + _rot(xn) * s)
        q_ref[...] = jnp.concatenate(outs[:ATTN_HEADS], axis=-1)
        k_ref[...] = jnp.concatenate(outs[ATTN_HEADS:], axis=-1)

    gspec = pl.BlockSpec((1, HEAD_DIM), lambda i: (0, 0))
    tspec = pl.BlockSpec((tm, HEAD_DIM), lambda i: (i, 0))
    return pl.pallas_call(
        body, name=name, grid=(t // tm,),
        in_specs=[pl.BlockSpec((tm, Q_W + KV_W), lambda i: (i, 0)), gspec, gspec, tspec, tspec],
        out_specs=(pl.BlockSpec((tm, Q_W), lambda i: (i, 0)), pl.BlockSpec((tm, KV_W), lambda i: (i, 0))),
        out_shape=(jax.ShapeDtypeStruct((t, Q_W), F32), jax.ShapeDtypeStruct((t, KV_W), F32)),
        compiler_params=_cp("parallel"))(proj, q_gain, k_gain, cosf, sinf)


def qk_prep_bwd(proj, q_gain, k_gain, cosf, sinf, dq, dk, name, tm=256):
    t = proj.shape[0]
    nh = ATTN_HEADS + KV_HEADS

    def body(p_ref, qg_ref, kg_ref, c_ref, s_ref, dq_ref, dk_ref, o_ref, dqg_ref, dkg_ref):
        i = pl.program_id(0)
        c, s = c_ref[...], s_ref[...]
        outs = []
        dqg = jnp.zeros((1, HEAD_DIM), F32)
        dkg = jnp.zeros((1, HEAD_DIM), F32)
        for h in range(nh):
            xh = p_ref[:, h * HEAD_DIM:(h + 1) * HEAD_DIM]
            if h < ATTN_HEADS:
                gain = qg_ref[...]
                dout = dq_ref[:, h * HEAD_DIM:(h + 1) * HEAD_DIM]
            else:
                gain = kg_ref[...]
                dout = dk_ref[:, (h - ATTN_HEADS) * HEAD_DIM:(h - ATTN_HEADS + 1) * HEAD_DIM]
            r = lax.rsqrt(jnp.mean(xh * xh, axis=-1, keepdims=True) + EPS)
            xhat = xh * r
            dxn = dout * c + _rot_t(dout * s)
            part = jnp.sum(dxn * xhat, axis=0, keepdims=True)
            if h < ATTN_HEADS:
                dqg = dqg + part
            else:
                dkg = dkg + part
            dxh = dxn * gain
            outs.append(r * (dxh - xhat * jnp.mean(dxh * xhat, axis=-1, keepdims=True)))
        o_ref[...] = jnp.concatenate(outs, axis=-1).astype(o_ref.dtype)

        @pl.when(i == 0)
        def _():
            dqg_ref[...] = dqg
            dkg_ref[...] = dkg

        @pl.when(i > 0)
        def _():
            dqg_ref[...] += dqg
            dkg_ref[...] += dkg

    gspec = pl.BlockSpec((1, HEAD_DIM), lambda i: (0, 0))
    tspec = pl.BlockSpec((tm, HEAD_DIM), lambda i: (i, 0))
    return pl.pallas_call(
        body, name=name, grid=(t // tm,),
        in_specs=[pl.BlockSpec((tm, Q_W + KV_W), lambda i: (i, 0)), gspec, gspec, tspec, tspec,
                  pl.BlockSpec((tm, Q_W), lambda i: (i, 0)), pl.BlockSpec((tm, KV_W), lambda i: (i, 0))],
        out_specs=(pl.BlockSpec((tm, Q_W + KV_W), lambda i: (i, 0)), gspec, gspec),
        out_shape=(jax.ShapeDtypeStruct((t, Q_W + KV_W), MXU_DTYPE), jax.ShapeDtypeStruct((1, HEAD_DIM), F32),
                   jax.ShapeDtypeStruct((1, HEAD_DIM), F32)),
        compiler_params=_cp("arbitrary"))(proj, q_gain, k_gain, cosf, sinf, dq, dk)


def _swa_valid(n):
    qi = lax.broadcasted_iota(jnp.int32, (ATTN_BLOCK, 2 * ATTN_BLOCK), 0)
    kj = lax.broadcasted_iota(jnp.int32, (ATTN_BLOCK, 2 * ATTN_BLOCK), 1)
    diff = qi + ATTN_BLOCK - kj
    return (diff >= 0) & (diff < ATTN_BLOCK) & (n * ATTN_BLOCK - ATTN_BLOCK + kj >= 0)


def swa_fwd(q, k, proj, sinks, name):
    t = q.shape[0]
    nb = t // ATTN_BLOCK
    scale = HEAD_DIM ** -0.5
    grp = ATTN_HEADS // KV_HEADS

    def body(q_ref, kc_ref, kp_ref, vc_ref, vp_ref, s_ref, y_ref, lse_ref):
        n = pl.program_id(0)
        valid = _swa_valid(n)
        kk = jnp.concatenate([kp_ref[...], kc_ref[...]], axis=0)
        vv = jnp.concatenate([vp_ref[...], vc_ref[...]], axis=0)
        lane = lax.broadcasted_iota(jnp.int32, (ATTN_BLOCK, ATTN_HEADS), 1)
        lse = jnp.zeros((ATTN_BLOCK, ATTN_HEADS), F32)
        outs = []
        for h in range(ATTN_HEADS):
            kv = h // grp
            qh = q_ref[:, h * HEAD_DIM:(h + 1) * HEAD_DIM]
            kh = kk[:, kv * HEAD_DIM:(kv + 1) * HEAD_DIM]
            vh = vv[:, kv * HEAD_DIM:(kv + 1) * HEAD_DIM]
            sc = jnp.where(valid, _mx_nt(qh, kh) * scale, NEG)
            sink = s_ref[0:1, h:h + 1]
            m = jnp.maximum(jnp.max(sc, axis=-1, keepdims=True), sink)
            e = jnp.exp(sc - m)
            den = jnp.sum(e, axis=-1, keepdims=True) + jnp.exp(sink - m)
            outs.append(_mx(e / den, vh))
            lse = jnp.where(lane == h, m + jnp.log(den), lse)
        y_ref[...] = jnp.concatenate(outs, axis=-1)
        lse_ref[...] = lse

    cur = lambda n: (n, 0)
    prev = lambda n: (jnp.maximum(n - 1, 0), 0)
    vcol = (Q_W + KV_W) // KV_W
    return pl.pallas_call(
        body, name=name, grid=(nb,),
        in_specs=[pl.BlockSpec((ATTN_BLOCK, Q_W), cur), pl.BlockSpec((ATTN_BLOCK, KV_W), cur),
                  pl.BlockSpec((ATTN_BLOCK, KV_W), prev),
                  pl.BlockSpec((ATTN_BLOCK, KV_W), lambda n: (n, vcol)),
                  pl.BlockSpec((ATTN_BLOCK, KV_W), lambda n: (jnp.maximum(n - 1, 0), vcol)),
                  pl.BlockSpec((1, ATTN_HEADS), lambda n: (0, 0))],
        out_specs=(pl.BlockSpec((ATTN_BLOCK, Q_W), cur), pl.BlockSpec((ATTN_BLOCK, ATTN_HEADS), cur)),
        out_shape=(jax.ShapeDtypeStruct((t, Q_W), F32), jax.ShapeDtypeStruct((t, ATTN_HEADS), F32)),
        compiler_params=_cp("parallel"))(q, k, k, proj, proj, sinks)


def swa_bwd(q, k, proj, sinks, y, lse, dmix, name):
    t = q.shape[0]
    nb = t // ATTN_BLOCK
    scale = HEAD_DIM ** -0.5
    grp = ATTN_HEADS // KV_HEADS

    def body(q_ref, kc_ref, kp_ref, vc_ref, vp_ref, s_ref, y_ref, lse_ref, dy_ref,
             dq_ref, dk_ref, dv_ref, ds_ref, dkc, dvc):
        n = pl.program_id(0)

        @pl.when(n == 0)
        def _():
            dkc[...] = jnp.zeros_like(dkc)
            dvc[...] = jnp.zeros_like(dvc)
            ds_ref[...] = jnp.zeros_like(ds_ref)

        @pl.when(n < nb)
        def _():
            valid = _swa_valid(n)
            kk = jnp.concatenate([kp_ref[...], kc_ref[...]], axis=0)
            vv = jnp.concatenate([vp_ref[...], vc_ref[...]], axis=0)
            lane = lax.broadcasted_iota(jnp.int32, (1, ATTN_HEADS), 1)
            dsink = jnp.zeros((1, ATTN_HEADS), F32)
            dqs, dks, dvs = [], [], []
            for h in range(ATTN_HEADS):
                kv = h // grp
                qh = q_ref[:, h * HEAD_DIM:(h + 1) * HEAD_DIM]
                kh = kk[:, kv * HEAD_DIM:(kv + 1) * HEAD_DIM]
                vh = vv[:, kv * HEAD_DIM:(kv + 1) * HEAD_DIM]
                do = dy_ref[:, h * HEAD_DIM:(h + 1) * HEAD_DIM]
                oh = y_ref[:, h * HEAD_DIM:(h + 1) * HEAD_DIM]
                lse_h = lse_ref[:, h:h + 1]
                sc = jnp.where(valid, _mx_nt(qh, kh) * scale, NEG)
                p = jnp.exp(sc - lse_h)
                delta = jnp.sum(do * oh, axis=-1, keepdims=True)
                ds = p * (_mx_nt(do, vh) - delta)
                dqs.append(_mx(ds, kh) * scale)
                dkh = _mx_tn(ds, qh) * scale
                dvh = _mx_tn(p, do)
                if h % grp == 0:
                    dks.append(dkh)
                    dvs.append(dvh)
                else:
                    dks[kv] = dks[kv] + dkh
                    dvs[kv] = dvs[kv] + dvh
                dsh = -jnp.sum(jnp.exp(s_ref[0:1, h:h + 1] - lse_h) * delta, axis=0, keepdims=True)
                dsink = jnp.where(lane == h, dsh, dsink)
            dq_ref[...] = jnp.concatenate(dqs, axis=-1)
            dkf = jnp.concatenate(dks, axis=-1)
            dvf = jnp.concatenate(dvs, axis=-1)
            dk_ref[...] = dkc[...] + dkf[:ATTN_BLOCK]
            dv_ref[...] = (dvc[...] + dvf[:ATTN_BLOCK]).astype(dv_ref.dtype)
            dkc[...] = dkf[ATTN_BLOCK:]
            dvc[...] = dvf[ATTN_BLOCK:]
            ds_ref[...] += dsink

        @pl.when(n == nb)
        def _():
            dk_ref[...] = dkc[...]
            dv_ref[...] = dvc[...].astype(dv_ref.dtype)

    cur = lambda n: (jnp.minimum(n, nb - 1), 0)
    prev = lambda n: (jnp.clip(n - 1, 0, nb - 1), 0)
    vcol = (Q_W + KV_W) // KV_W
    return pl.pallas_call(
        body, name=name, grid=(nb + 1,),
        in_specs=[pl.BlockSpec((ATTN_BLOCK, Q_W), cur), pl.BlockSpec((ATTN_BLOCK, KV_W), cur),
                  pl.BlockSpec((ATTN_BLOCK, KV_W), prev),
                  pl.BlockSpec((ATTN_BLOCK, KV_W), lambda n: (jnp.minimum(n, nb - 1), vcol)),
                  pl.BlockSpec((ATTN_BLOCK, KV_W), lambda n: (jnp.clip(n - 1, 0, nb - 1), vcol)),
                  pl.BlockSpec((1, ATTN_HEADS), lambda n: (0, 0)),
                  pl.BlockSpec((ATTN_BLOCK, Q_W), cur), pl.BlockSpec((ATTN_BLOCK, ATTN_HEADS), cur),
                  pl.BlockSpec((ATTN_BLOCK, Q_W), cur)],
        out_specs=(pl.BlockSpec((ATTN_BLOCK, Q_W), cur), pl.BlockSpec((ATTN_BLOCK, KV_W), prev),
                   pl.BlockSpec((ATTN_BLOCK, KV_W), prev), pl.BlockSpec((1, ATTN_HEADS), lambda n: (0, 0))),
        out_shape=(jax.ShapeDtypeStruct((t, Q_W), F32), jax.ShapeDtypeStruct((t, KV_W), F32),
                   jax.ShapeDtypeStruct((t, KV_W), MXU_DTYPE), jax.ShapeDtypeStruct((1, ATTN_HEADS), F32)),
        scratch_shapes=[pltpu.VMEM((ATTN_BLOCK, KV_W), F32), pltpu.VMEM((ATTN_BLOCK, KV_W), F32)],
        compiler_params=_cp("arbitrary"))(q, k, k, proj, proj, sinks, y, lse, dmix)


GC_W = 256
_GB0, _GC0, _XI0 = 768 // GC_W, 1280 // GC_W, 1792 // GC_W
HALO = 8


def gconv_fwd(proj, conv_w, name, tm=512):
    t = proj.shape[0]
    hb = tm // HALO

    def body(gb_ref, gc_ref, xi_ref, gch_ref, xih_ref, w_ref, y_ref):
        i = pl.program_id(1)
        u = gc_ref[...] * xi_ref[...]
        uh = jnp.where(i == 0, 0.0, gch_ref[...] * xih_ref[...])
        up = jnp.concatenate([uh, u], axis=0)
        cv = w_ref[0:1, :] * up[HALO - 2:HALO - 2 + tm]
        cv = cv + w_ref[1:2, :] * up[HALO - 1:HALO - 1 + tm]
        cv = cv + w_ref[2:3, :] * u
        y_ref[...] = (gb_ref[...] * cv).astype(y_ref.dtype)

    def col(c0):
        return pl.BlockSpec((tm, GC_W), lambda cj, i: (i, c0 + cj))

    def halo(c0):
        return pl.BlockSpec((HALO, GC_W), lambda cj, i: (jnp.maximum(i * hb - 1, 0), c0 + cj))

    return pl.pallas_call(
        body, name=name, grid=(CONV_CH // GC_W, t // tm),
        in_specs=[col(_GB0), col(_GC0), col(_XI0), halo(_GC0), halo(_XI0),
                  pl.BlockSpec((3, GC_W), lambda cj, i: (0, cj))],
        out_specs=pl.BlockSpec((tm, GC_W), lambda cj, i: (i, cj)),
        out_shape=jax.ShapeDtypeStruct((t, CONV_CH), MXU_DTYPE),
        compiler_params=_cp("parallel", "parallel"))(proj, proj, proj, proj, proj, conv_w)


def gconv_bwd(proj, conv_w, dmix, name, tm=512):
    t = proj.shape[0]
    hb = tm // HALO
    nt = t // tm
    dy0 = Q_W // GC_W

    def body(gb_ref, gc_ref, xi_ref, gch_ref, xih_ref, gbn_ref, dyn_ref, dy_ref, w_ref,
             dgb_ref, dgc_ref, dxi_ref, dw_ref):
        i = pl.program_id(1)
        gc, xi, gb, dy = gc_ref[...], xi_ref[...], gb_ref[...], dy_ref[...]
        u = gc * xi
        uh = jnp.where(i == 0, 0.0, gch_ref[...] * xih_ref[...])
        up = jnp.concatenate([uh, u], axis=0)
        u2 = up[HALO - 2:HALO - 2 + tm]
        u1 = up[HALO - 1:HALO - 1 + tm]
        cv = w_ref[0:1, :] * u2 + w_ref[1:2, :] * u1 + w_ref[2:3, :] * u
        dgb_ref[...] = (dy * cv).astype(dgb_ref.dtype)
        dcv = dy * gb
        dcvn = jnp.where(i == nt - 1, 0.0, dyn_ref[...] * gbn_ref[...])
        dcvp = jnp.concatenate([dcv, dcvn], axis=0)
        du = w_ref[0:1, :] * dcvp[2:2 + tm] + w_ref[1:2, :] * dcvp[1:1 + tm] + w_ref[2:3, :] * dcv
        dgc_ref[...] = (du * xi).astype(dgc_ref.dtype)
        dxi_ref[...] = (du * gc).astype(dxi_ref.dtype)
        dw = jnp.concatenate([jnp.sum(dcv * u2, axis=0, keepdims=True), jnp.sum(dcv * u1, axis=0, keepdims=True),
                              jnp.sum(dcv * u, axis=0, keepdims=True)], axis=0)

        @pl.when(i == 0)
        def _():
            dw_ref[...] = dw

        @pl.when(i > 0)
        def _():
            dw_ref[...] += dw

    def col(c0):
        return pl.BlockSpec((tm, GC_W), lambda cj, i: (i, c0 + cj))

    def halo(c0):
        return pl.BlockSpec((HALO, GC_W), lambda cj, i: (jnp.maximum(i * hb - 1, 0), c0 + cj))

    def nxt(c0):
        return pl.BlockSpec((HALO, GC_W), lambda cj, i: (jnp.minimum((i + 1) * hb, t // HALO - 1), c0 + cj))

    out = pl.BlockSpec((tm, GC_W), lambda cj, i: (i, cj))
    return pl.pallas_call(
        body, name=name, grid=(CONV_CH // GC_W, nt),
        in_specs=[col(_GB0), col(_GC0), col(_XI0), halo(_GC0), halo(_XI0), nxt(_GB0), nxt(dy0), col(dy0),
                  pl.BlockSpec((3, GC_W), lambda cj, i: (0, cj))],
        out_specs=(out, out, out, pl.BlockSpec((3, GC_W), lambda cj, i: (0, cj))),
        out_shape=(jax.ShapeDtypeStruct((t, CONV_CH), MXU_DTYPE),) * 3 + (jax.ShapeDtypeStruct((3, CONV_CH), F32),),
        compiler_params=_cp("parallel", "arbitrary"))(proj, proj, proj, proj, proj, proj, dmix, dmix, conv_w)


_QKV_W = 3 * DN_W
_BA_COL = (4 * DN_W) // 128
_Z_COL = _QKV_W // DN_W


def gdn_prep_fwd(proj, conv_w, alog_row, dtb_row, name, tm=256):
    t = proj.shape[0]
    hb = tm // HALO
    qscale = DN_DIM ** -0.5

    def body(x_ref, xh_ref, w_ref, ba_ref, al_ref, dt_ref, q_ref, k_ref, v_ref, bg_ref):
        i = pl.program_id(0)
        for gi in range(3 * DN_HEADS):
            sl = slice(gi * DN_DIM, (gi + 1) * DN_DIM)
            xp = jnp.concatenate([jnp.where(i == 0, 0.0, xh_ref[:, sl]), x_ref[:, sl]], axis=0)
            c = w_ref[0:1, sl] * xp[HALO - 3:HALO - 3 + tm]
            for j in range(1, 4):
                c = c + w_ref[j:j + 1, sl] * xp[HALO - 3 + j:HALO - 3 + j + tm]
            s = c * _sigmoid(c)
            osl = slice((gi % DN_HEADS) * DN_DIM, (gi % DN_HEADS + 1) * DN_DIM)
            if gi < DN_HEADS:
                q_ref[:, osl] = s * lax.rsqrt(jnp.sum(s * s, axis=-1, keepdims=True) + EPS) * qscale
            elif gi < 2 * DN_HEADS:
                k_ref[:, osl] = s * lax.rsqrt(jnp.sum(s * s, axis=-1, keepdims=True) + EPS)
            else:
                v_ref[:, osl] = s
        ba = ba_ref[...]
        lane = lax.broadcasted_iota(jnp.int32, ba.shape, 1)
        gval = -jnp.exp(al_ref[...]) * _softplus(ba + dt_ref[...])
        bg_ref[...] = jnp.where(lane < DN_HEADS, _sigmoid(ba), jnp.where(lane < 2 * DN_HEADS, gval, 0.0))

    row = pl.BlockSpec((tm, DN_W), lambda i: (i, 0))
    one = pl.BlockSpec((1, 128), lambda i: (0, 0))
    return pl.pallas_call(
        body, name=name, grid=(t // tm,),
        in_specs=[pl.BlockSpec((tm, _QKV_W), lambda i: (i, 0)),
                  pl.BlockSpec((HALO, _QKV_W), lambda i: (jnp.maximum(i * hb - 1, 0), 0)),
                  pl.BlockSpec((4, _QKV_W), lambda i: (0, 0)),
                  pl.BlockSpec((tm, 128), lambda i: (i, _BA_COL)), one, one],
        out_specs=(row, row, row, pl.BlockSpec((tm, 128), lambda i: (i, 0))),
        out_shape=(jax.ShapeDtypeStruct((t, DN_W), F32),) * 3 + (jax.ShapeDtypeStruct((t, 128), F32),),
        compiler_params=_cp("parallel"))(proj, proj, conv_w, proj, alog_row, dtb_row)


def gdn_prep_bwd(proj, conv_w, alog_row, dtb_row, dq, dk, dv, dbg, name, tm=256):
    t = proj.shape[0]
    hb = tm // HALO
    nt = t // tm
    qscale = DN_DIM ** -0.5
    te = tm + HALO

    def body(x_ref, xh_ref, xn_ref, w_ref, ba_ref, al_ref, dt_ref, dq_ref, dk_ref, dv_ref,
             dqn_ref, dkn_ref, dvn_ref, dbg_ref, dx_ref, dba_ref, dw_ref, ddt_ref, dal_ref):
        i = pl.program_id(0)
        first = i == 0
        last = i == nt - 1
        dws = []
        for gi in range(3 * DN_HEADS):
            sl = slice(gi * DN_DIM, (gi + 1) * DN_DIM)
            osl = slice((gi % DN_HEADS) * DN_DIM, (gi % DN_HEADS + 1) * DN_DIM)
            xe = jnp.concatenate([jnp.where(first, 0.0, xh_ref[:, sl]), x_ref[:, sl], xn_ref[:, sl]], axis=0)
            c = w_ref[0:1, sl] * xe[HALO - 3:HALO - 3 + te]
            for j in range(1, 4):
                c = c + w_ref[j:j + 1, sl] * xe[HALO - 3 + j:HALO - 3 + j + te]
            sg = _sigmoid(c)
            s = c * sg
            d_ref, dn_ref = ((dq_ref, dqn_ref), (dk_ref, dkn_ref), (dv_ref, dvn_ref))[gi // DN_HEADS]
            dy = jnp.concatenate([d_ref[:, osl], jnp.where(last, 0.0, dn_ref[:, osl])], axis=0)
            if gi < 2 * DN_HEADS:
                r = lax.rsqrt(jnp.sum(s * s, axis=-1, keepdims=True) + EPS)
                sh = s * r
                ds = r * (dy - sh * jnp.sum(sh * dy, axis=-1, keepdims=True))
                if gi < DN_HEADS:
                    ds = ds * qscale
            else:
                ds = dy
            dc = ds * sg * (1.0 + c * (1.0 - sg))
            dx = w_ref[0:1, sl] * dc[3:3 + tm]
            for j in range(1, 4):
                dx = dx + w_ref[j:j + 1, sl] * dc[3 - j:3 - j + tm]
            dx_ref[:, sl] = dx.astype(dx_ref.dtype)
            dc0 = dc[:tm]
            dws.append(jnp.concatenate(
                [jnp.sum(dc0 * xe[HALO - 3 + j:HALO - 3 + j + tm], axis=0, keepdims=True) for j in range(4)], axis=0))
        dw = jnp.concatenate(dws, axis=-1)
        ba = ba_ref[...]
        dbgv = dbg_ref[...]
        lane = lax.broadcasted_iota(jnp.int32, ba.shape, 1)
        beta = _sigmoid(ba)
        ea = -jnp.exp(al_ref[...])
        zin = ba + dt_ref[...]
        is_b = lane < DN_HEADS
        is_a = (lane >= DN_HEADS) & (lane < 2 * DN_HEADS)
        da = jnp.where(is_a, dbgv * ea * _sigmoid(zin), 0.0)
        dba_ref[...] = jnp.where(is_b, dbgv * beta * (1.0 - beta), da).astype(dba_ref.dtype)
        ddt = jnp.sum(da, axis=0, keepdims=True)
        dal = jnp.sum(jnp.where(is_a, dbgv * ea * _softplus(zin), 0.0), axis=0, keepdims=True)

        @pl.when(first)
        def _():
            dw_ref[...] = dw
            ddt_ref[...] = ddt
            dal_ref[...] = dal

        @pl.when(i > 0)
        def _():
            dw_ref[...] += dw
            ddt_ref[...] += ddt
            dal_ref[...] += dal

    row = pl.BlockSpec((tm, DN_W), lambda i: (i, 0))
    nrow = pl.BlockSpec((HALO, DN_W), lambda i: (jnp.minimum((i + 1) * hb, t // HALO - 1), 0))
    one = pl.BlockSpec((1, 128), lambda i: (0, 0))
    return pl.pallas_call(
        body, name=name, grid=(nt,),
        in_specs=[pl.BlockSpec((tm, _QKV_W), lambda i: (i, 0)),
                  pl.BlockSpec((HALO, _QKV_W), lambda i: (jnp.maximum(i * hb - 1, 0), 0)),
                  pl.BlockSpec((HALO, _QKV_W), lambda i: (jnp.minimum((i + 1) * hb, t // HALO - 1), 0)),
                  pl.BlockSpec((4, _QKV_W), lambda i: (0, 0)),
                  pl.BlockSpec((tm, 128), lambda i: (i, _BA_COL)), one, one,
                  row, row, row, nrow, nrow, nrow, pl.BlockSpec((tm, 128), lambda i: (i, 0))],
        out_specs=(pl.BlockSpec((tm, _QKV_W), lambda i: (i, 0)), pl.BlockSpec((tm, 128), lambda i: (i, 0)),
                   pl.BlockSpec((4, _QKV_W), lambda i: (0, 0)), one, one),
        out_shape=(jax.ShapeDtypeStruct((t, _QKV_W), MXU_DTYPE), jax.ShapeDtypeStruct((t, 128), MXU_DTYPE),
                   jax.ShapeDtypeStruct((4, _QKV_W), F32), jax.ShapeDtypeStruct((1, 128), F32),
                   jax.ShapeDtypeStruct((1, 128), F32)),
        compiler_params=_cp("arbitrary"))(proj, proj, proj, conv_w, proj, alog_row, dtb_row, dq, dk, dv, dq, dk, dv, dbg)


def _chunk_masks():
    r = lax.broadcasted_iota(jnp.int32, (DN_CHUNK, DN_CHUNK), 0)
    c = lax.broadcasted_iota(jnp.int32, (DN_CHUNK, DN_CHUNK), 1)
    return r >= c, r > c


def _inv_unit_lower_many(mats):
    r = lax.broadcasted_iota(jnp.int32, mats[0].shape, 0)
    c = lax.broadcasted_iota(jnp.int32, mats[0].shape, 1)
    eye = jnp.where(r == c, 1.0, 0.0)
    xs = [eye - a for a in mats]
    pws = [_hi(a, a) for a in mats]
    for step in range(5):
        xs = [x + _hi(x, pw) for x, pw in zip(xs, pws)]
        if step < 4:
            pws = [_hi(pw, pw) for pw in pws]
    return xs


def _chunk_common(q, k, v, beta, gc, gcr, lower, strict):
    gam = jnp.exp(jnp.where(lower, gc - gcr, NEG))
    eg = jnp.exp(gc)
    gl = gc[DN_CHUNK - 1:DN_CHUNK, :]
    kdf = jnp.exp(gl - gc)
    kb = k * beta
    bmat = _mx_nt(kb, k)
    qmat = _mx_nt(q, k)
    return gam, eg, jnp.exp(gl), kdf, kb, bmat, qmat


def gdn_fwd(q, k, v, bg, name):
    t = q.shape[0]
    n_chunks = t // DN_CHUNK

    def body(q_ref, k_ref, v_ref, bg_ref, o_ref, sall_ref, tall_ref, s_ref):
        n = pl.program_id(0)

        @pl.when(n == 0)
        def _():
            s_ref[...] = jnp.zeros_like(s_ref)

        lower, strict = _chunk_masks()
        bgv = bg_ref[...]
        gcs = _hi(jnp.where(lower, 1.0, 0.0), bgv)
        gcs_t = gcs.T
        hs = range(DN_HEADS)
        sl = [slice(h * DN_DIM, (h + 1) * DN_DIM) for h in hs]
        st = [s_ref[h] for h in hs]
        for h in hs:
            sall_ref[0, h] = st[h]
        qh = [q_ref[:, sl[h]] for h in hs]
        kh = [k_ref[:, sl[h]] for h in hs]
        vh = [v_ref[:, sl[h]] for h in hs]
        beta = [bgv[:, h:h + 1] for h in hs]
        com = [_chunk_common(qh[h], kh[h], vh[h], beta[h], gcs[:, DN_HEADS + h:DN_HEADS + h + 1],
                             gcs_t[DN_HEADS + h:DN_HEADS + h + 1, :], lower, strict) for h in hs]
        gam, eg, dec, kdf, kb, bmat, qmat = zip(*com)
        tms = _inv_unit_lower_many([jnp.where(strict, bmat[h] * gam[h], 0.0) for h in hs])
        for h in hs:
            tall_ref[0, h] = tms[h]
        uw = [_hi(tms[h], jnp.concatenate([vh[h] * beta[h], kb[h] * eg[h]], axis=-1)) for h in hs]
        v_new = [uw[h][:, :DN_DIM] - _mx(uw[h][:, DN_DIM:], st[h]) for h in hs]
        o_st = [_mx(qh[h] * eg[h], st[h]) for h in hs]
        o_in = [_mx(qmat[h] * gam[h], v_new[h]) for h in hs]
        s_up = [_mx_tn(kh[h] * kdf[h], v_new[h]) for h in hs]
        for h in hs:
            o_ref[:, sl[h]] = o_st[h] + o_in[h]
            s_ref[h] = st[h] * dec[h] + s_up[h]

    row = pl.BlockSpec((DN_CHUNK, DN_W), lambda n: (n, 0))
    return pl.pallas_call(
        body, name=name, grid=(n_chunks,),
        in_specs=[row, row, row, pl.BlockSpec((DN_CHUNK, 128), lambda n: (n, 0))],
        out_specs=(row, pl.BlockSpec((1, DN_HEADS, DN_DIM, DN_DIM), lambda n: (n, 0, 0, 0)),
                   pl.BlockSpec((1, DN_HEADS, DN_CHUNK, DN_CHUNK), lambda n: (n, 0, 0, 0))),
        out_shape=(jax.ShapeDtypeStruct((t, DN_W), F32),
                   jax.ShapeDtypeStruct((n_chunks, DN_HEADS, DN_DIM, DN_DIM), F32),
                   jax.ShapeDtypeStruct((n_chunks, DN_HEADS, DN_CHUNK, DN_CHUNK), F32)),
        scratch_shapes=[pltpu.VMEM((DN_HEADS, DN_DIM, DN_DIM), F32)],
        compiler_params=_cp("arbitrary"))(q, k, v, bg)


def gdn_bwd(q, k, v, bg, sall, tall, do, name):
    t = q.shape[0]
    n_chunks = t // DN_CHUNK

    def body(q_ref, k_ref, v_ref, bg_ref, sall_ref, tall_ref, do_ref, dq_ref, dk_ref, dv_ref, dbg_ref, ds_ref):
        n = pl.program_id(0)

        @pl.when(n == 0)
        def _():
            ds_ref[...] = jnp.zeros_like(ds_ref)

        lower, strict = _chunk_masks()
        ltri = jnp.where(lower, 1.0, 0.0)
        bgv = bg_ref[...]
        gcs = _hi(ltri, bgv)
        gcs_t = gcs.T
        lane = lax.broadcasted_iota(jnp.int32, (DN_CHUNK, 128), 1)
        rowi = lax.broadcasted_iota(jnp.int32, (DN_CHUNK, 1), 0)
        hs = range(DN_HEADS)
        each = lambda fn, *ls: [fn(*a) for a in zip(*ls)]
        rsum = lambda a: jnp.sum(a, axis=-1, keepdims=True)
        sl = [slice(h * DN_DIM, (h + 1) * DN_DIM) for h in hs]
        st = [sall_ref[0, h] for h in hs]
        tms = [tall_ref[0, h] for h in hs]
        dsn = [ds_ref[h] for h in hs]
        qh = [q_ref[:, sl[h]] for h in hs]
        kh = [k_ref[:, sl[h]] for h in hs]
        vh = [v_ref[:, sl[h]] for h in hs]
        doh = [do_ref[:, sl[h]] for h in hs]
        beta = [bgv[:, h:h + 1] for h in hs]
        com = [_chunk_common(qh[h], kh[h], vh[h], beta[h], gcs[:, DN_HEADS + h:DN_HEADS + h + 1],
                             gcs_t[DN_HEADS + h:DN_HEADS + h + 1, :], lower, strict) for h in hs]
        gam, eg, dec, kdf, kb, bmat, qmat = zip(*com)
        rhs_w = each(lambda a, b: a * b, kb, eg)
        uw = each(lambda t_, v_, b_, r_: _hi(t_, jnp.concatenate([v_ * b_, r_], axis=-1)), tms, vh, beta, rhs_w)
        qd = each(lambda a, b: a * b, qh, eg)
        kd = each(lambda a, b: a * b, kh, kdf)
        pmat = each(lambda a, b: a * b, qmat, gam)
        v_new = each(lambda uw_, s_: uw_[:, :DN_DIM] - _mx(uw_[:, DN_DIM:], s_), uw, st)
        dqd = each(_mx_nt, doh, st)
        ds_o = each(_mx_tn, qd, doh)
        dp = each(lambda d_, v_: jnp.where(lower, _mx_nt(d_, v_), 0.0), doh, v_new)
        dvn_o = each(_mx_tn, pmat, doh)
        ddec = each(lambda d_, s_: jnp.sum(rsum(d_ * s_), axis=0, keepdims=True), dsn, st)
        dkd = each(_mx_nt, v_new, dsn)
        dvn = each(lambda a, k_, d_: a + _mx(k_, d_), dvn_o, kd, dsn)
        dw = each(lambda d_, s_: -_mx_nt(d_, s_), dvn, st)
        ds_w = each(lambda uw_, d_: _mx_tn(uw_[:, DN_DIM:], d_), uw, dvn)
        for h in hs:
            ds_ref[h] = ds_o[h] + dec[h] * dsn[h] - ds_w[h]
        dr = each(lambda t_, a, b: _hi_tn(t_, jnp.concatenate([a, b], axis=-1)), tms, dvn, dw)
        da = each(lambda r_, uw_: jnp.where(strict, -_hi_nt(r_, uw_), 0.0), dr, uw)
        dru = [r_[:, :DN_DIM] for r_ in dr]
        drw = [r_[:, DN_DIM:] for r_ in dr]
        db = each(lambda a, b: a * b, da, gam)
        dq_m = each(lambda a, b: a * b, dp, gam)
        e = each(lambda a, bm, p_, qm, g_: (a * bm + p_ * qm) * g_, da, bmat, dp, qmat, gam)
        dkb = each(lambda b_, k_, r_, e_: _mx(b_, k_) + r_ * e_, db, kh, drw, eg)
        dk = each(lambda b_, kb_, m_, q_, d_, f_: _mx_tn(b_, kb_) + _mx_tn(m_, q_) + d_ * f_, db, kb, dq_m, qh, dkd, kdf)
        dq = each(lambda m_, k_, d_, e_: _mx(m_, k_) + d_ * e_, dq_m, kh, dqd, eg)
        tk = each(lambda a, b: rsum(a * b), dkd, kd)
        dbeta_all = jnp.zeros((DN_CHUNK, 128), F32)
        dgc_all = jnp.zeros((DN_CHUNK, 128), F32)
        for h in hs:
            dgc = (jnp.sum(e[h], axis=1, keepdims=True) - jnp.sum(e[h].T, axis=1, keepdims=True)
                   + rsum(dqd[h] * qd[h]) - tk[h] + rsum(drw[h] * rhs_w[h]))
            dgl = jnp.sum(tk[h], axis=0, keepdims=True) + ddec[h] * dec[h]
            dgc = dgc + jnp.where(rowi == DN_CHUNK - 1, dgl, 0.0)
            dbeta = rsum(dru[h] * vh[h]) + rsum(dkb[h] * kh[h])
            dq_ref[:, sl[h]] = dq[h]
            dk_ref[:, sl[h]] = dk[h] + dkb[h] * beta[h]
            dv_ref[:, sl[h]] = dru[h] * beta[h]
            dbeta_all = jnp.where(lane == h, dbeta, dbeta_all)
            dgc_all = jnp.where(lane == DN_HEADS + h, dgc, dgc_all)
        dbg_ref[...] = dbeta_all + _hi_tn(ltri, dgc_all)

    rev = lambda n: (n_chunks - 1 - n, 0)
    row = pl.BlockSpec((DN_CHUNK, DN_W), rev)
    small = pl.BlockSpec((DN_CHUNK, 128), rev)
    return pl.pallas_call(
        body, name=name, grid=(n_chunks,),
        in_specs=[row, row, row, small,
                  pl.BlockSpec((1, DN_HEADS, DN_DIM, DN_DIM), lambda n: (n_chunks - 1 - n, 0, 0, 0)),
                  pl.BlockSpec((1, DN_HEADS, DN_CHUNK, DN_CHUNK), lambda n: (n_chunks - 1 - n, 0, 0, 0)), row],
        out_specs=(row, row, row, small),
        out_shape=(jax.ShapeDtypeStruct((t, DN_W), F32),) * 3 + (jax.ShapeDtypeStruct((t, 128), F32),),
        scratch_shapes=[pltpu.VMEM((DN_HEADS, DN_DIM, DN_DIM), F32)],
        compiler_params=_cp("arbitrary"))(q, k, v, bg, sall, tall, do)


def gdn_out_fwd(o, proj, o_gain, name, tm=256):
    t = o.shape[0]

    def body(o_ref, z_ref, g_ref, y_ref):
        for h in range(DN_HEADS):
            sl = slice(h * DN_DIM, (h + 1) * DN_DIM)
            ov, zv = o_ref[:, sl], z_ref[:, sl]
            r = lax.rsqrt(jnp.mean(ov * ov, axis=-1, keepdims=True) + EPS)
            y_ref[:, sl] = (ov * r * g_ref[...] * (zv * _sigmoid(zv))).astype(y_ref.dtype)

    row = pl.BlockSpec((tm, DN_W), lambda i: (i, 0))
    return pl.pallas_call(
        body, name=name, grid=(t // tm,),
        in_specs=[row, pl.BlockSpec((tm, DN_W), lambda i: (i, _Z_COL)), pl.BlockSpec((1, DN_DIM), lambda i: (0, 0))],
        out_specs=row, out_shape=jax.ShapeDtypeStruct((t, DN_W), MXU_DTYPE),
        compiler_params=_cp("parallel"))(o, proj, o_gain)


def gdn_out_bwd(o, proj, o_gain, dy, name, tm=256):
    t = o.shape[0]

    def body(o_ref, z_ref, g_ref, dy_ref, do_ref, dz_ref, dg_ref):
        i = pl.program_id(0)
        dg = jnp.zeros((1, DN_DIM), F32)
        for h in range(DN_HEADS):
            sl = slice(h * DN_DIM, (h + 1) * DN_DIM)
            ov, zv, dyv = o_ref[:, sl], z_ref[:, sl], dy_ref[:, sl]
            r = lax.rsqrt(jnp.mean(ov * ov, axis=-1, keepdims=True) + EPS)
            oh = ov * r
            sg = _sigmoid(zv)
            dz_ref[:, sl] = (dyv * oh * g_ref[...] * sg * (1.0 + zv * (1.0 - sg))).astype(dz_ref.dtype)
            don = dyv * (zv * sg)
            dg = dg + jnp.sum(don * oh, axis=0, keepdims=True)
            doh = don * g_ref[...]
            do_ref[:, sl] = r * (doh - oh * jnp.mean(doh * oh, axis=-1, keepdims=True))

        @pl.when(i == 0)
        def _():
            dg_ref[...] = dg

        @pl.when(i > 0)
        def _():
            dg_ref[...] += dg

    row = pl.BlockSpec((tm, DN_W), lambda i: (i, 0))
    one = pl.BlockSpec((1, DN_DIM), lambda i: (0, 0))
    return pl.pallas_call(
        body, name=name, grid=(t // tm,),
        in_specs=[row, pl.BlockSpec((tm, DN_W), lambda i: (i, _Z_COL)), one, row],
        out_specs=(row, row, one),
        out_shape=(jax.ShapeDtypeStruct((t, DN_W), F32), jax.ShapeDtypeStruct((t, DN_W), MXU_DTYPE),
                   jax.ShapeDtypeStruct((1, DN_DIM), F32)),
        compiler_params=_cp("arbitrary"))(o, proj, o_gain, dy)


def _peer(k):
    x, y, c = lax.axis_index("x"), lax.axis_index("y"), lax.axis_index("c")
    px = 1 - x if k & 4 else x
    py = 1 - y if k & 2 else y
    pc = 1 - c if k & 1 else c
    return (px, py, pc), 4 * px + 2 * py + pc


def all_gather(shards, name):
    na = len(shards)

    def body(*refs):
        ins, outs = refs[:na], refs[na:2 * na]
        send_sems, recv_sems, local_sems = refs[2 * na:]
        _, me = _peer(0)
        local = [pltpu.make_async_copy(ins[a], outs[a].at[me], local_sems.at[a]) for a in range(na)]
        for cp in local:
            cp.start()
        sends = []
        for k in range(1, N_DEV):
            peer, _ = _peer(k)
            for a in range(na):
                cp = pltpu.make_async_remote_copy(
                    src_ref=ins[a], dst_ref=outs[a].at[me], send_sem=send_sems.at[a, k - 1],
                    recv_sem=recv_sems.at[a, k - 1], device_id=peer, device_id_type=MESH)
                cp.start()
                sends.append(cp)
        for k in range(1, N_DEV):
            peer, pid = _peer(k)
            for a in range(na):
                pltpu.make_async_remote_copy(
                    src_ref=ins[a], dst_ref=outs[a].at[pid], send_sem=send_sems.at[a, k - 1],
                    recv_sem=recv_sems.at[a, k - 1], device_id=peer, device_id_type=MESH).wait_recv()
        for cp in sends:
            cp.wait_send()
        for cp in local:
            cp.wait()

    anyspec = pl.BlockSpec(memory_space=pl.ANY)
    return pl.pallas_call(
        body, name=name, in_specs=[anyspec] * na, out_specs=tuple([anyspec] * na),
        out_shape=tuple(jax.ShapeDtypeStruct((N_DEV,) + s.shape, s.dtype) for s in shards),
        scratch_shapes=[pltpu.SemaphoreType.DMA((na, N_DEV - 1)), pltpu.SemaphoreType.DMA((na, N_DEV - 1)),
                        pltpu.SemaphoreType.DMA((na,))],
        compiler_params=pltpu.CompilerParams(has_side_effects=True))(*shards)


def all_to_all(pieces, name):
    na = len(pieces)

    def body(*refs):
        ins, outs = refs[:na], refs[na:2 * na]
        send_sems, recv_sems, local_sems = refs[2 * na:]
        _, me = _peer(0)
        local = [pltpu.make_async_copy(ins[a].at[me], outs[a].at[me], local_sems.at[a]) for a in range(na)]
        for cp in local:
            cp.start()
        sends = []
        for k in range(1, N_DEV):
            peer, pid = _peer(k)
            for a in range(na):
                cp = pltpu.make_async_remote_copy(
                    src_ref=ins[a].at[pid], dst_ref=outs[a].at[me], send_sem=send_sems.at[a, k - 1],
                    recv_sem=recv_sems.at[a, k - 1], device_id=peer, device_id_type=MESH)
                cp.start()
                sends.append(cp)
        for k in range(1, N_DEV):
            peer, pid = _peer(k)
            for a in range(na):
                pltpu.make_async_remote_copy(
                    src_ref=ins[a].at[me], dst_ref=outs[a].at[pid], send_sem=send_sems.at[a, k - 1],
                    recv_sem=recv_sems.at[a, k - 1], device_id=peer, device_id_type=MESH).wait_recv()
        for cp in sends:
            cp.wait_send()
        for cp in local:
            cp.wait()

    anyspec = pl.BlockSpec(memory_space=pl.ANY)
    return pl.pallas_call(
        body, name=name, in_specs=[anyspec] * na, out_specs=tuple([anyspec] * na),
        out_shape=tuple(jax.ShapeDtypeStruct(p.shape, p.dtype) for p in pieces),
        scratch_shapes=[pltpu.SemaphoreType.DMA((na, N_DEV - 1)), pltpu.SemaphoreType.DMA((na, N_DEV - 1)),
                        pltpu.SemaphoreType.DMA((na,))],
        compiler_params=pltpu.CompilerParams(has_side_effects=True))(*pieces)


_HBM = pl.BlockSpec(memory_space=pltpu.HBM)
_SEM = pl.BlockSpec(memory_space=pltpu.SEMAPHORE)
_DATAFLOW = pltpu.SideEffectType.DATAFLOW_SIDE_EFFECTING
N_PEER = N_DEV - 1


def send_start(srcs, name, scatter, after):
    na = len(srcs)
    ns = 2 * N_PEER * na
    lands = [lax.empty((N_DEV,) + (s.shape[1:] if scatter else s.shape), s.dtype) for s in srcs]
    extra = [] if after is None else [after]

    def body(*refs):
        src_refs, land_refs = refs[:na], refs[na:2 * na]
        sems, token = refs[2 * na + len(extra):2 * na + len(extra) + ns], refs[-1]
        _, me = _peer(0)
        for k in range(1, N_DEV):
            peer, pid = _peer(k)
            for a in range(na):
                pltpu.make_async_remote_copy(
                    src_ref=src_refs[a].at[pid] if scatter else src_refs[a], dst_ref=land_refs[a].at[me],
                    send_sem=sems[2 * (a * N_PEER + k - 1)], recv_sem=sems[2 * (a * N_PEER + k - 1) + 1],
                    device_id=peer, device_id_type=MESH).start()
        token[...] = jnp.zeros_like(token)

    hbm = lambda arrs: tuple(pltpu.HBM(a.shape, a.dtype) for a in arrs)
    outs = pl.pallas_call(
        body, name=name,
        out_shape=(pltpu.SemaphoreType.DMA(()),) * ns + hbm(srcs) + hbm(lands) + (jax.ShapeDtypeStruct((8, 128), F32),),
        in_specs=[_HBM] * (2 * na) + [pl.BlockSpec(memory_space=pl.ANY)] * len(extra),
        out_specs=(_SEM,) * ns + (_HBM,) * (2 * na) + (pl.BlockSpec(memory_space=pltpu.VMEM),),
        input_output_aliases={i: ns + i for i in range(2 * na)},
        compiler_params=pltpu.CompilerParams(has_side_effects=_DATAFLOW),
    )(*[pltpu.with_memory_space_constraint(a, pltpu.HBM) for a in list(srcs) + lands], *extra)
    return outs[:ns], outs[ns:ns + na], outs[ns + na:ns + 2 * na], outs[-1]


def send_wait(sems, srcs_thru, lands_thru, name, scatter, after):
    na = len(srcs_thru)
    ns = 2 * N_PEER * na

    def body(*refs):
        src_refs, land_refs, sm = refs[:na], refs[na:2 * na], refs[2 * na:2 * na + ns]
        for k in range(1, N_DEV):
            peer, pid = _peer(k)
            for a in range(na):
                cp = pltpu.make_async_remote_copy(
                    src_ref=src_refs[a].at[pid] if scatter else src_refs[a], dst_ref=land_refs[a].at[pid],
                    send_sem=sm[2 * (a * N_PEER + k - 1)], recv_sem=sm[2 * (a * N_PEER + k - 1) + 1],
                    device_id=peer, device_id_type=MESH)
                cp.wait_send()
                cp.wait_recv()

    hbm = lambda arrs: tuple(pltpu.HBM(a.shape, a.dtype) for a in arrs)
    outs = pl.pallas_call(
        body, name=name, out_shape=hbm(srcs_thru) + hbm(lands_thru),
        in_specs=[_HBM] * (2 * na) + [_SEM] * ns + [pl.BlockSpec(memory_space=pl.ANY)], out_specs=(_HBM,) * (2 * na),
        input_output_aliases={i: i for i in range(2 * na)},
        compiler_params=pltpu.CompilerParams(has_side_effects=_DATAFLOW),
    )(*srcs_thru, *lands_thru, *sems, after)
    return outs[na:]


def _adamw(w, g, m, v):
    m = ADAM_B1 * m + (1.0 - ADAM_B1) * g
    v = ADAM_B2 * v + (1.0 - ADAM_B2) * (g * g)
    m_hat = m / (1.0 - ADAM_B1 ** ADAM_STEP)
    v_hat = v / (1.0 - ADAM_B2 ** ADAM_STEP)
    return -ADAM_LR * (m_hat / (jnp.sqrt(v_hat) + ADAM_EPS) + ADAM_WD * w), m, v


def adam_sum(w, pieces, m, v, name):
    r, c = w.shape
    tr = r
    for cand in (256, 128, 64, 32, 16, 8):
        if r % cand == 0:
            tr = cand
            break

    def body(w_ref, p_ref, m_ref, v_ref, g_ref, d_ref, nm_ref, nv_ref):
        g = p_ref[0].astype(F32)
        for s in range(1, N_DEV):
            g = g + p_ref[s].astype(F32)
        g_ref[...] = g
        d_ref[...], nm_ref[...], nv_ref[...] = _adamw(w_ref[...], g, m_ref[...], v_ref[...])

    row = pl.BlockSpec((tr, c), lambda i: (i, 0))
    out = jax.ShapeDtypeStruct((r, c), F32)
    return pl.pallas_call(
        body, name=name, grid=(r // tr,),
        in_specs=[row, pl.BlockSpec((N_DEV, tr, c), lambda i: (0, i, 0)), row, row],
        out_specs=(row,) * 4, out_shape=(out,) * 4, compiler_params=_cp("parallel"))(w, pieces, m, v)


def sum_rows(gathered, name):
    _, r, c = gathered.shape

    def body(p_ref, o_ref):
        g = p_ref[0]
        for s in range(1, N_DEV):
            g = g + p_ref[s]
        o_ref[...] = g

    return pl.pallas_call(body, name=name, out_shape=jax.ShapeDtypeStruct((r, c), F32))(gathered)


def adam_small(w, g, m, v, name):
    def body(w_ref, g_ref, m_ref, v_ref, d_ref, nm_ref, nv_ref):
        d_ref[...], nm_ref[...], nv_ref[...] = _adamw(w_ref[...], g_ref[...], m_ref[...], v_ref[...])

    out = jax.ShapeDtypeStruct(w.shape, F32)
    return pl.pallas_call(body, name=name, out_shape=(out,) * 3)(w, g, m, v)


def _rope_tables(t):
    inv_freq = 10000.0 ** (-jnp.arange(0, HEAD_DIM, 2, dtype=F32) / HEAD_DIM)
    ang = jnp.arange(t, dtype=F32)[:, None] * inv_freq[None, :]
    cos, sin = jnp.cos(ang), jnp.sin(ang)
    return jnp.concatenate([cos, cos], axis=-1), jnp.concatenate([sin, sin], axis=-1)


def _lane_row(vec8):
    return jnp.pad(vec8.reshape(1, DN_HEADS), ((0, 0), (DN_HEADS, 128 - 2 * DN_HEADS)))


def _ffn_fwd(x, norm_g, w_gu, w_d, tag):
    f = rms_fwd(x, norm_g, f"{tag}_norm")
    gu = mm_nn(f, w_gu, f"{tag}_gate_up")
    a = swiglu_fwd(gu, f"{tag}_act")
    return mm_nn(a, w_d, f"{tag}_down", res=x), (f, gu, a)


def _ffn_bwd(x, norm_g, w_gu, w_d, saved, dy, tag, after=None):
    f, gu, a = saved
    da = mm_nt(dy, w_d, f"{tag}_d_act", after=after)
    dwd = mm_tn(a, dy, f"{tag}_dw_down")
    dgu = swiglu_bwd(gu, da, f"{tag}_d_gate_up")
    df = mm_nt(dgu, w_gu, f"{tag}_d_normed")
    dwgu = mm_tn(f, dgu, f"{tag}_dw_gate_up")
    dx, dg = rms_bwd(x, norm_g, df, dy, f"{tag}_d_norm")
    return dx, dwgu, dwd, dg


def local_step(x, target, small, weights_of, grads_out, after=None):
    t = x.shape[0]
    cosf, sinf = _rope_tables(t)
    alog_row, dtb_row = _lane_row(small["odd_a_log"]), _lane_row(small["odd_dt_bias"])

    we = weights_of("even", None)
    h0 = rms_fwd(x, small["even_norm"], "even_norm", after=after)
    proj0 = mm_nn(h0, we["w_in"], "even_in_proj")
    qr, kr = qk_prep_fwd(proj0, small["even_q_gain"], small["even_k_gain"], cosf, sinf, "even_qk_prep")
    y_attn, lse = swa_fwd(qr, kr, proj0, small["even_sinks"], "even_swa")
    y_conv = gconv_fwd(proj0, small["even_conv_w"], "even_gconv")
    mix0 = jnp.concatenate([y_attn.astype(MXU_DTYPE), y_conv], axis=-1)
    x1 = mm_nn(mix0, we["w_out"], "even_out_proj", res=x)
    w0 = weights_of("ffn0", x1)
    x2, ffn0 = _ffn_fwd(x1, small["ffn_norm0"], w0["gate_up"], w0["down"], "ffn0")

    wo = weights_of("odd", x2)
    h1 = rms_fwd(x2, small["odd_norm"], "odd_norm")
    proj1 = mm_nn(h1, wo["w_in"], "odd_in_proj")
    qn, kn, vs, bg = gdn_prep_fwd(proj1, small["odd_conv_w"], alog_row, dtb_row, "odd_prep")
    o, sall, tall = gdn_fwd(qn, kn, vs, bg, "odd_delta_rule")
    og = gdn_out_fwd(o, proj1, small["odd_o_gain"], "odd_gate_norm")
    x3 = mm_nn(og, wo["w_out"], "odd_out_proj", res=x2)
    w1 = weights_of("ffn1", x3)
    x4, ffn1 = _ffn_fwd(x3, small["ffn_norm1"], w1["gate_up"], w1["down"], "ffn1")

    dy, loss_row = loss_head(x4, target, "loss_head")

    gs = {}
    dx3, dwgu, dwd, gs["ffn_norm1"] = _ffn_bwd(x3, small["ffn_norm1"], w1["gate_up"], w1["down"], ffn1, dy, "ffn1")
    tok = grads_out("ffn1", {"gate_up": dwgu, "down": dwd})

    dog = mm_nt(dx3, wo["w_out"], "odd_d_gated", after=tok)
    dwo = mm_tn(og, dx3, "odd_dw_out")
    do, dz, gs["odd_o_gain"] = gdn_out_bwd(o, proj1, small["odd_o_gain"], dog, "odd_d_gate_norm")
    dqn, dkn, dvs, dbg = gdn_bwd(qn, kn, vs, bg, sall, tall, do, "odd_d_delta_rule")
    dqkv, dba, gs["odd_conv_w"], ddt_row, dal_row = gdn_prep_bwd(
        proj1, small["odd_conv_w"], alog_row, dtb_row, dqn, dkn, dvs, dbg, "odd_d_prep")
    gs["odd_dt_bias"] = ddt_row[:, DN_HEADS:2 * DN_HEADS]
    gs["odd_a_log"] = dal_row[:, DN_HEADS:2 * DN_HEADS]
    dproj1 = jnp.concatenate([dqkv, dz, dba], axis=-1)
    dh1 = mm_nt(dproj1, wo["w_in"], "odd_d_normed")
    dwi = mm_tn(h1, dproj1, "odd_dw_in")
    dx2, gs["odd_norm"] = rms_bwd(x2, small["odd_norm"], dh1, dx3, "odd_d_norm")
    tok = grads_out("odd", {"w_in": dwi, "w_out": dwo})

    dx1, dwgu, dwd, gs["ffn_norm0"] = _ffn_bwd(x1, small["ffn_norm0"], w0["gate_up"], w0["down"], ffn0, dx2, "ffn0",
                                               after=tok)
    tok = grads_out("ffn0", {"gate_up": dwgu, "down": dwd})

    dmix = mm_nt(dx1, we["w_out"], "even_d_mix", after=tok)
    dwo = mm_tn(mix0, dx1, "even_dw_out")
    dqr, dkr, dv, gs["even_sinks"] = swa_bwd(qr, kr, proj0, small["even_sinks"], y_attn, lse, dmix, "even_d_swa")
    dqk, gs["even_q_gain"], gs["even_k_gain"] = qk_prep_bwd(
        proj0, small["even_q_gain"], small["even_k_gain"], cosf, sinf, dqr, dkr, "even_d_qk_prep")
    dgb, dgc, dxi, gs["even_conv_w"] = gconv_bwd(proj0, small["even_conv_w"], dmix, "even_d_gconv")
    dproj0 = jnp.concatenate([dqk, dv, dgb, dgc, dxi], axis=-1)
    dh0 = mm_nt(dproj0, we["w_in"], "even_d_normed")
    dwi = mm_tn(h0, dproj0, "even_dw_in")
    grad_x, gs["even_norm"] = rms_bwd(x, small["even_norm"], dh0, dx1, "even_d_norm")
    grads_out("even", {"w_in": dwi, "w_out": dwo})
    return loss_row, grad_x, gs


_SMALL_ORDER = ("even_norm", "even_q_gain", "even_k_gain", "even_sinks", "odd_a_log", "odd_dt_bias", "odd_o_gain",
                "ffn_norm0", "ffn_norm1", "odd_norm", "even_conv_w", "odd_conv_w")
_SMALL_SIZE = {"even_norm": 1024, "even_q_gain": 64, "even_k_gain": 64, "even_sinks": 8, "odd_a_log": 8,
               "odd_dt_bias": 8, "odd_o_gain": 128, "ffn_norm0": 1024, "ffn_norm1": 1024, "odd_norm": 1024,
               "even_conv_w": 3 * 512, "odd_conv_w": 4 * 3072}
_N_REPL = 9


def _pack_rows(vals):
    flat = jnp.concatenate([v.reshape(-1) for v in vals])
    pad = (-flat.shape[0]) % 1024
    return jnp.pad(flat, (0, pad)).reshape(-1, 128)


def _my_block(full, size, axis):
    me = 4 * lax.axis_index("x") + 2 * lax.axis_index("y") + lax.axis_index("c")
    return lax.dynamic_slice_in_dim(full, me * size, size, axis=axis)


def _col_gathered(g):
    return g.transpose(1, 0, 2).reshape(g.shape[1], N_DEV * g.shape[2])


def _col_pieces(dw):
    k, n8 = dw.shape
    return dw.reshape(k, N_DEV, n8 // N_DEV).transpose(1, 0, 2)


def kernel(x, even_norm, even_w_in, even_q_gain, even_k_gain, even_sinks, even_conv_w, even_w_out, odd_norm, odd_w_in, odd_conv_w, odd_a_log, odd_dt_bias, odd_o_gain, odd_w_out, ffn_norm, ffn_w_gate_up, ffn_w_down, loss_target, m_even_norm, m_even_w_in, m_even_q_gain, m_even_k_gain, m_even_sinks, m_even_conv_w, m_even_w_out, m_odd_norm, m_odd_w_in, m_odd_conv_w, m_odd_a_log, m_odd_dt_bias, m_odd_o_gain, m_odd_w_out, m_ffn_norm, m_ffn_w_gate_up, m_ffn_w_down, v_even_norm, v_even_w_in, v_even_q_gain, v_even_k_gain, v_even_sinks, v_even_conv_w, v_even_w_out, v_odd_norm, v_odd_w_in, v_odd_conv_w, v_odd_a_log, v_odd_dt_bias, v_odd_o_gain, v_odd_w_out, v_ffn_norm, v_ffn_w_gate_up, v_ffn_w_down):
    t = x.shape[1]
    d = D_MODEL

    me = 4 * lax.axis_index("x") + 2 * lax.axis_index("y") + lax.axis_index("c")
    fpd = D_FF // N_DEV
    shard = {
        "even": {"w_in": even_w_in.reshape(d, EVEN_IN_W // N_DEV), "w_out": even_w_out.reshape(d // N_DEV, d)},
        "ffn0": {"gate_up": ffn_w_gate_up[0], "down": ffn_w_down[0]},
        "odd": {"w_in": odd_w_in.reshape(d, ODD_IN_W // N_DEV), "w_out": odd_w_out.reshape(d // N_DEV, d)},
        "ffn1": {"gate_up": ffn_w_gate_up[1], "down": ffn_w_down[1]},
    }
    mom_m = {
        "even": {"w_in": m_even_w_in, "w_out": m_even_w_out}, "odd": {"w_in": m_odd_w_in, "w_out": m_odd_w_out},
        "ffn0": {"gate_up": m_ffn_w_gate_up[0], "down": m_ffn_w_down[0]},
        "ffn1": {"gate_up": m_ffn_w_gate_up[1], "down": m_ffn_w_down[1]},
    }
    mom_v = {
        "even": {"w_in": v_even_w_in, "w_out": v_even_w_out}, "odd": {"w_in": v_odd_w_in, "w_out": v_odd_w_out},
        "ffn0": {"gate_up": v_ffn_w_gate_up[0], "down": v_ffn_w_down[0]},
        "ffn1": {"gate_up": v_ffn_w_gate_up[1], "down": v_ffn_w_down[1]},
    }
    later = ("ffn0", "odd", "ffn1")

    def whole(group, parts):
        col, row = tuple(shard[group])
        w_col = _col_gathered(parts[0])
        if group == "odd":
            w_col = jnp.pad(w_col, ((0, 0), (0, ODD_IN_PAD - ODD_IN_W)))
        return {col: w_col, row: parts[1].reshape(-1, d)}

    def own_slot(lands, blocks):
        return [lax.dynamic_update_index_in_dim(l, b, me, 0) for l, b in zip(lands, blocks)]

    shard_rows = _pack_rows([odd_norm, even_conv_w, odd_conv_w])
    wire = {g: [a.astype(MXU_DTYPE) for a in shard[g].values()] for g in shard}
    g_in, g_out, small_g = all_gather(wire["even"] + [shard_rows], "gather_even_weights")
    even_whole = whole("even", [g_in, g_out])
    gathers, tok = {}, g_in
    for g in later:
        sems, srcs_thru, lands_thru, tok = send_start(wire[g], f"gather_{g}_start", False, tok)
        gathers[g] = (sems, srcs_thru, lands_thru)

    def weights_of(group, after):
        if group == "even":
            return even_whole
        lands = send_wait(*gathers[group], f"gather_{group}_wait", False, after)
        return whole(group, own_slot(lands, wire[group]))

    sent, even_pieces = {}, []

    def grads_out(group, dws):
        col, row = tuple(shard[group])
        n_cols = N_DEV * shard[group][col].shape[1]
        pieces = [_col_pieces(dws[col][:, :n_cols]).astype(MXU_DTYPE),
                  dws[row].reshape((N_DEV,) + shard[group][row].shape).astype(MXU_DTYPE)]
        if group == "even":
            even_pieces.extend(pieces)
            return None
        sems, srcs_thru, lands_thru, token = send_start(pieces, f"exchange_{group}_start", True, None)
        sent[group] = (sems, srcs_thru, lands_thru, pieces)
        return token

    sg = small_g.reshape(N_DEV, -1)
    o1 = d // N_DEV
    o2 = o1 + 3 * CONV_CH // N_DEV
    small = {
        "even_norm": even_norm, "even_q_gain": even_q_gain, "even_k_gain": even_k_gain, "even_sinks": even_sinks,
        "odd_a_log": odd_a_log.reshape(-1), "odd_dt_bias": odd_dt_bias.reshape(-1), "odd_o_gain": odd_o_gain,
        "ffn_norm0": ffn_norm[0:1], "ffn_norm1": ffn_norm[1:2],
        "odd_norm": sg[:, :o1].reshape(1, d),
        "even_conv_w": sg[:, o1:o2].reshape(N_DEV, 3, CONV_CH // N_DEV).transpose(1, 0, 2).reshape(3, CONV_CH),
        "odd_conv_w": sg[:, o2:o2 + 4 * _QKV_W // N_DEV].reshape(N_DEV, 4, _QKV_W // N_DEV).transpose(1, 0, 2).reshape(4, _QKV_W),
    }

    loss_row, grad_x, gs = local_step(x.reshape(t, d), loss_target.reshape(t, d), small, weights_of, grads_out, after=tok)

    rows = _pack_rows([gs[n] for n in _SMALL_ORDER] + [loss_row[:, 0:1]])
    (rows_g,) = all_gather([rows], "gather_small_grads")
    tot = sum_rows(rows_g, "sum_small_grads").reshape(-1)
    off, sgrad = 0, {}
    for n in _SMALL_ORDER:
        sgrad[n] = tot[off:off + _SMALL_SIZE[n]]
        off += _SMALL_SIZE[n]
    loss = tot[off]

    landed = {"even": dict(zip(shard["even"], all_to_all(even_pieces, "exchange_even_grads")))}
    for g in reversed(later):
        sems, srcs_thru, lands_thru, pieces = sent[g]
        lands = send_wait(sems, srcs_thru, lands_thru, f"exchange_{g}_wait", True, grad_x)
        own = [lax.dynamic_index_in_dim(p, me, 0, keepdims=False) for p in pieces]
        landed[g] = dict(zip(shard[g], own_slot(lands, own)))
    upd = {}
    for g in shard:
        for key, w2 in shard[g].items():
            upd[g, key] = adam_sum(w2, landed[g][key], mom_m[g][key].reshape(w2.shape), mom_v[g][key].reshape(w2.shape),
                                   f"adamw_{g}_{key}")
    res = {
        "even_w_in": tuple(o.reshape(even_w_in.shape) for o in upd["even", "w_in"]),
        "even_w_out": tuple(o.reshape(even_w_out.shape) for o in upd["even", "w_out"]),
        "odd_w_in": tuple(o.reshape(odd_w_in.shape) for o in upd["odd", "w_in"]),
        "odd_w_out": tuple(o.reshape(odd_w_out.shape) for o in upd["odd", "w_out"]),
        "ffn_w_gate_up": tuple(jnp.stack([a, b]) for a, b in zip(upd["ffn0", "gate_up"], upd["ffn1", "gate_up"])),
        "ffn_w_down": tuple(jnp.stack([a, b]) for a, b in zip(upd["ffn0", "down"], upd["ffn1", "down"])),
    }

    repl = _SMALL_ORDER[:_N_REPL]
    repl_w = {"even_norm": even_norm, "even_q_gain": even_q_gain, "even_k_gain": even_k_gain, "even_sinks": even_sinks,
              "odd_a_log": odd_a_log, "odd_dt_bias": odd_dt_bias, "odd_o_gain": odd_o_gain,
              "ffn_norm0": ffn_norm[0], "ffn_norm1": ffn_norm[1]}
    repl_m = {"even_norm": m_even_norm, "even_q_gain": m_even_q_gain, "even_k_gain": m_even_k_gain,
              "even_sinks": m_even_sinks, "odd_a_log": m_odd_a_log, "odd_dt_bias": m_odd_dt_bias,
              "odd_o_gain": m_odd_o_gain, "ffn_norm0": m_ffn_norm[0], "ffn_norm1": m_ffn_norm[1]}
    repl_v = {"even_norm": v_even_norm, "even_q_gain": v_even_q_gain, "even_k_gain": v_even_k_gain,
              "even_sinks": v_even_sinks, "odd_a_log": v_odd_a_log, "odd_dt_bias": v_odd_dt_bias,
              "odd_o_gain": v_odd_o_gain, "ffn_norm0": v_ffn_norm[0], "ffn_norm1": v_ffn_norm[1]}
    pk = lambda dct: _pack_rows([dct[n] for n in repl])
    pd_, pm_, pv_ = adam_small(pk(repl_w), pk(sgrad), pk(repl_m), pk(repl_v), "adamw_replicated")
    sres = {}
    off = 0
    for n in repl:
        sz = _SMALL_SIZE[n]
        sres[n] = (sgrad[n], pd_.reshape(-1)[off:off + sz], pm_.reshape(-1)[off:off + sz], pv_.reshape(-1)[off:off + sz])
        off += sz
    g_on = _my_block(sgrad["odd_norm"].reshape(1, d), d // N_DEV, 1)
    g_ec = _my_block(sgrad["even_conv_w"].reshape(3, CONV_CH), CONV_CH // N_DEV, 1)
    g_oc = _my_block(sgrad["odd_conv_w"].reshape(4, _QKV_W), _QKV_W // N_DEV, 1)
    shard_w = _pack_rows([odd_norm, even_conv_w, odd_conv_w])
    sd_, sm_, sv_ = adam_small(shard_w, _pack_rows([g_on, g_ec, g_oc]),
                               _pack_rows([m_odd_norm, m_even_conv_w, m_odd_conv_w]),
                               _pack_rows([v_odd_norm, v_even_conv_w, v_odd_conv_w]), "adamw_sharded_small")
    off = 0
    for n, gfull, like in (("odd_norm", g_on, odd_norm), ("even_conv_w", g_ec, even_conv_w), ("odd_conv_w", g_oc, odd_conv_w)):
        sz = like.size
        sres[n] = (gfull, sd_.reshape(-1)[off:off + sz], sm_.reshape(-1)[off:off + sz], sv_.reshape(-1)[off:off + sz])
        off += sz

    def small_out(name, like, kind):
        if name == "ffn_norm":
            return jnp.stack([sres["ffn_norm0"][kind], sres["ffn_norm1"][kind]]).reshape(like.shape)
        return sres[name][kind].reshape(like.shape)

    order = (("even_norm", even_norm), ("even_w_in", even_w_in), ("even_q_gain", even_q_gain),
             ("even_k_gain", even_k_gain), ("even_sinks", even_sinks), ("even_conv_w", even_conv_w),
             ("even_w_out", even_w_out), ("odd_norm", odd_norm), ("odd_w_in", odd_w_in), ("odd_conv_w", odd_conv_w),
             ("odd_a_log", odd_a_log), ("odd_dt_bias", odd_dt_bias), ("odd_o_gain", odd_o_gain),
             ("odd_w_out", odd_w_out), ("ffn_norm", ffn_norm), ("ffn_w_gate_up", ffn_w_gate_up),
             ("ffn_w_down", ffn_w_down))
    outs = [loss, grad_x.reshape(x.shape)]
    for kind in range(4):
        for name, like in order:
            outs.append(res[name][kind] if name in res else small_out(name, like, kind))
    return tuple(outs)
```

```python
import functools

import jax
import jax.numpy as jnp
from jax import lax
from jax.experimental import pallas as pl
from jax.experimental.pallas import tpu as pltpu

F32 = jnp.float32
MXU_DTYPE = jnp.bfloat16
HI = lax.Precision.HIGH
EPS = 1e-6
N_DEV = 8
D_MODEL = 1024
HEAD_DIM = 64
ATTN_HEADS = 8
KV_HEADS = 2
ATTN_BLOCK = 128
Q_W = 512
KV_W = 128
CONV_CH = 512
EVEN_IN_W = 2304
DN_HEADS = 8
DN_DIM = 128
DN_W = 1024
DN_CHUNK = 64
ODD_IN_W = 4112
ODD_IN_PAD = 4224
D_FF = 2816
NEG = -1e30
VMEM_LIMIT = 56 * 1024 * 1024
ADAM_LR, ADAM_B1, ADAM_B2, ADAM_EPS, ADAM_WD, ADAM_STEP = 0.001, 0.9, 0.999, 1e-08, 0.01, 10
MESH = pl.DeviceIdType.MESH


def _cp(*sem):
    return pltpu.CompilerParams(dimension_semantics=sem, vmem_limit_bytes=VMEM_LIMIT)


def _pick(n, cap):
    best = 128
    for t in range(128, cap + 1, 128):
        if n % t == 0:
            best = t
    return best


def _mx(a, b):
    return jnp.dot(a.astype(MXU_DTYPE), b.astype(MXU_DTYPE), preferred_element_type=F32)


def _mx_nt(a, b):
    return lax.dot_general(a.astype(MXU_DTYPE), b.astype(MXU_DTYPE), (((1,), (1,)), ((), ())),
                           preferred_element_type=F32)


def _mx_tn(a, b):
    return lax.dot_general(a.astype(MXU_DTYPE), b.astype(MXU_DTYPE), (((0,), (0,)), ((), ())),
                           preferred_element_type=F32)


def _hi(a, b):
    return jnp.dot(a, b, precision=HI, preferred_element_type=F32)


def _hi_nt(a, b):
    return lax.dot_general(a, b, (((1,), (1,)), ((), ())), precision=HI, preferred_element_type=F32)


def _hi_tn(a, b):
    return lax.dot_general(a, b, (((0,), (0,)), ((), ())), precision=HI, preferred_element_type=F32)


def _sigmoid(x):
    return 1.0 / (1.0 + jnp.exp(-x))


def _softplus(x):
    return jnp.maximum(x, 0.0) + jnp.log(1.0 + jnp.exp(-jnp.abs(x)))


def mm_nn(a, b, name, res=None, out_dtype=F32, tm=1024):
    m, k = a.shape
    _, n = b.shape
    tn = _pick(n, 1536)
    tm = min(tm, m)

    def body(*refs):
        a_ref, b_ref = refs[0], refs[1]
        o_ref = refs[-1]
        acc = _mx(a_ref[...], b_ref[...])
        if res is not None:
            acc = acc + refs[2][...]
        o_ref[...] = acc.astype(o_ref.dtype)

    in_specs = [pl.BlockSpec((tm, k), lambda j, i: (i, 0)), pl.BlockSpec((k, tn), lambda j, i: (0, j))]
    args = [a, b]
    if res is not None:
        in_specs.append(pl.BlockSpec((tm, tn), lambda j, i: (i, j)))
        args.append(res)
    return pl.pallas_call(
        body, name=name, grid=(n // tn, m // tm), in_specs=in_specs,
        out_specs=pl.BlockSpec((tm, tn), lambda j, i: (i, j)),
        out_shape=jax.ShapeDtypeStruct((m, n), out_dtype), compiler_params=_cp("parallel", "parallel"))(*args)


def mm_nt(a, b, name, out_dtype=F32, tm=1024, after=None):
    m, k = a.shape
    n, _ = b.shape
    tn = _pick(n, 512 if k > 3000 else 1536)
    tm = min(tm, m)

    def body(a_ref, b_ref, *rest):
        o_ref = rest[-1]
        o_ref[...] = _mx_nt(a_ref[...], b_ref[...]).astype(o_ref.dtype)

    in_specs = [pl.BlockSpec((tm, k), lambda j, i: (i, 0)), pl.BlockSpec((tn, k), lambda j, i: (j, 0))]
    args = [a, b]
    if after is not None:
        in_specs.append(pl.BlockSpec(memory_space=pl.ANY))
        args.append(after)
    return pl.pallas_call(
        body, name=name, grid=(n // tn, m // tm), in_specs=in_specs,
        out_specs=pl.BlockSpec((tm, tn), lambda j, i: (i, j)),
        out_shape=jax.ShapeDtypeStruct((m, n), out_dtype), compiler_params=_cp("parallel", "parallel"))(*args)


def mm_tn(a, b, name, tk=512):
    kk, m = a.shape
    _, n = b.shape
    tm = _pick(m, 1408)
    tn = _pick(n, 1408)

    def body(a_ref, b_ref, o_ref):
        k = pl.program_id(2)
        p = _mx_tn(a_ref[...], b_ref[...])

        @pl.when(k == 0)
        def _():
            o_ref[...] = p

        @pl.when(k > 0)
        def _():
            o_ref[...] += p

    return pl.pallas_call(
        body, name=name, grid=(m // tm, n // tn, kk // tk),
        in_specs=[pl.BlockSpec((tk, tm), lambda i, j, k: (k, i)), pl.BlockSpec((tk, tn), lambda i, j, k: (k, j))],
        out_specs=pl.BlockSpec((tm, tn), lambda i, j, k: (i, j)),
        out_shape=jax.ShapeDtypeStruct((m, n), F32), compiler_params=_cp("parallel", "parallel", "arbitrary"))(a, b)


def rms_fwd(x, g, name, tm=512, after=None):
    t, d = x.shape

    def body(x_ref, g_ref, *rest):
        o_ref = rest[-1]
        xv = x_ref[...]
        r = lax.rsqrt(jnp.mean(xv * xv, axis=-1, keepdims=True) + EPS)
        o_ref[...] = (xv * r * g_ref[...]).astype(o_ref.dtype)

    in_specs = [pl.BlockSpec((tm, d), lambda i: (i, 0)), pl.BlockSpec((1, d), lambda i: (0, 0))]
    args = [x, g]
    if after is not None:
        in_specs.append(pl.BlockSpec(memory_space=pl.ANY))
        args.append(after)
    return pl.pallas_call(
        body, name=name, grid=(t // tm,), in_specs=in_specs,
        out_specs=pl.BlockSpec((tm, d), lambda i: (i, 0)),
        out_shape=jax.ShapeDtypeStruct((t, d), MXU_DTYPE), compiler_params=_cp("parallel"))(*args)


def rms_bwd(x, g, dh, dres, name, tm=512):
    t, d = x.shape

    def body(x_ref, g_ref, dh_ref, dres_ref, dx_ref, dg_ref):
        i = pl.program_id(0)
        xv = x_ref[...]
        r = lax.rsqrt(jnp.mean(xv * xv, axis=-1, keepdims=True) + EPS)
        xh = xv * r
        dhv = dh_ref[...]
        dxh = dhv * g_ref[...]
        dx_ref[...] = dres_ref[...] + r * (dxh - xh * jnp.mean(dxh * xh, axis=-1, keepdims=True))
        part = jnp.sum(dhv * xh, axis=0, keepdims=True)

        @pl.when(i == 0)
        def _():
            dg_ref[...] = part

        @pl.when(i > 0)
        def _():
            dg_ref[...] += part

    row = pl.BlockSpec((tm, d), lambda i: (i, 0))
    one = pl.BlockSpec((1, d), lambda i: (0, 0))
    return pl.pallas_call(
        body, name=name, grid=(t // tm,), in_specs=[row, one, row, row], out_specs=(row, one),
        out_shape=(jax.ShapeDtypeStruct((t, d), F32), jax.ShapeDtypeStruct((1, d), F32)),
        compiler_params=_cp("arbitrary"))(x, g, dh, dres)


def swiglu_fwd(gu, name, tm=256):
    t, n2 = gu.shape
    f = n2 // 2

    def body(g_ref, u_ref, o_ref):
        gv = g_ref[...]
        o_ref[...] = (gv * _sigmoid(gv) * u_ref[...]).astype(o_ref.dtype)

    return pl.pallas_call(
        body, name=name, grid=(t // tm,),
        in_specs=[pl.BlockSpec((tm, f), lambda i: (i, 0)), pl.BlockSpec((tm, f), lambda i: (i, 1))],
        out_specs=pl.BlockSpec((tm, f), lambda i: (i, 0)),
        out_shape=jax.ShapeDtypeStruct((t, f), MXU_DTYPE), compiler_params=_cp("parallel"))(gu, gu)


def swiglu_bwd(gu, da, name, tm=256):
    t, n2 = gu.shape
    f = n2 // 2

    def body(g_ref, u_ref, da_ref, dg_ref, du_ref):
        gv = g_ref[...]
        sg = _sigmoid(gv)
        dav = da_ref[...]
        dg_ref[...] = (dav * u_ref[...] * sg * (1.0 + gv * (1.0 - sg))).astype(dg_ref.dtype)
        du_ref[...] = (dav * gv * sg).astype(du_ref.dtype)

    lo = pl.BlockSpec((tm, f), lambda i: (i, 0))
    hi = pl.BlockSpec((tm, f), lambda i: (i, 1))
    dgate, dup = pl.pallas_call(
        body, name=name, grid=(t // tm,), in_specs=[lo, hi, lo], out_specs=(lo, lo),
        out_shape=(jax.ShapeDtypeStruct((t, f), MXU_DTYPE), jax.ShapeDtypeStruct((t, f), MXU_DTYPE)),
        compiler_params=_cp("parallel"))(gu, gu, da)
    return jnp.concatenate([dgate, dup], axis=-1)


def loss_head(y, target, name, tm=512):
    t, d = y.shape

    def body(y_ref, t_ref, dy_ref, l_ref):
        i = pl.program_id(0)
        e = y_ref[...] - t_ref[...]
        dy_ref[...] = e * (1.0 / d)
        part = jnp.zeros((1, 128), F32) + 0.5 * jnp.sum(jnp.mean(e * e, axis=-1, keepdims=True), axis=0, keepdims=True)

        @pl.when(i == 0)
        def _():
            l_ref[...] = part

        @pl.when(i > 0)
        def _():
            l_ref[...] += part

    row = pl.BlockSpec((tm, d), lambda i: (i, 0))
    return pl.pallas_call(
        body, name=name, grid=(t // tm,), in_specs=[row, row],
        out_specs=(row, pl.BlockSpec((1, 128), lambda i: (0, 0))),
        out_shape=(jax.ShapeDtypeStruct((t, d), F32), jax.ShapeDtypeStruct((1, 128), F32)),
        compiler_params=_cp("arbitrary"))(y, target)


def _rot(x):
    return jnp.concatenate([-x[:, HEAD_DIM // 2:], x[:, :HEAD_DIM // 2]], axis=-1)


def _rot_t(y):
    return jnp.concatenate([y[:, HEAD_DIM // 2:], -y[:, :HEAD_DIM // 2]], axis=-1)


def qk_prep_fwd(proj, q_gain, k_gain, cosf, sinf, name, tm=256):
    t = proj.shape[0]
    nh = ATTN_HEADS + KV_HEADS

    def body(p_ref, qg_ref, kg_ref, c_ref, s_ref, q_ref, k_ref):
        c, s = c_ref[...], s_ref[...]
        outs = []
        for h in range(nh):
            xh = p_ref[:, h * HEAD_DIM:(h + 1) * HEAD_DIM]
            gain = qg_ref[...] if h < ATTN_HEADS else kg_ref[...]
            r = lax.rsqrt(jnp.mean(xh * xh, axis=-1, keepdims=True) + EPS)
            xn = xh * r * gain
            outs.append(xn * c + _rot(xn) * s)
        q_ref[...] = jnp.concatenate(outs[:ATTN_HEADS], axis=-1)
        k_ref[...] = jnp.concatenate(outs[ATTN_HEADS:], axis=-1)

    gspec = pl.BlockSpec((1, HEAD_DIM), lambda i: (0, 0))
    tspec = pl.BlockSpec((tm, HEAD_DIM), lambda i: (i, 0))
    return pl.pallas_call(
        body, name=name, grid=(t // tm,),
        in_specs=[pl.BlockSpec((tm, Q_W + KV_W), lambda i: (i, 0)), gspec, gspec, tspec, tspec],
        out_specs=(pl.BlockSpec((tm, Q_W), lambda i: (i, 0)), pl.BlockSpec((tm, KV_W), lambda i: (i, 0))),
        out_shape=(jax.ShapeDtypeStruct((t, Q_W), F32), jax.ShapeDtypeStruct((t, KV_W), F32)),
        compiler_params=_cp("parallel"))(proj, q_gain, k_gain, cosf, sinf)


def qk_prep_bwd(proj, q_gain, k_gain, cosf, sinf, dq, dk, name, tm=256):
    t = proj.shape[0]
    nh = ATTN_HEADS + KV_HEADS

    def body(p_ref, qg_ref, kg_ref, c_ref, s_ref, dq_ref, dk_ref, o_ref, dqg_ref, dkg_ref):
        i = pl.program_id(0)
        c, s = c_ref[...], s_ref[...]
        outs = []
        dqg = jnp.zeros((1, HEAD_DIM), F32)
        dkg = jnp.zeros((1, HEAD_DIM), F32)
        for h in range(nh):
            xh = p_ref[:, h * HEAD_DIM:(h + 1) * HEAD_DIM]
            if h < ATTN_HEADS:
                gain = qg_ref[...]
                dout = dq_ref[:, h * HEAD_DIM:(h + 1) * HEAD_DIM]
            else:
                gain = kg_ref[...]
                dout = dk_ref[:, (h - ATTN_HEADS) * HEAD_DIM:(h - ATTN_HEADS + 1) * HEAD_DIM]
            r = lax.rsqrt(jnp.mean(xh * xh, axis=-1, keepdims=True) + EPS)
            xhat = xh * r
            dxn = dout * c + _rot_t(dout * s)
            part = jnp.sum(dxn * xhat, axis=0, keepdims=True)
            if h < ATTN_HEADS:
                dqg = dqg + part
            else:
                dkg = dkg + part
            dxh = dxn * gain
            outs.append(r * (dxh - xhat * jnp.mean(dxh * xhat, axis=-1, keepdims=True)))
        o_ref[...] = jnp.concatenate(outs, axis=-1).astype(o_ref.dtype)

        @pl.when(i == 0)
        def _():
            dqg_ref[...] = dqg
            dkg_ref[...] = dkg

        @pl.when(i > 0)
        def _():
            dqg_ref[...] += dqg
            dkg_ref[...] += dkg

    gspec = pl.BlockSpec((1, HEAD_DIM), lambda i: (0, 0))
    tspec = pl.BlockSpec((tm, HEAD_DIM), lambda i: (i, 0))
    return pl.pallas_call(
        body, name=name, grid=(t // tm,),
        in_specs=[pl.BlockSpec((tm, Q_W + KV_W), lambda i: (i, 0)), gspec, gspec, tspec, tspec,
                  pl.BlockSpec((tm, Q_W), lambda i: (i, 0)), pl.BlockSpec((tm, KV_W), lambda i: (i, 0))],
        out_specs=(pl.BlockSpec((tm, Q_W + KV_W), lambda i: (i, 0)), gspec, gspec),
        out_shape=(jax.ShapeDtypeStruct((t, Q_W + KV_W), MXU_DTYPE), jax.ShapeDtypeStruct((1, HEAD_DIM), F32),
                   jax.ShapeDtypeStruct((1, HEAD_DIM), F32)),
        compiler_params=_cp("arbitrary"))(proj, q_gain, k_gain, cosf, sinf, dq, dk)


def _swa_valid(n, grp):
    qi = lax.broadcasted_iota(jnp.int32, (grp * ATTN_BLOCK, 2 * ATTN_BLOCK), 0) & (ATTN_BLOCK - 1)
    kj = lax.broadcasted_iota(jnp.int32, (grp * ATTN_BLOCK, 2 * ATTN_BLOCK), 1)
    diff = qi + ATTN_BLOCK - kj
    return (diff >= 0) & (diff < ATTN_BLOCK) & (n * ATTN_BLOCK - ATTN_BLOCK + kj >= 0)


def _stack_heads(ref, g, grp):
    return jnp.concatenate([ref[:, (g * grp + j) * HEAD_DIM:(g * grp + j + 1) * HEAD_DIM] for j in range(grp)], axis=0)


def _stack_sinks(s_ref, g, grp):
    return jnp.concatenate([jnp.zeros((ATTN_BLOCK, 1), F32) + s_ref[0:1, g * grp + j:g * grp + j + 1]
                            for j in range(grp)], axis=0)


def swa_fwd(q, k, proj, sinks, name):
    t = q.shape[0]
    nb = t // ATTN_BLOCK
    scale = HEAD_DIM ** -0.5
    grp = ATTN_HEADS // KV_HEADS

    def body(q_ref, kc_ref, kp_ref, vc_ref, vp_ref, s_ref, y_ref, lse_ref):
        n = pl.program_id(0)
        valid = _swa_valid(n, grp)
        kk = jnp.concatenate([kp_ref[...], kc_ref[...]], axis=0).astype(MXU_DTYPE)
        vv = jnp.concatenate([vp_ref[...], vc_ref[...]], axis=0).astype(MXU_DTYPE)
        lane = lax.broadcasted_iota(jnp.int32, (ATTN_BLOCK, ATTN_HEADS), 1)
        gs = range(KV_HEADS)
        qg = [_stack_heads(q_ref, g, grp) for g in gs]
        sink = [_stack_sinks(s_ref, g, grp) for g in gs]
        sc = [jnp.where(valid, _mx_nt(qg[g], kk[:, g * HEAD_DIM:(g + 1) * HEAD_DIM]) * scale, NEG) for g in gs]
        m = [jnp.maximum(jnp.max(sc[g], axis=-1, keepdims=True), sink[g]) for g in gs]
        e = [jnp.exp(sc[g] - m[g]) for g in gs]
        den = [jnp.sum(e[g], axis=-1, keepdims=True) + jnp.exp(sink[g] - m[g]) for g in gs]
        og = [_mx(e[g] / den[g], vv[:, g * HEAD_DIM:(g + 1) * HEAD_DIM]) for g in gs]
        lg = [m[g] + jnp.log(den[g]) for g in gs]
        lse = jnp.zeros((ATTN_BLOCK, ATTN_HEADS), F32)
        outs = []
        for h in range(ATTN_HEADS):
            rows = slice((h % grp) * ATTN_BLOCK, (h % grp + 1) * ATTN_BLOCK)
            outs.append(og[h // grp][rows])
            lse = jnp.where(lane == h, lg[h // grp][rows], lse)
        y_ref[...] = jnp.concatenate(outs, axis=-1)
        lse_ref[...] = lse

    cur = lambda n: (n, 0)
    prev = lambda n: (jnp.maximum(n - 1, 0), 0)
    vcol = (Q_W + KV_W) // KV_W
    return pl.pallas_call(
        body, name=name, grid=(nb,),
        in_specs=[pl.BlockSpec((ATTN_BLOCK, Q_W), cur), pl.BlockSpec((ATTN_BLOCK, KV_W), cur),
                  pl.BlockSpec((ATTN_BLOCK, KV_W), prev),
                  pl.BlockSpec((ATTN_BLOCK, KV_W), lambda n: (n, vcol)),
                  pl.BlockSpec((ATTN_BLOCK, KV_W), lambda n: (jnp.maximum(n - 1, 0), vcol)),
                  pl.BlockSpec((1, ATTN_HEADS), lambda n: (0, 0))],
        out_specs=(pl.BlockSpec((ATTN_BLOCK, Q_W), cur), pl.BlockSpec((ATTN_BLOCK, ATTN_HEADS), cur)),
        out_shape=(jax.ShapeDtypeStruct((t, Q_W), F32), jax.ShapeDtypeStruct((t, ATTN_HEADS), F32)),
        compiler_params=_cp("parallel"))(q, k, k, proj, proj, sinks)


def swa_bwd(q, k, proj, sinks, y, lse, dmix, name):
    t = q.shape[0]
    nb = t // ATTN_BLOCK
    scale = HEAD_DIM ** -0.5
    grp = ATTN_HEADS // KV_HEADS

    def body(q_ref, kc_ref, kp_ref, vc_ref, vp_ref, s_ref, y_ref, lse_ref, dy_ref,
             dq_ref, dk_ref, dv_ref, ds_ref, dkc, dvc):
        n = pl.program_id(0)

        @pl.when(n == 0)
        def _():
            dkc[...] = jnp.zeros_like(dkc)
            dvc[...] = jnp.zeros_like(dvc)
            ds_ref[...] = jnp.zeros_like(ds_ref)

        @pl.when(n < nb)
        def _():
            valid = _swa_valid(n, grp)
            kk = jnp.concatenate([kp_ref[...], kc_ref[...]], axis=0).astype(MXU_DTYPE)
            vv = jnp.concatenate([vp_ref[...], vc_ref[...]], axis=0).astype(MXU_DTYPE)
            lane = lax.broadcasted_iota(jnp.int32, (1, ATTN_HEADS), 1)
            gs = range(KV_HEADS)
            kg = [kk[:, g * HEAD_DIM:(g + 1) * HEAD_DIM] for g in gs]
            vg = [vv[:, g * HEAD_DIM:(g + 1) * HEAD_DIM] for g in gs]
            qg = [_stack_heads(q_ref, g, grp).astype(MXU_DTYPE) for g in gs]
            dog = [_stack_heads(dy_ref, g, grp) for g in gs]
            og = [_stack_heads(y_ref, g, grp) for g in gs]
            lg = [jnp.concatenate([lse_ref[:, g * grp + j:g * grp + j + 1] for j in range(grp)], axis=0) for g in gs]
            sink = [_stack_sinks(s_ref, g, grp) for g in gs]
            sc = [jnp.where(valid, _mx_nt(qg[g], kg[g]) * scale, NEG) for g in gs]
            p = [jnp.exp(sc[g] - lg[g]) for g in gs]
            delta = [jnp.sum(dog[g] * og[g], axis=-1, keepdims=True) for g in gs]
            ds = [p[g] * (_mx_nt(dog[g], vg[g]) - delta[g]) for g in gs]
            dqg = [_mx(ds[g], kg[g]) * scale for g in gs]
            dkf = jnp.concatenate([_mx_tn(ds[g], qg[g]) * scale for g in gs], axis=-1)
            dvf = jnp.concatenate([_mx_tn(p[g], dog[g]) for g in gs], axis=-1)
            dsk = [jnp.exp(sink[g] - lg[g]) * delta[g] for g in gs]
            dsink = jnp.zeros((1, ATTN_HEADS), F32)
            dqs = []
            for h in range(ATTN_HEADS):
                rows = slice((h % grp) * ATTN_BLOCK, (h % grp + 1) * ATTN_BLOCK)
                dqs.append(dqg[h // grp][rows])
                dsink = jnp.where(lane == h, -jnp.sum(dsk[h // grp][rows], axis=0, keepdims=True), dsink)
            dq_ref[...] = jnp.concatenate(dqs, axis=-1)
            dk_ref[...] = dkc[...] + dkf[:ATTN_BLOCK]
            dv_ref[...] = (dvc[...] + dvf[:ATTN_BLOCK]).astype(dv_ref.dtype)
            dkc[...] = dkf[ATTN_BLOCK:]
            dvc[...] = dvf[ATTN_BLOCK:]
            ds_ref[...] += dsink

        @pl.when(n == nb)
        def _():
            dk_ref[...] = dkc[...]
            dv_ref[...] = dvc[...].astype(dv_ref.dtype)

    cur = lambda n: (jnp.minimum(n, nb - 1), 0)
    prev = lambda n: (jnp.clip(n - 1, 0, nb - 1), 0)
    vcol = (Q_W + KV_W) // KV_W
    return pl.pallas_call(
        body, name=name, grid=(nb + 1,),
        in_specs=[pl.BlockSpec((ATTN_BLOCK, Q_W), cur), pl.BlockSpec((ATTN_BLOCK, KV_W), cur),
                  pl.BlockSpec((ATTN_BLOCK, KV_W), prev),
                  pl.BlockSpec((ATTN_BLOCK, KV_W), lambda n: (jnp.minimum(n, nb - 1), vcol)),
                  pl.BlockSpec((ATTN_BLOCK, KV_W), lambda n: (jnp.clip(n - 1, 0, nb - 1), vcol)),
                  pl.BlockSpec((1, ATTN_HEADS), lambda n: (0, 0)),
                  pl.BlockSpec((ATTN_BLOCK, Q_W), cur), pl.BlockSpec((ATTN_BLOCK, ATTN_HEADS), cur),
                  pl.BlockSpec((ATTN_BLOCK, Q_W), cur)],
        out_specs=(pl.BlockSpec((ATTN_BLOCK, Q_W), cur), pl.BlockSpec((ATTN_BLOCK, KV_W), prev),
                   pl.BlockSpec((ATTN_BLOCK, KV_W), prev), pl.BlockSpec((1, ATTN_HEADS), lambda n: (0, 0))),
        out_shape=(jax.ShapeDtypeStruct((t, Q_W), F32), jax.ShapeDtypeStruct((t, KV_W), F32),
                   jax.ShapeDtypeStruct((t, KV_W), MXU_DTYPE), jax.ShapeDtypeStruct((1, ATTN_HEADS), F32)),
        scratch_shapes=[pltpu.VMEM((ATTN_BLOCK, KV_W), F32), pltpu.VMEM((ATTN_BLOCK, KV_W), F32)],
        compiler_params=_cp("arbitrary"))(q, k, k, proj, proj, sinks, y, lse, dmix)


GC_W = 256
_GB0, _GC0, _XI0 = 768 // GC_W, 1280 // GC_W, 1792 // GC_W
HALO = 8


def gconv_fwd(proj, conv_w, name, tm=512):
    t = proj.shape[0]
    hb = tm // HALO

    def body(gb_ref, gc_ref, xi_ref, gch_ref, xih_ref, w_ref, y_ref):
        i = pl.program_id(1)
        u = gc_ref[...] * xi_ref[...]
        uh = jnp.where(i == 0, 0.0, gch_ref[...] * xih_ref[...])
        up = jnp.concatenate([uh, u], axis=0)
        cv = w_ref[0:1, :] * up[HALO - 2:HALO - 2 + tm]
        cv = cv + w_ref[1:2, :] * up[HALO - 1:HALO - 1 + tm]
        cv = cv + w_ref[2:3, :] * u
        y_ref[...] = (gb_ref[...] * cv).astype(y_ref.dtype)

    def col(c0):
        return pl.BlockSpec((tm, GC_W), lambda cj, i: (i, c0 + cj))

    def halo(c0):
        return pl.BlockSpec((HALO, GC_W), lambda cj, i: (jnp.maximum(i * hb - 1, 0), c0 + cj))

    return pl.pallas_call(
        body, name=name, grid=(CONV_CH // GC_W, t // tm),
        in_specs=[col(_GB0), col(_GC0), col(_XI0), halo(_GC0), halo(_XI0),
                  pl.BlockSpec((3, GC_W), lambda cj, i: (0, cj))],
        out_specs=pl.BlockSpec((tm, GC_W), lambda cj, i: (i, cj)),
        out_shape=jax.ShapeDtypeStruct((t, CONV_CH), MXU_DTYPE),
        compiler_params=_cp("parallel", "parallel"))(proj, proj, proj, proj, proj, conv_w)


def gconv_bwd(proj, conv_w, dmix, name, tm=512):
    t = proj.shape[0]
    hb = tm // HALO
    nt = t // tm
    dy0 = Q_W // GC_W

    def body(gb_ref, gc_ref, xi_ref, gch_ref, xih_ref, gbn_ref, dyn_ref, dy_ref, w_ref,
             dgb_ref, dgc_ref, dxi_ref, dw_ref):
        i = pl.program_id(1)
        gc, xi, gb, dy = gc_ref[...], xi_ref[...], gb_ref[...], dy_ref[...]
        u = gc * xi
        uh = jnp.where(i == 0, 0.0, gch_ref[...] * xih_ref[...])
        up = jnp.concatenate([uh, u], axis=0)
        u2 = up[HALO - 2:HALO - 2 + tm]
        u1 = up[HALO - 1:HALO - 1 + tm]
        cv = w_ref[0:1, :] * u2 + w_ref[1:2, :] * u1 + w_ref[2:3, :] * u
        dgb_ref[...] = (dy * cv).astype(dgb_ref.dtype)
        dcv = dy * gb
        dcvn = jnp.where(i == nt - 1, 0.0, dyn_ref[...] * gbn_ref[...])
        dcvp = jnp.concatenate([dcv, dcvn], axis=0)
        du = w_ref[0:1, :] * dcvp[2:2 + tm] + w_ref[1:2, :] * dcvp[1:1 + tm] + w_ref[2:3, :] * dcv
        dgc_ref[...] = (du * xi).astype(dgc_ref.dtype)
        dxi_ref[...] = (du * gc).astype(dxi_ref.dtype)
        dw = jnp.concatenate([jnp.sum(dcv * u2, axis=0, keepdims=True), jnp.sum(dcv * u1, axis=0, keepdims=True),
                              jnp.sum(dcv * u, axis=0, keepdims=True)], axis=0)

        @pl.when(i == 0)
        def _():
            dw_ref[...] = dw

        @pl.when(i > 0)
        def _():
            dw_ref[...] += dw

    def col(c0):
        return pl.BlockSpec((tm, GC_W), lambda cj, i: (i, c0 + cj))

    def halo(c0):
        return pl.BlockSpec((HALO, GC_W), lambda cj, i: (jnp.maximum(i * hb - 1, 0), c0 + cj))

    def nxt(c0):
        return pl.BlockSpec((HALO, GC_W), lambda cj, i: (jnp.minimum((i + 1) * hb, t // HALO - 1), c0 + cj))

    out = pl.BlockSpec((tm, GC_W), lambda cj, i: (i, cj))
    return pl.pallas_call(
        body, name=name, grid=(CONV_CH // GC_W, nt),
        in_specs=[col(_GB0), col(_GC0), col(_XI0), halo(_GC0), halo(_XI0), nxt(_GB0), nxt(dy0), col(dy0),
                  pl.BlockSpec((3, GC_W), lambda cj, i: (0, cj))],
        out_specs=(out, out, out, pl.BlockSpec((3, GC_W), lambda cj, i: (0, cj))),
        out_shape=(jax.ShapeDtypeStruct((t, CONV_CH), MXU_DTYPE),) * 3 + (jax.ShapeDtypeStruct((3, CONV_CH), F32),),
        compiler_params=_cp("parallel", "arbitrary"))(proj, proj, proj, proj, proj, proj, dmix, dmix, conv_w)


_QKV_W = 3 * DN_W
_BA_COL = (4 * DN_W) // 128
_Z_COL = _QKV_W // DN_W


def gdn_prep_fwd(proj, conv_w, alog_row, dtb_row, name, tm=256):
    t = proj.shape[0]
    hb = tm // HALO
    qscale = DN_DIM ** -0.5

    def body(x_ref, xh_ref, w_ref, ba_ref, al_ref, dt_ref, q_ref, k_ref, v_ref, bg_ref):
        i = pl.program_id(0)
        for gi in range(3 * DN_HEADS):
            sl = slice(gi * DN_DIM, (gi + 1) * DN_DIM)
            xp = jnp.concatenate([jnp.where(i == 0, 0.0, xh_ref[:, sl]), x_ref[:, sl]], axis=0)
            c = w_ref[0:1, sl] * xp[HALO - 3:HALO - 3 + tm]
            for j in range(1, 4):
                c = c + w_ref[j:j + 1, sl] * xp[HALO - 3 + j:HALO - 3 + j + tm]
            s = c * _sigmoid(c)
            osl = slice((gi % DN_HEADS) * DN_DIM, (gi % DN_HEADS + 1) * DN_DIM)
            if gi < DN_HEADS:
                q_ref[:, osl] = s * lax.rsqrt(jnp.sum(s * s, axis=-1, keepdims=True) + EPS) * qscale
            elif gi < 2 * DN_HEADS:
                k_ref[:, osl] = s * lax.rsqrt(jnp.sum(s * s, axis=-1, keepdims=True) + EPS)
            else:
                v_ref[:, osl] = s
        ba = ba_ref[...]
        lane = lax.broadcasted_iota(jnp.int32, ba.shape, 1)
        gval = -jnp.exp(al_ref[...]) * _softplus(ba + dt_ref[...])
        bg_ref[...] = jnp.where(lane < DN_HEADS, _sigmoid(ba), jnp.where(lane < 2 * DN_HEADS, gval, 0.0))

    row = pl.BlockSpec((tm, DN_W), lambda i: (i, 0))
    one = pl.BlockSpec((1, 128), lambda i: (0, 0))
    return pl.pallas_call(
        body, name=name, grid=(t // tm,),
        in_specs=[pl.BlockSpec((tm, _QKV_W), lambda i: (i, 0)),
                  pl.BlockSpec((HALO, _QKV_W), lambda i: (jnp.maximum(i * hb - 1, 0), 0)),
                  pl.BlockSpec((4, _QKV_W), lambda i: (0, 0)),
                  pl.BlockSpec((tm, 128), lambda i: (i, _BA_COL)), one, one],
        out_specs=(row, row, row, pl.BlockSpec((tm, 128), lambda i: (i, 0))),
        out_shape=(jax.ShapeDtypeStruct((t, DN_W), F32),) * 3 + (jax.ShapeDtypeStruct((t, 128), F32),),
        compiler_params=_cp("parallel"))(proj, proj, conv_w, proj, alog_row, dtb_row)


def gdn_prep_bwd(proj, conv_w, alog_row, dtb_row, dq, dk, dv, dbg, name, tm=256):
    t = proj.shape[0]
    hb = tm // HALO
    nt = t // tm
    qscale = DN_DIM ** -0.5
    te = tm + HALO

    def body(x_ref, xh_ref, xn_ref, w_ref, ba_ref, al_ref, dt_ref, dq_ref, dk_ref, dv_ref,
             dqn_ref, dkn_ref, dvn_ref, dbg_ref, dx_ref, dba_ref, dw_ref, ddt_ref, dal_ref):
        i = pl.program_id(0)
        first = i == 0
        last = i == nt - 1
        dws = []
        for gi in range(3 * DN_HEADS):
            sl = slice(gi * DN_DIM, (gi + 1) * DN_DIM)
            osl = slice((gi % DN_HEADS) * DN_DIM, (gi % DN_HEADS + 1) * DN_DIM)
            xe = jnp.concatenate([jnp.where(first, 0.0, xh_ref[:, sl]), x_ref[:, sl], xn_ref[:, sl]], axis=0)
            c = w_ref[0:1, sl] * xe[HALO - 3:HALO - 3 + te]
            for j in range(1, 4):
                c = c + w_ref[j:j + 1, sl] * xe[HALO - 3 + j:HALO - 3 + j + te]
            sg = _sigmoid(c)
            s = c * sg
            d_ref, dn_ref = ((dq_ref, dqn_ref), (dk_ref, dkn_ref), (dv_ref, dvn_ref))[gi // DN_HEADS]
            dy = jnp.concatenate([d_ref[:, osl], jnp.where(last, 0.0, dn_ref[:, osl])], axis=0)
            if gi < 2 * DN_HEADS:
                r = lax.rsqrt(jnp.sum(s * s, axis=-1, keepdims=True) + EPS)
                sh = s * r
                ds = r * (dy - sh * jnp.sum(sh * dy, axis=-1, keepdims=True))
                if gi < DN_HEADS:
                    ds = ds * qscale
            else:
                ds = dy
            dc = ds * sg * (1.0 + c * (1.0 - sg))
            dx = w_ref[0:1, sl] * dc[3:3 + tm]
            for j in range(1, 4):
                dx = dx + w_ref[j:j + 1, sl] * dc[3 - j:3 - j + tm]
            dx_ref[:, sl] = dx.astype(dx_ref.dtype)
            dc0 = dc[:tm]
            dws.append(jnp.concatenate(
                [jnp.sum(dc0 * xe[HALO - 3 + j:HALO - 3 + j + tm], axis=0, keepdims=True) for j in range(4)], axis=0))
        dw = jnp.concatenate(dws, axis=-1)
        ba = ba_ref[...]
        dbgv = dbg_ref[...]
        lane = lax.broadcasted_iota(jnp.int32, ba.shape, 1)
        beta = _sigmoid(ba)
        ea = -jnp.exp(al_ref[...])
        zin = ba + dt_ref[...]
        is_b = lane < DN_HEADS
        is_a = (lane >= DN_HEADS) & (lane < 2 * DN_HEADS)
        da = jnp.where(is_a, dbgv * ea * _sigmoid(zin), 0.0)
        dba_ref[...] = jnp.where(is_b, dbgv * beta * (1.0 - beta), da).astype(dba_ref.dtype)
        ddt = jnp.sum(da, axis=0, keepdims=True)
        dal = jnp.sum(jnp.where(is_a, dbgv * ea * _softplus(zin), 0.0), axis=0, keepdims=True)

        @pl.when(first)
        def _():
            dw_ref[...] = dw
            ddt_ref[...] = ddt
            dal_ref[...] = dal

        @pl.when(i > 0)
        def _():
            dw_ref[...] += dw
            ddt_ref[...] += ddt
            dal_ref[...] += dal

    row = pl.BlockSpec((tm, DN_W), lambda i: (i, 0))
    nrow = pl.BlockSpec((HALO, DN_W), lambda i: (jnp.minimum((i + 1) * hb, t // HALO - 1), 0))
    one = pl.BlockSpec((1, 128), lambda i: (0, 0))
    return pl.pallas_call(
        body, name=name, grid=(nt,),
        in_specs=[pl.BlockSpec((tm, _QKV_W), lambda i: (i, 0)),
                  pl.BlockSpec((HALO, _QKV_W), lambda i: (jnp.maximum(i * hb - 1, 0), 0)),
                  pl.BlockSpec((HALO, _QKV_W), lambda i: (jnp.minimum((i + 1) * hb, t // HALO - 1), 0)),
                  pl.BlockSpec((4, _QKV_W), lambda i: (0, 0)),
                  pl.BlockSpec((tm, 128), lambda i: (i, _BA_COL)), one, one,
                  row, row, row, nrow, nrow, nrow, pl.BlockSpec((tm, 128), lambda i: (i, 0))],
        out_specs=(pl.BlockSpec((tm, _QKV_W), lambda i: (i, 0)), pl.BlockSpec((tm, 128), lambda i: (i, 0)),
                   pl.BlockSpec((4, _QKV_W), lambda i: (0, 0)), one, one),
        out_shape=(jax.ShapeDtypeStruct((t, _QKV_W), MXU_DTYPE), jax.ShapeDtypeStruct((t, 128), MXU_DTYPE),
                   jax.ShapeDtypeStruct((4, _QKV_W), F32), jax.ShapeDtypeStruct((1, 128), F32),
                   jax.ShapeDtypeStruct((1, 128), F32)),
        compiler_params=_cp("arbitrary"))(proj, proj, proj, conv_w, proj, alog_row, dtb_row, dq, dk, dv, dq, dk, dv, dbg)


def _chunk_masks():
    r = lax.broadcasted_iota(jnp.int32, (DN_CHUNK, DN_CHUNK), 0)
    c = lax.broadcasted_iota(jnp.int32, (DN_CHUNK, DN_CHUNK), 1)
    return r >= c, r > c


def _inv_unit_lower_many(mats):
    r = lax.broadcasted_iota(jnp.int32, mats[0].shape, 0)
    c = lax.broadcasted_iota(jnp.int32, mats[0].shape, 1)
    eye = jnp.where(r == c, 1.0, 0.0)
    xs = [eye - a for a in mats]
    pws = [_hi(a, a) for a in mats]
    for step in range(5):
        xs = [x + _hi(x, pw) for x, pw in zip(xs, pws)]
        if step < 4:
            pws = [_hi(pw, pw) for pw in pws]
    return xs


def _chunk_common(q, k, v, beta, gc, gcr, lower, strict):
    gam = jnp.exp(jnp.where(lower, gc - gcr, NEG))
    eg = jnp.exp(gc)
    gl = gc[DN_CHUNK - 1:DN_CHUNK, :]
    kdf = jnp.exp(gl - gc)
    kb = k * beta
    bmat = _mx_nt(kb, k)
    qmat = _mx_nt(q, k)
    return gam, eg, jnp.exp(gl), kdf, kb, bmat, qmat


def gdn_fwd(q, k, v, bg, name):
    t = q.shape[0]
    n_chunks = t // DN_CHUNK

    def body(q_ref, k_ref, v_ref, bg_ref, o_ref, sall_ref, tall_ref, s_ref):
        n = pl.program_id(0)

        @pl.when(n == 0)
        def _():
            s_ref[...] = jnp.zeros_like(s_ref)

        lower, strict = _chunk_masks()
        bgv = bg_ref[...]
        gcs = _hi(jnp.where(lower, 1.0, 0.0), bgv)
        gcs_t = gcs.T
        hs = range(DN_HEADS)
        sl = [slice(h * DN_DIM, (h + 1) * DN_DIM) for h in hs]
        st = [s_ref[h] for h in hs]
        for h in hs:
            sall_ref[0, h] = st[h]
        qh = [q_ref[:, sl[h]] for h in hs]
        kh = [k_ref[:, sl[h]] for h in hs]
        vh = [v_ref[:, sl[h]] for h in hs]
        beta = [bgv[:, h:h + 1] for h in hs]
        com = [_chunk_common(qh[h], kh[h], vh[h], beta[h], gcs[:, DN_HEADS + h:DN_HEADS + h + 1],
                             gcs_t[DN_HEADS + h:DN_HEADS + h + 1, :], lower, strict) for h in hs]
        gam, eg, dec, kdf, kb, bmat, qmat = zip(*com)
        tms = _inv_unit_lower_many([jnp.where(strict, bmat[h] * gam[h], 0.0) for h in hs])
        for h in hs:
            tall_ref[0, h] = tms[h]
        uw = [_hi(tms[h], jnp.concatenate([vh[h] * beta[h], kb[h] * eg[h]], axis=-1)) for h in hs]
        v_new = [uw[h][:, :DN_DIM] - _mx(uw[h][:, DN_DIM:], st[h]) for h in hs]
        o_st = [_mx(qh[h] * eg[h], st[h]) for h in hs]
        o_in = [_mx(qmat[h] * gam[h], v_new[h]) for h in hs]
        s_up = [_mx_tn(kh[h] * kdf[h], v_new[h]) for h in hs]
        for h in hs:
            o_ref[:, sl[h]] = o_st[h] + o_in[h]
            s_ref[h] = st[h] * dec[h] + s_up[h]

    row = pl.BlockSpec((DN_CHUNK, DN_W), lambda n: (n, 0))
    return pl.pallas_call(
        body, name=name, grid=(n_chunks,),
        in_specs=[row, row, row, pl.BlockSpec((DN_CHUNK, 128), lambda n: (n, 0))],
        out_specs=(row, pl.BlockSpec((1, DN_HEADS, DN_DIM, DN_DIM), lambda n: (n, 0, 0, 0)),
                   pl.BlockSpec((1, DN_HEADS, DN_CHUNK, DN_CHUNK), lambda n: (n, 0, 0, 0))),
        out_shape=(jax.ShapeDtypeStruct((t, DN_W), F32),
                   jax.ShapeDtypeStruct((n_chunks, DN_HEADS, DN_DIM, DN_DIM), F32),
                   jax.ShapeDtypeStruct((n_chunks, DN_HEADS, DN_CHUNK, DN_CHUNK), F32)),
        scratch_shapes=[pltpu.VMEM((DN_HEADS, DN_DIM, DN_DIM), F32)],
        compiler_params=_cp("arbitrary"))(q, k, v, bg)


def gdn_bwd(q, k, v, bg, sall, tall, do, name):
    t = q.shape[0]
    n_chunks = t // DN_CHUNK

    def body(q_ref, k_ref, v_ref, bg_ref, sall_ref, tall_ref, do_ref, dq_ref, dk_ref, dv_ref, dbg_ref, ds_ref):
        n = pl.program_id(0)

        @pl.when(n == 0)
        def _():
            ds_ref[...] = jnp.zeros_like(ds_ref)

        lower, strict = _chunk_masks()
        ltri = jnp.where(lower, 1.0, 0.0)
        bgv = bg_ref[...]
        gcs = _hi(ltri, bgv)
        gcs_t = gcs.T
        lane = lax.broadcasted_iota(jnp.int32, (DN_CHUNK, 128), 1)
        rowi = lax.broadcasted_iota(jnp.int32, (DN_CHUNK, 1), 0)
        hs = range(DN_HEADS)
        each = lambda fn, *ls: [fn(*a) for a in zip(*ls)]
        rsum = lambda a: jnp.sum(a, axis=-1, keepdims=True)
        sl = [slice(h * DN_DIM, (h + 1) * DN_DIM) for h in hs]
        st = [sall_ref[0, h] for h in hs]
        tms = [tall_ref[0, h] for h in hs]
        dsn = [ds_ref[h] for h in hs]
        qh = [q_ref[:, sl[h]] for h in hs]
        kh = [k_ref[:, sl[h]] for h in hs]
        vh = [v_ref[:, sl[h]] for h in hs]
        doh = [do_ref[:, sl[h]] for h in hs]
        beta = [bgv[:, h:h + 1] for h in hs]
        com = [_chunk_common(qh[h], kh[h], vh[h], beta[h], gcs[:, DN_HEADS + h:DN_HEADS + h + 1],
                             gcs_t[DN_HEADS + h:DN_HEADS + h + 1, :], lower, strict) for h in hs]
        gam, eg, dec, kdf, kb, bmat, qmat = zip(*com)
        rhs_w = each(lambda a, b: a * b, kb, eg)
        uw = each(lambda t_, v_, b_, r_: _hi(t_, jnp.concatenate([v_ * b_, r_], axis=-1)), tms, vh, beta, rhs_w)
        qd = each(lambda a, b: a * b, qh, eg)
        kd = each(lambda a, b: a * b, kh, kdf)
        pmat = each(lambda a, b: a * b, qmat, gam)
        v_new = each(lambda uw_, s_: uw_[:, :DN_DIM] - _mx(uw_[:, DN_DIM:], s_), uw, st)
        dqd = each(_mx_nt, doh, st)
        ds_o = each(_mx_tn, qd, doh)
        dp = each(lambda d_, v_: jnp.where(lower, _mx_nt(d_, v_), 0.0), doh, v_new)
        dvn_o = each(_mx_tn, pmat, doh)
        ddec = each(lambda d_, s_: jnp.sum(rsum(d_ * s_), axis=0, keepdims=True), dsn, st)
        dkd = each(_mx_nt, v_new, dsn)
        dvn = each(lambda a, k_, d_: a + _mx(k_, d_), dvn_o, kd, dsn)
        dw = each(lambda d_, s_: -_mx_nt(d_, s_), dvn, st)
        ds_w = each(lambda uw_, d_: _mx_tn(uw_[:, DN_DIM:], d_), uw, dvn)
        for h in hs:
            ds_ref[h] = ds_o[h] + dec[h] * dsn[h] - ds_w[h]
        dr = each(lambda t_, a, b: _hi_tn(t_, jnp.concatenate([a, b], axis=-1)), tms, dvn, dw)
        da = each(lambda r_, uw_: jnp.where(strict, -_hi_nt(r_, uw_), 0.0), dr, uw)
        dru = [r_[:, :DN_DIM] for r_ in dr]
        drw = [r_[:, DN_DIM:] for r_ in dr]
        db = each(lambda a, b: a * b, da, gam)
        dq_m = each(lambda a, b: a * b, dp, gam)
        e = each(lambda a, bm, p_, qm, g_: (a * bm + p_ * qm) * g_, da, bmat, dp, qmat, gam)
        dkb = each(lambda b_, k_, r_, e_: _mx(b_, k_) + r_ * e_, db, kh, drw, eg)
        dk = each(lambda b_, kb_, m_, q_, d_, f_: _mx_tn(b_, kb_) + _mx_tn(m_, q_) + d_ * f_, db, kb, dq_m, qh, dkd, kdf)
        dq = each(lambda m_, k_, d_, e_: _mx(m_, k_) + d_ * e_, dq_m, kh, dqd, eg)
        tk = each(lambda a, b: rsum(a * b), dkd, kd)
        dbeta_all = jnp.zeros((DN_CHUNK, 128), F32)
        dgc_all = jnp.zeros((DN_CHUNK, 128), F32)
        for h in hs:
            dgc = (jnp.sum(e[h], axis=1, keepdims=True) - jnp.sum(e[h].T, axis=1, keepdims=True)
                   + rsum(dqd[h] * qd[h]) - tk[h] + rsum(drw[h] * rhs_w[h]))
            dgl = jnp.sum(tk[h], axis=0, keepdims=True) + ddec[h] * dec[h]
            dgc = dgc + jnp.where(rowi == DN_CHUNK - 1, dgl, 0.0)
            dbeta = rsum(dru[h] * vh[h]) + rsum(dkb[h] * kh[h])
            dq_ref[:, sl[h]] = dq[h]
            dk_ref[:, sl[h]] = dk[h] + dkb[h] * beta[h]
            dv_ref[:, sl[h]] = dru[h] * beta[h]
            dbeta_all = jnp.where(lane == h, dbeta, dbeta_all)
            dgc_all = jnp.where(lane == DN_HEADS + h, dgc, dgc_all)
        dbg_ref[...] = dbeta_all + _hi_tn(ltri, dgc_all)

    rev = lambda n: (n_chunks - 1 - n, 0)
    row = pl.BlockSpec((DN_CHUNK, DN_W), rev)
    small = pl.BlockSpec((DN_CHUNK, 128), rev)
    return pl.pallas_call(
        body, name=name, grid=(n_chunks,),
        in_specs=[row, row, row, small,
                  pl.BlockSpec((1, DN_HEADS, DN_DIM, DN_DIM), lambda n: (n_chunks - 1 - n, 0, 0, 0)),
                  pl.BlockSpec((1, DN_HEADS, DN_CHUNK, DN_CHUNK), lambda n: (n_chunks - 1 - n, 0, 0, 0)), row],
        out_specs=(row, row, row, small),
        out_shape=(jax.ShapeDtypeStruct((t, DN_W), F32),) * 3 + (jax.ShapeDtypeStruct((t, 128), F32),),
        scratch_shapes=[pltpu.VMEM((DN_HEADS, DN_DIM, DN_DIM), F32)],
        compiler_params=_cp("arbitrary"))(q, k, v, bg, sall, tall, do)


def gdn_out_fwd(o, proj, o_gain, name, tm=256):
    t = o.shape[0]

    def body(o_ref, z_ref, g_ref, y_ref):
        for h in range(DN_HEADS):
            sl = slice(h * DN_DIM, (h + 1) * DN_DIM)
            ov, zv = o_ref[:, sl], z_ref[:, sl]
            r = lax.rsqrt(jnp.mean(ov * ov, axis=-1, keepdims=True) + EPS)
            y_ref[:, sl] = (ov * r * g_ref[...] * (zv * _sigmoid(zv))).astype(y_ref.dtype)

    row = pl.BlockSpec((tm, DN_W), lambda i: (i, 0))
    return pl.pallas_call(
        body, name=name, grid=(t // tm,),
        in_specs=[row, pl.BlockSpec((tm, DN_W), lambda i: (i, _Z_COL)), pl.BlockSpec((1, DN_DIM), lambda i: (0, 0))],
        out_specs=row, out_shape=jax.ShapeDtypeStruct((t, DN_W), MXU_DTYPE),
        compiler_params=_cp("parallel"))(o, proj, o_gain)


def gdn_out_bwd(o, proj, o_gain, dy, name, tm=256):
    t = o.shape[0]

    def body(o_ref, z_ref, g_ref, dy_ref, do_ref, dz_ref, dg_ref):
        i = pl.program_id(0)
        dg = jnp.zeros((1, DN_DIM), F32)
        for h in range(DN_HEADS):
            sl = slice(h * DN_DIM, (h + 1) * DN_DIM)
            ov, zv, dyv = o_ref[:, sl], z_ref[:, sl], dy_ref[:, sl]
            r = lax.rsqrt(jnp.mean(ov * ov, axis=-1, keepdims=True) + EPS)
            oh = ov * r
            sg = _sigmoid(zv)
            dz_ref[:, sl] = (dyv * oh * g_ref[...] * sg * (1.0 + zv * (1.0 - sg))).astype(dz_ref.dtype)
            don = dyv * (zv * sg)
            dg = dg + jnp.sum(don * oh, axis=0, keepdims=True)
            doh = don * g_ref[...]
            do_ref[:, sl] = r * (doh - oh * jnp.mean(doh * oh, axis=-1, keepdims=True))

        @pl.when(i == 0)
        def _():
            dg_ref[...] = dg

        @pl.when(i > 0)
        def _():
            dg_ref[...] += dg

    row = pl.BlockSpec((tm, DN_W), lambda i: (i, 0))
    one = pl.BlockSpec((1, DN_DIM), lambda i: (0, 0))
    return pl.pallas_call(
        body, name=name, grid=(t // tm,),
        in_specs=[row, pl.BlockSpec((tm, DN_W), lambda i: (i, _Z_COL)), one, row],
        out_specs=(row, row, one),
        out_shape=(jax.ShapeDtypeStruct((t, DN_W), F32), jax.ShapeDtypeStruct((t, DN_W), MXU_DTYPE),
                   jax.ShapeDtypeStruct((1, DN_DIM), F32)),
        compiler_params=_cp("arbitrary"))(o, proj, o_gain, dy)


def _peer(k):
    x, y, c = lax.axis_index("x"), lax.axis_index("y"), lax.axis_index("c")
    px = 1 - x if k & 4 else x
    py = 1 - y if k & 2 else y
    pc = 1 - c if k & 1 else c
    return (px, py, pc), 4 * px + 2 * py + pc


def all_gather(shards, name):
    na = len(shards)

    def body(*refs):
        ins, outs = refs[:na], refs[na:2 * na]
        send_sems, recv_sems, local_sems = refs[2 * na:]
        _, me = _peer(0)
        local = [pltpu.make_async_copy(ins[a], outs[a].at[me], local_sems.at[a]) for a in range(na)]
        for cp in local:
            cp.start()
        sends = []
        for k in range(1, N_DEV):
            peer, _ = _peer(k)
            for a in range(na):
                cp = pltpu.make_async_remote_copy(
                    src_ref=ins[a], dst_ref=outs[a].at[me], send_sem=send_sems.at[a, k - 1],
                    recv_sem=recv_sems.at[a, k - 1], device_id=peer, device_id_type=MESH)
                cp.start()
                sends.append(cp)
        for k in range(1, N_DEV):
            peer, pid = _peer(k)
            for a in range(na):
                pltpu.make_async_remote_copy(
                    src_ref=ins[a], dst_ref=outs[a].at[pid], send_sem=send_sems.at[a, k - 1],
                    recv_sem=recv_sems.at[a, k - 1], device_id=peer, device_id_type=MESH).wait_recv()
        for cp in sends:
            cp.wait_send()
        for cp in local:
            cp.wait()

    anyspec = pl.BlockSpec(memory_space=pl.ANY)
    return pl.pallas_call(
        body, name=name, in_specs=[anyspec] * na, out_specs=tuple([anyspec] * na),
        out_shape=tuple(jax.ShapeDtypeStruct((N_DEV,) + s.shape, s.dtype) for s in shards),
        scratch_shapes=[pltpu.SemaphoreType.DMA((na, N_DEV - 1)), pltpu.SemaphoreType.DMA((na, N_DEV - 1)),
                        pltpu.SemaphoreType.DMA((na,))],
        compiler_params=pltpu.CompilerParams(has_side_effects=True))(*shards)


_HBM = pl.BlockSpec(memory_space=pltpu.HBM)
_SEM = pl.BlockSpec(memory_space=pltpu.SEMAPHORE)
_DATAFLOW = pltpu.SideEffectType.DATAFLOW_SIDE_EFFECTING
N_PEER = N_DEV - 1


def send_start(srcs, name, scatter, after):
    na = len(srcs)
    ns = 2 * N_PEER * na
    lands = [lax.empty((N_DEV,) + (s.shape[1:] if scatter else s.shape), s.dtype) for s in srcs]
    extra = [] if after is None else [after]

    def body(*refs):
        src_refs, land_refs = refs[:na], refs[na:2 * na]
        sems, token = refs[2 * na + len(extra):2 * na + len(extra) + ns], refs[-1]
        _, me = _peer(0)
        for k in range(1, N_DEV):
            peer, pid = _peer(k)
            for a in range(na):
                pltpu.make_async_remote_copy(
                    src_ref=src_refs[a].at[pid] if scatter else src_refs[a], dst_ref=land_refs[a].at[me],
                    send_sem=sems[2 * (a * N_PEER + k - 1)], recv_sem=sems[2 * (a * N_PEER + k - 1) + 1],
                    device_id=peer, device_id_type=MESH).start()
        token[...] = jnp.zeros_like(token)

    hbm = lambda arrs: tuple(pltpu.HBM(a.shape, a.dtype) for a in arrs)
    outs = pl.pallas_call(
        body, name=name,
        out_shape=(pltpu.SemaphoreType.DMA(()),) * ns + hbm(srcs) + hbm(lands) + (jax.ShapeDtypeStruct((8, 128), F32),),
        in_specs=[_HBM] * (2 * na) + [pl.BlockSpec(memory_space=pl.ANY)] * len(extra),
        out_specs=(_SEM,) * ns + (_HBM,) * (2 * na) + (pl.BlockSpec(memory_space=pltpu.VMEM),),
        input_output_aliases={i: ns + i for i in range(2 * na)},
        compiler_params=pltpu.CompilerParams(has_side_effects=_DATAFLOW),
    )(*[pltpu.with_memory_space_constraint(a, pltpu.HBM) for a in list(srcs) + lands], *extra)
    return outs[:ns], outs[ns:ns + na], outs[ns + na:ns + 2 * na], outs[-1]


def send_wait(sems, srcs_thru, lands_thru, name, scatter, after):
    na = len(srcs_thru)
    ns = 2 * N_PEER * na

    def body(*refs):
        src_refs, land_refs, sm = refs[:na], refs[na:2 * na], refs[2 * na:2 * na + ns]
        for k in range(1, N_DEV):
            peer, pid = _peer(k)
            for a in range(na):
                cp = pltpu.make_async_remote_copy(
                    src_ref=src_refs[a].at[pid] if scatter else src_refs[a], dst_ref=land_refs[a].at[pid],
                    send_sem=sm[2 * (a * N_PEER + k - 1)], recv_sem=sm[2 * (a * N_PEER + k - 1) + 1],
                    device_id=peer, device_id_type=MESH)
                cp.wait_send()
                cp.wait_recv()

    hbm = lambda arrs: tuple(pltpu.HBM(a.shape, a.dtype) for a in arrs)
    outs = pl.pallas_call(
        body, name=name, out_shape=hbm(srcs_thru) + hbm(lands_thru),
        in_specs=[_HBM] * (2 * na) + [_SEM] * ns + [pl.BlockSpec(memory_space=pl.ANY)], out_specs=(_HBM,) * (2 * na),
        input_output_aliases={i: i for i in range(2 * na)},
        compiler_params=pltpu.CompilerParams(has_side_effects=_DATAFLOW),
    )(*srcs_thru, *lands_thru, *sems, after)
    return outs[na:]


def _adamw(w, g, m, v):
    m = ADAM_B1 * m + (1.0 - ADAM_B1) * g
    v = ADAM_B2 * v + (1.0 - ADAM_B2) * (g * g)
    m_hat = m / (1.0 - ADAM_B1 ** ADAM_STEP)
    v_hat = v / (1.0 - ADAM_B2 ** ADAM_STEP)
    return -ADAM_LR * (m_hat / (jnp.sqrt(v_hat) + ADAM_EPS) + ADAM_WD * w), m, v


def adam_sum(w, pieces, m, v, name):
    r, c = w.shape
    tr = r
    for cand in (256, 128, 64, 32, 16, 8):
        if r % cand == 0:
            tr = cand
            break

    def body(w_ref, p_ref, m_ref, v_ref, g_ref, d_ref, nm_ref, nv_ref):
        g = p_ref[0].astype(F32)
        for s in range(1, N_DEV):
            g = g + p_ref[s].astype(F32)
        g_ref[...] = g
        d_ref[...], nm_ref[...], nv_ref[...] = _adamw(w_ref[...], g, m_ref[...], v_ref[...])

    row = pl.BlockSpec((tr, c), lambda i: (i, 0))
    out = jax.ShapeDtypeStruct((r, c), F32)
    return pl.pallas_call(
        body, name=name, grid=(r // tr,),
        in_specs=[row, pl.BlockSpec((N_DEV, tr, c), lambda i: (0, i, 0)), row, row],
        out_specs=(row,) * 4, out_shape=(out,) * 4, compiler_params=_cp("parallel"))(w, pieces, m, v)


def sum_rows(gathered, name):
    _, r, c = gathered.shape

    def body(p_ref, o_ref):
        g = p_ref[0]
        for s in range(1, N_DEV):
            g = g + p_ref[s]
        o_ref[...] = g

    return pl.pallas_call(body, name=name, out_shape=jax.ShapeDtypeStruct((r, c), F32))(gathered)


def adam_small(w, g, m, v, name):
    def body(w_ref, g_ref, m_ref, v_ref, d_ref, nm_ref, nv_ref):
        d_ref[...], nm_ref[...], nv_ref[...] = _adamw(w_ref[...], g_ref[...], m_ref[...], v_ref[...])

    out = jax.ShapeDtypeStruct(w.shape, F32)
    return pl.pallas_call(body, name=name, out_shape=(out,) * 3)(w, g, m, v)


def _rope_tables(t):
    inv_freq = 10000.0 ** (-jnp.arange(0, HEAD_DIM, 2, dtype=F32) / HEAD_DIM)
    ang = jnp.arange(t, dtype=F32)[:, None] * inv_freq[None, :]
    cos, sin = jnp.cos(ang), jnp.sin(ang)
    return jnp.concatenate([cos, cos], axis=-1), jnp.concatenate([sin, sin], axis=-1)


def _lane_row(vec8):
    return jnp.pad(vec8.reshape(1, DN_HEADS), ((0, 0), (DN_HEADS, 128 - 2 * DN_HEADS)))


def _ffn_fwd(x, norm_g, w_gu, w_d, tag):
    f = rms_fwd(x, norm_g, f"{tag}_norm")
    gu = mm_nn(f, w_gu, f"{tag}_gate_up")
    a = swiglu_fwd(gu, f"{tag}_act")
    return mm_nn(a, w_d, f"{tag}_down", res=x), (f, gu, a)


def _ffn_bwd(x, norm_g, w_gu, w_d, saved, dy, tag, after=None):
    f, gu, a = saved
    da = mm_nt(dy, w_d, f"{tag}_d_act", after=after)
    dwd = mm_tn(a, dy, f"{tag}_dw_down")
    dgu = swiglu_bwd(gu, da, f"{tag}_d_gate_up")
    df = mm_nt(dgu, w_gu, f"{tag}_d_normed")
    dwgu = mm_tn(f, dgu, f"{tag}_dw_gate_up")
    dx, dg = rms_bwd(x, norm_g, df, dy, f"{tag}_d_norm")
    return dx, dwgu, dwd, dg


def local_step(x, target, small, weights_of, grads_out, after=None):
    t = x.shape[0]
    cosf, sinf = _rope_tables(t)
    alog_row, dtb_row = _lane_row(small["odd_a_log"]), _lane_row(small["odd_dt_bias"])

    h0 = rms_fwd(x, small["even_norm"], "even_norm", after=after)
    we = weights_of("even", h0)
    proj0 = mm_nn(h0, we["w_in"], "even_in_proj")
    qr, kr = qk_prep_fwd(proj0, small["even_q_gain"], small["even_k_gain"], cosf, sinf, "even_qk_prep")
    y_attn, lse = swa_fwd(qr, kr, proj0, small["even_sinks"], "even_swa")
    y_conv = gconv_fwd(proj0, small["even_conv_w"], "even_gconv")
    mix0 = jnp.concatenate([y_attn.astype(MXU_DTYPE), y_conv], axis=-1)
    x1 = mm_nn(mix0, we["w_out"], "even_out_proj", res=x)
    w0 = weights_of("ffn0", x1)
    x2, ffn0 = _ffn_fwd(x1, small["ffn_norm0"], w0["gate_up"], w0["down"], "ffn0")

    wo = weights_of("odd", x2)
    h1 = rms_fwd(x2, small["odd_norm"], "odd_norm")
    proj1 = mm_nn(h1, wo["w_in"], "odd_in_proj")
    qn, kn, vs, bg = gdn_prep_fwd(proj1, small["odd_conv_w"], alog_row, dtb_row, "odd_prep")
    o, sall, tall = gdn_fwd(qn, kn, vs, bg, "odd_delta_rule")
    og = gdn_out_fwd(o, proj1, small["odd_o_gain"], "odd_gate_norm")
    x3 = mm_nn(og, wo["w_out"], "odd_out_proj", res=x2)
    w1 = weights_of("ffn1", x3)
    x4, ffn1 = _ffn_fwd(x3, small["ffn_norm1"], w1["gate_up"], w1["down"], "ffn1")

    dy, loss_row = loss_head(x4, target, "loss_head")

    gs = {}
    dx3, dwgu, dwd, gs["ffn_norm1"] = _ffn_bwd(x3, small["ffn_norm1"], w1["gate_up"], w1["down"], ffn1, dy, "ffn1")
    tok = grads_out("ffn1", {"gate_up": dwgu, "down": dwd})

    dog = mm_nt(dx3, wo["w_out"], "odd_d_gated", after=tok)
    dwo = mm_tn(og, dx3, "odd_dw_out")
    do, dz, gs["odd_o_gain"] = gdn_out_bwd(o, proj1, small["odd_o_gain"], dog, "odd_d_gate_norm")
    dqn, dkn, dvs, dbg = gdn_bwd(qn, kn, vs, bg, sall, tall, do, "odd_d_delta_rule")
    dqkv, dba, gs["odd_conv_w"], ddt_row, dal_row = gdn_prep_bwd(
        proj1, small["odd_conv_w"], alog_row, dtb_row, dqn, dkn, dvs, dbg, "odd_d_prep")
    gs["odd_dt_bias"] = ddt_row[:, DN_HEADS:2 * DN_HEADS]
    gs["odd_a_log"] = dal_row[:, DN_HEADS:2 * DN_HEADS]
    dproj1 = jnp.concatenate([dqkv, dz, dba], axis=-1)
    dh1 = mm_nt(dproj1, wo["w_in"], "odd_d_normed")
    dwi = mm_tn(h1, dproj1, "odd_dw_in")
    dx2, gs["odd_norm"] = rms_bwd(x2, small["odd_norm"], dh1, dx3, "odd_d_norm")
    tok = grads_out("odd", {"w_in": dwi, "w_out": dwo})

    dx1, dwgu, dwd, gs["ffn_norm0"] = _ffn_bwd(x1, small["ffn_norm0"], w0["gate_up"], w0["down"], ffn0, dx2, "ffn0",
                                               after=tok)
    tok = grads_out("ffn0", {"gate_up": dwgu, "down": dwd})

    dmix = mm_nt(dx1, we["w_out"], "even_d_mix", after=tok)
    dwo = mm_tn(mix0, dx1, "even_dw_out")
    dqr, dkr, dv, gs["even_sinks"] = swa_bwd(qr, kr, proj0, small["even_sinks"], y_attn, lse, dmix, "even_d_swa")
    dqk, gs["even_q_gain"], gs["even_k_gain"] = qk_prep_bwd(
        proj0, small["even_q_gain"], small["even_k_gain"], cosf, sinf, dqr, dkr, "even_d_qk_prep")
    dgb, dgc, dxi, gs["even_conv_w"] = gconv_bwd(proj0, small["even_conv_w"], dmix, "even_d_gconv")
    dproj0 = jnp.concatenate([dqk, dv, dgb, dgc, dxi], axis=-1)
    dh0 = mm_nt(dproj0, we["w_in"], "even_d_normed")
    dwi = mm_tn(h0, dproj0, "even_dw_in")
    grad_x, gs["even_norm"] = rms_bwd(x, small["even_norm"], dh0, dx1, "even_d_norm")
    grads_out("even", {"w_in": dwi, "w_out": dwo})
    return loss_row, grad_x, gs


_SMALL_ORDER = ("even_norm", "even_q_gain", "even_k_gain", "even_sinks", "odd_a_log", "odd_dt_bias", "odd_o_gain",
                "ffn_norm0", "ffn_norm1", "odd_norm", "even_conv_w", "odd_conv_w")
_SMALL_SIZE = {"even_norm": 1024, "even_q_gain": 64, "even_k_gain": 64, "even_sinks": 8, "odd_a_log": 8,
               "odd_dt_bias": 8, "odd_o_gain": 128, "ffn_norm0": 1024, "ffn_norm1": 1024, "odd_norm": 1024,
               "even_conv_w": 3 * 512, "odd_conv_w": 4 * 3072}
_N_REPL = 9


def _pack_rows(vals):
    flat = jnp.concatenate([v.reshape(-1) for v in vals])
    pad = (-flat.shape[0]) % 1024
    return jnp.pad(flat, (0, pad)).reshape(-1, 128)


def _my_block(full, size, axis):
    me = 4 * lax.axis_index("x") + 2 * lax.axis_index("y") + lax.axis_index("c")
    return lax.dynamic_slice_in_dim(full, me * size, size, axis=axis)


def _col_gathered(g):
    return g.transpose(1, 0, 2).reshape(g.shape[1], N_DEV * g.shape[2])


def _col_pieces(dw):
    k, n8 = dw.shape
    return dw.reshape(k, N_DEV, n8 // N_DEV).transpose(1, 0, 2)


def kernel(x, even_norm, even_w_in, even_q_gain, even_k_gain, even_sinks, even_conv_w, even_w_out, odd_norm, odd_w_in, odd_conv_w, odd_a_log, odd_dt_bias, odd_o_gain, odd_w_out, ffn_norm, ffn_w_gate_up, ffn_w_down, loss_target, m_even_norm, m_even_w_in, m_even_q_gain, m_even_k_gain, m_even_sinks, m_even_conv_w, m_even_w_out, m_odd_norm, m_odd_w_in, m_odd_conv_w, m_odd_a_log, m_odd_dt_bias, m_odd_o_gain, m_odd_w_out, m_ffn_norm, m_ffn_w_gate_up, m_ffn_w_down, v_even_norm, v_even_w_in, v_even_q_gain, v_even_k_gain, v_even_sinks, v_even_conv_w, v_even_w_out, v_odd_norm, v_odd_w_in, v_odd_conv_w, v_odd_a_log, v_odd_dt_bias, v_odd_o_gain, v_odd_w_out, v_ffn_norm, v_ffn_w_gate_up, v_ffn_w_down):
    t = x.shape[1]
    d = D_MODEL

    me = 4 * lax.axis_index("x") + 2 * lax.axis_index("y") + lax.axis_index("c")
    fpd = D_FF // N_DEV
    shard = {
        "even": {"w_in": even_w_in.reshape(d, EVEN_IN_W // N_DEV), "w_out": even_w_out.reshape(d // N_DEV, d)},
        "ffn0": {"gate_up": ffn_w_gate_up[0], "down": ffn_w_down[0]},
        "odd": {"w_in": odd_w_in.reshape(d, ODD_IN_W // N_DEV), "w_out": odd_w_out.reshape(d // N_DEV, d)},
        "ffn1": {"gate_up": ffn_w_gate_up[1], "down": ffn_w_down[1]},
    }
    mom_m = {
        "even": {"w_in": m_even_w_in, "w_out": m_even_w_out}, "odd": {"w_in": m_odd_w_in, "w_out": m_odd_w_out},
        "ffn0": {"gate_up": m_ffn_w_gate_up[0], "down": m_ffn_w_down[0]},
        "ffn1": {"gate_up": m_ffn_w_gate_up[1], "down": m_ffn_w_down[1]},
    }
    mom_v = {
        "even": {"w_in": v_even_w_in, "w_out": v_even_w_out}, "odd": {"w_in": v_odd_w_in, "w_out": v_odd_w_out},
        "ffn0": {"gate_up": v_ffn_w_gate_up[0], "down": v_ffn_w_down[0]},
        "ffn1": {"gate_up": v_ffn_w_gate_up[1], "down": v_ffn_w_down[1]},
    }

    def whole(group, parts):
        col, row = tuple(shard[group])
        w_col = _col_gathered(parts[0])
        if group == "odd":
            w_col = jnp.pad(w_col, ((0, 0), (0, ODD_IN_PAD - ODD_IN_W)))
        return {col: w_col, row: parts[1].reshape(-1, d)}

    def own_slot(lands, blocks):
        return [lax.dynamic_update_index_in_dim(l, b, me, 0) for l, b in zip(lands, blocks)]

    shard_rows = _pack_rows([odd_norm, even_conv_w, odd_conv_w])
    (small_g,) = all_gather([shard_rows], "gather_small_weights")
    wire = {g: [a.astype(MXU_DTYPE) for a in shard[g].values()] for g in shard}
    gathers, tok = {}, small_g
    for g in shard:
        sems, srcs_thru, lands_thru, tok = send_start(wire[g], f"gather_{g}_start", False, tok)
        gathers[g] = (sems, srcs_thru, lands_thru)

    def weights_of(group, after):
        lands = send_wait(*gathers[group], f"gather_{group}_wait", False, after)
        return whole(group, own_slot(lands, wire[group]))

    sent = {}

    def grads_out(group, dws):
        col, row = tuple(shard[group])
        n_cols = N_DEV * shard[group][col].shape[1]
        pieces = [_col_pieces(dws[col][:, :n_cols]).astype(MXU_DTYPE),
                  dws[row].reshape((N_DEV,) + shard[group][row].shape).astype(MXU_DTYPE)]
        sems, srcs_thru, lands_thru, token = send_start(pieces, f"exchange_{group}_start", True, None)
        sent[group] = (sems, srcs_thru, lands_thru, pieces)
        return token

    sg = small_g.reshape(N_DEV, -1)
    o1 = d // N_DEV
    o2 = o1 + 3 * CONV_CH // N_DEV
    small = {
        "even_norm": even_norm, "even_q_gain": even_q_gain, "even_k_gain": even_k_gain, "even_sinks": even_sinks,
        "odd_a_log": odd_a_log.reshape(-1), "odd_dt_bias": odd_dt_bias.reshape(-1), "odd_o_gain": odd_o_gain,
        "ffn_norm0": ffn_norm[0:1], "ffn_norm1": ffn_norm[1:2],
        "odd_norm": sg[:, :o1].reshape(1, d),
        "even_conv_w": sg[:, o1:o2].reshape(N_DEV, 3, CONV_CH // N_DEV).transpose(1, 0, 2).reshape(3, CONV_CH),
        "odd_conv_w": sg[:, o2:o2 + 4 * _QKV_W // N_DEV].reshape(N_DEV, 4, _QKV_W // N_DEV).transpose(1, 0, 2).reshape(4, _QKV_W),
    }

    loss_row, grad_x, gs = local_step(x.reshape(t, d), loss_target.reshape(t, d), small, weights_of, grads_out, after=tok)

    rows = _pack_rows([gs[n] for n in _SMALL_ORDER] + [loss_row[:, 0:1]])
    (rows_g,) = all_gather([rows], "gather_small_grads")
    tot = sum_rows(rows_g, "sum_small_grads").reshape(-1)
    off, sgrad = 0, {}
    for n in _SMALL_ORDER:
        sgrad[n] = tot[off:off + _SMALL_SIZE[n]]
        off += _SMALL_SIZE[n]
    loss = tot[off]

    upd, behind = {}, tot
    for g in ("ffn1", "odd", "ffn0", "even"):
        sems, srcs_thru, lands_thru, pieces = sent[g]
        lands = send_wait(sems, srcs_thru, lands_thru, f"exchange_{g}_wait", True, behind)
        own = [lax.dynamic_index_in_dim(p, me, 0, keepdims=False) for p in pieces]
        for (key, w2), pcs in zip(shard[g].items(), own_slot(lands, own)):
            upd[g, key] = adam_sum(w2, pcs, mom_m[g][key].reshape(w2.shape), mom_v[g][key].reshape(w2.shape),
                                   f"adamw_{g}_{key}")
        behind = upd[g, key][0]
    res = {
        "even_w_in": tuple(o.reshape(even_w_in.shape) for o in upd["even", "w_in"]),
        "even_w_out": tuple(o.reshape(even_w_out.shape) for o in upd["even", "w_out"]),
        "odd_w_in": tuple(o.reshape(odd_w_in.shape) for o in upd["odd", "w_in"]),
        "odd_w_out": tuple(o.reshape(odd_w_out.shape) for o in upd["odd", "w_out"]),
        "ffn_w_gate_up": tuple(jnp.stack([a, b]) for a, b in zip(upd["ffn0", "gate_up"], upd["ffn1", "gate_up"])),
        "ffn_w_down": tuple(jnp.stack([a, b]) for a, b in zip(upd["ffn0", "down"], upd["ffn1", "down"])),
    }

    repl = _SMALL_ORDER[:_N_REPL]
    repl_w = {"even_norm": even_norm, "even_q_gain": even_q_gain, "even_k_gain": even_k_gain, "even_sinks": even_sinks,
              "odd_a_log": odd_a_log, "odd_dt_bias": odd_dt_bias, "odd_o_gain": odd_o_gain,
              "ffn_norm0": ffn_norm[0], "ffn_norm1": ffn_norm[1]}
    repl_m = {"even_norm": m_even_norm, "even_q_gain": m_even_q_gain, "even_k_gain": m_even_k_gain,
              "even_sinks": m_even_sinks, "odd_a_log": m_odd_a_log, "odd_dt_bias": m_odd_dt_bias,
              "odd_o_gain": m_odd_o_gain, "ffn_norm0": m_ffn_norm[0], "ffn_norm1": m_ffn_norm[1]}
    repl_v = {"even_norm": v_even_norm, "even_q_gain": v_even_q_gain, "even_k_gain": v_even_k_gain,
              "even_sinks": v_even_sinks, "odd_a_log": v_odd_a_log, "odd_dt_bias": v_odd_dt_bias,
              "odd_o_gain": v_odd_o_gain, "ffn_norm0": v_ffn_norm[0], "ffn_norm1": v_ffn_norm[1]}
    pk = lambda dct: _pack_rows([dct[n] for n in repl])
    pd_, pm_, pv_ = adam_small(pk(repl_w), pk(sgrad), pk(repl_m), pk(repl_v), "adamw_replicated")
    sres = {}
    off = 0
    for n in repl:
        sz = _SMALL_SIZE[n]
        sres[n] = (sgrad[n], pd_.reshape(-1)[off:off + sz], pm_.reshape(-1)[off:off + sz], pv_.reshape(-1)[off:off + sz])
        off += sz
    g_on = _my_block(sgrad["odd_norm"].reshape(1, d), d // N_DEV, 1)
    g_ec = _my_block(sgrad["even_conv_w"].reshape(3, CONV_CH), CONV_CH // N_DEV, 1)
    g_oc = _my_block(sgrad["odd_conv_w"].reshape(4, _QKV_W), _QKV_W // N_DEV, 1)
    shard_w = _pack_rows([odd_norm, even_conv_w, odd_conv_w])
    sd_, sm_, sv_ = adam_small(shard_w, _pack_rows([g_on, g_ec, g_oc]),
                               _pack_rows([m_odd_norm, m_even_conv_w, m_odd_conv_w]),
                               _pack_rows([v_odd_norm, v_even_conv_w, v_odd_conv_w]), "adamw_sharded_small")
    off = 0
    for n, gfull, like in (("odd_norm", g_on, odd_norm), ("even_conv_w", g_ec, even_conv_w), ("odd_conv_w", g_oc, odd_conv_w)):
        sz = like.size
        sres[n] = (gfull, sd_.reshape(-1)[off:off + sz], sm_.reshape(-1)[off:off + sz], sv_.reshape(-1)[off:off + sz])
        off += sz

    def small_out(name, like, kind):
        if name == "ffn_norm":
            return jnp.stack([sres["ffn_norm0"][kind], sres["ffn_norm1"][kind]]).reshape(like.shape)
        return sres[name][kind].reshape(like.shape)

    order = (("even_norm", even_norm), ("even_w_in", even_w_in), ("even_q_gain", even_q_gain),
             ("even_k_gain", even_k_gain), ("even_sinks", even_sinks), ("even_conv_w", even_conv_w),
             ("even_w_out", even_w_out), ("odd_norm", odd_norm), ("odd_w_in", odd_w_in), ("odd_conv_w", odd_conv_w),
             ("odd_a_log", odd_a_log), ("odd_dt_bias", odd_dt_bias), ("odd_o_gain", odd_o_gain),
             ("odd_w_out", odd_w_out), ("ffn_norm", ffn_norm), ("ffn_w_gate_up", ffn_w_gate_up),
             ("ffn_w_down", ffn_w_down))
    outs = [loss, grad_x.reshape(x.shape)]
    for kind in range(4):
        for name, like in order:
            outs.append(res[name][kind] if name in res else small_out(name, like, kind))
    return tuple(outs)
```

```python
import functools

import jax
import jax.numpy as jnp
from jax import lax
from jax.experimental import pallas as pl
from jax.experimental.pallas import tpu as pltpu

F32 = jnp.float32
MXU_DTYPE = jnp.bfloat16
HI = lax.Precision.HIGH
EPS = 1e-6
N_DEV = 8
D_MODEL = 1024
HEAD_DIM = 64
ATTN_HEADS = 8
KV_HEADS = 2
ATTN_BLOCK = 128
Q_W = 512
KV_W = 128
CONV_CH = 512
EVEN_IN_W = 2304
DN_HEADS = 8
DN_DIM = 128
DN_W = 1024
DN_CHUNK = 64
ODD_IN_W = 4112
ODD_IN_PAD = 4224
D_FF = 2816
NEG = -1e30
VMEM_LIMIT = 56 * 1024 * 1024
ADAM_LR, ADAM_B1, ADAM_B2, ADAM_EPS, ADAM_WD, ADAM_STEP = 0.001, 0.9, 0.999, 1e-08, 0.01, 10
MESH = pl.DeviceIdType.MESH


def _cp(*sem):
    return pltpu.CompilerParams(dimension_semantics=sem, vmem_limit_bytes=VMEM_LIMIT)


def _pick(n, cap):
    best = 128
    for t in range(128, cap + 1, 128):
        if n % t == 0:
            best = t
    return best


def _mx(a, b):
    return jnp.dot(a.astype(MXU_DTYPE), b.astype(MXU_DTYPE), preferred_element_type=F32)


def _mx_nt(a, b):
    return lax.dot_general(a.astype(MXU_DTYPE), b.astype(MXU_DTYPE), (((1,), (1,)), ((), ())),
                           preferred_element_type=F32)


def _mx_tn(a, b):
    return lax.dot_general(a.astype(MXU_DTYPE), b.astype(MXU_DTYPE), (((0,), (0,)), ((), ())),
                           preferred_element_type=F32)


def _hi(a, b):
    return jnp.dot(a, b, precision=HI, preferred_element_type=F32)


def _hi_nt(a, b):
    return lax.dot_general(a, b, (((1,), (1,)), ((), ())), precision=HI, preferred_element_type=F32)


def _hi_tn(a, b):
    return lax.dot_general(a, b, (((0,), (0,)), ((), ())), precision=HI, preferred_element_type=F32)


def _sigmoid(x):
    return 1.0 / (1.0 + jnp.exp(-x))


def _softplus(x):
    return jnp.maximum(x, 0.0) + jnp.log(1.0 + jnp.exp(-jnp.abs(x)))


def mm_nn(a, b, name, res=None, out_dtype=F32, tm=1024):
    m, k = a.shape
    _, n = b.shape
    tn = _pick(n, 1536)
    tm = min(tm, m)

    def body(*refs):
        a_ref, b_ref = refs[0], refs[1]
        o_ref = refs[-1]
        acc = _mx(a_ref[...], b_ref[...])
        if res is not None:
            acc = acc + refs[2][...]
        o_ref[...] = acc.astype(o_ref.dtype)

    in_specs = [pl.BlockSpec((tm, k), lambda j, i: (i, 0)), pl.BlockSpec((k, tn), lambda j, i: (0, j))]
    args = [a, b]
    if res is not None:
        in_specs.append(pl.BlockSpec((tm, tn), lambda j, i: (i, j)))
        args.append(res)
    return pl.pallas_call(
        body, name=name, grid=(n // tn, m // tm), in_specs=in_specs,
        out_specs=pl.BlockSpec((tm, tn), lambda j, i: (i, j)),
        out_shape=jax.ShapeDtypeStruct((m, n), out_dtype), compiler_params=_cp("parallel", "parallel"))(*args)


def mm_nt(a, b, name, out_dtype=F32, tm=1024, after=None):
    m, k = a.shape
    n, _ = b.shape
    tn = _pick(n, 512 if k > 3000 else 1536)
    tm = min(tm, m)

    def body(a_ref, b_ref, *rest):
        o_ref = rest[-1]
        o_ref[...] = _mx_nt(a_ref[...], b_ref[...]).astype(o_ref.dtype)

    in_specs = [pl.BlockSpec((tm, k), lambda j, i: (i, 0)), pl.BlockSpec((tn, k), lambda j, i: (j, 0))]
    args = [a, b]
    if after is not None:
        in_specs.append(pl.BlockSpec(memory_space=pl.ANY))
        args.append(after)
    return pl.pallas_call(
        body, name=name, grid=(n // tn, m // tm), in_specs=in_specs,
        out_specs=pl.BlockSpec((tm, tn), lambda j, i: (i, j)),
        out_shape=jax.ShapeDtypeStruct((m, n), out_dtype), compiler_params=_cp("parallel", "parallel"))(*args)


def mm_at(at, b, name, tk=1024):
    m, kk = at.shape
    _, n = b.shape
    tm, tn, tk = _pick(m, 1408), _pick(n, 1408), min(tk, kk)

    def body(a_ref, b_ref, o_ref):
        p = _mx(a_ref[...], b_ref[...])
        o_ref[...] = jnp.where(pl.program_id(2) == 0, p, o_ref[...] + p)

    return pl.pallas_call(
        body, name=name, grid=(m // tm, n // tn, kk // tk),
        in_specs=[pl.BlockSpec((tm, tk), lambda i, j, k: (i, k)), pl.BlockSpec((tk, tn), lambda i, j, k: (k, j))],
        out_specs=pl.BlockSpec((tm, tn), lambda i, j, k: (i, j)),
        out_shape=jax.ShapeDtypeStruct((m, n), F32), compiler_params=_cp("parallel", "parallel", "arbitrary"))(at, b)


def rms_fwd(x, g, name, tm=512, after=None):
    t, d = x.shape

    def body(x_ref, g_ref, *rest):
        o_ref, ot_ref = rest[-2:]
        xv = x_ref[...]
        r = lax.rsqrt(jnp.mean(xv * xv, axis=-1, keepdims=True) + EPS)
        h = xv * r * g_ref[...]
        o_ref[...] = h.astype(o_ref.dtype)
        ot_ref[...] = h.T.astype(ot_ref.dtype)

    in_specs = [pl.BlockSpec((tm, d), lambda i: (i, 0)), pl.BlockSpec((1, d), lambda i: (0, 0))]
    args = [x, g]
    if after is not None:
        in_specs.append(pl.BlockSpec(memory_space=pl.ANY))
        args.append(after)
    return pl.pallas_call(
        body, name=name, grid=(t // tm,), in_specs=in_specs,
        out_specs=(pl.BlockSpec((tm, d), lambda i: (i, 0)), pl.BlockSpec((d, tm), lambda i: (0, i))),
        out_shape=(jax.ShapeDtypeStruct((t, d), MXU_DTYPE), jax.ShapeDtypeStruct((d, t), MXU_DTYPE)),
        compiler_params=_cp("parallel"))(*args)


def rms_bwd(x, g, dh, dres, name, tm=512):
    t, d = x.shape

    def body(x_ref, g_ref, dh_ref, dres_ref, dx_ref, dg_ref):
        i = pl.program_id(0)
        xv = x_ref[...]
        r = lax.rsqrt(jnp.mean(xv * xv, axis=-1, keepdims=True) + EPS)
        xh = xv * r
        dhv = dh_ref[...]
        dxh = dhv * g_ref[...]
        dx_ref[...] = dres_ref[...] + r * (dxh - xh * jnp.mean(dxh * xh, axis=-1, keepdims=True))
        part = jnp.sum(dhv * xh, axis=0, keepdims=True)

        @pl.when(i == 0)
        def _():
            dg_ref[...] = part

        @pl.when(i > 0)
        def _():
            dg_ref[...] += part

    row = pl.BlockSpec((tm, d), lambda i: (i, 0))
    one = pl.BlockSpec((1, d), lambda i: (0, 0))
    return pl.pallas_call(
        body, name=name, grid=(t // tm,), in_specs=[row, one, row, row], out_specs=(row, one),
        out_shape=(jax.ShapeDtypeStruct((t, d), F32), jax.ShapeDtypeStruct((1, d), F32)),
        compiler_params=_cp("arbitrary"))(x, g, dh, dres)


GU_TILE = 1408


def _gu_tiled(w):
    k = w.shape[0]
    return w.reshape(k, 2, D_FF // GU_TILE, GU_TILE).transpose(0, 2, 1, 3).reshape(k, 2 * D_FF)


def _gu_untiled(w):
    k = w.shape[0]
    return w.reshape(k, D_FF // GU_TILE, 2, GU_TILE).transpose(0, 2, 1, 3).reshape(k, 2 * D_FF)


def ffn_up(f, w, name, tm=512):
    t, d = f.shape

    def body(f_ref, w_ref, gu_ref, a_ref, at_ref):
        gu = _mx(f_ref[...], w_ref[...])
        gu_ref[...] = gu
        g, u = gu[:, :GU_TILE], gu[:, GU_TILE:]
        act = g * _sigmoid(g) * u
        a_ref[...] = act.astype(a_ref.dtype)
        at_ref[...] = act.T.astype(at_ref.dtype)

    return pl.pallas_call(
        body, name=name, grid=(D_FF // GU_TILE, t // tm),
        in_specs=[pl.BlockSpec((tm, d), lambda j, i: (i, 0)), pl.BlockSpec((d, 2 * GU_TILE), lambda j, i: (0, j))],
        out_specs=(pl.BlockSpec((tm, 2 * GU_TILE), lambda j, i: (i, j)), pl.BlockSpec((tm, GU_TILE), lambda j, i: (i, j)),
                   pl.BlockSpec((GU_TILE, tm), lambda j, i: (j, i))),
        out_shape=(jax.ShapeDtypeStruct((t, 2 * D_FF), F32), jax.ShapeDtypeStruct((t, D_FF), MXU_DTYPE),
                   jax.ShapeDtypeStruct((D_FF, t), MXU_DTYPE)),
        compiler_params=_cp("parallel", "parallel"))(f, w)


def ffn_dact(dy, w_d, gu, name, tm=512, after=None):
    t, d = dy.shape

    def body(dy_ref, w_ref, gu_ref, *rest):
        o_ref = rest[-1]
        da = _mx_nt(dy_ref[...], w_ref[...])
        g, u = gu_ref[:, :GU_TILE], gu_ref[:, GU_TILE:]
        sg = _sigmoid(g)
        o_ref[:, :GU_TILE] = (da * u * sg * (1.0 + g * (1.0 - sg))).astype(o_ref.dtype)
        o_ref[:, GU_TILE:] = (da * g * sg).astype(o_ref.dtype)

    in_specs = [pl.BlockSpec((tm, d), lambda j, i: (i, 0)), pl.BlockSpec((GU_TILE, d), lambda j, i: (j, 0)),
                pl.BlockSpec((tm, 2 * GU_TILE), lambda j, i: (i, j))]
    args = [dy, w_d, gu]
    if after is not None:
        in_specs.append(pl.BlockSpec(memory_space=pl.ANY))
        args.append(after)
    return pl.pallas_call(
        body, name=name, grid=(D_FF // GU_TILE, t // tm), in_specs=in_specs,
        out_specs=pl.BlockSpec((tm, 2 * GU_TILE), lambda j, i: (i, j)),
        out_shape=jax.ShapeDtypeStruct((t, 2 * D_FF), MXU_DTYPE), compiler_params=_cp("parallel", "parallel"))(*args)


def loss_head(y, target, name, tm=512):
    t, d = y.shape

    def body(y_ref, t_ref, dy_ref, l_ref):
        i = pl.program_id(0)
        e = y_ref[...] - t_ref[...]
        dy_ref[...] = e * (1.0 / d)
        part = jnp.zeros((1, 128), F32) + 0.5 * jnp.sum(jnp.mean(e * e, axis=-1, keepdims=True), axis=0, keepdims=True)

        @pl.when(i == 0)
        def _():
            l_ref[...] = part

        @pl.when(i > 0)
        def _():
            l_ref[...] += part

    row = pl.BlockSpec((tm, d), lambda i: (i, 0))
    return pl.pallas_call(
        body, name=name, grid=(t // tm,), in_specs=[row, row],
        out_specs=(row, pl.BlockSpec((1, 128), lambda i: (0, 0))),
        out_shape=(jax.ShapeDtypeStruct((t, d), F32), jax.ShapeDtypeStruct((1, 128), F32)),
        compiler_params=_cp("arbitrary"))(y, target)


def _rot(x):
    return jnp.concatenate([-x[:, HEAD_DIM // 2:], x[:, :HEAD_DIM // 2]], axis=-1)


def _rot_t(y):
    return jnp.concatenate([y[:, HEAD_DIM // 2:], -y[:, :HEAD_DIM // 2]], axis=-1)


def qk_prep_fwd(proj, q_gain, k_gain, cosf, sinf, name, tm=256):
    t = proj.shape[0]
    nh = ATTN_HEADS + KV_HEADS

    def body(p_ref, qg_ref, kg_ref, c_ref, s_ref, q_ref, k_ref):
        c, s = c_ref[...], s_ref[...]
        outs = []
        for h in range(nh):
            xh = p_ref[:, h * HEAD_DIM:(h + 1) * HEAD_DIM]
            gain = qg_ref[...] if h < ATTN_HEADS else kg_ref[...]
            r = lax.rsqrt(jnp.mean(xh * xh, axis=-1, keepdims=True) + EPS)
            xn = xh * r * gain
            outs.append(xn * c + _rot(xn) * s)
        q_ref[...] = jnp.concatenate(outs[:ATTN_HEADS], axis=-1)
        k_ref[...] = jnp.concatenate(outs[ATTN_HEADS:], axis=-1)

    gspec = pl.BlockSpec((1, HEAD_DIM), lambda i: (0, 0))
    tspec = pl.BlockSpec((tm, HEAD_DIM), lambda i: (i, 0))
    return pl.pallas_call(
        body, name=name, grid=(t // tm,),
        in_specs=[pl.BlockSpec((tm, Q_W + KV_W), lambda i: (i, 0)), gspec, gspec, tspec, tspec],
        out_specs=(pl.BlockSpec((tm, Q_W), lambda i: (i, 0)), pl.BlockSpec((tm, KV_W), lambda i: (i, 0))),
        out_shape=(jax.ShapeDtypeStruct((t, Q_W), F32), jax.ShapeDtypeStruct((t, KV_W), F32)),
        compiler_params=_cp("parallel"))(proj, q_gain, k_gain, cosf, sinf)


def qk_prep_bwd(proj, q_gain, k_gain, cosf, sinf, dq, dk, name, tm=256):
    t = proj.shape[0]
    nh = ATTN_HEADS + KV_HEADS

    def body(p_ref, qg_ref, kg_ref, c_ref, s_ref, dq_ref, dk_ref, o_ref, dqg_ref, dkg_ref):
        i = pl.program_id(0)
        c, s = c_ref[...], s_ref[...]
        outs = []
        dqg = jnp.zeros((1, HEAD_DIM), F32)
        dkg = jnp.zeros((1, HEAD_DIM), F32)
        for h in range(nh):
            xh = p_ref[:, h * HEAD_DIM:(h + 1) * HEAD_DIM]
            if h < ATTN_HEADS:
                gain = qg_ref[...]
                dout = dq_ref[:, h * HEAD_DIM:(h + 1) * HEAD_DIM]
            else:
                gain = kg_ref[...]
                dout = dk_ref[:, (h - ATTN_HEADS) * HEAD_DIM:(h - ATTN_HEADS + 1) * HEAD_DIM]
            r = lax.rsqrt(jnp.mean(xh * xh, axis=-1, keepdims=True) + EPS)
            xhat = xh * r
            dxn = dout * c + _rot_t(dout * s)
            part = jnp.sum(dxn * xhat, axis=0, keepdims=True)
            if h < ATTN_HEADS:
                dqg = dqg + part
            else:
                dkg = dkg + part
            dxh = dxn * gain
            outs.append(r * (dxh - xhat * jnp.mean(dxh * xhat, axis=-1, keepdims=True)))
        o_ref[...] = jnp.concatenate(outs, axis=-1).astype(o_ref.dtype)

        @pl.when(i == 0)
        def _():
            dqg_ref[...] = dqg
            dkg_ref[...] = dkg

        @pl.when(i > 0)
        def _():
            dqg_ref[...] += dqg
            dkg_ref[...] += dkg

    gspec = pl.BlockSpec((1, HEAD_DIM), lambda i: (0, 0))
    tspec = pl.BlockSpec((tm, HEAD_DIM), lambda i: (i, 0))
    return pl.pallas_call(
        body, name=name, grid=(t // tm,),
        in_specs=[pl.BlockSpec((tm, Q_W + KV_W), lambda i: (i, 0)), gspec, gspec, tspec, tspec,
                  pl.BlockSpec((tm, Q_W), lambda i: (i, 0)), pl.BlockSpec((tm, KV_W), lambda i: (i, 0))],
        out_specs=(pl.BlockSpec((tm, Q_W + KV_W), lambda i: (i, 0)), gspec, gspec),
        out_shape=(jax.ShapeDtypeStruct((t, Q_W + KV_W), MXU_DTYPE), jax.ShapeDtypeStruct((1, HEAD_DIM), F32),
                   jax.ShapeDtypeStruct((1, HEAD_DIM), F32)),
        compiler_params=_cp("arbitrary"))(proj, q_gain, k_gain, cosf, sinf, dq, dk)


def _swa_valid(n, grp):
    qi = lax.broadcasted_iota(jnp.int32, (grp * ATTN_BLOCK, 2 * ATTN_BLOCK), 0) & (ATTN_BLOCK - 1)
    kj = lax.broadcasted_iota(jnp.int32, (grp * ATTN_BLOCK, 2 * ATTN_BLOCK), 1)
    diff = qi + ATTN_BLOCK - kj
    return (diff >= 0) & (diff < ATTN_BLOCK) & (n * ATTN_BLOCK - ATTN_BLOCK + kj >= 0)


def _stack_heads(ref, g, grp):
    return jnp.concatenate([ref[:, (g * grp + j) * HEAD_DIM:(g * grp + j + 1) * HEAD_DIM] for j in range(grp)], axis=0)


def _stack_sinks(s_ref, g, grp):
    return jnp.concatenate([jnp.zeros((ATTN_BLOCK, 1), F32) + s_ref[0:1, g * grp + j:g * grp + j + 1]
                            for j in range(grp)], axis=0)


def swa_fwd(q, k, proj, sinks, name):
    t = q.shape[0]
    nb = t // ATTN_BLOCK
    scale = HEAD_DIM ** -0.5
    grp = ATTN_HEADS // KV_HEADS

    def body(q_ref, kc_ref, kp_ref, vc_ref, vp_ref, s_ref, y_ref, lse_ref):
        n = pl.program_id(0)
        valid = _swa_valid(n, grp)
        kk = jnp.concatenate([kp_ref[...], kc_ref[...]], axis=0).astype(MXU_DTYPE)
        vv = jnp.concatenate([vp_ref[...], vc_ref[...]], axis=0).astype(MXU_DTYPE)
        lane = lax.broadcasted_iota(jnp.int32, (ATTN_BLOCK, ATTN_HEADS), 1)
        gs = range(KV_HEADS)
        qg = [_stack_heads(q_ref, g, grp) for g in gs]
        sink = [_stack_sinks(s_ref, g, grp) for g in gs]
        sc = [jnp.where(valid, _mx_nt(qg[g], kk[:, g * HEAD_DIM:(g + 1) * HEAD_DIM]) * scale, NEG) for g in gs]
        m = [jnp.maximum(jnp.max(sc[g], axis=-1, keepdims=True), sink[g]) for g in gs]
        e = [jnp.exp(sc[g] - m[g]) for g in gs]
        den = [jnp.sum(e[g], axis=-1, keepdims=True) + jnp.exp(sink[g] - m[g]) for g in gs]
        og = [_mx(e[g] / den[g], vv[:, g * HEAD_DIM:(g + 1) * HEAD_DIM]) for g in gs]
        lg = [m[g] + jnp.log(den[g]) for g in gs]
        lse = jnp.zeros((ATTN_BLOCK, ATTN_HEADS), F32)
        outs = []
        for h in range(ATTN_HEADS):
            rows = slice((h % grp) * ATTN_BLOCK, (h % grp + 1) * ATTN_BLOCK)
            outs.append(og[h // grp][rows])
            lse = jnp.where(lane == h, lg[h // grp][rows], lse)
        y_ref[...] = jnp.concatenate(outs, axis=-1)
        lse_ref[...] = lse

    cur = lambda n: (n, 0)
    prev = lambda n: (jnp.maximum(n - 1, 0), 0)
    vcol = (Q_W + KV_W) // KV_W
    return pl.pallas_call(
        body, name=name, grid=(nb,),
        in_specs=[pl.BlockSpec((ATTN_BLOCK, Q_W), cur), pl.BlockSpec((ATTN_BLOCK, KV_W), cur),
                  pl.BlockSpec((ATTN_BLOCK, KV_W), prev),
                  pl.BlockSpec((ATTN_BLOCK, KV_W), lambda n: (n, vcol)),
                  pl.BlockSpec((ATTN_BLOCK, KV_W), lambda n: (jnp.maximum(n - 1, 0), vcol)),
                  pl.BlockSpec((1, ATTN_HEADS), lambda n: (0, 0))],
        out_specs=(pl.BlockSpec((ATTN_BLOCK, Q_W), cur), pl.BlockSpec((ATTN_BLOCK, ATTN_HEADS), cur)),
        out_shape=(jax.ShapeDtypeStruct((t, Q_W), F32), jax.ShapeDtypeStruct((t, ATTN_HEADS), F32)),
        compiler_params=_cp("parallel"))(q, k, k, proj, proj, sinks)


def swa_bwd(q, k, proj, sinks, y, lse, dmix, name):
    t = q.shape[0]
    nb = t // ATTN_BLOCK
    scale = HEAD_DIM ** -0.5
    grp = ATTN_HEADS // KV_HEADS

    def body(q_ref, kc_ref, kp_ref, vc_ref, vp_ref, s_ref, y_ref, lse_ref, dy_ref,
             dq_ref, dk_ref, dv_ref, ds_ref, dkc, dvc):
        n = pl.program_id(0)

        @pl.when(n == 0)
        def _():
            dkc[...] = jnp.zeros_like(dkc)
            dvc[...] = jnp.zeros_like(dvc)
            ds_ref[...] = jnp.zeros_like(ds_ref)

        @pl.when(n < nb)
        def _():
            valid = _swa_valid(n, grp)
            kk = jnp.concatenate([kp_ref[...], kc_ref[...]], axis=0).astype(MXU_DTYPE)
            vv = jnp.concatenate([vp_ref[...], vc_ref[...]], axis=0).astype(MXU_DTYPE)
            lane = lax.broadcasted_iota(jnp.int32, (1, ATTN_HEADS), 1)
            gs = range(KV_HEADS)
            kg = [kk[:, g * HEAD_DIM:(g + 1) * HEAD_DIM] for g in gs]
            vg = [vv[:, g * HEAD_DIM:(g + 1) * HEAD_DIM] for g in gs]
            qg = [_stack_heads(q_ref, g, grp).astype(MXU_DTYPE) for g in gs]
            dog = [_stack_heads(dy_ref, g, grp) for g in gs]
            og = [_stack_heads(y_ref, g, grp) for g in gs]
            lg = [jnp.concatenate([lse_ref[:, g * grp + j:g * grp + j + 1] for j in range(grp)], axis=0) for g in gs]
            sink = [_stack_sinks(s_ref, g, grp) for g in gs]
            sc = [jnp.where(valid, _mx_nt(qg[g], kg[g]) * scale, NEG) for g in gs]
            p = [jnp.exp(sc[g] - lg[g]) for g in gs]
            delta = [jnp.sum(dog[g] * og[g], axis=-1, keepdims=True) for g in gs]
            ds = [p[g] * (_mx_nt(dog[g], vg[g]) - delta[g]) for g in gs]
            dqg = [_mx(ds[g], kg[g]) * scale for g in gs]
            dkf = jnp.concatenate([_mx_tn(ds[g], qg[g]) * scale for g in gs], axis=-1)
            dvf = jnp.concatenate([_mx_tn(p[g], dog[g]) for g in gs], axis=-1)
            dsk = [jnp.exp(sink[g] - lg[g]) * delta[g] for g in gs]
            dsink = jnp.zeros((1, ATTN_HEADS), F32)
            dqs = []
            for h in range(ATTN_HEADS):
                rows = slice((h % grp) * ATTN_BLOCK, (h % grp + 1) * ATTN_BLOCK)
                dqs.append(dqg[h // grp][rows])
                dsink = jnp.where(lane == h, -jnp.sum(dsk[h // grp][rows], axis=0, keepdims=True), dsink)
            dq_ref[...] = jnp.concatenate(dqs, axis=-1)
            dk_ref[...] = dkc[...] + dkf[:ATTN_BLOCK]
            dv_ref[...] = (dvc[...] + dvf[:ATTN_BLOCK]).astype(dv_ref.dtype)
            dkc[...] = dkf[ATTN_BLOCK:]
            dvc[...] = dvf[ATTN_BLOCK:]
            ds_ref[...] += dsink

        @pl.when(n == nb)
        def _():
            dk_ref[...] = dkc[...]
            dv_ref[...] = dvc[...].astype(dv_ref.dtype)

    cur = lambda n: (jnp.minimum(n, nb - 1), 0)
    prev = lambda n: (jnp.clip(n - 1, 0, nb - 1), 0)
    vcol = (Q_W + KV_W) // KV_W
    return pl.pallas_call(
        body, name=name, grid=(nb + 1,),
        in_specs=[pl.BlockSpec((ATTN_BLOCK, Q_W), cur), pl.BlockSpec((ATTN_BLOCK, KV_W), cur),
                  pl.BlockSpec((ATTN_BLOCK, KV_W), prev),
                  pl.BlockSpec((ATTN_BLOCK, KV_W), lambda n: (jnp.minimum(n, nb - 1), vcol)),
                  pl.BlockSpec((ATTN_BLOCK, KV_W), lambda n: (jnp.clip(n - 1, 0, nb - 1), vcol)),
                  pl.BlockSpec((1, ATTN_HEADS), lambda n: (0, 0)),
                  pl.BlockSpec((ATTN_BLOCK, Q_W), cur), pl.BlockSpec((ATTN_BLOCK, ATTN_HEADS), cur),
                  pl.BlockSpec((ATTN_BLOCK, Q_W), cur)],
        out_specs=(pl.BlockSpec((ATTN_BLOCK, Q_W), cur), pl.BlockSpec((ATTN_BLOCK, KV_W), prev),
                   pl.BlockSpec((ATTN_BLOCK, KV_W), prev), pl.BlockSpec((1, ATTN_HEADS), lambda n: (0, 0))),
        out_shape=(jax.ShapeDtypeStruct((t, Q_W), F32), jax.ShapeDtypeStruct((t, KV_W), F32),
                   jax.ShapeDtypeStruct((t, KV_W), MXU_DTYPE), jax.ShapeDtypeStruct((1, ATTN_HEADS), F32)),
        scratch_shapes=[pltpu.VMEM((ATTN_BLOCK, KV_W), F32), pltpu.VMEM((ATTN_BLOCK, KV_W), F32)],
        compiler_params=_cp("arbitrary"))(q, k, k, proj, proj, sinks, y, lse, dmix)


GC_W = 256
_GB0, _GC0, _XI0 = 768 // GC_W, 1280 // GC_W, 1792 // GC_W
HALO = 8


def gconv_fwd(proj, conv_w, name, tm=512):
    t = proj.shape[0]
    hb = tm // HALO

    def body(gb_ref, gc_ref, xi_ref, gch_ref, xih_ref, w_ref, y_ref):
        i = pl.program_id(1)
        u = gc_ref[...] * xi_ref[...]
        uh = jnp.where(i == 0, 0.0, gch_ref[...] * xih_ref[...])
        up = jnp.concatenate([uh, u], axis=0)
        cv = w_ref[0:1, :] * up[HALO - 2:HALO - 2 + tm]
        cv = cv + w_ref[1:2, :] * up[HALO - 1:HALO - 1 + tm]
        cv = cv + w_ref[2:3, :] * u
        y_ref[...] = (gb_ref[...] * cv).astype(y_ref.dtype)

    def col(c0):
        return pl.BlockSpec((tm, GC_W), lambda cj, i: (i, c0 + cj))

    def halo(c0):
        return pl.BlockSpec((HALO, GC_W), lambda cj, i: (jnp.maximum(i * hb - 1, 0), c0 + cj))

    return pl.pallas_call(
        body, name=name, grid=(CONV_CH // GC_W, t // tm),
        in_specs=[col(_GB0), col(_GC0), col(_XI0), halo(_GC0), halo(_XI0),
                  pl.BlockSpec((3, GC_W), lambda cj, i: (0, cj))],
        out_specs=pl.BlockSpec((tm, GC_W), lambda cj, i: (i, cj)),
        out_shape=jax.ShapeDtypeStruct((t, CONV_CH), MXU_DTYPE),
        compiler_params=_cp("parallel", "parallel"))(proj, proj, proj, proj, proj, conv_w)


def gconv_bwd(proj, conv_w, dmix, name, tm=512):
    t = proj.shape[0]
    hb = tm // HALO
    nt = t // tm
    dy0 = Q_W // GC_W

    def body(gb_ref, gc_ref, xi_ref, gch_ref, xih_ref, gbn_ref, dyn_ref, dy_ref, w_ref,
             dgb_ref, dgc_ref, dxi_ref, dw_ref):
        i = pl.program_id(1)
        gc, xi, gb, dy = gc_ref[...], xi_ref[...], gb_ref[...], dy_ref[...]
        u = gc * xi
        uh = jnp.where(i == 0, 0.0, gch_ref[...] * xih_ref[...])
        up = jnp.concatenate([uh, u], axis=0)
        u2 = up[HALO - 2:HALO - 2 + tm]
        u1 = up[HALO - 1:HALO - 1 + tm]
        cv = w_ref[0:1, :] * u2 + w_ref[1:2, :] * u1 + w_ref[2:3, :] * u
        dgb_ref[...] = (dy * cv).astype(dgb_ref.dtype)
        dcv = dy * gb
        dcvn = jnp.where(i == nt - 1, 0.0, dyn_ref[...] * gbn_ref[...])
        dcvp = jnp.concatenate([dcv, dcvn], axis=0)
        du = w_ref[0:1, :] * dcvp[2:2 + tm] + w_ref[1:2, :] * dcvp[1:1 + tm] + w_ref[2:3, :] * dcv
        dgc_ref[...] = (du * xi).astype(dgc_ref.dtype)
        dxi_ref[...] = (du * gc).astype(dxi_ref.dtype)
        dw = jnp.concatenate([jnp.sum(dcv * u2, axis=0, keepdims=True), jnp.sum(dcv * u1, axis=0, keepdims=True),
                              jnp.sum(dcv * u, axis=0, keepdims=True)], axis=0)

        @pl.when(i == 0)
        def _():
            dw_ref[...] = dw

        @pl.when(i > 0)
        def _():
            dw_ref[...] += dw

    def col(c0):
        return pl.BlockSpec((tm, GC_W), lambda cj, i: (i, c0 + cj))

    def halo(c0):
        return pl.BlockSpec((HALO, GC_W), lambda cj, i: (jnp.maximum(i * hb - 1, 0), c0 + cj))

    def nxt(c0):
        return pl.BlockSpec((HALO, GC_W), lambda cj, i: (jnp.minimum((i + 1) * hb, t // HALO - 1), c0 + cj))

    out = pl.BlockSpec((tm, GC_W), lambda cj, i: (i, cj))
    return pl.pallas_call(
        body, name=name, grid=(CONV_CH // GC_W, nt),
        in_specs=[col(_GB0), col(_GC0), col(_XI0), halo(_GC0), halo(_XI0), nxt(_GB0), nxt(dy0), col(dy0),
                  pl.BlockSpec((3, GC_W), lambda cj, i: (0, cj))],
        out_specs=(out, out, out, pl.BlockSpec((3, GC_W), lambda cj, i: (0, cj))),
        out_shape=(jax.ShapeDtypeStruct((t, CONV_CH), MXU_DTYPE),) * 3 + (jax.ShapeDtypeStruct((3, CONV_CH), F32),),
        compiler_params=_cp("parallel", "arbitrary"))(proj, proj, proj, proj, proj, proj, dmix, dmix, conv_w)


_QKV_W = 3 * DN_W
_BA_COL = (4 * DN_W) // 128
_Z_COL = _QKV_W // DN_W


def gdn_prep_fwd(proj, conv_w, alog_row, dtb_row, name, tm=256):
    t = proj.shape[0]
    hb = tm // HALO
    qscale = DN_DIM ** -0.5

    def body(x_ref, xh_ref, w_ref, ba_ref, al_ref, dt_ref, q_ref, k_ref, v_ref, bg_ref):
        i = pl.program_id(0)
        for gi in range(3 * DN_HEADS):
            sl = slice(gi * DN_DIM, (gi + 1) * DN_DIM)
            xp = jnp.concatenate([jnp.where(i == 0, 0.0, xh_ref[:, sl]), x_ref[:, sl]], axis=0)
            c = w_ref[0:1, sl] * xp[HALO - 3:HALO - 3 + tm]
            for j in range(1, 4):
                c = c + w_ref[j:j + 1, sl] * xp[HALO - 3 + j:HALO - 3 + j + tm]
            s = c * _sigmoid(c)
            osl = slice((gi % DN_HEADS) * DN_DIM, (gi % DN_HEADS + 1) * DN_DIM)
            if gi < DN_HEADS:
                q_ref[:, osl] = s * lax.rsqrt(jnp.sum(s * s, axis=-1, keepdims=True) + EPS) * qscale
            elif gi < 2 * DN_HEADS:
                k_ref[:, osl] = s * lax.rsqrt(jnp.sum(s * s, axis=-1, keepdims=True) + EPS)
            else:
                v_ref[:, osl] = s
        ba = ba_ref[...]
        lane = lax.broadcasted_iota(jnp.int32, ba.shape, 1)
        gval = -jnp.exp(al_ref[...]) * _softplus(ba + dt_ref[...])
        bg_ref[...] = jnp.where(lane < DN_HEADS, _sigmoid(ba), jnp.where(lane < 2 * DN_HEADS, gval, 0.0))

    row = pl.BlockSpec((tm, DN_W), lambda i: (i, 0))
    one = pl.BlockSpec((1, 128), lambda i: (0, 0))
    return pl.pallas_call(
        body, name=name, grid=(t // tm,),
        in_specs=[pl.BlockSpec((tm, _QKV_W), lambda i: (i, 0)),
                  pl.BlockSpec((HALO, _QKV_W), lambda i: (jnp.maximum(i * hb - 1, 0), 0)),
                  pl.BlockSpec((4, _QKV_W), lambda i: (0, 0)),
                  pl.BlockSpec((tm, 128), lambda i: (i, _BA_COL)), one, one],
        out_specs=(row, row, row, pl.BlockSpec((tm, 128), lambda i: (i, 0))),
        out_shape=(jax.ShapeDtypeStruct((t, DN_W), F32),) * 3 + (jax.ShapeDtypeStruct((t, 128), F32),),
        compiler_params=_cp("parallel"))(proj, proj, conv_w, proj, alog_row, dtb_row)


def gdn_prep_bwd(proj, conv_w, alog_row, dtb_row, dq, dk, dv, dbg, name, tm=256):
    t = proj.shape[0]
    hb = tm // HALO
    nt = t // tm
    qscale = DN_DIM ** -0.5
    te = tm + HALO

    def body(x_ref, xh_ref, xn_ref, w_ref, ba_ref, al_ref, dt_ref, dq_ref, dk_ref, dv_ref,
             dqn_ref, dkn_ref, dvn_ref, dbg_ref, dx_ref, dba_ref, dw_ref, ddt_ref, dal_ref):
        i = pl.program_id(0)
        first = i == 0
        last = i == nt - 1
        dws = []
        for gi in range(3 * DN_HEADS):
            sl = slice(gi * DN_DIM, (gi + 1) * DN_DIM)
            osl = slice((gi % DN_HEADS) * DN_DIM, (gi % DN_HEADS + 1) * DN_DIM)
            xe = jnp.concatenate([jnp.where(first, 0.0, xh_ref[:, sl]), x_ref[:, sl], xn_ref[:, sl]], axis=0)
            c = w_ref[0:1, sl] * xe[HALO - 3:HALO - 3 + te]
            for j in range(1, 4):
                c = c + w_ref[j:j + 1, sl] * xe[HALO - 3 + j:HALO - 3 + j + te]
            sg = _sigmoid(c)
            s = c * sg
            d_ref, dn_ref = ((dq_ref, dqn_ref), (dk_ref, dkn_ref), (dv_ref, dvn_ref))[gi // DN_HEADS]
            dy = jnp.concatenate([d_ref[:, osl], jnp.where(last, 0.0, dn_ref[:, osl])], axis=0)
            if gi < 2 * DN_HEADS:
                r = lax.rsqrt(jnp.sum(s * s, axis=-1, keepdims=True) + EPS)
                sh = s * r
                ds = r * (dy - sh * jnp.sum(sh * dy, axis=-1, keepdims=True))
                if gi < DN_HEADS:
                    ds = ds * qscale
            else:
                ds = dy
            dc = ds * sg * (1.0 + c * (1.0 - sg))
            dx = w_ref[0:1, sl] * dc[3:3 + tm]
            for j in range(1, 4):
                dx = dx + w_ref[j:j + 1, sl] * dc[3 - j:3 - j + tm]
            dx_ref[:, sl] = dx.astype(dx_ref.dtype)
            dc0 = dc[:tm]
            dws.append(jnp.concatenate(
                [jnp.sum(dc0 * xe[HALO - 3 + j:HALO - 3 + j + tm], axis=0, keepdims=True) for j in range(4)], axis=0))
        dw = jnp.concatenate(dws, axis=-1)
        ba = ba_ref[...]
        dbgv = dbg_ref[...]
        lane = lax.broadcasted_iota(jnp.int32, ba.shape, 1)
        beta = _sigmoid(ba)
        ea = -jnp.exp(al_ref[...])
        zin = ba + dt_ref[...]
        is_b = lane < DN_HEADS
        is_a = (lane >= DN_HEADS) & (lane < 2 * DN_HEADS)
        da = jnp.where(is_a, dbgv * ea * _sigmoid(zin), 0.0)
        dba_ref[...] = jnp.where(is_b, dbgv * beta * (1.0 - beta), da).astype(dba_ref.dtype)
        ddt = jnp.sum(da, axis=0, keepdims=True)
        dal = jnp.sum(jnp.where(is_a, dbgv * ea * _softplus(zin), 0.0), axis=0, keepdims=True)

        @pl.when(first)
        def _():
            dw_ref[...] = dw
            ddt_ref[...] = ddt
            dal_ref[...] = dal

        @pl.when(i > 0)
        def _():
            dw_ref[...] += dw
            ddt_ref[...] += ddt
            dal_ref[...] += dal

    row = pl.BlockSpec((tm, DN_W), lambda i: (i, 0))
    nrow = pl.BlockSpec((HALO, DN_W), lambda i: (jnp.minimum((i + 1) * hb, t // HALO - 1), 0))
    one = pl.BlockSpec((1, 128), lambda i: (0, 0))
    return pl.pallas_call(
        body, name=name, grid=(nt,),
        in_specs=[pl.BlockSpec((tm, _QKV_W), lambda i: (i, 0)),
                  pl.BlockSpec((HALO, _QKV_W), lambda i: (jnp.maximum(i * hb - 1, 0), 0)),
                  pl.BlockSpec((HALO, _QKV_W), lambda i: (jnp.minimum((i + 1) * hb, t // HALO - 1), 0)),
                  pl.BlockSpec((4, _QKV_W), lambda i: (0, 0)),
                  pl.BlockSpec((tm, 128), lambda i: (i, _BA_COL)), one, one,
                  row, row, row, nrow, nrow, nrow, pl.BlockSpec((tm, 128), lambda i: (i, 0))],
        out_specs=(pl.BlockSpec((tm, _QKV_W), lambda i: (i, 0)), pl.BlockSpec((tm, 128), lambda i: (i, 0)),
                   pl.BlockSpec((4, _QKV_W), lambda i: (0, 0)), one, one),
        out_shape=(jax.ShapeDtypeStruct((t, _QKV_W), MXU_DTYPE), jax.ShapeDtypeStruct((t, 128), MXU_DTYPE),
                   jax.ShapeDtypeStruct((4, _QKV_W), F32), jax.ShapeDtypeStruct((1, 128), F32),
                   jax.ShapeDtypeStruct((1, 128), F32)),
        compiler_params=_cp("arbitrary"))(proj, proj, proj, conv_w, proj, alog_row, dtb_row, dq, dk, dv, dq, dk, dv, dbg)


def _chunk_masks():
    r = lax.broadcasted_iota(jnp.int32, (DN_CHUNK, DN_CHUNK), 0)
    c = lax.broadcasted_iota(jnp.int32, (DN_CHUNK, DN_CHUNK), 1)
    return r >= c, r > c


def _inv_unit_lower_many(mats):
    r = lax.broadcasted_iota(jnp.int32, mats[0].shape, 0)
    c = lax.broadcasted_iota(jnp.int32, mats[0].shape, 1)
    eye = jnp.where(r == c, 1.0, 0.0)
    xs = [eye - a for a in mats]
    pws = [_hi(a, a) for a in mats]
    for step in range(5):
        xs = [x + _hi(x, pw) for x, pw in zip(xs, pws)]
        if step < 4:
            pws = [_hi(pw, pw) for pw in pws]
    return xs


def _chunk_common(q, k, v, beta, gc, gcr, lower, strict):
    gam = jnp.exp(jnp.where(lower, gc - gcr, NEG))
    eg = jnp.exp(gc)
    gl = gc[DN_CHUNK - 1:DN_CHUNK, :]
    kdf = jnp.exp(gl - gc)
    kb = k * beta
    bmat = _mx_nt(kb, k)
    qmat = _mx_nt(q, k)
    return gam, eg, jnp.exp(gl), kdf, kb, bmat, qmat


def gdn_fwd(q, k, v, bg, name):
    t = q.shape[0]
    n_chunks = t // DN_CHUNK

    def body(q_ref, k_ref, v_ref, bg_ref, o_ref, sall_ref, tall_ref, s_ref):
        n = pl.program_id(0)

        @pl.when(n == 0)
        def _():
            s_ref[...] = jnp.zeros_like(s_ref)

        lower, strict = _chunk_masks()
        bgv = bg_ref[...]
        gcs = _hi(jnp.where(lower, 1.0, 0.0), bgv)
        gcs_t = gcs.T
        hs = range(DN_HEADS)
        sl = [slice(h * DN_DIM, (h + 1) * DN_DIM) for h in hs]
        st = [s_ref[h] for h in hs]
        for h in hs:
            sall_ref[0, h] = st[h]
        qh = [q_ref[:, sl[h]] for h in hs]
        kh = [k_ref[:, sl[h]] for h in hs]
        vh = [v_ref[:, sl[h]] for h in hs]
        beta = [bgv[:, h:h + 1] for h in hs]
        com = [_chunk_common(qh[h], kh[h], vh[h], beta[h], gcs[:, DN_HEADS + h:DN_HEADS + h + 1],
                             gcs_t[DN_HEADS + h:DN_HEADS + h + 1, :], lower, strict) for h in hs]
        gam, eg, dec, kdf, kb, bmat, qmat = zip(*com)
        tms = _inv_unit_lower_many([jnp.where(strict, bmat[h] * gam[h], 0.0) for h in hs])
        for h in hs:
            tall_ref[0, h] = tms[h]
        uw = [_hi(tms[h], jnp.concatenate([vh[h] * beta[h], kb[h] * eg[h]], axis=-1)) for h in hs]
        v_new = [uw[h][:, :DN_DIM] - _mx(uw[h][:, DN_DIM:], st[h]) for h in hs]
        o_st = [_mx(qh[h] * eg[h], st[h]) for h in hs]
        o_in = [_mx(qmat[h] * gam[h], v_new[h]) for h in hs]
        s_up = [_mx_tn(kh[h] * kdf[h], v_new[h]) for h in hs]
        for h in hs:
            o_ref[:, sl[h]] = o_st[h] + o_in[h]
            s_ref[h] = st[h] * dec[h] + s_up[h]

    row = pl.BlockSpec((DN_CHUNK, DN_W), lambda n: (n, 0))
    return pl.pallas_call(
        body, name=name, grid=(n_chunks,),
        in_specs=[row, row, row, pl.BlockSpec((DN_CHUNK, 128), lambda n: (n, 0))],
        out_specs=(row, pl.BlockSpec((1, DN_HEADS, DN_DIM, DN_DIM), lambda n: (n, 0, 0, 0)),
                   pl.BlockSpec((1, DN_HEADS, DN_CHUNK, DN_CHUNK), lambda n: (n, 0, 0, 0))),
        out_shape=(jax.ShapeDtypeStruct((t, DN_W), F32),
                   jax.ShapeDtypeStruct((n_chunks, DN_HEADS, DN_DIM, DN_DIM), F32),
                   jax.ShapeDtypeStruct((n_chunks, DN_HEADS, DN_CHUNK, DN_CHUNK), F32)),
        scratch_shapes=[pltpu.VMEM((DN_HEADS, DN_DIM, DN_DIM), F32)],
        compiler_params=_cp("arbitrary"))(q, k, v, bg)


def gdn_bwd(q, k, v, bg, sall, tall, do, name):
    t = q.shape[0]
    n_chunks = t // DN_CHUNK

    def body(q_ref, k_ref, v_ref, bg_ref, sall_ref, tall_ref, do_ref, dq_ref, dk_ref, dv_ref, dbg_ref, ds_ref):
        n = pl.program_id(0)

        @pl.when(n == 0)
        def _():
            ds_ref[...] = jnp.zeros_like(ds_ref)

        lower, strict = _chunk_masks()
        ltri = jnp.where(lower, 1.0, 0.0)
        bgv = bg_ref[...]
        gcs = _hi(ltri, bgv)
        gcs_t = gcs.T
        lane = lax.broadcasted_iota(jnp.int32, (DN_CHUNK, 128), 1)
        rowi = lax.broadcasted_iota(jnp.int32, (DN_CHUNK, 1), 0)
        hs = range(DN_HEADS)
        each = lambda fn, *ls: [fn(*a) for a in zip(*ls)]
        rsum = lambda a: jnp.sum(a, axis=-1, keepdims=True)
        sl = [slice(h * DN_DIM, (h + 1) * DN_DIM) for h in hs]
        st = [sall_ref[0, h] for h in hs]
        tms = [tall_ref[0, h] for h in hs]
        dsn = [ds_ref[h] for h in hs]
        qh = [q_ref[:, sl[h]] for h in hs]
        kh = [k_ref[:, sl[h]] for h in hs]
        vh = [v_ref[:, sl[h]] for h in hs]
        doh = [do_ref[:, sl[h]] for h in hs]
        beta = [bgv[:, h:h + 1] for h in hs]
        com = [_chunk_common(qh[h], kh[h], vh[h], beta[h], gcs[:, DN_HEADS + h:DN_HEADS + h + 1],
                             gcs_t[DN_HEADS + h:DN_HEADS + h + 1, :], lower, strict) for h in hs]
        gam, eg, dec, kdf, kb, bmat, qmat = zip(*com)
        rhs_w = each(lambda a, b: a * b, kb, eg)
        uw = each(lambda t_, v_, b_, r_: _hi(t_, jnp.concatenate([v_ * b_, r_], axis=-1)), tms, vh, beta, rhs_w)
        qd = each(lambda a, b: a * b, qh, eg)
        kd = each(lambda a, b: a * b, kh, kdf)
        pmat = each(lambda a, b: a * b, qmat, gam)
        v_new = each(lambda uw_, s_: uw_[:, :DN_DIM] - _mx(uw_[:, DN_DIM:], s_), uw, st)
        dqd = each(_mx_nt, doh, st)
        ds_o = each(_mx_tn, qd, doh)
        dp = each(lambda d_, v_: jnp.where(lower, _mx_nt(d_, v_), 0.0), doh, v_new)
        dvn_o = each(_mx_tn, pmat, doh)
        ddec = each(lambda d_, s_: jnp.sum(rsum(d_ * s_), axis=0, keepdims=True), dsn, st)
        dkd = each(_mx_nt, v_new, dsn)
        dvn = each(lambda a, k_, d_: a + _mx(k_, d_), dvn_o, kd, dsn)
        dw = each(lambda d_, s_: -_mx_nt(d_, s_), dvn, st)
        ds_w = each(lambda uw_, d_: _mx_tn(uw_[:, DN_DIM:], d_), uw, dvn)
        for h in hs:
            ds_ref[h] = ds_o[h] + dec[h] * dsn[h] - ds_w[h]
        dr = each(lambda t_, a, b: _hi_tn(t_, jnp.concatenate([a, b], axis=-1)), tms, dvn, dw)
        da = each(lambda r_, uw_: jnp.where(strict, -_hi_nt(r_, uw_), 0.0), dr, uw)
        dru = [r_[:, :DN_DIM] for r_ in dr]
        drw = [r_[:, DN_DIM:] for r_ in dr]
        db = each(lambda a, b: a * b, da, gam)
        dq_m = each(lambda a, b: a * b, dp, gam)
        e = each(lambda a, bm, p_, qm, g_: (a * bm + p_ * qm) * g_, da, bmat, dp, qmat, gam)
        dkb = each(lambda b_, k_, r_, e_: _mx(b_, k_) + r_ * e_, db, kh, drw, eg)
        dk = each(lambda b_, kb_, m_, q_, d_, f_: _mx_tn(b_, kb_) + _mx_tn(m_, q_) + d_ * f_, db, kb, dq_m, qh, dkd, kdf)
        dq = each(lambda m_, k_, d_, e_: _mx(m_, k_) + d_ * e_, dq_m, kh, dqd, eg)
        tk = each(lambda a, b: rsum(a * b), dkd, kd)
        dbeta_all = jnp.zeros((DN_CHUNK, 128), F32)
        dgc_all = jnp.zeros((DN_CHUNK, 128), F32)
        for h in hs:
            dgc = (jnp.sum(e[h], axis=1, keepdims=True) - jnp.sum(e[h].T, axis=1, keepdims=True)
                   + rsum(dqd[h] * qd[h]) - tk[h] + rsum(drw[h] * rhs_w[h]))
            dgl = jnp.sum(tk[h], axis=0, keepdims=True) + ddec[h] * dec[h]
            dgc = dgc + jnp.where(rowi == DN_CHUNK - 1, dgl, 0.0)
            dbeta = rsum(dru[h] * vh[h]) + rsum(dkb[h] * kh[h])
            dq_ref[:, sl[h]] = dq[h]
            dk_ref[:, sl[h]] = dk[h] + dkb[h] * beta[h]
            dv_ref[:, sl[h]] = dru[h] * beta[h]
            dbeta_all = jnp.where(lane == h, dbeta, dbeta_all)
            dgc_all = jnp.where(lane == DN_HEADS + h, dgc, dgc_all)
        dbg_ref[...] = dbeta_all + _hi_tn(ltri, dgc_all)

    rev = lambda n: (n_chunks - 1 - n, 0)
    row = pl.BlockSpec((DN_CHUNK, DN_W), rev)
    small = pl.BlockSpec((DN_CHUNK, 128), rev)
    return pl.pallas_call(
        body, name=name, grid=(n_chunks,),
        in_specs=[row, row, row, small,
                  pl.BlockSpec((1, DN_HEADS, DN_DIM, DN_DIM), lambda n: (n_chunks - 1 - n, 0, 0, 0)),
                  pl.BlockSpec((1, DN_HEADS, DN_CHUNK, DN_CHUNK), lambda n: (n_chunks - 1 - n, 0, 0, 0)), row],
        out_specs=(row, row, row, small),
        out_shape=(jax.ShapeDtypeStruct((t, DN_W), F32),) * 3 + (jax.ShapeDtypeStruct((t, 128), F32),),
        scratch_shapes=[pltpu.VMEM((DN_HEADS, DN_DIM, DN_DIM), F32)],
        compiler_params=_cp("arbitrary"))(q, k, v, bg, sall, tall, do)


def gdn_out_fwd(o, proj, o_gain, name, tm=256):
    t = o.shape[0]

    def body(o_ref, z_ref, g_ref, y_ref, yt_ref):
        for h in range(DN_HEADS):
            sl = slice(h * DN_DIM, (h + 1) * DN_DIM)
            ov, zv = o_ref[:, sl], z_ref[:, sl]
            r = lax.rsqrt(jnp.mean(ov * ov, axis=-1, keepdims=True) + EPS)
            y = ov * r * g_ref[...] * (zv * _sigmoid(zv))
            y_ref[:, sl] = y.astype(y_ref.dtype)
            yt_ref[sl, :] = y.T.astype(yt_ref.dtype)

    row = pl.BlockSpec((tm, DN_W), lambda i: (i, 0))
    return pl.pallas_call(
        body, name=name, grid=(t // tm,),
        in_specs=[row, pl.BlockSpec((tm, DN_W), lambda i: (i, _Z_COL)), pl.BlockSpec((1, DN_DIM), lambda i: (0, 0))],
        out_specs=(row, pl.BlockSpec((DN_W, tm), lambda i: (0, i))),
        out_shape=(jax.ShapeDtypeStruct((t, DN_W), MXU_DTYPE), jax.ShapeDtypeStruct((DN_W, t), MXU_DTYPE)),
        compiler_params=_cp("parallel"))(o, proj, o_gain)


def gdn_out_bwd(o, proj, o_gain, dy, name, tm=256):
    t = o.shape[0]

    def body(o_ref, z_ref, g_ref, dy_ref, do_ref, dz_ref, dg_ref):
        i = pl.program_id(0)
        dg = jnp.zeros((1, DN_DIM), F32)
        for h in range(DN_HEADS):
            sl = slice(h * DN_DIM, (h + 1) * DN_DIM)
            ov, zv, dyv = o_ref[:, sl], z_ref[:, sl], dy_ref[:, sl]
            r = lax.rsqrt(jnp.mean(ov * ov, axis=-1, keepdims=True) + EPS)
            oh = ov * r
            sg = _sigmoid(zv)
            dz_ref[:, sl] = (dyv * oh * g_ref[...] * sg * (1.0 + zv * (1.0 - sg))).astype(dz_ref.dtype)
            don = dyv * (zv * sg)
            dg = dg + jnp.sum(don * oh, axis=0, keepdims=True)
            doh = don * g_ref[...]
            do_ref[:, sl] = r * (doh - oh * jnp.mean(doh * oh, axis=-1, keepdims=True))

        @pl.when(i == 0)
        def _():
            dg_ref[...] = dg

        @pl.when(i > 0)
        def _():
            dg_ref[...] += dg

    row = pl.BlockSpec((tm, DN_W), lambda i: (i, 0))
    one = pl.BlockSpec((1, DN_DIM), lambda i: (0, 0))
    return pl.pallas_call(
        body, name=name, grid=(t // tm,),
        in_specs=[row, pl.BlockSpec((tm, DN_W), lambda i: (i, _Z_COL)), one, row],
        out_specs=(row, row, one),
        out_shape=(jax.ShapeDtypeStruct((t, DN_W), F32), jax.ShapeDtypeStruct((t, DN_W), MXU_DTYPE),
                   jax.ShapeDtypeStruct((1, DN_DIM), F32)),
        compiler_params=_cp("arbitrary"))(o, proj, o_gain, dy)


def _peer(k):
    x, y, c = lax.axis_index("x"), lax.axis_index("y"), lax.axis_index("c")
    px = 1 - x if k & 4 else x
    py = 1 - y if k & 2 else y
    pc = 1 - c if k & 1 else c
    return (px, py, pc), 4 * px + 2 * py + pc


def all_gather(shards, name):
    na = len(shards)

    def body(*refs):
        ins, outs = refs[:na], refs[na:2 * na]
        send_sems, recv_sems, local_sems = refs[2 * na:]
        _, me = _peer(0)
        local = [pltpu.make_async_copy(ins[a], outs[a].at[me], local_sems.at[a]) for a in range(na)]
        for cp in local:
            cp.start()
        sends = []
        for k in range(1, N_DEV):
            peer, _ = _peer(k)
            for a in range(na):
                cp = pltpu.make_async_remote_copy(
                    src_ref=ins[a], dst_ref=outs[a].at[me], send_sem=send_sems.at[a, k - 1],
                    recv_sem=recv_sems.at[a, k - 1], device_id=peer, device_id_type=MESH)
                cp.start()
                sends.append(cp)
        for k in range(1, N_DEV):
            peer, pid = _peer(k)
            for a in range(na):
                pltpu.make_async_remote_copy(
                    src_ref=ins[a], dst_ref=outs[a].at[pid], send_sem=send_sems.at[a, k - 1],
                    recv_sem=recv_sems.at[a, k - 1], device_id=peer, device_id_type=MESH).wait_recv()
        for cp in sends:
            cp.wait_send()
        for cp in local:
            cp.wait()

    anyspec = pl.BlockSpec(memory_space=pl.ANY)
    return pl.pallas_call(
        body, name=name, in_specs=[anyspec] * na, out_specs=tuple([anyspec] * na),
        out_shape=tuple(jax.ShapeDtypeStruct((N_DEV,) + s.shape, s.dtype) for s in shards),
        scratch_shapes=[pltpu.SemaphoreType.DMA((na, N_DEV - 1)), pltpu.SemaphoreType.DMA((na, N_DEV - 1)),
                        pltpu.SemaphoreType.DMA((na,))],
        compiler_params=pltpu.CompilerParams(has_side_effects=True))(*shards)


_HBM = pl.BlockSpec(memory_space=pltpu.HBM)
_SEM = pl.BlockSpec(memory_space=pltpu.SEMAPHORE)
_DATAFLOW = pltpu.SideEffectType.DATAFLOW_SIDE_EFFECTING
N_PEER = N_DEV - 1


def send_start(srcs, name, scatter, after):
    na = len(srcs)
    ns = 2 * N_PEER * na
    lands = [lax.empty((N_DEV,) + (s.shape[1:] if scatter else s.shape), s.dtype) for s in srcs]
    extra = [] if after is None else [after]

    def body(*refs):
        src_refs, land_refs = refs[:na], refs[na:2 * na]
        sems, token = refs[2 * na + len(extra):2 * na + len(extra) + ns], refs[-1]
        _, me = _peer(0)
        for k in range(1, N_DEV):
            peer, pid = _peer(k)
            for a in range(na):
                pltpu.make_async_remote_copy(
                    src_ref=src_refs[a].at[pid] if scatter else src_refs[a], dst_ref=land_refs[a].at[me],
                    send_sem=sems[2 * (a * N_PEER + k - 1)], recv_sem=sems[2 * (a * N_PEER + k - 1) + 1],
                    device_id=peer, device_id_type=MESH).start()
        token[...] = jnp.zeros_like(token)

    hbm = lambda arrs: tuple(pltpu.HBM(a.shape, a.dtype) for a in arrs)
    outs = pl.pallas_call(
        body, name=name,
        out_shape=(pltpu.SemaphoreType.DMA(()),) * ns + hbm(srcs) + hbm(lands) + (jax.ShapeDtypeStruct((8, 128), F32),),
        in_specs=[_HBM] * (2 * na) + [pl.BlockSpec(memory_space=pl.ANY)] * len(extra),
        out_specs=(_SEM,) * ns + (_HBM,) * (2 * na) + (pl.BlockSpec(memory_space=pltpu.VMEM),),
        input_output_aliases={i: ns + i for i in range(2 * na)},
        compiler_params=pltpu.CompilerParams(has_side_effects=_DATAFLOW),
    )(*[pltpu.with_memory_space_constraint(a, pltpu.HBM) for a in list(srcs) + lands], *extra)
    return outs[:ns], outs[ns:ns + na], outs[ns + na:ns + 2 * na], outs[-1]


def send_wait(sems, srcs_thru, lands_thru, name, scatter, after):
    na = len(srcs_thru)
    ns = 2 * N_PEER * na

    def body(*refs):
        src_refs, land_refs, sm = refs[:na], refs[na:2 * na], refs[2 * na:2 * na + ns]
        for k in range(1, N_DEV):
            peer, pid = _peer(k)
            for a in range(na):
                cp = pltpu.make_async_remote_copy(
                    src_ref=src_refs[a].at[pid] if scatter else src_refs[a], dst_ref=land_refs[a].at[pid],
                    send_sem=sm[2 * (a * N_PEER + k - 1)], recv_sem=sm[2 * (a * N_PEER + k - 1) + 1],
                    device_id=peer, device_id_type=MESH)
                cp.wait_send()
                cp.wait_recv()

    hbm = lambda arrs: tuple(pltpu.HBM(a.shape, a.dtype) for a in arrs)
    outs = pl.pallas_call(
        body, name=name, out_shape=hbm(srcs_thru) + hbm(lands_thru),
        in_specs=[_HBM] * (2 * na) + [_SEM] * ns + [pl.BlockSpec(memory_space=pl.ANY)], out_specs=(_HBM,) * (2 * na),
        input_output_aliases={i: i for i in range(2 * na)},
        compiler_params=pltpu.CompilerParams(has_side_effects=_DATAFLOW),
    )(*srcs_thru, *lands_thru, *sems, after)
    return outs[na:]


def _adamw(w, g, m, v):
    m = ADAM_B1 * m + (1.0 - ADAM_B1) * g
    v = ADAM_B2 * v + (1.0 - ADAM_B2) * (g * g)
    m_hat = m / (1.0 - ADAM_B1 ** ADAM_STEP)
    v_hat = v / (1.0 - ADAM_B2 ** ADAM_STEP)
    return -ADAM_LR * (m_hat / (jnp.sqrt(v_hat) + ADAM_EPS) + ADAM_WD * w), m, v


def adam_sum(w, pieces, m, v, name):
    r, c = w.shape
    tr = r
    for cand in (256, 128, 64, 32, 16, 8):
        if r % cand == 0:
            tr = cand
            break

    def body(w_ref, p_ref, m_ref, v_ref, g_ref, d_ref, nm_ref, nv_ref):
        g = p_ref[0].astype(F32)
        for s in range(1, N_DEV):
            g = g + p_ref[s].astype(F32)
        g_ref[...] = g
        d_ref[...], nm_ref[...], nv_ref[...] = _adamw(w_ref[...], g, m_ref[...], v_ref[...])

    row = pl.BlockSpec((tr, c), lambda i: (i, 0))
    out = jax.ShapeDtypeStruct((r, c), F32)
    return pl.pallas_call(
        body, name=name, grid=(r // tr,),
        in_specs=[row, pl.BlockSpec((N_DEV, tr, c), lambda i: (0, i, 0)), row, row],
        out_specs=(row,) * 4, out_shape=(out,) * 4, compiler_params=_cp("parallel"))(w, pieces, m, v)


def sum_rows(gathered, name):
    _, r, c = gathered.shape

    def body(p_ref, o_ref):
        g = p_ref[0]
        for s in range(1, N_DEV):
            g = g + p_ref[s]
        o_ref[...] = g

    return pl.pallas_call(body, name=name, out_shape=jax.ShapeDtypeStruct((r, c), F32))(gathered)


def adam_small(w, g, m, v, name):
    def body(w_ref, g_ref, m_ref, v_ref, d_ref, nm_ref, nv_ref):
        d_ref[...], nm_ref[...], nv_ref[...] = _adamw(w_ref[...], g_ref[...], m_ref[...], v_ref[...])

    out = jax.ShapeDtypeStruct(w.shape, F32)
    return pl.pallas_call(body, name=name, out_shape=(out,) * 3)(w, g, m, v)


def _rope_tables(t):
    inv_freq = 10000.0 ** (-jnp.arange(0, HEAD_DIM, 2, dtype=F32) / HEAD_DIM)
    ang = jnp.arange(t, dtype=F32)[:, None] * inv_freq[None, :]
    cos, sin = jnp.cos(ang), jnp.sin(ang)
    return jnp.concatenate([cos, cos], axis=-1), jnp.concatenate([sin, sin], axis=-1)


def _lane_row(vec8):
    return jnp.pad(vec8.reshape(1, DN_HEADS), ((0, 0), (DN_HEADS, 128 - 2 * DN_HEADS)))


def _ffn_fwd(x, norm_g, w_gu, w_d, tag):
    f, ft = rms_fwd(x, norm_g, f"{tag}_norm")
    gu, a, at = ffn_up(f, w_gu, f"{tag}_gate_up")
    return mm_nn(a, w_d, f"{tag}_down", res=x), (ft, gu, at)


def _ffn_bwd(x, norm_g, w_gu, w_d, saved, dy, tag, after=None):
    ft, gu, at = saved
    dgu = ffn_dact(dy, w_d, gu, f"{tag}_d_gate_up", after=after)
    dwd = mm_at(at, dy, f"{tag}_dw_down")
    df = mm_nt(dgu, w_gu, f"{tag}_d_normed")
    dwgu = mm_at(ft, dgu, f"{tag}_dw_gate_up")
    dx, dg = rms_bwd(x, norm_g, df, dy, f"{tag}_d_norm")
    return dx, dwgu, dwd, dg


def local_step(x, target, small, weights_of, grads_out, after=None):
    t = x.shape[0]
    cosf, sinf = _rope_tables(t)
    alog_row, dtb_row = _lane_row(small["odd_a_log"]), _lane_row(small["odd_dt_bias"])

    h0, h0t = rms_fwd(x, small["even_norm"], "even_norm", after=after)
    we = weights_of("even", h0)
    proj0 = mm_nn(h0, we["w_in"], "even_in_proj")
    qr, kr = qk_prep_fwd(proj0, small["even_q_gain"], small["even_k_gain"], cosf, sinf, "even_qk_prep")
    y_attn, lse = swa_fwd(qr, kr, proj0, small["even_sinks"], "even_swa")
    y_conv = gconv_fwd(proj0, small["even_conv_w"], "even_gconv")
    mix0 = jnp.concatenate([y_attn.astype(MXU_DTYPE), y_conv], axis=-1)
    x1 = mm_nn(mix0, we["w_out"], "even_out_proj", res=x)
    w0 = weights_of("ffn0", x1)
    x2, ffn0 = _ffn_fwd(x1, small["ffn_norm0"], w0["gate_up"], w0["down"], "ffn0")

    wo = weights_of("odd", x2)
    h1, h1t = rms_fwd(x2, small["odd_norm"], "odd_norm")
    proj1 = mm_nn(h1, wo["w_in"], "odd_in_proj")
    qn, kn, vs, bg = gdn_prep_fwd(proj1, small["odd_conv_w"], alog_row, dtb_row, "odd_prep")
    o, sall, tall = gdn_fwd(qn, kn, vs, bg, "odd_delta_rule")
    og, ogt = gdn_out_fwd(o, proj1, small["odd_o_gain"], "odd_gate_norm")
    x3 = mm_nn(og, wo["w_out"], "odd_out_proj", res=x2)
    w1 = weights_of("ffn1", x3)
    x4, ffn1 = _ffn_fwd(x3, small["ffn_norm1"], w1["gate_up"], w1["down"], "ffn1")

    dy, loss_row = loss_head(x4, target, "loss_head")

    gs = {}
    dx3, dwgu, dwd, gs["ffn_norm1"] = _ffn_bwd(x3, small["ffn_norm1"], w1["gate_up"], w1["down"], ffn1, dy, "ffn1")
    tok = grads_out("ffn1", {"gate_up": dwgu, "down": dwd})

    dog = mm_nt(dx3, wo["w_out"], "odd_d_gated", after=tok)
    dwo = mm_at(ogt, dx3, "odd_dw_out")
    do, dz, gs["odd_o_gain"] = gdn_out_bwd(o, proj1, small["odd_o_gain"], dog, "odd_d_gate_norm")
    dqn, dkn, dvs, dbg = gdn_bwd(qn, kn, vs, bg, sall, tall, do, "odd_d_delta_rule")
    dqkv, dba, gs["odd_conv_w"], ddt_row, dal_row = gdn_prep_bwd(
        proj1, small["odd_conv_w"], alog_row, dtb_row, dqn, dkn, dvs, dbg, "odd_d_prep")
    gs["odd_dt_bias"] = ddt_row[:, DN_HEADS:2 * DN_HEADS]
    gs["odd_a_log"] = dal_row[:, DN_HEADS:2 * DN_HEADS]
    dproj1 = jnp.concatenate([dqkv, dz, dba], axis=-1)
    dh1 = mm_nt(dproj1, wo["w_in"], "odd_d_normed")
    dwi = mm_at(h1t, dproj1, "odd_dw_in")
    dx2, gs["odd_norm"] = rms_bwd(x2, small["odd_norm"], dh1, dx3, "odd_d_norm")
    tok = grads_out("odd", {"w_in": dwi, "w_out": dwo})

    dx1, dwgu, dwd, gs["ffn_norm0"] = _ffn_bwd(x1, small["ffn_norm0"], w0["gate_up"], w0["down"], ffn0, dx2, "ffn0",
                                               after=tok)
    tok = grads_out("ffn0", {"gate_up": dwgu, "down": dwd})

    dmix = mm_nt(dx1, we["w_out"], "even_d_mix", after=tok)
    dwo = mm_at(mix0.T, dx1, "even_dw_out")
    dqr, dkr, dv, gs["even_sinks"] = swa_bwd(qr, kr, proj0, small["even_sinks"], y_attn, lse, dmix, "even_d_swa")
    dqk, gs["even_q_gain"], gs["even_k_gain"] = qk_prep_bwd(
        proj0, small["even_q_gain"], small["even_k_gain"], cosf, sinf, dqr, dkr, "even_d_qk_prep")
    dgb, dgc, dxi, gs["even_conv_w"] = gconv_bwd(proj0, small["even_conv_w"], dmix, "even_d_gconv")
    dproj0 = jnp.concatenate([dqk, dv, dgb, dgc, dxi], axis=-1)
    dwi = mm_at(h0t, dproj0, "even_dw_in")
    tok = grads_out("even", {"w_in": dwi, "w_out": dwo})
    dh0 = mm_nt(dproj0, we["w_in"], "even_d_normed", after=tok)
    grad_x, gs["even_norm"] = rms_bwd(x, small["even_norm"], dh0, dx1, "even_d_norm")
    return loss_row, grad_x, gs


_SMALL_ORDER = ("even_norm", "even_q_gain", "even_k_gain", "even_sinks", "odd_a_log", "odd_dt_bias", "odd_o_gain",
                "ffn_norm0", "ffn_norm1", "odd_norm", "even_conv_w", "odd_conv_w")
_SMALL_SIZE = {"even_norm": 1024, "even_q_gain": 64, "even_k_gain": 64, "even_sinks": 8, "odd_a_log": 8,
               "odd_dt_bias": 8, "odd_o_gain": 128, "ffn_norm0": 1024, "ffn_norm1": 1024, "odd_norm": 1024,
               "even_conv_w": 3 * 512, "odd_conv_w": 4 * 3072}
_N_REPL = 9


def _pack_rows(vals):
    flat = jnp.concatenate([v.reshape(-1) for v in vals])
    pad = (-flat.shape[0]) % 1024
    return jnp.pad(flat, (0, pad)).reshape(-1, 128)


def _my_block(full, size, axis):
    me = 4 * lax.axis_index("x") + 2 * lax.axis_index("y") + lax.axis_index("c")
    return lax.dynamic_slice_in_dim(full, me * size, size, axis=axis)


def _col_gathered(g):
    return g.transpose(1, 0, 2).reshape(g.shape[1], N_DEV * g.shape[2])


def _col_pieces(dw):
    k, n8 = dw.shape
    return dw.reshape(k, N_DEV, n8 // N_DEV).transpose(1, 0, 2)


def kernel(x, even_norm, even_w_in, even_q_gain, even_k_gain, even_sinks, even_conv_w, even_w_out, odd_norm, odd_w_in, odd_conv_w, odd_a_log, odd_dt_bias, odd_o_gain, odd_w_out, ffn_norm, ffn_w_gate_up, ffn_w_down, loss_target, m_even_norm, m_even_w_in, m_even_q_gain, m_even_k_gain, m_even_sinks, m_even_conv_w, m_even_w_out, m_odd_norm, m_odd_w_in, m_odd_conv_w, m_odd_a_log, m_odd_dt_bias, m_odd_o_gain, m_odd_w_out, m_ffn_norm, m_ffn_w_gate_up, m_ffn_w_down, v_even_norm, v_even_w_in, v_even_q_gain, v_even_k_gain, v_even_sinks, v_even_conv_w, v_even_w_out, v_odd_norm, v_odd_w_in, v_odd_conv_w, v_odd_a_log, v_odd_dt_bias, v_odd_o_gain, v_odd_w_out, v_ffn_norm, v_ffn_w_gate_up, v_ffn_w_down):
    t = x.shape[1]
    d = D_MODEL

    me = 4 * lax.axis_index("x") + 2 * lax.axis_index("y") + lax.axis_index("c")
    fpd = D_FF // N_DEV
    shard = {
        "even": {"w_in": even_w_in.reshape(d, EVEN_IN_W // N_DEV), "w_out": even_w_out.reshape(d // N_DEV, d)},
        "ffn0": {"gate_up": ffn_w_gate_up[0], "down": ffn_w_down[0]},
        "odd": {"w_in": odd_w_in.reshape(d, ODD_IN_W // N_DEV), "w_out": odd_w_out.reshape(d // N_DEV, d)},
        "ffn1": {"gate_up": ffn_w_gate_up[1], "down": ffn_w_down[1]},
    }
    mom_m = {
        "even": {"w_in": m_even_w_in, "w_out": m_even_w_out}, "odd": {"w_in": m_odd_w_in, "w_out": m_odd_w_out},
        "ffn0": {"gate_up": m_ffn_w_gate_up[0], "down": m_ffn_w_down[0]},
        "ffn1": {"gate_up": m_ffn_w_gate_up[1], "down": m_ffn_w_down[1]},
    }
    mom_v = {
        "even": {"w_in": v_even_w_in, "w_out": v_even_w_out}, "odd": {"w_in": v_odd_w_in, "w_out": v_odd_w_out},
        "ffn0": {"gate_up": v_ffn_w_gate_up[0], "down": v_ffn_w_down[0]},
        "ffn1": {"gate_up": v_ffn_w_gate_up[1], "down": v_ffn_w_down[1]},
    }

    def whole(group, parts):
        col, row = tuple(shard[group])
        w_col = _col_gathered(parts[0])
        if group == "odd":
            w_col = jnp.pad(w_col, ((0, 0), (0, ODD_IN_PAD - ODD_IN_W)))
        if col == "gate_up":
            w_col = _gu_tiled(w_col)
        return {col: w_col, row: parts[1].reshape(-1, d)}

    def own_slot(lands, blocks):
        return [lax.dynamic_update_index_in_dim(l, b, me, 0) for l, b in zip(lands, blocks)]

    shard_rows = _pack_rows([odd_norm, even_conv_w, odd_conv_w])
    (small_g,) = all_gather([shard_rows], "gather_small_weights")
    wire = {g: [a.astype(MXU_DTYPE) for a in shard[g].values()] for g in shard}
    gathers, tok = {}, small_g
    for g in shard:
        sems, srcs_thru, lands_thru, tok = send_start(wire[g], f"gather_{g}_start", False, tok)
        gathers[g] = (sems, srcs_thru, lands_thru)

    def weights_of(group, after):
        lands = send_wait(*gathers[group], f"gather_{group}_wait", False, after)
        return whole(group, own_slot(lands, wire[group]))

    sent = {}

    def grads_out(group, dws):
        col, row = tuple(shard[group])
        n_cols = N_DEV * shard[group][col].shape[1]
        dw_col = _gu_untiled(dws[col]) if col == "gate_up" else dws[col][:, :n_cols]
        pieces = [_col_pieces(dw_col).astype(MXU_DTYPE),
                  dws[row].reshape((N_DEV,) + shard[group][row].shape).astype(MXU_DTYPE)]
        sems, srcs_thru, lands_thru, token = send_start(pieces, f"exchange_{group}_start", True, None)
        sent[group] = (sems, srcs_thru, lands_thru, pieces)
        return token

    sg = small_g.reshape(N_DEV, -1)
    o1 = d // N_DEV
    o2 = o1 + 3 * CONV_CH // N_DEV
    small = {
        "even_norm": even_norm, "even_q_gain": even_q_gain, "even_k_gain": even_k_gain, "even_sinks": even_sinks,
        "odd_a_log": odd_a_log.reshape(-1), "odd_dt_bias": odd_dt_bias.reshape(-1), "odd_o_gain": odd_o_gain,
        "ffn_norm0": ffn_norm[0:1], "ffn_norm1": ffn_norm[1:2],
        "odd_norm": sg[:, :o1].reshape(1, d),
        "even_conv_w": sg[:, o1:o2].reshape(N_DEV, 3, CONV_CH // N_DEV).transpose(1, 0, 2).reshape(3, CONV_CH),
        "odd_conv_w": sg[:, o2:o2 + 4 * _QKV_W // N_DEV].reshape(N_DEV, 4, _QKV_W // N_DEV).transpose(1, 0, 2).reshape(4, _QKV_W),
    }

    loss_row, grad_x, gs = local_step(x.reshape(t, d), loss_target.reshape(t, d), small, weights_of, grads_out, after=tok)

    rows = _pack_rows([gs[n] for n in _SMALL_ORDER] + [loss_row[:, 0:1]])
    (rows_g,) = all_gather([rows], "gather_small_grads")
    tot = sum_rows(rows_g, "sum_small_grads").reshape(-1)
    off, sgrad = 0, {}
    for n in _SMALL_ORDER:
        sgrad[n] = tot[off:off + _SMALL_SIZE[n]]
        off += _SMALL_SIZE[n]
    loss = tot[off]

    upd, behind = {}, tot
    for g in ("ffn1", "odd", "ffn0", "even"):
        sems, srcs_thru, lands_thru, pieces = sent[g]
        lands = send_wait(sems, srcs_thru, lands_thru, f"exchange_{g}_wait", True, behind)
        own = [lax.dynamic_index_in_dim(p, me, 0, keepdims=False) for p in pieces]
        for (key, w2), pcs in zip(shard[g].items(), own_slot(lands, own)):
            upd[g, key] = adam_sum(w2, pcs, mom_m[g][key].reshape(w2.shape), mom_v[g][key].reshape(w2.shape),
                                   f"adamw_{g}_{key}")
        behind = upd[g, key][0]
    res = {
        "even_w_in": tuple(o.reshape(even_w_in.shape) for o in upd["even", "w_in"]),
        "even_w_out": tuple(o.reshape(even_w_out.shape) for o in upd["even", "w_out"]),
        "odd_w_in": tuple(o.reshape(odd_w_in.shape) for o in upd["odd", "w_in"]),
        "odd_w_out": tuple(o.reshape(odd_w_out.shape) for o in upd["odd", "w_out"]),
        "ffn_w_gate_up": tuple(jnp.stack([a, b]) for a, b in zip(upd["ffn0", "gate_up"], upd["ffn1", "gate_up"])),
        "ffn_w_down": tuple(jnp.stack([a, b]) for a, b in zip(upd["ffn0", "down"], upd["ffn1", "down"])),
    }

    repl = _SMALL_ORDER[:_N_REPL]
    repl_w = {"even_norm": even_norm, "even_q_gain": even_q_gain, "even_k_gain": even_k_gain, "even_sinks": even_sinks,
              "odd_a_log": odd_a_log, "odd_dt_bias": odd_dt_bias, "odd_o_gain": odd_o_gain,
              "ffn_norm0": ffn_norm[0], "ffn_norm1": ffn_norm[1]}
    repl_m = {"even_norm": m_even_norm, "even_q_gain": m_even_q_gain, "even_k_gain": m_even_k_gain,
              "even_sinks": m_even_sinks, "odd_a_log": m_odd_a_log, "odd_dt_bias": m_odd_dt_bias,
              "odd_o_gain": m_odd_o_gain, "ffn_norm0": m_ffn_norm[0], "ffn_norm1": m_ffn_norm[1]}
    repl_v = {"even_norm": v_even_norm, "even_q_gain": v_even_q_gain, "even_k_gain": v_even_k_gain,
              "even_sinks": v_even_sinks, "odd_a_log": v_odd_a_log, "odd_dt_bias": v_odd_dt_bias,
              "odd_o_gain": v_odd_o_gain, "ffn_norm0": v_ffn_norm[0], "ffn_norm1": v_ffn_norm[1]}
    pk = lambda dct: _pack_rows([dct[n] for n in repl])
    pd_, pm_, pv_ = adam_small(pk(repl_w), pk(sgrad), pk(repl_m), pk(repl_v), "adamw_replicated")
    sres = {}
    off = 0
    for n in repl:
        sz = _SMALL_SIZE[n]
        sres[n] = (sgrad[n], pd_.reshape(-1)[off:off + sz], pm_.reshape(-1)[off:off + sz], pv_.reshape(-1)[off:off + sz])
        off += sz
    g_on = _my_block(sgrad["odd_norm"].reshape(1, d), d // N_DEV, 1)
    g_ec = _my_block(sgrad["even_conv_w"].reshape(3, CONV_CH), CONV_CH // N_DEV, 1)
    g_oc = _my_block(sgrad["odd_conv_w"].reshape(4, _QKV_W), _QKV_W // N_DEV, 1)
    shard_w = _pack_rows([odd_norm, even_conv_w, odd_conv_w])
    sd_, sm_, sv_ = adam_small(shard_w, _pack_rows([g_on, g_ec, g_oc]),
                               _pack_rows([m_odd_norm, m_even_conv_w, m_odd_conv_w]),
                               _pack_rows([v_odd_norm, v_even_conv_w, v_odd_conv_w]), "adamw_sharded_small")
    off = 0
    for n, gfull, like in (("odd_norm", g_on, odd_norm), ("even_conv_w", g_ec, even_conv_w), ("odd_conv_w", g_oc, odd_conv_w)):
        sz = like.size
        sres[n] = (gfull, sd_.reshape(-1)[off:off + sz], sm_.reshape(-1)[off:off + sz], sv_.reshape(-1)[off:off + sz])
        off += sz

    def small_out(name, like, kind):
        if name == "ffn_norm":
            return jnp.stack([sres["ffn_norm0"][kind], sres["ffn_norm1"][kind]]).reshape(like.shape)
        return sres[name][kind].reshape(like.shape)

    order = (("even_norm", even_norm), ("even_w_in", even_w_in), ("even_q_gain", even_q_gain),
             ("even_k_gain", even_k_gain), ("even_sinks", even_sinks), ("even_conv_w", even_conv_w),
             ("even_w_out", even_w_out), ("odd_norm", odd_norm), ("odd_w_in", odd_w_in), ("odd_conv_w", odd_conv_w),
             ("odd_a_log", odd_a_log), ("odd_dt_bias", odd_dt_bias), ("odd_o_gain", odd_o_gain),
             ("odd_w_out", odd_w_out), ("ffn_norm", ffn_norm), ("ffn_w_gate_up", ffn_w_gate_up),
             ("ffn_w_down", ffn_w_down))
    outs = [loss, grad_x.reshape(x.shape)]
    for kind in range(4):
        for name, like in order:
            outs.append(res[name][kind] if name in res else small_out(name, like, kind))
    return tuple(outs)
```

```python
import functools

import jax
import jax.numpy as jnp
from jax import lax
from jax.experimental import pallas as pl
from jax.experimental.pallas import tpu as pltpu

F32 = jnp.float32
MXU_DTYPE = jnp.bfloat16
HI = lax.Precision.HIGH
EPS = 1e-6
N_DEV = 8
D_MODEL = 1024
HEAD_DIM = 64
ATTN_HEADS = 8
KV_HEADS = 2
ATTN_BLOCK = 128
Q_W = 512
KV_W = 128
CONV_CH = 512
EVEN_IN_W = 2304
DN_HEADS = 8
DN_DIM = 128
DN_W = 1024
DN_CHUNK = 64
ODD_IN_W = 4112
ODD_IN_PAD = 4224
D_FF = 2816
NEG = -1e30
VMEM_LIMIT = 56 * 1024 * 1024
ADAM_LR, ADAM_B1, ADAM_B2, ADAM_EPS, ADAM_WD, ADAM_STEP = 0.001, 0.9, 0.999, 1e-08, 0.01, 10
MESH = pl.DeviceIdType.MESH


def _cp(*sem):
    return pltpu.CompilerParams(dimension_semantics=sem, vmem_limit_bytes=VMEM_LIMIT)


def _pick(n, cap):
    best = 128
    for t in range(128, cap + 1, 128):
        if n % t == 0:
            best = t
    return best


def _mx(a, b):
    return jnp.dot(a.astype(MXU_DTYPE), b.astype(MXU_DTYPE), preferred_element_type=F32)


def _mx_nt(a, b):
    return lax.dot_general(a.astype(MXU_DTYPE), b.astype(MXU_DTYPE), (((1,), (1,)), ((), ())),
                           preferred_element_type=F32)


def _mx_tn(a, b):
    return lax.dot_general(a.astype(MXU_DTYPE), b.astype(MXU_DTYPE), (((0,), (0,)), ((), ())),
                           preferred_element_type=F32)


def _hi(a, b):
    return jnp.dot(a, b, precision=HI, preferred_element_type=F32)


def _hi_nt(a, b):
    return lax.dot_general(a, b, (((1,), (1,)), ((), ())), precision=HI, preferred_element_type=F32)


def _hi_tn(a, b):
    return lax.dot_general(a, b, (((0,), (0,)), ((), ())), precision=HI, preferred_element_type=F32)


def _sigmoid(x):
    return 1.0 / (1.0 + jnp.exp(-x))


def _softplus(x):
    return jnp.maximum(x, 0.0) + jnp.log(1.0 + jnp.exp(-jnp.abs(x)))


def mm_nn(a, b, name, res=None, out_dtype=F32, tm=1024):
    m, k = a.shape
    _, n = b.shape
    tn = _pick(n, 1536)
    tm = min(tm, m)

    def body(*refs):
        a_ref, b_ref = refs[0], refs[1]
        o_ref = refs[-1]
        acc = _mx(a_ref[...], b_ref[...])
        if res is not None:
            acc = acc + refs[2][...]
        o_ref[...] = acc.astype(o_ref.dtype)

    in_specs = [pl.BlockSpec((tm, k), lambda j, i: (i, 0)), pl.BlockSpec((k, tn), lambda j, i: (0, j))]
    args = [a, b]
    if res is not None:
        in_specs.append(pl.BlockSpec((tm, tn), lambda j, i: (i, j)))
        args.append(res)
    return pl.pallas_call(
        body, name=name, grid=(n // tn, m // tm), in_specs=in_specs,
        out_specs=pl.BlockSpec((tm, tn), lambda j, i: (i, j)),
        out_shape=jax.ShapeDtypeStruct((m, n), out_dtype), compiler_params=_cp("parallel", "parallel"))(*args)


def mm_nt(a, b, name, out_dtype=F32, tm=1024, after=None):
    m, k = a.shape
    n, _ = b.shape
    tn = _pick(n, 512 if k > 3000 else 1536)
    tm = min(tm, m)

    def body(a_ref, b_ref, *rest):
        o_ref = rest[-1]
        o_ref[...] = _mx_nt(a_ref[...], b_ref[...]).astype(o_ref.dtype)

    in_specs = [pl.BlockSpec((tm, k), lambda j, i: (i, 0)), pl.BlockSpec((tn, k), lambda j, i: (j, 0))]
    args = [a, b]
    if after is not None:
        in_specs.append(pl.BlockSpec(memory_space=pl.ANY))
        args.append(after)
    return pl.pallas_call(
        body, name=name, grid=(n // tn, m // tm), in_specs=in_specs,
        out_specs=pl.BlockSpec((tm, tn), lambda j, i: (i, j)),
        out_shape=jax.ShapeDtypeStruct((m, n), out_dtype), compiler_params=_cp("parallel", "parallel"))(*args)


def mm_at(at, b, name, tk=1024):
    m, kk = at.shape
    _, n = b.shape
    tm, tn, tk = _pick(m, 1408), _pick(n, 1408), min(tk, kk)
    nk = kk // tk

    def body(a_ref, b_ref, o_ref, acc_ref):
        k = pl.program_id(2)
        p = _mx(a_ref[...], b_ref[...])
        acc = jnp.where(k == 0, p, acc_ref[...] + p)
        acc_ref[...] = acc

        @pl.when(k == nk - 1)
        def _():
            o_ref[...] = acc.astype(o_ref.dtype)

    return pl.pallas_call(
        body, name=name, grid=(m // tm, n // tn, nk),
        in_specs=[pl.BlockSpec((tm, tk), lambda i, j, k: (i, k)), pl.BlockSpec((tk, tn), lambda i, j, k: (k, j))],
        out_specs=pl.BlockSpec((tm, tn), lambda i, j, k: (i, j)),
        out_shape=jax.ShapeDtypeStruct((m, n), MXU_DTYPE), scratch_shapes=[pltpu.VMEM((tm, tn), F32)],
        compiler_params=_cp("parallel", "parallel", "arbitrary"))(at, b)


def rms_fwd(x, g, name, tm=512, after=None):
    t, d = x.shape

    def body(x_ref, g_ref, *rest):
        o_ref, ot_ref = rest[-2:]
        xv = x_ref[...]
        r = lax.rsqrt(jnp.mean(xv * xv, axis=-1, keepdims=True) + EPS)
        h = xv * r * g_ref[...]
        o_ref[...] = h.astype(o_ref.dtype)
        ot_ref[...] = h.T.astype(ot_ref.dtype)

    in_specs = [pl.BlockSpec((tm, d), lambda i: (i, 0)), pl.BlockSpec((1, d), lambda i: (0, 0))]
    args = [x, g]
    if after is not None:
        in_specs.append(pl.BlockSpec(memory_space=pl.ANY))
        args.append(after)
    return pl.pallas_call(
        body, name=name, grid=(t // tm,), in_specs=in_specs,
        out_specs=(pl.BlockSpec((tm, d), lambda i: (i, 0)), pl.BlockSpec((d, tm), lambda i: (0, i))),
        out_shape=(jax.ShapeDtypeStruct((t, d), MXU_DTYPE), jax.ShapeDtypeStruct((d, t), MXU_DTYPE)),
        compiler_params=_cp("parallel"))(*args)


def rms_bwd(x, g, dh, dres, name, tm=512):
    t, d = x.shape

    def body(x_ref, g_ref, dh_ref, dres_ref, dx_ref, dg_ref):
        i = pl.program_id(0)
        xv = x_ref[...]
        r = lax.rsqrt(jnp.mean(xv * xv, axis=-1, keepdims=True) + EPS)
        xh = xv * r
        dhv = dh_ref[...]
        dxh = dhv * g_ref[...]
        dx_ref[...] = dres_ref[...] + r * (dxh - xh * jnp.mean(dxh * xh, axis=-1, keepdims=True))
        part = jnp.sum(dhv * xh, axis=0, keepdims=True)

        @pl.when(i == 0)
        def _():
            dg_ref[...] = part

        @pl.when(i > 0)
        def _():
            dg_ref[...] += part

    row = pl.BlockSpec((tm, d), lambda i: (i, 0))
    one = pl.BlockSpec((1, d), lambda i: (0, 0))
    return pl.pallas_call(
        body, name=name, grid=(t // tm,), in_specs=[row, one, row, row], out_specs=(row, one),
        out_shape=(jax.ShapeDtypeStruct((t, d), F32), jax.ShapeDtypeStruct((1, d), F32)),
        compiler_params=_cp("arbitrary"))(x, g, dh, dres)


GU_TILE = 1408


_GU_PER_TILE = GU_TILE * N_DEV // (2 * D_FF)


def _gu_gathered(g):
    _, k, c = g.shape
    nj = N_DEV // (2 * _GU_PER_TILE)
    return g.reshape(2, nj, _GU_PER_TILE, k, c).transpose(3, 1, 0, 2, 4).reshape(k, N_DEV * c)


def _gu_pieces(dw):
    k, n8 = dw.shape
    nj = N_DEV // (2 * _GU_PER_TILE)
    return dw.reshape(k, nj, 2, _GU_PER_TILE, n8 // N_DEV).transpose(2, 1, 3, 0, 4).reshape(N_DEV, k, n8 // N_DEV)


def ffn_up(f, w, name, tm=512):
    t, d = f.shape

    def body(f_ref, w_ref, gu_ref, a_ref, at_ref):
        gu = _mx(f_ref[...], w_ref[...])
        gu_ref[...] = gu
        g, u = gu[:, :GU_TILE], gu[:, GU_TILE:]
        act = g * _sigmoid(g) * u
        a_ref[...] = act.astype(a_ref.dtype)
        at_ref[...] = act.T.astype(at_ref.dtype)

    return pl.pallas_call(
        body, name=name, grid=(D_FF // GU_TILE, t // tm),
        in_specs=[pl.BlockSpec((tm, d), lambda j, i: (i, 0)), pl.BlockSpec((d, 2 * GU_TILE), lambda j, i: (0, j))],
        out_specs=(pl.BlockSpec((tm, 2 * GU_TILE), lambda j, i: (i, j)), pl.BlockSpec((tm, GU_TILE), lambda j, i: (i, j)),
                   pl.BlockSpec((GU_TILE, tm), lambda j, i: (j, i))),
        out_shape=(jax.ShapeDtypeStruct((t, 2 * D_FF), F32), jax.ShapeDtypeStruct((t, D_FF), MXU_DTYPE),
                   jax.ShapeDtypeStruct((D_FF, t), MXU_DTYPE)),
        compiler_params=_cp("parallel", "parallel"))(f, w)


def ffn_dact(dy, w_d, gu, name, tm=512, after=None):
    t, d = dy.shape

    def body(dy_ref, w_ref, gu_ref, *rest):
        o_ref = rest[-1]
        da = _mx_nt(dy_ref[...], w_ref[...])
        g, u = gu_ref[:, :GU_TILE], gu_ref[:, GU_TILE:]
        sg = _sigmoid(g)
        o_ref[:, :GU_TILE] = (da * u * sg * (1.0 + g * (1.0 - sg))).astype(o_ref.dtype)
        o_ref[:, GU_TILE:] = (da * g * sg).astype(o_ref.dtype)

    in_specs = [pl.BlockSpec((tm, d), lambda j, i: (i, 0)), pl.BlockSpec((GU_TILE, d), lambda j, i: (j, 0)),
                pl.BlockSpec((tm, 2 * GU_TILE), lambda j, i: (i, j))]
    args = [dy, w_d, gu]
    if after is not None:
        in_specs.append(pl.BlockSpec(memory_space=pl.ANY))
        args.append(after)
    return pl.pallas_call(
        body, name=name, grid=(D_FF // GU_TILE, t // tm), in_specs=in_specs,
        out_specs=pl.BlockSpec((tm, 2 * GU_TILE), lambda j, i: (i, j)),
        out_shape=jax.ShapeDtypeStruct((t, 2 * D_FF), MXU_DTYPE), compiler_params=_cp("parallel", "parallel"))(*args)


def loss_head(y, target, name, tm=512):
    t, d = y.shape

    def body(y_ref, t_ref, dy_ref, l_ref):
        i = pl.program_id(0)
        e = y_ref[...] - t_ref[...]
        dy_ref[...] = e * (1.0 / d)
        part = jnp.zeros((1, 128), F32) + 0.5 * jnp.sum(jnp.mean(e * e, axis=-1, keepdims=True), axis=0, keepdims=True)

        @pl.when(i == 0)
        def _():
            l_ref[...] = part

        @pl.when(i > 0)
        def _():
            l_ref[...] += part

    row = pl.BlockSpec((tm, d), lambda i: (i, 0))
    return pl.pallas_call(
        body, name=name, grid=(t // tm,), in_specs=[row, row],
        out_specs=(row, pl.BlockSpec((1, 128), lambda i: (0, 0))),
        out_shape=(jax.ShapeDtypeStruct((t, d), F32), jax.ShapeDtypeStruct((1, 128), F32)),
        compiler_params=_cp("arbitrary"))(y, target)


def _rot(x):
    return jnp.concatenate([-x[:, HEAD_DIM // 2:], x[:, :HEAD_DIM // 2]], axis=-1)


def _rot_t(y):
    return jnp.concatenate([y[:, HEAD_DIM // 2:], -y[:, :HEAD_DIM // 2]], axis=-1)


def qk_prep_fwd(proj, q_gain, k_gain, cosf, sinf, name, tm=256):
    t = proj.shape[0]
    nh = ATTN_HEADS + KV_HEADS

    def body(p_ref, qg_ref, kg_ref, c_ref, s_ref, q_ref, k_ref):
        c, s = c_ref[...], s_ref[...]
        outs = []
        for h in range(nh):
            xh = p_ref[:, h * HEAD_DIM:(h + 1) * HEAD_DIM]
            gain = qg_ref[...] if h < ATTN_HEADS else kg_ref[...]
            r = lax.rsqrt(jnp.mean(xh * xh, axis=-1, keepdims=True) + EPS)
            xn = xh * r * gain
            outs.append(xn * c + _rot(xn) * s)
        q_ref[...] = jnp.concatenate(outs[:ATTN_HEADS], axis=-1)
        k_ref[...] = jnp.concatenate(outs[ATTN_HEADS:], axis=-1)

    gspec = pl.BlockSpec((1, HEAD_DIM), lambda i: (0, 0))
    tspec = pl.BlockSpec((tm, HEAD_DIM), lambda i: (i, 0))
    return pl.pallas_call(
        body, name=name, grid=(t // tm,),
        in_specs=[pl.BlockSpec((tm, Q_W + KV_W), lambda i: (i, 0)), gspec, gspec, tspec, tspec],
        out_specs=(pl.BlockSpec((tm, Q_W), lambda i: (i, 0)), pl.BlockSpec((tm, KV_W), lambda i: (i, 0))),
        out_shape=(jax.ShapeDtypeStruct((t, Q_W), F32), jax.ShapeDtypeStruct((t, KV_W), F32)),
        compiler_params=_cp("parallel"))(proj, q_gain, k_gain, cosf, sinf)


def qk_prep_bwd(proj, q_gain, k_gain, cosf, sinf, dq, dk, name, tm=256):
    t = proj.shape[0]
    nh = ATTN_HEADS + KV_HEADS

    def body(p_ref, qg_ref, kg_ref, c_ref, s_ref, dq_ref, dk_ref, o_ref, dqg_ref, dkg_ref):
        i = pl.program_id(0)
        c, s = c_ref[...], s_ref[...]
        outs = []
        dqg = jnp.zeros((1, HEAD_DIM), F32)
        dkg = jnp.zeros((1, HEAD_DIM), F32)
        for h in range(nh):
            xh = p_ref[:, h * HEAD_DIM:(h + 1) * HEAD_DIM]
            if h < ATTN_HEADS:
                gain = qg_ref[...]
                dout = dq_ref[:, h * HEAD_DIM:(h + 1) * HEAD_DIM]
            else:
                gain = kg_ref[...]
                dout = dk_ref[:, (h - ATTN_HEADS) * HEAD_DIM:(h - ATTN_HEADS + 1) * HEAD_DIM]
            r = lax.rsqrt(jnp.mean(xh * xh, axis=-1, keepdims=True) + EPS)
            xhat = xh * r
            dxn = dout * c + _rot_t(dout * s)
            part = jnp.sum(dxn * xhat, axis=0, keepdims=True)
            if h < ATTN_HEADS:
                dqg = dqg + part
            else:
                dkg = dkg + part
            dxh = dxn * gain
            outs.append(r * (dxh - xhat * jnp.mean(dxh * xhat, axis=-1, keepdims=True)))
        o_ref[...] = jnp.concatenate(outs, axis=-1).astype(o_ref.dtype)

        @pl.when(i == 0)
        def _():
            dqg_ref[...] = dqg
            dkg_ref[...] = dkg

        @pl.when(i > 0)
        def _():
            dqg_ref[...] += dqg
            dkg_ref[...] += dkg

    gspec = pl.BlockSpec((1, HEAD_DIM), lambda i: (0, 0))
    tspec = pl.BlockSpec((tm, HEAD_DIM), lambda i: (i, 0))
    return pl.pallas_call(
        body, name=name, grid=(t // tm,),
        in_specs=[pl.BlockSpec((tm, Q_W + KV_W), lambda i: (i, 0)), gspec, gspec, tspec, tspec,
                  pl.BlockSpec((tm, Q_W), lambda i: (i, 0)), pl.BlockSpec((tm, KV_W), lambda i: (i, 0))],
        out_specs=(pl.BlockSpec((tm, Q_W + KV_W), lambda i: (i, 0)), gspec, gspec),
        out_shape=(jax.ShapeDtypeStruct((t, Q_W + KV_W), MXU_DTYPE), jax.ShapeDtypeStruct((1, HEAD_DIM), F32),
                   jax.ShapeDtypeStruct((1, HEAD_DIM), F32)),
        compiler_params=_cp("arbitrary"))(proj, q_gain, k_gain, cosf, sinf, dq, dk)


def _swa_valid(n, grp):
    qi = lax.broadcasted_iota(jnp.int32, (grp * ATTN_BLOCK, 2 * ATTN_BLOCK), 0) & (ATTN_BLOCK - 1)
    kj = lax.broadcasted_iota(jnp.int32, (grp * ATTN_BLOCK, 2 * ATTN_BLOCK), 1)
    diff = qi + ATTN_BLOCK - kj
    return (diff >= 0) & (diff < ATTN_BLOCK) & (n * ATTN_BLOCK - ATTN_BLOCK + kj >= 0)


def _stack_heads(ref, g, grp):
    return jnp.concatenate([ref[:, (g * grp + j) * HEAD_DIM:(g * grp + j + 1) * HEAD_DIM] for j in range(grp)], axis=0)


def _stack_sinks(s_ref, g, grp):
    return jnp.concatenate([jnp.zeros((ATTN_BLOCK, 1), F32) + s_ref[0:1, g * grp + j:g * grp + j + 1]
                            for j in range(grp)], axis=0)


def swa_fwd(q, k, proj, sinks, name):
    t = q.shape[0]
    nb = t // ATTN_BLOCK
    scale = HEAD_DIM ** -0.5
    grp = ATTN_HEADS // KV_HEADS

    def body(q_ref, kc_ref, kp_ref, vc_ref, vp_ref, s_ref, y_ref, lse_ref):
        n = pl.program_id(0)
        valid = _swa_valid(n, grp)
        kk = jnp.concatenate([kp_ref[...], kc_ref[...]], axis=0).astype(MXU_DTYPE)
        vv = jnp.concatenate([vp_ref[...], vc_ref[...]], axis=0).astype(MXU_DTYPE)
        lane = lax.broadcasted_iota(jnp.int32, (ATTN_BLOCK, ATTN_HEADS), 1)
        gs = range(KV_HEADS)
        qg = [_stack_heads(q_ref, g, grp) for g in gs]
        sink = [_stack_sinks(s_ref, g, grp) for g in gs]
        sc = [jnp.where(valid, _mx_nt(qg[g], kk[:, g * HEAD_DIM:(g + 1) * HEAD_DIM]) * scale, NEG) for g in gs]
        m = [jnp.maximum(jnp.max(sc[g], axis=-1, keepdims=True), sink[g]) for g in gs]
        e = [jnp.exp(sc[g] - m[g]) for g in gs]
        den = [jnp.sum(e[g], axis=-1, keepdims=True) + jnp.exp(sink[g] - m[g]) for g in gs]
        og = [_mx(e[g] / den[g], vv[:, g * HEAD_DIM:(g + 1) * HEAD_DIM]) for g in gs]
        lg = [m[g] + jnp.log(den[g]) for g in gs]
        lse = jnp.zeros((ATTN_BLOCK, ATTN_HEADS), F32)
        outs = []
        for h in range(ATTN_HEADS):
            rows = slice((h % grp) * ATTN_BLOCK, (h % grp + 1) * ATTN_BLOCK)
            outs.append(og[h // grp][rows])
            lse = jnp.where(lane == h, lg[h // grp][rows], lse)
        y_ref[...] = jnp.concatenate(outs, axis=-1)
        lse_ref[...] = lse

    cur = lambda n: (n, 0)
    prev = lambda n: (jnp.maximum(n - 1, 0), 0)
    vcol = (Q_W + KV_W) // KV_W
    return pl.pallas_call(
        body, name=name, grid=(nb,),
        in_specs=[pl.BlockSpec((ATTN_BLOCK, Q_W), cur), pl.BlockSpec((ATTN_BLOCK, KV_W), cur),
                  pl.BlockSpec((ATTN_BLOCK, KV_W), prev),
                  pl.BlockSpec((ATTN_BLOCK, KV_W), lambda n: (n, vcol)),
                  pl.BlockSpec((ATTN_BLOCK, KV_W), lambda n: (jnp.maximum(n - 1, 0), vcol)),
                  pl.BlockSpec((1, ATTN_HEADS), lambda n: (0, 0))],
        out_specs=(pl.BlockSpec((ATTN_BLOCK, Q_W), cur), pl.BlockSpec((ATTN_BLOCK, ATTN_HEADS), cur)),
        out_shape=(jax.ShapeDtypeStruct((t, Q_W), F32), jax.ShapeDtypeStruct((t, ATTN_HEADS), F32)),
        compiler_params=_cp("parallel"))(q, k, k, proj, proj, sinks)


def swa_bwd(q, k, proj, sinks, y, lse, dmix, name):
    t = q.shape[0]
    nb = t // ATTN_BLOCK
    scale = HEAD_DIM ** -0.5
    grp = ATTN_HEADS // KV_HEADS

    def body(q_ref, kc_ref, kp_ref, vc_ref, vp_ref, s_ref, y_ref, lse_ref, dy_ref,
             dq_ref, dk_ref, dv_ref, ds_ref, dkc, dvc):
        n = pl.program_id(0)

        @pl.when(n == 0)
        def _():
            dkc[...] = jnp.zeros_like(dkc)
            dvc[...] = jnp.zeros_like(dvc)
            ds_ref[...] = jnp.zeros_like(ds_ref)

        @pl.when(n < nb)
        def _():
            valid = _swa_valid(n, grp)
            kk = jnp.concatenate([kp_ref[...], kc_ref[...]], axis=0).astype(MXU_DTYPE)
            vv = jnp.concatenate([vp_ref[...], vc_ref[...]], axis=0).astype(MXU_DTYPE)
            lane = lax.broadcasted_iota(jnp.int32, (1, ATTN_HEADS), 1)
            gs = range(KV_HEADS)
            kg = [kk[:, g * HEAD_DIM:(g + 1) * HEAD_DIM] for g in gs]
            vg = [vv[:, g * HEAD_DIM:(g + 1) * HEAD_DIM] for g in gs]
            qg = [_stack_heads(q_ref, g, grp).astype(MXU_DTYPE) for g in gs]
            dog = [_stack_heads(dy_ref, g, grp) for g in gs]
            og = [_stack_heads(y_ref, g, grp) for g in gs]
            lg = [jnp.concatenate([lse_ref[:, g * grp + j:g * grp + j + 1] for j in range(grp)], axis=0) for g in gs]
            sink = [_stack_sinks(s_ref, g, grp) for g in gs]
            sc = [jnp.where(valid, _mx_nt(qg[g], kg[g]) * scale, NEG) for g in gs]
            p = [jnp.exp(sc[g] - lg[g]) for g in gs]
            delta = [jnp.sum(dog[g] * og[g], axis=-1, keepdims=True) for g in gs]
            ds = [p[g] * (_mx_nt(dog[g], vg[g]) - delta[g]) for g in gs]
            dqg = [_mx(ds[g], kg[g]) * scale for g in gs]
            dkf = jnp.concatenate([_mx_tn(ds[g], qg[g]) * scale for g in gs], axis=-1)
            dvf = jnp.concatenate([_mx_tn(p[g], dog[g]) for g in gs], axis=-1)
            dsk = [jnp.exp(sink[g] - lg[g]) * delta[g] for g in gs]
            dsink = jnp.zeros((1, ATTN_HEADS), F32)
            dqs = []
            for h in range(ATTN_HEADS):
                rows = slice((h % grp) * ATTN_BLOCK, (h % grp + 1) * ATTN_BLOCK)
                dqs.append(dqg[h // grp][rows])
                dsink = jnp.where(lane == h, -jnp.sum(dsk[h // grp][rows], axis=0, keepdims=True), dsink)
            dq_ref[...] = jnp.concatenate(dqs, axis=-1)
            dk_ref[...] = dkc[...] + dkf[:ATTN_BLOCK]
            dv_ref[...] = (dvc[...] + dvf[:ATTN_BLOCK]).astype(dv_ref.dtype)
            dkc[...] = dkf[ATTN_BLOCK:]
            dvc[...] = dvf[ATTN_BLOCK:]
            ds_ref[...] += dsink

        @pl.when(n == nb)
        def _():
            dk_ref[...] = dkc[...]
            dv_ref[...] = dvc[...].astype(dv_ref.dtype)

    cur = lambda n: (jnp.minimum(n, nb - 1), 0)
    prev = lambda n: (jnp.clip(n - 1, 0, nb - 1), 0)
    vcol = (Q_W + KV_W) // KV_W
    return pl.pallas_call(
        body, name=name, grid=(nb + 1,),
        in_specs=[pl.BlockSpec((ATTN_BLOCK, Q_W), cur), pl.BlockSpec((ATTN_BLOCK, KV_W), cur),
                  pl.BlockSpec((ATTN_BLOCK, KV_W), prev),
                  pl.BlockSpec((ATTN_BLOCK, KV_W), lambda n: (jnp.minimum(n, nb - 1), vcol)),
                  pl.BlockSpec((ATTN_BLOCK, KV_W), lambda n: (jnp.clip(n - 1, 0, nb - 1), vcol)),
                  pl.BlockSpec((1, ATTN_HEADS), lambda n: (0, 0)),
                  pl.BlockSpec((ATTN_BLOCK, Q_W), cur), pl.BlockSpec((ATTN_BLOCK, ATTN_HEADS), cur),
                  pl.BlockSpec((ATTN_BLOCK, Q_W), cur)],
        out_specs=(pl.BlockSpec((ATTN_BLOCK, Q_W), cur), pl.BlockSpec((ATTN_BLOCK, KV_W), prev),
                   pl.BlockSpec((ATTN_BLOCK, KV_W), prev), pl.BlockSpec((1, ATTN_HEADS), lambda n: (0, 0))),
        out_shape=(jax.ShapeDtypeStruct((t, Q_W), F32), jax.ShapeDtypeStruct((t, KV_W), F32),
                   jax.ShapeDtypeStruct((t, KV_W), MXU_DTYPE), jax.ShapeDtypeStruct((1, ATTN_HEADS), F32)),
        scratch_shapes=[pltpu.VMEM((ATTN_BLOCK, KV_W), F32), pltpu.VMEM((ATTN_BLOCK, KV_W), F32)],
        compiler_params=_cp("arbitrary"))(q, k, k, proj, proj, sinks, y, lse, dmix)


GC_W = 256
_GB0, _GC0, _XI0 = 768 // GC_W, 1280 // GC_W, 1792 // GC_W
HALO = 8


def gconv_fwd(proj, conv_w, name, tm=512):
    t = proj.shape[0]
    hb = tm // HALO

    def body(gb_ref, gc_ref, xi_ref, gch_ref, xih_ref, w_ref, y_ref):
        i = pl.program_id(1)
        u = gc_ref[...] * xi_ref[...]
        uh = jnp.where(i == 0, 0.0, gch_ref[...] * xih_ref[...])
        up = jnp.concatenate([uh, u], axis=0)
        cv = w_ref[0:1, :] * up[HALO - 2:HALO - 2 + tm]
        cv = cv + w_ref[1:2, :] * up[HALO - 1:HALO - 1 + tm]
        cv = cv + w_ref[2:3, :] * u
        y_ref[...] = (gb_ref[...] * cv).astype(y_ref.dtype)

    def col(c0):
        return pl.BlockSpec((tm, GC_W), lambda cj, i: (i, c0 + cj))

    def halo(c0):
        return pl.BlockSpec((HALO, GC_W), lambda cj, i: (jnp.maximum(i * hb - 1, 0), c0 + cj))

    return pl.pallas_call(
        body, name=name, grid=(CONV_CH // GC_W, t // tm),
        in_specs=[col(_GB0), col(_GC0), col(_XI0), halo(_GC0), halo(_XI0),
                  pl.BlockSpec((3, GC_W), lambda cj, i: (0, cj))],
        out_specs=pl.BlockSpec((tm, GC_W), lambda cj, i: (i, cj)),
        out_shape=jax.ShapeDtypeStruct((t, CONV_CH), MXU_DTYPE),
        compiler_params=_cp("parallel", "parallel"))(proj, proj, proj, proj, proj, conv_w)


def gconv_bwd(proj, conv_w, dmix, name, tm=512):
    t = proj.shape[0]
    hb = tm // HALO
    nt = t // tm
    dy0 = Q_W // GC_W

    def body(gb_ref, gc_ref, xi_ref, gch_ref, xih_ref, gbn_ref, dyn_ref, dy_ref, w_ref,
             dgb_ref, dgc_ref, dxi_ref, dw_ref):
        i = pl.program_id(1)
        gc, xi, gb, dy = gc_ref[...], xi_ref[...], gb_ref[...], dy_ref[...]
        u = gc * xi
        uh = jnp.where(i == 0, 0.0, gch_ref[...] * xih_ref[...])
        up = jnp.concatenate([uh, u], axis=0)
        u2 = up[HALO - 2:HALO - 2 + tm]
        u1 = up[HALO - 1:HALO - 1 + tm]
        cv = w_ref[0:1, :] * u2 + w_ref[1:2, :] * u1 + w_ref[2:3, :] * u
        dgb_ref[...] = (dy * cv).astype(dgb_ref.dtype)
        dcv = dy * gb
        dcvn = jnp.where(i == nt - 1, 0.0, dyn_ref[...] * gbn_ref[...])
        dcvp = jnp.concatenate([dcv, dcvn], axis=0)
        du = w_ref[0:1, :] * dcvp[2:2 + tm] + w_ref[1:2, :] * dcvp[1:1 + tm] + w_ref[2:3, :] * dcv
        dgc_ref[...] = (du * xi).astype(dgc_ref.dtype)
        dxi_ref[...] = (du * gc).astype(dxi_ref.dtype)
        dw = jnp.concatenate([jnp.sum(dcv * u2, axis=0, keepdims=True), jnp.sum(dcv * u1, axis=0, keepdims=True),
                              jnp.sum(dcv * u, axis=0, keepdims=True)], axis=0)

        @pl.when(i == 0)
        def _():
            dw_ref[...] = dw

        @pl.when(i > 0)
        def _():
            dw_ref[...] += dw

    def col(c0):
        return pl.BlockSpec((tm, GC_W), lambda cj, i: (i, c0 + cj))

    def halo(c0):
        return pl.BlockSpec((HALO, GC_W), lambda cj, i: (jnp.maximum(i * hb - 1, 0), c0 + cj))

    def nxt(c0):
        return pl.BlockSpec((HALO, GC_W), lambda cj, i: (jnp.minimum((i + 1) * hb, t // HALO - 1), c0 + cj))

    out = pl.BlockSpec((tm, GC_W), lambda cj, i: (i, cj))
    return pl.pallas_call(
        body, name=name, grid=(CONV_CH // GC_W, nt),
        in_specs=[col(_GB0), col(_GC0), col(_XI0), halo(_GC0), halo(_XI0), nxt(_GB0), nxt(dy0), col(dy0),
                  pl.BlockSpec((3, GC_W), lambda cj, i: (0, cj))],
        out_specs=(out, out, out, pl.BlockSpec((3, GC_W), lambda cj, i: (0, cj))),
        out_shape=(jax.ShapeDtypeStruct((t, CONV_CH), MXU_DTYPE),) * 3 + (jax.ShapeDtypeStruct((3, CONV_CH), F32),),
        compiler_params=_cp("parallel", "arbitrary"))(proj, proj, proj, proj, proj, proj, dmix, dmix, conv_w)


_QKV_W = 3 * DN_W
_BA_COL = (4 * DN_W) // 128
_Z_COL = _QKV_W // DN_W


def gdn_prep_fwd(proj, conv_w, alog_row, dtb_row, name, tm=256):
    t = proj.shape[0]
    hb = tm // HALO
    qscale = DN_DIM ** -0.5

    def body(x_ref, xh_ref, w_ref, ba_ref, al_ref, dt_ref, q_ref, k_ref, v_ref, bg_ref):
        i = pl.program_id(0)
        for gi in range(3 * DN_HEADS):
            sl = slice(gi * DN_DIM, (gi + 1) * DN_DIM)
            xp = jnp.concatenate([jnp.where(i == 0, 0.0, xh_ref[:, sl]), x_ref[:, sl]], axis=0)
            c = w_ref[0:1, sl] * xp[HALO - 3:HALO - 3 + tm]
            for j in range(1, 4):
                c = c + w_ref[j:j + 1, sl] * xp[HALO - 3 + j:HALO - 3 + j + tm]
            s = c * _sigmoid(c)
            osl = slice((gi % DN_HEADS) * DN_DIM, (gi % DN_HEADS + 1) * DN_DIM)
            if gi < DN_HEADS:
                q_ref[:, osl] = s * lax.rsqrt(jnp.sum(s * s, axis=-1, keepdims=True) + EPS) * qscale
            elif gi < 2 * DN_HEADS:
                k_ref[:, osl] = s * lax.rsqrt(jnp.sum(s * s, axis=-1, keepdims=True) + EPS)
            else:
                v_ref[:, osl] = s
        ba = ba_ref[...]
        lane = lax.broadcasted_iota(jnp.int32, ba.shape, 1)
        gval = -jnp.exp(al_ref[...]) * _softplus(ba + dt_ref[...])
        bg_ref[...] = jnp.where(lane < DN_HEADS, _sigmoid(ba), jnp.where(lane < 2 * DN_HEADS, gval, 0.0))

    row = pl.BlockSpec((tm, DN_W), lambda i: (i, 0))
    one = pl.BlockSpec((1, 128), lambda i: (0, 0))
    return pl.pallas_call(
        body, name=name, grid=(t // tm,),
        in_specs=[pl.BlockSpec((tm, _QKV_W), lambda i: (i, 0)),
                  pl.BlockSpec((HALO, _QKV_W), lambda i: (jnp.maximum(i * hb - 1, 0), 0)),
                  pl.BlockSpec((4, _QKV_W), lambda i: (0, 0)),
                  pl.BlockSpec((tm, 128), lambda i: (i, _BA_COL)), one, one],
        out_specs=(row, row, row, pl.BlockSpec((tm, 128), lambda i: (i, 0))),
        out_shape=(jax.ShapeDtypeStruct((t, DN_W), F32),) * 3 + (jax.ShapeDtypeStruct((t, 128), F32),),
        compiler_params=_cp("parallel"))(proj, proj, conv_w, proj, alog_row, dtb_row)


def gdn_prep_bwd(proj, conv_w, alog_row, dtb_row, dq, dk, dv, dbg, name, tm=256):
    t = proj.shape[0]
    hb = tm // HALO
    nt = t // tm
    qscale = DN_DIM ** -0.5
    te = tm + HALO

    def body(x_ref, xh_ref, xn_ref, w_ref, ba_ref, al_ref, dt_ref, dq_ref, dk_ref, dv_ref,
             dqn_ref, dkn_ref, dvn_ref, dbg_ref, dx_ref, dba_ref, dw_ref, ddt_ref, dal_ref):
        i = pl.program_id(0)
        first = i == 0
        last = i == nt - 1
        dws = []
        for gi in range(3 * DN_HEADS):
            sl = slice(gi * DN_DIM, (gi + 1) * DN_DIM)
            osl = slice((gi % DN_HEADS) * DN_DIM, (gi % DN_HEADS + 1) * DN_DIM)
            xe = jnp.concatenate([jnp.where(first, 0.0, xh_ref[:, sl]), x_ref[:, sl], xn_ref[:, sl]], axis=0)
            c = w_ref[0:1, sl] * xe[HALO - 3:HALO - 3 + te]
            for j in range(1, 4):
                c = c + w_ref[j:j + 1, sl] * xe[HALO - 3 + j:HALO - 3 + j + te]
            sg = _sigmoid(c)
            s = c * sg
            d_ref, dn_ref = ((dq_ref, dqn_ref), (dk_ref, dkn_ref), (dv_ref, dvn_ref))[gi // DN_HEADS]
            dy = jnp.concatenate([d_ref[:, osl], jnp.where(last, 0.0, dn_ref[:, osl])], axis=0)
            if gi < 2 * DN_HEADS:
                r = lax.rsqrt(jnp.sum(s * s, axis=-1, keepdims=True) + EPS)
                sh = s * r
                ds = r * (dy - sh * jnp.sum(sh * dy, axis=-1, keepdims=True))
                if gi < DN_HEADS:
                    ds = ds * qscale
            else:
                ds = dy
            dc = ds * sg * (1.0 + c * (1.0 - sg))
            dcs = [dc[3 - j:3 - j + tm] for j in range(4)]
            dx = w_ref[0:1, sl] * dcs[0]
            for j in range(1, 4):
                dx = dx + w_ref[j:j + 1, sl] * dcs[j]
            dx_ref[:, sl] = dx.astype(dx_ref.dtype)
            x0 = x_ref[:, sl]
            dws.append(jnp.concatenate([jnp.sum(dcs[j] * x0, axis=0, keepdims=True) for j in range(4)], axis=0))
        dw = jnp.concatenate(dws, axis=-1)
        ba = ba_ref[...]
        dbgv = dbg_ref[...]
        lane = lax.broadcasted_iota(jnp.int32, ba.shape, 1)
        beta = _sigmoid(ba)
        ea = -jnp.exp(al_ref[...])
        zin = ba + dt_ref[...]
        is_b = lane < DN_HEADS
        is_a = (lane >= DN_HEADS) & (lane < 2 * DN_HEADS)
        da = jnp.where(is_a, dbgv * ea * _sigmoid(zin), 0.0)
        dba_ref[...] = jnp.where(is_b, dbgv * beta * (1.0 - beta), da).astype(dba_ref.dtype)
        ddt = jnp.sum(da, axis=0, keepdims=True)
        dal = jnp.sum(jnp.where(is_a, dbgv * ea * _softplus(zin), 0.0), axis=0, keepdims=True)

        @pl.when(first)
        def _():
            dw_ref[...] = dw
            ddt_ref[...] = ddt
            dal_ref[...] = dal

        @pl.when(i > 0)
        def _():
            dw_ref[...] += dw
            ddt_ref[...] += ddt
            dal_ref[...] += dal

    row = pl.BlockSpec((tm, DN_W), lambda i: (i, 0))
    nrow = pl.BlockSpec((HALO, DN_W), lambda i: (jnp.minimum((i + 1) * hb, t // HALO - 1), 0))
    one = pl.BlockSpec((1, 128), lambda i: (0, 0))
    return pl.pallas_call(
        body, name=name, grid=(nt,),
        in_specs=[pl.BlockSpec((tm, _QKV_W), lambda i: (i, 0)),
                  pl.BlockSpec((HALO, _QKV_W), lambda i: (jnp.maximum(i * hb - 1, 0), 0)),
                  pl.BlockSpec((HALO, _QKV_W), lambda i: (jnp.minimum((i + 1) * hb, t // HALO - 1), 0)),
                  pl.BlockSpec((4, _QKV_W), lambda i: (0, 0)),
                  pl.BlockSpec((tm, 128), lambda i: (i, _BA_COL)), one, one,
                  row, row, row, nrow, nrow, nrow, pl.BlockSpec((tm, 128), lambda i: (i, 0))],
        out_specs=(pl.BlockSpec((tm, _QKV_W), lambda i: (i, 0)), pl.BlockSpec((tm, 128), lambda i: (i, 0)),
                   pl.BlockSpec((4, _QKV_W), lambda i: (0, 0)), one, one),
        out_shape=(jax.ShapeDtypeStruct((t, _QKV_W), MXU_DTYPE), jax.ShapeDtypeStruct((t, 128), MXU_DTYPE),
                   jax.ShapeDtypeStruct((4, _QKV_W), F32), jax.ShapeDtypeStruct((1, 128), F32),
                   jax.ShapeDtypeStruct((1, 128), F32)),
        compiler_params=_cp("arbitrary"))(proj, proj, proj, conv_w, proj, alog_row, dtb_row, dq, dk, dv, dq, dk, dv, dbg)


def _chunk_masks():
    r = lax.broadcasted_iota(jnp.int32, (DN_CHUNK, DN_CHUNK), 0)
    c = lax.broadcasted_iota(jnp.int32, (DN_CHUNK, DN_CHUNK), 1)
    return r >= c, r > c


def _inv_unit_lower_many(mats):
    r = lax.broadcasted_iota(jnp.int32, mats[0].shape, 0)
    c = lax.broadcasted_iota(jnp.int32, mats[0].shape, 1)
    eye = jnp.where(r == c, 1.0, 0.0)
    xs = [eye - a for a in mats]
    pws = [_hi(a, a) for a in mats]
    for step in range(5):
        xs = [x + _hi(x, pw) for x, pw in zip(xs, pws)]
        if step < 4:
            pws = [_hi(pw, pw) for pw in pws]
    return xs


def _chunk_common(q, k, v, beta, gc, gcr, lower, strict):
    gam = jnp.exp(jnp.where(lower, gc - gcr, NEG))
    eg = jnp.exp(gc)
    gl = gc[DN_CHUNK - 1:DN_CHUNK, :]
    kdf = jnp.exp(gl - gc)
    kb = k * beta
    bmat = _mx_nt(kb, k)
    qmat = _mx_nt(q, k)
    return gam, eg, jnp.exp(gl), kdf, kb, bmat, qmat


def gdn_fwd(q, k, v, bg, name):
    t = q.shape[0]
    n_chunks = t // DN_CHUNK

    def body(q_ref, k_ref, v_ref, bg_ref, o_ref, sall_ref, tall_ref, s_ref):
        n = pl.program_id(0)

        @pl.when(n == 0)
        def _():
            s_ref[...] = jnp.zeros_like(s_ref)

        lower, strict = _chunk_masks()
        bgv = bg_ref[...]
        gcs = _hi(jnp.where(lower, 1.0, 0.0), bgv)
        gcs_t = gcs.T
        hs = range(DN_HEADS)
        sl = [slice(h * DN_DIM, (h + 1) * DN_DIM) for h in hs]
        st = [s_ref[h] for h in hs]
        for h in hs:
            sall_ref[0, h] = st[h]
        qh = [q_ref[:, sl[h]] for h in hs]
        kh = [k_ref[:, sl[h]] for h in hs]
        vh = [v_ref[:, sl[h]] for h in hs]
        beta = [bgv[:, h:h + 1] for h in hs]
        com = [_chunk_common(qh[h], kh[h], vh[h], beta[h], gcs[:, DN_HEADS + h:DN_HEADS + h + 1],
                             gcs_t[DN_HEADS + h:DN_HEADS + h + 1, :], lower, strict) for h in hs]
        gam, eg, dec, kdf, kb, bmat, qmat = zip(*com)
        tms = _inv_unit_lower_many([jnp.where(strict, bmat[h] * gam[h], 0.0) for h in hs])
        for h in hs:
            tall_ref[0, h] = tms[h]
        uw = [_hi(tms[h], jnp.concatenate([vh[h] * beta[h], kb[h] * eg[h]], axis=-1)) for h in hs]
        v_new = [uw[h][:, :DN_DIM] - _mx(uw[h][:, DN_DIM:], st[h]) for h in hs]
        o_st = [_mx(qh[h] * eg[h], st[h]) for h in hs]
        o_in = [_mx(qmat[h] * gam[h], v_new[h]) for h in hs]
        s_up = [_mx_tn(kh[h] * kdf[h], v_new[h]) for h in hs]
        for h in hs:
            o_ref[:, sl[h]] = o_st[h] + o_in[h]
            s_ref[h] = st[h] * dec[h] + s_up[h]

    row = pl.BlockSpec((DN_CHUNK, DN_W), lambda n: (n, 0))
    return pl.pallas_call(
        body, name=name, grid=(n_chunks,),
        in_specs=[row, row, row, pl.BlockSpec((DN_CHUNK, 128), lambda n: (n, 0))],
        out_specs=(row, pl.BlockSpec((1, DN_HEADS, DN_DIM, DN_DIM), lambda n: (n, 0, 0, 0)),
                   pl.BlockSpec((1, DN_HEADS, DN_CHUNK, DN_CHUNK), lambda n: (n, 0, 0, 0))),
        out_shape=(jax.ShapeDtypeStruct((t, DN_W), F32),
                   jax.ShapeDtypeStruct((n_chunks, DN_HEADS, DN_DIM, DN_DIM), F32),
                   jax.ShapeDtypeStruct((n_chunks, DN_HEADS, DN_CHUNK, DN_CHUNK), F32)),
        scratch_shapes=[pltpu.VMEM((DN_HEADS, DN_DIM, DN_DIM), F32)],
        compiler_params=_cp("arbitrary"))(q, k, v, bg)


def gdn_bwd(q, k, v, bg, sall, tall, do, name):
    t = q.shape[0]
    n_chunks = t // DN_CHUNK

    def body(q_ref, k_ref, v_ref, bg_ref, sall_ref, tall_ref, do_ref, dq_ref, dk_ref, dv_ref, dbg_ref, ds_ref):
        n = pl.program_id(0)

        @pl.when(n == 0)
        def _():
            ds_ref[...] = jnp.zeros_like(ds_ref)

        lower, strict = _chunk_masks()
        ltri = jnp.where(lower, 1.0, 0.0)
        bgv = bg_ref[...]
        gcs = _hi(ltri, bgv)
        gcs_t = gcs.T
        lane = lax.broadcasted_iota(jnp.int32, (DN_CHUNK, 128), 1)
        rowi = lax.broadcasted_iota(jnp.int32, (DN_CHUNK, 1), 0)
        hs = range(DN_HEADS)
        each = lambda fn, *ls: [fn(*a) for a in zip(*ls)]
        rsum = lambda a: jnp.sum(a, axis=-1, keepdims=True)
        sl = [slice(h * DN_DIM, (h + 1) * DN_DIM) for h in hs]
        st = [sall_ref[0, h] for h in hs]
        tms = [tall_ref[0, h] for h in hs]
        dsn = [ds_ref[h] for h in hs]
        qh = [q_ref[:, sl[h]] for h in hs]
        kh = [k_ref[:, sl[h]] for h in hs]
        vh = [v_ref[:, sl[h]] for h in hs]
        doh = [do_ref[:, sl[h]] for h in hs]
        beta = [bgv[:, h:h + 1] for h in hs]
        com = [_chunk_common(qh[h], kh[h], vh[h], beta[h], gcs[:, DN_HEADS + h:DN_HEADS + h + 1],
                             gcs_t[DN_HEADS + h:DN_HEADS + h + 1, :], lower, strict) for h in hs]
        gam, eg, dec, kdf, kb, bmat, qmat = zip(*com)
        rhs_w = each(lambda a, b: a * b, kb, eg)
        uw = each(lambda t_, v_, b_, r_: _hi(t_, jnp.concatenate([v_ * b_, r_], axis=-1)), tms, vh, beta, rhs_w)
        qd = each(lambda a, b: a * b, qh, eg)
        kd = each(lambda a, b: a * b, kh, kdf)
        pmat = each(lambda a, b: a * b, qmat, gam)
        v_new = each(lambda uw_, s_: uw_[:, :DN_DIM] - _mx(uw_[:, DN_DIM:], s_), uw, st)
        dqd = each(_mx_nt, doh, st)
        ds_o = each(_mx_tn, qd, doh)
        dp = each(lambda d_, v_: jnp.where(lower, _mx_nt(d_, v_), 0.0), doh, v_new)
        dvn_o = each(_mx_tn, pmat, doh)
        ddec = each(lambda d_, s_: jnp.sum(rsum(d_ * s_), axis=0, keepdims=True), dsn, st)
        dkd = each(_mx_nt, v_new, dsn)
        dvn = each(lambda a, k_, d_: a + _mx(k_, d_), dvn_o, kd, dsn)
        dw = each(lambda d_, s_: -_mx_nt(d_, s_), dvn, st)
        ds_w = each(lambda uw_, d_: _mx_tn(uw_[:, DN_DIM:], d_), uw, dvn)
        for h in hs:
            ds_ref[h] = ds_o[h] + dec[h] * dsn[h] - ds_w[h]
        dr = each(lambda t_, a, b: _hi_tn(t_, jnp.concatenate([a, b], axis=-1)), tms, dvn, dw)
        da = each(lambda r_, uw_: jnp.where(strict, -_hi_nt(r_, uw_), 0.0), dr, uw)
        dru = [r_[:, :DN_DIM] for r_ in dr]
        drw = [r_[:, DN_DIM:] for r_ in dr]
        db = each(lambda a, b: a * b, da, gam)
        dq_m = each(lambda a, b: a * b, dp, gam)
        e = each(lambda a, bm, p_, qm, g_: (a * bm + p_ * qm) * g_, da, bmat, dp, qmat, gam)
        dkb = each(lambda b_, k_, r_, e_: _mx(b_, k_) + r_ * e_, db, kh, drw, eg)
        dk = each(lambda b_, kb_, m_, q_, d_, f_: _mx_tn(b_, kb_) + _mx_tn(m_, q_) + d_ * f_, db, kb, dq_m, qh, dkd, kdf)
        dq = each(lambda m_, k_, d_, e_: _mx(m_, k_) + d_ * e_, dq_m, kh, dqd, eg)
        tk = each(lambda a, b: rsum(a * b), dkd, kd)
        dbeta_all = jnp.zeros((DN_CHUNK, 128), F32)
        dgc_all = jnp.zeros((DN_CHUNK, 128), F32)
        for h in hs:
            dgc = (jnp.sum(e[h], axis=1, keepdims=True) - jnp.sum(e[h].T, axis=1, keepdims=True)
                   + rsum(dqd[h] * qd[h]) - tk[h] + rsum(drw[h] * rhs_w[h]))
            dgl = jnp.sum(tk[h], axis=0, keepdims=True) + ddec[h] * dec[h]
            dgc = dgc + jnp.where(rowi == DN_CHUNK - 1, dgl, 0.0)
            dbeta = rsum(dru[h] * vh[h]) + rsum(dkb[h] * kh[h])
            dq_ref[:, sl[h]] = dq[h]
            dk_ref[:, sl[h]] = dk[h] + dkb[h] * beta[h]
            dv_ref[:, sl[h]] = dru[h] * beta[h]
            dbeta_all = jnp.where(lane == h, dbeta, dbeta_all)
            dgc_all = jnp.where(lane == DN_HEADS + h, dgc, dgc_all)
        dbg_ref[...] = dbeta_all + _hi_tn(ltri, dgc_all)

    rev = lambda n: (n_chunks - 1 - n, 0)
    row = pl.BlockSpec((DN_CHUNK, DN_W), rev)
    small = pl.BlockSpec((DN_CHUNK, 128), rev)
    return pl.pallas_call(
        body, name=name, grid=(n_chunks,),
        in_specs=[row, row, row, small,
                  pl.BlockSpec((1, DN_HEADS, DN_DIM, DN_DIM), lambda n: (n_chunks - 1 - n, 0, 0, 0)),
                  pl.BlockSpec((1, DN_HEADS, DN_CHUNK, DN_CHUNK), lambda n: (n_chunks - 1 - n, 0, 0, 0)), row],
        out_specs=(row, row, row, small),
        out_shape=(jax.ShapeDtypeStruct((t, DN_W), F32),) * 3 + (jax.ShapeDtypeStruct((t, 128), F32),),
        scratch_shapes=[pltpu.VMEM((DN_HEADS, DN_DIM, DN_DIM), F32)],
        compiler_params=_cp("arbitrary"))(q, k, v, bg, sall, tall, do)


def gdn_out_fwd(o, proj, o_gain, name, tm=256):
    t = o.shape[0]

    def body(o_ref, z_ref, g_ref, y_ref, yt_ref):
        for h in range(DN_HEADS):
            sl = slice(h * DN_DIM, (h + 1) * DN_DIM)
            ov, zv = o_ref[:, sl], z_ref[:, sl]
            r = lax.rsqrt(jnp.mean(ov * ov, axis=-1, keepdims=True) + EPS)
            y = ov * r * g_ref[...] * (zv * _sigmoid(zv))
            y_ref[:, sl] = y.astype(y_ref.dtype)
            yt_ref[sl, :] = y.T.astype(yt_ref.dtype)

    row = pl.BlockSpec((tm, DN_W), lambda i: (i, 0))
    return pl.pallas_call(
        body, name=name, grid=(t // tm,),
        in_specs=[row, pl.BlockSpec((tm, DN_W), lambda i: (i, _Z_COL)), pl.BlockSpec((1, DN_DIM), lambda i: (0, 0))],
        out_specs=(row, pl.BlockSpec((DN_W, tm), lambda i: (0, i))),
        out_shape=(jax.ShapeDtypeStruct((t, DN_W), MXU_DTYPE), jax.ShapeDtypeStruct((DN_W, t), MXU_DTYPE)),
        compiler_params=_cp("parallel"))(o, proj, o_gain)


def gdn_out_bwd(o, proj, o_gain, dy, name, tm=256):
    t = o.shape[0]

    def body(o_ref, z_ref, g_ref, dy_ref, do_ref, dz_ref, dg_ref):
        i = pl.program_id(0)
        dg = jnp.zeros((1, DN_DIM), F32)
        for h in range(DN_HEADS):
            sl = slice(h * DN_DIM, (h + 1) * DN_DIM)
            ov, zv, dyv = o_ref[:, sl], z_ref[:, sl], dy_ref[:, sl]
            r = lax.rsqrt(jnp.mean(ov * ov, axis=-1, keepdims=True) + EPS)
            oh = ov * r
            sg = _sigmoid(zv)
            dz_ref[:, sl] = (dyv * oh * g_ref[...] * sg * (1.0 + zv * (1.0 - sg))).astype(dz_ref.dtype)
            don = dyv * (zv * sg)
            dg = dg + jnp.sum(don * oh, axis=0, keepdims=True)
            doh = don * g_ref[...]
            do_ref[:, sl] = r * (doh - oh * jnp.mean(doh * oh, axis=-1, keepdims=True))

        @pl.when(i == 0)
        def _():
            dg_ref[...] = dg

        @pl.when(i > 0)
        def _():
            dg_ref[...] += dg

    row = pl.BlockSpec((tm, DN_W), lambda i: (i, 0))
    one = pl.BlockSpec((1, DN_DIM), lambda i: (0, 0))
    return pl.pallas_call(
        body, name=name, grid=(t // tm,),
        in_specs=[row, pl.BlockSpec((tm, DN_W), lambda i: (i, _Z_COL)), one, row],
        out_specs=(row, row, one),
        out_shape=(jax.ShapeDtypeStruct((t, DN_W), F32), jax.ShapeDtypeStruct((t, DN_W), MXU_DTYPE),
                   jax.ShapeDtypeStruct((1, DN_DIM), F32)),
        compiler_params=_cp("arbitrary"))(o, proj, o_gain, dy)


def _peer(k):
    x, y, c = lax.axis_index("x"), lax.axis_index("y"), lax.axis_index("c")
    px = 1 - x if k & 4 else x
    py = 1 - y if k & 2 else y
    pc = 1 - c if k & 1 else c
    return (px, py, pc), 4 * px + 2 * py + pc


def all_gather(shards, name):
    na = len(shards)

    def body(*refs):
        ins, outs = refs[:na], refs[na:2 * na]
        send_sems, recv_sems, local_sems = refs[2 * na:]
        _, me = _peer(0)
        local = [pltpu.make_async_copy(ins[a], outs[a].at[me], local_sems.at[a]) for a in range(na)]
        for cp in local:
            cp.start()
        sends = []
        for k in range(1, N_DEV):
            peer, _ = _peer(k)
            for a in range(na):
                cp = pltpu.make_async_remote_copy(
                    src_ref=ins[a], dst_ref=outs[a].at[me], send_sem=send_sems.at[a, k - 1],
                    recv_sem=recv_sems.at[a, k - 1], device_id=peer, device_id_type=MESH)
                cp.start()
                sends.append(cp)
        for k in range(1, N_DEV):
            peer, pid = _peer(k)
            for a in range(na):
                pltpu.make_async_remote_copy(
                    src_ref=ins[a], dst_ref=outs[a].at[pid], send_sem=send_sems.at[a, k - 1],
                    recv_sem=recv_sems.at[a, k - 1], device_id=peer, device_id_type=MESH).wait_recv()
        for cp in sends:
            cp.wait_send()
        for cp in local:
            cp.wait()

    anyspec = pl.BlockSpec(memory_space=pl.ANY)
    return pl.pallas_call(
        body, name=name, in_specs=[anyspec] * na, out_specs=tuple([anyspec] * na),
        out_shape=tuple(jax.ShapeDtypeStruct((N_DEV,) + s.shape, s.dtype) for s in shards),
        scratch_shapes=[pltpu.SemaphoreType.DMA((na, N_DEV - 1)), pltpu.SemaphoreType.DMA((na, N_DEV - 1)),
                        pltpu.SemaphoreType.DMA((na,))],
        compiler_params=pltpu.CompilerParams(has_side_effects=True))(*shards)


_HBM = pl.BlockSpec(memory_space=pltpu.HBM)
_SEM = pl.BlockSpec(memory_space=pltpu.SEMAPHORE)
_DATAFLOW = pltpu.SideEffectType.DATAFLOW_SIDE_EFFECTING
N_PEER = N_DEV - 1


def send_start(srcs, name, scatter, after):
    na = len(srcs)
    ns = 2 * N_PEER * na
    lands = [lax.empty((N_DEV,) + (s.shape[1:] if scatter else s.shape), s.dtype) for s in srcs]
    extra = [] if after is None else [after]

    def body(*refs):
        src_refs, land_refs = refs[:na], refs[na:2 * na]
        sems, token = refs[2 * na + len(extra):2 * na + len(extra) + ns], refs[-1]
        _, me = _peer(0)
        for k in range(1, N_DEV):
            peer, pid = _peer(k)
            for a in range(na):
                pltpu.make_async_remote_copy(
                    src_ref=src_refs[a].at[pid] if scatter else src_refs[a], dst_ref=land_refs[a].at[me],
                    send_sem=sems[2 * (a * N_PEER + k - 1)], recv_sem=sems[2 * (a * N_PEER + k - 1) + 1],
                    device_id=peer, device_id_type=MESH).start()
        token[...] = jnp.zeros_like(token)

    hbm = lambda arrs: tuple(pltpu.HBM(a.shape, a.dtype) for a in arrs)
    outs = pl.pallas_call(
        body, name=name,
        out_shape=(pltpu.SemaphoreType.DMA(()),) * ns + hbm(srcs) + hbm(lands) + (jax.ShapeDtypeStruct((8, 128), F32),),
        in_specs=[_HBM] * (2 * na) + [pl.BlockSpec(memory_space=pl.ANY)] * len(extra),
        out_specs=(_SEM,) * ns + (_HBM,) * (2 * na) + (pl.BlockSpec(memory_space=pltpu.VMEM),),
        input_output_aliases={i: ns + i for i in range(2 * na)},
        compiler_params=pltpu.CompilerParams(has_side_effects=_DATAFLOW),
    )(*[pltpu.with_memory_space_constraint(a, pltpu.HBM) for a in list(srcs) + lands], *extra)
    return outs[:ns], outs[ns:ns + na], outs[ns + na:ns + 2 * na], outs[-1]


def send_wait(sems, srcs_thru, lands_thru, name, scatter, after):
    na = len(srcs_thru)
    ns = 2 * N_PEER * na

    def body(*refs):
        src_refs, land_refs, sm = refs[:na], refs[na:2 * na], refs[2 * na:2 * na + ns]
        for k in range(1, N_DEV):
            peer, pid = _peer(k)
            for a in range(na):
                cp = pltpu.make_async_remote_copy(
                    src_ref=src_refs[a].at[pid] if scatter else src_refs[a], dst_ref=land_refs[a].at[pid],
                    send_sem=sm[2 * (a * N_PEER + k - 1)], recv_sem=sm[2 * (a * N_PEER + k - 1) + 1],
                    device_id=peer, device_id_type=MESH)
                cp.wait_send()
                cp.wait_recv()

    hbm = lambda arrs: tuple(pltpu.HBM(a.shape, a.dtype) for a in arrs)
    outs = pl.pallas_call(
        body, name=name, out_shape=hbm(srcs_thru) + hbm(lands_thru),
        in_specs=[_HBM] * (2 * na) + [_SEM] * ns + [pl.BlockSpec(memory_space=pl.ANY)], out_specs=(_HBM,) * (2 * na),
        input_output_aliases={i: i for i in range(2 * na)},
        compiler_params=pltpu.CompilerParams(has_side_effects=_DATAFLOW),
    )(*srcs_thru, *lands_thru, *sems, after)
    return outs[na:]


def _adamw(w, g, m, v):
    m = ADAM_B1 * m + (1.0 - ADAM_B1) * g
    v = ADAM_B2 * v + (1.0 - ADAM_B2) * (g * g)
    m_hat = m / (1.0 - ADAM_B1 ** ADAM_STEP)
    v_hat = v / (1.0 - ADAM_B2 ** ADAM_STEP)
    return -ADAM_LR * (m_hat / (jnp.sqrt(v_hat) + ADAM_EPS) + ADAM_WD * w), m, v


def adam_sum(w, pieces, m, v, name):
    r, c = w.shape
    tr = r
    for cand in (256, 128, 64, 32, 16, 8):
        if r % cand == 0:
            tr = cand
            break

    def body(w_ref, p_ref, m_ref, v_ref, g_ref, d_ref, nm_ref, nv_ref):
        g = p_ref[0].astype(F32)
        for s in range(1, N_DEV):
            g = g + p_ref[s].astype(F32)
        g_ref[...] = g
        d_ref[...], nm_ref[...], nv_ref[...] = _adamw(w_ref[...], g, m_ref[...], v_ref[...])

    row = pl.BlockSpec((tr, c), lambda i: (i, 0))
    out = jax.ShapeDtypeStruct((r, c), F32)
    return pl.pallas_call(
        body, name=name, grid=(r // tr,),
        in_specs=[row, pl.BlockSpec((N_DEV, tr, c), lambda i: (0, i, 0)), row, row],
        out_specs=(row,) * 4, out_shape=(out,) * 4, compiler_params=_cp("parallel"))(w, pieces, m, v)


def sum_rows(gathered, name):
    _, r, c = gathered.shape

    def body(p_ref, o_ref):
        g = p_ref[0]
        for s in range(1, N_DEV):
            g = g + p_ref[s]
        o_ref[...] = g

    return pl.pallas_call(body, name=name, out_shape=jax.ShapeDtypeStruct((r, c), F32))(gathered)


def adam_small(w, g, m, v, name):
    def body(w_ref, g_ref, m_ref, v_ref, d_ref, nm_ref, nv_ref):
        d_ref[...], nm_ref[...], nv_ref[...] = _adamw(w_ref[...], g_ref[...], m_ref[...], v_ref[...])

    out = jax.ShapeDtypeStruct(w.shape, F32)
    return pl.pallas_call(body, name=name, out_shape=(out,) * 3)(w, g, m, v)


def _rope_tables(t):
    inv_freq = 10000.0 ** (-jnp.arange(0, HEAD_DIM, 2, dtype=F32) / HEAD_DIM)
    ang = jnp.arange(t, dtype=F32)[:, None] * inv_freq[None, :]
    cos, sin = jnp.cos(ang), jnp.sin(ang)
    return jnp.concatenate([cos, cos], axis=-1), jnp.concatenate([sin, sin], axis=-1)


def _lane_row(vec8):
    return jnp.pad(vec8.reshape(1, DN_HEADS), ((0, 0), (DN_HEADS, 128 - 2 * DN_HEADS)))


def _ffn_fwd(x, norm_g, w_gu, w_d, tag):
    f, ft = rms_fwd(x, norm_g, f"{tag}_norm")
    gu, a, at = ffn_up(f, w_gu, f"{tag}_gate_up")
    return mm_nn(a, w_d, f"{tag}_down", res=x), (ft, gu, at)


def _ffn_bwd(x, norm_g, w_gu, w_d, saved, dy, tag, after=None):
    ft, gu, at = saved
    dgu = ffn_dact(dy, w_d, gu, f"{tag}_d_gate_up", after=after)
    dwd = mm_at(at, dy, f"{tag}_dw_down")
    df = mm_nt(dgu, w_gu, f"{tag}_d_normed")
    dwgu = mm_at(ft, dgu, f"{tag}_dw_gate_up")
    dx, dg = rms_bwd(x, norm_g, df, dy, f"{tag}_d_norm")
    return dx, dwgu, dwd, dg


def local_step(x, target, small, weights_of, grads_out, after=None):
    t = x.shape[0]
    cosf, sinf = _rope_tables(t)
    alog_row, dtb_row = _lane_row(small["odd_a_log"]), _lane_row(small["odd_dt_bias"])

    h0, h0t = rms_fwd(x, small["even_norm"], "even_norm", after=after)
    we = weights_of("even", h0)
    proj0 = mm_nn(h0, we["w_in"], "even_in_proj")
    qr, kr = qk_prep_fwd(proj0, small["even_q_gain"], small["even_k_gain"], cosf, sinf, "even_qk_prep")
    y_attn, lse = swa_fwd(qr, kr, proj0, small["even_sinks"], "even_swa")
    y_conv = gconv_fwd(proj0, small["even_conv_w"], "even_gconv")
    mix0 = jnp.concatenate([y_attn.astype(MXU_DTYPE), y_conv], axis=-1)
    x1 = mm_nn(mix0, we["w_out"], "even_out_proj", res=x)
    w0 = weights_of("ffn0", x1)
    x2, ffn0 = _ffn_fwd(x1, small["ffn_norm0"], w0["gate_up"], w0["down"], "ffn0")

    wo = weights_of("odd", x2)
    h1, h1t = rms_fwd(x2, small["odd_norm"], "odd_norm")
    proj1 = mm_nn(h1, wo["w_in"], "odd_in_proj")
    qn, kn, vs, bg = gdn_prep_fwd(proj1, small["odd_conv_w"], alog_row, dtb_row, "odd_prep")
    o, sall, tall = gdn_fwd(qn, kn, vs, bg, "odd_delta_rule")
    og, ogt = gdn_out_fwd(o, proj1, small["odd_o_gain"], "odd_gate_norm")
    x3 = mm_nn(og, wo["w_out"], "odd_out_proj", res=x2)
    w1 = weights_of("ffn1", x3)
    x4, ffn1 = _ffn_fwd(x3, small["ffn_norm1"], w1["gate_up"], w1["down"], "ffn1")

    dy, loss_row = loss_head(x4, target, "loss_head")

    gs = {}
    dx3, dwgu, dwd, gs["ffn_norm1"] = _ffn_bwd(x3, small["ffn_norm1"], w1["gate_up"], w1["down"], ffn1, dy, "ffn1")
    tok = grads_out("ffn1", {"gate_up": dwgu, "down": dwd})

    dog = mm_nt(dx3, wo["w_out"], "odd_d_gated", after=tok)
    dwo = mm_at(ogt, dx3, "odd_dw_out")
    do, dz, gs["odd_o_gain"] = gdn_out_bwd(o, proj1, small["odd_o_gain"], dog, "odd_d_gate_norm")
    dqn, dkn, dvs, dbg = gdn_bwd(qn, kn, vs, bg, sall, tall, do, "odd_d_delta_rule")
    dqkv, dba, gs["odd_conv_w"], ddt_row, dal_row = gdn_prep_bwd(
        proj1, small["odd_conv_w"], alog_row, dtb_row, dqn, dkn, dvs, dbg, "odd_d_prep")
    gs["odd_dt_bias"] = ddt_row[:, DN_HEADS:2 * DN_HEADS]
    gs["odd_a_log"] = dal_row[:, DN_HEADS:2 * DN_HEADS]
    dproj1 = jnp.concatenate([dqkv, dz, dba], axis=-1)
    dh1 = mm_nt(dproj1, wo["w_in"], "odd_d_normed")
    dwi = mm_at(h1t, dproj1, "odd_dw_in")
    dx2, gs["odd_norm"] = rms_bwd(x2, small["odd_norm"], dh1, dx3, "odd_d_norm")
    tok = grads_out("odd", {"w_in": dwi, "w_out": dwo})

    dx1, dwgu, dwd, gs["ffn_norm0"] = _ffn_bwd(x1, small["ffn_norm0"], w0["gate_up"], w0["down"], ffn0, dx2, "ffn0",
                                               after=tok)
    tok = grads_out("ffn0", {"gate_up": dwgu, "down": dwd})

    dmix = mm_nt(dx1, we["w_out"], "even_d_mix", after=tok)
    dwo = mm_at(mix0.T, dx1, "even_dw_out")
    dqr, dkr, dv, gs["even_sinks"] = swa_bwd(qr, kr, proj0, small["even_sinks"], y_attn, lse, dmix, "even_d_swa")
    dqk, gs["even_q_gain"], gs["even_k_gain"] = qk_prep_bwd(
        proj0, small["even_q_gain"], small["even_k_gain"], cosf, sinf, dqr, dkr, "even_d_qk_prep")
    dgb, dgc, dxi, gs["even_conv_w"] = gconv_bwd(proj0, small["even_conv_w"], dmix, "even_d_gconv")
    dproj0 = jnp.concatenate([dqk, dv, dgb, dgc, dxi], axis=-1)
    dwi = mm_at(h0t, dproj0, "even_dw_in")
    tok = grads_out("even", {"w_in": dwi, "w_out": dwo})
    dh0 = mm_nt(dproj0, we["w_in"], "even_d_normed", after=tok)
    grad_x, gs["even_norm"] = rms_bwd(x, small["even_norm"], dh0, dx1, "even_d_norm")
    return loss_row, grad_x, gs


_SMALL_ORDER = ("even_norm", "even_q_gain", "even_k_gain", "even_sinks", "odd_a_log", "odd_dt_bias", "odd_o_gain",
                "ffn_norm0", "ffn_norm1", "odd_norm", "even_conv_w", "odd_conv_w")
_SMALL_SIZE = {"even_norm": 1024, "even_q_gain": 64, "even_k_gain": 64, "even_sinks": 8, "odd_a_log": 8,
               "odd_dt_bias": 8, "odd_o_gain": 128, "ffn_norm0": 1024, "ffn_norm1": 1024, "odd_norm": 1024,
               "even_conv_w": 3 * 512, "odd_conv_w": 4 * 3072}
_N_REPL = 9


def _pack_rows(vals):
    flat = jnp.concatenate([v.reshape(-1) for v in vals])
    pad = (-flat.shape[0]) % 1024
    return jnp.pad(flat, (0, pad)).reshape(-1, 128)


def _my_block(full, size, axis):
    me = 4 * lax.axis_index("x") + 2 * lax.axis_index("y") + lax.axis_index("c")
    return lax.dynamic_slice_in_dim(full, me * size, size, axis=axis)


def _col_gathered(g):
    return g.transpose(1, 0, 2).reshape(g.shape[1], N_DEV * g.shape[2])


def _col_pieces(dw):
    k, n8 = dw.shape
    return dw.reshape(k, N_DEV, n8 // N_DEV).transpose(1, 0, 2)


def kernel(x, even_norm, even_w_in, even_q_gain, even_k_gain, even_sinks, even_conv_w, even_w_out, odd_norm, odd_w_in, odd_conv_w, odd_a_log, odd_dt_bias, odd_o_gain, odd_w_out, ffn_norm, ffn_w_gate_up, ffn_w_down, loss_target, m_even_norm, m_even_w_in, m_even_q_gain, m_even_k_gain, m_even_sinks, m_even_conv_w, m_even_w_out, m_odd_norm, m_odd_w_in, m_odd_conv_w, m_odd_a_log, m_odd_dt_bias, m_odd_o_gain, m_odd_w_out, m_ffn_norm, m_ffn_w_gate_up, m_ffn_w_down, v_even_norm, v_even_w_in, v_even_q_gain, v_even_k_gain, v_even_sinks, v_even_conv_w, v_even_w_out, v_odd_norm, v_odd_w_in, v_odd_conv_w, v_odd_a_log, v_odd_dt_bias, v_odd_o_gain, v_odd_w_out, v_ffn_norm, v_ffn_w_gate_up, v_ffn_w_down):
    t = x.shape[1]
    d = D_MODEL

    me = 4 * lax.axis_index("x") + 2 * lax.axis_index("y") + lax.axis_index("c")
    fpd = D_FF // N_DEV
    shard = {
        "even": {"w_in": even_w_in.reshape(d, EVEN_IN_W // N_DEV), "w_out": even_w_out.reshape(d // N_DEV, d)},
        "ffn0": {"gate_up": ffn_w_gate_up[0], "down": ffn_w_down[0]},
        "odd": {"w_in": odd_w_in.reshape(d, ODD_IN_W // N_DEV), "w_out": odd_w_out.reshape(d // N_DEV, d)},
        "ffn1": {"gate_up": ffn_w_gate_up[1], "down": ffn_w_down[1]},
    }
    mom_m = {
        "even": {"w_in": m_even_w_in, "w_out": m_even_w_out}, "odd": {"w_in": m_odd_w_in, "w_out": m_odd_w_out},
        "ffn0": {"gate_up": m_ffn_w_gate_up[0], "down": m_ffn_w_down[0]},
        "ffn1": {"gate_up": m_ffn_w_gate_up[1], "down": m_ffn_w_down[1]},
    }
    mom_v = {
        "even": {"w_in": v_even_w_in, "w_out": v_even_w_out}, "odd": {"w_in": v_odd_w_in, "w_out": v_odd_w_out},
        "ffn0": {"gate_up": v_ffn_w_gate_up[0], "down": v_ffn_w_down[0]},
        "ffn1": {"gate_up": v_ffn_w_gate_up[1], "down": v_ffn_w_down[1]},
    }

    def whole(group, parts):
        col, row = tuple(shard[group])
        w_col = _gu_gathered(parts[0]) if col == "gate_up" else _col_gathered(parts[0])
        if group == "odd":
            w_col = jnp.pad(w_col, ((0, 0), (0, ODD_IN_PAD - ODD_IN_W)))
        return {col: w_col, row: parts[1].reshape(-1, d)}

    def own_slot(lands, blocks):
        return [lax.dynamic_update_index_in_dim(l, b, me, 0) for l, b in zip(lands, blocks)]

    shard_rows = _pack_rows([odd_norm, even_conv_w, odd_conv_w])
    (small_g,) = all_gather([shard_rows], "gather_small_weights")
    wire = {g: [a.astype(MXU_DTYPE) for a in shard[g].values()] for g in shard}
    gathers, tok = {}, small_g
    for g in shard:
        sems, srcs_thru, lands_thru, tok = send_start(wire[g], f"gather_{g}_start", False, tok)
        gathers[g] = (sems, srcs_thru, lands_thru)

    def weights_of(group, after):
        lands = send_wait(*gathers[group], f"gather_{group}_wait", False, after)
        return whole(group, own_slot(lands, wire[group]))

    sent = {}

    def grads_out(group, dws):
        col, row = tuple(shard[group])
        n_cols = N_DEV * shard[group][col].shape[1]
        pieces = [_gu_pieces(dws[col]) if col == "gate_up" else _col_pieces(dws[col][:, :n_cols]),
                  dws[row].reshape((N_DEV,) + shard[group][row].shape)]
        sems, srcs_thru, lands_thru, token = send_start(pieces, f"exchange_{group}_start", True, None)
        sent[group] = (sems, srcs_thru, lands_thru, pieces)
        return token

    sg = small_g.reshape(N_DEV, -1)
    o1 = d // N_DEV
    o2 = o1 + 3 * CONV_CH // N_DEV
    small = {
        "even_norm": even_norm, "even_q_gain": even_q_gain, "even_k_gain": even_k_gain, "even_sinks": even_sinks,
        "odd_a_log": odd_a_log.reshape(-1), "odd_dt_bias": odd_dt_bias.reshape(-1), "odd_o_gain": odd_o_gain,
        "ffn_norm0": ffn_norm[0:1], "ffn_norm1": ffn_norm[1:2],
        "odd_norm": sg[:, :o1].reshape(1, d),
        "even_conv_w": sg[:, o1:o2].reshape(N_DEV, 3, CONV_CH // N_DEV).transpose(1, 0, 2).reshape(3, CONV_CH),
        "odd_conv_w": sg[:, o2:o2 + 4 * _QKV_W // N_DEV].reshape(N_DEV, 4, _QKV_W // N_DEV).transpose(1, 0, 2).reshape(4, _QKV_W),
    }

    loss_row, grad_x, gs = local_step(x.reshape(t, d), loss_target.reshape(t, d), small, weights_of, grads_out, after=tok)

    rows = _pack_rows([gs[n] for n in _SMALL_ORDER] + [loss_row[:, 0:1]])
    (rows_g,) = all_gather([rows], "gather_small_grads")
    tot = sum_rows(rows_g, "sum_small_grads").reshape(-1)
    off, sgrad = 0, {}
    for n in _SMALL_ORDER:
        sgrad[n] = tot[off:off + _SMALL_SIZE[n]]
        off += _SMALL_SIZE[n]
    loss = tot[off]

    upd, behind = {}, tot
    for g in ("ffn1", "odd", "ffn0", "even"):
        sems, srcs_thru, lands_thru, pieces = sent[g]
        lands = send_wait(sems, srcs_thru, lands_thru, f"exchange_{g}_wait", True, behind)
        own = [lax.dynamic_index_in_dim(p, me, 0, keepdims=False) for p in pieces]
        for (key, w2), pcs in zip(shard[g].items(), own_slot(lands, own)):
            upd[g, key] = adam_sum(w2, pcs, mom_m[g][key].reshape(w2.shape), mom_v[g][key].reshape(w2.shape),
                                   f"adamw_{g}_{key}")
        behind = upd[g, key][0]
    res = {
        "even_w_in": tuple(o.reshape(even_w_in.shape) for o in upd["even", "w_in"]),
        "even_w_out": tuple(o.reshape(even_w_out.shape) for o in upd["even", "w_out"]),
        "odd_w_in": tuple(o.reshape(odd_w_in.shape) for o in upd["odd", "w_in"]),
        "odd_w_out": tuple(o.reshape(odd_w_out.shape) for o in upd["odd", "w_out"]),
        "ffn_w_gate_up": tuple(jnp.stack([a, b]) for a, b in zip(upd["ffn0", "gate_up"], upd["ffn1", "gate_up"])),
        "ffn_w_down": tuple(jnp.stack([a, b]) for a, b in zip(upd["ffn0", "down"], upd["ffn1", "down"])),
    }

    repl = _SMALL_ORDER[:_N_REPL]
    repl_w = {"even_norm": even_norm, "even_q_gain": even_q_gain, "even_k_gain": even_k_gain, "even_sinks": even_sinks,
              "odd_a_log": odd_a_log, "odd_dt_bias": odd_dt_bias, "odd_o_gain": odd_o_gain,
              "ffn_norm0": ffn_norm[0], "ffn_norm1": ffn_norm[1]}
    repl_m = {"even_norm": m_even_norm, "even_q_gain": m_even_q_gain, "even_k_gain": m_even_k_gain,
              "even_sinks": m_even_sinks, "odd_a_log": m_odd_a_log, "odd_dt_bias": m_odd_dt_bias,
              "odd_o_gain": m_odd_o_gain, "ffn_norm0": m_ffn_norm[0], "ffn_norm1": m_ffn_norm[1]}
    repl_v = {"even_norm": v_even_norm, "even_q_gain": v_even_q_gain, "even_k_gain": v_even_k_gain,
              "even_sinks": v_even_sinks, "odd_a_log": v_odd_a_log, "odd_dt_bias": v_odd_dt_bias,
              "odd_o_gain": v_odd_o_gain, "ffn_norm0": v_ffn_norm[0], "ffn_norm1": v_ffn_norm[1]}
    pk = lambda dct: _pack_rows([dct[n] for n in repl])
    pd_, pm_, pv_ = adam_small(pk(repl_w), pk(sgrad), pk(repl_m), pk(repl_v), "adamw_replicated")
    sres = {}
    off = 0
    for n in repl:
        sz = _SMALL_SIZE[n]
        sres[n] = (sgrad[n], pd_.reshape(-1)[off:off + sz], pm_.reshape(-1)[off:off + sz], pv_.reshape(-1)[off:off + sz])
        off += sz
    g_on = _my_block(sgrad["odd_norm"].reshape(1, d), d // N_DEV, 1)
    g_ec = _my_block(sgrad["even_conv_w"].reshape(3, CONV_CH), CONV_CH // N_DEV, 1)
    g_oc = _my_block(sgrad["odd_conv_w"].reshape(4, _QKV_W), _QKV_W // N_DEV, 1)
    shard_w = _pack_rows([odd_norm, even_conv_w, odd_conv_w])
    sd_, sm_, sv_ = adam_small(shard_w, _pack_rows([g_on, g_ec, g_oc]),
                               _pack_rows([m_odd_norm, m_even_conv_w, m_odd_conv_w]),
                               _pack_rows([v_odd_norm, v_even_conv_w, v_odd_conv_w]), "adamw_sharded_small")
    off = 0
    for n, gfull, like in (("odd_norm", g_on, odd_norm), ("even_conv_w", g_ec, even_conv_w), ("odd_conv_w", g_oc, odd_conv_w)):
        sz = like.size
        sres[n] = (gfull, sd_.reshape(-1)[off:off + sz], sm_.reshape(-1)[off:off + sz], sv_.reshape(-1)[off:off + sz])
        off += sz

    def small_out(name, like, kind):
        if name == "ffn_norm":
            return jnp.stack([sres["ffn_norm0"][kind], sres["ffn_norm1"][kind]]).reshape(like.shape)
        return sres[name][kind].reshape(like.shape)

    order = (("even_norm", even_norm), ("even_w_in", even_w_in), ("even_q_gain", even_q_gain),
             ("even_k_gain", even_k_gain), ("even_sinks", even_sinks), ("even_conv_w", even_conv_w),
             ("even_w_out", even_w_out), ("odd_norm", odd_norm), ("odd_w_in", odd_w_in), ("odd_conv_w", odd_conv_w),
             ("odd_a_log", odd_a_log), ("odd_dt_bias", odd_dt_bias), ("odd_o_gain", odd_o_gain),
             ("odd_w_out", odd_w_out), ("ffn_norm", ffn_norm), ("ffn_w_gate_up", ffn_w_gate_up),
             ("ffn_w_down", ffn_w_down))
    outs = [loss, grad_x.reshape(x.shape)]
    for kind in range(4):
        for name, like in order:
            outs.append(res[name][kind] if name in res else small_out(name, like, kind))
    return tuple(outs)
```

```python
import functools

import jax
import jax.numpy as jnp
from jax import lax
from jax.experimental import pallas as pl
from jax.experimental.pallas import tpu as pltpu

F32 = jnp.float32
MXU_DTYPE = jnp.bfloat16
HI = lax.Precision.HIGH
EPS = 1e-6
N_DEV = 8
D_MODEL = 1024
HEAD_DIM = 64
ATTN_HEADS = 8
KV_HEADS = 2
ATTN_BLOCK = 128
Q_W = 512
KV_W = 128
CONV_CH = 512
EVEN_IN_W = 2304
DN_HEADS = 8
DN_DIM = 128
DN_W = 1024
DN_CHUNK = 64
ODD_IN_W = 4112
ODD_IN_PAD = 4224
D_FF = 2816
NEG = -1e30
VMEM_LIMIT = 56 * 1024 * 1024
ADAM_LR, ADAM_B1, ADAM_B2, ADAM_EPS, ADAM_WD, ADAM_STEP = 0.001, 0.9, 0.999, 1e-08, 0.01, 10
MESH = pl.DeviceIdType.MESH


def _cp(*sem):
    return pltpu.CompilerParams(dimension_semantics=sem, vmem_limit_bytes=VMEM_LIMIT)


def _pick(n, cap):
    best = 128
    for t in range(128, cap + 1, 128):
        if n % t == 0:
            best = t
    return best


def _mx(a, b):
    return jnp.dot(a.astype(MXU_DTYPE), b.astype(MXU_DTYPE), preferred_element_type=F32)


def _mx_nt(a, b):
    return lax.dot_general(a.astype(MXU_DTYPE), b.astype(MXU_DTYPE), (((1,), (1,)), ((), ())),
                           preferred_element_type=F32)


def _mx_tn(a, b):
    return lax.dot_general(a.astype(MXU_DTYPE), b.astype(MXU_DTYPE), (((0,), (0,)), ((), ())),
                           preferred_element_type=F32)


def _hi(a, b):
    return jnp.dot(a, b, precision=HI, preferred_element_type=F32)


def _hi_nt(a, b):
    return lax.dot_general(a, b, (((1,), (1,)), ((), ())), precision=HI, preferred_element_type=F32)


def _hi_tn(a, b):
    return lax.dot_general(a, b, (((0,), (0,)), ((), ())), precision=HI, preferred_element_type=F32)


def _sigmoid(x):
    return 1.0 / (1.0 + jnp.exp(-x))


def _softplus(x):
    return jnp.maximum(x, 0.0) + jnp.log(1.0 + jnp.exp(-jnp.abs(x)))


def mm_nn(a, b, name, res=None, out_dtype=F32, tm=1024):
    m, k = a.shape
    _, n = b.shape
    tn = _pick(n, 1536)
    tm = min(tm, m)

    def body(*refs):
        a_ref, b_ref = refs[0], refs[1]
        o_ref = refs[-1]
        acc = _mx(a_ref[...], b_ref[...])
        if res is not None:
            acc = acc + refs[2][...]
        o_ref[...] = acc.astype(o_ref.dtype)

    in_specs = [pl.BlockSpec((tm, k), lambda j, i: (i, 0)), pl.BlockSpec((k, tn), lambda j, i: (0, j))]
    args = [a, b]
    if res is not None:
        in_specs.append(pl.BlockSpec((tm, tn), lambda j, i: (i, j)))
        args.append(res)
    return pl.pallas_call(
        body, name=name, grid=(n // tn, m // tm), in_specs=in_specs,
        out_specs=pl.BlockSpec((tm, tn), lambda j, i: (i, j)),
        out_shape=jax.ShapeDtypeStruct((m, n), out_dtype), compiler_params=_cp("parallel", "parallel"))(*args)


def mm_nt(a, b, name, out_dtype=F32, tm=1024, after=None):
    m, k = a.shape
    n, _ = b.shape
    tn = _pick(n, 512 if k > 3000 else 1536)
    tm = min(tm, m)

    def body(a_ref, b_ref, *rest):
        o_ref = rest[-1]
        o_ref[...] = _mx_nt(a_ref[...], b_ref[...]).astype(o_ref.dtype)

    in_specs = [pl.BlockSpec((tm, k), lambda j, i: (i, 0)), pl.BlockSpec((tn, k), lambda j, i: (j, 0))]
    args = [a, b]
    if after is not None:
        in_specs.append(pl.BlockSpec(memory_space=pl.ANY))
        args.append(after)
    return pl.pallas_call(
        body, name=name, grid=(n // tn, m // tm), in_specs=in_specs,
        out_specs=pl.BlockSpec((tm, tn), lambda j, i: (i, j)),
        out_shape=jax.ShapeDtypeStruct((m, n), out_dtype), compiler_params=_cp("parallel", "parallel"))(*args)


def mm_at(at, b, name, tk=1024):
    m, kk = at.shape
    _, n = b.shape
    tm, tn, tk = _pick(m, 1408), _pick(n, 1408), min(tk, kk)
    nk = kk // tk

    def body(a_ref, b_ref, o_ref, acc_ref):
        k = pl.program_id(2)
        p = _mx(a_ref[...], b_ref[...])
        acc = jnp.where(k == 0, p, acc_ref[...] + p)
        acc_ref[...] = acc

        @pl.when(k == nk - 1)
        def _():
            o_ref[...] = acc.astype(o_ref.dtype)

    return pl.pallas_call(
        body, name=name, grid=(m // tm, n // tn, nk),
        in_specs=[pl.BlockSpec((tm, tk), lambda i, j, k: (i, k)), pl.BlockSpec((tk, tn), lambda i, j, k: (k, j))],
        out_specs=pl.BlockSpec((tm, tn), lambda i, j, k: (i, j)),
        out_shape=jax.ShapeDtypeStruct((m, n), MXU_DTYPE), scratch_shapes=[pltpu.VMEM((tm, tn), F32)],
        compiler_params=_cp("parallel", "parallel", "arbitrary"))(at, b)


def rms_fwd(x, g, name, tm=512, after=None):
    t, d = x.shape

    def body(x_ref, g_ref, *rest):
        o_ref, ot_ref = rest[-2:]
        xv = x_ref[...]
        r = lax.rsqrt(jnp.mean(xv * xv, axis=-1, keepdims=True) + EPS)
        h = xv * r * g_ref[...]
        o_ref[...] = h.astype(o_ref.dtype)
        ot_ref[...] = h.T.astype(ot_ref.dtype)

    in_specs = [pl.BlockSpec((tm, d), lambda i: (i, 0)), pl.BlockSpec((1, d), lambda i: (0, 0))]
    args = [x, g]
    if after is not None:
        in_specs.append(pl.BlockSpec(memory_space=pl.ANY))
        args.append(after)
    return pl.pallas_call(
        body, name=name, grid=(t // tm,), in_specs=in_specs,
        out_specs=(pl.BlockSpec((tm, d), lambda i: (i, 0)), pl.BlockSpec((d, tm), lambda i: (0, i))),
        out_shape=(jax.ShapeDtypeStruct((t, d), MXU_DTYPE), jax.ShapeDtypeStruct((d, t), MXU_DTYPE)),
        compiler_params=_cp("parallel"))(*args)


def rms_bwd(x, g, dh, dres, name, tm=512):
    t, d = x.shape

    def body(x_ref, g_ref, dh_ref, dres_ref, dx_ref, dg_ref):
        i = pl.program_id(0)
        xv = x_ref[...]
        r = lax.rsqrt(jnp.mean(xv * xv, axis=-1, keepdims=True) + EPS)
        xh = xv * r
        dhv = dh_ref[...]
        dxh = dhv * g_ref[...]
        dx_ref[...] = dres_ref[...] + r * (dxh - xh * jnp.mean(dxh * xh, axis=-1, keepdims=True))
        part = jnp.sum(dhv * xh, axis=0, keepdims=True)

        @pl.when(i == 0)
        def _():
            dg_ref[...] = part

        @pl.when(i > 0)
        def _():
            dg_ref[...] += part

    row = pl.BlockSpec((tm, d), lambda i: (i, 0))
    one = pl.BlockSpec((1, d), lambda i: (0, 0))
    return pl.pallas_call(
        body, name=name, grid=(t // tm,), in_specs=[row, one, row, row], out_specs=(row, one),
        out_shape=(jax.ShapeDtypeStruct((t, d), F32), jax.ShapeDtypeStruct((1, d), F32)),
        compiler_params=_cp("arbitrary"))(x, g, dh, dres)


GU_TILE = 1408


_GU_PER_TILE = GU_TILE * N_DEV // (2 * D_FF)


def _gu_gathered(g):
    _, k, c = g.shape
    nj = N_DEV // (2 * _GU_PER_TILE)
    return g.reshape(2, nj, _GU_PER_TILE, k, c).transpose(3, 1, 0, 2, 4).reshape(k, N_DEV * c)


def _gu_pieces(dw):
    k, n8 = dw.shape
    nj = N_DEV // (2 * _GU_PER_TILE)
    return dw.reshape(k, nj, 2, _GU_PER_TILE, n8 // N_DEV).transpose(2, 1, 3, 0, 4).reshape(N_DEV, k, n8 // N_DEV)


def ffn_up(f, w, name, tm=512):
    t, d = f.shape

    def body(f_ref, w_ref, gu_ref, a_ref, at_ref):
        gu = _mx(f_ref[...], w_ref[...])
        gu_ref[...] = gu
        g, u = gu[:, :GU_TILE], gu[:, GU_TILE:]
        act = g * _sigmoid(g) * u
        a_ref[...] = act.astype(a_ref.dtype)
        at_ref[...] = act.T.astype(at_ref.dtype)

    return pl.pallas_call(
        body, name=name, grid=(D_FF // GU_TILE, t // tm),
        in_specs=[pl.BlockSpec((tm, d), lambda j, i: (i, 0)), pl.BlockSpec((d, 2 * GU_TILE), lambda j, i: (0, j))],
        out_specs=(pl.BlockSpec((tm, 2 * GU_TILE), lambda j, i: (i, j)), pl.BlockSpec((tm, GU_TILE), lambda j, i: (i, j)),
                   pl.BlockSpec((GU_TILE, tm), lambda j, i: (j, i))),
        out_shape=(jax.ShapeDtypeStruct((t, 2 * D_FF), F32), jax.ShapeDtypeStruct((t, D_FF), MXU_DTYPE),
                   jax.ShapeDtypeStruct((D_FF, t), MXU_DTYPE)),
        compiler_params=_cp("parallel", "parallel"))(f, w)


def ffn_dact(dy, w_d, gu, name, tm=512, after=None):
    t, d = dy.shape

    def body(dy_ref, w_ref, gu_ref, *rest):
        o_ref = rest[-1]
        da = _mx_nt(dy_ref[...], w_ref[...])
        g, u = gu_ref[:, :GU_TILE], gu_ref[:, GU_TILE:]
        sg = _sigmoid(g)
        o_ref[:, :GU_TILE] = (da * u * sg * (1.0 + g * (1.0 - sg))).astype(o_ref.dtype)
        o_ref[:, GU_TILE:] = (da * g * sg).astype(o_ref.dtype)

    in_specs = [pl.BlockSpec((tm, d), lambda j, i: (i, 0)), pl.BlockSpec((GU_TILE, d), lambda j, i: (j, 0)),
                pl.BlockSpec((tm, 2 * GU_TILE), lambda j, i: (i, j))]
    args = [dy, w_d, gu]
    if after is not None:
        in_specs.append(pl.BlockSpec(memory_space=pl.ANY))
        args.append(after)
    return pl.pallas_call(
        body, name=name, grid=(D_FF // GU_TILE, t // tm), in_specs=in_specs,
        out_specs=pl.BlockSpec((tm, 2 * GU_TILE), lambda j, i: (i, j)),
        out_shape=jax.ShapeDtypeStruct((t, 2 * D_FF), MXU_DTYPE), compiler_params=_cp("parallel", "parallel"))(*args)


def loss_head(y, target, name, tm=512):
    t, d = y.shape

    def body(y_ref, t_ref, dy_ref, l_ref):
        i = pl.program_id(0)
        e = y_ref[...] - t_ref[...]
        dy_ref[...] = e * (1.0 / d)
        part = jnp.zeros((1, 128), F32) + 0.5 * jnp.sum(jnp.mean(e * e, axis=-1, keepdims=True), axis=0, keepdims=True)

        @pl.when(i == 0)
        def _():
            l_ref[...] = part

        @pl.when(i > 0)
        def _():
            l_ref[...] += part

    row = pl.BlockSpec((tm, d), lambda i: (i, 0))
    return pl.pallas_call(
        body, name=name, grid=(t // tm,), in_specs=[row, row],
        out_specs=(row, pl.BlockSpec((1, 128), lambda i: (0, 0))),
        out_shape=(jax.ShapeDtypeStruct((t, d), F32), jax.ShapeDtypeStruct((1, 128), F32)),
        compiler_params=_cp("arbitrary"))(y, target)


def _rot(x):
    return jnp.concatenate([-x[:, HEAD_DIM // 2:], x[:, :HEAD_DIM // 2]], axis=-1)


def _rot_t(y):
    return jnp.concatenate([y[:, HEAD_DIM // 2:], -y[:, :HEAD_DIM // 2]], axis=-1)


def qk_prep_fwd(proj, q_gain, k_gain, cosf, sinf, name, tm=256):
    t = proj.shape[0]
    nh = ATTN_HEADS + KV_HEADS

    def body(p_ref, qg_ref, kg_ref, c_ref, s_ref, q_ref, k_ref):
        c, s = c_ref[...], s_ref[...]
        outs = []
        for h in range(nh):
            xh = p_ref[:, h * HEAD_DIM:(h + 1) * HEAD_DIM]
            gain = qg_ref[...] if h < ATTN_HEADS else kg_ref[...]
            r = lax.rsqrt(jnp.mean(xh * xh, axis=-1, keepdims=True) + EPS)
            xn = xh * r * gain
            outs.append(xn * c + _rot(xn) * s)
        q_ref[...] = jnp.concatenate(outs[:ATTN_HEADS], axis=-1)
        k_ref[...] = jnp.concatenate(outs[ATTN_HEADS:], axis=-1)

    gspec = pl.BlockSpec((1, HEAD_DIM), lambda i: (0, 0))
    tspec = pl.BlockSpec((tm, HEAD_DIM), lambda i: (i, 0))
    return pl.pallas_call(
        body, name=name, grid=(t // tm,),
        in_specs=[pl.BlockSpec((tm, Q_W + KV_W), lambda i: (i, 0)), gspec, gspec, tspec, tspec],
        out_specs=(pl.BlockSpec((tm, Q_W), lambda i: (i, 0)), pl.BlockSpec((tm, KV_W), lambda i: (i, 0))),
        out_shape=(jax.ShapeDtypeStruct((t, Q_W), F32), jax.ShapeDtypeStruct((t, KV_W), F32)),
        compiler_params=_cp("parallel"))(proj, q_gain, k_gain, cosf, sinf)


def qk_prep_bwd(proj, q_gain, k_gain, cosf, sinf, dq, dk, name, tm=256):
    t = proj.shape[0]
    nh = ATTN_HEADS + KV_HEADS

    def body(p_ref, qg_ref, kg_ref, c_ref, s_ref, dq_ref, dk_ref, o_ref, dqg_ref, dkg_ref):
        i = pl.program_id(0)
        c, s = c_ref[...], s_ref[...]
        outs = []
        dqg = jnp.zeros((1, HEAD_DIM), F32)
        dkg = jnp.zeros((1, HEAD_DIM), F32)
        for h in range(nh):
            xh = p_ref[:, h * HEAD_DIM:(h + 1) * HEAD_DIM]
            if h < ATTN_HEADS:
                gain = qg_ref[...]
                dout = dq_ref[:, h * HEAD_DIM:(h + 1) * HEAD_DIM]
            else:
                gain = kg_ref[...]
                dout = dk_ref[:, (h - ATTN_HEADS) * HEAD_DIM:(h - ATTN_HEADS + 1) * HEAD_DIM]
            r = lax.rsqrt(jnp.mean(xh * xh, axis=-1, keepdims=True) + EPS)
            xhat = xh * r
            dxn = dout * c + _rot_t(dout * s)
            part = jnp.sum(dxn * xhat, axis=0, keepdims=True)
            if h < ATTN_HEADS:
                dqg = dqg + part
            else:
                dkg = dkg + part
            dxh = dxn * gain
            outs.append(r * (dxh - xhat * jnp.mean(dxh * xhat, axis=-1, keepdims=True)))
        o_ref[...] = jnp.concatenate(outs, axis=-1).astype(o_ref.dtype)

        @pl.when(i == 0)
        def _():
            dqg_ref[...] = dqg
            dkg_ref[...] = dkg

        @pl.when(i > 0)
        def _():
            dqg_ref[...] += dqg
            dkg_ref[...] += dkg

    gspec = pl.BlockSpec((1, HEAD_DIM), lambda i: (0, 0))
    tspec = pl.BlockSpec((tm, HEAD_DIM), lambda i: (i, 0))
    return pl.pallas_call(
        body, name=name, grid=(t // tm,),
        in_specs=[pl.BlockSpec((tm, Q_W + KV_W), lambda i: (i, 0)), gspec, gspec, tspec, tspec,
                  pl.BlockSpec((tm, Q_W), lambda i: (i, 0)), pl.BlockSpec((tm, KV_W), lambda i: (i, 0))],
        out_specs=(pl.BlockSpec((tm, Q_W + KV_W), lambda i: (i, 0)), gspec, gspec),
        out_shape=(jax.ShapeDtypeStruct((t, Q_W + KV_W), MXU_DTYPE), jax.ShapeDtypeStruct((1, HEAD_DIM), F32),
                   jax.ShapeDtypeStruct((1, HEAD_DIM), F32)),
        compiler_params=_cp("arbitrary"))(proj, q_gain, k_gain, cosf, sinf, dq, dk)


def _swa_valid(n, grp):
    qi = lax.broadcasted_iota(jnp.int32, (grp * ATTN_BLOCK, 2 * ATTN_BLOCK), 0) & (ATTN_BLOCK - 1)
    kj = lax.broadcasted_iota(jnp.int32, (grp * ATTN_BLOCK, 2 * ATTN_BLOCK), 1)
    diff = qi + ATTN_BLOCK - kj
    return (diff >= 0) & (diff < ATTN_BLOCK) & (n * ATTN_BLOCK - ATTN_BLOCK + kj >= 0)


def _stack_heads(ref, g, grp):
    return jnp.concatenate([ref[:, (g * grp + j) * HEAD_DIM:(g * grp + j + 1) * HEAD_DIM] for j in range(grp)], axis=0)


def _stack_sinks(s_ref, g, grp):
    return jnp.concatenate([jnp.zeros((ATTN_BLOCK, 1), F32) + s_ref[0:1, g * grp + j:g * grp + j + 1]
                            for j in range(grp)], axis=0)


def swa_fwd(q, k, proj, sinks, name):
    t = q.shape[0]
    nb = t // ATTN_BLOCK
    scale = HEAD_DIM ** -0.5
    grp = ATTN_HEADS // KV_HEADS

    def body(q_ref, kc_ref, kp_ref, vc_ref, vp_ref, s_ref, y_ref, lse_ref):
        n = pl.program_id(0)
        valid = _swa_valid(n, grp)
        kk = jnp.concatenate([kp_ref[...], kc_ref[...]], axis=0).astype(MXU_DTYPE)
        vv = jnp.concatenate([vp_ref[...], vc_ref[...]], axis=0).astype(MXU_DTYPE)
        lane = lax.broadcasted_iota(jnp.int32, (ATTN_BLOCK, ATTN_HEADS), 1)
        gs = range(KV_HEADS)
        qg = [_stack_heads(q_ref, g, grp) for g in gs]
        sink = [_stack_sinks(s_ref, g, grp) for g in gs]
        sc = [jnp.where(valid, _mx_nt(qg[g], kk[:, g * HEAD_DIM:(g + 1) * HEAD_DIM]) * scale, NEG) for g in gs]
        m = [jnp.maximum(jnp.max(sc[g], axis=-1, keepdims=True), sink[g]) for g in gs]
        e = [jnp.exp(sc[g] - m[g]) for g in gs]
        den = [jnp.sum(e[g], axis=-1, keepdims=True) + jnp.exp(sink[g] - m[g]) for g in gs]
        og = [_mx(e[g] / den[g], vv[:, g * HEAD_DIM:(g + 1) * HEAD_DIM]) for g in gs]
        lg = [m[g] + jnp.log(den[g]) for g in gs]
        lse = jnp.zeros((ATTN_BLOCK, ATTN_HEADS), F32)
        outs = []
        for h in range(ATTN_HEADS):
            rows = slice((h % grp) * ATTN_BLOCK, (h % grp + 1) * ATTN_BLOCK)
            outs.append(og[h // grp][rows])
            lse = jnp.where(lane == h, lg[h // grp][rows], lse)
        y_ref[...] = jnp.concatenate(outs, axis=-1)
        lse_ref[...] = lse

    cur = lambda n: (n, 0)
    prev = lambda n: (jnp.maximum(n - 1, 0), 0)
    vcol = (Q_W + KV_W) // KV_W
    return pl.pallas_call(
        body, name=name, grid=(nb,),
        in_specs=[pl.BlockSpec((ATTN_BLOCK, Q_W), cur), pl.BlockSpec((ATTN_BLOCK, KV_W), cur),
                  pl.BlockSpec((ATTN_BLOCK, KV_W), prev),
                  pl.BlockSpec((ATTN_BLOCK, KV_W), lambda n: (n, vcol)),
                  pl.BlockSpec((ATTN_BLOCK, KV_W), lambda n: (jnp.maximum(n - 1, 0), vcol)),
                  pl.BlockSpec((1, ATTN_HEADS), lambda n: (0, 0))],
        out_specs=(pl.BlockSpec((ATTN_BLOCK, Q_W), cur), pl.BlockSpec((ATTN_BLOCK, ATTN_HEADS), cur)),
        out_shape=(jax.ShapeDtypeStruct((t, Q_W), F32), jax.ShapeDtypeStruct((t, ATTN_HEADS), F32)),
        compiler_params=_cp("parallel"))(q, k, k, proj, proj, sinks)


def swa_bwd(q, k, proj, sinks, y, lse, dmix, name):
    t = q.shape[0]
    nb = t // ATTN_BLOCK
    scale = HEAD_DIM ** -0.5
    grp = ATTN_HEADS // KV_HEADS

    def body(q_ref, kc_ref, kp_ref, vc_ref, vp_ref, s_ref, y_ref, lse_ref, dy_ref,
             dq_ref, dk_ref, dv_ref, ds_ref, dkc, dvc):
        n = pl.program_id(0)

        @pl.when(n == 0)
        def _():
            dkc[...] = jnp.zeros_like(dkc)
            dvc[...] = jnp.zeros_like(dvc)
            ds_ref[...] = jnp.zeros_like(ds_ref)

        @pl.when(n < nb)
        def _():
            valid = _swa_valid(n, grp)
            kk = jnp.concatenate([kp_ref[...], kc_ref[...]], axis=0).astype(MXU_DTYPE)
            vv = jnp.concatenate([vp_ref[...], vc_ref[...]], axis=0).astype(MXU_DTYPE)
            lane = lax.broadcasted_iota(jnp.int32, (1, ATTN_HEADS), 1)
            gs = range(KV_HEADS)
            kg = [kk[:, g * HEAD_DIM:(g + 1) * HEAD_DIM] for g in gs]
            vg = [vv[:, g * HEAD_DIM:(g + 1) * HEAD_DIM] for g in gs]
            qg = [_stack_heads(q_ref, g, grp).astype(MXU_DTYPE) for g in gs]
            dog = [_stack_heads(dy_ref, g, grp) for g in gs]
            og = [_stack_heads(y_ref, g, grp) for g in gs]
            lg = [jnp.concatenate([lse_ref[:, g * grp + j:g * grp + j + 1] for j in range(grp)], axis=0) for g in gs]
            sink = [_stack_sinks(s_ref, g, grp) for g in gs]
            sc = [jnp.where(valid, _mx_nt(qg[g], kg[g]) * scale, NEG) for g in gs]
            p = [jnp.exp(sc[g] - lg[g]) for g in gs]
            delta = [jnp.sum(dog[g] * og[g], axis=-1, keepdims=True) for g in gs]
            ds = [p[g] * (_mx_nt(dog[g], vg[g]) - delta[g]) for g in gs]
            dqg = [_mx(ds[g], kg[g]) * scale for g in gs]
            dkf = jnp.concatenate([_mx_tn(ds[g], qg[g]) * scale for g in gs], axis=-1)
            dvf = jnp.concatenate([_mx_tn(p[g], dog[g]) for g in gs], axis=-1)
            dsk = [jnp.exp(sink[g] - lg[g]) * delta[g] for g in gs]
            dsink = jnp.zeros((1, ATTN_HEADS), F32)
            dqs = []
            for h in range(ATTN_HEADS):
                rows = slice((h % grp) * ATTN_BLOCK, (h % grp + 1) * ATTN_BLOCK)
                dqs.append(dqg[h // grp][rows])
                dsink = jnp.where(lane == h, -jnp.sum(dsk[h // grp][rows], axis=0, keepdims=True), dsink)
            dq_ref[...] = jnp.concatenate(dqs, axis=-1)
            dk_ref[...] = dkc[...] + dkf[:ATTN_BLOCK]
            dv_ref[...] = (dvc[...] + dvf[:ATTN_BLOCK]).astype(dv_ref.dtype)
            dkc[...] = dkf[ATTN_BLOCK:]
            dvc[...] = dvf[ATTN_BLOCK:]
            ds_ref[...] += dsink

        @pl.when(n == nb)
        def _():
            dk_ref[...] = dkc[...]
            dv_ref[...] = dvc[...].astype(dv_ref.dtype)

    cur = lambda n: (jnp.minimum(n, nb - 1), 0)
    prev = lambda n: (jnp.clip(n - 1, 0, nb - 1), 0)
    vcol = (Q_W + KV_W) // KV_W
    return pl.pallas_call(
        body, name=name, grid=(nb + 1,),
        in_specs=[pl.BlockSpec((ATTN_BLOCK, Q_W), cur), pl.BlockSpec((ATTN_BLOCK, KV_W), cur),
                  pl.BlockSpec((ATTN_BLOCK, KV_W), prev),
                  pl.BlockSpec((ATTN_BLOCK, KV_W), lambda n: (jnp.minimum(n, nb - 1), vcol)),
                  pl.BlockSpec((ATTN_BLOCK, KV_W), lambda n: (jnp.clip(n - 1, 0, nb - 1), vcol)),
                  pl.BlockSpec((1, ATTN_HEADS), lambda n: (0, 0)),
                  pl.BlockSpec((ATTN_BLOCK, Q_W), cur), pl.BlockSpec((ATTN_BLOCK, ATTN_HEADS), cur),
                  pl.BlockSpec((ATTN_BLOCK, Q_W), cur)],
        out_specs=(pl.BlockSpec((ATTN_BLOCK, Q_W), cur), pl.BlockSpec((ATTN_BLOCK, KV_W), prev),
                   pl.BlockSpec((ATTN_BLOCK, KV_W), prev), pl.BlockSpec((1, ATTN_HEADS), lambda n: (0, 0))),
        out_shape=(jax.ShapeDtypeStruct((t, Q_W), F32), jax.ShapeDtypeStruct((t, KV_W), F32),
                   jax.ShapeDtypeStruct((t, KV_W), MXU_DTYPE), jax.ShapeDtypeStruct((1, ATTN_HEADS), F32)),
        scratch_shapes=[pltpu.VMEM((ATTN_BLOCK, KV_W), F32), pltpu.VMEM((ATTN_BLOCK, KV_W), F32)],
        compiler_params=_cp("arbitrary"))(q, k, k, proj, proj, sinks, y, lse, dmix)


GC_W = 256
_GB0, _GC0, _XI0 = 768 // GC_W, 1280 // GC_W, 1792 // GC_W
HALO = 8


def gconv_fwd(proj, conv_w, name, tm=512):
    t = proj.shape[0]
    hb = tm // HALO

    def body(gb_ref, gc_ref, xi_ref, gch_ref, xih_ref, w_ref, y_ref):
        i = pl.program_id(1)
        u = gc_ref[...] * xi_ref[...]
        uh = jnp.where(i == 0, 0.0, gch_ref[...] * xih_ref[...])
        up = jnp.concatenate([uh, u], axis=0)
        cv = w_ref[0:1, :] * up[HALO - 2:HALO - 2 + tm]
        cv = cv + w_ref[1:2, :] * up[HALO - 1:HALO - 1 + tm]
        cv = cv + w_ref[2:3, :] * u
        y_ref[...] = (gb_ref[...] * cv).astype(y_ref.dtype)

    def col(c0):
        return pl.BlockSpec((tm, GC_W), lambda cj, i: (i, c0 + cj))

    def halo(c0):
        return pl.BlockSpec((HALO, GC_W), lambda cj, i: (jnp.maximum(i * hb - 1, 0), c0 + cj))

    return pl.pallas_call(
        body, name=name, grid=(CONV_CH // GC_W, t // tm),
        in_specs=[col(_GB0), col(_GC0), col(_XI0), halo(_GC0), halo(_XI0),
                  pl.BlockSpec((3, GC_W), lambda cj, i: (0, cj))],
        out_specs=pl.BlockSpec((tm, GC_W), lambda cj, i: (i, cj)),
        out_shape=jax.ShapeDtypeStruct((t, CONV_CH), MXU_DTYPE),
        compiler_params=_cp("parallel", "parallel"))(proj, proj, proj, proj, proj, conv_w)


def gconv_bwd(proj, conv_w, dmix, name, tm=512):
    t = proj.shape[0]
    hb = tm // HALO
    nt = t // tm
    dy0 = Q_W // GC_W

    def body(gb_ref, gc_ref, xi_ref, gch_ref, xih_ref, gbn_ref, dyn_ref, dy_ref, w_ref,
             dgb_ref, dgc_ref, dxi_ref, dw_ref):
        i = pl.program_id(1)
        gc, xi, gb, dy = gc_ref[...], xi_ref[...], gb_ref[...], dy_ref[...]
        u = gc * xi
        uh = jnp.where(i == 0, 0.0, gch_ref[...] * xih_ref[...])
        up = jnp.concatenate([uh, u], axis=0)
        u2 = up[HALO - 2:HALO - 2 + tm]
        u1 = up[HALO - 1:HALO - 1 + tm]
        cv = w_ref[0:1, :] * u2 + w_ref[1:2, :] * u1 + w_ref[2:3, :] * u
        dgb_ref[...] = (dy * cv).astype(dgb_ref.dtype)
        dcv = dy * gb
        dcvn = jnp.where(i == nt - 1, 0.0, dyn_ref[...] * gbn_ref[...])
        dcvp = jnp.concatenate([dcv, dcvn], axis=0)
        du = w_ref[0:1, :] * dcvp[2:2 + tm] + w_ref[1:2, :] * dcvp[1:1 + tm] + w_ref[2:3, :] * dcv
        dgc_ref[...] = (du * xi).astype(dgc_ref.dtype)
        dxi_ref[...] = (du * gc).astype(dxi_ref.dtype)
        dw = jnp.concatenate([jnp.sum(dcv * u2, axis=0, keepdims=True), jnp.sum(dcv * u1, axis=0, keepdims=True),
                              jnp.sum(dcv * u, axis=0, keepdims=True)], axis=0)

        @pl.when(i == 0)
        def _():
            dw_ref[...] = dw

        @pl.when(i > 0)
        def _():
            dw_ref[...] += dw

    def col(c0):
        return pl.BlockSpec((tm, GC_W), lambda cj, i: (i, c0 + cj))

    def halo(c0):
        return pl.BlockSpec((HALO, GC_W), lambda cj, i: (jnp.maximum(i * hb - 1, 0), c0 + cj))

    def nxt(c0):
        return pl.BlockSpec((HALO, GC_W), lambda cj, i: (jnp.minimum((i + 1) * hb, t // HALO - 1), c0 + cj))

    out = pl.BlockSpec((tm, GC_W), lambda cj, i: (i, cj))
    return pl.pallas_call(
        body, name=name, grid=(CONV_CH // GC_W, nt),
        in_specs=[col(_GB0), col(_GC0), col(_XI0), halo(_GC0), halo(_XI0), nxt(_GB0), nxt(dy0), col(dy0),
                  pl.BlockSpec((3, GC_W), lambda cj, i: (0, cj))],
        out_specs=(out, out, out, pl.BlockSpec((3, GC_W), lambda cj, i: (0, cj))),
        out_shape=(jax.ShapeDtypeStruct((t, CONV_CH), MXU_DTYPE),) * 3 + (jax.ShapeDtypeStruct((3, CONV_CH), F32),),
        compiler_params=_cp("parallel", "arbitrary"))(proj, proj, proj, proj, proj, proj, dmix, dmix, conv_w)


_QKV_W = 3 * DN_W
_BA_COL = (4 * DN_W) // 128
_Z_COL = _QKV_W // DN_W


def gdn_prep_fwd(proj, conv_w, alog_row, dtb_row, name, tm=256):
    t = proj.shape[0]
    hb = tm // HALO
    qscale = DN_DIM ** -0.5

    def body(x_ref, xh_ref, w_ref, ba_ref, al_ref, dt_ref, q_ref, k_ref, v_ref, bg_ref):
        i = pl.program_id(0)
        for gi in range(3 * DN_HEADS):
            sl = slice(gi * DN_DIM, (gi + 1) * DN_DIM)
            xp = jnp.concatenate([jnp.where(i == 0, 0.0, xh_ref[:, sl]), x_ref[:, sl]], axis=0)
            c = w_ref[0:1, sl] * xp[HALO - 3:HALO - 3 + tm]
            for j in range(1, 4):
                c = c + w_ref[j:j + 1, sl] * xp[HALO - 3 + j:HALO - 3 + j + tm]
            s = c * _sigmoid(c)
            osl = slice((gi % DN_HEADS) * DN_DIM, (gi % DN_HEADS + 1) * DN_DIM)
            if gi < DN_HEADS:
                q_ref[:, osl] = s * lax.rsqrt(jnp.sum(s * s, axis=-1, keepdims=True) + EPS) * qscale
            elif gi < 2 * DN_HEADS:
                k_ref[:, osl] = s * lax.rsqrt(jnp.sum(s * s, axis=-1, keepdims=True) + EPS)
            else:
                v_ref[:, osl] = s
        ba = ba_ref[...]
        lane = lax.broadcasted_iota(jnp.int32, ba.shape, 1)
        gval = -jnp.exp(al_ref[...]) * _softplus(ba + dt_ref[...])
        bg_ref[...] = jnp.where(lane < DN_HEADS, _sigmoid(ba), jnp.where(lane < 2 * DN_HEADS, gval, 0.0))

    row = pl.BlockSpec((tm, DN_W), lambda i: (i, 0))
    one = pl.BlockSpec((1, 128), lambda i: (0, 0))
    return pl.pallas_call(
        body, name=name, grid=(t // tm,),
        in_specs=[pl.BlockSpec((tm, _QKV_W), lambda i: (i, 0)),
                  pl.BlockSpec((HALO, _QKV_W), lambda i: (jnp.maximum(i * hb - 1, 0), 0)),
                  pl.BlockSpec((4, _QKV_W), lambda i: (0, 0)),
                  pl.BlockSpec((tm, 128), lambda i: (i, _BA_COL)), one, one],
        out_specs=(row, row, row, pl.BlockSpec((tm, 128), lambda i: (i, 0))),
        out_shape=(jax.ShapeDtypeStruct((t, DN_W), F32),) * 3 + (jax.ShapeDtypeStruct((t, 128), F32),),
        compiler_params=_cp("parallel"))(proj, proj, conv_w, proj, alog_row, dtb_row)


def gdn_prep_bwd(proj, conv_w, alog_row, dtb_row, dq, dk, dv, dbg, name, tm=256):
    t = proj.shape[0]
    hb = tm // HALO
    nt = t // tm
    qscale = DN_DIM ** -0.5
    te = tm + HALO

    def body(x_ref, xh_ref, xn_ref, w_ref, ba_ref, al_ref, dt_ref, dq_ref, dk_ref, dv_ref,
             dqn_ref, dkn_ref, dvn_ref, dbg_ref, dx_ref, dba_ref, dw_ref, ddt_ref, dal_ref):
        i = pl.program_id(0)
        first = i == 0
        last = i == nt - 1
        dws = []
        for gi in range(3 * DN_HEADS):
            sl = slice(gi * DN_DIM, (gi + 1) * DN_DIM)
            osl = slice((gi % DN_HEADS) * DN_DIM, (gi % DN_HEADS + 1) * DN_DIM)
            xe = jnp.concatenate([jnp.where(first, 0.0, xh_ref[:, sl]), x_ref[:, sl], xn_ref[:, sl]], axis=0)
            c = w_ref[0:1, sl] * xe[HALO - 3:HALO - 3 + te]
            for j in range(1, 4):
                c = c + w_ref[j:j + 1, sl] * xe[HALO - 3 + j:HALO - 3 + j + te]
            sg = _sigmoid(c)
            s = c * sg
            d_ref, dn_ref = ((dq_ref, dqn_ref), (dk_ref, dkn_ref), (dv_ref, dvn_ref))[gi // DN_HEADS]
            dy = jnp.concatenate([d_ref[:, osl], jnp.where(last, 0.0, dn_ref[:, osl])], axis=0)
            if gi < 2 * DN_HEADS:
                r = lax.rsqrt(jnp.sum(s * s, axis=-1, keepdims=True) + EPS)
                sh = s * r
                ds = r * (dy - sh * jnp.sum(sh * dy, axis=-1, keepdims=True))
                if gi < DN_HEADS:
                    ds = ds * qscale
            else:
                ds = dy
            dc = ds * sg * (1.0 + c * (1.0 - sg))
            dcs = [dc[3 - j:3 - j + tm] for j in range(4)]
            dx = w_ref[0:1, sl] * dcs[0]
            for j in range(1, 4):
                dx = dx + w_ref[j:j + 1, sl] * dcs[j]
            dx_ref[:, sl] = dx.astype(dx_ref.dtype)
            x0 = x_ref[:, sl]
            dws.append(jnp.concatenate([jnp.sum(dcs[j] * x0, axis=0, keepdims=True) for j in range(4)], axis=0))
        dw = jnp.concatenate(dws, axis=-1)
        ba = ba_ref[...]
        dbgv = dbg_ref[...]
        lane = lax.broadcasted_iota(jnp.int32, ba.shape, 1)
        beta = _sigmoid(ba)
        ea = -jnp.exp(al_ref[...])
        zin = ba + dt_ref[...]
        is_b = lane < DN_HEADS
        is_a = (lane >= DN_HEADS) & (lane < 2 * DN_HEADS)
        da = jnp.where(is_a, dbgv * ea * _sigmoid(zin), 0.0)
        dba_ref[...] = jnp.where(is_b, dbgv * beta * (1.0 - beta), da).astype(dba_ref.dtype)
        ddt = jnp.sum(da, axis=0, keepdims=True)
        dal = jnp.sum(jnp.where(is_a, dbgv * ea * _softplus(zin), 0.0), axis=0, keepdims=True)

        @pl.when(first)
        def _():
            dw_ref[...] = dw
            ddt_ref[...] = ddt
            dal_ref[...] = dal

        @pl.when(i > 0)
        def _():
            dw_ref[...] += dw
            ddt_ref[...] += ddt
            dal_ref[...] += dal

    row = pl.BlockSpec((tm, DN_W), lambda i: (i, 0))
    nrow = pl.BlockSpec((HALO, DN_W), lambda i: (jnp.minimum((i + 1) * hb, t // HALO - 1), 0))
    one = pl.BlockSpec((1, 128), lambda i: (0, 0))
    return pl.pallas_call(
        body, name=name, grid=(nt,),
        in_specs=[pl.BlockSpec((tm, _QKV_W), lambda i: (i, 0)),
                  pl.BlockSpec((HALO, _QKV_W), lambda i: (jnp.maximum(i * hb - 1, 0), 0)),
                  pl.BlockSpec((HALO, _QKV_W), lambda i: (jnp.minimum((i + 1) * hb, t // HALO - 1), 0)),
                  pl.BlockSpec((4, _QKV_W), lambda i: (0, 0)),
                  pl.BlockSpec((tm, 128), lambda i: (i, _BA_COL)), one, one,
                  row, row, row, nrow, nrow, nrow, pl.BlockSpec((tm, 128), lambda i: (i, 0))],
        out_specs=(pl.BlockSpec((tm, _QKV_W), lambda i: (i, 0)), pl.BlockSpec((tm, 128), lambda i: (i, 0)),
                   pl.BlockSpec((4, _QKV_W), lambda i: (0, 0)), one, one),
        out_shape=(jax.ShapeDtypeStruct((t, _QKV_W), MXU_DTYPE), jax.ShapeDtypeStruct((t, 128), MXU_DTYPE),
                   jax.ShapeDtypeStruct((4, _QKV_W), F32), jax.ShapeDtypeStruct((1, 128), F32),
                   jax.ShapeDtypeStruct((1, 128), F32)),
        compiler_params=_cp("arbitrary"))(proj, proj, proj, conv_w, proj, alog_row, dtb_row, dq, dk, dv, dq, dk, dv, dbg)


def _chunk_masks():
    r = lax.broadcasted_iota(jnp.int32, (DN_CHUNK, DN_CHUNK), 0)
    c = lax.broadcasted_iota(jnp.int32, (DN_CHUNK, DN_CHUNK), 1)
    return r >= c, r > c


INV_PACK = 2


def _inv_unit_lower_many(mats):
    n = DN_CHUNK
    wide = INV_PACK * n
    r = lax.broadcasted_iota(jnp.int32, (wide, wide), 0)
    c = lax.broadcasted_iota(jnp.int32, (wide, wide), 1)
    same = (r // n) == (c // n)
    eye = jnp.where((r[:n] == (c[:n] % n)), 1.0, 0.0)

    def blockdiag(row):
        return jnp.where(same, jnp.concatenate([row] * INV_PACK, axis=0), 0.0)

    packs = [jnp.concatenate(mats[g:g + INV_PACK], axis=-1) for g in range(0, len(mats), INV_PACK)]
    xs = [eye - a for a in packs]
    pws = [_hi(a, blockdiag(a)) for a in packs]
    for step in range(5):
        if step < 4:
            both = [_hi(jnp.concatenate([x, pw], axis=0), blockdiag(pw)) for x, pw in zip(xs, pws)]
            xs = [x + b[:n] for x, b in zip(xs, both)]
            pws = [b[n:] for b in both]
        else:
            xs = [x + _hi(x, blockdiag(pw)) for x, pw in zip(xs, pws)]
    return [x[:, j * n:(j + 1) * n] for x in xs for j in range(INV_PACK)]


def _chunk_common(q, k, v, beta, gc, gcr, lower, strict):
    gam = jnp.exp(jnp.where(lower, gc - gcr, NEG))
    eg = jnp.exp(gc)
    gl = gc[DN_CHUNK - 1:DN_CHUNK, :]
    kdf = jnp.exp(gl - gc)
    kb = k * beta
    bmat = _mx_nt(kb, k)
    qmat = _mx_nt(q, k)
    return gam, eg, jnp.exp(gl), kdf, kb, bmat, qmat


def gdn_fwd(q, k, v, bg, name):
    t = q.shape[0]
    n_chunks = t // DN_CHUNK

    def body(q_ref, k_ref, v_ref, bg_ref, o_ref, sall_ref, tall_ref, s_ref):
        n = pl.program_id(0)

        @pl.when(n == 0)
        def _():
            s_ref[...] = jnp.zeros_like(s_ref)

        lower, strict = _chunk_masks()
        bgv = bg_ref[...]
        gcs = _hi(jnp.where(lower, 1.0, 0.0), bgv)
        gcs_t = gcs.T
        hs = range(DN_HEADS)
        sl = [slice(h * DN_DIM, (h + 1) * DN_DIM) for h in hs]
        st = [s_ref[h] for h in hs]
        for h in hs:
            sall_ref[0, h] = st[h]
        qh = [q_ref[:, sl[h]] for h in hs]
        kh = [k_ref[:, sl[h]] for h in hs]
        vh = [v_ref[:, sl[h]] for h in hs]
        beta = [bgv[:, h:h + 1] for h in hs]
        com = [_chunk_common(qh[h], kh[h], vh[h], beta[h], gcs[:, DN_HEADS + h:DN_HEADS + h + 1],
                             gcs_t[DN_HEADS + h:DN_HEADS + h + 1, :], lower, strict) for h in hs]
        gam, eg, dec, kdf, kb, bmat, qmat = zip(*com)
        tms = _inv_unit_lower_many([jnp.where(strict, bmat[h] * gam[h], 0.0) for h in hs])
        for h in hs:
            tall_ref[0, h] = tms[h]
        uw = [_hi(tms[h], jnp.concatenate([vh[h] * beta[h], kb[h] * eg[h]], axis=-1)) for h in hs]
        v_new = [uw[h][:, :DN_DIM] - _mx(uw[h][:, DN_DIM:], st[h]) for h in hs]
        o_st = [_mx(qh[h] * eg[h], st[h]) for h in hs]
        o_in = [_mx(qmat[h] * gam[h], v_new[h]) for h in hs]
        s_up = [_mx_tn(kh[h] * kdf[h], v_new[h]) for h in hs]
        for h in hs:
            o_ref[:, sl[h]] = o_st[h] + o_in[h]
            s_ref[h] = st[h] * dec[h] + s_up[h]

    row = pl.BlockSpec((DN_CHUNK, DN_W), lambda n: (n, 0))
    return pl.pallas_call(
        body, name=name, grid=(n_chunks,),
        in_specs=[row, row, row, pl.BlockSpec((DN_CHUNK, 128), lambda n: (n, 0))],
        out_specs=(row, pl.BlockSpec((1, DN_HEADS, DN_DIM, DN_DIM), lambda n: (n, 0, 0, 0)),
                   pl.BlockSpec((1, DN_HEADS, DN_CHUNK, DN_CHUNK), lambda n: (n, 0, 0, 0))),
        out_shape=(jax.ShapeDtypeStruct((t, DN_W), F32),
                   jax.ShapeDtypeStruct((n_chunks, DN_HEADS, DN_DIM, DN_DIM), F32),
                   jax.ShapeDtypeStruct((n_chunks, DN_HEADS, DN_CHUNK, DN_CHUNK), F32)),
        scratch_shapes=[pltpu.VMEM((DN_HEADS, DN_DIM, DN_DIM), F32)],
        compiler_params=_cp("arbitrary"))(q, k, v, bg)


def gdn_bwd(q, k, v, bg, sall, tall, do, name):
    t = q.shape[0]
    n_chunks = t // DN_CHUNK

    def body(q_ref, k_ref, v_ref, bg_ref, sall_ref, tall_ref, do_ref, dq_ref, dk_ref, dv_ref, dbg_ref, ds_ref):
        n = pl.program_id(0)

        @pl.when(n == 0)
        def _():
            ds_ref[...] = jnp.zeros_like(ds_ref)

        lower, strict = _chunk_masks()
        ltri = jnp.where(lower, 1.0, 0.0)
        bgv = bg_ref[...]
        gcs = _hi(ltri, bgv)
        gcs_t = gcs.T
        lane = lax.broadcasted_iota(jnp.int32, (DN_CHUNK, 128), 1)
        rowi = lax.broadcasted_iota(jnp.int32, (DN_CHUNK, 1), 0)
        hs = range(DN_HEADS)
        each = lambda fn, *ls: [fn(*a) for a in zip(*ls)]
        rsum = lambda a: jnp.sum(a, axis=-1, keepdims=True)
        sl = [slice(h * DN_DIM, (h + 1) * DN_DIM) for h in hs]
        st = [sall_ref[0, h] for h in hs]
        tms = [tall_ref[0, h] for h in hs]
        dsn = [ds_ref[h] for h in hs]
        qh = [q_ref[:, sl[h]] for h in hs]
        kh = [k_ref[:, sl[h]] for h in hs]
        vh = [v_ref[:, sl[h]] for h in hs]
        doh = [do_ref[:, sl[h]] for h in hs]
        beta = [bgv[:, h:h + 1] for h in hs]
        com = [_chunk_common(qh[h], kh[h], vh[h], beta[h], gcs[:, DN_HEADS + h:DN_HEADS + h + 1],
                             gcs_t[DN_HEADS + h:DN_HEADS + h + 1, :], lower, strict) for h in hs]
        gam, eg, dec, kdf, kb, bmat, qmat = zip(*com)
        rhs_w = each(lambda a, b: a * b, kb, eg)
        uw = each(lambda t_, v_, b_, r_: _hi(t_, jnp.concatenate([v_ * b_, r_], axis=-1)), tms, vh, beta, rhs_w)
        qd = each(lambda a, b: a * b, qh, eg)
        kd = each(lambda a, b: a * b, kh, kdf)
        pmat = each(lambda a, b: a * b, qmat, gam)
        v_new = each(lambda uw_, s_: uw_[:, :DN_DIM] - _mx(uw_[:, DN_DIM:], s_), uw, st)
        dqd = each(_mx_nt, doh, st)
        ds_o = each(_mx_tn, qd, doh)
        dp = each(lambda d_, v_: jnp.where(lower, _mx_nt(d_, v_), 0.0), doh, v_new)
        dvn_o = each(_mx_tn, pmat, doh)
        ddec = each(lambda d_, s_: jnp.sum(rsum(d_ * s_), axis=0, keepdims=True), dsn, st)
        dkd = each(_mx_nt, v_new, dsn)
        dvn = each(lambda a, k_, d_: a + _mx(k_, d_), dvn_o, kd, dsn)
        dw = each(lambda d_, s_: -_mx_nt(d_, s_), dvn, st)
        ds_w = each(lambda uw_, d_: _mx_tn(uw_[:, DN_DIM:], d_), uw, dvn)
        for h in hs:
            ds_ref[h] = ds_o[h] + dec[h] * dsn[h] - ds_w[h]
        dr = each(lambda t_, a, b: _hi_tn(t_, jnp.concatenate([a, b], axis=-1)), tms, dvn, dw)
        da = each(lambda r_, uw_: jnp.where(strict, -_hi_nt(r_, uw_), 0.0), dr, uw)
        dru = [r_[:, :DN_DIM] for r_ in dr]
        drw = [r_[:, DN_DIM:] for r_ in dr]
        db = each(lambda a, b: a * b, da, gam)
        dq_m = each(lambda a, b: a * b, dp, gam)
        e = each(lambda a, bm, p_, qm, g_: (a * bm + p_ * qm) * g_, da, bmat, dp, qmat, gam)
        dkb = each(lambda b_, k_, r_, e_: _mx(b_, k_) + r_ * e_, db, kh, drw, eg)
        dk = each(lambda b_, kb_, m_, q_, d_, f_: _mx_tn(b_, kb_) + _mx_tn(m_, q_) + d_ * f_, db, kb, dq_m, qh, dkd, kdf)
        dq = each(lambda m_, k_, d_, e_: _mx(m_, k_) + d_ * e_, dq_m, kh, dqd, eg)
        tk = each(lambda a, b: rsum(a * b), dkd, kd)
        dbeta_all = jnp.zeros((DN_CHUNK, 128), F32)
        dgc_all = jnp.zeros((DN_CHUNK, 128), F32)
        for h in hs:
            dgc = (jnp.sum(e[h], axis=1, keepdims=True) - jnp.sum(e[h].T, axis=1, keepdims=True)
                   + rsum(dqd[h] * qd[h]) - tk[h] + rsum(drw[h] * rhs_w[h]))
            dgl = jnp.sum(tk[h], axis=0, keepdims=True) + ddec[h] * dec[h]
            dgc = dgc + jnp.where(rowi == DN_CHUNK - 1, dgl, 0.0)
            dbeta = rsum(dru[h] * vh[h]) + rsum(dkb[h] * kh[h])
            dq_ref[:, sl[h]] = dq[h]
            dk_ref[:, sl[h]] = dk[h] + dkb[h] * beta[h]
            dv_ref[:, sl[h]] = dru[h] * beta[h]
            dbeta_all = jnp.where(lane == h, dbeta, dbeta_all)
            dgc_all = jnp.where(lane == DN_HEADS + h, dgc, dgc_all)
        dbg_ref[...] = dbeta_all + _hi_tn(ltri, dgc_all)

    rev = lambda n: (n_chunks - 1 - n, 0)
    row = pl.BlockSpec((DN_CHUNK, DN_W), rev)
    small = pl.BlockSpec((DN_CHUNK, 128), rev)
    return pl.pallas_call(
        body, name=name, grid=(n_chunks,),
        in_specs=[row, row, row, small,
                  pl.BlockSpec((1, DN_HEADS, DN_DIM, DN_DIM), lambda n: (n_chunks - 1 - n, 0, 0, 0)),
                  pl.BlockSpec((1, DN_HEADS, DN_CHUNK, DN_CHUNK), lambda n: (n_chunks - 1 - n, 0, 0, 0)), row],
        out_specs=(row, row, row, small),
        out_shape=(jax.ShapeDtypeStruct((t, DN_W), F32),) * 3 + (jax.ShapeDtypeStruct((t, 128), F32),),
        scratch_shapes=[pltpu.VMEM((DN_HEADS, DN_DIM, DN_DIM), F32)],
        compiler_params=_cp("arbitrary"))(q, k, v, bg, sall, tall, do)


def gdn_out_fwd(o, proj, o_gain, name, tm=256):
    t = o.shape[0]

    def body(o_ref, z_ref, g_ref, y_ref, yt_ref):
        for h in range(DN_HEADS):
            sl = slice(h * DN_DIM, (h + 1) * DN_DIM)
            ov, zv = o_ref[:, sl], z_ref[:, sl]
            r = lax.rsqrt(jnp.mean(ov * ov, axis=-1, keepdims=True) + EPS)
            y = ov * r * g_ref[...] * (zv * _sigmoid(zv))
            y_ref[:, sl] = y.astype(y_ref.dtype)
            yt_ref[sl, :] = y.T.astype(yt_ref.dtype)

    row = pl.BlockSpec((tm, DN_W), lambda i: (i, 0))
    return pl.pallas_call(
        body, name=name, grid=(t // tm,),
        in_specs=[row, pl.BlockSpec((tm, DN_W), lambda i: (i, _Z_COL)), pl.BlockSpec((1, DN_DIM), lambda i: (0, 0))],
        out_specs=(row, pl.BlockSpec((DN_W, tm), lambda i: (0, i))),
        out_shape=(jax.ShapeDtypeStruct((t, DN_W), MXU_DTYPE), jax.ShapeDtypeStruct((DN_W, t), MXU_DTYPE)),
        compiler_params=_cp("parallel"))(o, proj, o_gain)


def gdn_out_bwd(o, proj, o_gain, dy, name, tm=256):
    t = o.shape[0]

    def body(o_ref, z_ref, g_ref, dy_ref, do_ref, dz_ref, dg_ref):
        i = pl.program_id(0)
        dg = jnp.zeros((1, DN_DIM), F32)
        for h in range(DN_HEADS):
            sl = slice(h * DN_DIM, (h + 1) * DN_DIM)
            ov, zv, dyv = o_ref[:, sl], z_ref[:, sl], dy_ref[:, sl]
            r = lax.rsqrt(jnp.mean(ov * ov, axis=-1, keepdims=True) + EPS)
            oh = ov * r
            sg = _sigmoid(zv)
            dz_ref[:, sl] = (dyv * oh * g_ref[...] * sg * (1.0 + zv * (1.0 - sg))).astype(dz_ref.dtype)
            don = dyv * (zv * sg)
            dg = dg + jnp.sum(don * oh, axis=0, keepdims=True)
            doh = don * g_ref[...]
            do_ref[:, sl] = r * (doh - oh * jnp.mean(doh * oh, axis=-1, keepdims=True))

        @pl.when(i == 0)
        def _():
            dg_ref[...] = dg

        @pl.when(i > 0)
        def _():
            dg_ref[...] += dg

    row = pl.BlockSpec((tm, DN_W), lambda i: (i, 0))
    one = pl.BlockSpec((1, DN_DIM), lambda i: (0, 0))
    return pl.pallas_call(
        body, name=name, grid=(t // tm,),
        in_specs=[row, pl.BlockSpec((tm, DN_W), lambda i: (i, _Z_COL)), one, row],
        out_specs=(row, row, one),
        out_shape=(jax.ShapeDtypeStruct((t, DN_W), F32), jax.ShapeDtypeStruct((t, DN_W), MXU_DTYPE),
                   jax.ShapeDtypeStruct((1, DN_DIM), F32)),
        compiler_params=_cp("arbitrary"))(o, proj, o_gain, dy)


def _peer(k):
    x, y, c = lax.axis_index("x"), lax.axis_index("y"), lax.axis_index("c")
    px = 1 - x if k & 4 else x
    py = 1 - y if k & 2 else y
    pc = 1 - c if k & 1 else c
    return (px, py, pc), 4 * px + 2 * py + pc


def all_gather(shards, name):
    na = len(shards)

    def body(*refs):
        ins, outs = refs[:na], refs[na:2 * na]
        send_sems, recv_sems, local_sems = refs[2 * na:]
        _, me = _peer(0)
        local = [pltpu.make_async_copy(ins[a], outs[a].at[me], local_sems.at[a]) for a in range(na)]
        for cp in local:
            cp.start()
        sends = []
        for k in range(1, N_DEV):
            peer, _ = _peer(k)
            for a in range(na):
                cp = pltpu.make_async_remote_copy(
                    src_ref=ins[a], dst_ref=outs[a].at[me], send_sem=send_sems.at[a, k - 1],
                    recv_sem=recv_sems.at[a, k - 1], device_id=peer, device_id_type=MESH)
                cp.start()
                sends.append(cp)
        for k in range(1, N_DEV):
            peer, pid = _peer(k)
            for a in range(na):
                pltpu.make_async_remote_copy(
                    src_ref=ins[a], dst_ref=outs[a].at[pid], send_sem=send_sems.at[a, k - 1],
                    recv_sem=recv_sems.at[a, k - 1], device_id=peer, device_id_type=MESH).wait_recv()
        for cp in sends:
            cp.wait_send()
        for cp in local:
            cp.wait()

    anyspec = pl.BlockSpec(memory_space=pl.ANY)
    return pl.pallas_call(
        body, name=name, in_specs=[anyspec] * na, out_specs=tuple([anyspec] * na),
        out_shape=tuple(jax.ShapeDtypeStruct((N_DEV,) + s.shape, s.dtype) for s in shards),
        scratch_shapes=[pltpu.SemaphoreType.DMA((na, N_DEV - 1)), pltpu.SemaphoreType.DMA((na, N_DEV - 1)),
                        pltpu.SemaphoreType.DMA((na,))],
        compiler_params=pltpu.CompilerParams(has_side_effects=True))(*shards)


_HBM = pl.BlockSpec(memory_space=pltpu.HBM)
_SEM = pl.BlockSpec(memory_space=pltpu.SEMAPHORE)
_DATAFLOW = pltpu.SideEffectType.DATAFLOW_SIDE_EFFECTING
N_PEER = N_DEV - 1


def send_start(srcs, name, scatter, after):
    na = len(srcs)
    ns = 2 * N_PEER * na
    lands = [lax.empty((N_DEV,) + (s.shape[1:] if scatter else s.shape), s.dtype) for s in srcs]
    extra = [] if after is None else [after]

    def body(*refs):
        src_refs, land_refs = refs[:na], refs[na:2 * na]
        sems, token = refs[2 * na + len(extra):2 * na + len(extra) + ns], refs[-1]
        _, me = _peer(0)
        for k in range(1, N_DEV):
            peer, pid = _peer(k)
            for a in range(na):
                pltpu.make_async_remote_copy(
                    src_ref=src_refs[a].at[pid] if scatter else src_refs[a], dst_ref=land_refs[a].at[me],
                    send_sem=sems[2 * (a * N_PEER + k - 1)], recv_sem=sems[2 * (a * N_PEER + k - 1) + 1],
                    device_id=peer, device_id_type=MESH).start()
        token[...] = jnp.zeros_like(token)

    hbm = lambda arrs: tuple(pltpu.HBM(a.shape, a.dtype) for a in arrs)
    outs = pl.pallas_call(
        body, name=name,
        out_shape=(pltpu.SemaphoreType.DMA(()),) * ns + hbm(srcs) + hbm(lands) + (jax.ShapeDtypeStruct((8, 128), F32),),
        in_specs=[_HBM] * (2 * na) + [pl.BlockSpec(memory_space=pl.ANY)] * len(extra),
        out_specs=(_SEM,) * ns + (_HBM,) * (2 * na) + (pl.BlockSpec(memory_space=pltpu.VMEM),),
        input_output_aliases={i: ns + i for i in range(2 * na)},
        compiler_params=pltpu.CompilerParams(has_side_effects=_DATAFLOW),
    )(*[pltpu.with_memory_space_constraint(a, pltpu.HBM) for a in list(srcs) + lands], *extra)
    return outs[:ns], outs[ns:ns + na], outs[ns + na:ns + 2 * na], outs[-1]


def send_wait(sems, srcs_thru, lands_thru, name, scatter, after):
    na = len(srcs_thru)
    ns = 2 * N_PEER * na

    def body(*refs):
        src_refs, land_refs, sm = refs[:na], refs[na:2 * na], refs[2 * na:2 * na + ns]
        land_out, local_sems = refs[3 * na + ns + 1:4 * na + ns + 1], refs[-1]
        _, me = _peer(0)
        own = [pltpu.make_async_copy(src_refs[a].at[me] if scatter else src_refs[a], land_out[a].at[me], local_sems.at[a])
               for a in range(na)]
        for cp in own:
            cp.start()
        for k in range(1, N_DEV):
            peer, pid = _peer(k)
            for a in range(na):
                cp = pltpu.make_async_remote_copy(
                    src_ref=src_refs[a].at[pid] if scatter else src_refs[a], dst_ref=land_refs[a].at[pid],
                    send_sem=sm[2 * (a * N_PEER + k - 1)], recv_sem=sm[2 * (a * N_PEER + k - 1) + 1],
                    device_id=peer, device_id_type=MESH)
                cp.wait_send()
                cp.wait_recv()
        for cp in own:
            cp.wait()

    hbm = lambda arrs: tuple(pltpu.HBM(a.shape, a.dtype) for a in arrs)
    outs = pl.pallas_call(
        body, name=name, out_shape=hbm(srcs_thru) + hbm(lands_thru),
        in_specs=[_HBM] * (2 * na) + [_SEM] * ns + [pl.BlockSpec(memory_space=pl.ANY)], out_specs=(_HBM,) * (2 * na),
        input_output_aliases={i: i for i in range(2 * na)}, scratch_shapes=[pltpu.SemaphoreType.DMA((na,))],
        compiler_params=pltpu.CompilerParams(has_side_effects=_DATAFLOW),
    )(*srcs_thru, *lands_thru, *sems, after)
    return outs[na:]


def _adamw(w, g, m, v):
    m = ADAM_B1 * m + (1.0 - ADAM_B1) * g
    v = ADAM_B2 * v + (1.0 - ADAM_B2) * (g * g)
    m_hat = m / (1.0 - ADAM_B1 ** ADAM_STEP)
    v_hat = v / (1.0 - ADAM_B2 ** ADAM_STEP)
    return -ADAM_LR * (m_hat / (jnp.sqrt(v_hat) + ADAM_EPS) + ADAM_WD * w), m, v


def adam_sum(w, pieces, m, v, name):
    r, c = w.shape
    tr = r
    for cand in (256, 128, 64, 32, 16, 8):
        if r % cand == 0:
            tr = cand
            break

    def body(w_ref, p_ref, m_ref, v_ref, g_ref, d_ref, nm_ref, nv_ref):
        g = p_ref[0].astype(F32)
        for s in range(1, N_DEV):
            g = g + p_ref[s].astype(F32)
        g_ref[...] = g
        d_ref[...], nm_ref[...], nv_ref[...] = _adamw(w_ref[...], g, m_ref[...], v_ref[...])

    row = pl.BlockSpec((tr, c), lambda i: (i, 0))
    out = jax.ShapeDtypeStruct((r, c), F32)
    return pl.pallas_call(
        body, name=name, grid=(r // tr,),
        in_specs=[row, pl.BlockSpec((N_DEV, tr, c), lambda i: (0, i, 0)), row, row],
        out_specs=(row,) * 4, out_shape=(out,) * 4, compiler_params=_cp("parallel"))(w, pieces, m, v)


def sum_rows(gathered, name):
    _, r, c = gathered.shape

    def body(p_ref, o_ref):
        g = p_ref[0]
        for s in range(1, N_DEV):
            g = g + p_ref[s]
        o_ref[...] = g

    return pl.pallas_call(body, name=name, out_shape=jax.ShapeDtypeStruct((r, c), F32))(gathered)


def adam_small(w, g, m, v, name):
    def body(w_ref, g_ref, m_ref, v_ref, d_ref, nm_ref, nv_ref):
        d_ref[...], nm_ref[...], nv_ref[...] = _adamw(w_ref[...], g_ref[...], m_ref[...], v_ref[...])

    out = jax.ShapeDtypeStruct(w.shape, F32)
    return pl.pallas_call(body, name=name, out_shape=(out,) * 3)(w, g, m, v)


def _rope_tables(t):
    inv_freq = 10000.0 ** (-jnp.arange(0, HEAD_DIM, 2, dtype=F32) / HEAD_DIM)
    ang = jnp.arange(t, dtype=F32)[:, None] * inv_freq[None, :]
    cos, sin = jnp.cos(ang), jnp.sin(ang)
    return jnp.concatenate([cos, cos], axis=-1), jnp.concatenate([sin, sin], axis=-1)


def _lane_row(vec8):
    return jnp.pad(vec8.reshape(1, DN_HEADS), ((0, 0), (DN_HEADS, 128 - 2 * DN_HEADS)))


def _ffn_fwd(x, norm_g, w_gu, w_d, tag):
    f, ft = rms_fwd(x, norm_g, f"{tag}_norm")
    gu, a, at = ffn_up(f, w_gu, f"{tag}_gate_up")
    return mm_nn(a, w_d, f"{tag}_down", res=x), (ft, gu, at)


def _ffn_bwd(x, norm_g, w_gu, w_d, saved, dy, tag, after=None):
    ft, gu, at = saved
    dgu = ffn_dact(dy, w_d, gu, f"{tag}_d_gate_up", after=after)
    dwd = mm_at(at, dy, f"{tag}_dw_down")
    df = mm_nt(dgu, w_gu, f"{tag}_d_normed")
    dwgu = mm_at(ft, dgu, f"{tag}_dw_gate_up")
    dx, dg = rms_bwd(x, norm_g, df, dy, f"{tag}_d_norm")
    return dx, dwgu, dwd, dg


def local_step(x, target, small, weights_of, grads_out, after=None):
    t = x.shape[0]
    cosf, sinf = _rope_tables(t)
    alog_row, dtb_row = _lane_row(small["odd_a_log"]), _lane_row(small["odd_dt_bias"])

    h0, h0t = rms_fwd(x, small["even_norm"], "even_norm", after=after)
    we = weights_of("even", h0)
    proj0 = mm_nn(h0, we["w_in"], "even_in_proj")
    qr, kr = qk_prep_fwd(proj0, small["even_q_gain"], small["even_k_gain"], cosf, sinf, "even_qk_prep")
    y_attn, lse = swa_fwd(qr, kr, proj0, small["even_sinks"], "even_swa")
    y_conv = gconv_fwd(proj0, small["even_conv_w"], "even_gconv")
    mix0 = jnp.concatenate([y_attn.astype(MXU_DTYPE), y_conv], axis=-1)
    x1 = mm_nn(mix0, we["w_out"], "even_out_proj", res=x)
    w0 = weights_of("ffn0", x1)
    x2, ffn0 = _ffn_fwd(x1, small["ffn_norm0"], w0["gate_up"], w0["down"], "ffn0")

    wo = weights_of("odd", x2)
    h1, h1t = rms_fwd(x2, small["odd_norm"], "odd_norm")
    proj1 = mm_nn(h1, wo["w_in"], "odd_in_proj")
    qn, kn, vs, bg = gdn_prep_fwd(proj1, small["odd_conv_w"], alog_row, dtb_row, "odd_prep")
    o, sall, tall = gdn_fwd(qn, kn, vs, bg, "odd_delta_rule")
    og, ogt = gdn_out_fwd(o, proj1, small["odd_o_gain"], "odd_gate_norm")
    x3 = mm_nn(og, wo["w_out"], "odd_out_proj", res=x2)
    w1 = weights_of("ffn1", x3)
    x4, ffn1 = _ffn_fwd(x3, small["ffn_norm1"], w1["gate_up"], w1["down"], "ffn1")

    dy, loss_row = loss_head(x4, target, "loss_head")

    gs = {}
    dx3, dwgu, dwd, gs["ffn_norm1"] = _ffn_bwd(x3, small["ffn_norm1"], w1["gate_up"], w1["down"], ffn1, dy, "ffn1")
    tok = grads_out("ffn1", {"gate_up": dwgu, "down": dwd})

    dog = mm_nt(dx3, wo["w_out"], "odd_d_gated", after=tok)
    dwo = mm_at(ogt, dx3, "odd_dw_out")
    do, dz, gs["odd_o_gain"] = gdn_out_bwd(o, proj1, small["odd_o_gain"], dog, "odd_d_gate_norm")
    dqn, dkn, dvs, dbg = gdn_bwd(qn, kn, vs, bg, sall, tall, do, "odd_d_delta_rule")
    dqkv, dba, gs["odd_conv_w"], ddt_row, dal_row = gdn_prep_bwd(
        proj1, small["odd_conv_w"], alog_row, dtb_row, dqn, dkn, dvs, dbg, "odd_d_prep")
    gs["odd_dt_bias"] = ddt_row[:, DN_HEADS:2 * DN_HEADS]
    gs["odd_a_log"] = dal_row[:, DN_HEADS:2 * DN_HEADS]
    dproj1 = jnp.concatenate([dqkv, dz, dba], axis=-1)
    dh1 = mm_nt(dproj1, wo["w_in"], "odd_d_normed")
    dwi = mm_at(h1t, dproj1, "odd_dw_in")
    dx2, gs["odd_norm"] = rms_bwd(x2, small["odd_norm"], dh1, dx3, "odd_d_norm")
    tok = grads_out("odd", {"w_in": dwi, "w_out": dwo})

    dx1, dwgu, dwd, gs["ffn_norm0"] = _ffn_bwd(x1, small["ffn_norm0"], w0["gate_up"], w0["down"], ffn0, dx2, "ffn0",
                                               after=tok)
    tok = grads_out("ffn0", {"gate_up": dwgu, "down": dwd})

    dmix = mm_nt(dx1, we["w_out"], "even_d_mix", after=tok)
    dwo = mm_at(mix0.T, dx1, "even_dw_out")
    dqr, dkr, dv, gs["even_sinks"] = swa_bwd(qr, kr, proj0, small["even_sinks"], y_attn, lse, dmix, "even_d_swa")
    dqk, gs["even_q_gain"], gs["even_k_gain"] = qk_prep_bwd(
        proj0, small["even_q_gain"], small["even_k_gain"], cosf, sinf, dqr, dkr, "even_d_qk_prep")
    dgb, dgc, dxi, gs["even_conv_w"] = gconv_bwd(proj0, small["even_conv_w"], dmix, "even_d_gconv")
    dproj0 = jnp.concatenate([dqk, dv, dgb, dgc, dxi], axis=-1)
    dwi = mm_at(h0t, dproj0, "even_dw_in")
    tok = grads_out("even", {"w_in": dwi, "w_out": dwo})
    dh0 = mm_nt(dproj0, we["w_in"], "even_d_normed", after=tok)
    grad_x, gs["even_norm"] = rms_bwd(x, small["even_norm"], dh0, dx1, "even_d_norm")
    return loss_row, grad_x, gs


_SMALL_ORDER = ("even_norm", "even_q_gain", "even_k_gain", "even_sinks", "odd_a_log", "odd_dt_bias", "odd_o_gain",
                "ffn_norm0", "ffn_norm1", "odd_norm", "even_conv_w", "odd_conv_w")
_SMALL_SIZE = {"even_norm": 1024, "even_q_gain": 64, "even_k_gain": 64, "even_sinks": 8, "odd_a_log": 8,
               "odd_dt_bias": 8, "odd_o_gain": 128, "ffn_norm0": 1024, "ffn_norm1": 1024, "odd_norm": 1024,
               "even_conv_w": 3 * 512, "odd_conv_w": 4 * 3072}
_N_REPL = 9


def _pack_rows(vals):
    flat = jnp.concatenate([v.reshape(-1) for v in vals])
    pad = (-flat.shape[0]) % 1024
    return jnp.pad(flat, (0, pad)).reshape(-1, 128)


def _my_block(full, size, axis):
    me = 4 * lax.axis_index("x") + 2 * lax.axis_index("y") + lax.axis_index("c")
    return lax.dynamic_slice_in_dim(full, me * size, size, axis=axis)


def _col_gathered(g):
    return g.transpose(1, 0, 2).reshape(g.shape[1], N_DEV * g.shape[2])


def _col_pieces(dw):
    k, n8 = dw.shape
    return dw.reshape(k, N_DEV, n8 // N_DEV).transpose(1, 0, 2)


def kernel(x, even_norm, even_w_in, even_q_gain, even_k_gain, even_sinks, even_conv_w, even_w_out, odd_norm, odd_w_in, odd_conv_w, odd_a_log, odd_dt_bias, odd_o_gain, odd_w_out, ffn_norm, ffn_w_gate_up, ffn_w_down, loss_target, m_even_norm, m_even_w_in, m_even_q_gain, m_even_k_gain, m_even_sinks, m_even_conv_w, m_even_w_out, m_odd_norm, m_odd_w_in, m_odd_conv_w, m_odd_a_log, m_odd_dt_bias, m_odd_o_gain, m_odd_w_out, m_ffn_norm, m_ffn_w_gate_up, m_ffn_w_down, v_even_norm, v_even_w_in, v_even_q_gain, v_even_k_gain, v_even_sinks, v_even_conv_w, v_even_w_out, v_odd_norm, v_odd_w_in, v_odd_conv_w, v_odd_a_log, v_odd_dt_bias, v_odd_o_gain, v_odd_w_out, v_ffn_norm, v_ffn_w_gate_up, v_ffn_w_down):
    t = x.shape[1]
    d = D_MODEL

    me = 4 * lax.axis_index("x") + 2 * lax.axis_index("y") + lax.axis_index("c")
    fpd = D_FF // N_DEV
    shard = {
        "even": {"w_in": even_w_in.reshape(d, EVEN_IN_W // N_DEV), "w_out": even_w_out.reshape(d // N_DEV, d)},
        "ffn0": {"gate_up": ffn_w_gate_up[0], "down": ffn_w_down[0]},
        "odd": {"w_in": odd_w_in.reshape(d, ODD_IN_W // N_DEV), "w_out": odd_w_out.reshape(d // N_DEV, d)},
        "ffn1": {"gate_up": ffn_w_gate_up[1], "down": ffn_w_down[1]},
    }
    mom_m = {
        "even": {"w_in": m_even_w_in, "w_out": m_even_w_out}, "odd": {"w_in": m_odd_w_in, "w_out": m_odd_w_out},
        "ffn0": {"gate_up": m_ffn_w_gate_up[0], "down": m_ffn_w_down[0]},
        "ffn1": {"gate_up": m_ffn_w_gate_up[1], "down": m_ffn_w_down[1]},
    }
    mom_v = {
        "even": {"w_in": v_even_w_in, "w_out": v_even_w_out}, "odd": {"w_in": v_odd_w_in, "w_out": v_odd_w_out},
        "ffn0": {"gate_up": v_ffn_w_gate_up[0], "down": v_ffn_w_down[0]},
        "ffn1": {"gate_up": v_ffn_w_gate_up[1], "down": v_ffn_w_down[1]},
    }

    def whole(group, parts):
        col, row = tuple(shard[group])
        w_col = _gu_gathered(parts[0]) if col == "gate_up" else _col_gathered(parts[0])
        if group == "odd":
            w_col = jnp.pad(w_col, ((0, 0), (0, ODD_IN_PAD - ODD_IN_W)))
        return {col: w_col, row: parts[1].reshape(-1, d)}

    shard_rows = _pack_rows([odd_norm, even_conv_w, odd_conv_w])
    (small_g,) = all_gather([shard_rows], "gather_small_weights")
    wire = {g: [a.astype(MXU_DTYPE) for a in shard[g].values()] for g in shard}
    gathers, tok = {}, small_g
    for g in shard:
        sems, srcs_thru, lands_thru, tok = send_start(wire[g], f"gather_{g}_start", False, tok)
        gathers[g] = (sems, srcs_thru, lands_thru)

    def weights_of(group, after):
        lands = send_wait(*gathers[group], f"gather_{group}_wait", False, after)
        return whole(group, lands)

    sent = {}

    def grads_out(group, dws):
        col, row = tuple(shard[group])
        n_cols = N_DEV * shard[group][col].shape[1]
        pieces = [_gu_pieces(dws[col]) if col == "gate_up" else _col_pieces(dws[col][:, :n_cols]),
                  dws[row].reshape((N_DEV,) + shard[group][row].shape)]
        sems, srcs_thru, lands_thru, token = send_start(pieces, f"exchange_{group}_start", True, None)
        sent[group] = (sems, srcs_thru, lands_thru, pieces)
        return token

    sg = small_g.reshape(N_DEV, -1)
    o1 = d // N_DEV
    o2 = o1 + 3 * CONV_CH // N_DEV
    small = {
        "even_norm": even_norm, "even_q_gain": even_q_gain, "even_k_gain": even_k_gain, "even_sinks": even_sinks,
        "odd_a_log": odd_a_log.reshape(-1), "odd_dt_bias": odd_dt_bias.reshape(-1), "odd_o_gain": odd_o_gain,
        "ffn_norm0": ffn_norm[0:1], "ffn_norm1": ffn_norm[1:2],
        "odd_norm": sg[:, :o1].reshape(1, d),
        "even_conv_w": sg[:, o1:o2].reshape(N_DEV, 3, CONV_CH // N_DEV).transpose(1, 0, 2).reshape(3, CONV_CH),
        "odd_conv_w": sg[:, o2:o2 + 4 * _QKV_W // N_DEV].reshape(N_DEV, 4, _QKV_W // N_DEV).transpose(1, 0, 2).reshape(4, _QKV_W),
    }

    loss_row, grad_x, gs = local_step(x.reshape(t, d), loss_target.reshape(t, d), small, weights_of, grads_out, after=tok)

    rows = _pack_rows([gs[n] for n in _SMALL_ORDER] + [loss_row[:, 0:1]])
    (rows_g,) = all_gather([rows], "gather_small_grads")
    tot = sum_rows(rows_g, "sum_small_grads").reshape(-1)
    off, sgrad = 0, {}
    for n in _SMALL_ORDER:
        sgrad[n] = tot[off:off + _SMALL_SIZE[n]]
        off += _SMALL_SIZE[n]
    loss = tot[off]

    upd, behind = {}, tot
    for g in ("ffn1", "odd", "ffn0", "even"):
        sems, srcs_thru, lands_thru, pieces = sent[g]
        lands = send_wait(sems, srcs_thru, lands_thru, f"exchange_{g}_wait", True, behind)
        for (key, w2), pcs in zip(shard[g].items(), lands):
            upd[g, key] = adam_sum(w2, pcs, mom_m[g][key].reshape(w2.shape), mom_v[g][key].reshape(w2.shape),
                                   f"adamw_{g}_{key}")
        behind = upd[g, key][0]
    res = {
        "even_w_in": tuple(o.reshape(even_w_in.shape) for o in upd["even", "w_in"]),
        "even_w_out": tuple(o.reshape(even_w_out.shape) for o in upd["even", "w_out"]),
        "odd_w_in": tuple(o.reshape(odd_w_in.shape) for o in upd["odd", "w_in"]),
        "odd_w_out": tuple(o.reshape(odd_w_out.shape) for o in upd["odd", "w_out"]),
        "ffn_w_gate_up": tuple(jnp.stack([a, b]) for a, b in zip(upd["ffn0", "gate_up"], upd["ffn1", "gate_up"])),
        "ffn_w_down": tuple(jnp.stack([a, b]) for a, b in zip(upd["ffn0", "down"], upd["ffn1", "down"])),
    }

    repl = _SMALL_ORDER[:_N_REPL]
    repl_w = {"even_norm": even_norm, "even_q_gain": even_q_gain, "even_k_gain": even_k_gain, "even_sinks": even_sinks,
              "odd_a_log": odd_a_log, "odd_dt_bias": odd_dt_bias, "odd_o_gain": odd_o_gain,
              "ffn_norm0": ffn_norm[0], "ffn_norm1": ffn_norm[1]}
    repl_m = {"even_norm": m_even_norm, "even_q_gain": m_even_q_gain, "even_k_gain": m_even_k_gain,
              "even_sinks": m_even_sinks, "odd_a_log": m_odd_a_log, "odd_dt_bias": m_odd_dt_bias,
              "odd_o_gain": m_odd_o_gain, "ffn_norm0": m_ffn_norm[0], "ffn_norm1": m_ffn_norm[1]}
    repl_v = {"even_norm": v_even_norm, "even_q_gain": v_even_q_gain, "even_k_gain": v_even_k_gain,
              "even_sinks": v_even_sinks, "odd_a_log": v_odd_a_log, "odd_dt_bias": v_odd_dt_bias,
              "odd_o_gain": v_odd_o_gain, "ffn_norm0": v_ffn_norm[0], "ffn_norm1": v_ffn_norm[1]}
    pk = lambda dct: _pack_rows([dct[n] for n in repl])
    pd_, pm_, pv_ = adam_small(pk(repl_w), pk(sgrad), pk(repl_m), pk(repl_v), "adamw_replicated")
    sres = {}
    off = 0
    for n in repl:
        sz = _SMALL_SIZE[n]
        sres[n] = (sgrad[n], pd_.reshape(-1)[off:off + sz], pm_.reshape(-1)[off:off + sz], pv_.reshape(-1)[off:off + sz])
        off += sz
    g_on = _my_block(sgrad["odd_norm"].reshape(1, d), d // N_DEV, 1)
    g_ec = _my_block(sgrad["even_conv_w"].reshape(3, CONV_CH), CONV_CH // N_DEV, 1)
    g_oc = _my_block(sgrad["odd_conv_w"].reshape(4, _QKV_W), _QKV_W // N_DEV, 1)
    shard_w = _pack_rows([odd_norm, even_conv_w, odd_conv_w])
    sd_, sm_, sv_ = adam_small(shard_w, _pack_rows([g_on, g_ec, g_oc]),
                               _pack_rows([m_odd_norm, m_even_conv_w, m_odd_conv_w]),
                               _pack_rows([v_odd_norm, v_even_conv_w, v_odd_conv_w]), "adamw_sharded_small")
    off = 0
    for n, gfull, like in (("odd_norm", g_on, odd_norm), ("even_conv_w", g_ec, even_conv_w), ("odd_conv_w", g_oc, odd_conv_w)):
        sz = like.size
        sres[n] = (gfull, sd_.reshape(-1)[off:off + sz], sm_.reshape(-1)[off:off + sz], sv_.reshape(-1)[off:off + sz])
        off += sz

    def small_out(name, like, kind):
        if name == "ffn_norm":
            return jnp.stack([sres["ffn_norm0"][kind], sres["ffn_norm1"][kind]]).reshape(like.shape)
        return sres[name][kind].reshape(like.shape)

    order = (("even_norm", even_norm), ("even_w_in", even_w_in), ("even_q_gain", even_q_gain),
             ("even_k_gain", even_k_gain), ("even_sinks", even_sinks), ("even_conv_w", even_conv_w),
             ("even_w_out", even_w_out), ("odd_norm", odd_norm), ("odd_w_in", odd_w_in), ("odd_conv_w", odd_conv_w),
             ("odd_a_log", odd_a_log), ("odd_dt_bias", odd_dt_bias), ("odd_o_gain", odd_o_gain),
             ("odd_w_out", odd_w_out), ("ffn_norm", ffn_norm), ("ffn_w_gate_up", ffn_w_gate_up),
             ("ffn_w_down", ffn_w_down))
    outs = [loss, grad_x.reshape(x.shape)]
    for kind in range(4):
        for name, like in order:
            outs.append(res[name][kind] if name in res else small_out(name, like, kind))
    return tuple(outs)
```

```python
import functools

import jax
import jax.numpy as jnp
from jax import lax
from jax.experimental import pallas as pl
from jax.experimental.pallas import tpu as pltpu

F32 = jnp.float32
MXU_DTYPE = jnp.bfloat16
HI = lax.Precision.HIGH
EPS = 1e-6
N_DEV = 8
D_MODEL = 1024
HEAD_DIM = 64
ATTN_HEADS = 8
KV_HEADS = 2
ATTN_BLOCK = 128
Q_W = 512
KV_W = 128
CONV_CH = 512
EVEN_IN_W = 2304
DN_HEADS = 8
DN_DIM = 128
DN_W = 1024
DN_CHUNK = 64
ODD_IN_W = 4112
ODD_IN_PAD = 4224
D_FF = 2816
NEG = -1e30
VMEM_LIMIT = 56 * 1024 * 1024
ADAM_LR, ADAM_B1, ADAM_B2, ADAM_EPS, ADAM_WD, ADAM_STEP = 0.001, 0.9, 0.999, 1e-08, 0.01, 10
MESH = pl.DeviceIdType.MESH


def _cp(*sem):
    return pltpu.CompilerParams(dimension_semantics=sem, vmem_limit_bytes=VMEM_LIMIT)


def _pick(n, cap):
    best = 128
    for t in range(128, cap + 1, 128):
        if n % t == 0:
            best = t
    return best


def _mx(a, b):
    return jnp.dot(a.astype(MXU_DTYPE), b.astype(MXU_DTYPE), preferred_element_type=F32)


def _mx_nt(a, b):
    return lax.dot_general(a.astype(MXU_DTYPE), b.astype(MXU_DTYPE), (((1,), (1,)), ((), ())),
                           preferred_element_type=F32)


def _mx_tn(a, b):
    return lax.dot_general(a.astype(MXU_DTYPE), b.astype(MXU_DTYPE), (((0,), (0,)), ((), ())),
                           preferred_element_type=F32)


def _hi(a, b):
    return jnp.dot(a, b, precision=HI, preferred_element_type=F32)


def _hi_nt(a, b):
    return lax.dot_general(a, b, (((1,), (1,)), ((), ())), precision=HI, preferred_element_type=F32)


def _hi_tn(a, b):
    return lax.dot_general(a, b, (((0,), (0,)), ((), ())), precision=HI, preferred_element_type=F32)


def _sigmoid(x):
    return 1.0 / (1.0 + jnp.exp(-x))


def _softplus(x):
    return jnp.maximum(x, 0.0) + jnp.log(1.0 + jnp.exp(-jnp.abs(x)))


def mm_nn(a, b, name, res=None, out_dtype=F32, tm=1024):
    m, k = a.shape
    _, n = b.shape
    tn = _pick(n, 1536)
    tm = min(tm, m)

    def body(*refs):
        a_ref, b_ref = refs[0], refs[1]
        o_ref = refs[-1]
        acc = _mx(a_ref[...], b_ref[...])
        if res is not None:
            acc = acc + refs[2][...]
        o_ref[...] = acc.astype(o_ref.dtype)

    in_specs = [pl.BlockSpec((tm, k), lambda j, i: (i, 0)), pl.BlockSpec((k, tn), lambda j, i: (0, j))]
    args = [a, b]
    if res is not None:
        in_specs.append(pl.BlockSpec((tm, tn), lambda j, i: (i, j)))
        args.append(res)
    return pl.pallas_call(
        body, name=name, grid=(n // tn, m // tm), in_specs=in_specs,
        out_specs=pl.BlockSpec((tm, tn), lambda j, i: (i, j)),
        out_shape=jax.ShapeDtypeStruct((m, n), out_dtype), compiler_params=_cp("parallel", "parallel"))(*args)


def mm_nt(a, b, name, out_dtype=F32, tm=1024, after=None):
    m, k = a.shape
    n, _ = b.shape
    tn = _pick(n, 512 if k > 3000 else 1536)
    tm = min(tm, m)

    def body(a_ref, b_ref, *rest):
        o_ref = rest[-1]
        o_ref[...] = _mx_nt(a_ref[...], b_ref[...]).astype(o_ref.dtype)

    in_specs = [pl.BlockSpec((tm, k), lambda j, i: (i, 0)), pl.BlockSpec((tn, k), lambda j, i: (j, 0))]
    args = [a, b]
    if after is not None:
        in_specs.append(pl.BlockSpec(memory_space=pl.ANY))
        args.append(after)
    return pl.pallas_call(
        body, name=name, grid=(n // tn, m // tm), in_specs=in_specs,
        out_specs=pl.BlockSpec((tm, tn), lambda j, i: (i, j)),
        out_shape=jax.ShapeDtypeStruct((m, n), out_dtype), compiler_params=_cp("parallel", "parallel"))(*args)


def mm_at(at, b, name, tk=1024):
    m, kk = at.shape
    _, n = b.shape
    tm, tn, tk = _pick(m, 1408), _pick(n, 1408), min(tk, kk)
    nk = kk // tk

    def body(a_ref, b_ref, o_ref, acc_ref):
        k = pl.program_id(2)
        p = _mx(a_ref[...], b_ref[...])
        acc = jnp.where(k == 0, p, acc_ref[...] + p)
        acc_ref[...] = acc

        @pl.when(k == nk - 1)
        def _():
            o_ref[...] = acc.astype(o_ref.dtype)

    return pl.pallas_call(
        body, name=name, grid=(m // tm, n // tn, nk),
        in_specs=[pl.BlockSpec((tm, tk), lambda i, j, k: (i, k)), pl.BlockSpec((tk, tn), lambda i, j, k: (k, j))],
        out_specs=pl.BlockSpec((tm, tn), lambda i, j, k: (i, j)),
        out_shape=jax.ShapeDtypeStruct((m, n), MXU_DTYPE), scratch_shapes=[pltpu.VMEM((tm, tn), F32)],
        compiler_params=_cp("parallel", "parallel", "arbitrary"))(at, b)


def rms_fwd(x, g, name, tm=512, after=None):
    t, d = x.shape

    def body(x_ref, g_ref, *rest):
        o_ref, ot_ref = rest[-2:]
        xv = x_ref[...]
        r = lax.rsqrt(jnp.mean(xv * xv, axis=-1, keepdims=True) + EPS)
        h = xv * r * g_ref[...]
        o_ref[...] = h.astype(o_ref.dtype)
        ot_ref[...] = h.T.astype(ot_ref.dtype)

    in_specs = [pl.BlockSpec((tm, d), lambda i: (i, 0)), pl.BlockSpec((1, d), lambda i: (0, 0))]
    args = [x, g]
    if after is not None:
        in_specs.append(pl.BlockSpec(memory_space=pl.ANY))
        args.append(after)
    return pl.pallas_call(
        body, name=name, grid=(t // tm,), in_specs=in_specs,
        out_specs=(pl.BlockSpec((tm, d), lambda i: (i, 0)), pl.BlockSpec((d, tm), lambda i: (0, i))),
        out_shape=(jax.ShapeDtypeStruct((t, d), MXU_DTYPE), jax.ShapeDtypeStruct((d, t), MXU_DTYPE)),
        compiler_params=_cp("parallel"))(*args)


def rms_bwd(x, g, dh, dres, name, tm=512):
    t, d = x.shape

    def body(x_ref, g_ref, dh_ref, dres_ref, dx_ref, dg_ref):
        i = pl.program_id(0)
        xv = x_ref[...]
        r = lax.rsqrt(jnp.mean(xv * xv, axis=-1, keepdims=True) + EPS)
        xh = xv * r
        dhv = dh_ref[...]
        dxh = dhv * g_ref[...]
        dx_ref[...] = dres_ref[...] + r * (dxh - xh * jnp.mean(dxh * xh, axis=-1, keepdims=True))
        part = jnp.sum(dhv * xh, axis=0, keepdims=True)

        @pl.when(i == 0)
        def _():
            dg_ref[...] = part

        @pl.when(i > 0)
        def _():
            dg_ref[...] += part

    row = pl.BlockSpec((tm, d), lambda i: (i, 0))
    one = pl.BlockSpec((1, d), lambda i: (0, 0))
    return pl.pallas_call(
        body, name=name, grid=(t // tm,), in_specs=[row, one, row, row], out_specs=(row, one),
        out_shape=(jax.ShapeDtypeStruct((t, d), F32), jax.ShapeDtypeStruct((1, d), F32)),
        compiler_params=_cp("arbitrary"))(x, g, dh, dres)


GU_TILE = 1408


_GU_PER_TILE = GU_TILE * N_DEV // (2 * D_FF)


def _gu_gathered(g):
    _, k, c = g.shape
    nj = N_DEV // (2 * _GU_PER_TILE)
    return g.reshape(2, nj, _GU_PER_TILE, k, c).transpose(3, 1, 0, 2, 4).reshape(k, N_DEV * c)


def _gu_pieces(dw):
    k, n8 = dw.shape
    nj = N_DEV // (2 * _GU_PER_TILE)
    return dw.reshape(k, nj, 2, _GU_PER_TILE, n8 // N_DEV).transpose(2, 1, 3, 0, 4).reshape(N_DEV, k, n8 // N_DEV)


def ffn_up(f, w, name, tm=512):
    t, d = f.shape

    def body(f_ref, w_ref, gu_ref, a_ref, at_ref):
        gu = _mx(f_ref[...], w_ref[...])
        gu_ref[...] = gu
        g, u = gu[:, :GU_TILE], gu[:, GU_TILE:]
        act = g * _sigmoid(g) * u
        a_ref[...] = act.astype(a_ref.dtype)
        at_ref[...] = act.T.astype(at_ref.dtype)

    return pl.pallas_call(
        body, name=name, grid=(D_FF // GU_TILE, t // tm),
        in_specs=[pl.BlockSpec((tm, d), lambda j, i: (i, 0)), pl.BlockSpec((d, 2 * GU_TILE), lambda j, i: (0, j))],
        out_specs=(pl.BlockSpec((tm, 2 * GU_TILE), lambda j, i: (i, j)), pl.BlockSpec((tm, GU_TILE), lambda j, i: (i, j)),
                   pl.BlockSpec((GU_TILE, tm), lambda j, i: (j, i))),
        out_shape=(jax.ShapeDtypeStruct((t, 2 * D_FF), F32), jax.ShapeDtypeStruct((t, D_FF), MXU_DTYPE),
                   jax.ShapeDtypeStruct((D_FF, t), MXU_DTYPE)),
        compiler_params=_cp("parallel", "parallel"))(f, w)


def ffn_dact(dy, w_d, gu, name, tm=512, after=None):
    t, d = dy.shape

    def body(dy_ref, w_ref, gu_ref, *rest):
        o_ref = rest[-1]
        da = _mx_nt(dy_ref[...], w_ref[...])
        g, u = gu_ref[:, :GU_TILE], gu_ref[:, GU_TILE:]
        sg = _sigmoid(g)
        o_ref[:, :GU_TILE] = (da * u * sg * (1.0 + g * (1.0 - sg))).astype(o_ref.dtype)
        o_ref[:, GU_TILE:] = (da * g * sg).astype(o_ref.dtype)

    in_specs = [pl.BlockSpec((tm, d), lambda j, i: (i, 0)), pl.BlockSpec((GU_TILE, d), lambda j, i: (j, 0)),
                pl.BlockSpec((tm, 2 * GU_TILE), lambda j, i: (i, j))]
    args = [dy, w_d, gu]
    if after is not None:
        in_specs.append(pl.BlockSpec(memory_space=pl.ANY))
        args.append(after)
    return pl.pallas_call(
        body, name=name, grid=(D_FF // GU_TILE, t // tm), in_specs=in_specs,
        out_specs=pl.BlockSpec((tm, 2 * GU_TILE), lambda j, i: (i, j)),
        out_shape=jax.ShapeDtypeStruct((t, 2 * D_FF), MXU_DTYPE), compiler_params=_cp("parallel", "parallel"))(*args)


def loss_head(y, target, name, tm=512):
    t, d = y.shape

    def body(y_ref, t_ref, dy_ref, l_ref):
        i = pl.program_id(0)
        e = y_ref[...] - t_ref[...]
        dy_ref[...] = e * (1.0 / d)
        part = jnp.zeros((1, 128), F32) + 0.5 * jnp.sum(jnp.mean(e * e, axis=-1, keepdims=True), axis=0, keepdims=True)

        @pl.when(i == 0)
        def _():
            l_ref[...] = part

        @pl.when(i > 0)
        def _():
            l_ref[...] += part

    row = pl.BlockSpec((tm, d), lambda i: (i, 0))
    return pl.pallas_call(
        body, name=name, grid=(t // tm,), in_specs=[row, row],
        out_specs=(row, pl.BlockSpec((1, 128), lambda i: (0, 0))),
        out_shape=(jax.ShapeDtypeStruct((t, d), F32), jax.ShapeDtypeStruct((1, 128), F32)),
        compiler_params=_cp("arbitrary"))(y, target)


def _rot(x):
    return jnp.concatenate([-x[:, HEAD_DIM // 2:], x[:, :HEAD_DIM // 2]], axis=-1)


def _rot_t(y):
    return jnp.concatenate([y[:, HEAD_DIM // 2:], -y[:, :HEAD_DIM // 2]], axis=-1)


def qk_prep_fwd(proj, q_gain, k_gain, cosf, sinf, name, tm=256):
    t = proj.shape[0]
    nh = ATTN_HEADS + KV_HEADS

    def body(p_ref, qg_ref, kg_ref, c_ref, s_ref, q_ref, k_ref):
        c, s = c_ref[...], s_ref[...]
        outs = []
        for h in range(nh):
            xh = p_ref[:, h * HEAD_DIM:(h + 1) * HEAD_DIM]
            gain = qg_ref[...] if h < ATTN_HEADS else kg_ref[...]
            r = lax.rsqrt(jnp.mean(xh * xh, axis=-1, keepdims=True) + EPS)
            xn = xh * r * gain
            outs.append(xn * c + _rot(xn) * s)
        q_ref[...] = jnp.concatenate(outs[:ATTN_HEADS], axis=-1)
        k_ref[...] = jnp.concatenate(outs[ATTN_HEADS:], axis=-1)

    gspec = pl.BlockSpec((1, HEAD_DIM), lambda i: (0, 0))
    tspec = pl.BlockSpec((tm, HEAD_DIM), lambda i: (i, 0))
    return pl.pallas_call(
        body, name=name, grid=(t // tm,),
        in_specs=[pl.BlockSpec((tm, Q_W + KV_W), lambda i: (i, 0)), gspec, gspec, tspec, tspec],
        out_specs=(pl.BlockSpec((tm, Q_W), lambda i: (i, 0)), pl.BlockSpec((tm, KV_W), lambda i: (i, 0))),
        out_shape=(jax.ShapeDtypeStruct((t, Q_W), F32), jax.ShapeDtypeStruct((t, KV_W), F32)),
        compiler_params=_cp("parallel"))(proj, q_gain, k_gain, cosf, sinf)


def qk_prep_bwd(proj, q_gain, k_gain, cosf, sinf, dq, dk, name, tm=256):
    t = proj.shape[0]
    nh = ATTN_HEADS + KV_HEADS

    def body(p_ref, qg_ref, kg_ref, c_ref, s_ref, dq_ref, dk_ref, o_ref, dqg_ref, dkg_ref):
        i = pl.program_id(0)
        c, s = c_ref[...], s_ref[...]
        outs = []
        dqg = jnp.zeros((1, HEAD_DIM), F32)
        dkg = jnp.zeros((1, HEAD_DIM), F32)
        for h in range(nh):
            xh = p_ref[:, h * HEAD_DIM:(h + 1) * HEAD_DIM]
            if h < ATTN_HEADS:
                gain = qg_ref[...]
                dout = dq_ref[:, h * HEAD_DIM:(h + 1) * HEAD_DIM]
            else:
                gain = kg_ref[...]
                dout = dk_ref[:, (h - ATTN_HEADS) * HEAD_DIM:(h - ATTN_HEADS + 1) * HEAD_DIM]
            r = lax.rsqrt(jnp.mean(xh * xh, axis=-1, keepdims=True) + EPS)
            xhat = xh * r
            dxn = dout * c + _rot_t(dout * s)
            part = jnp.sum(dxn * xhat, axis=0, keepdims=True)
            if h < ATTN_HEADS:
                dqg = dqg + part
            else:
                dkg = dkg + part
            dxh = dxn * gain
            outs.append(r * (dxh - xhat * jnp.mean(dxh * xhat, axis=-1, keepdims=True)))
        o_ref[...] = jnp.concatenate(outs, axis=-1).astype(o_ref.dtype)

        @pl.when(i == 0)
        def _():
            dqg_ref[...] = dqg
            dkg_ref[...] = dkg

        @pl.when(i > 0)
        def _():
            dqg_ref[...] += dqg
            dkg_ref[...] += dkg

    gspec = pl.BlockSpec((1, HEAD_DIM), lambda i: (0, 0))
    tspec = pl.BlockSpec((tm, HEAD_DIM), lambda i: (i, 0))
    return pl.pallas_call(
        body, name=name, grid=(t // tm,),
        in_specs=[pl.BlockSpec((tm, Q_W + KV_W), lambda i: (i, 0)), gspec, gspec, tspec, tspec,
                  pl.BlockSpec((tm, Q_W), lambda i: (i, 0)), pl.BlockSpec((tm, KV_W), lambda i: (i, 0))],
        out_specs=(pl.BlockSpec((tm, Q_W + KV_W), lambda i: (i, 0)), gspec, gspec),
        out_shape=(jax.ShapeDtypeStruct((t, Q_W + KV_W), MXU_DTYPE), jax.ShapeDtypeStruct((1, HEAD_DIM), F32),
                   jax.ShapeDtypeStruct((1, HEAD_DIM), F32)),
        compiler_params=_cp("arbitrary"))(proj, q_gain, k_gain, cosf, sinf, dq, dk)


def _swa_valid(n, grp):
    qi = lax.broadcasted_iota(jnp.int32, (grp * ATTN_BLOCK, 2 * ATTN_BLOCK), 0) & (ATTN_BLOCK - 1)
    kj = lax.broadcasted_iota(jnp.int32, (grp * ATTN_BLOCK, 2 * ATTN_BLOCK), 1)
    diff = qi + ATTN_BLOCK - kj
    return (diff >= 0) & (diff < ATTN_BLOCK) & (n * ATTN_BLOCK - ATTN_BLOCK + kj >= 0)


def _stack_heads(ref, g, grp):
    return jnp.concatenate([ref[:, (g * grp + j) * HEAD_DIM:(g * grp + j + 1) * HEAD_DIM] for j in range(grp)], axis=0)


def _stack_sinks(s_ref, g, grp):
    return jnp.concatenate([jnp.zeros((ATTN_BLOCK, 1), F32) + s_ref[0:1, g * grp + j:g * grp + j + 1]
                            for j in range(grp)], axis=0)


def swa_fwd(q, k, proj, sinks, name):
    t = q.shape[0]
    nb = t // ATTN_BLOCK
    scale = HEAD_DIM ** -0.5
    grp = ATTN_HEADS // KV_HEADS

    def body(q_ref, kc_ref, kp_ref, vc_ref, vp_ref, s_ref, y_ref, lse_ref):
        n = pl.program_id(0)
        valid = _swa_valid(n, grp)
        kk = jnp.concatenate([kp_ref[...], kc_ref[...]], axis=0).astype(MXU_DTYPE)
        vv = jnp.concatenate([vp_ref[...], vc_ref[...]], axis=0).astype(MXU_DTYPE)
        lane = lax.broadcasted_iota(jnp.int32, (ATTN_BLOCK, ATTN_HEADS), 1)
        gs = range(KV_HEADS)
        qg = [_stack_heads(q_ref, g, grp) for g in gs]
        sink = [_stack_sinks(s_ref, g, grp) for g in gs]
        sc = [jnp.where(valid, _mx_nt(qg[g], kk[:, g * HEAD_DIM:(g + 1) * HEAD_DIM]) * scale, NEG) for g in gs]
        m = [jnp.maximum(jnp.max(sc[g], axis=-1, keepdims=True), sink[g]) for g in gs]
        e = [jnp.exp(sc[g] - m[g]) for g in gs]
        den = [jnp.sum(e[g], axis=-1, keepdims=True) + jnp.exp(sink[g] - m[g]) for g in gs]
        og = [_mx(e[g] / den[g], vv[:, g * HEAD_DIM:(g + 1) * HEAD_DIM]) for g in gs]
        lg = [m[g] + jnp.log(den[g]) for g in gs]
        lse = jnp.zeros((ATTN_BLOCK, ATTN_HEADS), F32)
        outs = []
        for h in range(ATTN_HEADS):
            rows = slice((h % grp) * ATTN_BLOCK, (h % grp + 1) * ATTN_BLOCK)
            outs.append(og[h // grp][rows])
            lse = jnp.where(lane == h, lg[h // grp][rows], lse)
        y_ref[...] = jnp.concatenate(outs, axis=-1)
        lse_ref[...] = lse

    cur = lambda n: (n, 0)
    prev = lambda n: (jnp.maximum(n - 1, 0), 0)
    vcol = (Q_W + KV_W) // KV_W
    return pl.pallas_call(
        body, name=name, grid=(nb,),
        in_specs=[pl.BlockSpec((ATTN_BLOCK, Q_W), cur), pl.BlockSpec((ATTN_BLOCK, KV_W), cur),
                  pl.BlockSpec((ATTN_BLOCK, KV_W), prev),
                  pl.BlockSpec((ATTN_BLOCK, KV_W), lambda n: (n, vcol)),
                  pl.BlockSpec((ATTN_BLOCK, KV_W), lambda n: (jnp.maximum(n - 1, 0), vcol)),
                  pl.BlockSpec((1, ATTN_HEADS), lambda n: (0, 0))],
        out_specs=(pl.BlockSpec((ATTN_BLOCK, Q_W), cur), pl.BlockSpec((ATTN_BLOCK, ATTN_HEADS), cur)),
        out_shape=(jax.ShapeDtypeStruct((t, Q_W), F32), jax.ShapeDtypeStruct((t, ATTN_HEADS), F32)),
        compiler_params=_cp("parallel"))(q, k, k, proj, proj, sinks)


def swa_bwd(q, k, proj, sinks, y, lse, dmix, name):
    t = q.shape[0]
    nb = t // ATTN_BLOCK
    scale = HEAD_DIM ** -0.5
    grp = ATTN_HEADS // KV_HEADS

    def body(q_ref, kc_ref, kp_ref, vc_ref, vp_ref, s_ref, y_ref, lse_ref, dy_ref,
             dq_ref, dk_ref, dv_ref, ds_ref, dkc, dvc):
        n = pl.program_id(0)

        @pl.when(n == 0)
        def _():
            dkc[...] = jnp.zeros_like(dkc)
            dvc[...] = jnp.zeros_like(dvc)
            ds_ref[...] = jnp.zeros_like(ds_ref)

        @pl.when(n < nb)
        def _():
            valid = _swa_valid(n, grp)
            kk = jnp.concatenate([kp_ref[...], kc_ref[...]], axis=0).astype(MXU_DTYPE)
            vv = jnp.concatenate([vp_ref[...], vc_ref[...]], axis=0).astype(MXU_DTYPE)
            lane = lax.broadcasted_iota(jnp.int32, (1, ATTN_HEADS), 1)
            gs = range(KV_HEADS)
            kg = [kk[:, g * HEAD_DIM:(g + 1) * HEAD_DIM] for g in gs]
            vg = [vv[:, g * HEAD_DIM:(g + 1) * HEAD_DIM] for g in gs]
            qg = [_stack_heads(q_ref, g, grp).astype(MXU_DTYPE) for g in gs]
            dog = [_stack_heads(dy_ref, g, grp) for g in gs]
            og = [_stack_heads(y_ref, g, grp) for g in gs]
            lg = [jnp.concatenate([lse_ref[:, g * grp + j:g * grp + j + 1] for j in range(grp)], axis=0) for g in gs]
            sink = [_stack_sinks(s_ref, g, grp) for g in gs]
            sc = [jnp.where(valid, _mx_nt(qg[g], kg[g]) * scale, NEG) for g in gs]
            p = [jnp.exp(sc[g] - lg[g]) for g in gs]
            delta = [jnp.sum(dog[g] * og[g], axis=-1, keepdims=True) for g in gs]
            ds = [p[g] * (_mx_nt(dog[g], vg[g]) - delta[g]) for g in gs]
            dqg = [_mx(ds[g], kg[g]) * scale for g in gs]
            dkf = jnp.concatenate([_mx_tn(ds[g], qg[g]) * scale for g in gs], axis=-1)
            dvf = jnp.concatenate([_mx_tn(p[g], dog[g]) for g in gs], axis=-1)
            dsk = [jnp.exp(sink[g] - lg[g]) * delta[g] for g in gs]
            dsink = jnp.zeros((1, ATTN_HEADS), F32)
            dqs = []
            for h in range(ATTN_HEADS):
                rows = slice((h % grp) * ATTN_BLOCK, (h % grp + 1) * ATTN_BLOCK)
                dqs.append(dqg[h // grp][rows])
                dsink = jnp.where(lane == h, -jnp.sum(dsk[h // grp][rows], axis=0, keepdims=True), dsink)
            dq_ref[...] = jnp.concatenate(dqs, axis=-1)
            dk_ref[...] = dkc[...] + dkf[:ATTN_BLOCK]
            dv_ref[...] = (dvc[...] + dvf[:ATTN_BLOCK]).astype(dv_ref.dtype)
            dkc[...] = dkf[ATTN_BLOCK:]
            dvc[...] = dvf[ATTN_BLOCK:]
            ds_ref[...] += dsink

        @pl.when(n == nb)
        def _():
            dk_ref[...] = dkc[...]
            dv_ref[...] = dvc[...].astype(dv_ref.dtype)

    cur = lambda n: (jnp.minimum(n, nb - 1), 0)
    prev = lambda n: (jnp.clip(n - 1, 0, nb - 1), 0)
    vcol = (Q_W + KV_W) // KV_W
    return pl.pallas_call(
        body, name=name, grid=(nb + 1,),
        in_specs=[pl.BlockSpec((ATTN_BLOCK, Q_W), cur), pl.BlockSpec((ATTN_BLOCK, KV_W), cur),
                  pl.BlockSpec((ATTN_BLOCK, KV_W), prev),
                  pl.BlockSpec((ATTN_BLOCK, KV_W), lambda n: (jnp.minimum(n, nb - 1), vcol)),
                  pl.BlockSpec((ATTN_BLOCK, KV_W), lambda n: (jnp.clip(n - 1, 0, nb - 1), vcol)),
                  pl.BlockSpec((1, ATTN_HEADS), lambda n: (0, 0)),
                  pl.BlockSpec((ATTN_BLOCK, Q_W), cur), pl.BlockSpec((ATTN_BLOCK, ATTN_HEADS), cur),
                  pl.BlockSpec((ATTN_BLOCK, Q_W), cur)],
        out_specs=(pl.BlockSpec((ATTN_BLOCK, Q_W), cur), pl.BlockSpec((ATTN_BLOCK, KV_W), prev),
                   pl.BlockSpec((ATTN_BLOCK, KV_W), prev), pl.BlockSpec((1, ATTN_HEADS), lambda n: (0, 0))),
        out_shape=(jax.ShapeDtypeStruct((t, Q_W), F32), jax.ShapeDtypeStruct((t, KV_W), F32),
                   jax.ShapeDtypeStruct((t, KV_W), MXU_DTYPE), jax.ShapeDtypeStruct((1, ATTN_HEADS), F32)),
        scratch_shapes=[pltpu.VMEM((ATTN_BLOCK, KV_W), F32), pltpu.VMEM((ATTN_BLOCK, KV_W), F32)],
        compiler_params=_cp("arbitrary"))(q, k, k, proj, proj, sinks, y, lse, dmix)


GC_W = 256
_GB0, _GC0, _XI0 = 768 // GC_W, 1280 // GC_W, 1792 // GC_W
HALO = 8


def gconv_fwd(proj, conv_w, name, tm=512):
    t = proj.shape[0]
    hb = tm // HALO

    def body(gb_ref, gc_ref, xi_ref, gch_ref, xih_ref, w_ref, y_ref):
        i = pl.program_id(1)
        u = gc_ref[...] * xi_ref[...]
        uh = jnp.where(i == 0, 0.0, gch_ref[...] * xih_ref[...])
        up = jnp.concatenate([uh, u], axis=0)
        cv = w_ref[0:1, :] * up[HALO - 2:HALO - 2 + tm]
        cv = cv + w_ref[1:2, :] * up[HALO - 1:HALO - 1 + tm]
        cv = cv + w_ref[2:3, :] * u
        y_ref[...] = (gb_ref[...] * cv).astype(y_ref.dtype)

    def col(c0):
        return pl.BlockSpec((tm, GC_W), lambda cj, i: (i, c0 + cj))

    def halo(c0):
        return pl.BlockSpec((HALO, GC_W), lambda cj, i: (jnp.maximum(i * hb - 1, 0), c0 + cj))

    return pl.pallas_call(
        body, name=name, grid=(CONV_CH // GC_W, t // tm),
        in_specs=[col(_GB0), col(_GC0), col(_XI0), halo(_GC0), halo(_XI0),
                  pl.BlockSpec((3, GC_W), lambda cj, i: (0, cj))],
        out_specs=pl.BlockSpec((tm, GC_W), lambda cj, i: (i, cj)),
        out_shape=jax.ShapeDtypeStruct((t, CONV_CH), MXU_DTYPE),
        compiler_params=_cp("parallel", "parallel"))(proj, proj, proj, proj, proj, conv_w)


def gconv_bwd(proj, conv_w, dmix, name, tm=512):
    t = proj.shape[0]
    hb = tm // HALO
    nt = t // tm
    dy0 = Q_W // GC_W

    def body(gb_ref, gc_ref, xi_ref, gch_ref, xih_ref, gbn_ref, dyn_ref, dy_ref, w_ref,
             dgb_ref, dgc_ref, dxi_ref, dw_ref):
        i = pl.program_id(1)
        gc, xi, gb, dy = gc_ref[...], xi_ref[...], gb_ref[...], dy_ref[...]
        u = gc * xi
        uh = jnp.where(i == 0, 0.0, gch_ref[...] * xih_ref[...])
        up = jnp.concatenate([uh, u], axis=0)
        u2 = up[HALO - 2:HALO - 2 + tm]
        u1 = up[HALO - 1:HALO - 1 + tm]
        cv = w_ref[0:1, :] * u2 + w_ref[1:2, :] * u1 + w_ref[2:3, :] * u
        dgb_ref[...] = (dy * cv).astype(dgb_ref.dtype)
        dcv = dy * gb
        dcvn = jnp.where(i == nt - 1, 0.0, dyn_ref[...] * gbn_ref[...])
        dcvp = jnp.concatenate([dcv, dcvn], axis=0)
        du = w_ref[0:1, :] * dcvp[2:2 + tm] + w_ref[1:2, :] * dcvp[1:1 + tm] + w_ref[2:3, :] * dcv
        dgc_ref[...] = (du * xi).astype(dgc_ref.dtype)
        dxi_ref[...] = (du * gc).astype(dxi_ref.dtype)
        dw = jnp.concatenate([jnp.sum(dcv * u2, axis=0, keepdims=True), jnp.sum(dcv * u1, axis=0, keepdims=True),
                              jnp.sum(dcv * u, axis=0, keepdims=True)], axis=0)

        @pl.when(i == 0)
        def _():
            dw_ref[...] = dw

        @pl.when(i > 0)
        def _():
            dw_ref[...] += dw

    def col(c0):
        return pl.BlockSpec((tm, GC_W), lambda cj, i: (i, c0 + cj))

    def halo(c0):
        return pl.BlockSpec((HALO, GC_W), lambda cj, i: (jnp.maximum(i * hb - 1, 0), c0 + cj))

    def nxt(c0):
        return pl.BlockSpec((HALO, GC_W), lambda cj, i: (jnp.minimum((i + 1) * hb, t // HALO - 1), c0 + cj))

    out = pl.BlockSpec((tm, GC_W), lambda cj, i: (i, cj))
    return pl.pallas_call(
        body, name=name, grid=(CONV_CH // GC_W, nt),
        in_specs=[col(_GB0), col(_GC0), col(_XI0), halo(_GC0), halo(_XI0), nxt(_GB0), nxt(dy0), col(dy0),
                  pl.BlockSpec((3, GC_W), lambda cj, i: (0, cj))],
        out_specs=(out, out, out, pl.BlockSpec((3, GC_W), lambda cj, i: (0, cj))),
        out_shape=(jax.ShapeDtypeStruct((t, CONV_CH), MXU_DTYPE),) * 3 + (jax.ShapeDtypeStruct((3, CONV_CH), F32),),
        compiler_params=_cp("parallel", "arbitrary"))(proj, proj, proj, proj, proj, proj, dmix, dmix, conv_w)


_QKV_W = 3 * DN_W
_BA_COL = (4 * DN_W) // 128
_Z_COL = _QKV_W // DN_W


def gdn_prep_fwd(proj, conv_w, alog_row, dtb_row, name, tm=256):
    t = proj.shape[0]
    hb = tm // HALO
    qscale = DN_DIM ** -0.5

    def body(x_ref, xh_ref, w_ref, ba_ref, al_ref, dt_ref, q_ref, k_ref, v_ref, bg_ref):
        i = pl.program_id(0)
        for gi in range(3 * DN_HEADS):
            sl = slice(gi * DN_DIM, (gi + 1) * DN_DIM)
            xp = jnp.concatenate([jnp.where(i == 0, 0.0, xh_ref[:, sl]), x_ref[:, sl]], axis=0)
            c = w_ref[0:1, sl] * xp[HALO - 3:HALO - 3 + tm]
            for j in range(1, 4):
                c = c + w_ref[j:j + 1, sl] * xp[HALO - 3 + j:HALO - 3 + j + tm]
            s = c * _sigmoid(c)
            osl = slice((gi % DN_HEADS) * DN_DIM, (gi % DN_HEADS + 1) * DN_DIM)
            if gi < DN_HEADS:
                q_ref[:, osl] = s * lax.rsqrt(jnp.sum(s * s, axis=-1, keepdims=True) + EPS) * qscale
            elif gi < 2 * DN_HEADS:
                k_ref[:, osl] = s * lax.rsqrt(jnp.sum(s * s, axis=-1, keepdims=True) + EPS)
            else:
                v_ref[:, osl] = s
        ba = ba_ref[...]
        lane = lax.broadcasted_iota(jnp.int32, ba.shape, 1)
        gval = -jnp.exp(al_ref[...]) * _softplus(ba + dt_ref[...])
        bg_ref[...] = jnp.where(lane < DN_HEADS, _sigmoid(ba), jnp.where(lane < 2 * DN_HEADS, gval, 0.0))

    row = pl.BlockSpec((tm, DN_W), lambda i: (i, 0))
    one = pl.BlockSpec((1, 128), lambda i: (0, 0))
    return pl.pallas_call(
        body, name=name, grid=(t // tm,),
        in_specs=[pl.BlockSpec((tm, _QKV_W), lambda i: (i, 0)),
                  pl.BlockSpec((HALO, _QKV_W), lambda i: (jnp.maximum(i * hb - 1, 0), 0)),
                  pl.BlockSpec((4, _QKV_W), lambda i: (0, 0)),
                  pl.BlockSpec((tm, 128), lambda i: (i, _BA_COL)), one, one],
        out_specs=(row, row, row, pl.BlockSpec((tm, 128), lambda i: (i, 0))),
        out_shape=(jax.ShapeDtypeStruct((t, DN_W), F32),) * 3 + (jax.ShapeDtypeStruct((t, 128), F32),),
        compiler_params=_cp("parallel"))(proj, proj, conv_w, proj, alog_row, dtb_row)


def gdn_prep_bwd(proj, conv_w, alog_row, dtb_row, dq, dk, dv, dbg, name, tm=256):
    t = proj.shape[0]
    hb = tm // HALO
    nt = t // tm
    qscale = DN_DIM ** -0.5
    te = tm + HALO

    def body(x_ref, xh_ref, xn_ref, w_ref, ba_ref, al_ref, dt_ref, dq_ref, dk_ref, dv_ref,
             dqn_ref, dkn_ref, dvn_ref, dbg_ref, dx_ref, dba_ref, dw_ref, ddt_ref, dal_ref):
        i = pl.program_id(0)
        first = i == 0
        last = i == nt - 1
        dws = []
        for gi in range(3 * DN_HEADS):
            sl = slice(gi * DN_DIM, (gi + 1) * DN_DIM)
            osl = slice((gi % DN_HEADS) * DN_DIM, (gi % DN_HEADS + 1) * DN_DIM)
            xe = jnp.concatenate([jnp.where(first, 0.0, xh_ref[:, sl]), x_ref[:, sl], xn_ref[:, sl]], axis=0)
            c = w_ref[0:1, sl] * xe[HALO - 3:HALO - 3 + te]
            for j in range(1, 4):
                c = c + w_ref[j:j + 1, sl] * xe[HALO - 3 + j:HALO - 3 + j + te]
            sg = _sigmoid(c)
            s = c * sg
            d_ref, dn_ref = ((dq_ref, dqn_ref), (dk_ref, dkn_ref), (dv_ref, dvn_ref))[gi // DN_HEADS]
            dy = jnp.concatenate([d_ref[:, osl], jnp.where(last, 0.0, dn_ref[:, osl])], axis=0)
            if gi < 2 * DN_HEADS:
                r = lax.rsqrt(jnp.sum(s * s, axis=-1, keepdims=True) + EPS)
                sh = s * r
                ds = r * (dy - sh * jnp.sum(sh * dy, axis=-1, keepdims=True))
                if gi < DN_HEADS:
                    ds = ds * qscale
            else:
                ds = dy
            dc = ds * sg * (1.0 + c * (1.0 - sg))
            dcs = [dc[3 - j:3 - j + tm] for j in range(4)]
            dx = w_ref[0:1, sl] * dcs[0]
            for j in range(1, 4):
                dx = dx + w_ref[j:j + 1, sl] * dcs[j]
            dx_ref[:, sl] = dx.astype(dx_ref.dtype)
            x0 = x_ref[:, sl]
            dws.append(jnp.concatenate([jnp.sum(dcs[j] * x0, axis=0, keepdims=True) for j in range(4)], axis=0))
        dw = jnp.concatenate(dws, axis=-1)
        ba = ba_ref[...]
        dbgv = dbg_ref[...]
        lane = lax.broadcasted_iota(jnp.int32, ba.shape, 1)
        beta = _sigmoid(ba)
        ea = -jnp.exp(al_ref[...])
        zin = ba + dt_ref[...]
        is_b = lane < DN_HEADS
        is_a = (lane >= DN_HEADS) & (lane < 2 * DN_HEADS)
        da = jnp.where(is_a, dbgv * ea * _sigmoid(zin), 0.0)
        dba_ref[...] = jnp.where(is_b, dbgv * beta * (1.0 - beta), da).astype(dba_ref.dtype)
        ddt = jnp.sum(da, axis=0, keepdims=True)
        dal = jnp.sum(jnp.where(is_a, dbgv * ea * _softplus(zin), 0.0), axis=0, keepdims=True)

        @pl.when(first)
        def _():
            dw_ref[...] = dw
            ddt_ref[...] = ddt
            dal_ref[...] = dal

        @pl.when(i > 0)
        def _():
            dw_ref[...] += dw
            ddt_ref[...] += ddt
            dal_ref[...] += dal

    row = pl.BlockSpec((tm, DN_W), lambda i: (i, 0))
    nrow = pl.BlockSpec((HALO, DN_W), lambda i: (jnp.minimum((i + 1) * hb, t // HALO - 1), 0))
    one = pl.BlockSpec((1, 128), lambda i: (0, 0))
    return pl.pallas_call(
        body, name=name, grid=(nt,),
        in_specs=[pl.BlockSpec((tm, _QKV_W), lambda i: (i, 0)),
                  pl.BlockSpec((HALO, _QKV_W), lambda i: (jnp.maximum(i * hb - 1, 0), 0)),
                  pl.BlockSpec((HALO, _QKV_W), lambda i: (jnp.minimum((i + 1) * hb, t // HALO - 1), 0)),
                  pl.BlockSpec((4, _QKV_W), lambda i: (0, 0)),
                  pl.BlockSpec((tm, 128), lambda i: (i, _BA_COL)), one, one,
                  row, row, row, nrow, nrow, nrow, pl.BlockSpec((tm, 128), lambda i: (i, 0))],
        out_specs=(pl.BlockSpec((tm, _QKV_W), lambda i: (i, 0)), pl.BlockSpec((tm, 128), lambda i: (i, 0)),
                   pl.BlockSpec((4, _QKV_W), lambda i: (0, 0)), one, one),
        out_shape=(jax.ShapeDtypeStruct((t, _QKV_W), MXU_DTYPE), jax.ShapeDtypeStruct((t, 128), MXU_DTYPE),
                   jax.ShapeDtypeStruct((4, _QKV_W), F32), jax.ShapeDtypeStruct((1, 128), F32),
                   jax.ShapeDtypeStruct((1, 128), F32)),
        compiler_params=_cp("arbitrary"))(proj, proj, proj, conv_w, proj, alog_row, dtb_row, dq, dk, dv, dq, dk, dv, dbg)


def _chunk_masks():
    r = lax.broadcasted_iota(jnp.int32, (DN_CHUNK, DN_CHUNK), 0)
    c = lax.broadcasted_iota(jnp.int32, (DN_CHUNK, DN_CHUNK), 1)
    return r >= c, r > c


INV_PACK = 2


def _inv_unit_lower_many(mats):
    n = DN_CHUNK
    wide = INV_PACK * n
    r = lax.broadcasted_iota(jnp.int32, (wide, wide), 0)
    c = lax.broadcasted_iota(jnp.int32, (wide, wide), 1)
    same = (r // n) == (c // n)
    eye = jnp.where((r[:n] == (c[:n] % n)), 1.0, 0.0)

    def blockdiag(row):
        return jnp.where(same, jnp.concatenate([row] * INV_PACK, axis=0), 0.0)

    packs = [jnp.concatenate(mats[g:g + INV_PACK], axis=-1) for g in range(0, len(mats), INV_PACK)]
    xs = [eye - a for a in packs]
    pws = [_hi(a, blockdiag(a)) for a in packs]
    for step in range(5):
        if step < 4:
            both = [_hi(jnp.concatenate([x, pw], axis=0), blockdiag(pw)) for x, pw in zip(xs, pws)]
            xs = [x + b[:n] for x, b in zip(xs, both)]
            pws = [b[n:] for b in both]
        else:
            xs = [x + _hi(x, blockdiag(pw)) for x, pw in zip(xs, pws)]
    return [x[:, j * n:(j + 1) * n] for x in xs for j in range(INV_PACK)]


def _chunk_common(q, k, v, beta, gc, gcr, lower, strict):
    gam = jnp.exp(jnp.where(lower, gc - gcr, NEG))
    eg = jnp.exp(gc)
    gl = gc[DN_CHUNK - 1:DN_CHUNK, :]
    kdf = jnp.exp(gl - gc)
    kb = k * beta
    bmat = _mx_nt(kb, k)
    qmat = _mx_nt(q, k)
    return gam, eg, jnp.exp(gl), kdf, kb, bmat, qmat


def gdn_fwd(q, k, v, bg, name):
    t = q.shape[0]
    n_chunks = t // DN_CHUNK

    def body(q_ref, k_ref, v_ref, bg_ref, o_ref, sall_ref, tall_ref, s_ref):
        n = pl.program_id(0)

        @pl.when(n == 0)
        def _():
            s_ref[...] = jnp.zeros_like(s_ref)

        lower, strict = _chunk_masks()
        bgv = bg_ref[...]
        gcs = _hi(jnp.where(lower, 1.0, 0.0), bgv)
        gcs_t = gcs.T
        hs = range(DN_HEADS)
        sl = [slice(h * DN_DIM, (h + 1) * DN_DIM) for h in hs]
        st = [s_ref[h] for h in hs]
        for h in hs:
            sall_ref[0, h] = st[h]
        qh = [q_ref[:, sl[h]] for h in hs]
        kh = [k_ref[:, sl[h]] for h in hs]
        vh = [v_ref[:, sl[h]] for h in hs]
        beta = [bgv[:, h:h + 1] for h in hs]
        com = [_chunk_common(qh[h], kh[h], vh[h], beta[h], gcs[:, DN_HEADS + h:DN_HEADS + h + 1],
                             gcs_t[DN_HEADS + h:DN_HEADS + h + 1, :], lower, strict) for h in hs]
        gam, eg, dec, kdf, kb, bmat, qmat = zip(*com)
        tms = _inv_unit_lower_many([jnp.where(strict, bmat[h] * gam[h], 0.0) for h in hs])
        for h in hs:
            tall_ref[0, h] = tms[h]
        uw = [_hi(tms[h], jnp.concatenate([vh[h] * beta[h], kb[h] * eg[h]], axis=-1)) for h in hs]
        v_new = [uw[h][:, :DN_DIM] - _mx(uw[h][:, DN_DIM:], st[h]) for h in hs]
        o_st = [_mx(qh[h] * eg[h], st[h]) for h in hs]
        o_in = [_mx(qmat[h] * gam[h], v_new[h]) for h in hs]
        s_up = [_mx_tn(kh[h] * kdf[h], v_new[h]) for h in hs]
        for h in hs:
            o_ref[:, sl[h]] = o_st[h] + o_in[h]
            s_ref[h] = st[h] * dec[h] + s_up[h]

    row = pl.BlockSpec((DN_CHUNK, DN_W), lambda n: (n, 0))
    return pl.pallas_call(
        body, name=name, grid=(n_chunks,),
        in_specs=[row, row, row, pl.BlockSpec((DN_CHUNK, 128), lambda n: (n, 0))],
        out_specs=(row, pl.BlockSpec((1, DN_HEADS, DN_DIM, DN_DIM), lambda n: (n, 0, 0, 0)),
                   pl.BlockSpec((1, DN_HEADS, DN_CHUNK, DN_CHUNK), lambda n: (n, 0, 0, 0))),
        out_shape=(jax.ShapeDtypeStruct((t, DN_W), F32),
                   jax.ShapeDtypeStruct((n_chunks, DN_HEADS, DN_DIM, DN_DIM), F32),
                   jax.ShapeDtypeStruct((n_chunks, DN_HEADS, DN_CHUNK, DN_CHUNK), F32)),
        scratch_shapes=[pltpu.VMEM((DN_HEADS, DN_DIM, DN_DIM), F32)],
        compiler_params=_cp("arbitrary"))(q, k, v, bg)


def gdn_bwd(q, k, v, bg, sall, tall, do, name):
    t = q.shape[0]
    n_chunks = t // DN_CHUNK

    def body(q_ref, k_ref, v_ref, bg_ref, sall_ref, tall_ref, do_ref, dq_ref, dk_ref, dv_ref, dbg_ref, ds_ref):
        n = pl.program_id(0)

        @pl.when(n == 0)
        def _():
            ds_ref[...] = jnp.zeros_like(ds_ref)

        lower, strict = _chunk_masks()
        ltri = jnp.where(lower, 1.0, 0.0)
        bgv = bg_ref[...]
        gcs = _hi(ltri, bgv)
        gcs_t = gcs.T
        lane = lax.broadcasted_iota(jnp.int32, (DN_CHUNK, 128), 1)
        rowi = lax.broadcasted_iota(jnp.int32, (DN_CHUNK, 1), 0)
        hs = range(DN_HEADS)
        each = lambda fn, *ls: [fn(*a) for a in zip(*ls)]
        rsum = lambda a: jnp.sum(a, axis=-1, keepdims=True)
        sl = [slice(h * DN_DIM, (h + 1) * DN_DIM) for h in hs]
        st = [sall_ref[0, h] for h in hs]
        tms = [tall_ref[0, h] for h in hs]
        dsn = [ds_ref[h] for h in hs]
        qh = [q_ref[:, sl[h]] for h in hs]
        kh = [k_ref[:, sl[h]] for h in hs]
        vh = [v_ref[:, sl[h]] for h in hs]
        doh = [do_ref[:, sl[h]] for h in hs]
        beta = [bgv[:, h:h + 1] for h in hs]
        com = [_chunk_common(qh[h], kh[h], vh[h], beta[h], gcs[:, DN_HEADS + h:DN_HEADS + h + 1],
                             gcs_t[DN_HEADS + h:DN_HEADS + h + 1, :], lower, strict) for h in hs]
        gam, eg, dec, kdf, kb, bmat, qmat = zip(*com)
        rhs_w = each(lambda a, b: a * b, kb, eg)
        uw = each(lambda t_, v_, b_, r_: _hi(t_, jnp.concatenate([v_ * b_, r_], axis=-1)), tms, vh, beta, rhs_w)
        qd = each(lambda a, b: a * b, qh, eg)
        kd = each(lambda a, b: a * b, kh, kdf)
        pmat = each(lambda a, b: a * b, qmat, gam)
        v_new = each(lambda uw_, s_: uw_[:, :DN_DIM] - _mx(uw_[:, DN_DIM:], s_), uw, st)
        dqd = each(_mx_nt, doh, st)
        ds_o = each(_mx_tn, qd, doh)
        dp = each(lambda d_, v_: jnp.where(lower, _mx_nt(d_, v_), 0.0), doh, v_new)
        dvn_o = each(_mx_tn, pmat, doh)
        ddec = each(lambda d_, s_: jnp.sum(rsum(d_ * s_), axis=0, keepdims=True), dsn, st)
        dkd = each(_mx_nt, v_new, dsn)
        dvn = each(lambda a, k_, d_: a + _mx(k_, d_), dvn_o, kd, dsn)
        dw = each(lambda d_, s_: -_mx_nt(d_, s_), dvn, st)
        ds_w = each(lambda uw_, d_: _mx_tn(uw_[:, DN_DIM:], d_), uw, dvn)
        for h in hs:
            ds_ref[h] = ds_o[h] + dec[h] * dsn[h] - ds_w[h]
        dr = each(lambda t_, a, b: _hi_tn(t_, jnp.concatenate([a, b], axis=-1)), tms, dvn, dw)
        da = each(lambda r_, uw_: jnp.where(strict, -_hi_nt(r_, uw_), 0.0), dr, uw)
        dru = [r_[:, :DN_DIM] for r_ in dr]
        drw = [r_[:, DN_DIM:] for r_ in dr]
        db = each(lambda a, b: a * b, da, gam)
        dq_m = each(lambda a, b: a * b, dp, gam)
        e = each(lambda a, bm, p_, qm, g_: (a * bm + p_ * qm) * g_, da, bmat, dp, qmat, gam)
        dkb = each(lambda b_, k_, r_, e_: _mx(b_, k_) + r_ * e_, db, kh, drw, eg)
        dk = each(lambda b_, kb_, m_, q_, d_, f_: _mx_tn(b_, kb_) + _mx_tn(m_, q_) + d_ * f_, db, kb, dq_m, qh, dkd, kdf)
        dq = each(lambda m_, k_, d_, e_: _mx(m_, k_) + d_ * e_, dq_m, kh, dqd, eg)
        tk = each(lambda a, b: rsum(a * b), dkd, kd)
        dbeta_all = jnp.zeros((DN_CHUNK, 128), F32)
        dgc_all = jnp.zeros((DN_CHUNK, 128), F32)
        for h in hs:
            dgc = (jnp.sum(e[h], axis=1, keepdims=True) - jnp.sum(e[h].T, axis=1, keepdims=True)
                   + rsum(dqd[h] * qd[h]) - tk[h] + rsum(drw[h] * rhs_w[h]))
            dgl = jnp.sum(tk[h], axis=0, keepdims=True) + ddec[h] * dec[h]
            dgc = dgc + jnp.where(rowi == DN_CHUNK - 1, dgl, 0.0)
            dbeta = rsum(dru[h] * vh[h]) + rsum(dkb[h] * kh[h])
            dq_ref[:, sl[h]] = dq[h]
            dk_ref[:, sl[h]] = dk[h] + dkb[h] * beta[h]
            dv_ref[:, sl[h]] = dru[h] * beta[h]
            dbeta_all = jnp.where(lane == h, dbeta, dbeta_all)
            dgc_all = jnp.where(lane == DN_HEADS + h, dgc, dgc_all)
        dbg_ref[...] = dbeta_all + _hi_tn(ltri, dgc_all)

    rev = lambda n: (n_chunks - 1 - n, 0)
    row = pl.BlockSpec((DN_CHUNK, DN_W), rev)
    small = pl.BlockSpec((DN_CHUNK, 128), rev)
    return pl.pallas_call(
        body, name=name, grid=(n_chunks,),
        in_specs=[row, row, row, small,
                  pl.BlockSpec((1, DN_HEADS, DN_DIM, DN_DIM), lambda n: (n_chunks - 1 - n, 0, 0, 0)),
                  pl.BlockSpec((1, DN_HEADS, DN_CHUNK, DN_CHUNK), lambda n: (n_chunks - 1 - n, 0, 0, 0)), row],
        out_specs=(row, row, row, small),
        out_shape=(jax.ShapeDtypeStruct((t, DN_W), F32),) * 3 + (jax.ShapeDtypeStruct((t, 128), F32),),
        scratch_shapes=[pltpu.VMEM((DN_HEADS, DN_DIM, DN_DIM), F32)],
        compiler_params=_cp("arbitrary"))(q, k, v, bg, sall, tall, do)


def gdn_out_fwd(o, proj, o_gain, name, tm=256):
    t = o.shape[0]

    def body(o_ref, z_ref, g_ref, y_ref, yt_ref):
        for h in range(DN_HEADS):
            sl = slice(h * DN_DIM, (h + 1) * DN_DIM)
            ov, zv = o_ref[:, sl], z_ref[:, sl]
            r = lax.rsqrt(jnp.mean(ov * ov, axis=-1, keepdims=True) + EPS)
            y = ov * r * g_ref[...] * (zv * _sigmoid(zv))
            y_ref[:, sl] = y.astype(y_ref.dtype)
            yt_ref[sl, :] = y.T.astype(yt_ref.dtype)

    row = pl.BlockSpec((tm, DN_W), lambda i: (i, 0))
    return pl.pallas_call(
        body, name=name, grid=(t // tm,),
        in_specs=[row, pl.BlockSpec((tm, DN_W), lambda i: (i, _Z_COL)), pl.BlockSpec((1, DN_DIM), lambda i: (0, 0))],
        out_specs=(row, pl.BlockSpec((DN_W, tm), lambda i: (0, i))),
        out_shape=(jax.ShapeDtypeStruct((t, DN_W), MXU_DTYPE), jax.ShapeDtypeStruct((DN_W, t), MXU_DTYPE)),
        compiler_params=_cp("parallel"))(o, proj, o_gain)


def gdn_out_bwd(o, proj, o_gain, dy, name, tm=256):
    t = o.shape[0]

    def body(o_ref, z_ref, g_ref, dy_ref, do_ref, dz_ref, dg_ref):
        i = pl.program_id(0)
        dg = jnp.zeros((1, DN_DIM), F32)
        for h in range(DN_HEADS):
            sl = slice(h * DN_DIM, (h + 1) * DN_DIM)
            ov, zv, dyv = o_ref[:, sl], z_ref[:, sl], dy_ref[:, sl]
            r = lax.rsqrt(jnp.mean(ov * ov, axis=-1, keepdims=True) + EPS)
            oh = ov * r
            sg = _sigmoid(zv)
            dz_ref[:, sl] = (dyv * oh * g_ref[...] * sg * (1.0 + zv * (1.0 - sg))).astype(dz_ref.dtype)
            don = dyv * (zv * sg)
            dg = dg + jnp.sum(don * oh, axis=0, keepdims=True)
            doh = don * g_ref[...]
            do_ref[:, sl] = r * (doh - oh * jnp.mean(doh * oh, axis=-1, keepdims=True))

        @pl.when(i == 0)
        def _():
            dg_ref[...] = dg

        @pl.when(i > 0)
        def _():
            dg_ref[...] += dg

    row = pl.BlockSpec((tm, DN_W), lambda i: (i, 0))
    one = pl.BlockSpec((1, DN_DIM), lambda i: (0, 0))
    return pl.pallas_call(
        body, name=name, grid=(t // tm,),
        in_specs=[row, pl.BlockSpec((tm, DN_W), lambda i: (i, _Z_COL)), one, row],
        out_specs=(row, row, one),
        out_shape=(jax.ShapeDtypeStruct((t, DN_W), F32), jax.ShapeDtypeStruct((t, DN_W), MXU_DTYPE),
                   jax.ShapeDtypeStruct((1, DN_DIM), F32)),
        compiler_params=_cp("arbitrary"))(o, proj, o_gain, dy)


def _peer(k):
    x, y, c = lax.axis_index("x"), lax.axis_index("y"), lax.axis_index("c")
    px = 1 - x if k & 4 else x
    py = 1 - y if k & 2 else y
    pc = 1 - c if k & 1 else c
    return (px, py, pc), 4 * px + 2 * py + pc


def all_gather(shards, name):
    na = len(shards)

    def body(*refs):
        ins, outs = refs[:na], refs[na:2 * na]
        send_sems, recv_sems, local_sems = refs[2 * na:]
        _, me = _peer(0)
        local = [pltpu.make_async_copy(ins[a], outs[a].at[me], local_sems.at[a]) for a in range(na)]
        for cp in local:
            cp.start()
        sends = []
        for k in range(1, N_DEV):
            peer, _ = _peer(k)
            for a in range(na):
                cp = pltpu.make_async_remote_copy(
                    src_ref=ins[a], dst_ref=outs[a].at[me], send_sem=send_sems.at[a, k - 1],
                    recv_sem=recv_sems.at[a, k - 1], device_id=peer, device_id_type=MESH)
                cp.start()
                sends.append(cp)
        for k in range(1, N_DEV):
            peer, pid = _peer(k)
            for a in range(na):
                pltpu.make_async_remote_copy(
                    src_ref=ins[a], dst_ref=outs[a].at[pid], send_sem=send_sems.at[a, k - 1],
                    recv_sem=recv_sems.at[a, k - 1], device_id=peer, device_id_type=MESH).wait_recv()
        for cp in sends:
            cp.wait_send()
        for cp in local:
            cp.wait()

    anyspec = pl.BlockSpec(memory_space=pl.ANY)
    return pl.pallas_call(
        body, name=name, in_specs=[anyspec] * na, out_specs=tuple([anyspec] * na),
        out_shape=tuple(jax.ShapeDtypeStruct((N_DEV,) + s.shape, s.dtype) for s in shards),
        scratch_shapes=[pltpu.SemaphoreType.DMA((na, N_DEV - 1)), pltpu.SemaphoreType.DMA((na, N_DEV - 1)),
                        pltpu.SemaphoreType.DMA((na,))],
        compiler_params=pltpu.CompilerParams(has_side_effects=True))(*shards)


_HBM = pl.BlockSpec(memory_space=pltpu.HBM)
_SEM = pl.BlockSpec(memory_space=pltpu.SEMAPHORE)
_DATAFLOW = pltpu.SideEffectType.DATAFLOW_SIDE_EFFECTING
N_PEER = N_DEV - 1


def send_start(srcs, name, scatter, after):
    na = len(srcs)
    ns = (2 * N_PEER + 1) * na
    lands = [lax.empty((N_DEV,) + (s.shape[1:] if scatter else s.shape), s.dtype) for s in srcs]
    extra = [] if after is None else [after]

    def body(*refs):
        src_refs, land_refs = refs[:na], refs[na:2 * na]
        sems = refs[2 * na + len(extra):2 * na + len(extra) + ns]
        land_out, token = refs[-1 - na:-1], refs[-1]
        _, me = _peer(0)
        for a in range(na):
            pltpu.make_async_copy(src_refs[a].at[me] if scatter else src_refs[a], land_out[a].at[me],
                                  sems[2 * N_PEER * na + a]).start()
        for k in range(1, N_DEV):
            peer, pid = _peer(k)
            for a in range(na):
                pltpu.make_async_remote_copy(
                    src_ref=src_refs[a].at[pid] if scatter else src_refs[a], dst_ref=land_refs[a].at[me],
                    send_sem=sems[2 * (a * N_PEER + k - 1)], recv_sem=sems[2 * (a * N_PEER + k - 1) + 1],
                    device_id=peer, device_id_type=MESH).start()
        token[...] = jnp.zeros_like(token)

    hbm = lambda arrs: tuple(pltpu.HBM(a.shape, a.dtype) for a in arrs)
    outs = pl.pallas_call(
        body, name=name,
        out_shape=(pltpu.SemaphoreType.DMA(()),) * ns + hbm(srcs) + hbm(lands) + (jax.ShapeDtypeStruct((8, 128), F32),),
        in_specs=[_HBM] * (2 * na) + [pl.BlockSpec(memory_space=pl.ANY)] * len(extra),
        out_specs=(_SEM,) * ns + (_HBM,) * (2 * na) + (pl.BlockSpec(memory_space=pltpu.VMEM),),
        input_output_aliases={i: ns + i for i in range(2 * na)},
        compiler_params=pltpu.CompilerParams(has_side_effects=_DATAFLOW),
    )(*[pltpu.with_memory_space_constraint(a, pltpu.HBM) for a in list(srcs) + lands], *extra)
    return outs[:ns], outs[ns:ns + na], outs[ns + na:ns + 2 * na], outs[-1]


def send_wait(sems, srcs_thru, lands_thru, name, scatter, after):
    na = len(srcs_thru)
    ns = (2 * N_PEER + 1) * na

    def body(*refs):
        src_refs, land_refs, sm = refs[:na], refs[na:2 * na], refs[2 * na:2 * na + ns]
        _, me = _peer(0)
        for a in range(na):
            pltpu.make_async_copy(src_refs[a].at[me] if scatter else src_refs[a], land_refs[a].at[me],
                                  sm[2 * N_PEER * na + a]).wait()
        for k in range(1, N_DEV):
            peer, pid = _peer(k)
            for a in range(na):
                cp = pltpu.make_async_remote_copy(
                    src_ref=src_refs[a].at[pid] if scatter else src_refs[a], dst_ref=land_refs[a].at[pid],
                    send_sem=sm[2 * (a * N_PEER + k - 1)], recv_sem=sm[2 * (a * N_PEER + k - 1) + 1],
                    device_id=peer, device_id_type=MESH)
                cp.wait_send()
                cp.wait_recv()

    hbm = lambda arrs: tuple(pltpu.HBM(a.shape, a.dtype) for a in arrs)
    outs = pl.pallas_call(
        body, name=name, out_shape=hbm(srcs_thru) + hbm(lands_thru),
        in_specs=[_HBM] * (2 * na) + [_SEM] * ns + [pl.BlockSpec(memory_space=pl.ANY)], out_specs=(_HBM,) * (2 * na),
        input_output_aliases={i: i for i in range(2 * na)},
        compiler_params=pltpu.CompilerParams(has_side_effects=_DATAFLOW),
    )(*srcs_thru, *lands_thru, *sems, after)
    return outs[na:]


def _adamw(w, g, m, v):
    m = ADAM_B1 * m + (1.0 - ADAM_B1) * g
    v = ADAM_B2 * v + (1.0 - ADAM_B2) * (g * g)
    m_hat = m / (1.0 - ADAM_B1 ** ADAM_STEP)
    v_hat = v / (1.0 - ADAM_B2 ** ADAM_STEP)
    return -ADAM_LR * (m_hat / (jnp.sqrt(v_hat) + ADAM_EPS) + ADAM_WD * w), m, v


def adam_sum(w, pieces, m, v, name):
    r, c = w.shape
    tr = r
    for cand in (256, 128, 64, 32, 16, 8):
        if r % cand == 0:
            tr = cand
            break

    def body(w_ref, p_ref, m_ref, v_ref, g_ref, d_ref, nm_ref, nv_ref):
        g = p_ref[0].astype(F32)
        for s in range(1, N_DEV):
            g = g + p_ref[s].astype(F32)
        g_ref[...] = g
        d_ref[...], nm_ref[...], nv_ref[...] = _adamw(w_ref[...], g, m_ref[...], v_ref[...])

    row = pl.BlockSpec((tr, c), lambda i: (i, 0))
    out = jax.ShapeDtypeStruct((r, c), F32)
    return pl.pallas_call(
        body, name=name, grid=(r // tr,),
        in_specs=[row, pl.BlockSpec((N_DEV, tr, c), lambda i: (0, i, 0)), row, row],
        out_specs=(row,) * 4, out_shape=(out,) * 4, compiler_params=_cp("parallel"))(w, pieces, m, v)


def sum_rows(gathered, name):
    _, r, c = gathered.shape

    def body(p_ref, o_ref):
        g = p_ref[0]
        for s in range(1, N_DEV):
            g = g + p_ref[s]
        o_ref[...] = g

    return pl.pallas_call(body, name=name, out_shape=jax.ShapeDtypeStruct((r, c), F32))(gathered)


def adam_small(w, g, m, v, name):
    def body(w_ref, g_ref, m_ref, v_ref, d_ref, nm_ref, nv_ref):
        d_ref[...], nm_ref[...], nv_ref[...] = _adamw(w_ref[...], g_ref[...], m_ref[...], v_ref[...])

    out = jax.ShapeDtypeStruct(w.shape, F32)
    return pl.pallas_call(body, name=name, out_shape=(out,) * 3)(w, g, m, v)


def _rope_tables(t):
    inv_freq = 10000.0 ** (-jnp.arange(0, HEAD_DIM, 2, dtype=F32) / HEAD_DIM)
    ang = jnp.arange(t, dtype=F32)[:, None] * inv_freq[None, :]
    cos, sin = jnp.cos(ang), jnp.sin(ang)
    return jnp.concatenate([cos, cos], axis=-1), jnp.concatenate([sin, sin], axis=-1)


def _lane_row(vec8):
    return jnp.pad(vec8.reshape(1, DN_HEADS), ((0, 0), (DN_HEADS, 128 - 2 * DN_HEADS)))


def _ffn_fwd(x, norm_g, w_gu, w_d, tag):
    f, ft = rms_fwd(x, norm_g, f"{tag}_norm")
    gu, a, at = ffn_up(f, w_gu, f"{tag}_gate_up")
    return mm_nn(a, w_d, f"{tag}_down", res=x), (ft, gu, at)


def _ffn_bwd(x, norm_g, w_gu, w_d, saved, dy, tag, after=None):
    ft, gu, at = saved
    dgu = ffn_dact(dy, w_d, gu, f"{tag}_d_gate_up", after=after)
    dwd = mm_at(at, dy, f"{tag}_dw_down")
    df = mm_nt(dgu, w_gu, f"{tag}_d_normed")
    dwgu = mm_at(ft, dgu, f"{tag}_dw_gate_up")
    dx, dg = rms_bwd(x, norm_g, df, dy, f"{tag}_d_norm")
    return dx, dwgu, dwd, dg


def local_step(x, target, small, weights_of, grads_out, after=None):
    t = x.shape[0]
    cosf, sinf = _rope_tables(t)
    alog_row, dtb_row = _lane_row(small["odd_a_log"]), _lane_row(small["odd_dt_bias"])

    h0, h0t = rms_fwd(x, small["even_norm"], "even_norm", after=after)
    we = weights_of("even", h0)
    proj0 = mm_nn(h0, we["w_in"], "even_in_proj")
    qr, kr = qk_prep_fwd(proj0, small["even_q_gain"], small["even_k_gain"], cosf, sinf, "even_qk_prep")
    y_attn, lse = swa_fwd(qr, kr, proj0, small["even_sinks"], "even_swa")
    y_conv = gconv_fwd(proj0, small["even_conv_w"], "even_gconv")
    mix0 = jnp.concatenate([y_attn.astype(MXU_DTYPE), y_conv], axis=-1)
    x1 = mm_nn(mix0, we["w_out"], "even_out_proj", res=x)
    w0 = weights_of("ffn0", x1)
    x2, ffn0 = _ffn_fwd(x1, small["ffn_norm0"], w0["gate_up"], w0["down"], "ffn0")

    wo = weights_of("odd", x2)
    h1, h1t = rms_fwd(x2, small["odd_norm"], "odd_norm")
    proj1 = mm_nn(h1, wo["w_in"], "odd_in_proj")
    qn, kn, vs, bg = gdn_prep_fwd(proj1, small["odd_conv_w"], alog_row, dtb_row, "odd_prep")
    o, sall, tall = gdn_fwd(qn, kn, vs, bg, "odd_delta_rule")
    og, ogt = gdn_out_fwd(o, proj1, small["odd_o_gain"], "odd_gate_norm")
    x3 = mm_nn(og, wo["w_out"], "odd_out_proj", res=x2)
    w1 = weights_of("ffn1", x3)
    x4, ffn1 = _ffn_fwd(x3, small["ffn_norm1"], w1["gate_up"], w1["down"], "ffn1")

    dy, loss_row = loss_head(x4, target, "loss_head")

    gs = {}
    dx3, dwgu, dwd, gs["ffn_norm1"] = _ffn_bwd(x3, small["ffn_norm1"], w1["gate_up"], w1["down"], ffn1, dy, "ffn1")
    tok = grads_out("ffn1", {"gate_up": dwgu, "down": dwd})

    dog = mm_nt(dx3, wo["w_out"], "odd_d_gated", after=tok)
    dwo = mm_at(ogt, dx3, "odd_dw_out")
    do, dz, gs["odd_o_gain"] = gdn_out_bwd(o, proj1, small["odd_o_gain"], dog, "odd_d_gate_norm")
    dqn, dkn, dvs, dbg = gdn_bwd(qn, kn, vs, bg, sall, tall, do, "odd_d_delta_rule")
    dqkv, dba, gs["odd_conv_w"], ddt_row, dal_row = gdn_prep_bwd(
        proj1, small["odd_conv_w"], alog_row, dtb_row, dqn, dkn, dvs, dbg, "odd_d_prep")
    gs["odd_dt_bias"] = ddt_row[:, DN_HEADS:2 * DN_HEADS]
    gs["odd_a_log"] = dal_row[:, DN_HEADS:2 * DN_HEADS]
    dproj1 = jnp.concatenate([dqkv, dz, dba], axis=-1)
    dh1 = mm_nt(dproj1, wo["w_in"], "odd_d_normed")
    dwi = mm_at(h1t, dproj1, "odd_dw_in")
    dx2, gs["odd_norm"] = rms_bwd(x2, small["odd_norm"], dh1, dx3, "odd_d_norm")
    tok = grads_out("odd", {"w_in": dwi, "w_out": dwo})

    dx1, dwgu, dwd, gs["ffn_norm0"] = _ffn_bwd(x1, small["ffn_norm0"], w0["gate_up"], w0["down"], ffn0, dx2, "ffn0",
                                               after=tok)
    tok = grads_out("ffn0", {"gate_up": dwgu, "down": dwd})

    dmix = mm_nt(dx1, we["w_out"], "even_d_mix", after=tok)
    dwo = mm_at(mix0.T, dx1, "even_dw_out")
    dqr, dkr, dv, gs["even_sinks"] = swa_bwd(qr, kr, proj0, small["even_sinks"], y_attn, lse, dmix, "even_d_swa")
    dqk, gs["even_q_gain"], gs["even_k_gain"] = qk_prep_bwd(
        proj0, small["even_q_gain"], small["even_k_gain"], cosf, sinf, dqr, dkr, "even_d_qk_prep")
    dgb, dgc, dxi, gs["even_conv_w"] = gconv_bwd(proj0, small["even_conv_w"], dmix, "even_d_gconv")
    dproj0 = jnp.concatenate([dqk, dv, dgb, dgc, dxi], axis=-1)
    dwi = mm_at(h0t, dproj0, "even_dw_in")
    tok = grads_out("even", {"w_in": dwi, "w_out": dwo})
    dh0 = mm_nt(dproj0, we["w_in"], "even_d_normed", after=tok)
    grad_x, gs["even_norm"] = rms_bwd(x, small["even_norm"], dh0, dx1, "even_d_norm")
    return loss_row, grad_x, gs


_SMALL_ORDER = ("even_norm", "even_q_gain", "even_k_gain", "even_sinks", "odd_a_log", "odd_dt_bias", "odd_o_gain",
                "ffn_norm0", "ffn_norm1", "odd_norm", "even_conv_w", "odd_conv_w")
_SMALL_SIZE = {"even_norm": 1024, "even_q_gain": 64, "even_k_gain": 64, "even_sinks": 8, "odd_a_log": 8,
               "odd_dt_bias": 8, "odd_o_gain": 128, "ffn_norm0": 1024, "ffn_norm1": 1024, "odd_norm": 1024,
               "even_conv_w": 3 * 512, "odd_conv_w": 4 * 3072}
_N_REPL = 9


def _pack_rows(vals):
    flat = jnp.concatenate([v.reshape(-1) for v in vals])
    pad = (-flat.shape[0]) % 1024
    return jnp.pad(flat, (0, pad)).reshape(-1, 128)


def _my_block(full, size, axis):
    me = 4 * lax.axis_index("x") + 2 * lax.axis_index("y") + lax.axis_index("c")
    return lax.dynamic_slice_in_dim(full, me * size, size, axis=axis)


def _col_gathered(g):
    return g.transpose(1, 0, 2).reshape(g.shape[1], N_DEV * g.shape[2])


def _col_pieces(dw):
    k, n8 = dw.shape
    return dw.reshape(k, N_DEV, n8 // N_DEV).transpose(1, 0, 2)


def kernel(x, even_norm, even_w_in, even_q_gain, even_k_gain, even_sinks, even_conv_w, even_w_out, odd_norm, odd_w_in, odd_conv_w, odd_a_log, odd_dt_bias, odd_o_gain, odd_w_out, ffn_norm, ffn_w_gate_up, ffn_w_down, loss_target, m_even_norm, m_even_w_in, m_even_q_gain, m_even_k_gain, m_even_sinks, m_even_conv_w, m_even_w_out, m_odd_norm, m_odd_w_in, m_odd_conv_w, m_odd_a_log, m_odd_dt_bias, m_odd_o_gain, m_odd_w_out, m_ffn_norm, m_ffn_w_gate_up, m_ffn_w_down, v_even_norm, v_even_w_in, v_even_q_gain, v_even_k_gain, v_even_sinks, v_even_conv_w, v_even_w_out, v_odd_norm, v_odd_w_in, v_odd_conv_w, v_odd_a_log, v_odd_dt_bias, v_odd_o_gain, v_odd_w_out, v_ffn_norm, v_ffn_w_gate_up, v_ffn_w_down):
    t = x.shape[1]
    d = D_MODEL

    me = 4 * lax.axis_index("x") + 2 * lax.axis_index("y") + lax.axis_index("c")
    fpd = D_FF // N_DEV
    shard = {
        "even": {"w_in": even_w_in.reshape(d, EVEN_IN_W // N_DEV), "w_out": even_w_out.reshape(d // N_DEV, d)},
        "ffn0": {"gate_up": ffn_w_gate_up[0], "down": ffn_w_down[0]},
        "odd": {"w_in": odd_w_in.reshape(d, ODD_IN_W // N_DEV), "w_out": odd_w_out.reshape(d // N_DEV, d)},
        "ffn1": {"gate_up": ffn_w_gate_up[1], "down": ffn_w_down[1]},
    }
    mom_m = {
        "even": {"w_in": m_even_w_in, "w_out": m_even_w_out}, "odd": {"w_in": m_odd_w_in, "w_out": m_odd_w_out},
        "ffn0": {"gate_up": m_ffn_w_gate_up[0], "down": m_ffn_w_down[0]},
        "ffn1": {"gate_up": m_ffn_w_gate_up[1], "down": m_ffn_w_down[1]},
    }
    mom_v = {
        "even": {"w_in": v_even_w_in, "w_out": v_even_w_out}, "odd": {"w_in": v_odd_w_in, "w_out": v_odd_w_out},
        "ffn0": {"gate_up": v_ffn_w_gate_up[0], "down": v_ffn_w_down[0]},
        "ffn1": {"gate_up": v_ffn_w_gate_up[1], "down": v_ffn_w_down[1]},
    }

    def whole(group, parts):
        col, row = tuple(shard[group])
        w_col = _gu_gathered(parts[0]) if col == "gate_up" else _col_gathered(parts[0])
        if group == "odd":
            w_col = jnp.pad(w_col, ((0, 0), (0, ODD_IN_PAD - ODD_IN_W)))
        return {col: w_col, row: parts[1].reshape(-1, d)}

    shard_rows = _pack_rows([odd_norm, even_conv_w, odd_conv_w])
    (small_g,) = all_gather([shard_rows], "gather_small_weights")
    wire = {g: [a.astype(MXU_DTYPE) for a in shard[g].values()] for g in shard}
    gathers, tok = {}, small_g
    for g in shard:
        sems, srcs_thru, lands_thru, tok = send_start(wire[g], f"gather_{g}_start", False, tok)
        gathers[g] = (sems, srcs_thru, lands_thru)

    def weights_of(group, after):
        lands = send_wait(*gathers[group], f"gather_{group}_wait", False, after)
        return whole(group, lands)

    sent = {}

    def grads_out(group, dws):
        col, row = tuple(shard[group])
        n_cols = N_DEV * shard[group][col].shape[1]
        pieces = [_gu_pieces(dws[col]) if col == "gate_up" else _col_pieces(dws[col][:, :n_cols]),
                  dws[row].reshape((N_DEV,) + shard[group][row].shape)]
        sems, srcs_thru, lands_thru, token = send_start(pieces, f"exchange_{group}_start", True, None)
        sent[group] = (sems, srcs_thru, lands_thru, pieces)
        return token

    sg = small_g.reshape(N_DEV, -1)
    o1 = d // N_DEV
    o2 = o1 + 3 * CONV_CH // N_DEV
    small = {
        "even_norm": even_norm, "even_q_gain": even_q_gain, "even_k_gain": even_k_gain, "even_sinks": even_sinks,
        "odd_a_log": odd_a_log.reshape(-1), "odd_dt_bias": odd_dt_bias.reshape(-1), "odd_o_gain": odd_o_gain,
        "ffn_norm0": ffn_norm[0:1], "ffn_norm1": ffn_norm[1:2],
        "odd_norm": sg[:, :o1].reshape(1, d),
        "even_conv_w": sg[:, o1:o2].reshape(N_DEV, 3, CONV_CH // N_DEV).transpose(1, 0, 2).reshape(3, CONV_CH),
        "odd_conv_w": sg[:, o2:o2 + 4 * _QKV_W // N_DEV].reshape(N_DEV, 4, _QKV_W // N_DEV).transpose(1, 0, 2).reshape(4, _QKV_W),
    }

    loss_row, grad_x, gs = local_step(x.reshape(t, d), loss_target.reshape(t, d), small, weights_of, grads_out, after=tok)

    rows = _pack_rows([gs[n] for n in _SMALL_ORDER] + [loss_row[:, 0:1]])
    (rows_g,) = all_gather([rows], "gather_small_grads")
    tot = sum_rows(rows_g, "sum_small_grads").reshape(-1)
    off, sgrad = 0, {}
    for n in _SMALL_ORDER:
        sgrad[n] = tot[off:off + _SMALL_SIZE[n]]
        off += _SMALL_SIZE[n]
    loss = tot[off]

    upd, behind = {}, tot
    for g in ("ffn1", "odd", "ffn0", "even"):
        sems, srcs_thru, lands_thru, pieces = sent[g]
        lands = send_wait(sems, srcs_thru, lands_thru, f"exchange_{g}_wait", True, behind)
        for (key, w2), pcs in zip(shard[g].items(), lands):
            upd[g, key] = adam_sum(w2, pcs, mom_m[g][key].reshape(w2.shape), mom_v[g][key].reshape(w2.shape),
                                   f"adamw_{g}_{key}")
        behind = upd[g, key][0]
    res = {
        "even_w_in": tuple(o.reshape(even_w_in.shape) for o in upd["even", "w_in"]),
        "even_w_out": tuple(o.reshape(even_w_out.shape) for o in upd["even", "w_out"]),
        "odd_w_in": tuple(o.reshape(odd_w_in.shape) for o in upd["odd", "w_in"]),
        "odd_w_out": tuple(o.reshape(odd_w_out.shape) for o in upd["odd", "w_out"]),
        "ffn_w_gate_up": tuple(jnp.stack([a, b]) for a, b in zip(upd["ffn0", "gate_up"], upd["ffn1", "gate_up"])),
        "ffn_w_down": tuple(jnp.stack([a, b]) for a, b in zip(upd["ffn0", "down"], upd["ffn1", "down"])),
    }

    repl = _SMALL_ORDER[:_N_REPL]
    repl_w = {"even_norm": even_norm, "even_q_gain": even_q_gain, "even_k_gain": even_k_gain, "even_sinks": even_sinks,
              "odd_a_log": odd_a_log, "odd_dt_bias": odd_dt_bias, "odd_o_gain": odd_o_gain,
              "ffn_norm0": ffn_norm[0], "ffn_norm1": ffn_norm[1]}
    repl_m = {"even_norm": m_even_norm, "even_q_gain": m_even_q_gain, "even_k_gain": m_even_k_gain,
              "even_sinks": m_even_sinks, "odd_a_log": m_odd_a_log, "odd_dt_bias": m_odd_dt_bias,
              "odd_o_gain": m_odd_o_gain, "ffn_norm0": m_ffn_norm[0], "ffn_norm1": m_ffn_norm[1]}
    repl_v = {"even_norm": v_even_norm, "even_q_gain": v_even_q_gain, "even_k_gain": v_even_k_gain,
              "even_sinks": v_even_sinks, "odd_a_log": v_odd_a_log, "odd_dt_bias": v_odd_dt_bias,
              "odd_o_gain": v_odd_o_gain, "ffn_norm0": v_ffn_norm[0], "ffn_norm1": v_ffn_norm[1]}
    pk = lambda dct: _pack_rows([dct[n] for n in repl])
    pd_, pm_, pv_ = adam_small(pk(repl_w), pk(sgrad), pk(repl_m), pk(repl_v), "adamw_replicated")
    sres = {}
    off = 0
    for n in repl:
        sz = _SMALL_SIZE[n]
        sres[n] = (sgrad[n], pd_.reshape(-1)[off:off + sz], pm_.reshape(-1)[off:off + sz], pv_.reshape(-1)[off:off + sz])
        off += sz
    g_on = _my_block(sgrad["odd_norm"].reshape(1, d), d // N_DEV, 1)
    g_ec = _my_block(sgrad["even_conv_w"].reshape(3, CONV_CH), CONV_CH // N_DEV, 1)
    g_oc = _my_block(sgrad["odd_conv_w"].reshape(4, _QKV_W), _QKV_W // N_DEV, 1)
    shard_w = _pack_rows([odd_norm, even_conv_w, odd_conv_w])
    sd_, sm_, sv_ = adam_small(shard_w, _pack_rows([g_on, g_ec, g_oc]),
                               _pack_rows([m_odd_norm, m_even_conv_w, m_odd_conv_w]),
                               _pack_rows([v_odd_norm, v_even_conv_w, v_odd_conv_w]), "adamw_sharded_small")
    off = 0
    for n, gfull, like in (("odd_norm", g_on, odd_norm), ("even_conv_w", g_ec, even_conv_w), ("odd_conv_w", g_oc, odd_conv_w)):
        sz = like.size
        sres[n] = (gfull, sd_.reshape(-1)[off:off + sz], sm_.reshape(-1)[off:off + sz], sv_.reshape(-1)[off:off + sz])
        off += sz

    def small_out(name, like, kind):
        if name == "ffn_norm":
            return jnp.stack([sres["ffn_norm0"][kind], sres["ffn_norm1"][kind]]).reshape(like.shape)
        return sres[name][kind].reshape(like.shape)

    order = (("even_norm", even_norm), ("even_w_in", even_w_in), ("even_q_gain", even_q_gain),
             ("even_k_gain", even_k_gain), ("even_sinks", even_sinks), ("even_conv_w", even_conv_w),
             ("even_w_out", even_w_out), ("odd_norm", odd_norm), ("odd_w_in", odd_w_in), ("odd_conv_w", odd_conv_w),
             ("odd_a_log", odd_a_log), ("odd_dt_bias", odd_dt_bias), ("odd_o_gain", odd_o_gain),
             ("odd_w_out", odd_w_out), ("ffn_norm", ffn_norm), ("ffn_w_gate_up", ffn_w_gate_up),
             ("ffn_w_down", ffn_w_down))
    outs = [loss, grad_x.reshape(x.shape)]
    for kind in range(4):
        for name, like in order:
            outs.append(res[name][kind] if name in res else small_out(name, like, kind))
    return tuple(outs)
```

```python
import functools

import jax
import jax.numpy as jnp
from jax import lax
from jax.experimental import pallas as pl
from jax.experimental.pallas import tpu as pltpu

F32 = jnp.float32
MXU_DTYPE = jnp.bfloat16
HI = lax.Precision.HIGH
EPS = 1e-6
N_DEV = 8
D_MODEL = 1024
HEAD_DIM = 64
ATTN_HEADS = 8
KV_HEADS = 2
ATTN_BLOCK = 128
Q_W = 512
KV_W = 128
CONV_CH = 512
EVEN_IN_W = 2304
DN_HEADS = 8
DN_DIM = 128
DN_W = 1024
DN_CHUNK = 64
ODD_IN_W = 4112
ODD_IN_PAD = 4224
D_FF = 2816
NEG = -1e30
VMEM_LIMIT = 56 * 1024 * 1024
ADAM_LR, ADAM_B1, ADAM_B2, ADAM_EPS, ADAM_WD, ADAM_STEP = 0.001, 0.9, 0.999, 1e-08, 0.01, 10
MESH = pl.DeviceIdType.MESH


def _cp(*sem):
    return pltpu.CompilerParams(dimension_semantics=sem, vmem_limit_bytes=VMEM_LIMIT)


def _pick(n, cap):
    best = 128
    for t in range(128, cap + 1, 128):
        if n % t == 0:
            best = t
    return best


def _mx(a, b):
    return jnp.dot(a.astype(MXU_DTYPE), b.astype(MXU_DTYPE), preferred_element_type=F32)


def _mx_nt(a, b):
    return lax.dot_general(a.astype(MXU_DTYPE), b.astype(MXU_DTYPE), (((1,), (1,)), ((), ())),
                           preferred_element_type=F32)


def _mx_tn(a, b):
    return lax.dot_general(a.astype(MXU_DTYPE), b.astype(MXU_DTYPE), (((0,), (0,)), ((), ())),
                           preferred_element_type=F32)


def _hi(a, b):
    return jnp.dot(a, b, precision=HI, preferred_element_type=F32)


def _hi_nt(a, b):
    return lax.dot_general(a, b, (((1,), (1,)), ((), ())), precision=HI, preferred_element_type=F32)


def _hi_tn(a, b):
    return lax.dot_general(a, b, (((0,), (0,)), ((), ())), precision=HI, preferred_element_type=F32)


def _sigmoid(x):
    return 1.0 / (1.0 + jnp.exp(-x))


def _softplus(x):
    return jnp.maximum(x, 0.0) + jnp.log(1.0 + jnp.exp(-jnp.abs(x)))


def mm_nn(a, b, name, res=None, out_dtype=F32, tm=1024):
    m, k = a.shape
    _, n = b.shape
    tn = _pick(n, 1536)
    tm = min(tm, m)

    def body(*refs):
        a_ref, b_ref = refs[0], refs[1]
        o_ref = refs[-1]
        acc = _mx(a_ref[...], b_ref[...])
        if res is not None:
            acc = acc + refs[2][...]
        o_ref[...] = acc.astype(o_ref.dtype)

    in_specs = [pl.BlockSpec((tm, k), lambda j, i: (i, 0)), pl.BlockSpec((k, tn), lambda j, i: (0, j))]
    args = [a, b]
    if res is not None:
        in_specs.append(pl.BlockSpec((tm, tn), lambda j, i: (i, j)))
        args.append(res)
    return pl.pallas_call(
        body, name=name, grid=(n // tn, m // tm), in_specs=in_specs,
        out_specs=pl.BlockSpec((tm, tn), lambda j, i: (i, j)),
        out_shape=jax.ShapeDtypeStruct((m, n), out_dtype), compiler_params=_cp("parallel", "parallel"))(*args)


def mm_nt(a, b, name, out_dtype=F32, tm=1024, after=None):
    m, k = a.shape
    n, _ = b.shape
    tn = _pick(n, 512 if k > 3000 else 1536)
    tm = min(tm, m)

    def body(a_ref, b_ref, *rest):
        o_ref = rest[-1]
        o_ref[...] = _mx_nt(a_ref[...], b_ref[...]).astype(o_ref.dtype)

    in_specs = [pl.BlockSpec((tm, k), lambda j, i: (i, 0)), pl.BlockSpec((tn, k), lambda j, i: (j, 0))]
    args = [a, b]
    if after is not None:
        in_specs.append(pl.BlockSpec(memory_space=pl.ANY))
        args.append(after)
    return pl.pallas_call(
        body, name=name, grid=(n // tn, m // tm), in_specs=in_specs,
        out_specs=pl.BlockSpec((tm, tn), lambda j, i: (i, j)),
        out_shape=jax.ShapeDtypeStruct((m, n), out_dtype), compiler_params=_cp("parallel", "parallel"))(*args)


def mm_at(at, b, name, tk=1024):
    m, kk = at.shape
    _, n = b.shape
    tm, tn, tk = _pick(m, 1408), _pick(n, 1408), min(tk, kk)
    nk = kk // tk

    def body(a_ref, b_ref, o_ref, acc_ref):
        k = pl.program_id(2)
        p = _mx(a_ref[...], b_ref[...])
        acc = jnp.where(k == 0, p, acc_ref[...] + p)
        acc_ref[...] = acc

        @pl.when(k == nk - 1)
        def _():
            o_ref[...] = acc.astype(o_ref.dtype)

    return pl.pallas_call(
        body, name=name, grid=(m // tm, n // tn, nk),
        in_specs=[pl.BlockSpec((tm, tk), lambda i, j, k: (i, k)), pl.BlockSpec((tk, tn), lambda i, j, k: (k, j))],
        out_specs=pl.BlockSpec((tm, tn), lambda i, j, k: (i, j)),
        out_shape=jax.ShapeDtypeStruct((m, n), MXU_DTYPE), scratch_shapes=[pltpu.VMEM((tm, tn), F32)],
        compiler_params=_cp("parallel", "parallel", "arbitrary"))(at, b)


def rms_fwd(x, g, name, tm=512, after=None):
    t, d = x.shape

    def body(x_ref, g_ref, *rest):
        o_ref, ot_ref = rest[-2:]
        xv = x_ref[...]
        r = lax.rsqrt(jnp.mean(xv * xv, axis=-1, keepdims=True) + EPS)
        h = xv * r * g_ref[...]
        o_ref[...] = h.astype(o_ref.dtype)
        ot_ref[...] = h.T.astype(ot_ref.dtype)

    in_specs = [pl.BlockSpec((tm, d), lambda i: (i, 0)), pl.BlockSpec((1, d), lambda i: (0, 0))]
    args = [x, g]
    if after is not None:
        in_specs.append(pl.BlockSpec(memory_space=pl.ANY))
        args.append(after)
    return pl.pallas_call(
        body, name=name, grid=(t // tm,), in_specs=in_specs,
        out_specs=(pl.BlockSpec((tm, d), lambda i: (i, 0)), pl.BlockSpec((d, tm), lambda i: (0, i))),
        out_shape=(jax.ShapeDtypeStruct((t, d), MXU_DTYPE), jax.ShapeDtypeStruct((d, t), MXU_DTYPE)),
        compiler_params=_cp("parallel"))(*args)


def rms_bwd(x, g, dh, dres, name, tm=512):
    t, d = x.shape

    def body(x_ref, g_ref, dh_ref, dres_ref, dx_ref, dg_ref):
        i = pl.program_id(0)
        xv = x_ref[...]
        r = lax.rsqrt(jnp.mean(xv * xv, axis=-1, keepdims=True) + EPS)
        xh = xv * r
        dhv = dh_ref[...]
        dxh = dhv * g_ref[...]
        dx_ref[...] = dres_ref[...] + r * (dxh - xh * jnp.mean(dxh * xh, axis=-1, keepdims=True))
        part = jnp.sum(dhv * xh, axis=0, keepdims=True)

        @pl.when(i == 0)
        def _():
            dg_ref[...] = part

        @pl.when(i > 0)
        def _():
            dg_ref[...] += part

    row = pl.BlockSpec((tm, d), lambda i: (i, 0))
    one = pl.BlockSpec((1, d), lambda i: (0, 0))
    return pl.pallas_call(
        body, name=name, grid=(t // tm,), in_specs=[row, one, row, row], out_specs=(row, one),
        out_shape=(jax.ShapeDtypeStruct((t, d), F32), jax.ShapeDtypeStruct((1, d), F32)),
        compiler_params=_cp("arbitrary"))(x, g, dh, dres)


GU_TILE = 1408


_GU_PER_TILE = GU_TILE * N_DEV // (2 * D_FF)


def _gu_gathered(g):
    _, k, c = g.shape
    nj = N_DEV // (2 * _GU_PER_TILE)
    return g.reshape(2, nj, _GU_PER_TILE, k, c).transpose(3, 1, 0, 2, 4).reshape(k, N_DEV * c)


def _gu_pieces(dw):
    k, n8 = dw.shape
    nj = N_DEV // (2 * _GU_PER_TILE)
    return dw.reshape(k, nj, 2, _GU_PER_TILE, n8 // N_DEV).transpose(2, 1, 3, 0, 4).reshape(N_DEV, k, n8 // N_DEV)


def ffn_up(f, w, name, tm=512):
    t, d = f.shape

    def body(f_ref, w_ref, gu_ref, a_ref, at_ref):
        gu = _mx(f_ref[...], w_ref[...])
        gu_ref[...] = gu
        g, u = gu[:, :GU_TILE], gu[:, GU_TILE:]
        act = g * _sigmoid(g) * u
        a_ref[...] = act.astype(a_ref.dtype)
        at_ref[...] = act.T.astype(at_ref.dtype)

    return pl.pallas_call(
        body, name=name, grid=(D_FF // GU_TILE, t // tm),
        in_specs=[pl.BlockSpec((tm, d), lambda j, i: (i, 0)), pl.BlockSpec((d, 2 * GU_TILE), lambda j, i: (0, j))],
        out_specs=(pl.BlockSpec((tm, 2 * GU_TILE), lambda j, i: (i, j)), pl.BlockSpec((tm, GU_TILE), lambda j, i: (i, j)),
                   pl.BlockSpec((GU_TILE, tm), lambda j, i: (j, i))),
        out_shape=(jax.ShapeDtypeStruct((t, 2 * D_FF), F32), jax.ShapeDtypeStruct((t, D_FF), MXU_DTYPE),
                   jax.ShapeDtypeStruct((D_FF, t), MXU_DTYPE)),
        compiler_params=_cp("parallel", "parallel"))(f, w)


def ffn_dact(dy, w_d, gu, name, tm=512, after=None):
    t, d = dy.shape

    def body(dy_ref, w_ref, gu_ref, *rest):
        o_ref = rest[-1]
        da = _mx_nt(dy_ref[...], w_ref[...])
        g, u = gu_ref[:, :GU_TILE], gu_ref[:, GU_TILE:]
        sg = _sigmoid(g)
        o_ref[:, :GU_TILE] = (da * u * sg * (1.0 + g * (1.0 - sg))).astype(o_ref.dtype)
        o_ref[:, GU_TILE:] = (da * g * sg).astype(o_ref.dtype)

    in_specs = [pl.BlockSpec((tm, d), lambda j, i: (i, 0)), pl.BlockSpec((GU_TILE, d), lambda j, i: (j, 0)),
                pl.BlockSpec((tm, 2 * GU_TILE), lambda j, i: (i, j))]
    args = [dy, w_d, gu]
    if after is not None:
        in_specs.append(pl.BlockSpec(memory_space=pl.ANY))
        args.append(after)
    return pl.pallas_call(
        body, name=name, grid=(D_FF // GU_TILE, t // tm), in_specs=in_specs,
        out_specs=pl.BlockSpec((tm, 2 * GU_TILE), lambda j, i: (i, j)),
        out_shape=jax.ShapeDtypeStruct((t, 2 * D_FF), MXU_DTYPE), compiler_params=_cp("parallel", "parallel"))(*args)


def loss_head(y, target, name, tm=512):
    t, d = y.shape

    def body(y_ref, t_ref, dy_ref, l_ref):
        i = pl.program_id(0)
        e = y_ref[...] - t_ref[...]
        dy_ref[...] = e * (1.0 / d)
        part = jnp.zeros((1, 128), F32) + 0.5 * jnp.sum(jnp.mean(e * e, axis=-1, keepdims=True), axis=0, keepdims=True)

        @pl.when(i == 0)
        def _():
            l_ref[...] = part

        @pl.when(i > 0)
        def _():
            l_ref[...] += part

    row = pl.BlockSpec((tm, d), lambda i: (i, 0))
    return pl.pallas_call(
        body, name=name, grid=(t // tm,), in_specs=[row, row],
        out_specs=(row, pl.BlockSpec((1, 128), lambda i: (0, 0))),
        out_shape=(jax.ShapeDtypeStruct((t, d), F32), jax.ShapeDtypeStruct((1, 128), F32)),
        compiler_params=_cp("arbitrary"))(y, target)


def _rot(x):
    return jnp.concatenate([-x[:, HEAD_DIM // 2:], x[:, :HEAD_DIM // 2]], axis=-1)


def _rot_t(y):
    return jnp.concatenate([y[:, HEAD_DIM // 2:], -y[:, :HEAD_DIM // 2]], axis=-1)


def qk_prep_fwd(proj, q_gain, k_gain, cosf, sinf, name, tm=256):
    t = proj.shape[0]
    nh = ATTN_HEADS + KV_HEADS

    def body(p_ref, qg_ref, kg_ref, c_ref, s_ref, q_ref, k_ref):
        c, s = c_ref[...], s_ref[...]
        outs = []
        for h in range(nh):
            xh = p_ref[:, h * HEAD_DIM:(h + 1) * HEAD_DIM]
            gain = qg_ref[...] if h < ATTN_HEADS else kg_ref[...]
            r = lax.rsqrt(jnp.mean(xh * xh, axis=-1, keepdims=True) + EPS)
            xn = xh * r * gain
            outs.append(xn * c + _rot(xn) * s)
        q_ref[...] = jnp.concatenate(outs[:ATTN_HEADS], axis=-1)
        k_ref[...] = jnp.concatenate(outs[ATTN_HEADS:], axis=-1)

    gspec = pl.BlockSpec((1, HEAD_DIM), lambda i: (0, 0))
    tspec = pl.BlockSpec((tm, HEAD_DIM), lambda i: (i, 0))
    return pl.pallas_call(
        body, name=name, grid=(t // tm,),
        in_specs=[pl.BlockSpec((tm, Q_W + KV_W), lambda i: (i, 0)), gspec, gspec, tspec, tspec],
        out_specs=(pl.BlockSpec((tm, Q_W), lambda i: (i, 0)), pl.BlockSpec((tm, KV_W), lambda i: (i, 0))),
        out_shape=(jax.ShapeDtypeStruct((t, Q_W), F32), jax.ShapeDtypeStruct((t, KV_W), F32)),
        compiler_params=_cp("parallel"))(proj, q_gain, k_gain, cosf, sinf)


def qk_prep_bwd(proj, q_gain, k_gain, cosf, sinf, dq, dk, name, tm=256):
    t = proj.shape[0]
    nh = ATTN_HEADS + KV_HEADS

    def body(p_ref, qg_ref, kg_ref, c_ref, s_ref, dq_ref, dk_ref, o_ref, dqg_ref, dkg_ref):
        i = pl.program_id(0)
        c, s = c_ref[...], s_ref[...]
        outs = []
        dqg = jnp.zeros((1, HEAD_DIM), F32)
        dkg = jnp.zeros((1, HEAD_DIM), F32)
        for h in range(nh):
            xh = p_ref[:, h * HEAD_DIM:(h + 1) * HEAD_DIM]
            if h < ATTN_HEADS:
                gain = qg_ref[...]
                dout = dq_ref[:, h * HEAD_DIM:(h + 1) * HEAD_DIM]
            else:
                gain = kg_ref[...]
                dout = dk_ref[:, (h - ATTN_HEADS) * HEAD_DIM:(h - ATTN_HEADS + 1) * HEAD_DIM]
            r = lax.rsqrt(jnp.mean(xh * xh, axis=-1, keepdims=True) + EPS)
            xhat = xh * r
            dxn = dout * c + _rot_t(dout * s)
            part = jnp.sum(dxn * xhat, axis=0, keepdims=True)
            if h < ATTN_HEADS:
                dqg = dqg + part
            else:
                dkg = dkg + part
            dxh = dxn * gain
            outs.append(r * (dxh - xhat * jnp.mean(dxh * xhat, axis=-1, keepdims=True)))
        o_ref[...] = jnp.concatenate(outs, axis=-1).astype(o_ref.dtype)

        @pl.when(i == 0)
        def _():
            dqg_ref[...] = dqg
            dkg_ref[...] = dkg

        @pl.when(i > 0)
        def _():
            dqg_ref[...] += dqg
            dkg_ref[...] += dkg

    gspec = pl.BlockSpec((1, HEAD_DIM), lambda i: (0, 0))
    tspec = pl.BlockSpec((tm, HEAD_DIM), lambda i: (i, 0))
    return pl.pallas_call(
        body, name=name, grid=(t // tm,),
        in_specs=[pl.BlockSpec((tm, Q_W + KV_W), lambda i: (i, 0)), gspec, gspec, tspec, tspec,
                  pl.BlockSpec((tm, Q_W), lambda i: (i, 0)), pl.BlockSpec((tm, KV_W), lambda i: (i, 0))],
        out_specs=(pl.BlockSpec((tm, Q_W + KV_W), lambda i: (i, 0)), gspec, gspec),
        out_shape=(jax.ShapeDtypeStruct((t, Q_W + KV_W), MXU_DTYPE), jax.ShapeDtypeStruct((1, HEAD_DIM), F32),
                   jax.ShapeDtypeStruct((1, HEAD_DIM), F32)),
        compiler_params=_cp("arbitrary"))(proj, q_gain, k_gain, cosf, sinf, dq, dk)


def _swa_valid(n, grp):
    qi = lax.broadcasted_iota(jnp.int32, (grp * ATTN_BLOCK, 2 * ATTN_BLOCK), 0) & (ATTN_BLOCK - 1)
    kj = lax.broadcasted_iota(jnp.int32, (grp * ATTN_BLOCK, 2 * ATTN_BLOCK), 1)
    diff = qi + ATTN_BLOCK - kj
    return (diff >= 0) & (diff < ATTN_BLOCK) & (n * ATTN_BLOCK - ATTN_BLOCK + kj >= 0)


def _stack_heads(ref, g, grp):
    return jnp.concatenate([ref[:, (g * grp + j) * HEAD_DIM:(g * grp + j + 1) * HEAD_DIM] for j in range(grp)], axis=0)


def _stack_sinks(s_ref, g, grp):
    return jnp.concatenate([jnp.zeros((ATTN_BLOCK, 1), F32) + s_ref[0:1, g * grp + j:g * grp + j + 1]
                            for j in range(grp)], axis=0)


def swa_fwd(q, k, proj, sinks, name):
    t = q.shape[0]
    nb = t // ATTN_BLOCK
    scale = HEAD_DIM ** -0.5
    grp = ATTN_HEADS // KV_HEADS

    def body(q_ref, kc_ref, kp_ref, vc_ref, vp_ref, s_ref, y_ref, yt_ref, lse_ref):
        n = pl.program_id(0)
        valid = _swa_valid(n, grp)
        kk = jnp.concatenate([kp_ref[...], kc_ref[...]], axis=0).astype(MXU_DTYPE)
        vv = jnp.concatenate([vp_ref[...], vc_ref[...]], axis=0).astype(MXU_DTYPE)
        lane = lax.broadcasted_iota(jnp.int32, (ATTN_BLOCK, ATTN_HEADS), 1)
        gs = range(KV_HEADS)
        qg = [_stack_heads(q_ref, g, grp) for g in gs]
        sink = [_stack_sinks(s_ref, g, grp) for g in gs]
        sc = [jnp.where(valid, _mx_nt(qg[g], kk[:, g * HEAD_DIM:(g + 1) * HEAD_DIM]) * scale, NEG) for g in gs]
        m = [jnp.maximum(jnp.max(sc[g], axis=-1, keepdims=True), sink[g]) for g in gs]
        e = [jnp.exp(sc[g] - m[g]) for g in gs]
        den = [jnp.sum(e[g], axis=-1, keepdims=True) + jnp.exp(sink[g] - m[g]) for g in gs]
        og = [_mx(e[g] / den[g], vv[:, g * HEAD_DIM:(g + 1) * HEAD_DIM]) for g in gs]
        lg = [m[g] + jnp.log(den[g]) for g in gs]
        lse = jnp.zeros((ATTN_BLOCK, ATTN_HEADS), F32)
        outs = []
        for h in range(ATTN_HEADS):
            rows = slice((h % grp) * ATTN_BLOCK, (h % grp + 1) * ATTN_BLOCK)
            outs.append(og[h // grp][rows])
            lse = jnp.where(lane == h, lg[h // grp][rows], lse)
        y = jnp.concatenate(outs, axis=-1)
        y_ref[...] = y
        yt_ref[...] = y.T.astype(yt_ref.dtype)
        lse_ref[...] = lse

    cur = lambda n: (n, 0)
    prev = lambda n: (jnp.maximum(n - 1, 0), 0)
    vcol = (Q_W + KV_W) // KV_W
    return pl.pallas_call(
        body, name=name, grid=(nb,),
        in_specs=[pl.BlockSpec((ATTN_BLOCK, Q_W), cur), pl.BlockSpec((ATTN_BLOCK, KV_W), cur),
                  pl.BlockSpec((ATTN_BLOCK, KV_W), prev),
                  pl.BlockSpec((ATTN_BLOCK, KV_W), lambda n: (n, vcol)),
                  pl.BlockSpec((ATTN_BLOCK, KV_W), lambda n: (jnp.maximum(n - 1, 0), vcol)),
                  pl.BlockSpec((1, ATTN_HEADS), lambda n: (0, 0))],
        out_specs=(pl.BlockSpec((ATTN_BLOCK, Q_W), cur), pl.BlockSpec((Q_W, ATTN_BLOCK), lambda n: (0, n)),
                   pl.BlockSpec((ATTN_BLOCK, ATTN_HEADS), cur)),
        out_shape=(jax.ShapeDtypeStruct((t, Q_W), F32), jax.ShapeDtypeStruct((Q_W, t), MXU_DTYPE),
                   jax.ShapeDtypeStruct((t, ATTN_HEADS), F32)),
        compiler_params=_cp("parallel"))(q, k, k, proj, proj, sinks)


def swa_bwd(q, k, proj, sinks, y, lse, dmix, name):
    t = q.shape[0]
    nb = t // ATTN_BLOCK
    scale = HEAD_DIM ** -0.5
    grp = ATTN_HEADS // KV_HEADS

    def body(q_ref, kc_ref, kp_ref, vc_ref, vp_ref, s_ref, y_ref, lse_ref, dy_ref,
             dq_ref, dk_ref, dv_ref, ds_ref, dkc, dvc):
        n = pl.program_id(0)

        @pl.when(n == 0)
        def _():
            dkc[...] = jnp.zeros_like(dkc)
            dvc[...] = jnp.zeros_like(dvc)
            ds_ref[...] = jnp.zeros_like(ds_ref)

        @pl.when(n < nb)
        def _():
            valid = _swa_valid(n, grp)
            kk = jnp.concatenate([kp_ref[...], kc_ref[...]], axis=0).astype(MXU_DTYPE)
            vv = jnp.concatenate([vp_ref[...], vc_ref[...]], axis=0).astype(MXU_DTYPE)
            lane = lax.broadcasted_iota(jnp.int32, (1, ATTN_HEADS), 1)
            gs = range(KV_HEADS)
            kg = [kk[:, g * HEAD_DIM:(g + 1) * HEAD_DIM] for g in gs]
            vg = [vv[:, g * HEAD_DIM:(g + 1) * HEAD_DIM] for g in gs]
            qg = [_stack_heads(q_ref, g, grp).astype(MXU_DTYPE) for g in gs]
            dog = [_stack_heads(dy_ref, g, grp) for g in gs]
            og = [_stack_heads(y_ref, g, grp) for g in gs]
            lg = [jnp.concatenate([lse_ref[:, g * grp + j:g * grp + j + 1] for j in range(grp)], axis=0) for g in gs]
            sink = [_stack_sinks(s_ref, g, grp) for g in gs]
            sc = [jnp.where(valid, _mx_nt(qg[g], kg[g]) * scale, NEG) for g in gs]
            p = [jnp.exp(sc[g] - lg[g]) for g in gs]
            delta = [jnp.sum(dog[g] * og[g], axis=-1, keepdims=True) for g in gs]
            ds = [p[g] * (_mx_nt(dog[g], vg[g]) - delta[g]) for g in gs]
            dqg = [_mx(ds[g], kg[g]) * scale for g in gs]
            dkf = jnp.concatenate([_mx_tn(ds[g], qg[g]) * scale for g in gs], axis=-1)
            dvf = jnp.concatenate([_mx_tn(p[g], dog[g]) for g in gs], axis=-1)
            dsk = [jnp.exp(sink[g] - lg[g]) * delta[g] for g in gs]
            dsink = jnp.zeros((1, ATTN_HEADS), F32)
            dqs = []
            for h in range(ATTN_HEADS):
                rows = slice((h % grp) * ATTN_BLOCK, (h % grp + 1) * ATTN_BLOCK)
                dqs.append(dqg[h // grp][rows])
                dsink = jnp.where(lane == h, -jnp.sum(dsk[h // grp][rows], axis=0, keepdims=True), dsink)
            dq_ref[...] = jnp.concatenate(dqs, axis=-1)
            dk_ref[...] = dkc[...] + dkf[:ATTN_BLOCK]
            dv_ref[...] = (dvc[...] + dvf[:ATTN_BLOCK]).astype(dv_ref.dtype)
            dkc[...] = dkf[ATTN_BLOCK:]
            dvc[...] = dvf[ATTN_BLOCK:]
            ds_ref[...] += dsink

        @pl.when(n == nb)
        def _():
            dk_ref[...] = dkc[...]
            dv_ref[...] = dvc[...].astype(dv_ref.dtype)

    cur = lambda n: (jnp.minimum(n, nb - 1), 0)
    prev = lambda n: (jnp.clip(n - 1, 0, nb - 1), 0)
    vcol = (Q_W + KV_W) // KV_W
    return pl.pallas_call(
        body, name=name, grid=(nb + 1,),
        in_specs=[pl.BlockSpec((ATTN_BLOCK, Q_W), cur), pl.BlockSpec((ATTN_BLOCK, KV_W), cur),
                  pl.BlockSpec((ATTN_BLOCK, KV_W), prev),
                  pl.BlockSpec((ATTN_BLOCK, KV_W), lambda n: (jnp.minimum(n, nb - 1), vcol)),
                  pl.BlockSpec((ATTN_BLOCK, KV_W), lambda n: (jnp.clip(n - 1, 0, nb - 1), vcol)),
                  pl.BlockSpec((1, ATTN_HEADS), lambda n: (0, 0)),
                  pl.BlockSpec((ATTN_BLOCK, Q_W), cur), pl.BlockSpec((ATTN_BLOCK, ATTN_HEADS), cur),
                  pl.BlockSpec((ATTN_BLOCK, Q_W), cur)],
        out_specs=(pl.BlockSpec((ATTN_BLOCK, Q_W), cur), pl.BlockSpec((ATTN_BLOCK, KV_W), prev),
                   pl.BlockSpec((ATTN_BLOCK, KV_W), prev), pl.BlockSpec((1, ATTN_HEADS), lambda n: (0, 0))),
        out_shape=(jax.ShapeDtypeStruct((t, Q_W), F32), jax.ShapeDtypeStruct((t, KV_W), F32),
                   jax.ShapeDtypeStruct((t, KV_W), MXU_DTYPE), jax.ShapeDtypeStruct((1, ATTN_HEADS), F32)),
        scratch_shapes=[pltpu.VMEM((ATTN_BLOCK, KV_W), F32), pltpu.VMEM((ATTN_BLOCK, KV_W), F32)],
        compiler_params=_cp("arbitrary"))(q, k, k, proj, proj, sinks, y, lse, dmix)


GC_W = 256
_GB0, _GC0, _XI0 = 768 // GC_W, 1280 // GC_W, 1792 // GC_W
HALO = 8


def gconv_fwd(proj, conv_w, name, tm=512):
    t = proj.shape[0]
    hb = tm // HALO

    def body(gb_ref, gc_ref, xi_ref, gch_ref, xih_ref, w_ref, y_ref, yt_ref):
        i = pl.program_id(1)
        u = gc_ref[...] * xi_ref[...]
        uh = jnp.where(i == 0, 0.0, gch_ref[...] * xih_ref[...])
        up = jnp.concatenate([uh, u], axis=0)
        cv = w_ref[0:1, :] * up[HALO - 2:HALO - 2 + tm]
        cv = cv + w_ref[1:2, :] * up[HALO - 1:HALO - 1 + tm]
        cv = cv + w_ref[2:3, :] * u
        y = gb_ref[...] * cv
        y_ref[...] = y.astype(y_ref.dtype)
        yt_ref[...] = y.T.astype(yt_ref.dtype)

    def col(c0):
        return pl.BlockSpec((tm, GC_W), lambda cj, i: (i, c0 + cj))

    def halo(c0):
        return pl.BlockSpec((HALO, GC_W), lambda cj, i: (jnp.maximum(i * hb - 1, 0), c0 + cj))

    return pl.pallas_call(
        body, name=name, grid=(CONV_CH // GC_W, t // tm),
        in_specs=[col(_GB0), col(_GC0), col(_XI0), halo(_GC0), halo(_XI0),
                  pl.BlockSpec((3, GC_W), lambda cj, i: (0, cj))],
        out_specs=(pl.BlockSpec((tm, GC_W), lambda cj, i: (i, cj)), pl.BlockSpec((GC_W, tm), lambda cj, i: (cj, i))),
        out_shape=(jax.ShapeDtypeStruct((t, CONV_CH), MXU_DTYPE), jax.ShapeDtypeStruct((CONV_CH, t), MXU_DTYPE)),
        compiler_params=_cp("parallel", "parallel"))(proj, proj, proj, proj, proj, conv_w)


def gconv_bwd(proj, conv_w, dmix, name, tm=512):
    t = proj.shape[0]
    hb = tm // HALO
    nt = t // tm
    dy0 = Q_W // GC_W

    def body(gb_ref, gc_ref, xi_ref, gch_ref, xih_ref, gbn_ref, dyn_ref, dy_ref, w_ref,
             dgb_ref, dgc_ref, dxi_ref, dw_ref):
        i = pl.program_id(1)
        gc, xi, gb, dy = gc_ref[...], xi_ref[...], gb_ref[...], dy_ref[...]
        u = gc * xi
        uh = jnp.where(i == 0, 0.0, gch_ref[...] * xih_ref[...])
        up = jnp.concatenate([uh, u], axis=0)
        u2 = up[HALO - 2:HALO - 2 + tm]
        u1 = up[HALO - 1:HALO - 1 + tm]
        cv = w_ref[0:1, :] * u2 + w_ref[1:2, :] * u1 + w_ref[2:3, :] * u
        dgb_ref[...] = (dy * cv).astype(dgb_ref.dtype)
        dcv = dy * gb
        dcvn = jnp.where(i == nt - 1, 0.0, dyn_ref[...] * gbn_ref[...])
        dcvp = jnp.concatenate([dcv, dcvn], axis=0)
        du = w_ref[0:1, :] * dcvp[2:2 + tm] + w_ref[1:2, :] * dcvp[1:1 + tm] + w_ref[2:3, :] * dcv
        dgc_ref[...] = (du * xi).astype(dgc_ref.dtype)
        dxi_ref[...] = (du * gc).astype(dxi_ref.dtype)
        dw = jnp.concatenate([jnp.sum(dcv * u2, axis=0, keepdims=True), jnp.sum(dcv * u1, axis=0, keepdims=True),
                              jnp.sum(dcv * u, axis=0, keepdims=True)], axis=0)

        @pl.when(i == 0)
        def _():
            dw_ref[...] = dw

        @pl.when(i > 0)
        def _():
            dw_ref[...] += dw

    def col(c0):
        return pl.BlockSpec((tm, GC_W), lambda cj, i: (i, c0 + cj))

    def halo(c0):
        return pl.BlockSpec((HALO, GC_W), lambda cj, i: (jnp.maximum(i * hb - 1, 0), c0 + cj))

    def nxt(c0):
        return pl.BlockSpec((HALO, GC_W), lambda cj, i: (jnp.minimum((i + 1) * hb, t // HALO - 1), c0 + cj))

    out = pl.BlockSpec((tm, GC_W), lambda cj, i: (i, cj))
    return pl.pallas_call(
        body, name=name, grid=(CONV_CH // GC_W, nt),
        in_specs=[col(_GB0), col(_GC0), col(_XI0), halo(_GC0), halo(_XI0), nxt(_GB0), nxt(dy0), col(dy0),
                  pl.BlockSpec((3, GC_W), lambda cj, i: (0, cj))],
        out_specs=(out, out, out, pl.BlockSpec((3, GC_W), lambda cj, i: (0, cj))),
        out_shape=(jax.ShapeDtypeStruct((t, CONV_CH), MXU_DTYPE),) * 3 + (jax.ShapeDtypeStruct((3, CONV_CH), F32),),
        compiler_params=_cp("parallel", "arbitrary"))(proj, proj, proj, proj, proj, proj, dmix, dmix, conv_w)


_QKV_W = 3 * DN_W
_BA_COL = (4 * DN_W) // 128
_Z_COL = _QKV_W // DN_W


def gdn_prep_fwd(proj, conv_w, alog_row, dtb_row, name, tm=256):
    t = proj.shape[0]
    hb = tm // HALO
    qscale = DN_DIM ** -0.5

    def body(x_ref, xh_ref, w_ref, ba_ref, al_ref, dt_ref, q_ref, k_ref, v_ref, bg_ref):
        i = pl.program_id(0)
        for gi in range(3 * DN_HEADS):
            sl = slice(gi * DN_DIM, (gi + 1) * DN_DIM)
            xp = jnp.concatenate([jnp.where(i == 0, 0.0, xh_ref[:, sl]), x_ref[:, sl]], axis=0)
            c = w_ref[0:1, sl] * xp[HALO - 3:HALO - 3 + tm]
            for j in range(1, 4):
                c = c + w_ref[j:j + 1, sl] * xp[HALO - 3 + j:HALO - 3 + j + tm]
            s = c * _sigmoid(c)
            osl = slice((gi % DN_HEADS) * DN_DIM, (gi % DN_HEADS + 1) * DN_DIM)
            if gi < DN_HEADS:
                q_ref[:, osl] = s * lax.rsqrt(jnp.sum(s * s, axis=-1, keepdims=True) + EPS) * qscale
            elif gi < 2 * DN_HEADS:
                k_ref[:, osl] = s * lax.rsqrt(jnp.sum(s * s, axis=-1, keepdims=True) + EPS)
            else:
                v_ref[:, osl] = s
        ba = ba_ref[...]
        lane = lax.broadcasted_iota(jnp.int32, ba.shape, 1)
        gval = -jnp.exp(al_ref[...]) * _softplus(ba + dt_ref[...])
        bg_ref[...] = jnp.where(lane < DN_HEADS, _sigmoid(ba), jnp.where(lane < 2 * DN_HEADS, gval, 0.0))

    row = pl.BlockSpec((tm, DN_W), lambda i: (i, 0))
    one = pl.BlockSpec((1, 128), lambda i: (0, 0))
    return pl.pallas_call(
        body, name=name, grid=(t // tm,),
        in_specs=[pl.BlockSpec((tm, _QKV_W), lambda i: (i, 0)),
                  pl.BlockSpec((HALO, _QKV_W), lambda i: (jnp.maximum(i * hb - 1, 0), 0)),
                  pl.BlockSpec((4, _QKV_W), lambda i: (0, 0)),
                  pl.BlockSpec((tm, 128), lambda i: (i, _BA_COL)), one, one],
        out_specs=(row, row, row, pl.BlockSpec((tm, 128), lambda i: (i, 0))),
        out_shape=(jax.ShapeDtypeStruct((t, DN_W), F32),) * 3 + (jax.ShapeDtypeStruct((t, 128), F32),),
        compiler_params=_cp("parallel"))(proj, proj, conv_w, proj, alog_row, dtb_row)


def gdn_prep_bwd(proj, conv_w, alog_row, dtb_row, dq, dk, dv, dbg, name, tm=256):
    t = proj.shape[0]
    hb = tm // HALO
    nt = t // tm
    qscale = DN_DIM ** -0.5
    te = tm + HALO

    def body(x_ref, xh_ref, xn_ref, w_ref, ba_ref, al_ref, dt_ref, dq_ref, dk_ref, dv_ref,
             dqn_ref, dkn_ref, dvn_ref, dbg_ref, dx_ref, dba_ref, dw_ref, ddt_ref, dal_ref):
        i = pl.program_id(0)
        first = i == 0
        last = i == nt - 1
        dws = []
        for gi in range(3 * DN_HEADS):
            sl = slice(gi * DN_DIM, (gi + 1) * DN_DIM)
            osl = slice((gi % DN_HEADS) * DN_DIM, (gi % DN_HEADS + 1) * DN_DIM)
            xe = jnp.concatenate([jnp.where(first, 0.0, xh_ref[:, sl]), x_ref[:, sl], xn_ref[:, sl]], axis=0)
            c = w_ref[0:1, sl] * xe[HALO - 3:HALO - 3 + te]
            for j in range(1, 4):
                c = c + w_ref[j:j + 1, sl] * xe[HALO - 3 + j:HALO - 3 + j + te]
            sg = _sigmoid(c)
            s = c * sg
            d_ref, dn_ref = ((dq_ref, dqn_ref), (dk_ref, dkn_ref), (dv_ref, dvn_ref))[gi // DN_HEADS]
            dy = jnp.concatenate([d_ref[:, osl], jnp.where(last, 0.0, dn_ref[:, osl])], axis=0)
            if gi < 2 * DN_HEADS:
                r = lax.rsqrt(jnp.sum(s * s, axis=-1, keepdims=True) + EPS)
                sh = s * r
                ds = r * (dy - sh * jnp.sum(sh * dy, axis=-1, keepdims=True))
                if gi < DN_HEADS:
                    ds = ds * qscale
            else:
                ds = dy
            dc = ds * sg * (1.0 + c * (1.0 - sg))
            dcs = [dc[3 - j:3 - j + tm] for j in range(4)]
            dx = w_ref[0:1, sl] * dcs[0]
            for j in range(1, 4):
                dx = dx + w_ref[j:j + 1, sl] * dcs[j]
            dx_ref[:, sl] = dx.astype(dx_ref.dtype)
            x0 = x_ref[:, sl]
            dws.append(jnp.concatenate([jnp.sum(dcs[j] * x0, axis=0, keepdims=True) for j in range(4)], axis=0))
        dw = jnp.concatenate(dws, axis=-1)
        ba = ba_ref[...]
        dbgv = dbg_ref[...]
        lane = lax.broadcasted_iota(jnp.int32, ba.shape, 1)
        beta = _sigmoid(ba)
        ea = -jnp.exp(al_ref[...])
        zin = ba + dt_ref[...]
        is_b = lane < DN_HEADS
        is_a = (lane >= DN_HEADS) & (lane < 2 * DN_HEADS)
        da = jnp.where(is_a, dbgv * ea * _sigmoid(zin), 0.0)
        dba_ref[...] = jnp.where(is_b, dbgv * beta * (1.0 - beta), da).astype(dba_ref.dtype)
        ddt = jnp.sum(da, axis=0, keepdims=True)
        dal = jnp.sum(jnp.where(is_a, dbgv * ea * _softplus(zin), 0.0), axis=0, keepdims=True)

        @pl.when(first)
        def _():
            dw_ref[...] = dw
            ddt_ref[...] = ddt
            dal_ref[...] = dal

        @pl.when(i > 0)
        def _():
            dw_ref[...] += dw
            ddt_ref[...] += ddt
            dal_ref[...] += dal

    row = pl.BlockSpec((tm, DN_W), lambda i: (i, 0))
    nrow = pl.BlockSpec((HALO, DN_W), lambda i: (jnp.minimum((i + 1) * hb, t // HALO - 1), 0))
    one = pl.BlockSpec((1, 128), lambda i: (0, 0))
    return pl.pallas_call(
        body, name=name, grid=(nt,),
        in_specs=[pl.BlockSpec((tm, _QKV_W), lambda i: (i, 0)),
                  pl.BlockSpec((HALO, _QKV_W), lambda i: (jnp.maximum(i * hb - 1, 0), 0)),
                  pl.BlockSpec((HALO, _QKV_W), lambda i: (jnp.minimum((i + 1) * hb, t // HALO - 1), 0)),
                  pl.BlockSpec((4, _QKV_W), lambda i: (0, 0)),
                  pl.BlockSpec((tm, 128), lambda i: (i, _BA_COL)), one, one,
                  row, row, row, nrow, nrow, nrow, pl.BlockSpec((tm, 128), lambda i: (i, 0))],
        out_specs=(pl.BlockSpec((tm, _QKV_W), lambda i: (i, 0)), pl.BlockSpec((tm, 128), lambda i: (i, 0)),
                   pl.BlockSpec((4, _QKV_W), lambda i: (0, 0)), one, one),
        out_shape=(jax.ShapeDtypeStruct((t, _QKV_W), MXU_DTYPE), jax.ShapeDtypeStruct((t, 128), MXU_DTYPE),
                   jax.ShapeDtypeStruct((4, _QKV_W), F32), jax.ShapeDtypeStruct((1, 128), F32),
                   jax.ShapeDtypeStruct((1, 128), F32)),
        compiler_params=_cp("arbitrary"))(proj, proj, proj, conv_w, proj, alog_row, dtb_row, dq, dk, dv, dq, dk, dv, dbg)


def _chunk_masks():
    r = lax.broadcasted_iota(jnp.int32, (DN_CHUNK, DN_CHUNK), 0)
    c = lax.broadcasted_iota(jnp.int32, (DN_CHUNK, DN_CHUNK), 1)
    return r >= c, r > c


INV_PACK = 2


def _inv_unit_lower_many(mats):
    n = DN_CHUNK
    wide = INV_PACK * n
    r = lax.broadcasted_iota(jnp.int32, (wide, wide), 0)
    c = lax.broadcasted_iota(jnp.int32, (wide, wide), 1)
    same = (r // n) == (c // n)
    eye = jnp.where((r[:n] == (c[:n] % n)), 1.0, 0.0)

    def blockdiag(row):
        return jnp.where(same, jnp.concatenate([row] * INV_PACK, axis=0), 0.0)

    packs = [jnp.concatenate(mats[g:g + INV_PACK], axis=-1) for g in range(0, len(mats), INV_PACK)]
    xs = [eye - a for a in packs]
    pws = [_hi(a, blockdiag(a)) for a in packs]
    for step in range(5):
        if step < 4:
            both = [_hi(jnp.concatenate([x, pw], axis=0), blockdiag(pw)) for x, pw in zip(xs, pws)]
            xs = [x + b[:n] for x, b in zip(xs, both)]
            pws = [b[n:] for b in both]
        else:
            xs = [x + _hi(x, blockdiag(pw)) for x, pw in zip(xs, pws)]
    return [x[:, j * n:(j + 1) * n] for x in xs for j in range(INV_PACK)]


def _chunk_common(q, k, v, beta, gc, gcr, lower, strict):
    gam = jnp.exp(jnp.where(lower, gc - gcr, NEG))
    eg = jnp.exp(gc)
    gl = gc[DN_CHUNK - 1:DN_CHUNK, :]
    kdf = jnp.exp(gl - gc)
    kb = k * beta
    bmat = _mx_nt(kb, k)
    qmat = _mx_nt(q, k)
    return gam, eg, jnp.exp(gl), kdf, kb, bmat, qmat


def gdn_fwd(q, k, v, bg, name):
    t = q.shape[0]
    n_chunks = t // DN_CHUNK

    def body(q_ref, k_ref, v_ref, bg_ref, o_ref, sall_ref, tall_ref, s_ref):
        n = pl.program_id(0)

        @pl.when(n == 0)
        def _():
            s_ref[...] = jnp.zeros_like(s_ref)

        lower, strict = _chunk_masks()
        bgv = bg_ref[...]
        gcs = _hi(jnp.where(lower, 1.0, 0.0), bgv)
        gcs_t = gcs.T
        hs = range(DN_HEADS)
        sl = [slice(h * DN_DIM, (h + 1) * DN_DIM) for h in hs]
        st = [s_ref[h] for h in hs]
        for h in hs:
            sall_ref[0, h] = st[h]
        qh = [q_ref[:, sl[h]] for h in hs]
        kh = [k_ref[:, sl[h]] for h in hs]
        vh = [v_ref[:, sl[h]] for h in hs]
        beta = [bgv[:, h:h + 1] for h in hs]
        com = [_chunk_common(qh[h], kh[h], vh[h], beta[h], gcs[:, DN_HEADS + h:DN_HEADS + h + 1],
                             gcs_t[DN_HEADS + h:DN_HEADS + h + 1, :], lower, strict) for h in hs]
        gam, eg, dec, kdf, kb, bmat, qmat = zip(*com)
        tms = _inv_unit_lower_many([jnp.where(strict, bmat[h] * gam[h], 0.0) for h in hs])
        for h in hs:
            tall_ref[0, h] = tms[h]
        uw = [_hi(tms[h], jnp.concatenate([vh[h] * beta[h], kb[h] * eg[h]], axis=-1)) for h in hs]
        v_new = [uw[h][:, :DN_DIM] - _mx(uw[h][:, DN_DIM:], st[h]) for h in hs]
        o_st = [_mx(qh[h] * eg[h], st[h]) for h in hs]
        o_in = [_mx(qmat[h] * gam[h], v_new[h]) for h in hs]
        s_up = [_mx_tn(kh[h] * kdf[h], v_new[h]) for h in hs]
        for h in hs:
            o_ref[:, sl[h]] = o_st[h] + o_in[h]
            s_ref[h] = st[h] * dec[h] + s_up[h]

    row = pl.BlockSpec((DN_CHUNK, DN_W), lambda n: (n, 0))
    return pl.pallas_call(
        body, name=name, grid=(n_chunks,),
        in_specs=[row, row, row, pl.BlockSpec((DN_CHUNK, 128), lambda n: (n, 0))],
        out_specs=(row, pl.BlockSpec((1, DN_HEADS, DN_DIM, DN_DIM), lambda n: (n, 0, 0, 0)),
                   pl.BlockSpec((1, DN_HEADS, DN_CHUNK, DN_CHUNK), lambda n: (n, 0, 0, 0))),
        out_shape=(jax.ShapeDtypeStruct((t, DN_W), F32),
                   jax.ShapeDtypeStruct((n_chunks, DN_HEADS, DN_DIM, DN_DIM), F32),
                   jax.ShapeDtypeStruct((n_chunks, DN_HEADS, DN_CHUNK, DN_CHUNK), F32)),
        scratch_shapes=[pltpu.VMEM((DN_HEADS, DN_DIM, DN_DIM), F32)],
        compiler_params=_cp("arbitrary"))(q, k, v, bg)


def gdn_bwd(q, k, v, bg, sall, tall, do, name):
    t = q.shape[0]
    n_chunks = t // DN_CHUNK

    def body(q_ref, k_ref, v_ref, bg_ref, sall_ref, tall_ref, do_ref, dq_ref, dk_ref, dv_ref, dbg_ref, ds_ref):
        n = pl.program_id(0)

        @pl.when(n == 0)
        def _():
            ds_ref[...] = jnp.zeros_like(ds_ref)

        lower, strict = _chunk_masks()
        ltri = jnp.where(lower, 1.0, 0.0)
        bgv = bg_ref[...]
        gcs = _hi(ltri, bgv)
        gcs_t = gcs.T
        lane = lax.broadcasted_iota(jnp.int32, (DN_CHUNK, 128), 1)
        rowi = lax.broadcasted_iota(jnp.int32, (DN_CHUNK, 1), 0)
        hs = range(DN_HEADS)
        each = lambda fn, *ls: [fn(*a) for a in zip(*ls)]
        rsum = lambda a: jnp.sum(a, axis=-1, keepdims=True)
        sl = [slice(h * DN_DIM, (h + 1) * DN_DIM) for h in hs]
        st = [sall_ref[0, h] for h in hs]
        tms = [tall_ref[0, h] for h in hs]
        dsn = [ds_ref[h] for h in hs]
        qh = [q_ref[:, sl[h]] for h in hs]
        kh = [k_ref[:, sl[h]] for h in hs]
        vh = [v_ref[:, sl[h]] for h in hs]
        doh = [do_ref[:, sl[h]] for h in hs]
        beta = [bgv[:, h:h + 1] for h in hs]
        com = [_chunk_common(qh[h], kh[h], vh[h], beta[h], gcs[:, DN_HEADS + h:DN_HEADS + h + 1],
                             gcs_t[DN_HEADS + h:DN_HEADS + h + 1, :], lower, strict) for h in hs]
        gam, eg, dec, kdf, kb, bmat, qmat = zip(*com)
        rhs_w = each(lambda a, b: a * b, kb, eg)
        uw = each(lambda t_, v_, b_, r_: _hi(t_, jnp.concatenate([v_ * b_, r_], axis=-1)), tms, vh, beta, rhs_w)
        qd = each(lambda a, b: a * b, qh, eg)
        kd = each(lambda a, b: a * b, kh, kdf)
        pmat = each(lambda a, b: a * b, qmat, gam)
        v_new = each(lambda uw_, s_: uw_[:, :DN_DIM] - _mx(uw_[:, DN_DIM:], s_), uw, st)
        dqd = each(_mx_nt, doh, st)
        ds_o = each(_mx_tn, qd, doh)
        dp = each(lambda d_, v_: jnp.where(lower, _mx_nt(d_, v_), 0.0), doh, v_new)
        dvn_o = each(_mx_tn, pmat, doh)
        ddec = each(lambda d_, s_: jnp.sum(rsum(d_ * s_), axis=0, keepdims=True), dsn, st)
        dkd = each(_mx_nt, v_new, dsn)
        dvn = each(lambda a, k_, d_: a + _mx(k_, d_), dvn_o, kd, dsn)
        dw = each(lambda d_, s_: -_mx_nt(d_, s_), dvn, st)
        ds_w = each(lambda uw_, d_: _mx_tn(uw_[:, DN_DIM:], d_), uw, dvn)
        for h in hs:
            ds_ref[h] = ds_o[h] + dec[h] * dsn[h] - ds_w[h]
        dr = each(lambda t_, a, b: _hi_tn(t_, jnp.concatenate([a, b], axis=-1)), tms, dvn, dw)
        da = each(lambda r_, uw_: jnp.where(strict, -_hi_nt(r_, uw_), 0.0), dr, uw)
        dru = [r_[:, :DN_DIM] for r_ in dr]
        drw = [r_[:, DN_DIM:] for r_ in dr]
        db = each(lambda a, b: a * b, da, gam)
        dq_m = each(lambda a, b: a * b, dp, gam)
        e = each(lambda a, bm, p_, qm, g_: (a * bm + p_ * qm) * g_, da, bmat, dp, qmat, gam)
        dkb = each(lambda b_, k_, r_, e_: _mx(b_, k_) + r_ * e_, db, kh, drw, eg)
        dk = each(lambda b_, kb_, m_, q_, d_, f_: _mx_tn(b_, kb_) + _mx_tn(m_, q_) + d_ * f_, db, kb, dq_m, qh, dkd, kdf)
        dq = each(lambda m_, k_, d_, e_: _mx(m_, k_) + d_ * e_, dq_m, kh, dqd, eg)
        tk = each(lambda a, b: rsum(a * b), dkd, kd)
        dbeta_all = jnp.zeros((DN_CHUNK, 128), F32)
        dgc_all = jnp.zeros((DN_CHUNK, 128), F32)
        for h in hs:
            dgc = (jnp.sum(e[h], axis=1, keepdims=True) - jnp.sum(e[h].T, axis=1, keepdims=True)
                   + rsum(dqd[h] * qd[h]) - tk[h] + rsum(drw[h] * rhs_w[h]))
            dgl = jnp.sum(tk[h], axis=0, keepdims=True) + ddec[h] * dec[h]
            dgc = dgc + jnp.where(rowi == DN_CHUNK - 1, dgl, 0.0)
            dbeta = rsum(dru[h] * vh[h]) + rsum(dkb[h] * kh[h])
            dq_ref[:, sl[h]] = dq[h]
            dk_ref[:, sl[h]] = dk[h] + dkb[h] * beta[h]
            dv_ref[:, sl[h]] = dru[h] * beta[h]
            dbeta_all = jnp.where(lane == h, dbeta, dbeta_all)
            dgc_all = jnp.where(lane == DN_HEADS + h, dgc, dgc_all)
        dbg_ref[...] = dbeta_all + _hi_tn(ltri, dgc_all)

    rev = lambda n: (n_chunks - 1 - n, 0)
    row = pl.BlockSpec((DN_CHUNK, DN_W), rev)
    small = pl.BlockSpec((DN_CHUNK, 128), rev)
    return pl.pallas_call(
        body, name=name, grid=(n_chunks,),
        in_specs=[row, row, row, small,
                  pl.BlockSpec((1, DN_HEADS, DN_DIM, DN_DIM), lambda n: (n_chunks - 1 - n, 0, 0, 0)),
                  pl.BlockSpec((1, DN_HEADS, DN_CHUNK, DN_CHUNK), lambda n: (n_chunks - 1 - n, 0, 0, 0)), row],
        out_specs=(row, row, row, small),
        out_shape=(jax.ShapeDtypeStruct((t, DN_W), F32),) * 3 + (jax.ShapeDtypeStruct((t, 128), F32),),
        scratch_shapes=[pltpu.VMEM((DN_HEADS, DN_DIM, DN_DIM), F32)],
        compiler_params=_cp("arbitrary"))(q, k, v, bg, sall, tall, do)


def gdn_out_fwd(o, proj, o_gain, name, tm=256):
    t = o.shape[0]

    def body(o_ref, z_ref, g_ref, y_ref, yt_ref):
        for h in range(DN_HEADS):
            sl = slice(h * DN_DIM, (h + 1) * DN_DIM)
            ov, zv = o_ref[:, sl], z_ref[:, sl]
            r = lax.rsqrt(jnp.mean(ov * ov, axis=-1, keepdims=True) + EPS)
            y = ov * r * g_ref[...] * (zv * _sigmoid(zv))
            y_ref[:, sl] = y.astype(y_ref.dtype)
            yt_ref[sl, :] = y.T.astype(yt_ref.dtype)

    row = pl.BlockSpec((tm, DN_W), lambda i: (i, 0))
    return pl.pallas_call(
        body, name=name, grid=(t // tm,),
        in_specs=[row, pl.BlockSpec((tm, DN_W), lambda i: (i, _Z_COL)), pl.BlockSpec((1, DN_DIM), lambda i: (0, 0))],
        out_specs=(row, pl.BlockSpec((DN_W, tm), lambda i: (0, i))),
        out_shape=(jax.ShapeDtypeStruct((t, DN_W), MXU_DTYPE), jax.ShapeDtypeStruct((DN_W, t), MXU_DTYPE)),
        compiler_params=_cp("parallel"))(o, proj, o_gain)


def gdn_out_bwd(o, proj, o_gain, dy, name, tm=256):
    t = o.shape[0]

    def body(o_ref, z_ref, g_ref, dy_ref, do_ref, dz_ref, dg_ref):
        i = pl.program_id(0)
        dg = jnp.zeros((1, DN_DIM), F32)
        for h in range(DN_HEADS):
            sl = slice(h * DN_DIM, (h + 1) * DN_DIM)
            ov, zv, dyv = o_ref[:, sl], z_ref[:, sl], dy_ref[:, sl]
            r = lax.rsqrt(jnp.mean(ov * ov, axis=-1, keepdims=True) + EPS)
            oh = ov * r
            sg = _sigmoid(zv)
            dz_ref[:, sl] = (dyv * oh * g_ref[...] * sg * (1.0 + zv * (1.0 - sg))).astype(dz_ref.dtype)
            don = dyv * (zv * sg)
            dg = dg + jnp.sum(don * oh, axis=0, keepdims=True)
            doh = don * g_ref[...]
            do_ref[:, sl] = r * (doh - oh * jnp.mean(doh * oh, axis=-1, keepdims=True))

        @pl.when(i == 0)
        def _():
            dg_ref[...] = dg

        @pl.when(i > 0)
        def _():
            dg_ref[...] += dg

    row = pl.BlockSpec((tm, DN_W), lambda i: (i, 0))
    one = pl.BlockSpec((1, DN_DIM), lambda i: (0, 0))
    return pl.pallas_call(
        body, name=name, grid=(t // tm,),
        in_specs=[row, pl.BlockSpec((tm, DN_W), lambda i: (i, _Z_COL)), one, row],
        out_specs=(row, row, one),
        out_shape=(jax.ShapeDtypeStruct((t, DN_W), F32), jax.ShapeDtypeStruct((t, DN_W), MXU_DTYPE),
                   jax.ShapeDtypeStruct((1, DN_DIM), F32)),
        compiler_params=_cp("arbitrary"))(o, proj, o_gain, dy)


def _peer(k):
    x, y, c = lax.axis_index("x"), lax.axis_index("y"), lax.axis_index("c")
    px = 1 - x if k & 4 else x
    py = 1 - y if k & 2 else y
    pc = 1 - c if k & 1 else c
    return (px, py, pc), 4 * px + 2 * py + pc


def all_gather(shards, name):
    na = len(shards)

    def body(*refs):
        ins, outs = refs[:na], refs[na:2 * na]
        send_sems, recv_sems, local_sems = refs[2 * na:]
        _, me = _peer(0)
        local = [pltpu.make_async_copy(ins[a], outs[a].at[me], local_sems.at[a]) for a in range(na)]
        for cp in local:
            cp.start()
        sends = []
        for k in range(1, N_DEV):
            peer, _ = _peer(k)
            for a in range(na):
                cp = pltpu.make_async_remote_copy(
                    src_ref=ins[a], dst_ref=outs[a].at[me], send_sem=send_sems.at[a, k - 1],
                    recv_sem=recv_sems.at[a, k - 1], device_id=peer, device_id_type=MESH)
                cp.start()
                sends.append(cp)
        for k in range(1, N_DEV):
            peer, pid = _peer(k)
            for a in range(na):
                pltpu.make_async_remote_copy(
                    src_ref=ins[a], dst_ref=outs[a].at[pid], send_sem=send_sems.at[a, k - 1],
                    recv_sem=recv_sems.at[a, k - 1], device_id=peer, device_id_type=MESH).wait_recv()
        for cp in sends:
            cp.wait_send()
        for cp in local:
            cp.wait()

    anyspec = pl.BlockSpec(memory_space=pl.ANY)
    return pl.pallas_call(
        body, name=name, in_specs=[anyspec] * na, out_specs=tuple([anyspec] * na),
        out_shape=tuple(jax.ShapeDtypeStruct((N_DEV,) + s.shape, s.dtype) for s in shards),
        scratch_shapes=[pltpu.SemaphoreType.DMA((na, N_DEV - 1)), pltpu.SemaphoreType.DMA((na, N_DEV - 1)),
                        pltpu.SemaphoreType.DMA((na,))],
        compiler_params=pltpu.CompilerParams(has_side_effects=True))(*shards)


_HBM = pl.BlockSpec(memory_space=pltpu.HBM)
_SEM = pl.BlockSpec(memory_space=pltpu.SEMAPHORE)
_DATAFLOW = pltpu.SideEffectType.DATAFLOW_SIDE_EFFECTING
N_PEER = N_DEV - 1


def send_start(srcs, name, scatter, after):
    na = len(srcs)
    ns = (2 * N_PEER + 1) * na
    lands = [lax.empty((N_DEV,) + (s.shape[1:] if scatter else s.shape), s.dtype) for s in srcs]
    extra = [] if after is None else [after]

    def body(*refs):
        src_refs, land_refs = refs[:na], refs[na:2 * na]
        sems = refs[2 * na + len(extra):2 * na + len(extra) + ns]
        land_out, token = refs[-1 - na:-1], refs[-1]
        _, me = _peer(0)
        for a in range(na):
            pltpu.make_async_copy(src_refs[a].at[me] if scatter else src_refs[a], land_out[a].at[me],
                                  sems[2 * N_PEER * na + a]).start()
        for k in range(1, N_DEV):
            peer, pid = _peer(k)
            for a in range(na):
                pltpu.make_async_remote_copy(
                    src_ref=src_refs[a].at[pid] if scatter else src_refs[a], dst_ref=land_refs[a].at[me],
                    send_sem=sems[2 * (a * N_PEER + k - 1)], recv_sem=sems[2 * (a * N_PEER + k - 1) + 1],
                    device_id=peer, device_id_type=MESH).start()
        token[...] = jnp.zeros_like(token)

    hbm = lambda arrs: tuple(pltpu.HBM(a.shape, a.dtype) for a in arrs)
    outs = pl.pallas_call(
        body, name=name,
        out_shape=(pltpu.SemaphoreType.DMA(()),) * ns + hbm(srcs) + hbm(lands) + (jax.ShapeDtypeStruct((8, 128), F32),),
        in_specs=[_HBM] * (2 * na) + [pl.BlockSpec(memory_space=pl.ANY)] * len(extra),
        out_specs=(_SEM,) * ns + (_HBM,) * (2 * na) + (pl.BlockSpec(memory_space=pltpu.VMEM),),
        input_output_aliases={i: ns + i for i in range(2 * na)},
        compiler_params=pltpu.CompilerParams(has_side_effects=_DATAFLOW),
    )(*[pltpu.with_memory_space_constraint(a, pltpu.HBM) for a in list(srcs) + lands], *extra)
    return outs[:ns], outs[ns:ns + na], outs[ns + na:ns + 2 * na], outs[-1]


def send_wait(sems, srcs_thru, lands_thru, name, scatter, after):
    na = len(srcs_thru)
    ns = (2 * N_PEER + 1) * na

    def body(*refs):
        src_refs, land_refs, sm = refs[:na], refs[na:2 * na], refs[2 * na:2 * na + ns]
        _, me = _peer(0)
        for a in range(na):
            pltpu.make_async_copy(src_refs[a].at[me] if scatter else src_refs[a], land_refs[a].at[me],
                                  sm[2 * N_PEER * na + a]).wait()
        for k in range(1, N_DEV):
            peer, pid = _peer(k)
            for a in range(na):
                cp = pltpu.make_async_remote_copy(
                    src_ref=src_refs[a].at[pid] if scatter else src_refs[a], dst_ref=land_refs[a].at[pid],
                    send_sem=sm[2 * (a * N_PEER + k - 1)], recv_sem=sm[2 * (a * N_PEER + k - 1) + 1],
                    device_id=peer, device_id_type=MESH)
                cp.wait_send()
                cp.wait_recv()

    hbm = lambda arrs: tuple(pltpu.HBM(a.shape, a.dtype) for a in arrs)
    outs = pl.pallas_call(
        body, name=name, out_shape=hbm(srcs_thru) + hbm(lands_thru),
        in_specs=[_HBM] * (2 * na) + [_SEM] * ns + [pl.BlockSpec(memory_space=pl.ANY)], out_specs=(_HBM,) * (2 * na),
        input_output_aliases={i: i for i in range(2 * na)},
        compiler_params=pltpu.CompilerParams(has_side_effects=_DATAFLOW),
    )(*srcs_thru, *lands_thru, *sems, after)
    return outs[na:]


def _adamw(w, g, m, v):
    m = ADAM_B1 * m + (1.0 - ADAM_B1) * g
    v = ADAM_B2 * v + (1.0 - ADAM_B2) * (g * g)
    m_hat = m / (1.0 - ADAM_B1 ** ADAM_STEP)
    v_hat = v / (1.0 - ADAM_B2 ** ADAM_STEP)
    return -ADAM_LR * (m_hat / (jnp.sqrt(v_hat) + ADAM_EPS) + ADAM_WD * w), m, v


def adam_sum(w, pieces, m, v, name, layer=0, into=None):
    nl, r, c = w.shape
    tr = r
    for cand in (256, 128, 64, 32, 16, 8):
        if r % cand == 0:
            tr = cand
            break

    def body(w_ref, p_ref, m_ref, v_ref, *rest):
        g_ref, d_ref, nm_ref, nv_ref = rest[-4:]
        g = p_ref[0].astype(F32)
        for s in range(1, N_DEV):
            g = g + p_ref[s].astype(F32)
        g_ref[0] = g
        d_ref[0], nm_ref[0], nv_ref[0] = _adamw(w_ref[0], g, m_ref[0], v_ref[0])

    row = pl.BlockSpec((1, tr, c), lambda i: (layer, i, 0))
    out = jax.ShapeDtypeStruct((nl, r, c), F32)
    extra = [] if into is None else list(into)
    return pl.pallas_call(
        body, name=name, grid=(r // tr,),
        in_specs=[row, pl.BlockSpec((N_DEV, tr, c), lambda i: (0, i, 0)), row, row]
        + [pl.BlockSpec(memory_space=pl.ANY)] * len(extra),
        out_specs=(row,) * 4, out_shape=(out,) * 4,
        input_output_aliases={4 + i: i for i in range(len(extra))},
        compiler_params=_cp("parallel"))(w, pieces, m, v, *extra)


def sum_rows(gathered, name):
    _, r, c = gathered.shape

    def body(p_ref, o_ref):
        g = p_ref[0]
        for s in range(1, N_DEV):
            g = g + p_ref[s]
        o_ref[...] = g

    return pl.pallas_call(body, name=name, out_shape=jax.ShapeDtypeStruct((r, c), F32))(gathered)


def adam_small(w, g, m, v, name):
    def body(w_ref, g_ref, m_ref, v_ref, d_ref, nm_ref, nv_ref):
        d_ref[...], nm_ref[...], nv_ref[...] = _adamw(w_ref[...], g_ref[...], m_ref[...], v_ref[...])

    out = jax.ShapeDtypeStruct(w.shape, F32)
    return pl.pallas_call(body, name=name, out_shape=(out,) * 3)(w, g, m, v)


def _rope_tables(t):
    inv_freq = 10000.0 ** (-jnp.arange(0, HEAD_DIM, 2, dtype=F32) / HEAD_DIM)
    ang = jnp.arange(t, dtype=F32)[:, None] * inv_freq[None, :]
    cos, sin = jnp.cos(ang), jnp.sin(ang)
    return jnp.concatenate([cos, cos], axis=-1), jnp.concatenate([sin, sin], axis=-1)


def _lane_row(vec8):
    return jnp.pad(vec8.reshape(1, DN_HEADS), ((0, 0), (DN_HEADS, 128 - 2 * DN_HEADS)))


def _ffn_fwd(x, norm_g, w_gu, w_d, tag):
    f, ft = rms_fwd(x, norm_g, f"{tag}_norm")
    gu, a, at = ffn_up(f, w_gu, f"{tag}_gate_up")
    return mm_nn(a, w_d, f"{tag}_down", res=x), (ft, gu, at)


def _ffn_bwd(x, norm_g, w_gu, w_d, saved, dy, tag, after=None):
    ft, gu, at = saved
    dgu = ffn_dact(dy, w_d, gu, f"{tag}_d_gate_up", after=after)
    dwd = mm_at(at, dy, f"{tag}_dw_down")
    df = mm_nt(dgu, w_gu, f"{tag}_d_normed")
    dwgu = mm_at(ft, dgu, f"{tag}_dw_gate_up")
    dx, dg = rms_bwd(x, norm_g, df, dy, f"{tag}_d_norm")
    return dx, dwgu, dwd, dg


def local_step(x, target, small, weights_of, grads_out, after=None):
    t = x.shape[0]
    cosf, sinf = _rope_tables(t)
    alog_row, dtb_row = _lane_row(small["odd_a_log"]), _lane_row(small["odd_dt_bias"])

    h0, h0t = rms_fwd(x, small["even_norm"], "even_norm", after=after)
    we = weights_of("even", h0)
    proj0 = mm_nn(h0, we["w_in"], "even_in_proj")
    qr, kr = qk_prep_fwd(proj0, small["even_q_gain"], small["even_k_gain"], cosf, sinf, "even_qk_prep")
    y_attn, y_attn_t, lse = swa_fwd(qr, kr, proj0, small["even_sinks"], "even_swa")
    y_conv, y_conv_t = gconv_fwd(proj0, small["even_conv_w"], "even_gconv")
    mix0 = jnp.concatenate([y_attn.astype(MXU_DTYPE), y_conv], axis=-1)
    mix0_t = jnp.concatenate([y_attn_t, y_conv_t], axis=0)
    x1 = mm_nn(mix0, we["w_out"], "even_out_proj", res=x)
    w0 = weights_of("ffn0", x1)
    x2, ffn0 = _ffn_fwd(x1, small["ffn_norm0"], w0["gate_up"], w0["down"], "ffn0")

    wo = weights_of("odd", x2)
    h1, h1t = rms_fwd(x2, small["odd_norm"], "odd_norm")
    proj1 = mm_nn(h1, wo["w_in"], "odd_in_proj")
    qn, kn, vs, bg = gdn_prep_fwd(proj1, small["odd_conv_w"], alog_row, dtb_row, "odd_prep")
    o, sall, tall = gdn_fwd(qn, kn, vs, bg, "odd_delta_rule")
    og, ogt = gdn_out_fwd(o, proj1, small["odd_o_gain"], "odd_gate_norm")
    x3 = mm_nn(og, wo["w_out"], "odd_out_proj", res=x2)
    w1 = weights_of("ffn1", x3)
    x4, ffn1 = _ffn_fwd(x3, small["ffn_norm1"], w1["gate_up"], w1["down"], "ffn1")

    dy, loss_row = loss_head(x4, target, "loss_head")

    gs = {}
    dx3, dwgu, dwd, gs["ffn_norm1"] = _ffn_bwd(x3, small["ffn_norm1"], w1["gate_up"], w1["down"], ffn1, dy, "ffn1")
    tok = grads_out("ffn1", {"gate_up": dwgu, "down": dwd})

    dog = mm_nt(dx3, wo["w_out"], "odd_d_gated", after=tok)
    dwo = mm_at(ogt, dx3, "odd_dw_out")
    do, dz, gs["odd_o_gain"] = gdn_out_bwd(o, proj1, small["odd_o_gain"], dog, "odd_d_gate_norm")
    dqn, dkn, dvs, dbg = gdn_bwd(qn, kn, vs, bg, sall, tall, do, "odd_d_delta_rule")
    dqkv, dba, gs["odd_conv_w"], ddt_row, dal_row = gdn_prep_bwd(
        proj1, small["odd_conv_w"], alog_row, dtb_row, dqn, dkn, dvs, dbg, "odd_d_prep")
    gs["odd_dt_bias"] = ddt_row[:, DN_HEADS:2 * DN_HEADS]
    gs["odd_a_log"] = dal_row[:, DN_HEADS:2 * DN_HEADS]
    dproj1 = jnp.concatenate([dqkv, dz, dba], axis=-1)
    dh1 = mm_nt(dproj1, wo["w_in"], "odd_d_normed")
    dwi = mm_at(h1t, dproj1, "odd_dw_in")
    dx2, gs["odd_norm"] = rms_bwd(x2, small["odd_norm"], dh1, dx3, "odd_d_norm")
    tok = grads_out("odd", {"w_in": dwi, "w_out": dwo})

    dx1, dwgu, dwd, gs["ffn_norm0"] = _ffn_bwd(x1, small["ffn_norm0"], w0["gate_up"], w0["down"], ffn0, dx2, "ffn0",
                                               after=tok)
    tok = grads_out("ffn0", {"gate_up": dwgu, "down": dwd})

    dmix = mm_nt(dx1, we["w_out"], "even_d_mix", after=tok)
    dwo = mm_at(mix0_t, dx1, "even_dw_out")
    dqr, dkr, dv, gs["even_sinks"] = swa_bwd(qr, kr, proj0, small["even_sinks"], y_attn, lse, dmix, "even_d_swa")
    dqk, gs["even_q_gain"], gs["even_k_gain"] = qk_prep_bwd(
        proj0, small["even_q_gain"], small["even_k_gain"], cosf, sinf, dqr, dkr, "even_d_qk_prep")
    dgb, dgc, dxi, gs["even_conv_w"] = gconv_bwd(proj0, small["even_conv_w"], dmix, "even_d_gconv")
    dproj0 = jnp.concatenate([dqk, dv, dgb, dgc, dxi], axis=-1)
    dwi = mm_at(h0t, dproj0, "even_dw_in")
    tok = grads_out("even", {"w_in": dwi, "w_out": dwo})
    dh0 = mm_nt(dproj0, we["w_in"], "even_d_normed", after=tok)
    grad_x, gs["even_norm"] = rms_bwd(x, small["even_norm"], dh0, dx1, "even_d_norm")
    return loss_row, grad_x, gs


_SMALL_ORDER = ("even_norm", "even_q_gain", "even_k_gain", "even_sinks", "odd_a_log", "odd_dt_bias", "odd_o_gain",
                "ffn_norm0", "ffn_norm1", "odd_norm", "even_conv_w", "odd_conv_w")
_SMALL_SIZE = {"even_norm": 1024, "even_q_gain": 64, "even_k_gain": 64, "even_sinks": 8, "odd_a_log": 8,
               "odd_dt_bias": 8, "odd_o_gain": 128, "ffn_norm0": 1024, "ffn_norm1": 1024, "odd_norm": 1024,
               "even_conv_w": 3 * 512, "odd_conv_w": 4 * 3072}
_N_REPL = 9


def _pack_rows(vals):
    flat = jnp.concatenate([v.reshape(-1) for v in vals])
    pad = (-flat.shape[0]) % 1024
    return jnp.pad(flat, (0, pad)).reshape(-1, 128)


def _my_block(full, size, axis):
    me = 4 * lax.axis_index("x") + 2 * lax.axis_index("y") + lax.axis_index("c")
    return lax.dynamic_slice_in_dim(full, me * size, size, axis=axis)


def _col_gathered(g):
    return g.transpose(1, 0, 2).reshape(g.shape[1], N_DEV * g.shape[2])


def _col_pieces(dw):
    k, n8 = dw.shape
    return dw.reshape(k, N_DEV, n8 // N_DEV).transpose(1, 0, 2)


def kernel(x, even_norm, even_w_in, even_q_gain, even_k_gain, even_sinks, even_conv_w, even_w_out, odd_norm, odd_w_in, odd_conv_w, odd_a_log, odd_dt_bias, odd_o_gain, odd_w_out, ffn_norm, ffn_w_gate_up, ffn_w_down, loss_target, m_even_norm, m_even_w_in, m_even_q_gain, m_even_k_gain, m_even_sinks, m_even_conv_w, m_even_w_out, m_odd_norm, m_odd_w_in, m_odd_conv_w, m_odd_a_log, m_odd_dt_bias, m_odd_o_gain, m_odd_w_out, m_ffn_norm, m_ffn_w_gate_up, m_ffn_w_down, v_even_norm, v_even_w_in, v_even_q_gain, v_even_k_gain, v_even_sinks, v_even_conv_w, v_even_w_out, v_odd_norm, v_odd_w_in, v_odd_conv_w, v_odd_a_log, v_odd_dt_bias, v_odd_o_gain, v_odd_w_out, v_ffn_norm, v_ffn_w_gate_up, v_ffn_w_down):
    t = x.shape[1]
    d = D_MODEL

    me = 4 * lax.axis_index("x") + 2 * lax.axis_index("y") + lax.axis_index("c")
    fpd = D_FF // N_DEV
    shard = {
        "even": {"w_in": even_w_in.reshape(d, EVEN_IN_W // N_DEV), "w_out": even_w_out.reshape(d // N_DEV, d)},
        "ffn0": {"gate_up": ffn_w_gate_up[0], "down": ffn_w_down[0]},
        "odd": {"w_in": odd_w_in.reshape(d, ODD_IN_W // N_DEV), "w_out": odd_w_out.reshape(d // N_DEV, d)},
        "ffn1": {"gate_up": ffn_w_gate_up[1], "down": ffn_w_down[1]},
    }
    given = {
        ("even", "w_in"): ("even_w_in", even_w_in, m_even_w_in, v_even_w_in, 0),
        ("even", "w_out"): ("even_w_out", even_w_out, m_even_w_out, v_even_w_out, 0),
        ("odd", "w_in"): ("odd_w_in", odd_w_in, m_odd_w_in, v_odd_w_in, 0),
        ("odd", "w_out"): ("odd_w_out", odd_w_out, m_odd_w_out, v_odd_w_out, 0),
        ("ffn0", "gate_up"): ("ffn_w_gate_up", ffn_w_gate_up, m_ffn_w_gate_up, v_ffn_w_gate_up, 0),
        ("ffn1", "gate_up"): ("ffn_w_gate_up", ffn_w_gate_up, m_ffn_w_gate_up, v_ffn_w_gate_up, 1),
        ("ffn0", "down"): ("ffn_w_down", ffn_w_down, m_ffn_w_down, v_ffn_w_down, 0),
        ("ffn1", "down"): ("ffn_w_down", ffn_w_down, m_ffn_w_down, v_ffn_w_down, 1),
    }

    def whole(group, parts):
        col, row = tuple(shard[group])
        w_col = _gu_gathered(parts[0]) if col == "gate_up" else _col_gathered(parts[0])
        if group == "odd":
            w_col = jnp.pad(w_col, ((0, 0), (0, ODD_IN_PAD - ODD_IN_W)))
        return {col: w_col, row: parts[1].reshape(-1, d)}

    wire = {g: [a.astype(MXU_DTYPE) for a in shard[g].values()] for g in shard}
    gathers, tok = {}, None
    for g in shard:
        sems, srcs_thru, lands_thru, tok = send_start(wire[g], f"gather_{g}_start", False, tok)
        gathers[g] = (sems, srcs_thru, lands_thru)
        if g == "even":
            shard_rows = _pack_rows([odd_norm, even_conv_w, odd_conv_w]) + tok[0:1, 0:1]
            (small_g,) = all_gather([shard_rows], "gather_small_weights")
            tok = small_g

    def weights_of(group, after):
        lands = send_wait(*gathers[group], f"gather_{group}_wait", False, after)
        return whole(group, lands)

    sent = {}

    def grads_out(group, dws):
        col, row = tuple(shard[group])
        n_cols = N_DEV * shard[group][col].shape[1]
        pieces = [_gu_pieces(dws[col]) if col == "gate_up" else _col_pieces(dws[col][:, :n_cols]),
                  dws[row].reshape((N_DEV,) + shard[group][row].shape)]
        sems, srcs_thru, lands_thru, token = send_start(pieces, f"exchange_{group}_start", True, None)
        sent[group] = (sems, srcs_thru, lands_thru, pieces)
        return token

    sg = small_g.reshape(N_DEV, -1)
    o1 = d // N_DEV
    o2 = o1 + 3 * CONV_CH // N_DEV
    small = {
        "even_norm": even_norm, "even_q_gain": even_q_gain, "even_k_gain": even_k_gain, "even_sinks": even_sinks,
        "odd_a_log": odd_a_log.reshape(-1), "odd_dt_bias": odd_dt_bias.reshape(-1), "odd_o_gain": odd_o_gain,
        "ffn_norm0": ffn_norm[0:1], "ffn_norm1": ffn_norm[1:2],
        "odd_norm": sg[:, :o1].reshape(1, d),
        "even_conv_w": sg[:, o1:o2].reshape(N_DEV, 3, CONV_CH // N_DEV).transpose(1, 0, 2).reshape(3, CONV_CH),
        "odd_conv_w": sg[:, o2:o2 + 4 * _QKV_W // N_DEV].reshape(N_DEV, 4, _QKV_W // N_DEV).transpose(1, 0, 2).reshape(4, _QKV_W),
    }

    loss_row, grad_x, gs = local_step(x.reshape(t, d), loss_target.reshape(t, d), small, weights_of, grads_out, after=tok)

    rows = _pack_rows([gs[n] for n in _SMALL_ORDER] + [loss_row[:, 0:1]])
    (rows_g,) = all_gather([rows], "gather_small_grads")
    tot = sum_rows(rows_g, "sum_small_grads").reshape(-1)
    off, sgrad = 0, {}
    for n in _SMALL_ORDER:
        sgrad[n] = tot[off:off + _SMALL_SIZE[n]]
        off += _SMALL_SIZE[n]
    loss = tot[off]

    res, behind = {}, tot
    for g in ("ffn1", "odd", "ffn0", "even"):
        sems, srcs_thru, lands_thru, pieces = sent[g]
        lands = send_wait(sems, srcs_thru, lands_thru, f"exchange_{g}_wait", True, behind)
        for key, pcs in zip(shard[g], lands):
            name, w_, m_, v_, layer = given[g, key]
            res[name] = adam_sum(w_, pcs, m_, v_, f"adamw_{g}_{key}", layer=layer, into=res.get(name))
        behind = res[name][0]

    repl = _SMALL_ORDER[:_N_REPL]
    repl_w = {"even_norm": even_norm, "even_q_gain": even_q_gain, "even_k_gain": even_k_gain, "even_sinks": even_sinks,
              "odd_a_log": odd_a_log, "odd_dt_bias": odd_dt_bias, "odd_o_gain": odd_o_gain,
              "ffn_norm0": ffn_norm[0], "ffn_norm1": ffn_norm[1]}
    repl_m = {"even_norm": m_even_norm, "even_q_gain": m_even_q_gain, "even_k_gain": m_even_k_gain,
              "even_sinks": m_even_sinks, "odd_a_log": m_odd_a_log, "odd_dt_bias": m_odd_dt_bias,
              "odd_o_gain": m_odd_o_gain, "ffn_norm0": m_ffn_norm[0], "ffn_norm1": m_ffn_norm[1]}
    repl_v = {"even_norm": v_even_norm, "even_q_gain": v_even_q_gain, "even_k_gain": v_even_k_gain,
              "even_sinks": v_even_sinks, "odd_a_log": v_odd_a_log, "odd_dt_bias": v_odd_dt_bias,
              "odd_o_gain": v_odd_o_gain, "ffn_norm0": v_ffn_norm[0], "ffn_norm1": v_ffn_norm[1]}
    pk = lambda dct: _pack_rows([dct[n] for n in repl])
    pd_, pm_, pv_ = adam_small(pk(repl_w), pk(sgrad), pk(repl_m), pk(repl_v), "adamw_replicated")
    sres = {}
    off = 0
    for n in repl:
        sz = _SMALL_SIZE[n]
        sres[n] = (sgrad[n], pd_.reshape(-1)[off:off + sz], pm_.reshape(-1)[off:off + sz], pv_.reshape(-1)[off:off + sz])
        off += sz
    g_on = _my_block(sgrad["odd_norm"].reshape(1, d), d // N_DEV, 1)
    g_ec = _my_block(sgrad["even_conv_w"].reshape(3, CONV_CH), CONV_CH // N_DEV, 1)
    g_oc = _my_block(sgrad["odd_conv_w"].reshape(4, _QKV_W), _QKV_W // N_DEV, 1)
    shard_w = _pack_rows([odd_norm, even_conv_w, odd_conv_w])
    sd_, sm_, sv_ = adam_small(shard_w, _pack_rows([g_on, g_ec, g_oc]),
                               _pack_rows([m_odd_norm, m_even_conv_w, m_odd_conv_w]),
                               _pack_rows([v_odd_norm, v_even_conv_w, v_odd_conv_w]), "adamw_sharded_small")
    off = 0
    for n, gfull, like in (("odd_norm", g_on, odd_norm), ("even_conv_w", g_ec, even_conv_w), ("odd_conv_w", g_oc, odd_conv_w)):
        sz = like.size
        sres[n] = (gfull, sd_.reshape(-1)[off:off + sz], sm_.reshape(-1)[off:off + sz], sv_.reshape(-1)[off:off + sz])
        off += sz

    def small_out(name, like, kind):
        if name == "ffn_norm":
            return jnp.stack([sres["ffn_norm0"][kind], sres["ffn_norm1"][kind]]).reshape(like.shape)
        return sres[name][kind].reshape(like.shape)

    order = (("even_norm", even_norm), ("even_w_in", even_w_in), ("even_q_gain", even_q_gain),
             ("even_k_gain", even_k_gain), ("even_sinks", even_sinks), ("even_conv_w", even_conv_w),
             ("even_w_out", even_w_out), ("odd_norm", odd_norm), ("odd_w_in", odd_w_in), ("odd_conv_w", odd_conv_w),
             ("odd_a_log", odd_a_log), ("odd_dt_bias", odd_dt_bias), ("odd_o_gain", odd_o_gain),
             ("odd_w_out", odd_w_out), ("ffn_norm", ffn_norm), ("ffn_w_gate_up", ffn_w_gate_up),
             ("ffn_w_down", ffn_w_down))
    outs = [loss, grad_x.reshape(x.shape)]
    for kind in range(4):
        for name, like in order:
            outs.append(res[name][kind] if name in res else small_out(name, like, kind))
    return tuple(outs)
```

```python
import jax
import jax.numpy as jnp
import numpy as np
from jax import lax
from jax.experimental import pallas as pl
from jax.experimental.pallas import tpu as pltpu

F32 = jnp.float32
MXU_DTYPE = jnp.bfloat16
HI = lax.Precision.HIGH
EPS = 1e-6
N_DEV = 8
D_MODEL = 1024
HEAD_DIM = 64
ATTN_HEADS = 8
KV_HEADS = 2
ATTN_BLOCK = 128
Q_W = 512
KV_W = 128
CONV_CH = 512
EVEN_IN_W = 2304
DN_HEADS = 8
DN_DIM = 128
DN_W = 1024
DN_CHUNK = 64
ODD_IN_W = 4112
ODD_IN_PAD = 4224
D_FF = 2816
NEG = -1e30
VMEM_LIMIT = 56 * 1024 * 1024
ADAM_LR, ADAM_B1, ADAM_B2, ADAM_EPS, ADAM_WD, ADAM_STEP = 0.001, 0.9, 0.999, 1e-08, 0.01, 10
MESH = pl.DeviceIdType.MESH


def _cp(*sem):
    return pltpu.CompilerParams(dimension_semantics=sem, vmem_limit_bytes=VMEM_LIMIT)


def _pick(n, cap):
    best = 128
    for t in range(128, cap + 1, 128):
        if n % t == 0:
            best = t
    return best


def _mx(a, b):
    return jnp.dot(a.astype(MXU_DTYPE), b.astype(MXU_DTYPE), preferred_element_type=F32)


def _mx_nt(a, b):
    return lax.dot_general(a.astype(MXU_DTYPE), b.astype(MXU_DTYPE), (((1,), (1,)), ((), ())),
                           preferred_element_type=F32)


def _mx_tn(a, b):
    return lax.dot_general(a.astype(MXU_DTYPE), b.astype(MXU_DTYPE), (((0,), (0,)), ((), ())),
                           preferred_element_type=F32)


def _hi(a, b):
    return jnp.dot(a, b, precision=HI, preferred_element_type=F32)


def _hi_nt(a, b):
    return lax.dot_general(a, b, (((1,), (1,)), ((), ())), precision=HI, preferred_element_type=F32)


def _hi_tn(a, b):
    return lax.dot_general(a, b, (((0,), (0,)), ((), ())), precision=HI, preferred_element_type=F32)


def _sigmoid(x):
    return 0.5 * jnp.tanh(0.5 * x) + 0.5


def _softplus(x):
    return jnp.maximum(x, 0.0) + jnp.log(1.0 + jnp.exp(-jnp.abs(x)))


def mm_nn(a, b, name, res=None, out_dtype=F32, tm=1024):
    m, k = a.shape
    _, n = b.shape
    tn = _pick(n, 1536)
    tm = min(tm, m)

    def body(*refs):
        a_ref, b_ref = refs[0], refs[1]
        o_ref = refs[-1]
        acc = _mx(a_ref[...], b_ref[...])
        if res is not None:
            acc = acc + refs[2][...]
        o_ref[...] = acc.astype(o_ref.dtype)

    in_specs = [pl.BlockSpec((tm, k), lambda j, i: (i, 0)), pl.BlockSpec((k, tn), lambda j, i: (0, j))]
    args = [a, b]
    if res is not None:
        in_specs.append(pl.BlockSpec((tm, tn), lambda j, i: (i, j)))
        args.append(res)
    return pl.pallas_call(
        body, name=name, grid=(n // tn, m // tm), in_specs=in_specs,
        out_specs=pl.BlockSpec((tm, tn), lambda j, i: (i, j)),
        out_shape=jax.ShapeDtypeStruct((m, n), out_dtype), compiler_params=_cp("parallel", "parallel"))(*args)


def mm_nt(a, b, name, out_dtype=F32, tm=1024, after=None):
    m, k = a.shape
    n, _ = b.shape
    tn = _pick(n, 512 if k > 3000 else 1536)
    tm = min(tm, m)

    def body(a_ref, b_ref, *rest):
        o_ref = rest[-1]
        o_ref[...] = _mx_nt(a_ref[...], b_ref[...]).astype(o_ref.dtype)

    in_specs = [pl.BlockSpec((tm, k), lambda j, i: (i, 0)), pl.BlockSpec((tn, k), lambda j, i: (j, 0))]
    args = [a, b]
    if after is not None:
        in_specs.append(pl.BlockSpec(memory_space=pl.ANY))
        args.append(after)
    return pl.pallas_call(
        body, name=name, grid=(n // tn, m // tm), in_specs=in_specs,
        out_specs=pl.BlockSpec((tm, tn), lambda j, i: (i, j)),
        out_shape=jax.ShapeDtypeStruct((m, n), out_dtype), compiler_params=_cp("parallel", "parallel"))(*args)


def mm_at(at, b, name, tk=1024):
    m, kk = at.shape
    _, n = b.shape
    tm, tn, tk = _pick(m, 1408), _pick(n, 1408), min(tk, kk)
    nk = kk // tk

    def body(a_ref, b_ref, o_ref, acc_ref):
        k = pl.program_id(2)
        p = _mx(a_ref[...], b_ref[...])
        acc = jnp.where(k == 0, p, acc_ref[...] + p)
        acc_ref[...] = acc

        @pl.when(k == nk - 1)
        def _():
            o_ref[...] = acc.astype(o_ref.dtype)

    return pl.pallas_call(
        body, name=name, grid=(m // tm, n // tn, nk),
        in_specs=[pl.BlockSpec((tm, tk), lambda i, j, k: (i, k)), pl.BlockSpec((tk, tn), lambda i, j, k: (k, j))],
        out_specs=pl.BlockSpec((tm, tn), lambda i, j, k: (i, j)),
        out_shape=jax.ShapeDtypeStruct((m, n), MXU_DTYPE), scratch_shapes=[pltpu.VMEM((tm, tn), F32)],
        compiler_params=_cp("parallel", "parallel", "arbitrary"))(at, b)


def rms_fwd(x, g, name, tm=512, after=None):
    t, d = x.shape

    def body(x_ref, g_ref, *rest):
        o_ref, ot_ref = rest[-2:]
        xv = x_ref[...]
        r = lax.rsqrt(jnp.mean(xv * xv, axis=-1, keepdims=True) + EPS)
        h = xv * r * g_ref[...]
        o_ref[...] = h.astype(o_ref.dtype)
        ot_ref[...] = h.T.astype(ot_ref.dtype)

    in_specs = [pl.BlockSpec((tm, d), lambda i: (i, 0)), pl.BlockSpec((1, d), lambda i: (0, 0))]
    args = [x, g]
    if after is not None:
        in_specs.append(pl.BlockSpec(memory_space=pl.ANY))
        args.append(after)
    return pl.pallas_call(
        body, name=name, grid=(t // tm,), in_specs=in_specs,
        out_specs=(pl.BlockSpec((tm, d), lambda i: (i, 0)), pl.BlockSpec((d, tm), lambda i: (0, i))),
        out_shape=(jax.ShapeDtypeStruct((t, d), MXU_DTYPE), jax.ShapeDtypeStruct((d, t), MXU_DTYPE)),
        compiler_params=_cp("parallel"))(*args)


def rms_bwd(x, g, dh, dres, name, tm=512):
    t, d = x.shape

    def body(x_ref, g_ref, dh_ref, dres_ref, dx_ref, dg_ref):
        i = pl.program_id(0)
        xv = x_ref[...]
        r = lax.rsqrt(jnp.mean(xv * xv, axis=-1, keepdims=True) + EPS)
        xh = xv * r
        dhv = dh_ref[...]
        dxh = dhv * g_ref[...]
        dx_ref[...] = dres_ref[...] + r * (dxh - xh * jnp.mean(dxh * xh, axis=-1, keepdims=True))
        part = jnp.sum(dhv * xh, axis=0, keepdims=True)

        @pl.when(i == 0)
        def _():
            dg_ref[...] = part

        @pl.when(i > 0)
        def _():
            dg_ref[...] += part

    row = pl.BlockSpec((tm, d), lambda i: (i, 0))
    one = pl.BlockSpec((1, d), lambda i: (0, 0))
    return pl.pallas_call(
        body, name=name, grid=(t // tm,), in_specs=[row, one, row, row], out_specs=(row, one),
        out_shape=(jax.ShapeDtypeStruct((t, d), F32), jax.ShapeDtypeStruct((1, d), F32)),
        compiler_params=_cp("arbitrary"))(x, g, dh, dres)


GU_TILE = 1408


_GU_PER_TILE = GU_TILE * N_DEV // (2 * D_FF)


def _gu_gathered(g):
    _, k, c = g.shape
    nj = N_DEV // (2 * _GU_PER_TILE)
    return g.reshape(2, nj, _GU_PER_TILE, k, c).transpose(3, 1, 0, 2, 4).reshape(k, N_DEV * c)


def _gu_pieces(dw):
    k, n8 = dw.shape
    nj = N_DEV // (2 * _GU_PER_TILE)
    return dw.reshape(k, nj, 2, _GU_PER_TILE, n8 // N_DEV).transpose(2, 1, 3, 0, 4).reshape(N_DEV, k, n8 // N_DEV)


def ffn_up(f, w, name, tm=512):
    t, d = f.shape

    def body(f_ref, w_ref, gu_ref, a_ref, at_ref):
        gu = _mx(f_ref[...], w_ref[...])
        gu_ref[...] = gu
        g, u = gu[:, :GU_TILE], gu[:, GU_TILE:]
        act = g * _sigmoid(g) * u
        a_ref[...] = act.astype(a_ref.dtype)
        at_ref[...] = act.T.astype(at_ref.dtype)

    return pl.pallas_call(
        body, name=name, grid=(D_FF // GU_TILE, t // tm),
        in_specs=[pl.BlockSpec((tm, d), lambda j, i: (i, 0)), pl.BlockSpec((d, 2 * GU_TILE), lambda j, i: (0, j))],
        out_specs=(pl.BlockSpec((tm, 2 * GU_TILE), lambda j, i: (i, j)), pl.BlockSpec((tm, GU_TILE), lambda j, i: (i, j)),
                   pl.BlockSpec((GU_TILE, tm), lambda j, i: (j, i))),
        out_shape=(jax.ShapeDtypeStruct((t, 2 * D_FF), F32), jax.ShapeDtypeStruct((t, D_FF), MXU_DTYPE),
                   jax.ShapeDtypeStruct((D_FF, t), MXU_DTYPE)),
        compiler_params=_cp("parallel", "parallel"))(f, w)


def ffn_dact(dy, w_d, gu, name, tm=512, after=None):
    t, d = dy.shape

    def body(dy_ref, w_ref, gu_ref, *rest):
        o_ref = rest[-1]
        da = _mx_nt(dy_ref[...], w_ref[...])
        g, u = gu_ref[:, :GU_TILE], gu_ref[:, GU_TILE:]
        sg = _sigmoid(g)
        o_ref[:, :GU_TILE] = (da * u * sg * (1.0 + g * (1.0 - sg))).astype(o_ref.dtype)
        o_ref[:, GU_TILE:] = (da * g * sg).astype(o_ref.dtype)

    in_specs = [pl.BlockSpec((tm, d), lambda j, i: (i, 0)), pl.BlockSpec((GU_TILE, d), lambda j, i: (j, 0)),
                pl.BlockSpec((tm, 2 * GU_TILE), lambda j, i: (i, j))]
    args = [dy, w_d, gu]
    if after is not None:
        in_specs.append(pl.BlockSpec(memory_space=pl.ANY))
        args.append(after)
    return pl.pallas_call(
        body, name=name, grid=(D_FF // GU_TILE, t // tm), in_specs=in_specs,
        out_specs=pl.BlockSpec((tm, 2 * GU_TILE), lambda j, i: (i, j)),
        out_shape=jax.ShapeDtypeStruct((t, 2 * D_FF), MXU_DTYPE), compiler_params=_cp("parallel", "parallel"))(*args)


def loss_head(y, target, name, tm=512):
    t, d = y.shape

    def body(y_ref, t_ref, dy_ref, l_ref):
        i = pl.program_id(0)
        e = y_ref[...] - t_ref[...]
        dy_ref[...] = e * (1.0 / d)
        part = jnp.zeros((1, 128), F32) + 0.5 * jnp.sum(jnp.mean(e * e, axis=-1, keepdims=True), axis=0, keepdims=True)

        @pl.when(i == 0)
        def _():
            l_ref[...] = part

        @pl.when(i > 0)
        def _():
            l_ref[...] += part

    row = pl.BlockSpec((tm, d), lambda i: (i, 0))
    return pl.pallas_call(
        body, name=name, grid=(t // tm,), in_specs=[row, row],
        out_specs=(row, pl.BlockSpec((1, 128), lambda i: (0, 0))),
        out_shape=(jax.ShapeDtypeStruct((t, d), F32), jax.ShapeDtypeStruct((1, 128), F32)),
        compiler_params=_cp("arbitrary"))(y, target)


QK_W = Q_W + KV_W
_QK_TILE = 256


def _qk_mats():
    idx = np.arange(_QK_TILE)
    half = HEAD_DIM // 2
    same = (idx[:, None] // HEAD_DIM) == (idx[None, :] // HEAD_DIM)
    lo = (idx % HEAD_DIM) < half
    rot = np.where((idx[:, None] == idx[None, :] + half) & lo[None, :], -1.0, 0.0)
    rot = rot + np.where((idx[:, None] == idx[None, :] - half) & ~lo[None, :], 1.0, 0.0)
    return jnp.asarray(same, F32), jnp.asarray(rot, F32)


def _qk_rows(q_gain, k_gain, cosf, sinf):
    gain = jnp.concatenate([q_gain] * ATTN_HEADS + [k_gain] * KV_HEADS, axis=-1)
    return gain, jnp.concatenate([cosf, cosf], axis=-1), jnp.concatenate([sinf, sinf], axis=-1)


def _qk_tiles(a, mat, transposed=False):
    outs = []
    for c0 in range(0, QK_W, _QK_TILE):
        w = min(_QK_TILE, QK_W - c0)
        mt = (mat.T if transposed else mat)[:w, :w].astype(MXU_DTYPE)
        at = a[:, c0:c0 + w]
        hi = at.astype(MXU_DTYPE)
        lo = (at - hi.astype(F32)).astype(MXU_DTYPE)
        outs.append(jnp.dot(hi, mt, preferred_element_type=F32) + jnp.dot(lo, mt, preferred_element_type=F32))
    return jnp.concatenate(outs, axis=-1)


def qk_prep_fwd(proj, q_gain, k_gain, cosf, sinf, name, tm=256):
    t = proj.shape[0]
    gmat, rmat = _qk_mats()
    gain, c2, s2 = _qk_rows(q_gain, k_gain, cosf, sinf)
    rep = QK_W // 128

    def body(p_ref, g_ref, c_ref, s_ref, gm_ref, rm_ref, q_ref, k_ref):
        x = p_ref[...]
        r = lax.rsqrt(_qk_tiles(x * x, gm_ref[...]) * (1.0 / HEAD_DIM) + EPS)
        xn = x * r * g_ref[...]
        c = jnp.concatenate([c_ref[...]] * rep, axis=-1)
        s = jnp.concatenate([s_ref[...]] * rep, axis=-1)
        out = xn * c + _qk_tiles(xn, rm_ref[...]) * s
        q_ref[...] = out[:, :Q_W]
        k_ref[...] = out[:, Q_W:]

    full = pl.BlockSpec((_QK_TILE, _QK_TILE), lambda i: (0, 0))
    tab = pl.BlockSpec((tm, 128), lambda i: (i, 0))
    return pl.pallas_call(
        body, name=name, grid=(t // tm,),
        in_specs=[pl.BlockSpec((tm, QK_W), lambda i: (i, 0)), pl.BlockSpec((1, QK_W), lambda i: (0, 0)), tab, tab,
                  full, full],
        out_specs=(pl.BlockSpec((tm, Q_W), lambda i: (i, 0)), pl.BlockSpec((tm, KV_W), lambda i: (i, 0))),
        out_shape=(jax.ShapeDtypeStruct((t, Q_W), F32), jax.ShapeDtypeStruct((t, KV_W), F32)),
        compiler_params=_cp("parallel"))(proj, gain, c2, s2, gmat, rmat)


def qk_prep_bwd(proj, q_gain, k_gain, cosf, sinf, dq, dk, name, tm=256):
    t = proj.shape[0]
    gmat, rmat = _qk_mats()
    gain, c2, s2 = _qk_rows(q_gain, k_gain, cosf, sinf)
    rep = QK_W // 128
    lanes = np.arange(QK_W)[:, None]
    fold = jnp.asarray(lanes % HEAD_DIM + np.where(lanes >= Q_W, HEAD_DIM, 0) == np.arange(128)[None, :], F32)

    def body(p_ref, g_ref, c_ref, s_ref, gm_ref, rm_ref, f_ref, dq_ref, dk_ref, o_ref, dg_ref):
        x = p_ref[...]
        r = lax.rsqrt(_qk_tiles(x * x, gm_ref[...]) * (1.0 / HEAD_DIM) + EPS)
        xh = x * r
        c = jnp.concatenate([c_ref[...]] * rep, axis=-1)
        s = jnp.concatenate([s_ref[...]] * rep, axis=-1)
        dout = jnp.concatenate([dq_ref[...], dk_ref[...]], axis=-1)
        dxn = dout * c + _qk_tiles(dout * s, rm_ref[...], transposed=True)
        part = _hi(jnp.sum(dxn * xh, axis=0, keepdims=True), f_ref[...])
        dxh = dxn * g_ref[...]
        mean = _qk_tiles(dxh * xh, gm_ref[...]) * (1.0 / HEAD_DIM)
        o_ref[...] = (r * (dxh - xh * mean)).astype(o_ref.dtype)
        dg_ref[...] = jnp.where(pl.program_id(0) == 0, part, dg_ref[...] + part)

    full = pl.BlockSpec((_QK_TILE, _QK_TILE), lambda i: (0, 0))
    tab = pl.BlockSpec((tm, 128), lambda i: (i, 0))
    dqk, dg = pl.pallas_call(
        body, name=name, grid=(t // tm,),
        in_specs=[pl.BlockSpec((tm, QK_W), lambda i: (i, 0)), pl.BlockSpec((1, QK_W), lambda i: (0, 0)), tab, tab,
                  full, full, pl.BlockSpec((QK_W, 128), lambda i: (0, 0)),
                  pl.BlockSpec((tm, Q_W), lambda i: (i, 0)), pl.BlockSpec((tm, KV_W), lambda i: (i, 0))],
        out_specs=(pl.BlockSpec((tm, QK_W), lambda i: (i, 0)), pl.BlockSpec((1, 128), lambda i: (0, 0))),
        out_shape=(jax.ShapeDtypeStruct((t, QK_W), MXU_DTYPE), jax.ShapeDtypeStruct((1, 128), F32)),
        compiler_params=_cp("arbitrary"))(proj, gain, c2, s2, gmat, rmat, fold, dq, dk)
    return dqk, dg[:, :HEAD_DIM], dg[:, HEAD_DIM:]


def _swa_valid(n, grp):
    qi = lax.broadcasted_iota(jnp.int32, (grp * ATTN_BLOCK, 2 * ATTN_BLOCK), 0) & (ATTN_BLOCK - 1)
    kj = lax.broadcasted_iota(jnp.int32, (grp * ATTN_BLOCK, 2 * ATTN_BLOCK), 1)
    diff = qi + ATTN_BLOCK - kj
    return (diff >= 0) & (diff < ATTN_BLOCK) & (n * ATTN_BLOCK - ATTN_BLOCK + kj >= 0)


def _stack_heads(ref, g, grp):
    return jnp.concatenate([ref[:, (g * grp + j) * HEAD_DIM:(g * grp + j + 1) * HEAD_DIM] for j in range(grp)], axis=0)


def _stack_sinks(s_ref, g, grp):
    return jnp.concatenate([jnp.zeros((ATTN_BLOCK, 1), F32) + s_ref[0:1, g * grp + j:g * grp + j + 1]
                            for j in range(grp)], axis=0)


def swa_fwd(q, k, proj, sinks, name):
    t = q.shape[0]
    nb = t // ATTN_BLOCK
    scale = HEAD_DIM ** -0.5
    grp = ATTN_HEADS // KV_HEADS

    def body(q_ref, kc_ref, kp_ref, vc_ref, vp_ref, s_ref, y_ref, yt_ref, lse_ref):
        n = pl.program_id(0)
        valid = _swa_valid(n, grp)
        kk = jnp.concatenate([kp_ref[...], kc_ref[...]], axis=0).astype(MXU_DTYPE)
        vv = jnp.concatenate([vp_ref[...], vc_ref[...]], axis=0).astype(MXU_DTYPE)
        lane = lax.broadcasted_iota(jnp.int32, (ATTN_BLOCK, ATTN_HEADS), 1)
        gs = range(KV_HEADS)
        qg = [_stack_heads(q_ref, g, grp) for g in gs]
        sink = [_stack_sinks(s_ref, g, grp) for g in gs]
        sc = [jnp.where(valid, _mx_nt(qg[g], kk[:, g * HEAD_DIM:(g + 1) * HEAD_DIM]) * scale, NEG) for g in gs]
        m = [jnp.maximum(jnp.max(sc[g], axis=-1, keepdims=True), sink[g]) for g in gs]
        e = [jnp.exp(sc[g] - m[g]) for g in gs]
        den = [jnp.sum(e[g], axis=-1, keepdims=True) + jnp.exp(sink[g] - m[g]) for g in gs]
        og = [_mx(e[g] / den[g], vv[:, g * HEAD_DIM:(g + 1) * HEAD_DIM]) for g in gs]
        lg = [m[g] + jnp.log(den[g]) for g in gs]
        lse = jnp.zeros((ATTN_BLOCK, ATTN_HEADS), F32)
        outs = []
        for h in range(ATTN_HEADS):
            rows = slice((h % grp) * ATTN_BLOCK, (h % grp + 1) * ATTN_BLOCK)
            outs.append(og[h // grp][rows])
            lse = jnp.where(lane == h, lg[h // grp][rows], lse)
        y = jnp.concatenate(outs, axis=-1)
        y_ref[...] = y
        yt_ref[...] = y.T.astype(yt_ref.dtype)
        lse_ref[...] = lse

    cur = lambda n: (n, 0)
    prev = lambda n: (jnp.maximum(n - 1, 0), 0)
    vcol = (Q_W + KV_W) // KV_W
    return pl.pallas_call(
        body, name=name, grid=(nb,),
        in_specs=[pl.BlockSpec((ATTN_BLOCK, Q_W), cur), pl.BlockSpec((ATTN_BLOCK, KV_W), cur),
                  pl.BlockSpec((ATTN_BLOCK, KV_W), prev),
                  pl.BlockSpec((ATTN_BLOCK, KV_W), lambda n: (n, vcol)),
                  pl.BlockSpec((ATTN_BLOCK, KV_W), lambda n: (jnp.maximum(n - 1, 0), vcol)),
                  pl.BlockSpec((1, ATTN_HEADS), lambda n: (0, 0))],
        out_specs=(pl.BlockSpec((ATTN_BLOCK, Q_W), cur), pl.BlockSpec((Q_W, ATTN_BLOCK), lambda n: (0, n)),
                   pl.BlockSpec((ATTN_BLOCK, ATTN_HEADS), cur)),
        out_shape=(jax.ShapeDtypeStruct((t, Q_W), F32), jax.ShapeDtypeStruct((Q_W, t), MXU_DTYPE),
                   jax.ShapeDtypeStruct((t, ATTN_HEADS), F32)),
        compiler_params=_cp("parallel"))(q, k, k, proj, proj, sinks)


def swa_bwd(q, k, proj, sinks, y, lse, dmix, name):
    t = q.shape[0]
    nb = t // ATTN_BLOCK
    scale = HEAD_DIM ** -0.5
    grp = ATTN_HEADS // KV_HEADS

    def body(q_ref, kc_ref, kp_ref, vc_ref, vp_ref, s_ref, y_ref, lse_ref, dy_ref,
             dq_ref, dk_ref, dv_ref, ds_ref, dkc, dvc):
        n = pl.program_id(0)

        @pl.when(n == 0)
        def _():
            dkc[...] = jnp.zeros_like(dkc)
            dvc[...] = jnp.zeros_like(dvc)
            ds_ref[...] = jnp.zeros_like(ds_ref)

        @pl.when(n < nb)
        def _():
            valid = _swa_valid(n, grp)
            kk = jnp.concatenate([kp_ref[...], kc_ref[...]], axis=0).astype(MXU_DTYPE)
            vv = jnp.concatenate([vp_ref[...], vc_ref[...]], axis=0).astype(MXU_DTYPE)
            lane = lax.broadcasted_iota(jnp.int32, (1, ATTN_HEADS), 1)
            gs = range(KV_HEADS)
            kg = [kk[:, g * HEAD_DIM:(g + 1) * HEAD_DIM] for g in gs]
            vg = [vv[:, g * HEAD_DIM:(g + 1) * HEAD_DIM] for g in gs]
            qg = [_stack_heads(q_ref, g, grp).astype(MXU_DTYPE) for g in gs]
            dog = [_stack_heads(dy_ref, g, grp) for g in gs]
            og = [_stack_heads(y_ref, g, grp) for g in gs]
            lg = [jnp.concatenate([lse_ref[:, g * grp + j:g * grp + j + 1] for j in range(grp)], axis=0) for g in gs]
            sink = [_stack_sinks(s_ref, g, grp) for g in gs]
            sc = [jnp.where(valid, _mx_nt(qg[g], kg[g]) * scale, NEG) for g in gs]
            p = [jnp.exp(sc[g] - lg[g]) for g in gs]
            delta = [jnp.sum(dog[g] * og[g], axis=-1, keepdims=True) for g in gs]
            ds = [p[g] * (_mx_nt(dog[g], vg[g]) - delta[g]) for g in gs]
            dqg = [_mx(ds[g], kg[g]) * scale for g in gs]
            dkf = jnp.concatenate([_mx_tn(ds[g], qg[g]) * scale for g in gs], axis=-1)
            dvf = jnp.concatenate([_mx_tn(p[g], dog[g]) for g in gs], axis=-1)
            dsk = [jnp.exp(sink[g] - lg[g]) * delta[g] for g in gs]
            dsink = jnp.zeros((1, ATTN_HEADS), F32)
            dqs = []
            for h in range(ATTN_HEADS):
                rows = slice((h % grp) * ATTN_BLOCK, (h % grp + 1) * ATTN_BLOCK)
                dqs.append(dqg[h // grp][rows])
                dsink = jnp.where(lane == h, -jnp.sum(dsk[h // grp][rows], axis=0, keepdims=True), dsink)
            dq_ref[...] = jnp.concatenate(dqs, axis=-1)
            dk_ref[...] = dkc[...] + dkf[:ATTN_BLOCK]
            dv_ref[...] = (dvc[...] + dvf[:ATTN_BLOCK]).astype(dv_ref.dtype)
            dkc[...] = dkf[ATTN_BLOCK:]
            dvc[...] = dvf[ATTN_BLOCK:]
            ds_ref[...] += dsink

        @pl.when(n == nb)
        def _():
            dk_ref[...] = dkc[...]
            dv_ref[...] = dvc[...].astype(dv_ref.dtype)

    cur = lambda n: (jnp.minimum(n, nb - 1), 0)
    prev = lambda n: (jnp.clip(n - 1, 0, nb - 1), 0)
    vcol = (Q_W + KV_W) // KV_W
    return pl.pallas_call(
        body, name=name, grid=(nb + 1,),
        in_specs=[pl.BlockSpec((ATTN_BLOCK, Q_W), cur), pl.BlockSpec((ATTN_BLOCK, KV_W), cur),
                  pl.BlockSpec((ATTN_BLOCK, KV_W), prev),
                  pl.BlockSpec((ATTN_BLOCK, KV_W), lambda n: (jnp.minimum(n, nb - 1), vcol)),
                  pl.BlockSpec((ATTN_BLOCK, KV_W), lambda n: (jnp.clip(n - 1, 0, nb - 1), vcol)),
                  pl.BlockSpec((1, ATTN_HEADS), lambda n: (0, 0)),
                  pl.BlockSpec((ATTN_BLOCK, Q_W), cur), pl.BlockSpec((ATTN_BLOCK, ATTN_HEADS), cur),
                  pl.BlockSpec((ATTN_BLOCK, Q_W), cur)],
        out_specs=(pl.BlockSpec((ATTN_BLOCK, Q_W), cur), pl.BlockSpec((ATTN_BLOCK, KV_W), prev),
                   pl.BlockSpec((ATTN_BLOCK, KV_W), prev), pl.BlockSpec((1, ATTN_HEADS), lambda n: (0, 0))),
        out_shape=(jax.ShapeDtypeStruct((t, Q_W), F32), jax.ShapeDtypeStruct((t, KV_W), F32),
                   jax.ShapeDtypeStruct((t, KV_W), MXU_DTYPE), jax.ShapeDtypeStruct((1, ATTN_HEADS), F32)),
        scratch_shapes=[pltpu.VMEM((ATTN_BLOCK, KV_W), F32), pltpu.VMEM((ATTN_BLOCK, KV_W), F32)],
        compiler_params=_cp("arbitrary"))(q, k, k, proj, proj, sinks, y, lse, dmix)


GC_W = 256
_GB0, _GC0, _XI0 = 768 // GC_W, 1280 // GC_W, 1792 // GC_W
HALO = 8


def gconv_fwd(proj, conv_w, name, tm=512):
    t = proj.shape[0]
    hb = tm // HALO

    def body(gb_ref, gc_ref, xi_ref, gch_ref, xih_ref, w_ref, y_ref, yt_ref):
        i = pl.program_id(1)
        u = gc_ref[...] * xi_ref[...]
        uh = jnp.where(i == 0, 0.0, gch_ref[...] * xih_ref[...])
        up = jnp.concatenate([uh, u], axis=0)
        cv = w_ref[0:1, :] * up[HALO - 2:HALO - 2 + tm]
        cv = cv + w_ref[1:2, :] * up[HALO - 1:HALO - 1 + tm]
        cv = cv + w_ref[2:3, :] * u
        y = gb_ref[...] * cv
        y_ref[...] = y.astype(y_ref.dtype)
        yt_ref[...] = y.T.astype(yt_ref.dtype)

    def col(c0):
        return pl.BlockSpec((tm, GC_W), lambda cj, i: (i, c0 + cj))

    def halo(c0):
        return pl.BlockSpec((HALO, GC_W), lambda cj, i: (jnp.maximum(i * hb - 1, 0), c0 + cj))

    return pl.pallas_call(
        body, name=name, grid=(CONV_CH // GC_W, t // tm),
        in_specs=[col(_GB0), col(_GC0), col(_XI0), halo(_GC0), halo(_XI0),
                  pl.BlockSpec((3, GC_W), lambda cj, i: (0, cj))],
        out_specs=(pl.BlockSpec((tm, GC_W), lambda cj, i: (i, cj)), pl.BlockSpec((GC_W, tm), lambda cj, i: (cj, i))),
        out_shape=(jax.ShapeDtypeStruct((t, CONV_CH), MXU_DTYPE), jax.ShapeDtypeStruct((CONV_CH, t), MXU_DTYPE)),
        compiler_params=_cp("parallel", "parallel"))(proj, proj, proj, proj, proj, conv_w)


def gconv_bwd(proj, conv_w, dmix, name, tm=512):
    t = proj.shape[0]
    hb = tm // HALO
    nt = t // tm
    dy0 = Q_W // GC_W

    def body(gb_ref, gc_ref, xi_ref, gch_ref, xih_ref, gbn_ref, dyn_ref, dy_ref, w_ref,
             dgb_ref, dgc_ref, dxi_ref, dw_ref):
        i = pl.program_id(1)
        gc, xi, gb, dy = gc_ref[...], xi_ref[...], gb_ref[...], dy_ref[...]
        u = gc * xi
        uh = jnp.where(i == 0, 0.0, gch_ref[...] * xih_ref[...])
        up = jnp.concatenate([uh, u], axis=0)
        u2 = up[HALO - 2:HALO - 2 + tm]
        u1 = up[HALO - 1:HALO - 1 + tm]
        cv = w_ref[0:1, :] * u2 + w_ref[1:2, :] * u1 + w_ref[2:3, :] * u
        dgb_ref[...] = (dy * cv).astype(dgb_ref.dtype)
        dcv = dy * gb
        dcvn = jnp.where(i == nt - 1, 0.0, dyn_ref[...] * gbn_ref[...])
        dcvp = jnp.concatenate([dcv, dcvn], axis=0)
        du = w_ref[0:1, :] * dcvp[2:2 + tm] + w_ref[1:2, :] * dcvp[1:1 + tm] + w_ref[2:3, :] * dcv
        dgc_ref[...] = (du * xi).astype(dgc_ref.dtype)
        dxi_ref[...] = (du * gc).astype(dxi_ref.dtype)
        dw = jnp.concatenate([jnp.sum(dcv * u2, axis=0, keepdims=True), jnp.sum(dcv * u1, axis=0, keepdims=True),
                              jnp.sum(dcv * u, axis=0, keepdims=True)], axis=0)

        @pl.when(i == 0)
        def _():
            dw_ref[...] = dw

        @pl.when(i > 0)
        def _():
            dw_ref[...] += dw

    def col(c0):
        return pl.BlockSpec((tm, GC_W), lambda cj, i: (i, c0 + cj))

    def halo(c0):
        return pl.BlockSpec((HALO, GC_W), lambda cj, i: (jnp.maximum(i * hb - 1, 0), c0 + cj))

    def nxt(c0):
        return pl.BlockSpec((HALO, GC_W), lambda cj, i: (jnp.minimum((i + 1) * hb, t // HALO - 1), c0 + cj))

    out = pl.BlockSpec((tm, GC_W), lambda cj, i: (i, cj))
    return pl.pallas_call(
        body, name=name, grid=(CONV_CH // GC_W, nt),
        in_specs=[col(_GB0), col(_GC0), col(_XI0), halo(_GC0), halo(_XI0), nxt(_GB0), nxt(dy0), col(dy0),
                  pl.BlockSpec((3, GC_W), lambda cj, i: (0, cj))],
        out_specs=(out, out, out, pl.BlockSpec((3, GC_W), lambda cj, i: (0, cj))),
        out_shape=(jax.ShapeDtypeStruct((t, CONV_CH), MXU_DTYPE),) * 3 + (jax.ShapeDtypeStruct((3, CONV_CH), F32),),
        compiler_params=_cp("parallel", "arbitrary"))(proj, proj, proj, proj, proj, proj, dmix, dmix, conv_w)


_QKV_W = 3 * DN_W
_BA_COL = (4 * DN_W) // 128
_Z_COL = _QKV_W // DN_W


def gdn_prep_fwd(proj, conv_w, alog_row, dtb_row, name, tm=256):
    t = proj.shape[0]
    hb = tm // HALO
    qscale = DN_DIM ** -0.5

    def body(x_ref, xh_ref, w_ref, ba_ref, al_ref, dt_ref, q_ref, k_ref, v_ref, bg_ref):
        i = pl.program_id(0)
        for gi in range(3 * DN_HEADS):
            sl = slice(gi * DN_DIM, (gi + 1) * DN_DIM)
            xp = jnp.concatenate([jnp.where(i == 0, 0.0, xh_ref[:, sl]), x_ref[:, sl]], axis=0)
            c = w_ref[0:1, sl] * xp[HALO - 3:HALO - 3 + tm]
            for j in range(1, 4):
                c = c + w_ref[j:j + 1, sl] * xp[HALO - 3 + j:HALO - 3 + j + tm]
            s = c * _sigmoid(c)
            osl = slice((gi % DN_HEADS) * DN_DIM, (gi % DN_HEADS + 1) * DN_DIM)
            if gi < DN_HEADS:
                q_ref[:, osl] = s * lax.rsqrt(jnp.sum(s * s, axis=-1, keepdims=True) + EPS) * qscale
            elif gi < 2 * DN_HEADS:
                k_ref[:, osl] = s * lax.rsqrt(jnp.sum(s * s, axis=-1, keepdims=True) + EPS)
            else:
                v_ref[:, osl] = s
        ba = ba_ref[...]
        lane = lax.broadcasted_iota(jnp.int32, ba.shape, 1)
        gval = -jnp.exp(al_ref[...]) * _softplus(ba + dt_ref[...])
        bg_ref[...] = jnp.where(lane < DN_HEADS, _sigmoid(ba), jnp.where(lane < 2 * DN_HEADS, gval, 0.0))

    row = pl.BlockSpec((tm, DN_W), lambda i: (i, 0))
    one = pl.BlockSpec((1, 128), lambda i: (0, 0))
    return pl.pallas_call(
        body, name=name, grid=(t // tm,),
        in_specs=[pl.BlockSpec((tm, _QKV_W), lambda i: (i, 0)),
                  pl.BlockSpec((HALO, _QKV_W), lambda i: (jnp.maximum(i * hb - 1, 0), 0)),
                  pl.BlockSpec((4, _QKV_W), lambda i: (0, 0)),
                  pl.BlockSpec((tm, 128), lambda i: (i, _BA_COL)), one, one],
        out_specs=(row, row, row, pl.BlockSpec((tm, 128), lambda i: (i, 0))),
        out_shape=(jax.ShapeDtypeStruct((t, DN_W), F32),) * 3 + (jax.ShapeDtypeStruct((t, 128), F32),),
        compiler_params=_cp("parallel"))(proj, proj, conv_w, proj, alog_row, dtb_row)


def gdn_prep_bwd(proj, conv_w, alog_row, dtb_row, dq, dk, dv, dbg, name, tm=256):
    t = proj.shape[0]
    hb = tm // HALO
    nt = t // tm
    qscale = DN_DIM ** -0.5
    te = tm + HALO

    def body(x_ref, xh_ref, xn_ref, w_ref, ba_ref, al_ref, dt_ref, dq_ref, dk_ref, dv_ref,
             dqn_ref, dkn_ref, dvn_ref, dbg_ref, dx_ref, dba_ref, dw_ref, ddt_ref, dal_ref):
        i = pl.program_id(0)
        first = i == 0
        last = i == nt - 1
        dws = []
        for gi in range(3 * DN_HEADS):
            sl = slice(gi * DN_DIM, (gi + 1) * DN_DIM)
            osl = slice((gi % DN_HEADS) * DN_DIM, (gi % DN_HEADS + 1) * DN_DIM)
            xe = jnp.concatenate([jnp.where(first, 0.0, xh_ref[:, sl]), x_ref[:, sl], xn_ref[:, sl]], axis=0)
            c = w_ref[0:1, sl] * xe[HALO - 3:HALO - 3 + te]
            for j in range(1, 4):
                c = c + w_ref[j:j + 1, sl] * xe[HALO - 3 + j:HALO - 3 + j + te]
            sg = _sigmoid(c)
            s = c * sg
            d_ref, dn_ref = ((dq_ref, dqn_ref), (dk_ref, dkn_ref), (dv_ref, dvn_ref))[gi // DN_HEADS]
            dy = jnp.concatenate([d_ref[:, osl], jnp.where(last, 0.0, dn_ref[:, osl])], axis=0)
            if gi < 2 * DN_HEADS:
                r = lax.rsqrt(jnp.sum(s * s, axis=-1, keepdims=True) + EPS)
                sh = s * r
                ds = r * (dy - sh * jnp.sum(sh * dy, axis=-1, keepdims=True))
                if gi < DN_HEADS:
                    ds = ds * qscale
            else:
                ds = dy
            dc = ds * sg * (1.0 + c * (1.0 - sg))
            dcs = [dc[3 - j:3 - j + tm] for j in range(4)]
            dx = w_ref[0:1, sl] * dcs[0]
            for j in range(1, 4):
                dx = dx + w_ref[j:j + 1, sl] * dcs[j]
            dx_ref[:, sl] = dx.astype(dx_ref.dtype)
            x0 = x_ref[:, sl]
            dws.append(jnp.concatenate([jnp.sum(dcs[j] * x0, axis=0, keepdims=True) for j in range(4)], axis=0))
        dw = jnp.concatenate(dws, axis=-1)
        ba = ba_ref[...]
        dbgv = dbg_ref[...]
        lane = lax.broadcasted_iota(jnp.int32, ba.shape, 1)
        beta = _sigmoid(ba)
        ea = -jnp.exp(al_ref[...])
        zin = ba + dt_ref[...]
        is_b = lane < DN_HEADS
        is_a = (lane >= DN_HEADS) & (lane < 2 * DN_HEADS)
        da = jnp.where(is_a, dbgv * ea * _sigmoid(zin), 0.0)
        dba_ref[...] = jnp.where(is_b, dbgv * beta * (1.0 - beta), da).astype(dba_ref.dtype)
        ddt = jnp.sum(da, axis=0, keepdims=True)
        dal = jnp.sum(jnp.where(is_a, dbgv * ea * _softplus(zin), 0.0), axis=0, keepdims=True)

        @pl.when(first)
        def _():
            dw_ref[...] = dw
            ddt_ref[...] = ddt
            dal_ref[...] = dal

        @pl.when(i > 0)
        def _():
            dw_ref[...] += dw
            ddt_ref[...] += ddt
            dal_ref[...] += dal

    row = pl.BlockSpec((tm, DN_W), lambda i: (i, 0))
    nrow = pl.BlockSpec((HALO, DN_W), lambda i: (jnp.minimum((i + 1) * hb, t // HALO - 1), 0))
    one = pl.BlockSpec((1, 128), lambda i: (0, 0))
    return pl.pallas_call(
        body, name=name, grid=(nt,),
        in_specs=[pl.BlockSpec((tm, _QKV_W), lambda i: (i, 0)),
                  pl.BlockSpec((HALO, _QKV_W), lambda i: (jnp.maximum(i * hb - 1, 0), 0)),
                  pl.BlockSpec((HALO, _QKV_W), lambda i: (jnp.minimum((i + 1) * hb, t // HALO - 1), 0)),
                  pl.BlockSpec((4, _QKV_W), lambda i: (0, 0)),
                  pl.BlockSpec((tm, 128), lambda i: (i, _BA_COL)), one, one,
                  row, row, row, nrow, nrow, nrow, pl.BlockSpec((tm, 128), lambda i: (i, 0))],
        out_specs=(pl.BlockSpec((tm, _QKV_W), lambda i: (i, 0)), pl.BlockSpec((tm, 128), lambda i: (i, 0)),
                   pl.BlockSpec((4, _QKV_W), lambda i: (0, 0)), one, one),
        out_shape=(jax.ShapeDtypeStruct((t, _QKV_W), MXU_DTYPE), jax.ShapeDtypeStruct((t, 128), MXU_DTYPE),
                   jax.ShapeDtypeStruct((4, _QKV_W), F32), jax.ShapeDtypeStruct((1, 128), F32),
                   jax.ShapeDtypeStruct((1, 128), F32)),
        compiler_params=_cp("arbitrary"))(proj, proj, proj, conv_w, proj, alog_row, dtb_row, dq, dk, dv, dq, dk, dv, dbg)


def _chunk_masks():
    r = lax.broadcasted_iota(jnp.int32, (DN_CHUNK, DN_CHUNK), 0)
    c = lax.broadcasted_iota(jnp.int32, (DN_CHUNK, DN_CHUNK), 1)
    return r >= c, r > c


INV_PACK = 2


def _inv_unit_lower_many(mats):
    n = DN_CHUNK
    wide = INV_PACK * n
    r = lax.broadcasted_iota(jnp.int32, (wide, wide), 0)
    c = lax.broadcasted_iota(jnp.int32, (wide, wide), 1)
    same = (r // n) == (c // n)
    eye = jnp.where((r[:n] == (c[:n] % n)), 1.0, 0.0)

    def blockdiag(row):
        return jnp.where(same, jnp.concatenate([row] * INV_PACK, axis=0), 0.0)

    packs = [jnp.concatenate(mats[g:g + INV_PACK], axis=-1) for g in range(0, len(mats), INV_PACK)]
    xs = [eye - a for a in packs]
    pws = [_hi(a, blockdiag(a)) for a in packs]
    for step in range(5):
        if step < 4:
            both = [_hi(jnp.concatenate([x, pw], axis=0), blockdiag(pw)) for x, pw in zip(xs, pws)]
            xs = [x + b[:n] for x, b in zip(xs, both)]
            pws = [b[n:] for b in both]
        else:
            xs = [x + _hi(x, blockdiag(pw)) for x, pw in zip(xs, pws)]
    return [x[:, j * n:(j + 1) * n] for x in xs for j in range(INV_PACK)]


def _chunk_common(q, k, v, beta, gc, gcr, lower, strict):
    gam = jnp.exp(jnp.where(lower, gc - gcr, NEG))
    eg = jnp.exp(gc)
    gl = gc[DN_CHUNK - 1:DN_CHUNK, :]
    kdf = jnp.exp(gl - gc)
    kb = k * beta
    bmat = _mx_nt(kb, k)
    qmat = _mx_nt(q, k)
    return gam, eg, jnp.exp(gl), kdf, kb, bmat, qmat


def gdn_fwd(q, k, v, bg, name):
    t = q.shape[0]
    n_chunks = t // DN_CHUNK

    def body(q_ref, k_ref, v_ref, bg_ref, o_ref, sall_ref, tall_ref, s_ref):
        n = pl.program_id(0)

        @pl.when(n == 0)
        def _():
            s_ref[...] = jnp.zeros_like(s_ref)

        lower, strict = _chunk_masks()
        bgv = bg_ref[...]
        gcs = _hi(jnp.where(lower, 1.0, 0.0), bgv)
        gcs_t = gcs.T
        hs = range(DN_HEADS)
        sl = [slice(h * DN_DIM, (h + 1) * DN_DIM) for h in hs]
        st = [s_ref[h] for h in hs]
        for h in hs:
            sall_ref[0, h] = st[h]
        qh = [q_ref[:, sl[h]] for h in hs]
        kh = [k_ref[:, sl[h]] for h in hs]
        vh = [v_ref[:, sl[h]] for h in hs]
        beta = [bgv[:, h:h + 1] for h in hs]
        com = [_chunk_common(qh[h], kh[h], vh[h], beta[h], gcs[:, DN_HEADS + h:DN_HEADS + h + 1],
                             gcs_t[DN_HEADS + h:DN_HEADS + h + 1, :], lower, strict) for h in hs]
        gam, eg, dec, kdf, kb, bmat, qmat = zip(*com)
        tms = _inv_unit_lower_many([jnp.where(strict, bmat[h] * gam[h], 0.0) for h in hs])
        for h in hs:
            tall_ref[0, h] = tms[h]
        uw = [_hi(tms[h], jnp.concatenate([vh[h] * beta[h], kb[h] * eg[h]], axis=-1)) for h in hs]
        v_new = [uw[h][:, :DN_DIM] - _mx(uw[h][:, DN_DIM:], st[h]) for h in hs]
        o_st = [_mx(qh[h] * eg[h], st[h]) for h in hs]
        o_in = [_mx(qmat[h] * gam[h], v_new[h]) for h in hs]
        s_up = [_mx_tn(kh[h] * kdf[h], v_new[h]) for h in hs]
        for h in hs:
            o_ref[:, sl[h]] = o_st[h] + o_in[h]
            s_ref[h] = st[h] * dec[h] + s_up[h]

    row = pl.BlockSpec((DN_CHUNK, DN_W), lambda n: (n, 0))
    return pl.pallas_call(
        body, name=name, grid=(n_chunks,),
        in_specs=[row, row, row, pl.BlockSpec((DN_CHUNK, 128), lambda n: (n, 0))],
        out_specs=(row, pl.BlockSpec((1, DN_HEADS, DN_DIM, DN_DIM), lambda n: (n, 0, 0, 0)),
                   pl.BlockSpec((1, DN_HEADS, DN_CHUNK, DN_CHUNK), lambda n: (n, 0, 0, 0))),
        out_shape=(jax.ShapeDtypeStruct((t, DN_W), F32),
                   jax.ShapeDtypeStruct((n_chunks, DN_HEADS, DN_DIM, DN_DIM), F32),
                   jax.ShapeDtypeStruct((n_chunks, DN_HEADS, DN_CHUNK, DN_CHUNK), F32)),
        scratch_shapes=[pltpu.VMEM((DN_HEADS, DN_DIM, DN_DIM), F32)],
        compiler_params=_cp("arbitrary"))(q, k, v, bg)


def gdn_bwd(q, k, v, bg, sall, tall, do, name):
    t = q.shape[0]
    n_chunks = t // DN_CHUNK

    def body(q_ref, k_ref, v_ref, bg_ref, sall_ref, tall_ref, do_ref, dq_ref, dk_ref, dv_ref, dbg_ref, ds_ref):
        n = pl.program_id(0)

        @pl.when(n == 0)
        def _():
            ds_ref[...] = jnp.zeros_like(ds_ref)

        lower, strict = _chunk_masks()
        ltri = jnp.where(lower, 1.0, 0.0)
        bgv = bg_ref[...]
        gcs = _hi(ltri, bgv)
        gcs_t = gcs.T
        lane = lax.broadcasted_iota(jnp.int32, (DN_CHUNK, 128), 1)
        rowi = lax.broadcasted_iota(jnp.int32, (DN_CHUNK, 1), 0)
        hs = range(DN_HEADS)
        each = lambda fn, *ls: [fn(*a) for a in zip(*ls)]
        rsum = lambda a: jnp.sum(a, axis=-1, keepdims=True)
        sl = [slice(h * DN_DIM, (h + 1) * DN_DIM) for h in hs]
        st = [sall_ref[0, h] for h in hs]
        tms = [tall_ref[0, h] for h in hs]
        dsn = [ds_ref[h] for h in hs]
        qh = [q_ref[:, sl[h]] for h in hs]
        kh = [k_ref[:, sl[h]] for h in hs]
        vh = [v_ref[:, sl[h]] for h in hs]
        doh = [do_ref[:, sl[h]] for h in hs]
        beta = [bgv[:, h:h + 1] for h in hs]
        com = [_chunk_common(qh[h], kh[h], vh[h], beta[h], gcs[:, DN_HEADS + h:DN_HEADS + h + 1],
                             gcs_t[DN_HEADS + h:DN_HEADS + h + 1, :], lower, strict) for h in hs]
        gam, eg, dec, kdf, kb, bmat, qmat = zip(*com)
        rhs_w = each(lambda a, b: a * b, kb, eg)
        uw = each(lambda t_, v_, b_, r_: _hi(t_, jnp.concatenate([v_ * b_, r_], axis=-1)), tms, vh, beta, rhs_w)
        qd = each(lambda a, b: a * b, qh, eg)
        kd = each(lambda a, b: a * b, kh, kdf)
        pmat = each(lambda a, b: a * b, qmat, gam)
        v_new = each(lambda uw_, s_: uw_[:, :DN_DIM] - _mx(uw_[:, DN_DIM:], s_), uw, st)
        dqd = each(_mx_nt, doh, st)
        ds_o = each(_mx_tn, qd, doh)
        dp = each(lambda d_, v_: jnp.where(lower, _mx_nt(d_, v_), 0.0), doh, v_new)
        dvn_o = each(_mx_tn, pmat, doh)
        ddec = each(lambda d_, s_: jnp.sum(rsum(d_ * s_), axis=0, keepdims=True), dsn, st)
        dkd = each(_mx_nt, v_new, dsn)
        dvn = each(lambda a, k_, d_: a + _mx(k_, d_), dvn_o, kd, dsn)
        dw = each(lambda d_, s_: -_mx_nt(d_, s_), dvn, st)
        ds_w = each(lambda uw_, d_: _mx_tn(uw_[:, DN_DIM:], d_), uw, dvn)
        for h in hs:
            ds_ref[h] = ds_o[h] + dec[h] * dsn[h] - ds_w[h]
        dr = each(lambda t_, a, b: _hi_tn(t_, jnp.concatenate([a, b], axis=-1)), tms, dvn, dw)
        da = each(lambda r_, uw_: jnp.where(strict, -_hi_nt(r_, uw_), 0.0), dr, uw)
        dru = [r_[:, :DN_DIM] for r_ in dr]
        drw = [r_[:, DN_DIM:] for r_ in dr]
        db = each(lambda a, b: a * b, da, gam)
        dq_m = each(lambda a, b: a * b, dp, gam)
        e = each(lambda a, bm, p_, qm, g_: (a * bm + p_ * qm) * g_, da, bmat, dp, qmat, gam)
        dkb = each(lambda b_, k_, r_, e_: _mx(b_, k_) + r_ * e_, db, kh, drw, eg)
        dk = each(lambda b_, kb_, m_, q_, d_, f_: _mx_tn(b_, kb_) + _mx_tn(m_, q_) + d_ * f_, db, kb, dq_m, qh, dkd, kdf)
        dq = each(lambda m_, k_, d_, e_: _mx(m_, k_) + d_ * e_, dq_m, kh, dqd, eg)
        tk = each(lambda a, b: rsum(a * b), dkd, kd)
        dbeta_all = jnp.zeros((DN_CHUNK, 128), F32)
        dgc_all = jnp.zeros((DN_CHUNK, 128), F32)
        for h in hs:
            dgc = (jnp.sum(e[h], axis=1, keepdims=True) - jnp.sum(e[h].T, axis=1, keepdims=True)
                   + rsum(dqd[h] * qd[h]) - tk[h] + rsum(drw[h] * rhs_w[h]))
            dgl = jnp.sum(tk[h], axis=0, keepdims=True) + ddec[h] * dec[h]
            dgc = dgc + jnp.where(rowi == DN_CHUNK - 1, dgl, 0.0)
            dbeta = rsum(dru[h] * vh[h]) + rsum(dkb[h] * kh[h])
            dq_ref[:, sl[h]] = dq[h]
            dk_ref[:, sl[h]] = dk[h] + dkb[h] * beta[h]
            dv_ref[:, sl[h]] = dru[h] * beta[h]
            dbeta_all = jnp.where(lane == h, dbeta, dbeta_all)
            dgc_all = jnp.where(lane == DN_HEADS + h, dgc, dgc_all)
        dbg_ref[...] = dbeta_all + _hi_tn(ltri, dgc_all)

    rev = lambda n: (n_chunks - 1 - n, 0)
    row = pl.BlockSpec((DN_CHUNK, DN_W), rev)
    small = pl.BlockSpec((DN_CHUNK, 128), rev)
    return pl.pallas_call(
        body, name=name, grid=(n_chunks,),
        in_specs=[row, row, row, small,
                  pl.BlockSpec((1, DN_HEADS, DN_DIM, DN_DIM), lambda n: (n_chunks - 1 - n, 0, 0, 0)),
                  pl.BlockSpec((1, DN_HEADS, DN_CHUNK, DN_CHUNK), lambda n: (n_chunks - 1 - n, 0, 0, 0)), row],
        out_specs=(row, row, row, small),
        out_shape=(jax.ShapeDtypeStruct((t, DN_W), F32),) * 3 + (jax.ShapeDtypeStruct((t, 128), F32),),
        scratch_shapes=[pltpu.VMEM((DN_HEADS, DN_DIM, DN_DIM), F32)],
        compiler_params=_cp("arbitrary"))(q, k, v, bg, sall, tall, do)


def gdn_out_fwd(o, proj, o_gain, name, tm=256):
    t = o.shape[0]

    def body(o_ref, z_ref, g_ref, y_ref, yt_ref):
        for h in range(DN_HEADS):
            sl = slice(h * DN_DIM, (h + 1) * DN_DIM)
            ov, zv = o_ref[:, sl], z_ref[:, sl]
            r = lax.rsqrt(jnp.mean(ov * ov, axis=-1, keepdims=True) + EPS)
            y = ov * r * g_ref[...] * (zv * _sigmoid(zv))
            y_ref[:, sl] = y.astype(y_ref.dtype)
            yt_ref[sl, :] = y.T.astype(yt_ref.dtype)

    row = pl.BlockSpec((tm, DN_W), lambda i: (i, 0))
    return pl.pallas_call(
        body, name=name, grid=(t // tm,),
        in_specs=[row, pl.BlockSpec((tm, DN_W), lambda i: (i, _Z_COL)), pl.BlockSpec((1, DN_DIM), lambda i: (0, 0))],
        out_specs=(row, pl.BlockSpec((DN_W, tm), lambda i: (0, i))),
        out_shape=(jax.ShapeDtypeStruct((t, DN_W), MXU_DTYPE), jax.ShapeDtypeStruct((DN_W, t), MXU_DTYPE)),
        compiler_params=_cp("parallel"))(o, proj, o_gain)


def gdn_out_bwd(o, proj, o_gain, dy, name, tm=256):
    t = o.shape[0]

    def body(o_ref, z_ref, g_ref, dy_ref, do_ref, dz_ref, dg_ref):
        i = pl.program_id(0)
        dg = jnp.zeros((1, DN_DIM), F32)
        for h in range(DN_HEADS):
            sl = slice(h * DN_DIM, (h + 1) * DN_DIM)
            ov, zv, dyv = o_ref[:, sl], z_ref[:, sl], dy_ref[:, sl]
            r = lax.rsqrt(jnp.mean(ov * ov, axis=-1, keepdims=True) + EPS)
            oh = ov * r
            sg = _sigmoid(zv)
            dz_ref[:, sl] = (dyv * oh * g_ref[...] * sg * (1.0 + zv * (1.0 - sg))).astype(dz_ref.dtype)
            don = dyv * (zv * sg)
            dg = dg + jnp.sum(don * oh, axis=0, keepdims=True)
            doh = don * g_ref[...]
            do_ref[:, sl] = r * (doh - oh * jnp.mean(doh * oh, axis=-1, keepdims=True))

        @pl.when(i == 0)
        def _():
            dg_ref[...] = dg

        @pl.when(i > 0)
        def _():
            dg_ref[...] += dg

    row = pl.BlockSpec((tm, DN_W), lambda i: (i, 0))
    one = pl.BlockSpec((1, DN_DIM), lambda i: (0, 0))
    return pl.pallas_call(
        body, name=name, grid=(t // tm,),
        in_specs=[row, pl.BlockSpec((tm, DN_W), lambda i: (i, _Z_COL)), one, row],
        out_specs=(row, row, one),
        out_shape=(jax.ShapeDtypeStruct((t, DN_W), F32), jax.ShapeDtypeStruct((t, DN_W), MXU_DTYPE),
                   jax.ShapeDtypeStruct((1, DN_DIM), F32)),
        compiler_params=_cp("arbitrary"))(o, proj, o_gain, dy)


def _peer(k):
    x, y, c = lax.axis_index("x"), lax.axis_index("y"), lax.axis_index("c")
    px = 1 - x if k & 4 else x
    py = 1 - y if k & 2 else y
    pc = 1 - c if k & 1 else c
    return (px, py, pc), 4 * px + 2 * py + pc


def all_gather(shards, name):
    na = len(shards)

    def body(*refs):
        ins, outs = refs[:na], refs[na:2 * na]
        send_sems, recv_sems, local_sems = refs[2 * na:]
        _, me = _peer(0)
        local = [pltpu.make_async_copy(ins[a], outs[a].at[me], local_sems.at[a]) for a in range(na)]
        for cp in local:
            cp.start()
        sends = []
        for k in range(1, N_DEV):
            peer, _ = _peer(k)
            for a in range(na):
                cp = pltpu.make_async_remote_copy(
                    src_ref=ins[a], dst_ref=outs[a].at[me], send_sem=send_sems.at[a, k - 1],
                    recv_sem=recv_sems.at[a, k - 1], device_id=peer, device_id_type=MESH)
                cp.start()
                sends.append(cp)
        for k in range(1, N_DEV):
            peer, pid = _peer(k)
            for a in range(na):
                pltpu.make_async_remote_copy(
                    src_ref=ins[a], dst_ref=outs[a].at[pid], send_sem=send_sems.at[a, k - 1],
                    recv_sem=recv_sems.at[a, k - 1], device_id=peer, device_id_type=MESH).wait_recv()
        for cp in sends:
            cp.wait_send()
        for cp in local:
            cp.wait()

    anyspec = pl.BlockSpec(memory_space=pl.ANY)
    return pl.pallas_call(
        body, name=name, in_specs=[anyspec] * na, out_specs=tuple([anyspec] * na),
        out_shape=tuple(jax.ShapeDtypeStruct((N_DEV,) + s.shape, s.dtype) for s in shards),
        scratch_shapes=[pltpu.SemaphoreType.DMA((na, N_DEV - 1)), pltpu.SemaphoreType.DMA((na, N_DEV - 1)),
                        pltpu.SemaphoreType.DMA((na,))],
        compiler_params=pltpu.CompilerParams(has_side_effects=True))(*shards)


_HBM = pl.BlockSpec(memory_space=pltpu.HBM)
_SEM = pl.BlockSpec(memory_space=pltpu.SEMAPHORE)
_DATAFLOW = pltpu.SideEffectType.DATAFLOW_SIDE_EFFECTING
N_PEER = N_DEV - 1


def send_start(srcs, name, scatter, after):
    na = len(srcs)
    ns = (2 * N_PEER + 1) * na
    lands = [lax.empty((N_DEV,) + (s.shape[1:] if scatter else s.shape), s.dtype) for s in srcs]
    extra = [] if after is None else [after]

    def body(*refs):
        src_refs, land_refs = refs[:na], refs[na:2 * na]
        sems = refs[2 * na + len(extra):2 * na + len(extra) + ns]
        land_out, token = refs[-1 - na:-1], refs[-1]
        _, me = _peer(0)
        for a in range(na):
            pltpu.make_async_copy(src_refs[a].at[me] if scatter else src_refs[a], land_out[a].at[me],
                                  sems[2 * N_PEER * na + a]).start()
        for k in range(1, N_DEV):
            peer, pid = _peer(k)
            for a in range(na):
                pltpu.make_async_remote_copy(
                    src_ref=src_refs[a].at[pid] if scatter else src_refs[a], dst_ref=land_refs[a].at[me],
                    send_sem=sems[2 * (a * N_PEER + k - 1)], recv_sem=sems[2 * (a * N_PEER + k - 1) + 1],
                    device_id=peer, device_id_type=MESH).start()
        token[...] = jnp.zeros_like(token)

    hbm = lambda arrs: tuple(pltpu.HBM(a.shape, a.dtype) for a in arrs)
    outs = pl.pallas_call(
        body, name=name,
        out_shape=(pltpu.SemaphoreType.DMA(()),) * ns + hbm(srcs) + hbm(lands) + (jax.ShapeDtypeStruct((8, 128), F32),),
        in_specs=[_HBM] * (2 * na) + [pl.BlockSpec(memory_space=pl.ANY)] * len(extra),
        out_specs=(_SEM,) * ns + (_HBM,) * (2 * na) + (pl.BlockSpec(memory_space=pltpu.VMEM),),
        input_output_aliases={i: ns + i for i in range(2 * na)},
        compiler_params=pltpu.CompilerParams(has_side_effects=_DATAFLOW),
    )(*[pltpu.with_memory_space_constraint(a, pltpu.HBM) for a in list(srcs) + lands], *extra)
    return outs[:ns], outs[ns:ns + na], outs[ns + na:ns + 2 * na], outs[-1]


def send_wait(sems, srcs_thru, lands_thru, name, scatter, after):
    na = len(srcs_thru)
    ns = (2 * N_PEER + 1) * na

    def body(*refs):
        src_refs, land_refs, sm = refs[:na], refs[na:2 * na], refs[2 * na:2 * na + ns]
        _, me = _peer(0)
        for a in range(na):
            pltpu.make_async_copy(src_refs[a].at[me] if scatter else src_refs[a], land_refs[a].at[me],
                                  sm[2 * N_PEER * na + a]).wait()
        for k in range(1, N_DEV):
            peer, pid = _peer(k)
            for a in range(na):
                cp = pltpu.make_async_remote_copy(
                    src_ref=src_refs[a].at[pid] if scatter else src_refs[a], dst_ref=land_refs[a].at[pid],
                    send_sem=sm[2 * (a * N_PEER + k - 1)], recv_sem=sm[2 * (a * N_PEER + k - 1) + 1],
                    device_id=peer, device_id_type=MESH)
                cp.wait_send()
                cp.wait_recv()

    hbm = lambda arrs: tuple(pltpu.HBM(a.shape, a.dtype) for a in arrs)
    outs = pl.pallas_call(
        body, name=name, out_shape=hbm(srcs_thru) + hbm(lands_thru),
        in_specs=[_HBM] * (2 * na) + [_SEM] * ns + [pl.BlockSpec(memory_space=pl.ANY)], out_specs=(_HBM,) * (2 * na),
        input_output_aliases={i: i for i in range(2 * na)},
        compiler_params=pltpu.CompilerParams(has_side_effects=_DATAFLOW),
    )(*srcs_thru, *lands_thru, *sems, after)
    return outs[na:]


def _adamw(w, g, m, v):
    m = ADAM_B1 * m + (1.0 - ADAM_B1) * g
    v = ADAM_B2 * v + (1.0 - ADAM_B2) * (g * g)
    m_hat = m / (1.0 - ADAM_B1 ** ADAM_STEP)
    v_hat = v / (1.0 - ADAM_B2 ** ADAM_STEP)
    return -ADAM_LR * (m_hat / (jnp.sqrt(v_hat) + ADAM_EPS) + ADAM_WD * w), m, v


def adam_sum(w, pieces, m, v, name, layer=0, into=None):
    nl, r, c = w.shape
    tr = r
    for cand in (256, 128, 64, 32, 16, 8):
        if r % cand == 0:
            tr = cand
            break

    def body(w_ref, p_ref, m_ref, v_ref, *rest):
        g_ref, d_ref, nm_ref, nv_ref = rest[-4:]
        g = p_ref[0].astype(F32)
        for s in range(1, N_DEV):
            g = g + p_ref[s].astype(F32)
        g_ref[0] = g
        d_ref[0], nm_ref[0], nv_ref[0] = _adamw(w_ref[0], g, m_ref[0], v_ref[0])

    row = pl.BlockSpec((1, tr, c), lambda i: (layer, i, 0))
    out = jax.ShapeDtypeStruct((nl, r, c), F32)
    extra = [] if into is None else list(into)
    return pl.pallas_call(
        body, name=name, grid=(r // tr,),
        in_specs=[row, pl.BlockSpec((N_DEV, tr, c), lambda i: (0, i, 0)), row, row]
        + [pl.BlockSpec(memory_space=pl.ANY)] * len(extra),
        out_specs=(row,) * 4, out_shape=(out,) * 4,
        input_output_aliases={4 + i: i for i in range(len(extra))},
        compiler_params=_cp("parallel"))(w, pieces, m, v, *extra)


def sum_rows(gathered, name):
    _, r, c = gathered.shape

    def body(p_ref, o_ref):
        g = p_ref[0]
        for s in range(1, N_DEV):
            g = g + p_ref[s]
        o_ref[...] = g

    return pl.pallas_call(body, name=name, out_shape=jax.ShapeDtypeStruct((r, c), F32))(gathered)


def adam_small(w, g, m, v, name):
    def body(w_ref, g_ref, m_ref, v_ref, d_ref, nm_ref, nv_ref):
        d_ref[...], nm_ref[...], nv_ref[...] = _adamw(w_ref[...], g_ref[...], m_ref[...], v_ref[...])

    out = jax.ShapeDtypeStruct(w.shape, F32)
    return pl.pallas_call(body, name=name, out_shape=(out,) * 3)(w, g, m, v)


def _rope_tables(t):
    inv_freq = 10000.0 ** (-jnp.arange(0, HEAD_DIM, 2, dtype=F32) / HEAD_DIM)
    ang = jnp.arange(t, dtype=F32)[:, None] * inv_freq[None, :]
    cos, sin = jnp.cos(ang), jnp.sin(ang)
    return jnp.concatenate([cos, cos], axis=-1), jnp.concatenate([sin, sin], axis=-1)


def _lane_row(vec8):
    return jnp.pad(vec8.reshape(1, DN_HEADS), ((0, 0), (DN_HEADS, 128 - 2 * DN_HEADS)))


def _ffn_fwd(x, norm_g, w_gu, w_d, tag):
    f, ft = rms_fwd(x, norm_g, f"{tag}_norm")
    gu, a, at = ffn_up(f, w_gu, f"{tag}_gate_up")
    return mm_nn(a, w_d, f"{tag}_down", res=x), (ft, gu, at)


def _ffn_bwd(x, norm_g, w_gu, w_d, saved, dy, tag, after=None):
    ft, gu, at = saved
    dgu = ffn_dact(dy, w_d, gu, f"{tag}_d_gate_up", after=after)
    dwd = mm_at(at, dy, f"{tag}_dw_down")
    df = mm_nt(dgu, w_gu, f"{tag}_d_normed")
    dwgu = mm_at(ft, dgu, f"{tag}_dw_gate_up")
    dx, dg = rms_bwd(x, norm_g, df, dy, f"{tag}_d_norm")
    return dx, dwgu, dwd, dg


def local_step(x, target, small, weights_of, grads_out, after=None):
    t = x.shape[0]
    cosf, sinf = _rope_tables(t)
    alog_row, dtb_row = _lane_row(small["odd_a_log"]), _lane_row(small["odd_dt_bias"])

    h0, h0t = rms_fwd(x, small["even_norm"], "even_norm", after=after)
    we = weights_of("even", h0)
    proj0 = mm_nn(h0, we["w_in"], "even_in_proj")
    qr, kr = qk_prep_fwd(proj0, small["even_q_gain"], small["even_k_gain"], cosf, sinf, "even_qk_prep")
    y_attn, y_attn_t, lse = swa_fwd(qr, kr, proj0, small["even_sinks"], "even_swa")
    y_conv, y_conv_t = gconv_fwd(proj0, small["even_conv_w"], "even_gconv")
    mix0 = jnp.concatenate([y_attn.astype(MXU_DTYPE), y_conv], axis=-1)
    mix0_t = jnp.concatenate([y_attn_t, y_conv_t], axis=0)
    x1 = mm_nn(mix0, we["w_out"], "even_out_proj", res=x)
    w0 = weights_of("ffn0", x1)
    x2, ffn0 = _ffn_fwd(x1, small["ffn_norm0"], w0["gate_up"], w0["down"], "ffn0")

    wo = weights_of("odd", x2)
    h1, h1t = rms_fwd(x2, small["odd_norm"], "odd_norm")
    proj1 = mm_nn(h1, wo["w_in"], "odd_in_proj")
    qn, kn, vs, bg = gdn_prep_fwd(proj1, small["odd_conv_w"], alog_row, dtb_row, "odd_prep")
    o, sall, tall = gdn_fwd(qn, kn, vs, bg, "odd_delta_rule")
    og, ogt = gdn_out_fwd(o, proj1, small["odd_o_gain"], "odd_gate_norm")
    x3 = mm_nn(og, wo["w_out"], "odd_out_proj", res=x2)
    w1 = weights_of("ffn1", x3)
    x4, ffn1 = _ffn_fwd(x3, small["ffn_norm1"], w1["gate_up"], w1["down"], "ffn1")

    dy, loss_row = loss_head(x4, target, "loss_head")

    gs = {}
    dx3, dwgu, dwd, gs["ffn_norm1"] = _ffn_bwd(x3, small["ffn_norm1"], w1["gate_up"], w1["down"], ffn1, dy, "ffn1")
    tok = grads_out("ffn1", {"gate_up": dwgu, "down": dwd})

    dog = mm_nt(dx3, wo["w_out"], "odd_d_gated", after=tok)
    dwo = mm_at(ogt, dx3, "odd_dw_out")
    do, dz, gs["odd_o_gain"] = gdn_out_bwd(o, proj1, small["odd_o_gain"], dog, "odd_d_gate_norm")
    dqn, dkn, dvs, dbg = gdn_bwd(qn, kn, vs, bg, sall, tall, do, "odd_d_delta_rule")
    dqkv, dba, gs["odd_conv_w"], ddt_row, dal_row = gdn_prep_bwd(
        proj1, small["odd_conv_w"], alog_row, dtb_row, dqn, dkn, dvs, dbg, "odd_d_prep")
    gs["odd_dt_bias"] = ddt_row[:, DN_HEADS:2 * DN_HEADS]
    gs["odd_a_log"] = dal_row[:, DN_HEADS:2 * DN_HEADS]
    dproj1 = jnp.concatenate([dqkv, dz, dba], axis=-1)
    dh1 = mm_nt(dproj1, wo["w_in"], "odd_d_normed")
    dwi = mm_at(h1t, dproj1, "odd_dw_in")
    dx2, gs["odd_norm"] = rms_bwd(x2, small["odd_norm"], dh1, dx3, "odd_d_norm")
    tok = grads_out("odd", {"w_in": dwi, "w_out": dwo})

    dx1, dwgu, dwd, gs["ffn_norm0"] = _ffn_bwd(x1, small["ffn_norm0"], w0["gate_up"], w0["down"], ffn0, dx2, "ffn0",
                                               after=tok)
    tok = grads_out("ffn0", {"gate_up": dwgu, "down": dwd})

    dmix = mm_nt(dx1, we["w_out"], "even_d_mix", after=tok)
    dwo = mm_at(mix0_t, dx1, "even_dw_out")
    dqr, dkr, dv, gs["even_sinks"] = swa_bwd(qr, kr, proj0, small["even_sinks"], y_attn, lse, dmix, "even_d_swa")
    dqk, gs["even_q_gain"], gs["even_k_gain"] = qk_prep_bwd(
        proj0, small["even_q_gain"], small["even_k_gain"], cosf, sinf, dqr, dkr, "even_d_qk_prep")
    dgb, dgc, dxi, gs["even_conv_w"] = gconv_bwd(proj0, small["even_conv_w"], dmix, "even_d_gconv")
    dproj0 = jnp.concatenate([dqk, dv, dgb, dgc, dxi], axis=-1)
    dwi = mm_at(h0t, dproj0, "even_dw_in")
    tok = grads_out("even", {"w_in": dwi, "w_out": dwo})
    dh0 = mm_nt(dproj0, we["w_in"], "even_d_normed", after=tok)
    grad_x, gs["even_norm"] = rms_bwd(x, small["even_norm"], dh0, dx1, "even_d_norm")
    return loss_row, grad_x, gs


_SMALL_ORDER = ("even_norm", "even_q_gain", "even_k_gain", "even_sinks", "odd_a_log", "odd_dt_bias", "odd_o_gain",
                "ffn_norm0", "ffn_norm1", "odd_norm", "even_conv_w", "odd_conv_w")
_SMALL_SIZE = {"even_norm": 1024, "even_q_gain": 64, "even_k_gain": 64, "even_sinks": 8, "odd_a_log": 8,
               "odd_dt_bias": 8, "odd_o_gain": 128, "ffn_norm0": 1024, "ffn_norm1": 1024, "odd_norm": 1024,
               "even_conv_w": 3 * 512, "odd_conv_w": 4 * 3072}
_N_REPL = 9


def _pack_rows(vals):
    flat = jnp.concatenate([v.reshape(-1) for v in vals])
    pad = (-flat.shape[0]) % 1024
    return jnp.pad(flat, (0, pad)).reshape(-1, 128)


def _my_block(full, size, axis):
    me = 4 * lax.axis_index("x") + 2 * lax.axis_index("y") + lax.axis_index("c")
    return lax.dynamic_slice_in_dim(full, me * size, size, axis=axis)


def _col_gathered(g):
    return g.transpose(1, 0, 2).reshape(g.shape[1], N_DEV * g.shape[2])


def _col_pieces(dw):
    k, n8 = dw.shape
    return dw.reshape(k, N_DEV, n8 // N_DEV).transpose(1, 0, 2)


def kernel(x, even_norm, even_w_in, even_q_gain, even_k_gain, even_sinks, even_conv_w, even_w_out, odd_norm, odd_w_in, odd_conv_w, odd_a_log, odd_dt_bias, odd_o_gain, odd_w_out, ffn_norm, ffn_w_gate_up, ffn_w_down, loss_target, m_even_norm, m_even_w_in, m_even_q_gain, m_even_k_gain, m_even_sinks, m_even_conv_w, m_even_w_out, m_odd_norm, m_odd_w_in, m_odd_conv_w, m_odd_a_log, m_odd_dt_bias, m_odd_o_gain, m_odd_w_out, m_ffn_norm, m_ffn_w_gate_up, m_ffn_w_down, v_even_norm, v_even_w_in, v_even_q_gain, v_even_k_gain, v_even_sinks, v_even_conv_w, v_even_w_out, v_odd_norm, v_odd_w_in, v_odd_conv_w, v_odd_a_log, v_odd_dt_bias, v_odd_o_gain, v_odd_w_out, v_ffn_norm, v_ffn_w_gate_up, v_ffn_w_down):
    t = x.shape[1]
    d = D_MODEL

    me = 4 * lax.axis_index("x") + 2 * lax.axis_index("y") + lax.axis_index("c")
    fpd = D_FF // N_DEV
    shard = {
        "even": {"w_in": even_w_in.reshape(d, EVEN_IN_W // N_DEV), "w_out": even_w_out.reshape(d // N_DEV, d)},
        "ffn0": {"gate_up": ffn_w_gate_up[0], "down": ffn_w_down[0]},
        "odd": {"w_in": odd_w_in.reshape(d, ODD_IN_W // N_DEV), "w_out": odd_w_out.reshape(d // N_DEV, d)},
        "ffn1": {"gate_up": ffn_w_gate_up[1], "down": ffn_w_down[1]},
    }
    given = {
        ("even", "w_in"): ("even_w_in", even_w_in, m_even_w_in, v_even_w_in, 0),
        ("even", "w_out"): ("even_w_out", even_w_out, m_even_w_out, v_even_w_out, 0),
        ("odd", "w_in"): ("odd_w_in", odd_w_in, m_odd_w_in, v_odd_w_in, 0),
        ("odd", "w_out"): ("odd_w_out", odd_w_out, m_odd_w_out, v_odd_w_out, 0),
        ("ffn0", "gate_up"): ("ffn_w_gate_up", ffn_w_gate_up, m_ffn_w_gate_up, v_ffn_w_gate_up, 0),
        ("ffn1", "gate_up"): ("ffn_w_gate_up", ffn_w_gate_up, m_ffn_w_gate_up, v_ffn_w_gate_up, 1),
        ("ffn0", "down"): ("ffn_w_down", ffn_w_down, m_ffn_w_down, v_ffn_w_down, 0),
        ("ffn1", "down"): ("ffn_w_down", ffn_w_down, m_ffn_w_down, v_ffn_w_down, 1),
    }

    def whole(group, parts):
        col, row = tuple(shard[group])
        w_col = _gu_gathered(parts[0]) if col == "gate_up" else _col_gathered(parts[0])
        if group == "odd":
            w_col = jnp.pad(w_col, ((0, 0), (0, ODD_IN_PAD - ODD_IN_W)))
        return {col: w_col, row: parts[1].reshape(-1, d)}

    wire = {g: [a.astype(MXU_DTYPE) for a in shard[g].values()] for g in shard}
    gathers, tok = {}, None
    for g in shard:
        sems, srcs_thru, lands_thru, tok = send_start(wire[g], f"gather_{g}_start", False, tok)
        gathers[g] = (sems, srcs_thru, lands_thru)
        if g == "even":
            shard_rows = _pack_rows([odd_norm, even_conv_w, odd_conv_w]) + tok[0:1, 0:1]
            (small_g,) = all_gather([shard_rows], "gather_small_weights")
            tok = small_g

    def weights_of(group, after):
        lands = send_wait(*gathers[group], f"gather_{group}_wait", False, after)
        return whole(group, lands)

    sent = {}

    def grads_out(group, dws):
        col, row = tuple(shard[group])
        n_cols = N_DEV * shard[group][col].shape[1]
        pieces = [_gu_pieces(dws[col]) if col == "gate_up" else _col_pieces(dws[col][:, :n_cols]),
                  dws[row].reshape((N_DEV,) + shard[group][row].shape)]
        sems, srcs_thru, lands_thru, token = send_start(pieces, f"exchange_{group}_start", True, None)
        sent[group] = (sems, srcs_thru, lands_thru, pieces)
        return token

    sg = small_g.reshape(N_DEV, -1)
    o1 = d // N_DEV
    o2 = o1 + 3 * CONV_CH // N_DEV
    small = {
        "even_norm": even_norm, "even_q_gain": even_q_gain, "even_k_gain": even_k_gain, "even_sinks": even_sinks,
        "odd_a_log": odd_a_log.reshape(-1), "odd_dt_bias": odd_dt_bias.reshape(-1), "odd_o_gain": odd_o_gain,
        "ffn_norm0": ffn_norm[0:1], "ffn_norm1": ffn_norm[1:2],
        "odd_norm": sg[:, :o1].reshape(1, d),
        "even_conv_w": sg[:, o1:o2].reshape(N_DEV, 3, CONV_CH // N_DEV).transpose(1, 0, 2).reshape(3, CONV_CH),
        "odd_conv_w": sg[:, o2:o2 + 4 * _QKV_W // N_DEV].reshape(N_DEV, 4, _QKV_W // N_DEV).transpose(1, 0, 2).reshape(4, _QKV_W),
    }

    loss_row, grad_x, gs = local_step(x.reshape(t, d), loss_target.reshape(t, d), small, weights_of, grads_out, after=tok)

    rows = _pack_rows([gs[n] for n in _SMALL_ORDER] + [loss_row[:, 0:1]])
    (rows_g,) = all_gather([rows], "gather_small_grads")
    tot = sum_rows(rows_g, "sum_small_grads").reshape(-1)
    off, sgrad = 0, {}
    for n in _SMALL_ORDER:
        sgrad[n] = tot[off:off + _SMALL_SIZE[n]]
        off += _SMALL_SIZE[n]
    loss = tot[off]

    res, behind = {}, tot
    for g in ("ffn1", "odd", "ffn0", "even"):
        sems, srcs_thru, lands_thru, pieces = sent[g]
        lands = send_wait(sems, srcs_thru, lands_thru, f"exchange_{g}_wait", True, behind)
        for key, pcs in zip(shard[g], lands):
            name, w_, m_, v_, layer = given[g, key]
            res[name] = adam_sum(w_, pcs, m_, v_, f"adamw_{g}_{key}", layer=layer, into=res.get(name))
        behind = res[name][0]

    repl = _SMALL_ORDER[:_N_REPL]
    repl_w = {"even_norm": even_norm, "even_q_gain": even_q_gain, "even_k_gain": even_k_gain, "even_sinks": even_sinks,
              "odd_a_log": odd_a_log, "odd_dt_bias": odd_dt_bias, "odd_o_gain": odd_o_gain,
              "ffn_norm0": ffn_norm[0], "ffn_norm1": ffn_norm[1]}
    repl_m = {"even_norm": m_even_norm, "even_q_gain": m_even_q_gain, "even_k_gain": m_even_k_gain,
              "even_sinks": m_even_sinks, "odd_a_log": m_odd_a_log, "odd_dt_bias": m_odd_dt_bias,
              "odd_o_gain": m_odd_o_gain, "ffn_norm0": m_ffn_norm[0], "ffn_norm1": m_ffn_norm[1]}
    repl_v = {"even_norm": v_even_norm, "even_q_gain": v_even_q_gain, "even_k_gain": v_even_k_gain,
              "even_sinks": v_even_sinks, "odd_a_log": v_odd_a_log, "odd_dt_bias": v_odd_dt_bias,
              "odd_o_gain": v_odd_o_gain, "ffn_norm0": v_ffn_norm[0], "ffn_norm1": v_ffn_norm[1]}
    pk = lambda dct: _pack_rows([dct[n] for n in repl])
    pd_, pm_, pv_ = adam_small(pk(repl_w), pk(sgrad), pk(repl_m), pk(repl_v), "adamw_replicated")
    sres = {}
    off = 0
    for n in repl:
        sz = _SMALL_SIZE[n]
        sres[n] = (sgrad[n], pd_.reshape(-1)[off:off + sz], pm_.reshape(-1)[off:off + sz], pv_.reshape(-1)[off:off + sz])
        off += sz
    g_on = _my_block(sgrad["odd_norm"].reshape(1, d), d // N_DEV, 1)
    g_ec = _my_block(sgrad["even_conv_w"].reshape(3, CONV_CH), CONV_CH // N_DEV, 1)
    g_oc = _my_block(sgrad["odd_conv_w"].reshape(4, _QKV_W), _QKV_W // N_DEV, 1)
    shard_w = _pack_rows([odd_norm, even_conv_w, odd_conv_w])
    sd_, sm_, sv_ = adam_small(shard_w, _pack_rows([g_on, g_ec, g_oc]),
                               _pack_rows([m_odd_norm, m_even_conv_w, m_odd_conv_w]),
                               _pack_rows([v_odd_norm, v_even_conv_w, v_odd_conv_w]), "adamw_sharded_small")
    off = 0
    for n, gfull, like in (("odd_norm", g_on, odd_norm), ("even_conv_w", g_ec, even_conv_w), ("odd_conv_w", g_oc, odd_conv_w)):
        sz = like.size
        sres[n] = (gfull, sd_.reshape(-1)[off:off + sz], sm_.reshape(-1)[off:off + sz], sv_.reshape(-1)[off:off + sz])
        off += sz

    def small_out(name, like, kind):
        if name == "ffn_norm":
            return jnp.stack([sres["ffn_norm0"][kind], sres["ffn_norm1"][kind]]).reshape(like.shape)
        return sres[name][kind].reshape(like.shape)

    order = (("even_norm", even_norm), ("even_w_in", even_w_in), ("even_q_gain", even_q_gain),
             ("even_k_gain", even_k_gain), ("even_sinks", even_sinks), ("even_conv_w", even_conv_w),
             ("even_w_out", even_w_out), ("odd_norm", odd_norm), ("odd_w_in", odd_w_in), ("odd_conv_w", odd_conv_w),
             ("odd_a_log", odd_a_log), ("odd_dt_bias", odd_dt_bias), ("odd_o_gain", odd_o_gain),
             ("odd_w_out", odd_w_out), ("ffn_norm", ffn_norm), ("ffn_w_gate_up", ffn_w_gate_up),
             ("ffn_w_down", ffn_w_down))
    outs = [loss, grad_x.reshape(x.shape)]
    for kind in range(4):
        for name, like in order:
            outs.append(res[name][kind] if name in res else small_out(name, like, kind))
    return tuple(outs)
```

```python
import jax
import jax.numpy as jnp
import numpy as np
from jax import lax
from jax.experimental import pallas as pl
from jax.experimental.pallas import tpu as pltpu

F32 = jnp.float32
MXU_DTYPE = jnp.bfloat16
HI = lax.Precision.HIGH
EPS = 1e-6
N_DEV = 8
D_MODEL = 1024
HEAD_DIM = 64
ATTN_HEADS = 8
KV_HEADS = 2
ATTN_BLOCK = 128
Q_W = 512
KV_W = 128
CONV_CH = 512
EVEN_IN_W = 2304
DN_HEADS = 8
DN_DIM = 128
DN_W = 1024
DN_CHUNK = 64
ODD_IN_W = 4112
ODD_IN_PAD = 4224
D_FF = 2816
NEG = -1e30
VMEM_LIMIT = 56 * 1024 * 1024
ADAM_LR, ADAM_B1, ADAM_B2, ADAM_EPS, ADAM_WD, ADAM_STEP = 0.001, 0.9, 0.999, 1e-08, 0.01, 10
MESH = pl.DeviceIdType.MESH


def _cp(*sem):
    return pltpu.CompilerParams(dimension_semantics=sem, vmem_limit_bytes=VMEM_LIMIT)


def _pick(n, cap):
    best = 128
    for t in range(128, cap + 1, 128):
        if n % t == 0:
            best = t
    return best


def _mx(a, b):
    return jnp.dot(a.astype(MXU_DTYPE), b.astype(MXU_DTYPE), preferred_element_type=F32)


def _mx_nt(a, b):
    return lax.dot_general(a.astype(MXU_DTYPE), b.astype(MXU_DTYPE), (((1,), (1,)), ((), ())),
                           preferred_element_type=F32)


def _mx_tn(a, b):
    return lax.dot_general(a.astype(MXU_DTYPE), b.astype(MXU_DTYPE), (((0,), (0,)), ((), ())),
                           preferred_element_type=F32)


def _hi(a, b):
    return jnp.dot(a, b, precision=HI, preferred_element_type=F32)


def _hi_nt(a, b):
    return lax.dot_general(a, b, (((1,), (1,)), ((), ())), precision=HI, preferred_element_type=F32)


def _hi_tn(a, b):
    return lax.dot_general(a, b, (((0,), (0,)), ((), ())), precision=HI, preferred_element_type=F32)


def _sigmoid(x):
    return 0.5 * jnp.tanh(0.5 * x) + 0.5


def _softplus(x):
    return jnp.maximum(x, 0.0) + jnp.log(1.0 + jnp.exp(-jnp.abs(x)))


def mm_nn(a, b, name, res=None, out_dtype=F32, tm=1024):
    m, k = a.shape
    _, n = b.shape
    tn = _pick(n, 1536)
    tm = min(tm, m)

    def body(*refs):
        a_ref, b_ref = refs[0], refs[1]
        o_ref = refs[-1]
        acc = _mx(a_ref[...], b_ref[...])
        if res is not None:
            acc = acc + refs[2][...]
        o_ref[...] = acc.astype(o_ref.dtype)

    in_specs = [pl.BlockSpec((tm, k), lambda j, i: (i, 0)), pl.BlockSpec((k, tn), lambda j, i: (0, j))]
    args = [a, b]
    if res is not None:
        in_specs.append(pl.BlockSpec((tm, tn), lambda j, i: (i, j)))
        args.append(res)
    return pl.pallas_call(
        body, name=name, grid=(n // tn, m // tm), in_specs=in_specs,
        out_specs=pl.BlockSpec((tm, tn), lambda j, i: (i, j)),
        out_shape=jax.ShapeDtypeStruct((m, n), out_dtype), compiler_params=_cp("parallel", "parallel"))(*args)


def mm_nn_res_norm(a, b, res, g, name, tm=512):
    t, k = a.shape
    d = b.shape[1]
    tm = min(tm, t)

    def body(a_ref, b_ref, res_ref, g_ref, y_ref, h_ref, ht_ref):
        y = res_ref[...] + _mx(a_ref[...], b_ref[...])
        y_ref[...] = y
        h = y * lax.rsqrt(jnp.mean(y * y, axis=-1, keepdims=True) + EPS) * g_ref[...]
        h_ref[...] = h.astype(h_ref.dtype)
        ht_ref[...] = h.T.astype(ht_ref.dtype)

    row = pl.BlockSpec((tm, d), lambda i: (i, 0))
    return pl.pallas_call(
        body, name=name, grid=(t // tm,),
        in_specs=[pl.BlockSpec((tm, k), lambda i: (i, 0)), pl.BlockSpec((k, d), lambda i: (0, 0)), row,
                  pl.BlockSpec((1, d), lambda i: (0, 0))],
        out_specs=(row, row, pl.BlockSpec((d, tm), lambda i: (0, i))),
        out_shape=(jax.ShapeDtypeStruct((t, d), F32), jax.ShapeDtypeStruct((t, d), MXU_DTYPE),
                   jax.ShapeDtypeStruct((d, t), MXU_DTYPE)),
        compiler_params=_cp("parallel"))(a, b, res, g)


def mm_nt(a, b, name, out_dtype=F32, tm=1024, after=None):
    m, k = a.shape
    n, _ = b.shape
    tn = _pick(n, 512 if k > 3000 else 1536)
    tm = min(tm, m)

    def body(a_ref, b_ref, *rest):
        o_ref = rest[-1]
        o_ref[...] = _mx_nt(a_ref[...], b_ref[...]).astype(o_ref.dtype)

    in_specs = [pl.BlockSpec((tm, k), lambda j, i: (i, 0)), pl.BlockSpec((tn, k), lambda j, i: (j, 0))]
    args = [a, b]
    if after is not None:
        in_specs.append(pl.BlockSpec(memory_space=pl.ANY))
        args.append(after)
    return pl.pallas_call(
        body, name=name, grid=(n // tn, m // tm), in_specs=in_specs,
        out_specs=pl.BlockSpec((tm, tn), lambda j, i: (i, j)),
        out_shape=jax.ShapeDtypeStruct((m, n), out_dtype), compiler_params=_cp("parallel", "parallel"))(*args)


def mm_at(at, b, name, tk=1024):
    m, kk = at.shape
    _, n = b.shape
    tm, tn, tk = _pick(m, 1408), _pick(n, 1408), min(tk, kk)
    nk = kk // tk

    def body(a_ref, b_ref, o_ref, acc_ref):
        k = pl.program_id(2)
        p = _mx(a_ref[...], b_ref[...])
        acc = jnp.where(k == 0, p, acc_ref[...] + p)
        acc_ref[...] = acc

        @pl.when(k == nk - 1)
        def _():
            o_ref[...] = acc.astype(o_ref.dtype)

    return pl.pallas_call(
        body, name=name, grid=(m // tm, n // tn, nk),
        in_specs=[pl.BlockSpec((tm, tk), lambda i, j, k: (i, k)), pl.BlockSpec((tk, tn), lambda i, j, k: (k, j))],
        out_specs=pl.BlockSpec((tm, tn), lambda i, j, k: (i, j)),
        out_shape=jax.ShapeDtypeStruct((m, n), MXU_DTYPE), scratch_shapes=[pltpu.VMEM((tm, tn), F32)],
        compiler_params=_cp("parallel", "parallel", "arbitrary"))(at, b)


def rms_fwd(x, g, name, tm=512, after=None):
    t, d = x.shape

    def body(x_ref, g_ref, *rest):
        o_ref, ot_ref = rest[-2:]
        xv = x_ref[...]
        r = lax.rsqrt(jnp.mean(xv * xv, axis=-1, keepdims=True) + EPS)
        h = xv * r * g_ref[...]
        o_ref[...] = h.astype(o_ref.dtype)
        ot_ref[...] = h.T.astype(ot_ref.dtype)

    in_specs = [pl.BlockSpec((tm, d), lambda i: (i, 0)), pl.BlockSpec((1, d), lambda i: (0, 0))]
    args = [x, g]
    if after is not None:
        in_specs.append(pl.BlockSpec(memory_space=pl.ANY))
        args.append(after)
    return pl.pallas_call(
        body, name=name, grid=(t // tm,), in_specs=in_specs,
        out_specs=(pl.BlockSpec((tm, d), lambda i: (i, 0)), pl.BlockSpec((d, tm), lambda i: (0, i))),
        out_shape=(jax.ShapeDtypeStruct((t, d), MXU_DTYPE), jax.ShapeDtypeStruct((d, t), MXU_DTYPE)),
        compiler_params=_cp("parallel"))(*args)


def mm_nt_rms_bwd(a, b, x, g, dres, name, tm=512, after=None):
    t, k = a.shape
    d = b.shape[0]
    tm = min(tm, t)

    def body(a_ref, b_ref, x_ref, g_ref, dres_ref, *rest):
        dx_ref, dg_ref = rest[-2:]
        dhv = _mx_nt(a_ref[...], b_ref[...])
        xv = x_ref[...]
        r = lax.rsqrt(jnp.mean(xv * xv, axis=-1, keepdims=True) + EPS)
        xh = xv * r
        dxh = dhv * g_ref[...]
        dx_ref[...] = dres_ref[...] + r * (dxh - xh * jnp.mean(dxh * xh, axis=-1, keepdims=True))
        part = jnp.sum(dhv * xh, axis=0, keepdims=True)
        dg_ref[...] = jnp.where(pl.program_id(0) == 0, part, dg_ref[...] + part)

    row = pl.BlockSpec((tm, d), lambda i: (i, 0))
    one = pl.BlockSpec((1, d), lambda i: (0, 0))
    in_specs = [pl.BlockSpec((tm, k), lambda i: (i, 0)), pl.BlockSpec((d, k), lambda i: (0, 0)), row, one, row]
    args = [a, b, x, g, dres]
    if after is not None:
        in_specs.append(pl.BlockSpec(memory_space=pl.ANY))
        args.append(after)
    return pl.pallas_call(
        body, name=name, grid=(t // tm,), in_specs=in_specs, out_specs=(row, one),
        out_shape=(jax.ShapeDtypeStruct((t, d), F32), jax.ShapeDtypeStruct((1, d), F32)),
        compiler_params=_cp("arbitrary"))(*args)


GU_TILE = 1408


_GU_PER_TILE = GU_TILE * N_DEV // (2 * D_FF)


def _gu_gathered(g):
    _, k, c = g.shape
    nj = N_DEV // (2 * _GU_PER_TILE)
    return g.reshape(2, nj, _GU_PER_TILE, k, c).transpose(3, 1, 0, 2, 4).reshape(k, N_DEV * c)


def _gu_pieces(dw):
    k, n8 = dw.shape
    nj = N_DEV // (2 * _GU_PER_TILE)
    return dw.reshape(k, nj, 2, _GU_PER_TILE, n8 // N_DEV).transpose(2, 1, 3, 0, 4).reshape(N_DEV, k, n8 // N_DEV)


def ffn_up(f, w, name, tm=512):
    t, d = f.shape

    def body(f_ref, w_ref, gu_ref, a_ref, at_ref):
        gu = _mx(f_ref[...], w_ref[...])
        gu_ref[...] = gu
        g, u = gu[:, :GU_TILE], gu[:, GU_TILE:]
        act = g * _sigmoid(g) * u
        a_ref[...] = act.astype(a_ref.dtype)
        at_ref[...] = act.T.astype(at_ref.dtype)

    return pl.pallas_call(
        body, name=name, grid=(D_FF // GU_TILE, t // tm),
        in_specs=[pl.BlockSpec((tm, d), lambda j, i: (i, 0)), pl.BlockSpec((d, 2 * GU_TILE), lambda j, i: (0, j))],
        out_specs=(pl.BlockSpec((tm, 2 * GU_TILE), lambda j, i: (i, j)), pl.BlockSpec((tm, GU_TILE), lambda j, i: (i, j)),
                   pl.BlockSpec((GU_TILE, tm), lambda j, i: (j, i))),
        out_shape=(jax.ShapeDtypeStruct((t, 2 * D_FF), F32), jax.ShapeDtypeStruct((t, D_FF), MXU_DTYPE),
                   jax.ShapeDtypeStruct((D_FF, t), MXU_DTYPE)),
        compiler_params=_cp("parallel", "parallel"))(f, w)


def ffn_dact(dy, w_d, gu, name, tm=512, after=None):
    t, d = dy.shape

    def body(dy_ref, w_ref, gu_ref, *rest):
        o_ref = rest[-1]
        da = _mx_nt(dy_ref[...], w_ref[...])
        g, u = gu_ref[:, :GU_TILE], gu_ref[:, GU_TILE:]
        sg = _sigmoid(g)
        o_ref[:, :GU_TILE] = (da * u * sg * (1.0 + g * (1.0 - sg))).astype(o_ref.dtype)
        o_ref[:, GU_TILE:] = (da * g * sg).astype(o_ref.dtype)

    in_specs = [pl.BlockSpec((tm, d), lambda j, i: (i, 0)), pl.BlockSpec((GU_TILE, d), lambda j, i: (j, 0)),
                pl.BlockSpec((tm, 2 * GU_TILE), lambda j, i: (i, j))]
    args = [dy, w_d, gu]
    if after is not None:
        in_specs.append(pl.BlockSpec(memory_space=pl.ANY))
        args.append(after)
    return pl.pallas_call(
        body, name=name, grid=(D_FF // GU_TILE, t // tm), in_specs=in_specs,
        out_specs=pl.BlockSpec((tm, 2 * GU_TILE), lambda j, i: (i, j)),
        out_shape=jax.ShapeDtypeStruct((t, 2 * D_FF), MXU_DTYPE), compiler_params=_cp("parallel", "parallel"))(*args)


def loss_head(y, target, name, tm=512):
    t, d = y.shape

    def body(y_ref, t_ref, dy_ref, l_ref):
        i = pl.program_id(0)
        e = y_ref[...] - t_ref[...]
        dy_ref[...] = e * (1.0 / d)
        part = jnp.zeros((1, 128), F32) + 0.5 * jnp.sum(jnp.mean(e * e, axis=-1, keepdims=True), axis=0, keepdims=True)

        @pl.when(i == 0)
        def _():
            l_ref[...] = part

        @pl.when(i > 0)
        def _():
            l_ref[...] += part

    row = pl.BlockSpec((tm, d), lambda i: (i, 0))
    return pl.pallas_call(
        body, name=name, grid=(t // tm,), in_specs=[row, row],
        out_specs=(row, pl.BlockSpec((1, 128), lambda i: (0, 0))),
        out_shape=(jax.ShapeDtypeStruct((t, d), F32), jax.ShapeDtypeStruct((1, 128), F32)),
        compiler_params=_cp("arbitrary"))(y, target)


QK_W = Q_W + KV_W
_QK_TILE = 256


def _qk_mats():
    idx = np.arange(_QK_TILE)
    half = HEAD_DIM // 2
    same = (idx[:, None] // HEAD_DIM) == (idx[None, :] // HEAD_DIM)
    lo = (idx % HEAD_DIM) < half
    rot = np.where((idx[:, None] == idx[None, :] + half) & lo[None, :], -1.0, 0.0)
    rot = rot + np.where((idx[:, None] == idx[None, :] - half) & ~lo[None, :], 1.0, 0.0)
    return jnp.asarray(same, F32), jnp.asarray(rot, F32)


def _qk_rows(q_gain, k_gain, cosf, sinf):
    gain = jnp.concatenate([q_gain] * ATTN_HEADS + [k_gain] * KV_HEADS, axis=-1)
    return gain, jnp.concatenate([cosf, cosf], axis=-1), jnp.concatenate([sinf, sinf], axis=-1)


def _qk_tiles(a, mat, transposed=False):
    outs = []
    for c0 in range(0, QK_W, _QK_TILE):
        w = min(_QK_TILE, QK_W - c0)
        mt = (mat.T if transposed else mat)[:w, :w].astype(MXU_DTYPE)
        at = a[:, c0:c0 + w]
        hi = at.astype(MXU_DTYPE)
        lo = (at - hi.astype(F32)).astype(MXU_DTYPE)
        outs.append(jnp.dot(hi, mt, preferred_element_type=F32) + jnp.dot(lo, mt, preferred_element_type=F32))
    return jnp.concatenate(outs, axis=-1)


def qk_prep_fwd(proj, q_gain, k_gain, cosf, sinf, name, tm=256):
    t = proj.shape[0]
    gmat, rmat = _qk_mats()
    gain, c2, s2 = _qk_rows(q_gain, k_gain, cosf, sinf)
    rep = QK_W // 128

    def body(p_ref, g_ref, c_ref, s_ref, gm_ref, rm_ref, q_ref, k_ref):
        x = p_ref[...]
        r = lax.rsqrt(_qk_tiles(x * x, gm_ref[...]) * (1.0 / HEAD_DIM) + EPS)
        xn = x * r * g_ref[...]
        c = jnp.concatenate([c_ref[...]] * rep, axis=-1)
        s = jnp.concatenate([s_ref[...]] * rep, axis=-1)
        out = xn * c + _qk_tiles(xn, rm_ref[...]) * s
        q_ref[...] = out[:, :Q_W]
        k_ref[...] = out[:, Q_W:]

    full = pl.BlockSpec((_QK_TILE, _QK_TILE), lambda i: (0, 0))
    tab = pl.BlockSpec((tm, 128), lambda i: (i, 0))
    return pl.pallas_call(
        body, name=name, grid=(t // tm,),
        in_specs=[pl.BlockSpec((tm, QK_W), lambda i: (i, 0)), pl.BlockSpec((1, QK_W), lambda i: (0, 0)), tab, tab,
                  full, full],
        out_specs=(pl.BlockSpec((tm, Q_W), lambda i: (i, 0)), pl.BlockSpec((tm, KV_W), lambda i: (i, 0))),
        out_shape=(jax.ShapeDtypeStruct((t, Q_W), F32), jax.ShapeDtypeStruct((t, KV_W), F32)),
        compiler_params=_cp("parallel"))(proj, gain, c2, s2, gmat, rmat)


def qk_prep_bwd(proj, q_gain, k_gain, cosf, sinf, dq, dk, name, tm=256):
    t = proj.shape[0]
    gmat, rmat = _qk_mats()
    gain, c2, s2 = _qk_rows(q_gain, k_gain, cosf, sinf)
    rep = QK_W // 128
    lanes = np.arange(QK_W)[:, None]
    fold = jnp.asarray(lanes % HEAD_DIM + np.where(lanes >= Q_W, HEAD_DIM, 0) == np.arange(128)[None, :], F32)

    def body(p_ref, g_ref, c_ref, s_ref, gm_ref, rm_ref, f_ref, dq_ref, dk_ref, o_ref, dg_ref):
        x = p_ref[...]
        r = lax.rsqrt(_qk_tiles(x * x, gm_ref[...]) * (1.0 / HEAD_DIM) + EPS)
        xh = x * r
        c = jnp.concatenate([c_ref[...]] * rep, axis=-1)
        s = jnp.concatenate([s_ref[...]] * rep, axis=-1)
        dout = jnp.concatenate([dq_ref[...], dk_ref[...]], axis=-1)
        dxn = dout * c + _qk_tiles(dout * s, rm_ref[...], transposed=True)
        part = _hi(jnp.sum(dxn * xh, axis=0, keepdims=True), f_ref[...])
        dxh = dxn * g_ref[...]
        mean = _qk_tiles(dxh * xh, gm_ref[...]) * (1.0 / HEAD_DIM)
        o_ref[...] = (r * (dxh - xh * mean)).astype(o_ref.dtype)
        dg_ref[...] = jnp.where(pl.program_id(0) == 0, part, dg_ref[...] + part)

    full = pl.BlockSpec((_QK_TILE, _QK_TILE), lambda i: (0, 0))
    tab = pl.BlockSpec((tm, 128), lambda i: (i, 0))
    dqk, dg = pl.pallas_call(
        body, name=name, grid=(t // tm,),
        in_specs=[pl.BlockSpec((tm, QK_W), lambda i: (i, 0)), pl.BlockSpec((1, QK_W), lambda i: (0, 0)), tab, tab,
                  full, full, pl.BlockSpec((QK_W, 128), lambda i: (0, 0)),
                  pl.BlockSpec((tm, Q_W), lambda i: (i, 0)), pl.BlockSpec((tm, KV_W), lambda i: (i, 0))],
        out_specs=(pl.BlockSpec((tm, QK_W), lambda i: (i, 0)), pl.BlockSpec((1, 128), lambda i: (0, 0))),
        out_shape=(jax.ShapeDtypeStruct((t, QK_W), MXU_DTYPE), jax.ShapeDtypeStruct((1, 128), F32)),
        compiler_params=_cp("arbitrary"))(proj, gain, c2, s2, gmat, rmat, fold, dq, dk)
    return dqk, dg[:, :HEAD_DIM], dg[:, HEAD_DIM:]


def _swa_valid(n, grp):
    qi = lax.broadcasted_iota(jnp.int32, (grp * ATTN_BLOCK, 2 * ATTN_BLOCK), 0) & (ATTN_BLOCK - 1)
    kj = lax.broadcasted_iota(jnp.int32, (grp * ATTN_BLOCK, 2 * ATTN_BLOCK), 1)
    diff = qi + ATTN_BLOCK - kj
    return (diff >= 0) & (diff < ATTN_BLOCK) & (n * ATTN_BLOCK - ATTN_BLOCK + kj >= 0)


def _stack_heads(ref, g, grp):
    return jnp.concatenate([ref[:, (g * grp + j) * HEAD_DIM:(g * grp + j + 1) * HEAD_DIM] for j in range(grp)], axis=0)


def _stack_sinks(s_ref, g, grp):
    return jnp.concatenate([jnp.zeros((ATTN_BLOCK, 1), F32) + s_ref[0:1, g * grp + j:g * grp + j + 1]
                            for j in range(grp)], axis=0)


def swa_fwd(q, k, proj, sinks, name):
    t = q.shape[0]
    nb = t // ATTN_BLOCK
    scale = HEAD_DIM ** -0.5
    grp = ATTN_HEADS // KV_HEADS

    def body(q_ref, kc_ref, kp_ref, vc_ref, vp_ref, s_ref, y_ref, yt_ref, lse_ref):
        n = pl.program_id(0)
        valid = _swa_valid(n, grp)
        kk = jnp.concatenate([kp_ref[...], kc_ref[...]], axis=0).astype(MXU_DTYPE)
        vv = jnp.concatenate([vp_ref[...], vc_ref[...]], axis=0).astype(MXU_DTYPE)
        lane = lax.broadcasted_iota(jnp.int32, (ATTN_BLOCK, ATTN_HEADS), 1)
        gs = range(KV_HEADS)
        qg = [_stack_heads(q_ref, g, grp) for g in gs]
        sink = [_stack_sinks(s_ref, g, grp) for g in gs]
        sc = [jnp.where(valid, _mx_nt(qg[g], kk[:, g * HEAD_DIM:(g + 1) * HEAD_DIM]) * scale, NEG) for g in gs]
        m = [jnp.maximum(jnp.max(sc[g], axis=-1, keepdims=True), sink[g]) for g in gs]
        e = [jnp.exp(sc[g] - m[g]) for g in gs]
        den = [jnp.sum(e[g], axis=-1, keepdims=True) + jnp.exp(sink[g] - m[g]) for g in gs]
        og = [_mx(e[g] / den[g], vv[:, g * HEAD_DIM:(g + 1) * HEAD_DIM]) for g in gs]
        lg = [m[g] + jnp.log(den[g]) for g in gs]
        lse = jnp.zeros((ATTN_BLOCK, ATTN_HEADS), F32)
        outs = []
        for h in range(ATTN_HEADS):
            rows = slice((h % grp) * ATTN_BLOCK, (h % grp + 1) * ATTN_BLOCK)
            outs.append(og[h // grp][rows])
            lse = jnp.where(lane == h, lg[h // grp][rows], lse)
        y = jnp.concatenate(outs, axis=-1)
        y_ref[...] = y
        yt_ref[...] = y.T.astype(yt_ref.dtype)
        lse_ref[...] = lse

    cur = lambda n: (n, 0)
    prev = lambda n: (jnp.maximum(n - 1, 0), 0)
    vcol = (Q_W + KV_W) // KV_W
    return pl.pallas_call(
        body, name=name, grid=(nb,),
        in_specs=[pl.BlockSpec((ATTN_BLOCK, Q_W), cur), pl.BlockSpec((ATTN_BLOCK, KV_W), cur),
                  pl.BlockSpec((ATTN_BLOCK, KV_W), prev),
                  pl.BlockSpec((ATTN_BLOCK, KV_W), lambda n: (n, vcol)),
                  pl.BlockSpec((ATTN_BLOCK, KV_W), lambda n: (jnp.maximum(n - 1, 0), vcol)),
                  pl.BlockSpec((1, ATTN_HEADS), lambda n: (0, 0))],
        out_specs=(pl.BlockSpec((ATTN_BLOCK, Q_W), cur), pl.BlockSpec((Q_W, ATTN_BLOCK), lambda n: (0, n)),
                   pl.BlockSpec((ATTN_BLOCK, ATTN_HEADS), cur)),
        out_shape=(jax.ShapeDtypeStruct((t, Q_W), F32), jax.ShapeDtypeStruct((Q_W, t), MXU_DTYPE),
                   jax.ShapeDtypeStruct((t, ATTN_HEADS), F32)),
        compiler_params=_cp("parallel"))(q, k, k, proj, proj, sinks)


def swa_bwd(q, k, proj, sinks, y, lse, dmix, name):
    t = q.shape[0]
    nb = t // ATTN_BLOCK
    scale = HEAD_DIM ** -0.5
    grp = ATTN_HEADS // KV_HEADS

    def body(q_ref, kc_ref, kp_ref, vc_ref, vp_ref, s_ref, y_ref, lse_ref, dy_ref,
             dq_ref, dk_ref, dv_ref, ds_ref, dkc, dvc):
        n = pl.program_id(0)

        @pl.when(n == 0)
        def _():
            dkc[...] = jnp.zeros_like(dkc)
            dvc[...] = jnp.zeros_like(dvc)
            ds_ref[...] = jnp.zeros_like(ds_ref)

        @pl.when(n < nb)
        def _():
            valid = _swa_valid(n, grp)
            kk = jnp.concatenate([kp_ref[...], kc_ref[...]], axis=0).astype(MXU_DTYPE)
            vv = jnp.concatenate([vp_ref[...], vc_ref[...]], axis=0).astype(MXU_DTYPE)
            lane = lax.broadcasted_iota(jnp.int32, (1, ATTN_HEADS), 1)
            gs = range(KV_HEADS)
            kg = [kk[:, g * HEAD_DIM:(g + 1) * HEAD_DIM] for g in gs]
            vg = [vv[:, g * HEAD_DIM:(g + 1) * HEAD_DIM] for g in gs]
            qg = [_stack_heads(q_ref, g, grp).astype(MXU_DTYPE) for g in gs]
            dog = [_stack_heads(dy_ref, g, grp) for g in gs]
            og = [_stack_heads(y_ref, g, grp) for g in gs]
            lg = [jnp.concatenate([lse_ref[:, g * grp + j:g * grp + j + 1] for j in range(grp)], axis=0) for g in gs]
            sink = [_stack_sinks(s_ref, g, grp) for g in gs]
            sc = [jnp.where(valid, _mx_nt(qg[g], kg[g]) * scale, NEG) for g in gs]
            p = [jnp.exp(sc[g] - lg[g]) for g in gs]
            delta = [jnp.sum(dog[g] * og[g], axis=-1, keepdims=True) for g in gs]
            ds = [p[g] * (_mx_nt(dog[g], vg[g]) - delta[g]) for g in gs]
            dqg = [_mx(ds[g], kg[g]) * scale for g in gs]
            dkf = jnp.concatenate([_mx_tn(ds[g], qg[g]) * scale for g in gs], axis=-1)
            dvf = jnp.concatenate([_mx_tn(p[g], dog[g]) for g in gs], axis=-1)
            dsk = [jnp.exp(sink[g] - lg[g]) * delta[g] for g in gs]
            dsink = jnp.zeros((1, ATTN_HEADS), F32)
            dqs = []
            for h in range(ATTN_HEADS):
                rows = slice((h % grp) * ATTN_BLOCK, (h % grp + 1) * ATTN_BLOCK)
                dqs.append(dqg[h // grp][rows])
                dsink = jnp.where(lane == h, -jnp.sum(dsk[h // grp][rows], axis=0, keepdims=True), dsink)
            dq_ref[...] = jnp.concatenate(dqs, axis=-1)
            dk_ref[...] = dkc[...] + dkf[:ATTN_BLOCK]
            dv_ref[...] = (dvc[...] + dvf[:ATTN_BLOCK]).astype(dv_ref.dtype)
            dkc[...] = dkf[ATTN_BLOCK:]
            dvc[...] = dvf[ATTN_BLOCK:]
            ds_ref[...] += dsink

        @pl.when(n == nb)
        def _():
            dk_ref[...] = dkc[...]
            dv_ref[...] = dvc[...].astype(dv_ref.dtype)

    cur = lambda n: (jnp.minimum(n, nb - 1), 0)
    prev = lambda n: (jnp.clip(n - 1, 0, nb - 1), 0)
    vcol = (Q_W + KV_W) // KV_W
    return pl.pallas_call(
        body, name=name, grid=(nb + 1,),
        in_specs=[pl.BlockSpec((ATTN_BLOCK, Q_W), cur), pl.BlockSpec((ATTN_BLOCK, KV_W), cur),
                  pl.BlockSpec((ATTN_BLOCK, KV_W), prev),
                  pl.BlockSpec((ATTN_BLOCK, KV_W), lambda n: (jnp.minimum(n, nb - 1), vcol)),
                  pl.BlockSpec((ATTN_BLOCK, KV_W), lambda n: (jnp.clip(n - 1, 0, nb - 1), vcol)),
                  pl.BlockSpec((1, ATTN_HEADS), lambda n: (0, 0)),
                  pl.BlockSpec((ATTN_BLOCK, Q_W), cur), pl.BlockSpec((ATTN_BLOCK, ATTN_HEADS), cur),
                  pl.BlockSpec((ATTN_BLOCK, Q_W), cur)],
        out_specs=(pl.BlockSpec((ATTN_BLOCK, Q_W), cur), pl.BlockSpec((ATTN_BLOCK, KV_W), prev),
                   pl.BlockSpec((ATTN_BLOCK, KV_W), prev), pl.BlockSpec((1, ATTN_HEADS), lambda n: (0, 0))),
        out_shape=(jax.ShapeDtypeStruct((t, Q_W), F32), jax.ShapeDtypeStruct((t, KV_W), F32),
                   jax.ShapeDtypeStruct((t, KV_W), MXU_DTYPE), jax.ShapeDtypeStruct((1, ATTN_HEADS), F32)),
        scratch_shapes=[pltpu.VMEM((ATTN_BLOCK, KV_W), F32), pltpu.VMEM((ATTN_BLOCK, KV_W), F32)],
        compiler_params=_cp("arbitrary"))(q, k, k, proj, proj, sinks, y, lse, dmix)


GC_W = 256
_GB0, _GC0, _XI0 = 768 // GC_W, 1280 // GC_W, 1792 // GC_W
HALO = 8


def gconv_fwd(proj, conv_w, name, tm=512):
    t = proj.shape[0]
    hb = tm // HALO

    def body(gb_ref, gc_ref, xi_ref, gch_ref, xih_ref, w_ref, y_ref, yt_ref):
        i = pl.program_id(1)
        u = gc_ref[...] * xi_ref[...]
        uh = jnp.where(i == 0, 0.0, gch_ref[...] * xih_ref[...])
        up = jnp.concatenate([uh, u], axis=0)
        cv = w_ref[0:1, :] * up[HALO - 2:HALO - 2 + tm]
        cv = cv + w_ref[1:2, :] * up[HALO - 1:HALO - 1 + tm]
        cv = cv + w_ref[2:3, :] * u
        y = gb_ref[...] * cv
        y_ref[...] = y.astype(y_ref.dtype)
        yt_ref[...] = y.T.astype(yt_ref.dtype)

    def col(c0):
        return pl.BlockSpec((tm, GC_W), lambda cj, i: (i, c0 + cj))

    def halo(c0):
        return pl.BlockSpec((HALO, GC_W), lambda cj, i: (jnp.maximum(i * hb - 1, 0), c0 + cj))

    return pl.pallas_call(
        body, name=name, grid=(CONV_CH // GC_W, t // tm),
        in_specs=[col(_GB0), col(_GC0), col(_XI0), halo(_GC0), halo(_XI0),
                  pl.BlockSpec((3, GC_W), lambda cj, i: (0, cj))],
        out_specs=(pl.BlockSpec((tm, GC_W), lambda cj, i: (i, cj)), pl.BlockSpec((GC_W, tm), lambda cj, i: (cj, i))),
        out_shape=(jax.ShapeDtypeStruct((t, CONV_CH), MXU_DTYPE), jax.ShapeDtypeStruct((CONV_CH, t), MXU_DTYPE)),
        compiler_params=_cp("parallel", "parallel"))(proj, proj, proj, proj, proj, conv_w)


def gconv_bwd(proj, conv_w, dmix, name, tm=512):
    t = proj.shape[0]
    hb = tm // HALO
    nt = t // tm
    dy0 = Q_W // GC_W

    def body(gb_ref, gc_ref, xi_ref, gch_ref, xih_ref, gbn_ref, dyn_ref, dy_ref, w_ref,
             dgb_ref, dgc_ref, dxi_ref, dw_ref):
        i = pl.program_id(1)
        gc, xi, gb, dy = gc_ref[...], xi_ref[...], gb_ref[...], dy_ref[...]
        u = gc * xi
        uh = jnp.where(i == 0, 0.0, gch_ref[...] * xih_ref[...])
        up = jnp.concatenate([uh, u], axis=0)
        u2 = up[HALO - 2:HALO - 2 + tm]
        u1 = up[HALO - 1:HALO - 1 + tm]
        cv = w_ref[0:1, :] * u2 + w_ref[1:2, :] * u1 + w_ref[2:3, :] * u
        dgb_ref[...] = (dy * cv).astype(dgb_ref.dtype)
        dcv = dy * gb
        dcvn = jnp.where(i == nt - 1, 0.0, dyn_ref[...] * gbn_ref[...])
        dcvp = jnp.concatenate([dcv, dcvn], axis=0)
        du = w_ref[0:1, :] * dcvp[2:2 + tm] + w_ref[1:2, :] * dcvp[1:1 + tm] + w_ref[2:3, :] * dcv
        dgc_ref[...] = (du * xi).astype(dgc_ref.dtype)
        dxi_ref[...] = (du * gc).astype(dxi_ref.dtype)
        dw = jnp.concatenate([jnp.sum(dcv * u2, axis=0, keepdims=True), jnp.sum(dcv * u1, axis=0, keepdims=True),
                              jnp.sum(dcv * u, axis=0, keepdims=True)], axis=0)

        @pl.when(i == 0)
        def _():
            dw_ref[...] = dw

        @pl.when(i > 0)
        def _():
            dw_ref[...] += dw

    def col(c0):
        return pl.BlockSpec((tm, GC_W), lambda cj, i: (i, c0 + cj))

    def halo(c0):
        return pl.BlockSpec((HALO, GC_W), lambda cj, i: (jnp.maximum(i * hb - 1, 0), c0 + cj))

    def nxt(c0):
        return pl.BlockSpec((HALO, GC_W), lambda cj, i: (jnp.minimum((i + 1) * hb, t // HALO - 1), c0 + cj))

    out = pl.BlockSpec((tm, GC_W), lambda cj, i: (i, cj))
    return pl.pallas_call(
        body, name=name, grid=(CONV_CH // GC_W, nt),
        in_specs=[col(_GB0), col(_GC0), col(_XI0), halo(_GC0), halo(_XI0), nxt(_GB0), nxt(dy0), col(dy0),
                  pl.BlockSpec((3, GC_W), lambda cj, i: (0, cj))],
        out_specs=(out, out, out, pl.BlockSpec((3, GC_W), lambda cj, i: (0, cj))),
        out_shape=(jax.ShapeDtypeStruct((t, CONV_CH), MXU_DTYPE),) * 3 + (jax.ShapeDtypeStruct((3, CONV_CH), F32),),
        compiler_params=_cp("parallel", "arbitrary"))(proj, proj, proj, proj, proj, proj, dmix, dmix, conv_w)


_QKV_W = 3 * DN_W
_BA_COL = (4 * DN_W) // 128
_Z_COL = _QKV_W // DN_W


def gdn_prep_fwd(proj, conv_w, alog_row, dtb_row, name, tm=256):
    t = proj.shape[0]
    hb = tm // HALO
    qscale = DN_DIM ** -0.5

    def body(x_ref, xh_ref, w_ref, ba_ref, al_ref, dt_ref, q_ref, k_ref, v_ref, bg_ref):
        i = pl.program_id(0)
        for gi in range(3 * DN_HEADS):
            sl = slice(gi * DN_DIM, (gi + 1) * DN_DIM)
            xp = jnp.concatenate([jnp.where(i == 0, 0.0, xh_ref[:, sl]), x_ref[:, sl]], axis=0)
            c = w_ref[0:1, sl] * xp[HALO - 3:HALO - 3 + tm]
            for j in range(1, 4):
                c = c + w_ref[j:j + 1, sl] * xp[HALO - 3 + j:HALO - 3 + j + tm]
            s = c * _sigmoid(c)
            osl = slice((gi % DN_HEADS) * DN_DIM, (gi % DN_HEADS + 1) * DN_DIM)
            if gi < DN_HEADS:
                q_ref[:, osl] = s * lax.rsqrt(jnp.sum(s * s, axis=-1, keepdims=True) + EPS) * qscale
            elif gi < 2 * DN_HEADS:
                k_ref[:, osl] = s * lax.rsqrt(jnp.sum(s * s, axis=-1, keepdims=True) + EPS)
            else:
                v_ref[:, osl] = s
        ba = ba_ref[...]
        lane = lax.broadcasted_iota(jnp.int32, ba.shape, 1)
        gval = -jnp.exp(al_ref[...]) * _softplus(ba + dt_ref[...])
        bg_ref[...] = jnp.where(lane < DN_HEADS, _sigmoid(ba), jnp.where(lane < 2 * DN_HEADS, gval, 0.0))

    row = pl.BlockSpec((tm, DN_W), lambda i: (i, 0))
    one = pl.BlockSpec((1, 128), lambda i: (0, 0))
    return pl.pallas_call(
        body, name=name, grid=(t // tm,),
        in_specs=[pl.BlockSpec((tm, _QKV_W), lambda i: (i, 0)),
                  pl.BlockSpec((HALO, _QKV_W), lambda i: (jnp.maximum(i * hb - 1, 0), 0)),
                  pl.BlockSpec((4, _QKV_W), lambda i: (0, 0)),
                  pl.BlockSpec((tm, 128), lambda i: (i, _BA_COL)), one, one],
        out_specs=(row, row, row, pl.BlockSpec((tm, 128), lambda i: (i, 0))),
        out_shape=(jax.ShapeDtypeStruct((t, DN_W), F32),) * 3 + (jax.ShapeDtypeStruct((t, 128), F32),),
        compiler_params=_cp("parallel"))(proj, proj, conv_w, proj, alog_row, dtb_row)


def gdn_prep_bwd(proj, conv_w, alog_row, dtb_row, dq, dk, dv, dbg, name, tm=256):
    t = proj.shape[0]
    hb = tm // HALO
    nt = t // tm
    qscale = DN_DIM ** -0.5
    te = tm + HALO

    def body(x_ref, xh_ref, xn_ref, w_ref, ba_ref, al_ref, dt_ref, dq_ref, dk_ref, dv_ref,
             dqn_ref, dkn_ref, dvn_ref, dbg_ref, dx_ref, dba_ref, dw_ref, ddt_ref, dal_ref):
        i = pl.program_id(0)
        first = i == 0
        last = i == nt - 1
        dws = []
        for gi in range(3 * DN_HEADS):
            sl = slice(gi * DN_DIM, (gi + 1) * DN_DIM)
            osl = slice((gi % DN_HEADS) * DN_DIM, (gi % DN_HEADS + 1) * DN_DIM)
            xe = jnp.concatenate([jnp.where(first, 0.0, xh_ref[:, sl]), x_ref[:, sl], xn_ref[:, sl]], axis=0)
            c = w_ref[0:1, sl] * xe[HALO - 3:HALO - 3 + te]
            for j in range(1, 4):
                c = c + w_ref[j:j + 1, sl] * xe[HALO - 3 + j:HALO - 3 + j + te]
            sg = _sigmoid(c)
            s = c * sg
            d_ref, dn_ref = ((dq_ref, dqn_ref), (dk_ref, dkn_ref), (dv_ref, dvn_ref))[gi // DN_HEADS]
            dy = jnp.concatenate([d_ref[:, osl], jnp.where(last, 0.0, dn_ref[:, osl])], axis=0)
            if gi < 2 * DN_HEADS:
                r = lax.rsqrt(jnp.sum(s * s, axis=-1, keepdims=True) + EPS)
                sh = s * r
                ds = r * (dy - sh * jnp.sum(sh * dy, axis=-1, keepdims=True))
                if gi < DN_HEADS:
                    ds = ds * qscale
            else:
                ds = dy
            dc = ds * sg * (1.0 + c * (1.0 - sg))
            dcs = [dc[3 - j:3 - j + tm] for j in range(4)]
            dx = w_ref[0:1, sl] * dcs[0]
            for j in range(1, 4):
                dx = dx + w_ref[j:j + 1, sl] * dcs[j]
            dx_ref[:, sl] = dx.astype(dx_ref.dtype)
            x0 = x_ref[:, sl]
            dws.append(jnp.concatenate([jnp.sum(dcs[j] * x0, axis=0, keepdims=True) for j in range(4)], axis=0))
        dw = jnp.concatenate(dws, axis=-1)
        ba = ba_ref[...]
        dbgv = dbg_ref[...]
        lane = lax.broadcasted_iota(jnp.int32, ba.shape, 1)
        beta = _sigmoid(ba)
        ea = -jnp.exp(al_ref[...])
        zin = ba + dt_ref[...]
        is_b = lane < DN_HEADS
        is_a = (lane >= DN_HEADS) & (lane < 2 * DN_HEADS)
        da = jnp.where(is_a, dbgv * ea * _sigmoid(zin), 0.0)
        dba_ref[...] = jnp.where(is_b, dbgv * beta * (1.0 - beta), da).astype(dba_ref.dtype)
        ddt = jnp.sum(da, axis=0, keepdims=True)
        dal = jnp.sum(jnp.where(is_a, dbgv * ea * _softplus(zin), 0.0), axis=0, keepdims=True)

        @pl.when(first)
        def _():
            dw_ref[...] = dw
            ddt_ref[...] = ddt
            dal_ref[...] = dal

        @pl.when(i > 0)
        def _():
            dw_ref[...] += dw
            ddt_ref[...] += ddt
            dal_ref[...] += dal

    row = pl.BlockSpec((tm, DN_W), lambda i: (i, 0))
    nrow = pl.BlockSpec((HALO, DN_W), lambda i: (jnp.minimum((i + 1) * hb, t // HALO - 1), 0))
    one = pl.BlockSpec((1, 128), lambda i: (0, 0))
    return pl.pallas_call(
        body, name=name, grid=(nt,),
        in_specs=[pl.BlockSpec((tm, _QKV_W), lambda i: (i, 0)),
                  pl.BlockSpec((HALO, _QKV_W), lambda i: (jnp.maximum(i * hb - 1, 0), 0)),
                  pl.BlockSpec((HALO, _QKV_W), lambda i: (jnp.minimum((i + 1) * hb, t // HALO - 1), 0)),
                  pl.BlockSpec((4, _QKV_W), lambda i: (0, 0)),
                  pl.BlockSpec((tm, 128), lambda i: (i, _BA_COL)), one, one,
                  row, row, row, nrow, nrow, nrow, pl.BlockSpec((tm, 128), lambda i: (i, 0))],
        out_specs=(pl.BlockSpec((tm, _QKV_W), lambda i: (i, 0)), pl.BlockSpec((tm, 128), lambda i: (i, 0)),
                   pl.BlockSpec((4, _QKV_W), lambda i: (0, 0)), one, one),
        out_shape=(jax.ShapeDtypeStruct((t, _QKV_W), MXU_DTYPE), jax.ShapeDtypeStruct((t, 128), MXU_DTYPE),
                   jax.ShapeDtypeStruct((4, _QKV_W), F32), jax.ShapeDtypeStruct((1, 128), F32),
                   jax.ShapeDtypeStruct((1, 128), F32)),
        compiler_params=_cp("arbitrary"))(proj, proj, proj, conv_w, proj, alog_row, dtb_row, dq, dk, dv, dq, dk, dv, dbg)


def _chunk_masks():
    r = lax.broadcasted_iota(jnp.int32, (DN_CHUNK, DN_CHUNK), 0)
    c = lax.broadcasted_iota(jnp.int32, (DN_CHUNK, DN_CHUNK), 1)
    return r >= c, r > c


INV_PACK = 2


def _inv_unit_lower_many(mats):
    n = DN_CHUNK
    wide = INV_PACK * n
    r = lax.broadcasted_iota(jnp.int32, (wide, wide), 0)
    c = lax.broadcasted_iota(jnp.int32, (wide, wide), 1)
    same = (r // n) == (c // n)
    eye = jnp.where((r[:n] == (c[:n] % n)), 1.0, 0.0)

    def blockdiag(row):
        return jnp.where(same, jnp.concatenate([row] * INV_PACK, axis=0), 0.0)

    packs = [jnp.concatenate(mats[g:g + INV_PACK], axis=-1) for g in range(0, len(mats), INV_PACK)]
    xs = [eye - a for a in packs]
    pws = [_hi(a, blockdiag(a)) for a in packs]
    for step in range(5):
        if step < 4:
            both = [_hi(jnp.concatenate([x, pw], axis=0), blockdiag(pw)) for x, pw in zip(xs, pws)]
            xs = [x + b[:n] for x, b in zip(xs, both)]
            pws = [b[n:] for b in both]
        else:
            xs = [x + _hi(x, blockdiag(pw)) for x, pw in zip(xs, pws)]
    return [x[:, j * n:(j + 1) * n] for x in xs for j in range(INV_PACK)]


def _chunk_common(q, k, v, beta, gc, gcr, lower, strict):
    gam = jnp.exp(jnp.where(lower, gc - gcr, NEG))
    eg = jnp.exp(gc)
    gl = gc[DN_CHUNK - 1:DN_CHUNK, :]
    kdf = jnp.exp(gl - gc)
    kb = k * beta
    bmat = _mx_nt(kb, k)
    qmat = _mx_nt(q, k)
    return gam, eg, jnp.exp(gl), kdf, kb, bmat, qmat


def gdn_fwd(q, k, v, bg, name):
    t = q.shape[0]
    n_chunks = t // DN_CHUNK

    def body(q_ref, k_ref, v_ref, bg_ref, o_ref, sall_ref, tall_ref, s_ref):
        n = pl.program_id(0)

        @pl.when(n == 0)
        def _():
            s_ref[...] = jnp.zeros_like(s_ref)

        lower, strict = _chunk_masks()
        bgv = bg_ref[...]
        gcs = _hi(jnp.where(lower, 1.0, 0.0), bgv)
        gcs_t = gcs.T
        hs = range(DN_HEADS)
        sl = [slice(h * DN_DIM, (h + 1) * DN_DIM) for h in hs]
        st = [s_ref[h] for h in hs]
        for h in hs:
            sall_ref[0, h] = st[h]
        qh = [q_ref[:, sl[h]] for h in hs]
        kh = [k_ref[:, sl[h]] for h in hs]
        vh = [v_ref[:, sl[h]] for h in hs]
        beta = [bgv[:, h:h + 1] for h in hs]
        com = [_chunk_common(qh[h], kh[h], vh[h], beta[h], gcs[:, DN_HEADS + h:DN_HEADS + h + 1],
                             gcs_t[DN_HEADS + h:DN_HEADS + h + 1, :], lower, strict) for h in hs]
        gam, eg, dec, kdf, kb, bmat, qmat = zip(*com)
        tms = _inv_unit_lower_many([jnp.where(strict, bmat[h] * gam[h], 0.0) for h in hs])
        for h in hs:
            tall_ref[0, h] = tms[h]
        uw = [_hi(tms[h], jnp.concatenate([vh[h] * beta[h], kb[h] * eg[h]], axis=-1)) for h in hs]
        v_new = [uw[h][:, :DN_DIM] - _mx(uw[h][:, DN_DIM:], st[h]) for h in hs]
        o_st = [_mx(qh[h] * eg[h], st[h]) for h in hs]
        o_in = [_mx(qmat[h] * gam[h], v_new[h]) for h in hs]
        s_up = [_mx_tn(kh[h] * kdf[h], v_new[h]) for h in hs]
        for h in hs:
            o_ref[:, sl[h]] = o_st[h] + o_in[h]
            s_ref[h] = st[h] * dec[h] + s_up[h]

    row = pl.BlockSpec((DN_CHUNK, DN_W), lambda n: (n, 0))
    return pl.pallas_call(
        body, name=name, grid=(n_chunks,),
        in_specs=[row, row, row, pl.BlockSpec((DN_CHUNK, 128), lambda n: (n, 0))],
        out_specs=(row, pl.BlockSpec((1, DN_HEADS, DN_DIM, DN_DIM), lambda n: (n, 0, 0, 0)),
                   pl.BlockSpec((1, DN_HEADS, DN_CHUNK, DN_CHUNK), lambda n: (n, 0, 0, 0))),
        out_shape=(jax.ShapeDtypeStruct((t, DN_W), F32),
                   jax.ShapeDtypeStruct((n_chunks, DN_HEADS, DN_DIM, DN_DIM), F32),
                   jax.ShapeDtypeStruct((n_chunks, DN_HEADS, DN_CHUNK, DN_CHUNK), F32)),
        scratch_shapes=[pltpu.VMEM((DN_HEADS, DN_DIM, DN_DIM), F32)],
        compiler_params=_cp("arbitrary"))(q, k, v, bg)


def gdn_bwd(q, k, v, bg, sall, tall, do, name):
    t = q.shape[0]
    n_chunks = t // DN_CHUNK

    def body(q_ref, k_ref, v_ref, bg_ref, sall_ref, tall_ref, do_ref, dq_ref, dk_ref, dv_ref, dbg_ref, ds_ref):
        n = pl.program_id(0)

        @pl.when(n == 0)
        def _():
            ds_ref[...] = jnp.zeros_like(ds_ref)

        lower, strict = _chunk_masks()
        ltri = jnp.where(lower, 1.0, 0.0)
        bgv = bg_ref[...]
        gcs = _hi(ltri, bgv)
        gcs_t = gcs.T
        lane = lax.broadcasted_iota(jnp.int32, (DN_CHUNK, 128), 1)
        rowi = lax.broadcasted_iota(jnp.int32, (DN_CHUNK, 1), 0)
        hs = range(DN_HEADS)
        each = lambda fn, *ls: [fn(*a) for a in zip(*ls)]
        rsum = lambda a: jnp.sum(a, axis=-1, keepdims=True)
        sl = [slice(h * DN_DIM, (h + 1) * DN_DIM) for h in hs]
        st = [sall_ref[0, h] for h in hs]
        tms = [tall_ref[0, h] for h in hs]
        dsn = [ds_ref[h] for h in hs]
        qh = [q_ref[:, sl[h]] for h in hs]
        kh = [k_ref[:, sl[h]] for h in hs]
        vh = [v_ref[:, sl[h]] for h in hs]
        doh = [do_ref[:, sl[h]] for h in hs]
        beta = [bgv[:, h:h + 1] for h in hs]
        com = [_chunk_common(qh[h], kh[h], vh[h], beta[h], gcs[:, DN_HEADS + h:DN_HEADS + h + 1],
                             gcs_t[DN_HEADS + h:DN_HEADS + h + 1, :], lower, strict) for h in hs]
        gam, eg, dec, kdf, kb, bmat, qmat = zip(*com)
        rhs_w = each(lambda a, b: a * b, kb, eg)
        uw = each(lambda t_, v_, b_, r_: _hi(t_, jnp.concatenate([v_ * b_, r_], axis=-1)), tms, vh, beta, rhs_w)
        qd = each(lambda a, b: a * b, qh, eg)
        kd = each(lambda a, b: a * b, kh, kdf)
        pmat = each(lambda a, b: a * b, qmat, gam)
        v_new = each(lambda uw_, s_: uw_[:, :DN_DIM] - _mx(uw_[:, DN_DIM:], s_), uw, st)
        dqd = each(_mx_nt, doh, st)
        ds_o = each(_mx_tn, qd, doh)
        dp = each(lambda d_, v_: jnp.where(lower, _mx_nt(d_, v_), 0.0), doh, v_new)
        dvn_o = each(_mx_tn, pmat, doh)
        ddec = each(lambda d_, s_: jnp.sum(rsum(d_ * s_), axis=0, keepdims=True), dsn, st)
        dkd = each(_mx_nt, v_new, dsn)
        dvn = each(lambda a, k_, d_: a + _mx(k_, d_), dvn_o, kd, dsn)
        dw = each(lambda d_, s_: -_mx_nt(d_, s_), dvn, st)
        ds_w = each(lambda uw_, d_: _mx_tn(uw_[:, DN_DIM:], d_), uw, dvn)
        for h in hs:
            ds_ref[h] = ds_o[h] + dec[h] * dsn[h] - ds_w[h]
        dr = each(lambda t_, a, b: _hi_tn(t_, jnp.concatenate([a, b], axis=-1)), tms, dvn, dw)
        da = each(lambda r_, uw_: jnp.where(strict, -_hi_nt(r_, uw_), 0.0), dr, uw)
        dru = [r_[:, :DN_DIM] for r_ in dr]
        drw = [r_[:, DN_DIM:] for r_ in dr]
        db = each(lambda a, b: a * b, da, gam)
        dq_m = each(lambda a, b: a * b, dp, gam)
        e = each(lambda a, bm, p_, qm, g_: (a * bm + p_ * qm) * g_, da, bmat, dp, qmat, gam)
        dkb = each(lambda b_, k_, r_, e_: _mx(b_, k_) + r_ * e_, db, kh, drw, eg)
        dk = each(lambda b_, kb_, m_, q_, d_, f_: _mx_tn(b_, kb_) + _mx_tn(m_, q_) + d_ * f_, db, kb, dq_m, qh, dkd, kdf)
        dq = each(lambda m_, k_, d_, e_: _mx(m_, k_) + d_ * e_, dq_m, kh, dqd, eg)
        tk = each(lambda a, b: rsum(a * b), dkd, kd)
        dbeta_all = jnp.zeros((DN_CHUNK, 128), F32)
        dgc_all = jnp.zeros((DN_CHUNK, 128), F32)
        for h in hs:
            dgc = (jnp.sum(e[h], axis=1, keepdims=True) - jnp.sum(e[h].T, axis=1, keepdims=True)
                   + rsum(dqd[h] * qd[h]) - tk[h] + rsum(drw[h] * rhs_w[h]))
            dgl = jnp.sum(tk[h], axis=0, keepdims=True) + ddec[h] * dec[h]
            dgc = dgc + jnp.where(rowi == DN_CHUNK - 1, dgl, 0.0)
            dbeta = rsum(dru[h] * vh[h]) + rsum(dkb[h] * kh[h])
            dq_ref[:, sl[h]] = dq[h]
            dk_ref[:, sl[h]] = dk[h] + dkb[h] * beta[h]
            dv_ref[:, sl[h]] = dru[h] * beta[h]
            dbeta_all = jnp.where(lane == h, dbeta, dbeta_all)
            dgc_all = jnp.where(lane == DN_HEADS + h, dgc, dgc_all)
        dbg_ref[...] = dbeta_all + _hi_tn(ltri, dgc_all)

    rev = lambda n: (n_chunks - 1 - n, 0)
    row = pl.BlockSpec((DN_CHUNK, DN_W), rev)
    small = pl.BlockSpec((DN_CHUNK, 128), rev)
    return pl.pallas_call(
        body, name=name, grid=(n_chunks,),
        in_specs=[row, row, row, small,
                  pl.BlockSpec((1, DN_HEADS, DN_DIM, DN_DIM), lambda n: (n_chunks - 1 - n, 0, 0, 0)),
                  pl.BlockSpec((1, DN_HEADS, DN_CHUNK, DN_CHUNK), lambda n: (n_chunks - 1 - n, 0, 0, 0)), row],
        out_specs=(row, row, row, small),
        out_shape=(jax.ShapeDtypeStruct((t, DN_W), F32),) * 3 + (jax.ShapeDtypeStruct((t, 128), F32),),
        scratch_shapes=[pltpu.VMEM((DN_HEADS, DN_DIM, DN_DIM), F32)],
        compiler_params=_cp("arbitrary"))(q, k, v, bg, sall, tall, do)


def gdn_out_fwd(o, proj, o_gain, name, tm=256):
    t = o.shape[0]

    def body(o_ref, z_ref, g_ref, y_ref, yt_ref):
        for h in range(DN_HEADS):
            sl = slice(h * DN_DIM, (h + 1) * DN_DIM)
            ov, zv = o_ref[:, sl], z_ref[:, sl]
            r = lax.rsqrt(jnp.mean(ov * ov, axis=-1, keepdims=True) + EPS)
            y = ov * r * g_ref[...] * (zv * _sigmoid(zv))
            y_ref[:, sl] = y.astype(y_ref.dtype)
            yt_ref[sl, :] = y.T.astype(yt_ref.dtype)

    row = pl.BlockSpec((tm, DN_W), lambda i: (i, 0))
    return pl.pallas_call(
        body, name=name, grid=(t // tm,),
        in_specs=[row, pl.BlockSpec((tm, DN_W), lambda i: (i, _Z_COL)), pl.BlockSpec((1, DN_DIM), lambda i: (0, 0))],
        out_specs=(row, pl.BlockSpec((DN_W, tm), lambda i: (0, i))),
        out_shape=(jax.ShapeDtypeStruct((t, DN_W), MXU_DTYPE), jax.ShapeDtypeStruct((DN_W, t), MXU_DTYPE)),
        compiler_params=_cp("parallel"))(o, proj, o_gain)


def gdn_out_bwd(o, proj, o_gain, dy, name, tm=256):
    t = o.shape[0]

    def body(o_ref, z_ref, g_ref, dy_ref, do_ref, dz_ref, dg_ref):
        i = pl.program_id(0)
        dg = jnp.zeros((1, DN_DIM), F32)
        for h in range(DN_HEADS):
            sl = slice(h * DN_DIM, (h + 1) * DN_DIM)
            ov, zv, dyv = o_ref[:, sl], z_ref[:, sl], dy_ref[:, sl]
            r = lax.rsqrt(jnp.mean(ov * ov, axis=-1, keepdims=True) + EPS)
            oh = ov * r
            sg = _sigmoid(zv)
            dz_ref[:, sl] = (dyv * oh * g_ref[...] * sg * (1.0 + zv * (1.0 - sg))).astype(dz_ref.dtype)
            don = dyv * (zv * sg)
            dg = dg + jnp.sum(don * oh, axis=0, keepdims=True)
            doh = don * g_ref[...]
            do_ref[:, sl] = r * (doh - oh * jnp.mean(doh * oh, axis=-1, keepdims=True))

        @pl.when(i == 0)
        def _():
            dg_ref[...] = dg

        @pl.when(i > 0)
        def _():
            dg_ref[...] += dg

    row = pl.BlockSpec((tm, DN_W), lambda i: (i, 0))
    one = pl.BlockSpec((1, DN_DIM), lambda i: (0, 0))
    return pl.pallas_call(
        body, name=name, grid=(t // tm,),
        in_specs=[row, pl.BlockSpec((tm, DN_W), lambda i: (i, _Z_COL)), one, row],
        out_specs=(row, row, one),
        out_shape=(jax.ShapeDtypeStruct((t, DN_W), F32), jax.ShapeDtypeStruct((t, DN_W), MXU_DTYPE),
                   jax.ShapeDtypeStruct((1, DN_DIM), F32)),
        compiler_params=_cp("arbitrary"))(o, proj, o_gain, dy)


def _peer(k):
    x, y, c = lax.axis_index("x"), lax.axis_index("y"), lax.axis_index("c")
    px = 1 - x if k & 4 else x
    py = 1 - y if k & 2 else y
    pc = 1 - c if k & 1 else c
    return (px, py, pc), 4 * px + 2 * py + pc


def all_gather(shards, name):
    na = len(shards)

    def body(*refs):
        ins, outs = refs[:na], refs[na:2 * na]
        send_sems, recv_sems, local_sems = refs[2 * na:]
        _, me = _peer(0)
        local = [pltpu.make_async_copy(ins[a], outs[a].at[me], local_sems.at[a]) for a in range(na)]
        for cp in local:
            cp.start()
        sends = []
        for k in range(1, N_DEV):
            peer, _ = _peer(k)
            for a in range(na):
                cp = pltpu.make_async_remote_copy(
                    src_ref=ins[a], dst_ref=outs[a].at[me], send_sem=send_sems.at[a, k - 1],
                    recv_sem=recv_sems.at[a, k - 1], device_id=peer, device_id_type=MESH)
                cp.start()
                sends.append(cp)
        for k in range(1, N_DEV):
            peer, pid = _peer(k)
            for a in range(na):
                pltpu.make_async_remote_copy(
                    src_ref=ins[a], dst_ref=outs[a].at[pid], send_sem=send_sems.at[a, k - 1],
                    recv_sem=recv_sems.at[a, k - 1], device_id=peer, device_id_type=MESH).wait_recv()
        for cp in sends:
            cp.wait_send()
        for cp in local:
            cp.wait()

    anyspec = pl.BlockSpec(memory_space=pl.ANY)
    return pl.pallas_call(
        body, name=name, in_specs=[anyspec] * na, out_specs=tuple([anyspec] * na),
        out_shape=tuple(jax.ShapeDtypeStruct((N_DEV,) + s.shape, s.dtype) for s in shards),
        scratch_shapes=[pltpu.SemaphoreType.DMA((na, N_DEV - 1)), pltpu.SemaphoreType.DMA((na, N_DEV - 1)),
                        pltpu.SemaphoreType.DMA((na,))],
        compiler_params=pltpu.CompilerParams(has_side_effects=True))(*shards)


_HBM = pl.BlockSpec(memory_space=pltpu.HBM)
_SEM = pl.BlockSpec(memory_space=pltpu.SEMAPHORE)
_DATAFLOW = pltpu.SideEffectType.DATAFLOW_SIDE_EFFECTING
N_PEER = N_DEV - 1


def send_start(srcs, name, scatter, after):
    na = len(srcs)
    ns = (2 * N_PEER + 1) * na
    lands = [lax.empty((N_DEV,) + (s.shape[1:] if scatter else s.shape), s.dtype) for s in srcs]
    extra = [] if after is None else [after]

    def body(*refs):
        src_refs, land_refs = refs[:na], refs[na:2 * na]
        sems = refs[2 * na + len(extra):2 * na + len(extra) + ns]
        land_out, token = refs[-1 - na:-1], refs[-1]
        _, me = _peer(0)
        for a in range(na):
            pltpu.make_async_copy(src_refs[a].at[me] if scatter else src_refs[a], land_out[a].at[me],
                                  sems[2 * N_PEER * na + a]).start()
        for k in range(1, N_DEV):
            peer, pid = _peer(k)
            for a in range(na):
                pltpu.make_async_remote_copy(
                    src_ref=src_refs[a].at[pid] if scatter else src_refs[a], dst_ref=land_refs[a].at[me],
                    send_sem=sems[2 * (a * N_PEER + k - 1)], recv_sem=sems[2 * (a * N_PEER + k - 1) + 1],
                    device_id=peer, device_id_type=MESH).start()
        token[...] = jnp.zeros_like(token)

    hbm = lambda arrs: tuple(pltpu.HBM(a.shape, a.dtype) for a in arrs)
    outs = pl.pallas_call(
        body, name=name,
        out_shape=(pltpu.SemaphoreType.DMA(()),) * ns + hbm(srcs) + hbm(lands) + (jax.ShapeDtypeStruct((8, 128), F32),),
        in_specs=[_HBM] * (2 * na) + [pl.BlockSpec(memory_space=pl.ANY)] * len(extra),
        out_specs=(_SEM,) * ns + (_HBM,) * (2 * na) + (pl.BlockSpec(memory_space=pltpu.VMEM),),
        input_output_aliases={i: ns + i for i in range(2 * na)},
        compiler_params=pltpu.CompilerParams(has_side_effects=_DATAFLOW),
    )(*[pltpu.with_memory_space_constraint(a, pltpu.HBM) for a in list(srcs) + lands], *extra)
    return outs[:ns], outs[ns:ns + na], outs[ns + na:ns + 2 * na], outs[-1]


def send_wait(sems, srcs_thru, lands_thru, name, scatter, after):
    na = len(srcs_thru)
    ns = (2 * N_PEER + 1) * na

    def body(*refs):
        src_refs, land_refs, sm = refs[:na], refs[na:2 * na], refs[2 * na:2 * na + ns]
        _, me = _peer(0)
        for a in range(na):
            pltpu.make_async_copy(src_refs[a].at[me] if scatter else src_refs[a], land_refs[a].at[me],
                                  sm[2 * N_PEER * na + a]).wait()
        for k in range(1, N_DEV):
            peer, pid = _peer(k)
            for a in range(na):
                cp = pltpu.make_async_remote_copy(
                    src_ref=src_refs[a].at[pid] if scatter else src_refs[a], dst_ref=land_refs[a].at[pid],
                    send_sem=sm[2 * (a * N_PEER + k - 1)], recv_sem=sm[2 * (a * N_PEER + k - 1) + 1],
                    device_id=peer, device_id_type=MESH)
                cp.wait_send()
                cp.wait_recv()

    hbm = lambda arrs: tuple(pltpu.HBM(a.shape, a.dtype) for a in arrs)
    outs = pl.pallas_call(
        body, name=name, out_shape=hbm(srcs_thru) + hbm(lands_thru),
        in_specs=[_HBM] * (2 * na) + [_SEM] * ns + [pl.BlockSpec(memory_space=pl.ANY)], out_specs=(_HBM,) * (2 * na),
        input_output_aliases={i: i for i in range(2 * na)},
        compiler_params=pltpu.CompilerParams(has_side_effects=_DATAFLOW),
    )(*srcs_thru, *lands_thru, *sems, after)
    return outs[na:]


def _adamw(w, g, m, v):
    m = ADAM_B1 * m + (1.0 - ADAM_B1) * g
    v = ADAM_B2 * v + (1.0 - ADAM_B2) * (g * g)
    m_hat = m / (1.0 - ADAM_B1 ** ADAM_STEP)
    v_hat = v / (1.0 - ADAM_B2 ** ADAM_STEP)
    return -ADAM_LR * (m_hat / (jnp.sqrt(v_hat) + ADAM_EPS) + ADAM_WD * w), m, v


def adam_sum(w, pieces, m, v, name, layer=0, into=None):
    nl, r, c = w.shape
    tr = r
    for cand in (256, 128, 64, 32, 16, 8):
        if r % cand == 0:
            tr = cand
            break

    def body(w_ref, p_ref, m_ref, v_ref, *rest):
        g_ref, d_ref, nm_ref, nv_ref = rest[-4:]
        g = p_ref[0].astype(F32)
        for s in range(1, N_DEV):
            g = g + p_ref[s].astype(F32)
        g_ref[0] = g
        d_ref[0], nm_ref[0], nv_ref[0] = _adamw(w_ref[0], g, m_ref[0], v_ref[0])

    row = pl.BlockSpec((1, tr, c), lambda i: (layer, i, 0))
    out = jax.ShapeDtypeStruct((nl, r, c), F32)
    extra = [] if into is None else list(into)
    return pl.pallas_call(
        body, name=name, grid=(r // tr,),
        in_specs=[row, pl.BlockSpec((N_DEV, tr, c), lambda i: (0, i, 0)), row, row]
        + [pl.BlockSpec(memory_space=pl.ANY)] * len(extra),
        out_specs=(row,) * 4, out_shape=(out,) * 4,
        input_output_aliases={4 + i: i for i in range(len(extra))},
        compiler_params=_cp("parallel"))(w, pieces, m, v, *extra)


def sum_rows(gathered, name):
    _, r, c = gathered.shape

    def body(p_ref, o_ref):
        g = p_ref[0]
        for s in range(1, N_DEV):
            g = g + p_ref[s]
        o_ref[...] = g

    return pl.pallas_call(body, name=name, out_shape=jax.ShapeDtypeStruct((r, c), F32))(gathered)


def adam_small(w, g, m, v, name):
    def body(w_ref, g_ref, m_ref, v_ref, d_ref, nm_ref, nv_ref):
        d_ref[...], nm_ref[...], nv_ref[...] = _adamw(w_ref[...], g_ref[...], m_ref[...], v_ref[...])

    out = jax.ShapeDtypeStruct(w.shape, F32)
    return pl.pallas_call(body, name=name, out_shape=(out,) * 3)(w, g, m, v)


def _rope_tables(t):
    inv_freq = 10000.0 ** (-jnp.arange(0, HEAD_DIM, 2, dtype=F32) / HEAD_DIM)
    ang = jnp.arange(t, dtype=F32)[:, None] * inv_freq[None, :]
    cos, sin = jnp.cos(ang), jnp.sin(ang)
    return jnp.concatenate([cos, cos], axis=-1), jnp.concatenate([sin, sin], axis=-1)


def _lane_row(vec8):
    return jnp.pad(vec8.reshape(1, DN_HEADS), ((0, 0), (DN_HEADS, 128 - 2 * DN_HEADS)))


def _ffn_bwd(x, norm_g, w_gu, w_d, saved, dy, tag, after=None):
    ft, gu, at = saved
    dgu = ffn_dact(dy, w_d, gu, f"{tag}_d_gate_up", after=after)
    dwd = mm_at(at, dy, f"{tag}_dw_down")
    dwgu = mm_at(ft, dgu, f"{tag}_dw_gate_up")
    dx, dg = mm_nt_rms_bwd(dgu, w_gu, x, norm_g, dy, f"{tag}_d_norm")
    return dx, dwgu, dwd, dg


def local_step(x, target, small, weights_of, grads_out, after=None):
    t = x.shape[0]
    cosf, sinf = _rope_tables(t)
    alog_row, dtb_row = _lane_row(small["odd_a_log"]), _lane_row(small["odd_dt_bias"])

    h0, h0t = rms_fwd(x, small["even_norm"], "even_norm", after=after)
    we = weights_of("even", h0)
    small = {**small, **we.get("small", {})}
    proj0 = mm_nn(h0, we["w_in"], "even_in_proj")
    qr, kr = qk_prep_fwd(proj0, small["even_q_gain"], small["even_k_gain"], cosf, sinf, "even_qk_prep")
    y_attn, y_attn_t, lse = swa_fwd(qr, kr, proj0, small["even_sinks"], "even_swa")
    y_conv, y_conv_t = gconv_fwd(proj0, small["even_conv_w"], "even_gconv")
    mix0 = jnp.concatenate([y_attn.astype(MXU_DTYPE), y_conv], axis=-1)
    mix0_t = jnp.concatenate([y_attn_t, y_conv_t], axis=0)
    x1, f0, f0t = mm_nn_res_norm(mix0, we["w_out"], x, small["ffn_norm0"], "even_out_proj")
    w0 = weights_of("ffn0", x1)
    gu0, a0, a0t = ffn_up(f0, w0["gate_up"], "ffn0_gate_up")
    ffn0 = (f0t, gu0, a0t)
    x2, h1, h1t = mm_nn_res_norm(a0, w0["down"], x1, small["odd_norm"], "ffn0_down")

    wo = weights_of("odd", x2)
    proj1 = mm_nn(h1, wo["w_in"], "odd_in_proj")
    qn, kn, vs, bg = gdn_prep_fwd(proj1, small["odd_conv_w"], alog_row, dtb_row, "odd_prep")
    o, sall, tall = gdn_fwd(qn, kn, vs, bg, "odd_delta_rule")
    og, ogt = gdn_out_fwd(o, proj1, small["odd_o_gain"], "odd_gate_norm")
    x3, f1, f1t = mm_nn_res_norm(og, wo["w_out"], x2, small["ffn_norm1"], "odd_out_proj")
    w1 = weights_of("ffn1", x3)
    gu1, a1, a1t = ffn_up(f1, w1["gate_up"], "ffn1_gate_up")
    ffn1 = (f1t, gu1, a1t)
    x4 = mm_nn(a1, w1["down"], "ffn1_down", res=x3)

    dy, loss_row = loss_head(x4, target, "loss_head")

    gs = {}
    dx3, dwgu, dwd, gs["ffn_norm1"] = _ffn_bwd(x3, small["ffn_norm1"], w1["gate_up"], w1["down"], ffn1, dy, "ffn1")
    tok = grads_out("ffn1", {"gate_up": dwgu, "down": dwd})

    dog = mm_nt(dx3, wo["w_out"], "odd_d_gated", after=tok)
    dwo = mm_at(ogt, dx3, "odd_dw_out")
    do, dz, gs["odd_o_gain"] = gdn_out_bwd(o, proj1, small["odd_o_gain"], dog, "odd_d_gate_norm")
    dqn, dkn, dvs, dbg = gdn_bwd(qn, kn, vs, bg, sall, tall, do, "odd_d_delta_rule")
    dqkv, dba, gs["odd_conv_w"], ddt_row, dal_row = gdn_prep_bwd(
        proj1, small["odd_conv_w"], alog_row, dtb_row, dqn, dkn, dvs, dbg, "odd_d_prep")
    gs["odd_dt_bias"] = ddt_row[:, DN_HEADS:2 * DN_HEADS]
    gs["odd_a_log"] = dal_row[:, DN_HEADS:2 * DN_HEADS]
    dproj1 = jnp.concatenate([dqkv, dz, dba], axis=-1)
    dwi = mm_at(h1t, dproj1, "odd_dw_in")
    dx2, gs["odd_norm"] = mm_nt_rms_bwd(dproj1, wo["w_in"], x2, small["odd_norm"], dx3, "odd_d_norm")
    tok = grads_out("odd", {"w_in": dwi, "w_out": dwo})

    dx1, dwgu, dwd, gs["ffn_norm0"] = _ffn_bwd(x1, small["ffn_norm0"], w0["gate_up"], w0["down"], ffn0, dx2, "ffn0",
                                               after=tok)
    tok = grads_out("ffn0", {"gate_up": dwgu, "down": dwd})

    dmix = mm_nt(dx1, we["w_out"], "even_d_mix", after=tok)
    dwo = mm_at(mix0_t, dx1, "even_dw_out")
    dqr, dkr, dv, gs["even_sinks"] = swa_bwd(qr, kr, proj0, small["even_sinks"], y_attn, lse, dmix, "even_d_swa")
    dqk, gs["even_q_gain"], gs["even_k_gain"] = qk_prep_bwd(
        proj0, small["even_q_gain"], small["even_k_gain"], cosf, sinf, dqr, dkr, "even_d_qk_prep")
    dgb, dgc, dxi, gs["even_conv_w"] = gconv_bwd(proj0, small["even_conv_w"], dmix, "even_d_gconv")
    dproj0 = jnp.concatenate([dqk, dv, dgb, dgc, dxi], axis=-1)
    dwi = mm_at(h0t, dproj0, "even_dw_in")
    tok = grads_out("even", {"w_in": dwi, "w_out": dwo})
    grad_x, gs["even_norm"] = mm_nt_rms_bwd(dproj0, we["w_in"], x, small["even_norm"], dx1, "even_d_norm", after=tok)
    return loss_row, grad_x, gs


_SMALL_ORDER = ("even_norm", "even_q_gain", "even_k_gain", "even_sinks", "odd_a_log", "odd_dt_bias", "odd_o_gain",
                "ffn_norm0", "ffn_norm1", "odd_norm", "even_conv_w", "odd_conv_w")
_SMALL_SIZE = {"even_norm": 1024, "even_q_gain": 64, "even_k_gain": 64, "even_sinks": 8, "odd_a_log": 8,
               "odd_dt_bias": 8, "odd_o_gain": 128, "ffn_norm0": 1024, "ffn_norm1": 1024, "odd_norm": 1024,
               "even_conv_w": 3 * 512, "odd_conv_w": 4 * 3072}
_N_REPL = 9


def _pack_rows(vals):
    flat = jnp.concatenate([v.reshape(-1) for v in vals])
    pad = (-flat.shape[0]) % 1024
    return jnp.pad(flat, (0, pad)).reshape(-1, 128)


def _my_block(full, size, axis):
    me = 4 * lax.axis_index("x") + 2 * lax.axis_index("y") + lax.axis_index("c")
    return lax.dynamic_slice_in_dim(full, me * size, size, axis=axis)


def _col_gathered(g):
    return g.transpose(1, 0, 2).reshape(g.shape[1], N_DEV * g.shape[2])


def _col_pieces(dw):
    k, n8 = dw.shape
    return dw.reshape(k, N_DEV, n8 // N_DEV).transpose(1, 0, 2)


def kernel(x, even_norm, even_w_in, even_q_gain, even_k_gain, even_sinks, even_conv_w, even_w_out, odd_norm, odd_w_in, odd_conv_w, odd_a_log, odd_dt_bias, odd_o_gain, odd_w_out, ffn_norm, ffn_w_gate_up, ffn_w_down, loss_target, m_even_norm, m_even_w_in, m_even_q_gain, m_even_k_gain, m_even_sinks, m_even_conv_w, m_even_w_out, m_odd_norm, m_odd_w_in, m_odd_conv_w, m_odd_a_log, m_odd_dt_bias, m_odd_o_gain, m_odd_w_out, m_ffn_norm, m_ffn_w_gate_up, m_ffn_w_down, v_even_norm, v_even_w_in, v_even_q_gain, v_even_k_gain, v_even_sinks, v_even_conv_w, v_even_w_out, v_odd_norm, v_odd_w_in, v_odd_conv_w, v_odd_a_log, v_odd_dt_bias, v_odd_o_gain, v_odd_w_out, v_ffn_norm, v_ffn_w_gate_up, v_ffn_w_down):
    t = x.shape[1]
    d = D_MODEL

    me = 4 * lax.axis_index("x") + 2 * lax.axis_index("y") + lax.axis_index("c")
    fpd = D_FF // N_DEV
    shard = {
        "even": {"w_in": even_w_in.reshape(d, EVEN_IN_W // N_DEV), "w_out": even_w_out.reshape(d // N_DEV, d)},
        "ffn0": {"gate_up": ffn_w_gate_up[0], "down": ffn_w_down[0]},
        "odd": {"w_in": odd_w_in.reshape(d, ODD_IN_W // N_DEV), "w_out": odd_w_out.reshape(d // N_DEV, d)},
        "ffn1": {"gate_up": ffn_w_gate_up[1], "down": ffn_w_down[1]},
    }
    given = {
        ("even", "w_in"): ("even_w_in", even_w_in, m_even_w_in, v_even_w_in, 0),
        ("even", "w_out"): ("even_w_out", even_w_out, m_even_w_out, v_even_w_out, 0),
        ("odd", "w_in"): ("odd_w_in", odd_w_in, m_odd_w_in, v_odd_w_in, 0),
        ("odd", "w_out"): ("odd_w_out", odd_w_out, m_odd_w_out, v_odd_w_out, 0),
        ("ffn0", "gate_up"): ("ffn_w_gate_up", ffn_w_gate_up, m_ffn_w_gate_up, v_ffn_w_gate_up, 0),
        ("ffn1", "gate_up"): ("ffn_w_gate_up", ffn_w_gate_up, m_ffn_w_gate_up, v_ffn_w_gate_up, 1),
        ("ffn0", "down"): ("ffn_w_down", ffn_w_down, m_ffn_w_down, v_ffn_w_down, 0),
        ("ffn1", "down"): ("ffn_w_down", ffn_w_down, m_ffn_w_down, v_ffn_w_down, 1),
    }

    def whole(group, parts):
        col, row = tuple(shard[group])
        w_col = _gu_gathered(parts[0]) if col == "gate_up" else _col_gathered(parts[0])
        if group == "odd":
            w_col = jnp.pad(w_col, ((0, 0), (0, ODD_IN_PAD - ODD_IN_W)))
        return {col: w_col, row: parts[1].reshape(-1, d)}

    wire = {g: [a.astype(MXU_DTYPE) for a in shard[g].values()] for g in shard}
    wire["even"].append(_pack_rows([odd_norm, even_conv_w, odd_conv_w]))
    gathers, tok = {}, None
    for g in shard:
        sems, srcs_thru, lands_thru, tok = send_start(wire[g], f"gather_{g}_start", False, tok)
        gathers[g] = (sems, srcs_thru, lands_thru)
    o1 = d // N_DEV
    o2 = o1 + 3 * CONV_CH // N_DEV

    def weights_of(group, after):
        lands = send_wait(*gathers[group], f"gather_{group}_wait", False, after)
        out = whole(group, lands)
        if group == "even":
            sg = lands[2].reshape(N_DEV, -1)
            out["small"] = {
                "odd_norm": sg[:, :o1].reshape(1, d),
                "even_conv_w": sg[:, o1:o2].reshape(N_DEV, 3, CONV_CH // N_DEV).transpose(1, 0, 2).reshape(3, CONV_CH),
                "odd_conv_w": sg[:, o2:o2 + 4 * _QKV_W // N_DEV].reshape(N_DEV, 4, _QKV_W // N_DEV)
                .transpose(1, 0, 2).reshape(4, _QKV_W),
            }
        return out

    sent = {}

    def grads_out(group, dws):
        col, row = tuple(shard[group])
        n_cols = N_DEV * shard[group][col].shape[1]
        pieces = [_gu_pieces(dws[col]) if col == "gate_up" else _col_pieces(dws[col][:, :n_cols]),
                  dws[row].reshape((N_DEV,) + shard[group][row].shape)]
        sems, srcs_thru, lands_thru, token = send_start(pieces, f"exchange_{group}_start", True, None)
        sent[group] = (sems, srcs_thru, lands_thru, pieces)
        return token

    small = {
        "even_norm": even_norm, "even_q_gain": even_q_gain, "even_k_gain": even_k_gain, "even_sinks": even_sinks,
        "odd_a_log": odd_a_log.reshape(-1), "odd_dt_bias": odd_dt_bias.reshape(-1), "odd_o_gain": odd_o_gain,
        "ffn_norm0": ffn_norm[0:1], "ffn_norm1": ffn_norm[1:2],
    }

    loss_row, grad_x, gs = local_step(x.reshape(t, d), loss_target.reshape(t, d), small, weights_of, grads_out, after=tok)

    rows = _pack_rows([gs[n] for n in _SMALL_ORDER] + [loss_row[:, 0:1]])
    (rows_g,) = all_gather([rows], "gather_small_grads")
    tot = sum_rows(rows_g, "sum_small_grads").reshape(-1)
    off, sgrad = 0, {}
    for n in _SMALL_ORDER:
        sgrad[n] = tot[off:off + _SMALL_SIZE[n]]
        off += _SMALL_SIZE[n]
    loss = tot[off]

    res, behind = {}, tot
    for g in ("ffn1", "odd", "ffn0", "even"):
        sems, srcs_thru, lands_thru, pieces = sent[g]
        lands = send_wait(sems, srcs_thru, lands_thru, f"exchange_{g}_wait", True, behind)
        for key, pcs in zip(shard[g], lands):
            name, w_, m_, v_, layer = given[g, key]
            res[name] = adam_sum(w_, pcs, m_, v_, f"adamw_{g}_{key}", layer=layer, into=res.get(name))
        behind = res[name][0]

    repl = _SMALL_ORDER[:_N_REPL]
    repl_w = {"even_norm": even_norm, "even_q_gain": even_q_gain, "even_k_gain": even_k_gain, "even_sinks": even_sinks,
              "odd_a_log": odd_a_log, "odd_dt_bias": odd_dt_bias, "odd_o_gain": odd_o_gain,
              "ffn_norm0": ffn_norm[0], "ffn_norm1": ffn_norm[1]}
    repl_m = {"even_norm": m_even_norm, "even_q_gain": m_even_q_gain, "even_k_gain": m_even_k_gain,
              "even_sinks": m_even_sinks, "odd_a_log": m_odd_a_log, "odd_dt_bias": m_odd_dt_bias,
              "odd_o_gain": m_odd_o_gain, "ffn_norm0": m_ffn_norm[0], "ffn_norm1": m_ffn_norm[1]}
    repl_v = {"even_norm": v_even_norm, "even_q_gain": v_even_q_gain, "even_k_gain": v_even_k_gain,
              "even_sinks": v_even_sinks, "odd_a_log": v_odd_a_log, "odd_dt_bias": v_odd_dt_bias,
              "odd_o_gain": v_odd_o_gain, "ffn_norm0": v_ffn_norm[0], "ffn_norm1": v_ffn_norm[1]}
    pk = lambda dct: _pack_rows([dct[n] for n in repl])
    pd_, pm_, pv_ = adam_small(pk(repl_w), pk(sgrad), pk(repl_m), pk(repl_v), "adamw_replicated")
    sres = {}
    off = 0
    for n in repl:
        sz = _SMALL_SIZE[n]
        sres[n] = (sgrad[n], pd_.reshape(-1)[off:off + sz], pm_.reshape(-1)[off:off + sz], pv_.reshape(-1)[off:off + sz])
        off += sz
    g_on = _my_block(sgrad["odd_norm"].reshape(1, d), d // N_DEV, 1)
    g_ec = _my_block(sgrad["even_conv_w"].reshape(3, CONV_CH), CONV_CH // N_DEV, 1)
    g_oc = _my_block(sgrad["odd_conv_w"].reshape(4, _QKV_W), _QKV_W // N_DEV, 1)
    shard_w = _pack_rows([odd_norm, even_conv_w, odd_conv_w])
    sd_, sm_, sv_ = adam_small(shard_w, _pack_rows([g_on, g_ec, g_oc]),
                               _pack_rows([m_odd_norm, m_even_conv_w, m_odd_conv_w]),
                               _pack_rows([v_odd_norm, v_even_conv_w, v_odd_conv_w]), "adamw_sharded_small")
    off = 0
    for n, gfull, like in (("odd_norm", g_on, odd_norm), ("even_conv_w", g_ec, even_conv_w), ("odd_conv_w", g_oc, odd_conv_w)):
        sz = like.size
        sres[n] = (gfull, sd_.reshape(-1)[off:off + sz], sm_.reshape(-1)[off:off + sz], sv_.reshape(-1)[off:off + sz])
        off += sz

    def small_out(name, like, kind):
        if name == "ffn_norm":
            return jnp.stack([sres["ffn_norm0"][kind], sres["ffn_norm1"][kind]]).reshape(like.shape)
        return sres[name][kind].reshape(like.shape)

    order = (("even_norm", even_norm), ("even_w_in", even_w_in), ("even_q_gain", even_q_gain),
             ("even_k_gain", even_k_gain), ("even_sinks", even_sinks), ("even_conv_w", even_conv_w),
             ("even_w_out", even_w_out), ("odd_norm", odd_norm), ("odd_w_in", odd_w_in), ("odd_conv_w", odd_conv_w),
             ("odd_a_log", odd_a_log), ("odd_dt_bias", odd_dt_bias), ("odd_o_gain", odd_o_gain),
             ("odd_w_out", odd_w_out), ("ffn_norm", ffn_norm), ("ffn_w_gate_up", ffn_w_gate_up),
             ("ffn_w_down", ffn_w_down))
    outs = [loss, grad_x.reshape(x.shape)]
    for kind in range(4):
        for name, like in order:
            outs.append(res[name][kind] if name in res else small_out(name, like, kind))
    return tuple(outs)
```

```python
import jax
import jax.numpy as jnp
import numpy as np
from jax import lax
from jax.experimental import pallas as pl
from jax.experimental.pallas import tpu as pltpu

F32 = jnp.float32
MXU_DTYPE = jnp.bfloat16
HI = lax.Precision.HIGH
EPS = 1e-6
N_DEV = 8
D_MODEL = 1024
HEAD_DIM = 64
ATTN_HEADS = 8
KV_HEADS = 2
ATTN_BLOCK = 128
Q_W = 512
KV_W = 128
CONV_CH = 512
EVEN_IN_W = 2304
DN_HEADS = 8
DN_DIM = 128
DN_W = 1024
DN_CHUNK = 64
ODD_IN_W = 4112
ODD_IN_PAD = 4224
D_FF = 2816
NEG = -1e30
VMEM_LIMIT = 56 * 1024 * 1024
ADAM_LR, ADAM_B1, ADAM_B2, ADAM_EPS, ADAM_WD, ADAM_STEP = 0.001, 0.9, 0.999, 1e-08, 0.01, 10
MESH = pl.DeviceIdType.MESH


def _cp(*sem):
    return pltpu.CompilerParams(dimension_semantics=sem, vmem_limit_bytes=VMEM_LIMIT)


def _pick(n, cap):
    best = 128
    for t in range(128, cap + 1, 128):
        if n % t == 0:
            best = t
    return best


def _mx(a, b):
    return jnp.dot(a.astype(MXU_DTYPE), b.astype(MXU_DTYPE), preferred_element_type=F32)


def _mx_nt(a, b):
    return lax.dot_general(a.astype(MXU_DTYPE), b.astype(MXU_DTYPE), (((1,), (1,)), ((), ())),
                           preferred_element_type=F32)


def _mx_tn(a, b):
    return lax.dot_general(a.astype(MXU_DTYPE), b.astype(MXU_DTYPE), (((0,), (0,)), ((), ())),
                           preferred_element_type=F32)


def _hi(a, b):
    return jnp.dot(a, b, precision=HI, preferred_element_type=F32)


def _hi_nt(a, b):
    return lax.dot_general(a, b, (((1,), (1,)), ((), ())), precision=HI, preferred_element_type=F32)


def _hi_tn(a, b):
    return lax.dot_general(a, b, (((0,), (0,)), ((), ())), precision=HI, preferred_element_type=F32)


def _sigmoid(x):
    return 0.5 * jnp.tanh(0.5 * x) + 0.5


def _softplus(x):
    return jnp.maximum(x, 0.0) + jnp.log(1.0 + jnp.exp(-jnp.abs(x)))


def mm_nn(a, b, name, res=None, out_dtype=F32, tm=1024):
    m, k = a.shape
    _, n = b.shape
    tn = _pick(n, 1536)
    tm = min(tm, m)

    def body(*refs):
        a_ref, b_ref = refs[0], refs[1]
        o_ref = refs[-1]
        acc = _mx(a_ref[...], b_ref[...])
        if res is not None:
            acc = acc + refs[2][...]
        o_ref[...] = acc.astype(o_ref.dtype)

    in_specs = [pl.BlockSpec((tm, k), lambda j, i: (i, 0)), pl.BlockSpec((k, tn), lambda j, i: (0, j))]
    args = [a, b]
    if res is not None:
        in_specs.append(pl.BlockSpec((tm, tn), lambda j, i: (i, j)))
        args.append(res)
    return pl.pallas_call(
        body, name=name, grid=(n // tn, m // tm), in_specs=in_specs,
        out_specs=pl.BlockSpec((tm, tn), lambda j, i: (i, j)),
        out_shape=jax.ShapeDtypeStruct((m, n), out_dtype), compiler_params=_cp("parallel", "parallel"))(*args)


def mm_nn_res_norm(a, b, res, g, name, tm=512):
    t, k = a.shape
    d = b.shape[1]
    tm = min(tm, t)

    def body(a_ref, b_ref, res_ref, g_ref, y_ref, h_ref, ht_ref):
        y = res_ref[...] + _mx(a_ref[...], b_ref[...])
        y_ref[...] = y
        h = y * lax.rsqrt(jnp.mean(y * y, axis=-1, keepdims=True) + EPS) * g_ref[...]
        h_ref[...] = h.astype(h_ref.dtype)
        ht_ref[...] = h.T.astype(ht_ref.dtype)

    row = pl.BlockSpec((tm, d), lambda i: (i, 0))
    return pl.pallas_call(
        body, name=name, grid=(t // tm,),
        in_specs=[pl.BlockSpec((tm, k), lambda i: (i, 0)), pl.BlockSpec((k, d), lambda i: (0, 0)), row,
                  pl.BlockSpec((1, d), lambda i: (0, 0))],
        out_specs=(row, row, pl.BlockSpec((d, tm), lambda i: (0, i))),
        out_shape=(jax.ShapeDtypeStruct((t, d), F32), jax.ShapeDtypeStruct((t, d), MXU_DTYPE),
                   jax.ShapeDtypeStruct((d, t), MXU_DTYPE)),
        compiler_params=_cp("parallel"))(a, b, res, g)


def mm_nt(a, b, name, out_dtype=F32, tm=1024, after=None):
    m, k = a.shape
    n, _ = b.shape
    tn = _pick(n, 512 if k > 3000 else 1536)
    tm = min(tm, m)

    def body(a_ref, b_ref, *rest):
        o_ref = rest[-1]
        o_ref[...] = _mx_nt(a_ref[...], b_ref[...]).astype(o_ref.dtype)

    in_specs = [pl.BlockSpec((tm, k), lambda j, i: (i, 0)), pl.BlockSpec((tn, k), lambda j, i: (j, 0))]
    args = [a, b]
    if after is not None:
        in_specs.append(pl.BlockSpec(memory_space=pl.ANY))
        args.append(after)
    return pl.pallas_call(
        body, name=name, grid=(n // tn, m // tm), in_specs=in_specs,
        out_specs=pl.BlockSpec((tm, tn), lambda j, i: (i, j)),
        out_shape=jax.ShapeDtypeStruct((m, n), out_dtype), compiler_params=_cp("parallel", "parallel"))(*args)


def mm_at(at, b, name, tk=1024):
    m, kk = at.shape
    _, n = b.shape
    tm, tn, tk = _pick(m, 1408), _pick(n, 1408), min(tk, kk)
    nk = kk // tk

    def body(a_ref, b_ref, o_ref, acc_ref):
        k = pl.program_id(2)
        p = _mx(a_ref[...], b_ref[...])
        acc = jnp.where(k == 0, p, acc_ref[...] + p)
        acc_ref[...] = acc

        @pl.when(k == nk - 1)
        def _():
            o_ref[...] = acc.astype(o_ref.dtype)

    return pl.pallas_call(
        body, name=name, grid=(m // tm, n // tn, nk),
        in_specs=[pl.BlockSpec((tm, tk), lambda i, j, k: (i, k)), pl.BlockSpec((tk, tn), lambda i, j, k: (k, j))],
        out_specs=pl.BlockSpec((tm, tn), lambda i, j, k: (i, j)),
        out_shape=jax.ShapeDtypeStruct((m, n), MXU_DTYPE), scratch_shapes=[pltpu.VMEM((tm, tn), F32)],
        compiler_params=_cp("parallel", "parallel", "arbitrary"))(at, b)


def rms_fwd(x, g, name, tm=512, after=None):
    t, d = x.shape

    def body(x_ref, g_ref, *rest):
        o_ref, ot_ref = rest[-2:]
        xv = x_ref[...]
        r = lax.rsqrt(jnp.mean(xv * xv, axis=-1, keepdims=True) + EPS)
        h = xv * r * g_ref[...]
        o_ref[...] = h.astype(o_ref.dtype)
        ot_ref[...] = h.T.astype(ot_ref.dtype)

    in_specs = [pl.BlockSpec((tm, d), lambda i: (i, 0)), pl.BlockSpec((1, d), lambda i: (0, 0))]
    args = [x, g]
    if after is not None:
        in_specs.append(pl.BlockSpec(memory_space=pl.ANY))
        args.append(after)
    return pl.pallas_call(
        body, name=name, grid=(t // tm,), in_specs=in_specs,
        out_specs=(pl.BlockSpec((tm, d), lambda i: (i, 0)), pl.BlockSpec((d, tm), lambda i: (0, i))),
        out_shape=(jax.ShapeDtypeStruct((t, d), MXU_DTYPE), jax.ShapeDtypeStruct((d, t), MXU_DTYPE)),
        compiler_params=_cp("parallel"))(*args)


def mm_nt_rms_bwd(a, b, x, g, dres, name, tm=512, after=None):
    t, k = a.shape
    d = b.shape[0]
    tm = min(tm, t)

    def body(a_ref, b_ref, x_ref, g_ref, dres_ref, *rest):
        dx_ref, dg_ref = rest[-2:]
        dhv = _mx_nt(a_ref[...], b_ref[...])
        xv = x_ref[...]
        r = lax.rsqrt(jnp.mean(xv * xv, axis=-1, keepdims=True) + EPS)
        xh = xv * r
        dxh = dhv * g_ref[...]
        dx_ref[...] = dres_ref[...] + r * (dxh - xh * jnp.mean(dxh * xh, axis=-1, keepdims=True))
        part = jnp.sum(dhv * xh, axis=0, keepdims=True)
        dg_ref[...] = jnp.where(pl.program_id(0) == 0, part, dg_ref[...] + part)

    row = pl.BlockSpec((tm, d), lambda i: (i, 0))
    one = pl.BlockSpec((1, d), lambda i: (0, 0))
    in_specs = [pl.BlockSpec((tm, k), lambda i: (i, 0)), pl.BlockSpec((d, k), lambda i: (0, 0)), row, one, row]
    args = [a, b, x, g, dres]
    if after is not None:
        in_specs.append(pl.BlockSpec(memory_space=pl.ANY))
        args.append(after)
    return pl.pallas_call(
        body, name=name, grid=(t // tm,), in_specs=in_specs, out_specs=(row, one),
        out_shape=(jax.ShapeDtypeStruct((t, d), F32), jax.ShapeDtypeStruct((1, d), F32)),
        compiler_params=_cp("arbitrary"))(*args)


GU_TILE = 1408


_GU_PER_TILE = GU_TILE * N_DEV // (2 * D_FF)


def _gu_gathered(g):
    _, k, c = g.shape
    nj = N_DEV // (2 * _GU_PER_TILE)
    return g.reshape(2, nj, _GU_PER_TILE, k, c).transpose(3, 1, 0, 2, 4).reshape(k, N_DEV * c)


def _gu_pieces(dw):
    k, n8 = dw.shape
    nj = N_DEV // (2 * _GU_PER_TILE)
    return dw.reshape(k, nj, 2, _GU_PER_TILE, n8 // N_DEV).transpose(2, 1, 3, 0, 4).reshape(N_DEV, k, n8 // N_DEV)


def ffn_up(f, w, name, tm=512):
    t, d = f.shape

    def body(f_ref, w_ref, gu_ref, a_ref, at_ref):
        gu = _mx(f_ref[...], w_ref[...])
        gu_ref[...] = gu
        g, u = gu[:, :GU_TILE], gu[:, GU_TILE:]
        act = g * _sigmoid(g) * u
        a_ref[...] = act.astype(a_ref.dtype)
        at_ref[...] = act.T.astype(at_ref.dtype)

    return pl.pallas_call(
        body, name=name, grid=(D_FF // GU_TILE, t // tm),
        in_specs=[pl.BlockSpec((tm, d), lambda j, i: (i, 0)), pl.BlockSpec((d, 2 * GU_TILE), lambda j, i: (0, j))],
        out_specs=(pl.BlockSpec((tm, 2 * GU_TILE), lambda j, i: (i, j)), pl.BlockSpec((tm, GU_TILE), lambda j, i: (i, j)),
                   pl.BlockSpec((GU_TILE, tm), lambda j, i: (j, i))),
        out_shape=(jax.ShapeDtypeStruct((t, 2 * D_FF), F32), jax.ShapeDtypeStruct((t, D_FF), MXU_DTYPE),
                   jax.ShapeDtypeStruct((D_FF, t), MXU_DTYPE)),
        compiler_params=_cp("parallel", "parallel"))(f, w)


def ffn_dact(dy, w_d, gu, name, tm=512, after=None):
    t, d = dy.shape

    def body(dy_ref, w_ref, gu_ref, *rest):
        o_ref = rest[-1]
        da = _mx_nt(dy_ref[...], w_ref[...])
        g, u = gu_ref[:, :GU_TILE], gu_ref[:, GU_TILE:]
        sg = _sigmoid(g)
        o_ref[:, :GU_TILE] = (da * u * sg * (1.0 + g * (1.0 - sg))).astype(o_ref.dtype)
        o_ref[:, GU_TILE:] = (da * g * sg).astype(o_ref.dtype)

    in_specs = [pl.BlockSpec((tm, d), lambda j, i: (i, 0)), pl.BlockSpec((GU_TILE, d), lambda j, i: (j, 0)),
                pl.BlockSpec((tm, 2 * GU_TILE), lambda j, i: (i, j))]
    args = [dy, w_d, gu]
    if after is not None:
        in_specs.append(pl.BlockSpec(memory_space=pl.ANY))
        args.append(after)
    return pl.pallas_call(
        body, name=name, grid=(D_FF // GU_TILE, t // tm), in_specs=in_specs,
        out_specs=pl.BlockSpec((tm, 2 * GU_TILE), lambda j, i: (i, j)),
        out_shape=jax.ShapeDtypeStruct((t, 2 * D_FF), MXU_DTYPE), compiler_params=_cp("parallel", "parallel"))(*args)


def mm_nn_res_loss(a, b, res, target, name, tm=512):
    t, k = a.shape
    d = b.shape[1]
    tm = min(tm, t)

    def body(a_ref, b_ref, res_ref, t_ref, dy_ref, l_ref):
        e = res_ref[...] + _mx(a_ref[...], b_ref[...]) - t_ref[...]
        dy_ref[...] = e * (1.0 / d)
        part = jnp.zeros((1, 128), F32) + 0.5 * jnp.sum(jnp.mean(e * e, axis=-1, keepdims=True), axis=0, keepdims=True)
        l_ref[...] = jnp.where(pl.program_id(0) == 0, part, l_ref[...] + part)

    row = pl.BlockSpec((tm, d), lambda i: (i, 0))
    return pl.pallas_call(
        body, name=name, grid=(t // tm,),
        in_specs=[pl.BlockSpec((tm, k), lambda i: (i, 0)), pl.BlockSpec((k, d), lambda i: (0, 0)), row, row],
        out_specs=(row, pl.BlockSpec((1, 128), lambda i: (0, 0))),
        out_shape=(jax.ShapeDtypeStruct((t, d), F32), jax.ShapeDtypeStruct((1, 128), F32)),
        compiler_params=_cp("arbitrary"))(a, b, res, target)


QK_W = Q_W + KV_W
_QK_TILE = 256


def _qk_mats():
    idx = np.arange(_QK_TILE)
    half = HEAD_DIM // 2
    same = (idx[:, None] // HEAD_DIM) == (idx[None, :] // HEAD_DIM)
    lo = (idx % HEAD_DIM) < half
    rot = np.where((idx[:, None] == idx[None, :] + half) & lo[None, :], -1.0, 0.0)
    rot = rot + np.where((idx[:, None] == idx[None, :] - half) & ~lo[None, :], 1.0, 0.0)
    return jnp.asarray(same, F32), jnp.asarray(rot, F32)


def _qk_rows(q_gain, k_gain, cosf, sinf):
    gain = jnp.concatenate([q_gain] * ATTN_HEADS + [k_gain] * KV_HEADS, axis=-1)
    return gain, jnp.concatenate([cosf, cosf], axis=-1), jnp.concatenate([sinf, sinf], axis=-1)


def _qk_tiles(a, mat, transposed=False):
    outs = []
    for c0 in range(0, QK_W, _QK_TILE):
        w = min(_QK_TILE, QK_W - c0)
        mt = (mat.T if transposed else mat)[:w, :w].astype(MXU_DTYPE)
        at = a[:, c0:c0 + w]
        hi = at.astype(MXU_DTYPE)
        lo = (at - hi.astype(F32)).astype(MXU_DTYPE)
        outs.append(jnp.dot(hi, mt, preferred_element_type=F32) + jnp.dot(lo, mt, preferred_element_type=F32))
    return jnp.concatenate(outs, axis=-1)


def qk_prep_fwd(proj, q_gain, k_gain, cosf, sinf, name, tm=256):
    t = proj.shape[0]
    gmat, rmat = _qk_mats()
    gain, c2, s2 = _qk_rows(q_gain, k_gain, cosf, sinf)
    rep = QK_W // 128

    def body(p_ref, g_ref, c_ref, s_ref, gm_ref, rm_ref, q_ref, k_ref):
        x = p_ref[...]
        r = lax.rsqrt(_qk_tiles(x * x, gm_ref[...]) * (1.0 / HEAD_DIM) + EPS)
        xn = x * r * g_ref[...]
        c = jnp.concatenate([c_ref[...]] * rep, axis=-1)
        s = jnp.concatenate([s_ref[...]] * rep, axis=-1)
        out = xn * c + _qk_tiles(xn, rm_ref[...]) * s
        q_ref[...] = out[:, :Q_W]
        k_ref[...] = out[:, Q_W:]

    full = pl.BlockSpec((_QK_TILE, _QK_TILE), lambda i: (0, 0))
    tab = pl.BlockSpec((tm, 128), lambda i: (i, 0))
    return pl.pallas_call(
        body, name=name, grid=(t // tm,),
        in_specs=[pl.BlockSpec((tm, QK_W), lambda i: (i, 0)), pl.BlockSpec((1, QK_W), lambda i: (0, 0)), tab, tab,
                  full, full],
        out_specs=(pl.BlockSpec((tm, Q_W), lambda i: (i, 0)), pl.BlockSpec((tm, KV_W), lambda i: (i, 0))),
        out_shape=(jax.ShapeDtypeStruct((t, Q_W), F32), jax.ShapeDtypeStruct((t, KV_W), F32)),
        compiler_params=_cp("parallel"))(proj, gain, c2, s2, gmat, rmat)


def qk_prep_bwd(proj, q_gain, k_gain, cosf, sinf, dq, dk, name, tm=256):
    t = proj.shape[0]
    gmat, rmat = _qk_mats()
    gain, c2, s2 = _qk_rows(q_gain, k_gain, cosf, sinf)
    rep = QK_W // 128
    lanes = np.arange(QK_W)[:, None]
    fold = jnp.asarray(lanes % HEAD_DIM + np.where(lanes >= Q_W, HEAD_DIM, 0) == np.arange(128)[None, :], F32)

    def body(p_ref, g_ref, c_ref, s_ref, gm_ref, rm_ref, f_ref, dq_ref, dk_ref, o_ref, dg_ref):
        x = p_ref[...]
        r = lax.rsqrt(_qk_tiles(x * x, gm_ref[...]) * (1.0 / HEAD_DIM) + EPS)
        xh = x * r
        c = jnp.concatenate([c_ref[...]] * rep, axis=-1)
        s = jnp.concatenate([s_ref[...]] * rep, axis=-1)
        dout = jnp.concatenate([dq_ref[...], dk_ref[...]], axis=-1)
        dxn = dout * c + _qk_tiles(dout * s, rm_ref[...], transposed=True)
        part = _hi(jnp.sum(dxn * xh, axis=0, keepdims=True), f_ref[...])
        dxh = dxn * g_ref[...]
        mean = _qk_tiles(dxh * xh, gm_ref[...]) * (1.0 / HEAD_DIM)
        o_ref[...] = (r * (dxh - xh * mean)).astype(o_ref.dtype)
        dg_ref[...] = jnp.where(pl.program_id(0) == 0, part, dg_ref[...] + part)

    full = pl.BlockSpec((_QK_TILE, _QK_TILE), lambda i: (0, 0))
    tab = pl.BlockSpec((tm, 128), lambda i: (i, 0))
    dqk, dg = pl.pallas_call(
        body, name=name, grid=(t // tm,),
        in_specs=[pl.BlockSpec((tm, QK_W), lambda i: (i, 0)), pl.BlockSpec((1, QK_W), lambda i: (0, 0)), tab, tab,
                  full, full, pl.BlockSpec((QK_W, 128), lambda i: (0, 0)),
                  pl.BlockSpec((tm, Q_W), lambda i: (i, 0)), pl.BlockSpec((tm, KV_W), lambda i: (i, 0))],
        out_specs=(pl.BlockSpec((tm, QK_W), lambda i: (i, 0)), pl.BlockSpec((1, 128), lambda i: (0, 0))),
        out_shape=(jax.ShapeDtypeStruct((t, QK_W), MXU_DTYPE), jax.ShapeDtypeStruct((1, 128), F32)),
        compiler_params=_cp("arbitrary"))(proj, gain, c2, s2, gmat, rmat, fold, dq, dk)
    return dqk, dg[:, :HEAD_DIM], dg[:, HEAD_DIM:]


def _swa_valid(n, grp):
    qi = lax.broadcasted_iota(jnp.int32, (grp * ATTN_BLOCK, 2 * ATTN_BLOCK), 0) & (ATTN_BLOCK - 1)
    kj = lax.broadcasted_iota(jnp.int32, (grp * ATTN_BLOCK, 2 * ATTN_BLOCK), 1)
    diff = qi + ATTN_BLOCK - kj
    return (diff >= 0) & (diff < ATTN_BLOCK) & (n * ATTN_BLOCK - ATTN_BLOCK + kj >= 0)


def _stack_heads(ref, g, grp):
    return jnp.concatenate([ref[:, (g * grp + j) * HEAD_DIM:(g * grp + j + 1) * HEAD_DIM] for j in range(grp)], axis=0)


def _stack_sinks(s_ref, g, grp):
    return jnp.concatenate([jnp.zeros((ATTN_BLOCK, 1), F32) + s_ref[0:1, g * grp + j:g * grp + j + 1]
                            for j in range(grp)], axis=0)


def swa_fwd(q, k, proj, sinks, name):
    t = q.shape[0]
    nb = t // ATTN_BLOCK
    scale = HEAD_DIM ** -0.5
    grp = ATTN_HEADS // KV_HEADS

    def body(q_ref, kc_ref, kp_ref, vc_ref, vp_ref, s_ref, y_ref, yt_ref, lse_ref):
        n = pl.program_id(0)
        valid = _swa_valid(n, grp)
        kk = jnp.concatenate([kp_ref[...], kc_ref[...]], axis=0).astype(MXU_DTYPE)
        vv = jnp.concatenate([vp_ref[...], vc_ref[...]], axis=0).astype(MXU_DTYPE)
        lane = lax.broadcasted_iota(jnp.int32, (ATTN_BLOCK, ATTN_HEADS), 1)
        gs = range(KV_HEADS)
        qg = [_stack_heads(q_ref, g, grp) for g in gs]
        sink = [_stack_sinks(s_ref, g, grp) for g in gs]
        sc = [jnp.where(valid, _mx_nt(qg[g], kk[:, g * HEAD_DIM:(g + 1) * HEAD_DIM]) * scale, NEG) for g in gs]
        m = [jnp.maximum(jnp.max(sc[g], axis=-1, keepdims=True), sink[g]) for g in gs]
        e = [jnp.exp(sc[g] - m[g]) for g in gs]
        den = [jnp.sum(e[g], axis=-1, keepdims=True) + jnp.exp(sink[g] - m[g]) for g in gs]
        og = [_mx(e[g] / den[g], vv[:, g * HEAD_DIM:(g + 1) * HEAD_DIM]) for g in gs]
        lg = [m[g] + jnp.log(den[g]) for g in gs]
        lse = jnp.zeros((ATTN_BLOCK, ATTN_HEADS), F32)
        outs = []
        for h in range(ATTN_HEADS):
            rows = slice((h % grp) * ATTN_BLOCK, (h % grp + 1) * ATTN_BLOCK)
            outs.append(og[h // grp][rows])
            lse = jnp.where(lane == h, lg[h // grp][rows], lse)
        y = jnp.concatenate(outs, axis=-1)
        y_ref[...] = y
        yt_ref[...] = y.T.astype(yt_ref.dtype)
        lse_ref[...] = lse

    cur = lambda n: (n, 0)
    prev = lambda n: (jnp.maximum(n - 1, 0), 0)
    vcol = (Q_W + KV_W) // KV_W
    return pl.pallas_call(
        body, name=name, grid=(nb,),
        in_specs=[pl.BlockSpec((ATTN_BLOCK, Q_W), cur), pl.BlockSpec((ATTN_BLOCK, KV_W), cur),
                  pl.BlockSpec((ATTN_BLOCK, KV_W), prev),
                  pl.BlockSpec((ATTN_BLOCK, KV_W), lambda n: (n, vcol)),
                  pl.BlockSpec((ATTN_BLOCK, KV_W), lambda n: (jnp.maximum(n - 1, 0), vcol)),
                  pl.BlockSpec((1, ATTN_HEADS), lambda n: (0, 0))],
        out_specs=(pl.BlockSpec((ATTN_BLOCK, Q_W), cur), pl.BlockSpec((Q_W, ATTN_BLOCK), lambda n: (0, n)),
                   pl.BlockSpec((ATTN_BLOCK, ATTN_HEADS), cur)),
        out_shape=(jax.ShapeDtypeStruct((t, Q_W), F32), jax.ShapeDtypeStruct((Q_W, t), MXU_DTYPE),
                   jax.ShapeDtypeStruct((t, ATTN_HEADS), F32)),
        compiler_params=_cp("parallel"))(q, k, k, proj, proj, sinks)


def swa_bwd(q, k, proj, sinks, y, lse, dmix, name):
    t = q.shape[0]
    nb = t // ATTN_BLOCK
    scale = HEAD_DIM ** -0.5
    grp = ATTN_HEADS // KV_HEADS

    def body(q_ref, kc_ref, kp_ref, vc_ref, vp_ref, s_ref, y_ref, lse_ref, dy_ref,
             dq_ref, dk_ref, dv_ref, ds_ref, dkc, dvc):
        n = pl.program_id(0)

        @pl.when(n == 0)
        def _():
            dkc[...] = jnp.zeros_like(dkc)
            dvc[...] = jnp.zeros_like(dvc)
            ds_ref[...] = jnp.zeros_like(ds_ref)

        @pl.when(n < nb)
        def _():
            valid = _swa_valid(n, grp)
            kk = jnp.concatenate([kp_ref[...], kc_ref[...]], axis=0).astype(MXU_DTYPE)
            vv = jnp.concatenate([vp_ref[...], vc_ref[...]], axis=0).astype(MXU_DTYPE)
            lane = lax.broadcasted_iota(jnp.int32, (1, ATTN_HEADS), 1)
            gs = range(KV_HEADS)
            kg = [kk[:, g * HEAD_DIM:(g + 1) * HEAD_DIM] for g in gs]
            vg = [vv[:, g * HEAD_DIM:(g + 1) * HEAD_DIM] for g in gs]
            qg = [_stack_heads(q_ref, g, grp).astype(MXU_DTYPE) for g in gs]
            dog = [_stack_heads(dy_ref, g, grp) for g in gs]
            og = [_stack_heads(y_ref, g, grp) for g in gs]
            lg = [jnp.concatenate([lse_ref[:, g * grp + j:g * grp + j + 1] for j in range(grp)], axis=0) for g in gs]
            sink = [_stack_sinks(s_ref, g, grp) for g in gs]
            sc = [jnp.where(valid, _mx_nt(qg[g], kg[g]) * scale, NEG) for g in gs]
            p = [jnp.exp(sc[g] - lg[g]) for g in gs]
            delta = [jnp.sum(dog[g] * og[g], axis=-1, keepdims=True) for g in gs]
            ds = [p[g] * (_mx_nt(dog[g], vg[g]) - delta[g]) for g in gs]
            dqg = [_mx(ds[g], kg[g]) * scale for g in gs]
            dkf = jnp.concatenate([_mx_tn(ds[g], qg[g]) * scale for g in gs], axis=-1)
            dvf = jnp.concatenate([_mx_tn(p[g], dog[g]) for g in gs], axis=-1)
            dsk = [jnp.exp(sink[g] - lg[g]) * delta[g] for g in gs]
            dsink = jnp.zeros((1, ATTN_HEADS), F32)
            dqs = []
            for h in range(ATTN_HEADS):
                rows = slice((h % grp) * ATTN_BLOCK, (h % grp + 1) * ATTN_BLOCK)
                dqs.append(dqg[h // grp][rows])
                dsink = jnp.where(lane == h, -jnp.sum(dsk[h // grp][rows], axis=0, keepdims=True), dsink)
            dq_ref[...] = jnp.concatenate(dqs, axis=-1)
            dk_ref[...] = dkc[...] + dkf[:ATTN_BLOCK]
            dv_ref[...] = (dvc[...] + dvf[:ATTN_BLOCK]).astype(dv_ref.dtype)
            dkc[...] = dkf[ATTN_BLOCK:]
            dvc[...] = dvf[ATTN_BLOCK:]
            ds_ref[...] += dsink

        @pl.when(n == nb)
        def _():
            dk_ref[...] = dkc[...]
            dv_ref[...] = dvc[...].astype(dv_ref.dtype)

    cur = lambda n: (jnp.minimum(n, nb - 1), 0)
    prev = lambda n: (jnp.clip(n - 1, 0, nb - 1), 0)
    vcol = (Q_W + KV_W) // KV_W
    return pl.pallas_call(
        body, name=name, grid=(nb + 1,),
        in_specs=[pl.BlockSpec((ATTN_BLOCK, Q_W), cur), pl.BlockSpec((ATTN_BLOCK, KV_W), cur),
                  pl.BlockSpec((ATTN_BLOCK, KV_W), prev),
                  pl.BlockSpec((ATTN_BLOCK, KV_W), lambda n: (jnp.minimum(n, nb - 1), vcol)),
                  pl.BlockSpec((ATTN_BLOCK, KV_W), lambda n: (jnp.clip(n - 1, 0, nb - 1), vcol)),
                  pl.BlockSpec((1, ATTN_HEADS), lambda n: (0, 0)),
                  pl.BlockSpec((ATTN_BLOCK, Q_W), cur), pl.BlockSpec((ATTN_BLOCK, ATTN_HEADS), cur),
                  pl.BlockSpec((ATTN_BLOCK, Q_W), cur)],
        out_specs=(pl.BlockSpec((ATTN_BLOCK, Q_W), cur), pl.BlockSpec((ATTN_BLOCK, KV_W), prev),
                   pl.BlockSpec((ATTN_BLOCK, KV_W), prev), pl.BlockSpec((1, ATTN_HEADS), lambda n: (0, 0))),
        out_shape=(jax.ShapeDtypeStruct((t, Q_W), F32), jax.ShapeDtypeStruct((t, KV_W), F32),
                   jax.ShapeDtypeStruct((t, KV_W), MXU_DTYPE), jax.ShapeDtypeStruct((1, ATTN_HEADS), F32)),
        scratch_shapes=[pltpu.VMEM((ATTN_BLOCK, KV_W), F32), pltpu.VMEM((ATTN_BLOCK, KV_W), F32)],
        compiler_params=_cp("arbitrary"))(q, k, k, proj, proj, sinks, y, lse, dmix)


GC_W = 256
_GB0, _GC0, _XI0 = 768 // GC_W, 1280 // GC_W, 1792 // GC_W
HALO = 8


def gconv_fwd(proj, conv_w, name, tm=512):
    t = proj.shape[0]
    hb = tm // HALO

    def body(gb_ref, gc_ref, xi_ref, gch_ref, xih_ref, w_ref, y_ref, yt_ref):
        i = pl.program_id(1)
        u = gc_ref[...] * xi_ref[...]
        uh = jnp.where(i == 0, 0.0, gch_ref[...] * xih_ref[...])
        up = jnp.concatenate([uh, u], axis=0)
        cv = w_ref[0:1, :] * up[HALO - 2:HALO - 2 + tm]
        cv = cv + w_ref[1:2, :] * up[HALO - 1:HALO - 1 + tm]
        cv = cv + w_ref[2:3, :] * u
        y = gb_ref[...] * cv
        y_ref[...] = y.astype(y_ref.dtype)
        yt_ref[...] = y.T.astype(yt_ref.dtype)

    def col(c0):
        return pl.BlockSpec((tm, GC_W), lambda cj, i: (i, c0 + cj))

    def halo(c0):
        return pl.BlockSpec((HALO, GC_W), lambda cj, i: (jnp.maximum(i * hb - 1, 0), c0 + cj))

    return pl.pallas_call(
        body, name=name, grid=(CONV_CH // GC_W, t // tm),
        in_specs=[col(_GB0), col(_GC0), col(_XI0), halo(_GC0), halo(_XI0),
                  pl.BlockSpec((3, GC_W), lambda cj, i: (0, cj))],
        out_specs=(pl.BlockSpec((tm, GC_W), lambda cj, i: (i, cj)), pl.BlockSpec((GC_W, tm), lambda cj, i: (cj, i))),
        out_shape=(jax.ShapeDtypeStruct((t, CONV_CH), MXU_DTYPE), jax.ShapeDtypeStruct((CONV_CH, t), MXU_DTYPE)),
        compiler_params=_cp("parallel", "parallel"))(proj, proj, proj, proj, proj, conv_w)


def gconv_bwd(proj, conv_w, dmix, name, tm=512):
    t = proj.shape[0]
    hb = tm // HALO
    nt = t // tm
    dy0 = Q_W // GC_W

    def body(gb_ref, gc_ref, xi_ref, gch_ref, xih_ref, gbn_ref, dyn_ref, dy_ref, w_ref,
             dgb_ref, dgc_ref, dxi_ref, dw_ref):
        i = pl.program_id(1)
        gc, xi, gb, dy = gc_ref[...], xi_ref[...], gb_ref[...], dy_ref[...]
        u = gc * xi
        uh = jnp.where(i == 0, 0.0, gch_ref[...] * xih_ref[...])
        up = jnp.concatenate([uh, u], axis=0)
        u2 = up[HALO - 2:HALO - 2 + tm]
        u1 = up[HALO - 1:HALO - 1 + tm]
        cv = w_ref[0:1, :] * u2 + w_ref[1:2, :] * u1 + w_ref[2:3, :] * u
        dgb_ref[...] = (dy * cv).astype(dgb_ref.dtype)
        dcv = dy * gb
        dcvn = jnp.where(i == nt - 1, 0.0, dyn_ref[...] * gbn_ref[...])
        dcvp = jnp.concatenate([dcv, dcvn], axis=0)
        du = w_ref[0:1, :] * dcvp[2:2 + tm] + w_ref[1:2, :] * dcvp[1:1 + tm] + w_ref[2:3, :] * dcv
        dgc_ref[...] = (du * xi).astype(dgc_ref.dtype)
        dxi_ref[...] = (du * gc).astype(dxi_ref.dtype)
        dw = jnp.concatenate([jnp.sum(dcv * u2, axis=0, keepdims=True), jnp.sum(dcv * u1, axis=0, keepdims=True),
                              jnp.sum(dcv * u, axis=0, keepdims=True)], axis=0)

        @pl.when(i == 0)
        def _():
            dw_ref[...] = dw

        @pl.when(i > 0)
        def _():
            dw_ref[...] += dw

    def col(c0):
        return pl.BlockSpec((tm, GC_W), lambda cj, i: (i, c0 + cj))

    def halo(c0):
        return pl.BlockSpec((HALO, GC_W), lambda cj, i: (jnp.maximum(i * hb - 1, 0), c0 + cj))

    def nxt(c0):
        return pl.BlockSpec((HALO, GC_W), lambda cj, i: (jnp.minimum((i + 1) * hb, t // HALO - 1), c0 + cj))

    out = pl.BlockSpec((tm, GC_W), lambda cj, i: (i, cj))
    return pl.pallas_call(
        body, name=name, grid=(CONV_CH // GC_W, nt),
        in_specs=[col(_GB0), col(_GC0), col(_XI0), halo(_GC0), halo(_XI0), nxt(_GB0), nxt(dy0), col(dy0),
                  pl.BlockSpec((3, GC_W), lambda cj, i: (0, cj))],
        out_specs=(out, out, out, pl.BlockSpec((3, GC_W), lambda cj, i: (0, cj))),
        out_shape=(jax.ShapeDtypeStruct((t, CONV_CH), MXU_DTYPE),) * 3 + (jax.ShapeDtypeStruct((3, CONV_CH), F32),),
        compiler_params=_cp("parallel", "arbitrary"))(proj, proj, proj, proj, proj, proj, dmix, dmix, conv_w)


_QKV_W = 3 * DN_W
_BA_COL = (4 * DN_W) // 128
_Z_COL = _QKV_W // DN_W


def gdn_prep_fwd(proj, conv_w, alog_row, dtb_row, name, tm=256):
    t = proj.shape[0]
    hb = tm // HALO
    qscale = DN_DIM ** -0.5

    def body(x_ref, xh_ref, w_ref, ba_ref, al_ref, dt_ref, q_ref, k_ref, v_ref, bg_ref):
        i = pl.program_id(0)
        for gi in range(3 * DN_HEADS):
            sl = slice(gi * DN_DIM, (gi + 1) * DN_DIM)
            xp = jnp.concatenate([jnp.where(i == 0, 0.0, xh_ref[:, sl]), x_ref[:, sl]], axis=0)
            c = w_ref[0:1, sl] * xp[HALO - 3:HALO - 3 + tm]
            for j in range(1, 4):
                c = c + w_ref[j:j + 1, sl] * xp[HALO - 3 + j:HALO - 3 + j + tm]
            s = c * _sigmoid(c)
            osl = slice((gi % DN_HEADS) * DN_DIM, (gi % DN_HEADS + 1) * DN_DIM)
            if gi < DN_HEADS:
                q_ref[:, osl] = s * lax.rsqrt(jnp.sum(s * s, axis=-1, keepdims=True) + EPS) * qscale
            elif gi < 2 * DN_HEADS:
                k_ref[:, osl] = s * lax.rsqrt(jnp.sum(s * s, axis=-1, keepdims=True) + EPS)
            else:
                v_ref[:, osl] = s
        ba = ba_ref[...]
        lane = lax.broadcasted_iota(jnp.int32, ba.shape, 1)
        gval = -jnp.exp(al_ref[...]) * _softplus(ba + dt_ref[...])
        bg_ref[...] = jnp.where(lane < DN_HEADS, _sigmoid(ba), jnp.where(lane < 2 * DN_HEADS, gval, 0.0))

    row = pl.BlockSpec((tm, DN_W), lambda i: (i, 0))
    one = pl.BlockSpec((1, 128), lambda i: (0, 0))
    return pl.pallas_call(
        body, name=name, grid=(t // tm,),
        in_specs=[pl.BlockSpec((tm, _QKV_W), lambda i: (i, 0)),
                  pl.BlockSpec((HALO, _QKV_W), lambda i: (jnp.maximum(i * hb - 1, 0), 0)),
                  pl.BlockSpec((4, _QKV_W), lambda i: (0, 0)),
                  pl.BlockSpec((tm, 128), lambda i: (i, _BA_COL)), one, one],
        out_specs=(row, row, row, pl.BlockSpec((tm, 128), lambda i: (i, 0))),
        out_shape=(jax.ShapeDtypeStruct((t, DN_W), F32),) * 3 + (jax.ShapeDtypeStruct((t, 128), F32),),
        compiler_params=_cp("parallel"))(proj, proj, conv_w, proj, alog_row, dtb_row)


def gdn_prep_bwd(proj, conv_w, alog_row, dtb_row, dq, dk, dv, dbg, name, tm=256):
    t = proj.shape[0]
    hb = tm // HALO
    nt = t // tm
    qscale = DN_DIM ** -0.5
    te = tm + HALO

    def body(x_ref, xh_ref, xn_ref, w_ref, ba_ref, al_ref, dt_ref, dq_ref, dk_ref, dv_ref,
             dqn_ref, dkn_ref, dvn_ref, dbg_ref, dx_ref, dba_ref, dw_ref, ddt_ref, dal_ref):
        i = pl.program_id(0)
        first = i == 0
        last = i == nt - 1
        dws = []
        for gi in range(3 * DN_HEADS):
            sl = slice(gi * DN_DIM, (gi + 1) * DN_DIM)
            osl = slice((gi % DN_HEADS) * DN_DIM, (gi % DN_HEADS + 1) * DN_DIM)
            xe = jnp.concatenate([jnp.where(first, 0.0, xh_ref[:, sl]), x_ref[:, sl], xn_ref[:, sl]], axis=0)
            c = w_ref[0:1, sl] * xe[HALO - 3:HALO - 3 + te]
            for j in range(1, 4):
                c = c + w_ref[j:j + 1, sl] * xe[HALO - 3 + j:HALO - 3 + j + te]
            sg = _sigmoid(c)
            s = c * sg
            d_ref, dn_ref = ((dq_ref, dqn_ref), (dk_ref, dkn_ref), (dv_ref, dvn_ref))[gi // DN_HEADS]
            dy = jnp.concatenate([d_ref[:, osl], jnp.where(last, 0.0, dn_ref[:, osl])], axis=0)
            if gi < 2 * DN_HEADS:
                r = lax.rsqrt(jnp.sum(s * s, axis=-1, keepdims=True) + EPS)
                sh = s * r
                ds = r * (dy - sh * jnp.sum(sh * dy, axis=-1, keepdims=True))
                if gi < DN_HEADS:
                    ds = ds * qscale
            else:
                ds = dy
            dc = ds * sg * (1.0 + c * (1.0 - sg))
            dcs = [dc[3 - j:3 - j + tm] for j in range(4)]
            dx = w_ref[0:1, sl] * dcs[0]
            for j in range(1, 4):
                dx = dx + w_ref[j:j + 1, sl] * dcs[j]
            dx_ref[:, sl] = dx.astype(dx_ref.dtype)
            x0 = x_ref[:, sl]
            dws.append(jnp.concatenate([jnp.sum(dcs[j] * x0, axis=0, keepdims=True) for j in range(4)], axis=0))
        dw = jnp.concatenate(dws, axis=-1)
        ba = ba_ref[...]
        dbgv = dbg_ref[...]
        lane = lax.broadcasted_iota(jnp.int32, ba.shape, 1)
        beta = _sigmoid(ba)
        ea = -jnp.exp(al_ref[...])
        zin = ba + dt_ref[...]
        is_b = lane < DN_HEADS
        is_a = (lane >= DN_HEADS) & (lane < 2 * DN_HEADS)
        da = jnp.where(is_a, dbgv * ea * _sigmoid(zin), 0.0)
        dba_ref[...] = jnp.where(is_b, dbgv * beta * (1.0 - beta), da).astype(dba_ref.dtype)
        ddt = jnp.sum(da, axis=0, keepdims=True)
        dal = jnp.sum(jnp.where(is_a, dbgv * ea * _softplus(zin), 0.0), axis=0, keepdims=True)

        @pl.when(first)
        def _():
            dw_ref[...] = dw
            ddt_ref[...] = ddt
            dal_ref[...] = dal

        @pl.when(i > 0)
        def _():
            dw_ref[...] += dw
            ddt_ref[...] += ddt
            dal_ref[...] += dal

    row = pl.BlockSpec((tm, DN_W), lambda i: (i, 0))
    nrow = pl.BlockSpec((HALO, DN_W), lambda i: (jnp.minimum((i + 1) * hb, t // HALO - 1), 0))
    one = pl.BlockSpec((1, 128), lambda i: (0, 0))
    return pl.pallas_call(
        body, name=name, grid=(nt,),
        in_specs=[pl.BlockSpec((tm, _QKV_W), lambda i: (i, 0)),
                  pl.BlockSpec((HALO, _QKV_W), lambda i: (jnp.maximum(i * hb - 1, 0), 0)),
                  pl.BlockSpec((HALO, _QKV_W), lambda i: (jnp.minimum((i + 1) * hb, t // HALO - 1), 0)),
                  pl.BlockSpec((4, _QKV_W), lambda i: (0, 0)),
                  pl.BlockSpec((tm, 128), lambda i: (i, _BA_COL)), one, one,
                  row, row, row, nrow, nrow, nrow, pl.BlockSpec((tm, 128), lambda i: (i, 0))],
        out_specs=(pl.BlockSpec((tm, _QKV_W), lambda i: (i, 0)), pl.BlockSpec((tm, 128), lambda i: (i, 0)),
                   pl.BlockSpec((4, _QKV_W), lambda i: (0, 0)), one, one),
        out_shape=(jax.ShapeDtypeStruct((t, _QKV_W), MXU_DTYPE), jax.ShapeDtypeStruct((t, 128), MXU_DTYPE),
                   jax.ShapeDtypeStruct((4, _QKV_W), F32), jax.ShapeDtypeStruct((1, 128), F32),
                   jax.ShapeDtypeStruct((1, 128), F32)),
        compiler_params=_cp("arbitrary"))(proj, proj, proj, conv_w, proj, alog_row, dtb_row, dq, dk, dv, dq, dk, dv, dbg)


def _chunk_masks():
    r = lax.broadcasted_iota(jnp.int32, (DN_CHUNK, DN_CHUNK), 0)
    c = lax.broadcasted_iota(jnp.int32, (DN_CHUNK, DN_CHUNK), 1)
    return r >= c, r > c


INV_PACK = 2


def _inv_unit_lower_many(mats):
    n = DN_CHUNK
    wide = INV_PACK * n
    r = lax.broadcasted_iota(jnp.int32, (wide, wide), 0)
    c = lax.broadcasted_iota(jnp.int32, (wide, wide), 1)
    same = (r & -n) == (c & -n)
    eye = jnp.where((r[:n] == (c[:n] & (n - 1))), 1.0, 0.0)

    def blockdiag(row):
        return jnp.where(same, jnp.concatenate([row] * INV_PACK, axis=0), 0.0)

    packs = [jnp.concatenate(mats[g:g + INV_PACK], axis=-1) for g in range(0, len(mats), INV_PACK)]
    xs = [eye - a for a in packs]
    pws = [_hi(a, blockdiag(a)) for a in packs]
    for step in range(5):
        if step < 4:
            both = [_hi(jnp.concatenate([x, pw], axis=0), blockdiag(pw)) for x, pw in zip(xs, pws)]
            xs = [x + b[:n] for x, b in zip(xs, both)]
            pws = [b[n:] for b in both]
        else:
            xs = [x + _hi(x, blockdiag(pw)) for x, pw in zip(xs, pws)]
    return [x[:, j * n:(j + 1) * n] for x in xs for j in range(INV_PACK)]


def _chunk_common(q, k, v, beta, gc, gcr, lower, strict):
    gam = jnp.exp(jnp.where(lower, gc - gcr, NEG))
    eg = jnp.exp(gc)
    gl = gc[DN_CHUNK - 1:DN_CHUNK, :]
    kdf = jnp.exp(gl - gc)
    kb = k * beta
    bmat = _mx_nt(kb, k)
    qmat = _mx_nt(q, k)
    return gam, eg, jnp.exp(gl), kdf, kb, bmat, qmat


DN_STEP = 4


def gdn_fwd(q, k, v, bg, name):
    t = q.shape[0]
    n_chunks = t // DN_CHUNK

    def body(q_ref, k_ref, v_ref, bg_ref, o_ref, sall_ref, tall_ref, s_ref):
        n = pl.program_id(0)

        @pl.when(n == 0)
        def _():
            s_ref[...] = jnp.zeros_like(s_ref)

        lower, strict = _chunk_masks()
        ltri = jnp.where(lower, 1.0, 0.0)
        hs = range(DN_HEADS)
        sl = [slice(h * DN_DIM, (h + 1) * DN_DIM) for h in hs]
        units = [(c, h) for c in range(DN_STEP) for h in hs]
        nu = range(len(units))
        rs = [slice(c * DN_CHUNK, (c + 1) * DN_CHUNK) for c in range(DN_STEP)]
        bgv = [bg_ref[rs[c], :] for c in range(DN_STEP)]
        gcs = [_hi(ltri, b) for b in bgv]
        gcs_t = [g.T for g in gcs]
        qh = [q_ref[rs[c], sl[h]] for c, h in units]
        kh = [k_ref[rs[c], sl[h]] for c, h in units]
        vh = [v_ref[rs[c], sl[h]] for c, h in units]
        beta = [bgv[c][:, h:h + 1] for c, h in units]
        com = [_chunk_common(qh[u], kh[u], vh[u], beta[u], gcs[c][:, DN_HEADS + h:DN_HEADS + h + 1],
                             gcs_t[c][DN_HEADS + h:DN_HEADS + h + 1, :], lower, strict) for u, (c, h) in enumerate(units)]
        gam, eg, dec, kdf, kb, bmat, qmat = zip(*com)
        tms = _inv_unit_lower_many([jnp.where(strict, bmat[u] * gam[u], 0.0) for u in nu])
        for u, (c, h) in enumerate(units):
            tall_ref[c, h] = tms[u]
        uw = [_hi(tms[u], jnp.concatenate([vh[u] * beta[u], kb[u] * eg[u]], axis=-1)) for u in nu]
        qd = [qh[u] * eg[u] for u in nu]
        pm = [qmat[u] * gam[u] for u in nu]
        kd = [kh[u] * kdf[u] for u in nu]
        st = [s_ref[h] for h in hs]
        for c in range(DN_STEP):
            us = [c * DN_HEADS + h for h in hs]
            for h in hs:
                sall_ref[c, h] = st[h]
            v_new = [uw[us[h]][:, :DN_DIM] - _mx(uw[us[h]][:, DN_DIM:], st[h]) for h in hs]
            o_st = [_mx(qd[us[h]], st[h]) for h in hs]
            o_in = [_mx(pm[us[h]], v_new[h]) for h in hs]
            s_up = [_mx_tn(kd[us[h]], v_new[h]) for h in hs]
            for h in hs:
                o_ref[rs[c], sl[h]] = o_st[h] + o_in[h]
            st = [st[h] * dec[us[h]] + s_up[h] for h in hs]
        for h in hs:
            s_ref[h] = st[h]

    rows = DN_STEP * DN_CHUNK
    row = pl.BlockSpec((rows, DN_W), lambda n: (n, 0))
    return pl.pallas_call(
        body, name=name, grid=(n_chunks // DN_STEP,),
        in_specs=[row, row, row, pl.BlockSpec((rows, 128), lambda n: (n, 0))],
        out_specs=(row, pl.BlockSpec((DN_STEP, DN_HEADS, DN_DIM, DN_DIM), lambda n: (n, 0, 0, 0)),
                   pl.BlockSpec((DN_STEP, DN_HEADS, DN_CHUNK, DN_CHUNK), lambda n: (n, 0, 0, 0))),
        out_shape=(jax.ShapeDtypeStruct((t, DN_W), F32),
                   jax.ShapeDtypeStruct((n_chunks, DN_HEADS, DN_DIM, DN_DIM), F32),
                   jax.ShapeDtypeStruct((n_chunks, DN_HEADS, DN_CHUNK, DN_CHUNK), F32)),
        scratch_shapes=[pltpu.VMEM((DN_HEADS, DN_DIM, DN_DIM), F32)],
        compiler_params=_cp("arbitrary"))(q, k, v, bg)


def gdn_bwd(q, k, v, bg, sall, tall, do, name):
    t = q.shape[0]
    n_chunks = t // DN_CHUNK

    def body(q_ref, k_ref, v_ref, bg_ref, sall_ref, tall_ref, do_ref, dq_ref, dk_ref, dv_ref, dbg_ref, ds_ref):
        n = pl.program_id(0)

        @pl.when(n == 0)
        def _():
            ds_ref[...] = jnp.zeros_like(ds_ref)

        lower, strict = _chunk_masks()
        ltri = jnp.where(lower, 1.0, 0.0)
        bgv = bg_ref[...]
        gcs = _hi(ltri, bgv)
        gcs_t = gcs.T
        lane = lax.broadcasted_iota(jnp.int32, (DN_CHUNK, 128), 1)
        rowi = lax.broadcasted_iota(jnp.int32, (DN_CHUNK, 1), 0)
        hs = range(DN_HEADS)
        each = lambda fn, *ls: [fn(*a) for a in zip(*ls)]
        rsum = lambda a: jnp.sum(a, axis=-1, keepdims=True)
        sl = [slice(h * DN_DIM, (h + 1) * DN_DIM) for h in hs]
        st = [sall_ref[0, h] for h in hs]
        tms = [tall_ref[0, h] for h in hs]
        dsn = [ds_ref[h] for h in hs]
        qh = [q_ref[:, sl[h]] for h in hs]
        kh = [k_ref[:, sl[h]] for h in hs]
        vh = [v_ref[:, sl[h]] for h in hs]
        doh = [do_ref[:, sl[h]] for h in hs]
        beta = [bgv[:, h:h + 1] for h in hs]
        com = [_chunk_common(qh[h], kh[h], vh[h], beta[h], gcs[:, DN_HEADS + h:DN_HEADS + h + 1],
                             gcs_t[DN_HEADS + h:DN_HEADS + h + 1, :], lower, strict) for h in hs]
        gam, eg, dec, kdf, kb, bmat, qmat = zip(*com)
        rhs_w = each(lambda a, b: a * b, kb, eg)
        uw = each(lambda t_, v_, b_, r_: _hi(t_, jnp.concatenate([v_ * b_, r_], axis=-1)), tms, vh, beta, rhs_w)
        qd = each(lambda a, b: a * b, qh, eg)
        kd = each(lambda a, b: a * b, kh, kdf)
        pmat = each(lambda a, b: a * b, qmat, gam)
        v_new = each(lambda uw_, s_: uw_[:, :DN_DIM] - _mx(uw_[:, DN_DIM:], s_), uw, st)
        dqd = each(_mx_nt, doh, st)
        ds_o = each(_mx_tn, qd, doh)
        dp = each(lambda d_, v_: jnp.where(lower, _mx_nt(d_, v_), 0.0), doh, v_new)
        dvn_o = each(_mx_tn, pmat, doh)
        ddec = each(lambda d_, s_: jnp.sum(rsum(d_ * s_), axis=0, keepdims=True), dsn, st)
        dkd = each(_mx_nt, v_new, dsn)
        dvn = each(lambda a, k_, d_: a + _mx(k_, d_), dvn_o, kd, dsn)
        dw = each(lambda d_, s_: -_mx_nt(d_, s_), dvn, st)
        ds_w = each(lambda uw_, d_: _mx_tn(uw_[:, DN_DIM:], d_), uw, dvn)
        for h in hs:
            ds_ref[h] = ds_o[h] + dec[h] * dsn[h] - ds_w[h]
        dr = each(lambda t_, a, b: _hi_tn(t_, jnp.concatenate([a, b], axis=-1)), tms, dvn, dw)
        da = each(lambda r_, uw_: jnp.where(strict, -_hi_nt(r_, uw_), 0.0), dr, uw)
        dru = [r_[:, :DN_DIM] for r_ in dr]
        drw = [r_[:, DN_DIM:] for r_ in dr]
        db = each(lambda a, b: a * b, da, gam)
        dq_m = each(lambda a, b: a * b, dp, gam)
        e = each(lambda a, bm, p_, qm, g_: (a * bm + p_ * qm) * g_, da, bmat, dp, qmat, gam)
        dkb = each(lambda b_, k_, r_, e_: _mx(b_, k_) + r_ * e_, db, kh, drw, eg)
        dk = each(lambda b_, kb_, m_, q_, d_, f_: _mx_tn(b_, kb_) + _mx_tn(m_, q_) + d_ * f_, db, kb, dq_m, qh, dkd, kdf)
        dq = each(lambda m_, k_, d_, e_: _mx(m_, k_) + d_ * e_, dq_m, kh, dqd, eg)
        tk = each(lambda a, b: rsum(a * b), dkd, kd)
        dbeta_all = jnp.zeros((DN_CHUNK, 128), F32)
        dgc_all = jnp.zeros((DN_CHUNK, 128), F32)
        for h in hs:
            dgc = (jnp.sum(e[h], axis=1, keepdims=True) - jnp.sum(e[h].T, axis=1, keepdims=True)
                   + rsum(dqd[h] * qd[h]) - tk[h] + rsum(drw[h] * rhs_w[h]))
            dgl = jnp.sum(tk[h], axis=0, keepdims=True) + ddec[h] * dec[h]
            dgc = dgc + jnp.where(rowi == DN_CHUNK - 1, dgl, 0.0)
            dbeta = rsum(dru[h] * vh[h]) + rsum(dkb[h] * kh[h])
            dq_ref[:, sl[h]] = dq[h]
            dk_ref[:, sl[h]] = dk[h] + dkb[h] * beta[h]
            dv_ref[:, sl[h]] = dru[h] * beta[h]
            dbeta_all = jnp.where(lane == h, dbeta, dbeta_all)
            dgc_all = jnp.where(lane == DN_HEADS + h, dgc, dgc_all)
        dbg_ref[...] = dbeta_all + _hi_tn(ltri, dgc_all)

    rev = lambda n: (n_chunks - 1 - n, 0)
    row = pl.BlockSpec((DN_CHUNK, DN_W), rev)
    small = pl.BlockSpec((DN_CHUNK, 128), rev)
    return pl.pallas_call(
        body, name=name, grid=(n_chunks,),
        in_specs=[row, row, row, small,
                  pl.BlockSpec((1, DN_HEADS, DN_DIM, DN_DIM), lambda n: (n_chunks - 1 - n, 0, 0, 0)),
                  pl.BlockSpec((1, DN_HEADS, DN_CHUNK, DN_CHUNK), lambda n: (n_chunks - 1 - n, 0, 0, 0)), row],
        out_specs=(row, row, row, small),
        out_shape=(jax.ShapeDtypeStruct((t, DN_W), F32),) * 3 + (jax.ShapeDtypeStruct((t, 128), F32),),
        scratch_shapes=[pltpu.VMEM((DN_HEADS, DN_DIM, DN_DIM), F32)],
        compiler_params=_cp("arbitrary"))(q, k, v, bg, sall, tall, do)


def gdn_out_fwd(o, proj, o_gain, name, tm=256):
    t = o.shape[0]

    def body(o_ref, z_ref, g_ref, y_ref, yt_ref):
        for h in range(DN_HEADS):
            sl = slice(h * DN_DIM, (h + 1) * DN_DIM)
            ov, zv = o_ref[:, sl], z_ref[:, sl]
            r = lax.rsqrt(jnp.mean(ov * ov, axis=-1, keepdims=True) + EPS)
            y = ov * r * g_ref[...] * (zv * _sigmoid(zv))
            y_ref[:, sl] = y.astype(y_ref.dtype)
            yt_ref[sl, :] = y.T.astype(yt_ref.dtype)

    row = pl.BlockSpec((tm, DN_W), lambda i: (i, 0))
    return pl.pallas_call(
        body, name=name, grid=(t // tm,),
        in_specs=[row, pl.BlockSpec((tm, DN_W), lambda i: (i, _Z_COL)), pl.BlockSpec((1, DN_DIM), lambda i: (0, 0))],
        out_specs=(row, pl.BlockSpec((DN_W, tm), lambda i: (0, i))),
        out_shape=(jax.ShapeDtypeStruct((t, DN_W), MXU_DTYPE), jax.ShapeDtypeStruct((DN_W, t), MXU_DTYPE)),
        compiler_params=_cp("parallel"))(o, proj, o_gain)


def gdn_out_bwd(o, proj, o_gain, dy, name, tm=256):
    t = o.shape[0]

    def body(o_ref, z_ref, g_ref, dy_ref, do_ref, dz_ref, dg_ref):
        i = pl.program_id(0)
        dg = jnp.zeros((1, DN_DIM), F32)
        for h in range(DN_HEADS):
            sl = slice(h * DN_DIM, (h + 1) * DN_DIM)
            ov, zv, dyv = o_ref[:, sl], z_ref[:, sl], dy_ref[:, sl]
            r = lax.rsqrt(jnp.mean(ov * ov, axis=-1, keepdims=True) + EPS)
            oh = ov * r
            sg = _sigmoid(zv)
            dz_ref[:, sl] = (dyv * oh * g_ref[...] * sg * (1.0 + zv * (1.0 - sg))).astype(dz_ref.dtype)
            don = dyv * (zv * sg)
            dg = dg + jnp.sum(don * oh, axis=0, keepdims=True)
            doh = don * g_ref[...]
            do_ref[:, sl] = r * (doh - oh * jnp.mean(doh * oh, axis=-1, keepdims=True))

        @pl.when(i == 0)
        def _():
            dg_ref[...] = dg

        @pl.when(i > 0)
        def _():
            dg_ref[...] += dg

    row = pl.BlockSpec((tm, DN_W), lambda i: (i, 0))
    one = pl.BlockSpec((1, DN_DIM), lambda i: (0, 0))
    return pl.pallas_call(
        body, name=name, grid=(t // tm,),
        in_specs=[row, pl.BlockSpec((tm, DN_W), lambda i: (i, _Z_COL)), one, row],
        out_specs=(row, row, one),
        out_shape=(jax.ShapeDtypeStruct((t, DN_W), F32), jax.ShapeDtypeStruct((t, DN_W), MXU_DTYPE),
                   jax.ShapeDtypeStruct((1, DN_DIM), F32)),
        compiler_params=_cp("arbitrary"))(o, proj, o_gain, dy)


def _peer(k):
    x, y, c = lax.axis_index("x"), lax.axis_index("y"), lax.axis_index("c")
    px = 1 - x if k & 4 else x
    py = 1 - y if k & 2 else y
    pc = 1 - c if k & 1 else c
    return (px, py, pc), 4 * px + 2 * py + pc


_HBM = pl.BlockSpec(memory_space=pltpu.HBM)
_SEM = pl.BlockSpec(memory_space=pltpu.SEMAPHORE)
_DATAFLOW = pltpu.SideEffectType.DATAFLOW_SIDE_EFFECTING
N_PEER = N_DEV - 1


def send_start(srcs, name, scatter, after):
    na = len(srcs)
    ns = (2 * N_PEER + 1) * na
    lands = [lax.empty((N_DEV,) + (s.shape[1:] if scatter else s.shape), s.dtype) for s in srcs]
    extra = [] if after is None else [after]

    def body(*refs):
        src_refs, land_refs = refs[:na], refs[na:2 * na]
        sems = refs[2 * na + len(extra):2 * na + len(extra) + ns]
        land_out, token = refs[-1 - na:-1], refs[-1]
        _, me = _peer(0)
        for a in range(na):
            pltpu.make_async_copy(src_refs[a].at[me] if scatter else src_refs[a], land_out[a].at[me],
                                  sems[2 * N_PEER * na + a]).start()
        for k in range(1, N_DEV):
            peer, pid = _peer(k)
            for a in range(na):
                pltpu.make_async_remote_copy(
                    src_ref=src_refs[a].at[pid] if scatter else src_refs[a], dst_ref=land_refs[a].at[me],
                    send_sem=sems[2 * (a * N_PEER + k - 1)], recv_sem=sems[2 * (a * N_PEER + k - 1) + 1],
                    device_id=peer, device_id_type=MESH).start()
        token[...] = jnp.zeros_like(token)

    hbm = lambda arrs: tuple(pltpu.HBM(a.shape, a.dtype) for a in arrs)
    outs = pl.pallas_call(
        body, name=name,
        out_shape=(pltpu.SemaphoreType.DMA(()),) * ns + hbm(srcs) + hbm(lands) + (jax.ShapeDtypeStruct((8, 128), F32),),
        in_specs=[_HBM] * (2 * na) + [pl.BlockSpec(memory_space=pl.ANY)] * len(extra),
        out_specs=(_SEM,) * ns + (_HBM,) * (2 * na) + (pl.BlockSpec(memory_space=pltpu.VMEM),),
        input_output_aliases={i: ns + i for i in range(2 * na)},
        compiler_params=pltpu.CompilerParams(has_side_effects=_DATAFLOW),
    )(*[pltpu.with_memory_space_constraint(a, pltpu.HBM) for a in list(srcs) + lands], *extra)
    return outs[:ns], outs[ns:ns + na], outs[ns + na:ns + 2 * na], outs[-1]


def send_wait(sems, srcs_thru, lands_thru, name, scatter, after):
    na = len(srcs_thru)
    ns = (2 * N_PEER + 1) * na

    def body(*refs):
        src_refs, land_refs, sm = refs[:na], refs[na:2 * na], refs[2 * na:2 * na + ns]
        _, me = _peer(0)
        for a in range(na):
            pltpu.make_async_copy(src_refs[a].at[me] if scatter else src_refs[a], land_refs[a].at[me],
                                  sm[2 * N_PEER * na + a]).wait()
        for k in range(1, N_DEV):
            peer, pid = _peer(k)
            for a in range(na):
                cp = pltpu.make_async_remote_copy(
                    src_ref=src_refs[a].at[pid] if scatter else src_refs[a], dst_ref=land_refs[a].at[pid],
                    send_sem=sm[2 * (a * N_PEER + k - 1)], recv_sem=sm[2 * (a * N_PEER + k - 1) + 1],
                    device_id=peer, device_id_type=MESH)
                cp.wait_send()
                cp.wait_recv()

    hbm = lambda arrs: tuple(pltpu.HBM(a.shape, a.dtype) for a in arrs)
    outs = pl.pallas_call(
        body, name=name, out_shape=hbm(srcs_thru) + hbm(lands_thru),
        in_specs=[_HBM] * (2 * na) + [_SEM] * ns + [pl.BlockSpec(memory_space=pl.ANY)], out_specs=(_HBM,) * (2 * na),
        input_output_aliases={i: i for i in range(2 * na)},
        compiler_params=pltpu.CompilerParams(has_side_effects=_DATAFLOW),
    )(*srcs_thru, *lands_thru, *sems, after)
    return outs[na:]


def _adamw(w, g, m, v):
    m = ADAM_B1 * m + (1.0 - ADAM_B1) * g
    v = ADAM_B2 * v + (1.0 - ADAM_B2) * (g * g)
    m_hat = m / (1.0 - ADAM_B1 ** ADAM_STEP)
    v_hat = v / (1.0 - ADAM_B2 ** ADAM_STEP)
    return -ADAM_LR * (m_hat / (jnp.sqrt(v_hat) + ADAM_EPS) + ADAM_WD * w), m, v


def adam_sum(w, pieces, m, v, name, layer=0, into=None):
    nl, r, c = w.shape
    tr = r
    for cand in (256, 128, 64, 32, 16, 8):
        if r % cand == 0:
            tr = cand
            break

    def body(w_ref, p_ref, m_ref, v_ref, *rest):
        g_ref, d_ref, nm_ref, nv_ref = rest[-4:]
        g = p_ref[0].astype(F32)
        for s in range(1, N_DEV):
            g = g + p_ref[s].astype(F32)
        g_ref[0] = g
        d_ref[0], nm_ref[0], nv_ref[0] = _adamw(w_ref[0], g, m_ref[0], v_ref[0])

    row = pl.BlockSpec((1, tr, c), lambda i: (layer, i, 0))
    out = jax.ShapeDtypeStruct((nl, r, c), F32)
    extra = [] if into is None else list(into)
    return pl.pallas_call(
        body, name=name, grid=(r // tr,),
        in_specs=[row, pl.BlockSpec((N_DEV, tr, c), lambda i: (0, i, 0)), row, row]
        + [pl.BlockSpec(memory_space=pl.ANY)] * len(extra),
        out_specs=(row,) * 4, out_shape=(out,) * 4,
        input_output_aliases={4 + i: i for i in range(len(extra))},
        compiler_params=_cp("parallel"))(w, pieces, m, v, *extra)


def sum_rows(gathered, name):
    _, r, c = gathered.shape

    def body(p_ref, o_ref):
        g = p_ref[0]
        for s in range(1, N_DEV):
            g = g + p_ref[s]
        o_ref[...] = g

    return pl.pallas_call(body, name=name, out_shape=jax.ShapeDtypeStruct((r, c), F32))(gathered)


def adam_small(w, g, m, v, name):
    def body(w_ref, g_ref, m_ref, v_ref, d_ref, nm_ref, nv_ref):
        d_ref[...], nm_ref[...], nv_ref[...] = _adamw(w_ref[...], g_ref[...], m_ref[...], v_ref[...])

    out = jax.ShapeDtypeStruct(w.shape, F32)
    return pl.pallas_call(body, name=name, out_shape=(out,) * 3)(w, g, m, v)


def _rope_tables(t):
    inv_freq = 10000.0 ** (-jnp.arange(0, HEAD_DIM, 2, dtype=F32) / HEAD_DIM)
    ang = jnp.arange(t, dtype=F32)[:, None] * inv_freq[None, :]
    cos, sin = jnp.cos(ang), jnp.sin(ang)
    return jnp.concatenate([cos, cos], axis=-1), jnp.concatenate([sin, sin], axis=-1)


def _lane_row(vec8):
    return jnp.pad(vec8.reshape(1, DN_HEADS), ((0, 0), (DN_HEADS, 128 - 2 * DN_HEADS)))


def _ffn_bwd(x, norm_g, w_gu, w_d, saved, dy, tag, after=None):
    ft, gu, at = saved
    dgu = ffn_dact(dy, w_d, gu, f"{tag}_d_gate_up", after=after)
    dwd = mm_at(at, dy, f"{tag}_dw_down")
    dwgu = mm_at(ft, dgu, f"{tag}_dw_gate_up")
    dx, dg = mm_nt_rms_bwd(dgu, w_gu, x, norm_g, dy, f"{tag}_d_norm")
    return dx, dwgu, dwd, dg


def local_step(x, target, small, weights_of, grads_out, after=None):
    t = x.shape[0]
    cosf, sinf = _rope_tables(t)
    alog_row, dtb_row = _lane_row(small["odd_a_log"]), _lane_row(small["odd_dt_bias"])

    h0, h0t = rms_fwd(x, small["even_norm"], "even_norm", after=after)
    we = weights_of("even", h0)
    small = {**small, **we.get("small", {})}
    proj0 = mm_nn(h0, we["w_in"], "even_in_proj")
    qr, kr = qk_prep_fwd(proj0, small["even_q_gain"], small["even_k_gain"], cosf, sinf, "even_qk_prep")
    y_attn, y_attn_t, lse = swa_fwd(qr, kr, proj0, small["even_sinks"], "even_swa")
    y_conv, y_conv_t = gconv_fwd(proj0, small["even_conv_w"], "even_gconv")
    mix0 = jnp.concatenate([y_attn.astype(MXU_DTYPE), y_conv], axis=-1)
    mix0_t = jnp.concatenate([y_attn_t, y_conv_t], axis=0)
    x1, f0, f0t = mm_nn_res_norm(mix0, we["w_out"], x, small["ffn_norm0"], "even_out_proj")
    w0 = weights_of("ffn0", x1)
    gu0, a0, a0t = ffn_up(f0, w0["gate_up"], "ffn0_gate_up")
    ffn0 = (f0t, gu0, a0t)
    x2, h1, h1t = mm_nn_res_norm(a0, w0["down"], x1, small["odd_norm"], "ffn0_down")

    wo = weights_of("odd", x2)
    proj1 = mm_nn(h1, wo["w_in"], "odd_in_proj")
    qn, kn, vs, bg = gdn_prep_fwd(proj1, small["odd_conv_w"], alog_row, dtb_row, "odd_prep")
    o, sall, tall = gdn_fwd(qn, kn, vs, bg, "odd_delta_rule")
    og, ogt = gdn_out_fwd(o, proj1, small["odd_o_gain"], "odd_gate_norm")
    x3, f1, f1t = mm_nn_res_norm(og, wo["w_out"], x2, small["ffn_norm1"], "odd_out_proj")
    w1 = weights_of("ffn1", x3)
    gu1, a1, a1t = ffn_up(f1, w1["gate_up"], "ffn1_gate_up")
    ffn1 = (f1t, gu1, a1t)
    dy, loss_row = mm_nn_res_loss(a1, w1["down"], x3, target, "ffn1_down_loss")

    gs = {}
    dx3, dwgu, dwd, gs["ffn_norm1"] = _ffn_bwd(x3, small["ffn_norm1"], w1["gate_up"], w1["down"], ffn1, dy, "ffn1")
    tok = grads_out("ffn1", {"gate_up": dwgu, "down": dwd})

    dog = mm_nt(dx3, wo["w_out"], "odd_d_gated", after=tok)
    dwo = mm_at(ogt, dx3, "odd_dw_out")
    do, dz, gs["odd_o_gain"] = gdn_out_bwd(o, proj1, small["odd_o_gain"], dog, "odd_d_gate_norm")
    dqn, dkn, dvs, dbg = gdn_bwd(qn, kn, vs, bg, sall, tall, do, "odd_d_delta_rule")
    dqkv, dba, gs["odd_conv_w"], ddt_row, dal_row = gdn_prep_bwd(
        proj1, small["odd_conv_w"], alog_row, dtb_row, dqn, dkn, dvs, dbg, "odd_d_prep")
    gs["odd_dt_bias"] = ddt_row[:, DN_HEADS:2 * DN_HEADS]
    gs["odd_a_log"] = dal_row[:, DN_HEADS:2 * DN_HEADS]
    dproj1 = jnp.concatenate([dqkv, dz, dba], axis=-1)
    dwi = mm_at(h1t, dproj1, "odd_dw_in")
    dx2, gs["odd_norm"] = mm_nt_rms_bwd(dproj1, wo["w_in"], x2, small["odd_norm"], dx3, "odd_d_norm")
    tok = grads_out("odd", {"w_in": dwi, "w_out": dwo})

    dx1, dwgu, dwd, gs["ffn_norm0"] = _ffn_bwd(x1, small["ffn_norm0"], w0["gate_up"], w0["down"], ffn0, dx2, "ffn0",
                                               after=tok)
    tok = grads_out("ffn0", {"gate_up": dwgu, "down": dwd})

    dmix = mm_nt(dx1, we["w_out"], "even_d_mix", after=tok)
    dwo = mm_at(mix0_t, dx1, "even_dw_out")
    dqr, dkr, dv, gs["even_sinks"] = swa_bwd(qr, kr, proj0, small["even_sinks"], y_attn, lse, dmix, "even_d_swa")
    dqk, gs["even_q_gain"], gs["even_k_gain"] = qk_prep_bwd(
        proj0, small["even_q_gain"], small["even_k_gain"], cosf, sinf, dqr, dkr, "even_d_qk_prep")
    dgb, dgc, dxi, gs["even_conv_w"] = gconv_bwd(proj0, small["even_conv_w"], dmix, "even_d_gconv")
    dproj0 = jnp.concatenate([dqk, dv, dgb, dgc, dxi], axis=-1)
    dwi = mm_at(h0t, dproj0, "even_dw_in")
    tok = grads_out("even", {"w_in": dwi, "w_out": dwo})
    grad_x, gs["even_norm"] = mm_nt_rms_bwd(dproj0, we["w_in"], x, small["even_norm"], dx1, "even_d_norm", after=tok)
    return loss_row, grad_x, gs


_SMALL_ORDER = ("even_norm", "even_q_gain", "even_k_gain", "even_sinks", "odd_a_log", "odd_dt_bias", "odd_o_gain",
                "ffn_norm0", "ffn_norm1", "odd_norm", "even_conv_w", "odd_conv_w")
_SMALL_SIZE = {"even_norm": 1024, "even_q_gain": 64, "even_k_gain": 64, "even_sinks": 8, "odd_a_log": 8,
               "odd_dt_bias": 8, "odd_o_gain": 128, "ffn_norm0": 1024, "ffn_norm1": 1024, "odd_norm": 1024,
               "even_conv_w": 3 * 512, "odd_conv_w": 4 * 3072}
_N_REPL = 9


def _pack_rows(vals):
    flat = jnp.concatenate([v.reshape(-1) for v in vals])
    pad = (-flat.shape[0]) % 1024
    return jnp.pad(flat, (0, pad)).reshape(-1, 128)


def _my_block(full, size, axis):
    me = 4 * lax.axis_index("x") + 2 * lax.axis_index("y") + lax.axis_index("c")
    return lax.dynamic_slice_in_dim(full, me * size, size, axis=axis)


def _col_gathered(g):
    return g.transpose(1, 0, 2).reshape(g.shape[1], N_DEV * g.shape[2])


def _col_pieces(dw):
    k, n8 = dw.shape
    return dw.reshape(k, N_DEV, n8 // N_DEV).transpose(1, 0, 2)


def kernel(x, even_norm, even_w_in, even_q_gain, even_k_gain, even_sinks, even_conv_w, even_w_out, odd_norm, odd_w_in, odd_conv_w, odd_a_log, odd_dt_bias, odd_o_gain, odd_w_out, ffn_norm, ffn_w_gate_up, ffn_w_down, loss_target, m_even_norm, m_even_w_in, m_even_q_gain, m_even_k_gain, m_even_sinks, m_even_conv_w, m_even_w_out, m_odd_norm, m_odd_w_in, m_odd_conv_w, m_odd_a_log, m_odd_dt_bias, m_odd_o_gain, m_odd_w_out, m_ffn_norm, m_ffn_w_gate_up, m_ffn_w_down, v_even_norm, v_even_w_in, v_even_q_gain, v_even_k_gain, v_even_sinks, v_even_conv_w, v_even_w_out, v_odd_norm, v_odd_w_in, v_odd_conv_w, v_odd_a_log, v_odd_dt_bias, v_odd_o_gain, v_odd_w_out, v_ffn_norm, v_ffn_w_gate_up, v_ffn_w_down):
    t = x.shape[1]
    d = D_MODEL

    me = 4 * lax.axis_index("x") + 2 * lax.axis_index("y") + lax.axis_index("c")
    fpd = D_FF // N_DEV
    shard = {
        "even": {"w_in": even_w_in.reshape(d, EVEN_IN_W // N_DEV), "w_out": even_w_out.reshape(d // N_DEV, d)},
        "ffn0": {"gate_up": ffn_w_gate_up[0], "down": ffn_w_down[0]},
        "odd": {"w_in": odd_w_in.reshape(d, ODD_IN_W // N_DEV), "w_out": odd_w_out.reshape(d // N_DEV, d)},
        "ffn1": {"gate_up": ffn_w_gate_up[1], "down": ffn_w_down[1]},
    }
    given = {
        ("even", "w_in"): ("even_w_in", even_w_in, m_even_w_in, v_even_w_in, 0),
        ("even", "w_out"): ("even_w_out", even_w_out, m_even_w_out, v_even_w_out, 0),
        ("odd", "w_in"): ("odd_w_in", odd_w_in, m_odd_w_in, v_odd_w_in, 0),
        ("odd", "w_out"): ("odd_w_out", odd_w_out, m_odd_w_out, v_odd_w_out, 0),
        ("ffn0", "gate_up"): ("ffn_w_gate_up", ffn_w_gate_up, m_ffn_w_gate_up, v_ffn_w_gate_up, 0),
        ("ffn1", "gate_up"): ("ffn_w_gate_up", ffn_w_gate_up, m_ffn_w_gate_up, v_ffn_w_gate_up, 1),
        ("ffn0", "down"): ("ffn_w_down", ffn_w_down, m_ffn_w_down, v_ffn_w_down, 0),
        ("ffn1", "down"): ("ffn_w_down", ffn_w_down, m_ffn_w_down, v_ffn_w_down, 1),
    }

    def whole(group, parts):
        col, row = tuple(shard[group])
        w_col = _gu_gathered(parts[0]) if col == "gate_up" else _col_gathered(parts[0])
        if group == "odd":
            w_col = jnp.pad(w_col, ((0, 0), (0, ODD_IN_PAD - ODD_IN_W)))
        return {col: w_col, row: parts[1].reshape(-1, d)}

    wire = {g: [a.astype(MXU_DTYPE) for a in shard[g].values()] for g in shard}
    wire["even"].append(_pack_rows([odd_norm, even_conv_w, odd_conv_w]))
    gathers, tok = {}, None
    for g in shard:
        sems, srcs_thru, lands_thru, tok = send_start(wire[g], f"gather_{g}_start", False, tok)
        gathers[g] = (sems, srcs_thru, lands_thru)
    o1 = d // N_DEV
    o2 = o1 + 3 * CONV_CH // N_DEV

    def weights_of(group, after):
        lands = send_wait(*gathers[group], f"gather_{group}_wait", False, after)
        out = whole(group, lands)
        if group == "even":
            sg = lands[2].reshape(N_DEV, -1)
            out["small"] = {
                "odd_norm": sg[:, :o1].reshape(1, d),
                "even_conv_w": sg[:, o1:o2].reshape(N_DEV, 3, CONV_CH // N_DEV).transpose(1, 0, 2).reshape(3, CONV_CH),
                "odd_conv_w": sg[:, o2:o2 + 4 * _QKV_W // N_DEV].reshape(N_DEV, 4, _QKV_W // N_DEV)
                .transpose(1, 0, 2).reshape(4, _QKV_W),
            }
        return out

    sent = {}

    def grads_out(group, dws):
        col, row = tuple(shard[group])
        n_cols = N_DEV * shard[group][col].shape[1]
        pieces = [_gu_pieces(dws[col]) if col == "gate_up" else _col_pieces(dws[col][:, :n_cols]),
                  dws[row].reshape((N_DEV,) + shard[group][row].shape)]
        sems, srcs_thru, lands_thru, token = send_start(pieces, f"exchange_{group}_start", True, None)
        sent[group] = (sems, srcs_thru, lands_thru, pieces)
        return token

    small = {
        "even_norm": even_norm, "even_q_gain": even_q_gain, "even_k_gain": even_k_gain, "even_sinks": even_sinks,
        "odd_a_log": odd_a_log.reshape(-1), "odd_dt_bias": odd_dt_bias.reshape(-1), "odd_o_gain": odd_o_gain,
        "ffn_norm0": ffn_norm[0:1], "ffn_norm1": ffn_norm[1:2],
    }

    loss_row, grad_x, gs = local_step(x.reshape(t, d), loss_target.reshape(t, d), small, weights_of, grads_out, after=tok)

    rows = _pack_rows([gs[n] for n in _SMALL_ORDER] + [loss_row[:, 0:1]])
    small_sent = send_start([rows], "gather_small_grads_start", False, None)

    res, behind = {}, small_sent[3]
    for g in ("ffn1", "odd", "ffn0", "even"):
        sems, srcs_thru, lands_thru, pieces = sent[g]
        lands = send_wait(sems, srcs_thru, lands_thru, f"exchange_{g}_wait", True, behind)
        for key, pcs in zip(shard[g], lands):
            name, w_, m_, v_, layer = given[g, key]
            res[name] = adam_sum(w_, pcs, m_, v_, f"adamw_{g}_{key}", layer=layer, into=res.get(name))
        behind = res[name][0]

    (rows_g,) = send_wait(*small_sent[:3], "gather_small_grads_wait", False, behind)
    tot = sum_rows(rows_g, "sum_small_grads").reshape(-1)
    off, sgrad = 0, {}
    for n in _SMALL_ORDER:
        sgrad[n] = tot[off:off + _SMALL_SIZE[n]]
        off += _SMALL_SIZE[n]
    loss = tot[off]

    repl = _SMALL_ORDER[:_N_REPL]
    repl_w = {"even_norm": even_norm, "even_q_gain": even_q_gain, "even_k_gain": even_k_gain, "even_sinks": even_sinks,
              "odd_a_log": odd_a_log, "odd_dt_bias": odd_dt_bias, "odd_o_gain": odd_o_gain,
              "ffn_norm0": ffn_norm[0], "ffn_norm1": ffn_norm[1]}
    repl_m = {"even_norm": m_even_norm, "even_q_gain": m_even_q_gain, "even_k_gain": m_even_k_gain,
              "even_sinks": m_even_sinks, "odd_a_log": m_odd_a_log, "odd_dt_bias": m_odd_dt_bias,
              "odd_o_gain": m_odd_o_gain, "ffn_norm0": m_ffn_norm[0], "ffn_norm1": m_ffn_norm[1]}
    repl_v = {"even_norm": v_even_norm, "even_q_gain": v_even_q_gain, "even_k_gain": v_even_k_gain,
              "even_sinks": v_even_sinks, "odd_a_log": v_odd_a_log, "odd_dt_bias": v_odd_dt_bias,
              "odd_o_gain": v_odd_o_gain, "ffn_norm0": v_ffn_norm[0], "ffn_norm1": v_ffn_norm[1]}
    pk = lambda dct: _pack_rows([dct[n] for n in repl])
    pd_, pm_, pv_ = adam_small(pk(repl_w), pk(sgrad), pk(repl_m), pk(repl_v), "adamw_replicated")
    sres = {}
    off = 0
    for n in repl:
        sz = _SMALL_SIZE[n]
        sres[n] = (sgrad[n], pd_.reshape(-1)[off:off + sz], pm_.reshape(-1)[off:off + sz], pv_.reshape(-1)[off:off + sz])
        off += sz
    g_on = _my_block(sgrad["odd_norm"].reshape(1, d), d // N_DEV, 1)
    g_ec = _my_block(sgrad["even_conv_w"].reshape(3, CONV_CH), CONV_CH // N_DEV, 1)
    g_oc = _my_block(sgrad["odd_conv_w"].reshape(4, _QKV_W), _QKV_W // N_DEV, 1)
    shard_w = _pack_rows([odd_norm, even_conv_w, odd_conv_w])
    sd_, sm_, sv_ = adam_small(shard_w, _pack_rows([g_on, g_ec, g_oc]),
                               _pack_rows([m_odd_norm, m_even_conv_w, m_odd_conv_w]),
                               _pack_rows([v_odd_norm, v_even_conv_w, v_odd_conv_w]), "adamw_sharded_small")
    off = 0
    for n, gfull, like in (("odd_norm", g_on, odd_norm), ("even_conv_w", g_ec, even_conv_w), ("odd_conv_w", g_oc, odd_conv_w)):
        sz = like.size
        sres[n] = (gfull, sd_.reshape(-1)[off:off + sz], sm_.reshape(-1)[off:off + sz], sv_.reshape(-1)[off:off + sz])
        off += sz

    def small_out(name, like, kind):
        if name == "ffn_norm":
            return jnp.stack([sres["ffn_norm0"][kind], sres["ffn_norm1"][kind]]).reshape(like.shape)
        return sres[name][kind].reshape(like.shape)

    order = (("even_norm", even_norm), ("even_w_in", even_w_in), ("even_q_gain", even_q_gain),
             ("even_k_gain", even_k_gain), ("even_sinks", even_sinks), ("even_conv_w", even_conv_w),
             ("even_w_out", even_w_out), ("odd_norm", odd_norm), ("odd_w_in", odd_w_in), ("odd_conv_w", odd_conv_w),
             ("odd_a_log", odd_a_log), ("odd_dt_bias", odd_dt_bias), ("odd_o_gain", odd_o_gain),
             ("odd_w_out", odd_w_out), ("ffn_norm", ffn_norm), ("ffn_w_gate_up", ffn_w_gate_up),
             ("ffn_w_down", ffn_w_down))
    outs = [loss, grad_x.reshape(x.shape)]
    for kind in range(4):
        for name, like in order:
            outs.append(res[name][kind] if name in res else small_out(name, like, kind))
    return tuple(outs)
```

```python
import jax
import jax.numpy as jnp
import numpy as np
from jax import lax
from jax.experimental import pallas as pl
from jax.experimental.pallas import tpu as pltpu

F32 = jnp.float32
MXU_DTYPE = jnp.bfloat16
HI = lax.Precision.HIGH
EPS = 1e-6
N_DEV = 8
D_MODEL = 1024
HEAD_DIM = 64
ATTN_HEADS = 8
KV_HEADS = 2
ATTN_BLOCK = 128
Q_W = 512
KV_W = 128
CONV_CH = 512
EVEN_IN_W = 2304
DN_HEADS = 8
DN_DIM = 128
DN_W = 1024
DN_CHUNK = 64
ODD_IN_W = 4112
ODD_IN_PAD = 4224
D_FF = 2816
NEG = -1e30
VMEM_LIMIT = 56 * 1024 * 1024
ADAM_LR, ADAM_B1, ADAM_B2, ADAM_EPS, ADAM_WD, ADAM_STEP = 0.001, 0.9, 0.999, 1e-08, 0.01, 10
MESH = pl.DeviceIdType.MESH


def _cp(*sem):
    return pltpu.CompilerParams(dimension_semantics=sem, vmem_limit_bytes=VMEM_LIMIT)


def _pick(n, cap):
    best = 128
    for t in range(128, cap + 1, 128):
        if n % t == 0:
            best = t
    return best


def _mx(a, b):
    return jnp.dot(a.astype(MXU_DTYPE), b.astype(MXU_DTYPE), preferred_element_type=F32)


def _mx_nt(a, b):
    return lax.dot_general(a.astype(MXU_DTYPE), b.astype(MXU_DTYPE), (((1,), (1,)), ((), ())),
                           preferred_element_type=F32)


def _mx_tn(a, b):
    return lax.dot_general(a.astype(MXU_DTYPE), b.astype(MXU_DTYPE), (((0,), (0,)), ((), ())),
                           preferred_element_type=F32)


def _hi(a, b):
    return jnp.dot(a, b, precision=HI, preferred_element_type=F32)


def _hi_nt(a, b):
    return lax.dot_general(a, b, (((1,), (1,)), ((), ())), precision=HI, preferred_element_type=F32)


def _hi_tn(a, b):
    return lax.dot_general(a, b, (((0,), (0,)), ((), ())), precision=HI, preferred_element_type=F32)


def _sigmoid(x):
    return 0.5 * jnp.tanh(0.5 * x) + 0.5


def _softplus(x):
    return jnp.maximum(x, 0.0) + jnp.log(1.0 + jnp.exp(-jnp.abs(x)))


def mm_nn(a, b, name, res=None, out_dtype=F32, tm=1024):
    m, k = a.shape
    _, n = b.shape
    tn = _pick(n, 1536)
    tm = min(tm, m)

    def body(*refs):
        a_ref, b_ref = refs[0], refs[1]
        o_ref = refs[-1]
        acc = _mx(a_ref[...], b_ref[...])
        if res is not None:
            acc = acc + refs[2][...]
        o_ref[...] = acc.astype(o_ref.dtype)

    in_specs = [pl.BlockSpec((tm, k), lambda j, i: (i, 0)), pl.BlockSpec((k, tn), lambda j, i: (0, j))]
    args = [a, b]
    if res is not None:
        in_specs.append(pl.BlockSpec((tm, tn), lambda j, i: (i, j)))
        args.append(res)
    return pl.pallas_call(
        body, name=name, grid=(n // tn, m // tm), in_specs=in_specs,
        out_specs=pl.BlockSpec((tm, tn), lambda j, i: (i, j)),
        out_shape=jax.ShapeDtypeStruct((m, n), out_dtype), compiler_params=_cp("parallel", "parallel"))(*args)


def mm_nn_res_norm(a, b, res, g, name, tm=512):
    t, k = a.shape
    d = b.shape[1]
    tm = min(tm, t)

    def body(a_ref, b_ref, res_ref, g_ref, y_ref, h_ref, ht_ref):
        y = res_ref[...] + _mx(a_ref[...], b_ref[...])
        y_ref[...] = y
        h = y * lax.rsqrt(jnp.mean(y * y, axis=-1, keepdims=True) + EPS) * g_ref[...]
        h_ref[...] = h.astype(h_ref.dtype)
        ht_ref[...] = h.T.astype(ht_ref.dtype)

    row = pl.BlockSpec((tm, d), lambda i: (i, 0))
    return pl.pallas_call(
        body, name=name, grid=(t // tm,),
        in_specs=[pl.BlockSpec((tm, k), lambda i: (i, 0)), pl.BlockSpec((k, d), lambda i: (0, 0)), row,
                  pl.BlockSpec((1, d), lambda i: (0, 0))],
        out_specs=(row, row, pl.BlockSpec((d, tm), lambda i: (0, i))),
        out_shape=(jax.ShapeDtypeStruct((t, d), F32), jax.ShapeDtypeStruct((t, d), MXU_DTYPE),
                   jax.ShapeDtypeStruct((d, t), MXU_DTYPE)),
        compiler_params=_cp("parallel"))(a, b, res, g)


def mm_nt(a, b, name, out_dtype=F32, tm=1024, after=None):
    m, k = a.shape
    n, _ = b.shape
    tn = _pick(n, 512 if k > 3000 else 1536)
    tm = min(tm, m)

    def body(a_ref, b_ref, *rest):
        o_ref = rest[-1]
        o_ref[...] = _mx_nt(a_ref[...], b_ref[...]).astype(o_ref.dtype)

    in_specs = [pl.BlockSpec((tm, k), lambda j, i: (i, 0)), pl.BlockSpec((tn, k), lambda j, i: (j, 0))]
    args = [a, b]
    if after is not None:
        in_specs.append(pl.BlockSpec(memory_space=pl.ANY))
        args.append(after)
    return pl.pallas_call(
        body, name=name, grid=(n // tn, m // tm), in_specs=in_specs,
        out_specs=pl.BlockSpec((tm, tn), lambda j, i: (i, j)),
        out_shape=jax.ShapeDtypeStruct((m, n), out_dtype), compiler_params=_cp("parallel", "parallel"))(*args)


def mm_at(at, b, name, tk=1024):
    m, kk = at.shape
    _, n = b.shape
    tm, tn, tk = _pick(m, 1408), _pick(n, 2816), min(tk, kk)
    nk = kk // tk

    def body(a_ref, b_ref, o_ref, acc_ref):
        k = pl.program_id(2)
        p = _mx(a_ref[...], b_ref[...])
        acc = jnp.where(k == 0, p, acc_ref[...] + p)
        acc_ref[...] = acc

        @pl.when(k == nk - 1)
        def _():
            o_ref[...] = acc.astype(o_ref.dtype)

    return pl.pallas_call(
        body, name=name, grid=(m // tm, n // tn, nk),
        in_specs=[pl.BlockSpec((tm, tk), lambda i, j, k: (i, k)), pl.BlockSpec((tk, tn), lambda i, j, k: (k, j))],
        out_specs=pl.BlockSpec((tm, tn), lambda i, j, k: (i, j)),
        out_shape=jax.ShapeDtypeStruct((m, n), MXU_DTYPE), scratch_shapes=[pltpu.VMEM((tm, tn), F32)],
        compiler_params=_cp("parallel", "parallel", "arbitrary"))(at, b)


def rms_fwd(x, g, name, tm=512, after=None):
    t, d = x.shape

    def body(x_ref, g_ref, *rest):
        o_ref, ot_ref = rest[-2:]
        xv = x_ref[...]
        r = lax.rsqrt(jnp.mean(xv * xv, axis=-1, keepdims=True) + EPS)
        h = xv * r * g_ref[...]
        o_ref[...] = h.astype(o_ref.dtype)
        ot_ref[...] = h.T.astype(ot_ref.dtype)

    in_specs = [pl.BlockSpec((tm, d), lambda i: (i, 0)), pl.BlockSpec((1, d), lambda i: (0, 0))]
    args = [x, g]
    if after is not None:
        in_specs.append(pl.BlockSpec(memory_space=pl.ANY))
        args.append(after)
    return pl.pallas_call(
        body, name=name, grid=(t // tm,), in_specs=in_specs,
        out_specs=(pl.BlockSpec((tm, d), lambda i: (i, 0)), pl.BlockSpec((d, tm), lambda i: (0, i))),
        out_shape=(jax.ShapeDtypeStruct((t, d), MXU_DTYPE), jax.ShapeDtypeStruct((d, t), MXU_DTYPE)),
        compiler_params=_cp("parallel"))(*args)


def mm_nt_rms_bwd(a, b, x, g, dres, name, tm=512, after=None):
    t, k = a.shape
    d = b.shape[0]
    tm = min(tm, t)

    def body(a_ref, b_ref, x_ref, g_ref, dres_ref, *rest):
        dx_ref, dg_ref = rest[-2:]
        dhv = _mx_nt(a_ref[...], b_ref[...])
        xv = x_ref[...]
        r = lax.rsqrt(jnp.mean(xv * xv, axis=-1, keepdims=True) + EPS)
        xh = xv * r
        dxh = dhv * g_ref[...]
        dx_ref[...] = dres_ref[...] + r * (dxh - xh * jnp.mean(dxh * xh, axis=-1, keepdims=True))
        part = jnp.sum(dhv * xh, axis=0, keepdims=True)
        dg_ref[...] = jnp.where(pl.program_id(0) == 0, part, dg_ref[...] + part)

    row = pl.BlockSpec((tm, d), lambda i: (i, 0))
    one = pl.BlockSpec((1, d), lambda i: (0, 0))
    in_specs = [pl.BlockSpec((tm, k), lambda i: (i, 0)), pl.BlockSpec((d, k), lambda i: (0, 0)), row, one, row]
    args = [a, b, x, g, dres]
    if after is not None:
        in_specs.append(pl.BlockSpec(memory_space=pl.ANY))
        args.append(after)
    return pl.pallas_call(
        body, name=name, grid=(t // tm,), in_specs=in_specs, out_specs=(row, one),
        out_shape=(jax.ShapeDtypeStruct((t, d), F32), jax.ShapeDtypeStruct((1, d), F32)),
        compiler_params=_cp("arbitrary"))(*args)


GU_TILE = 1408


_GU_PER_TILE = GU_TILE * N_DEV // (2 * D_FF)


def _gu_gathered(g):
    _, k, c = g.shape
    nj = N_DEV // (2 * _GU_PER_TILE)
    return g.reshape(2, nj, _GU_PER_TILE, k, c).transpose(3, 1, 0, 2, 4).reshape(k, N_DEV * c)


def _gu_pieces(dw):
    k, n8 = dw.shape
    nj = N_DEV // (2 * _GU_PER_TILE)
    return dw.reshape(k, nj, 2, _GU_PER_TILE, n8 // N_DEV).transpose(2, 1, 3, 0, 4).reshape(N_DEV, k, n8 // N_DEV)


def ffn_up(f, w, name, tm=512):
    t, d = f.shape

    def body(f_ref, w_ref, gu_ref, a_ref, at_ref):
        gu = _mx(f_ref[...], w_ref[...])
        gu_ref[...] = gu
        g, u = gu[:, :GU_TILE], gu[:, GU_TILE:]
        act = g * _sigmoid(g) * u
        a_ref[...] = act.astype(a_ref.dtype)
        at_ref[...] = act.T.astype(at_ref.dtype)

    return pl.pallas_call(
        body, name=name, grid=(D_FF // GU_TILE, t // tm),
        in_specs=[pl.BlockSpec((tm, d), lambda j, i: (i, 0)), pl.BlockSpec((d, 2 * GU_TILE), lambda j, i: (0, j))],
        out_specs=(pl.BlockSpec((tm, 2 * GU_TILE), lambda j, i: (i, j)), pl.BlockSpec((tm, GU_TILE), lambda j, i: (i, j)),
                   pl.BlockSpec((GU_TILE, tm), lambda j, i: (j, i))),
        out_shape=(jax.ShapeDtypeStruct((t, 2 * D_FF), F32), jax.ShapeDtypeStruct((t, D_FF), MXU_DTYPE),
                   jax.ShapeDtypeStruct((D_FF, t), MXU_DTYPE)),
        compiler_params=_cp("parallel", "parallel"))(f, w)


def ffn_dact(dy, w_d, gu, name, tm=512, after=None):
    t, d = dy.shape

    def body(dy_ref, w_ref, gu_ref, *rest):
        o_ref = rest[-1]
        da = _mx_nt(dy_ref[...], w_ref[...])
        g, u = gu_ref[:, :GU_TILE], gu_ref[:, GU_TILE:]
        sg = _sigmoid(g)
        o_ref[:, :GU_TILE] = (da * u * sg * (1.0 + g * (1.0 - sg))).astype(o_ref.dtype)
        o_ref[:, GU_TILE:] = (da * g * sg).astype(o_ref.dtype)

    in_specs = [pl.BlockSpec((tm, d), lambda j, i: (i, 0)), pl.BlockSpec((GU_TILE, d), lambda j, i: (j, 0)),
                pl.BlockSpec((tm, 2 * GU_TILE), lambda j, i: (i, j))]
    args = [dy, w_d, gu]
    if after is not None:
        in_specs.append(pl.BlockSpec(memory_space=pl.ANY))
        args.append(after)
    return pl.pallas_call(
        body, name=name, grid=(D_FF // GU_TILE, t // tm), in_specs=in_specs,
        out_specs=pl.BlockSpec((tm, 2 * GU_TILE), lambda j, i: (i, j)),
        out_shape=jax.ShapeDtypeStruct((t, 2 * D_FF), MXU_DTYPE), compiler_params=_cp("parallel", "parallel"))(*args)


def mm_nn_res_loss(a, b, res, target, name, tm=512):
    t, k = a.shape
    d = b.shape[1]
    tm = min(tm, t)

    def body(a_ref, b_ref, res_ref, t_ref, dy_ref, l_ref):
        e = res_ref[...] + _mx(a_ref[...], b_ref[...]) - t_ref[...]
        dy_ref[...] = e * (1.0 / d)
        part = jnp.zeros((1, 128), F32) + 0.5 * jnp.sum(jnp.mean(e * e, axis=-1, keepdims=True), axis=0, keepdims=True)
        l_ref[...] = jnp.where(pl.program_id(0) == 0, part, l_ref[...] + part)

    row = pl.BlockSpec((tm, d), lambda i: (i, 0))
    return pl.pallas_call(
        body, name=name, grid=(t // tm,),
        in_specs=[pl.BlockSpec((tm, k), lambda i: (i, 0)), pl.BlockSpec((k, d), lambda i: (0, 0)), row, row],
        out_specs=(row, pl.BlockSpec((1, 128), lambda i: (0, 0))),
        out_shape=(jax.ShapeDtypeStruct((t, d), F32), jax.ShapeDtypeStruct((1, 128), F32)),
        compiler_params=_cp("arbitrary"))(a, b, res, target)


QK_W = Q_W + KV_W
_QK_TILE = 256


def _qk_mats():
    idx = np.arange(_QK_TILE)
    half = HEAD_DIM // 2
    same = (idx[:, None] // HEAD_DIM) == (idx[None, :] // HEAD_DIM)
    lo = (idx % HEAD_DIM) < half
    rot = np.where((idx[:, None] == idx[None, :] + half) & lo[None, :], -1.0, 0.0)
    rot = rot + np.where((idx[:, None] == idx[None, :] - half) & ~lo[None, :], 1.0, 0.0)
    return jnp.asarray(same, F32), jnp.asarray(rot, F32)


def _qk_rows(q_gain, k_gain, cosf, sinf):
    gain = jnp.concatenate([q_gain] * ATTN_HEADS + [k_gain] * KV_HEADS, axis=-1)
    return gain, jnp.concatenate([cosf, cosf], axis=-1), jnp.concatenate([sinf, sinf], axis=-1)


def _qk_tiles(a, mat, transposed=False):
    outs = []
    for c0 in range(0, QK_W, _QK_TILE):
        w = min(_QK_TILE, QK_W - c0)
        mt = (mat.T if transposed else mat)[:w, :w].astype(MXU_DTYPE)
        at = a[:, c0:c0 + w]
        hi = at.astype(MXU_DTYPE)
        lo = (at - hi.astype(F32)).astype(MXU_DTYPE)
        outs.append(jnp.dot(hi, mt, preferred_element_type=F32) + jnp.dot(lo, mt, preferred_element_type=F32))
    return jnp.concatenate(outs, axis=-1)


def qk_prep_fwd(proj, q_gain, k_gain, cosf, sinf, name, tm=256):
    t = proj.shape[0]
    gmat, rmat = _qk_mats()
    gain, c2, s2 = _qk_rows(q_gain, k_gain, cosf, sinf)
    rep = QK_W // 128

    def body(p_ref, g_ref, c_ref, s_ref, gm_ref, rm_ref, q_ref, k_ref):
        x = p_ref[...]
        r = lax.rsqrt(_qk_tiles(x * x, gm_ref[...]) * (1.0 / HEAD_DIM) + EPS)
        xn = x * r * g_ref[...]
        c = jnp.concatenate([c_ref[...]] * rep, axis=-1)
        s = jnp.concatenate([s_ref[...]] * rep, axis=-1)
        out = xn * c + _qk_tiles(xn, rm_ref[...]) * s
        q_ref[...] = out[:, :Q_W]
        k_ref[...] = out[:, Q_W:]

    full = pl.BlockSpec((_QK_TILE, _QK_TILE), lambda i: (0, 0))
    tab = pl.BlockSpec((tm, 128), lambda i: (i, 0))
    return pl.pallas_call(
        body, name=name, grid=(t // tm,),
        in_specs=[pl.BlockSpec((tm, QK_W), lambda i: (i, 0)), pl.BlockSpec((1, QK_W), lambda i: (0, 0)), tab, tab,
                  full, full],
        out_specs=(pl.BlockSpec((tm, Q_W), lambda i: (i, 0)), pl.BlockSpec((tm, KV_W), lambda i: (i, 0))),
        out_shape=(jax.ShapeDtypeStruct((t, Q_W), F32), jax.ShapeDtypeStruct((t, KV_W), F32)),
        compiler_params=_cp("parallel"))(proj, gain, c2, s2, gmat, rmat)


def qk_prep_bwd(proj, q_gain, k_gain, cosf, sinf, dq, dk, name, tm=256):
    t = proj.shape[0]
    gmat, rmat = _qk_mats()
    gain, c2, s2 = _qk_rows(q_gain, k_gain, cosf, sinf)
    rep = QK_W // 128
    lanes = np.arange(QK_W)[:, None]
    fold = jnp.asarray(lanes % HEAD_DIM + np.where(lanes >= Q_W, HEAD_DIM, 0) == np.arange(128)[None, :], F32)

    def body(p_ref, g_ref, c_ref, s_ref, gm_ref, rm_ref, f_ref, dq_ref, dk_ref, o_ref, dg_ref):
        x = p_ref[...]
        r = lax.rsqrt(_qk_tiles(x * x, gm_ref[...]) * (1.0 / HEAD_DIM) + EPS)
        xh = x * r
        c = jnp.concatenate([c_ref[...]] * rep, axis=-1)
        s = jnp.concatenate([s_ref[...]] * rep, axis=-1)
        dout = jnp.concatenate([dq_ref[...], dk_ref[...]], axis=-1)
        dxn = dout * c + _qk_tiles(dout * s, rm_ref[...], transposed=True)
        part = _hi(jnp.sum(dxn * xh, axis=0, keepdims=True), f_ref[...])
        dxh = dxn * g_ref[...]
        mean = _qk_tiles(dxh * xh, gm_ref[...]) * (1.0 / HEAD_DIM)
        o_ref[...] = (r * (dxh - xh * mean)).astype(o_ref.dtype)
        dg_ref[...] = jnp.where(pl.program_id(0) == 0, part, dg_ref[...] + part)

    full = pl.BlockSpec((_QK_TILE, _QK_TILE), lambda i: (0, 0))
    tab = pl.BlockSpec((tm, 128), lambda i: (i, 0))
    dqk, dg = pl.pallas_call(
        body, name=name, grid=(t // tm,),
        in_specs=[pl.BlockSpec((tm, QK_W), lambda i: (i, 0)), pl.BlockSpec((1, QK_W), lambda i: (0, 0)), tab, tab,
                  full, full, pl.BlockSpec((QK_W, 128), lambda i: (0, 0)),
                  pl.BlockSpec((tm, Q_W), lambda i: (i, 0)), pl.BlockSpec((tm, KV_W), lambda i: (i, 0))],
        out_specs=(pl.BlockSpec((tm, QK_W), lambda i: (i, 0)), pl.BlockSpec((1, 128), lambda i: (0, 0))),
        out_shape=(jax.ShapeDtypeStruct((t, QK_W), MXU_DTYPE), jax.ShapeDtypeStruct((1, 128), F32)),
        compiler_params=_cp("arbitrary"))(proj, gain, c2, s2, gmat, rmat, fold, dq, dk)
    return dqk, dg[:, :HEAD_DIM], dg[:, HEAD_DIM:]


def _swa_valid(n, grp):
    qi = lax.broadcasted_iota(jnp.int32, (grp * ATTN_BLOCK, 2 * ATTN_BLOCK), 0) & (ATTN_BLOCK - 1)
    kj = lax.broadcasted_iota(jnp.int32, (grp * ATTN_BLOCK, 2 * ATTN_BLOCK), 1)
    diff = qi + ATTN_BLOCK - kj
    return (diff >= 0) & (diff < ATTN_BLOCK) & (n * ATTN_BLOCK - ATTN_BLOCK + kj >= 0)


def _stack_heads(ref, g, grp):
    return jnp.concatenate([ref[:, (g * grp + j) * HEAD_DIM:(g * grp + j + 1) * HEAD_DIM] for j in range(grp)], axis=0)


def _stack_sinks(s_ref, g, grp):
    return jnp.concatenate([jnp.zeros((ATTN_BLOCK, 1), F32) + s_ref[0:1, g * grp + j:g * grp + j + 1]
                            for j in range(grp)], axis=0)


def swa_fwd(q, k, proj, sinks, name):
    t = q.shape[0]
    nb = t // ATTN_BLOCK
    scale = HEAD_DIM ** -0.5
    grp = ATTN_HEADS // KV_HEADS

    def body(q_ref, kc_ref, kp_ref, vc_ref, vp_ref, s_ref, y_ref, mix_ref, yt_ref, lse_ref):
        n = pl.program_id(0)
        valid = _swa_valid(n, grp)
        kk = jnp.concatenate([kp_ref[...], kc_ref[...]], axis=0).astype(MXU_DTYPE)
        vv = jnp.concatenate([vp_ref[...], vc_ref[...]], axis=0).astype(MXU_DTYPE)
        lane = lax.broadcasted_iota(jnp.int32, (ATTN_BLOCK, ATTN_HEADS), 1)
        gs = range(KV_HEADS)
        qg = [_stack_heads(q_ref, g, grp) for g in gs]
        sink = [_stack_sinks(s_ref, g, grp) for g in gs]
        sc = [jnp.where(valid, _mx_nt(qg[g], kk[:, g * HEAD_DIM:(g + 1) * HEAD_DIM]) * scale, NEG) for g in gs]
        m = [jnp.maximum(jnp.max(sc[g], axis=-1, keepdims=True), sink[g]) for g in gs]
        e = [jnp.exp(sc[g] - m[g]) for g in gs]
        den = [jnp.sum(e[g], axis=-1, keepdims=True) + jnp.exp(sink[g] - m[g]) for g in gs]
        og = [_mx(e[g] / den[g], vv[:, g * HEAD_DIM:(g + 1) * HEAD_DIM]) for g in gs]
        lg = [m[g] + jnp.log(den[g]) for g in gs]
        lse = jnp.zeros((ATTN_BLOCK, ATTN_HEADS), F32)
        outs = []
        for h in range(ATTN_HEADS):
            rows = slice((h % grp) * ATTN_BLOCK, (h % grp + 1) * ATTN_BLOCK)
            outs.append(og[h // grp][rows])
            lse = jnp.where(lane == h, lg[h // grp][rows], lse)
        y = jnp.concatenate(outs, axis=-1)
        y_ref[...] = y
        mix_ref[...] = y.astype(mix_ref.dtype)
        yt_ref[...] = y.T.astype(yt_ref.dtype)
        lse_ref[...] = lse

    cur = lambda n: (n, 0)
    prev = lambda n: (jnp.maximum(n - 1, 0), 0)
    vcol = (Q_W + KV_W) // KV_W
    return pl.pallas_call(
        body, name=name, grid=(nb,),
        in_specs=[pl.BlockSpec((ATTN_BLOCK, Q_W), cur), pl.BlockSpec((ATTN_BLOCK, KV_W), cur),
                  pl.BlockSpec((ATTN_BLOCK, KV_W), prev),
                  pl.BlockSpec((ATTN_BLOCK, KV_W), lambda n: (n, vcol)),
                  pl.BlockSpec((ATTN_BLOCK, KV_W), lambda n: (jnp.maximum(n - 1, 0), vcol)),
                  pl.BlockSpec((1, ATTN_HEADS), lambda n: (0, 0))],
        out_specs=(pl.BlockSpec((ATTN_BLOCK, Q_W), cur), pl.BlockSpec((ATTN_BLOCK, Q_W), cur),
                   pl.BlockSpec((Q_W, ATTN_BLOCK), lambda n: (0, n)), pl.BlockSpec((ATTN_BLOCK, ATTN_HEADS), cur)),
        out_shape=(jax.ShapeDtypeStruct((t, Q_W), F32), jax.ShapeDtypeStruct((t, Q_W + CONV_CH), MXU_DTYPE),
                   jax.ShapeDtypeStruct((Q_W + CONV_CH, t), MXU_DTYPE), jax.ShapeDtypeStruct((t, ATTN_HEADS), F32)),
        compiler_params=_cp("parallel"))(q, k, k, proj, proj, sinks)


def swa_bwd(q, k, proj, sinks, y, lse, dmix, name):
    t = q.shape[0]
    nb = t // ATTN_BLOCK
    scale = HEAD_DIM ** -0.5
    grp = ATTN_HEADS // KV_HEADS

    def body(q_ref, kc_ref, kp_ref, vc_ref, vp_ref, s_ref, y_ref, lse_ref, dy_ref,
             dq_ref, dk_ref, dv_ref, ds_ref, dkc, dvc):
        n = pl.program_id(0)

        @pl.when(n == 0)
        def _():
            dkc[...] = jnp.zeros_like(dkc)
            dvc[...] = jnp.zeros_like(dvc)
            ds_ref[...] = jnp.zeros_like(ds_ref)

        @pl.when(n < nb)
        def _():
            valid = _swa_valid(n, grp)
            kk = jnp.concatenate([kp_ref[...], kc_ref[...]], axis=0).astype(MXU_DTYPE)
            vv = jnp.concatenate([vp_ref[...], vc_ref[...]], axis=0).astype(MXU_DTYPE)
            lane = lax.broadcasted_iota(jnp.int32, (1, ATTN_HEADS), 1)
            gs = range(KV_HEADS)
            kg = [kk[:, g * HEAD_DIM:(g + 1) * HEAD_DIM] for g in gs]
            vg = [vv[:, g * HEAD_DIM:(g + 1) * HEAD_DIM] for g in gs]
            qg = [_stack_heads(q_ref, g, grp).astype(MXU_DTYPE) for g in gs]
            dog = [_stack_heads(dy_ref, g, grp) for g in gs]
            og = [_stack_heads(y_ref, g, grp) for g in gs]
            lg = [jnp.concatenate([lse_ref[:, g * grp + j:g * grp + j + 1] for j in range(grp)], axis=0) for g in gs]
            sink = [_stack_sinks(s_ref, g, grp) for g in gs]
            sc = [jnp.where(valid, _mx_nt(qg[g], kg[g]) * scale, NEG) for g in gs]
            p = [jnp.exp(sc[g] - lg[g]) for g in gs]
            delta = [jnp.sum(dog[g] * og[g], axis=-1, keepdims=True) for g in gs]
            ds = [p[g] * (_mx_nt(dog[g], vg[g]) - delta[g]) for g in gs]
            dqg = [_mx(ds[g], kg[g]) * scale for g in gs]
            dkf = jnp.concatenate([_mx_tn(ds[g], qg[g]) * scale for g in gs], axis=-1)
            dvf = jnp.concatenate([_mx_tn(p[g], dog[g]) for g in gs], axis=-1)
            dsk = [jnp.exp(sink[g] - lg[g]) * delta[g] for g in gs]
            dsink = jnp.zeros((1, ATTN_HEADS), F32)
            dqs = []
            for h in range(ATTN_HEADS):
                rows = slice((h % grp) * ATTN_BLOCK, (h % grp + 1) * ATTN_BLOCK)
                dqs.append(dqg[h // grp][rows])
                dsink = jnp.where(lane == h, -jnp.sum(dsk[h // grp][rows], axis=0, keepdims=True), dsink)
            dq_ref[...] = jnp.concatenate(dqs, axis=-1)
            dk_ref[...] = dkc[...] + dkf[:ATTN_BLOCK]
            dv_ref[...] = (dvc[...] + dvf[:ATTN_BLOCK]).astype(dv_ref.dtype)
            dkc[...] = dkf[ATTN_BLOCK:]
            dvc[...] = dvf[ATTN_BLOCK:]
            ds_ref[...] += dsink

        @pl.when(n == nb)
        def _():
            dk_ref[...] = dkc[...]
            dv_ref[...] = dvc[...].astype(dv_ref.dtype)

    cur = lambda n: (jnp.minimum(n, nb - 1), 0)
    prev = lambda n: (jnp.clip(n - 1, 0, nb - 1), 0)
    vcol = (Q_W + KV_W) // KV_W
    return pl.pallas_call(
        body, name=name, grid=(nb + 1,),
        in_specs=[pl.BlockSpec((ATTN_BLOCK, Q_W), cur), pl.BlockSpec((ATTN_BLOCK, KV_W), cur),
                  pl.BlockSpec((ATTN_BLOCK, KV_W), prev),
                  pl.BlockSpec((ATTN_BLOCK, KV_W), lambda n: (jnp.minimum(n, nb - 1), vcol)),
                  pl.BlockSpec((ATTN_BLOCK, KV_W), lambda n: (jnp.clip(n - 1, 0, nb - 1), vcol)),
                  pl.BlockSpec((1, ATTN_HEADS), lambda n: (0, 0)),
                  pl.BlockSpec((ATTN_BLOCK, Q_W), cur), pl.BlockSpec((ATTN_BLOCK, ATTN_HEADS), cur),
                  pl.BlockSpec((ATTN_BLOCK, Q_W), cur)],
        out_specs=(pl.BlockSpec((ATTN_BLOCK, Q_W), cur), pl.BlockSpec((ATTN_BLOCK, KV_W), prev),
                   pl.BlockSpec((ATTN_BLOCK, KV_W), prev), pl.BlockSpec((1, ATTN_HEADS), lambda n: (0, 0))),
        out_shape=(jax.ShapeDtypeStruct((t, Q_W), F32), jax.ShapeDtypeStruct((t, KV_W), F32),
                   jax.ShapeDtypeStruct((t, KV_W), MXU_DTYPE), jax.ShapeDtypeStruct((1, ATTN_HEADS), F32)),
        scratch_shapes=[pltpu.VMEM((ATTN_BLOCK, KV_W), F32), pltpu.VMEM((ATTN_BLOCK, KV_W), F32)],
        compiler_params=_cp("arbitrary"))(q, k, k, proj, proj, sinks, y, lse, dmix)


GC_W = 256
_GB0, _GC0, _XI0 = 768 // GC_W, 1280 // GC_W, 1792 // GC_W
HALO = 8


def gconv_fwd(proj, conv_w, mix, mix_t, name, tm=512):
    t = proj.shape[0]
    hb = tm // HALO
    half = Q_W // GC_W

    def body(gb_ref, gc_ref, xi_ref, gch_ref, xih_ref, w_ref, mix_in, mixt_in, y_ref, yt_ref):
        i = pl.program_id(1)
        u = gc_ref[...] * xi_ref[...]
        uh = jnp.where(i == 0, 0.0, gch_ref[...] * xih_ref[...])
        up = jnp.concatenate([uh, u], axis=0)
        cv = w_ref[0:1, :] * up[HALO - 2:HALO - 2 + tm]
        cv = cv + w_ref[1:2, :] * up[HALO - 1:HALO - 1 + tm]
        cv = cv + w_ref[2:3, :] * u
        y = gb_ref[...] * cv
        y_ref[...] = y.astype(y_ref.dtype)
        yt_ref[...] = y.T.astype(yt_ref.dtype)

    def col(c0):
        return pl.BlockSpec((tm, GC_W), lambda cj, i: (i, c0 + cj))

    def halo(c0):
        return pl.BlockSpec((HALO, GC_W), lambda cj, i: (jnp.maximum(i * hb - 1, 0), c0 + cj))

    return pl.pallas_call(
        body, name=name, grid=(CONV_CH // GC_W, t // tm),
        in_specs=[col(_GB0), col(_GC0), col(_XI0), halo(_GC0), halo(_XI0),
                  pl.BlockSpec((3, GC_W), lambda cj, i: (0, cj)),
                  pl.BlockSpec(memory_space=pl.ANY), pl.BlockSpec(memory_space=pl.ANY)],
        out_specs=(pl.BlockSpec((tm, GC_W), lambda cj, i: (i, half + cj)),
                   pl.BlockSpec((GC_W, tm), lambda cj, i: (half + cj, i))),
        out_shape=(jax.ShapeDtypeStruct(mix.shape, mix.dtype), jax.ShapeDtypeStruct(mix_t.shape, mix_t.dtype)),
        input_output_aliases={6: 0, 7: 1},
        compiler_params=_cp("parallel", "parallel"))(proj, proj, proj, proj, proj, conv_w, mix, mix_t)


def gconv_bwd(proj, conv_w, dmix, name, tm=512):
    t = proj.shape[0]
    hb = tm // HALO
    nt = t // tm
    dy0 = Q_W // GC_W

    def body(gb_ref, gc_ref, xi_ref, gch_ref, xih_ref, gbn_ref, dyn_ref, dy_ref, w_ref,
             dgb_ref, dgc_ref, dxi_ref, dw_ref):
        i = pl.program_id(1)
        gc, xi, gb, dy = gc_ref[...], xi_ref[...], gb_ref[...], dy_ref[...]
        u = gc * xi
        uh = jnp.where(i == 0, 0.0, gch_ref[...] * xih_ref[...])
        up = jnp.concatenate([uh, u], axis=0)
        u2 = up[HALO - 2:HALO - 2 + tm]
        u1 = up[HALO - 1:HALO - 1 + tm]
        cv = w_ref[0:1, :] * u2 + w_ref[1:2, :] * u1 + w_ref[2:3, :] * u
        dgb_ref[...] = (dy * cv).astype(dgb_ref.dtype)
        dcv = dy * gb
        dcvn = jnp.where(i == nt - 1, 0.0, dyn_ref[...] * gbn_ref[...])
        dcvp = jnp.concatenate([dcv, dcvn], axis=0)
        du = w_ref[0:1, :] * dcvp[2:2 + tm] + w_ref[1:2, :] * dcvp[1:1 + tm] + w_ref[2:3, :] * dcv
        dgc_ref[...] = (du * xi).astype(dgc_ref.dtype)
        dxi_ref[...] = (du * gc).astype(dxi_ref.dtype)
        dw = jnp.concatenate([jnp.sum(dcv * u2, axis=0, keepdims=True), jnp.sum(dcv * u1, axis=0, keepdims=True),
                              jnp.sum(dcv * u, axis=0, keepdims=True)], axis=0)

        @pl.when(i == 0)
        def _():
            dw_ref[...] = dw

        @pl.when(i > 0)
        def _():
            dw_ref[...] += dw

    def col(c0):
        return pl.BlockSpec((tm, GC_W), lambda cj, i: (i, c0 + cj))

    def halo(c0):
        return pl.BlockSpec((HALO, GC_W), lambda cj, i: (jnp.maximum(i * hb - 1, 0), c0 + cj))

    def nxt(c0):
        return pl.BlockSpec((HALO, GC_W), lambda cj, i: (jnp.minimum((i + 1) * hb, t // HALO - 1), c0 + cj))

    out = pl.BlockSpec((tm, GC_W), lambda cj, i: (i, cj))
    return pl.pallas_call(
        body, name=name, grid=(CONV_CH // GC_W, nt),
        in_specs=[col(_GB0), col(_GC0), col(_XI0), halo(_GC0), halo(_XI0), nxt(_GB0), nxt(dy0), col(dy0),
                  pl.BlockSpec((3, GC_W), lambda cj, i: (0, cj))],
        out_specs=(out, out, out, pl.BlockSpec((3, GC_W), lambda cj, i: (0, cj))),
        out_shape=(jax.ShapeDtypeStruct((t, CONV_CH), MXU_DTYPE),) * 3 + (jax.ShapeDtypeStruct((3, CONV_CH), F32),),
        compiler_params=_cp("parallel", "arbitrary"))(proj, proj, proj, proj, proj, proj, dmix, dmix, conv_w)


_QKV_W = 3 * DN_W
_BA_COL = (4 * DN_W) // 128
_Z_COL = _QKV_W // DN_W


def gdn_prep_fwd(proj, conv_w, alog_row, dtb_row, name, tm=256):
    t = proj.shape[0]
    hb = tm // HALO
    qscale = DN_DIM ** -0.5

    def body(x_ref, xh_ref, w_ref, ba_ref, al_ref, dt_ref, q_ref, k_ref, v_ref, bg_ref):
        i = pl.program_id(0)
        for gi in range(3 * DN_HEADS):
            sl = slice(gi * DN_DIM, (gi + 1) * DN_DIM)
            xp = jnp.concatenate([jnp.where(i == 0, 0.0, xh_ref[:, sl]), x_ref[:, sl]], axis=0)
            c = w_ref[0:1, sl] * xp[HALO - 3:HALO - 3 + tm]
            for j in range(1, 4):
                c = c + w_ref[j:j + 1, sl] * xp[HALO - 3 + j:HALO - 3 + j + tm]
            s = c * _sigmoid(c)
            osl = slice((gi % DN_HEADS) * DN_DIM, (gi % DN_HEADS + 1) * DN_DIM)
            if gi < DN_HEADS:
                q_ref[:, osl] = s * lax.rsqrt(jnp.sum(s * s, axis=-1, keepdims=True) + EPS) * qscale
            elif gi < 2 * DN_HEADS:
                k_ref[:, osl] = s * lax.rsqrt(jnp.sum(s * s, axis=-1, keepdims=True) + EPS)
            else:
                v_ref[:, osl] = s
        ba = ba_ref[...]
        lane = lax.broadcasted_iota(jnp.int32, ba.shape, 1)
        gval = -jnp.exp(al_ref[...]) * _softplus(ba + dt_ref[...])
        bg_ref[...] = jnp.where(lane < DN_HEADS, _sigmoid(ba), jnp.where(lane < 2 * DN_HEADS, gval, 0.0))

    row = pl.BlockSpec((tm, DN_W), lambda i: (i, 0))
    one = pl.BlockSpec((1, 128), lambda i: (0, 0))
    return pl.pallas_call(
        body, name=name, grid=(t // tm,),
        in_specs=[pl.BlockSpec((tm, _QKV_W), lambda i: (i, 0)),
                  pl.BlockSpec((HALO, _QKV_W), lambda i: (jnp.maximum(i * hb - 1, 0), 0)),
                  pl.BlockSpec((4, _QKV_W), lambda i: (0, 0)),
                  pl.BlockSpec((tm, 128), lambda i: (i, _BA_COL)), one, one],
        out_specs=(row, row, row, pl.BlockSpec((tm, 128), lambda i: (i, 0))),
        out_shape=(jax.ShapeDtypeStruct((t, DN_W), F32),) * 3 + (jax.ShapeDtypeStruct((t, 128), F32),),
        compiler_params=_cp("parallel"))(proj, proj, conv_w, proj, alog_row, dtb_row)


def gdn_prep_bwd(proj, conv_w, alog_row, dtb_row, dq, dk, dv, dbg, name, tm=256):
    t = proj.shape[0]
    hb = tm // HALO
    nt = t // tm
    qscale = DN_DIM ** -0.5
    te = tm + HALO

    def body(x_ref, xh_ref, xn_ref, w_ref, ba_ref, al_ref, dt_ref, dq_ref, dk_ref, dv_ref,
             dqn_ref, dkn_ref, dvn_ref, dbg_ref, dx_ref, dba_ref, dw_ref, ddt_ref, dal_ref):
        i = pl.program_id(0)
        first = i == 0
        last = i == nt - 1
        dws = []
        for gi in range(3 * DN_HEADS):
            sl = slice(gi * DN_DIM, (gi + 1) * DN_DIM)
            osl = slice((gi % DN_HEADS) * DN_DIM, (gi % DN_HEADS + 1) * DN_DIM)
            xe = jnp.concatenate([jnp.where(first, 0.0, xh_ref[:, sl]), x_ref[:, sl], xn_ref[:, sl]], axis=0)
            c = w_ref[0:1, sl] * xe[HALO - 3:HALO - 3 + te]
            for j in range(1, 4):
                c = c + w_ref[j:j + 1, sl] * xe[HALO - 3 + j:HALO - 3 + j + te]
            sg = _sigmoid(c)
            s = c * sg
            d_ref, dn_ref = ((dq_ref, dqn_ref), (dk_ref, dkn_ref), (dv_ref, dvn_ref))[gi // DN_HEADS]
            dy = jnp.concatenate([d_ref[:, osl], jnp.where(last, 0.0, dn_ref[:, osl])], axis=0)
            if gi < 2 * DN_HEADS:
                r = lax.rsqrt(jnp.sum(s * s, axis=-1, keepdims=True) + EPS)
                sh = s * r
                ds = r * (dy - sh * jnp.sum(sh * dy, axis=-1, keepdims=True))
                if gi < DN_HEADS:
                    ds = ds * qscale
            else:
                ds = dy
            dc = ds * sg * (1.0 + c * (1.0 - sg))
            dcs = [dc[3 - j:3 - j + tm] for j in range(4)]
            dx = w_ref[0:1, sl] * dcs[0]
            for j in range(1, 4):
                dx = dx + w_ref[j:j + 1, sl] * dcs[j]
            dx_ref[:, sl] = dx.astype(dx_ref.dtype)
            x0 = x_ref[:, sl]
            dws.append(jnp.concatenate([jnp.sum(dcs[j] * x0, axis=0, keepdims=True) for j in range(4)], axis=0))
        dw = jnp.concatenate(dws, axis=-1)
        ba = ba_ref[...]
        dbgv = dbg_ref[...]
        lane = lax.broadcasted_iota(jnp.int32, ba.shape, 1)
        beta = _sigmoid(ba)
        ea = -jnp.exp(al_ref[...])
        zin = ba + dt_ref[...]
        is_b = lane < DN_HEADS
        is_a = (lane >= DN_HEADS) & (lane < 2 * DN_HEADS)
        da = jnp.where(is_a, dbgv * ea * _sigmoid(zin), 0.0)
        dba_ref[...] = jnp.where(is_b, dbgv * beta * (1.0 - beta), da).astype(dba_ref.dtype)
        ddt = jnp.sum(da, axis=0, keepdims=True)
        dal = jnp.sum(jnp.where(is_a, dbgv * ea * _softplus(zin), 0.0), axis=0, keepdims=True)

        @pl.when(first)
        def _():
            dw_ref[...] = dw
            ddt_ref[...] = ddt
            dal_ref[...] = dal

        @pl.when(i > 0)
        def _():
            dw_ref[...] += dw
            ddt_ref[...] += ddt
            dal_ref[...] += dal

    row = pl.BlockSpec((tm, DN_W), lambda i: (i, 0))
    nrow = pl.BlockSpec((HALO, DN_W), lambda i: (jnp.minimum((i + 1) * hb, t // HALO - 1), 0))
    one = pl.BlockSpec((1, 128), lambda i: (0, 0))
    return pl.pallas_call(
        body, name=name, grid=(nt,),
        in_specs=[pl.BlockSpec((tm, _QKV_W), lambda i: (i, 0)),
                  pl.BlockSpec((HALO, _QKV_W), lambda i: (jnp.maximum(i * hb - 1, 0), 0)),
                  pl.BlockSpec((HALO, _QKV_W), lambda i: (jnp.minimum((i + 1) * hb, t // HALO - 1), 0)),
                  pl.BlockSpec((4, _QKV_W), lambda i: (0, 0)),
                  pl.BlockSpec((tm, 128), lambda i: (i, _BA_COL)), one, one,
                  row, row, row, nrow, nrow, nrow, pl.BlockSpec((tm, 128), lambda i: (i, 0))],
        out_specs=(pl.BlockSpec((tm, _QKV_W), lambda i: (i, 0)), pl.BlockSpec((tm, 128), lambda i: (i, 0)),
                   pl.BlockSpec((4, _QKV_W), lambda i: (0, 0)), one, one),
        out_shape=(jax.ShapeDtypeStruct((t, _QKV_W), MXU_DTYPE), jax.ShapeDtypeStruct((t, 128), MXU_DTYPE),
                   jax.ShapeDtypeStruct((4, _QKV_W), F32), jax.ShapeDtypeStruct((1, 128), F32),
                   jax.ShapeDtypeStruct((1, 128), F32)),
        compiler_params=_cp("arbitrary"))(proj, proj, proj, conv_w, proj, alog_row, dtb_row, dq, dk, dv, dq, dk, dv, dbg)


def _chunk_masks():
    r = lax.broadcasted_iota(jnp.int32, (DN_CHUNK, DN_CHUNK), 0)
    c = lax.broadcasted_iota(jnp.int32, (DN_CHUNK, DN_CHUNK), 1)
    return r >= c, r > c


INV_PACK = 2


def _inv_unit_lower_many(mats):
    n = DN_CHUNK
    wide = INV_PACK * n
    r = lax.broadcasted_iota(jnp.int32, (wide, wide), 0)
    c = lax.broadcasted_iota(jnp.int32, (wide, wide), 1)
    same = (r & -n) == (c & -n)
    eye = jnp.where((r[:n] == (c[:n] & (n - 1))), 1.0, 0.0)

    def blockdiag(row):
        return jnp.where(same, jnp.concatenate([row] * INV_PACK, axis=0), 0.0)

    packs = [jnp.concatenate(mats[g:g + INV_PACK], axis=-1) for g in range(0, len(mats), INV_PACK)]
    xs = [eye - a for a in packs]
    pws = [_hi(a, blockdiag(a)) for a in packs]
    for step in range(5):
        if step < 4:
            both = [_hi(jnp.concatenate([x, pw], axis=0), blockdiag(pw)) for x, pw in zip(xs, pws)]
            xs = [x + b[:n] for x, b in zip(xs, both)]
            pws = [b[n:] for b in both]
        else:
            xs = [x + _hi(x, blockdiag(pw)) for x, pw in zip(xs, pws)]
    return [x[:, j * n:(j + 1) * n] for x in xs for j in range(INV_PACK)]


def _chunk_common(q, k, v, beta, gc, gcr, lower, strict):
    gam = jnp.exp(jnp.where(lower, gc - gcr, NEG))
    eg = jnp.exp(gc)
    gl = gc[DN_CHUNK - 1:DN_CHUNK, :]
    kdf = jnp.exp(gl - gc)
    kb = k * beta
    bmat = _mx_nt(kb, k)
    qmat = _mx_nt(q, k)
    return gam, eg, jnp.exp(gl), kdf, kb, bmat, qmat


DN_STEP = 4


def gdn_fwd(q, k, v, bg, name):
    t = q.shape[0]
    n_chunks = t // DN_CHUNK

    def body(q_ref, k_ref, v_ref, bg_ref, o_ref, sall_ref, tall_ref, s_ref):
        n = pl.program_id(0)

        @pl.when(n == 0)
        def _():
            s_ref[...] = jnp.zeros_like(s_ref)

        lower, strict = _chunk_masks()
        ltri = jnp.where(lower, 1.0, 0.0)
        hs = range(DN_HEADS)
        sl = [slice(h * DN_DIM, (h + 1) * DN_DIM) for h in hs]
        units = [(c, h) for c in range(DN_STEP) for h in hs]
        nu = range(len(units))
        rs = [slice(c * DN_CHUNK, (c + 1) * DN_CHUNK) for c in range(DN_STEP)]
        bgv = [bg_ref[rs[c], :] for c in range(DN_STEP)]
        gcs = [_hi(ltri, b) for b in bgv]
        gcs_t = [g.T for g in gcs]
        qh = [q_ref[rs[c], sl[h]] for c, h in units]
        kh = [k_ref[rs[c], sl[h]] for c, h in units]
        vh = [v_ref[rs[c], sl[h]] for c, h in units]
        beta = [bgv[c][:, h:h + 1] for c, h in units]
        com = [_chunk_common(qh[u], kh[u], vh[u], beta[u], gcs[c][:, DN_HEADS + h:DN_HEADS + h + 1],
                             gcs_t[c][DN_HEADS + h:DN_HEADS + h + 1, :], lower, strict) for u, (c, h) in enumerate(units)]
        gam, eg, dec, kdf, kb, bmat, qmat = zip(*com)
        tms = _inv_unit_lower_many([jnp.where(strict, bmat[u] * gam[u], 0.0) for u in nu])
        for u, (c, h) in enumerate(units):
            tall_ref[c, h] = tms[u]
        uw = [_hi(tms[u], jnp.concatenate([vh[u] * beta[u], kb[u] * eg[u]], axis=-1)) for u in nu]
        qd = [qh[u] * eg[u] for u in nu]
        pm = [qmat[u] * gam[u] for u in nu]
        kd = [kh[u] * kdf[u] for u in nu]
        st = [s_ref[h] for h in hs]
        for c in range(DN_STEP):
            us = [c * DN_HEADS + h for h in hs]
            for h in hs:
                sall_ref[c, h] = st[h]
            v_new = [uw[us[h]][:, :DN_DIM] - _mx(uw[us[h]][:, DN_DIM:], st[h]) for h in hs]
            o_st = [_mx(qd[us[h]], st[h]) for h in hs]
            o_in = [_mx(pm[us[h]], v_new[h]) for h in hs]
            s_up = [_mx_tn(kd[us[h]], v_new[h]) for h in hs]
            for h in hs:
                o_ref[rs[c], sl[h]] = o_st[h] + o_in[h]
            st = [st[h] * dec[us[h]] + s_up[h] for h in hs]
        for h in hs:
            s_ref[h] = st[h]

    rows = DN_STEP * DN_CHUNK
    row = pl.BlockSpec((rows, DN_W), lambda n: (n, 0))
    return pl.pallas_call(
        body, name=name, grid=(n_chunks // DN_STEP,),
        in_specs=[row, row, row, pl.BlockSpec((rows, 128), lambda n: (n, 0))],
        out_specs=(row, pl.BlockSpec((DN_STEP, DN_HEADS, DN_DIM, DN_DIM), lambda n: (n, 0, 0, 0)),
                   pl.BlockSpec((DN_STEP, DN_HEADS, DN_CHUNK, DN_CHUNK), lambda n: (n, 0, 0, 0))),
        out_shape=(jax.ShapeDtypeStruct((t, DN_W), F32),
                   jax.ShapeDtypeStruct((n_chunks, DN_HEADS, DN_DIM, DN_DIM), F32),
                   jax.ShapeDtypeStruct((n_chunks, DN_HEADS, DN_CHUNK, DN_CHUNK), F32)),
        scratch_shapes=[pltpu.VMEM((DN_HEADS, DN_DIM, DN_DIM), F32)],
        compiler_params=_cp("arbitrary"))(q, k, v, bg)


def gdn_bwd(q, k, v, bg, sall, tall, do, name):
    t = q.shape[0]
    n_chunks = t // DN_CHUNK

    def body(q_ref, k_ref, v_ref, bg_ref, sall_ref, tall_ref, do_ref, dq_ref, dk_ref, dv_ref, dbg_ref, ds_ref):
        n = pl.program_id(0)

        @pl.when(n == 0)
        def _():
            ds_ref[...] = jnp.zeros_like(ds_ref)

        lower, strict = _chunk_masks()
        ltri = jnp.where(lower, 1.0, 0.0)
        bgv = bg_ref[...]
        gcs = _hi(ltri, bgv)
        gcs_t = gcs.T
        lane = lax.broadcasted_iota(jnp.int32, (DN_CHUNK, 128), 1)
        rowi = lax.broadcasted_iota(jnp.int32, (DN_CHUNK, 1), 0)
        hs = range(DN_HEADS)
        each = lambda fn, *ls: [fn(*a) for a in zip(*ls)]
        rsum = lambda a: jnp.sum(a, axis=-1, keepdims=True)
        sl = [slice(h * DN_DIM, (h + 1) * DN_DIM) for h in hs]
        st = [sall_ref[0, h] for h in hs]
        tms = [tall_ref[0, h] for h in hs]
        dsn = [ds_ref[h] for h in hs]
        qh = [q_ref[:, sl[h]] for h in hs]
        kh = [k_ref[:, sl[h]] for h in hs]
        vh = [v_ref[:, sl[h]] for h in hs]
        doh = [do_ref[:, sl[h]] for h in hs]
        beta = [bgv[:, h:h + 1] for h in hs]
        com = [_chunk_common(qh[h], kh[h], vh[h], beta[h], gcs[:, DN_HEADS + h:DN_HEADS + h + 1],
                             gcs_t[DN_HEADS + h:DN_HEADS + h + 1, :], lower, strict) for h in hs]
        gam, eg, dec, kdf, kb, bmat, qmat = zip(*com)
        rhs_w = each(lambda a, b: a * b, kb, eg)
        uw = each(lambda t_, v_, b_, r_: _hi(t_, jnp.concatenate([v_ * b_, r_], axis=-1)), tms, vh, beta, rhs_w)
        qd = each(lambda a, b: a * b, qh, eg)
        kd = each(lambda a, b: a * b, kh, kdf)
        pmat = each(lambda a, b: a * b, qmat, gam)
        v_new = each(lambda uw_, s_: uw_[:, :DN_DIM] - _mx(uw_[:, DN_DIM:], s_), uw, st)
        dqd = each(_mx_nt, doh, st)
        ds_o = each(_mx_tn, qd, doh)
        dp = each(lambda d_, v_: jnp.where(lower, _mx_nt(d_, v_), 0.0), doh, v_new)
        dvn_o = each(_mx_tn, pmat, doh)
        ddec = each(lambda d_, s_: jnp.sum(rsum(d_ * s_), axis=0, keepdims=True), dsn, st)
        dkd = each(_mx_nt, v_new, dsn)
        dvn = each(lambda a, k_, d_: a + _mx(k_, d_), dvn_o, kd, dsn)
        dw = each(lambda d_, s_: -_mx_nt(d_, s_), dvn, st)
        ds_w = each(lambda uw_, d_: _mx_tn(uw_[:, DN_DIM:], d_), uw, dvn)
        for h in hs:
            ds_ref[h] = ds_o[h] + dec[h] * dsn[h] - ds_w[h]
        dr = each(lambda t_, a, b: _hi_tn(t_, jnp.concatenate([a, b], axis=-1)), tms, dvn, dw)
        da = each(lambda r_, uw_: jnp.where(strict, -_hi_nt(r_, uw_), 0.0), dr, uw)
        dru = [r_[:, :DN_DIM] for r_ in dr]
        drw = [r_[:, DN_DIM:] for r_ in dr]
        db = each(lambda a, b: a * b, da, gam)
        dq_m = each(lambda a, b: a * b, dp, gam)
        e = each(lambda a, bm, p_, qm, g_: (a * bm + p_ * qm) * g_, da, bmat, dp, qmat, gam)
        dkb = each(lambda b_, k_, r_, e_: _mx(b_, k_) + r_ * e_, db, kh, drw, eg)
        dk = each(lambda b_, kb_, m_, q_, d_, f_: _mx_tn(b_, kb_) + _mx_tn(m_, q_) + d_ * f_, db, kb, dq_m, qh, dkd, kdf)
        dq = each(lambda m_, k_, d_, e_: _mx(m_, k_) + d_ * e_, dq_m, kh, dqd, eg)
        tk = each(lambda a, b: rsum(a * b), dkd, kd)
        dbeta_all = jnp.zeros((DN_CHUNK, 128), F32)
        dgc_all = jnp.zeros((DN_CHUNK, 128), F32)
        for h in hs:
            dgc = (jnp.sum(e[h], axis=1, keepdims=True) - jnp.sum(e[h].T, axis=1, keepdims=True)
                   + rsum(dqd[h] * qd[h]) - tk[h] + rsum(drw[h] * rhs_w[h]))
            dgl = jnp.sum(tk[h], axis=0, keepdims=True) + ddec[h] * dec[h]
            dgc = dgc + jnp.where(rowi == DN_CHUNK - 1, dgl, 0.0)
            dbeta = rsum(dru[h] * vh[h]) + rsum(dkb[h] * kh[h])
            dq_ref[:, sl[h]] = dq[h]
            dk_ref[:, sl[h]] = dk[h] + dkb[h] * beta[h]
            dv_ref[:, sl[h]] = dru[h] * beta[h]
            dbeta_all = jnp.where(lane == h, dbeta, dbeta_all)
            dgc_all = jnp.where(lane == DN_HEADS + h, dgc, dgc_all)
        dbg_ref[...] = dbeta_all + _hi_tn(ltri, dgc_all)

    rev = lambda n: (n_chunks - 1 - n, 0)
    row = pl.BlockSpec((DN_CHUNK, DN_W), rev)
    small = pl.BlockSpec((DN_CHUNK, 128), rev)
    return pl.pallas_call(
        body, name=name, grid=(n_chunks,),
        in_specs=[row, row, row, small,
                  pl.BlockSpec((1, DN_HEADS, DN_DIM, DN_DIM), lambda n: (n_chunks - 1 - n, 0, 0, 0)),
                  pl.BlockSpec((1, DN_HEADS, DN_CHUNK, DN_CHUNK), lambda n: (n_chunks - 1 - n, 0, 0, 0)), row],
        out_specs=(row, row, row, small),
        out_shape=(jax.ShapeDtypeStruct((t, DN_W), F32),) * 3 + (jax.ShapeDtypeStruct((t, 128), F32),),
        scratch_shapes=[pltpu.VMEM((DN_HEADS, DN_DIM, DN_DIM), F32)],
        compiler_params=_cp("arbitrary"))(q, k, v, bg, sall, tall, do)


def gdn_out_fwd(o, proj, o_gain, name, tm=256):
    t = o.shape[0]

    def body(o_ref, z_ref, g_ref, y_ref, yt_ref):
        for h in range(DN_HEADS):
            sl = slice(h * DN_DIM, (h + 1) * DN_DIM)
            ov, zv = o_ref[:, sl], z_ref[:, sl]
            r = lax.rsqrt(jnp.mean(ov * ov, axis=-1, keepdims=True) + EPS)
            y = ov * r * g_ref[...] * (zv * _sigmoid(zv))
            y_ref[:, sl] = y.astype(y_ref.dtype)
            yt_ref[sl, :] = y.T.astype(yt_ref.dtype)

    row = pl.BlockSpec((tm, DN_W), lambda i: (i, 0))
    return pl.pallas_call(
        body, name=name, grid=(t // tm,),
        in_specs=[row, pl.BlockSpec((tm, DN_W), lambda i: (i, _Z_COL)), pl.BlockSpec((1, DN_DIM), lambda i: (0, 0))],
        out_specs=(row, pl.BlockSpec((DN_W, tm), lambda i: (0, i))),
        out_shape=(jax.ShapeDtypeStruct((t, DN_W), MXU_DTYPE), jax.ShapeDtypeStruct((DN_W, t), MXU_DTYPE)),
        compiler_params=_cp("parallel"))(o, proj, o_gain)


def gdn_out_bwd(o, proj, o_gain, dy, name, tm=256):
    t = o.shape[0]

    def body(o_ref, z_ref, g_ref, dy_ref, do_ref, dz_ref, dg_ref):
        i = pl.program_id(0)
        dg = jnp.zeros((1, DN_DIM), F32)
        for h in range(DN_HEADS):
            sl = slice(h * DN_DIM, (h + 1) * DN_DIM)
            ov, zv, dyv = o_ref[:, sl], z_ref[:, sl], dy_ref[:, sl]
            r = lax.rsqrt(jnp.mean(ov * ov, axis=-1, keepdims=True) + EPS)
            oh = ov * r
            sg = _sigmoid(zv)
            dz_ref[:, sl] = (dyv * oh * g_ref[...] * sg * (1.0 + zv * (1.0 - sg))).astype(dz_ref.dtype)
            don = dyv * (zv * sg)
            dg = dg + jnp.sum(don * oh, axis=0, keepdims=True)
            doh = don * g_ref[...]
            do_ref[:, sl] = r * (doh - oh * jnp.mean(doh * oh, axis=-1, keepdims=True))

        @pl.when(i == 0)
        def _():
            dg_ref[...] = dg

        @pl.when(i > 0)
        def _():
            dg_ref[...] += dg

    row = pl.BlockSpec((tm, DN_W), lambda i: (i, 0))
    one = pl.BlockSpec((1, DN_DIM), lambda i: (0, 0))
    return pl.pallas_call(
        body, name=name, grid=(t // tm,),
        in_specs=[row, pl.BlockSpec((tm, DN_W), lambda i: (i, _Z_COL)), one, row],
        out_specs=(row, row, one),
        out_shape=(jax.ShapeDtypeStruct((t, DN_W), F32), jax.ShapeDtypeStruct((t, DN_W), MXU_DTYPE),
                   jax.ShapeDtypeStruct((1, DN_DIM), F32)),
        compiler_params=_cp("arbitrary"))(o, proj, o_gain, dy)


def _peer(k):
    x, y, c = lax.axis_index("x"), lax.axis_index("y"), lax.axis_index("c")
    px = 1 - x if k & 4 else x
    py = 1 - y if k & 2 else y
    pc = 1 - c if k & 1 else c
    return (px, py, pc), 4 * px + 2 * py + pc


_HBM = pl.BlockSpec(memory_space=pltpu.HBM)
_SEM = pl.BlockSpec(memory_space=pltpu.SEMAPHORE)
_DATAFLOW = pltpu.SideEffectType.DATAFLOW_SIDE_EFFECTING
N_PEER = N_DEV - 1


def send_start(srcs, name, scatter, after):
    na = len(srcs)
    ns = (2 * N_PEER + 1) * na
    lands = [lax.empty((N_DEV,) + (s.shape[1:] if scatter else s.shape), s.dtype) for s in srcs]
    extra = [] if after is None else [after]

    def body(*refs):
        src_refs, land_refs = refs[:na], refs[na:2 * na]
        sems = refs[2 * na + len(extra):2 * na + len(extra) + ns]
        land_out, token = refs[-1 - na:-1], refs[-1]
        _, me = _peer(0)
        for a in range(na):
            pltpu.make_async_copy(src_refs[a].at[me] if scatter else src_refs[a], land_out[a].at[me],
                                  sems[2 * N_PEER * na + a]).start()
        for k in range(1, N_DEV):
            peer, pid = _peer(k)
            for a in range(na):
                pltpu.make_async_remote_copy(
                    src_ref=src_refs[a].at[pid] if scatter else src_refs[a], dst_ref=land_refs[a].at[me],
                    send_sem=sems[2 * (a * N_PEER + k - 1)], recv_sem=sems[2 * (a * N_PEER + k - 1) + 1],
                    device_id=peer, device_id_type=MESH).start()
        token[...] = jnp.zeros_like(token)

    hbm = lambda arrs: tuple(pltpu.HBM(a.shape, a.dtype) for a in arrs)
    outs = pl.pallas_call(
        body, name=name,
        out_shape=(pltpu.SemaphoreType.DMA(()),) * ns + hbm(srcs) + hbm(lands) + (jax.ShapeDtypeStruct((8, 128), F32),),
        in_specs=[_HBM] * (2 * na) + [pl.BlockSpec(memory_space=pl.ANY)] * len(extra),
        out_specs=(_SEM,) * ns + (_HBM,) * (2 * na) + (pl.BlockSpec(memory_space=pltpu.VMEM),),
        input_output_aliases={i: ns + i for i in range(2 * na)},
        compiler_params=pltpu.CompilerParams(has_side_effects=_DATAFLOW),
    )(*[pltpu.with_memory_space_constraint(a, pltpu.HBM) for a in list(srcs) + lands], *extra)
    return outs[:ns], outs[ns:ns + na], outs[ns + na:ns + 2 * na], outs[-1]


def send_wait(sems, srcs_thru, lands_thru, name, scatter, after):
    na = len(srcs_thru)
    ns = (2 * N_PEER + 1) * na

    def body(*refs):
        src_refs, land_refs, sm = refs[:na], refs[na:2 * na], refs[2 * na:2 * na + ns]
        _, me = _peer(0)
        for a in range(na):
            pltpu.make_async_copy(src_refs[a].at[me] if scatter else src_refs[a], land_refs[a].at[me],
                                  sm[2 * N_PEER * na + a]).wait()
        for k in range(1, N_DEV):
            peer, pid = _peer(k)
            for a in range(na):
                cp = pltpu.make_async_remote_copy(
                    src_ref=src_refs[a].at[pid] if scatter else src_refs[a], dst_ref=land_refs[a].at[pid],
                    send_sem=sm[2 * (a * N_PEER + k - 1)], recv_sem=sm[2 * (a * N_PEER + k - 1) + 1],
                    device_id=peer, device_id_type=MESH)
                cp.wait_send()
                cp.wait_recv()

    hbm = lambda arrs: tuple(pltpu.HBM(a.shape, a.dtype) for a in arrs)
    outs = pl.pallas_call(
        body, name=name, out_shape=hbm(srcs_thru) + hbm(lands_thru),
        in_specs=[_HBM] * (2 * na) + [_SEM] * ns + [pl.BlockSpec(memory_space=pl.ANY)], out_specs=(_HBM,) * (2 * na),
        input_output_aliases={i: i for i in range(2 * na)},
        compiler_params=pltpu.CompilerParams(has_side_effects=_DATAFLOW),
    )(*srcs_thru, *lands_thru, *sems, after)
    return outs[na:]


def _adamw(w, g, m, v):
    m = ADAM_B1 * m + (1.0 - ADAM_B1) * g
    v = ADAM_B2 * v + (1.0 - ADAM_B2) * (g * g)
    m_hat = m / (1.0 - ADAM_B1 ** ADAM_STEP)
    v_hat = v / (1.0 - ADAM_B2 ** ADAM_STEP)
    return -ADAM_LR * (m_hat / (jnp.sqrt(v_hat) + ADAM_EPS) + ADAM_WD * w), m, v


def adam_sum(w, pieces, m, v, name, layer=0, into=None):
    nl, r, c = w.shape
    tr = r
    for cand in (256, 128, 64, 32, 16, 8):
        if r % cand == 0:
            tr = cand
            break

    def body(w_ref, p_ref, m_ref, v_ref, *rest):
        g_ref, d_ref, nm_ref, nv_ref = rest[-4:]
        g = p_ref[0].astype(F32)
        for s in range(1, N_DEV):
            g = g + p_ref[s].astype(F32)
        g_ref[0] = g
        d_ref[0], nm_ref[0], nv_ref[0] = _adamw(w_ref[0], g, m_ref[0], v_ref[0])

    row = pl.BlockSpec((1, tr, c), lambda i: (layer, i, 0))
    out = jax.ShapeDtypeStruct((nl, r, c), F32)
    extra = [] if into is None else list(into)
    return pl.pallas_call(
        body, name=name, grid=(r // tr,),
        in_specs=[row, pl.BlockSpec((N_DEV, tr, c), lambda i: (0, i, 0)), row, row]
        + [pl.BlockSpec(memory_space=pl.ANY)] * len(extra),
        out_specs=(row,) * 4, out_shape=(out,) * 4,
        input_output_aliases={4 + i: i for i in range(len(extra))},
        compiler_params=_cp("parallel"))(w, pieces, m, v, *extra)


def sum_rows(gathered, name):
    _, r, c = gathered.shape

    def body(p_ref, o_ref):
        g = p_ref[0]
        for s in range(1, N_DEV):
            g = g + p_ref[s]
        o_ref[...] = g

    return pl.pallas_call(body, name=name, out_shape=jax.ShapeDtypeStruct((r, c), F32))(gathered)


def adam_small(w, g, m, v, name):
    def body(w_ref, g_ref, m_ref, v_ref, d_ref, nm_ref, nv_ref):
        d_ref[...], nm_ref[...], nv_ref[...] = _adamw(w_ref[...], g_ref[...], m_ref[...], v_ref[...])

    out = jax.ShapeDtypeStruct(w.shape, F32)
    return pl.pallas_call(body, name=name, out_shape=(out,) * 3)(w, g, m, v)


def _rope_tables(t):
    inv_freq = 10000.0 ** (-jnp.arange(0, HEAD_DIM, 2, dtype=F32) / HEAD_DIM)
    ang = jnp.arange(t, dtype=F32)[:, None] * inv_freq[None, :]
    cos, sin = jnp.cos(ang), jnp.sin(ang)
    return jnp.concatenate([cos, cos], axis=-1), jnp.concatenate([sin, sin], axis=-1)


def _lane_row(vec8):
    return jnp.pad(vec8.reshape(1, DN_HEADS), ((0, 0), (DN_HEADS, 128 - 2 * DN_HEADS)))


def _ffn_bwd(x, norm_g, w_gu, w_d, saved, dy, tag, after=None):
    ft, gu, at = saved
    dgu = ffn_dact(dy, w_d, gu, f"{tag}_d_gate_up", after=after)
    dwd = mm_at(at, dy, f"{tag}_dw_down")
    dwgu = mm_at(ft, dgu, f"{tag}_dw_gate_up")
    dx, dg = mm_nt_rms_bwd(dgu, w_gu, x, norm_g, dy, f"{tag}_d_norm")
    return dx, dwgu, dwd, dg


def local_step(x, target, small, weights_of, grads_out, after=None):
    t = x.shape[0]
    cosf, sinf = _rope_tables(t)
    alog_row, dtb_row = _lane_row(small["odd_a_log"]), _lane_row(small["odd_dt_bias"])

    h0, h0t = rms_fwd(x, small["even_norm"], "even_norm", after=after)
    we = weights_of("even", h0)
    small = {**small, **we.get("small", {})}
    proj0 = mm_nn(h0, we["w_in"], "even_in_proj")
    qr, kr = qk_prep_fwd(proj0, small["even_q_gain"], small["even_k_gain"], cosf, sinf, "even_qk_prep")
    y_attn, mix0, mix0_t, lse = swa_fwd(qr, kr, proj0, small["even_sinks"], "even_swa")
    mix0, mix0_t = gconv_fwd(proj0, small["even_conv_w"], mix0, mix0_t, "even_gconv")
    we = {**we, **weights_of("even_out", mix0)}
    x1, f0, f0t = mm_nn_res_norm(mix0, we["w_out"], x, small["ffn_norm0"], "even_out_proj")
    w0 = weights_of("ffn0", x1)
    gu0, a0, a0t = ffn_up(f0, w0["gate_up"], "ffn0_gate_up")
    ffn0 = (f0t, gu0, a0t)
    x2, h1, h1t = mm_nn_res_norm(a0, w0["down"], x1, small["odd_norm"], "ffn0_down")

    wo = weights_of("odd", x2)
    proj1 = mm_nn(h1, wo["w_in"], "odd_in_proj")
    qn, kn, vs, bg = gdn_prep_fwd(proj1, small["odd_conv_w"], alog_row, dtb_row, "odd_prep")
    o, sall, tall = gdn_fwd(qn, kn, vs, bg, "odd_delta_rule")
    og, ogt = gdn_out_fwd(o, proj1, small["odd_o_gain"], "odd_gate_norm")
    x3, f1, f1t = mm_nn_res_norm(og, wo["w_out"], x2, small["ffn_norm1"], "odd_out_proj")
    w1 = weights_of("ffn1", x3)
    gu1, a1, a1t = ffn_up(f1, w1["gate_up"], "ffn1_gate_up")
    ffn1 = (f1t, gu1, a1t)
    dy, loss_row = mm_nn_res_loss(a1, w1["down"], x3, target, "ffn1_down_loss")

    gs = {}
    dx3, dwgu, dwd, gs["ffn_norm1"] = _ffn_bwd(x3, small["ffn_norm1"], w1["gate_up"], w1["down"], ffn1, dy, "ffn1")
    tok = grads_out("ffn1", {"gate_up": dwgu, "down": dwd})

    dog = mm_nt(dx3, wo["w_out"], "odd_d_gated", after=tok)
    dwo = mm_at(ogt, dx3, "odd_dw_out")
    do, dz, gs["odd_o_gain"] = gdn_out_bwd(o, proj1, small["odd_o_gain"], dog, "odd_d_gate_norm")
    dqn, dkn, dvs, dbg = gdn_bwd(qn, kn, vs, bg, sall, tall, do, "odd_d_delta_rule")
    dqkv, dba, gs["odd_conv_w"], ddt_row, dal_row = gdn_prep_bwd(
        proj1, small["odd_conv_w"], alog_row, dtb_row, dqn, dkn, dvs, dbg, "odd_d_prep")
    gs["odd_dt_bias"] = ddt_row[:, DN_HEADS:2 * DN_HEADS]
    gs["odd_a_log"] = dal_row[:, DN_HEADS:2 * DN_HEADS]
    dproj1 = jnp.concatenate([dqkv, dz, dba], axis=-1)
    dwi = mm_at(h1t, dproj1, "odd_dw_in")
    dx2, gs["odd_norm"] = mm_nt_rms_bwd(dproj1, wo["w_in"], x2, small["odd_norm"], dx3, "odd_d_norm")
    tok = grads_out("odd", {"w_in": dwi, "w_out": dwo})

    dx1, dwgu, dwd, gs["ffn_norm0"] = _ffn_bwd(x1, small["ffn_norm0"], w0["gate_up"], w0["down"], ffn0, dx2, "ffn0",
                                               after=tok)
    tok = grads_out("ffn0", {"gate_up": dwgu, "down": dwd})

    dmix = mm_nt(dx1, we["w_out"], "even_d_mix", after=tok)
    dwo = mm_at(mix0_t, dx1, "even_dw_out")
    dqr, dkr, dv, gs["even_sinks"] = swa_bwd(qr, kr, proj0, small["even_sinks"], y_attn, lse, dmix, "even_d_swa")
    dqk, gs["even_q_gain"], gs["even_k_gain"] = qk_prep_bwd(
        proj0, small["even_q_gain"], small["even_k_gain"], cosf, sinf, dqr, dkr, "even_d_qk_prep")
    dgb, dgc, dxi, gs["even_conv_w"] = gconv_bwd(proj0, small["even_conv_w"], dmix, "even_d_gconv")
    dproj0 = jnp.concatenate([dqk, dv, dgb, dgc, dxi], axis=-1)
    dwi = mm_at(h0t, dproj0, "even_dw_in")
    tok = grads_out("even", {"w_in": dwi, "w_out": dwo})
    grad_x, gs["even_norm"] = mm_nt_rms_bwd(dproj0, we["w_in"], x, small["even_norm"], dx1, "even_d_norm", after=tok)
    return loss_row, grad_x, gs


_SMALL_ORDER = ("even_norm", "even_q_gain", "even_k_gain", "even_sinks", "odd_a_log", "odd_dt_bias", "odd_o_gain",
                "ffn_norm0", "ffn_norm1", "odd_norm", "even_conv_w", "odd_conv_w")
_SMALL_SIZE = {"even_norm": 1024, "even_q_gain": 64, "even_k_gain": 64, "even_sinks": 8, "odd_a_log": 8,
               "odd_dt_bias": 8, "odd_o_gain": 128, "ffn_norm0": 1024, "ffn_norm1": 1024, "odd_norm": 1024,
               "even_conv_w": 3 * 512, "odd_conv_w": 4 * 3072}
_N_REPL = 9


def _pack_rows(vals):
    flat = jnp.concatenate([v.reshape(-1) for v in vals])
    pad = (-flat.shape[0]) % 1024
    return jnp.pad(flat, (0, pad)).reshape(-1, 128)


def _my_block(full, size, axis):
    me = 4 * lax.axis_index("x") + 2 * lax.axis_index("y") + lax.axis_index("c")
    return lax.dynamic_slice_in_dim(full, me * size, size, axis=axis)


def _col_gathered(g):
    return g.transpose(1, 0, 2).reshape(g.shape[1], N_DEV * g.shape[2])


def _col_pieces(dw):
    k, n8 = dw.shape
    return dw.reshape(k, N_DEV, n8 // N_DEV).transpose(1, 0, 2)


def kernel(x, even_norm, even_w_in, even_q_gain, even_k_gain, even_sinks, even_conv_w, even_w_out, odd_norm, odd_w_in, odd_conv_w, odd_a_log, odd_dt_bias, odd_o_gain, odd_w_out, ffn_norm, ffn_w_gate_up, ffn_w_down, loss_target, m_even_norm, m_even_w_in, m_even_q_gain, m_even_k_gain, m_even_sinks, m_even_conv_w, m_even_w_out, m_odd_norm, m_odd_w_in, m_odd_conv_w, m_odd_a_log, m_odd_dt_bias, m_odd_o_gain, m_odd_w_out, m_ffn_norm, m_ffn_w_gate_up, m_ffn_w_down, v_even_norm, v_even_w_in, v_even_q_gain, v_even_k_gain, v_even_sinks, v_even_conv_w, v_even_w_out, v_odd_norm, v_odd_w_in, v_odd_conv_w, v_odd_a_log, v_odd_dt_bias, v_odd_o_gain, v_odd_w_out, v_ffn_norm, v_ffn_w_gate_up, v_ffn_w_down):
    t = x.shape[1]
    d = D_MODEL

    me = 4 * lax.axis_index("x") + 2 * lax.axis_index("y") + lax.axis_index("c")
    fpd = D_FF // N_DEV
    shard = {
        "even": {"w_in": even_w_in.reshape(d, EVEN_IN_W // N_DEV), "w_out": even_w_out.reshape(d // N_DEV, d)},
        "ffn0": {"gate_up": ffn_w_gate_up[0], "down": ffn_w_down[0]},
        "odd": {"w_in": odd_w_in.reshape(d, ODD_IN_W // N_DEV), "w_out": odd_w_out.reshape(d // N_DEV, d)},
        "ffn1": {"gate_up": ffn_w_gate_up[1], "down": ffn_w_down[1]},
    }
    given = {
        ("even", "w_in"): ("even_w_in", even_w_in, m_even_w_in, v_even_w_in, 0),
        ("even", "w_out"): ("even_w_out", even_w_out, m_even_w_out, v_even_w_out, 0),
        ("odd", "w_in"): ("odd_w_in", odd_w_in, m_odd_w_in, v_odd_w_in, 0),
        ("odd", "w_out"): ("odd_w_out", odd_w_out, m_odd_w_out, v_odd_w_out, 0),
        ("ffn0", "gate_up"): ("ffn_w_gate_up", ffn_w_gate_up, m_ffn_w_gate_up, v_ffn_w_gate_up, 0),
        ("ffn1", "gate_up"): ("ffn_w_gate_up", ffn_w_gate_up, m_ffn_w_gate_up, v_ffn_w_gate_up, 1),
        ("ffn0", "down"): ("ffn_w_down", ffn_w_down, m_ffn_w_down, v_ffn_w_down, 0),
        ("ffn1", "down"): ("ffn_w_down", ffn_w_down, m_ffn_w_down, v_ffn_w_down, 1),
    }

    def whole(group, parts):
        col, row = tuple(shard[group])
        w_col = _gu_gathered(parts[0]) if col == "gate_up" else _col_gathered(parts[0])
        if group == "odd":
            w_col = jnp.pad(w_col, ((0, 0), (0, ODD_IN_PAD - ODD_IN_W)))
        return {col: w_col, row: parts[1].reshape(-1, d)}

    wire = {g: [a.astype(MXU_DTYPE) for a in shard[g].values()] for g in shard}
    wire["even_out"] = [wire["even"].pop()]
    wire["even"].append(_pack_rows([odd_norm, even_conv_w, odd_conv_w]))
    gathers, tok = {}, None
    for g in ("even", "even_out", "ffn0", "odd", "ffn1"):
        sems, srcs_thru, lands_thru, tok = send_start(wire[g], f"gather_{g}_start", False, tok)
        gathers[g] = (sems, srcs_thru, lands_thru)
    o1 = d // N_DEV
    o2 = o1 + 3 * CONV_CH // N_DEV

    def weights_of(group, after):
        lands = send_wait(*gathers[group], f"gather_{group}_wait", False, after)
        if group == "even_out":
            return {"w_out": lands[0].reshape(-1, d)}
        if group != "even":
            return whole(group, lands)
        sg = lands[1].reshape(N_DEV, -1)
        return {"w_in": _col_gathered(lands[0]), "small": {
            "odd_norm": sg[:, :o1].reshape(1, d),
            "even_conv_w": sg[:, o1:o2].reshape(N_DEV, 3, CONV_CH // N_DEV).transpose(1, 0, 2).reshape(3, CONV_CH),
            "odd_conv_w": sg[:, o2:o2 + 4 * _QKV_W // N_DEV].reshape(N_DEV, 4, _QKV_W // N_DEV)
            .transpose(1, 0, 2).reshape(4, _QKV_W)}}

    sent = {}

    def grads_out(group, dws):
        col, row = tuple(shard[group])
        n_cols = N_DEV * shard[group][col].shape[1]
        pieces = [_gu_pieces(dws[col]) if col == "gate_up" else _col_pieces(dws[col][:, :n_cols]),
                  dws[row].reshape((N_DEV,) + shard[group][row].shape)]
        sems, srcs_thru, lands_thru, token = send_start(pieces, f"exchange_{group}_start", True, None)
        sent[group] = (sems, srcs_thru, lands_thru, pieces)
        return token

    small = {
        "even_norm": even_norm, "even_q_gain": even_q_gain, "even_k_gain": even_k_gain, "even_sinks": even_sinks,
        "odd_a_log": odd_a_log.reshape(-1), "odd_dt_bias": odd_dt_bias.reshape(-1), "odd_o_gain": odd_o_gain,
        "ffn_norm0": ffn_norm[0:1], "ffn_norm1": ffn_norm[1:2],
    }

    loss_row, grad_x, gs = local_step(x.reshape(t, d), loss_target.reshape(t, d), small, weights_of, grads_out, after=tok)

    rows = _pack_rows([gs[n] for n in _SMALL_ORDER] + [loss_row[:, 0:1]])
    small_sent = send_start([rows], "gather_small_grads_start", False, None)

    res, behind = {}, small_sent[3]
    for g in ("ffn1", "odd", "ffn0", "even"):
        sems, srcs_thru, lands_thru, pieces = sent[g]
        lands = send_wait(sems, srcs_thru, lands_thru, f"exchange_{g}_wait", True, behind)
        for key, pcs in zip(shard[g], lands):
            name, w_, m_, v_, layer = given[g, key]
            res[name] = adam_sum(w_, pcs, m_, v_, f"adamw_{g}_{key}", layer=layer, into=res.get(name))
        behind = res[name][0]

    (rows_g,) = send_wait(*small_sent[:3], "gather_small_grads_wait", False, behind)
    tot = sum_rows(rows_g, "sum_small_grads").reshape(-1)
    off, sgrad = 0, {}
    for n in _SMALL_ORDER:
        sgrad[n] = tot[off:off + _SMALL_SIZE[n]]
        off += _SMALL_SIZE[n]
    loss = tot[off]

    repl = _SMALL_ORDER[:_N_REPL]
    repl_w = {"even_norm": even_norm, "even_q_gain": even_q_gain, "even_k_gain": even_k_gain, "even_sinks": even_sinks,
              "odd_a_log": odd_a_log, "odd_dt_bias": odd_dt_bias, "odd_o_gain": odd_o_gain,
              "ffn_norm0": ffn_norm[0], "ffn_norm1": ffn_norm[1]}
    repl_m = {"even_norm": m_even_norm, "even_q_gain": m_even_q_gain, "even_k_gain": m_even_k_gain,
              "even_sinks": m_even_sinks, "odd_a_log": m_odd_a_log, "odd_dt_bias": m_odd_dt_bias,
              "odd_o_gain": m_odd_o_gain, "ffn_norm0": m_ffn_norm[0], "ffn_norm1": m_ffn_norm[1]}
    repl_v = {"even_norm": v_even_norm, "even_q_gain": v_even_q_gain, "even_k_gain": v_even_k_gain,
              "even_sinks": v_even_sinks, "odd_a_log": v_odd_a_log, "odd_dt_bias": v_odd_dt_bias,
              "odd_o_gain": v_odd_o_gain, "ffn_norm0": v_ffn_norm[0], "ffn_norm1": v_ffn_norm[1]}
    pk = lambda dct: _pack_rows([dct[n] for n in repl])
    pd_, pm_, pv_ = adam_small(pk(repl_w), pk(sgrad), pk(repl_m), pk(repl_v), "adamw_replicated")
    sres = {}
    off = 0
    for n in repl:
        sz = _SMALL_SIZE[n]
        sres[n] = (sgrad[n], pd_.reshape(-1)[off:off + sz], pm_.reshape(-1)[off:off + sz], pv_.reshape(-1)[off:off + sz])
        off += sz
    g_on = _my_block(sgrad["odd_norm"].reshape(1, d), d // N_DEV, 1)
    g_ec = _my_block(sgrad["even_conv_w"].reshape(3, CONV_CH), CONV_CH // N_DEV, 1)
    g_oc = _my_block(sgrad["odd_conv_w"].reshape(4, _QKV_W), _QKV_W // N_DEV, 1)
    shard_w = _pack_rows([odd_norm, even_conv_w, odd_conv_w])
    sd_, sm_, sv_ = adam_small(shard_w, _pack_rows([g_on, g_ec, g_oc]),
                               _pack_rows([m_odd_norm, m_even_conv_w, m_odd_conv_w]),
                               _pack_rows([v_odd_norm, v_even_conv_w, v_odd_conv_w]), "adamw_sharded_small")
    off = 0
    for n, gfull, like in (("odd_norm", g_on, odd_norm), ("even_conv_w", g_ec, even_conv_w), ("odd_conv_w", g_oc, odd_conv_w)):
        sz = like.size
        sres[n] = (gfull, sd_.reshape(-1)[off:off + sz], sm_.reshape(-1)[off:off + sz], sv_.reshape(-1)[off:off + sz])
        off += sz

    def small_out(name, like, kind):
        if name == "ffn_norm":
            return jnp.stack([sres["ffn_norm0"][kind], sres["ffn_norm1"][kind]]).reshape(like.shape)
        return sres[name][kind].reshape(like.shape)

    order = (("even_norm", even_norm), ("even_w_in", even_w_in), ("even_q_gain", even_q_gain),
             ("even_k_gain", even_k_gain), ("even_sinks", even_sinks), ("even_conv_w", even_conv_w),
             ("even_w_out", even_w_out), ("odd_norm", odd_norm), ("odd_w_in", odd_w_in), ("odd_conv_w", odd_conv_w),
             ("odd_a_log", odd_a_log), ("odd_dt_bias", odd_dt_bias), ("odd_o_gain", odd_o_gain),
             ("odd_w_out", odd_w_out), ("ffn_norm", ffn_norm), ("ffn_w_gate_up", ffn_w_gate_up),
             ("ffn_w_down", ffn_w_down))
    outs = [loss, grad_x.reshape(x.shape)]
    for kind in range(4):
        for name, like in order:
            outs.append(res[name][kind] if name in res else small_out(name, like, kind))
    return tuple(outs)
```

```python
import jax
import jax.numpy as jnp
import numpy as np
from jax import lax
from jax.experimental import pallas as pl
from jax.experimental.pallas import tpu as pltpu

F32 = jnp.float32
MXU_DTYPE = jnp.bfloat16
HI = lax.Precision.HIGH
EPS = 1e-6
N_DEV = 8
D_MODEL = 1024
HEAD_DIM = 64
ATTN_HEADS = 8
KV_HEADS = 2
ATTN_BLOCK = 128
Q_W = 512
KV_W = 128
CONV_CH = 512
EVEN_IN_W = 2304
DN_HEADS = 8
DN_DIM = 128
DN_W = 1024
DN_CHUNK = 64
ODD_IN_W = 4112
ODD_IN_PAD = 4224
D_FF = 2816
NEG = -1e30
VMEM_LIMIT = 56 * 1024 * 1024
ADAM_LR, ADAM_B1, ADAM_B2, ADAM_EPS, ADAM_WD, ADAM_STEP = 0.001, 0.9, 0.999, 1e-08, 0.01, 10
MESH = pl.DeviceIdType.MESH


def _cp(*sem):
    return pltpu.CompilerParams(dimension_semantics=sem, vmem_limit_bytes=VMEM_LIMIT)


def _pick(n, cap):
    best = 128
    for t in range(128, cap + 1, 128):
        if n % t == 0:
            best = t
    return best


def _mx(a, b):
    return jnp.dot(a.astype(MXU_DTYPE), b.astype(MXU_DTYPE), preferred_element_type=F32)


def _mx_nt(a, b):
    return lax.dot_general(a.astype(MXU_DTYPE), b.astype(MXU_DTYPE), (((1,), (1,)), ((), ())),
                           preferred_element_type=F32)


def _mx_tn(a, b):
    return lax.dot_general(a.astype(MXU_DTYPE), b.astype(MXU_DTYPE), (((0,), (0,)), ((), ())),
                           preferred_element_type=F32)


def _hi(a, b):
    return jnp.dot(a, b, precision=HI, preferred_element_type=F32)


def _hi_nt(a, b):
    return lax.dot_general(a, b, (((1,), (1,)), ((), ())), precision=HI, preferred_element_type=F32)


def _hi_tn(a, b):
    return lax.dot_general(a, b, (((0,), (0,)), ((), ())), precision=HI, preferred_element_type=F32)


def _sigmoid(x):
    return 0.5 * jnp.tanh(0.5 * x) + 0.5


def _softplus(x):
    return jnp.maximum(x, 0.0) + jnp.log(1.0 + jnp.exp(-jnp.abs(x)))


def mm_nn(a, b, name, res=None, out_dtype=F32, tm=1024):
    m, k = a.shape
    _, n = b.shape
    tn = _pick(n, 1536)
    tm = min(tm, m)

    def body(*refs):
        a_ref, b_ref = refs[0], refs[1]
        o_ref = refs[-1]
        acc = _mx(a_ref[...], b_ref[...])
        if res is not None:
            acc = acc + refs[2][...]
        o_ref[...] = acc.astype(o_ref.dtype)

    in_specs = [pl.BlockSpec((tm, k), lambda j, i: (i, 0)), pl.BlockSpec((k, tn), lambda j, i: (0, j))]
    args = [a, b]
    if res is not None:
        in_specs.append(pl.BlockSpec((tm, tn), lambda j, i: (i, j)))
        args.append(res)
    return pl.pallas_call(
        body, name=name, grid=(n // tn, m // tm), in_specs=in_specs,
        out_specs=pl.BlockSpec((tm, tn), lambda j, i: (i, j)),
        out_shape=jax.ShapeDtypeStruct((m, n), out_dtype), compiler_params=_cp("parallel", "parallel"))(*args)


def mm_nn_res_norm(a, b, res, g, name, tm=512):
    t, k = a.shape
    d = b.shape[1]
    tm = min(tm, t)

    def body(a_ref, b_ref, res_ref, g_ref, y_ref, h_ref, ht_ref):
        y = res_ref[...] + _mx(a_ref[...], b_ref[...])
        y_ref[...] = y
        h = y * lax.rsqrt(jnp.mean(y * y, axis=-1, keepdims=True) + EPS) * g_ref[...]
        h_ref[...] = h.astype(h_ref.dtype)
        ht_ref[...] = h.T.astype(ht_ref.dtype)

    row = pl.BlockSpec((tm, d), lambda i: (i, 0))
    return pl.pallas_call(
        body, name=name, grid=(t // tm,),
        in_specs=[pl.BlockSpec((tm, k), lambda i: (i, 0)), pl.BlockSpec((k, d), lambda i: (0, 0)), row,
                  pl.BlockSpec((1, d), lambda i: (0, 0))],
        out_specs=(row, row, pl.BlockSpec((d, tm), lambda i: (0, i))),
        out_shape=(jax.ShapeDtypeStruct((t, d), F32), jax.ShapeDtypeStruct((t, d), MXU_DTYPE),
                   jax.ShapeDtypeStruct((d, t), MXU_DTYPE)),
        compiler_params=_cp("parallel"))(a, b, res, g)


def mm_nt(a, b, name, out_dtype=F32, tm=1024, after=None):
    m, k = a.shape
    n, _ = b.shape
    tn = _pick(n, 512 if k > 3000 else 1536)
    tm = min(tm, m)

    def body(a_ref, b_ref, *rest):
        o_ref = rest[-1]
        o_ref[...] = _mx_nt(a_ref[...], b_ref[...]).astype(o_ref.dtype)

    in_specs = [pl.BlockSpec((tm, k), lambda j, i: (i, 0)), pl.BlockSpec((tn, k), lambda j, i: (j, 0))]
    args = [a, b]
    if after is not None:
        in_specs.append(pl.BlockSpec(memory_space=pl.ANY))
        args.append(after)
    return pl.pallas_call(
        body, name=name, grid=(n // tn, m // tm), in_specs=in_specs,
        out_specs=pl.BlockSpec((tm, tn), lambda j, i: (i, j)),
        out_shape=jax.ShapeDtypeStruct((m, n), out_dtype), compiler_params=_cp("parallel", "parallel"))(*args)


def mm_at(at, b, name, tk=1024):
    m, kk = at.shape
    _, n = b.shape
    tm, tn, tk = _pick(m, 1408), _pick(n, 2816), min(tk, kk)
    nk = kk // tk

    def body(a_ref, b_ref, o_ref, acc_ref):
        k = pl.program_id(2)
        p = _mx(a_ref[...], b_ref[...])
        acc = jnp.where(k == 0, p, acc_ref[...] + p)
        acc_ref[...] = acc

        @pl.when(k == nk - 1)
        def _():
            o_ref[...] = acc.astype(o_ref.dtype)

    return pl.pallas_call(
        body, name=name, grid=(m // tm, n // tn, nk),
        in_specs=[pl.BlockSpec((tm, tk), lambda i, j, k: (i, k)), pl.BlockSpec((tk, tn), lambda i, j, k: (k, j))],
        out_specs=pl.BlockSpec((tm, tn), lambda i, j, k: (i, j)),
        out_shape=jax.ShapeDtypeStruct((m, n), MXU_DTYPE), scratch_shapes=[pltpu.VMEM((tm, tn), F32)],
        compiler_params=_cp("parallel", "parallel", "arbitrary"))(at, b)


def rms_fwd(x, g, name, tm=512, after=None):
    t, d = x.shape

    def body(x_ref, g_ref, *rest):
        o_ref, ot_ref = rest[-2:]
        xv = x_ref[...]
        r = lax.rsqrt(jnp.mean(xv * xv, axis=-1, keepdims=True) + EPS)
        h = xv * r * g_ref[...]
        o_ref[...] = h.astype(o_ref.dtype)
        ot_ref[...] = h.T.astype(ot_ref.dtype)

    in_specs = [pl.BlockSpec((tm, d), lambda i: (i, 0)), pl.BlockSpec((1, d), lambda i: (0, 0))]
    args = [x, g]
    if after is not None:
        in_specs.append(pl.BlockSpec(memory_space=pl.ANY))
        args.append(after)
    return pl.pallas_call(
        body, name=name, grid=(t // tm,), in_specs=in_specs,
        out_specs=(pl.BlockSpec((tm, d), lambda i: (i, 0)), pl.BlockSpec((d, tm), lambda i: (0, i))),
        out_shape=(jax.ShapeDtypeStruct((t, d), MXU_DTYPE), jax.ShapeDtypeStruct((d, t), MXU_DTYPE)),
        compiler_params=_cp("parallel"))(*args)


def mm_nt_rms_bwd(a, b, x, g, dres, name, tm=512, after=None):
    t, k = a.shape
    d = b.shape[0]
    tm = min(tm, t)

    def body(a_ref, b_ref, x_ref, g_ref, dres_ref, *rest):
        dx_ref, dg_ref = rest[-2:]
        dhv = _mx_nt(a_ref[...], b_ref[...])
        xv = x_ref[...]
        r = lax.rsqrt(jnp.mean(xv * xv, axis=-1, keepdims=True) + EPS)
        xh = xv * r
        dxh = dhv * g_ref[...]
        dx_ref[...] = dres_ref[...] + r * (dxh - xh * jnp.mean(dxh * xh, axis=-1, keepdims=True))
        part = jnp.sum(dhv * xh, axis=0, keepdims=True)
        dg_ref[...] = jnp.where(pl.program_id(0) == 0, part, dg_ref[...] + part)

    row = pl.BlockSpec((tm, d), lambda i: (i, 0))
    one = pl.BlockSpec((1, d), lambda i: (0, 0))
    in_specs = [pl.BlockSpec((tm, k), lambda i: (i, 0)), pl.BlockSpec((d, k), lambda i: (0, 0)), row, one, row]
    args = [a, b, x, g, dres]
    if after is not None:
        in_specs.append(pl.BlockSpec(memory_space=pl.ANY))
        args.append(after)
    return pl.pallas_call(
        body, name=name, grid=(t // tm,), in_specs=in_specs, out_specs=(row, one),
        out_shape=(jax.ShapeDtypeStruct((t, d), F32), jax.ShapeDtypeStruct((1, d), F32)),
        compiler_params=_cp("arbitrary"))(*args)


GU_TILE = 1408


_GU_PER_TILE = GU_TILE * N_DEV // (2 * D_FF)


def _gu_gathered(g):
    _, c, k = g.shape
    nj = N_DEV // (2 * _GU_PER_TILE)
    return g.reshape(2, nj, _GU_PER_TILE, c, k).transpose(4, 1, 0, 2, 3).reshape(k, N_DEV * c)


def _gu_pieces(dw):
    k, n8 = dw.shape
    nj = N_DEV // (2 * _GU_PER_TILE)
    return dw.reshape(k, nj, 2, _GU_PER_TILE, n8 // N_DEV).transpose(2, 1, 3, 4, 0).reshape(N_DEV, n8 // N_DEV, k)


def ffn_up(f, w, name, tm=512):
    t, d = f.shape

    def body(f_ref, w_ref, gu_ref, a_ref, at_ref):
        gu = _mx(f_ref[...], w_ref[...])
        gu_ref[...] = gu
        g, u = gu[:, :GU_TILE], gu[:, GU_TILE:]
        act = g * _sigmoid(g) * u
        a_ref[...] = act.astype(a_ref.dtype)
        at_ref[...] = act.T.astype(at_ref.dtype)

    return pl.pallas_call(
        body, name=name, grid=(D_FF // GU_TILE, t // tm),
        in_specs=[pl.BlockSpec((tm, d), lambda j, i: (i, 0)), pl.BlockSpec((d, 2 * GU_TILE), lambda j, i: (0, j))],
        out_specs=(pl.BlockSpec((tm, 2 * GU_TILE), lambda j, i: (i, j)), pl.BlockSpec((tm, GU_TILE), lambda j, i: (i, j)),
                   pl.BlockSpec((GU_TILE, tm), lambda j, i: (j, i))),
        out_shape=(jax.ShapeDtypeStruct((t, 2 * D_FF), F32), jax.ShapeDtypeStruct((t, D_FF), MXU_DTYPE),
                   jax.ShapeDtypeStruct((D_FF, t), MXU_DTYPE)),
        compiler_params=_cp("parallel", "parallel"))(f, w)


def ffn_dact(dy, w_d, gu, name, tm=512, after=None):
    t, d = dy.shape

    def body(dy_ref, w_ref, gu_ref, *rest):
        o_ref = rest[-1]
        da = _mx_nt(dy_ref[...], w_ref[...])
        g, u = gu_ref[:, :GU_TILE], gu_ref[:, GU_TILE:]
        sg = _sigmoid(g)
        o_ref[:, :GU_TILE] = (da * u * sg * (1.0 + g * (1.0 - sg))).astype(o_ref.dtype)
        o_ref[:, GU_TILE:] = (da * g * sg).astype(o_ref.dtype)

    in_specs = [pl.BlockSpec((tm, d), lambda j, i: (i, 0)), pl.BlockSpec((GU_TILE, d), lambda j, i: (j, 0)),
                pl.BlockSpec((tm, 2 * GU_TILE), lambda j, i: (i, j))]
    args = [dy, w_d, gu]
    if after is not None:
        in_specs.append(pl.BlockSpec(memory_space=pl.ANY))
        args.append(after)
    return pl.pallas_call(
        body, name=name, grid=(D_FF // GU_TILE, t // tm), in_specs=in_specs,
        out_specs=pl.BlockSpec((tm, 2 * GU_TILE), lambda j, i: (i, j)),
        out_shape=jax.ShapeDtypeStruct((t, 2 * D_FF), MXU_DTYPE), compiler_params=_cp("parallel", "parallel"))(*args)


def mm_nn_res_loss(a, b, res, target, name, tm=512):
    t, k = a.shape
    d = b.shape[1]
    tm = min(tm, t)

    def body(a_ref, b_ref, res_ref, t_ref, dy_ref, l_ref):
        e = res_ref[...] + _mx(a_ref[...], b_ref[...]) - t_ref[...]
        dy_ref[...] = e * (1.0 / d)
        part = jnp.zeros((1, 128), F32) + 0.5 * jnp.sum(jnp.mean(e * e, axis=-1, keepdims=True), axis=0, keepdims=True)
        l_ref[...] = jnp.where(pl.program_id(0) == 0, part, l_ref[...] + part)

    row = pl.BlockSpec((tm, d), lambda i: (i, 0))
    return pl.pallas_call(
        body, name=name, grid=(t // tm,),
        in_specs=[pl.BlockSpec((tm, k), lambda i: (i, 0)), pl.BlockSpec((k, d), lambda i: (0, 0)), row, row],
        out_specs=(row, pl.BlockSpec((1, 128), lambda i: (0, 0))),
        out_shape=(jax.ShapeDtypeStruct((t, d), F32), jax.ShapeDtypeStruct((1, 128), F32)),
        compiler_params=_cp("arbitrary"))(a, b, res, target)


QK_W = Q_W + KV_W
_QK_TILE = 256


def _qk_mats():
    idx = np.arange(_QK_TILE)
    half = HEAD_DIM // 2
    same = (idx[:, None] // HEAD_DIM) == (idx[None, :] // HEAD_DIM)
    lo = (idx % HEAD_DIM) < half
    rot = np.where((idx[:, None] == idx[None, :] + half) & lo[None, :], -1.0, 0.0)
    rot = rot + np.where((idx[:, None] == idx[None, :] - half) & ~lo[None, :], 1.0, 0.0)
    return jnp.asarray(same, F32), jnp.asarray(rot, F32)


def _qk_rows(q_gain, k_gain, cosf, sinf):
    gain = jnp.concatenate([q_gain] * ATTN_HEADS + [k_gain] * KV_HEADS, axis=-1)
    return gain, jnp.concatenate([cosf, cosf], axis=-1), jnp.concatenate([sinf, sinf], axis=-1)


def _qk_tiles(a, mat, transposed=False):
    outs = []
    for c0 in range(0, QK_W, _QK_TILE):
        w = min(_QK_TILE, QK_W - c0)
        mt = (mat.T if transposed else mat)[:w, :w].astype(MXU_DTYPE)
        at = a[:, c0:c0 + w]
        hi = at.astype(MXU_DTYPE)
        lo = (at - hi.astype(F32)).astype(MXU_DTYPE)
        outs.append(jnp.dot(hi, mt, preferred_element_type=F32) + jnp.dot(lo, mt, preferred_element_type=F32))
    return jnp.concatenate(outs, axis=-1)


def qk_prep_fwd(proj, q_gain, k_gain, cosf, sinf, name, tm=256):
    t = proj.shape[0]
    gmat, rmat = _qk_mats()
    gain, c2, s2 = _qk_rows(q_gain, k_gain, cosf, sinf)
    rep = QK_W // 128

    def body(p_ref, g_ref, c_ref, s_ref, gm_ref, rm_ref, q_ref, k_ref):
        x = p_ref[...]
        r = lax.rsqrt(_qk_tiles(x * x, gm_ref[...]) * (1.0 / HEAD_DIM) + EPS)
        xn = x * r * g_ref[...]
        c = jnp.concatenate([c_ref[...]] * rep, axis=-1)
        s = jnp.concatenate([s_ref[...]] * rep, axis=-1)
        out = xn * c + _qk_tiles(xn, rm_ref[...]) * s
        q_ref[...] = out[:, :Q_W]
        k_ref[...] = out[:, Q_W:]

    full = pl.BlockSpec((_QK_TILE, _QK_TILE), lambda i: (0, 0))
    tab = pl.BlockSpec((tm, 128), lambda i: (i, 0))
    return pl.pallas_call(
        body, name=name, grid=(t // tm,),
        in_specs=[pl.BlockSpec((tm, QK_W), lambda i: (i, 0)), pl.BlockSpec((1, QK_W), lambda i: (0, 0)), tab, tab,
                  full, full],
        out_specs=(pl.BlockSpec((tm, Q_W), lambda i: (i, 0)), pl.BlockSpec((tm, KV_W), lambda i: (i, 0))),
        out_shape=(jax.ShapeDtypeStruct((t, Q_W), F32), jax.ShapeDtypeStruct((t, KV_W), F32)),
        compiler_params=_cp("parallel"))(proj, gain, c2, s2, gmat, rmat)


def qk_prep_bwd(proj, q_gain, k_gain, cosf, sinf, dq, dk, name, tm=256):
    t = proj.shape[0]
    gmat, rmat = _qk_mats()
    gain, c2, s2 = _qk_rows(q_gain, k_gain, cosf, sinf)
    rep = QK_W // 128
    lanes = np.arange(QK_W)[:, None]
    fold = jnp.asarray(lanes % HEAD_DIM + np.where(lanes >= Q_W, HEAD_DIM, 0) == np.arange(128)[None, :], F32)

    def body(p_ref, g_ref, c_ref, s_ref, gm_ref, rm_ref, f_ref, dq_ref, dk_ref, o_ref, dg_ref):
        x = p_ref[...]
        r = lax.rsqrt(_qk_tiles(x * x, gm_ref[...]) * (1.0 / HEAD_DIM) + EPS)
        xh = x * r
        c = jnp.concatenate([c_ref[...]] * rep, axis=-1)
        s = jnp.concatenate([s_ref[...]] * rep, axis=-1)
        dout = jnp.concatenate([dq_ref[...], dk_ref[...]], axis=-1)
        dxn = dout * c + _qk_tiles(dout * s, rm_ref[...], transposed=True)
        part = _hi(jnp.sum(dxn * xh, axis=0, keepdims=True), f_ref[...])
        dxh = dxn * g_ref[...]
        mean = _qk_tiles(dxh * xh, gm_ref[...]) * (1.0 / HEAD_DIM)
        o_ref[...] = (r * (dxh - xh * mean)).astype(o_ref.dtype)
        dg_ref[...] = jnp.where(pl.program_id(0) == 0, part, dg_ref[...] + part)

    full = pl.BlockSpec((_QK_TILE, _QK_TILE), lambda i: (0, 0))
    tab = pl.BlockSpec((tm, 128), lambda i: (i, 0))
    dqk, dg = pl.pallas_call(
        body, name=name, grid=(t // tm,),
        in_specs=[pl.BlockSpec((tm, QK_W), lambda i: (i, 0)), pl.BlockSpec((1, QK_W), lambda i: (0, 0)), tab, tab,
                  full, full, pl.BlockSpec((QK_W, 128), lambda i: (0, 0)),
                  pl.BlockSpec((tm, Q_W), lambda i: (i, 0)), pl.BlockSpec((tm, KV_W), lambda i: (i, 0))],
        out_specs=(pl.BlockSpec((tm, QK_W), lambda i: (i, 0)), pl.BlockSpec((1, 128), lambda i: (0, 0))),
        out_shape=(jax.ShapeDtypeStruct((t, QK_W), MXU_DTYPE), jax.ShapeDtypeStruct((1, 128), F32)),
        compiler_params=_cp("arbitrary"))(proj, gain, c2, s2, gmat, rmat, fold, dq, dk)
    return dqk, dg[:, :HEAD_DIM], dg[:, HEAD_DIM:]


def _swa_valid(n, grp):
    qi = lax.broadcasted_iota(jnp.int32, (grp * ATTN_BLOCK, 2 * ATTN_BLOCK), 0) & (ATTN_BLOCK - 1)
    kj = lax.broadcasted_iota(jnp.int32, (grp * ATTN_BLOCK, 2 * ATTN_BLOCK), 1)
    diff = qi + ATTN_BLOCK - kj
    return (diff >= 0) & (diff < ATTN_BLOCK) & (n * ATTN_BLOCK - ATTN_BLOCK + kj >= 0)


def _stack_heads(ref, g, grp):
    return jnp.concatenate([ref[:, (g * grp + j) * HEAD_DIM:(g * grp + j + 1) * HEAD_DIM] for j in range(grp)], axis=0)


def _stack_sinks(s_ref, g, grp):
    return jnp.concatenate([jnp.zeros((ATTN_BLOCK, 1), F32) + s_ref[0:1, g * grp + j:g * grp + j + 1]
                            for j in range(grp)], axis=0)


def swa_fwd(q, k, proj, sinks, name):
    t = q.shape[0]
    nb = t // ATTN_BLOCK
    scale = HEAD_DIM ** -0.5
    grp = ATTN_HEADS // KV_HEADS

    def body(q_ref, kc_ref, kp_ref, vc_ref, vp_ref, s_ref, y_ref, mix_ref, yt_ref, lse_ref):
        n = pl.program_id(0)
        valid = _swa_valid(n, grp)
        kk = jnp.concatenate([kp_ref[...], kc_ref[...]], axis=0).astype(MXU_DTYPE)
        vv = jnp.concatenate([vp_ref[...], vc_ref[...]], axis=0).astype(MXU_DTYPE)
        lane = lax.broadcasted_iota(jnp.int32, (ATTN_BLOCK, ATTN_HEADS), 1)
        gs = range(KV_HEADS)
        qg = [_stack_heads(q_ref, g, grp) for g in gs]
        sink = [_stack_sinks(s_ref, g, grp) for g in gs]
        sc = [jnp.where(valid, _mx_nt(qg[g], kk[:, g * HEAD_DIM:(g + 1) * HEAD_DIM]) * scale, NEG) for g in gs]
        m = [jnp.maximum(jnp.max(sc[g], axis=-1, keepdims=True), sink[g]) for g in gs]
        e = [jnp.exp(sc[g] - m[g]) for g in gs]
        den = [jnp.sum(e[g], axis=-1, keepdims=True) + jnp.exp(sink[g] - m[g]) for g in gs]
        og = [_mx(e[g] / den[g], vv[:, g * HEAD_DIM:(g + 1) * HEAD_DIM]) for g in gs]
        lg = [m[g] + jnp.log(den[g]) for g in gs]
        lse = jnp.zeros((ATTN_BLOCK, ATTN_HEADS), F32)
        outs = []
        for h in range(ATTN_HEADS):
            rows = slice((h % grp) * ATTN_BLOCK, (h % grp + 1) * ATTN_BLOCK)
            outs.append(og[h // grp][rows])
            lse = jnp.where(lane == h, lg[h // grp][rows], lse)
        y = jnp.concatenate(outs, axis=-1)
        y_ref[...] = y
        mix_ref[...] = y.astype(mix_ref.dtype)
        yt_ref[...] = y.T.astype(yt_ref.dtype)
        lse_ref[...] = lse

    cur = lambda n: (n, 0)
    prev = lambda n: (jnp.maximum(n - 1, 0), 0)
    vcol = (Q_W + KV_W) // KV_W
    return pl.pallas_call(
        body, name=name, grid=(nb,),
        in_specs=[pl.BlockSpec((ATTN_BLOCK, Q_W), cur), pl.BlockSpec((ATTN_BLOCK, KV_W), cur),
                  pl.BlockSpec((ATTN_BLOCK, KV_W), prev),
                  pl.BlockSpec((ATTN_BLOCK, KV_W), lambda n: (n, vcol)),
                  pl.BlockSpec((ATTN_BLOCK, KV_W), lambda n: (jnp.maximum(n - 1, 0), vcol)),
                  pl.BlockSpec((1, ATTN_HEADS), lambda n: (0, 0))],
        out_specs=(pl.BlockSpec((ATTN_BLOCK, Q_W), cur), pl.BlockSpec((ATTN_BLOCK, Q_W), cur),
                   pl.BlockSpec((Q_W, ATTN_BLOCK), lambda n: (0, n)), pl.BlockSpec((ATTN_BLOCK, ATTN_HEADS), cur)),
        out_shape=(jax.ShapeDtypeStruct((t, Q_W), F32), jax.ShapeDtypeStruct((t, Q_W + CONV_CH), MXU_DTYPE),
                   jax.ShapeDtypeStruct((Q_W + CONV_CH, t), MXU_DTYPE), jax.ShapeDtypeStruct((t, ATTN_HEADS), F32)),
        compiler_params=_cp("parallel"))(q, k, k, proj, proj, sinks)


def swa_bwd(q, k, proj, sinks, y, lse, dmix, name):
    t = q.shape[0]
    nb = t // ATTN_BLOCK
    scale = HEAD_DIM ** -0.5
    grp = ATTN_HEADS // KV_HEADS

    def body(q_ref, kc_ref, kp_ref, vc_ref, vp_ref, s_ref, y_ref, lse_ref, dy_ref,
             dq_ref, dk_ref, dv_ref, ds_ref, dkc, dvc):
        n = pl.program_id(0)

        @pl.when(n == 0)
        def _():
            dkc[...] = jnp.zeros_like(dkc)
            dvc[...] = jnp.zeros_like(dvc)
            ds_ref[...] = jnp.zeros_like(ds_ref)

        @pl.when(n < nb)
        def _():
            valid = _swa_valid(n, grp)
            kk = jnp.concatenate([kp_ref[...], kc_ref[...]], axis=0).astype(MXU_DTYPE)
            vv = jnp.concatenate([vp_ref[...], vc_ref[...]], axis=0).astype(MXU_DTYPE)
            lane = lax.broadcasted_iota(jnp.int32, (1, ATTN_HEADS), 1)
            gs = range(KV_HEADS)
            kg = [kk[:, g * HEAD_DIM:(g + 1) * HEAD_DIM] for g in gs]
            vg = [vv[:, g * HEAD_DIM:(g + 1) * HEAD_DIM] for g in gs]
            qg = [_stack_heads(q_ref, g, grp).astype(MXU_DTYPE) for g in gs]
            dog = [_stack_heads(dy_ref, g, grp) for g in gs]
            og = [_stack_heads(y_ref, g, grp) for g in gs]
            lg = [jnp.concatenate([lse_ref[:, g * grp + j:g * grp + j + 1] for j in range(grp)], axis=0) for g in gs]
            sink = [_stack_sinks(s_ref, g, grp) for g in gs]
            sc = [jnp.where(valid, _mx_nt(qg[g], kg[g]) * scale, NEG) for g in gs]
            p = [jnp.exp(sc[g] - lg[g]) for g in gs]
            delta = [jnp.sum(dog[g] * og[g], axis=-1, keepdims=True) for g in gs]
            ds = [p[g] * (_mx_nt(dog[g], vg[g]) - delta[g]) for g in gs]
            dqg = [_mx(ds[g], kg[g]) * scale for g in gs]
            dkf = jnp.concatenate([_mx_tn(ds[g], qg[g]) * scale for g in gs], axis=-1)
            dvf = jnp.concatenate([_mx_tn(p[g], dog[g]) for g in gs], axis=-1)
            dsk = [jnp.exp(sink[g] - lg[g]) * delta[g] for g in gs]
            dsink = jnp.zeros((1, ATTN_HEADS), F32)
            dqs = []
            for h in range(ATTN_HEADS):
                rows = slice((h % grp) * ATTN_BLOCK, (h % grp + 1) * ATTN_BLOCK)
                dqs.append(dqg[h // grp][rows])
                dsink = jnp.where(lane == h, -jnp.sum(dsk[h // grp][rows], axis=0, keepdims=True), dsink)
            dq_ref[...] = jnp.concatenate(dqs, axis=-1)
            dk_ref[...] = dkc[...] + dkf[:ATTN_BLOCK]
            dv_ref[...] = (dvc[...] + dvf[:ATTN_BLOCK]).astype(dv_ref.dtype)
            dkc[...] = dkf[ATTN_BLOCK:]
            dvc[...] = dvf[ATTN_BLOCK:]
            ds_ref[...] += dsink

        @pl.when(n == nb)
        def _():
            dk_ref[...] = dkc[...]
            dv_ref[...] = dvc[...].astype(dv_ref.dtype)

    cur = lambda n: (jnp.minimum(n, nb - 1), 0)
    prev = lambda n: (jnp.clip(n - 1, 0, nb - 1), 0)
    vcol = (Q_W + KV_W) // KV_W
    return pl.pallas_call(
        body, name=name, grid=(nb + 1,),
        in_specs=[pl.BlockSpec((ATTN_BLOCK, Q_W), cur), pl.BlockSpec((ATTN_BLOCK, KV_W), cur),
                  pl.BlockSpec((ATTN_BLOCK, KV_W), prev),
                  pl.BlockSpec((ATTN_BLOCK, KV_W), lambda n: (jnp.minimum(n, nb - 1), vcol)),
                  pl.BlockSpec((ATTN_BLOCK, KV_W), lambda n: (jnp.clip(n - 1, 0, nb - 1), vcol)),
                  pl.BlockSpec((1, ATTN_HEADS), lambda n: (0, 0)),
                  pl.BlockSpec((ATTN_BLOCK, Q_W), cur), pl.BlockSpec((ATTN_BLOCK, ATTN_HEADS), cur),
                  pl.BlockSpec((ATTN_BLOCK, Q_W), cur)],
        out_specs=(pl.BlockSpec((ATTN_BLOCK, Q_W), cur), pl.BlockSpec((ATTN_BLOCK, KV_W), prev),
                   pl.BlockSpec((ATTN_BLOCK, KV_W), prev), pl.BlockSpec((1, ATTN_HEADS), lambda n: (0, 0))),
        out_shape=(jax.ShapeDtypeStruct((t, Q_W), F32), jax.ShapeDtypeStruct((t, KV_W), F32),
                   jax.ShapeDtypeStruct((t, KV_W), MXU_DTYPE), jax.ShapeDtypeStruct((1, ATTN_HEADS), F32)),
        scratch_shapes=[pltpu.VMEM((ATTN_BLOCK, KV_W), F32), pltpu.VMEM((ATTN_BLOCK, KV_W), F32)],
        compiler_params=_cp("arbitrary"))(q, k, k, proj, proj, sinks, y, lse, dmix)


GC_W = 256
_GB0, _GC0, _XI0 = 768 // GC_W, 1280 // GC_W, 1792 // GC_W
HALO = 8


def gconv_fwd(proj, conv_w, mix, mix_t, name, tm=512):
    t = proj.shape[0]
    hb = tm // HALO
    half = Q_W // GC_W

    def body(gb_ref, gc_ref, xi_ref, gch_ref, xih_ref, w_ref, mix_in, mixt_in, y_ref, yt_ref):
        i = pl.program_id(1)
        u = gc_ref[...] * xi_ref[...]
        uh = jnp.where(i == 0, 0.0, gch_ref[...] * xih_ref[...])
        up = jnp.concatenate([uh, u], axis=0)
        cv = w_ref[0:1, :] * up[HALO - 2:HALO - 2 + tm]
        cv = cv + w_ref[1:2, :] * up[HALO - 1:HALO - 1 + tm]
        cv = cv + w_ref[2:3, :] * u
        y = gb_ref[...] * cv
        y_ref[...] = y.astype(y_ref.dtype)
        yt_ref[...] = y.T.astype(yt_ref.dtype)

    def col(c0):
        return pl.BlockSpec((tm, GC_W), lambda cj, i: (i, c0 + cj))

    def halo(c0):
        return pl.BlockSpec((HALO, GC_W), lambda cj, i: (jnp.maximum(i * hb - 1, 0), c0 + cj))

    return pl.pallas_call(
        body, name=name, grid=(CONV_CH // GC_W, t // tm),
        in_specs=[col(_GB0), col(_GC0), col(_XI0), halo(_GC0), halo(_XI0),
                  pl.BlockSpec((3, GC_W), lambda cj, i: (0, cj)),
                  pl.BlockSpec(memory_space=pl.ANY), pl.BlockSpec(memory_space=pl.ANY)],
        out_specs=(pl.BlockSpec((tm, GC_W), lambda cj, i: (i, half + cj)),
                   pl.BlockSpec((GC_W, tm), lambda cj, i: (half + cj, i))),
        out_shape=(jax.ShapeDtypeStruct(mix.shape, mix.dtype), jax.ShapeDtypeStruct(mix_t.shape, mix_t.dtype)),
        input_output_aliases={6: 0, 7: 1},
        compiler_params=_cp("parallel", "parallel"))(proj, proj, proj, proj, proj, conv_w, mix, mix_t)


def gconv_bwd(proj, conv_w, dmix, name, tm=512):
    t = proj.shape[0]
    hb = tm // HALO
    nt = t // tm
    dy0 = Q_W // GC_W

    def body(gb_ref, gc_ref, xi_ref, gch_ref, xih_ref, gbn_ref, dyn_ref, dy_ref, w_ref,
             dgb_ref, dgc_ref, dxi_ref, dw_ref):
        i = pl.program_id(1)
        gc, xi, gb, dy = gc_ref[...], xi_ref[...], gb_ref[...], dy_ref[...]
        u = gc * xi
        uh = jnp.where(i == 0, 0.0, gch_ref[...] * xih_ref[...])
        up = jnp.concatenate([uh, u], axis=0)
        u2 = up[HALO - 2:HALO - 2 + tm]
        u1 = up[HALO - 1:HALO - 1 + tm]
        cv = w_ref[0:1, :] * u2 + w_ref[1:2, :] * u1 + w_ref[2:3, :] * u
        dgb_ref[...] = (dy * cv).astype(dgb_ref.dtype)
        dcv = dy * gb
        dcvn = jnp.where(i == nt - 1, 0.0, dyn_ref[...] * gbn_ref[...])
        dcvp = jnp.concatenate([dcv, dcvn], axis=0)
        du = w_ref[0:1, :] * dcvp[2:2 + tm] + w_ref[1:2, :] * dcvp[1:1 + tm] + w_ref[2:3, :] * dcv
        dgc_ref[...] = (du * xi).astype(dgc_ref.dtype)
        dxi_ref[...] = (du * gc).astype(dxi_ref.dtype)
        dw = jnp.concatenate([jnp.sum(dcv * u2, axis=0, keepdims=True), jnp.sum(dcv * u1, axis=0, keepdims=True),
                              jnp.sum(dcv * u, axis=0, keepdims=True)], axis=0)

        @pl.when(i == 0)
        def _():
            dw_ref[...] = dw

        @pl.when(i > 0)
        def _():
            dw_ref[...] += dw

    def col(c0):
        return pl.BlockSpec((tm, GC_W), lambda cj, i: (i, c0 + cj))

    def halo(c0):
        return pl.BlockSpec((HALO, GC_W), lambda cj, i: (jnp.maximum(i * hb - 1, 0), c0 + cj))

    def nxt(c0):
        return pl.BlockSpec((HALO, GC_W), lambda cj, i: (jnp.minimum((i + 1) * hb, t // HALO - 1), c0 + cj))

    out = pl.BlockSpec((tm, GC_W), lambda cj, i: (i, cj))
    return pl.pallas_call(
        body, name=name, grid=(CONV_CH // GC_W, nt),
        in_specs=[col(_GB0), col(_GC0), col(_XI0), halo(_GC0), halo(_XI0), nxt(_GB0), nxt(dy0), col(dy0),
                  pl.BlockSpec((3, GC_W), lambda cj, i: (0, cj))],
        out_specs=(out, out, out, pl.BlockSpec((3, GC_W), lambda cj, i: (0, cj))),
        out_shape=(jax.ShapeDtypeStruct((t, CONV_CH), MXU_DTYPE),) * 3 + (jax.ShapeDtypeStruct((3, CONV_CH), F32),),
        compiler_params=_cp("parallel", "arbitrary"))(proj, proj, proj, proj, proj, proj, dmix, dmix, conv_w)


_QKV_W = 3 * DN_W
_BA_COL = (4 * DN_W) // 128
_Z_COL = _QKV_W // DN_W


def gdn_prep_fwd(proj, conv_w, alog_row, dtb_row, name, tm=256):
    t = proj.shape[0]
    hb = tm // HALO
    qscale = DN_DIM ** -0.5

    def body(x_ref, xh_ref, w_ref, ba_ref, al_ref, dt_ref, q_ref, k_ref, v_ref, bg_ref):
        i = pl.program_id(0)
        for gi in range(3 * DN_HEADS):
            sl = slice(gi * DN_DIM, (gi + 1) * DN_DIM)
            xp = jnp.concatenate([jnp.where(i == 0, 0.0, xh_ref[:, sl]), x_ref[:, sl]], axis=0)
            c = w_ref[0:1, sl] * xp[HALO - 3:HALO - 3 + tm]
            for j in range(1, 4):
                c = c + w_ref[j:j + 1, sl] * xp[HALO - 3 + j:HALO - 3 + j + tm]
            s = c * _sigmoid(c)
            osl = slice((gi % DN_HEADS) * DN_DIM, (gi % DN_HEADS + 1) * DN_DIM)
            if gi < DN_HEADS:
                q_ref[:, osl] = s * lax.rsqrt(jnp.sum(s * s, axis=-1, keepdims=True) + EPS) * qscale
            elif gi < 2 * DN_HEADS:
                k_ref[:, osl] = s * lax.rsqrt(jnp.sum(s * s, axis=-1, keepdims=True) + EPS)
            else:
                v_ref[:, osl] = s
        ba = ba_ref[...]
        lane = lax.broadcasted_iota(jnp.int32, ba.shape, 1)
        gval = -jnp.exp(al_ref[...]) * _softplus(ba + dt_ref[...])
        bg_ref[...] = jnp.where(lane < DN_HEADS, _sigmoid(ba), jnp.where(lane < 2 * DN_HEADS, gval, 0.0))

    row = pl.BlockSpec((tm, DN_W), lambda i: (i, 0))
    one = pl.BlockSpec((1, 128), lambda i: (0, 0))
    return pl.pallas_call(
        body, name=name, grid=(t // tm,),
        in_specs=[pl.BlockSpec((tm, _QKV_W), lambda i: (i, 0)),
                  pl.BlockSpec((HALO, _QKV_W), lambda i: (jnp.maximum(i * hb - 1, 0), 0)),
                  pl.BlockSpec((4, _QKV_W), lambda i: (0, 0)),
                  pl.BlockSpec((tm, 128), lambda i: (i, _BA_COL)), one, one],
        out_specs=(row, row, row, pl.BlockSpec((tm, 128), lambda i: (i, 0))),
        out_shape=(jax.ShapeDtypeStruct((t, DN_W), F32),) * 3 + (jax.ShapeDtypeStruct((t, 128), F32),),
        compiler_params=_cp("parallel"))(proj, proj, conv_w, proj, alog_row, dtb_row)


def gdn_prep_bwd(proj, conv_w, alog_row, dtb_row, dq, dk, dv, dbg, name, tm=256):
    t = proj.shape[0]
    hb = tm // HALO
    nt = t // tm
    qscale = DN_DIM ** -0.5
    te = tm + HALO

    def body(x_ref, xh_ref, xn_ref, w_ref, ba_ref, al_ref, dt_ref, dq_ref, dk_ref, dv_ref,
             dqn_ref, dkn_ref, dvn_ref, dbg_ref, dx_ref, dba_ref, dw_ref, ddt_ref, dal_ref):
        i = pl.program_id(0)
        first = i == 0
        last = i == nt - 1
        dws = []
        for gi in range(3 * DN_HEADS):
            sl = slice(gi * DN_DIM, (gi + 1) * DN_DIM)
            osl = slice((gi % DN_HEADS) * DN_DIM, (gi % DN_HEADS + 1) * DN_DIM)
            xe = jnp.concatenate([jnp.where(first, 0.0, xh_ref[:, sl]), x_ref[:, sl], xn_ref[:, sl]], axis=0)
            c = w_ref[0:1, sl] * xe[HALO - 3:HALO - 3 + te]
            for j in range(1, 4):
                c = c + w_ref[j:j + 1, sl] * xe[HALO - 3 + j:HALO - 3 + j + te]
            sg = _sigmoid(c)
            s = c * sg
            d_ref, dn_ref = ((dq_ref, dqn_ref), (dk_ref, dkn_ref), (dv_ref, dvn_ref))[gi // DN_HEADS]
            dy = jnp.concatenate([d_ref[:, osl], jnp.where(last, 0.0, dn_ref[:, osl])], axis=0)
            if gi < 2 * DN_HEADS:
                r = lax.rsqrt(jnp.sum(s * s, axis=-1, keepdims=True) + EPS)
                sh = s * r
                ds = r * (dy - sh * jnp.sum(sh * dy, axis=-1, keepdims=True))
                if gi < DN_HEADS:
                    ds = ds * qscale
            else:
                ds = dy
            dc = ds * sg * (1.0 + c * (1.0 - sg))
            dcs = [dc[3 - j:3 - j + tm] for j in range(4)]
            dx = w_ref[0:1, sl] * dcs[0]
            for j in range(1, 4):
                dx = dx + w_ref[j:j + 1, sl] * dcs[j]
            dx_ref[:, sl] = dx.astype(dx_ref.dtype)
            x0 = x_ref[:, sl]
            dws.append(jnp.concatenate([jnp.sum(dcs[j] * x0, axis=0, keepdims=True) for j in range(4)], axis=0))
        dw = jnp.concatenate(dws, axis=-1)
        ba = ba_ref[...]
        dbgv = dbg_ref[...]
        lane = lax.broadcasted_iota(jnp.int32, ba.shape, 1)
        beta = _sigmoid(ba)
        ea = -jnp.exp(al_ref[...])
        zin = ba + dt_ref[...]
        is_b = lane < DN_HEADS
        is_a = (lane >= DN_HEADS) & (lane < 2 * DN_HEADS)
        da = jnp.where(is_a, dbgv * ea * _sigmoid(zin), 0.0)
        dba_ref[...] = jnp.where(is_b, dbgv * beta * (1.0 - beta), da).astype(dba_ref.dtype)
        ddt = jnp.sum(da, axis=0, keepdims=True)
        dal = jnp.sum(jnp.where(is_a, dbgv * ea * _softplus(zin), 0.0), axis=0, keepdims=True)

        @pl.when(first)
        def _():
            dw_ref[...] = dw
            ddt_ref[...] = ddt
            dal_ref[...] = dal

        @pl.when(i > 0)
        def _():
            dw_ref[...] += dw
            ddt_ref[...] += ddt
            dal_ref[...] += dal

    row = pl.BlockSpec((tm, DN_W), lambda i: (i, 0))
    nrow = pl.BlockSpec((HALO, DN_W), lambda i: (jnp.minimum((i + 1) * hb, t // HALO - 1), 0))
    one = pl.BlockSpec((1, 128), lambda i: (0, 0))
    return pl.pallas_call(
        body, name=name, grid=(nt,),
        in_specs=[pl.BlockSpec((tm, _QKV_W), lambda i: (i, 0)),
                  pl.BlockSpec((HALO, _QKV_W), lambda i: (jnp.maximum(i * hb - 1, 0), 0)),
                  pl.BlockSpec((HALO, _QKV_W), lambda i: (jnp.minimum((i + 1) * hb, t // HALO - 1), 0)),
                  pl.BlockSpec((4, _QKV_W), lambda i: (0, 0)),
                  pl.BlockSpec((tm, 128), lambda i: (i, _BA_COL)), one, one,
                  row, row, row, nrow, nrow, nrow, pl.BlockSpec((tm, 128), lambda i: (i, 0))],
        out_specs=(pl.BlockSpec((tm, _QKV_W), lambda i: (i, 0)), pl.BlockSpec((tm, 128), lambda i: (i, 0)),
                   pl.BlockSpec((4, _QKV_W), lambda i: (0, 0)), one, one),
        out_shape=(jax.ShapeDtypeStruct((t, _QKV_W), MXU_DTYPE), jax.ShapeDtypeStruct((t, 128), MXU_DTYPE),
                   jax.ShapeDtypeStruct((4, _QKV_W), F32), jax.ShapeDtypeStruct((1, 128), F32),
                   jax.ShapeDtypeStruct((1, 128), F32)),
        compiler_params=_cp("arbitrary"))(proj, proj, proj, conv_w, proj, alog_row, dtb_row, dq, dk, dv, dq, dk, dv, dbg)


def _chunk_masks():
    r = lax.broadcasted_iota(jnp.int32, (DN_CHUNK, DN_CHUNK), 0)
    c = lax.broadcasted_iota(jnp.int32, (DN_CHUNK, DN_CHUNK), 1)
    return r >= c, r > c


INV_PACK = 2


def _inv_unit_lower_many(mats):
    n = DN_CHUNK
    wide = INV_PACK * n
    r = lax.broadcasted_iota(jnp.int32, (wide, wide), 0)
    c = lax.broadcasted_iota(jnp.int32, (wide, wide), 1)
    same = (r & -n) == (c & -n)
    eye = jnp.where((r[:n] == (c[:n] & (n - 1))), 1.0, 0.0)

    def blockdiag(row):
        return jnp.where(same, jnp.concatenate([row] * INV_PACK, axis=0), 0.0)

    packs = [jnp.concatenate(mats[g:g + INV_PACK], axis=-1) for g in range(0, len(mats), INV_PACK)]
    xs = [eye - a for a in packs]
    pws = [_hi(a, blockdiag(a)) for a in packs]
    for step in range(5):
        if step < 4:
            both = [_hi(jnp.concatenate([x, pw], axis=0), blockdiag(pw)) for x, pw in zip(xs, pws)]
            xs = [x + b[:n] for x, b in zip(xs, both)]
            pws = [b[n:] for b in both]
        else:
            xs = [x + _hi(x, blockdiag(pw)) for x, pw in zip(xs, pws)]
    return [x[:, j * n:(j + 1) * n] for x in xs for j in range(INV_PACK)]


def _chunk_common(q, k, v, beta, gc, gcr, lower, strict):
    gam = jnp.exp(jnp.where(lower, gc - gcr, NEG))
    eg = jnp.exp(gc)
    gl = gc[DN_CHUNK - 1:DN_CHUNK, :]
    kdf = jnp.exp(gl - gc)
    kb = k * beta
    bmat = _mx_nt(kb, k)
    qmat = _mx_nt(q, k)
    return gam, eg, jnp.exp(gl), kdf, kb, bmat, qmat


DN_STEP = 4


def gdn_fwd(q, k, v, bg, name):
    t = q.shape[0]
    n_chunks = t // DN_CHUNK

    def body(q_ref, k_ref, v_ref, bg_ref, o_ref, sall_ref, tall_ref, s_ref):
        n = pl.program_id(0)

        @pl.when(n == 0)
        def _():
            s_ref[...] = jnp.zeros_like(s_ref)

        lower, strict = _chunk_masks()
        ltri = jnp.where(lower, 1.0, 0.0)
        hs = range(DN_HEADS)
        sl = [slice(h * DN_DIM, (h + 1) * DN_DIM) for h in hs]
        units = [(c, h) for c in range(DN_STEP) for h in hs]
        nu = range(len(units))
        rs = [slice(c * DN_CHUNK, (c + 1) * DN_CHUNK) for c in range(DN_STEP)]
        bgv = [bg_ref[rs[c], :] for c in range(DN_STEP)]
        gcs = [_hi(ltri, b) for b in bgv]
        gcs_t = [g.T for g in gcs]
        qh = [q_ref[rs[c], sl[h]] for c, h in units]
        kh = [k_ref[rs[c], sl[h]] for c, h in units]
        vh = [v_ref[rs[c], sl[h]] for c, h in units]
        beta = [bgv[c][:, h:h + 1] for c, h in units]
        com = [_chunk_common(qh[u], kh[u], vh[u], beta[u], gcs[c][:, DN_HEADS + h:DN_HEADS + h + 1],
                             gcs_t[c][DN_HEADS + h:DN_HEADS + h + 1, :], lower, strict) for u, (c, h) in enumerate(units)]
        gam, eg, dec, kdf, kb, bmat, qmat = zip(*com)
        tms = _inv_unit_lower_many([jnp.where(strict, bmat[u] * gam[u], 0.0) for u in nu])
        for u, (c, h) in enumerate(units):
            tall_ref[c, h] = tms[u]
        uw = [_hi(tms[u], jnp.concatenate([vh[u] * beta[u], kb[u] * eg[u]], axis=-1)) for u in nu]
        qd = [qh[u] * eg[u] for u in nu]
        pm = [qmat[u] * gam[u] for u in nu]
        kd = [kh[u] * kdf[u] for u in nu]
        st = [s_ref[h] for h in hs]
        for c in range(DN_STEP):
            us = [c * DN_HEADS + h for h in hs]
            for h in hs:
                sall_ref[c, h] = st[h]
            v_new = [uw[us[h]][:, :DN_DIM] - _mx(uw[us[h]][:, DN_DIM:], st[h]) for h in hs]
            o_st = [_mx(qd[us[h]], st[h]) for h in hs]
            o_in = [_mx(pm[us[h]], v_new[h]) for h in hs]
            s_up = [_mx_tn(kd[us[h]], v_new[h]) for h in hs]
            for h in hs:
                o_ref[rs[c], sl[h]] = o_st[h] + o_in[h]
            st = [st[h] * dec[us[h]] + s_up[h] for h in hs]
        for h in hs:
            s_ref[h] = st[h]

    rows = DN_STEP * DN_CHUNK
    row = pl.BlockSpec((rows, DN_W), lambda n: (n, 0))
    return pl.pallas_call(
        body, name=name, grid=(n_chunks // DN_STEP,),
        in_specs=[row, row, row, pl.BlockSpec((rows, 128), lambda n: (n, 0))],
        out_specs=(row, pl.BlockSpec((DN_STEP, DN_HEADS, DN_DIM, DN_DIM), lambda n: (n, 0, 0, 0)),
                   pl.BlockSpec((DN_STEP, DN_HEADS, DN_CHUNK, DN_CHUNK), lambda n: (n, 0, 0, 0))),
        out_shape=(jax.ShapeDtypeStruct((t, DN_W), F32),
                   jax.ShapeDtypeStruct((n_chunks, DN_HEADS, DN_DIM, DN_DIM), F32),
                   jax.ShapeDtypeStruct((n_chunks, DN_HEADS, DN_CHUNK, DN_CHUNK), F32)),
        scratch_shapes=[pltpu.VMEM((DN_HEADS, DN_DIM, DN_DIM), F32)],
        compiler_params=_cp("arbitrary"))(q, k, v, bg)


def gdn_bwd(q, k, v, bg, sall, tall, do, name):
    t = q.shape[0]
    n_chunks = t // DN_CHUNK

    def body(q_ref, k_ref, v_ref, bg_ref, sall_ref, tall_ref, do_ref, dq_ref, dk_ref, dv_ref, dbg_ref, ds_ref):
        n = pl.program_id(0)

        @pl.when(n == 0)
        def _():
            ds_ref[...] = jnp.zeros_like(ds_ref)

        lower, strict = _chunk_masks()
        ltri = jnp.where(lower, 1.0, 0.0)
        bgv = bg_ref[...]
        gcs = _hi(ltri, bgv)
        gcs_t = gcs.T
        lane = lax.broadcasted_iota(jnp.int32, (DN_CHUNK, 128), 1)
        rowi = lax.broadcasted_iota(jnp.int32, (DN_CHUNK, 1), 0)
        hs = range(DN_HEADS)
        each = lambda fn, *ls: [fn(*a) for a in zip(*ls)]
        rsum = lambda a: jnp.sum(a, axis=-1, keepdims=True)
        sl = [slice(h * DN_DIM, (h + 1) * DN_DIM) for h in hs]
        st = [sall_ref[0, h] for h in hs]
        tms = [tall_ref[0, h] for h in hs]
        dsn = [ds_ref[h] for h in hs]
        qh = [q_ref[:, sl[h]] for h in hs]
        kh = [k_ref[:, sl[h]] for h in hs]
        vh = [v_ref[:, sl[h]] for h in hs]
        doh = [do_ref[:, sl[h]] for h in hs]
        beta = [bgv[:, h:h + 1] for h in hs]
        com = [_chunk_common(qh[h], kh[h], vh[h], beta[h], gcs[:, DN_HEADS + h:DN_HEADS + h + 1],
                             gcs_t[DN_HEADS + h:DN_HEADS + h + 1, :], lower, strict) for h in hs]
        gam, eg, dec, kdf, kb, bmat, qmat = zip(*com)
        rhs_w = each(lambda a, b: a * b, kb, eg)
        uw = each(lambda t_, v_, b_, r_: _hi(t_, jnp.concatenate([v_ * b_, r_], axis=-1)), tms, vh, beta, rhs_w)
        qd = each(lambda a, b: a * b, qh, eg)
        kd = each(lambda a, b: a * b, kh, kdf)
        pmat = each(lambda a, b: a * b, qmat, gam)
        v_new = each(lambda uw_, s_: uw_[:, :DN_DIM] - _mx(uw_[:, DN_DIM:], s_), uw, st)
        dqd = each(_mx_nt, doh, st)
        ds_o = each(_mx_tn, qd, doh)
        dp = each(lambda d_, v_: jnp.where(lower, _mx_nt(d_, v_), 0.0), doh, v_new)
        dvn_o = each(_mx_tn, pmat, doh)
        ddec = each(lambda d_, s_: jnp.sum(rsum(d_ * s_), axis=0, keepdims=True), dsn, st)
        dkd = each(_mx_nt, v_new, dsn)
        dvn = each(lambda a, k_, d_: a + _mx(k_, d_), dvn_o, kd, dsn)
        dw = each(lambda d_, s_: -_mx_nt(d_, s_), dvn, st)
        ds_w = each(lambda uw_, d_: _mx_tn(uw_[:, DN_DIM:], d_), uw, dvn)
        for h in hs:
            ds_ref[h] = ds_o[h] + dec[h] * dsn[h] - ds_w[h]
        dr = each(lambda t_, a, b: _hi_tn(t_, jnp.concatenate([a, b], axis=-1)), tms, dvn, dw)
        da = each(lambda r_, uw_: jnp.where(strict, -_hi_nt(r_, uw_), 0.0), dr, uw)
        dru = [r_[:, :DN_DIM] for r_ in dr]
        drw = [r_[:, DN_DIM:] for r_ in dr]
        db = each(lambda a, b: a * b, da, gam)
        dq_m = each(lambda a, b: a * b, dp, gam)
        e = each(lambda a, bm, p_, qm, g_: (a * bm + p_ * qm) * g_, da, bmat, dp, qmat, gam)
        dkb = each(lambda b_, k_, r_, e_: _mx(b_, k_) + r_ * e_, db, kh, drw, eg)
        dk = each(lambda b_, kb_, m_, q_, d_, f_: _mx_tn(b_, kb_) + _mx_tn(m_, q_) + d_ * f_, db, kb, dq_m, qh, dkd, kdf)
        dq = each(lambda m_, k_, d_, e_: _mx(m_, k_) + d_ * e_, dq_m, kh, dqd, eg)
        tk = each(lambda a, b: rsum(a * b), dkd, kd)
        dbeta_all = jnp.zeros((DN_CHUNK, 128), F32)
        dgc_all = jnp.zeros((DN_CHUNK, 128), F32)
        for h in hs:
            dgc = (jnp.sum(e[h], axis=1, keepdims=True) - jnp.sum(e[h].T, axis=1, keepdims=True)
                   + rsum(dqd[h] * qd[h]) - tk[h] + rsum(drw[h] * rhs_w[h]))
            dgl = jnp.sum(tk[h], axis=0, keepdims=True) + ddec[h] * dec[h]
            dgc = dgc + jnp.where(rowi == DN_CHUNK - 1, dgl, 0.0)
            dbeta = rsum(dru[h] * vh[h]) + rsum(dkb[h] * kh[h])
            dq_ref[:, sl[h]] = dq[h]
            dk_ref[:, sl[h]] = dk[h] + dkb[h] * beta[h]
            dv_ref[:, sl[h]] = dru[h] * beta[h]
            dbeta_all = jnp.where(lane == h, dbeta, dbeta_all)
            dgc_all = jnp.where(lane == DN_HEADS + h, dgc, dgc_all)
        dbg_ref[...] = dbeta_all + _hi_tn(ltri, dgc_all)

    rev = lambda n: (n_chunks - 1 - n, 0)
    row = pl.BlockSpec((DN_CHUNK, DN_W), rev)
    small = pl.BlockSpec((DN_CHUNK, 128), rev)
    return pl.pallas_call(
        body, name=name, grid=(n_chunks,),
        in_specs=[row, row, row, small,
                  pl.BlockSpec((1, DN_HEADS, DN_DIM, DN_DIM), lambda n: (n_chunks - 1 - n, 0, 0, 0)),
                  pl.BlockSpec((1, DN_HEADS, DN_CHUNK, DN_CHUNK), lambda n: (n_chunks - 1 - n, 0, 0, 0)), row],
        out_specs=(row, row, row, small),
        out_shape=(jax.ShapeDtypeStruct((t, DN_W), F32),) * 3 + (jax.ShapeDtypeStruct((t, 128), F32),),
        scratch_shapes=[pltpu.VMEM((DN_HEADS, DN_DIM, DN_DIM), F32)],
        compiler_params=_cp("arbitrary"))(q, k, v, bg, sall, tall, do)


def gdn_out_fwd(o, proj, o_gain, name, tm=256):
    t = o.shape[0]

    def body(o_ref, z_ref, g_ref, y_ref, yt_ref):
        for h in range(DN_HEADS):
            sl = slice(h * DN_DIM, (h + 1) * DN_DIM)
            ov, zv = o_ref[:, sl], z_ref[:, sl]
            r = lax.rsqrt(jnp.mean(ov * ov, axis=-1, keepdims=True) + EPS)
            y = ov * r * g_ref[...] * (zv * _sigmoid(zv))
            y_ref[:, sl] = y.astype(y_ref.dtype)
            yt_ref[sl, :] = y.T.astype(yt_ref.dtype)

    row = pl.BlockSpec((tm, DN_W), lambda i: (i, 0))
    return pl.pallas_call(
        body, name=name, grid=(t // tm,),
        in_specs=[row, pl.BlockSpec((tm, DN_W), lambda i: (i, _Z_COL)), pl.BlockSpec((1, DN_DIM), lambda i: (0, 0))],
        out_specs=(row, pl.BlockSpec((DN_W, tm), lambda i: (0, i))),
        out_shape=(jax.ShapeDtypeStruct((t, DN_W), MXU_DTYPE), jax.ShapeDtypeStruct((DN_W, t), MXU_DTYPE)),
        compiler_params=_cp("parallel"))(o, proj, o_gain)


def gdn_out_bwd(o, proj, o_gain, dy, name, tm=256):
    t = o.shape[0]

    def body(o_ref, z_ref, g_ref, dy_ref, do_ref, dz_ref, dg_ref):
        i = pl.program_id(0)
        dg = jnp.zeros((1, DN_DIM), F32)
        for h in range(DN_HEADS):
            sl = slice(h * DN_DIM, (h + 1) * DN_DIM)
            ov, zv, dyv = o_ref[:, sl], z_ref[:, sl], dy_ref[:, sl]
            r = lax.rsqrt(jnp.mean(ov * ov, axis=-1, keepdims=True) + EPS)
            oh = ov * r
            sg = _sigmoid(zv)
            dz_ref[:, sl] = (dyv * oh * g_ref[...] * sg * (1.0 + zv * (1.0 - sg))).astype(dz_ref.dtype)
            don = dyv * (zv * sg)
            dg = dg + jnp.sum(don * oh, axis=0, keepdims=True)
            doh = don * g_ref[...]
            do_ref[:, sl] = r * (doh - oh * jnp.mean(doh * oh, axis=-1, keepdims=True))

        @pl.when(i == 0)
        def _():
            dg_ref[...] = dg

        @pl.when(i > 0)
        def _():
            dg_ref[...] += dg

    row = pl.BlockSpec((tm, DN_W), lambda i: (i, 0))
    one = pl.BlockSpec((1, DN_DIM), lambda i: (0, 0))
    return pl.pallas_call(
        body, name=name, grid=(t // tm,),
        in_specs=[row, pl.BlockSpec((tm, DN_W), lambda i: (i, _Z_COL)), one, row],
        out_specs=(row, row, one),
        out_shape=(jax.ShapeDtypeStruct((t, DN_W), F32), jax.ShapeDtypeStruct((t, DN_W), MXU_DTYPE),
                   jax.ShapeDtypeStruct((1, DN_DIM), F32)),
        compiler_params=_cp("arbitrary"))(o, proj, o_gain, dy)


def _peer(k):
    x, y, c = lax.axis_index("x"), lax.axis_index("y"), lax.axis_index("c")
    px = 1 - x if k & 4 else x
    py = 1 - y if k & 2 else y
    pc = 1 - c if k & 1 else c
    return (px, py, pc), 4 * px + 2 * py + pc


_HBM = pl.BlockSpec(memory_space=pltpu.HBM)
_SEM = pl.BlockSpec(memory_space=pltpu.SEMAPHORE)
_DATAFLOW = pltpu.SideEffectType.DATAFLOW_SIDE_EFFECTING
N_PEER = N_DEV - 1


def send_start(srcs, name, scatter, after):
    na = len(srcs)
    ns = (2 * N_PEER + 1) * na
    lands = [lax.empty((N_DEV,) + (s.shape[1:] if scatter else s.shape), s.dtype) for s in srcs]
    extra = [] if after is None else [after]

    def body(*refs):
        src_refs, land_refs = refs[:na], refs[na:2 * na]
        sems = refs[2 * na + len(extra):2 * na + len(extra) + ns]
        land_out, token = refs[-1 - na:-1], refs[-1]
        _, me = _peer(0)
        for a in range(na):
            pltpu.make_async_copy(src_refs[a].at[me] if scatter else src_refs[a], land_out[a].at[me],
                                  sems[2 * N_PEER * na + a]).start()
        for k in range(1, N_DEV):
            peer, pid = _peer(k)
            for a in range(na):
                pltpu.make_async_remote_copy(
                    src_ref=src_refs[a].at[pid] if scatter else src_refs[a], dst_ref=land_refs[a].at[me],
                    send_sem=sems[2 * (a * N_PEER + k - 1)], recv_sem=sems[2 * (a * N_PEER + k - 1) + 1],
                    device_id=peer, device_id_type=MESH).start()
        token[...] = jnp.zeros_like(token)

    hbm = lambda arrs: tuple(pltpu.HBM(a.shape, a.dtype) for a in arrs)
    outs = pl.pallas_call(
        body, name=name,
        out_shape=(pltpu.SemaphoreType.DMA(()),) * ns + hbm(srcs) + hbm(lands) + (jax.ShapeDtypeStruct((8, 128), F32),),
        in_specs=[_HBM] * (2 * na) + [pl.BlockSpec(memory_space=pl.ANY)] * len(extra),
        out_specs=(_SEM,) * ns + (_HBM,) * (2 * na) + (pl.BlockSpec(memory_space=pltpu.VMEM),),
        input_output_aliases={i: ns + i for i in range(2 * na)},
        compiler_params=pltpu.CompilerParams(has_side_effects=_DATAFLOW),
    )(*[pltpu.with_memory_space_constraint(a, pltpu.HBM) for a in list(srcs) + lands], *extra)
    return outs[:ns], outs[ns:ns + na], outs[ns + na:ns + 2 * na], outs[-1]


def send_wait(sems, srcs_thru, lands_thru, name, scatter, after):
    na = len(srcs_thru)
    ns = (2 * N_PEER + 1) * na

    def body(*refs):
        src_refs, land_refs, sm = refs[:na], refs[na:2 * na], refs[2 * na:2 * na + ns]
        _, me = _peer(0)
        for a in range(na):
            pltpu.make_async_copy(src_refs[a].at[me] if scatter else src_refs[a], land_refs[a].at[me],
                                  sm[2 * N_PEER * na + a]).wait()
        for k in range(1, N_DEV):
            peer, pid = _peer(k)
            for a in range(na):
                cp = pltpu.make_async_remote_copy(
                    src_ref=src_refs[a].at[pid] if scatter else src_refs[a], dst_ref=land_refs[a].at[pid],
                    send_sem=sm[2 * (a * N_PEER + k - 1)], recv_sem=sm[2 * (a * N_PEER + k - 1) + 1],
                    device_id=peer, device_id_type=MESH)
                cp.wait_send()
                cp.wait_recv()

    hbm = lambda arrs: tuple(pltpu.HBM(a.shape, a.dtype) for a in arrs)
    outs = pl.pallas_call(
        body, name=name, out_shape=hbm(srcs_thru) + hbm(lands_thru),
        in_specs=[_HBM] * (2 * na) + [_SEM] * ns + [pl.BlockSpec(memory_space=pl.ANY)], out_specs=(_HBM,) * (2 * na),
        input_output_aliases={i: i for i in range(2 * na)},
        compiler_params=pltpu.CompilerParams(has_side_effects=_DATAFLOW),
    )(*srcs_thru, *lands_thru, *sems, after)
    return outs[na:]


def _adamw(w, g, m, v):
    m = ADAM_B1 * m + (1.0 - ADAM_B1) * g
    v = ADAM_B2 * v + (1.0 - ADAM_B2) * (g * g)
    m_hat = m / (1.0 - ADAM_B1 ** ADAM_STEP)
    v_hat = v / (1.0 - ADAM_B2 ** ADAM_STEP)
    return -ADAM_LR * (m_hat / (jnp.sqrt(v_hat) + ADAM_EPS) + ADAM_WD * w), m, v


def adam_sum(w, pieces, m, v, name, layer=0, into=None):
    nl, r, c = w.shape
    tr = r
    for cand in (256, 128, 64, 32, 16, 8):
        if r % cand == 0:
            tr = cand
            break

    def body(w_ref, p_ref, m_ref, v_ref, *rest):
        g_ref, d_ref, nm_ref, nv_ref = rest[-4:]
        g = p_ref[0].astype(F32)
        for s in range(1, N_DEV):
            g = g + p_ref[s].astype(F32)
        g_ref[0] = g
        d_ref[0], nm_ref[0], nv_ref[0] = _adamw(w_ref[0], g, m_ref[0], v_ref[0])

    row = pl.BlockSpec((1, tr, c), lambda i: (layer, i, 0))
    out = jax.ShapeDtypeStruct((nl, r, c), F32)
    extra = [] if into is None else list(into)
    return pl.pallas_call(
        body, name=name, grid=(r // tr,),
        in_specs=[row, pl.BlockSpec((N_DEV, tr, c), lambda i: (0, i, 0)), row, row]
        + [pl.BlockSpec(memory_space=pl.ANY)] * len(extra),
        out_specs=(row,) * 4, out_shape=(out,) * 4,
        input_output_aliases={4 + i: i for i in range(len(extra))},
        compiler_params=_cp("parallel"))(w, pieces, m, v, *extra)


def sum_rows(gathered, name):
    _, r, c = gathered.shape

    def body(p_ref, o_ref):
        g = p_ref[0]
        for s in range(1, N_DEV):
            g = g + p_ref[s]
        o_ref[...] = g

    return pl.pallas_call(body, name=name, out_shape=jax.ShapeDtypeStruct((r, c), F32))(gathered)


def adam_small(w, g, m, v, name):
    def body(w_ref, g_ref, m_ref, v_ref, d_ref, nm_ref, nv_ref):
        d_ref[...], nm_ref[...], nv_ref[...] = _adamw(w_ref[...], g_ref[...], m_ref[...], v_ref[...])

    out = jax.ShapeDtypeStruct(w.shape, F32)
    return pl.pallas_call(body, name=name, out_shape=(out,) * 3)(w, g, m, v)


def _rope_tables(t):
    inv_freq = 10000.0 ** (-jnp.arange(0, HEAD_DIM, 2, dtype=F32) / HEAD_DIM)
    ang = jnp.arange(t, dtype=F32)[:, None] * inv_freq[None, :]
    cos, sin = jnp.cos(ang), jnp.sin(ang)
    return jnp.concatenate([cos, cos], axis=-1), jnp.concatenate([sin, sin], axis=-1)


def _lane_row(vec8):
    return jnp.pad(vec8.reshape(1, DN_HEADS), ((0, 0), (DN_HEADS, 128 - 2 * DN_HEADS)))


def _ffn_bwd(x, norm_g, w_gu, w_d, saved, dy, tag, after=None):
    ft, gu, at = saved
    dgu = ffn_dact(dy, w_d, gu, f"{tag}_d_gate_up", after=after)
    dwd = mm_at(at, dy, f"{tag}_dw_down")
    dwgu = mm_at(ft, dgu, f"{tag}_dw_gate_up")
    dx, dg = mm_nt_rms_bwd(dgu, w_gu, x, norm_g, dy, f"{tag}_d_norm")
    return dx, dwgu, dwd, dg


def local_step(x, target, small, weights_of, grads_out, after=None):
    t = x.shape[0]
    cosf, sinf = _rope_tables(t)
    alog_row, dtb_row = _lane_row(small["odd_a_log"]), _lane_row(small["odd_dt_bias"])

    h0, h0t = rms_fwd(x, small["even_norm"], "even_norm", after=after)
    we = weights_of("even", h0)
    small = {**small, **we.get("small", {})}
    proj0 = mm_nn(h0, we["w_in"], "even_in_proj")
    qr, kr = qk_prep_fwd(proj0, small["even_q_gain"], small["even_k_gain"], cosf, sinf, "even_qk_prep")
    y_attn, mix0, mix0_t, lse = swa_fwd(qr, kr, proj0, small["even_sinks"], "even_swa")
    mix0, mix0_t = gconv_fwd(proj0, small["even_conv_w"], mix0, mix0_t, "even_gconv")
    we = {**we, **weights_of("even_out", mix0)}
    x1, f0, f0t = mm_nn_res_norm(mix0, we["w_out"], x, small["ffn_norm0"], "even_out_proj")
    w0 = weights_of("ffn0", x1)
    gu0, a0, a0t = ffn_up(f0, w0["gate_up"], "ffn0_gate_up")
    ffn0 = (f0t, gu0, a0t)
    x2, h1, h1t = mm_nn_res_norm(a0, w0["down"], x1, small["odd_norm"], "ffn0_down")

    wo = weights_of("odd", x2)
    proj1 = mm_nn(h1, wo["w_in"], "odd_in_proj")
    qn, kn, vs, bg = gdn_prep_fwd(proj1, small["odd_conv_w"], alog_row, dtb_row, "odd_prep")
    o, sall, tall = gdn_fwd(qn, kn, vs, bg, "odd_delta_rule")
    og, ogt = gdn_out_fwd(o, proj1, small["odd_o_gain"], "odd_gate_norm")
    x3, f1, f1t = mm_nn_res_norm(og, wo["w_out"], x2, small["ffn_norm1"], "odd_out_proj")
    w1 = weights_of("ffn1", x3)
    gu1, a1, a1t = ffn_up(f1, w1["gate_up"], "ffn1_gate_up")
    ffn1 = (f1t, gu1, a1t)
    dy, loss_row = mm_nn_res_loss(a1, w1["down"], x3, target, "ffn1_down_loss")

    gs = {}
    dx3, dwgu, dwd, gs["ffn_norm1"] = _ffn_bwd(x3, small["ffn_norm1"], w1["gate_up"], w1["down"], ffn1, dy, "ffn1")
    tok = grads_out("ffn1", {"gate_up": dwgu, "down": dwd})

    dog = mm_nt(dx3, wo["w_out"], "odd_d_gated", after=tok)
    dwo = mm_at(ogt, dx3, "odd_dw_out")
    do, dz, gs["odd_o_gain"] = gdn_out_bwd(o, proj1, small["odd_o_gain"], dog, "odd_d_gate_norm")
    dqn, dkn, dvs, dbg = gdn_bwd(qn, kn, vs, bg, sall, tall, do, "odd_d_delta_rule")
    dqkv, dba, gs["odd_conv_w"], ddt_row, dal_row = gdn_prep_bwd(
        proj1, small["odd_conv_w"], alog_row, dtb_row, dqn, dkn, dvs, dbg, "odd_d_prep")
    gs["odd_dt_bias"] = ddt_row[:, DN_HEADS:2 * DN_HEADS]
    gs["odd_a_log"] = dal_row[:, DN_HEADS:2 * DN_HEADS]
    dproj1 = jnp.concatenate([dqkv, dz, dba], axis=-1)
    dwi = mm_at(h1t, dproj1, "odd_dw_in")
    dx2, gs["odd_norm"] = mm_nt_rms_bwd(dproj1, wo["w_in"], x2, small["odd_norm"], dx3, "odd_d_norm")
    tok = grads_out("odd", {"w_in": dwi, "w_out": dwo})

    dx1, dwgu, dwd, gs["ffn_norm0"] = _ffn_bwd(x1, small["ffn_norm0"], w0["gate_up"], w0["down"], ffn0, dx2, "ffn0",
                                               after=tok)
    tok = grads_out("ffn0", {"gate_up": dwgu, "down": dwd})

    dmix = mm_nt(dx1, we["w_out"], "even_d_mix", after=tok)
    dwo = mm_at(mix0_t, dx1, "even_dw_out")
    dqr, dkr, dv, gs["even_sinks"] = swa_bwd(qr, kr, proj0, small["even_sinks"], y_attn, lse, dmix, "even_d_swa")
    dqk, gs["even_q_gain"], gs["even_k_gain"] = qk_prep_bwd(
        proj0, small["even_q_gain"], small["even_k_gain"], cosf, sinf, dqr, dkr, "even_d_qk_prep")
    dgb, dgc, dxi, gs["even_conv_w"] = gconv_bwd(proj0, small["even_conv_w"], dmix, "even_d_gconv")
    dproj0 = jnp.concatenate([dqk, dv, dgb, dgc, dxi], axis=-1)
    dwi = mm_at(h0t, dproj0, "even_dw_in")
    tok = grads_out("even", {"w_in": dwi, "w_out": dwo})
    grad_x, gs["even_norm"] = mm_nt_rms_bwd(dproj0, we["w_in"], x, small["even_norm"], dx1, "even_d_norm", after=tok)
    return loss_row, grad_x, gs


_SMALL_ORDER = ("even_norm", "even_q_gain", "even_k_gain", "even_sinks", "odd_a_log", "odd_dt_bias", "odd_o_gain",
                "ffn_norm0", "ffn_norm1", "odd_norm", "even_conv_w", "odd_conv_w")
_SMALL_SIZE = {"even_norm": 1024, "even_q_gain": 64, "even_k_gain": 64, "even_sinks": 8, "odd_a_log": 8,
               "odd_dt_bias": 8, "odd_o_gain": 128, "ffn_norm0": 1024, "ffn_norm1": 1024, "odd_norm": 1024,
               "even_conv_w": 3 * 512, "odd_conv_w": 4 * 3072}
_N_REPL = 9


def _pack_rows(vals):
    flat = jnp.concatenate([v.reshape(-1) for v in vals])
    pad = (-flat.shape[0]) % 1024
    return jnp.pad(flat, (0, pad)).reshape(-1, 128)


def _my_block(full, size, axis):
    me = 4 * lax.axis_index("x") + 2 * lax.axis_index("y") + lax.axis_index("c")
    return lax.dynamic_slice_in_dim(full, me * size, size, axis=axis)


def _col_gathered(g):
    return g.transpose(2, 0, 1).reshape(g.shape[2], N_DEV * g.shape[1])


def _col_pieces(dw):
    k, n8 = dw.shape
    return dw.reshape(k, N_DEV, n8 // N_DEV).transpose(1, 2, 0)


def kernel(x, even_norm, even_w_in, even_q_gain, even_k_gain, even_sinks, even_conv_w, even_w_out, odd_norm, odd_w_in, odd_conv_w, odd_a_log, odd_dt_bias, odd_o_gain, odd_w_out, ffn_norm, ffn_w_gate_up, ffn_w_down, loss_target, m_even_norm, m_even_w_in, m_even_q_gain, m_even_k_gain, m_even_sinks, m_even_conv_w, m_even_w_out, m_odd_norm, m_odd_w_in, m_odd_conv_w, m_odd_a_log, m_odd_dt_bias, m_odd_o_gain, m_odd_w_out, m_ffn_norm, m_ffn_w_gate_up, m_ffn_w_down, v_even_norm, v_even_w_in, v_even_q_gain, v_even_k_gain, v_even_sinks, v_even_conv_w, v_even_w_out, v_odd_norm, v_odd_w_in, v_odd_conv_w, v_odd_a_log, v_odd_dt_bias, v_odd_o_gain, v_odd_w_out, v_ffn_norm, v_ffn_w_gate_up, v_ffn_w_down):
    t = x.shape[1]
    d = D_MODEL

    me = 4 * lax.axis_index("x") + 2 * lax.axis_index("y") + lax.axis_index("c")
    tr = lambda a: jnp.swapaxes(a, 1, 2)
    shard = {
        "even": {"w_in": tr(even_w_in)[0], "w_out": even_w_out[0]},
        "ffn0": {"gate_up": tr(ffn_w_gate_up)[0], "down": ffn_w_down[0]},
        "odd": {"w_in": tr(odd_w_in)[0], "w_out": odd_w_out[0]},
        "ffn1": {"gate_up": tr(ffn_w_gate_up)[1], "down": ffn_w_down[1]},
    }
    given = {
        ("even", "w_in"): ("even_w_in", even_w_in, m_even_w_in, v_even_w_in, 0),
        ("even", "w_out"): ("even_w_out", even_w_out, m_even_w_out, v_even_w_out, 0),
        ("odd", "w_in"): ("odd_w_in", odd_w_in, m_odd_w_in, v_odd_w_in, 0),
        ("odd", "w_out"): ("odd_w_out", odd_w_out, m_odd_w_out, v_odd_w_out, 0),
        ("ffn0", "gate_up"): ("ffn_w_gate_up", ffn_w_gate_up, m_ffn_w_gate_up, v_ffn_w_gate_up, 0),
        ("ffn1", "gate_up"): ("ffn_w_gate_up", ffn_w_gate_up, m_ffn_w_gate_up, v_ffn_w_gate_up, 1),
        ("ffn0", "down"): ("ffn_w_down", ffn_w_down, m_ffn_w_down, v_ffn_w_down, 0),
        ("ffn1", "down"): ("ffn_w_down", ffn_w_down, m_ffn_w_down, v_ffn_w_down, 1),
    }

    def whole(group, parts):
        col, row = tuple(shard[group])
        w_col = _gu_gathered(parts[0]) if col == "gate_up" else _col_gathered(parts[0])
        if group == "odd":
            w_col = jnp.pad(w_col, ((0, 0), (0, ODD_IN_PAD - ODD_IN_W)))
        return {col: w_col, row: parts[1].reshape(-1, d)}

    wire = {g: [a.astype(MXU_DTYPE) for a in shard[g].values()] for g in shard}
    wire["even_out"] = [wire["even"].pop()]
    wire["even"].append(_pack_rows([odd_norm, even_conv_w, odd_conv_w]))
    gathers, tok = {}, None
    for g in ("even", "even_out", "ffn0", "odd", "ffn1"):
        sems, srcs_thru, lands_thru, tok = send_start(wire[g], f"gather_{g}_start", False, tok)
        gathers[g] = (sems, srcs_thru, lands_thru)
    o1 = d // N_DEV
    o2 = o1 + 3 * CONV_CH // N_DEV

    def weights_of(group, after):
        lands = send_wait(*gathers[group], f"gather_{group}_wait", False, after)
        if group == "even_out":
            return {"w_out": lands[0].reshape(-1, d)}
        if group != "even":
            return whole(group, lands)
        sg = lands[1].reshape(N_DEV, -1)
        return {"w_in": _col_gathered(lands[0]), "small": {
            "odd_norm": sg[:, :o1].reshape(1, d),
            "even_conv_w": sg[:, o1:o2].reshape(N_DEV, 3, CONV_CH // N_DEV).transpose(1, 0, 2).reshape(3, CONV_CH),
            "odd_conv_w": sg[:, o2:o2 + 4 * _QKV_W // N_DEV].reshape(N_DEV, 4, _QKV_W // N_DEV)
            .transpose(1, 0, 2).reshape(4, _QKV_W)}}

    sent = {}

    def grads_out(group, dws):
        col, row = tuple(shard[group])
        n_cols = N_DEV * shard[group][col].shape[0]
        pieces = [_gu_pieces(dws[col]) if col == "gate_up" else _col_pieces(dws[col][:, :n_cols]),
                  dws[row].reshape((N_DEV,) + shard[group][row].shape)]
        sems, srcs_thru, lands_thru, token = send_start(pieces, f"exchange_{group}_start", True, None)
        sent[group] = (sems, srcs_thru, lands_thru, pieces)
        return token

    small = {
        "even_norm": even_norm, "even_q_gain": even_q_gain, "even_k_gain": even_k_gain, "even_sinks": even_sinks,
        "odd_a_log": odd_a_log.reshape(-1), "odd_dt_bias": odd_dt_bias.reshape(-1), "odd_o_gain": odd_o_gain,
        "ffn_norm0": ffn_norm[0:1], "ffn_norm1": ffn_norm[1:2],
    }

    loss_row, grad_x, gs = local_step(x.reshape(t, d), loss_target.reshape(t, d), small, weights_of, grads_out, after=tok)

    rows = _pack_rows([gs[n] for n in _SMALL_ORDER] + [loss_row[:, 0:1]])
    small_sent = send_start([rows], "gather_small_grads_start", False, None)

    res, behind = {}, small_sent[3]
    for g in ("ffn1", "odd", "ffn0", "even"):
        sems, srcs_thru, lands_thru, pieces = sent[g]
        lands = send_wait(sems, srcs_thru, lands_thru, f"exchange_{g}_wait", True, behind)
        for i, (key, pcs) in enumerate(zip(shard[g], lands)):
            name, w_, m_, v_, layer = given[g, key]
            view = tr if i == 0 else (lambda a: a)
            res[name] = adam_sum(view(w_), pcs, view(m_), view(v_), f"adamw_{g}_{key}", layer=layer, into=res.get(name))
        behind = res[name][0]
    for name in ("even_w_in", "odd_w_in", "ffn_w_gate_up"):
        res[name] = tuple(tr(a) for a in res[name])

    (rows_g,) = send_wait(*small_sent[:3], "gather_small_grads_wait", False, behind)
    tot = sum_rows(rows_g, "sum_small_grads").reshape(-1)
    off, sgrad = 0, {}
    for n in _SMALL_ORDER:
        sgrad[n] = tot[off:off + _SMALL_SIZE[n]]
        off += _SMALL_SIZE[n]
    loss = tot[off]

    repl = _SMALL_ORDER[:_N_REPL]
    repl_w = {"even_norm": even_norm, "even_q_gain": even_q_gain, "even_k_gain": even_k_gain, "even_sinks": even_sinks,
              "odd_a_log": odd_a_log, "odd_dt_bias": odd_dt_bias, "odd_o_gain": odd_o_gain,
              "ffn_norm0": ffn_norm[0], "ffn_norm1": ffn_norm[1]}
    repl_m = {"even_norm": m_even_norm, "even_q_gain": m_even_q_gain, "even_k_gain": m_even_k_gain,
              "even_sinks": m_even_sinks, "odd_a_log": m_odd_a_log, "odd_dt_bias": m_odd_dt_bias,
              "odd_o_gain": m_odd_o_gain, "ffn_norm0": m_ffn_norm[0], "ffn_norm1": m_ffn_norm[1]}
    repl_v = {"even_norm": v_even_norm, "even_q_gain": v_even_q_gain, "even_k_gain": v_even_k_gain,
              "even_sinks": v_even_sinks, "odd_a_log": v_odd_a_log, "odd_dt_bias": v_odd_dt_bias,
              "odd_o_gain": v_odd_o_gain, "ffn_norm0": v_ffn_norm[0], "ffn_norm1": v_ffn_norm[1]}
    pk = lambda dct: _pack_rows([dct[n] for n in repl])
    pd_, pm_, pv_ = adam_small(pk(repl_w), pk(sgrad), pk(repl_m), pk(repl_v), "adamw_replicated")
    sres = {}
    off = 0
    for n in repl:
        sz = _SMALL_SIZE[n]
        sres[n] = (sgrad[n], pd_.reshape(-1)[off:off + sz], pm_.reshape(-1)[off:off + sz], pv_.reshape(-1)[off:off + sz])
        off += sz
    g_on = _my_block(sgrad["odd_norm"].reshape(1, d), d // N_DEV, 1)
    g_ec = _my_block(sgrad["even_conv_w"].reshape(3, CONV_CH), CONV_CH // N_DEV, 1)
    g_oc = _my_block(sgrad["odd_conv_w"].reshape(4, _QKV_W), _QKV_W // N_DEV, 1)
    shard_w = _pack_rows([odd_norm, even_conv_w, odd_conv_w])
    sd_, sm_, sv_ = adam_small(shard_w, _pack_rows([g_on, g_ec, g_oc]),
                               _pack_rows([m_odd_norm, m_even_conv_w, m_odd_conv_w]),
                               _pack_rows([v_odd_norm, v_even_conv_w, v_odd_conv_w]), "adamw_sharded_small")
    off = 0
    for n, gfull, like in (("odd_norm", g_on, odd_norm), ("even_conv_w", g_ec, even_conv_w), ("odd_conv_w", g_oc, odd_conv_w)):
        sz = like.size
        sres[n] = (gfull, sd_.reshape(-1)[off:off + sz], sm_.reshape(-1)[off:off + sz], sv_.reshape(-1)[off:off + sz])
        off += sz

    def small_out(name, like, kind):
        if name == "ffn_norm":
            return jnp.stack([sres["ffn_norm0"][kind], sres["ffn_norm1"][kind]]).reshape(like.shape)
        return sres[name][kind].reshape(like.shape)

    order = (("even_norm", even_norm), ("even_w_in", even_w_in), ("even_q_gain", even_q_gain),
             ("even_k_gain", even_k_gain), ("even_sinks", even_sinks), ("even_conv_w", even_conv_w),
             ("even_w_out", even_w_out), ("odd_norm", odd_norm), ("odd_w_in", odd_w_in), ("odd_conv_w", odd_conv_w),
             ("odd_a_log", odd_a_log), ("odd_dt_bias", odd_dt_bias), ("odd_o_gain", odd_o_gain),
             ("odd_w_out", odd_w_out), ("ffn_norm", ffn_norm), ("ffn_w_gate_up", ffn_w_gate_up),
             ("ffn_w_down", ffn_w_down))
    outs = [loss, grad_x.reshape(x.shape)]
    for kind in range(4):
        for name, like in order:
            outs.append(res[name][kind] if name in res else small_out(name, like, kind))
    return tuple(outs)
```

```python
import jax
import jax.numpy as jnp
import numpy as np
from jax import lax
from jax.experimental import pallas as pl
from jax.experimental.pallas import tpu as pltpu

F32 = jnp.float32
MXU_DTYPE = jnp.bfloat16
HI = lax.Precision.HIGH
EPS = 1e-6
N_DEV = 8
D_MODEL = 1024
HEAD_DIM = 64
ATTN_HEADS = 8
KV_HEADS = 2
ATTN_BLOCK = 128
Q_W = 512
KV_W = 128
CONV_CH = 512
EVEN_IN_W = 2304
DN_HEADS = 8
DN_DIM = 128
DN_W = 1024
DN_CHUNK = 64
ODD_IN_W = 4112
ODD_IN_PAD = 4224
D_FF = 2816
NEG = -1e30
VMEM_LIMIT = 56 * 1024 * 1024
ADAM_LR, ADAM_B1, ADAM_B2, ADAM_EPS, ADAM_WD, ADAM_STEP = 0.001, 0.9, 0.999, 1e-08, 0.01, 10
MESH = pl.DeviceIdType.MESH


def _cp(*sem):
    return pltpu.CompilerParams(dimension_semantics=sem, vmem_limit_bytes=VMEM_LIMIT)


def _pick(n, cap):
    best = 128
    for t in range(128, cap + 1, 128):
        if n % t == 0:
            best = t
    return best


def _mx(a, b):
    return jnp.dot(a.astype(MXU_DTYPE), b.astype(MXU_DTYPE), preferred_element_type=F32)


def _mx_nt(a, b):
    return lax.dot_general(a.astype(MXU_DTYPE), b.astype(MXU_DTYPE), (((1,), (1,)), ((), ())),
                           preferred_element_type=F32)


def _mx_tn(a, b):
    return lax.dot_general(a.astype(MXU_DTYPE), b.astype(MXU_DTYPE), (((0,), (0,)), ((), ())),
                           preferred_element_type=F32)


def _hi(a, b):
    return jnp.dot(a, b, precision=HI, preferred_element_type=F32)


def _hi_nt(a, b):
    return lax.dot_general(a, b, (((1,), (1,)), ((), ())), precision=HI, preferred_element_type=F32)


def _hi_tn(a, b):
    return lax.dot_general(a, b, (((0,), (0,)), ((), ())), precision=HI, preferred_element_type=F32)


def _sigmoid(x):
    return 0.5 * jnp.tanh(0.5 * x) + 0.5


def _softplus(x):
    return jnp.maximum(x, 0.0) + jnp.log(1.0 + jnp.exp(-jnp.abs(x)))


def mm_nn_res_norm(a, b, res, g, name, tm=512):
    t, k = a.shape
    d = b.shape[1]
    tm = min(tm, t)

    def body(a_ref, b_ref, res_ref, g_ref, y_ref, h_ref, ht_ref):
        y = res_ref[...] + _mx(a_ref[...], b_ref[...])
        y_ref[...] = y
        h = y * lax.rsqrt(jnp.mean(y * y, axis=-1, keepdims=True) + EPS) * g_ref[...]
        h_ref[...] = h.astype(h_ref.dtype)
        ht_ref[...] = h.T.astype(ht_ref.dtype)

    row = pl.BlockSpec((tm, d), lambda i: (i, 0))
    return pl.pallas_call(
        body, name=name, grid=(t // tm,),
        in_specs=[pl.BlockSpec((tm, k), lambda i: (i, 0)), pl.BlockSpec((k, d), lambda i: (0, 0)), row,
                  pl.BlockSpec((1, d), lambda i: (0, 0))],
        out_specs=(row, row, pl.BlockSpec((d, tm), lambda i: (0, i))),
        out_shape=(jax.ShapeDtypeStruct((t, d), F32), jax.ShapeDtypeStruct((t, d), MXU_DTYPE),
                   jax.ShapeDtypeStruct((d, t), MXU_DTYPE)),
        compiler_params=_cp("parallel"))(a, b, res, g)


def mm_nt(a, b, name, out_dtype=F32, tm=1024, after=None):
    m, k = a.shape
    n, _ = b.shape
    tn = _pick(n, 512 if k > 3000 else 1536)
    tm = min(tm, m)

    def body(a_ref, b_ref, *rest):
        o_ref = rest[-1]
        o_ref[...] = _mx_nt(a_ref[...], b_ref[...]).astype(o_ref.dtype)

    in_specs = [pl.BlockSpec((tm, k), lambda j, i: (i, 0)), pl.BlockSpec((tn, k), lambda j, i: (j, 0))]
    args = [a, b]
    if after is not None:
        in_specs.append(pl.BlockSpec(memory_space=pl.ANY))
        args.append(after)
    return pl.pallas_call(
        body, name=name, grid=(n // tn, m // tm), in_specs=in_specs,
        out_specs=pl.BlockSpec((tm, tn), lambda j, i: (i, j)),
        out_shape=jax.ShapeDtypeStruct((m, n), out_dtype), compiler_params=_cp("parallel", "parallel"))(*args)


def mm_at(at, b, name, tk=1024):
    m, kk = at.shape
    _, n = b.shape
    tm, tn, tk = _pick(m, 1408), _pick(n, 2816), min(tk, kk)
    nk = kk // tk

    def body(a_ref, b_ref, o_ref, acc_ref):
        k = pl.program_id(2)
        p = _mx(a_ref[...], b_ref[...])
        acc = jnp.where(k == 0, p, acc_ref[...] + p)
        acc_ref[...] = acc

        @pl.when(k == nk - 1)
        def _():
            o_ref[...] = acc.astype(o_ref.dtype)

    return pl.pallas_call(
        body, name=name, grid=(m // tm, n // tn, nk),
        in_specs=[pl.BlockSpec((tm, tk), lambda i, j, k: (i, k)), pl.BlockSpec((tk, tn), lambda i, j, k: (k, j))],
        out_specs=pl.BlockSpec((tm, tn), lambda i, j, k: (i, j)),
        out_shape=jax.ShapeDtypeStruct((m, n), MXU_DTYPE), scratch_shapes=[pltpu.VMEM((tm, tn), F32)],
        compiler_params=_cp("parallel", "parallel", "arbitrary"))(at, b)


def rms_fwd(x, g, name, tm=512, after=None):
    t, d = x.shape

    def body(x_ref, g_ref, *rest):
        o_ref, ot_ref = rest[-2:]
        xv = x_ref[...]
        r = lax.rsqrt(jnp.mean(xv * xv, axis=-1, keepdims=True) + EPS)
        h = xv * r * g_ref[...]
        o_ref[...] = h.astype(o_ref.dtype)
        ot_ref[...] = h.T.astype(ot_ref.dtype)

    in_specs = [pl.BlockSpec((tm, d), lambda i: (i, 0)), pl.BlockSpec((1, d), lambda i: (0, 0))]
    args = [x, g]
    if after is not None:
        in_specs.append(pl.BlockSpec(memory_space=pl.ANY))
        args.append(after)
    return pl.pallas_call(
        body, name=name, grid=(t // tm,), in_specs=in_specs,
        out_specs=(pl.BlockSpec((tm, d), lambda i: (i, 0)), pl.BlockSpec((d, tm), lambda i: (0, i))),
        out_shape=(jax.ShapeDtypeStruct((t, d), MXU_DTYPE), jax.ShapeDtypeStruct((d, t), MXU_DTYPE)),
        compiler_params=_cp("parallel"))(*args)


def mm_rms_bwd(a, bt, x, g, dres, name, tm=512, after=None, chunks=None):
    t, k = a.shape
    d = bt.shape[1]
    tm = min(tm, t)
    chunks = chunks or ((0, 0, k),)

    def body(a_ref, b_ref, x_ref, g_ref, dres_ref, *rest):
        dx_ref, dg_ref = rest[-2:]
        dhv = None
        for ca, cb, size in chunks:
            part = _mx(a_ref[:, ca:ca + size], b_ref[cb:cb + size, :])
            dhv = part if dhv is None else dhv + part
        xv = x_ref[...]
        r = lax.rsqrt(jnp.mean(xv * xv, axis=-1, keepdims=True) + EPS)
        xh = xv * r
        dxh = dhv * g_ref[...]
        dx_ref[...] = dres_ref[...] + r * (dxh - xh * jnp.mean(dxh * xh, axis=-1, keepdims=True))
        part = jnp.sum(dhv * xh, axis=0, keepdims=True)
        dg_ref[...] = jnp.where(pl.program_id(0) == 0, part, dg_ref[...] + part)

    row = pl.BlockSpec((tm, d), lambda i: (i, 0))
    one = pl.BlockSpec((1, d), lambda i: (0, 0))
    in_specs = [pl.BlockSpec((tm, k), lambda i: (i, 0)), pl.BlockSpec((k, d), lambda i: (0, 0)), row, one, row]
    args = [a, bt, x, g, dres]
    if after is not None:
        in_specs.append(pl.BlockSpec(memory_space=pl.ANY))
        args.append(after)
    return pl.pallas_call(
        body, name=name, grid=(t // tm,), in_specs=in_specs, out_specs=(row, one),
        out_shape=(jax.ShapeDtypeStruct((t, d), F32), jax.ShapeDtypeStruct((1, d), F32)),
        compiler_params=_cp("arbitrary"))(*args)


GU_TILE = 1408


_GU_PER_TILE = GU_TILE * N_DEV // (2 * D_FF)


def _gu_pieces(dw):
    k, n8 = dw.shape
    nj = N_DEV // (2 * _GU_PER_TILE)
    return dw.reshape(k, nj, 2, _GU_PER_TILE, n8 // N_DEV).transpose(2, 1, 3, 4, 0).reshape(N_DEV, n8 // N_DEV, k)


def ffn_up(f, wt, name, tm=512):
    t, d = f.shape
    nj = D_FF // GU_TILE

    def body(f_ref, wg_ref, wu_ref, gu_ref, a_ref, at_ref):
        g = _mx_nt(f_ref[...], wg_ref[...])
        u = _mx_nt(f_ref[...], wu_ref[...])
        gu_ref[:, :GU_TILE] = g
        gu_ref[:, GU_TILE:] = u
        act = g * _sigmoid(g) * u
        a_ref[...] = act.astype(a_ref.dtype)
        at_ref[...] = act.T.astype(at_ref.dtype)

    return pl.pallas_call(
        body, name=name, grid=(nj, t // tm),
        in_specs=[pl.BlockSpec((tm, d), lambda j, i: (i, 0)), pl.BlockSpec((GU_TILE, d), lambda j, i: (j, 0)),
                  pl.BlockSpec((GU_TILE, d), lambda j, i: (nj + j, 0))],
        out_specs=(pl.BlockSpec((tm, 2 * GU_TILE), lambda j, i: (i, j)), pl.BlockSpec((tm, GU_TILE), lambda j, i: (i, j)),
                   pl.BlockSpec((GU_TILE, tm), lambda j, i: (j, i))),
        out_shape=(jax.ShapeDtypeStruct((t, 2 * D_FF), F32), jax.ShapeDtypeStruct((t, D_FF), MXU_DTYPE),
                   jax.ShapeDtypeStruct((D_FF, t), MXU_DTYPE)),
        compiler_params=_cp("parallel", "parallel"))(f, wt, wt)


_GU_CHUNKS = tuple((q * GU_TILE, ((q % 2) * (D_FF // GU_TILE) + q // 2) * GU_TILE, GU_TILE)
                   for q in range(2 * D_FF // GU_TILE))


def ffn_dact(dy, w_d, gu, name, tm=512, after=None):
    t, d = dy.shape

    def body(dy_ref, w_ref, gu_ref, *rest):
        o_ref = rest[-1]
        da = _mx_nt(dy_ref[...], w_ref[...])
        g, u = gu_ref[:, :GU_TILE], gu_ref[:, GU_TILE:]
        sg = _sigmoid(g)
        o_ref[:, :GU_TILE] = (da * u * sg * (1.0 + g * (1.0 - sg))).astype(o_ref.dtype)
        o_ref[:, GU_TILE:] = (da * g * sg).astype(o_ref.dtype)

    in_specs = [pl.BlockSpec((tm, d), lambda j, i: (i, 0)), pl.BlockSpec((GU_TILE, d), lambda j, i: (j, 0)),
                pl.BlockSpec((tm, 2 * GU_TILE), lambda j, i: (i, j))]
    args = [dy, w_d, gu]
    if after is not None:
        in_specs.append(pl.BlockSpec(memory_space=pl.ANY))
        args.append(after)
    return pl.pallas_call(
        body, name=name, grid=(D_FF // GU_TILE, t // tm), in_specs=in_specs,
        out_specs=pl.BlockSpec((tm, 2 * GU_TILE), lambda j, i: (i, j)),
        out_shape=jax.ShapeDtypeStruct((t, 2 * D_FF), MXU_DTYPE), compiler_params=_cp("parallel", "parallel"))(*args)


def mm_nn_res_loss(a, b, res, target, name, tm=512):
    t, k = a.shape
    d = b.shape[1]
    tm = min(tm, t)

    def body(a_ref, b_ref, res_ref, t_ref, dy_ref, l_ref):
        e = res_ref[...] + _mx(a_ref[...], b_ref[...]) - t_ref[...]
        dy_ref[...] = e * (1.0 / d)
        part = jnp.zeros((1, 128), F32) + 0.5 * jnp.sum(jnp.mean(e * e, axis=-1, keepdims=True), axis=0, keepdims=True)
        l_ref[...] = jnp.where(pl.program_id(0) == 0, part, l_ref[...] + part)

    row = pl.BlockSpec((tm, d), lambda i: (i, 0))
    return pl.pallas_call(
        body, name=name, grid=(t // tm,),
        in_specs=[pl.BlockSpec((tm, k), lambda i: (i, 0)), pl.BlockSpec((k, d), lambda i: (0, 0)), row, row],
        out_specs=(row, pl.BlockSpec((1, 128), lambda i: (0, 0))),
        out_shape=(jax.ShapeDtypeStruct((t, d), F32), jax.ShapeDtypeStruct((1, 128), F32)),
        compiler_params=_cp("arbitrary"))(a, b, res, target)


QK_W = Q_W + KV_W
_QK_TILE = 256


def _qk_mats():
    idx = np.arange(_QK_TILE)
    half = HEAD_DIM // 2
    same = (idx[:, None] // HEAD_DIM) == (idx[None, :] // HEAD_DIM)
    lo = (idx % HEAD_DIM) < half
    rot = np.where((idx[:, None] == idx[None, :] + half) & lo[None, :], -1.0, 0.0)
    rot = rot + np.where((idx[:, None] == idx[None, :] - half) & ~lo[None, :], 1.0, 0.0)
    return jnp.asarray(same, F32), jnp.asarray(rot, F32)


def _qk_rows(q_gain, k_gain, cosf, sinf):
    gain = jnp.concatenate([q_gain] * ATTN_HEADS + [k_gain] * KV_HEADS, axis=-1)
    return gain, jnp.concatenate([cosf, cosf], axis=-1), jnp.concatenate([sinf, sinf], axis=-1)


def _qk_tiles(a, mat, transposed=False):
    outs = []
    for c0 in range(0, QK_W, _QK_TILE):
        w = min(_QK_TILE, QK_W - c0)
        mt = (mat.T if transposed else mat)[:w, :w].astype(MXU_DTYPE)
        at = a[:, c0:c0 + w]
        hi = at.astype(MXU_DTYPE)
        lo = (at - hi.astype(F32)).astype(MXU_DTYPE)
        outs.append(jnp.dot(hi, mt, preferred_element_type=F32) + jnp.dot(lo, mt, preferred_element_type=F32))
    return jnp.concatenate(outs, axis=-1)


def qk_prep_fwd(proj, q_gain, k_gain, cosf, sinf, name, tm=256):
    t = proj.shape[0]
    gmat, rmat = _qk_mats()
    gain, c2, s2 = _qk_rows(q_gain, k_gain, cosf, sinf)
    rep = QK_W // 128

    def body(p_ref, g_ref, c_ref, s_ref, gm_ref, rm_ref, q_ref, k_ref):
        x = p_ref[...]
        r = lax.rsqrt(_qk_tiles(x * x, gm_ref[...]) * (1.0 / HEAD_DIM) + EPS)
        xn = x * r * g_ref[...]
        c = jnp.concatenate([c_ref[...]] * rep, axis=-1)
        s = jnp.concatenate([s_ref[...]] * rep, axis=-1)
        out = xn * c + _qk_tiles(xn, rm_ref[...]) * s
        q_ref[...] = out[:, :Q_W]
        k_ref[...] = out[:, Q_W:]

    full = pl.BlockSpec((_QK_TILE, _QK_TILE), lambda i: (0, 0))
    tab = pl.BlockSpec((tm, 128), lambda i: (i, 0))
    return pl.pallas_call(
        body, name=name, grid=(t // tm,),
        in_specs=[pl.BlockSpec((tm, QK_W), lambda i: (i, 0)), pl.BlockSpec((1, QK_W), lambda i: (0, 0)), tab, tab,
                  full, full],
        out_specs=(pl.BlockSpec((tm, Q_W), lambda i: (i, 0)), pl.BlockSpec((tm, KV_W), lambda i: (i, 0))),
        out_shape=(jax.ShapeDtypeStruct((t, Q_W), F32), jax.ShapeDtypeStruct((t, KV_W), F32)),
        compiler_params=_cp("parallel"))(proj, gain, c2, s2, gmat, rmat)


def qk_prep_bwd(proj, q_gain, k_gain, cosf, sinf, dq, dk, name, tm=256):
    t = proj.shape[0]
    gmat, rmat = _qk_mats()
    gain, c2, s2 = _qk_rows(q_gain, k_gain, cosf, sinf)
    rep = QK_W // 128
    lanes = np.arange(QK_W)[:, None]
    fold = jnp.asarray(lanes % HEAD_DIM + np.where(lanes >= Q_W, HEAD_DIM, 0) == np.arange(128)[None, :], F32)

    def body(p_ref, g_ref, c_ref, s_ref, gm_ref, rm_ref, f_ref, dq_ref, dk_ref, o_ref, dg_ref):
        x = p_ref[...]
        r = lax.rsqrt(_qk_tiles(x * x, gm_ref[...]) * (1.0 / HEAD_DIM) + EPS)
        xh = x * r
        c = jnp.concatenate([c_ref[...]] * rep, axis=-1)
        s = jnp.concatenate([s_ref[...]] * rep, axis=-1)
        dout = jnp.concatenate([dq_ref[...], dk_ref[...]], axis=-1)
        dxn = dout * c + _qk_tiles(dout * s, rm_ref[...], transposed=True)
        part = _hi(jnp.sum(dxn * xh, axis=0, keepdims=True), f_ref[...])
        dxh = dxn * g_ref[...]
        mean = _qk_tiles(dxh * xh, gm_ref[...]) * (1.0 / HEAD_DIM)
        o_ref[...] = (r * (dxh - xh * mean)).astype(o_ref.dtype)
        dg_ref[...] = jnp.where(pl.program_id(0) == 0, part, dg_ref[...] + part)

    full = pl.BlockSpec((_QK_TILE, _QK_TILE), lambda i: (0, 0))
    tab = pl.BlockSpec((tm, 128), lambda i: (i, 0))
    dqk, dg = pl.pallas_call(
        body, name=name, grid=(t // tm,),
        in_specs=[pl.BlockSpec((tm, QK_W), lambda i: (i, 0)), pl.BlockSpec((1, QK_W), lambda i: (0, 0)), tab, tab,
                  full, full, pl.BlockSpec((QK_W, 128), lambda i: (0, 0)),
                  pl.BlockSpec((tm, Q_W), lambda i: (i, 0)), pl.BlockSpec((tm, KV_W), lambda i: (i, 0))],
        out_specs=(pl.BlockSpec((tm, QK_W), lambda i: (i, 0)), pl.BlockSpec((1, 128), lambda i: (0, 0))),
        out_shape=(jax.ShapeDtypeStruct((t, QK_W), MXU_DTYPE), jax.ShapeDtypeStruct((1, 128), F32)),
        compiler_params=_cp("arbitrary"))(proj, gain, c2, s2, gmat, rmat, fold, dq, dk)
    return dqk, dg[:, :HEAD_DIM], dg[:, HEAD_DIM:]


def _swa_valid(n, grp):
    qi = lax.broadcasted_iota(jnp.int32, (grp * ATTN_BLOCK, 2 * ATTN_BLOCK), 0) & (ATTN_BLOCK - 1)
    kj = lax.broadcasted_iota(jnp.int32, (grp * ATTN_BLOCK, 2 * ATTN_BLOCK), 1)
    diff = qi + ATTN_BLOCK - kj
    return (diff >= 0) & (diff < ATTN_BLOCK) & (n * ATTN_BLOCK - ATTN_BLOCK + kj >= 0)


def _stack_heads(ref, g, grp):
    return jnp.concatenate([ref[:, (g * grp + j) * HEAD_DIM:(g * grp + j + 1) * HEAD_DIM] for j in range(grp)], axis=0)


def _stack_sinks(s_ref, g, grp):
    return jnp.concatenate([jnp.zeros((ATTN_BLOCK, 1), F32) + s_ref[0:1, g * grp + j:g * grp + j + 1]
                            for j in range(grp)], axis=0)


def swa_fwd(q, k, proj, sinks, name):
    t = q.shape[0]
    nb = t // ATTN_BLOCK
    scale = HEAD_DIM ** -0.5
    grp = ATTN_HEADS // KV_HEADS

    def body(q_ref, kc_ref, kp_ref, vc_ref, vp_ref, s_ref, y_ref, mix_ref, yt_ref, lse_ref):
        n = pl.program_id(0)
        valid = _swa_valid(n, grp)
        kk = jnp.concatenate([kp_ref[...], kc_ref[...]], axis=0).astype(MXU_DTYPE)
        vv = jnp.concatenate([vp_ref[...], vc_ref[...]], axis=0).astype(MXU_DTYPE)
        lane = lax.broadcasted_iota(jnp.int32, (ATTN_BLOCK, ATTN_HEADS), 1)
        gs = range(KV_HEADS)
        qg = [_stack_heads(q_ref, g, grp) for g in gs]
        sink = [_stack_sinks(s_ref, g, grp) for g in gs]
        sc = [jnp.where(valid, _mx_nt(qg[g], kk[:, g * HEAD_DIM:(g + 1) * HEAD_DIM]) * scale, NEG) for g in gs]
        m = [jnp.maximum(jnp.max(sc[g], axis=-1, keepdims=True), sink[g]) for g in gs]
        e = [jnp.exp(sc[g] - m[g]) for g in gs]
        den = [jnp.sum(e[g], axis=-1, keepdims=True) + jnp.exp(sink[g] - m[g]) for g in gs]
        og = [_mx(e[g] / den[g], vv[:, g * HEAD_DIM:(g + 1) * HEAD_DIM]) for g in gs]
        lg = [m[g] + jnp.log(den[g]) for g in gs]
        lse = jnp.zeros((ATTN_BLOCK, ATTN_HEADS), F32)
        outs = []
        for h in range(ATTN_HEADS):
            rows = slice((h % grp) * ATTN_BLOCK, (h % grp + 1) * ATTN_BLOCK)
            outs.append(og[h // grp][rows])
            lse = jnp.where(lane == h, lg[h // grp][rows], lse)
        y = jnp.concatenate(outs, axis=-1)
        y_ref[...] = y
        mix_ref[...] = y.astype(mix_ref.dtype)
        yt_ref[...] = y.T.astype(yt_ref.dtype)
        lse_ref[...] = lse

    cur = lambda n: (n, 0)
    prev = lambda n: (jnp.maximum(n - 1, 0), 0)
    vcol = (Q_W + KV_W) // KV_W
    return pl.pallas_call(
        body, name=name, grid=(nb,),
        in_specs=[pl.BlockSpec((ATTN_BLOCK, Q_W), cur), pl.BlockSpec((ATTN_BLOCK, KV_W), cur),
                  pl.BlockSpec((ATTN_BLOCK, KV_W), prev),
                  pl.BlockSpec((ATTN_BLOCK, KV_W), lambda n: (n, vcol)),
                  pl.BlockSpec((ATTN_BLOCK, KV_W), lambda n: (jnp.maximum(n - 1, 0), vcol)),
                  pl.BlockSpec((1, ATTN_HEADS), lambda n: (0, 0))],
        out_specs=(pl.BlockSpec((ATTN_BLOCK, Q_W), cur), pl.BlockSpec((ATTN_BLOCK, Q_W), cur),
                   pl.BlockSpec((Q_W, ATTN_BLOCK), lambda n: (0, n)), pl.BlockSpec((ATTN_BLOCK, ATTN_HEADS), cur)),
        out_shape=(jax.ShapeDtypeStruct((t, Q_W), F32), jax.ShapeDtypeStruct((t, Q_W + CONV_CH), MXU_DTYPE),
                   jax.ShapeDtypeStruct((Q_W + CONV_CH, t), MXU_DTYPE), jax.ShapeDtypeStruct((t, ATTN_HEADS), F32)),
        compiler_params=_cp("parallel"))(q, k, k, proj, proj, sinks)


def swa_bwd(q, k, proj, sinks, y, lse, dmix, name):
    t = q.shape[0]
    nb = t // ATTN_BLOCK
    scale = HEAD_DIM ** -0.5
    grp = ATTN_HEADS // KV_HEADS

    def body(q_ref, kc_ref, kp_ref, vc_ref, vp_ref, s_ref, y_ref, lse_ref, dy_ref,
             dq_ref, dk_ref, dv_ref, ds_ref, dkc, dvc):
        n = pl.program_id(0)

        @pl.when(n == 0)
        def _():
            dkc[...] = jnp.zeros_like(dkc)
            dvc[...] = jnp.zeros_like(dvc)
            ds_ref[...] = jnp.zeros_like(ds_ref)

        @pl.when(n < nb)
        def _():
            valid = _swa_valid(n, grp)
            kk = jnp.concatenate([kp_ref[...], kc_ref[...]], axis=0).astype(MXU_DTYPE)
            vv = jnp.concatenate([vp_ref[...], vc_ref[...]], axis=0).astype(MXU_DTYPE)
            lane = lax.broadcasted_iota(jnp.int32, (1, ATTN_HEADS), 1)
            gs = range(KV_HEADS)
            kg = [kk[:, g * HEAD_DIM:(g + 1) * HEAD_DIM] for g in gs]
            vg = [vv[:, g * HEAD_DIM:(g + 1) * HEAD_DIM] for g in gs]
            qg = [_stack_heads(q_ref, g, grp).astype(MXU_DTYPE) for g in gs]
            dog = [_stack_heads(dy_ref, g, grp) for g in gs]
            og = [_stack_heads(y_ref, g, grp) for g in gs]
            lg = [jnp.concatenate([lse_ref[:, g * grp + j:g * grp + j + 1] for j in range(grp)], axis=0) for g in gs]
            sink = [_stack_sinks(s_ref, g, grp) for g in gs]
            sc = [jnp.where(valid, _mx_nt(qg[g], kg[g]) * scale, NEG) for g in gs]
            p = [jnp.exp(sc[g] - lg[g]) for g in gs]
            delta = [jnp.sum(dog[g] * og[g], axis=-1, keepdims=True) for g in gs]
            ds = [p[g] * (_mx_nt(dog[g], vg[g]) - delta[g]) for g in gs]
            dqg = [_mx(ds[g], kg[g]) * scale for g in gs]
            dkf = jnp.concatenate([_mx_tn(ds[g], qg[g]) * scale for g in gs], axis=-1)
            dvf = jnp.concatenate([_mx_tn(p[g], dog[g]) for g in gs], axis=-1)
            dsk = [jnp.exp(sink[g] - lg[g]) * delta[g] for g in gs]
            dsink = jnp.zeros((1, ATTN_HEADS), F32)
            dqs = []
            for h in range(ATTN_HEADS):
                rows = slice((h % grp) * ATTN_BLOCK, (h % grp + 1) * ATTN_BLOCK)
                dqs.append(dqg[h // grp][rows])
                dsink = jnp.where(lane == h, -jnp.sum(dsk[h // grp][rows], axis=0, keepdims=True), dsink)
            dq_ref[...] = jnp.concatenate(dqs, axis=-1)
            dk_ref[...] = dkc[...] + dkf[:ATTN_BLOCK]
            dv_ref[...] = (dvc[...] + dvf[:ATTN_BLOCK]).astype(dv_ref.dtype)
            dkc[...] = dkf[ATTN_BLOCK:]
            dvc[...] = dvf[ATTN_BLOCK:]
            ds_ref[...] += dsink

        @pl.when(n == nb)
        def _():
            dk_ref[...] = dkc[...]
            dv_ref[...] = dvc[...].astype(dv_ref.dtype)

    cur = lambda n: (jnp.minimum(n, nb - 1), 0)
    prev = lambda n: (jnp.clip(n - 1, 0, nb - 1), 0)
    vcol = (Q_W + KV_W) // KV_W
    return pl.pallas_call(
        body, name=name, grid=(nb + 1,),
        in_specs=[pl.BlockSpec((ATTN_BLOCK, Q_W), cur), pl.BlockSpec((ATTN_BLOCK, KV_W), cur),
                  pl.BlockSpec((ATTN_BLOCK, KV_W), prev),
                  pl.BlockSpec((ATTN_BLOCK, KV_W), lambda n: (jnp.minimum(n, nb - 1), vcol)),
                  pl.BlockSpec((ATTN_BLOCK, KV_W), lambda n: (jnp.clip(n - 1, 0, nb - 1), vcol)),
                  pl.BlockSpec((1, ATTN_HEADS), lambda n: (0, 0)),
                  pl.BlockSpec((ATTN_BLOCK, Q_W), cur), pl.BlockSpec((ATTN_BLOCK, ATTN_HEADS), cur),
                  pl.BlockSpec((ATTN_BLOCK, Q_W), cur)],
        out_specs=(pl.BlockSpec((ATTN_BLOCK, Q_W), cur), pl.BlockSpec((ATTN_BLOCK, KV_W), prev),
                   pl.BlockSpec((ATTN_BLOCK, KV_W), prev), pl.BlockSpec((1, ATTN_HEADS), lambda n: (0, 0))),
        out_shape=(jax.ShapeDtypeStruct((t, Q_W), F32), jax.ShapeDtypeStruct((t, KV_W), F32),
                   jax.ShapeDtypeStruct((t, KV_W), MXU_DTYPE), jax.ShapeDtypeStruct((1, ATTN_HEADS), F32)),
        scratch_shapes=[pltpu.VMEM((ATTN_BLOCK, KV_W), F32), pltpu.VMEM((ATTN_BLOCK, KV_W), F32)],
        compiler_params=_cp("arbitrary"))(q, k, k, proj, proj, sinks, y, lse, dmix)


GC_W = 256
_GB0, _GC0, _XI0 = 768 // GC_W, 1280 // GC_W, 1792 // GC_W
HALO = 8


def gconv_fwd(proj, conv_w, mix, mix_t, name, tm=512):
    t = proj.shape[0]
    hb = tm // HALO
    half = Q_W // GC_W

    def body(gb_ref, gc_ref, xi_ref, gch_ref, xih_ref, w_ref, mix_in, mixt_in, y_ref, yt_ref):
        i = pl.program_id(1)
        u = gc_ref[...] * xi_ref[...]
        uh = jnp.where(i == 0, 0.0, gch_ref[...] * xih_ref[...])
        up = jnp.concatenate([uh, u], axis=0)
        cv = w_ref[0:1, :] * up[HALO - 2:HALO - 2 + tm]
        cv = cv + w_ref[1:2, :] * up[HALO - 1:HALO - 1 + tm]
        cv = cv + w_ref[2:3, :] * u
        y = gb_ref[...] * cv
        y_ref[...] = y.astype(y_ref.dtype)
        yt_ref[...] = y.T.astype(yt_ref.dtype)

    def col(c0):
        return pl.BlockSpec((tm, GC_W), lambda cj, i: (i, c0 + cj))

    def halo(c0):
        return pl.BlockSpec((HALO, GC_W), lambda cj, i: (jnp.maximum(i * hb - 1, 0), c0 + cj))

    return pl.pallas_call(
        body, name=name, grid=(CONV_CH // GC_W, t // tm),
        in_specs=[col(_GB0), col(_GC0), col(_XI0), halo(_GC0), halo(_XI0),
                  pl.BlockSpec((3, GC_W), lambda cj, i: (0, cj)),
                  pl.BlockSpec(memory_space=pl.ANY), pl.BlockSpec(memory_space=pl.ANY)],
        out_specs=(pl.BlockSpec((tm, GC_W), lambda cj, i: (i, half + cj)),
                   pl.BlockSpec((GC_W, tm), lambda cj, i: (half + cj, i))),
        out_shape=(jax.ShapeDtypeStruct(mix.shape, mix.dtype), jax.ShapeDtypeStruct(mix_t.shape, mix_t.dtype)),
        input_output_aliases={6: 0, 7: 1},
        compiler_params=_cp("parallel", "parallel"))(proj, proj, proj, proj, proj, conv_w, mix, mix_t)


def gconv_bwd(proj, conv_w, dmix, name, tm=512):
    t = proj.shape[0]
    hb = tm // HALO
    nt = t // tm
    dy0 = Q_W // GC_W

    def body(gb_ref, gc_ref, xi_ref, gch_ref, xih_ref, gbn_ref, dyn_ref, dy_ref, w_ref,
             dgb_ref, dgc_ref, dxi_ref, dw_ref):
        i = pl.program_id(1)
        gc, xi, gb, dy = gc_ref[...], xi_ref[...], gb_ref[...], dy_ref[...]
        u = gc * xi
        uh = jnp.where(i == 0, 0.0, gch_ref[...] * xih_ref[...])
        up = jnp.concatenate([uh, u], axis=0)
        u2 = up[HALO - 2:HALO - 2 + tm]
        u1 = up[HALO - 1:HALO - 1 + tm]
        cv = w_ref[0:1, :] * u2 + w_ref[1:2, :] * u1 + w_ref[2:3, :] * u
        dgb_ref[...] = (dy * cv).astype(dgb_ref.dtype)
        dcv = dy * gb
        dcvn = jnp.where(i == nt - 1, 0.0, dyn_ref[...] * gbn_ref[...])
        dcvp = jnp.concatenate([dcv, dcvn], axis=0)
        du = w_ref[0:1, :] * dcvp[2:2 + tm] + w_ref[1:2, :] * dcvp[1:1 + tm] + w_ref[2:3, :] * dcv
        dgc_ref[...] = (du * xi).astype(dgc_ref.dtype)
        dxi_ref[...] = (du * gc).astype(dxi_ref.dtype)
        dw = jnp.concatenate([jnp.sum(dcv * u2, axis=0, keepdims=True), jnp.sum(dcv * u1, axis=0, keepdims=True),
                              jnp.sum(dcv * u, axis=0, keepdims=True)], axis=0)

        @pl.when(i == 0)
        def _():
            dw_ref[...] = dw

        @pl.when(i > 0)
        def _():
            dw_ref[...] += dw

    def col(c0):
        return pl.BlockSpec((tm, GC_W), lambda cj, i: (i, c0 + cj))

    def halo(c0):
        return pl.BlockSpec((HALO, GC_W), lambda cj, i: (jnp.maximum(i * hb - 1, 0), c0 + cj))

    def nxt(c0):
        return pl.BlockSpec((HALO, GC_W), lambda cj, i: (jnp.minimum((i + 1) * hb, t // HALO - 1), c0 + cj))

    out = pl.BlockSpec((tm, GC_W), lambda cj, i: (i, cj))
    return pl.pallas_call(
        body, name=name, grid=(CONV_CH // GC_W, nt),
        in_specs=[col(_GB0), col(_GC0), col(_XI0), halo(_GC0), halo(_XI0), nxt(_GB0), nxt(dy0), col(dy0),
                  pl.BlockSpec((3, GC_W), lambda cj, i: (0, cj))],
        out_specs=(out, out, out, pl.BlockSpec((3, GC_W), lambda cj, i: (0, cj))),
        out_shape=(jax.ShapeDtypeStruct((t, CONV_CH), MXU_DTYPE),) * 3 + (jax.ShapeDtypeStruct((3, CONV_CH), F32),),
        compiler_params=_cp("parallel", "arbitrary"))(proj, proj, proj, proj, proj, proj, dmix, dmix, conv_w)


_QKV_W = 3 * DN_W
_BA_COL = (4 * DN_W) // 128
_Z_COL = _QKV_W // DN_W


def gdn_prep_fwd(proj, conv_w, alog_row, dtb_row, name, tm=256):
    t = proj.shape[0]
    hb = tm // HALO
    qscale = DN_DIM ** -0.5

    def body(x_ref, xh_ref, w_ref, ba_ref, al_ref, dt_ref, q_ref, k_ref, v_ref, bg_ref):
        i = pl.program_id(0)
        for gi in range(3 * DN_HEADS):
            sl = slice(gi * DN_DIM, (gi + 1) * DN_DIM)
            xp = jnp.concatenate([jnp.where(i == 0, 0.0, xh_ref[:, sl]), x_ref[:, sl]], axis=0)
            c = w_ref[0:1, sl] * xp[HALO - 3:HALO - 3 + tm]
            for j in range(1, 4):
                c = c + w_ref[j:j + 1, sl] * xp[HALO - 3 + j:HALO - 3 + j + tm]
            s = c * _sigmoid(c)
            osl = slice((gi % DN_HEADS) * DN_DIM, (gi % DN_HEADS + 1) * DN_DIM)
            if gi < DN_HEADS:
                q_ref[:, osl] = s * lax.rsqrt(jnp.sum(s * s, axis=-1, keepdims=True) + EPS) * qscale
            elif gi < 2 * DN_HEADS:
                k_ref[:, osl] = s * lax.rsqrt(jnp.sum(s * s, axis=-1, keepdims=True) + EPS)
            else:
                v_ref[:, osl] = s
        ba = ba_ref[...]
        lane = lax.broadcasted_iota(jnp.int32, ba.shape, 1)
        gval = -jnp.exp(al_ref[...]) * _softplus(ba + dt_ref[...])
        bg_ref[...] = jnp.where(lane < DN_HEADS, _sigmoid(ba), jnp.where(lane < 2 * DN_HEADS, gval, 0.0))

    row = pl.BlockSpec((tm, DN_W), lambda i: (i, 0))
    one = pl.BlockSpec((1, 128), lambda i: (0, 0))
    return pl.pallas_call(
        body, name=name, grid=(t // tm,),
        in_specs=[pl.BlockSpec((tm, _QKV_W), lambda i: (i, 0)),
                  pl.BlockSpec((HALO, _QKV_W), lambda i: (jnp.maximum(i * hb - 1, 0), 0)),
                  pl.BlockSpec((4, _QKV_W), lambda i: (0, 0)),
                  pl.BlockSpec((tm, 128), lambda i: (i, _BA_COL)), one, one],
        out_specs=(row, row, row, pl.BlockSpec((tm, 128), lambda i: (i, 0))),
        out_shape=(jax.ShapeDtypeStruct((t, DN_W), F32),) * 3 + (jax.ShapeDtypeStruct((t, 128), F32),),
        compiler_params=_cp("parallel"))(proj, proj, conv_w, proj, alog_row, dtb_row)


def gdn_prep_bwd(proj, conv_w, alog_row, dtb_row, dq, dk, dv, dbg, name, tm=256):
    t = proj.shape[0]
    hb = tm // HALO
    nt = t // tm
    qscale = DN_DIM ** -0.5
    te = tm + HALO

    def body(x_ref, xh_ref, xn_ref, w_ref, ba_ref, al_ref, dt_ref, dq_ref, dk_ref, dv_ref,
             dqn_ref, dkn_ref, dvn_ref, dbg_ref, dx_ref, dba_ref, dw_ref, ddt_ref, dal_ref):
        i = pl.program_id(0)
        first = i == 0
        last = i == nt - 1
        dws = []
        for gi in range(3 * DN_HEADS):
            sl = slice(gi * DN_DIM, (gi + 1) * DN_DIM)
            osl = slice((gi % DN_HEADS) * DN_DIM, (gi % DN_HEADS + 1) * DN_DIM)
            xe = jnp.concatenate([jnp.where(first, 0.0, xh_ref[:, sl]), x_ref[:, sl], xn_ref[:, sl]], axis=0)
            c = w_ref[0:1, sl] * xe[HALO - 3:HALO - 3 + te]
            for j in range(1, 4):
                c = c + w_ref[j:j + 1, sl] * xe[HALO - 3 + j:HALO - 3 + j + te]
            sg = _sigmoid(c)
            s = c * sg
            d_ref, dn_ref = ((dq_ref, dqn_ref), (dk_ref, dkn_ref), (dv_ref, dvn_ref))[gi // DN_HEADS]
            dy = jnp.concatenate([d_ref[:, osl], jnp.where(last, 0.0, dn_ref[:, osl])], axis=0)
            if gi < 2 * DN_HEADS:
                r = lax.rsqrt(jnp.sum(s * s, axis=-1, keepdims=True) + EPS)
                sh = s * r
                ds = r * (dy - sh * jnp.sum(sh * dy, axis=-1, keepdims=True))
                if gi < DN_HEADS:
                    ds = ds * qscale
            else:
                ds = dy
            dc = ds * sg * (1.0 + c * (1.0 - sg))
            dcs = [dc[3 - j:3 - j + tm] for j in range(4)]
            dx = w_ref[0:1, sl] * dcs[0]
            for j in range(1, 4):
                dx = dx + w_ref[j:j + 1, sl] * dcs[j]
            dx_ref[:, sl] = dx.astype(dx_ref.dtype)
            x0 = x_ref[:, sl]
            dws.append(jnp.concatenate([jnp.sum(dcs[j] * x0, axis=0, keepdims=True) for j in range(4)], axis=0))
        dw = jnp.concatenate(dws, axis=-1)
        ba = ba_ref[...]
        dbgv = dbg_ref[...]
        lane = lax.broadcasted_iota(jnp.int32, ba.shape, 1)
        beta = _sigmoid(ba)
        ea = -jnp.exp(al_ref[...])
        zin = ba + dt_ref[...]
        is_b = lane < DN_HEADS
        is_a = (lane >= DN_HEADS) & (lane < 2 * DN_HEADS)
        da = jnp.where(is_a, dbgv * ea * _sigmoid(zin), 0.0)
        dba_ref[...] = jnp.where(is_b, dbgv * beta * (1.0 - beta), da).astype(dba_ref.dtype)
        ddt = jnp.sum(da, axis=0, keepdims=True)
        dal = jnp.sum(jnp.where(is_a, dbgv * ea * _softplus(zin), 0.0), axis=0, keepdims=True)

        @pl.when(first)
        def _():
            dw_ref[...] = dw
            ddt_ref[...] = ddt
            dal_ref[...] = dal

        @pl.when(i > 0)
        def _():
            dw_ref[...] += dw
            ddt_ref[...] += ddt
            dal_ref[...] += dal

    row = pl.BlockSpec((tm, DN_W), lambda i: (i, 0))
    nrow = pl.BlockSpec((HALO, DN_W), lambda i: (jnp.minimum((i + 1) * hb, t // HALO - 1), 0))
    one = pl.BlockSpec((1, 128), lambda i: (0, 0))
    return pl.pallas_call(
        body, name=name, grid=(nt,),
        in_specs=[pl.BlockSpec((tm, _QKV_W), lambda i: (i, 0)),
                  pl.BlockSpec((HALO, _QKV_W), lambda i: (jnp.maximum(i * hb - 1, 0), 0)),
                  pl.BlockSpec((HALO, _QKV_W), lambda i: (jnp.minimum((i + 1) * hb, t // HALO - 1), 0)),
                  pl.BlockSpec((4, _QKV_W), lambda i: (0, 0)),
                  pl.BlockSpec((tm, 128), lambda i: (i, _BA_COL)), one, one,
                  row, row, row, nrow, nrow, nrow, pl.BlockSpec((tm, 128), lambda i: (i, 0))],
        out_specs=(pl.BlockSpec((tm, _QKV_W), lambda i: (i, 0)), pl.BlockSpec((tm, 128), lambda i: (i, 0)),
                   pl.BlockSpec((4, _QKV_W), lambda i: (0, 0)), one, one),
        out_shape=(jax.ShapeDtypeStruct((t, _QKV_W), MXU_DTYPE), jax.ShapeDtypeStruct((t, 128), MXU_DTYPE),
                   jax.ShapeDtypeStruct((4, _QKV_W), F32), jax.ShapeDtypeStruct((1, 128), F32),
                   jax.ShapeDtypeStruct((1, 128), F32)),
        compiler_params=_cp("arbitrary"))(proj, proj, proj, conv_w, proj, alog_row, dtb_row, dq, dk, dv, dq, dk, dv, dbg)


def _chunk_masks():
    r = lax.broadcasted_iota(jnp.int32, (DN_CHUNK, DN_CHUNK), 0)
    c = lax.broadcasted_iota(jnp.int32, (DN_CHUNK, DN_CHUNK), 1)
    return r >= c, r > c


INV_PACK = 2


def _inv_unit_lower_many(mats):
    n = DN_CHUNK
    wide = INV_PACK * n
    r = lax.broadcasted_iota(jnp.int32, (wide, wide), 0)
    c = lax.broadcasted_iota(jnp.int32, (wide, wide), 1)
    same = (r & -n) == (c & -n)
    eye = jnp.where((r[:n] == (c[:n] & (n - 1))), 1.0, 0.0)

    def blockdiag(row):
        return jnp.where(same, jnp.concatenate([row] * INV_PACK, axis=0), 0.0)

    packs = [jnp.concatenate(mats[g:g + INV_PACK], axis=-1) for g in range(0, len(mats), INV_PACK)]
    xs = [eye - a for a in packs]
    pws = [_hi(a, blockdiag(a)) for a in packs]
    for step in range(5):
        if step < 4:
            both = [_hi(jnp.concatenate([x, pw], axis=0), blockdiag(pw)) for x, pw in zip(xs, pws)]
            xs = [x + b[:n] for x, b in zip(xs, both)]
            pws = [b[n:] for b in both]
        else:
            xs = [x + _hi(x, blockdiag(pw)) for x, pw in zip(xs, pws)]
    return [x[:, j * n:(j + 1) * n] for x in xs for j in range(INV_PACK)]


def _chunk_common(q, k, v, beta, gc, gcr, lower, strict):
    gam = jnp.exp(jnp.where(lower, gc - gcr, NEG))
    eg = jnp.exp(gc)
    gl = gc[DN_CHUNK - 1:DN_CHUNK, :]
    kdf = jnp.exp(gl - gc)
    kb = k * beta
    bmat = _mx_nt(kb, k)
    qmat = _mx_nt(q, k)
    return gam, eg, jnp.exp(gl), kdf, kb, bmat, qmat


DN_STEP = 4


def gdn_fwd(q, k, v, bg, name):
    t = q.shape[0]
    n_chunks = t // DN_CHUNK

    def body(q_ref, k_ref, v_ref, bg_ref, o_ref, sall_ref, tall_ref, s_ref):
        n = pl.program_id(0)

        @pl.when(n == 0)
        def _():
            s_ref[...] = jnp.zeros_like(s_ref)

        lower, strict = _chunk_masks()
        ltri = jnp.where(lower, 1.0, 0.0)
        hs = range(DN_HEADS)
        sl = [slice(h * DN_DIM, (h + 1) * DN_DIM) for h in hs]
        units = [(c, h) for c in range(DN_STEP) for h in hs]
        nu = range(len(units))
        rs = [slice(c * DN_CHUNK, (c + 1) * DN_CHUNK) for c in range(DN_STEP)]
        bgv = [bg_ref[rs[c], :] for c in range(DN_STEP)]
        gcs = [_hi(ltri, b) for b in bgv]
        gcs_t = [g.T for g in gcs]
        qh = [q_ref[rs[c], sl[h]] for c, h in units]
        kh = [k_ref[rs[c], sl[h]] for c, h in units]
        vh = [v_ref[rs[c], sl[h]] for c, h in units]
        beta = [bgv[c][:, h:h + 1] for c, h in units]
        com = [_chunk_common(qh[u], kh[u], vh[u], beta[u], gcs[c][:, DN_HEADS + h:DN_HEADS + h + 1],
                             gcs_t[c][DN_HEADS + h:DN_HEADS + h + 1, :], lower, strict) for u, (c, h) in enumerate(units)]
        gam, eg, dec, kdf, kb, bmat, qmat = zip(*com)
        tms = _inv_unit_lower_many([jnp.where(strict, bmat[u] * gam[u], 0.0) for u in nu])
        for u, (c, h) in enumerate(units):
            tall_ref[c, h] = tms[u]
        uw = [_hi(tms[u], jnp.concatenate([vh[u] * beta[u], kb[u] * eg[u]], axis=-1)) for u in nu]
        qd = [qh[u] * eg[u] for u in nu]
        pm = [qmat[u] * gam[u] for u in nu]
        kd = [kh[u] * kdf[u] for u in nu]
        st = [s_ref[h] for h in hs]
        for c in range(DN_STEP):
            us = [c * DN_HEADS + h for h in hs]
            for h in hs:
                sall_ref[c, h] = st[h]
            v_new = [uw[us[h]][:, :DN_DIM] - _mx(uw[us[h]][:, DN_DIM:], st[h]) for h in hs]
            o_st = [_mx(qd[us[h]], st[h]) for h in hs]
            o_in = [_mx(pm[us[h]], v_new[h]) for h in hs]
            s_up = [_mx_tn(kd[us[h]], v_new[h]) for h in hs]
            for h in hs:
                o_ref[rs[c], sl[h]] = o_st[h] + o_in[h]
            st = [st[h] * dec[us[h]] + s_up[h] for h in hs]
        for h in hs:
            s_ref[h] = st[h]

    rows = DN_STEP * DN_CHUNK
    row = pl.BlockSpec((rows, DN_W), lambda n: (n, 0))
    return pl.pallas_call(
        body, name=name, grid=(n_chunks // DN_STEP,),
        in_specs=[row, row, row, pl.BlockSpec((rows, 128), lambda n: (n, 0))],
        out_specs=(row, pl.BlockSpec((DN_STEP, DN_HEADS, DN_DIM, DN_DIM), lambda n: (n, 0, 0, 0)),
                   pl.BlockSpec((DN_STEP, DN_HEADS, DN_CHUNK, DN_CHUNK), lambda n: (n, 0, 0, 0))),
        out_shape=(jax.ShapeDtypeStruct((t, DN_W), F32),
                   jax.ShapeDtypeStruct((n_chunks, DN_HEADS, DN_DIM, DN_DIM), F32),
                   jax.ShapeDtypeStruct((n_chunks, DN_HEADS, DN_CHUNK, DN_CHUNK), F32)),
        scratch_shapes=[pltpu.VMEM((DN_HEADS, DN_DIM, DN_DIM), F32)],
        compiler_params=_cp("arbitrary"))(q, k, v, bg)


def gdn_bwd(q, k, v, bg, sall, tall, do, name):
    t = q.shape[0]
    n_chunks = t // DN_CHUNK

    def body(q_ref, k_ref, v_ref, bg_ref, sall_ref, tall_ref, do_ref, dq_ref, dk_ref, dv_ref, dbg_ref, ds_ref):
        n = pl.program_id(0)

        @pl.when(n == 0)
        def _():
            ds_ref[...] = jnp.zeros_like(ds_ref)

        lower, strict = _chunk_masks()
        ltri = jnp.where(lower, 1.0, 0.0)
        bgv = bg_ref[...]
        gcs = _hi(ltri, bgv)
        gcs_t = gcs.T
        lane = lax.broadcasted_iota(jnp.int32, (DN_CHUNK, 128), 1)
        rowi = lax.broadcasted_iota(jnp.int32, (DN_CHUNK, 1), 0)
        hs = range(DN_HEADS)
        each = lambda fn, *ls: [fn(*a) for a in zip(*ls)]
        rsum = lambda a: jnp.sum(a, axis=-1, keepdims=True)
        sl = [slice(h * DN_DIM, (h + 1) * DN_DIM) for h in hs]
        st = [sall_ref[0, h] for h in hs]
        tms = [tall_ref[0, h] for h in hs]
        dsn = [ds_ref[h] for h in hs]
        qh = [q_ref[:, sl[h]] for h in hs]
        kh = [k_ref[:, sl[h]] for h in hs]
        vh = [v_ref[:, sl[h]] for h in hs]
        doh = [do_ref[:, sl[h]] for h in hs]
        beta = [bgv[:, h:h + 1] for h in hs]
        com = [_chunk_common(qh[h], kh[h], vh[h], beta[h], gcs[:, DN_HEADS + h:DN_HEADS + h + 1],
                             gcs_t[DN_HEADS + h:DN_HEADS + h + 1, :], lower, strict) for h in hs]
        gam, eg, dec, kdf, kb, bmat, qmat = zip(*com)
        rhs_w = each(lambda a, b: a * b, kb, eg)
        uw = each(lambda t_, v_, b_, r_: _hi(t_, jnp.concatenate([v_ * b_, r_], axis=-1)), tms, vh, beta, rhs_w)
        qd = each(lambda a, b: a * b, qh, eg)
        kd = each(lambda a, b: a * b, kh, kdf)
        pmat = each(lambda a, b: a * b, qmat, gam)
        v_new = each(lambda uw_, s_: uw_[:, :DN_DIM] - _mx(uw_[:, DN_DIM:], s_), uw, st)
        dqd = each(_mx_nt, doh, st)
        ds_o = each(_mx_tn, qd, doh)
        dp = each(lambda d_, v_: jnp.where(lower, _mx_nt(d_, v_), 0.0), doh, v_new)
        dvn_o = each(_mx_tn, pmat, doh)
        ddec = each(lambda d_, s_: jnp.sum(rsum(d_ * s_), axis=0, keepdims=True), dsn, st)
        dkd = each(_mx_nt, v_new, dsn)
        dvn = each(lambda a, k_, d_: a + _mx(k_, d_), dvn_o, kd, dsn)
        dw = each(lambda d_, s_: -_mx_nt(d_, s_), dvn, st)
        ds_w = each(lambda uw_, d_: _mx_tn(uw_[:, DN_DIM:], d_), uw, dvn)
        for h in hs:
            ds_ref[h] = ds_o[h] + dec[h] * dsn[h] - ds_w[h]
        dr = each(lambda t_, a, b: _hi_tn(t_, jnp.concatenate([a, b], axis=-1)), tms, dvn, dw)
        da = each(lambda r_, uw_: jnp.where(strict, -_hi_nt(r_, uw_), 0.0), dr, uw)
        dru = [r_[:, :DN_DIM] for r_ in dr]
        drw = [r_[:, DN_DIM:] for r_ in dr]
        db = each(lambda a, b: a * b, da, gam)
        dq_m = each(lambda a, b: a * b, dp, gam)
        e = each(lambda a, bm, p_, qm, g_: (a * bm + p_ * qm) * g_, da, bmat, dp, qmat, gam)
        dkb = each(lambda b_, k_, r_, e_: _mx(b_, k_) + r_ * e_, db, kh, drw, eg)
        dk = each(lambda b_, kb_, m_, q_, d_, f_: _mx_tn(b_, kb_) + _mx_tn(m_, q_) + d_ * f_, db, kb, dq_m, qh, dkd, kdf)
        dq = each(lambda m_, k_, d_, e_: _mx(m_, k_) + d_ * e_, dq_m, kh, dqd, eg)
        tk = each(lambda a, b: rsum(a * b), dkd, kd)
        dbeta_all = jnp.zeros((DN_CHUNK, 128), F32)
        dgc_all = jnp.zeros((DN_CHUNK, 128), F32)
        for h in hs:
            dgc = (jnp.sum(e[h], axis=1, keepdims=True) - jnp.sum(e[h].T, axis=1, keepdims=True)
                   + rsum(dqd[h] * qd[h]) - tk[h] + rsum(drw[h] * rhs_w[h]))
            dgl = jnp.sum(tk[h], axis=0, keepdims=True) + ddec[h] * dec[h]
            dgc = dgc + jnp.where(rowi == DN_CHUNK - 1, dgl, 0.0)
            dbeta = rsum(dru[h] * vh[h]) + rsum(dkb[h] * kh[h])
            dq_ref[:, sl[h]] = dq[h]
            dk_ref[:, sl[h]] = dk[h] + dkb[h] * beta[h]
            dv_ref[:, sl[h]] = dru[h] * beta[h]
            dbeta_all = jnp.where(lane == h, dbeta, dbeta_all)
            dgc_all = jnp.where(lane == DN_HEADS + h, dgc, dgc_all)
        dbg_ref[...] = dbeta_all + _hi_tn(ltri, dgc_all)

    rev = lambda n: (n_chunks - 1 - n, 0)
    row = pl.BlockSpec((DN_CHUNK, DN_W), rev)
    small = pl.BlockSpec((DN_CHUNK, 128), rev)
    return pl.pallas_call(
        body, name=name, grid=(n_chunks,),
        in_specs=[row, row, row, small,
                  pl.BlockSpec((1, DN_HEADS, DN_DIM, DN_DIM), lambda n: (n_chunks - 1 - n, 0, 0, 0)),
                  pl.BlockSpec((1, DN_HEADS, DN_CHUNK, DN_CHUNK), lambda n: (n_chunks - 1 - n, 0, 0, 0)), row],
        out_specs=(row, row, row, small),
        out_shape=(jax.ShapeDtypeStruct((t, DN_W), F32),) * 3 + (jax.ShapeDtypeStruct((t, 128), F32),),
        scratch_shapes=[pltpu.VMEM((DN_HEADS, DN_DIM, DN_DIM), F32)],
        compiler_params=_cp("arbitrary"))(q, k, v, bg, sall, tall, do)


def gdn_out_fwd(o, proj, o_gain, name, tm=256):
    t = o.shape[0]

    def body(o_ref, z_ref, g_ref, y_ref, yt_ref):
        for h in range(DN_HEADS):
            sl = slice(h * DN_DIM, (h + 1) * DN_DIM)
            ov, zv = o_ref[:, sl], z_ref[:, sl]
            r = lax.rsqrt(jnp.mean(ov * ov, axis=-1, keepdims=True) + EPS)
            y = ov * r * g_ref[...] * (zv * _sigmoid(zv))
            y_ref[:, sl] = y.astype(y_ref.dtype)
            yt_ref[sl, :] = y.T.astype(yt_ref.dtype)

    row = pl.BlockSpec((tm, DN_W), lambda i: (i, 0))
    return pl.pallas_call(
        body, name=name, grid=(t // tm,),
        in_specs=[row, pl.BlockSpec((tm, DN_W), lambda i: (i, _Z_COL)), pl.BlockSpec((1, DN_DIM), lambda i: (0, 0))],
        out_specs=(row, pl.BlockSpec((DN_W, tm), lambda i: (0, i))),
        out_shape=(jax.ShapeDtypeStruct((t, DN_W), MXU_DTYPE), jax.ShapeDtypeStruct((DN_W, t), MXU_DTYPE)),
        compiler_params=_cp("parallel"))(o, proj, o_gain)


def gdn_out_bwd(o, proj, o_gain, dy, name, tm=256):
    t = o.shape[0]

    def body(o_ref, z_ref, g_ref, dy_ref, do_ref, dz_ref, dg_ref):
        i = pl.program_id(0)
        dg = jnp.zeros((1, DN_DIM), F32)
        for h in range(DN_HEADS):
            sl = slice(h * DN_DIM, (h + 1) * DN_DIM)
            ov, zv, dyv = o_ref[:, sl], z_ref[:, sl], dy_ref[:, sl]
            r = lax.rsqrt(jnp.mean(ov * ov, axis=-1, keepdims=True) + EPS)
            oh = ov * r
            sg = _sigmoid(zv)
            dz_ref[:, sl] = (dyv * oh * g_ref[...] * sg * (1.0 + zv * (1.0 - sg))).astype(dz_ref.dtype)
            don = dyv * (zv * sg)
            dg = dg + jnp.sum(don * oh, axis=0, keepdims=True)
            doh = don * g_ref[...]
            do_ref[:, sl] = r * (doh - oh * jnp.mean(doh * oh, axis=-1, keepdims=True))

        @pl.when(i == 0)
        def _():
            dg_ref[...] = dg

        @pl.when(i > 0)
        def _():
            dg_ref[...] += dg

    row = pl.BlockSpec((tm, DN_W), lambda i: (i, 0))
    one = pl.BlockSpec((1, DN_DIM), lambda i: (0, 0))
    return pl.pallas_call(
        body, name=name, grid=(t // tm,),
        in_specs=[row, pl.BlockSpec((tm, DN_W), lambda i: (i, _Z_COL)), one, row],
        out_specs=(row, row, one),
        out_shape=(jax.ShapeDtypeStruct((t, DN_W), F32), jax.ShapeDtypeStruct((t, DN_W), MXU_DTYPE),
                   jax.ShapeDtypeStruct((1, DN_DIM), F32)),
        compiler_params=_cp("arbitrary"))(o, proj, o_gain, dy)


def _peer(k):
    x, y, c = lax.axis_index("x"), lax.axis_index("y"), lax.axis_index("c")
    px = 1 - x if k & 4 else x
    py = 1 - y if k & 2 else y
    pc = 1 - c if k & 1 else c
    return (px, py, pc), 4 * px + 2 * py + pc


_HBM = pl.BlockSpec(memory_space=pltpu.HBM)
_SEM = pl.BlockSpec(memory_space=pltpu.SEMAPHORE)
_DATAFLOW = pltpu.SideEffectType.DATAFLOW_SIDE_EFFECTING
N_PEER = N_DEV - 1


def send_start(srcs, name, scatter, after):
    na = len(srcs)
    ns = (2 * N_PEER + 1) * na
    lands = [lax.empty((N_DEV,) + (s.shape[1:] if scatter else s.shape), s.dtype) for s in srcs]
    extra = [] if after is None else [after]

    def body(*refs):
        src_refs, land_refs = refs[:na], refs[na:2 * na]
        sems = refs[2 * na + len(extra):2 * na + len(extra) + ns]
        land_out, token = refs[-1 - na:-1], refs[-1]
        _, me = _peer(0)
        for a in range(na):
            pltpu.make_async_copy(src_refs[a].at[me] if scatter else src_refs[a], land_out[a].at[me],
                                  sems[2 * N_PEER * na + a]).start()
        for k in range(1, N_DEV):
            peer, pid = _peer(k)
            for a in range(na):
                pltpu.make_async_remote_copy(
                    src_ref=src_refs[a].at[pid] if scatter else src_refs[a], dst_ref=land_refs[a].at[me],
                    send_sem=sems[2 * (a * N_PEER + k - 1)], recv_sem=sems[2 * (a * N_PEER + k - 1) + 1],
                    device_id=peer, device_id_type=MESH).start()
        token[...] = jnp.zeros_like(token)

    hbm = lambda arrs: tuple(pltpu.HBM(a.shape, a.dtype) for a in arrs)
    outs = pl.pallas_call(
        body, name=name,
        out_shape=(pltpu.SemaphoreType.DMA(()),) * ns + hbm(srcs) + hbm(lands) + (jax.ShapeDtypeStruct((8, 128), F32),),
        in_specs=[_HBM] * (2 * na) + [pl.BlockSpec(memory_space=pl.ANY)] * len(extra),
        out_specs=(_SEM,) * ns + (_HBM,) * (2 * na) + (pl.BlockSpec(memory_space=pltpu.VMEM),),
        input_output_aliases={i: ns + i for i in range(2 * na)},
        compiler_params=pltpu.CompilerParams(has_side_effects=_DATAFLOW),
    )(*[pltpu.with_memory_space_constraint(a, pltpu.HBM) for a in list(srcs) + lands], *extra)
    return outs[:ns], outs[ns:ns + na], outs[ns + na:ns + 2 * na], outs[-1]


def send_wait(sems, srcs_thru, lands_thru, name, scatter, after):
    na = len(srcs_thru)
    ns = (2 * N_PEER + 1) * na

    def body(*refs):
        src_refs, land_refs, sm = refs[:na], refs[na:2 * na], refs[2 * na:2 * na + ns]
        _, me = _peer(0)
        for a in range(na):
            pltpu.make_async_copy(src_refs[a].at[me] if scatter else src_refs[a], land_refs[a].at[me],
                                  sm[2 * N_PEER * na + a]).wait()
        for k in range(1, N_DEV):
            peer, pid = _peer(k)
            for a in range(na):
                cp = pltpu.make_async_remote_copy(
                    src_ref=src_refs[a].at[pid] if scatter else src_refs[a], dst_ref=land_refs[a].at[pid],
                    send_sem=sm[2 * (a * N_PEER + k - 1)], recv_sem=sm[2 * (a * N_PEER + k - 1) + 1],
                    device_id=peer, device_id_type=MESH)
                cp.wait_send()
                cp.wait_recv()

    hbm = lambda arrs: tuple(pltpu.HBM(a.shape, a.dtype) for a in arrs)
    outs = pl.pallas_call(
        body, name=name, out_shape=hbm(srcs_thru) + hbm(lands_thru),
        in_specs=[_HBM] * (2 * na) + [_SEM] * ns + [pl.BlockSpec(memory_space=pl.ANY)], out_specs=(_HBM,) * (2 * na),
        input_output_aliases={i: i for i in range(2 * na)},
        compiler_params=pltpu.CompilerParams(has_side_effects=_DATAFLOW),
    )(*srcs_thru, *lands_thru, *sems, after)
    return outs[na:]


def _adamw(w, g, m, v):
    m = ADAM_B1 * m + (1.0 - ADAM_B1) * g
    v = ADAM_B2 * v + (1.0 - ADAM_B2) * (g * g)
    m_hat = m / (1.0 - ADAM_B1 ** ADAM_STEP)
    v_hat = v / (1.0 - ADAM_B2 ** ADAM_STEP)
    return -ADAM_LR * (m_hat / (jnp.sqrt(v_hat) + ADAM_EPS) + ADAM_WD * w), m, v


def adam_sum(w, pieces, m, v, name, layer=0, into=None):
    nl, r, c = w.shape
    tr = r
    for cand in (256, 128, 64, 32, 16, 8):
        if r % cand == 0:
            tr = cand
            break

    def body(w_ref, p_ref, m_ref, v_ref, *rest):
        g_ref, d_ref, nm_ref, nv_ref = rest[-4:]
        g = p_ref[0].astype(F32)
        for s in range(1, N_DEV):
            g = g + p_ref[s].astype(F32)
        g_ref[0] = g
        d_ref[0], nm_ref[0], nv_ref[0] = _adamw(w_ref[0], g, m_ref[0], v_ref[0])

    row = pl.BlockSpec((1, tr, c), lambda i: (layer, i, 0))
    out = jax.ShapeDtypeStruct((nl, r, c), F32)
    extra = [] if into is None else list(into)
    return pl.pallas_call(
        body, name=name, grid=(r // tr,),
        in_specs=[row, pl.BlockSpec((N_DEV, tr, c), lambda i: (0, i, 0)), row, row]
        + [pl.BlockSpec(memory_space=pl.ANY)] * len(extra),
        out_specs=(row,) * 4, out_shape=(out,) * 4,
        input_output_aliases={4 + i: i for i in range(len(extra))},
        compiler_params=_cp("parallel"))(w, pieces, m, v, *extra)


def sum_rows(gathered, name):
    _, r, c = gathered.shape

    def body(p_ref, o_ref):
        g = p_ref[0]
        for s in range(1, N_DEV):
            g = g + p_ref[s]
        o_ref[...] = g

    return pl.pallas_call(body, name=name, out_shape=jax.ShapeDtypeStruct((r, c), F32))(gathered)


def adam_small(w, g, m, v, name):
    def body(w_ref, g_ref, m_ref, v_ref, d_ref, nm_ref, nv_ref):
        d_ref[...], nm_ref[...], nv_ref[...] = _adamw(w_ref[...], g_ref[...], m_ref[...], v_ref[...])

    out = jax.ShapeDtypeStruct(w.shape, F32)
    return pl.pallas_call(body, name=name, out_shape=(out,) * 3)(w, g, m, v)


def _rope_tables(t):
    inv_freq = 10000.0 ** (-jnp.arange(0, HEAD_DIM, 2, dtype=F32) / HEAD_DIM)
    ang = jnp.arange(t, dtype=F32)[:, None] * inv_freq[None, :]
    cos, sin = jnp.cos(ang), jnp.sin(ang)
    return jnp.concatenate([cos, cos], axis=-1), jnp.concatenate([sin, sin], axis=-1)


def _lane_row(vec8):
    return jnp.pad(vec8.reshape(1, DN_HEADS), ((0, 0), (DN_HEADS, 128 - 2 * DN_HEADS)))


def _ffn_bwd(x, norm_g, w_gu, w_d, saved, dy, tag, after=None):
    ft, gu, at = saved
    dgu = ffn_dact(dy, w_d, gu, f"{tag}_d_gate_up", after=after)
    dwd = mm_at(at, dy, f"{tag}_dw_down")
    dwgu = mm_at(ft, dgu, f"{tag}_dw_gate_up")
    dx, dg = mm_rms_bwd(dgu, w_gu, x, norm_g, dy, f"{tag}_d_norm", chunks=_GU_CHUNKS)
    return dx, dwgu, dwd, dg


def local_step(x, target, small, weights_of, grads_out, after=None):
    t = x.shape[0]
    cosf, sinf = _rope_tables(t)
    alog_row, dtb_row = _lane_row(small["odd_a_log"]), _lane_row(small["odd_dt_bias"])

    h0, h0t = rms_fwd(x, small["even_norm"], "even_norm", after=after)
    we = weights_of("even", h0)
    small = {**small, **we.get("small", {})}
    proj0 = mm_nt(h0, we["w_in"], "even_in_proj")
    qr, kr = qk_prep_fwd(proj0, small["even_q_gain"], small["even_k_gain"], cosf, sinf, "even_qk_prep")
    y_attn, mix0, mix0_t, lse = swa_fwd(qr, kr, proj0, small["even_sinks"], "even_swa")
    mix0, mix0_t = gconv_fwd(proj0, small["even_conv_w"], mix0, mix0_t, "even_gconv")
    we = {**we, **weights_of("even_out", mix0)}
    x1, f0, f0t = mm_nn_res_norm(mix0, we["w_out"], x, small["ffn_norm0"], "even_out_proj")
    w0 = weights_of("ffn0", x1)
    gu0, a0, a0t = ffn_up(f0, w0["gate_up"], "ffn0_gate_up")
    ffn0 = (f0t, gu0, a0t)
    x2, h1, h1t = mm_nn_res_norm(a0, w0["down"], x1, small["odd_norm"], "ffn0_down")

    wo = weights_of("odd", x2)
    proj1 = mm_nt(h1, wo["w_in"], "odd_in_proj")
    qn, kn, vs, bg = gdn_prep_fwd(proj1, small["odd_conv_w"], alog_row, dtb_row, "odd_prep")
    o, sall, tall = gdn_fwd(qn, kn, vs, bg, "odd_delta_rule")
    og, ogt = gdn_out_fwd(o, proj1, small["odd_o_gain"], "odd_gate_norm")
    x3, f1, f1t = mm_nn_res_norm(og, wo["w_out"], x2, small["ffn_norm1"], "odd_out_proj")
    w1 = weights_of("ffn1", x3)
    gu1, a1, a1t = ffn_up(f1, w1["gate_up"], "ffn1_gate_up")
    ffn1 = (f1t, gu1, a1t)
    dy, loss_row = mm_nn_res_loss(a1, w1["down"], x3, target, "ffn1_down_loss")

    gs = {}
    dx3, dwgu, dwd, gs["ffn_norm1"] = _ffn_bwd(x3, small["ffn_norm1"], w1["gate_up"], w1["down"], ffn1, dy, "ffn1")
    tok = grads_out("ffn1", {"gate_up": dwgu, "down": dwd})

    dog = mm_nt(dx3, wo["w_out"], "odd_d_gated", after=tok)
    dwo = mm_at(ogt, dx3, "odd_dw_out")
    do, dz, gs["odd_o_gain"] = gdn_out_bwd(o, proj1, small["odd_o_gain"], dog, "odd_d_gate_norm")
    dqn, dkn, dvs, dbg = gdn_bwd(qn, kn, vs, bg, sall, tall, do, "odd_d_delta_rule")
    dqkv, dba, gs["odd_conv_w"], ddt_row, dal_row = gdn_prep_bwd(
        proj1, small["odd_conv_w"], alog_row, dtb_row, dqn, dkn, dvs, dbg, "odd_d_prep")
    gs["odd_dt_bias"] = ddt_row[:, DN_HEADS:2 * DN_HEADS]
    gs["odd_a_log"] = dal_row[:, DN_HEADS:2 * DN_HEADS]
    dproj1 = jnp.concatenate([dqkv, dz, dba], axis=-1)
    dwi = mm_at(h1t, dproj1, "odd_dw_in")
    dx2, gs["odd_norm"] = mm_rms_bwd(dproj1, wo["w_in"], x2, small["odd_norm"], dx3, "odd_d_norm")
    tok = grads_out("odd", {"w_in": dwi, "w_out": dwo})

    dx1, dwgu, dwd, gs["ffn_norm0"] = _ffn_bwd(x1, small["ffn_norm0"], w0["gate_up"], w0["down"], ffn0, dx2, "ffn0",
                                               after=tok)
    tok = grads_out("ffn0", {"gate_up": dwgu, "down": dwd})

    dmix = mm_nt(dx1, we["w_out"], "even_d_mix", after=tok)
    dwo = mm_at(mix0_t, dx1, "even_dw_out")
    dqr, dkr, dv, gs["even_sinks"] = swa_bwd(qr, kr, proj0, small["even_sinks"], y_attn, lse, dmix, "even_d_swa")
    dqk, gs["even_q_gain"], gs["even_k_gain"] = qk_prep_bwd(
        proj0, small["even_q_gain"], small["even_k_gain"], cosf, sinf, dqr, dkr, "even_d_qk_prep")
    dgb, dgc, dxi, gs["even_conv_w"] = gconv_bwd(proj0, small["even_conv_w"], dmix, "even_d_gconv")
    dproj0 = jnp.concatenate([dqk, dv, dgb, dgc, dxi], axis=-1)
    dwi = mm_at(h0t, dproj0, "even_dw_in")
    tok = grads_out("even", {"w_in": dwi, "w_out": dwo})
    grad_x, gs["even_norm"] = mm_rms_bwd(dproj0, we["w_in"], x, small["even_norm"], dx1, "even_d_norm", after=tok)
    return loss_row, grad_x, gs


_SMALL_ORDER = ("even_norm", "even_q_gain", "even_k_gain", "even_sinks", "odd_a_log", "odd_dt_bias", "odd_o_gain",
                "ffn_norm0", "ffn_norm1", "odd_norm", "even_conv_w", "odd_conv_w")
_SMALL_SIZE = {"even_norm": 1024, "even_q_gain": 64, "even_k_gain": 64, "even_sinks": 8, "odd_a_log": 8,
               "odd_dt_bias": 8, "odd_o_gain": 128, "ffn_norm0": 1024, "ffn_norm1": 1024, "odd_norm": 1024,
               "even_conv_w": 3 * 512, "odd_conv_w": 4 * 3072}
_N_REPL = 9


def _pack_rows(vals):
    flat = jnp.concatenate([v.reshape(-1) for v in vals])
    pad = (-flat.shape[0]) % 1024
    return jnp.pad(flat, (0, pad)).reshape(-1, 128)


def _my_block(full, size, axis):
    me = 4 * lax.axis_index("x") + 2 * lax.axis_index("y") + lax.axis_index("c")
    return lax.dynamic_slice_in_dim(full, me * size, size, axis=axis)


def _col_pieces(dw):
    k, n8 = dw.shape
    return dw.reshape(k, N_DEV, n8 // N_DEV).transpose(1, 2, 0)


def kernel(x, even_norm, even_w_in, even_q_gain, even_k_gain, even_sinks, even_conv_w, even_w_out, odd_norm, odd_w_in, odd_conv_w, odd_a_log, odd_dt_bias, odd_o_gain, odd_w_out, ffn_norm, ffn_w_gate_up, ffn_w_down, loss_target, m_even_norm, m_even_w_in, m_even_q_gain, m_even_k_gain, m_even_sinks, m_even_conv_w, m_even_w_out, m_odd_norm, m_odd_w_in, m_odd_conv_w, m_odd_a_log, m_odd_dt_bias, m_odd_o_gain, m_odd_w_out, m_ffn_norm, m_ffn_w_gate_up, m_ffn_w_down, v_even_norm, v_even_w_in, v_even_q_gain, v_even_k_gain, v_even_sinks, v_even_conv_w, v_even_w_out, v_odd_norm, v_odd_w_in, v_odd_conv_w, v_odd_a_log, v_odd_dt_bias, v_odd_o_gain, v_odd_w_out, v_ffn_norm, v_ffn_w_gate_up, v_ffn_w_down):
    t = x.shape[1]
    d = D_MODEL

    me = 4 * lax.axis_index("x") + 2 * lax.axis_index("y") + lax.axis_index("c")
    tr = lambda a: jnp.swapaxes(a, 1, 2)
    shard = {
        "even": {"w_in": tr(even_w_in)[0], "w_out": even_w_out[0]},
        "ffn0": {"gate_up": tr(ffn_w_gate_up)[0], "down": ffn_w_down[0]},
        "odd": {"w_in": tr(odd_w_in)[0], "w_out": odd_w_out[0]},
        "ffn1": {"gate_up": tr(ffn_w_gate_up)[1], "down": ffn_w_down[1]},
    }
    given = {
        ("even", "w_in"): ("even_w_in", even_w_in, m_even_w_in, v_even_w_in, 0),
        ("even", "w_out"): ("even_w_out", even_w_out, m_even_w_out, v_even_w_out, 0),
        ("odd", "w_in"): ("odd_w_in", odd_w_in, m_odd_w_in, v_odd_w_in, 0),
        ("odd", "w_out"): ("odd_w_out", odd_w_out, m_odd_w_out, v_odd_w_out, 0),
        ("ffn0", "gate_up"): ("ffn_w_gate_up", ffn_w_gate_up, m_ffn_w_gate_up, v_ffn_w_gate_up, 0),
        ("ffn1", "gate_up"): ("ffn_w_gate_up", ffn_w_gate_up, m_ffn_w_gate_up, v_ffn_w_gate_up, 1),
        ("ffn0", "down"): ("ffn_w_down", ffn_w_down, m_ffn_w_down, v_ffn_w_down, 0),
        ("ffn1", "down"): ("ffn_w_down", ffn_w_down, m_ffn_w_down, v_ffn_w_down, 1),
    }

    def whole(group, parts):
        col, row = tuple(shard[group])
        w_col = parts[0].reshape(-1, d)
        if group == "odd":
            w_col = jnp.pad(w_col, ((0, ODD_IN_PAD - ODD_IN_W), (0, 0)))
        return {col: w_col, row: parts[1].reshape(-1, d)}

    wire = {g: [a.astype(MXU_DTYPE) for a in shard[g].values()] for g in shard}
    wire["even_out"] = [wire["even"].pop()]
    wire["even"].append(_pack_rows([odd_norm, even_conv_w, odd_conv_w]))
    gathers, tok = {}, None
    for g in ("even", "even_out", "ffn0", "odd", "ffn1"):
        sems, srcs_thru, lands_thru, tok = send_start(wire[g], f"gather_{g}_start", False, tok)
        gathers[g] = (sems, srcs_thru, lands_thru)
    o1 = d // N_DEV
    o2 = o1 + 3 * CONV_CH // N_DEV

    def weights_of(group, after):
        lands = send_wait(*gathers[group], f"gather_{group}_wait", False, after)
        if group == "even_out":
            return {"w_out": lands[0].reshape(-1, d)}
        if group != "even":
            return whole(group, lands)
        sg = lands[1].reshape(N_DEV, -1)
        return {"w_in": lands[0].reshape(-1, d), "small": {
            "odd_norm": sg[:, :o1].reshape(1, d),
            "even_conv_w": sg[:, o1:o2].reshape(N_DEV, 3, CONV_CH // N_DEV).transpose(1, 0, 2).reshape(3, CONV_CH),
            "odd_conv_w": sg[:, o2:o2 + 4 * _QKV_W // N_DEV].reshape(N_DEV, 4, _QKV_W // N_DEV)
            .transpose(1, 0, 2).reshape(4, _QKV_W)}}

    sent = {}

    def grads_out(group, dws):
        col, row = tuple(shard[group])
        n_cols = N_DEV * shard[group][col].shape[0]
        pieces = [_gu_pieces(dws[col]) if col == "gate_up" else _col_pieces(dws[col][:, :n_cols]),
                  dws[row].reshape((N_DEV,) + shard[group][row].shape)]
        sems, srcs_thru, lands_thru, token = send_start(pieces, f"exchange_{group}_start", True, None)
        sent[group] = (sems, srcs_thru, lands_thru, pieces)
        return token

    small = {
        "even_norm": even_norm, "even_q_gain": even_q_gain, "even_k_gain": even_k_gain, "even_sinks": even_sinks,
        "odd_a_log": odd_a_log.reshape(-1), "odd_dt_bias": odd_dt_bias.reshape(-1), "odd_o_gain": odd_o_gain,
        "ffn_norm0": ffn_norm[0:1], "ffn_norm1": ffn_norm[1:2],
    }

    loss_row, grad_x, gs = local_step(x.reshape(t, d), loss_target.reshape(t, d), small, weights_of, grads_out, after=tok)

    rows = _pack_rows([gs[n] for n in _SMALL_ORDER] + [loss_row[:, 0:1]])
    small_sent = send_start([rows], "gather_small_grads_start", False, None)

    res, behind = {}, small_sent[3]
    for g in ("ffn1", "odd", "ffn0", "even"):
        sems, srcs_thru, lands_thru, pieces = sent[g]
        lands = send_wait(sems, srcs_thru, lands_thru, f"exchange_{g}_wait", True, behind)
        for i, (key, pcs) in enumerate(zip(shard[g], lands)):
            name, w_, m_, v_, layer = given[g, key]
            view = tr if i == 0 else (lambda a: a)
            res[name] = adam_sum(view(w_), pcs, view(m_), view(v_), f"adamw_{g}_{key}", layer=layer, into=res.get(name))
        behind = res[name][0]
    for name in ("even_w_in", "odd_w_in", "ffn_w_gate_up"):
        res[name] = tuple(tr(a) for a in res[name])

    (rows_g,) = send_wait(*small_sent[:3], "gather_small_grads_wait", False, behind)
    tot = sum_rows(rows_g, "sum_small_grads").reshape(-1)
    off, sgrad = 0, {}
    for n in _SMALL_ORDER:
        sgrad[n] = tot[off:off + _SMALL_SIZE[n]]
        off += _SMALL_SIZE[n]
    loss = tot[off]

    repl = _SMALL_ORDER[:_N_REPL]
    repl_w = {"even_norm": even_norm, "even_q_gain": even_q_gain, "even_k_gain": even_k_gain, "even_sinks": even_sinks,
              "odd_a_log": odd_a_log, "odd_dt_bias": odd_dt_bias, "odd_o_gain": odd_o_gain,
              "ffn_norm0": ffn_norm[0], "ffn_norm1": ffn_norm[1]}
    repl_m = {"even_norm": m_even_norm, "even_q_gain": m_even_q_gain, "even_k_gain": m_even_k_gain,
              "even_sinks": m_even_sinks, "odd_a_log": m_odd_a_log, "odd_dt_bias": m_odd_dt_bias,
              "odd_o_gain": m_odd_o_gain, "ffn_norm0": m_ffn_norm[0], "ffn_norm1": m_ffn_norm[1]}
    repl_v = {"even_norm": v_even_norm, "even_q_gain": v_even_q_gain, "even_k_gain": v_even_k_gain,
              "even_sinks": v_even_sinks, "odd_a_log": v_odd_a_log, "odd_dt_bias": v_odd_dt_bias,
              "odd_o_gain": v_odd_o_gain, "ffn_norm0": v_ffn_norm[0], "ffn_norm1": v_ffn_norm[1]}
    pk = lambda dct: _pack_rows([dct[n] for n in repl])
    pd_, pm_, pv_ = adam_small(pk(repl_w), pk(sgrad), pk(repl_m), pk(repl_v), "adamw_replicated")
    sres = {}
    off = 0
    for n in repl:
        sz = _SMALL_SIZE[n]
        sres[n] = (sgrad[n], pd_.reshape(-1)[off:off + sz], pm_.reshape(-1)[off:off + sz], pv_.reshape(-1)[off:off + sz])
        off += sz
    g_on = _my_block(sgrad["odd_norm"].reshape(1, d), d // N_DEV, 1)
    g_ec = _my_block(sgrad["even_conv_w"].reshape(3, CONV_CH), CONV_CH // N_DEV, 1)
    g_oc = _my_block(sgrad["odd_conv_w"].reshape(4, _QKV_W), _QKV_W // N_DEV, 1)
    shard_w = _pack_rows([odd_norm, even_conv_w, odd_conv_w])
    sd_, sm_, sv_ = adam_small(shard_w, _pack_rows([g_on, g_ec, g_oc]),
                               _pack_rows([m_odd_norm, m_even_conv_w, m_odd_conv_w]),
                               _pack_rows([v_odd_norm, v_even_conv_w, v_odd_conv_w]), "adamw_sharded_small")
    off = 0
    for n, gfull, like in (("odd_norm", g_on, odd_norm), ("even_conv_w", g_ec, even_conv_w), ("odd_conv_w", g_oc, odd_conv_w)):
        sz = like.size
        sres[n] = (gfull, sd_.reshape(-1)[off:off + sz], sm_.reshape(-1)[off:off + sz], sv_.reshape(-1)[off:off + sz])
        off += sz

    def small_out(name, like, kind):
        if name == "ffn_norm":
            return jnp.stack([sres["ffn_norm0"][kind], sres["ffn_norm1"][kind]]).reshape(like.shape)
        return sres[name][kind].reshape(like.shape)

    order = (("even_norm", even_norm), ("even_w_in", even_w_in), ("even_q_gain", even_q_gain),
             ("even_k_gain", even_k_gain), ("even_sinks", even_sinks), ("even_conv_w", even_conv_w),
             ("even_w_out", even_w_out), ("odd_norm", odd_norm), ("odd_w_in", odd_w_in), ("odd_conv_w", odd_conv_w),
             ("odd_a_log", odd_a_log), ("odd_dt_bias", odd_dt_bias), ("odd_o_gain", odd_o_gain),
             ("odd_w_out", odd_w_out), ("ffn_norm", ffn_norm), ("ffn_w_gate_up", ffn_w_gate_up),
             ("ffn_w_down", ffn_w_down))
    outs = [loss, grad_x.reshape(x.shape)]
    for kind in range(4):
        for name, like in order:
            outs.append(res[name][kind] if name in res else small_out(name, like, kind))
    return tuple(outs)
```

```python
import jax
import jax.numpy as jnp
import numpy as np
from jax import lax
from jax.experimental import pallas as pl
from jax.experimental.pallas import tpu as pltpu

F32 = jnp.float32
MXU_DTYPE = jnp.bfloat16
HI = lax.Precision.HIGH
EPS = 1e-6
N_DEV = 8
D_MODEL = 1024
HEAD_DIM = 64
ATTN_HEADS = 8
KV_HEADS = 2
ATTN_BLOCK = 128
Q_W = 512
KV_W = 128
CONV_CH = 512
EVEN_IN_W = 2304
DN_HEADS = 8
DN_DIM = 128
DN_W = 1024
DN_CHUNK = 64
ODD_IN_W = 4112
ODD_IN_PAD = 4224
D_FF = 2816
NEG = -1e30
VMEM_LIMIT = 56 * 1024 * 1024
ADAM_LR, ADAM_B1, ADAM_B2, ADAM_EPS, ADAM_WD, ADAM_STEP = 0.001, 0.9, 0.999, 1e-08, 0.01, 10
MESH = pl.DeviceIdType.MESH


def _cp(*sem):
    return pltpu.CompilerParams(dimension_semantics=sem, vmem_limit_bytes=VMEM_LIMIT)


def _pick(n, cap):
    best = 128
    for t in range(128, cap + 1, 128):
        if n % t == 0:
            best = t
    return best


def _mx(a, b):
    return jnp.dot(a.astype(MXU_DTYPE), b.astype(MXU_DTYPE), preferred_element_type=F32)


def _mx_nt(a, b):
    return lax.dot_general(a.astype(MXU_DTYPE), b.astype(MXU_DTYPE), (((1,), (1,)), ((), ())),
                           preferred_element_type=F32)


def _mx_tn(a, b):
    return lax.dot_general(a.astype(MXU_DTYPE), b.astype(MXU_DTYPE), (((0,), (0,)), ((), ())),
                           preferred_element_type=F32)


def _hi(a, b):
    return jnp.dot(a, b, precision=HI, preferred_element_type=F32)


def _hi_nt(a, b):
    return lax.dot_general(a, b, (((1,), (1,)), ((), ())), precision=HI, preferred_element_type=F32)


def _hi_tn(a, b):
    return lax.dot_general(a, b, (((0,), (0,)), ((), ())), precision=HI, preferred_element_type=F32)


def _sigmoid(x):
    return 0.5 * jnp.tanh(0.5 * x) + 0.5


def _softplus(x):
    return jnp.maximum(x, 0.0) + jnp.log(1.0 + jnp.exp(-jnp.abs(x)))


def mm_nn_res_norm(a, b, res, g, name, tm=512):
    t, k = a.shape
    d = b.shape[1]
    tm = min(tm, t)

    def body(a_ref, b_ref, res_ref, g_ref, y_ref, h_ref, ht_ref):
        y = res_ref[...] + _mx(a_ref[...], b_ref[...])
        y_ref[...] = y
        h = y * lax.rsqrt(jnp.mean(y * y, axis=-1, keepdims=True) + EPS) * g_ref[...]
        h_ref[...] = h.astype(h_ref.dtype)
        ht_ref[...] = h.T.astype(ht_ref.dtype)

    row = pl.BlockSpec((tm, d), lambda i: (i, 0))
    return pl.pallas_call(
        body, name=name, grid=(t // tm,),
        in_specs=[pl.BlockSpec((tm, k), lambda i: (i, 0)), pl.BlockSpec((k, d), lambda i: (0, 0)), row,
                  pl.BlockSpec((1, d), lambda i: (0, 0))],
        out_specs=(row, row, pl.BlockSpec((d, tm), lambda i: (0, i))),
        out_shape=(jax.ShapeDtypeStruct((t, d), F32), jax.ShapeDtypeStruct((t, d), MXU_DTYPE),
                   jax.ShapeDtypeStruct((d, t), MXU_DTYPE)),
        compiler_params=_cp("parallel"))(a, b, res, g)


def mm_nt(a, b, name, out_dtype=F32, tm=1024, after=None):
    m, k = a.shape
    n, _ = b.shape
    tn = _pick(n, 512 if k > 3000 else 1536)
    tm = min(tm, m)

    def body(a_ref, b_ref, *rest):
        o_ref = rest[-1]
        o_ref[...] = _mx_nt(a_ref[...], b_ref[...]).astype(o_ref.dtype)

    in_specs = [pl.BlockSpec((tm, k), lambda j, i: (i, 0)), pl.BlockSpec((tn, k), lambda j, i: (j, 0))]
    args = [a, b]
    if after is not None:
        in_specs.append(pl.BlockSpec(memory_space=pl.ANY))
        args.append(after)
    return pl.pallas_call(
        body, name=name, grid=(n // tn, m // tm), in_specs=in_specs,
        out_specs=pl.BlockSpec((tm, tn), lambda j, i: (i, j)),
        out_shape=jax.ShapeDtypeStruct((m, n), out_dtype), compiler_params=_cp("parallel", "parallel"))(*args)


def mm_at(at, b, name, tk=1024, transposed=False, tn=None, row_block=None):
    m, kk = at.shape
    _, n = b.shape
    tm, tn, tk = _pick(m, 1408), tn or _pick(n, 2816), min(tk, kk)
    nk = kk // tk

    def body(a_ref, b_ref, o_ref, acc_ref):
        k = pl.program_id(2)
        p = _mx(a_ref[...], b_ref[...])
        acc = jnp.where(k == 0, p, acc_ref[...] + p)
        acc_ref[...] = acc

        @pl.when(k == nk - 1)
        def _():
            o_ref[...] = (acc.T if transposed else acc).astype(o_ref.dtype)

    if transposed:
        rb = row_block or (lambda j: j)
        out_spec = pl.BlockSpec((tn, tm), lambda i, j, k: (rb(j), i))
        out_shape = jax.ShapeDtypeStruct((n, m), MXU_DTYPE)
    else:
        out_spec = pl.BlockSpec((tm, tn), lambda i, j, k: (i, j))
        out_shape = jax.ShapeDtypeStruct((m, n), MXU_DTYPE)
    return pl.pallas_call(
        body, name=name, grid=(m // tm, n // tn, nk),
        in_specs=[pl.BlockSpec((tm, tk), lambda i, j, k: (i, k)), pl.BlockSpec((tk, tn), lambda i, j, k: (k, j))],
        out_specs=out_spec, out_shape=out_shape, scratch_shapes=[pltpu.VMEM((tm, tn), F32)],
        compiler_params=_cp("parallel", "parallel", "arbitrary"))(at, b)


def rms_fwd(x, g, name, tm=512, after=None):
    t, d = x.shape

    def body(x_ref, g_ref, *rest):
        o_ref, ot_ref = rest[-2:]
        xv = x_ref[...]
        r = lax.rsqrt(jnp.mean(xv * xv, axis=-1, keepdims=True) + EPS)
        h = xv * r * g_ref[...]
        o_ref[...] = h.astype(o_ref.dtype)
        ot_ref[...] = h.T.astype(ot_ref.dtype)

    in_specs = [pl.BlockSpec((tm, d), lambda i: (i, 0)), pl.BlockSpec((1, d), lambda i: (0, 0))]
    args = [x, g]
    if after is not None:
        in_specs.append(pl.BlockSpec(memory_space=pl.ANY))
        args.append(after)
    return pl.pallas_call(
        body, name=name, grid=(t // tm,), in_specs=in_specs,
        out_specs=(pl.BlockSpec((tm, d), lambda i: (i, 0)), pl.BlockSpec((d, tm), lambda i: (0, i))),
        out_shape=(jax.ShapeDtypeStruct((t, d), MXU_DTYPE), jax.ShapeDtypeStruct((d, t), MXU_DTYPE)),
        compiler_params=_cp("parallel"))(*args)


def mm_rms_bwd(a, bt, x, g, dres, name, tm=512, after=None, chunks=None):
    t, k = a.shape
    d = bt.shape[1]
    tm = min(tm, t)
    chunks = chunks or ((0, 0, k),)

    def body(a_ref, b_ref, x_ref, g_ref, dres_ref, *rest):
        dx_ref, dg_ref = rest[-2:]
        dhv = None
        for ca, cb, size in chunks:
            part = _mx(a_ref[:, ca:ca + size], b_ref[cb:cb + size, :])
            dhv = part if dhv is None else dhv + part
        xv = x_ref[...]
        r = lax.rsqrt(jnp.mean(xv * xv, axis=-1, keepdims=True) + EPS)
        xh = xv * r
        dxh = dhv * g_ref[...]
        dx_ref[...] = dres_ref[...] + r * (dxh - xh * jnp.mean(dxh * xh, axis=-1, keepdims=True))
        part = jnp.sum(dhv * xh, axis=0, keepdims=True)
        dg_ref[...] = jnp.where(pl.program_id(0) == 0, part, dg_ref[...] + part)

    row = pl.BlockSpec((tm, d), lambda i: (i, 0))
    one = pl.BlockSpec((1, d), lambda i: (0, 0))
    in_specs = [pl.BlockSpec((tm, k), lambda i: (i, 0)), pl.BlockSpec((k, d), lambda i: (0, 0)), row, one, row]
    args = [a, bt, x, g, dres]
    if after is not None:
        in_specs.append(pl.BlockSpec(memory_space=pl.ANY))
        args.append(after)
    return pl.pallas_call(
        body, name=name, grid=(t // tm,), in_specs=in_specs, out_specs=(row, one),
        out_shape=(jax.ShapeDtypeStruct((t, d), F32), jax.ShapeDtypeStruct((1, d), F32)),
        compiler_params=_cp("arbitrary"))(*args)


GU_TILE = 1408


def ffn_up(f, wt, name, tm=512):
    t, d = f.shape
    nj = D_FF // GU_TILE

    def body(f_ref, wg_ref, wu_ref, gu_ref, a_ref, at_ref):
        g = _mx_nt(f_ref[...], wg_ref[...])
        u = _mx_nt(f_ref[...], wu_ref[...])
        gu_ref[:, :GU_TILE] = g
        gu_ref[:, GU_TILE:] = u
        act = g * _sigmoid(g) * u
        a_ref[...] = act.astype(a_ref.dtype)
        at_ref[...] = act.T.astype(at_ref.dtype)

    return pl.pallas_call(
        body, name=name, grid=(nj, t // tm),
        in_specs=[pl.BlockSpec((tm, d), lambda j, i: (i, 0)), pl.BlockSpec((GU_TILE, d), lambda j, i: (j, 0)),
                  pl.BlockSpec((GU_TILE, d), lambda j, i: (nj + j, 0))],
        out_specs=(pl.BlockSpec((tm, 2 * GU_TILE), lambda j, i: (i, j)), pl.BlockSpec((tm, GU_TILE), lambda j, i: (i, j)),
                   pl.BlockSpec((GU_TILE, tm), lambda j, i: (j, i))),
        out_shape=(jax.ShapeDtypeStruct((t, 2 * D_FF), F32), jax.ShapeDtypeStruct((t, D_FF), MXU_DTYPE),
                   jax.ShapeDtypeStruct((D_FF, t), MXU_DTYPE)),
        compiler_params=_cp("parallel", "parallel"))(f, wt, wt)


_GU_CHUNKS = tuple((q * GU_TILE, ((q % 2) * (D_FF // GU_TILE) + q // 2) * GU_TILE, GU_TILE)
                   for q in range(2 * D_FF // GU_TILE))


def ffn_dact(dy, w_d, gu, name, tm=512, after=None):
    t, d = dy.shape

    def body(dy_ref, w_ref, gu_ref, *rest):
        o_ref = rest[-1]
        da = _mx_nt(dy_ref[...], w_ref[...])
        g, u = gu_ref[:, :GU_TILE], gu_ref[:, GU_TILE:]
        sg = _sigmoid(g)
        o_ref[:, :GU_TILE] = (da * u * sg * (1.0 + g * (1.0 - sg))).astype(o_ref.dtype)
        o_ref[:, GU_TILE:] = (da * g * sg).astype(o_ref.dtype)

    in_specs = [pl.BlockSpec((tm, d), lambda j, i: (i, 0)), pl.BlockSpec((GU_TILE, d), lambda j, i: (j, 0)),
                pl.BlockSpec((tm, 2 * GU_TILE), lambda j, i: (i, j))]
    args = [dy, w_d, gu]
    if after is not None:
        in_specs.append(pl.BlockSpec(memory_space=pl.ANY))
        args.append(after)
    return pl.pallas_call(
        body, name=name, grid=(D_FF // GU_TILE, t // tm), in_specs=in_specs,
        out_specs=pl.BlockSpec((tm, 2 * GU_TILE), lambda j, i: (i, j)),
        out_shape=jax.ShapeDtypeStruct((t, 2 * D_FF), MXU_DTYPE), compiler_params=_cp("parallel", "parallel"))(*args)


def mm_nn_res_loss(a, b, res, target, name, tm=512):
    t, k = a.shape
    d = b.shape[1]
    tm = min(tm, t)

    def body(a_ref, b_ref, res_ref, t_ref, dy_ref, l_ref):
        e = res_ref[...] + _mx(a_ref[...], b_ref[...]) - t_ref[...]
        dy_ref[...] = e * (1.0 / d)
        part = jnp.zeros((1, 128), F32) + 0.5 * jnp.sum(jnp.mean(e * e, axis=-1, keepdims=True), axis=0, keepdims=True)
        l_ref[...] = jnp.where(pl.program_id(0) == 0, part, l_ref[...] + part)

    row = pl.BlockSpec((tm, d), lambda i: (i, 0))
    return pl.pallas_call(
        body, name=name, grid=(t // tm,),
        in_specs=[pl.BlockSpec((tm, k), lambda i: (i, 0)), pl.BlockSpec((k, d), lambda i: (0, 0)), row, row],
        out_specs=(row, pl.BlockSpec((1, 128), lambda i: (0, 0))),
        out_shape=(jax.ShapeDtypeStruct((t, d), F32), jax.ShapeDtypeStruct((1, 128), F32)),
        compiler_params=_cp("arbitrary"))(a, b, res, target)


QK_W = Q_W + KV_W
_QK_TILE = 256


def _qk_mats():
    idx = np.arange(_QK_TILE)
    half = HEAD_DIM // 2
    same = (idx[:, None] // HEAD_DIM) == (idx[None, :] // HEAD_DIM)
    lo = (idx % HEAD_DIM) < half
    rot = np.where((idx[:, None] == idx[None, :] + half) & lo[None, :], -1.0, 0.0)
    rot = rot + np.where((idx[:, None] == idx[None, :] - half) & ~lo[None, :], 1.0, 0.0)
    return jnp.asarray(same, F32), jnp.asarray(rot, F32)


def _qk_rows(q_gain, k_gain, cosf, sinf):
    gain = jnp.concatenate([q_gain] * ATTN_HEADS + [k_gain] * KV_HEADS, axis=-1)
    return gain, jnp.concatenate([cosf, cosf], axis=-1), jnp.concatenate([sinf, sinf], axis=-1)


def _qk_tiles(a, mat, transposed=False):
    outs = []
    for c0 in range(0, QK_W, _QK_TILE):
        w = min(_QK_TILE, QK_W - c0)
        mt = (mat.T if transposed else mat)[:w, :w].astype(MXU_DTYPE)
        at = a[:, c0:c0 + w]
        hi = at.astype(MXU_DTYPE)
        lo = (at - hi.astype(F32)).astype(MXU_DTYPE)
        outs.append(jnp.dot(hi, mt, preferred_element_type=F32) + jnp.dot(lo, mt, preferred_element_type=F32))
    return jnp.concatenate(outs, axis=-1)


def qk_prep_fwd(proj, q_gain, k_gain, cosf, sinf, name, tm=256):
    t = proj.shape[0]
    gmat, rmat = _qk_mats()
    gain, c2, s2 = _qk_rows(q_gain, k_gain, cosf, sinf)
    rep = QK_W // 128

    def body(p_ref, g_ref, c_ref, s_ref, gm_ref, rm_ref, q_ref, k_ref):
        x = p_ref[...]
        r = lax.rsqrt(_qk_tiles(x * x, gm_ref[...]) * (1.0 / HEAD_DIM) + EPS)
        xn = x * r * g_ref[...]
        c = jnp.concatenate([c_ref[...]] * rep, axis=-1)
        s = jnp.concatenate([s_ref[...]] * rep, axis=-1)
        out = xn * c + _qk_tiles(xn, rm_ref[...]) * s
        q_ref[...] = out[:, :Q_W]
        k_ref[...] = out[:, Q_W:]

    full = pl.BlockSpec((_QK_TILE, _QK_TILE), lambda i: (0, 0))
    tab = pl.BlockSpec((tm, 128), lambda i: (i, 0))
    return pl.pallas_call(
        body, name=name, grid=(t // tm,),
        in_specs=[pl.BlockSpec((tm, QK_W), lambda i: (i, 0)), pl.BlockSpec((1, QK_W), lambda i: (0, 0)), tab, tab,
                  full, full],
        out_specs=(pl.BlockSpec((tm, Q_W), lambda i: (i, 0)), pl.BlockSpec((tm, KV_W), lambda i: (i, 0))),
        out_shape=(jax.ShapeDtypeStruct((t, Q_W), F32), jax.ShapeDtypeStruct((t, KV_W), F32)),
        compiler_params=_cp("parallel"))(proj, gain, c2, s2, gmat, rmat)


def qk_prep_bwd(proj, q_gain, k_gain, cosf, sinf, dq, dk, name, tm=256):
    t = proj.shape[0]
    gmat, rmat = _qk_mats()
    gain, c2, s2 = _qk_rows(q_gain, k_gain, cosf, sinf)
    rep = QK_W // 128
    lanes = np.arange(QK_W)[:, None]
    fold = jnp.asarray(lanes % HEAD_DIM + np.where(lanes >= Q_W, HEAD_DIM, 0) == np.arange(128)[None, :], F32)

    def body(p_ref, g_ref, c_ref, s_ref, gm_ref, rm_ref, f_ref, dq_ref, dk_ref, o_ref, dg_ref):
        x = p_ref[...]
        r = lax.rsqrt(_qk_tiles(x * x, gm_ref[...]) * (1.0 / HEAD_DIM) + EPS)
        xh = x * r
        c = jnp.concatenate([c_ref[...]] * rep, axis=-1)
        s = jnp.concatenate([s_ref[...]] * rep, axis=-1)
        dout = jnp.concatenate([dq_ref[...], dk_ref[...]], axis=-1)
        dxn = dout * c + _qk_tiles(dout * s, rm_ref[...], transposed=True)
        part = _hi(jnp.sum(dxn * xh, axis=0, keepdims=True), f_ref[...])
        dxh = dxn * g_ref[...]
        mean = _qk_tiles(dxh * xh, gm_ref[...]) * (1.0 / HEAD_DIM)
        o_ref[...] = (r * (dxh - xh * mean)).astype(o_ref.dtype)
        dg_ref[...] = jnp.where(pl.program_id(0) == 0, part, dg_ref[...] + part)

    full = pl.BlockSpec((_QK_TILE, _QK_TILE), lambda i: (0, 0))
    tab = pl.BlockSpec((tm, 128), lambda i: (i, 0))
    dqk, dg = pl.pallas_call(
        body, name=name, grid=(t // tm,),
        in_specs=[pl.BlockSpec((tm, QK_W), lambda i: (i, 0)), pl.BlockSpec((1, QK_W), lambda i: (0, 0)), tab, tab,
                  full, full, pl.BlockSpec((QK_W, 128), lambda i: (0, 0)),
                  pl.BlockSpec((tm, Q_W), lambda i: (i, 0)), pl.BlockSpec((tm, KV_W), lambda i: (i, 0))],
        out_specs=(pl.BlockSpec((tm, QK_W), lambda i: (i, 0)), pl.BlockSpec((1, 128), lambda i: (0, 0))),
        out_shape=(jax.ShapeDtypeStruct((t, QK_W), MXU_DTYPE), jax.ShapeDtypeStruct((1, 128), F32)),
        compiler_params=_cp("arbitrary"))(proj, gain, c2, s2, gmat, rmat, fold, dq, dk)
    return dqk, dg[:, :HEAD_DIM], dg[:, HEAD_DIM:]


def _swa_valid(n, grp):
    qi = lax.broadcasted_iota(jnp.int32, (grp * ATTN_BLOCK, 2 * ATTN_BLOCK), 0) & (ATTN_BLOCK - 1)
    kj = lax.broadcasted_iota(jnp.int32, (grp * ATTN_BLOCK, 2 * ATTN_BLOCK), 1)
    diff = qi + ATTN_BLOCK - kj
    return (diff >= 0) & (diff < ATTN_BLOCK) & (n * ATTN_BLOCK - ATTN_BLOCK + kj >= 0)


def _stack_heads(ref, g, grp):
    return jnp.concatenate([ref[:, (g * grp + j) * HEAD_DIM:(g * grp + j + 1) * HEAD_DIM] for j in range(grp)], axis=0)


def _stack_sinks(s_ref, g, grp):
    return jnp.concatenate([jnp.zeros((ATTN_BLOCK, 1), F32) + s_ref[0:1, g * grp + j:g * grp + j + 1]
                            for j in range(grp)], axis=0)


def swa_fwd(q, k, proj, sinks, name):
    t = q.shape[0]
    nb = t // ATTN_BLOCK
    scale = HEAD_DIM ** -0.5
    grp = ATTN_HEADS // KV_HEADS

    def body(q_ref, kc_ref, kp_ref, vc_ref, vp_ref, s_ref, y_ref, mix_ref, yt_ref, lse_ref):
        n = pl.program_id(0)
        valid = _swa_valid(n, grp)
        kk = jnp.concatenate([kp_ref[...], kc_ref[...]], axis=0).astype(MXU_DTYPE)
        vv = jnp.concatenate([vp_ref[...], vc_ref[...]], axis=0).astype(MXU_DTYPE)
        lane = lax.broadcasted_iota(jnp.int32, (ATTN_BLOCK, ATTN_HEADS), 1)
        gs = range(KV_HEADS)
        qg = [_stack_heads(q_ref, g, grp) for g in gs]
        sink = [_stack_sinks(s_ref, g, grp) for g in gs]
        sc = [jnp.where(valid, _mx_nt(qg[g], kk[:, g * HEAD_DIM:(g + 1) * HEAD_DIM]) * scale, NEG) for g in gs]
        m = [jnp.maximum(jnp.max(sc[g], axis=-1, keepdims=True), sink[g]) for g in gs]
        e = [jnp.exp(sc[g] - m[g]) for g in gs]
        den = [jnp.sum(e[g], axis=-1, keepdims=True) + jnp.exp(sink[g] - m[g]) for g in gs]
        og = [_mx(e[g] / den[g], vv[:, g * HEAD_DIM:(g + 1) * HEAD_DIM]) for g in gs]
        lg = [m[g] + jnp.log(den[g]) for g in gs]
        lse = jnp.zeros((ATTN_BLOCK, ATTN_HEADS), F32)
        outs = []
        for h in range(ATTN_HEADS):
            rows = slice((h % grp) * ATTN_BLOCK, (h % grp + 1) * ATTN_BLOCK)
            outs.append(og[h // grp][rows])
            lse = jnp.where(lane == h, lg[h // grp][rows], lse)
        y = jnp.concatenate(outs, axis=-1)
        y_ref[...] = y
        mix_ref[...] = y.astype(mix_ref.dtype)
        yt_ref[...] = y.T.astype(yt_ref.dtype)
        lse_ref[...] = lse

    cur = lambda n: (n, 0)
    prev = lambda n: (jnp.maximum(n - 1, 0), 0)
    vcol = (Q_W + KV_W) // KV_W
    return pl.pallas_call(
        body, name=name, grid=(nb,),
        in_specs=[pl.BlockSpec((ATTN_BLOCK, Q_W), cur), pl.BlockSpec((ATTN_BLOCK, KV_W), cur),
                  pl.BlockSpec((ATTN_BLOCK, KV_W), prev),
                  pl.BlockSpec((ATTN_BLOCK, KV_W), lambda n: (n, vcol)),
                  pl.BlockSpec((ATTN_BLOCK, KV_W), lambda n: (jnp.maximum(n - 1, 0), vcol)),
                  pl.BlockSpec((1, ATTN_HEADS), lambda n: (0, 0))],
        out_specs=(pl.BlockSpec((ATTN_BLOCK, Q_W), cur), pl.BlockSpec((ATTN_BLOCK, Q_W), cur),
                   pl.BlockSpec((Q_W, ATTN_BLOCK), lambda n: (0, n)), pl.BlockSpec((ATTN_BLOCK, ATTN_HEADS), cur)),
        out_shape=(jax.ShapeDtypeStruct((t, Q_W), F32), jax.ShapeDtypeStruct((t, Q_W + CONV_CH), MXU_DTYPE),
                   jax.ShapeDtypeStruct((Q_W + CONV_CH, t), MXU_DTYPE), jax.ShapeDtypeStruct((t, ATTN_HEADS), F32)),
        compiler_params=_cp("parallel"))(q, k, k, proj, proj, sinks)


def swa_bwd(q, k, proj, sinks, y, lse, dmix, name):
    t = q.shape[0]
    nb = t // ATTN_BLOCK
    scale = HEAD_DIM ** -0.5
    grp = ATTN_HEADS // KV_HEADS

    def body(q_ref, kc_ref, kp_ref, vc_ref, vp_ref, s_ref, y_ref, lse_ref, dy_ref,
             dq_ref, dk_ref, dv_ref, ds_ref, dkc, dvc):
        n = pl.program_id(0)

        @pl.when(n == 0)
        def _():
            dkc[...] = jnp.zeros_like(dkc)
            dvc[...] = jnp.zeros_like(dvc)
            ds_ref[...] = jnp.zeros_like(ds_ref)

        @pl.when(n < nb)
        def _():
            valid = _swa_valid(n, grp)
            kk = jnp.concatenate([kp_ref[...], kc_ref[...]], axis=0).astype(MXU_DTYPE)
            vv = jnp.concatenate([vp_ref[...], vc_ref[...]], axis=0).astype(MXU_DTYPE)
            lane = lax.broadcasted_iota(jnp.int32, (1, ATTN_HEADS), 1)
            gs = range(KV_HEADS)
            kg = [kk[:, g * HEAD_DIM:(g + 1) * HEAD_DIM] for g in gs]
            vg = [vv[:, g * HEAD_DIM:(g + 1) * HEAD_DIM] for g in gs]
            qg = [_stack_heads(q_ref, g, grp).astype(MXU_DTYPE) for g in gs]
            dog = [_stack_heads(dy_ref, g, grp) for g in gs]
            og = [_stack_heads(y_ref, g, grp) for g in gs]
            lg = [jnp.concatenate([lse_ref[:, g * grp + j:g * grp + j + 1] for j in range(grp)], axis=0) for g in gs]
            sink = [_stack_sinks(s_ref, g, grp) for g in gs]
            sc = [jnp.where(valid, _mx_nt(qg[g], kg[g]) * scale, NEG) for g in gs]
            p = [jnp.exp(sc[g] - lg[g]) for g in gs]
            delta = [jnp.sum(dog[g] * og[g], axis=-1, keepdims=True) for g in gs]
            ds = [p[g] * (_mx_nt(dog[g], vg[g]) - delta[g]) for g in gs]
            dqg = [_mx(ds[g], kg[g]) * scale for g in gs]
            dkf = jnp.concatenate([_mx_tn(ds[g], qg[g]) * scale for g in gs], axis=-1)
            dvf = jnp.concatenate([_mx_tn(p[g], dog[g]) for g in gs], axis=-1)
            dsk = [jnp.exp(sink[g] - lg[g]) * delta[g] for g in gs]
            dsink = jnp.zeros((1, ATTN_HEADS), F32)
            dqs = []
            for h in range(ATTN_HEADS):
                rows = slice((h % grp) * ATTN_BLOCK, (h % grp + 1) * ATTN_BLOCK)
                dqs.append(dqg[h // grp][rows])
                dsink = jnp.where(lane == h, -jnp.sum(dsk[h // grp][rows], axis=0, keepdims=True), dsink)
            dq_ref[...] = jnp.concatenate(dqs, axis=-1)
            dk_ref[...] = dkc[...] + dkf[:ATTN_BLOCK]
            dv_ref[...] = (dvc[...] + dvf[:ATTN_BLOCK]).astype(dv_ref.dtype)
            dkc[...] = dkf[ATTN_BLOCK:]
            dvc[...] = dvf[ATTN_BLOCK:]
            ds_ref[...] += dsink

        @pl.when(n == nb)
        def _():
            dk_ref[...] = dkc[...]
            dv_ref[...] = dvc[...].astype(dv_ref.dtype)

    cur = lambda n: (jnp.minimum(n, nb - 1), 0)
    prev = lambda n: (jnp.clip(n - 1, 0, nb - 1), 0)
    vcol = (Q_W + KV_W) // KV_W
    return pl.pallas_call(
        body, name=name, grid=(nb + 1,),
        in_specs=[pl.BlockSpec((ATTN_BLOCK, Q_W), cur), pl.BlockSpec((ATTN_BLOCK, KV_W), cur),
                  pl.BlockSpec((ATTN_BLOCK, KV_W), prev),
                  pl.BlockSpec((ATTN_BLOCK, KV_W), lambda n: (jnp.minimum(n, nb - 1), vcol)),
                  pl.BlockSpec((ATTN_BLOCK, KV_W), lambda n: (jnp.clip(n - 1, 0, nb - 1), vcol)),
                  pl.BlockSpec((1, ATTN_HEADS), lambda n: (0, 0)),
                  pl.BlockSpec((ATTN_BLOCK, Q_W), cur), pl.BlockSpec((ATTN_BLOCK, ATTN_HEADS), cur),
                  pl.BlockSpec((ATTN_BLOCK, Q_W), cur)],
        out_specs=(pl.BlockSpec((ATTN_BLOCK, Q_W), cur), pl.BlockSpec((ATTN_BLOCK, KV_W), prev),
                   pl.BlockSpec((ATTN_BLOCK, KV_W), prev), pl.BlockSpec((1, ATTN_HEADS), lambda n: (0, 0))),
        out_shape=(jax.ShapeDtypeStruct((t, Q_W), F32), jax.ShapeDtypeStruct((t, KV_W), F32),
                   jax.ShapeDtypeStruct((t, KV_W), MXU_DTYPE), jax.ShapeDtypeStruct((1, ATTN_HEADS), F32)),
        scratch_shapes=[pltpu.VMEM((ATTN_BLOCK, KV_W), F32), pltpu.VMEM((ATTN_BLOCK, KV_W), F32)],
        compiler_params=_cp("arbitrary"))(q, k, k, proj, proj, sinks, y, lse, dmix)


GC_W = 256
_GB0, _GC0, _XI0 = 768 // GC_W, 1280 // GC_W, 1792 // GC_W
HALO = 8


def gconv_fwd(proj, conv_w, mix, mix_t, name, tm=512):
    t = proj.shape[0]
    hb = tm // HALO
    half = Q_W // GC_W

    def body(gb_ref, gc_ref, xi_ref, gch_ref, xih_ref, w_ref, mix_in, mixt_in, y_ref, yt_ref):
        i = pl.program_id(1)
        u = gc_ref[...] * xi_ref[...]
        uh = jnp.where(i == 0, 0.0, gch_ref[...] * xih_ref[...])
        up = jnp.concatenate([uh, u], axis=0)
        cv = w_ref[0:1, :] * up[HALO - 2:HALO - 2 + tm]
        cv = cv + w_ref[1:2, :] * up[HALO - 1:HALO - 1 + tm]
        cv = cv + w_ref[2:3, :] * u
        y = gb_ref[...] * cv
        y_ref[...] = y.astype(y_ref.dtype)
        yt_ref[...] = y.T.astype(yt_ref.dtype)

    def col(c0):
        return pl.BlockSpec((tm, GC_W), lambda cj, i: (i, c0 + cj))

    def halo(c0):
        return pl.BlockSpec((HALO, GC_W), lambda cj, i: (jnp.maximum(i * hb - 1, 0), c0 + cj))

    return pl.pallas_call(
        body, name=name, grid=(CONV_CH // GC_W, t // tm),
        in_specs=[col(_GB0), col(_GC0), col(_XI0), halo(_GC0), halo(_XI0),
                  pl.BlockSpec((3, GC_W), lambda cj, i: (0, cj)),
                  pl.BlockSpec(memory_space=pl.ANY), pl.BlockSpec(memory_space=pl.ANY)],
        out_specs=(pl.BlockSpec((tm, GC_W), lambda cj, i: (i, half + cj)),
                   pl.BlockSpec((GC_W, tm), lambda cj, i: (half + cj, i))),
        out_shape=(jax.ShapeDtypeStruct(mix.shape, mix.dtype), jax.ShapeDtypeStruct(mix_t.shape, mix_t.dtype)),
        input_output_aliases={6: 0, 7: 1},
        compiler_params=_cp("parallel", "parallel"))(proj, proj, proj, proj, proj, conv_w, mix, mix_t)


def gconv_bwd(proj, conv_w, dmix, name, tm=512):
    t = proj.shape[0]
    hb = tm // HALO
    nt = t // tm
    dy0 = Q_W // GC_W

    def body(gb_ref, gc_ref, xi_ref, gch_ref, xih_ref, gbn_ref, dyn_ref, dy_ref, w_ref,
             dgb_ref, dgc_ref, dxi_ref, dw_ref):
        i = pl.program_id(1)
        gc, xi, gb, dy = gc_ref[...], xi_ref[...], gb_ref[...], dy_ref[...]
        u = gc * xi
        uh = jnp.where(i == 0, 0.0, gch_ref[...] * xih_ref[...])
        up = jnp.concatenate([uh, u], axis=0)
        u2 = up[HALO - 2:HALO - 2 + tm]
        u1 = up[HALO - 1:HALO - 1 + tm]
        cv = w_ref[0:1, :] * u2 + w_ref[1:2, :] * u1 + w_ref[2:3, :] * u
        dgb_ref[...] = (dy * cv).astype(dgb_ref.dtype)
        dcv = dy * gb
        dcvn = jnp.where(i == nt - 1, 0.0, dyn_ref[...] * gbn_ref[...])
        dcvp = jnp.concatenate([dcv, dcvn], axis=0)
        du = w_ref[0:1, :] * dcvp[2:2 + tm] + w_ref[1:2, :] * dcvp[1:1 + tm] + w_ref[2:3, :] * dcv
        dgc_ref[...] = (du * xi).astype(dgc_ref.dtype)
        dxi_ref[...] = (du * gc).astype(dxi_ref.dtype)
        dw = jnp.concatenate([jnp.sum(dcv * u2, axis=0, keepdims=True), jnp.sum(dcv * u1, axis=0, keepdims=True),
                              jnp.sum(dcv * u, axis=0, keepdims=True)], axis=0)

        @pl.when(i == 0)
        def _():
            dw_ref[...] = dw

        @pl.when(i > 0)
        def _():
            dw_ref[...] += dw

    def col(c0):
        return pl.BlockSpec((tm, GC_W), lambda cj, i: (i, c0 + cj))

    def halo(c0):
        return pl.BlockSpec((HALO, GC_W), lambda cj, i: (jnp.maximum(i * hb - 1, 0), c0 + cj))

    def nxt(c0):
        return pl.BlockSpec((HALO, GC_W), lambda cj, i: (jnp.minimum((i + 1) * hb, t // HALO - 1), c0 + cj))

    out = pl.BlockSpec((tm, GC_W), lambda cj, i: (i, cj))
    return pl.pallas_call(
        body, name=name, grid=(CONV_CH // GC_W, nt),
        in_specs=[col(_GB0), col(_GC0), col(_XI0), halo(_GC0), halo(_XI0), nxt(_GB0), nxt(dy0), col(dy0),
                  pl.BlockSpec((3, GC_W), lambda cj, i: (0, cj))],
        out_specs=(out, out, out, pl.BlockSpec((3, GC_W), lambda cj, i: (0, cj))),
        out_shape=(jax.ShapeDtypeStruct((t, CONV_CH), MXU_DTYPE),) * 3 + (jax.ShapeDtypeStruct((3, CONV_CH), F32),),
        compiler_params=_cp("parallel", "arbitrary"))(proj, proj, proj, proj, proj, proj, dmix, dmix, conv_w)


_QKV_W = 3 * DN_W
_BA_COL = (4 * DN_W) // 128
_Z_COL = _QKV_W // DN_W


def gdn_prep_fwd(proj, conv_w, alog_row, dtb_row, name, tm=256):
    t = proj.shape[0]
    hb = tm // HALO
    qscale = DN_DIM ** -0.5

    def body(x_ref, xh_ref, w_ref, ba_ref, al_ref, dt_ref, q_ref, k_ref, v_ref, bg_ref):
        i = pl.program_id(0)
        for gi in range(3 * DN_HEADS):
            sl = slice(gi * DN_DIM, (gi + 1) * DN_DIM)
            xp = jnp.concatenate([jnp.where(i == 0, 0.0, xh_ref[:, sl]), x_ref[:, sl]], axis=0)
            c = w_ref[0:1, sl] * xp[HALO - 3:HALO - 3 + tm]
            for j in range(1, 4):
                c = c + w_ref[j:j + 1, sl] * xp[HALO - 3 + j:HALO - 3 + j + tm]
            s = c * _sigmoid(c)
            osl = slice((gi % DN_HEADS) * DN_DIM, (gi % DN_HEADS + 1) * DN_DIM)
            if gi < DN_HEADS:
                q_ref[:, osl] = s * lax.rsqrt(jnp.sum(s * s, axis=-1, keepdims=True) + EPS) * qscale
            elif gi < 2 * DN_HEADS:
                k_ref[:, osl] = s * lax.rsqrt(jnp.sum(s * s, axis=-1, keepdims=True) + EPS)
            else:
                v_ref[:, osl] = s
        ba = ba_ref[...]
        lane = lax.broadcasted_iota(jnp.int32, ba.shape, 1)
        gval = -jnp.exp(al_ref[...]) * _softplus(ba + dt_ref[...])
        bg_ref[...] = jnp.where(lane < DN_HEADS, _sigmoid(ba), jnp.where(lane < 2 * DN_HEADS, gval, 0.0))

    row = pl.BlockSpec((tm, DN_W), lambda i: (i, 0))
    one = pl.BlockSpec((1, 128), lambda i: (0, 0))
    return pl.pallas_call(
        body, name=name, grid=(t // tm,),
        in_specs=[pl.BlockSpec((tm, _QKV_W), lambda i: (i, 0)),
                  pl.BlockSpec((HALO, _QKV_W), lambda i: (jnp.maximum(i * hb - 1, 0), 0)),
                  pl.BlockSpec((4, _QKV_W), lambda i: (0, 0)),
                  pl.BlockSpec((tm, 128), lambda i: (i, _BA_COL)), one, one],
        out_specs=(row, row, row, pl.BlockSpec((tm, 128), lambda i: (i, 0))),
        out_shape=(jax.ShapeDtypeStruct((t, DN_W), F32),) * 3 + (jax.ShapeDtypeStruct((t, 128), F32),),
        compiler_params=_cp("parallel"))(proj, proj, conv_w, proj, alog_row, dtb_row)


def gdn_prep_bwd(proj, conv_w, alog_row, dtb_row, dq, dk, dv, dbg, name, tm=256):
    t = proj.shape[0]
    hb = tm // HALO
    nt = t // tm
    qscale = DN_DIM ** -0.5
    te = tm + HALO

    def body(x_ref, xh_ref, xn_ref, w_ref, ba_ref, al_ref, dt_ref, dq_ref, dk_ref, dv_ref,
             dqn_ref, dkn_ref, dvn_ref, dbg_ref, dx_ref, dba_ref, dw_ref, ddt_ref, dal_ref):
        i = pl.program_id(0)
        first = i == 0
        last = i == nt - 1
        dws = []
        for gi in range(3 * DN_HEADS):
            sl = slice(gi * DN_DIM, (gi + 1) * DN_DIM)
            osl = slice((gi % DN_HEADS) * DN_DIM, (gi % DN_HEADS + 1) * DN_DIM)
            xe = jnp.concatenate([jnp.where(first, 0.0, xh_ref[:, sl]), x_ref[:, sl], xn_ref[:, sl]], axis=0)
            c = w_ref[0:1, sl] * xe[HALO - 3:HALO - 3 + te]
            for j in range(1, 4):
                c = c + w_ref[j:j + 1, sl] * xe[HALO - 3 + j:HALO - 3 + j + te]
            sg = _sigmoid(c)
            s = c * sg
            d_ref, dn_ref = ((dq_ref, dqn_ref), (dk_ref, dkn_ref), (dv_ref, dvn_ref))[gi // DN_HEADS]
            dy = jnp.concatenate([d_ref[:, osl], jnp.where(last, 0.0, dn_ref[:, osl])], axis=0)
            if gi < 2 * DN_HEADS:
                r = lax.rsqrt(jnp.sum(s * s, axis=-1, keepdims=True) + EPS)
                sh = s * r
                ds = r * (dy - sh * jnp.sum(sh * dy, axis=-1, keepdims=True))
                if gi < DN_HEADS:
                    ds = ds * qscale
            else:
                ds = dy
            dc = ds * sg * (1.0 + c * (1.0 - sg))
            dcs = [dc[3 - j:3 - j + tm] for j in range(4)]
            dx = w_ref[0:1, sl] * dcs[0]
            for j in range(1, 4):
                dx = dx + w_ref[j:j + 1, sl] * dcs[j]
            dx_ref[:, sl] = dx.astype(dx_ref.dtype)
            x0 = x_ref[:, sl]
            dws.append(jnp.concatenate([jnp.sum(dcs[j] * x0, axis=0, keepdims=True) for j in range(4)], axis=0))
        dw = jnp.concatenate(dws, axis=-1)
        ba = ba_ref[...]
        dbgv = dbg_ref[...]
        lane = lax.broadcasted_iota(jnp.int32, ba.shape, 1)
        beta = _sigmoid(ba)
        ea = -jnp.exp(al_ref[...])
        zin = ba + dt_ref[...]
        is_b = lane < DN_HEADS
        is_a = (lane >= DN_HEADS) & (lane < 2 * DN_HEADS)
        da = jnp.where(is_a, dbgv * ea * _sigmoid(zin), 0.0)
        dba_ref[...] = jnp.where(is_b, dbgv * beta * (1.0 - beta), da).astype(dba_ref.dtype)
        ddt = jnp.sum(da, axis=0, keepdims=True)
        dal = jnp.sum(jnp.where(is_a, dbgv * ea * _softplus(zin), 0.0), axis=0, keepdims=True)

        @pl.when(first)
        def _():
            dw_ref[...] = dw
            ddt_ref[...] = ddt
            dal_ref[...] = dal

        @pl.when(i > 0)
        def _():
            dw_ref[...] += dw
            ddt_ref[...] += ddt
            dal_ref[...] += dal

    row = pl.BlockSpec((tm, DN_W), lambda i: (i, 0))
    nrow = pl.BlockSpec((HALO, DN_W), lambda i: (jnp.minimum((i + 1) * hb, t // HALO - 1), 0))
    one = pl.BlockSpec((1, 128), lambda i: (0, 0))
    return pl.pallas_call(
        body, name=name, grid=(nt,),
        in_specs=[pl.BlockSpec((tm, _QKV_W), lambda i: (i, 0)),
                  pl.BlockSpec((HALO, _QKV_W), lambda i: (jnp.maximum(i * hb - 1, 0), 0)),
                  pl.BlockSpec((HALO, _QKV_W), lambda i: (jnp.minimum((i + 1) * hb, t // HALO - 1), 0)),
                  pl.BlockSpec((4, _QKV_W), lambda i: (0, 0)),
                  pl.BlockSpec((tm, 128), lambda i: (i, _BA_COL)), one, one,
                  row, row, row, nrow, nrow, nrow, pl.BlockSpec((tm, 128), lambda i: (i, 0))],
        out_specs=(pl.BlockSpec((tm, _QKV_W), lambda i: (i, 0)), pl.BlockSpec((tm, 128), lambda i: (i, 0)),
                   pl.BlockSpec((4, _QKV_W), lambda i: (0, 0)), one, one),
        out_shape=(jax.ShapeDtypeStruct((t, _QKV_W), MXU_DTYPE), jax.ShapeDtypeStruct((t, 128), MXU_DTYPE),
                   jax.ShapeDtypeStruct((4, _QKV_W), F32), jax.ShapeDtypeStruct((1, 128), F32),
                   jax.ShapeDtypeStruct((1, 128), F32)),
        compiler_params=_cp("arbitrary"))(proj, proj, proj, conv_w, proj, alog_row, dtb_row, dq, dk, dv, dq, dk, dv, dbg)


def _chunk_masks():
    r = lax.broadcasted_iota(jnp.int32, (DN_CHUNK, DN_CHUNK), 0)
    c = lax.broadcasted_iota(jnp.int32, (DN_CHUNK, DN_CHUNK), 1)
    return r >= c, r > c


INV_PACK = 2


def _inv_unit_lower_many(mats):
    n = DN_CHUNK
    wide = INV_PACK * n
    r = lax.broadcasted_iota(jnp.int32, (wide, wide), 0)
    c = lax.broadcasted_iota(jnp.int32, (wide, wide), 1)
    same = (r & -n) == (c & -n)
    eye = jnp.where((r[:n] == (c[:n] & (n - 1))), 1.0, 0.0)

    def blockdiag(row):
        return jnp.where(same, jnp.concatenate([row] * INV_PACK, axis=0), 0.0)

    packs = [jnp.concatenate(mats[g:g + INV_PACK], axis=-1) for g in range(0, len(mats), INV_PACK)]
    xs = [eye - a for a in packs]
    pws = [_hi(a, blockdiag(a)) for a in packs]
    for step in range(5):
        if step < 4:
            both = [_hi(jnp.concatenate([x, pw], axis=0), blockdiag(pw)) for x, pw in zip(xs, pws)]
            xs = [x + b[:n] for x, b in zip(xs, both)]
            pws = [b[n:] for b in both]
        else:
            xs = [x + _hi(x, blockdiag(pw)) for x, pw in zip(xs, pws)]
    return [x[:, j * n:(j + 1) * n] for x in xs for j in range(INV_PACK)]


def _chunk_common(q, k, v, beta, gc, gcr, lower, strict):
    gam = jnp.exp(jnp.where(lower, gc - gcr, NEG))
    eg = jnp.exp(gc)
    gl = gc[DN_CHUNK - 1:DN_CHUNK, :]
    kdf = jnp.exp(gl - gc)
    kb = k * beta
    bmat = _mx_nt(kb, k)
    qmat = _mx_nt(q, k)
    return gam, eg, jnp.exp(gl), kdf, kb, bmat, qmat


DN_STEP = 4


def gdn_fwd(q, k, v, bg, name):
    t = q.shape[0]
    n_chunks = t // DN_CHUNK

    def body(q_ref, k_ref, v_ref, bg_ref, o_ref, sall_ref, tall_ref, s_ref):
        n = pl.program_id(0)

        @pl.when(n == 0)
        def _():
            s_ref[...] = jnp.zeros_like(s_ref)

        lower, strict = _chunk_masks()
        ltri = jnp.where(lower, 1.0, 0.0)
        hs = range(DN_HEADS)
        sl = [slice(h * DN_DIM, (h + 1) * DN_DIM) for h in hs]
        units = [(c, h) for c in range(DN_STEP) for h in hs]
        nu = range(len(units))
        rs = [slice(c * DN_CHUNK, (c + 1) * DN_CHUNK) for c in range(DN_STEP)]
        bgv = [bg_ref[rs[c], :] for c in range(DN_STEP)]
        gcs = [_hi(ltri, b) for b in bgv]
        gcs_t = [g.T for g in gcs]
        qh = [q_ref[rs[c], sl[h]] for c, h in units]
        kh = [k_ref[rs[c], sl[h]] for c, h in units]
        vh = [v_ref[rs[c], sl[h]] for c, h in units]
        beta = [bgv[c][:, h:h + 1] for c, h in units]
        com = [_chunk_common(qh[u], kh[u], vh[u], beta[u], gcs[c][:, DN_HEADS + h:DN_HEADS + h + 1],
                             gcs_t[c][DN_HEADS + h:DN_HEADS + h + 1, :], lower, strict) for u, (c, h) in enumerate(units)]
        gam, eg, dec, kdf, kb, bmat, qmat = zip(*com)
        tms = _inv_unit_lower_many([jnp.where(strict, bmat[u] * gam[u], 0.0) for u in nu])
        for u, (c, h) in enumerate(units):
            tall_ref[c, h] = tms[u]
        uw = [_hi(tms[u], jnp.concatenate([vh[u] * beta[u], kb[u] * eg[u]], axis=-1)) for u in nu]
        qd = [qh[u] * eg[u] for u in nu]
        pm = [qmat[u] * gam[u] for u in nu]
        kd = [kh[u] * kdf[u] for u in nu]
        st = [s_ref[h] for h in hs]
        for c in range(DN_STEP):
            us = [c * DN_HEADS + h for h in hs]
            for h in hs:
                sall_ref[c, h] = st[h]
            v_new = [uw[us[h]][:, :DN_DIM] - _mx(uw[us[h]][:, DN_DIM:], st[h]) for h in hs]
            o_st = [_mx(qd[us[h]], st[h]) for h in hs]
            o_in = [_mx(pm[us[h]], v_new[h]) for h in hs]
            s_up = [_mx_tn(kd[us[h]], v_new[h]) for h in hs]
            for h in hs:
                o_ref[rs[c], sl[h]] = o_st[h] + o_in[h]
            st = [st[h] * dec[us[h]] + s_up[h] for h in hs]
        for h in hs:
            s_ref[h] = st[h]

    rows = DN_STEP * DN_CHUNK
    row = pl.BlockSpec((rows, DN_W), lambda n: (n, 0))
    return pl.pallas_call(
        body, name=name, grid=(n_chunks // DN_STEP,),
        in_specs=[row, row, row, pl.BlockSpec((rows, 128), lambda n: (n, 0))],
        out_specs=(row, pl.BlockSpec((DN_STEP, DN_HEADS, DN_DIM, DN_DIM), lambda n: (n, 0, 0, 0)),
                   pl.BlockSpec((DN_STEP, DN_HEADS, DN_CHUNK, DN_CHUNK), lambda n: (n, 0, 0, 0))),
        out_shape=(jax.ShapeDtypeStruct((t, DN_W), F32),
                   jax.ShapeDtypeStruct((n_chunks, DN_HEADS, DN_DIM, DN_DIM), F32),
                   jax.ShapeDtypeStruct((n_chunks, DN_HEADS, DN_CHUNK, DN_CHUNK), F32)),
        scratch_shapes=[pltpu.VMEM((DN_HEADS, DN_DIM, DN_DIM), F32)],
        compiler_params=_cp("arbitrary"))(q, k, v, bg)


def gdn_bwd(q, k, v, bg, sall, tall, do, name):
    t = q.shape[0]
    n_chunks = t // DN_CHUNK

    def body(q_ref, k_ref, v_ref, bg_ref, sall_ref, tall_ref, do_ref, dq_ref, dk_ref, dv_ref, dbg_ref, ds_ref):
        n = pl.program_id(0)

        @pl.when(n == 0)
        def _():
            ds_ref[...] = jnp.zeros_like(ds_ref)

        lower, strict = _chunk_masks()
        ltri = jnp.where(lower, 1.0, 0.0)
        bgv = bg_ref[...]
        gcs = _hi(ltri, bgv)
        gcs_t = gcs.T
        lane = lax.broadcasted_iota(jnp.int32, (DN_CHUNK, 128), 1)
        rowi = lax.broadcasted_iota(jnp.int32, (DN_CHUNK, 1), 0)
        hs = range(DN_HEADS)
        each = lambda fn, *ls: [fn(*a) for a in zip(*ls)]
        rsum = lambda a: jnp.sum(a, axis=-1, keepdims=True)
        sl = [slice(h * DN_DIM, (h + 1) * DN_DIM) for h in hs]
        st = [sall_ref[0, h] for h in hs]
        tms = [tall_ref[0, h] for h in hs]
        dsn = [ds_ref[h] for h in hs]
        qh = [q_ref[:, sl[h]] for h in hs]
        kh = [k_ref[:, sl[h]] for h in hs]
        vh = [v_ref[:, sl[h]] for h in hs]
        doh = [do_ref[:, sl[h]] for h in hs]
        beta = [bgv[:, h:h + 1] for h in hs]
        com = [_chunk_common(qh[h], kh[h], vh[h], beta[h], gcs[:, DN_HEADS + h:DN_HEADS + h + 1],
                             gcs_t[DN_HEADS + h:DN_HEADS + h + 1, :], lower, strict) for h in hs]
        gam, eg, dec, kdf, kb, bmat, qmat = zip(*com)
        rhs_w = each(lambda a, b: a * b, kb, eg)
        uw = each(lambda t_, v_, b_, r_: _hi(t_, jnp.concatenate([v_ * b_, r_], axis=-1)), tms, vh, beta, rhs_w)
        qd = each(lambda a, b: a * b, qh, eg)
        kd = each(lambda a, b: a * b, kh, kdf)
        pmat = each(lambda a, b: a * b, qmat, gam)
        v_new = each(lambda uw_, s_: uw_[:, :DN_DIM] - _mx(uw_[:, DN_DIM:], s_), uw, st)
        dqd = each(_mx_nt, doh, st)
        ds_o = each(_mx_tn, qd, doh)
        dp = each(lambda d_, v_: jnp.where(lower, _mx_nt(d_, v_), 0.0), doh, v_new)
        dvn_o = each(_mx_tn, pmat, doh)
        ddec = each(lambda d_, s_: jnp.sum(rsum(d_ * s_), axis=0, keepdims=True), dsn, st)
        dkd = each(_mx_nt, v_new, dsn)
        dvn = each(lambda a, k_, d_: a + _mx(k_, d_), dvn_o, kd, dsn)
        dw = each(lambda d_, s_: -_mx_nt(d_, s_), dvn, st)
        ds_w = each(lambda uw_, d_: _mx_tn(uw_[:, DN_DIM:], d_), uw, dvn)
        for h in hs:
            ds_ref[h] = ds_o[h] + dec[h] * dsn[h] - ds_w[h]
        dr = each(lambda t_, a, b: _hi_tn(t_, jnp.concatenate([a, b], axis=-1)), tms, dvn, dw)
        da = each(lambda r_, uw_: jnp.where(strict, -_hi_nt(r_, uw_), 0.0), dr, uw)
        dru = [r_[:, :DN_DIM] for r_ in dr]
        drw = [r_[:, DN_DIM:] for r_ in dr]
        db = each(lambda a, b: a * b, da, gam)
        dq_m = each(lambda a, b: a * b, dp, gam)
        e = each(lambda a, bm, p_, qm, g_: (a * bm + p_ * qm) * g_, da, bmat, dp, qmat, gam)
        dkb = each(lambda b_, k_, r_, e_: _mx(b_, k_) + r_ * e_, db, kh, drw, eg)
        dk = each(lambda b_, kb_, m_, q_, d_, f_: _mx_tn(b_, kb_) + _mx_tn(m_, q_) + d_ * f_, db, kb, dq_m, qh, dkd, kdf)
        dq = each(lambda m_, k_, d_, e_: _mx(m_, k_) + d_ * e_, dq_m, kh, dqd, eg)
        tk = each(lambda a, b: rsum(a * b), dkd, kd)
        dbeta_all = jnp.zeros((DN_CHUNK, 128), F32)
        dgc_all = jnp.zeros((DN_CHUNK, 128), F32)
        for h in hs:
            dgc = (jnp.sum(e[h], axis=1, keepdims=True) - jnp.sum(e[h].T, axis=1, keepdims=True)
                   + rsum(dqd[h] * qd[h]) - tk[h] + rsum(drw[h] * rhs_w[h]))
            dgl = jnp.sum(tk[h], axis=0, keepdims=True) + ddec[h] * dec[h]
            dgc = dgc + jnp.where(rowi == DN_CHUNK - 1, dgl, 0.0)
            dbeta = rsum(dru[h] * vh[h]) + rsum(dkb[h] * kh[h])
            dq_ref[:, sl[h]] = dq[h]
            dk_ref[:, sl[h]] = dk[h] + dkb[h] * beta[h]
            dv_ref[:, sl[h]] = dru[h] * beta[h]
            dbeta_all = jnp.where(lane == h, dbeta, dbeta_all)
            dgc_all = jnp.where(lane == DN_HEADS + h, dgc, dgc_all)
        dbg_ref[...] = dbeta_all + _hi_tn(ltri, dgc_all)

    rev = lambda n: (n_chunks - 1 - n, 0)
    row = pl.BlockSpec((DN_CHUNK, DN_W), rev)
    small = pl.BlockSpec((DN_CHUNK, 128), rev)
    return pl.pallas_call(
        body, name=name, grid=(n_chunks,),
        in_specs=[row, row, row, small,
                  pl.BlockSpec((1, DN_HEADS, DN_DIM, DN_DIM), lambda n: (n_chunks - 1 - n, 0, 0, 0)),
                  pl.BlockSpec((1, DN_HEADS, DN_CHUNK, DN_CHUNK), lambda n: (n_chunks - 1 - n, 0, 0, 0)), row],
        out_specs=(row, row, row, small),
        out_shape=(jax.ShapeDtypeStruct((t, DN_W), F32),) * 3 + (jax.ShapeDtypeStruct((t, 128), F32),),
        scratch_shapes=[pltpu.VMEM((DN_HEADS, DN_DIM, DN_DIM), F32)],
        compiler_params=_cp("arbitrary"))(q, k, v, bg, sall, tall, do)


def gdn_out_fwd(o, proj, o_gain, name, tm=256):
    t = o.shape[0]

    def body(o_ref, z_ref, g_ref, y_ref, yt_ref):
        for h in range(DN_HEADS):
            sl = slice(h * DN_DIM, (h + 1) * DN_DIM)
            ov, zv = o_ref[:, sl], z_ref[:, sl]
            r = lax.rsqrt(jnp.mean(ov * ov, axis=-1, keepdims=True) + EPS)
            y = ov * r * g_ref[...] * (zv * _sigmoid(zv))
            y_ref[:, sl] = y.astype(y_ref.dtype)
            yt_ref[sl, :] = y.T.astype(yt_ref.dtype)

    row = pl.BlockSpec((tm, DN_W), lambda i: (i, 0))
    return pl.pallas_call(
        body, name=name, grid=(t // tm,),
        in_specs=[row, pl.BlockSpec((tm, DN_W), lambda i: (i, _Z_COL)), pl.BlockSpec((1, DN_DIM), lambda i: (0, 0))],
        out_specs=(row, pl.BlockSpec((DN_W, tm), lambda i: (0, i))),
        out_shape=(jax.ShapeDtypeStruct((t, DN_W), MXU_DTYPE), jax.ShapeDtypeStruct((DN_W, t), MXU_DTYPE)),
        compiler_params=_cp("parallel"))(o, proj, o_gain)


def gdn_out_bwd(o, proj, o_gain, dy, name, tm=256):
    t = o.shape[0]

    def body(o_ref, z_ref, g_ref, dy_ref, do_ref, dz_ref, dg_ref):
        i = pl.program_id(0)
        dg = jnp.zeros((1, DN_DIM), F32)
        for h in range(DN_HEADS):
            sl = slice(h * DN_DIM, (h + 1) * DN_DIM)
            ov, zv, dyv = o_ref[:, sl], z_ref[:, sl], dy_ref[:, sl]
            r = lax.rsqrt(jnp.mean(ov * ov, axis=-1, keepdims=True) + EPS)
            oh = ov * r
            sg = _sigmoid(zv)
            dz_ref[:, sl] = (dyv * oh * g_ref[...] * sg * (1.0 + zv * (1.0 - sg))).astype(dz_ref.dtype)
            don = dyv * (zv * sg)
            dg = dg + jnp.sum(don * oh, axis=0, keepdims=True)
            doh = don * g_ref[...]
            do_ref[:, sl] = r * (doh - oh * jnp.mean(doh * oh, axis=-1, keepdims=True))

        @pl.when(i == 0)
        def _():
            dg_ref[...] = dg

        @pl.when(i > 0)
        def _():
            dg_ref[...] += dg

    row = pl.BlockSpec((tm, DN_W), lambda i: (i, 0))
    one = pl.BlockSpec((1, DN_DIM), lambda i: (0, 0))
    return pl.pallas_call(
        body, name=name, grid=(t // tm,),
        in_specs=[row, pl.BlockSpec((tm, DN_W), lambda i: (i, _Z_COL)), one, row],
        out_specs=(row, row, one),
        out_shape=(jax.ShapeDtypeStruct((t, DN_W), F32), jax.ShapeDtypeStruct((t, DN_W), MXU_DTYPE),
                   jax.ShapeDtypeStruct((1, DN_DIM), F32)),
        compiler_params=_cp("arbitrary"))(o, proj, o_gain, dy)


def _peer(k):
    x, y, c = lax.axis_index("x"), lax.axis_index("y"), lax.axis_index("c")
    px = 1 - x if k & 4 else x
    py = 1 - y if k & 2 else y
    pc = 1 - c if k & 1 else c
    return (px, py, pc), 4 * px + 2 * py + pc


_HBM = pl.BlockSpec(memory_space=pltpu.HBM)
_SEM = pl.BlockSpec(memory_space=pltpu.SEMAPHORE)
_DATAFLOW = pltpu.SideEffectType.DATAFLOW_SIDE_EFFECTING
N_PEER = N_DEV - 1


def send_start(srcs, name, scatter, after):
    na = len(srcs)
    ns = (2 * N_PEER + 1) * na
    lands = [lax.empty((N_DEV,) + (s.shape[1:] if scatter else s.shape), s.dtype) for s in srcs]
    extra = [] if after is None else [after]

    def body(*refs):
        src_refs, land_refs = refs[:na], refs[na:2 * na]
        sems = refs[2 * na + len(extra):2 * na + len(extra) + ns]
        land_out, token = refs[-1 - na:-1], refs[-1]
        _, me = _peer(0)
        for a in range(na):
            pltpu.make_async_copy(src_refs[a].at[me] if scatter else src_refs[a], land_out[a].at[me],
                                  sems[2 * N_PEER * na + a]).start()
        for k in range(1, N_DEV):
            peer, pid = _peer(k)
            for a in range(na):
                pltpu.make_async_remote_copy(
                    src_ref=src_refs[a].at[pid] if scatter else src_refs[a], dst_ref=land_refs[a].at[me],
                    send_sem=sems[2 * (a * N_PEER + k - 1)], recv_sem=sems[2 * (a * N_PEER + k - 1) + 1],
                    device_id=peer, device_id_type=MESH).start()
        token[...] = jnp.zeros_like(token)

    hbm = lambda arrs: tuple(pltpu.HBM(a.shape, a.dtype) for a in arrs)
    outs = pl.pallas_call(
        body, name=name,
        out_shape=(pltpu.SemaphoreType.DMA(()),) * ns + hbm(srcs) + hbm(lands) + (jax.ShapeDtypeStruct((8, 128), F32),),
        in_specs=[_HBM] * (2 * na) + [pl.BlockSpec(memory_space=pl.ANY)] * len(extra),
        out_specs=(_SEM,) * ns + (_HBM,) * (2 * na) + (pl.BlockSpec(memory_space=pltpu.VMEM),),
        input_output_aliases={i: ns + i for i in range(2 * na)},
        compiler_params=pltpu.CompilerParams(has_side_effects=_DATAFLOW),
    )(*[pltpu.with_memory_space_constraint(a, pltpu.HBM) for a in list(srcs) + lands], *extra)
    return outs[:ns], outs[ns:ns + na], outs[ns + na:ns + 2 * na], outs[-1]


def send_wait(sems, srcs_thru, lands_thru, name, scatter, after):
    na = len(srcs_thru)
    ns = (2 * N_PEER + 1) * na

    def body(*refs):
        src_refs, land_refs, sm = refs[:na], refs[na:2 * na], refs[2 * na:2 * na + ns]
        _, me = _peer(0)
        for a in range(na):
            pltpu.make_async_copy(src_refs[a].at[me] if scatter else src_refs[a], land_refs[a].at[me],
                                  sm[2 * N_PEER * na + a]).wait()
        for k in range(1, N_DEV):
            peer, pid = _peer(k)
            for a in range(na):
                cp = pltpu.make_async_remote_copy(
                    src_ref=src_refs[a].at[pid] if scatter else src_refs[a], dst_ref=land_refs[a].at[pid],
                    send_sem=sm[2 * (a * N_PEER + k - 1)], recv_sem=sm[2 * (a * N_PEER + k - 1) + 1],
                    device_id=peer, device_id_type=MESH)
                cp.wait_send()
                cp.wait_recv()

    hbm = lambda arrs: tuple(pltpu.HBM(a.shape, a.dtype) for a in arrs)
    outs = pl.pallas_call(
        body, name=name, out_shape=hbm(srcs_thru) + hbm(lands_thru),
        in_specs=[_HBM] * (2 * na) + [_SEM] * ns + [pl.BlockSpec(memory_space=pl.ANY)], out_specs=(_HBM,) * (2 * na),
        input_output_aliases={i: i for i in range(2 * na)},
        compiler_params=pltpu.CompilerParams(has_side_effects=_DATAFLOW),
    )(*srcs_thru, *lands_thru, *sems, after)
    return outs[na:]


def _adamw(w, g, m, v):
    m = ADAM_B1 * m + (1.0 - ADAM_B1) * g
    v = ADAM_B2 * v + (1.0 - ADAM_B2) * (g * g)
    m_hat = m / (1.0 - ADAM_B1 ** ADAM_STEP)
    v_hat = v / (1.0 - ADAM_B2 ** ADAM_STEP)
    return -ADAM_LR * (m_hat / (jnp.sqrt(v_hat) + ADAM_EPS) + ADAM_WD * w), m, v


def adam_sum(w, pieces, m, v, name, layer=0, into=None):
    nl, r, c = w.shape
    tr = r
    for cand in (256, 128, 64, 32, 16, 8):
        if r % cand == 0:
            tr = cand
            break

    def body(w_ref, p_ref, m_ref, v_ref, *rest):
        g_ref, d_ref, nm_ref, nv_ref = rest[-4:]
        g = p_ref[0].astype(F32)
        for s in range(1, N_DEV):
            g = g + p_ref[s].astype(F32)
        g_ref[0] = g
        d_ref[0], nm_ref[0], nv_ref[0] = _adamw(w_ref[0], g, m_ref[0], v_ref[0])

    row = pl.BlockSpec((1, tr, c), lambda i: (layer, i, 0))
    out = jax.ShapeDtypeStruct((nl, r, c), F32)
    extra = [] if into is None else list(into)
    return pl.pallas_call(
        body, name=name, grid=(r // tr,),
        in_specs=[row, pl.BlockSpec((N_DEV, tr, c), lambda i: (0, i, 0)), row, row]
        + [pl.BlockSpec(memory_space=pl.ANY)] * len(extra),
        out_specs=(row,) * 4, out_shape=(out,) * 4,
        input_output_aliases={4 + i: i for i in range(len(extra))},
        compiler_params=_cp("parallel"))(w, pieces, m, v, *extra)


def sum_rows(gathered, name):
    _, r, c = gathered.shape

    def body(p_ref, o_ref):
        g = p_ref[0]
        for s in range(1, N_DEV):
            g = g + p_ref[s]
        o_ref[...] = g

    return pl.pallas_call(body, name=name, out_shape=jax.ShapeDtypeStruct((r, c), F32))(gathered)


def adam_small(w, g, m, v, name):
    def body(w_ref, g_ref, m_ref, v_ref, d_ref, nm_ref, nv_ref):
        d_ref[...], nm_ref[...], nv_ref[...] = _adamw(w_ref[...], g_ref[...], m_ref[...], v_ref[...])

    out = jax.ShapeDtypeStruct(w.shape, F32)
    return pl.pallas_call(body, name=name, out_shape=(out,) * 3)(w, g, m, v)


def _rope_tables(t):
    inv_freq = 10000.0 ** (-jnp.arange(0, HEAD_DIM, 2, dtype=F32) / HEAD_DIM)
    ang = jnp.arange(t, dtype=F32)[:, None] * inv_freq[None, :]
    cos, sin = jnp.cos(ang), jnp.sin(ang)
    return jnp.concatenate([cos, cos], axis=-1), jnp.concatenate([sin, sin], axis=-1)


def _lane_row(vec8):
    return jnp.pad(vec8.reshape(1, DN_HEADS), ((0, 0), (DN_HEADS, 128 - 2 * DN_HEADS)))


def _ffn_bwd(x, norm_g, w_gu, w_d, saved, dy, tag, after=None):
    ft, gu, at = saved
    dgu = ffn_dact(dy, w_d, gu, f"{tag}_d_gate_up", after=after)
    dwd = mm_at(at, dy, f"{tag}_dw_down")
    nj = D_FF // GU_TILE
    dwgu = mm_at(ft, dgu, f"{tag}_dw_gate_up", transposed=True, tn=GU_TILE, row_block=lambda q: (q % 2) * nj + q // 2)
    dx, dg = mm_rms_bwd(dgu, w_gu, x, norm_g, dy, f"{tag}_d_norm", chunks=_GU_CHUNKS)
    return dx, dwgu, dwd, dg


def local_step(x, target, small, weights_of, grads_out, after=None):
    t = x.shape[0]
    cosf, sinf = _rope_tables(t)
    alog_row, dtb_row = _lane_row(small["odd_a_log"]), _lane_row(small["odd_dt_bias"])

    h0, h0t = rms_fwd(x, small["even_norm"], "even_norm", after=after)
    we = weights_of("even", h0)
    small = {**small, **we.get("small", {})}
    proj0 = mm_nt(h0, we["w_in"], "even_in_proj")
    qr, kr = qk_prep_fwd(proj0, small["even_q_gain"], small["even_k_gain"], cosf, sinf, "even_qk_prep")
    y_attn, mix0, mix0_t, lse = swa_fwd(qr, kr, proj0, small["even_sinks"], "even_swa")
    mix0, mix0_t = gconv_fwd(proj0, small["even_conv_w"], mix0, mix0_t, "even_gconv")
    we = {**we, **weights_of("even_out", mix0)}
    x1, f0, f0t = mm_nn_res_norm(mix0, we["w_out"], x, small["ffn_norm0"], "even_out_proj")
    w0 = weights_of("ffn0", x1)
    gu0, a0, a0t = ffn_up(f0, w0["gate_up"], "ffn0_gate_up")
    ffn0 = (f0t, gu0, a0t)
    x2, h1, h1t = mm_nn_res_norm(a0, w0["down"], x1, small["odd_norm"], "ffn0_down")

    wo = weights_of("odd", x2)
    proj1 = mm_nt(h1, wo["w_in"], "odd_in_proj")
    qn, kn, vs, bg = gdn_prep_fwd(proj1, small["odd_conv_w"], alog_row, dtb_row, "odd_prep")
    o, sall, tall = gdn_fwd(qn, kn, vs, bg, "odd_delta_rule")
    og, ogt = gdn_out_fwd(o, proj1, small["odd_o_gain"], "odd_gate_norm")
    x3, f1, f1t = mm_nn_res_norm(og, wo["w_out"], x2, small["ffn_norm1"], "odd_out_proj")
    w1 = weights_of("ffn1", x3)
    gu1, a1, a1t = ffn_up(f1, w1["gate_up"], "ffn1_gate_up")
    ffn1 = (f1t, gu1, a1t)
    dy, loss_row = mm_nn_res_loss(a1, w1["down"], x3, target, "ffn1_down_loss")

    gs = {}
    dx3, dwgu, dwd, gs["ffn_norm1"] = _ffn_bwd(x3, small["ffn_norm1"], w1["gate_up"], w1["down"], ffn1, dy, "ffn1")
    tok = grads_out("ffn1", {"gate_up": dwgu, "down": dwd})

    dog = mm_nt(dx3, wo["w_out"], "odd_d_gated", after=tok)
    dwo = mm_at(ogt, dx3, "odd_dw_out")
    do, dz, gs["odd_o_gain"] = gdn_out_bwd(o, proj1, small["odd_o_gain"], dog, "odd_d_gate_norm")
    dqn, dkn, dvs, dbg = gdn_bwd(qn, kn, vs, bg, sall, tall, do, "odd_d_delta_rule")
    dqkv, dba, gs["odd_conv_w"], ddt_row, dal_row = gdn_prep_bwd(
        proj1, small["odd_conv_w"], alog_row, dtb_row, dqn, dkn, dvs, dbg, "odd_d_prep")
    gs["odd_dt_bias"] = ddt_row[:, DN_HEADS:2 * DN_HEADS]
    gs["odd_a_log"] = dal_row[:, DN_HEADS:2 * DN_HEADS]
    dproj1 = jnp.concatenate([dqkv, dz, dba], axis=-1)
    dwi = mm_at(h1t, dproj1, "odd_dw_in", transposed=True)
    dx2, gs["odd_norm"] = mm_rms_bwd(dproj1, wo["w_in"], x2, small["odd_norm"], dx3, "odd_d_norm")
    tok = grads_out("odd", {"w_in": dwi, "w_out": dwo})

    dx1, dwgu, dwd, gs["ffn_norm0"] = _ffn_bwd(x1, small["ffn_norm0"], w0["gate_up"], w0["down"], ffn0, dx2, "ffn0",
                                               after=tok)
    tok = grads_out("ffn0", {"gate_up": dwgu, "down": dwd})

    dmix = mm_nt(dx1, we["w_out"], "even_d_mix", after=tok)
    dwo = mm_at(mix0_t, dx1, "even_dw_out")
    dqr, dkr, dv, gs["even_sinks"] = swa_bwd(qr, kr, proj0, small["even_sinks"], y_attn, lse, dmix, "even_d_swa")
    dqk, gs["even_q_gain"], gs["even_k_gain"] = qk_prep_bwd(
        proj0, small["even_q_gain"], small["even_k_gain"], cosf, sinf, dqr, dkr, "even_d_qk_prep")
    dgb, dgc, dxi, gs["even_conv_w"] = gconv_bwd(proj0, small["even_conv_w"], dmix, "even_d_gconv")
    dproj0 = jnp.concatenate([dqk, dv, dgb, dgc, dxi], axis=-1)
    dwi = mm_at(h0t, dproj0, "even_dw_in", transposed=True)
    tok = grads_out("even", {"w_in": dwi, "w_out": dwo})
    grad_x, gs["even_norm"] = mm_rms_bwd(dproj0, we["w_in"], x, small["even_norm"], dx1, "even_d_norm", after=tok)
    return loss_row, grad_x, gs


_SMALL_ORDER = ("even_norm", "even_q_gain", "even_k_gain", "even_sinks", "odd_a_log", "odd_dt_bias", "odd_o_gain",
                "ffn_norm0", "ffn_norm1", "odd_norm", "even_conv_w", "odd_conv_w")
_SMALL_SIZE = {"even_norm": 1024, "even_q_gain": 64, "even_k_gain": 64, "even_sinks": 8, "odd_a_log": 8,
               "odd_dt_bias": 8, "odd_o_gain": 128, "ffn_norm0": 1024, "ffn_norm1": 1024, "odd_norm": 1024,
               "even_conv_w": 3 * 512, "odd_conv_w": 4 * 3072}
_N_REPL = 9


def _pack_rows(vals):
    flat = jnp.concatenate([v.reshape(-1) for v in vals])
    pad = (-flat.shape[0]) % 1024
    return jnp.pad(flat, (0, pad)).reshape(-1, 128)


def _my_block(full, size, axis):
    me = 4 * lax.axis_index("x") + 2 * lax.axis_index("y") + lax.axis_index("c")
    return lax.dynamic_slice_in_dim(full, me * size, size, axis=axis)


def kernel(x, even_norm, even_w_in, even_q_gain, even_k_gain, even_sinks, even_conv_w, even_w_out, odd_norm, odd_w_in, odd_conv_w, odd_a_log, odd_dt_bias, odd_o_gain, odd_w_out, ffn_norm, ffn_w_gate_up, ffn_w_down, loss_target, m_even_norm, m_even_w_in, m_even_q_gain, m_even_k_gain, m_even_sinks, m_even_conv_w, m_even_w_out, m_odd_norm, m_odd_w_in, m_odd_conv_w, m_odd_a_log, m_odd_dt_bias, m_odd_o_gain, m_odd_w_out, m_ffn_norm, m_ffn_w_gate_up, m_ffn_w_down, v_even_norm, v_even_w_in, v_even_q_gain, v_even_k_gain, v_even_sinks, v_even_conv_w, v_even_w_out, v_odd_norm, v_odd_w_in, v_odd_conv_w, v_odd_a_log, v_odd_dt_bias, v_odd_o_gain, v_odd_w_out, v_ffn_norm, v_ffn_w_gate_up, v_ffn_w_down):
    t = x.shape[1]
    d = D_MODEL

    me = 4 * lax.axis_index("x") + 2 * lax.axis_index("y") + lax.axis_index("c")
    tr = lambda a: jnp.swapaxes(a, 1, 2)
    shard = {
        "even": {"w_in": tr(even_w_in)[0], "w_out": even_w_out[0]},
        "ffn0": {"gate_up": tr(ffn_w_gate_up)[0], "down": ffn_w_down[0]},
        "odd": {"w_in": tr(odd_w_in)[0], "w_out": odd_w_out[0]},
        "ffn1": {"gate_up": tr(ffn_w_gate_up)[1], "down": ffn_w_down[1]},
    }
    given = {
        ("even", "w_in"): ("even_w_in", even_w_in, m_even_w_in, v_even_w_in, 0),
        ("even", "w_out"): ("even_w_out", even_w_out, m_even_w_out, v_even_w_out, 0),
        ("odd", "w_in"): ("odd_w_in", odd_w_in, m_odd_w_in, v_odd_w_in, 0),
        ("odd", "w_out"): ("odd_w_out", odd_w_out, m_odd_w_out, v_odd_w_out, 0),
        ("ffn0", "gate_up"): ("ffn_w_gate_up", ffn_w_gate_up, m_ffn_w_gate_up, v_ffn_w_gate_up, 0),
        ("ffn1", "gate_up"): ("ffn_w_gate_up", ffn_w_gate_up, m_ffn_w_gate_up, v_ffn_w_gate_up, 1),
        ("ffn0", "down"): ("ffn_w_down", ffn_w_down, m_ffn_w_down, v_ffn_w_down, 0),
        ("ffn1", "down"): ("ffn_w_down", ffn_w_down, m_ffn_w_down, v_ffn_w_down, 1),
    }

    def whole(group, parts):
        col, row = tuple(shard[group])
        w_col = parts[0].reshape(-1, d)
        if group == "odd":
            w_col = jnp.pad(w_col, ((0, ODD_IN_PAD - ODD_IN_W), (0, 0)))
        return {col: w_col, row: parts[1].reshape(-1, d)}

    wire = {g: [a.astype(MXU_DTYPE) for a in shard[g].values()] for g in shard}
    wire["even_out"] = [wire["even"].pop()]
    wire["even"].append(_pack_rows([odd_norm, even_conv_w, odd_conv_w]))
    gathers, tok = {}, None
    for g in ("even", "even_out", "ffn0", "odd", "ffn1"):
        sems, srcs_thru, lands_thru, tok = send_start(wire[g], f"gather_{g}_start", False, tok)
        gathers[g] = (sems, srcs_thru, lands_thru)
    o1 = d // N_DEV
    o2 = o1 + 3 * CONV_CH // N_DEV

    def weights_of(group, after):
        lands = send_wait(*gathers[group], f"gather_{group}_wait", False, after)
        if group == "even_out":
            return {"w_out": lands[0].reshape(-1, d)}
        if group != "even":
            return whole(group, lands)
        sg = lands[1].reshape(N_DEV, -1)
        return {"w_in": lands[0].reshape(-1, d), "small": {
            "odd_norm": sg[:, :o1].reshape(1, d),
            "even_conv_w": sg[:, o1:o2].reshape(N_DEV, 3, CONV_CH // N_DEV).transpose(1, 0, 2).reshape(3, CONV_CH),
            "odd_conv_w": sg[:, o2:o2 + 4 * _QKV_W // N_DEV].reshape(N_DEV, 4, _QKV_W // N_DEV)
            .transpose(1, 0, 2).reshape(4, _QKV_W)}}

    sent = {}

    def grads_out(group, dws):
        col, row = tuple(shard[group])
        n_cols = N_DEV * shard[group][col].shape[0]
        pieces = [dws[col][:n_cols].reshape((N_DEV,) + shard[group][col].shape),
                  dws[row].reshape((N_DEV,) + shard[group][row].shape)]
        sems, srcs_thru, lands_thru, token = send_start(pieces, f"exchange_{group}_start", True, None)
        sent[group] = (sems, srcs_thru, lands_thru, pieces)
        return token

    small = {
        "even_norm": even_norm, "even_q_gain": even_q_gain, "even_k_gain": even_k_gain, "even_sinks": even_sinks,
        "odd_a_log": odd_a_log.reshape(-1), "odd_dt_bias": odd_dt_bias.reshape(-1), "odd_o_gain": odd_o_gain,
        "ffn_norm0": ffn_norm[0:1], "ffn_norm1": ffn_norm[1:2],
    }

    loss_row, grad_x, gs = local_step(x.reshape(t, d), loss_target.reshape(t, d), small, weights_of, grads_out, after=tok)

    rows = _pack_rows([gs[n] for n in _SMALL_ORDER] + [loss_row[:, 0:1]])
    small_sent = send_start([rows], "gather_small_grads_start", False, None)

    res, behind = {}, small_sent[3]
    for g in ("ffn1", "odd", "ffn0", "even"):
        sems, srcs_thru, lands_thru, pieces = sent[g]
        lands = send_wait(sems, srcs_thru, lands_thru, f"exchange_{g}_wait", True, behind)
        for i, (key, pcs) in enumerate(zip(shard[g], lands)):
            name, w_, m_, v_, layer = given[g, key]
            view = tr if i == 0 else (lambda a: a)
            res[name] = adam_sum(view(w_), pcs, view(m_), view(v_), f"adamw_{g}_{key}", layer=layer, into=res.get(name))
        behind = res[name][0]
    for name in ("even_w_in", "odd_w_in", "ffn_w_gate_up"):
        res[name] = tuple(tr(a) for a in res[name])

    (rows_g,) = send_wait(*small_sent[:3], "gather_small_grads_wait", False, behind)
    tot = sum_rows(rows_g, "sum_small_grads").reshape(-1)
    off, sgrad = 0, {}
    for n in _SMALL_ORDER:
        sgrad[n] = tot[off:off + _SMALL_SIZE[n]]
        off += _SMALL_SIZE[n]
    loss = tot[off]

    repl = _SMALL_ORDER[:_N_REPL]
    repl_w = {"even_norm": even_norm, "even_q_gain": even_q_gain, "even_k_gain": even_k_gain, "even_sinks": even_sinks,
              "odd_a_log": odd_a_log, "odd_dt_bias": odd_dt_bias, "odd_o_gain": odd_o_gain,
              "ffn_norm0": ffn_norm[0], "ffn_norm1": ffn_norm[1]}
    repl_m = {"even_norm": m_even_norm, "even_q_gain": m_even_q_gain, "even_k_gain": m_even_k_gain,
              "even_sinks": m_even_sinks, "odd_a_log": m_odd_a_log, "odd_dt_bias": m_odd_dt_bias,
              "odd_o_gain": m_odd_o_gain, "ffn_norm0": m_ffn_norm[0], "ffn_norm1": m_ffn_norm[1]}
    repl_v = {"even_norm": v_even_norm, "even_q_gain": v_even_q_gain, "even_k_gain": v_even_k_gain,
              "even_sinks": v_even_sinks, "odd_a_log": v_odd_a_log, "odd_dt_bias": v_odd_dt_bias,
              "odd_o_gain": v_odd_o_gain, "ffn_norm0": v_ffn_norm[0], "ffn_norm1": v_ffn_norm[1]}
    pk = lambda dct: _pack_rows([dct[n] for n in repl])
    pd_, pm_, pv_ = adam_small(pk(repl_w), pk(sgrad), pk(repl_m), pk(repl_v), "adamw_replicated")
    sres = {}
    off = 0
    for n in repl:
        sz = _SMALL_SIZE[n]
        sres[n] = (sgrad[n], pd_.reshape(-1)[off:off + sz], pm_.reshape(-1)[off:off + sz], pv_.reshape(-1)[off:off + sz])
        off += sz
    g_on = _my_block(sgrad["odd_norm"].reshape(1, d), d // N_DEV, 1)
    g_ec = _my_block(sgrad["even_conv_w"].reshape(3, CONV_CH), CONV_CH // N_DEV, 1)
    g_oc = _my_block(sgrad["odd_conv_w"].reshape(4, _QKV_W), _QKV_W // N_DEV, 1)
    shard_w = _pack_rows([odd_norm, even_conv_w, odd_conv_w])
    sd_, sm_, sv_ = adam_small(shard_w, _pack_rows([g_on, g_ec, g_oc]),
                               _pack_rows([m_odd_norm, m_even_conv_w, m_odd_conv_w]),
                               _pack_rows([v_odd_norm, v_even_conv_w, v_odd_conv_w]), "adamw_sharded_small")
    off = 0
    for n, gfull, like in (("odd_norm", g_on, odd_norm), ("even_conv_w", g_ec, even_conv_w), ("odd_conv_w", g_oc, odd_conv_w)):
        sz = like.size
        sres[n] = (gfull, sd_.reshape(-1)[off:off + sz], sm_.reshape(-1)[off:off + sz], sv_.reshape(-1)[off:off + sz])
        off += sz

    def small_out(name, like, kind):
        if name == "ffn_norm":
            return jnp.stack([sres["ffn_norm0"][kind], sres["ffn_norm1"][kind]]).reshape(like.shape)
        return sres[name][kind].reshape(like.shape)

    order = (("even_norm", even_norm), ("even_w_in", even_w_in), ("even_q_gain", even_q_gain),
             ("even_k_gain", even_k_gain), ("even_sinks", even_sinks), ("even_conv_w", even_conv_w),
             ("even_w_out", even_w_out), ("odd_norm", odd_norm), ("odd_w_in", odd_w_in), ("odd_conv_w", odd_conv_w),
             ("odd_a_log", odd_a_log), ("odd_dt_bias", odd_dt_bias), ("odd_o_gain", odd_o_gain),
             ("odd_w_out", odd_w_out), ("ffn_norm", ffn_norm), ("ffn_w_gate_up", ffn_w_gate_up),
             ("ffn_w_down", ffn_w_down))
    outs = [loss, grad_x.reshape(x.shape)]
    for kind in range(4):
        for name, like in order:
            outs.append(res[name][kind] if name in res else small_out(name, like, kind))
    return tuple(outs)
```

```python
import jax
import jax.numpy as jnp
import numpy as np
from jax import lax
from jax.experimental import pallas as pl
from jax.experimental.pallas import tpu as pltpu

F32 = jnp.float32
MXU_DTYPE = jnp.bfloat16
HI = lax.Precision.HIGH
EPS = 1e-6
N_DEV = 8
D_MODEL = 1024
HEAD_DIM = 64
ATTN_HEADS = 8
KV_HEADS = 2
ATTN_BLOCK = 128
Q_W = 512
KV_W = 128
CONV_CH = 512
EVEN_IN_W = 2304
DN_HEADS = 8
DN_DIM = 128
DN_W = 1024
DN_CHUNK = 64
ODD_IN_W = 4112
ODD_IN_PAD = 4224
D_FF = 2816
NEG = -1e30
VMEM_LIMIT = 56 * 1024 * 1024
ADAM_LR, ADAM_B1, ADAM_B2, ADAM_EPS, ADAM_WD, ADAM_STEP = 0.001, 0.9, 0.999, 1e-08, 0.01, 10
MESH = pl.DeviceIdType.MESH


def _cp(*sem):
    return pltpu.CompilerParams(dimension_semantics=sem, vmem_limit_bytes=VMEM_LIMIT)


def _pick(n, cap):
    best = 128
    for t in range(128, cap + 1, 128):
        if n % t == 0:
            best = t
    return best


def _mx(a, b):
    return jnp.dot(a.astype(MXU_DTYPE), b.astype(MXU_DTYPE), preferred_element_type=F32)


def _mx_nt(a, b):
    return lax.dot_general(a.astype(MXU_DTYPE), b.astype(MXU_DTYPE), (((1,), (1,)), ((), ())),
                           preferred_element_type=F32)


def _mx_tn(a, b):
    return lax.dot_general(a.astype(MXU_DTYPE), b.astype(MXU_DTYPE), (((0,), (0,)), ((), ())),
                           preferred_element_type=F32)


def _hi(a, b):
    return jnp.dot(a, b, precision=HI, preferred_element_type=F32)


def _hi_nt(a, b):
    return lax.dot_general(a, b, (((1,), (1,)), ((), ())), precision=HI, preferred_element_type=F32)


def _hi_tn(a, b):
    return lax.dot_general(a, b, (((0,), (0,)), ((), ())), precision=HI, preferred_element_type=F32)


def _sigmoid(x):
    return 0.5 * jnp.tanh(0.5 * x) + 0.5


def _softplus(x):
    return jnp.maximum(x, 0.0) + jnp.log(1.0 + jnp.exp(-jnp.abs(x)))


def mm_nn_res_norm(a, b, res, g, name, tm=512):
    t, k = a.shape
    d = b.shape[1]
    tm = min(tm, t)

    def body(a_ref, b_ref, res_ref, g_ref, y_ref, h_ref, ht_ref):
        y = res_ref[...] + _mx(a_ref[...], b_ref[...])
        y_ref[...] = y
        h = y * lax.rsqrt(jnp.mean(y * y, axis=-1, keepdims=True) + EPS) * g_ref[...]
        h_ref[...] = h.astype(h_ref.dtype)
        ht_ref[...] = h.T.astype(ht_ref.dtype)

    row = pl.BlockSpec((tm, d), lambda i: (i, 0))
    return pl.pallas_call(
        body, name=name, grid=(t // tm,),
        in_specs=[pl.BlockSpec((tm, k), lambda i: (i, 0)), pl.BlockSpec((k, d), lambda i: (0, 0)), row,
                  pl.BlockSpec((1, d), lambda i: (0, 0))],
        out_specs=(row, row, pl.BlockSpec((d, tm), lambda i: (0, i))),
        out_shape=(jax.ShapeDtypeStruct((t, d), F32), jax.ShapeDtypeStruct((t, d), MXU_DTYPE),
                   jax.ShapeDtypeStruct((d, t), MXU_DTYPE)),
        compiler_params=_cp("parallel"))(a, b, res, g)


def mm_nt(a, b, name, out_dtype=F32, tm=1024, after=None):
    m, k = a.shape
    n, _ = b.shape
    tn = _pick(n, 512 if k > 3000 else 1536)
    tm = min(tm, m)

    def body(a_ref, b_ref, *rest):
        o_ref = rest[-1]
        o_ref[...] = _mx_nt(a_ref[...], b_ref[...]).astype(o_ref.dtype)

    in_specs = [pl.BlockSpec((tm, k), lambda j, i: (i, 0)), pl.BlockSpec((tn, k), lambda j, i: (j, 0))]
    args = [a, b]
    if after is not None:
        in_specs.append(pl.BlockSpec(memory_space=pl.ANY))
        args.append(after)
    return pl.pallas_call(
        body, name=name, grid=(n // tn, m // tm), in_specs=in_specs,
        out_specs=pl.BlockSpec((tm, tn), lambda j, i: (i, j)),
        out_shape=jax.ShapeDtypeStruct((m, n), out_dtype), compiler_params=_cp("parallel", "parallel"))(*args)


def mm_at(at, b, name, tk=1024, transposed=False, tn=None, row_block=None):
    m, kk = at.shape
    _, n = b.shape
    tm, tn, tk = _pick(m, 1408), tn or _pick(n, 2816), min(tk, kk)
    nk = kk // tk

    def body(a_ref, b_ref, o_ref, acc_ref):
        k = pl.program_id(2)
        p = _mx(a_ref[...], b_ref[...])
        acc = jnp.where(k == 0, p, acc_ref[...] + p)
        acc_ref[...] = acc

        @pl.when(k == nk - 1)
        def _():
            o_ref[...] = (acc.T if transposed else acc).astype(o_ref.dtype)

    if transposed:
        rb = row_block or (lambda j: j)
        out_spec = pl.BlockSpec((tn, tm), lambda i, j, k: (rb(j), i))
        out_shape = jax.ShapeDtypeStruct((n, m), MXU_DTYPE)
    else:
        out_spec = pl.BlockSpec((tm, tn), lambda i, j, k: (i, j))
        out_shape = jax.ShapeDtypeStruct((m, n), MXU_DTYPE)
    return pl.pallas_call(
        body, name=name, grid=(m // tm, n // tn, nk),
        in_specs=[pl.BlockSpec((tm, tk), lambda i, j, k: (i, k)), pl.BlockSpec((tk, tn), lambda i, j, k: (k, j))],
        out_specs=out_spec, out_shape=out_shape, scratch_shapes=[pltpu.VMEM((tm, tn), F32)],
        compiler_params=_cp("parallel", "parallel", "arbitrary"))(at, b)


def rms_fwd(x, g, name, tm=512, after=None):
    t, d = x.shape

    def body(x_ref, g_ref, *rest):
        o_ref, ot_ref = rest[-2:]
        xv = x_ref[...]
        r = lax.rsqrt(jnp.mean(xv * xv, axis=-1, keepdims=True) + EPS)
        h = xv * r * g_ref[...]
        o_ref[...] = h.astype(o_ref.dtype)
        ot_ref[...] = h.T.astype(ot_ref.dtype)

    in_specs = [pl.BlockSpec((tm, d), lambda i: (i, 0)), pl.BlockSpec((1, d), lambda i: (0, 0))]
    args = [x, g]
    if after is not None:
        in_specs.append(pl.BlockSpec(memory_space=pl.ANY))
        args.append(after)
    return pl.pallas_call(
        body, name=name, grid=(t // tm,), in_specs=in_specs,
        out_specs=(pl.BlockSpec((tm, d), lambda i: (i, 0)), pl.BlockSpec((d, tm), lambda i: (0, i))),
        out_shape=(jax.ShapeDtypeStruct((t, d), MXU_DTYPE), jax.ShapeDtypeStruct((d, t), MXU_DTYPE)),
        compiler_params=_cp("parallel"))(*args)


def mm_rms_bwd(a, bt, x, g, dres, name, tm=512, after=None, chunks=None):
    t, k = a.shape
    d = bt.shape[1]
    tm = min(tm, t)
    chunks = chunks or ((0, 0, k),)

    def body(a_ref, b_ref, x_ref, g_ref, dres_ref, *rest):
        dx_ref, dg_ref = rest[-2:]
        dhv = None
        for ca, cb, size in chunks:
            part = _mx(a_ref[:, ca:ca + size], b_ref[cb:cb + size, :])
            dhv = part if dhv is None else dhv + part
        xv = x_ref[...]
        r = lax.rsqrt(jnp.mean(xv * xv, axis=-1, keepdims=True) + EPS)
        xh = xv * r
        dxh = dhv * g_ref[...]
        dx_ref[...] = dres_ref[...] + r * (dxh - xh * jnp.mean(dxh * xh, axis=-1, keepdims=True))
        part = jnp.sum(dhv * xh, axis=0, keepdims=True)
        dg_ref[...] = jnp.where(pl.program_id(0) == 0, part, dg_ref[...] + part)

    row = pl.BlockSpec((tm, d), lambda i: (i, 0))
    one = pl.BlockSpec((1, d), lambda i: (0, 0))
    in_specs = [pl.BlockSpec((tm, k), lambda i: (i, 0)), pl.BlockSpec((k, d), lambda i: (0, 0)), row, one, row]
    args = [a, bt, x, g, dres]
    if after is not None:
        in_specs.append(pl.BlockSpec(memory_space=pl.ANY))
        args.append(after)
    return pl.pallas_call(
        body, name=name, grid=(t // tm,), in_specs=in_specs, out_specs=(row, one),
        out_shape=(jax.ShapeDtypeStruct((t, d), F32), jax.ShapeDtypeStruct((1, d), F32)),
        compiler_params=_cp("arbitrary"))(*args)


GU_TILE = 1408


def ffn_up(f, wt, name, tm=512):
    t, d = f.shape
    nj = D_FF // GU_TILE

    def body(f_ref, wg_ref, wu_ref, gu_ref, a_ref, at_ref):
        g = _mx_nt(f_ref[...], wg_ref[...])
        u = _mx_nt(f_ref[...], wu_ref[...])
        sg = _sigmoid(g)
        gs = g * sg
        gu_ref[:, :GU_TILE] = (u * (sg + gs - gs * sg)).astype(gu_ref.dtype)
        gu_ref[:, GU_TILE:] = gs.astype(gu_ref.dtype)
        act = gs * u
        a_ref[...] = act.astype(a_ref.dtype)
        at_ref[...] = act.T.astype(at_ref.dtype)

    return pl.pallas_call(
        body, name=name, grid=(nj, t // tm),
        in_specs=[pl.BlockSpec((tm, d), lambda j, i: (i, 0)), pl.BlockSpec((GU_TILE, d), lambda j, i: (j, 0)),
                  pl.BlockSpec((GU_TILE, d), lambda j, i: (nj + j, 0))],
        out_specs=(pl.BlockSpec((tm, 2 * GU_TILE), lambda j, i: (i, j)), pl.BlockSpec((tm, GU_TILE), lambda j, i: (i, j)),
                   pl.BlockSpec((GU_TILE, tm), lambda j, i: (j, i))),
        out_shape=(jax.ShapeDtypeStruct((t, 2 * D_FF), MXU_DTYPE), jax.ShapeDtypeStruct((t, D_FF), MXU_DTYPE),
                   jax.ShapeDtypeStruct((D_FF, t), MXU_DTYPE)),
        compiler_params=_cp("parallel", "parallel"))(f, wt, wt)


_GU_CHUNKS = tuple((q * GU_TILE, ((q % 2) * (D_FF // GU_TILE) + q // 2) * GU_TILE, GU_TILE)
                   for q in range(2 * D_FF // GU_TILE))


def ffn_dact(dy, w_d, gu, name, tm=512, after=None):
    t, d = dy.shape

    def body(dy_ref, w_ref, gu_ref, *rest):
        o_ref = rest[-1]
        da = _mx_nt(dy_ref[...], w_ref[...])
        o_ref[:, :GU_TILE] = (da * gu_ref[:, :GU_TILE]).astype(o_ref.dtype)
        o_ref[:, GU_TILE:] = (da * gu_ref[:, GU_TILE:]).astype(o_ref.dtype)

    in_specs = [pl.BlockSpec((tm, d), lambda j, i: (i, 0)), pl.BlockSpec((GU_TILE, d), lambda j, i: (j, 0)),
                pl.BlockSpec((tm, 2 * GU_TILE), lambda j, i: (i, j))]
    args = [dy, w_d, gu]
    if after is not None:
        in_specs.append(pl.BlockSpec(memory_space=pl.ANY))
        args.append(after)
    return pl.pallas_call(
        body, name=name, grid=(D_FF // GU_TILE, t // tm), in_specs=in_specs,
        out_specs=pl.BlockSpec((tm, 2 * GU_TILE), lambda j, i: (i, j)),
        out_shape=jax.ShapeDtypeStruct((t, 2 * D_FF), MXU_DTYPE), compiler_params=_cp("parallel", "parallel"))(*args)


def mm_nn_res_loss(a, b, res, target, name, tm=512):
    t, k = a.shape
    d = b.shape[1]
    tm = min(tm, t)

    def body(a_ref, b_ref, res_ref, t_ref, dy_ref, l_ref):
        e = res_ref[...] + _mx(a_ref[...], b_ref[...]) - t_ref[...]
        dy_ref[...] = e * (1.0 / d)
        part = jnp.zeros((1, 128), F32) + 0.5 * jnp.sum(jnp.mean(e * e, axis=-1, keepdims=True), axis=0, keepdims=True)
        l_ref[...] = jnp.where(pl.program_id(0) == 0, part, l_ref[...] + part)

    row = pl.BlockSpec((tm, d), lambda i: (i, 0))
    return pl.pallas_call(
        body, name=name, grid=(t // tm,),
        in_specs=[pl.BlockSpec((tm, k), lambda i: (i, 0)), pl.BlockSpec((k, d), lambda i: (0, 0)), row, row],
        out_specs=(row, pl.BlockSpec((1, 128), lambda i: (0, 0))),
        out_shape=(jax.ShapeDtypeStruct((t, d), F32), jax.ShapeDtypeStruct((1, 128), F32)),
        compiler_params=_cp("arbitrary"))(a, b, res, target)


QK_W = Q_W + KV_W
_QK_TILE = 256


def _qk_mats():
    idx = np.arange(_QK_TILE)
    half = HEAD_DIM // 2
    same = (idx[:, None] // HEAD_DIM) == (idx[None, :] // HEAD_DIM)
    lo = (idx % HEAD_DIM) < half
    rot = np.where((idx[:, None] == idx[None, :] + half) & lo[None, :], -1.0, 0.0)
    rot = rot + np.where((idx[:, None] == idx[None, :] - half) & ~lo[None, :], 1.0, 0.0)
    return jnp.asarray(same, F32), jnp.asarray(rot, F32)


def _qk_rows(q_gain, k_gain, cosf, sinf):
    gain = jnp.concatenate([q_gain] * ATTN_HEADS + [k_gain] * KV_HEADS, axis=-1)
    return gain, jnp.concatenate([cosf, cosf], axis=-1), jnp.concatenate([sinf, sinf], axis=-1)


def _qk_tiles(a, mat, transposed=False):
    outs = []
    for c0 in range(0, QK_W, _QK_TILE):
        w = min(_QK_TILE, QK_W - c0)
        mt = (mat.T if transposed else mat)[:w, :w].astype(MXU_DTYPE)
        at = a[:, c0:c0 + w]
        hi = at.astype(MXU_DTYPE)
        lo = (at - hi.astype(F32)).astype(MXU_DTYPE)
        outs.append(jnp.dot(hi, mt, preferred_element_type=F32) + jnp.dot(lo, mt, preferred_element_type=F32))
    return jnp.concatenate(outs, axis=-1)


def qk_prep_fwd(proj, q_gain, k_gain, cosf, sinf, name, tm=256):
    t = proj.shape[0]
    gmat, rmat = _qk_mats()
    gain, c2, s2 = _qk_rows(q_gain, k_gain, cosf, sinf)
    rep = QK_W // 128

    def body(p_ref, g_ref, c_ref, s_ref, gm_ref, rm_ref, q_ref, k_ref):
        x = p_ref[...]
        r = lax.rsqrt(_qk_tiles(x * x, gm_ref[...]) * (1.0 / HEAD_DIM) + EPS)
        xn = x * r * g_ref[...]
        c = jnp.concatenate([c_ref[...]] * rep, axis=-1)
        s = jnp.concatenate([s_ref[...]] * rep, axis=-1)
        out = xn * c + _qk_tiles(xn, rm_ref[...]) * s
        q_ref[...] = out[:, :Q_W]
        k_ref[...] = out[:, Q_W:]

    full = pl.BlockSpec((_QK_TILE, _QK_TILE), lambda i: (0, 0))
    tab = pl.BlockSpec((tm, 128), lambda i: (i, 0))
    return pl.pallas_call(
        body, name=name, grid=(t // tm,),
        in_specs=[pl.BlockSpec((tm, QK_W), lambda i: (i, 0)), pl.BlockSpec((1, QK_W), lambda i: (0, 0)), tab, tab,
                  full, full],
        out_specs=(pl.BlockSpec((tm, Q_W), lambda i: (i, 0)), pl.BlockSpec((tm, KV_W), lambda i: (i, 0))),
        out_shape=(jax.ShapeDtypeStruct((t, Q_W), F32), jax.ShapeDtypeStruct((t, KV_W), F32)),
        compiler_params=_cp("parallel"))(proj, gain, c2, s2, gmat, rmat)


def qk_prep_bwd(proj, q_gain, k_gain, cosf, sinf, dq, dk, name, tm=256):
    t = proj.shape[0]
    gmat, rmat = _qk_mats()
    gain, c2, s2 = _qk_rows(q_gain, k_gain, cosf, sinf)
    rep = QK_W // 128
    lanes = np.arange(QK_W)[:, None]
    fold = jnp.asarray(lanes % HEAD_DIM + np.where(lanes >= Q_W, HEAD_DIM, 0) == np.arange(128)[None, :], F32)

    def body(p_ref, g_ref, c_ref, s_ref, gm_ref, rm_ref, f_ref, dq_ref, dk_ref, o_ref, dg_ref):
        x = p_ref[...]
        r = lax.rsqrt(_qk_tiles(x * x, gm_ref[...]) * (1.0 / HEAD_DIM) + EPS)
        xh = x * r
        c = jnp.concatenate([c_ref[...]] * rep, axis=-1)
        s = jnp.concatenate([s_ref[...]] * rep, axis=-1)
        dout = jnp.concatenate([dq_ref[...], dk_ref[...]], axis=-1)
        dxn = dout * c + _qk_tiles(dout * s, rm_ref[...], transposed=True)
        part = _hi(jnp.sum(dxn * xh, axis=0, keepdims=True), f_ref[...])
        dxh = dxn * g_ref[...]
        mean = _qk_tiles(dxh * xh, gm_ref[...]) * (1.0 / HEAD_DIM)
        o_ref[...] = (r * (dxh - xh * mean)).astype(o_ref.dtype)
        dg_ref[...] = jnp.where(pl.program_id(0) == 0, part, dg_ref[...] + part)

    full = pl.BlockSpec((_QK_TILE, _QK_TILE), lambda i: (0, 0))
    tab = pl.BlockSpec((tm, 128), lambda i: (i, 0))
    dqk, dg = pl.pallas_call(
        body, name=name, grid=(t // tm,),
        in_specs=[pl.BlockSpec((tm, QK_W), lambda i: (i, 0)), pl.BlockSpec((1, QK_W), lambda i: (0, 0)), tab, tab,
                  full, full, pl.BlockSpec((QK_W, 128), lambda i: (0, 0)),
                  pl.BlockSpec((tm, Q_W), lambda i: (i, 0)), pl.BlockSpec((tm, KV_W), lambda i: (i, 0))],
        out_specs=(pl.BlockSpec((tm, QK_W), lambda i: (i, 0)), pl.BlockSpec((1, 128), lambda i: (0, 0))),
        out_shape=(jax.ShapeDtypeStruct((t, QK_W), MXU_DTYPE), jax.ShapeDtypeStruct((1, 128), F32)),
        compiler_params=_cp("arbitrary"))(proj, gain, c2, s2, gmat, rmat, fold, dq, dk)
    return dqk, dg[:, :HEAD_DIM], dg[:, HEAD_DIM:]


def _swa_valid(n, grp):
    qi = lax.broadcasted_iota(jnp.int32, (grp * ATTN_BLOCK, 2 * ATTN_BLOCK), 0) & (ATTN_BLOCK - 1)
    kj = lax.broadcasted_iota(jnp.int32, (grp * ATTN_BLOCK, 2 * ATTN_BLOCK), 1)
    diff = qi + ATTN_BLOCK - kj
    return (diff >= 0) & (diff < ATTN_BLOCK) & (n * ATTN_BLOCK - ATTN_BLOCK + kj >= 0)


def _stack_heads(ref, g, grp):
    return jnp.concatenate([ref[:, (g * grp + j) * HEAD_DIM:(g * grp + j + 1) * HEAD_DIM] for j in range(grp)], axis=0)


def _stack_sinks(s_ref, g, grp):
    return jnp.concatenate([jnp.zeros((ATTN_BLOCK, 1), F32) + s_ref[0:1, g * grp + j:g * grp + j + 1]
                            for j in range(grp)], axis=0)


def swa_fwd(q, k, proj, sinks, name):
    t = q.shape[0]
    nb = t // ATTN_BLOCK
    scale = HEAD_DIM ** -0.5
    grp = ATTN_HEADS // KV_HEADS

    def body(q_ref, kc_ref, kp_ref, vc_ref, vp_ref, s_ref, y_ref, mix_ref, yt_ref, lse_ref):
        n = pl.program_id(0)
        valid = _swa_valid(n, grp)
        kk = jnp.concatenate([kp_ref[...], kc_ref[...]], axis=0).astype(MXU_DTYPE)
        vv = jnp.concatenate([vp_ref[...], vc_ref[...]], axis=0).astype(MXU_DTYPE)
        lane = lax.broadcasted_iota(jnp.int32, (ATTN_BLOCK, ATTN_HEADS), 1)
        gs = range(KV_HEADS)
        qg = [_stack_heads(q_ref, g, grp) for g in gs]
        sink = [_stack_sinks(s_ref, g, grp) for g in gs]
        sc = [jnp.where(valid, _mx_nt(qg[g], kk[:, g * HEAD_DIM:(g + 1) * HEAD_DIM]) * scale, NEG) for g in gs]
        m = [jnp.maximum(jnp.max(sc[g], axis=-1, keepdims=True), sink[g]) for g in gs]
        e = [jnp.exp(sc[g] - m[g]) for g in gs]
        den = [jnp.sum(e[g], axis=-1, keepdims=True) + jnp.exp(sink[g] - m[g]) for g in gs]
        og = [_mx(e[g] / den[g], vv[:, g * HEAD_DIM:(g + 1) * HEAD_DIM]) for g in gs]
        lg = [m[g] + jnp.log(den[g]) for g in gs]
        lse = jnp.zeros((ATTN_BLOCK, ATTN_HEADS), F32)
        outs = []
        for h in range(ATTN_HEADS):
            rows = slice((h % grp) * ATTN_BLOCK, (h % grp + 1) * ATTN_BLOCK)
            outs.append(og[h // grp][rows])
            lse = jnp.where(lane == h, lg[h // grp][rows], lse)
        y = jnp.concatenate(outs, axis=-1)
        y_ref[...] = y
        mix_ref[...] = y.astype(mix_ref.dtype)
        yt_ref[...] = y.T.astype(yt_ref.dtype)
        lse_ref[...] = lse

    cur = lambda n: (n, 0)
    prev = lambda n: (jnp.maximum(n - 1, 0), 0)
    vcol = (Q_W + KV_W) // KV_W
    return pl.pallas_call(
        body, name=name, grid=(nb,),
        in_specs=[pl.BlockSpec((ATTN_BLOCK, Q_W), cur), pl.BlockSpec((ATTN_BLOCK, KV_W), cur),
                  pl.BlockSpec((ATTN_BLOCK, KV_W), prev),
                  pl.BlockSpec((ATTN_BLOCK, KV_W), lambda n: (n, vcol)),
                  pl.BlockSpec((ATTN_BLOCK, KV_W), lambda n: (jnp.maximum(n - 1, 0), vcol)),
                  pl.BlockSpec((1, ATTN_HEADS), lambda n: (0, 0))],
        out_specs=(pl.BlockSpec((ATTN_BLOCK, Q_W), cur), pl.BlockSpec((ATTN_BLOCK, Q_W), cur),
                   pl.BlockSpec((Q_W, ATTN_BLOCK), lambda n: (0, n)), pl.BlockSpec((ATTN_BLOCK, ATTN_HEADS), cur)),
        out_shape=(jax.ShapeDtypeStruct((t, Q_W), F32), jax.ShapeDtypeStruct((t, Q_W + CONV_CH), MXU_DTYPE),
                   jax.ShapeDtypeStruct((Q_W + CONV_CH, t), MXU_DTYPE), jax.ShapeDtypeStruct((t, ATTN_HEADS), F32)),
        compiler_params=_cp("parallel"))(q, k, k, proj, proj, sinks)


def swa_bwd(q, k, proj, sinks, y, lse, dmix, name):
    t = q.shape[0]
    nb = t // ATTN_BLOCK
    scale = HEAD_DIM ** -0.5
    grp = ATTN_HEADS // KV_HEADS

    def body(q_ref, kc_ref, kp_ref, vc_ref, vp_ref, s_ref, y_ref, lse_ref, dy_ref,
             dq_ref, dk_ref, dv_ref, ds_ref, dkc, dvc):
        n = pl.program_id(0)

        @pl.when(n == 0)
        def _():
            dkc[...] = jnp.zeros_like(dkc)
            dvc[...] = jnp.zeros_like(dvc)
            ds_ref[...] = jnp.zeros_like(ds_ref)

        @pl.when(n < nb)
        def _():
            valid = _swa_valid(n, grp)
            kk = jnp.concatenate([kp_ref[...], kc_ref[...]], axis=0).astype(MXU_DTYPE)
            vv = jnp.concatenate([vp_ref[...], vc_ref[...]], axis=0).astype(MXU_DTYPE)
            lane = lax.broadcasted_iota(jnp.int32, (1, ATTN_HEADS), 1)
            gs = range(KV_HEADS)
            kg = [kk[:, g * HEAD_DIM:(g + 1) * HEAD_DIM] for g in gs]
            vg = [vv[:, g * HEAD_DIM:(g + 1) * HEAD_DIM] for g in gs]
            qg = [_stack_heads(q_ref, g, grp).astype(MXU_DTYPE) for g in gs]
            dog = [_stack_heads(dy_ref, g, grp) for g in gs]
            og = [_stack_heads(y_ref, g, grp) for g in gs]
            lg = [jnp.concatenate([lse_ref[:, g * grp + j:g * grp + j + 1] for j in range(grp)], axis=0) for g in gs]
            sink = [_stack_sinks(s_ref, g, grp) for g in gs]
            sc = [jnp.where(valid, _mx_nt(qg[g], kg[g]) * scale, NEG) for g in gs]
            p = [jnp.exp(sc[g] - lg[g]) for g in gs]
            delta = [jnp.sum(dog[g] * og[g], axis=-1, keepdims=True) for g in gs]
            ds = [p[g] * (_mx_nt(dog[g], vg[g]) - delta[g]) for g in gs]
            dqg = [_mx(ds[g], kg[g]) * scale for g in gs]
            dkf = jnp.concatenate([_mx_tn(ds[g], qg[g]) * scale for g in gs], axis=-1)
            dvf = jnp.concatenate([_mx_tn(p[g], dog[g]) for g in gs], axis=-1)
            dsk = [jnp.exp(sink[g] - lg[g]) * delta[g] for g in gs]
            dsink = jnp.zeros((1, ATTN_HEADS), F32)
            dqs = []
            for h in range(ATTN_HEADS):
                rows = slice((h % grp) * ATTN_BLOCK, (h % grp + 1) * ATTN_BLOCK)
                dqs.append(dqg[h // grp][rows])
                dsink = jnp.where(lane == h, -jnp.sum(dsk[h // grp][rows], axis=0, keepdims=True), dsink)
            dq_ref[...] = jnp.concatenate(dqs, axis=-1)
            dk_ref[...] = dkc[...] + dkf[:ATTN_BLOCK]
            dv_ref[...] = (dvc[...] + dvf[:ATTN_BLOCK]).astype(dv_ref.dtype)
            dkc[...] = dkf[ATTN_BLOCK:]
            dvc[...] = dvf[ATTN_BLOCK:]
            ds_ref[...] += dsink

        @pl.when(n == nb)
        def _():
            dk_ref[...] = dkc[...]
            dv_ref[...] = dvc[...].astype(dv_ref.dtype)

    cur = lambda n: (jnp.minimum(n, nb - 1), 0)
    prev = lambda n: (jnp.clip(n - 1, 0, nb - 1), 0)
    vcol = (Q_W + KV_W) // KV_W
    return pl.pallas_call(
        body, name=name, grid=(nb + 1,),
        in_specs=[pl.BlockSpec((ATTN_BLOCK, Q_W), cur), pl.BlockSpec((ATTN_BLOCK, KV_W), cur),
                  pl.BlockSpec((ATTN_BLOCK, KV_W), prev),
                  pl.BlockSpec((ATTN_BLOCK, KV_W), lambda n: (jnp.minimum(n, nb - 1), vcol)),
                  pl.BlockSpec((ATTN_BLOCK, KV_W), lambda n: (jnp.clip(n - 1, 0, nb - 1), vcol)),
                  pl.BlockSpec((1, ATTN_HEADS), lambda n: (0, 0)),
                  pl.BlockSpec((ATTN_BLOCK, Q_W), cur), pl.BlockSpec((ATTN_BLOCK, ATTN_HEADS), cur),
                  pl.BlockSpec((ATTN_BLOCK, Q_W), cur)],
        out_specs=(pl.BlockSpec((ATTN_BLOCK, Q_W), cur), pl.BlockSpec((ATTN_BLOCK, KV_W), prev),
                   pl.BlockSpec((ATTN_BLOCK, KV_W), prev), pl.BlockSpec((1, ATTN_HEADS), lambda n: (0, 0))),
        out_shape=(jax.ShapeDtypeStruct((t, Q_W), F32), jax.ShapeDtypeStruct((t, KV_W), F32),
                   jax.ShapeDtypeStruct((t, KV_W), MXU_DTYPE), jax.ShapeDtypeStruct((1, ATTN_HEADS), F32)),
        scratch_shapes=[pltpu.VMEM((ATTN_BLOCK, KV_W), F32), pltpu.VMEM((ATTN_BLOCK, KV_W), F32)],
        compiler_params=_cp("arbitrary"))(q, k, k, proj, proj, sinks, y, lse, dmix)


GC_W = 256
_GB0, _GC0, _XI0 = 768 // GC_W, 1280 // GC_W, 1792 // GC_W
HALO = 8


def gconv_fwd(proj, conv_w, mix, mix_t, name, tm=512):
    t = proj.shape[0]
    hb = tm // HALO
    half = Q_W // GC_W

    def body(gb_ref, gc_ref, xi_ref, gch_ref, xih_ref, w_ref, mix_in, mixt_in, y_ref, yt_ref):
        i = pl.program_id(1)
        u = gc_ref[...] * xi_ref[...]
        uh = jnp.where(i == 0, 0.0, gch_ref[...] * xih_ref[...])
        up = jnp.concatenate([uh, u], axis=0)
        cv = w_ref[0:1, :] * up[HALO - 2:HALO - 2 + tm]
        cv = cv + w_ref[1:2, :] * up[HALO - 1:HALO - 1 + tm]
        cv = cv + w_ref[2:3, :] * u
        y = gb_ref[...] * cv
        y_ref[...] = y.astype(y_ref.dtype)
        yt_ref[...] = y.T.astype(yt_ref.dtype)

    def col(c0):
        return pl.BlockSpec((tm, GC_W), lambda cj, i: (i, c0 + cj))

    def halo(c0):
        return pl.BlockSpec((HALO, GC_W), lambda cj, i: (jnp.maximum(i * hb - 1, 0), c0 + cj))

    return pl.pallas_call(
        body, name=name, grid=(CONV_CH // GC_W, t // tm),
        in_specs=[col(_GB0), col(_GC0), col(_XI0), halo(_GC0), halo(_XI0),
                  pl.BlockSpec((3, GC_W), lambda cj, i: (0, cj)),
                  pl.BlockSpec(memory_space=pl.ANY), pl.BlockSpec(memory_space=pl.ANY)],
        out_specs=(pl.BlockSpec((tm, GC_W), lambda cj, i: (i, half + cj)),
                   pl.BlockSpec((GC_W, tm), lambda cj, i: (half + cj, i))),
        out_shape=(jax.ShapeDtypeStruct(mix.shape, mix.dtype), jax.ShapeDtypeStruct(mix_t.shape, mix_t.dtype)),
        input_output_aliases={6: 0, 7: 1},
        compiler_params=_cp("parallel", "parallel"))(proj, proj, proj, proj, proj, conv_w, mix, mix_t)


def gconv_bwd(proj, conv_w, dmix, name, tm=512):
    t = proj.shape[0]
    hb = tm // HALO
    nt = t // tm
    dy0 = Q_W // GC_W

    def body(gb_ref, gc_ref, xi_ref, gch_ref, xih_ref, gbn_ref, dyn_ref, dy_ref, w_ref,
             dgb_ref, dgc_ref, dxi_ref, dw_ref):
        i = pl.program_id(1)
        gc, xi, gb, dy = gc_ref[...], xi_ref[...], gb_ref[...], dy_ref[...]
        u = gc * xi
        uh = jnp.where(i == 0, 0.0, gch_ref[...] * xih_ref[...])
        up = jnp.concatenate([uh, u], axis=0)
        u2 = up[HALO - 2:HALO - 2 + tm]
        u1 = up[HALO - 1:HALO - 1 + tm]
        cv = w_ref[0:1, :] * u2 + w_ref[1:2, :] * u1 + w_ref[2:3, :] * u
        dgb_ref[...] = (dy * cv).astype(dgb_ref.dtype)
        dcv = dy * gb
        dcvn = jnp.where(i == nt - 1, 0.0, dyn_ref[...] * gbn_ref[...])
        dcvp = jnp.concatenate([dcv, dcvn], axis=0)
        du = w_ref[0:1, :] * dcvp[2:2 + tm] + w_ref[1:2, :] * dcvp[1:1 + tm] + w_ref[2:3, :] * dcv
        dgc_ref[...] = (du * xi).astype(dgc_ref.dtype)
        dxi_ref[...] = (du * gc).astype(dxi_ref.dtype)
        dw = jnp.concatenate([jnp.sum(dcv * u2, axis=0, keepdims=True), jnp.sum(dcv * u1, axis=0, keepdims=True),
                              jnp.sum(dcv * u, axis=0, keepdims=True)], axis=0)

        @pl.when(i == 0)
        def _():
            dw_ref[...] = dw

        @pl.when(i > 0)
        def _():
            dw_ref[...] += dw

    def col(c0):
        return pl.BlockSpec((tm, GC_W), lambda cj, i: (i, c0 + cj))

    def halo(c0):
        return pl.BlockSpec((HALO, GC_W), lambda cj, i: (jnp.maximum(i * hb - 1, 0), c0 + cj))

    def nxt(c0):
        return pl.BlockSpec((HALO, GC_W), lambda cj, i: (jnp.minimum((i + 1) * hb, t // HALO - 1), c0 + cj))

    out = pl.BlockSpec((tm, GC_W), lambda cj, i: (i, cj))
    return pl.pallas_call(
        body, name=name, grid=(CONV_CH // GC_W, nt),
        in_specs=[col(_GB0), col(_GC0), col(_XI0), halo(_GC0), halo(_XI0), nxt(_GB0), nxt(dy0), col(dy0),
                  pl.BlockSpec((3, GC_W), lambda cj, i: (0, cj))],
        out_specs=(out, out, out, pl.BlockSpec((3, GC_W), lambda cj, i: (0, cj))),
        out_shape=(jax.ShapeDtypeStruct((t, CONV_CH), MXU_DTYPE),) * 3 + (jax.ShapeDtypeStruct((3, CONV_CH), F32),),
        compiler_params=_cp("parallel", "arbitrary"))(proj, proj, proj, proj, proj, proj, dmix, dmix, conv_w)


_QKV_W = 3 * DN_W
_BA_COL = (4 * DN_W) // 128
_Z_COL = _QKV_W // DN_W


def gdn_prep_fwd(proj, conv_w, alog_row, dtb_row, name, tm=256):
    t = proj.shape[0]
    hb = tm // HALO
    qscale = DN_DIM ** -0.5

    def body(x_ref, xh_ref, w_ref, ba_ref, al_ref, dt_ref, q_ref, k_ref, v_ref, bg_ref):
        i = pl.program_id(0)
        for gi in range(3 * DN_HEADS):
            sl = slice(gi * DN_DIM, (gi + 1) * DN_DIM)
            xp = jnp.concatenate([jnp.where(i == 0, 0.0, xh_ref[:, sl]), x_ref[:, sl]], axis=0)
            c = w_ref[0:1, sl] * xp[HALO - 3:HALO - 3 + tm]
            for j in range(1, 4):
                c = c + w_ref[j:j + 1, sl] * xp[HALO - 3 + j:HALO - 3 + j + tm]
            s = c * _sigmoid(c)
            osl = slice((gi % DN_HEADS) * DN_DIM, (gi % DN_HEADS + 1) * DN_DIM)
            if gi < DN_HEADS:
                q_ref[:, osl] = s * lax.rsqrt(jnp.sum(s * s, axis=-1, keepdims=True) + EPS) * qscale
            elif gi < 2 * DN_HEADS:
                k_ref[:, osl] = s * lax.rsqrt(jnp.sum(s * s, axis=-1, keepdims=True) + EPS)
            else:
                v_ref[:, osl] = s
        ba = ba_ref[...]
        lane = lax.broadcasted_iota(jnp.int32, ba.shape, 1)
        gval = -jnp.exp(al_ref[...]) * _softplus(ba + dt_ref[...])
        bg_ref[...] = jnp.where(lane < DN_HEADS, _sigmoid(ba), jnp.where(lane < 2 * DN_HEADS, gval, 0.0))

    row = pl.BlockSpec((tm, DN_W), lambda i: (i, 0))
    one = pl.BlockSpec((1, 128), lambda i: (0, 0))
    return pl.pallas_call(
        body, name=name, grid=(t // tm,),
        in_specs=[pl.BlockSpec((tm, _QKV_W), lambda i: (i, 0)),
                  pl.BlockSpec((HALO, _QKV_W), lambda i: (jnp.maximum(i * hb - 1, 0), 0)),
                  pl.BlockSpec((4, _QKV_W), lambda i: (0, 0)),
                  pl.BlockSpec((tm, 128), lambda i: (i, _BA_COL)), one, one],
        out_specs=(row, row, row, pl.BlockSpec((tm, 128), lambda i: (i, 0))),
        out_shape=(jax.ShapeDtypeStruct((t, DN_W), F32),) * 3 + (jax.ShapeDtypeStruct((t, 128), F32),),
        compiler_params=_cp("parallel"))(proj, proj, conv_w, proj, alog_row, dtb_row)


def gdn_prep_bwd(proj, conv_w, alog_row, dtb_row, dq, dk, dv, dbg, dz, name, tm=256):
    t = proj.shape[0]
    hb = tm // HALO
    nt = t // tm
    qscale = DN_DIM ** -0.5
    te = tm + HALO

    def body(x_ref, xh_ref, xn_ref, w_ref, ba_ref, al_ref, dt_ref, dq_ref, dk_ref, dv_ref,
             dqn_ref, dkn_ref, dvn_ref, dbg_ref, dz_ref, dx_ref, dw_ref, ddt_ref, dal_ref):
        i = pl.program_id(0)
        first = i == 0
        last = i == nt - 1
        dws = []
        for gi in range(3 * DN_HEADS):
            sl = slice(gi * DN_DIM, (gi + 1) * DN_DIM)
            osl = slice((gi % DN_HEADS) * DN_DIM, (gi % DN_HEADS + 1) * DN_DIM)
            xe = jnp.concatenate([jnp.where(first, 0.0, xh_ref[:, sl]), x_ref[:, sl], xn_ref[:, sl]], axis=0)
            c = w_ref[0:1, sl] * xe[HALO - 3:HALO - 3 + te]
            for j in range(1, 4):
                c = c + w_ref[j:j + 1, sl] * xe[HALO - 3 + j:HALO - 3 + j + te]
            sg = _sigmoid(c)
            s = c * sg
            d_ref, dn_ref = ((dq_ref, dqn_ref), (dk_ref, dkn_ref), (dv_ref, dvn_ref))[gi // DN_HEADS]
            dy = jnp.concatenate([d_ref[:, osl], jnp.where(last, 0.0, dn_ref[:, osl])], axis=0)
            if gi < 2 * DN_HEADS:
                r = lax.rsqrt(jnp.sum(s * s, axis=-1, keepdims=True) + EPS)
                sh = s * r
                ds = r * (dy - sh * jnp.sum(sh * dy, axis=-1, keepdims=True))
                if gi < DN_HEADS:
                    ds = ds * qscale
            else:
                ds = dy
            dc = ds * sg * (1.0 + c * (1.0 - sg))
            dcs = [dc[3 - j:3 - j + tm] for j in range(4)]
            dx = w_ref[0:1, sl] * dcs[0]
            for j in range(1, 4):
                dx = dx + w_ref[j:j + 1, sl] * dcs[j]
            dx_ref[:, sl] = dx.astype(dx_ref.dtype)
            x0 = x_ref[:, sl]
            dws.append(jnp.concatenate([jnp.sum(dcs[j] * x0, axis=0, keepdims=True) for j in range(4)], axis=0))
        dw = jnp.concatenate(dws, axis=-1)
        ba = ba_ref[...]
        dbgv = dbg_ref[...]
        lane = lax.broadcasted_iota(jnp.int32, ba.shape, 1)
        beta = _sigmoid(ba)
        ea = -jnp.exp(al_ref[...])
        zin = ba + dt_ref[...]
        is_b = lane < DN_HEADS
        is_a = (lane >= DN_HEADS) & (lane < 2 * DN_HEADS)
        da = jnp.where(is_a, dbgv * ea * _sigmoid(zin), 0.0)
        dx_ref[:, _QKV_W:_QKV_W + DN_W] = dz_ref[...]
        dx_ref[:, _QKV_W + DN_W:] = jnp.where(is_b, dbgv * beta * (1.0 - beta), da).astype(dx_ref.dtype)
        ddt = jnp.sum(da, axis=0, keepdims=True)
        dal = jnp.sum(jnp.where(is_a, dbgv * ea * _softplus(zin), 0.0), axis=0, keepdims=True)

        @pl.when(first)
        def _():
            dw_ref[...] = dw
            ddt_ref[...] = ddt
            dal_ref[...] = dal

        @pl.when(i > 0)
        def _():
            dw_ref[...] += dw
            ddt_ref[...] += ddt
            dal_ref[...] += dal

    row = pl.BlockSpec((tm, DN_W), lambda i: (i, 0))
    nrow = pl.BlockSpec((HALO, DN_W), lambda i: (jnp.minimum((i + 1) * hb, t // HALO - 1), 0))
    one = pl.BlockSpec((1, 128), lambda i: (0, 0))
    return pl.pallas_call(
        body, name=name, grid=(nt,),
        in_specs=[pl.BlockSpec((tm, _QKV_W), lambda i: (i, 0)),
                  pl.BlockSpec((HALO, _QKV_W), lambda i: (jnp.maximum(i * hb - 1, 0), 0)),
                  pl.BlockSpec((HALO, _QKV_W), lambda i: (jnp.minimum((i + 1) * hb, t // HALO - 1), 0)),
                  pl.BlockSpec((4, _QKV_W), lambda i: (0, 0)),
                  pl.BlockSpec((tm, 128), lambda i: (i, _BA_COL)), one, one,
                  row, row, row, nrow, nrow, nrow, pl.BlockSpec((tm, 128), lambda i: (i, 0)), row],
        out_specs=(pl.BlockSpec((tm, ODD_IN_PAD), lambda i: (i, 0)), pl.BlockSpec((4, _QKV_W), lambda i: (0, 0)), one, one),
        out_shape=(jax.ShapeDtypeStruct((t, ODD_IN_PAD), MXU_DTYPE), jax.ShapeDtypeStruct((4, _QKV_W), F32),
                   jax.ShapeDtypeStruct((1, 128), F32), jax.ShapeDtypeStruct((1, 128), F32)),
        compiler_params=_cp("arbitrary"))(proj, proj, proj, conv_w, proj, alog_row, dtb_row, dq, dk, dv, dq, dk, dv, dbg,
                                          dz)


def _chunk_masks():
    r = lax.broadcasted_iota(jnp.int32, (DN_CHUNK, DN_CHUNK), 0)
    c = lax.broadcasted_iota(jnp.int32, (DN_CHUNK, DN_CHUNK), 1)
    return r >= c, r > c


INV_PACK = 2


def _inv_unit_lower_many(mats):
    n = DN_CHUNK
    wide = INV_PACK * n
    r = lax.broadcasted_iota(jnp.int32, (wide, wide), 0)
    c = lax.broadcasted_iota(jnp.int32, (wide, wide), 1)
    same = (r & -n) == (c & -n)
    eye = jnp.where((r[:n] == (c[:n] & (n - 1))), 1.0, 0.0)

    def blockdiag(row):
        return jnp.where(same, jnp.concatenate([row] * INV_PACK, axis=0), 0.0)

    packs = [jnp.concatenate(mats[g:g + INV_PACK], axis=-1) for g in range(0, len(mats), INV_PACK)]
    xs = [eye - a for a in packs]
    pws = [_hi(a, blockdiag(a)) for a in packs]
    for step in range(5):
        if step < 4:
            both = [_hi(jnp.concatenate([x, pw], axis=0), blockdiag(pw)) for x, pw in zip(xs, pws)]
            xs = [x + b[:n] for x, b in zip(xs, both)]
            pws = [b[n:] for b in both]
        else:
            xs = [x + _hi(x, blockdiag(pw)) for x, pw in zip(xs, pws)]
    return [x[:, j * n:(j + 1) * n] for x in xs for j in range(INV_PACK)]


def _chunk_common(q, k, v, beta, gc, gcr, lower, strict):
    gam = jnp.exp(jnp.where(lower, gc - gcr, NEG))
    eg = jnp.exp(gc)
    gl = gc[DN_CHUNK - 1:DN_CHUNK, :]
    kdf = jnp.exp(gl - gc)
    kb = k * beta
    bmat = _mx_nt(kb, k)
    qmat = _mx_nt(q, k)
    return gam, eg, jnp.exp(gl), kdf, kb, bmat, qmat


DN_STEP = 4


def gdn_fwd(q, k, v, bg, name):
    t = q.shape[0]
    n_chunks = t // DN_CHUNK

    def body(q_ref, k_ref, v_ref, bg_ref, o_ref, sall_ref, tall_ref, s_ref):
        n = pl.program_id(0)

        @pl.when(n == 0)
        def _():
            s_ref[...] = jnp.zeros_like(s_ref)

        lower, strict = _chunk_masks()
        ltri = jnp.where(lower, 1.0, 0.0)
        hs = range(DN_HEADS)
        sl = [slice(h * DN_DIM, (h + 1) * DN_DIM) for h in hs]
        units = [(c, h) for c in range(DN_STEP) for h in hs]
        nu = range(len(units))
        rs = [slice(c * DN_CHUNK, (c + 1) * DN_CHUNK) for c in range(DN_STEP)]
        bgv = [bg_ref[rs[c], :] for c in range(DN_STEP)]
        gcs = [_hi(ltri, b) for b in bgv]
        gcs_t = [g.T for g in gcs]
        qh = [q_ref[rs[c], sl[h]] for c, h in units]
        kh = [k_ref[rs[c], sl[h]] for c, h in units]
        vh = [v_ref[rs[c], sl[h]] for c, h in units]
        beta = [bgv[c][:, h:h + 1] for c, h in units]
        com = [_chunk_common(qh[u], kh[u], vh[u], beta[u], gcs[c][:, DN_HEADS + h:DN_HEADS + h + 1],
                             gcs_t[c][DN_HEADS + h:DN_HEADS + h + 1, :], lower, strict) for u, (c, h) in enumerate(units)]
        gam, eg, dec, kdf, kb, bmat, qmat = zip(*com)
        tms = _inv_unit_lower_many([jnp.where(strict, bmat[u] * gam[u], 0.0) for u in nu])
        for u, (c, h) in enumerate(units):
            tall_ref[c, h] = tms[u]
        uw = [_hi(tms[u], jnp.concatenate([vh[u] * beta[u], kb[u] * eg[u]], axis=-1)) for u in nu]
        qd = [qh[u] * eg[u] for u in nu]
        pm = [qmat[u] * gam[u] for u in nu]
        kd = [kh[u] * kdf[u] for u in nu]
        st = [s_ref[h] for h in hs]
        for c in range(DN_STEP):
            us = [c * DN_HEADS + h for h in hs]
            for h in hs:
                sall_ref[c, h] = st[h]
            v_new = [uw[us[h]][:, :DN_DIM] - _mx(uw[us[h]][:, DN_DIM:], st[h]) for h in hs]
            o_st = [_mx(qd[us[h]], st[h]) for h in hs]
            o_in = [_mx(pm[us[h]], v_new[h]) for h in hs]
            s_up = [_mx_tn(kd[us[h]], v_new[h]) for h in hs]
            for h in hs:
                o_ref[rs[c], sl[h]] = o_st[h] + o_in[h]
            st = [st[h] * dec[us[h]] + s_up[h] for h in hs]
        for h in hs:
            s_ref[h] = st[h]

    rows = DN_STEP * DN_CHUNK
    row = pl.BlockSpec((rows, DN_W), lambda n: (n, 0))
    return pl.pallas_call(
        body, name=name, grid=(n_chunks // DN_STEP,),
        in_specs=[row, row, row, pl.BlockSpec((rows, 128), lambda n: (n, 0))],
        out_specs=(row, pl.BlockSpec((DN_STEP, DN_HEADS, DN_DIM, DN_DIM), lambda n: (n, 0, 0, 0)),
                   pl.BlockSpec((DN_STEP, DN_HEADS, DN_CHUNK, DN_CHUNK), lambda n: (n, 0, 0, 0))),
        out_shape=(jax.ShapeDtypeStruct((t, DN_W), F32),
                   jax.ShapeDtypeStruct((n_chunks, DN_HEADS, DN_DIM, DN_DIM), F32),
                   jax.ShapeDtypeStruct((n_chunks, DN_HEADS, DN_CHUNK, DN_CHUNK), F32)),
        scratch_shapes=[pltpu.VMEM((DN_HEADS, DN_DIM, DN_DIM), F32)],
        compiler_params=_cp("arbitrary"))(q, k, v, bg)


def gdn_bwd(q, k, v, bg, sall, tall, do, name):
    t = q.shape[0]
    n_chunks = t // DN_CHUNK

    def body(q_ref, k_ref, v_ref, bg_ref, sall_ref, tall_ref, do_ref, dq_ref, dk_ref, dv_ref, dbg_ref, ds_ref):
        n = pl.program_id(0)

        @pl.when(n == 0)
        def _():
            ds_ref[...] = jnp.zeros_like(ds_ref)

        lower, strict = _chunk_masks()
        ltri = jnp.where(lower, 1.0, 0.0)
        bgv = bg_ref[...]
        gcs = _hi(ltri, bgv)
        gcs_t = gcs.T
        lane = lax.broadcasted_iota(jnp.int32, (DN_CHUNK, 128), 1)
        rowi = lax.broadcasted_iota(jnp.int32, (DN_CHUNK, 1), 0)
        hs = range(DN_HEADS)
        each = lambda fn, *ls: [fn(*a) for a in zip(*ls)]
        rsum = lambda a: jnp.sum(a, axis=-1, keepdims=True)
        sl = [slice(h * DN_DIM, (h + 1) * DN_DIM) for h in hs]
        st = [sall_ref[0, h] for h in hs]
        tms = [tall_ref[0, h] for h in hs]
        dsn = [ds_ref[h] for h in hs]
        qh = [q_ref[:, sl[h]] for h in hs]
        kh = [k_ref[:, sl[h]] for h in hs]
        vh = [v_ref[:, sl[h]] for h in hs]
        doh = [do_ref[:, sl[h]] for h in hs]
        beta = [bgv[:, h:h + 1] for h in hs]
        com = [_chunk_common(qh[h], kh[h], vh[h], beta[h], gcs[:, DN_HEADS + h:DN_HEADS + h + 1],
                             gcs_t[DN_HEADS + h:DN_HEADS + h + 1, :], lower, strict) for h in hs]
        gam, eg, dec, kdf, kb, bmat, qmat = zip(*com)
        rhs_w = each(lambda a, b: a * b, kb, eg)
        uw = each(lambda t_, v_, b_, r_: _hi(t_, jnp.concatenate([v_ * b_, r_], axis=-1)), tms, vh, beta, rhs_w)
        qd = each(lambda a, b: a * b, qh, eg)
        kd = each(lambda a, b: a * b, kh, kdf)
        pmat = each(lambda a, b: a * b, qmat, gam)
        v_new = each(lambda uw_, s_: uw_[:, :DN_DIM] - _mx(uw_[:, DN_DIM:], s_), uw, st)
        dqd = each(_mx_nt, doh, st)
        ds_o = each(_mx_tn, qd, doh)
        dp = each(lambda d_, v_: jnp.where(lower, _mx_nt(d_, v_), 0.0), doh, v_new)
        dvn_o = each(_mx_tn, pmat, doh)
        ddec = each(lambda d_, s_: jnp.sum(rsum(d_ * s_), axis=0, keepdims=True), dsn, st)
        dkd = each(_mx_nt, v_new, dsn)
        dvn = each(lambda a, k_, d_: a + _mx(k_, d_), dvn_o, kd, dsn)
        dw = each(lambda d_, s_: -_mx_nt(d_, s_), dvn, st)
        ds_w = each(lambda uw_, d_: _mx_tn(uw_[:, DN_DIM:], d_), uw, dvn)
        for h in hs:
            ds_ref[h] = ds_o[h] + dec[h] * dsn[h] - ds_w[h]
        dr = each(lambda t_, a, b: _hi_tn(t_, jnp.concatenate([a, b], axis=-1)), tms, dvn, dw)
        da = each(lambda r_, uw_: jnp.where(strict, -_hi_nt(r_, uw_), 0.0), dr, uw)
        dru = [r_[:, :DN_DIM] for r_ in dr]
        drw = [r_[:, DN_DIM:] for r_ in dr]
        db = each(lambda a, b: a * b, da, gam)
        dq_m = each(lambda a, b: a * b, dp, gam)
        e = each(lambda a, bm, p_, qm, g_: (a * bm + p_ * qm) * g_, da, bmat, dp, qmat, gam)
        dkb = each(lambda b_, k_, r_, e_: _mx(b_, k_) + r_ * e_, db, kh, drw, eg)
        dk = each(lambda b_, kb_, m_, q_, d_, f_: _mx_tn(b_, kb_) + _mx_tn(m_, q_) + d_ * f_, db, kb, dq_m, qh, dkd, kdf)
        dq = each(lambda m_, k_, d_, e_: _mx(m_, k_) + d_ * e_, dq_m, kh, dqd, eg)
        tk = each(lambda a, b: rsum(a * b), dkd, kd)
        dbeta_all = jnp.zeros((DN_CHUNK, 128), F32)
        dgc_all = jnp.zeros((DN_CHUNK, 128), F32)
        for h in hs:
            dgc = (jnp.sum(e[h], axis=1, keepdims=True) - jnp.sum(e[h].T, axis=1, keepdims=True)
                   + rsum(dqd[h] * qd[h]) - tk[h] + rsum(drw[h] * rhs_w[h]))
            dgl = jnp.sum(tk[h], axis=0, keepdims=True) + ddec[h] * dec[h]
            dgc = dgc + jnp.where(rowi == DN_CHUNK - 1, dgl, 0.0)
            dbeta = rsum(dru[h] * vh[h]) + rsum(dkb[h] * kh[h])
            dq_ref[:, sl[h]] = dq[h]
            dk_ref[:, sl[h]] = dk[h] + dkb[h] * beta[h]
            dv_ref[:, sl[h]] = dru[h] * beta[h]
            dbeta_all = jnp.where(lane == h, dbeta, dbeta_all)
            dgc_all = jnp.where(lane == DN_HEADS + h, dgc, dgc_all)
        dbg_ref[...] = dbeta_all + _hi_tn(ltri, dgc_all)

    rev = lambda n: (n_chunks - 1 - n, 0)
    row = pl.BlockSpec((DN_CHUNK, DN_W), rev)
    small = pl.BlockSpec((DN_CHUNK, 128), rev)
    return pl.pallas_call(
        body, name=name, grid=(n_chunks,),
        in_specs=[row, row, row, small,
                  pl.BlockSpec((1, DN_HEADS, DN_DIM, DN_DIM), lambda n: (n_chunks - 1 - n, 0, 0, 0)),
                  pl.BlockSpec((1, DN_HEADS, DN_CHUNK, DN_CHUNK), lambda n: (n_chunks - 1 - n, 0, 0, 0)), row],
        out_specs=(row, row, row, small),
        out_shape=(jax.ShapeDtypeStruct((t, DN_W), F32),) * 3 + (jax.ShapeDtypeStruct((t, 128), F32),),
        scratch_shapes=[pltpu.VMEM((DN_HEADS, DN_DIM, DN_DIM), F32)],
        compiler_params=_cp("arbitrary"))(q, k, v, bg, sall, tall, do)


def gdn_out_fwd(o, proj, o_gain, name, tm=256):
    t = o.shape[0]

    def body(o_ref, z_ref, g_ref, y_ref, yt_ref):
        for h in range(DN_HEADS):
            sl = slice(h * DN_DIM, (h + 1) * DN_DIM)
            ov, zv = o_ref[:, sl], z_ref[:, sl]
            r = lax.rsqrt(jnp.mean(ov * ov, axis=-1, keepdims=True) + EPS)
            y = ov * r * g_ref[...] * (zv * _sigmoid(zv))
            y_ref[:, sl] = y.astype(y_ref.dtype)
            yt_ref[sl, :] = y.T.astype(yt_ref.dtype)

    row = pl.BlockSpec((tm, DN_W), lambda i: (i, 0))
    return pl.pallas_call(
        body, name=name, grid=(t // tm,),
        in_specs=[row, pl.BlockSpec((tm, DN_W), lambda i: (i, _Z_COL)), pl.BlockSpec((1, DN_DIM), lambda i: (0, 0))],
        out_specs=(row, pl.BlockSpec((DN_W, tm), lambda i: (0, i))),
        out_shape=(jax.ShapeDtypeStruct((t, DN_W), MXU_DTYPE), jax.ShapeDtypeStruct((DN_W, t), MXU_DTYPE)),
        compiler_params=_cp("parallel"))(o, proj, o_gain)


def gdn_out_bwd(o, proj, o_gain, dy, name, tm=256):
    t = o.shape[0]

    def body(o_ref, z_ref, g_ref, dy_ref, do_ref, dz_ref, dg_ref):
        i = pl.program_id(0)
        dg = jnp.zeros((1, DN_DIM), F32)
        for h in range(DN_HEADS):
            sl = slice(h * DN_DIM, (h + 1) * DN_DIM)
            ov, zv, dyv = o_ref[:, sl], z_ref[:, sl], dy_ref[:, sl]
            r = lax.rsqrt(jnp.mean(ov * ov, axis=-1, keepdims=True) + EPS)
            oh = ov * r
            sg = _sigmoid(zv)
            dz_ref[:, sl] = (dyv * oh * g_ref[...] * sg * (1.0 + zv * (1.0 - sg))).astype(dz_ref.dtype)
            don = dyv * (zv * sg)
            dg = dg + jnp.sum(don * oh, axis=0, keepdims=True)
            doh = don * g_ref[...]
            do_ref[:, sl] = r * (doh - oh * jnp.mean(doh * oh, axis=-1, keepdims=True))

        @pl.when(i == 0)
        def _():
            dg_ref[...] = dg

        @pl.when(i > 0)
        def _():
            dg_ref[...] += dg

    row = pl.BlockSpec((tm, DN_W), lambda i: (i, 0))
    one = pl.BlockSpec((1, DN_DIM), lambda i: (0, 0))
    return pl.pallas_call(
        body, name=name, grid=(t // tm,),
        in_specs=[row, pl.BlockSpec((tm, DN_W), lambda i: (i, _Z_COL)), one, row],
        out_specs=(row, row, one),
        out_shape=(jax.ShapeDtypeStruct((t, DN_W), F32), jax.ShapeDtypeStruct((t, DN_W), MXU_DTYPE),
                   jax.ShapeDtypeStruct((1, DN_DIM), F32)),
        compiler_params=_cp("arbitrary"))(o, proj, o_gain, dy)


def _peer(k):
    x, y, c = lax.axis_index("x"), lax.axis_index("y"), lax.axis_index("c")
    px = 1 - x if k & 4 else x
    py = 1 - y if k & 2 else y
    pc = 1 - c if k & 1 else c
    return (px, py, pc), 4 * px + 2 * py + pc


_HBM = pl.BlockSpec(memory_space=pltpu.HBM)
_SEM = pl.BlockSpec(memory_space=pltpu.SEMAPHORE)
_DATAFLOW = pltpu.SideEffectType.DATAFLOW_SIDE_EFFECTING
N_PEER = N_DEV - 1


def send_start(srcs, name, scatter, after):
    na = len(srcs)
    ns = (2 * N_PEER + 1) * na
    lands = [lax.empty((N_DEV,) + (s.shape[1:] if scatter else s.shape), s.dtype) for s in srcs]
    extra = [] if after is None else [after]

    def body(*refs):
        src_refs, land_refs = refs[:na], refs[na:2 * na]
        sems = refs[2 * na + len(extra):2 * na + len(extra) + ns]
        land_out, token = refs[-1 - na:-1], refs[-1]
        _, me = _peer(0)
        for a in range(na):
            pltpu.make_async_copy(src_refs[a].at[me] if scatter else src_refs[a], land_out[a].at[me],
                                  sems[2 * N_PEER * na + a]).start()
        for k in range(1, N_DEV):
            peer, pid = _peer(k)
            for a in range(na):
                pltpu.make_async_remote_copy(
                    src_ref=src_refs[a].at[pid] if scatter else src_refs[a], dst_ref=land_refs[a].at[me],
                    send_sem=sems[2 * (a * N_PEER + k - 1)], recv_sem=sems[2 * (a * N_PEER + k - 1) + 1],
                    device_id=peer, device_id_type=MESH).start()
        token[...] = jnp.zeros_like(token)

    hbm = lambda arrs: tuple(pltpu.HBM(a.shape, a.dtype) for a in arrs)
    outs = pl.pallas_call(
        body, name=name,
        out_shape=(pltpu.SemaphoreType.DMA(()),) * ns + hbm(srcs) + hbm(lands) + (jax.ShapeDtypeStruct((8, 128), F32),),
        in_specs=[_HBM] * (2 * na) + [pl.BlockSpec(memory_space=pl.ANY)] * len(extra),
        out_specs=(_SEM,) * ns + (_HBM,) * (2 * na) + (pl.BlockSpec(memory_space=pltpu.VMEM),),
        input_output_aliases={i: ns + i for i in range(2 * na)},
        compiler_params=pltpu.CompilerParams(has_side_effects=_DATAFLOW),
    )(*[pltpu.with_memory_space_constraint(a, pltpu.HBM) for a in list(srcs) + lands], *extra)
    return outs[:ns], outs[ns:ns + na], outs[ns + na:ns + 2 * na], outs[-1]


def send_wait(sems, srcs_thru, lands_thru, name, scatter, after):
    na = len(srcs_thru)
    ns = (2 * N_PEER + 1) * na

    def body(*refs):
        src_refs, land_refs, sm = refs[:na], refs[na:2 * na], refs[2 * na:2 * na + ns]
        _, me = _peer(0)
        for a in range(na):
            pltpu.make_async_copy(src_refs[a].at[me] if scatter else src_refs[a], land_refs[a].at[me],
                                  sm[2 * N_PEER * na + a]).wait()
        for k in range(1, N_DEV):
            peer, pid = _peer(k)
            for a in range(na):
                cp = pltpu.make_async_remote_copy(
                    src_ref=src_refs[a].at[pid] if scatter else src_refs[a], dst_ref=land_refs[a].at[pid],
                    send_sem=sm[2 * (a * N_PEER + k - 1)], recv_sem=sm[2 * (a * N_PEER + k - 1) + 1],
                    device_id=peer, device_id_type=MESH)
                cp.wait_send()
                cp.wait_recv()

    hbm = lambda arrs: tuple(pltpu.HBM(a.shape, a.dtype) for a in arrs)
    outs = pl.pallas_call(
        body, name=name, out_shape=hbm(srcs_thru) + hbm(lands_thru),
        in_specs=[_HBM] * (2 * na) + [_SEM] * ns + [pl.BlockSpec(memory_space=pl.ANY)], out_specs=(_HBM,) * (2 * na),
        input_output_aliases={i: i for i in range(2 * na)},
        compiler_params=pltpu.CompilerParams(has_side_effects=_DATAFLOW),
    )(*srcs_thru, *lands_thru, *sems, after)
    return outs[na:]


def _adamw(w, g, m, v):
    m = ADAM_B1 * m + (1.0 - ADAM_B1) * g
    v = ADAM_B2 * v + (1.0 - ADAM_B2) * (g * g)
    m_hat = m / (1.0 - ADAM_B1 ** ADAM_STEP)
    v_hat = v / (1.0 - ADAM_B2 ** ADAM_STEP)
    return -ADAM_LR * (m_hat / (jnp.sqrt(v_hat) + ADAM_EPS) + ADAM_WD * w), m, v


def adam_sum(w, pieces, m, v, name, layer=0, into=None):
    nl, r, c = w.shape
    tr = r
    for cand in (256, 128, 64, 32, 16, 8):
        if r % cand == 0:
            tr = cand
            break

    def body(w_ref, p_ref, m_ref, v_ref, *rest):
        g_ref, d_ref, nm_ref, nv_ref = rest[-4:]
        g = p_ref[0].astype(F32)
        for s in range(1, N_DEV):
            g = g + p_ref[s].astype(F32)
        g_ref[0] = g
        d_ref[0], nm_ref[0], nv_ref[0] = _adamw(w_ref[0], g, m_ref[0], v_ref[0])

    row = pl.BlockSpec((1, tr, c), lambda i: (layer, i, 0))
    out = jax.ShapeDtypeStruct((nl, r, c), F32)
    extra = [] if into is None else list(into)
    return pl.pallas_call(
        body, name=name, grid=(r // tr,),
        in_specs=[row, pl.BlockSpec((N_DEV, tr, c), lambda i: (0, i, 0)), row, row]
        + [pl.BlockSpec(memory_space=pl.ANY)] * len(extra),
        out_specs=(row,) * 4, out_shape=(out,) * 4,
        input_output_aliases={4 + i: i for i in range(len(extra))},
        compiler_params=_cp("parallel"))(w, pieces, m, v, *extra)


def sum_rows(gathered, name):
    _, r, c = gathered.shape

    def body(p_ref, o_ref):
        g = p_ref[0]
        for s in range(1, N_DEV):
            g = g + p_ref[s]
        o_ref[...] = g

    return pl.pallas_call(body, name=name, out_shape=jax.ShapeDtypeStruct((r, c), F32))(gathered)


def adam_small(w, g, m, v, name):
    def body(w_ref, g_ref, m_ref, v_ref, d_ref, nm_ref, nv_ref):
        d_ref[...], nm_ref[...], nv_ref[...] = _adamw(w_ref[...], g_ref[...], m_ref[...], v_ref[...])

    out = jax.ShapeDtypeStruct(w.shape, F32)
    return pl.pallas_call(body, name=name, out_shape=(out,) * 3)(w, g, m, v)


def _rope_tables(t):
    inv_freq = 10000.0 ** (-jnp.arange(0, HEAD_DIM, 2, dtype=F32) / HEAD_DIM)
    ang = jnp.arange(t, dtype=F32)[:, None] * inv_freq[None, :]
    cos, sin = jnp.cos(ang), jnp.sin(ang)
    return jnp.concatenate([cos, cos], axis=-1), jnp.concatenate([sin, sin], axis=-1)


def _lane_row(vec8):
    return jnp.pad(vec8.reshape(1, DN_HEADS), ((0, 0), (DN_HEADS, 128 - 2 * DN_HEADS)))


def _ffn_bwd(x, norm_g, w_gu, w_d, saved, dy, tag, after=None):
    ft, gu, at = saved
    dgu = ffn_dact(dy, w_d, gu, f"{tag}_d_gate_up", after=after)
    dwd = mm_at(at, dy, f"{tag}_dw_down")
    nj = D_FF // GU_TILE
    dwgu = mm_at(ft, dgu, f"{tag}_dw_gate_up", transposed=True, tn=GU_TILE, row_block=lambda q: (q % 2) * nj + q // 2)
    dx, dg = mm_rms_bwd(dgu, w_gu, x, norm_g, dy, f"{tag}_d_norm", chunks=_GU_CHUNKS)
    return dx, dwgu, dwd, dg


def local_step(x, target, small, weights_of, grads_out, after=None):
    t = x.shape[0]
    cosf, sinf = _rope_tables(t)
    alog_row, dtb_row = _lane_row(small["odd_a_log"]), _lane_row(small["odd_dt_bias"])

    h0, h0t = rms_fwd(x, small["even_norm"], "even_norm", after=after)
    we = weights_of("even", h0)
    small = {**small, **we.get("small", {})}
    proj0 = mm_nt(h0, we["w_in"], "even_in_proj")
    qr, kr = qk_prep_fwd(proj0, small["even_q_gain"], small["even_k_gain"], cosf, sinf, "even_qk_prep")
    y_attn, mix0, mix0_t, lse = swa_fwd(qr, kr, proj0, small["even_sinks"], "even_swa")
    mix0, mix0_t = gconv_fwd(proj0, small["even_conv_w"], mix0, mix0_t, "even_gconv")
    we = {**we, **weights_of("even_out", mix0)}
    x1, f0, f0t = mm_nn_res_norm(mix0, we["w_out"], x, small["ffn_norm0"], "even_out_proj")
    w0 = weights_of("ffn0", x1)
    gu0, a0, a0t = ffn_up(f0, w0["gate_up"], "ffn0_gate_up")
    ffn0 = (f0t, gu0, a0t)
    x2, h1, h1t = mm_nn_res_norm(a0, w0["down"], x1, small["odd_norm"], "ffn0_down")

    wo = weights_of("odd", x2)
    proj1 = mm_nt(h1, wo["w_in"], "odd_in_proj")
    qn, kn, vs, bg = gdn_prep_fwd(proj1, small["odd_conv_w"], alog_row, dtb_row, "odd_prep")
    o, sall, tall = gdn_fwd(qn, kn, vs, bg, "odd_delta_rule")
    og, ogt = gdn_out_fwd(o, proj1, small["odd_o_gain"], "odd_gate_norm")
    x3, f1, f1t = mm_nn_res_norm(og, wo["w_out"], x2, small["ffn_norm1"], "odd_out_proj")
    w1 = weights_of("ffn1", x3)
    gu1, a1, a1t = ffn_up(f1, w1["gate_up"], "ffn1_gate_up")
    ffn1 = (f1t, gu1, a1t)
    dy, loss_row = mm_nn_res_loss(a1, w1["down"], x3, target, "ffn1_down_loss")

    gs = {}
    dx3, dwgu, dwd, gs["ffn_norm1"] = _ffn_bwd(x3, small["ffn_norm1"], w1["gate_up"], w1["down"], ffn1, dy, "ffn1")
    tok = grads_out("ffn1", {"gate_up": dwgu, "down": dwd})

    dog = mm_nt(dx3, wo["w_out"], "odd_d_gated", after=tok)
    dwo = mm_at(ogt, dx3, "odd_dw_out")
    do, dz, gs["odd_o_gain"] = gdn_out_bwd(o, proj1, small["odd_o_gain"], dog, "odd_d_gate_norm")
    dqn, dkn, dvs, dbg = gdn_bwd(qn, kn, vs, bg, sall, tall, do, "odd_d_delta_rule")
    dproj1, gs["odd_conv_w"], ddt_row, dal_row = gdn_prep_bwd(
        proj1, small["odd_conv_w"], alog_row, dtb_row, dqn, dkn, dvs, dbg, dz, "odd_d_prep")
    gs["odd_dt_bias"] = ddt_row[:, DN_HEADS:2 * DN_HEADS]
    gs["odd_a_log"] = dal_row[:, DN_HEADS:2 * DN_HEADS]
    dwi = mm_at(h1t, dproj1, "odd_dw_in", transposed=True)
    dx2, gs["odd_norm"] = mm_rms_bwd(dproj1, wo["w_in"], x2, small["odd_norm"], dx3, "odd_d_norm")
    tok = grads_out("odd", {"w_in": dwi, "w_out": dwo})

    dx1, dwgu, dwd, gs["ffn_norm0"] = _ffn_bwd(x1, small["ffn_norm0"], w0["gate_up"], w0["down"], ffn0, dx2, "ffn0",
                                               after=tok)
    tok = grads_out("ffn0", {"gate_up": dwgu, "down": dwd})

    dmix = mm_nt(dx1, we["w_out"], "even_d_mix", after=tok)
    dwo = mm_at(mix0_t, dx1, "even_dw_out")
    dqr, dkr, dv, gs["even_sinks"] = swa_bwd(qr, kr, proj0, small["even_sinks"], y_attn, lse, dmix, "even_d_swa")
    dqk, gs["even_q_gain"], gs["even_k_gain"] = qk_prep_bwd(
        proj0, small["even_q_gain"], small["even_k_gain"], cosf, sinf, dqr, dkr, "even_d_qk_prep")
    dgb, dgc, dxi, gs["even_conv_w"] = gconv_bwd(proj0, small["even_conv_w"], dmix, "even_d_gconv")
    dproj0 = jnp.concatenate([dqk, dv, dgb, dgc, dxi], axis=-1)
    dwi = mm_at(h0t, dproj0, "even_dw_in", transposed=True)
    tok = grads_out("even", {"w_in": dwi, "w_out": dwo})
    grad_x, gs["even_norm"] = mm_rms_bwd(dproj0, we["w_in"], x, small["even_norm"], dx1, "even_d_norm", after=tok)
    return loss_row, grad_x, gs


_SMALL_ORDER = ("even_norm", "even_q_gain", "even_k_gain", "even_sinks", "odd_a_log", "odd_dt_bias", "odd_o_gain",
                "ffn_norm0", "ffn_norm1", "odd_norm", "even_conv_w", "odd_conv_w")
_SMALL_SIZE = {"even_norm": 1024, "even_q_gain": 64, "even_k_gain": 64, "even_sinks": 8, "odd_a_log": 8,
               "odd_dt_bias": 8, "odd_o_gain": 128, "ffn_norm0": 1024, "ffn_norm1": 1024, "odd_norm": 1024,
               "even_conv_w": 3 * 512, "odd_conv_w": 4 * 3072}
_N_REPL = 9


def _pack_rows(vals):
    flat = jnp.concatenate([v.reshape(-1) for v in vals])
    pad = (-flat.shape[0]) % 1024
    return jnp.pad(flat, (0, pad)).reshape(-1, 128)


def _my_block(full, size, axis):
    me = 4 * lax.axis_index("x") + 2 * lax.axis_index("y") + lax.axis_index("c")
    return lax.dynamic_slice_in_dim(full, me * size, size, axis=axis)


def kernel(x, even_norm, even_w_in, even_q_gain, even_k_gain, even_sinks, even_conv_w, even_w_out, odd_norm, odd_w_in, odd_conv_w, odd_a_log, odd_dt_bias, odd_o_gain, odd_w_out, ffn_norm, ffn_w_gate_up, ffn_w_down, loss_target, m_even_norm, m_even_w_in, m_even_q_gain, m_even_k_gain, m_even_sinks, m_even_conv_w, m_even_w_out, m_odd_norm, m_odd_w_in, m_odd_conv_w, m_odd_a_log, m_odd_dt_bias, m_odd_o_gain, m_odd_w_out, m_ffn_norm, m_ffn_w_gate_up, m_ffn_w_down, v_even_norm, v_even_w_in, v_even_q_gain, v_even_k_gain, v_even_sinks, v_even_conv_w, v_even_w_out, v_odd_norm, v_odd_w_in, v_odd_conv_w, v_odd_a_log, v_odd_dt_bias, v_odd_o_gain, v_odd_w_out, v_ffn_norm, v_ffn_w_gate_up, v_ffn_w_down):
    t = x.shape[1]
    d = D_MODEL

    me = 4 * lax.axis_index("x") + 2 * lax.axis_index("y") + lax.axis_index("c")
    tr = lambda a: jnp.swapaxes(a, 1, 2)
    shard = {
        "even": {"w_in": tr(even_w_in)[0], "w_out": even_w_out[0]},
        "ffn0": {"gate_up": tr(ffn_w_gate_up)[0], "down": ffn_w_down[0]},
        "odd": {"w_in": tr(odd_w_in)[0], "w_out": odd_w_out[0]},
        "ffn1": {"gate_up": tr(ffn_w_gate_up)[1], "down": ffn_w_down[1]},
    }
    given = {
        ("even", "w_in"): ("even_w_in", even_w_in, m_even_w_in, v_even_w_in, 0),
        ("even", "w_out"): ("even_w_out", even_w_out, m_even_w_out, v_even_w_out, 0),
        ("odd", "w_in"): ("odd_w_in", odd_w_in, m_odd_w_in, v_odd_w_in, 0),
        ("odd", "w_out"): ("odd_w_out", odd_w_out, m_odd_w_out, v_odd_w_out, 0),
        ("ffn0", "gate_up"): ("ffn_w_gate_up", ffn_w_gate_up, m_ffn_w_gate_up, v_ffn_w_gate_up, 0),
        ("ffn1", "gate_up"): ("ffn_w_gate_up", ffn_w_gate_up, m_ffn_w_gate_up, v_ffn_w_gate_up, 1),
        ("ffn0", "down"): ("ffn_w_down", ffn_w_down, m_ffn_w_down, v_ffn_w_down, 0),
        ("ffn1", "down"): ("ffn_w_down", ffn_w_down, m_ffn_w_down, v_ffn_w_down, 1),
    }

    def whole(group, parts):
        col, row = tuple(shard[group])
        w_col = parts[0].reshape(-1, d)
        if group == "odd":
            w_col = jnp.pad(w_col, ((0, ODD_IN_PAD - ODD_IN_W), (0, 0)))
        return {col: w_col, row: parts[1].reshape(-1, d)}

    wire = {g: [a.astype(MXU_DTYPE) for a in shard[g].values()] for g in shard}
    wire["even_out"] = [wire["even"].pop()]
    wire["even"].append(_pack_rows([odd_norm, even_conv_w, odd_conv_w]))
    gathers, tok = {}, None
    for g in ("even", "even_out", "ffn0", "odd", "ffn1"):
        sems, srcs_thru, lands_thru, tok = send_start(wire[g], f"gather_{g}_start", False, tok)
        gathers[g] = (sems, srcs_thru, lands_thru)
    o1 = d // N_DEV
    o2 = o1 + 3 * CONV_CH // N_DEV

    def weights_of(group, after):
        lands = send_wait(*gathers[group], f"gather_{group}_wait", False, after)
        if group == "even_out":
            return {"w_out": lands[0].reshape(-1, d)}
        if group != "even":
            return whole(group, lands)
        sg = lands[1].reshape(N_DEV, -1)
        return {"w_in": lands[0].reshape(-1, d), "small": {
            "odd_norm": sg[:, :o1].reshape(1, d),
            "even_conv_w": sg[:, o1:o2].reshape(N_DEV, 3, CONV_CH // N_DEV).transpose(1, 0, 2).reshape(3, CONV_CH),
            "odd_conv_w": sg[:, o2:o2 + 4 * _QKV_W // N_DEV].reshape(N_DEV, 4, _QKV_W // N_DEV)
            .transpose(1, 0, 2).reshape(4, _QKV_W)}}

    sent = {}

    def grads_out(group, dws):
        col, row = tuple(shard[group])
        n_cols = N_DEV * shard[group][col].shape[0]
        pieces = [dws[col][:n_cols].reshape((N_DEV,) + shard[group][col].shape),
                  dws[row].reshape((N_DEV,) + shard[group][row].shape)]
        sems, srcs_thru, lands_thru, token = send_start(pieces, f"exchange_{group}_start", True, None)
        sent[group] = (sems, srcs_thru, lands_thru, pieces)
        return token

    small = {
        "even_norm": even_norm, "even_q_gain": even_q_gain, "even_k_gain": even_k_gain, "even_sinks": even_sinks,
        "odd_a_log": odd_a_log.reshape(-1), "odd_dt_bias": odd_dt_bias.reshape(-1), "odd_o_gain": odd_o_gain,
        "ffn_norm0": ffn_norm[0:1], "ffn_norm1": ffn_norm[1:2],
    }

    loss_row, grad_x, gs = local_step(x.reshape(t, d), loss_target.reshape(t, d), small, weights_of, grads_out, after=tok)

    rows = _pack_rows([gs[n] for n in _SMALL_ORDER] + [loss_row[:, 0:1]])
    small_sent = send_start([rows], "gather_small_grads_start", False, None)

    res, behind = {}, small_sent[3]
    for g in ("ffn1", "odd", "ffn0", "even"):
        sems, srcs_thru, lands_thru, pieces = sent[g]
        lands = send_wait(sems, srcs_thru, lands_thru, f"exchange_{g}_wait", True, behind)
        for i, (key, pcs) in enumerate(zip(shard[g], lands)):
            name, w_, m_, v_, layer = given[g, key]
            view = tr if i == 0 else (lambda a: a)
            res[name] = adam_sum(view(w_), pcs, view(m_), view(v_), f"adamw_{g}_{key}", layer=layer, into=res.get(name))
        behind = res[name][0]
    for name in ("even_w_in", "odd_w_in", "ffn_w_gate_up"):
        res[name] = tuple(tr(a) for a in res[name])

    (rows_g,) = send_wait(*small_sent[:3], "gather_small_grads_wait", False, behind)
    tot = sum_rows(rows_g, "sum_small_grads").reshape(-1)
    off, sgrad = 0, {}
    for n in _SMALL_ORDER:
        sgrad[n] = tot[off:off + _SMALL_SIZE[n]]
        off += _SMALL_SIZE[n]
    loss = tot[off]

    repl = _SMALL_ORDER[:_N_REPL]
    repl_w = {"even_norm": even_norm, "even_q_gain": even_q_gain, "even_k_gain": even_k_gain, "even_sinks": even_sinks,
              "odd_a_log": odd_a_log, "odd_dt_bias": odd_dt_bias, "odd_o_gain": odd_o_gain,
              "ffn_norm0": ffn_norm[0], "ffn_norm1": ffn_norm[1]}
    repl_m = {"even_norm": m_even_norm, "even_q_gain": m_even_q_gain, "even_k_gain": m_even_k_gain,
              "even_sinks": m_even_sinks, "odd_a_log": m_odd_a_log, "odd_dt_bias": m_odd_dt_bias,
              "odd_o_gain": m_odd_o_gain, "ffn_norm0": m_ffn_norm[0], "ffn_norm1": m_ffn_norm[1]}
    repl_v = {"even_norm": v_even_norm, "even_q_gain": v_even_q_gain, "even_k_gain": v_even_k_gain,
              "even_sinks": v_even_sinks, "odd_a_log": v_odd_a_log, "odd_dt_bias": v_odd_dt_bias,
              "odd_o_gain": v_odd_o_gain, "ffn_norm0": v_ffn_norm[0], "ffn_norm1": v_ffn_norm[1]}
    pk = lambda dct: _pack_rows([dct[n] for n in repl])
    pd_, pm_, pv_ = adam_small(pk(repl_w), pk(sgrad), pk(repl_m), pk(repl_v), "adamw_replicated")
    sres = {}
    off = 0
    for n in repl:
        sz = _SMALL_SIZE[n]
        sres[n] = (sgrad[n], pd_.reshape(-1)[off:off + sz], pm_.reshape(-1)[off:off + sz], pv_.reshape(-1)[off:off + sz])
        off += sz
    g_on = _my_block(sgrad["odd_norm"].reshape(1, d), d // N_DEV, 1)
    g_ec = _my_block(sgrad["even_conv_w"].reshape(3, CONV_CH), CONV_CH // N_DEV, 1)
    g_oc = _my_block(sgrad["odd_conv_w"].reshape(4, _QKV_W), _QKV_W // N_DEV, 1)
    shard_w = _pack_rows([odd_norm, even_conv_w, odd_conv_w])
    sd_, sm_, sv_ = adam_small(shard_w, _pack_rows([g_on, g_ec, g_oc]),
                               _pack_rows([m_odd_norm, m_even_conv_w, m_odd_conv_w]),
                               _pack_rows([v_odd_norm, v_even_conv_w, v_odd_conv_w]), "adamw_sharded_small")
    off = 0
    for n, gfull, like in (("odd_norm", g_on, odd_norm), ("even_conv_w", g_ec, even_conv_w), ("odd_conv_w", g_oc, odd_conv_w)):
        sz = like.size
        sres[n] = (gfull, sd_.reshape(-1)[off:off + sz], sm_.reshape(-1)[off:off + sz], sv_.reshape(-1)[off:off + sz])
        off += sz

    def small_out(name, like, kind):
        if name == "ffn_norm":
            return jnp.stack([sres["ffn_norm0"][kind], sres["ffn_norm1"][kind]]).reshape(like.shape)
        return sres[name][kind].reshape(like.shape)

    order = (("even_norm", even_norm), ("even_w_in", even_w_in), ("even_q_gain", even_q_gain),
             ("even_k_gain", even_k_gain), ("even_sinks", even_sinks), ("even_conv_w", even_conv_w),
             ("even_w_out", even_w_out), ("odd_norm", odd_norm), ("odd_w_in", odd_w_in), ("odd_conv_w", odd_conv_w),
             ("odd_a_log", odd_a_log), ("odd_dt_bias", odd_dt_bias), ("odd_o_gain", odd_o_gain),
             ("odd_w_out", odd_w_out), ("ffn_norm", ffn_norm), ("ffn_w_gate_up", ffn_w_gate_up),
             ("ffn_w_down", ffn_w_down))
    outs = [loss, grad_x.reshape(x.shape)]
    for kind in range(4):
        for name, like in order:
            outs.append(res[name][kind] if name in res else small_out(name, like, kind))
    return tuple(outs)
```

```python
import jax
import jax.numpy as jnp
import numpy as np
from jax import lax
from jax.experimental import pallas as pl
from jax.experimental.pallas import tpu as pltpu

F32 = jnp.float32
MXU_DTYPE = jnp.bfloat16
HI = lax.Precision.HIGH
EPS = 1e-6
N_DEV = 8
D_MODEL = 1024
HEAD_DIM = 64
ATTN_HEADS = 8
KV_HEADS = 2
ATTN_BLOCK = 128
Q_W = 512
KV_W = 128
CONV_CH = 512
EVEN_IN_W = 2304
DN_HEADS = 8
DN_DIM = 128
DN_W = 1024
DN_CHUNK = 64
ODD_IN_W = 4112
ODD_IN_PAD = 4224
D_FF = 2816
NEG = -1e30
VMEM_LIMIT = 56 * 1024 * 1024
ADAM_LR, ADAM_B1, ADAM_B2, ADAM_EPS, ADAM_WD, ADAM_STEP = 0.001, 0.9, 0.999, 1e-08, 0.01, 10
MESH = pl.DeviceIdType.MESH


def _cp(*sem):
    return pltpu.CompilerParams(dimension_semantics=sem, vmem_limit_bytes=VMEM_LIMIT)


def _pick(n, cap):
    best = 128
    for t in range(128, cap + 1, 128):
        if n % t == 0:
            best = t
    return best


def _mx(a, b):
    return jnp.dot(a.astype(MXU_DTYPE), b.astype(MXU_DTYPE), preferred_element_type=F32)


def _mx_nt(a, b):
    return lax.dot_general(a.astype(MXU_DTYPE), b.astype(MXU_DTYPE), (((1,), (1,)), ((), ())),
                           preferred_element_type=F32)


def _mx_tn(a, b):
    return lax.dot_general(a.astype(MXU_DTYPE), b.astype(MXU_DTYPE), (((0,), (0,)), ((), ())),
                           preferred_element_type=F32)


def _hi(a, b):
    return jnp.dot(a, b, precision=HI, preferred_element_type=F32)


def _hi_nt(a, b):
    return lax.dot_general(a, b, (((1,), (1,)), ((), ())), precision=HI, preferred_element_type=F32)


def _hi_tn(a, b):
    return lax.dot_general(a, b, (((0,), (0,)), ((), ())), precision=HI, preferred_element_type=F32)


def _sigmoid(x):
    return 0.5 * jnp.tanh(0.5 * x) + 0.5


def _softplus(x):
    return jnp.maximum(x, 0.0) + jnp.log(1.0 + jnp.exp(-jnp.abs(x)))


def mm_nn_res_norm(a, b, res, g, name, tm=512):
    t, k = a.shape
    d = b.shape[1]
    tm = min(tm, t)

    def body(a_ref, b_ref, res_ref, g_ref, y_ref, h_ref, ht_ref):
        y = res_ref[...] + _mx(a_ref[...], b_ref[...])
        y_ref[...] = y
        h = y * lax.rsqrt(jnp.mean(y * y, axis=-1, keepdims=True) + EPS) * g_ref[...]
        h_ref[...] = h.astype(h_ref.dtype)
        ht_ref[...] = h.T.astype(ht_ref.dtype)

    row = pl.BlockSpec((tm, d), lambda i: (i, 0))
    return pl.pallas_call(
        body, name=name, grid=(t // tm,),
        in_specs=[pl.BlockSpec((tm, k), lambda i: (i, 0)), pl.BlockSpec((k, d), lambda i: (0, 0)), row,
                  pl.BlockSpec((1, d), lambda i: (0, 0))],
        out_specs=(row, row, pl.BlockSpec((d, tm), lambda i: (0, i))),
        out_shape=(jax.ShapeDtypeStruct((t, d), F32), jax.ShapeDtypeStruct((t, d), MXU_DTYPE),
                   jax.ShapeDtypeStruct((d, t), MXU_DTYPE)),
        compiler_params=_cp("parallel"))(a, b, res, g)


def mm_nt(a, b, name, out_dtype=F32, tm=1024, after=None):
    m, k = a.shape
    n, _ = b.shape
    tn = _pick(n, 512 if k > 3000 else 1536)
    tm = min(tm, m)

    def body(a_ref, b_ref, *rest):
        o_ref = rest[-1]
        o_ref[...] = _mx_nt(a_ref[...], b_ref[...]).astype(o_ref.dtype)

    in_specs = [pl.BlockSpec((tm, k), lambda j, i: (i, 0)), pl.BlockSpec((tn, k), lambda j, i: (j, 0))]
    args = [a, b]
    if after is not None:
        in_specs.append(pl.BlockSpec(memory_space=pl.ANY))
        args.append(after)
    return pl.pallas_call(
        body, name=name, grid=(n // tn, m // tm), in_specs=in_specs,
        out_specs=pl.BlockSpec((tm, tn), lambda j, i: (i, j)),
        out_shape=jax.ShapeDtypeStruct((m, n), out_dtype), compiler_params=_cp("parallel", "parallel"))(*args)


def mm_at(at, b, name, tk=1024, transposed=False, tn=None, row_block=None):
    m, kk = at.shape
    _, n = b.shape
    tm, tn, tk = _pick(m, 1408), tn or _pick(n, 2816), min(tk, kk)
    nk = kk // tk

    def body(a_ref, b_ref, o_ref, acc_ref):
        k = pl.program_id(2)
        p = _mx(a_ref[...], b_ref[...])
        acc = jnp.where(k == 0, p, acc_ref[...] + p)
        acc_ref[...] = acc

        @pl.when(k == nk - 1)
        def _():
            o_ref[...] = (acc.T if transposed else acc).astype(o_ref.dtype)

    if transposed:
        rb = row_block or (lambda j: j)
        out_spec = pl.BlockSpec((tn, tm), lambda i, j, k: (rb(j), i))
        out_shape = jax.ShapeDtypeStruct((n, m), MXU_DTYPE)
    else:
        out_spec = pl.BlockSpec((tm, tn), lambda i, j, k: (i, j))
        out_shape = jax.ShapeDtypeStruct((m, n), MXU_DTYPE)
    return pl.pallas_call(
        body, name=name, grid=(m // tm, n // tn, nk),
        in_specs=[pl.BlockSpec((tm, tk), lambda i, j, k: (i, k)), pl.BlockSpec((tk, tn), lambda i, j, k: (k, j))],
        out_specs=out_spec, out_shape=out_shape, scratch_shapes=[pltpu.VMEM((tm, tn), F32)],
        compiler_params=_cp("parallel", "parallel", "arbitrary"))(at, b)


def rms_fwd(x, g, name, tm=512, after=None):
    t, d = x.shape

    def body(x_ref, g_ref, *rest):
        o_ref, ot_ref = rest[-2:]
        xv = x_ref[...]
        r = lax.rsqrt(jnp.mean(xv * xv, axis=-1, keepdims=True) + EPS)
        h = xv * r * g_ref[...]
        o_ref[...] = h.astype(o_ref.dtype)
        ot_ref[...] = h.T.astype(ot_ref.dtype)

    in_specs = [pl.BlockSpec((tm, d), lambda i: (i, 0)), pl.BlockSpec((1, d), lambda i: (0, 0))]
    args = [x, g]
    if after is not None:
        in_specs.append(pl.BlockSpec(memory_space=pl.ANY))
        args.append(after)
    return pl.pallas_call(
        body, name=name, grid=(t // tm,), in_specs=in_specs,
        out_specs=(pl.BlockSpec((tm, d), lambda i: (i, 0)), pl.BlockSpec((d, tm), lambda i: (0, i))),
        out_shape=(jax.ShapeDtypeStruct((t, d), MXU_DTYPE), jax.ShapeDtypeStruct((d, t), MXU_DTYPE)),
        compiler_params=_cp("parallel"))(*args)


def mm_rms_bwd(a, bt, x, g, dres, name, tm=512, after=None, chunks=None):
    t, k = a.shape
    d = bt.shape[1]
    tm = min(tm, t)
    chunks = chunks or ((0, 0, k),)

    def body(a_ref, b_ref, x_ref, g_ref, dres_ref, *rest):
        dx_ref, dg_ref = rest[-2:]
        dhv = None
        for ca, cb, size in chunks:
            part = _mx(a_ref[:, ca:ca + size], b_ref[cb:cb + size, :])
            dhv = part if dhv is None else dhv + part
        xv = x_ref[...]
        r = lax.rsqrt(jnp.mean(xv * xv, axis=-1, keepdims=True) + EPS)
        xh = xv * r
        dxh = dhv * g_ref[...]
        dx_ref[...] = dres_ref[...] + r * (dxh - xh * jnp.mean(dxh * xh, axis=-1, keepdims=True))
        part = jnp.sum(dhv * xh, axis=0, keepdims=True)
        dg_ref[...] = jnp.where(pl.program_id(0) == 0, part, dg_ref[...] + part)

    row = pl.BlockSpec((tm, d), lambda i: (i, 0))
    one = pl.BlockSpec((1, d), lambda i: (0, 0))
    in_specs = [pl.BlockSpec((tm, k), lambda i: (i, 0)), pl.BlockSpec((k, d), lambda i: (0, 0)), row, one, row]
    args = [a, bt, x, g, dres]
    if after is not None:
        in_specs.append(pl.BlockSpec(memory_space=pl.ANY))
        args.append(after)
    return pl.pallas_call(
        body, name=name, grid=(t // tm,), in_specs=in_specs, out_specs=(row, one),
        out_shape=(jax.ShapeDtypeStruct((t, d), F32), jax.ShapeDtypeStruct((1, d), F32)),
        compiler_params=_cp("arbitrary"))(*args)


GU_TILE = 1408


def ffn_up(f, wt, name, tm=512):
    t, d = f.shape
    nj = D_FF // GU_TILE

    def body(f_ref, wg_ref, wu_ref, gu_ref, a_ref, at_ref):
        g = _mx_nt(f_ref[...], wg_ref[...])
        u = _mx_nt(f_ref[...], wu_ref[...])
        sg = _sigmoid(g)
        gs = g * sg
        gu_ref[:, :GU_TILE] = (u * (sg + gs - gs * sg)).astype(gu_ref.dtype)
        gu_ref[:, GU_TILE:] = gs.astype(gu_ref.dtype)
        act = gs * u
        a_ref[...] = act.astype(a_ref.dtype)
        at_ref[...] = act.T.astype(at_ref.dtype)

    return pl.pallas_call(
        body, name=name, grid=(nj, t // tm),
        in_specs=[pl.BlockSpec((tm, d), lambda j, i: (i, 0)), pl.BlockSpec((GU_TILE, d), lambda j, i: (j, 0)),
                  pl.BlockSpec((GU_TILE, d), lambda j, i: (nj + j, 0))],
        out_specs=(pl.BlockSpec((tm, 2 * GU_TILE), lambda j, i: (i, j)), pl.BlockSpec((tm, GU_TILE), lambda j, i: (i, j)),
                   pl.BlockSpec((GU_TILE, tm), lambda j, i: (j, i))),
        out_shape=(jax.ShapeDtypeStruct((t, 2 * D_FF), MXU_DTYPE), jax.ShapeDtypeStruct((t, D_FF), MXU_DTYPE),
                   jax.ShapeDtypeStruct((D_FF, t), MXU_DTYPE)),
        compiler_params=_cp("parallel", "parallel"))(f, wt, wt)


_GU_CHUNKS = tuple((q * GU_TILE, ((q % 2) * (D_FF // GU_TILE) + q // 2) * GU_TILE, GU_TILE)
                   for q in range(2 * D_FF // GU_TILE))


def ffn_dact(dy, w_d, gu, name, tm=512, after=None):
    t, d = dy.shape

    def body(dy_ref, w_ref, gu_ref, *rest):
        o_ref = rest[-1]
        da = _mx_nt(dy_ref[...], w_ref[...])
        o_ref[:, :GU_TILE] = (da * gu_ref[:, :GU_TILE]).astype(o_ref.dtype)
        o_ref[:, GU_TILE:] = (da * gu_ref[:, GU_TILE:]).astype(o_ref.dtype)

    in_specs = [pl.BlockSpec((tm, d), lambda j, i: (i, 0)), pl.BlockSpec((GU_TILE, d), lambda j, i: (j, 0)),
                pl.BlockSpec((tm, 2 * GU_TILE), lambda j, i: (i, j))]
    args = [dy, w_d, gu]
    if after is not None:
        in_specs.append(pl.BlockSpec(memory_space=pl.ANY))
        args.append(after)
    return pl.pallas_call(
        body, name=name, grid=(D_FF // GU_TILE, t // tm), in_specs=in_specs,
        out_specs=pl.BlockSpec((tm, 2 * GU_TILE), lambda j, i: (i, j)),
        out_shape=jax.ShapeDtypeStruct((t, 2 * D_FF), MXU_DTYPE), compiler_params=_cp("parallel", "parallel"))(*args)


def mm_nn_res_loss(a, b, res, target, name, tm=512):
    t, k = a.shape
    d = b.shape[1]
    tm = min(tm, t)

    def body(a_ref, b_ref, res_ref, t_ref, dy_ref, l_ref):
        e = res_ref[...] + _mx(a_ref[...], b_ref[...]) - t_ref[...]
        dy_ref[...] = e * (1.0 / d)
        part = jnp.zeros((1, 128), F32) + 0.5 * jnp.sum(jnp.mean(e * e, axis=-1, keepdims=True), axis=0, keepdims=True)
        l_ref[...] = jnp.where(pl.program_id(0) == 0, part, l_ref[...] + part)

    row = pl.BlockSpec((tm, d), lambda i: (i, 0))
    return pl.pallas_call(
        body, name=name, grid=(t // tm,),
        in_specs=[pl.BlockSpec((tm, k), lambda i: (i, 0)), pl.BlockSpec((k, d), lambda i: (0, 0)), row, row],
        out_specs=(row, pl.BlockSpec((1, 128), lambda i: (0, 0))),
        out_shape=(jax.ShapeDtypeStruct((t, d), F32), jax.ShapeDtypeStruct((1, 128), F32)),
        compiler_params=_cp("arbitrary"))(a, b, res, target)


QK_W = Q_W + KV_W
_QK_TILE = 256


def _qk_mats():
    idx = np.arange(_QK_TILE)
    half = HEAD_DIM // 2
    same = (idx[:, None] // HEAD_DIM) == (idx[None, :] // HEAD_DIM)
    lo = (idx % HEAD_DIM) < half
    rot = np.where((idx[:, None] == idx[None, :] + half) & lo[None, :], -1.0, 0.0)
    rot = rot + np.where((idx[:, None] == idx[None, :] - half) & ~lo[None, :], 1.0, 0.0)
    return jnp.asarray(same, F32), jnp.asarray(rot, F32)


def _qk_rows(q_gain, k_gain, cosf, sinf):
    gain = jnp.concatenate([q_gain] * ATTN_HEADS + [k_gain] * KV_HEADS, axis=-1)
    return gain, jnp.concatenate([cosf, cosf], axis=-1), jnp.concatenate([sinf, sinf], axis=-1)


def _qk_tiles(a, mat, transposed=False):
    outs = []
    for c0 in range(0, QK_W, _QK_TILE):
        w = min(_QK_TILE, QK_W - c0)
        mt = (mat.T if transposed else mat)[:w, :w].astype(MXU_DTYPE)
        at = a[:, c0:c0 + w]
        hi = at.astype(MXU_DTYPE)
        lo = (at - hi.astype(F32)).astype(MXU_DTYPE)
        outs.append(jnp.dot(hi, mt, preferred_element_type=F32) + jnp.dot(lo, mt, preferred_element_type=F32))
    return jnp.concatenate(outs, axis=-1)


def qk_prep_fwd(proj, q_gain, k_gain, cosf, sinf, name, tm=256):
    t = proj.shape[0]
    gmat, rmat = _qk_mats()
    gain, c2, s2 = _qk_rows(q_gain, k_gain, cosf, sinf)
    rep = QK_W // 128

    def body(p_ref, g_ref, c_ref, s_ref, gm_ref, rm_ref, q_ref, k_ref):
        x = p_ref[...]
        r = lax.rsqrt(_qk_tiles(x * x, gm_ref[...]) * (1.0 / HEAD_DIM) + EPS)
        xn = x * r * g_ref[...]
        c = jnp.concatenate([c_ref[...]] * rep, axis=-1)
        s = jnp.concatenate([s_ref[...]] * rep, axis=-1)
        out = xn * c + _qk_tiles(xn, rm_ref[...]) * s
        q_ref[...] = out[:, :Q_W]
        k_ref[...] = out[:, Q_W:]

    full = pl.BlockSpec((_QK_TILE, _QK_TILE), lambda i: (0, 0))
    tab = pl.BlockSpec((tm, 128), lambda i: (i, 0))
    return pl.pallas_call(
        body, name=name, grid=(t // tm,),
        in_specs=[pl.BlockSpec((tm, QK_W), lambda i: (i, 0)), pl.BlockSpec((1, QK_W), lambda i: (0, 0)), tab, tab,
                  full, full],
        out_specs=(pl.BlockSpec((tm, Q_W), lambda i: (i, 0)), pl.BlockSpec((tm, KV_W), lambda i: (i, 0))),
        out_shape=(jax.ShapeDtypeStruct((t, Q_W), F32), jax.ShapeDtypeStruct((t, KV_W), F32)),
        compiler_params=_cp("parallel"))(proj, gain, c2, s2, gmat, rmat)


def qk_prep_bwd(proj, q_gain, k_gain, cosf, sinf, dq, dk, name, tm=256):
    t = proj.shape[0]
    gmat, rmat = _qk_mats()
    gain, c2, s2 = _qk_rows(q_gain, k_gain, cosf, sinf)
    rep = QK_W // 128
    lanes = np.arange(QK_W)[:, None]
    fold = jnp.asarray(lanes % HEAD_DIM + np.where(lanes >= Q_W, HEAD_DIM, 0) == np.arange(128)[None, :], F32)

    def body(p_ref, g_ref, c_ref, s_ref, gm_ref, rm_ref, f_ref, dq_ref, dk_ref, o_ref, dg_ref):
        x = p_ref[...]
        r = lax.rsqrt(_qk_tiles(x * x, gm_ref[...]) * (1.0 / HEAD_DIM) + EPS)
        xh = x * r
        c = jnp.concatenate([c_ref[...]] * rep, axis=-1)
        s = jnp.concatenate([s_ref[...]] * rep, axis=-1)
        dout = jnp.concatenate([dq_ref[...], dk_ref[...]], axis=-1)
        dxn = dout * c + _qk_tiles(dout * s, rm_ref[...], transposed=True)
        part = _hi(jnp.sum(dxn * xh, axis=0, keepdims=True), f_ref[...])
        dxh = dxn * g_ref[...]
        mean = _qk_tiles(dxh * xh, gm_ref[...]) * (1.0 / HEAD_DIM)
        o_ref[...] = (r * (dxh - xh * mean)).astype(o_ref.dtype)
        dg_ref[...] = jnp.where(pl.program_id(0) == 0, part, dg_ref[...] + part)

    full = pl.BlockSpec((_QK_TILE, _QK_TILE), lambda i: (0, 0))
    tab = pl.BlockSpec((tm, 128), lambda i: (i, 0))
    dqk, dg = pl.pallas_call(
        body, name=name, grid=(t // tm,),
        in_specs=[pl.BlockSpec((tm, QK_W), lambda i: (i, 0)), pl.BlockSpec((1, QK_W), lambda i: (0, 0)), tab, tab,
                  full, full, pl.BlockSpec((QK_W, 128), lambda i: (0, 0)),
                  pl.BlockSpec((tm, Q_W), lambda i: (i, 0)), pl.BlockSpec((tm, KV_W), lambda i: (i, 0))],
        out_specs=(pl.BlockSpec((tm, QK_W), lambda i: (i, 0)), pl.BlockSpec((1, 128), lambda i: (0, 0))),
        out_shape=(jax.ShapeDtypeStruct((t, QK_W), MXU_DTYPE), jax.ShapeDtypeStruct((1, 128), F32)),
        compiler_params=_cp("arbitrary"))(proj, gain, c2, s2, gmat, rmat, fold, dq, dk)
    return dqk, dg[:, :HEAD_DIM], dg[:, HEAD_DIM:]


def _swa_valid(n, grp):
    qi = lax.broadcasted_iota(jnp.int32, (grp * ATTN_BLOCK, 2 * ATTN_BLOCK), 0) & (ATTN_BLOCK - 1)
    kj = lax.broadcasted_iota(jnp.int32, (grp * ATTN_BLOCK, 2 * ATTN_BLOCK), 1)
    diff = qi + ATTN_BLOCK - kj
    return (diff >= 0) & (diff < ATTN_BLOCK) & (n * ATTN_BLOCK - ATTN_BLOCK + kj >= 0)


def _stack_heads(ref, g, grp):
    return jnp.concatenate([ref[:, (g * grp + j) * HEAD_DIM:(g * grp + j + 1) * HEAD_DIM] for j in range(grp)], axis=0)


def _stack_sinks(s_ref, g, grp):
    return jnp.concatenate([jnp.zeros((ATTN_BLOCK, 1), F32) + s_ref[0:1, g * grp + j:g * grp + j + 1]
                            for j in range(grp)], axis=0)


def swa_fwd(q, k, proj, sinks, name):
    t = q.shape[0]
    nb = t // ATTN_BLOCK
    scale = HEAD_DIM ** -0.5
    grp = ATTN_HEADS // KV_HEADS

    def body(q_ref, kc_ref, kp_ref, vc_ref, vp_ref, s_ref, y_ref, mix_ref, yt_ref, lse_ref):
        n = pl.program_id(0)
        valid = _swa_valid(n, grp)
        kk = jnp.concatenate([kp_ref[...], kc_ref[...]], axis=0).astype(MXU_DTYPE)
        vv = jnp.concatenate([vp_ref[...], vc_ref[...]], axis=0).astype(MXU_DTYPE)
        lane = lax.broadcasted_iota(jnp.int32, (ATTN_BLOCK, ATTN_HEADS), 1)
        gs = range(KV_HEADS)
        qg = [_stack_heads(q_ref, g, grp) for g in gs]
        sink = [_stack_sinks(s_ref, g, grp) for g in gs]
        sc = [jnp.where(valid, _mx_nt(qg[g], kk[:, g * HEAD_DIM:(g + 1) * HEAD_DIM]) * scale, NEG) for g in gs]
        m = [jnp.maximum(jnp.max(sc[g], axis=-1, keepdims=True), sink[g]) for g in gs]
        e = [jnp.exp(sc[g] - m[g]) for g in gs]
        den = [jnp.sum(e[g], axis=-1, keepdims=True) + jnp.exp(sink[g] - m[g]) for g in gs]
        og = [_mx(e[g] / den[g], vv[:, g * HEAD_DIM:(g + 1) * HEAD_DIM]) for g in gs]
        lg = [m[g] + jnp.log(den[g]) for g in gs]
        lse = jnp.zeros((ATTN_BLOCK, ATTN_HEADS), F32)
        outs = []
        for h in range(ATTN_HEADS):
            rows = slice((h % grp) * ATTN_BLOCK, (h % grp + 1) * ATTN_BLOCK)
            outs.append(og[h // grp][rows])
            lse = jnp.where(lane == h, lg[h // grp][rows], lse)
        y = jnp.concatenate(outs, axis=-1)
        y_ref[...] = y
        mix_ref[...] = y.astype(mix_ref.dtype)
        yt_ref[...] = y.T.astype(yt_ref.dtype)
        lse_ref[...] = lse

    cur = lambda n: (n, 0)
    prev = lambda n: (jnp.maximum(n - 1, 0), 0)
    vcol = (Q_W + KV_W) // KV_W
    return pl.pallas_call(
        body, name=name, grid=(nb,),
        in_specs=[pl.BlockSpec((ATTN_BLOCK, Q_W), cur), pl.BlockSpec((ATTN_BLOCK, KV_W), cur),
                  pl.BlockSpec((ATTN_BLOCK, KV_W), prev),
                  pl.BlockSpec((ATTN_BLOCK, KV_W), lambda n: (n, vcol)),
                  pl.BlockSpec((ATTN_BLOCK, KV_W), lambda n: (jnp.maximum(n - 1, 0), vcol)),
                  pl.BlockSpec((1, ATTN_HEADS), lambda n: (0, 0))],
        out_specs=(pl.BlockSpec((ATTN_BLOCK, Q_W), cur), pl.BlockSpec((ATTN_BLOCK, Q_W), cur),
                   pl.BlockSpec((Q_W, ATTN_BLOCK), lambda n: (0, n)), pl.BlockSpec((ATTN_BLOCK, ATTN_HEADS), cur)),
        out_shape=(jax.ShapeDtypeStruct((t, Q_W), F32), jax.ShapeDtypeStruct((t, Q_W + CONV_CH), MXU_DTYPE),
                   jax.ShapeDtypeStruct((Q_W + CONV_CH, t), MXU_DTYPE), jax.ShapeDtypeStruct((t, ATTN_HEADS), F32)),
        compiler_params=_cp("parallel"))(q, k, k, proj, proj, sinks)


def swa_bwd(q, k, proj, sinks, y, lse, dmix, name):
    t = q.shape[0]
    nb = t // ATTN_BLOCK
    scale = HEAD_DIM ** -0.5
    grp = ATTN_HEADS // KV_HEADS

    def body(q_ref, kc_ref, kp_ref, vc_ref, vp_ref, s_ref, y_ref, lse_ref, dy_ref,
             dq_ref, dk_ref, dv_ref, ds_ref, dkc, dvc):
        n = pl.program_id(0)

        @pl.when(n == 0)
        def _():
            dkc[...] = jnp.zeros_like(dkc)
            dvc[...] = jnp.zeros_like(dvc)
            ds_ref[...] = jnp.zeros_like(ds_ref)

        @pl.when(n < nb)
        def _():
            valid = _swa_valid(n, grp)
            kk = jnp.concatenate([kp_ref[...], kc_ref[...]], axis=0).astype(MXU_DTYPE)
            vv = jnp.concatenate([vp_ref[...], vc_ref[...]], axis=0).astype(MXU_DTYPE)
            lane = lax.broadcasted_iota(jnp.int32, (1, ATTN_HEADS), 1)
            gs = range(KV_HEADS)
            kg = [kk[:, g * HEAD_DIM:(g + 1) * HEAD_DIM] for g in gs]
            vg = [vv[:, g * HEAD_DIM:(g + 1) * HEAD_DIM] for g in gs]
            qg = [_stack_heads(q_ref, g, grp).astype(MXU_DTYPE) for g in gs]
            dog = [_stack_heads(dy_ref, g, grp) for g in gs]
            og = [_stack_heads(y_ref, g, grp) for g in gs]
            lg = [jnp.concatenate([lse_ref[:, g * grp + j:g * grp + j + 1] for j in range(grp)], axis=0) for g in gs]
            sink = [_stack_sinks(s_ref, g, grp) for g in gs]
            sc = [jnp.where(valid, _mx_nt(qg[g], kg[g]) * scale, NEG) for g in gs]
            p = [jnp.exp(sc[g] - lg[g]) for g in gs]
            delta = [jnp.sum(dog[g] * og[g], axis=-1, keepdims=True) for g in gs]
            ds = [p[g] * (_mx_nt(dog[g], vg[g]) - delta[g]) for g in gs]
            dqg = [_mx(ds[g], kg[g]) * scale for g in gs]
            dkf = jnp.concatenate([_mx_tn(ds[g], qg[g]) * scale for g in gs], axis=-1)
            dvf = jnp.concatenate([_mx_tn(p[g], dog[g]) for g in gs], axis=-1)
            dsk = [jnp.exp(sink[g] - lg[g]) * delta[g] for g in gs]
            dsink = jnp.zeros((1, ATTN_HEADS), F32)
            dqs = []
            for h in range(ATTN_HEADS):
                rows = slice((h % grp) * ATTN_BLOCK, (h % grp + 1) * ATTN_BLOCK)
                dqs.append(dqg[h // grp][rows])
                dsink = jnp.where(lane == h, -jnp.sum(dsk[h // grp][rows], axis=0, keepdims=True), dsink)
            dq_ref[...] = jnp.concatenate(dqs, axis=-1)
            dk_ref[...] = dkc[...] + dkf[:ATTN_BLOCK]
            dv_ref[...] = (dvc[...] + dvf[:ATTN_BLOCK]).astype(dv_ref.dtype)
            dkc[...] = dkf[ATTN_BLOCK:]
            dvc[...] = dvf[ATTN_BLOCK:]
            ds_ref[...] += dsink

        @pl.when(n == nb)
        def _():
            dk_ref[...] = dkc[...]
            dv_ref[...] = dvc[...].astype(dv_ref.dtype)

    cur = lambda n: (jnp.minimum(n, nb - 1), 0)
    prev = lambda n: (jnp.clip(n - 1, 0, nb - 1), 0)
    vcol = (Q_W + KV_W) // KV_W
    return pl.pallas_call(
        body, name=name, grid=(nb + 1,),
        in_specs=[pl.BlockSpec((ATTN_BLOCK, Q_W), cur), pl.BlockSpec((ATTN_BLOCK, KV_W), cur),
                  pl.BlockSpec((ATTN_BLOCK, KV_W), prev),
                  pl.BlockSpec((ATTN_BLOCK, KV_W), lambda n: (jnp.minimum(n, nb - 1), vcol)),
                  pl.BlockSpec((ATTN_BLOCK, KV_W), lambda n: (jnp.clip(n - 1, 0, nb - 1), vcol)),
                  pl.BlockSpec((1, ATTN_HEADS), lambda n: (0, 0)),
                  pl.BlockSpec((ATTN_BLOCK, Q_W), cur), pl.BlockSpec((ATTN_BLOCK, ATTN_HEADS), cur),
                  pl.BlockSpec((ATTN_BLOCK, Q_W), cur)],
        out_specs=(pl.BlockSpec((ATTN_BLOCK, Q_W), cur), pl.BlockSpec((ATTN_BLOCK, KV_W), prev),
                   pl.BlockSpec((ATTN_BLOCK, KV_W), prev), pl.BlockSpec((1, ATTN_HEADS), lambda n: (0, 0))),
        out_shape=(jax.ShapeDtypeStruct((t, Q_W), F32), jax.ShapeDtypeStruct((t, KV_W), F32),
                   jax.ShapeDtypeStruct((t, KV_W), MXU_DTYPE), jax.ShapeDtypeStruct((1, ATTN_HEADS), F32)),
        scratch_shapes=[pltpu.VMEM((ATTN_BLOCK, KV_W), F32), pltpu.VMEM((ATTN_BLOCK, KV_W), F32)],
        compiler_params=_cp("arbitrary"))(q, k, k, proj, proj, sinks, y, lse, dmix)


GC_W = 256
_GB0, _GC0, _XI0 = 768 // GC_W, 1280 // GC_W, 1792 // GC_W
HALO = 8


def gconv_fwd(proj, conv_w, mix, mix_t, name, tm=512):
    t = proj.shape[0]
    hb = tm // HALO
    half = Q_W // GC_W

    def body(gb_ref, gc_ref, xi_ref, gch_ref, xih_ref, w_ref, mix_in, mixt_in, y_ref, yt_ref):
        i = pl.program_id(1)
        u = gc_ref[...] * xi_ref[...]
        uh = jnp.where(i == 0, 0.0, gch_ref[...] * xih_ref[...])
        up = jnp.concatenate([uh, u], axis=0)
        cv = w_ref[0:1, :] * up[HALO - 2:HALO - 2 + tm]
        cv = cv + w_ref[1:2, :] * up[HALO - 1:HALO - 1 + tm]
        cv = cv + w_ref[2:3, :] * u
        y = gb_ref[...] * cv
        y_ref[...] = y.astype(y_ref.dtype)
        yt_ref[...] = y.T.astype(yt_ref.dtype)

    def col(c0):
        return pl.BlockSpec((tm, GC_W), lambda cj, i: (i, c0 + cj))

    def halo(c0):
        return pl.BlockSpec((HALO, GC_W), lambda cj, i: (jnp.maximum(i * hb - 1, 0), c0 + cj))

    return pl.pallas_call(
        body, name=name, grid=(CONV_CH // GC_W, t // tm),
        in_specs=[col(_GB0), col(_GC0), col(_XI0), halo(_GC0), halo(_XI0),
                  pl.BlockSpec((3, GC_W), lambda cj, i: (0, cj)),
                  pl.BlockSpec(memory_space=pl.ANY), pl.BlockSpec(memory_space=pl.ANY)],
        out_specs=(pl.BlockSpec((tm, GC_W), lambda cj, i: (i, half + cj)),
                   pl.BlockSpec((GC_W, tm), lambda cj, i: (half + cj, i))),
        out_shape=(jax.ShapeDtypeStruct(mix.shape, mix.dtype), jax.ShapeDtypeStruct(mix_t.shape, mix_t.dtype)),
        input_output_aliases={6: 0, 7: 1},
        compiler_params=_cp("parallel", "parallel"))(proj, proj, proj, proj, proj, conv_w, mix, mix_t)


def gconv_bwd(proj, conv_w, dmix, name, tm=512):
    t = proj.shape[0]
    hb = tm // HALO
    nt = t // tm
    dy0 = Q_W // GC_W

    def body(gb_ref, gc_ref, xi_ref, gch_ref, xih_ref, gbn_ref, dyn_ref, dy_ref, w_ref,
             dgb_ref, dgc_ref, dxi_ref, dw_ref):
        i = pl.program_id(1)
        gc, xi, gb, dy = gc_ref[...], xi_ref[...], gb_ref[...], dy_ref[...]
        u = gc * xi
        uh = jnp.where(i == 0, 0.0, gch_ref[...] * xih_ref[...])
        up = jnp.concatenate([uh, u], axis=0)
        u2 = up[HALO - 2:HALO - 2 + tm]
        u1 = up[HALO - 1:HALO - 1 + tm]
        cv = w_ref[0:1, :] * u2 + w_ref[1:2, :] * u1 + w_ref[2:3, :] * u
        dgb_ref[...] = (dy * cv).astype(dgb_ref.dtype)
        dcv = dy * gb
        dcvn = jnp.where(i == nt - 1, 0.0, dyn_ref[...] * gbn_ref[...])
        dcvp = jnp.concatenate([dcv, dcvn], axis=0)
        du = w_ref[0:1, :] * dcvp[2:2 + tm] + w_ref[1:2, :] * dcvp[1:1 + tm] + w_ref[2:3, :] * dcv
        dgc_ref[...] = (du * xi).astype(dgc_ref.dtype)
        dxi_ref[...] = (du * gc).astype(dxi_ref.dtype)
        dw = jnp.concatenate([jnp.sum(dcv * u2, axis=0, keepdims=True), jnp.sum(dcv * u1, axis=0, keepdims=True),
                              jnp.sum(dcv * u, axis=0, keepdims=True)], axis=0)

        @pl.when(i == 0)
        def _():
            dw_ref[...] = dw

        @pl.when(i > 0)
        def _():
            dw_ref[...] += dw

    def col(c0):
        return pl.BlockSpec((tm, GC_W), lambda cj, i: (i, c0 + cj))

    def halo(c0):
        return pl.BlockSpec((HALO, GC_W), lambda cj, i: (jnp.maximum(i * hb - 1, 0), c0 + cj))

    def nxt(c0):
        return pl.BlockSpec((HALO, GC_W), lambda cj, i: (jnp.minimum((i + 1) * hb, t // HALO - 1), c0 + cj))

    out = pl.BlockSpec((tm, GC_W), lambda cj, i: (i, cj))
    return pl.pallas_call(
        body, name=name, grid=(CONV_CH // GC_W, nt),
        in_specs=[col(_GB0), col(_GC0), col(_XI0), halo(_GC0), halo(_XI0), nxt(_GB0), nxt(dy0), col(dy0),
                  pl.BlockSpec((3, GC_W), lambda cj, i: (0, cj))],
        out_specs=(out, out, out, pl.BlockSpec((3, GC_W), lambda cj, i: (0, cj))),
        out_shape=(jax.ShapeDtypeStruct((t, CONV_CH), MXU_DTYPE),) * 3 + (jax.ShapeDtypeStruct((3, CONV_CH), F32),),
        compiler_params=_cp("parallel", "arbitrary"))(proj, proj, proj, proj, proj, proj, dmix, dmix, conv_w)


_QKV_W = 3 * DN_W
_BA_COL = (4 * DN_W) // 128
_Z_COL = _QKV_W // DN_W


def gdn_prep_fwd(proj, conv_w, alog_row, dtb_row, name, tm=256):
    t = proj.shape[0]
    hb = tm // HALO
    qscale = DN_DIM ** -0.5

    def body(x_ref, xh_ref, w_ref, ba_ref, al_ref, dt_ref, q_ref, k_ref, v_ref, bg_ref, c_ref):
        i = pl.program_id(0)
        for gi in range(3 * DN_HEADS):
            sl = slice(gi * DN_DIM, (gi + 1) * DN_DIM)
            xp = jnp.concatenate([jnp.where(i == 0, 0.0, xh_ref[:, sl]), x_ref[:, sl]], axis=0)
            c = w_ref[0:1, sl] * xp[HALO - 3:HALO - 3 + tm]
            for j in range(1, 4):
                c = c + w_ref[j:j + 1, sl] * xp[HALO - 3 + j:HALO - 3 + j + tm]
            c_ref[:, sl] = c
            s = c * _sigmoid(c)
            osl = slice((gi % DN_HEADS) * DN_DIM, (gi % DN_HEADS + 1) * DN_DIM)
            if gi < DN_HEADS:
                q_ref[:, osl] = s * lax.rsqrt(jnp.sum(s * s, axis=-1, keepdims=True) + EPS) * qscale
            elif gi < 2 * DN_HEADS:
                k_ref[:, osl] = s * lax.rsqrt(jnp.sum(s * s, axis=-1, keepdims=True) + EPS)
            else:
                v_ref[:, osl] = s
        ba = ba_ref[...]
        lane = lax.broadcasted_iota(jnp.int32, ba.shape, 1)
        gval = -jnp.exp(al_ref[...]) * _softplus(ba + dt_ref[...])
        bg_ref[...] = jnp.where(lane < DN_HEADS, _sigmoid(ba), jnp.where(lane < 2 * DN_HEADS, gval, 0.0))

    row = pl.BlockSpec((tm, DN_W), lambda i: (i, 0))
    one = pl.BlockSpec((1, 128), lambda i: (0, 0))
    return pl.pallas_call(
        body, name=name, grid=(t // tm,),
        in_specs=[pl.BlockSpec((tm, _QKV_W), lambda i: (i, 0)),
                  pl.BlockSpec((HALO, _QKV_W), lambda i: (jnp.maximum(i * hb - 1, 0), 0)),
                  pl.BlockSpec((4, _QKV_W), lambda i: (0, 0)),
                  pl.BlockSpec((tm, 128), lambda i: (i, _BA_COL)), one, one],
        out_specs=(row, row, row, pl.BlockSpec((tm, 128), lambda i: (i, 0)), pl.BlockSpec((tm, _QKV_W), lambda i: (i, 0))),
        out_shape=(jax.ShapeDtypeStruct((t, DN_W), F32),) * 3 + (jax.ShapeDtypeStruct((t, 128), F32),
                                                                 jax.ShapeDtypeStruct((t, _QKV_W), F32)),
        compiler_params=_cp("parallel"))(proj, proj, conv_w, proj, alog_row, dtb_row)


def gdn_prep_bwd(proj, conv, conv_w, alog_row, dtb_row, dq, dk, dv, dbg, dz, name, tm=256):
    t = proj.shape[0]
    hb = tm // HALO
    nt = t // tm
    qscale = DN_DIM ** -0.5
    te = tm + HALO

    def body(x_ref, c_ref, cn_ref, w_ref, ba_ref, al_ref, dt_ref, dq_ref, dk_ref, dv_ref,
             dqn_ref, dkn_ref, dvn_ref, dbg_ref, dz_ref, dx_ref, dw_ref, ddt_ref, dal_ref):
        i = pl.program_id(0)
        first = i == 0
        last = i == nt - 1
        dws = []
        for gi in range(3 * DN_HEADS):
            sl = slice(gi * DN_DIM, (gi + 1) * DN_DIM)
            osl = slice((gi % DN_HEADS) * DN_DIM, (gi % DN_HEADS + 1) * DN_DIM)
            c = jnp.concatenate([c_ref[:, sl], cn_ref[:, sl]], axis=0)
            sg = _sigmoid(c)
            s = c * sg
            d_ref, dn_ref = ((dq_ref, dqn_ref), (dk_ref, dkn_ref), (dv_ref, dvn_ref))[gi // DN_HEADS]
            dy = jnp.concatenate([d_ref[:, osl], jnp.where(last, 0.0, dn_ref[:, osl])], axis=0)
            if gi < 2 * DN_HEADS:
                r = lax.rsqrt(jnp.sum(s * s, axis=-1, keepdims=True) + EPS)
                sh = s * r
                ds = r * (dy - sh * jnp.sum(sh * dy, axis=-1, keepdims=True))
                if gi < DN_HEADS:
                    ds = ds * qscale
            else:
                ds = dy
            dc = ds * sg * (1.0 + c * (1.0 - sg))
            dcs = [dc[3 - j:3 - j + tm] for j in range(4)]
            dx = w_ref[0:1, sl] * dcs[0]
            for j in range(1, 4):
                dx = dx + w_ref[j:j + 1, sl] * dcs[j]
            dx_ref[:, sl] = dx.astype(dx_ref.dtype)
            x0 = x_ref[:, sl]
            dws.append(jnp.concatenate([jnp.sum(dcs[j] * x0, axis=0, keepdims=True) for j in range(4)], axis=0))
        dw = jnp.concatenate(dws, axis=-1)
        ba = ba_ref[...]
        dbgv = dbg_ref[...]
        lane = lax.broadcasted_iota(jnp.int32, ba.shape, 1)
        beta = _sigmoid(ba)
        ea = -jnp.exp(al_ref[...])
        zin = ba + dt_ref[...]
        is_b = lane < DN_HEADS
        is_a = (lane >= DN_HEADS) & (lane < 2 * DN_HEADS)
        da = jnp.where(is_a, dbgv * ea * _sigmoid(zin), 0.0)
        dx_ref[:, _QKV_W:_QKV_W + DN_W] = dz_ref[...]
        dx_ref[:, _QKV_W + DN_W:] = jnp.where(is_b, dbgv * beta * (1.0 - beta), da).astype(dx_ref.dtype)
        ddt = jnp.sum(da, axis=0, keepdims=True)
        dal = jnp.sum(jnp.where(is_a, dbgv * ea * _softplus(zin), 0.0), axis=0, keepdims=True)

        @pl.when(first)
        def _():
            dw_ref[...] = dw
            ddt_ref[...] = ddt
            dal_ref[...] = dal

        @pl.when(i > 0)
        def _():
            dw_ref[...] += dw
            ddt_ref[...] += ddt
            dal_ref[...] += dal

    row = pl.BlockSpec((tm, DN_W), lambda i: (i, 0))
    nrow = pl.BlockSpec((HALO, DN_W), lambda i: (jnp.minimum((i + 1) * hb, t // HALO - 1), 0))
    one = pl.BlockSpec((1, 128), lambda i: (0, 0))
    return pl.pallas_call(
        body, name=name, grid=(nt,),
        in_specs=[pl.BlockSpec((tm, _QKV_W), lambda i: (i, 0)), pl.BlockSpec((tm, _QKV_W), lambda i: (i, 0)),
                  pl.BlockSpec((HALO, _QKV_W), lambda i: (jnp.minimum((i + 1) * hb, t // HALO - 1), 0)),
                  pl.BlockSpec((4, _QKV_W), lambda i: (0, 0)),
                  pl.BlockSpec((tm, 128), lambda i: (i, _BA_COL)), one, one,
                  row, row, row, nrow, nrow, nrow, pl.BlockSpec((tm, 128), lambda i: (i, 0)), row],
        out_specs=(pl.BlockSpec((tm, ODD_IN_PAD), lambda i: (i, 0)), pl.BlockSpec((4, _QKV_W), lambda i: (0, 0)), one, one),
        out_shape=(jax.ShapeDtypeStruct((t, ODD_IN_PAD), MXU_DTYPE), jax.ShapeDtypeStruct((4, _QKV_W), F32),
                   jax.ShapeDtypeStruct((1, 128), F32), jax.ShapeDtypeStruct((1, 128), F32)),
        compiler_params=_cp("arbitrary"))(proj, conv, conv, conv_w, proj, alog_row, dtb_row, dq, dk, dv, dq, dk, dv, dbg,
                                          dz)


def _chunk_masks():
    r = lax.broadcasted_iota(jnp.int32, (DN_CHUNK, DN_CHUNK), 0)
    c = lax.broadcasted_iota(jnp.int32, (DN_CHUNK, DN_CHUNK), 1)
    return r >= c, r > c


INV_PACK = 2


def _inv_unit_lower_many(mats):
    n = DN_CHUNK
    wide = INV_PACK * n
    r = lax.broadcasted_iota(jnp.int32, (wide, wide), 0)
    c = lax.broadcasted_iota(jnp.int32, (wide, wide), 1)
    same = (r & -n) == (c & -n)
    eye = jnp.where((r[:n] == (c[:n] & (n - 1))), 1.0, 0.0)

    def blockdiag(row):
        return jnp.where(same, jnp.concatenate([row] * INV_PACK, axis=0), 0.0)

    packs = [jnp.concatenate(mats[g:g + INV_PACK], axis=-1) for g in range(0, len(mats), INV_PACK)]
    xs = [eye - a for a in packs]
    pws = [_hi(a, blockdiag(a)) for a in packs]
    for step in range(5):
        if step < 4:
            both = [_hi(jnp.concatenate([x, pw], axis=0), blockdiag(pw)) for x, pw in zip(xs, pws)]
            xs = [x + b[:n] for x, b in zip(xs, both)]
            pws = [b[n:] for b in both]
        else:
            xs = [x + _hi(x, blockdiag(pw)) for x, pw in zip(xs, pws)]
    return [x[:, j * n:(j + 1) * n] for x in xs for j in range(INV_PACK)]


def _chunk_common(q, k, v, beta, gc, gcr, lower, strict):
    gam = jnp.exp(jnp.where(lower, gc - gcr, NEG))
    eg = jnp.exp(gc)
    gl = gc[DN_CHUNK - 1:DN_CHUNK, :]
    kdf = jnp.exp(gl - gc)
    kb = k * beta
    bmat = _mx_nt(kb, k)
    qmat = _mx_nt(q, k)
    return gam, eg, jnp.exp(gl), kdf, kb, bmat, qmat


DN_STEP = 4


def gdn_fwd(q, k, v, bg, name):
    t = q.shape[0]
    n_chunks = t // DN_CHUNK

    def body(q_ref, k_ref, v_ref, bg_ref, o_ref, sall_ref, tall_ref, s_ref):
        n = pl.program_id(0)

        @pl.when(n == 0)
        def _():
            s_ref[...] = jnp.zeros_like(s_ref)

        lower, strict = _chunk_masks()
        ltri = jnp.where(lower, 1.0, 0.0)
        hs = range(DN_HEADS)
        sl = [slice(h * DN_DIM, (h + 1) * DN_DIM) for h in hs]
        units = [(c, h) for c in range(DN_STEP) for h in hs]
        nu = range(len(units))
        rs = [slice(c * DN_CHUNK, (c + 1) * DN_CHUNK) for c in range(DN_STEP)]
        bgv = [bg_ref[rs[c], :] for c in range(DN_STEP)]
        gcs = [_hi(ltri, b) for b in bgv]
        gcs_t = [g.T for g in gcs]
        qh = [q_ref[rs[c], sl[h]] for c, h in units]
        kh = [k_ref[rs[c], sl[h]] for c, h in units]
        vh = [v_ref[rs[c], sl[h]] for c, h in units]
        beta = [bgv[c][:, h:h + 1] for c, h in units]
        com = [_chunk_common(qh[u], kh[u], vh[u], beta[u], gcs[c][:, DN_HEADS + h:DN_HEADS + h + 1],
                             gcs_t[c][DN_HEADS + h:DN_HEADS + h + 1, :], lower, strict) for u, (c, h) in enumerate(units)]
        gam, eg, dec, kdf, kb, bmat, qmat = zip(*com)
        tms = _inv_unit_lower_many([jnp.where(strict, bmat[u] * gam[u], 0.0) for u in nu])
        for u, (c, h) in enumerate(units):
            tall_ref[c, h] = tms[u]
        uw = [_hi(tms[u], jnp.concatenate([vh[u] * beta[u], kb[u] * eg[u]], axis=-1)) for u in nu]
        qd = [qh[u] * eg[u] for u in nu]
        pm = [qmat[u] * gam[u] for u in nu]
        kd = [kh[u] * kdf[u] for u in nu]
        st = [s_ref[h] for h in hs]
        for c in range(DN_STEP):
            us = [c * DN_HEADS + h for h in hs]
            for h in hs:
                sall_ref[c, h] = st[h]
            v_new = [uw[us[h]][:, :DN_DIM] - _mx(uw[us[h]][:, DN_DIM:], st[h]) for h in hs]
            o_st = [_mx(qd[us[h]], st[h]) for h in hs]
            o_in = [_mx(pm[us[h]], v_new[h]) for h in hs]
            s_up = [_mx_tn(kd[us[h]], v_new[h]) for h in hs]
            for h in hs:
                o_ref[rs[c], sl[h]] = o_st[h] + o_in[h]
            st = [st[h] * dec[us[h]] + s_up[h] for h in hs]
        for h in hs:
            s_ref[h] = st[h]

    rows = DN_STEP * DN_CHUNK
    row = pl.BlockSpec((rows, DN_W), lambda n: (n, 0))
    return pl.pallas_call(
        body, name=name, grid=(n_chunks // DN_STEP,),
        in_specs=[row, row, row, pl.BlockSpec((rows, 128), lambda n: (n, 0))],
        out_specs=(row, pl.BlockSpec((DN_STEP, DN_HEADS, DN_DIM, DN_DIM), lambda n: (n, 0, 0, 0)),
                   pl.BlockSpec((DN_STEP, DN_HEADS, DN_CHUNK, DN_CHUNK), lambda n: (n, 0, 0, 0))),
        out_shape=(jax.ShapeDtypeStruct((t, DN_W), F32),
                   jax.ShapeDtypeStruct((n_chunks, DN_HEADS, DN_DIM, DN_DIM), F32),
                   jax.ShapeDtypeStruct((n_chunks, DN_HEADS, DN_CHUNK, DN_CHUNK), F32)),
        scratch_shapes=[pltpu.VMEM((DN_HEADS, DN_DIM, DN_DIM), F32)],
        compiler_params=_cp("arbitrary"))(q, k, v, bg)


def gdn_bwd(q, k, v, bg, sall, tall, do, name):
    t = q.shape[0]
    n_chunks = t // DN_CHUNK

    def body(q_ref, k_ref, v_ref, bg_ref, sall_ref, tall_ref, do_ref, dq_ref, dk_ref, dv_ref, dbg_ref, ds_ref):
        n = pl.program_id(0)

        @pl.when(n == 0)
        def _():
            ds_ref[...] = jnp.zeros_like(ds_ref)

        lower, strict = _chunk_masks()
        ltri = jnp.where(lower, 1.0, 0.0)
        bgv = bg_ref[...]
        gcs = _hi(ltri, bgv)
        gcs_t = gcs.T
        lane = lax.broadcasted_iota(jnp.int32, (DN_CHUNK, 128), 1)
        rowi = lax.broadcasted_iota(jnp.int32, (DN_CHUNK, 1), 0)
        hs = range(DN_HEADS)
        each = lambda fn, *ls: [fn(*a) for a in zip(*ls)]
        rsum = lambda a: jnp.sum(a, axis=-1, keepdims=True)
        sl = [slice(h * DN_DIM, (h + 1) * DN_DIM) for h in hs]
        st = [sall_ref[0, h] for h in hs]
        tms = [tall_ref[0, h] for h in hs]
        dsn = [ds_ref[h] for h in hs]
        qh = [q_ref[:, sl[h]] for h in hs]
        kh = [k_ref[:, sl[h]] for h in hs]
        vh = [v_ref[:, sl[h]] for h in hs]
        doh = [do_ref[:, sl[h]] for h in hs]
        beta = [bgv[:, h:h + 1] for h in hs]
        com = [_chunk_common(qh[h], kh[h], vh[h], beta[h], gcs[:, DN_HEADS + h:DN_HEADS + h + 1],
                             gcs_t[DN_HEADS + h:DN_HEADS + h + 1, :], lower, strict) for h in hs]
        gam, eg, dec, kdf, kb, bmat, qmat = zip(*com)
        rhs_w = each(lambda a, b: a * b, kb, eg)
        uw = each(lambda t_, v_, b_, r_: _hi(t_, jnp.concatenate([v_ * b_, r_], axis=-1)), tms, vh, beta, rhs_w)
        qd = each(lambda a, b: a * b, qh, eg)
        kd = each(lambda a, b: a * b, kh, kdf)
        pmat = each(lambda a, b: a * b, qmat, gam)
        v_new = each(lambda uw_, s_: uw_[:, :DN_DIM] - _mx(uw_[:, DN_DIM:], s_), uw, st)
        dqd = each(_mx_nt, doh, st)
        ds_o = each(_mx_tn, qd, doh)
        dp = each(lambda d_, v_: jnp.where(lower, _mx_nt(d_, v_), 0.0), doh, v_new)
        dvn_o = each(_mx_tn, pmat, doh)
        ddec = each(lambda d_, s_: jnp.sum(rsum(d_ * s_), axis=0, keepdims=True), dsn, st)
        dkd = each(_mx_nt, v_new, dsn)
        dvn = each(lambda a, k_, d_: a + _mx(k_, d_), dvn_o, kd, dsn)
        dw = each(lambda d_, s_: -_mx_nt(d_, s_), dvn, st)
        ds_w = each(lambda uw_, d_: _mx_tn(uw_[:, DN_DIM:], d_), uw, dvn)
        for h in hs:
            ds_ref[h] = ds_o[h] + dec[h] * dsn[h] - ds_w[h]
        dr = each(lambda t_, a, b: _hi_tn(t_, jnp.concatenate([a, b], axis=-1)), tms, dvn, dw)
        da = each(lambda r_, uw_: jnp.where(strict, -_hi_nt(r_, uw_), 0.0), dr, uw)
        dru = [r_[:, :DN_DIM] for r_ in dr]
        drw = [r_[:, DN_DIM:] for r_ in dr]
        db = each(lambda a, b: a * b, da, gam)
        dq_m = each(lambda a, b: a * b, dp, gam)
        e = each(lambda a, bm, p_, qm, g_: (a * bm + p_ * qm) * g_, da, bmat, dp, qmat, gam)
        dkb = each(lambda b_, k_, r_, e_: _mx(b_, k_) + r_ * e_, db, kh, drw, eg)
        dk = each(lambda b_, kb_, m_, q_, d_, f_: _mx_tn(b_, kb_) + _mx_tn(m_, q_) + d_ * f_, db, kb, dq_m, qh, dkd, kdf)
        dq = each(lambda m_, k_, d_, e_: _mx(m_, k_) + d_ * e_, dq_m, kh, dqd, eg)
        tk = each(lambda a, b: rsum(a * b), dkd, kd)
        dbeta_all = jnp.zeros((DN_CHUNK, 128), F32)
        dgc_all = jnp.zeros((DN_CHUNK, 128), F32)
        for h in hs:
            dgc = (jnp.sum(e[h], axis=1, keepdims=True) - jnp.sum(e[h].T, axis=1, keepdims=True)
                   + rsum(dqd[h] * qd[h]) - tk[h] + rsum(drw[h] * rhs_w[h]))
            dgl = jnp.sum(tk[h], axis=0, keepdims=True) + ddec[h] * dec[h]
            dgc = dgc + jnp.where(rowi == DN_CHUNK - 1, dgl, 0.0)
            dbeta = rsum(dru[h] * vh[h]) + rsum(dkb[h] * kh[h])
            dq_ref[:, sl[h]] = dq[h]
            dk_ref[:, sl[h]] = dk[h] + dkb[h] * beta[h]
            dv_ref[:, sl[h]] = dru[h] * beta[h]
            dbeta_all = jnp.where(lane == h, dbeta, dbeta_all)
            dgc_all = jnp.where(lane == DN_HEADS + h, dgc, dgc_all)
        dbg_ref[...] = dbeta_all + _hi_tn(ltri, dgc_all)

    rev = lambda n: (n_chunks - 1 - n, 0)
    row = pl.BlockSpec((DN_CHUNK, DN_W), rev)
    small = pl.BlockSpec((DN_CHUNK, 128), rev)
    return pl.pallas_call(
        body, name=name, grid=(n_chunks,),
        in_specs=[row, row, row, small,
                  pl.BlockSpec((1, DN_HEADS, DN_DIM, DN_DIM), lambda n: (n_chunks - 1 - n, 0, 0, 0)),
                  pl.BlockSpec((1, DN_HEADS, DN_CHUNK, DN_CHUNK), lambda n: (n_chunks - 1 - n, 0, 0, 0)), row],
        out_specs=(row, row, row, small),
        out_shape=(jax.ShapeDtypeStruct((t, DN_W), F32),) * 3 + (jax.ShapeDtypeStruct((t, 128), F32),),
        scratch_shapes=[pltpu.VMEM((DN_HEADS, DN_DIM, DN_DIM), F32)],
        compiler_params=_cp("arbitrary"))(q, k, v, bg, sall, tall, do)


def gdn_out_fwd(o, proj, o_gain, name, tm=256):
    t = o.shape[0]

    def body(o_ref, z_ref, g_ref, y_ref, yt_ref):
        for h in range(DN_HEADS):
            sl = slice(h * DN_DIM, (h + 1) * DN_DIM)
            ov, zv = o_ref[:, sl], z_ref[:, sl]
            r = lax.rsqrt(jnp.mean(ov * ov, axis=-1, keepdims=True) + EPS)
            y = ov * r * g_ref[...] * (zv * _sigmoid(zv))
            y_ref[:, sl] = y.astype(y_ref.dtype)
            yt_ref[sl, :] = y.T.astype(yt_ref.dtype)

    row = pl.BlockSpec((tm, DN_W), lambda i: (i, 0))
    return pl.pallas_call(
        body, name=name, grid=(t // tm,),
        in_specs=[row, pl.BlockSpec((tm, DN_W), lambda i: (i, _Z_COL)), pl.BlockSpec((1, DN_DIM), lambda i: (0, 0))],
        out_specs=(row, pl.BlockSpec((DN_W, tm), lambda i: (0, i))),
        out_shape=(jax.ShapeDtypeStruct((t, DN_W), MXU_DTYPE), jax.ShapeDtypeStruct((DN_W, t), MXU_DTYPE)),
        compiler_params=_cp("parallel"))(o, proj, o_gain)


def gdn_out_bwd(o, proj, o_gain, dy, name, tm=256):
    t = o.shape[0]

    def body(o_ref, z_ref, g_ref, dy_ref, do_ref, dz_ref, dg_ref):
        i = pl.program_id(0)
        dg = jnp.zeros((1, DN_DIM), F32)
        for h in range(DN_HEADS):
            sl = slice(h * DN_DIM, (h + 1) * DN_DIM)
            ov, zv, dyv = o_ref[:, sl], z_ref[:, sl], dy_ref[:, sl]
            r = lax.rsqrt(jnp.mean(ov * ov, axis=-1, keepdims=True) + EPS)
            oh = ov * r
            sg = _sigmoid(zv)
            dz_ref[:, sl] = (dyv * oh * g_ref[...] * sg * (1.0 + zv * (1.0 - sg))).astype(dz_ref.dtype)
            don = dyv * (zv * sg)
            dg = dg + jnp.sum(don * oh, axis=0, keepdims=True)
            doh = don * g_ref[...]
            do_ref[:, sl] = r * (doh - oh * jnp.mean(doh * oh, axis=-1, keepdims=True))

        @pl.when(i == 0)
        def _():
            dg_ref[...] = dg

        @pl.when(i > 0)
        def _():
            dg_ref[...] += dg

    row = pl.BlockSpec((tm, DN_W), lambda i: (i, 0))
    one = pl.BlockSpec((1, DN_DIM), lambda i: (0, 0))
    return pl.pallas_call(
        body, name=name, grid=(t // tm,),
        in_specs=[row, pl.BlockSpec((tm, DN_W), lambda i: (i, _Z_COL)), one, row],
        out_specs=(row, row, one),
        out_shape=(jax.ShapeDtypeStruct((t, DN_W), F32), jax.ShapeDtypeStruct((t, DN_W), MXU_DTYPE),
                   jax.ShapeDtypeStruct((1, DN_DIM), F32)),
        compiler_params=_cp("arbitrary"))(o, proj, o_gain, dy)


def _peer(k):
    x, y, c = lax.axis_index("x"), lax.axis_index("y"), lax.axis_index("c")
    px = 1 - x if k & 4 else x
    py = 1 - y if k & 2 else y
    pc = 1 - c if k & 1 else c
    return (px, py, pc), 4 * px + 2 * py + pc


_HBM = pl.BlockSpec(memory_space=pltpu.HBM)
_SEM = pl.BlockSpec(memory_space=pltpu.SEMAPHORE)
_DATAFLOW = pltpu.SideEffectType.DATAFLOW_SIDE_EFFECTING
N_PEER = N_DEV - 1


def send_start(srcs, name, scatter, after):
    na = len(srcs)
    ns = (2 * N_PEER + 1) * na
    lands = [lax.empty((N_DEV,) + (s.shape[1:] if scatter else s.shape), s.dtype) for s in srcs]
    extra = [] if after is None else [after]

    def body(*refs):
        src_refs, land_refs = refs[:na], refs[na:2 * na]
        sems = refs[2 * na + len(extra):2 * na + len(extra) + ns]
        land_out, token = refs[-1 - na:-1], refs[-1]
        _, me = _peer(0)
        for a in range(na):
            pltpu.make_async_copy(src_refs[a].at[me] if scatter else src_refs[a], land_out[a].at[me],
                                  sems[2 * N_PEER * na + a]).start()
        for k in range(1, N_DEV):
            peer, pid = _peer(k)
            for a in range(na):
                pltpu.make_async_remote_copy(
                    src_ref=src_refs[a].at[pid] if scatter else src_refs[a], dst_ref=land_refs[a].at[me],
                    send_sem=sems[2 * (a * N_PEER + k - 1)], recv_sem=sems[2 * (a * N_PEER + k - 1) + 1],
                    device_id=peer, device_id_type=MESH).start()
        token[...] = jnp.zeros_like(token)

    hbm = lambda arrs: tuple(pltpu.HBM(a.shape, a.dtype) for a in arrs)
    outs = pl.pallas_call(
        body, name=name,
        out_shape=(pltpu.SemaphoreType.DMA(()),) * ns + hbm(srcs) + hbm(lands) + (jax.ShapeDtypeStruct((8, 128), F32),),
        in_specs=[_HBM] * (2 * na) + [pl.BlockSpec(memory_space=pl.ANY)] * len(extra),
        out_specs=(_SEM,) * ns + (_HBM,) * (2 * na) + (pl.BlockSpec(memory_space=pltpu.VMEM),),
        input_output_aliases={i: ns + i for i in range(2 * na)},
        compiler_params=pltpu.CompilerParams(has_side_effects=_DATAFLOW),
    )(*[pltpu.with_memory_space_constraint(a, pltpu.HBM) for a in list(srcs) + lands], *extra)
    return outs[:ns], outs[ns:ns + na], outs[ns + na:ns + 2 * na], outs[-1]


def send_wait(sems, srcs_thru, lands_thru, name, scatter, after):
    na = len(srcs_thru)
    ns = (2 * N_PEER + 1) * na

    def body(*refs):
        src_refs, land_refs, sm = refs[:na], refs[na:2 * na], refs[2 * na:2 * na + ns]
        _, me = _peer(0)
        for a in range(na):
            pltpu.make_async_copy(src_refs[a].at[me] if scatter else src_refs[a], land_refs[a].at[me],
                                  sm[2 * N_PEER * na + a]).wait()
        for k in range(1, N_DEV):
            peer, pid = _peer(k)
            for a in range(na):
                cp = pltpu.make_async_remote_copy(
                    src_ref=src_refs[a].at[pid] if scatter else src_refs[a], dst_ref=land_refs[a].at[pid],
                    send_sem=sm[2 * (a * N_PEER + k - 1)], recv_sem=sm[2 * (a * N_PEER + k - 1) + 1],
                    device_id=peer, device_id_type=MESH)
                cp.wait_send()
                cp.wait_recv()

    hbm = lambda arrs: tuple(pltpu.HBM(a.shape, a.dtype) for a in arrs)
    outs = pl.pallas_call(
        body, name=name, out_shape=hbm(srcs_thru) + hbm(lands_thru),
        in_specs=[_HBM] * (2 * na) + [_SEM] * ns + [pl.BlockSpec(memory_space=pl.ANY)], out_specs=(_HBM,) * (2 * na),
        input_output_aliases={i: i for i in range(2 * na)},
        compiler_params=pltpu.CompilerParams(has_side_effects=_DATAFLOW),
    )(*srcs_thru, *lands_thru, *sems, after)
    return outs[na:]


def _adamw(w, g, m, v):
    m = ADAM_B1 * m + (1.0 - ADAM_B1) * g
    v = ADAM_B2 * v + (1.0 - ADAM_B2) * (g * g)
    m_hat = m / (1.0 - ADAM_B1 ** ADAM_STEP)
    v_hat = v / (1.0 - ADAM_B2 ** ADAM_STEP)
    return -ADAM_LR * (m_hat / (jnp.sqrt(v_hat) + ADAM_EPS) + ADAM_WD * w), m, v


def adam_sum(w, pieces, m, v, name, layer=0, into=None):
    nl, r, c = w.shape
    tr = r
    for cand in (256, 128, 64, 32, 16, 8):
        if r % cand == 0:
            tr = cand
            break

    def body(w_ref, p_ref, m_ref, v_ref, *rest):
        g_ref, d_ref, nm_ref, nv_ref = rest[-4:]
        g = p_ref[0].astype(F32)
        for s in range(1, N_DEV):
            g = g + p_ref[s].astype(F32)
        g_ref[0] = g
        d_ref[0], nm_ref[0], nv_ref[0] = _adamw(w_ref[0], g, m_ref[0], v_ref[0])

    row = pl.BlockSpec((1, tr, c), lambda i: (layer, i, 0))
    out = jax.ShapeDtypeStruct((nl, r, c), F32)
    extra = [] if into is None else list(into)
    return pl.pallas_call(
        body, name=name, grid=(r // tr,),
        in_specs=[row, pl.BlockSpec((N_DEV, tr, c), lambda i: (0, i, 0)), row, row]
        + [pl.BlockSpec(memory_space=pl.ANY)] * len(extra),
        out_specs=(row,) * 4, out_shape=(out,) * 4,
        input_output_aliases={4 + i: i for i in range(len(extra))},
        compiler_params=_cp("parallel"))(w, pieces, m, v, *extra)


def sum_rows(gathered, name):
    _, r, c = gathered.shape

    def body(p_ref, o_ref):
        g = p_ref[0]
        for s in range(1, N_DEV):
            g = g + p_ref[s]
        o_ref[...] = g

    return pl.pallas_call(body, name=name, out_shape=jax.ShapeDtypeStruct((r, c), F32))(gathered)


def adam_small(w, g, m, v, name):
    def body(w_ref, g_ref, m_ref, v_ref, d_ref, nm_ref, nv_ref):
        d_ref[...], nm_ref[...], nv_ref[...] = _adamw(w_ref[...], g_ref[...], m_ref[...], v_ref[...])

    out = jax.ShapeDtypeStruct(w.shape, F32)
    return pl.pallas_call(body, name=name, out_shape=(out,) * 3)(w, g, m, v)


def _rope_tables(t):
    inv_freq = 10000.0 ** (-jnp.arange(0, HEAD_DIM, 2, dtype=F32) / HEAD_DIM)
    ang = jnp.arange(t, dtype=F32)[:, None] * inv_freq[None, :]
    cos, sin = jnp.cos(ang), jnp.sin(ang)
    return jnp.concatenate([cos, cos], axis=-1), jnp.concatenate([sin, sin], axis=-1)


def _lane_row(vec8):
    return jnp.pad(vec8.reshape(1, DN_HEADS), ((0, 0), (DN_HEADS, 128 - 2 * DN_HEADS)))


def _ffn_bwd(x, norm_g, w_gu, w_d, saved, dy, tag, after=None):
    ft, gu, at = saved
    dgu = ffn_dact(dy, w_d, gu, f"{tag}_d_gate_up", after=after)
    dwd = mm_at(at, dy, f"{tag}_dw_down")
    nj = D_FF // GU_TILE
    dwgu = mm_at(ft, dgu, f"{tag}_dw_gate_up", transposed=True, tn=GU_TILE, row_block=lambda q: (q % 2) * nj + q // 2)
    dx, dg = mm_rms_bwd(dgu, w_gu, x, norm_g, dy, f"{tag}_d_norm", chunks=_GU_CHUNKS)
    return dx, dwgu, dwd, dg


def local_step(x, target, small, weights_of, grads_out, after=None):
    t = x.shape[0]
    cosf, sinf = _rope_tables(t)
    alog_row, dtb_row = _lane_row(small["odd_a_log"]), _lane_row(small["odd_dt_bias"])

    h0, h0t = rms_fwd(x, small["even_norm"], "even_norm", after=after)
    we = weights_of("even", h0)
    small = {**small, **we.get("small", {})}
    proj0 = mm_nt(h0, we["w_in"], "even_in_proj")
    qr, kr = qk_prep_fwd(proj0, small["even_q_gain"], small["even_k_gain"], cosf, sinf, "even_qk_prep")
    y_attn, mix0, mix0_t, lse = swa_fwd(qr, kr, proj0, small["even_sinks"], "even_swa")
    mix0, mix0_t = gconv_fwd(proj0, small["even_conv_w"], mix0, mix0_t, "even_gconv")
    we = {**we, **weights_of("even_out", mix0)}
    x1, f0, f0t = mm_nn_res_norm(mix0, we["w_out"], x, small["ffn_norm0"], "even_out_proj")
    w0 = weights_of("ffn0", x1)
    gu0, a0, a0t = ffn_up(f0, w0["gate_up"], "ffn0_gate_up")
    ffn0 = (f0t, gu0, a0t)
    x2, h1, h1t = mm_nn_res_norm(a0, w0["down"], x1, small["odd_norm"], "ffn0_down")

    wo = weights_of("odd", x2)
    proj1 = mm_nt(h1, wo["w_in"], "odd_in_proj")
    qn, kn, vs, bg, conv1 = gdn_prep_fwd(proj1, small["odd_conv_w"], alog_row, dtb_row, "odd_prep")
    o, sall, tall = gdn_fwd(qn, kn, vs, bg, "odd_delta_rule")
    og, ogt = gdn_out_fwd(o, proj1, small["odd_o_gain"], "odd_gate_norm")
    x3, f1, f1t = mm_nn_res_norm(og, wo["w_out"], x2, small["ffn_norm1"], "odd_out_proj")
    w1 = weights_of("ffn1", x3)
    gu1, a1, a1t = ffn_up(f1, w1["gate_up"], "ffn1_gate_up")
    ffn1 = (f1t, gu1, a1t)
    dy, loss_row = mm_nn_res_loss(a1, w1["down"], x3, target, "ffn1_down_loss")

    gs = {}
    dx3, dwgu, dwd, gs["ffn_norm1"] = _ffn_bwd(x3, small["ffn_norm1"], w1["gate_up"], w1["down"], ffn1, dy, "ffn1")
    tok = grads_out("ffn1", {"gate_up": dwgu, "down": dwd})

    dog = mm_nt(dx3, wo["w_out"], "odd_d_gated", after=tok)
    dwo = mm_at(ogt, dx3, "odd_dw_out")
    do, dz, gs["odd_o_gain"] = gdn_out_bwd(o, proj1, small["odd_o_gain"], dog, "odd_d_gate_norm")
    dqn, dkn, dvs, dbg = gdn_bwd(qn, kn, vs, bg, sall, tall, do, "odd_d_delta_rule")
    dproj1, gs["odd_conv_w"], ddt_row, dal_row = gdn_prep_bwd(
        proj1, conv1, small["odd_conv_w"], alog_row, dtb_row, dqn, dkn, dvs, dbg, dz, "odd_d_prep")
    gs["odd_dt_bias"] = ddt_row[:, DN_HEADS:2 * DN_HEADS]
    gs["odd_a_log"] = dal_row[:, DN_HEADS:2 * DN_HEADS]
    dwi = mm_at(h1t, dproj1, "odd_dw_in", transposed=True)
    dx2, gs["odd_norm"] = mm_rms_bwd(dproj1, wo["w_in"], x2, small["odd_norm"], dx3, "odd_d_norm")
    tok = grads_out("odd", {"w_in": dwi, "w_out": dwo})

    dx1, dwgu, dwd, gs["ffn_norm0"] = _ffn_bwd(x1, small["ffn_norm0"], w0["gate_up"], w0["down"], ffn0, dx2, "ffn0",
                                               after=tok)
    tok = grads_out("ffn0", {"gate_up": dwgu, "down": dwd})

    dmix = mm_nt(dx1, we["w_out"], "even_d_mix", after=tok)
    dwo = mm_at(mix0_t, dx1, "even_dw_out")
    dqr, dkr, dv, gs["even_sinks"] = swa_bwd(qr, kr, proj0, small["even_sinks"], y_attn, lse, dmix, "even_d_swa")
    dqk, gs["even_q_gain"], gs["even_k_gain"] = qk_prep_bwd(
        proj0, small["even_q_gain"], small["even_k_gain"], cosf, sinf, dqr, dkr, "even_d_qk_prep")
    dgb, dgc, dxi, gs["even_conv_w"] = gconv_bwd(proj0, small["even_conv_w"], dmix, "even_d_gconv")
    dproj0 = jnp.concatenate([dqk, dv, dgb, dgc, dxi], axis=-1)
    dwi = mm_at(h0t, dproj0, "even_dw_in", transposed=True)
    tok = grads_out("even", {"w_in": dwi, "w_out": dwo})
    grad_x, gs["even_norm"] = mm_rms_bwd(dproj0, we["w_in"], x, small["even_norm"], dx1, "even_d_norm", after=tok)
    return loss_row, grad_x, gs


_SMALL_ORDER = ("even_norm", "even_q_gain", "even_k_gain", "even_sinks", "odd_a_log", "odd_dt_bias", "odd_o_gain",
                "ffn_norm0", "ffn_norm1", "odd_norm", "even_conv_w", "odd_conv_w")
_SMALL_SIZE = {"even_norm": 1024, "even_q_gain": 64, "even_k_gain": 64, "even_sinks": 8, "odd_a_log": 8,
               "odd_dt_bias": 8, "odd_o_gain": 128, "ffn_norm0": 1024, "ffn_norm1": 1024, "odd_norm": 1024,
               "even_conv_w": 3 * 512, "odd_conv_w": 4 * 3072}
_N_REPL = 9


def _pack_rows(vals):
    flat = jnp.concatenate([v.reshape(-1) for v in vals])
    pad = (-flat.shape[0]) % 1024
    return jnp.pad(flat, (0, pad)).reshape(-1, 128)


def _my_block(full, size, axis):
    me = 4 * lax.axis_index("x") + 2 * lax.axis_index("y") + lax.axis_index("c")
    return lax.dynamic_slice_in_dim(full, me * size, size, axis=axis)


def kernel(x, even_norm, even_w_in, even_q_gain, even_k_gain, even_sinks, even_conv_w, even_w_out, odd_norm, odd_w_in, odd_conv_w, odd_a_log, odd_dt_bias, odd_o_gain, odd_w_out, ffn_norm, ffn_w_gate_up, ffn_w_down, loss_target, m_even_norm, m_even_w_in, m_even_q_gain, m_even_k_gain, m_even_sinks, m_even_conv_w, m_even_w_out, m_odd_norm, m_odd_w_in, m_odd_conv_w, m_odd_a_log, m_odd_dt_bias, m_odd_o_gain, m_odd_w_out, m_ffn_norm, m_ffn_w_gate_up, m_ffn_w_down, v_even_norm, v_even_w_in, v_even_q_gain, v_even_k_gain, v_even_sinks, v_even_conv_w, v_even_w_out, v_odd_norm, v_odd_w_in, v_odd_conv_w, v_odd_a_log, v_odd_dt_bias, v_odd_o_gain, v_odd_w_out, v_ffn_norm, v_ffn_w_gate_up, v_ffn_w_down):
    t = x.shape[1]
    d = D_MODEL

    me = 4 * lax.axis_index("x") + 2 * lax.axis_index("y") + lax.axis_index("c")
    tr = lambda a: jnp.swapaxes(a, 1, 2)
    shard = {
        "even": {"w_in": tr(even_w_in)[0], "w_out": even_w_out[0]},
        "ffn0": {"gate_up": tr(ffn_w_gate_up)[0], "down": ffn_w_down[0]},
        "odd": {"w_in": tr(odd_w_in)[0], "w_out": odd_w_out[0]},
        "ffn1": {"gate_up": tr(ffn_w_gate_up)[1], "down": ffn_w_down[1]},
    }
    given = {
        ("even", "w_in"): ("even_w_in", even_w_in, m_even_w_in, v_even_w_in, 0),
        ("even", "w_out"): ("even_w_out", even_w_out, m_even_w_out, v_even_w_out, 0),
        ("odd", "w_in"): ("odd_w_in", odd_w_in, m_odd_w_in, v_odd_w_in, 0),
        ("odd", "w_out"): ("odd_w_out", odd_w_out, m_odd_w_out, v_odd_w_out, 0),
        ("ffn0", "gate_up"): ("ffn_w_gate_up", ffn_w_gate_up, m_ffn_w_gate_up, v_ffn_w_gate_up, 0),
        ("ffn1", "gate_up"): ("ffn_w_gate_up", ffn_w_gate_up, m_ffn_w_gate_up, v_ffn_w_gate_up, 1),
        ("ffn0", "down"): ("ffn_w_down", ffn_w_down, m_ffn_w_down, v_ffn_w_down, 0),
        ("ffn1", "down"): ("ffn_w_down", ffn_w_down, m_ffn_w_down, v_ffn_w_down, 1),
    }

    def whole(group, parts):
        col, row = tuple(shard[group])
        w_col = parts[0].reshape(-1, d)
        if group == "odd":
            w_col = jnp.pad(w_col, ((0, ODD_IN_PAD - ODD_IN_W), (0, 0)))
        return {col: w_col, row: parts[1].reshape(-1, d)}

    wire = {g: [a.astype(MXU_DTYPE) for a in shard[g].values()] for g in shard}
    wire["even_out"] = [wire["even"].pop()]
    wire["even"].append(_pack_rows([odd_norm, even_conv_w, odd_conv_w]))
    gathers, tok = {}, None
    for g in ("even", "even_out", "ffn0", "odd", "ffn1"):
        sems, srcs_thru, lands_thru, tok = send_start(wire[g], f"gather_{g}_start", False, tok)
        gathers[g] = (sems, srcs_thru, lands_thru)
    o1 = d // N_DEV
    o2 = o1 + 3 * CONV_CH // N_DEV

    def weights_of(group, after):
        lands = send_wait(*gathers[group], f"gather_{group}_wait", False, after)
        if group == "even_out":
            return {"w_out": lands[0].reshape(-1, d)}
        if group != "even":
            return whole(group, lands)
        sg = lands[1].reshape(N_DEV, -1)
        return {"w_in": lands[0].reshape(-1, d), "small": {
            "odd_norm": sg[:, :o1].reshape(1, d),
            "even_conv_w": sg[:, o1:o2].reshape(N_DEV, 3, CONV_CH // N_DEV).transpose(1, 0, 2).reshape(3, CONV_CH),
            "odd_conv_w": sg[:, o2:o2 + 4 * _QKV_W // N_DEV].reshape(N_DEV, 4, _QKV_W // N_DEV)
            .transpose(1, 0, 2).reshape(4, _QKV_W)}}

    sent = {}

    def grads_out(group, dws):
        col, row = tuple(shard[group])
        n_cols = N_DEV * shard[group][col].shape[0]
        pieces = [dws[col][:n_cols].reshape((N_DEV,) + shard[group][col].shape),
                  dws[row].reshape((N_DEV,) + shard[group][row].shape)]
        sems, srcs_thru, lands_thru, token = send_start(pieces, f"exchange_{group}_start", True, None)
        sent[group] = (sems, srcs_thru, lands_thru, pieces)
        return token

    small = {
        "even_norm": even_norm, "even_q_gain": even_q_gain, "even_k_gain": even_k_gain, "even_sinks": even_sinks,
        "odd_a_log": odd_a_log.reshape(-1), "odd_dt_bias": odd_dt_bias.reshape(-1), "odd_o_gain": odd_o_gain,
        "ffn_norm0": ffn_norm[0:1], "ffn_norm1": ffn_norm[1:2],
    }

    loss_row, grad_x, gs = local_step(x.reshape(t, d), loss_target.reshape(t, d), small, weights_of, grads_out, after=tok)

    rows = _pack_rows([gs[n] for n in _SMALL_ORDER] + [loss_row[:, 0:1]])
    small_sent = send_start([rows], "gather_small_grads_start", False, None)

    res, behind = {}, small_sent[3]
    for g in ("ffn1", "odd", "ffn0", "even"):
        sems, srcs_thru, lands_thru, pieces = sent[g]
        lands = send_wait(sems, srcs_thru, lands_thru, f"exchange_{g}_wait", True, behind)
        for i, (key, pcs) in enumerate(zip(shard[g], lands)):
            name, w_, m_, v_, layer = given[g, key]
            view = tr if i == 0 else (lambda a: a)
            res[name] = adam_sum(view(w_), pcs, view(m_), view(v_), f"adamw_{g}_{key}", layer=layer, into=res.get(name))
        behind = res[name][0]
    for name in ("even_w_in", "odd_w_in", "ffn_w_gate_up"):
        res[name] = tuple(tr(a) for a in res[name])

    (rows_g,) = send_wait(*small_sent[:3], "gather_small_grads_wait", False, behind)
    tot = sum_rows(rows_g, "sum_small_grads").reshape(-1)
    off, sgrad = 0, {}
    for n in _SMALL_ORDER:
        sgrad[n] = tot[off:off + _SMALL_SIZE[n]]
        off += _SMALL_SIZE[n]
    loss = tot[off]

    repl = _SMALL_ORDER[:_N_REPL]
    repl_w = {"even_norm": even_norm, "even_q_gain": even_q_gain, "even_k_gain": even_k_gain, "even_sinks": even_sinks,
              "odd_a_log": odd_a_log, "odd_dt_bias": odd_dt_bias, "odd_o_gain": odd_o_gain,
              "ffn_norm0": ffn_norm[0], "ffn_norm1": ffn_norm[1]}
    repl_m = {"even_norm": m_even_norm, "even_q_gain": m_even_q_gain, "even_k_gain": m_even_k_gain,
              "even_sinks": m_even_sinks, "odd_a_log": m_odd_a_log, "odd_dt_bias": m_odd_dt_bias,
              "odd_o_gain": m_odd_o_gain, "ffn_norm0": m_ffn_norm[0], "ffn_norm1": m_ffn_norm[1]}
    repl_v = {"even_norm": v_even_norm, "even_q_gain": v_even_q_gain, "even_k_gain": v_even_k_gain,
              "even_sinks": v_even_sinks, "odd_a_log": v_odd_a_log, "odd_dt_bias": v_odd_dt_bias,
              "odd_o_gain": v_odd_o_gain, "ffn_norm0": v_ffn_norm[0], "ffn_norm1": v_ffn_norm[1]}
    pk = lambda dct: _pack_rows([dct[n] for n in repl])
    pd_, pm_, pv_ = adam_small(pk(repl_w), pk(sgrad), pk(repl_m), pk(repl_v), "adamw_replicated")
    sres = {}
    off = 0
    for n in repl:
        sz = _SMALL_SIZE[n]
        sres[n] = (sgrad[n], pd_.reshape(-1)[off:off + sz], pm_.reshape(-1)[off:off + sz], pv_.reshape(-1)[off:off + sz])
        off += sz
    g_on = _my_block(sgrad["odd_norm"].reshape(1, d), d // N_DEV, 1)
    g_ec = _my_block(sgrad["even_conv_w"].reshape(3, CONV_CH), CONV_CH // N_DEV, 1)
    g_oc = _my_block(sgrad["odd_conv_w"].reshape(4, _QKV_W), _QKV_W // N_DEV, 1)
    shard_w = _pack_rows([odd_norm, even_conv_w, odd_conv_w])
    sd_, sm_, sv_ = adam_small(shard_w, _pack_rows([g_on, g_ec, g_oc]),
                               _pack_rows([m_odd_norm, m_even_conv_w, m_odd_conv_w]),
                               _pack_rows([v_odd_norm, v_even_conv_w, v_odd_conv_w]), "adamw_sharded_small")
    off = 0
    for n, gfull, like in (("odd_norm", g_on, odd_norm), ("even_conv_w", g_ec, even_conv_w), ("odd_conv_w", g_oc, odd_conv_w)):
        sz = like.size
        sres[n] = (gfull, sd_.reshape(-1)[off:off + sz], sm_.reshape(-1)[off:off + sz], sv_.reshape(-1)[off:off + sz])
        off += sz

    def small_out(name, like, kind):
        if name == "ffn_norm":
            return jnp.stack([sres["ffn_norm0"][kind], sres["ffn_norm1"][kind]]).reshape(like.shape)
        return sres[name][kind].reshape(like.shape)

    order = (("even_norm", even_norm), ("even_w_in", even_w_in), ("even_q_gain", even_q_gain),
             ("even_k_gain", even_k_gain), ("even_sinks", even_sinks), ("even_conv_w", even_conv_w),
             ("even_w_out", even_w_out), ("odd_norm", odd_norm), ("odd_w_in", odd_w_in), ("odd_conv_w", odd_conv_w),
             ("odd_a_log", odd_a_log), ("odd_dt_bias", odd_dt_bias), ("odd_o_gain", odd_o_gain),
             ("odd_w_out", odd_w_out), ("ffn_norm", ffn_norm), ("ffn_w_gate_up", ffn_w_gate_up),
             ("ffn_w_down", ffn_w_down))
    outs = [loss, grad_x.reshape(x.shape)]
    for kind in range(4):
        for name, like in order:
            outs.append(res[name][kind] if name in res else small_out(name, like, kind))
    return tuple(outs)
```

```python
import jax
import jax.numpy as jnp
import numpy as np
from jax import lax
from jax.experimental import pallas as pl
from jax.experimental.pallas import tpu as pltpu

F32 = jnp.float32
MXU_DTYPE = jnp.bfloat16
HI = lax.Precision.HIGH
EPS = 1e-6
N_DEV = 8
D_MODEL = 1024
HEAD_DIM = 64
ATTN_HEADS = 8
KV_HEADS = 2
ATTN_BLOCK = 128
Q_W = 512
KV_W = 128
CONV_CH = 512
EVEN_IN_W = 2304
DN_HEADS = 8
DN_DIM = 128
DN_W = 1024
DN_CHUNK = 64
ODD_IN_W = 4112
ODD_IN_PAD = 4224
D_FF = 2816
NEG = -1e30
VMEM_LIMIT = 56 * 1024 * 1024
ADAM_LR, ADAM_B1, ADAM_B2, ADAM_EPS, ADAM_WD, ADAM_STEP = 0.001, 0.9, 0.999, 1e-08, 0.01, 10
MESH = pl.DeviceIdType.MESH


def _cp(*sem):
    return pltpu.CompilerParams(dimension_semantics=sem, vmem_limit_bytes=VMEM_LIMIT)


def _pick(n, cap):
    best = 128
    for t in range(128, cap + 1, 128):
        if n % t == 0:
            best = t
    return best


def _mx(a, b):
    return jnp.dot(a.astype(MXU_DTYPE), b.astype(MXU_DTYPE), preferred_element_type=F32)


def _mx_nt(a, b):
    return lax.dot_general(a.astype(MXU_DTYPE), b.astype(MXU_DTYPE), (((1,), (1,)), ((), ())),
                           preferred_element_type=F32)


def _mx_tn(a, b):
    return lax.dot_general(a.astype(MXU_DTYPE), b.astype(MXU_DTYPE), (((0,), (0,)), ((), ())),
                           preferred_element_type=F32)


def _hi(a, b):
    return jnp.dot(a, b, precision=HI, preferred_element_type=F32)


def _hi_nt(a, b):
    return lax.dot_general(a, b, (((1,), (1,)), ((), ())), precision=HI, preferred_element_type=F32)


def _hi_tn(a, b):
    return lax.dot_general(a, b, (((0,), (0,)), ((), ())), precision=HI, preferred_element_type=F32)


def _sigmoid(x):
    return 0.5 * jnp.tanh(0.5 * x) + 0.5


def _softplus(x):
    return jnp.maximum(x, 0.0) + jnp.log(1.0 + jnp.exp(-jnp.abs(x)))


def mm_nn_res_norm(a, b, res, g, name, tm=512):
    t, k = a.shape
    d = b.shape[1]
    tm = min(tm, t)

    def body(a_ref, b_ref, res_ref, g_ref, y_ref, h_ref, ht_ref):
        y = res_ref[...] + _mx(a_ref[...], b_ref[...])
        y_ref[...] = y
        h = y * lax.rsqrt(jnp.mean(y * y, axis=-1, keepdims=True) + EPS) * g_ref[...]
        h_ref[...] = h.astype(h_ref.dtype)
        ht_ref[...] = h.T.astype(ht_ref.dtype)

    row = pl.BlockSpec((tm, d), lambda i: (i, 0))
    return pl.pallas_call(
        body, name=name, grid=(t // tm,),
        in_specs=[pl.BlockSpec((tm, k), lambda i: (i, 0)), pl.BlockSpec((k, d), lambda i: (0, 0)), row,
                  pl.BlockSpec((1, d), lambda i: (0, 0))],
        out_specs=(row, row, pl.BlockSpec((d, tm), lambda i: (0, i))),
        out_shape=(jax.ShapeDtypeStruct((t, d), F32), jax.ShapeDtypeStruct((t, d), MXU_DTYPE),
                   jax.ShapeDtypeStruct((d, t), MXU_DTYPE)),
        compiler_params=_cp("parallel"))(a, b, res, g)


def mm_nt(a, b, name, out_dtype=F32, tm=1024, after=None):
    m, k = a.shape
    n, _ = b.shape
    tn = _pick(n, 512 if k > 3000 else 1536)
    tm = min(tm, m)

    def body(a_ref, b_ref, *rest):
        o_ref = rest[-1]
        o_ref[...] = _mx_nt(a_ref[...], b_ref[...]).astype(o_ref.dtype)

    in_specs = [pl.BlockSpec((tm, k), lambda j, i: (i, 0)), pl.BlockSpec((tn, k), lambda j, i: (j, 0))]
    args = [a, b]
    if after is not None:
        in_specs.append(pl.BlockSpec(memory_space=pl.ANY))
        args.append(after)
    return pl.pallas_call(
        body, name=name, grid=(n // tn, m // tm), in_specs=in_specs,
        out_specs=pl.BlockSpec((tm, tn), lambda j, i: (i, j)),
        out_shape=jax.ShapeDtypeStruct((m, n), out_dtype), compiler_params=_cp("parallel", "parallel"))(*args)


def mm_at(at, b, name, tk=1024, transposed=False, tn=None, row_block=None):
    m, kk = at.shape
    _, n = b.shape
    tm, tn, tk = _pick(m, 1408), tn or _pick(n, 2816), min(tk, kk)
    nk = kk // tk

    def body(a_ref, b_ref, o_ref, acc_ref):
        k = pl.program_id(2)
        p = _mx(a_ref[...], b_ref[...])
        acc = jnp.where(k == 0, p, acc_ref[...] + p)
        acc_ref[...] = acc

        @pl.when(k == nk - 1)
        def _():
            o_ref[...] = (acc.T if transposed else acc).astype(o_ref.dtype)

    if transposed:
        rb = row_block or (lambda j: j)
        out_spec = pl.BlockSpec((tn, tm), lambda i, j, k: (rb(j), i))
        out_shape = jax.ShapeDtypeStruct((n, m), MXU_DTYPE)
    else:
        out_spec = pl.BlockSpec((tm, tn), lambda i, j, k: (i, j))
        out_shape = jax.ShapeDtypeStruct((m, n), MXU_DTYPE)
    return pl.pallas_call(
        body, name=name, grid=(m // tm, n // tn, nk),
        in_specs=[pl.BlockSpec((tm, tk), lambda i, j, k: (i, k)), pl.BlockSpec((tk, tn), lambda i, j, k: (k, j))],
        out_specs=out_spec, out_shape=out_shape, scratch_shapes=[pltpu.VMEM((tm, tn), F32)],
        compiler_params=_cp("parallel", "parallel", "arbitrary"))(at, b)


def rms_fwd(x, g, name, tm=512, after=None):
    t, d = x.shape

    def body(x_ref, g_ref, *rest):
        o_ref, ot_ref = rest[-2:]
        xv = x_ref[...]
        r = lax.rsqrt(jnp.mean(xv * xv, axis=-1, keepdims=True) + EPS)
        h = xv * r * g_ref[...]
        o_ref[...] = h.astype(o_ref.dtype)
        ot_ref[...] = h.T.astype(ot_ref.dtype)

    in_specs = [pl.BlockSpec((tm, d), lambda i: (i, 0)), pl.BlockSpec((1, d), lambda i: (0, 0))]
    args = [x, g]
    if after is not None:
        in_specs.append(pl.BlockSpec(memory_space=pl.ANY))
        args.append(after)
    return pl.pallas_call(
        body, name=name, grid=(t // tm,), in_specs=in_specs,
        out_specs=(pl.BlockSpec((tm, d), lambda i: (i, 0)), pl.BlockSpec((d, tm), lambda i: (0, i))),
        out_shape=(jax.ShapeDtypeStruct((t, d), MXU_DTYPE), jax.ShapeDtypeStruct((d, t), MXU_DTYPE)),
        compiler_params=_cp("parallel"))(*args)


def mm_rms_bwd(a, bt, x, g, dres, name, tm=512, after=None, chunks=None):
    t, k = a.shape
    d = bt.shape[1]
    tm = min(tm, t)
    chunks = chunks or ((0, 0, k),)

    def body(a_ref, b_ref, x_ref, g_ref, dres_ref, *rest):
        dx_ref, dg_ref = rest[-2:]
        dhv = None
        for ca, cb, size in chunks:
            part = _mx(a_ref[:, ca:ca + size], b_ref[cb:cb + size, :])
            dhv = part if dhv is None else dhv + part
        xv = x_ref[...]
        r = lax.rsqrt(jnp.mean(xv * xv, axis=-1, keepdims=True) + EPS)
        xh = xv * r
        dxh = dhv * g_ref[...]
        dx_ref[...] = dres_ref[...] + r * (dxh - xh * jnp.mean(dxh * xh, axis=-1, keepdims=True))
        part = jnp.sum(dhv * xh, axis=0, keepdims=True)
        dg_ref[...] = jnp.where(pl.program_id(0) == 0, part, dg_ref[...] + part)

    row = pl.BlockSpec((tm, d), lambda i: (i, 0))
    one = pl.BlockSpec((1, d), lambda i: (0, 0))
    in_specs = [pl.BlockSpec((tm, k), lambda i: (i, 0)), pl.BlockSpec((k, d), lambda i: (0, 0)), row, one, row]
    args = [a, bt, x, g, dres]
    if after is not None:
        in_specs.append(pl.BlockSpec(memory_space=pl.ANY))
        args.append(after)
    return pl.pallas_call(
        body, name=name, grid=(t // tm,), in_specs=in_specs, out_specs=(row, one),
        out_shape=(jax.ShapeDtypeStruct((t, d), F32), jax.ShapeDtypeStruct((1, d), F32)),
        compiler_params=_cp("arbitrary"))(*args)


GU_TILE = 1408


def ffn_up(f, wt, name, tm=1024):
    t, d = f.shape
    tm = min(tm, t)
    nj = D_FF // GU_TILE

    def body(f_ref, wg_ref, wu_ref, gu_ref, a_ref, at_ref):
        g = _mx_nt(f_ref[...], wg_ref[...])
        u = _mx_nt(f_ref[...], wu_ref[...])
        sg = _sigmoid(g)
        gs = g * sg
        gu_ref[:, :GU_TILE] = (u * (sg + gs - gs * sg)).astype(gu_ref.dtype)
        gu_ref[:, GU_TILE:] = gs.astype(gu_ref.dtype)
        act = gs * u
        a_ref[...] = act.astype(a_ref.dtype)
        at_ref[...] = act.T.astype(at_ref.dtype)

    return pl.pallas_call(
        body, name=name, grid=(nj, t // tm),
        in_specs=[pl.BlockSpec((tm, d), lambda j, i: (i, 0)), pl.BlockSpec((GU_TILE, d), lambda j, i: (j, 0)),
                  pl.BlockSpec((GU_TILE, d), lambda j, i: (nj + j, 0))],
        out_specs=(pl.BlockSpec((tm, 2 * GU_TILE), lambda j, i: (i, j)), pl.BlockSpec((tm, GU_TILE), lambda j, i: (i, j)),
                   pl.BlockSpec((GU_TILE, tm), lambda j, i: (j, i))),
        out_shape=(jax.ShapeDtypeStruct((t, 2 * D_FF), MXU_DTYPE), jax.ShapeDtypeStruct((t, D_FF), MXU_DTYPE),
                   jax.ShapeDtypeStruct((D_FF, t), MXU_DTYPE)),
        compiler_params=_cp("parallel", "parallel"))(f, wt, wt)


_GU_CHUNKS = tuple((q * GU_TILE, ((q % 2) * (D_FF // GU_TILE) + q // 2) * GU_TILE, GU_TILE)
                   for q in range(2 * D_FF // GU_TILE))


def ffn_dact(dy, w_d, gu, name, tm=1024, after=None):
    t, d = dy.shape
    tm = min(tm, t)

    def body(dy_ref, w_ref, gu_ref, *rest):
        o_ref = rest[-1]
        da = _mx_nt(dy_ref[...], w_ref[...])
        o_ref[:, :GU_TILE] = (da * gu_ref[:, :GU_TILE]).astype(o_ref.dtype)
        o_ref[:, GU_TILE:] = (da * gu_ref[:, GU_TILE:]).astype(o_ref.dtype)

    in_specs = [pl.BlockSpec((tm, d), lambda j, i: (i, 0)), pl.BlockSpec((GU_TILE, d), lambda j, i: (j, 0)),
                pl.BlockSpec((tm, 2 * GU_TILE), lambda j, i: (i, j))]
    args = [dy, w_d, gu]
    if after is not None:
        in_specs.append(pl.BlockSpec(memory_space=pl.ANY))
        args.append(after)
    return pl.pallas_call(
        body, name=name, grid=(D_FF // GU_TILE, t // tm), in_specs=in_specs,
        out_specs=pl.BlockSpec((tm, 2 * GU_TILE), lambda j, i: (i, j)),
        out_shape=jax.ShapeDtypeStruct((t, 2 * D_FF), MXU_DTYPE), compiler_params=_cp("parallel", "parallel"))(*args)


def mm_nn_res_loss(a, b, res, target, name, tm=512):
    t, k = a.shape
    d = b.shape[1]
    tm = min(tm, t)

    def body(a_ref, b_ref, res_ref, t_ref, dy_ref, l_ref):
        e = res_ref[...] + _mx(a_ref[...], b_ref[...]) - t_ref[...]
        dy_ref[...] = e * (1.0 / d)
        part = jnp.zeros((1, 128), F32) + 0.5 * jnp.sum(jnp.mean(e * e, axis=-1, keepdims=True), axis=0, keepdims=True)
        l_ref[...] = jnp.where(pl.program_id(0) == 0, part, l_ref[...] + part)

    row = pl.BlockSpec((tm, d), lambda i: (i, 0))
    return pl.pallas_call(
        body, name=name, grid=(t // tm,),
        in_specs=[pl.BlockSpec((tm, k), lambda i: (i, 0)), pl.BlockSpec((k, d), lambda i: (0, 0)), row, row],
        out_specs=(row, pl.BlockSpec((1, 128), lambda i: (0, 0))),
        out_shape=(jax.ShapeDtypeStruct((t, d), F32), jax.ShapeDtypeStruct((1, 128), F32)),
        compiler_params=_cp("arbitrary"))(a, b, res, target)


QK_W = Q_W + KV_W
_QK_TILE = 256


def _qk_mats():
    idx = np.arange(_QK_TILE)
    half = HEAD_DIM // 2
    same = (idx[:, None] // HEAD_DIM) == (idx[None, :] // HEAD_DIM)
    lo = (idx % HEAD_DIM) < half
    rot = np.where((idx[:, None] == idx[None, :] + half) & lo[None, :], -1.0, 0.0)
    rot = rot + np.where((idx[:, None] == idx[None, :] - half) & ~lo[None, :], 1.0, 0.0)
    return jnp.asarray(same, F32), jnp.asarray(rot, F32)


def _qk_rows(q_gain, k_gain, cosf, sinf):
    gain = jnp.concatenate([q_gain] * ATTN_HEADS + [k_gain] * KV_HEADS, axis=-1)
    return gain, jnp.concatenate([cosf, cosf], axis=-1), jnp.concatenate([sinf, sinf], axis=-1)


def _qk_tiles(a, mat, transposed=False):
    outs = []
    for c0 in range(0, QK_W, _QK_TILE):
        w = min(_QK_TILE, QK_W - c0)
        mt = (mat.T if transposed else mat)[:w, :w].astype(MXU_DTYPE)
        at = a[:, c0:c0 + w]
        hi = at.astype(MXU_DTYPE)
        lo = (at - hi.astype(F32)).astype(MXU_DTYPE)
        outs.append(jnp.dot(hi, mt, preferred_element_type=F32) + jnp.dot(lo, mt, preferred_element_type=F32))
    return jnp.concatenate(outs, axis=-1)


def qk_prep_fwd(proj, q_gain, k_gain, cosf, sinf, name, tm=256):
    t = proj.shape[0]
    gmat, rmat = _qk_mats()
    gain, c2, s2 = _qk_rows(q_gain, k_gain, cosf, sinf)
    rep = QK_W // 128

    def body(p_ref, g_ref, c_ref, s_ref, gm_ref, rm_ref, q_ref, k_ref):
        x = p_ref[...]
        r = lax.rsqrt(_qk_tiles(x * x, gm_ref[...]) * (1.0 / HEAD_DIM) + EPS)
        xn = x * r * g_ref[...]
        c = jnp.concatenate([c_ref[...]] * rep, axis=-1)
        s = jnp.concatenate([s_ref[...]] * rep, axis=-1)
        out = xn * c + _qk_tiles(xn, rm_ref[...]) * s
        q_ref[...] = out[:, :Q_W]
        k_ref[...] = out[:, Q_W:]

    full = pl.BlockSpec((_QK_TILE, _QK_TILE), lambda i: (0, 0))
    tab = pl.BlockSpec((tm, 128), lambda i: (i, 0))
    return pl.pallas_call(
        body, name=name, grid=(t // tm,),
        in_specs=[pl.BlockSpec((tm, QK_W), lambda i: (i, 0)), pl.BlockSpec((1, QK_W), lambda i: (0, 0)), tab, tab,
                  full, full],
        out_specs=(pl.BlockSpec((tm, Q_W), lambda i: (i, 0)), pl.BlockSpec((tm, KV_W), lambda i: (i, 0))),
        out_shape=(jax.ShapeDtypeStruct((t, Q_W), F32), jax.ShapeDtypeStruct((t, KV_W), F32)),
        compiler_params=_cp("parallel"))(proj, gain, c2, s2, gmat, rmat)


def qk_prep_bwd(proj, q_gain, k_gain, cosf, sinf, dq, dk, name, tm=256):
    t = proj.shape[0]
    gmat, rmat = _qk_mats()
    gain, c2, s2 = _qk_rows(q_gain, k_gain, cosf, sinf)
    rep = QK_W // 128
    lanes = np.arange(QK_W)[:, None]
    fold = jnp.asarray(lanes % HEAD_DIM + np.where(lanes >= Q_W, HEAD_DIM, 0) == np.arange(128)[None, :], F32)

    def body(p_ref, g_ref, c_ref, s_ref, gm_ref, rm_ref, f_ref, dq_ref, dk_ref, o_ref, dg_ref):
        x = p_ref[...]
        r = lax.rsqrt(_qk_tiles(x * x, gm_ref[...]) * (1.0 / HEAD_DIM) + EPS)
        xh = x * r
        c = jnp.concatenate([c_ref[...]] * rep, axis=-1)
        s = jnp.concatenate([s_ref[...]] * rep, axis=-1)
        dout = jnp.concatenate([dq_ref[...], dk_ref[...]], axis=-1)
        dxn = dout * c + _qk_tiles(dout * s, rm_ref[...], transposed=True)
        part = _hi(jnp.sum(dxn * xh, axis=0, keepdims=True), f_ref[...])
        dxh = dxn * g_ref[...]
        mean = _qk_tiles(dxh * xh, gm_ref[...]) * (1.0 / HEAD_DIM)
        o_ref[...] = (r * (dxh - xh * mean)).astype(o_ref.dtype)
        dg_ref[...] = jnp.where(pl.program_id(0) == 0, part, dg_ref[...] + part)

    full = pl.BlockSpec((_QK_TILE, _QK_TILE), lambda i: (0, 0))
    tab = pl.BlockSpec((tm, 128), lambda i: (i, 0))
    dqk, dg = pl.pallas_call(
        body, name=name, grid=(t // tm,),
        in_specs=[pl.BlockSpec((tm, QK_W), lambda i: (i, 0)), pl.BlockSpec((1, QK_W), lambda i: (0, 0)), tab, tab,
                  full, full, pl.BlockSpec((QK_W, 128), lambda i: (0, 0)),
                  pl.BlockSpec((tm, Q_W), lambda i: (i, 0)), pl.BlockSpec((tm, KV_W), lambda i: (i, 0))],
        out_specs=(pl.BlockSpec((tm, QK_W), lambda i: (i, 0)), pl.BlockSpec((1, 128), lambda i: (0, 0))),
        out_shape=(jax.ShapeDtypeStruct((t, QK_W), MXU_DTYPE), jax.ShapeDtypeStruct((1, 128), F32)),
        compiler_params=_cp("arbitrary"))(proj, gain, c2, s2, gmat, rmat, fold, dq, dk)
    return dqk, dg[:, :HEAD_DIM], dg[:, HEAD_DIM:]


def _swa_valid(n, grp):
    qi = lax.broadcasted_iota(jnp.int32, (grp * ATTN_BLOCK, 2 * ATTN_BLOCK), 0) & (ATTN_BLOCK - 1)
    kj = lax.broadcasted_iota(jnp.int32, (grp * ATTN_BLOCK, 2 * ATTN_BLOCK), 1)
    diff = qi + ATTN_BLOCK - kj
    return (diff >= 0) & (diff < ATTN_BLOCK) & (n * ATTN_BLOCK - ATTN_BLOCK + kj >= 0)


def _stack_heads(ref, g, grp, rows=slice(None)):
    return jnp.concatenate([ref[rows, (g * grp + j) * HEAD_DIM:(g * grp + j + 1) * HEAD_DIM] for j in range(grp)], axis=0)


def _stack_sinks(s_ref, g, grp):
    return jnp.concatenate([jnp.zeros((ATTN_BLOCK, 1), F32) + s_ref[0:1, g * grp + j:g * grp + j + 1]
                            for j in range(grp)], axis=0)


SWA_STEP = 2


def swa_fwd(q, k, proj, sinks, name):
    t = q.shape[0]
    nb = t // ATTN_BLOCK
    scale = HEAD_DIM ** -0.5
    grp = ATTN_HEADS // KV_HEADS

    rows = SWA_STEP * ATTN_BLOCK

    def body(q_ref, kc_ref, kp_ref, vc_ref, vp_ref, s_ref, y_ref, mix_ref, yt_ref, lse_ref):
        n0 = pl.program_id(0) * SWA_STEP
        kk = jnp.concatenate([kp_ref[...], kc_ref[...]], axis=0).astype(MXU_DTYPE)
        vv = jnp.concatenate([vp_ref[...], vc_ref[...]], axis=0).astype(MXU_DTYPE)
        lane = lax.broadcasted_iota(jnp.int32, (ATTN_BLOCK, ATTN_HEADS), 1)
        units = [(b, g) for b in range(SWA_STEP) for g in range(KV_HEADS)]
        blk = lambda b: slice(b * ATTN_BLOCK, (b + 1) * ATTN_BLOCK)
        keys = lambda b: slice(b * ATTN_BLOCK, (b + 2) * ATTN_BLOCK)
        col = lambda g: slice(g * HEAD_DIM, (g + 1) * HEAD_DIM)
        valid = [_swa_valid(n0 + b, grp) for b in range(SWA_STEP)]
        qg = [_stack_heads(q_ref, g, grp, blk(b)) for b, g in units]
        sink = [_stack_sinks(s_ref, g, grp) for b, g in units]
        sc = [jnp.where(valid[b], _mx_nt(qg[u], kk[keys(b), col(g)]) * scale, NEG) for u, (b, g) in enumerate(units)]
        m = [jnp.maximum(jnp.max(sc_, axis=-1, keepdims=True), sk) for sc_, sk in zip(sc, sink)]
        e = [jnp.exp(sc_ - m_) for sc_, m_ in zip(sc, m)]
        den = [jnp.sum(e_, axis=-1, keepdims=True) + jnp.exp(sk - m_) for e_, sk, m_ in zip(e, sink, m)]
        og = [_mx(e[u] / den[u], vv[keys(b), col(g)]) for u, (b, g) in enumerate(units)]
        lg = [m_ + jnp.log(d_) for m_, d_ in zip(m, den)]
        for b in range(SWA_STEP):
            lse = jnp.zeros((ATTN_BLOCK, ATTN_HEADS), F32)
            outs = []
            for h in range(ATTN_HEADS):
                u = b * KV_HEADS + h // grp
                sub = blk(h % grp)
                outs.append(og[u][sub])
                lse = jnp.where(lane == h, lg[u][sub], lse)
            y = jnp.concatenate(outs, axis=-1)
            y_ref[blk(b), :] = y
            mix_ref[blk(b), :] = y.astype(mix_ref.dtype)
            yt_ref[:, blk(b)] = y.T.astype(yt_ref.dtype)
            lse_ref[blk(b), :] = lse

    cur = lambda n: (n, 0)
    prev = lambda n: (jnp.maximum(n * SWA_STEP - 1, 0), 0)
    vcol = (Q_W + KV_W) // KV_W
    return pl.pallas_call(
        body, name=name, grid=(nb // SWA_STEP,),
        in_specs=[pl.BlockSpec((rows, Q_W), cur), pl.BlockSpec((rows, KV_W), cur),
                  pl.BlockSpec((ATTN_BLOCK, KV_W), prev),
                  pl.BlockSpec((rows, KV_W), lambda n: (n, vcol)),
                  pl.BlockSpec((ATTN_BLOCK, KV_W), lambda n: (jnp.maximum(n * SWA_STEP - 1, 0), vcol)),
                  pl.BlockSpec((1, ATTN_HEADS), lambda n: (0, 0))],
        out_specs=(pl.BlockSpec((rows, Q_W), cur), pl.BlockSpec((rows, Q_W), cur),
                   pl.BlockSpec((Q_W, rows), lambda n: (0, n)), pl.BlockSpec((rows, ATTN_HEADS), cur)),
        out_shape=(jax.ShapeDtypeStruct((t, Q_W), F32), jax.ShapeDtypeStruct((t, Q_W + CONV_CH), MXU_DTYPE),
                   jax.ShapeDtypeStruct((Q_W + CONV_CH, t), MXU_DTYPE), jax.ShapeDtypeStruct((t, ATTN_HEADS), F32)),
        compiler_params=_cp("parallel"))(q, k, k, proj, proj, sinks)


def swa_bwd(q, k, proj, sinks, y, lse, dmix, name):
    t = q.shape[0]
    nb = t // ATTN_BLOCK
    scale = HEAD_DIM ** -0.5
    grp = ATTN_HEADS // KV_HEADS

    def body(q_ref, kc_ref, kp_ref, vc_ref, vp_ref, s_ref, y_ref, lse_ref, dy_ref,
             dq_ref, dk_ref, dv_ref, ds_ref, dkc, dvc):
        n = pl.program_id(0)

        @pl.when(n == 0)
        def _():
            dkc[...] = jnp.zeros_like(dkc)
            dvc[...] = jnp.zeros_like(dvc)
            ds_ref[...] = jnp.zeros_like(ds_ref)

        @pl.when(n < nb)
        def _():
            valid = _swa_valid(n, grp)
            kk = jnp.concatenate([kp_ref[...], kc_ref[...]], axis=0).astype(MXU_DTYPE)
            vv = jnp.concatenate([vp_ref[...], vc_ref[...]], axis=0).astype(MXU_DTYPE)
            lane = lax.broadcasted_iota(jnp.int32, (1, ATTN_HEADS), 1)
            gs = range(KV_HEADS)
            kg = [kk[:, g * HEAD_DIM:(g + 1) * HEAD_DIM] for g in gs]
            vg = [vv[:, g * HEAD_DIM:(g + 1) * HEAD_DIM] for g in gs]
            qg = [_stack_heads(q_ref, g, grp).astype(MXU_DTYPE) for g in gs]
            dog = [_stack_heads(dy_ref, g, grp) for g in gs]
            og = [_stack_heads(y_ref, g, grp) for g in gs]
            lg = [jnp.concatenate([lse_ref[:, g * grp + j:g * grp + j + 1] for j in range(grp)], axis=0) for g in gs]
            sink = [_stack_sinks(s_ref, g, grp) for g in gs]
            sc = [jnp.where(valid, _mx_nt(qg[g], kg[g]) * scale, NEG) for g in gs]
            p = [jnp.exp(sc[g] - lg[g]) for g in gs]
            delta = [jnp.sum(dog[g] * og[g], axis=-1, keepdims=True) for g in gs]
            ds = [p[g] * (_mx_nt(dog[g], vg[g]) - delta[g]) for g in gs]
            dqg = [_mx(ds[g], kg[g]) * scale for g in gs]
            dkf = jnp.concatenate([_mx_tn(ds[g], qg[g]) * scale for g in gs], axis=-1)
            dvf = jnp.concatenate([_mx_tn(p[g], dog[g]) for g in gs], axis=-1)
            dsk = [jnp.exp(sink[g] - lg[g]) * delta[g] for g in gs]
            dsink = jnp.zeros((1, ATTN_HEADS), F32)
            dqs = []
            for h in range(ATTN_HEADS):
                rows = slice((h % grp) * ATTN_BLOCK, (h % grp + 1) * ATTN_BLOCK)
                dqs.append(dqg[h // grp][rows])
                dsink = jnp.where(lane == h, -jnp.sum(dsk[h // grp][rows], axis=0, keepdims=True), dsink)
            dq_ref[...] = jnp.concatenate(dqs, axis=-1)
            dk_ref[...] = dkc[...] + dkf[:ATTN_BLOCK]
            dv_ref[...] = (dvc[...] + dvf[:ATTN_BLOCK]).astype(dv_ref.dtype)
            dkc[...] = dkf[ATTN_BLOCK:]
            dvc[...] = dvf[ATTN_BLOCK:]
            ds_ref[...] += dsink

        @pl.when(n == nb)
        def _():
            dk_ref[...] = dkc[...]
            dv_ref[...] = dvc[...].astype(dv_ref.dtype)

    cur = lambda n: (jnp.minimum(n, nb - 1), 0)
    prev = lambda n: (jnp.clip(n - 1, 0, nb - 1), 0)
    vcol = (Q_W + KV_W) // KV_W
    return pl.pallas_call(
        body, name=name, grid=(nb + 1,),
        in_specs=[pl.BlockSpec((ATTN_BLOCK, Q_W), cur), pl.BlockSpec((ATTN_BLOCK, KV_W), cur),
                  pl.BlockSpec((ATTN_BLOCK, KV_W), prev),
                  pl.BlockSpec((ATTN_BLOCK, KV_W), lambda n: (jnp.minimum(n, nb - 1), vcol)),
                  pl.BlockSpec((ATTN_BLOCK, KV_W), lambda n: (jnp.clip(n - 1, 0, nb - 1), vcol)),
                  pl.BlockSpec((1, ATTN_HEADS), lambda n: (0, 0)),
                  pl.BlockSpec((ATTN_BLOCK, Q_W), cur), pl.BlockSpec((ATTN_BLOCK, ATTN_HEADS), cur),
                  pl.BlockSpec((ATTN_BLOCK, Q_W), cur)],
        out_specs=(pl.BlockSpec((ATTN_BLOCK, Q_W), cur), pl.BlockSpec((ATTN_BLOCK, KV_W), prev),
                   pl.BlockSpec((ATTN_BLOCK, KV_W), prev), pl.BlockSpec((1, ATTN_HEADS), lambda n: (0, 0))),
        out_shape=(jax.ShapeDtypeStruct((t, Q_W), F32), jax.ShapeDtypeStruct((t, KV_W), F32),
                   jax.ShapeDtypeStruct((t, KV_W), MXU_DTYPE), jax.ShapeDtypeStruct((1, ATTN_HEADS), F32)),
        scratch_shapes=[pltpu.VMEM((ATTN_BLOCK, KV_W), F32), pltpu.VMEM((ATTN_BLOCK, KV_W), F32)],
        compiler_params=_cp("arbitrary"))(q, k, k, proj, proj, sinks, y, lse, dmix)


GC_W = 256
_GB0, _GC0, _XI0 = 768 // GC_W, 1280 // GC_W, 1792 // GC_W
HALO = 8


def gconv_fwd(proj, conv_w, mix, mix_t, name, tm=512):
    t = proj.shape[0]
    hb = tm // HALO
    half = Q_W // GC_W

    def body(gb_ref, gc_ref, xi_ref, gch_ref, xih_ref, w_ref, mix_in, mixt_in, y_ref, yt_ref):
        i = pl.program_id(1)
        u = gc_ref[...] * xi_ref[...]
        uh = jnp.where(i == 0, 0.0, gch_ref[...] * xih_ref[...])
        up = jnp.concatenate([uh, u], axis=0)
        cv = w_ref[0:1, :] * up[HALO - 2:HALO - 2 + tm]
        cv = cv + w_ref[1:2, :] * up[HALO - 1:HALO - 1 + tm]
        cv = cv + w_ref[2:3, :] * u
        y = gb_ref[...] * cv
        y_ref[...] = y.astype(y_ref.dtype)
        yt_ref[...] = y.T.astype(yt_ref.dtype)

    def col(c0):
        return pl.BlockSpec((tm, GC_W), lambda cj, i: (i, c0 + cj))

    def halo(c0):
        return pl.BlockSpec((HALO, GC_W), lambda cj, i: (jnp.maximum(i * hb - 1, 0), c0 + cj))

    return pl.pallas_call(
        body, name=name, grid=(CONV_CH // GC_W, t // tm),
        in_specs=[col(_GB0), col(_GC0), col(_XI0), halo(_GC0), halo(_XI0),
                  pl.BlockSpec((3, GC_W), lambda cj, i: (0, cj)),
                  pl.BlockSpec(memory_space=pl.ANY), pl.BlockSpec(memory_space=pl.ANY)],
        out_specs=(pl.BlockSpec((tm, GC_W), lambda cj, i: (i, half + cj)),
                   pl.BlockSpec((GC_W, tm), lambda cj, i: (half + cj, i))),
        out_shape=(jax.ShapeDtypeStruct(mix.shape, mix.dtype), jax.ShapeDtypeStruct(mix_t.shape, mix_t.dtype)),
        input_output_aliases={6: 0, 7: 1},
        compiler_params=_cp("parallel", "parallel"))(proj, proj, proj, proj, proj, conv_w, mix, mix_t)


def gconv_bwd(proj, conv_w, dmix, name, tm=512):
    t = proj.shape[0]
    hb = tm // HALO
    nt = t // tm
    dy0 = Q_W // GC_W

    def body(gb_ref, gc_ref, xi_ref, gch_ref, xih_ref, gbn_ref, dyn_ref, dy_ref, w_ref,
             dgb_ref, dgc_ref, dxi_ref, dw_ref):
        i = pl.program_id(1)
        gc, xi, gb, dy = gc_ref[...], xi_ref[...], gb_ref[...], dy_ref[...]
        u = gc * xi
        uh = jnp.where(i == 0, 0.0, gch_ref[...] * xih_ref[...])
        up = jnp.concatenate([uh, u], axis=0)
        u2 = up[HALO - 2:HALO - 2 + tm]
        u1 = up[HALO - 1:HALO - 1 + tm]
        cv = w_ref[0:1, :] * u2 + w_ref[1:2, :] * u1 + w_ref[2:3, :] * u
        dgb_ref[...] = (dy * cv).astype(dgb_ref.dtype)
        dcv = dy * gb
        dcvn = jnp.where(i == nt - 1, 0.0, dyn_ref[...] * gbn_ref[...])
        dcvp = jnp.concatenate([dcv, dcvn], axis=0)
        du = w_ref[0:1, :] * dcvp[2:2 + tm] + w_ref[1:2, :] * dcvp[1:1 + tm] + w_ref[2:3, :] * dcv
        dgc_ref[...] = (du * xi).astype(dgc_ref.dtype)
        dxi_ref[...] = (du * gc).astype(dxi_ref.dtype)
        dw = jnp.concatenate([jnp.sum(dcv * u2, axis=0, keepdims=True), jnp.sum(dcv * u1, axis=0, keepdims=True),
                              jnp.sum(dcv * u, axis=0, keepdims=True)], axis=0)

        @pl.when(i == 0)
        def _():
            dw_ref[...] = dw

        @pl.when(i > 0)
        def _():
            dw_ref[...] += dw

    def col(c0):
        return pl.BlockSpec((tm, GC_W), lambda cj, i: (i, c0 + cj))

    def halo(c0):
        return pl.BlockSpec((HALO, GC_W), lambda cj, i: (jnp.maximum(i * hb - 1, 0), c0 + cj))

    def nxt(c0):
        return pl.BlockSpec((HALO, GC_W), lambda cj, i: (jnp.minimum((i + 1) * hb, t // HALO - 1), c0 + cj))

    out = pl.BlockSpec((tm, GC_W), lambda cj, i: (i, cj))
    return pl.pallas_call(
        body, name=name, grid=(CONV_CH // GC_W, nt),
        in_specs=[col(_GB0), col(_GC0), col(_XI0), halo(_GC0), halo(_XI0), nxt(_GB0), nxt(dy0), col(dy0),
                  pl.BlockSpec((3, GC_W), lambda cj, i: (0, cj))],
        out_specs=(out, out, out, pl.BlockSpec((3, GC_W), lambda cj, i: (0, cj))),
        out_shape=(jax.ShapeDtypeStruct((t, CONV_CH), MXU_DTYPE),) * 3 + (jax.ShapeDtypeStruct((3, CONV_CH), F32),),
        compiler_params=_cp("parallel", "arbitrary"))(proj, proj, proj, proj, proj, proj, dmix, dmix, conv_w)


_QKV_W = 3 * DN_W
_BA_COL = (4 * DN_W) // 128
_Z_COL = _QKV_W // DN_W


def gdn_prep_fwd(proj, conv_w, alog_row, dtb_row, name, tm=256):
    t = proj.shape[0]
    hb = tm // HALO
    qscale = DN_DIM ** -0.5

    def body(x_ref, xh_ref, w_ref, ba_ref, al_ref, dt_ref, q_ref, k_ref, v_ref, bg_ref, c_ref):
        i = pl.program_id(0)
        for gi in range(3 * DN_HEADS):
            sl = slice(gi * DN_DIM, (gi + 1) * DN_DIM)
            xp = jnp.concatenate([jnp.where(i == 0, 0.0, xh_ref[:, sl]), x_ref[:, sl]], axis=0)
            c = w_ref[0:1, sl] * xp[HALO - 3:HALO - 3 + tm]
            for j in range(1, 4):
                c = c + w_ref[j:j + 1, sl] * xp[HALO - 3 + j:HALO - 3 + j + tm]
            c_ref[:, sl] = c
            s = c * _sigmoid(c)
            osl = slice((gi % DN_HEADS) * DN_DIM, (gi % DN_HEADS + 1) * DN_DIM)
            if gi < DN_HEADS:
                q_ref[:, osl] = s * lax.rsqrt(jnp.sum(s * s, axis=-1, keepdims=True) + EPS) * qscale
            elif gi < 2 * DN_HEADS:
                k_ref[:, osl] = s * lax.rsqrt(jnp.sum(s * s, axis=-1, keepdims=True) + EPS)
            else:
                v_ref[:, osl] = s
        ba = ba_ref[...]
        lane = lax.broadcasted_iota(jnp.int32, ba.shape, 1)
        gval = -jnp.exp(al_ref[...]) * _softplus(ba + dt_ref[...])
        bg_ref[...] = jnp.where(lane < DN_HEADS, _sigmoid(ba), jnp.where(lane < 2 * DN_HEADS, gval, 0.0))

    row = pl.BlockSpec((tm, DN_W), lambda i: (i, 0))
    one = pl.BlockSpec((1, 128), lambda i: (0, 0))
    return pl.pallas_call(
        body, name=name, grid=(t // tm,),
        in_specs=[pl.BlockSpec((tm, _QKV_W), lambda i: (i, 0)),
                  pl.BlockSpec((HALO, _QKV_W), lambda i: (jnp.maximum(i * hb - 1, 0), 0)),
                  pl.BlockSpec((4, _QKV_W), lambda i: (0, 0)),
                  pl.BlockSpec((tm, 128), lambda i: (i, _BA_COL)), one, one],
        out_specs=(row, row, row, pl.BlockSpec((tm, 128), lambda i: (i, 0)), pl.BlockSpec((tm, _QKV_W), lambda i: (i, 0))),
        out_shape=(jax.ShapeDtypeStruct((t, DN_W), F32),) * 3 + (jax.ShapeDtypeStruct((t, 128), F32),
                                                                 jax.ShapeDtypeStruct((t, _QKV_W), F32)),
        compiler_params=_cp("parallel"))(proj, proj, conv_w, proj, alog_row, dtb_row)


def gdn_prep_bwd(proj, conv, conv_w, alog_row, dtb_row, dq, dk, dv, dbg, dz, name, tm=256):
    t = proj.shape[0]
    hb = tm // HALO
    nt = t // tm
    qscale = DN_DIM ** -0.5
    te = tm + HALO

    def body(x_ref, c_ref, cn_ref, w_ref, ba_ref, al_ref, dt_ref, dq_ref, dk_ref, dv_ref,
             dqn_ref, dkn_ref, dvn_ref, dbg_ref, dz_ref, dx_ref, dw_ref, ddt_ref, dal_ref):
        i = pl.program_id(0)
        first = i == 0
        last = i == nt - 1
        dws = []
        for gi in range(3 * DN_HEADS):
            sl = slice(gi * DN_DIM, (gi + 1) * DN_DIM)
            osl = slice((gi % DN_HEADS) * DN_DIM, (gi % DN_HEADS + 1) * DN_DIM)
            c = jnp.concatenate([c_ref[:, sl], cn_ref[:, sl]], axis=0)
            sg = _sigmoid(c)
            s = c * sg
            d_ref, dn_ref = ((dq_ref, dqn_ref), (dk_ref, dkn_ref), (dv_ref, dvn_ref))[gi // DN_HEADS]
            dy = jnp.concatenate([d_ref[:, osl], jnp.where(last, 0.0, dn_ref[:, osl])], axis=0)
            if gi < 2 * DN_HEADS:
                r = lax.rsqrt(jnp.sum(s * s, axis=-1, keepdims=True) + EPS)
                sh = s * r
                ds = r * (dy - sh * jnp.sum(sh * dy, axis=-1, keepdims=True))
                if gi < DN_HEADS:
                    ds = ds * qscale
            else:
                ds = dy
            dc = ds * sg * (1.0 + c * (1.0 - sg))
            dcs = [dc[3 - j:3 - j + tm] for j in range(4)]
            dx = w_ref[0:1, sl] * dcs[0]
            for j in range(1, 4):
                dx = dx + w_ref[j:j + 1, sl] * dcs[j]
            dx_ref[:, sl] = dx.astype(dx_ref.dtype)
            x0 = x_ref[:, sl]
            dws.append(jnp.concatenate([jnp.sum(dcs[j] * x0, axis=0, keepdims=True) for j in range(4)], axis=0))
        dw = jnp.concatenate(dws, axis=-1)
        ba = ba_ref[...]
        dbgv = dbg_ref[...]
        lane = lax.broadcasted_iota(jnp.int32, ba.shape, 1)
        beta = _sigmoid(ba)
        ea = -jnp.exp(al_ref[...])
        zin = ba + dt_ref[...]
        is_b = lane < DN_HEADS
        is_a = (lane >= DN_HEADS) & (lane < 2 * DN_HEADS)
        da = jnp.where(is_a, dbgv * ea * _sigmoid(zin), 0.0)
        dx_ref[:, _QKV_W:_QKV_W + DN_W] = dz_ref[...]
        dx_ref[:, _QKV_W + DN_W:] = jnp.where(is_b, dbgv * beta * (1.0 - beta), da).astype(dx_ref.dtype)
        ddt = jnp.sum(da, axis=0, keepdims=True)
        dal = jnp.sum(jnp.where(is_a, dbgv * ea * _softplus(zin), 0.0), axis=0, keepdims=True)

        @pl.when(first)
        def _():
            dw_ref[...] = dw
            ddt_ref[...] = ddt
            dal_ref[...] = dal

        @pl.when(i > 0)
        def _():
            dw_ref[...] += dw
            ddt_ref[...] += ddt
            dal_ref[...] += dal

    row = pl.BlockSpec((tm, DN_W), lambda i: (i, 0))
    nrow = pl.BlockSpec((HALO, DN_W), lambda i: (jnp.minimum((i + 1) * hb, t // HALO - 1), 0))
    one = pl.BlockSpec((1, 128), lambda i: (0, 0))
    return pl.pallas_call(
        body, name=name, grid=(nt,),
        in_specs=[pl.BlockSpec((tm, _QKV_W), lambda i: (i, 0)), pl.BlockSpec((tm, _QKV_W), lambda i: (i, 0)),
                  pl.BlockSpec((HALO, _QKV_W), lambda i: (jnp.minimum((i + 1) * hb, t // HALO - 1), 0)),
                  pl.BlockSpec((4, _QKV_W), lambda i: (0, 0)),
                  pl.BlockSpec((tm, 128), lambda i: (i, _BA_COL)), one, one,
                  row, row, row, nrow, nrow, nrow, pl.BlockSpec((tm, 128), lambda i: (i, 0)), row],
        out_specs=(pl.BlockSpec((tm, ODD_IN_PAD), lambda i: (i, 0)), pl.BlockSpec((4, _QKV_W), lambda i: (0, 0)), one, one),
        out_shape=(jax.ShapeDtypeStruct((t, ODD_IN_PAD), MXU_DTYPE), jax.ShapeDtypeStruct((4, _QKV_W), F32),
                   jax.ShapeDtypeStruct((1, 128), F32), jax.ShapeDtypeStruct((1, 128), F32)),
        compiler_params=_cp("arbitrary"))(proj, conv, conv, conv_w, proj, alog_row, dtb_row, dq, dk, dv, dq, dk, dv, dbg,
                                          dz)


def _chunk_masks():
    r = lax.broadcasted_iota(jnp.int32, (DN_CHUNK, DN_CHUNK), 0)
    c = lax.broadcasted_iota(jnp.int32, (DN_CHUNK, DN_CHUNK), 1)
    return r >= c, r > c


INV_PACK = 2


def _inv_unit_lower_many(mats):
    n = DN_CHUNK
    wide = INV_PACK * n
    r = lax.broadcasted_iota(jnp.int32, (wide, wide), 0)
    c = lax.broadcasted_iota(jnp.int32, (wide, wide), 1)
    same = (r & -n) == (c & -n)
    eye = jnp.where((r[:n] == (c[:n] & (n - 1))), 1.0, 0.0)

    def blockdiag(row):
        return jnp.where(same, jnp.concatenate([row] * INV_PACK, axis=0), 0.0)

    packs = [jnp.concatenate(mats[g:g + INV_PACK], axis=-1) for g in range(0, len(mats), INV_PACK)]
    xs = [eye - a for a in packs]
    pws = [_hi(a, blockdiag(a)) for a in packs]
    for step in range(5):
        if step < 4:
            both = [_hi(jnp.concatenate([x, pw], axis=0), blockdiag(pw)) for x, pw in zip(xs, pws)]
            xs = [x + b[:n] for x, b in zip(xs, both)]
            pws = [b[n:] for b in both]
        else:
            xs = [x + _hi(x, blockdiag(pw)) for x, pw in zip(xs, pws)]
    return [x[:, j * n:(j + 1) * n] for x in xs for j in range(INV_PACK)]


def _chunk_common(q, k, v, beta, gc, gcr, lower, strict):
    gam = jnp.exp(jnp.where(lower, gc - gcr, NEG))
    eg = jnp.exp(gc)
    gl = gc[DN_CHUNK - 1:DN_CHUNK, :]
    kdf = jnp.exp(gl - gc)
    kb = k * beta
    bmat = _mx_nt(kb, k)
    qmat = _mx_nt(q, k)
    return gam, eg, jnp.exp(gl), kdf, kb, bmat, qmat


DN_STEP = 4


def gdn_fwd(q, k, v, bg, name):
    t = q.shape[0]
    n_chunks = t // DN_CHUNK

    def body(q_ref, k_ref, v_ref, bg_ref, o_ref, sall_ref, tall_ref, s_ref):
        n = pl.program_id(0)

        @pl.when(n == 0)
        def _():
            s_ref[...] = jnp.zeros_like(s_ref)

        lower, strict = _chunk_masks()
        ltri = jnp.where(lower, 1.0, 0.0)
        hs = range(DN_HEADS)
        sl = [slice(h * DN_DIM, (h + 1) * DN_DIM) for h in hs]
        units = [(c, h) for c in range(DN_STEP) for h in hs]
        nu = range(len(units))
        rs = [slice(c * DN_CHUNK, (c + 1) * DN_CHUNK) for c in range(DN_STEP)]
        bgv = [bg_ref[rs[c], :] for c in range(DN_STEP)]
        gcs = [_hi(ltri, b) for b in bgv]
        gcs_t = [g.T for g in gcs]
        qh = [q_ref[rs[c], sl[h]] for c, h in units]
        kh = [k_ref[rs[c], sl[h]] for c, h in units]
        vh = [v_ref[rs[c], sl[h]] for c, h in units]
        beta = [bgv[c][:, h:h + 1] for c, h in units]
        com = [_chunk_common(qh[u], kh[u], vh[u], beta[u], gcs[c][:, DN_HEADS + h:DN_HEADS + h + 1],
                             gcs_t[c][DN_HEADS + h:DN_HEADS + h + 1, :], lower, strict) for u, (c, h) in enumerate(units)]
        gam, eg, dec, kdf, kb, bmat, qmat = zip(*com)
        tms = _inv_unit_lower_many([jnp.where(strict, bmat[u] * gam[u], 0.0) for u in nu])
        for u, (c, h) in enumerate(units):
            tall_ref[c, h] = tms[u]
        uw = [_hi(tms[u], jnp.concatenate([vh[u] * beta[u], kb[u] * eg[u]], axis=-1)) for u in nu]
        qd = [qh[u] * eg[u] for u in nu]
        pm = [qmat[u] * gam[u] for u in nu]
        kd = [kh[u] * kdf[u] for u in nu]
        st = [s_ref[h] for h in hs]
        for c in range(DN_STEP):
            us = [c * DN_HEADS + h for h in hs]
            for h in hs:
                sall_ref[c, h] = st[h]
            v_new = [uw[us[h]][:, :DN_DIM] - _mx(uw[us[h]][:, DN_DIM:], st[h]) for h in hs]
            o_st = [_mx(qd[us[h]], st[h]) for h in hs]
            o_in = [_mx(pm[us[h]], v_new[h]) for h in hs]
            s_up = [_mx_tn(kd[us[h]], v_new[h]) for h in hs]
            for h in hs:
                o_ref[rs[c], sl[h]] = o_st[h] + o_in[h]
            st = [st[h] * dec[us[h]] + s_up[h] for h in hs]
        for h in hs:
            s_ref[h] = st[h]

    rows = DN_STEP * DN_CHUNK
    row = pl.BlockSpec((rows, DN_W), lambda n: (n, 0))
    return pl.pallas_call(
        body, name=name, grid=(n_chunks // DN_STEP,),
        in_specs=[row, row, row, pl.BlockSpec((rows, 128), lambda n: (n, 0))],
        out_specs=(row, pl.BlockSpec((DN_STEP, DN_HEADS, DN_DIM, DN_DIM), lambda n: (n, 0, 0, 0)),
                   pl.BlockSpec((DN_STEP, DN_HEADS, DN_CHUNK, DN_CHUNK), lambda n: (n, 0, 0, 0))),
        out_shape=(jax.ShapeDtypeStruct((t, DN_W), F32),
                   jax.ShapeDtypeStruct((n_chunks, DN_HEADS, DN_DIM, DN_DIM), F32),
                   jax.ShapeDtypeStruct((n_chunks, DN_HEADS, DN_CHUNK, DN_CHUNK), F32)),
        scratch_shapes=[pltpu.VMEM((DN_HEADS, DN_DIM, DN_DIM), F32)],
        compiler_params=_cp("arbitrary"))(q, k, v, bg)


def gdn_bwd(q, k, v, bg, sall, tall, do, name):
    t = q.shape[0]
    n_chunks = t // DN_CHUNK

    def body(q_ref, k_ref, v_ref, bg_ref, sall_ref, tall_ref, do_ref, dq_ref, dk_ref, dv_ref, dbg_ref, ds_ref):
        n = pl.program_id(0)

        @pl.when(n == 0)
        def _():
            ds_ref[...] = jnp.zeros_like(ds_ref)

        lower, strict = _chunk_masks()
        ltri = jnp.where(lower, 1.0, 0.0)
        bgv = bg_ref[...]
        gcs = _hi(ltri, bgv)
        gcs_t = gcs.T
        lane = lax.broadcasted_iota(jnp.int32, (DN_CHUNK, 128), 1)
        rowi = lax.broadcasted_iota(jnp.int32, (DN_CHUNK, 1), 0)
        hs = range(DN_HEADS)
        each = lambda fn, *ls: [fn(*a) for a in zip(*ls)]
        rsum = lambda a: jnp.sum(a, axis=-1, keepdims=True)
        sl = [slice(h * DN_DIM, (h + 1) * DN_DIM) for h in hs]
        st = [sall_ref[0, h] for h in hs]
        tms = [tall_ref[0, h] for h in hs]
        dsn = [ds_ref[h] for h in hs]
        qh = [q_ref[:, sl[h]] for h in hs]
        kh = [k_ref[:, sl[h]] for h in hs]
        vh = [v_ref[:, sl[h]] for h in hs]
        doh = [do_ref[:, sl[h]] for h in hs]
        beta = [bgv[:, h:h + 1] for h in hs]
        com = [_chunk_common(qh[h], kh[h], vh[h], beta[h], gcs[:, DN_HEADS + h:DN_HEADS + h + 1],
                             gcs_t[DN_HEADS + h:DN_HEADS + h + 1, :], lower, strict) for h in hs]
        gam, eg, dec, kdf, kb, bmat, qmat = zip(*com)
        rhs_w = each(lambda a, b: a * b, kb, eg)
        uw = each(lambda t_, v_, b_, r_: _hi(t_, jnp.concatenate([v_ * b_, r_], axis=-1)), tms, vh, beta, rhs_w)
        qd = each(lambda a, b: a * b, qh, eg)
        kd = each(lambda a, b: a * b, kh, kdf)
        pmat = each(lambda a, b: a * b, qmat, gam)
        v_new = each(lambda uw_, s_: uw_[:, :DN_DIM] - _mx(uw_[:, DN_DIM:], s_), uw, st)
        dqd = each(_mx_nt, doh, st)
        ds_o = each(_mx_tn, qd, doh)
        dp = each(lambda d_, v_: jnp.where(lower, _mx_nt(d_, v_), 0.0), doh, v_new)
        dvn_o = each(_mx_tn, pmat, doh)
        ddec = each(lambda d_, s_: jnp.sum(rsum(d_ * s_), axis=0, keepdims=True), dsn, st)
        dkd = each(_mx_nt, v_new, dsn)
        dvn = each(lambda a, k_, d_: a + _mx(k_, d_), dvn_o, kd, dsn)
        dw = each(lambda d_, s_: -_mx_nt(d_, s_), dvn, st)
        ds_w = each(lambda uw_, d_: _mx_tn(uw_[:, DN_DIM:], d_), uw, dvn)
        for h in hs:
            ds_ref[h] = ds_o[h] + dec[h] * dsn[h] - ds_w[h]
        dr = each(lambda t_, a, b: _hi_tn(t_, jnp.concatenate([a, b], axis=-1)), tms, dvn, dw)
        da = each(lambda r_, uw_: jnp.where(strict, -_hi_nt(r_, uw_), 0.0), dr, uw)
        dru = [r_[:, :DN_DIM] for r_ in dr]
        drw = [r_[:, DN_DIM:] for r_ in dr]
        db = each(lambda a, b: a * b, da, gam)
        dq_m = each(lambda a, b: a * b, dp, gam)
        e = each(lambda a, bm, p_, qm, g_: (a * bm + p_ * qm) * g_, da, bmat, dp, qmat, gam)
        dkb = each(lambda b_, k_, r_, e_: _mx(b_, k_) + r_ * e_, db, kh, drw, eg)
        dk = each(lambda b_, kb_, m_, q_, d_, f_: _mx_tn(b_, kb_) + _mx_tn(m_, q_) + d_ * f_, db, kb, dq_m, qh, dkd, kdf)
        dq = each(lambda m_, k_, d_, e_: _mx(m_, k_) + d_ * e_, dq_m, kh, dqd, eg)
        tk = each(lambda a, b: rsum(a * b), dkd, kd)
        dbeta_all = jnp.zeros((DN_CHUNK, 128), F32)
        dgc_all = jnp.zeros((DN_CHUNK, 128), F32)
        for h in hs:
            dgc = (jnp.sum(e[h], axis=1, keepdims=True) - jnp.sum(e[h].T, axis=1, keepdims=True)
                   + rsum(dqd[h] * qd[h]) - tk[h] + rsum(drw[h] * rhs_w[h]))
            dgl = jnp.sum(tk[h], axis=0, keepdims=True) + ddec[h] * dec[h]
            dgc = dgc + jnp.where(rowi == DN_CHUNK - 1, dgl, 0.0)
            dbeta = rsum(dru[h] * vh[h]) + rsum(dkb[h] * kh[h])
            dq_ref[:, sl[h]] = dq[h]
            dk_ref[:, sl[h]] = dk[h] + dkb[h] * beta[h]
            dv_ref[:, sl[h]] = dru[h] * beta[h]
            dbeta_all = jnp.where(lane == h, dbeta, dbeta_all)
            dgc_all = jnp.where(lane == DN_HEADS + h, dgc, dgc_all)
        dbg_ref[...] = dbeta_all + _hi_tn(ltri, dgc_all)

    rev = lambda n: (n_chunks - 1 - n, 0)
    row = pl.BlockSpec((DN_CHUNK, DN_W), rev)
    small = pl.BlockSpec((DN_CHUNK, 128), rev)
    return pl.pallas_call(
        body, name=name, grid=(n_chunks,),
        in_specs=[row, row, row, small,
                  pl.BlockSpec((1, DN_HEADS, DN_DIM, DN_DIM), lambda n: (n_chunks - 1 - n, 0, 0, 0)),
                  pl.BlockSpec((1, DN_HEADS, DN_CHUNK, DN_CHUNK), lambda n: (n_chunks - 1 - n, 0, 0, 0)), row],
        out_specs=(row, row, row, small),
        out_shape=(jax.ShapeDtypeStruct((t, DN_W), F32),) * 3 + (jax.ShapeDtypeStruct((t, 128), F32),),
        scratch_shapes=[pltpu.VMEM((DN_HEADS, DN_DIM, DN_DIM), F32)],
        compiler_params=_cp("arbitrary"))(q, k, v, bg, sall, tall, do)


def gdn_out_fwd(o, proj, o_gain, name, tm=256):
    t = o.shape[0]

    def body(o_ref, z_ref, g_ref, y_ref, yt_ref):
        for h in range(DN_HEADS):
            sl = slice(h * DN_DIM, (h + 1) * DN_DIM)
            ov, zv = o_ref[:, sl], z_ref[:, sl]
            r = lax.rsqrt(jnp.mean(ov * ov, axis=-1, keepdims=True) + EPS)
            y = ov * r * g_ref[...] * (zv * _sigmoid(zv))
            y_ref[:, sl] = y.astype(y_ref.dtype)
            yt_ref[sl, :] = y.T.astype(yt_ref.dtype)

    row = pl.BlockSpec((tm, DN_W), lambda i: (i, 0))
    return pl.pallas_call(
        body, name=name, grid=(t // tm,),
        in_specs=[row, pl.BlockSpec((tm, DN_W), lambda i: (i, _Z_COL)), pl.BlockSpec((1, DN_DIM), lambda i: (0, 0))],
        out_specs=(row, pl.BlockSpec((DN_W, tm), lambda i: (0, i))),
        out_shape=(jax.ShapeDtypeStruct((t, DN_W), MXU_DTYPE), jax.ShapeDtypeStruct((DN_W, t), MXU_DTYPE)),
        compiler_params=_cp("parallel"))(o, proj, o_gain)


def gdn_out_bwd(o, proj, o_gain, dy, name, tm=256):
    t = o.shape[0]

    def body(o_ref, z_ref, g_ref, dy_ref, do_ref, dz_ref, dg_ref):
        i = pl.program_id(0)
        dg = jnp.zeros((1, DN_DIM), F32)
        for h in range(DN_HEADS):
            sl = slice(h * DN_DIM, (h + 1) * DN_DIM)
            ov, zv, dyv = o_ref[:, sl], z_ref[:, sl], dy_ref[:, sl]
            r = lax.rsqrt(jnp.mean(ov * ov, axis=-1, keepdims=True) + EPS)
            oh = ov * r
            sg = _sigmoid(zv)
            dz_ref[:, sl] = (dyv * oh * g_ref[...] * sg * (1.0 + zv * (1.0 - sg))).astype(dz_ref.dtype)
            don = dyv * (zv * sg)
            dg = dg + jnp.sum(don * oh, axis=0, keepdims=True)
            doh = don * g_ref[...]
            do_ref[:, sl] = r * (doh - oh * jnp.mean(doh * oh, axis=-1, keepdims=True))

        @pl.when(i == 0)
        def _():
            dg_ref[...] = dg

        @pl.when(i > 0)
        def _():
            dg_ref[...] += dg

    row = pl.BlockSpec((tm, DN_W), lambda i: (i, 0))
    one = pl.BlockSpec((1, DN_DIM), lambda i: (0, 0))
    return pl.pallas_call(
        body, name=name, grid=(t // tm,),
        in_specs=[row, pl.BlockSpec((tm, DN_W), lambda i: (i, _Z_COL)), one, row],
        out_specs=(row, row, one),
        out_shape=(jax.ShapeDtypeStruct((t, DN_W), F32), jax.ShapeDtypeStruct((t, DN_W), MXU_DTYPE),
                   jax.ShapeDtypeStruct((1, DN_DIM), F32)),
        compiler_params=_cp("arbitrary"))(o, proj, o_gain, dy)


def _peer(k):
    x, y, c = lax.axis_index("x"), lax.axis_index("y"), lax.axis_index("c")
    px = 1 - x if k & 4 else x
    py = 1 - y if k & 2 else y
    pc = 1 - c if k & 1 else c
    return (px, py, pc), 4 * px + 2 * py + pc


_HBM = pl.BlockSpec(memory_space=pltpu.HBM)
_SEM = pl.BlockSpec(memory_space=pltpu.SEMAPHORE)
_DATAFLOW = pltpu.SideEffectType.DATAFLOW_SIDE_EFFECTING
N_PEER = N_DEV - 1


def send_start(srcs, name, scatter, after):
    na = len(srcs)
    ns = (2 * N_PEER + 1) * na
    lands = [lax.empty((N_DEV,) + (s.shape[1:] if scatter else s.shape), s.dtype) for s in srcs]
    extra = [] if after is None else [after]

    def body(*refs):
        src_refs, land_refs = refs[:na], refs[na:2 * na]
        sems = refs[2 * na + len(extra):2 * na + len(extra) + ns]
        land_out, token = refs[-1 - na:-1], refs[-1]
        _, me = _peer(0)
        for a in range(na):
            pltpu.make_async_copy(src_refs[a].at[me] if scatter else src_refs[a], land_out[a].at[me],
                                  sems[2 * N_PEER * na + a]).start()
        for k in range(1, N_DEV):
            peer, pid = _peer(k)
            for a in range(na):
                pltpu.make_async_remote_copy(
                    src_ref=src_refs[a].at[pid] if scatter else src_refs[a], dst_ref=land_refs[a].at[me],
                    send_sem=sems[2 * (a * N_PEER + k - 1)], recv_sem=sems[2 * (a * N_PEER + k - 1) + 1],
                    device_id=peer, device_id_type=MESH).start()
        token[...] = jnp.zeros_like(token)

    hbm = lambda arrs: tuple(pltpu.HBM(a.shape, a.dtype) for a in arrs)
    outs = pl.pallas_call(
        body, name=name,
        out_shape=(pltpu.SemaphoreType.DMA(()),) * ns + hbm(srcs) + hbm(lands) + (jax.ShapeDtypeStruct((8, 128), F32),),
        in_specs=[_HBM] * (2 * na) + [pl.BlockSpec(memory_space=pl.ANY)] * len(extra),
        out_specs=(_SEM,) * ns + (_HBM,) * (2 * na) + (pl.BlockSpec(memory_space=pltpu.VMEM),),
        input_output_aliases={i: ns + i for i in range(2 * na)},
        compiler_params=pltpu.CompilerParams(has_side_effects=_DATAFLOW),
    )(*[pltpu.with_memory_space_constraint(a, pltpu.HBM) for a in list(srcs) + lands], *extra)
    return outs[:ns], outs[ns:ns + na], outs[ns + na:ns + 2 * na], outs[-1]


def send_wait(sems, srcs_thru, lands_thru, name, scatter, after):
    na = len(srcs_thru)
    ns = (2 * N_PEER + 1) * na

    def body(*refs):
        src_refs, land_refs, sm = refs[:na], refs[na:2 * na], refs[2 * na:2 * na + ns]
        _, me = _peer(0)
        for a in range(na):
            pltpu.make_async_copy(src_refs[a].at[me] if scatter else src_refs[a], land_refs[a].at[me],
                                  sm[2 * N_PEER * na + a]).wait()
        for k in range(1, N_DEV):
            peer, pid = _peer(k)
            for a in range(na):
                cp = pltpu.make_async_remote_copy(
                    src_ref=src_refs[a].at[pid] if scatter else src_refs[a], dst_ref=land_refs[a].at[pid],
                    send_sem=sm[2 * (a * N_PEER + k - 1)], recv_sem=sm[2 * (a * N_PEER + k - 1) + 1],
                    device_id=peer, device_id_type=MESH)
                cp.wait_send()
                cp.wait_recv()

    hbm = lambda arrs: tuple(pltpu.HBM(a.shape, a.dtype) for a in arrs)
    outs = pl.pallas_call(
        body, name=name, out_shape=hbm(srcs_thru) + hbm(lands_thru),
        in_specs=[_HBM] * (2 * na) + [_SEM] * ns + [pl.BlockSpec(memory_space=pl.ANY)], out_specs=(_HBM,) * (2 * na),
        input_output_aliases={i: i for i in range(2 * na)},
        compiler_params=pltpu.CompilerParams(has_side_effects=_DATAFLOW),
    )(*srcs_thru, *lands_thru, *sems, after)
    return outs[na:]


def _adamw(w, g, m, v):
    m = ADAM_B1 * m + (1.0 - ADAM_B1) * g
    v = ADAM_B2 * v + (1.0 - ADAM_B2) * (g * g)
    m_hat = m / (1.0 - ADAM_B1 ** ADAM_STEP)
    v_hat = v / (1.0 - ADAM_B2 ** ADAM_STEP)
    return -ADAM_LR * (m_hat / (jnp.sqrt(v_hat) + ADAM_EPS) + ADAM_WD * w), m, v


def adam_sum(w, pieces, m, v, name, layer=0, into=None):
    nl, r, c = w.shape
    tr = r
    for cand in (256, 128, 64, 32, 16, 8):
        if r % cand == 0:
            tr = cand
            break

    def body(w_ref, p_ref, m_ref, v_ref, *rest):
        g_ref, d_ref, nm_ref, nv_ref = rest[-4:]
        g = p_ref[0].astype(F32)
        for s in range(1, N_DEV):
            g = g + p_ref[s].astype(F32)
        g_ref[0] = g
        d_ref[0], nm_ref[0], nv_ref[0] = _adamw(w_ref[0], g, m_ref[0], v_ref[0])

    row = pl.BlockSpec((1, tr, c), lambda i: (layer, i, 0))
    out = jax.ShapeDtypeStruct((nl, r, c), F32)
    extra = [] if into is None else list(into)
    return pl.pallas_call(
        body, name=name, grid=(r // tr,),
        in_specs=[row, pl.BlockSpec((N_DEV, tr, c), lambda i: (0, i, 0)), row, row]
        + [pl.BlockSpec(memory_space=pl.ANY)] * len(extra),
        out_specs=(row,) * 4, out_shape=(out,) * 4,
        input_output_aliases={4 + i: i for i in range(len(extra))},
        compiler_params=_cp("parallel"))(w, pieces, m, v, *extra)


def sum_rows(gathered, name):
    _, r, c = gathered.shape

    def body(p_ref, o_ref):
        g = p_ref[0]
        for s in range(1, N_DEV):
            g = g + p_ref[s]
        o_ref[...] = g

    return pl.pallas_call(body, name=name, out_shape=jax.ShapeDtypeStruct((r, c), F32))(gathered)


def adam_small(w, g, m, v, name):
    def body(w_ref, g_ref, m_ref, v_ref, d_ref, nm_ref, nv_ref):
        d_ref[...], nm_ref[...], nv_ref[...] = _adamw(w_ref[...], g_ref[...], m_ref[...], v_ref[...])

    out = jax.ShapeDtypeStruct(w.shape, F32)
    return pl.pallas_call(body, name=name, out_shape=(out,) * 3)(w, g, m, v)


def _rope_tables(t):
    inv_freq = 10000.0 ** (-jnp.arange(0, HEAD_DIM, 2, dtype=F32) / HEAD_DIM)
    ang = jnp.arange(t, dtype=F32)[:, None] * inv_freq[None, :]
    cos, sin = jnp.cos(ang), jnp.sin(ang)
    return jnp.concatenate([cos, cos], axis=-1), jnp.concatenate([sin, sin], axis=-1)


def _lane_row(vec8):
    return jnp.pad(vec8.reshape(1, DN_HEADS), ((0, 0), (DN_HEADS, 128 - 2 * DN_HEADS)))


def _ffn_bwd(x, norm_g, w_gu, w_d, saved, dy, tag, after=None):
    ft, gu, at = saved
    dgu = ffn_dact(dy, w_d, gu, f"{tag}_d_gate_up", after=after)
    dwd = mm_at(at, dy, f"{tag}_dw_down")
    nj = D_FF // GU_TILE
    dwgu = mm_at(ft, dgu, f"{tag}_dw_gate_up", transposed=True, tn=GU_TILE, row_block=lambda q: (q % 2) * nj + q // 2)
    dx, dg = mm_rms_bwd(dgu, w_gu, x, norm_g, dy, f"{tag}_d_norm", chunks=_GU_CHUNKS)
    return dx, dwgu, dwd, dg


def local_step(x, target, small, weights_of, grads_out, after=None):
    t = x.shape[0]
    cosf, sinf = _rope_tables(t)
    alog_row, dtb_row = _lane_row(small["odd_a_log"]), _lane_row(small["odd_dt_bias"])

    h0, h0t = rms_fwd(x, small["even_norm"], "even_norm", after=after)
    we = weights_of("even", h0)
    small = {**small, **we.get("small", {})}
    proj0 = mm_nt(h0, we["w_in"], "even_in_proj")
    qr, kr = qk_prep_fwd(proj0, small["even_q_gain"], small["even_k_gain"], cosf, sinf, "even_qk_prep")
    y_attn, mix0, mix0_t, lse = swa_fwd(qr, kr, proj0, small["even_sinks"], "even_swa")
    mix0, mix0_t = gconv_fwd(proj0, small["even_conv_w"], mix0, mix0_t, "even_gconv")
    we = {**we, **weights_of("even_out", mix0)}
    x1, f0, f0t = mm_nn_res_norm(mix0, we["w_out"], x, small["ffn_norm0"], "even_out_proj")
    w0 = weights_of("ffn0", x1)
    gu0, a0, a0t = ffn_up(f0, w0["gate_up"], "ffn0_gate_up")
    ffn0 = (f0t, gu0, a0t)
    x2, h1, h1t = mm_nn_res_norm(a0, w0["down"], x1, small["odd_norm"], "ffn0_down")

    wo = weights_of("odd", x2)
    proj1 = mm_nt(h1, wo["w_in"], "odd_in_proj")
    qn, kn, vs, bg, conv1 = gdn_prep_fwd(proj1, small["odd_conv_w"], alog_row, dtb_row, "odd_prep")
    o, sall, tall = gdn_fwd(qn, kn, vs, bg, "odd_delta_rule")
    og, ogt = gdn_out_fwd(o, proj1, small["odd_o_gain"], "odd_gate_norm")
    x3, f1, f1t = mm_nn_res_norm(og, wo["w_out"], x2, small["ffn_norm1"], "odd_out_proj")
    w1 = weights_of("ffn1", x3)
    gu1, a1, a1t = ffn_up(f1, w1["gate_up"], "ffn1_gate_up")
    ffn1 = (f1t, gu1, a1t)
    dy, loss_row = mm_nn_res_loss(a1, w1["down"], x3, target, "ffn1_down_loss")

    gs = {}
    dx3, dwgu, dwd, gs["ffn_norm1"] = _ffn_bwd(x3, small["ffn_norm1"], w1["gate_up"], w1["down"], ffn1, dy, "ffn1")
    tok = grads_out("ffn1", {"gate_up": dwgu, "down": dwd})

    dog = mm_nt(dx3, wo["w_out"], "odd_d_gated", after=tok)
    dwo = mm_at(ogt, dx3, "odd_dw_out")
    do, dz, gs["odd_o_gain"] = gdn_out_bwd(o, proj1, small["odd_o_gain"], dog, "odd_d_gate_norm")
    dqn, dkn, dvs, dbg = gdn_bwd(qn, kn, vs, bg, sall, tall, do, "odd_d_delta_rule")
    dproj1, gs["odd_conv_w"], ddt_row, dal_row = gdn_prep_bwd(
        proj1, conv1, small["odd_conv_w"], alog_row, dtb_row, dqn, dkn, dvs, dbg, dz, "odd_d_prep")
    gs["odd_dt_bias"] = ddt_row[:, DN_HEADS:2 * DN_HEADS]
    gs["odd_a_log"] = dal_row[:, DN_HEADS:2 * DN_HEADS]
    dwi = mm_at(h1t, dproj1, "odd_dw_in", transposed=True)
    dx2, gs["odd_norm"] = mm_rms_bwd(dproj1, wo["w_in"], x2, small["odd_norm"], dx3, "odd_d_norm")
    tok = grads_out("odd", {"w_in": dwi, "w_out": dwo})

    dx1, dwgu, dwd, gs["ffn_norm0"] = _ffn_bwd(x1, small["ffn_norm0"], w0["gate_up"], w0["down"], ffn0, dx2, "ffn0",
                                               after=tok)
    tok = grads_out("ffn0", {"gate_up": dwgu, "down": dwd})

    dmix = mm_nt(dx1, we["w_out"], "even_d_mix", after=tok)
    dwo = mm_at(mix0_t, dx1, "even_dw_out")
    dqr, dkr, dv, gs["even_sinks"] = swa_bwd(qr, kr, proj0, small["even_sinks"], y_attn, lse, dmix, "even_d_swa")
    dqk, gs["even_q_gain"], gs["even_k_gain"] = qk_prep_bwd(
        proj0, small["even_q_gain"], small["even_k_gain"], cosf, sinf, dqr, dkr, "even_d_qk_prep")
    dgb, dgc, dxi, gs["even_conv_w"] = gconv_bwd(proj0, small["even_conv_w"], dmix, "even_d_gconv")
    dproj0 = jnp.concatenate([dqk, dv, dgb, dgc, dxi], axis=-1)
    dwi = mm_at(h0t, dproj0, "even_dw_in", transposed=True)
    tok = grads_out("even", {"w_in": dwi, "w_out": dwo})
    grad_x, gs["even_norm"] = mm_rms_bwd(dproj0, we["w_in"], x, small["even_norm"], dx1, "even_d_norm", after=tok)
    return loss_row, grad_x, gs


_SMALL_ORDER = ("even_norm", "even_q_gain", "even_k_gain", "even_sinks", "odd_a_log", "odd_dt_bias", "odd_o_gain",
                "ffn_norm0", "ffn_norm1", "odd_norm", "even_conv_w", "odd_conv_w")
_SMALL_SIZE = {"even_norm": 1024, "even_q_gain": 64, "even_k_gain": 64, "even_sinks": 8, "odd_a_log": 8,
               "odd_dt_bias": 8, "odd_o_gain": 128, "ffn_norm0": 1024, "ffn_norm1": 1024, "odd_norm": 1024,
               "even_conv_w": 3 * 512, "odd_conv_w": 4 * 3072}
_N_REPL = 9


def _pack_rows(vals):
    flat = jnp.concatenate([v.reshape(-1) for v in vals])
    pad = (-flat.shape[0]) % 1024
    return jnp.pad(flat, (0, pad)).reshape(-1, 128)


def _my_block(full, size, axis):
    me = 4 * lax.axis_index("x") + 2 * lax.axis_index("y") + lax.axis_index("c")
    return lax.dynamic_slice_in_dim(full, me * size, size, axis=axis)


def kernel(x, even_norm, even_w_in, even_q_gain, even_k_gain, even_sinks, even_conv_w, even_w_out, odd_norm, odd_w_in, odd_conv_w, odd_a_log, odd_dt_bias, odd_o_gain, odd_w_out, ffn_norm, ffn_w_gate_up, ffn_w_down, loss_target, m_even_norm, m_even_w_in, m_even_q_gain, m_even_k_gain, m_even_sinks, m_even_conv_w, m_even_w_out, m_odd_norm, m_odd_w_in, m_odd_conv_w, m_odd_a_log, m_odd_dt_bias, m_odd_o_gain, m_odd_w_out, m_ffn_norm, m_ffn_w_gate_up, m_ffn_w_down, v_even_norm, v_even_w_in, v_even_q_gain, v_even_k_gain, v_even_sinks, v_even_conv_w, v_even_w_out, v_odd_norm, v_odd_w_in, v_odd_conv_w, v_odd_a_log, v_odd_dt_bias, v_odd_o_gain, v_odd_w_out, v_ffn_norm, v_ffn_w_gate_up, v_ffn_w_down):
    t = x.shape[1]
    d = D_MODEL

    me = 4 * lax.axis_index("x") + 2 * lax.axis_index("y") + lax.axis_index("c")
    tr = lambda a: jnp.swapaxes(a, 1, 2)
    shard = {
        "even": {"w_in": tr(even_w_in)[0], "w_out": even_w_out[0]},
        "ffn0": {"gate_up": tr(ffn_w_gate_up)[0], "down": ffn_w_down[0]},
        "odd": {"w_in": tr(odd_w_in)[0], "w_out": odd_w_out[0]},
        "ffn1": {"gate_up": tr(ffn_w_gate_up)[1], "down": ffn_w_down[1]},
    }
    given = {
        ("even", "w_in"): ("even_w_in", even_w_in, m_even_w_in, v_even_w_in, 0),
        ("even", "w_out"): ("even_w_out", even_w_out, m_even_w_out, v_even_w_out, 0),
        ("odd", "w_in"): ("odd_w_in", odd_w_in, m_odd_w_in, v_odd_w_in, 0),
        ("odd", "w_out"): ("odd_w_out", odd_w_out, m_odd_w_out, v_odd_w_out, 0),
        ("ffn0", "gate_up"): ("ffn_w_gate_up", ffn_w_gate_up, m_ffn_w_gate_up, v_ffn_w_gate_up, 0),
        ("ffn1", "gate_up"): ("ffn_w_gate_up", ffn_w_gate_up, m_ffn_w_gate_up, v_ffn_w_gate_up, 1),
        ("ffn0", "down"): ("ffn_w_down", ffn_w_down, m_ffn_w_down, v_ffn_w_down, 0),
        ("ffn1", "down"): ("ffn_w_down", ffn_w_down, m_ffn_w_down, v_ffn_w_down, 1),
    }

    def whole(group, parts):
        col, row = tuple(shard[group])
        w_col = parts[0].reshape(-1, d)
        if group == "odd":
            w_col = jnp.pad(w_col, ((0, ODD_IN_PAD - ODD_IN_W), (0, 0)))
        return {col: w_col, row: parts[1].reshape(-1, d)}

    wire = {g: [a.astype(MXU_DTYPE) for a in shard[g].values()] for g in shard}
    wire["even_out"] = [wire["even"].pop()]
    wire["even"].append(_pack_rows([odd_norm, even_conv_w, odd_conv_w]))
    gathers, tok = {}, None
    for g in ("even", "even_out", "ffn0", "odd", "ffn1"):
        sems, srcs_thru, lands_thru, tok = send_start(wire[g], f"gather_{g}_start", False, tok)
        gathers[g] = (sems, srcs_thru, lands_thru)
    o1 = d // N_DEV
    o2 = o1 + 3 * CONV_CH // N_DEV

    def weights_of(group, after):
        lands = send_wait(*gathers[group], f"gather_{group}_wait", False, after)
        if group == "even_out":
            return {"w_out": lands[0].reshape(-1, d)}
        if group != "even":
            return whole(group, lands)
        sg = lands[1].reshape(N_DEV, -1)
        return {"w_in": lands[0].reshape(-1, d), "small": {
            "odd_norm": sg[:, :o1].reshape(1, d),
            "even_conv_w": sg[:, o1:o2].reshape(N_DEV, 3, CONV_CH // N_DEV).transpose(1, 0, 2).reshape(3, CONV_CH),
            "odd_conv_w": sg[:, o2:o2 + 4 * _QKV_W // N_DEV].reshape(N_DEV, 4, _QKV_W // N_DEV)
            .transpose(1, 0, 2).reshape(4, _QKV_W)}}

    sent = {}

    def grads_out(group, dws):
        col, row = tuple(shard[group])
        n_cols = N_DEV * shard[group][col].shape[0]
        pieces = [dws[col][:n_cols].reshape((N_DEV,) + shard[group][col].shape),
                  dws[row].reshape((N_DEV,) + shard[group][row].shape)]
        sems, srcs_thru, lands_thru, token = send_start(pieces, f"exchange_{group}_start", True, None)
        sent[group] = (sems, srcs_thru, lands_thru, pieces)
        return token

    small = {
        "even_norm": even_norm, "even_q_gain": even_q_gain, "even_k_gain": even_k_gain, "even_sinks": even_sinks,
        "odd_a_log": odd_a_log.reshape(-1), "odd_dt_bias": odd_dt_bias.reshape(-1), "odd_o_gain": odd_o_gain,
        "ffn_norm0": ffn_norm[0:1], "ffn_norm1": ffn_norm[1:2],
    }

    loss_row, grad_x, gs = local_step(x.reshape(t, d), loss_target.reshape(t, d), small, weights_of, grads_out, after=tok)

    rows = _pack_rows([gs[n] for n in _SMALL_ORDER] + [loss_row[:, 0:1]])
    small_sent = send_start([rows], "gather_small_grads_start", False, None)

    res, behind = {}, small_sent[3]
    for g in ("ffn1", "odd", "ffn0", "even"):
        sems, srcs_thru, lands_thru, pieces = sent[g]
        lands = send_wait(sems, srcs_thru, lands_thru, f"exchange_{g}_wait", True, behind)
        for i, (key, pcs) in enumerate(zip(shard[g], lands)):
            name, w_, m_, v_, layer = given[g, key]
            view = tr if i == 0 else (lambda a: a)
            res[name] = adam_sum(view(w_), pcs, view(m_), view(v_), f"adamw_{g}_{key}", layer=layer, into=res.get(name))
        behind = res[name][0]
    for name in ("even_w_in", "odd_w_in", "ffn_w_gate_up"):
        res[name] = tuple(tr(a) for a in res[name])

    (rows_g,) = send_wait(*small_sent[:3], "gather_small_grads_wait", False, behind)
    tot = sum_rows(rows_g, "sum_small_grads").reshape(-1)
    off, sgrad = 0, {}
    for n in _SMALL_ORDER:
        sgrad[n] = tot[off:off + _SMALL_SIZE[n]]
        off += _SMALL_SIZE[n]
    loss = tot[off]

    repl = _SMALL_ORDER[:_N_REPL]
    repl_w = {"even_norm": even_norm, "even_q_gain": even_q_gain, "even_k_gain": even_k_gain, "even_sinks": even_sinks,
              "odd_a_log": odd_a_log, "odd_dt_bias": odd_dt_bias, "odd_o_gain": odd_o_gain,
              "ffn_norm0": ffn_norm[0], "ffn_norm1": ffn_norm[1]}
    repl_m = {"even_norm": m_even_norm, "even_q_gain": m_even_q_gain, "even_k_gain": m_even_k_gain,
              "even_sinks": m_even_sinks, "odd_a_log": m_odd_a_log, "odd_dt_bias": m_odd_dt_bias,
              "odd_o_gain": m_odd_o_gain, "ffn_norm0": m_ffn_norm[0], "ffn_norm1": m_ffn_norm[1]}
    repl_v = {"even_norm": v_even_norm, "even_q_gain": v_even_q_gain, "even_k_gain": v_even_k_gain,
              "even_sinks": v_even_sinks, "odd_a_log": v_odd_a_log, "odd_dt_bias": v_odd_dt_bias,
              "odd_o_gain": v_odd_o_gain, "ffn_norm0": v_ffn_norm[0], "ffn_norm1": v_ffn_norm[1]}
    pk = lambda dct: _pack_rows([dct[n] for n in repl])
    pd_, pm_, pv_ = adam_small(pk(repl_w), pk(sgrad), pk(repl_m), pk(repl_v), "adamw_replicated")
    sres = {}
    off = 0
    for n in repl:
        sz = _SMALL_SIZE[n]
        sres[n] = (sgrad[n], pd_.reshape(-1)[off:off + sz], pm_.reshape(-1)[off:off + sz], pv_.reshape(-1)[off:off + sz])
        off += sz
    g_on = _my_block(sgrad["odd_norm"].reshape(1, d), d // N_DEV, 1)
    g_ec = _my_block(sgrad["even_conv_w"].reshape(3, CONV_CH), CONV_CH // N_DEV, 1)
    g_oc = _my_block(sgrad["odd_conv_w"].reshape(4, _QKV_W), _QKV_W // N_DEV, 1)
    shard_w = _pack_rows([odd_norm, even_conv_w, odd_conv_w])
    sd_, sm_, sv_ = adam_small(shard_w, _pack_rows([g_on, g_ec, g_oc]),
                               _pack_rows([m_odd_norm, m_even_conv_w, m_odd_conv_w]),
                               _pack_rows([v_odd_norm, v_even_conv_w, v_odd_conv_w]), "adamw_sharded_small")
    off = 0
    for n, gfull, like in (("odd_norm", g_on, odd_norm), ("even_conv_w", g_ec, even_conv_w), ("odd_conv_w", g_oc, odd_conv_w)):
        sz = like.size
        sres[n] = (gfull, sd_.reshape(-1)[off:off + sz], sm_.reshape(-1)[off:off + sz], sv_.reshape(-1)[off:off + sz])
        off += sz

    def small_out(name, like, kind):
        if name == "ffn_norm":
            return jnp.stack([sres["ffn_norm0"][kind], sres["ffn_norm1"][kind]]).reshape(like.shape)
        return sres[name][kind].reshape(like.shape)

    order = (("even_norm", even_norm), ("even_w_in", even_w_in), ("even_q_gain", even_q_gain),
             ("even_k_gain", even_k_gain), ("even_sinks", even_sinks), ("even_conv_w", even_conv_w),
             ("even_w_out", even_w_out), ("odd_norm", odd_norm), ("odd_w_in", odd_w_in), ("odd_conv_w", odd_conv_w),
             ("odd_a_log", odd_a_log), ("odd_dt_bias", odd_dt_bias), ("odd_o_gain", odd_o_gain),
             ("odd_w_out", odd_w_out), ("ffn_norm", ffn_norm), ("ffn_w_gate_up", ffn_w_gate_up),
             ("ffn_w_down", ffn_w_down))
    outs = [loss, grad_x.reshape(x.shape)]
    for kind in range(4):
        for name, like in order:
            outs.append(res[name][kind] if name in res else small_out(name, like, kind))
    return tuple(outs)
```

```python
import jax
import jax.numpy as jnp
import numpy as np
from jax import lax
from jax.experimental import pallas as pl
from jax.experimental.pallas import tpu as pltpu

F32 = jnp.float32
MXU_DTYPE = jnp.bfloat16
HI = lax.Precision.HIGH
EPS = 1e-6
N_DEV = 8
D_MODEL = 1024
HEAD_DIM = 64
ATTN_HEADS = 8
KV_HEADS = 2
ATTN_BLOCK = 128
Q_W = 512
KV_W = 128
CONV_CH = 512
EVEN_IN_W = 2304
DN_HEADS = 8
DN_DIM = 128
DN_W = 1024
DN_CHUNK = 64
ODD_IN_W = 4112
ODD_IN_PAD = 4224
D_FF = 2816
NEG = -1e30
VMEM_LIMIT = 56 * 1024 * 1024
ADAM_LR, ADAM_B1, ADAM_B2, ADAM_EPS, ADAM_WD, ADAM_STEP = 0.001, 0.9, 0.999, 1e-08, 0.01, 10
MESH = pl.DeviceIdType.MESH


def _cp(*sem):
    return pltpu.CompilerParams(dimension_semantics=sem, vmem_limit_bytes=VMEM_LIMIT)


def _pick(n, cap):
    best = 128
    for t in range(128, cap + 1, 128):
        if n % t == 0:
            best = t
    return best


def _mx(a, b):
    return jnp.dot(a.astype(MXU_DTYPE), b.astype(MXU_DTYPE), preferred_element_type=F32)


def _mx_nt(a, b):
    return lax.dot_general(a.astype(MXU_DTYPE), b.astype(MXU_DTYPE), (((1,), (1,)), ((), ())),
                           preferred_element_type=F32)


def _mx_tn(a, b):
    return lax.dot_general(a.astype(MXU_DTYPE), b.astype(MXU_DTYPE), (((0,), (0,)), ((), ())),
                           preferred_element_type=F32)


def _hi(a, b):
    return jnp.dot(a, b, precision=HI, preferred_element_type=F32)


def _hi_nt(a, b):
    return lax.dot_general(a, b, (((1,), (1,)), ((), ())), precision=HI, preferred_element_type=F32)


def _hi_tn(a, b):
    return lax.dot_general(a, b, (((0,), (0,)), ((), ())), precision=HI, preferred_element_type=F32)


def _sigmoid(x):
    return 0.5 * jnp.tanh(0.5 * x) + 0.5


def _softplus(x):
    return jnp.maximum(x, 0.0) + jnp.log(1.0 + jnp.exp(-jnp.abs(x)))


def mm_nn_res_norm(a, b, res, g, name, tm=512):
    t, k = a.shape
    d = b.shape[1]
    tm = min(tm, t)

    def body(a_ref, b_ref, res_ref, g_ref, y_ref, h_ref, ht_ref):
        y = res_ref[...] + _mx(a_ref[...], b_ref[...])
        y_ref[...] = y
        h = y * lax.rsqrt(jnp.mean(y * y, axis=-1, keepdims=True) + EPS) * g_ref[...]
        h_ref[...] = h.astype(h_ref.dtype)
        ht_ref[...] = h.T.astype(ht_ref.dtype)

    row = pl.BlockSpec((tm, d), lambda i: (i, 0))
    return pl.pallas_call(
        body, name=name, grid=(t // tm,),
        in_specs=[pl.BlockSpec((tm, k), lambda i: (i, 0)), pl.BlockSpec((k, d), lambda i: (0, 0)), row,
                  pl.BlockSpec((1, d), lambda i: (0, 0))],
        out_specs=(row, row, pl.BlockSpec((d, tm), lambda i: (0, i))),
        out_shape=(jax.ShapeDtypeStruct((t, d), F32), jax.ShapeDtypeStruct((t, d), MXU_DTYPE),
                   jax.ShapeDtypeStruct((d, t), MXU_DTYPE)),
        compiler_params=_cp("parallel"))(a, b, res, g)


def mm_nt(a, b, name, out_dtype=F32, tm=1024, after=None):
    m, k = a.shape
    n, _ = b.shape
    tn = _pick(n, 512 if k > 3000 else 1536)
    tm = min(tm, m)

    def body(a_ref, b_ref, *rest):
        o_ref = rest[-1]
        o_ref[...] = _mx_nt(a_ref[...], b_ref[...]).astype(o_ref.dtype)

    in_specs = [pl.BlockSpec((tm, k), lambda j, i: (i, 0)), pl.BlockSpec((tn, k), lambda j, i: (j, 0))]
    args = [a, b]
    if after is not None:
        in_specs.append(pl.BlockSpec(memory_space=pl.ANY))
        args.append(after)
    return pl.pallas_call(
        body, name=name, grid=(n // tn, m // tm), in_specs=in_specs,
        out_specs=pl.BlockSpec((tm, tn), lambda j, i: (i, j)),
        out_shape=jax.ShapeDtypeStruct((m, n), out_dtype), compiler_params=_cp("parallel", "parallel"))(*args)


def mm_at(at, b, name, tk=2048, transposed=False, tn=None, row_block=None):
    m, kk = at.shape
    _, n = b.shape
    tm, tn, tk = _pick(m, 1408), tn or _pick(n, 2816), min(tk, kk)
    nk = kk // tk

    def body(a_ref, b_ref, o_ref, acc_ref):
        k = pl.program_id(2)
        p = _mx(a_ref[...], b_ref[...])
        acc = jnp.where(k == 0, p, acc_ref[...] + p)
        acc_ref[...] = acc

        @pl.when(k == nk - 1)
        def _():
            o_ref[...] = (acc.T if transposed else acc).astype(o_ref.dtype)

    if transposed:
        rb = row_block or (lambda j: j)
        out_spec = pl.BlockSpec((tn, tm), lambda i, j, k: (rb(j), i))
        out_shape = jax.ShapeDtypeStruct((n, m), MXU_DTYPE)
    else:
        out_spec = pl.BlockSpec((tm, tn), lambda i, j, k: (i, j))
        out_shape = jax.ShapeDtypeStruct((m, n), MXU_DTYPE)
    return pl.pallas_call(
        body, name=name, grid=(m // tm, n // tn, nk),
        in_specs=[pl.BlockSpec((tm, tk), lambda i, j, k: (i, k)), pl.BlockSpec((tk, tn), lambda i, j, k: (k, j))],
        out_specs=out_spec, out_shape=out_shape, scratch_shapes=[pltpu.VMEM((tm, tn), F32)],
        compiler_params=_cp("parallel", "parallel", "arbitrary"))(at, b)


def rms_fwd(x, g, name, tm=512, after=None):
    t, d = x.shape

    def body(x_ref, g_ref, *rest):
        o_ref, ot_ref = rest[-2:]
        xv = x_ref[...]
        r = lax.rsqrt(jnp.mean(xv * xv, axis=-1, keepdims=True) + EPS)
        h = xv * r * g_ref[...]
        o_ref[...] = h.astype(o_ref.dtype)
        ot_ref[...] = h.T.astype(ot_ref.dtype)

    in_specs = [pl.BlockSpec((tm, d), lambda i: (i, 0)), pl.BlockSpec((1, d), lambda i: (0, 0))]
    args = [x, g]
    if after is not None:
        in_specs.append(pl.BlockSpec(memory_space=pl.ANY))
        args.append(after)
    return pl.pallas_call(
        body, name=name, grid=(t // tm,), in_specs=in_specs,
        out_specs=(pl.BlockSpec((tm, d), lambda i: (i, 0)), pl.BlockSpec((d, tm), lambda i: (0, i))),
        out_shape=(jax.ShapeDtypeStruct((t, d), MXU_DTYPE), jax.ShapeDtypeStruct((d, t), MXU_DTYPE)),
        compiler_params=_cp("parallel"))(*args)


def mm_rms_bwd(a, bt, x, g, dres, name, tm=512, after=None, chunks=None):
    t, k = a.shape
    d = bt.shape[1]
    tm = min(tm, t)
    chunks = chunks or ((0, 0, k),)

    def body(a_ref, b_ref, x_ref, g_ref, dres_ref, *rest):
        dx_ref, dg_ref = rest[-2:]
        dhv = None
        for ca, cb, size in chunks:
            part = _mx(a_ref[:, ca:ca + size], b_ref[cb:cb + size, :])
            dhv = part if dhv is None else dhv + part
        xv = x_ref[...]
        r = lax.rsqrt(jnp.mean(xv * xv, axis=-1, keepdims=True) + EPS)
        xh = xv * r
        dxh = dhv * g_ref[...]
        dx_ref[...] = dres_ref[...] + r * (dxh - xh * jnp.mean(dxh * xh, axis=-1, keepdims=True))
        part = jnp.sum(dhv * xh, axis=0, keepdims=True)
        dg_ref[...] = jnp.where(pl.program_id(0) == 0, part, dg_ref[...] + part)

    row = pl.BlockSpec((tm, d), lambda i: (i, 0))
    one = pl.BlockSpec((1, d), lambda i: (0, 0))
    in_specs = [pl.BlockSpec((tm, k), lambda i: (i, 0)), pl.BlockSpec((k, d), lambda i: (0, 0)), row, one, row]
    args = [a, bt, x, g, dres]
    if after is not None:
        in_specs.append(pl.BlockSpec(memory_space=pl.ANY))
        args.append(after)
    return pl.pallas_call(
        body, name=name, grid=(t // tm,), in_specs=in_specs, out_specs=(row, one),
        out_shape=(jax.ShapeDtypeStruct((t, d), F32), jax.ShapeDtypeStruct((1, d), F32)),
        compiler_params=_cp("arbitrary"))(*args)


GU_TILE = 1408


def ffn_up(f, wt, name, tm=1024):
    t, d = f.shape
    tm = min(tm, t)
    nj = D_FF // GU_TILE

    def body(f_ref, wg_ref, wu_ref, gu_ref, a_ref, at_ref):
        g = _mx_nt(f_ref[...], wg_ref[...])
        u = _mx_nt(f_ref[...], wu_ref[...])
        sg = _sigmoid(g)
        gs = g * sg
        gu_ref[:, :GU_TILE] = (u * (sg + gs - gs * sg)).astype(gu_ref.dtype)
        gu_ref[:, GU_TILE:] = gs.astype(gu_ref.dtype)
        act = gs * u
        a_ref[...] = act.astype(a_ref.dtype)
        at_ref[...] = act.T.astype(at_ref.dtype)

    return pl.pallas_call(
        body, name=name, grid=(nj, t // tm),
        in_specs=[pl.BlockSpec((tm, d), lambda j, i: (i, 0)), pl.BlockSpec((GU_TILE, d), lambda j, i: (j, 0)),
                  pl.BlockSpec((GU_TILE, d), lambda j, i: (nj + j, 0))],
        out_specs=(pl.BlockSpec((tm, 2 * GU_TILE), lambda j, i: (i, j)), pl.BlockSpec((tm, GU_TILE), lambda j, i: (i, j)),
                   pl.BlockSpec((GU_TILE, tm), lambda j, i: (j, i))),
        out_shape=(jax.ShapeDtypeStruct((t, 2 * D_FF), MXU_DTYPE), jax.ShapeDtypeStruct((t, D_FF), MXU_DTYPE),
                   jax.ShapeDtypeStruct((D_FF, t), MXU_DTYPE)),
        compiler_params=_cp("parallel", "parallel"))(f, wt, wt)


_GU_CHUNKS = tuple((q * GU_TILE, ((q % 2) * (D_FF // GU_TILE) + q // 2) * GU_TILE, GU_TILE)
                   for q in range(2 * D_FF // GU_TILE))


def ffn_dact(dy, w_d, gu, name, tm=1024, after=None):
    t, d = dy.shape
    tm = min(tm, t)

    def body(dy_ref, w_ref, gu_ref, *rest):
        o_ref = rest[-1]
        da = _mx_nt(dy_ref[...], w_ref[...])
        o_ref[:, :GU_TILE] = (da * gu_ref[:, :GU_TILE]).astype(o_ref.dtype)
        o_ref[:, GU_TILE:] = (da * gu_ref[:, GU_TILE:]).astype(o_ref.dtype)

    in_specs = [pl.BlockSpec((tm, d), lambda j, i: (i, 0)), pl.BlockSpec((GU_TILE, d), lambda j, i: (j, 0)),
                pl.BlockSpec((tm, 2 * GU_TILE), lambda j, i: (i, j))]
    args = [dy, w_d, gu]
    if after is not None:
        in_specs.append(pl.BlockSpec(memory_space=pl.ANY))
        args.append(after)
    return pl.pallas_call(
        body, name=name, grid=(D_FF // GU_TILE, t // tm), in_specs=in_specs,
        out_specs=pl.BlockSpec((tm, 2 * GU_TILE), lambda j, i: (i, j)),
        out_shape=jax.ShapeDtypeStruct((t, 2 * D_FF), MXU_DTYPE), compiler_params=_cp("parallel", "parallel"))(*args)


def mm_nn_res_loss(a, b, res, target, name, tm=512):
    t, k = a.shape
    d = b.shape[1]
    tm = min(tm, t)

    def body(a_ref, b_ref, res_ref, t_ref, dy_ref, l_ref):
        e = res_ref[...] + _mx(a_ref[...], b_ref[...]) - t_ref[...]
        dy_ref[...] = e * (1.0 / d)
        part = jnp.zeros((1, 128), F32) + 0.5 * jnp.sum(jnp.mean(e * e, axis=-1, keepdims=True), axis=0, keepdims=True)
        l_ref[...] = jnp.where(pl.program_id(0) == 0, part, l_ref[...] + part)

    row = pl.BlockSpec((tm, d), lambda i: (i, 0))
    return pl.pallas_call(
        body, name=name, grid=(t // tm,),
        in_specs=[pl.BlockSpec((tm, k), lambda i: (i, 0)), pl.BlockSpec((k, d), lambda i: (0, 0)), row, row],
        out_specs=(row, pl.BlockSpec((1, 128), lambda i: (0, 0))),
        out_shape=(jax.ShapeDtypeStruct((t, d), F32), jax.ShapeDtypeStruct((1, 128), F32)),
        compiler_params=_cp("arbitrary"))(a, b, res, target)


QK_W = Q_W + KV_W
_QK_TILE = 256


def _qk_mats():
    idx = np.arange(_QK_TILE)
    half = HEAD_DIM // 2
    same = (idx[:, None] // HEAD_DIM) == (idx[None, :] // HEAD_DIM)
    lo = (idx % HEAD_DIM) < half
    rot = np.where((idx[:, None] == idx[None, :] + half) & lo[None, :], -1.0, 0.0)
    rot = rot + np.where((idx[:, None] == idx[None, :] - half) & ~lo[None, :], 1.0, 0.0)
    return jnp.asarray(same, F32), jnp.asarray(rot, F32)


def _qk_rows(q_gain, k_gain, cosf, sinf):
    gain = jnp.concatenate([q_gain] * ATTN_HEADS + [k_gain] * KV_HEADS, axis=-1)
    return gain, jnp.concatenate([cosf, cosf], axis=-1), jnp.concatenate([sinf, sinf], axis=-1)


def _qk_tiles(a, mat, transposed=False):
    outs = []
    for c0 in range(0, QK_W, _QK_TILE):
        w = min(_QK_TILE, QK_W - c0)
        mt = (mat.T if transposed else mat)[:w, :w].astype(MXU_DTYPE)
        at = a[:, c0:c0 + w]
        hi = at.astype(MXU_DTYPE)
        lo = (at - hi.astype(F32)).astype(MXU_DTYPE)
        outs.append(jnp.dot(hi, mt, preferred_element_type=F32) + jnp.dot(lo, mt, preferred_element_type=F32))
    return jnp.concatenate(outs, axis=-1)


def qk_prep_fwd(proj, q_gain, k_gain, cosf, sinf, name, tm=256):
    t = proj.shape[0]
    gmat, rmat = _qk_mats()
    gain, c2, s2 = _qk_rows(q_gain, k_gain, cosf, sinf)
    rep = QK_W // 128

    def body(p_ref, g_ref, c_ref, s_ref, gm_ref, rm_ref, q_ref, k_ref):
        x = p_ref[...]
        r = lax.rsqrt(_qk_tiles(x * x, gm_ref[...]) * (1.0 / HEAD_DIM) + EPS)
        xn = x * r * g_ref[...]
        c = jnp.concatenate([c_ref[...]] * rep, axis=-1)
        s = jnp.concatenate([s_ref[...]] * rep, axis=-1)
        out = xn * c + _qk_tiles(xn, rm_ref[...]) * s
        q_ref[...] = out[:, :Q_W]
        k_ref[...] = out[:, Q_W:]

    full = pl.BlockSpec((_QK_TILE, _QK_TILE), lambda i: (0, 0))
    tab = pl.BlockSpec((tm, 128), lambda i: (i, 0))
    return pl.pallas_call(
        body, name=name, grid=(t // tm,),
        in_specs=[pl.BlockSpec((tm, QK_W), lambda i: (i, 0)), pl.BlockSpec((1, QK_W), lambda i: (0, 0)), tab, tab,
                  full, full],
        out_specs=(pl.BlockSpec((tm, Q_W), lambda i: (i, 0)), pl.BlockSpec((tm, KV_W), lambda i: (i, 0))),
        out_shape=(jax.ShapeDtypeStruct((t, Q_W), F32), jax.ShapeDtypeStruct((t, KV_W), F32)),
        compiler_params=_cp("parallel"))(proj, gain, c2, s2, gmat, rmat)


def qk_prep_bwd(proj, q_gain, k_gain, cosf, sinf, dq, dk, name, tm=256):
    t = proj.shape[0]
    gmat, rmat = _qk_mats()
    gain, c2, s2 = _qk_rows(q_gain, k_gain, cosf, sinf)
    rep = QK_W // 128
    lanes = np.arange(QK_W)[:, None]
    fold = jnp.asarray(lanes % HEAD_DIM + np.where(lanes >= Q_W, HEAD_DIM, 0) == np.arange(128)[None, :], F32)

    def body(p_ref, g_ref, c_ref, s_ref, gm_ref, rm_ref, f_ref, dq_ref, dk_ref, o_ref, dg_ref):
        x = p_ref[...]
        r = lax.rsqrt(_qk_tiles(x * x, gm_ref[...]) * (1.0 / HEAD_DIM) + EPS)
        xh = x * r
        c = jnp.concatenate([c_ref[...]] * rep, axis=-1)
        s = jnp.concatenate([s_ref[...]] * rep, axis=-1)
        dout = jnp.concatenate([dq_ref[...], dk_ref[...]], axis=-1)
        dxn = dout * c + _qk_tiles(dout * s, rm_ref[...], transposed=True)
        part = _hi(jnp.sum(dxn * xh, axis=0, keepdims=True), f_ref[...])
        dxh = dxn * g_ref[...]
        mean = _qk_tiles(dxh * xh, gm_ref[...]) * (1.0 / HEAD_DIM)
        o_ref[...] = (r * (dxh - xh * mean)).astype(o_ref.dtype)
        dg_ref[...] = jnp.where(pl.program_id(0) == 0, part, dg_ref[...] + part)

    full = pl.BlockSpec((_QK_TILE, _QK_TILE), lambda i: (0, 0))
    tab = pl.BlockSpec((tm, 128), lambda i: (i, 0))
    dqk, dg = pl.pallas_call(
        body, name=name, grid=(t // tm,),
        in_specs=[pl.BlockSpec((tm, QK_W), lambda i: (i, 0)), pl.BlockSpec((1, QK_W), lambda i: (0, 0)), tab, tab,
                  full, full, pl.BlockSpec((QK_W, 128), lambda i: (0, 0)),
                  pl.BlockSpec((tm, Q_W), lambda i: (i, 0)), pl.BlockSpec((tm, KV_W), lambda i: (i, 0))],
        out_specs=(pl.BlockSpec((tm, QK_W), lambda i: (i, 0)), pl.BlockSpec((1, 128), lambda i: (0, 0))),
        out_shape=(jax.ShapeDtypeStruct((t, QK_W), MXU_DTYPE), jax.ShapeDtypeStruct((1, 128), F32)),
        compiler_params=_cp("arbitrary"))(proj, gain, c2, s2, gmat, rmat, fold, dq, dk)
    return dqk, dg[:, :HEAD_DIM], dg[:, HEAD_DIM:]


def _swa_valid(n, grp):
    qi = lax.broadcasted_iota(jnp.int32, (grp * ATTN_BLOCK, 2 * ATTN_BLOCK), 0) & (ATTN_BLOCK - 1)
    kj = lax.broadcasted_iota(jnp.int32, (grp * ATTN_BLOCK, 2 * ATTN_BLOCK), 1)
    diff = qi + ATTN_BLOCK - kj
    return (diff >= 0) & (diff < ATTN_BLOCK) & (n * ATTN_BLOCK - ATTN_BLOCK + kj >= 0)


def _stack_heads(ref, g, grp, rows=slice(None)):
    return jnp.concatenate([ref[rows, (g * grp + j) * HEAD_DIM:(g * grp + j + 1) * HEAD_DIM] for j in range(grp)], axis=0)


def _stack_sinks(s_ref, g, grp):
    return jnp.concatenate([jnp.zeros((ATTN_BLOCK, 1), F32) + s_ref[0:1, g * grp + j:g * grp + j + 1]
                            for j in range(grp)], axis=0)


SWA_STEP = 2


def swa_fwd(q, k, proj, sinks, name):
    t = q.shape[0]
    nb = t // ATTN_BLOCK
    scale = HEAD_DIM ** -0.5
    grp = ATTN_HEADS // KV_HEADS

    rows = SWA_STEP * ATTN_BLOCK

    def body(q_ref, kc_ref, kp_ref, vc_ref, vp_ref, s_ref, y_ref, mix_ref, yt_ref, lse_ref):
        n0 = pl.program_id(0) * SWA_STEP
        kk = jnp.concatenate([kp_ref[...], kc_ref[...]], axis=0).astype(MXU_DTYPE)
        vv = jnp.concatenate([vp_ref[...], vc_ref[...]], axis=0).astype(MXU_DTYPE)
        lane = lax.broadcasted_iota(jnp.int32, (ATTN_BLOCK, ATTN_HEADS), 1)
        units = [(b, g) for b in range(SWA_STEP) for g in range(KV_HEADS)]
        blk = lambda b: slice(b * ATTN_BLOCK, (b + 1) * ATTN_BLOCK)
        keys = lambda b: slice(b * ATTN_BLOCK, (b + 2) * ATTN_BLOCK)
        col = lambda g: slice(g * HEAD_DIM, (g + 1) * HEAD_DIM)
        valid = [_swa_valid(n0 + b, grp) for b in range(SWA_STEP)]
        qg = [_stack_heads(q_ref, g, grp, blk(b)) for b, g in units]
        sink = [_stack_sinks(s_ref, g, grp) for b, g in units]
        sc = [jnp.where(valid[b], _mx_nt(qg[u], kk[keys(b), col(g)]) * scale, NEG) for u, (b, g) in enumerate(units)]
        m = [jnp.maximum(jnp.max(sc_, axis=-1, keepdims=True), sk) for sc_, sk in zip(sc, sink)]
        e = [jnp.exp(sc_ - m_) for sc_, m_ in zip(sc, m)]
        den = [jnp.sum(e_, axis=-1, keepdims=True) + jnp.exp(sk - m_) for e_, sk, m_ in zip(e, sink, m)]
        og = [_mx(e[u] / den[u], vv[keys(b), col(g)]) for u, (b, g) in enumerate(units)]
        lg = [m_ + jnp.log(d_) for m_, d_ in zip(m, den)]
        for b in range(SWA_STEP):
            lse = jnp.zeros((ATTN_BLOCK, ATTN_HEADS), F32)
            outs = []
            for h in range(ATTN_HEADS):
                u = b * KV_HEADS + h // grp
                sub = blk(h % grp)
                outs.append(og[u][sub])
                lse = jnp.where(lane == h, lg[u][sub], lse)
            y = jnp.concatenate(outs, axis=-1)
            y_ref[blk(b), :] = y
            mix_ref[blk(b), :] = y.astype(mix_ref.dtype)
            yt_ref[:, blk(b)] = y.T.astype(yt_ref.dtype)
            lse_ref[blk(b), :] = lse

    cur = lambda n: (n, 0)
    prev = lambda n: (jnp.maximum(n * SWA_STEP - 1, 0), 0)
    vcol = (Q_W + KV_W) // KV_W
    return pl.pallas_call(
        body, name=name, grid=(nb // SWA_STEP,),
        in_specs=[pl.BlockSpec((rows, Q_W), cur), pl.BlockSpec((rows, KV_W), cur),
                  pl.BlockSpec((ATTN_BLOCK, KV_W), prev),
                  pl.BlockSpec((rows, KV_W), lambda n: (n, vcol)),
                  pl.BlockSpec((ATTN_BLOCK, KV_W), lambda n: (jnp.maximum(n * SWA_STEP - 1, 0), vcol)),
                  pl.BlockSpec((1, ATTN_HEADS), lambda n: (0, 0))],
        out_specs=(pl.BlockSpec((rows, Q_W), cur), pl.BlockSpec((rows, Q_W), cur),
                   pl.BlockSpec((Q_W, rows), lambda n: (0, n)), pl.BlockSpec((rows, ATTN_HEADS), cur)),
        out_shape=(jax.ShapeDtypeStruct((t, Q_W), F32), jax.ShapeDtypeStruct((t, Q_W + CONV_CH), MXU_DTYPE),
                   jax.ShapeDtypeStruct((Q_W + CONV_CH, t), MXU_DTYPE), jax.ShapeDtypeStruct((t, ATTN_HEADS), F32)),
        compiler_params=_cp("parallel"))(q, k, k, proj, proj, sinks)


def swa_bwd(q, k, proj, sinks, y, lse, dmix, name):
    t = q.shape[0]
    nb = t // ATTN_BLOCK
    scale = HEAD_DIM ** -0.5
    grp = ATTN_HEADS // KV_HEADS

    def body(q_ref, kc_ref, kp_ref, vc_ref, vp_ref, s_ref, y_ref, lse_ref, dy_ref,
             dq_ref, dk_ref, dv_ref, ds_ref, dkc, dvc):
        n = pl.program_id(0)

        @pl.when(n == 0)
        def _():
            dkc[...] = jnp.zeros_like(dkc)
            dvc[...] = jnp.zeros_like(dvc)
            ds_ref[...] = jnp.zeros_like(ds_ref)

        @pl.when(n < nb)
        def _():
            valid = _swa_valid(n, grp)
            kk = jnp.concatenate([kp_ref[...], kc_ref[...]], axis=0).astype(MXU_DTYPE)
            vv = jnp.concatenate([vp_ref[...], vc_ref[...]], axis=0).astype(MXU_DTYPE)
            lane = lax.broadcasted_iota(jnp.int32, (1, ATTN_HEADS), 1)
            gs = range(KV_HEADS)
            kg = [kk[:, g * HEAD_DIM:(g + 1) * HEAD_DIM] for g in gs]
            vg = [vv[:, g * HEAD_DIM:(g + 1) * HEAD_DIM] for g in gs]
            qg = [_stack_heads(q_ref, g, grp).astype(MXU_DTYPE) for g in gs]
            dog = [_stack_heads(dy_ref, g, grp) for g in gs]
            og = [_stack_heads(y_ref, g, grp) for g in gs]
            lg = [jnp.concatenate([lse_ref[:, g * grp + j:g * grp + j + 1] for j in range(grp)], axis=0) for g in gs]
            sink = [_stack_sinks(s_ref, g, grp) for g in gs]
            sc = [jnp.where(valid, _mx_nt(qg[g], kg[g]) * scale, NEG) for g in gs]
            p = [jnp.exp(sc[g] - lg[g]) for g in gs]
            delta = [jnp.sum(dog[g] * og[g], axis=-1, keepdims=True) for g in gs]
            ds = [p[g] * (_mx_nt(dog[g], vg[g]) - delta[g]) for g in gs]
            dqg = [_mx(ds[g], kg[g]) * scale for g in gs]
            dkf = jnp.concatenate([_mx_tn(ds[g], qg[g]) * scale for g in gs], axis=-1)
            dvf = jnp.concatenate([_mx_tn(p[g], dog[g]) for g in gs], axis=-1)
            dsk = [jnp.exp(sink[g] - lg[g]) * delta[g] for g in gs]
            dsink = jnp.zeros((1, ATTN_HEADS), F32)
            dqs = []
            for h in range(ATTN_HEADS):
                rows = slice((h % grp) * ATTN_BLOCK, (h % grp + 1) * ATTN_BLOCK)
                dqs.append(dqg[h // grp][rows])
                dsink = jnp.where(lane == h, -jnp.sum(dsk[h // grp][rows], axis=0, keepdims=True), dsink)
            dq_ref[...] = jnp.concatenate(dqs, axis=-1)
            dk_ref[...] = dkc[...] + dkf[:ATTN_BLOCK]
            dv_ref[...] = (dvc[...] + dvf[:ATTN_BLOCK]).astype(dv_ref.dtype)
            dkc[...] = dkf[ATTN_BLOCK:]
            dvc[...] = dvf[ATTN_BLOCK:]
            ds_ref[...] += dsink

        @pl.when(n == nb)
        def _():
            dk_ref[...] = dkc[...]
            dv_ref[...] = dvc[...].astype(dv_ref.dtype)

    cur = lambda n: (jnp.minimum(n, nb - 1), 0)
    prev = lambda n: (jnp.clip(n - 1, 0, nb - 1), 0)
    vcol = (Q_W + KV_W) // KV_W
    return pl.pallas_call(
        body, name=name, grid=(nb + 1,),
        in_specs=[pl.BlockSpec((ATTN_BLOCK, Q_W), cur), pl.BlockSpec((ATTN_BLOCK, KV_W), cur),
                  pl.BlockSpec((ATTN_BLOCK, KV_W), prev),
                  pl.BlockSpec((ATTN_BLOCK, KV_W), lambda n: (jnp.minimum(n, nb - 1), vcol)),
                  pl.BlockSpec((ATTN_BLOCK, KV_W), lambda n: (jnp.clip(n - 1, 0, nb - 1), vcol)),
                  pl.BlockSpec((1, ATTN_HEADS), lambda n: (0, 0)),
                  pl.BlockSpec((ATTN_BLOCK, Q_W), cur), pl.BlockSpec((ATTN_BLOCK, ATTN_HEADS), cur),
                  pl.BlockSpec((ATTN_BLOCK, Q_W), cur)],
        out_specs=(pl.BlockSpec((ATTN_BLOCK, Q_W), cur), pl.BlockSpec((ATTN_BLOCK, KV_W), prev),
                   pl.BlockSpec((ATTN_BLOCK, KV_W), prev), pl.BlockSpec((1, ATTN_HEADS), lambda n: (0, 0))),
        out_shape=(jax.ShapeDtypeStruct((t, Q_W), F32), jax.ShapeDtypeStruct((t, KV_W), F32),
                   jax.ShapeDtypeStruct((t, KV_W), MXU_DTYPE), jax.ShapeDtypeStruct((1, ATTN_HEADS), F32)),
        scratch_shapes=[pltpu.VMEM((ATTN_BLOCK, KV_W), F32), pltpu.VMEM((ATTN_BLOCK, KV_W), F32)],
        compiler_params=_cp("arbitrary"))(q, k, k, proj, proj, sinks, y, lse, dmix)


GC_W = 256
_GB0, _GC0, _XI0 = 768 // GC_W, 1280 // GC_W, 1792 // GC_W
HALO = 8


def gconv_fwd(proj, conv_w, mix, mix_t, name, tm=512):
    t = proj.shape[0]
    hb = tm // HALO
    half = Q_W // GC_W

    def body(gb_ref, gc_ref, xi_ref, gch_ref, xih_ref, w_ref, mix_in, mixt_in, y_ref, yt_ref):
        i = pl.program_id(1)
        u = gc_ref[...] * xi_ref[...]
        uh = jnp.where(i == 0, 0.0, gch_ref[...] * xih_ref[...])
        up = jnp.concatenate([uh, u], axis=0)
        cv = w_ref[0:1, :] * up[HALO - 2:HALO - 2 + tm]
        cv = cv + w_ref[1:2, :] * up[HALO - 1:HALO - 1 + tm]
        cv = cv + w_ref[2:3, :] * u
        y = gb_ref[...] * cv
        y_ref[...] = y.astype(y_ref.dtype)
        yt_ref[...] = y.T.astype(yt_ref.dtype)

    def col(c0):
        return pl.BlockSpec((tm, GC_W), lambda cj, i: (i, c0 + cj))

    def halo(c0):
        return pl.BlockSpec((HALO, GC_W), lambda cj, i: (jnp.maximum(i * hb - 1, 0), c0 + cj))

    return pl.pallas_call(
        body, name=name, grid=(CONV_CH // GC_W, t // tm),
        in_specs=[col(_GB0), col(_GC0), col(_XI0), halo(_GC0), halo(_XI0),
                  pl.BlockSpec((3, GC_W), lambda cj, i: (0, cj)),
                  pl.BlockSpec(memory_space=pl.ANY), pl.BlockSpec(memory_space=pl.ANY)],
        out_specs=(pl.BlockSpec((tm, GC_W), lambda cj, i: (i, half + cj)),
                   pl.BlockSpec((GC_W, tm), lambda cj, i: (half + cj, i))),
        out_shape=(jax.ShapeDtypeStruct(mix.shape, mix.dtype), jax.ShapeDtypeStruct(mix_t.shape, mix_t.dtype)),
        input_output_aliases={6: 0, 7: 1},
        compiler_params=_cp("parallel", "parallel"))(proj, proj, proj, proj, proj, conv_w, mix, mix_t)


def gconv_bwd(proj, conv_w, dmix, name, tm=512):
    t = proj.shape[0]
    hb = tm // HALO
    nt = t // tm
    dy0 = Q_W // GC_W

    def body(gb_ref, gc_ref, xi_ref, gch_ref, xih_ref, gbn_ref, dyn_ref, dy_ref, w_ref,
             dgb_ref, dgc_ref, dxi_ref, dw_ref):
        i = pl.program_id(1)
        gc, xi, gb, dy = gc_ref[...], xi_ref[...], gb_ref[...], dy_ref[...]
        u = gc * xi
        uh = jnp.where(i == 0, 0.0, gch_ref[...] * xih_ref[...])
        up = jnp.concatenate([uh, u], axis=0)
        u2 = up[HALO - 2:HALO - 2 + tm]
        u1 = up[HALO - 1:HALO - 1 + tm]
        cv = w_ref[0:1, :] * u2 + w_ref[1:2, :] * u1 + w_ref[2:3, :] * u
        dgb_ref[...] = (dy * cv).astype(dgb_ref.dtype)
        dcv = dy * gb
        dcvn = jnp.where(i == nt - 1, 0.0, dyn_ref[...] * gbn_ref[...])
        dcvp = jnp.concatenate([dcv, dcvn], axis=0)
        du = w_ref[0:1, :] * dcvp[2:2 + tm] + w_ref[1:2, :] * dcvp[1:1 + tm] + w_ref[2:3, :] * dcv
        dgc_ref[...] = (du * xi).astype(dgc_ref.dtype)
        dxi_ref[...] = (du * gc).astype(dxi_ref.dtype)
        dw = jnp.concatenate([jnp.sum(dcv * u2, axis=0, keepdims=True), jnp.sum(dcv * u1, axis=0, keepdims=True),
                              jnp.sum(dcv * u, axis=0, keepdims=True)], axis=0)

        @pl.when(i == 0)
        def _():
            dw_ref[...] = dw

        @pl.when(i > 0)
        def _():
            dw_ref[...] += dw

    def col(c0):
        return pl.BlockSpec((tm, GC_W), lambda cj, i: (i, c0 + cj))

    def halo(c0):
        return pl.BlockSpec((HALO, GC_W), lambda cj, i: (jnp.maximum(i * hb - 1, 0), c0 + cj))

    def nxt(c0):
        return pl.BlockSpec((HALO, GC_W), lambda cj, i: (jnp.minimum((i + 1) * hb, t // HALO - 1), c0 + cj))

    out = pl.BlockSpec((tm, GC_W), lambda cj, i: (i, cj))
    return pl.pallas_call(
        body, name=name, grid=(CONV_CH // GC_W, nt),
        in_specs=[col(_GB0), col(_GC0), col(_XI0), halo(_GC0), halo(_XI0), nxt(_GB0), nxt(dy0), col(dy0),
                  pl.BlockSpec((3, GC_W), lambda cj, i: (0, cj))],
        out_specs=(out, out, out, pl.BlockSpec((3, GC_W), lambda cj, i: (0, cj))),
        out_shape=(jax.ShapeDtypeStruct((t, CONV_CH), MXU_DTYPE),) * 3 + (jax.ShapeDtypeStruct((3, CONV_CH), F32),),
        compiler_params=_cp("parallel", "arbitrary"))(proj, proj, proj, proj, proj, proj, dmix, dmix, conv_w)


_QKV_W = 3 * DN_W
_BA_COL = (4 * DN_W) // 128
_Z_COL = _QKV_W // DN_W


def gdn_prep_fwd(proj, conv_w, alog_row, dtb_row, name, tm=256):
    t = proj.shape[0]
    hb = tm // HALO
    qscale = DN_DIM ** -0.5

    def body(x_ref, xh_ref, w_ref, ba_ref, al_ref, dt_ref, q_ref, k_ref, v_ref, bg_ref, c_ref):
        i = pl.program_id(0)
        for gi in range(3 * DN_HEADS):
            sl = slice(gi * DN_DIM, (gi + 1) * DN_DIM)
            xp = jnp.concatenate([jnp.where(i == 0, 0.0, xh_ref[:, sl]), x_ref[:, sl]], axis=0)
            c = w_ref[0:1, sl] * xp[HALO - 3:HALO - 3 + tm]
            for j in range(1, 4):
                c = c + w_ref[j:j + 1, sl] * xp[HALO - 3 + j:HALO - 3 + j + tm]
            c_ref[:, sl] = c
            s = c * _sigmoid(c)
            osl = slice((gi % DN_HEADS) * DN_DIM, (gi % DN_HEADS + 1) * DN_DIM)
            if gi < DN_HEADS:
                q_ref[:, osl] = s * lax.rsqrt(jnp.sum(s * s, axis=-1, keepdims=True) + EPS) * qscale
            elif gi < 2 * DN_HEADS:
                k_ref[:, osl] = s * lax.rsqrt(jnp.sum(s * s, axis=-1, keepdims=True) + EPS)
            else:
                v_ref[:, osl] = s
        ba = ba_ref[...]
        lane = lax.broadcasted_iota(jnp.int32, ba.shape, 1)
        gval = -jnp.exp(al_ref[...]) * _softplus(ba + dt_ref[...])
        bg_ref[...] = jnp.where(lane < DN_HEADS, _sigmoid(ba), jnp.where(lane < 2 * DN_HEADS, gval, 0.0))

    row = pl.BlockSpec((tm, DN_W), lambda i: (i, 0))
    one = pl.BlockSpec((1, 128), lambda i: (0, 0))
    return pl.pallas_call(
        body, name=name, grid=(t // tm,),
        in_specs=[pl.BlockSpec((tm, _QKV_W), lambda i: (i, 0)),
                  pl.BlockSpec((HALO, _QKV_W), lambda i: (jnp.maximum(i * hb - 1, 0), 0)),
                  pl.BlockSpec((4, _QKV_W), lambda i: (0, 0)),
                  pl.BlockSpec((tm, 128), lambda i: (i, _BA_COL)), one, one],
        out_specs=(row, row, row, pl.BlockSpec((tm, 128), lambda i: (i, 0)), pl.BlockSpec((tm, _QKV_W), lambda i: (i, 0))),
        out_shape=(jax.ShapeDtypeStruct((t, DN_W), F32),) * 3 + (jax.ShapeDtypeStruct((t, 128), F32),
                                                                 jax.ShapeDtypeStruct((t, _QKV_W), F32)),
        compiler_params=_cp("parallel"))(proj, proj, conv_w, proj, alog_row, dtb_row)


def gdn_prep_bwd(proj, conv, conv_w, alog_row, dtb_row, dq, dk, dv, dbg, dz, name, tm=256):
    t = proj.shape[0]
    hb = tm // HALO
    nt = t // tm
    qscale = DN_DIM ** -0.5
    te = tm + HALO

    def body(x_ref, c_ref, cn_ref, w_ref, ba_ref, al_ref, dt_ref, dq_ref, dk_ref, dv_ref,
             dqn_ref, dkn_ref, dvn_ref, dbg_ref, dz_ref, dx_ref, dw_ref, ddt_ref, dal_ref):
        i = pl.program_id(0)
        first = i == 0
        last = i == nt - 1
        dws = []
        for gi in range(3 * DN_HEADS):
            sl = slice(gi * DN_DIM, (gi + 1) * DN_DIM)
            osl = slice((gi % DN_HEADS) * DN_DIM, (gi % DN_HEADS + 1) * DN_DIM)
            c = jnp.concatenate([c_ref[:, sl], cn_ref[:, sl]], axis=0)
            sg = _sigmoid(c)
            s = c * sg
            d_ref, dn_ref = ((dq_ref, dqn_ref), (dk_ref, dkn_ref), (dv_ref, dvn_ref))[gi // DN_HEADS]
            dy = jnp.concatenate([d_ref[:, osl], jnp.where(last, 0.0, dn_ref[:, osl])], axis=0)
            if gi < 2 * DN_HEADS:
                r = lax.rsqrt(jnp.sum(s * s, axis=-1, keepdims=True) + EPS)
                sh = s * r
                ds = r * (dy - sh * jnp.sum(sh * dy, axis=-1, keepdims=True))
                if gi < DN_HEADS:
                    ds = ds * qscale
            else:
                ds = dy
            dc = ds * sg * (1.0 + c * (1.0 - sg))
            dcs = [dc[3 - j:3 - j + tm] for j in range(4)]
            dx = w_ref[0:1, sl] * dcs[0]
            for j in range(1, 4):
                dx = dx + w_ref[j:j + 1, sl] * dcs[j]
            dx_ref[:, sl] = dx.astype(dx_ref.dtype)
            x0 = x_ref[:, sl]
            dws.append(jnp.concatenate([jnp.sum(dcs[j] * x0, axis=0, keepdims=True) for j in range(4)], axis=0))
        dw = jnp.concatenate(dws, axis=-1)
        ba = ba_ref[...]
        dbgv = dbg_ref[...]
        lane = lax.broadcasted_iota(jnp.int32, ba.shape, 1)
        beta = _sigmoid(ba)
        ea = -jnp.exp(al_ref[...])
        zin = ba + dt_ref[...]
        is_b = lane < DN_HEADS
        is_a = (lane >= DN_HEADS) & (lane < 2 * DN_HEADS)
        da = jnp.where(is_a, dbgv * ea * _sigmoid(zin), 0.0)
        dx_ref[:, _QKV_W:_QKV_W + DN_W] = dz_ref[...]
        dx_ref[:, _QKV_W + DN_W:] = jnp.where(is_b, dbgv * beta * (1.0 - beta), da).astype(dx_ref.dtype)
        ddt = jnp.sum(da, axis=0, keepdims=True)
        dal = jnp.sum(jnp.where(is_a, dbgv * ea * _softplus(zin), 0.0), axis=0, keepdims=True)

        @pl.when(first)
        def _():
            dw_ref[...] = dw
            ddt_ref[...] = ddt
            dal_ref[...] = dal

        @pl.when(i > 0)
        def _():
            dw_ref[...] += dw
            ddt_ref[...] += ddt
            dal_ref[...] += dal

    row = pl.BlockSpec((tm, DN_W), lambda i: (i, 0))
    nrow = pl.BlockSpec((HALO, DN_W), lambda i: (jnp.minimum((i + 1) * hb, t // HALO - 1), 0))
    one = pl.BlockSpec((1, 128), lambda i: (0, 0))
    return pl.pallas_call(
        body, name=name, grid=(nt,),
        in_specs=[pl.BlockSpec((tm, _QKV_W), lambda i: (i, 0)), pl.BlockSpec((tm, _QKV_W), lambda i: (i, 0)),
                  pl.BlockSpec((HALO, _QKV_W), lambda i: (jnp.minimum((i + 1) * hb, t // HALO - 1), 0)),
                  pl.BlockSpec((4, _QKV_W), lambda i: (0, 0)),
                  pl.BlockSpec((tm, 128), lambda i: (i, _BA_COL)), one, one,
                  row, row, row, nrow, nrow, nrow, pl.BlockSpec((tm, 128), lambda i: (i, 0)), row],
        out_specs=(pl.BlockSpec((tm, ODD_IN_PAD), lambda i: (i, 0)), pl.BlockSpec((4, _QKV_W), lambda i: (0, 0)), one, one),
        out_shape=(jax.ShapeDtypeStruct((t, ODD_IN_PAD), MXU_DTYPE), jax.ShapeDtypeStruct((4, _QKV_W), F32),
                   jax.ShapeDtypeStruct((1, 128), F32), jax.ShapeDtypeStruct((1, 128), F32)),
        compiler_params=_cp("arbitrary"))(proj, conv, conv, conv_w, proj, alog_row, dtb_row, dq, dk, dv, dq, dk, dv, dbg,
                                          dz)


def _chunk_masks():
    r = lax.broadcasted_iota(jnp.int32, (DN_CHUNK, DN_CHUNK), 0)
    c = lax.broadcasted_iota(jnp.int32, (DN_CHUNK, DN_CHUNK), 1)
    return r >= c, r > c


INV_PACK = 2


def _inv_unit_lower_many(mats):
    n = DN_CHUNK
    wide = INV_PACK * n
    r = lax.broadcasted_iota(jnp.int32, (wide, wide), 0)
    c = lax.broadcasted_iota(jnp.int32, (wide, wide), 1)
    same = (r & -n) == (c & -n)
    eye = jnp.where((r[:n] == (c[:n] & (n - 1))), 1.0, 0.0)

    def blockdiag(row):
        return jnp.where(same, jnp.concatenate([row] * INV_PACK, axis=0), 0.0)

    packs = [jnp.concatenate(mats[g:g + INV_PACK], axis=-1) for g in range(0, len(mats), INV_PACK)]
    xs = [eye - a for a in packs]
    pws = [_hi(a, blockdiag(a)) for a in packs]
    for step in range(5):
        if step < 4:
            both = [_hi(jnp.concatenate([x, pw], axis=0), blockdiag(pw)) for x, pw in zip(xs, pws)]
            xs = [x + b[:n] for x, b in zip(xs, both)]
            pws = [b[n:] for b in both]
        else:
            xs = [x + _hi(x, blockdiag(pw)) for x, pw in zip(xs, pws)]
    return [x[:, j * n:(j + 1) * n] for x in xs for j in range(INV_PACK)]


def _chunk_common(q, k, beta, gc, gcr, lower):
    gam = jnp.exp(jnp.where(lower, gc - gcr, NEG))
    eg = jnp.exp(gc)
    gl = gc[DN_CHUNK - 1:DN_CHUNK, :]
    kdf = jnp.exp(gl - gc)
    kb = k * beta
    bmat = _mx_nt(kb, k)
    qmat = _mx_nt(q, k)
    return gam, eg, jnp.exp(gl), kdf, kb, bmat, qmat


DN_STEP = 4


def gdn_fwd(q, k, v, bg, name):
    t = q.shape[0]
    n_chunks = t // DN_CHUNK

    def body(q_ref, k_ref, v_ref, bg_ref, o_ref, sall_ref, tall_ref, s_ref):
        n = pl.program_id(0)

        @pl.when(n == 0)
        def _():
            s_ref[...] = jnp.zeros_like(s_ref)

        lower, strict = _chunk_masks()
        ltri = jnp.where(lower, 1.0, 0.0)
        hs = range(DN_HEADS)
        sl = [slice(h * DN_DIM, (h + 1) * DN_DIM) for h in hs]
        units = [(c, h) for c in range(DN_STEP) for h in hs]
        nu = range(len(units))
        rs = [slice(c * DN_CHUNK, (c + 1) * DN_CHUNK) for c in range(DN_STEP)]
        bgv = [bg_ref[rs[c], :] for c in range(DN_STEP)]
        gcs = [_hi(ltri, b) for b in bgv]
        gcs_t = [g.T for g in gcs]
        qh = [q_ref[rs[c], sl[h]] for c, h in units]
        kh = [k_ref[rs[c], sl[h]] for c, h in units]
        vh = [v_ref[rs[c], sl[h]] for c, h in units]
        beta = [bgv[c][:, h:h + 1] for c, h in units]
        com = [_chunk_common(qh[u], kh[u], beta[u], gcs[c][:, DN_HEADS + h:DN_HEADS + h + 1],
                             gcs_t[c][DN_HEADS + h:DN_HEADS + h + 1, :], lower) for u, (c, h) in enumerate(units)]
        gam, eg, dec, kdf, kb, bmat, qmat = zip(*com)
        tms = _inv_unit_lower_many([jnp.where(strict, bmat[u] * gam[u], 0.0) for u in nu])
        for u, (c, h) in enumerate(units):
            tall_ref[c, h] = tms[u]
        uw = [_hi(tms[u], jnp.concatenate([vh[u] * beta[u], kb[u] * eg[u]], axis=-1)) for u in nu]
        qd = [qh[u] * eg[u] for u in nu]
        pm = [qmat[u] * gam[u] for u in nu]
        kd = [kh[u] * kdf[u] for u in nu]
        st = [s_ref[h] for h in hs]
        for c in range(DN_STEP):
            us = [c * DN_HEADS + h for h in hs]
            for h in hs:
                sall_ref[c, h] = st[h]
            v_new = [uw[us[h]][:, :DN_DIM] - _mx(uw[us[h]][:, DN_DIM:], st[h]) for h in hs]
            o_st = [_mx(qd[us[h]], st[h]) for h in hs]
            o_in = [_mx(pm[us[h]], v_new[h]) for h in hs]
            s_up = [_mx_tn(kd[us[h]], v_new[h]) for h in hs]
            for h in hs:
                o_ref[rs[c], sl[h]] = o_st[h] + o_in[h]
            st = [st[h] * dec[us[h]] + s_up[h] for h in hs]
        for h in hs:
            s_ref[h] = st[h]

    rows = DN_STEP * DN_CHUNK
    row = pl.BlockSpec((rows, DN_W), lambda n: (n, 0))
    return pl.pallas_call(
        body, name=name, grid=(n_chunks // DN_STEP,),
        in_specs=[row, row, row, pl.BlockSpec((rows, 128), lambda n: (n, 0))],
        out_specs=(row, pl.BlockSpec((DN_STEP, DN_HEADS, DN_DIM, DN_DIM), lambda n: (n, 0, 0, 0)),
                   pl.BlockSpec((DN_STEP, DN_HEADS, DN_CHUNK, DN_CHUNK), lambda n: (n, 0, 0, 0))),
        out_shape=(jax.ShapeDtypeStruct((t, DN_W), F32),
                   jax.ShapeDtypeStruct((n_chunks, DN_HEADS, DN_DIM, DN_DIM), F32),
                   jax.ShapeDtypeStruct((n_chunks, DN_HEADS, DN_CHUNK, DN_CHUNK), F32)),
        scratch_shapes=[pltpu.VMEM((DN_HEADS, DN_DIM, DN_DIM), F32)],
        compiler_params=_cp("arbitrary"))(q, k, v, bg)


def gdn_bwd(q, k, v, bg, sall, tall, do, name):
    t = q.shape[0]
    n_chunks = t // DN_CHUNK

    def body(q_ref, k_ref, v_ref, bg_ref, sall_ref, tall_ref, do_ref, dq_ref, dk_ref, dv_ref, dbg_ref, ds_ref):
        n = pl.program_id(0)

        @pl.when(n == 0)
        def _():
            ds_ref[...] = jnp.zeros_like(ds_ref)

        lower, strict = _chunk_masks()
        ltri = jnp.where(lower, 1.0, 0.0)
        bgv = bg_ref[...]
        gcs = _hi(ltri, bgv)
        gcs_t = gcs.T
        lane = lax.broadcasted_iota(jnp.int32, (DN_CHUNK, 128), 1)
        rowi = lax.broadcasted_iota(jnp.int32, (DN_CHUNK, 1), 0)
        hs = range(DN_HEADS)
        each = lambda fn, *ls: [fn(*a) for a in zip(*ls)]
        rsum = lambda a: jnp.sum(a, axis=-1, keepdims=True)
        sl = [slice(h * DN_DIM, (h + 1) * DN_DIM) for h in hs]
        st = [sall_ref[0, h] for h in hs]
        tms = [tall_ref[0, h] for h in hs]
        dsn = [ds_ref[h] for h in hs]
        qh = [q_ref[:, sl[h]] for h in hs]
        kh = [k_ref[:, sl[h]] for h in hs]
        vh = [v_ref[:, sl[h]] for h in hs]
        doh = [do_ref[:, sl[h]] for h in hs]
        beta = [bgv[:, h:h + 1] for h in hs]
        com = [_chunk_common(qh[h], kh[h], beta[h], gcs[:, DN_HEADS + h:DN_HEADS + h + 1],
                             gcs_t[DN_HEADS + h:DN_HEADS + h + 1, :], lower) for h in hs]
        gam, eg, dec, kdf, kb, bmat, qmat = zip(*com)
        rhs_w = each(lambda a, b: a * b, kb, eg)
        uw = each(lambda t_, v_, b_, r_: _hi(t_, jnp.concatenate([v_ * b_, r_], axis=-1)), tms, vh, beta, rhs_w)
        qd = each(lambda a, b: a * b, qh, eg)
        kd = each(lambda a, b: a * b, kh, kdf)
        pmat = each(lambda a, b: a * b, qmat, gam)
        v_new = each(lambda uw_, s_: uw_[:, :DN_DIM] - _mx(uw_[:, DN_DIM:], s_), uw, st)
        dqd = each(_mx_nt, doh, st)
        ds_o = each(_mx_tn, qd, doh)
        dp = each(lambda d_, v_: jnp.where(lower, _mx_nt(d_, v_), 0.0), doh, v_new)
        dvn_o = each(_mx_tn, pmat, doh)
        ddec = each(lambda d_, s_: jnp.sum(rsum(d_ * s_), axis=0, keepdims=True), dsn, st)
        dkd = each(_mx_nt, v_new, dsn)
        dvn = each(lambda a, k_, d_: a + _mx(k_, d_), dvn_o, kd, dsn)
        dw = each(lambda d_, s_: -_mx_nt(d_, s_), dvn, st)
        ds_w = each(lambda uw_, d_: _mx_tn(uw_[:, DN_DIM:], d_), uw, dvn)
        for h in hs:
            ds_ref[h] = ds_o[h] + dec[h] * dsn[h] - ds_w[h]
        dr = each(lambda t_, a, b: _hi_tn(t_, jnp.concatenate([a, b], axis=-1)), tms, dvn, dw)
        da = each(lambda r_, uw_: jnp.where(strict, -_hi_nt(r_, uw_), 0.0), dr, uw)
        dru = [r_[:, :DN_DIM] for r_ in dr]
        drw = [r_[:, DN_DIM:] for r_ in dr]
        db = each(lambda a, b: a * b, da, gam)
        dq_m = each(lambda a, b: a * b, dp, gam)
        e = each(lambda a, bm, p_, qm, g_: (a * bm + p_ * qm) * g_, da, bmat, dp, qmat, gam)
        dkb = each(lambda b_, k_, r_, e_: _mx(b_, k_) + r_ * e_, db, kh, drw, eg)
        dk = each(lambda b_, kb_, m_, q_, d_, f_: _mx_tn(b_, kb_) + _mx_tn(m_, q_) + d_ * f_, db, kb, dq_m, qh, dkd, kdf)
        dq = each(lambda m_, k_, d_, e_: _mx(m_, k_) + d_ * e_, dq_m, kh, dqd, eg)
        tk = each(lambda a, b: rsum(a * b), dkd, kd)
        dbeta_all = jnp.zeros((DN_CHUNK, 128), F32)
        dgc_all = jnp.zeros((DN_CHUNK, 128), F32)
        for h in hs:
            dgc = (jnp.sum(e[h], axis=1, keepdims=True) - jnp.sum(e[h].T, axis=1, keepdims=True)
                   + rsum(dqd[h] * qd[h]) - tk[h] + rsum(drw[h] * rhs_w[h]))
            dgl = jnp.sum(tk[h], axis=0, keepdims=True) + ddec[h] * dec[h]
            dgc = dgc + jnp.where(rowi == DN_CHUNK - 1, dgl, 0.0)
            dbeta = rsum(dru[h] * vh[h]) + rsum(dkb[h] * kh[h])
            dq_ref[:, sl[h]] = dq[h]
            dk_ref[:, sl[h]] = dk[h] + dkb[h] * beta[h]
            dv_ref[:, sl[h]] = dru[h] * beta[h]
            dbeta_all = jnp.where(lane == h, dbeta, dbeta_all)
            dgc_all = jnp.where(lane == DN_HEADS + h, dgc, dgc_all)
        dbg_ref[...] = dbeta_all + _hi_tn(ltri, dgc_all)

    rev = lambda n: (n_chunks - 1 - n, 0)
    row = pl.BlockSpec((DN_CHUNK, DN_W), rev)
    small = pl.BlockSpec((DN_CHUNK, 128), rev)
    return pl.pallas_call(
        body, name=name, grid=(n_chunks,),
        in_specs=[row, row, row, small,
                  pl.BlockSpec((1, DN_HEADS, DN_DIM, DN_DIM), lambda n: (n_chunks - 1 - n, 0, 0, 0)),
                  pl.BlockSpec((1, DN_HEADS, DN_CHUNK, DN_CHUNK), lambda n: (n_chunks - 1 - n, 0, 0, 0)), row],
        out_specs=(row, row, row, small),
        out_shape=(jax.ShapeDtypeStruct((t, DN_W), F32),) * 3 + (jax.ShapeDtypeStruct((t, 128), F32),),
        scratch_shapes=[pltpu.VMEM((DN_HEADS, DN_DIM, DN_DIM), F32)],
        compiler_params=_cp("arbitrary"))(q, k, v, bg, sall, tall, do)


def gdn_out_fwd(o, proj, o_gain, name, tm=256):
    t = o.shape[0]

    def body(o_ref, z_ref, g_ref, y_ref, yt_ref):
        for h in range(DN_HEADS):
            sl = slice(h * DN_DIM, (h + 1) * DN_DIM)
            ov, zv = o_ref[:, sl], z_ref[:, sl]
            r = lax.rsqrt(jnp.mean(ov * ov, axis=-1, keepdims=True) + EPS)
            y = ov * r * g_ref[...] * (zv * _sigmoid(zv))
            y_ref[:, sl] = y.astype(y_ref.dtype)
            yt_ref[sl, :] = y.T.astype(yt_ref.dtype)

    row = pl.BlockSpec((tm, DN_W), lambda i: (i, 0))
    return pl.pallas_call(
        body, name=name, grid=(t // tm,),
        in_specs=[row, pl.BlockSpec((tm, DN_W), lambda i: (i, _Z_COL)), pl.BlockSpec((1, DN_DIM), lambda i: (0, 0))],
        out_specs=(row, pl.BlockSpec((DN_W, tm), lambda i: (0, i))),
        out_shape=(jax.ShapeDtypeStruct((t, DN_W), MXU_DTYPE), jax.ShapeDtypeStruct((DN_W, t), MXU_DTYPE)),
        compiler_params=_cp("parallel"))(o, proj, o_gain)


def gdn_out_bwd(o, proj, o_gain, dy, name, tm=256):
    t = o.shape[0]

    def body(o_ref, z_ref, g_ref, dy_ref, do_ref, dz_ref, dg_ref):
        i = pl.program_id(0)
        dg = jnp.zeros((1, DN_DIM), F32)
        for h in range(DN_HEADS):
            sl = slice(h * DN_DIM, (h + 1) * DN_DIM)
            ov, zv, dyv = o_ref[:, sl], z_ref[:, sl], dy_ref[:, sl]
            r = lax.rsqrt(jnp.mean(ov * ov, axis=-1, keepdims=True) + EPS)
            oh = ov * r
            sg = _sigmoid(zv)
            dz_ref[:, sl] = (dyv * oh * g_ref[...] * sg * (1.0 + zv * (1.0 - sg))).astype(dz_ref.dtype)
            don = dyv * (zv * sg)
            dg = dg + jnp.sum(don * oh, axis=0, keepdims=True)
            doh = don * g_ref[...]
            do_ref[:, sl] = r * (doh - oh * jnp.mean(doh * oh, axis=-1, keepdims=True))

        @pl.when(i == 0)
        def _():
            dg_ref[...] = dg

        @pl.when(i > 0)
        def _():
            dg_ref[...] += dg

    row = pl.BlockSpec((tm, DN_W), lambda i: (i, 0))
    one = pl.BlockSpec((1, DN_DIM), lambda i: (0, 0))
    return pl.pallas_call(
        body, name=name, grid=(t // tm,),
        in_specs=[row, pl.BlockSpec((tm, DN_W), lambda i: (i, _Z_COL)), one, row],
        out_specs=(row, row, one),
        out_shape=(jax.ShapeDtypeStruct((t, DN_W), F32), jax.ShapeDtypeStruct((t, DN_W), MXU_DTYPE),
                   jax.ShapeDtypeStruct((1, DN_DIM), F32)),
        compiler_params=_cp("arbitrary"))(o, proj, o_gain, dy)


def _peer(k):
    x, y, c = lax.axis_index("x"), lax.axis_index("y"), lax.axis_index("c")
    px = 1 - x if k & 4 else x
    py = 1 - y if k & 2 else y
    pc = 1 - c if k & 1 else c
    return (px, py, pc), 4 * px + 2 * py + pc


_HBM = pl.BlockSpec(memory_space=pltpu.HBM)
_SEM = pl.BlockSpec(memory_space=pltpu.SEMAPHORE)
_DATAFLOW = pltpu.SideEffectType.DATAFLOW_SIDE_EFFECTING
N_PEER = N_DEV - 1


def send_start(srcs, name, scatter, after):
    na = len(srcs)
    ns = (2 * N_PEER + 1) * na
    lands = [lax.empty((N_DEV,) + (s.shape[1:] if scatter else s.shape), s.dtype) for s in srcs]
    extra = [] if after is None else [after]

    def body(*refs):
        src_refs, land_refs = refs[:na], refs[na:2 * na]
        sems = refs[2 * na + len(extra):2 * na + len(extra) + ns]
        land_out, token = refs[-1 - na:-1], refs[-1]
        _, me = _peer(0)
        for a in range(na):
            pltpu.make_async_copy(src_refs[a].at[me] if scatter else src_refs[a], land_out[a].at[me],
                                  sems[2 * N_PEER * na + a]).start()
        for k in range(1, N_DEV):
            peer, pid = _peer(k)
            for a in range(na):
                pltpu.make_async_remote_copy(
                    src_ref=src_refs[a].at[pid] if scatter else src_refs[a], dst_ref=land_refs[a].at[me],
                    send_sem=sems[2 * (a * N_PEER + k - 1)], recv_sem=sems[2 * (a * N_PEER + k - 1) + 1],
                    device_id=peer, device_id_type=MESH).start()
        token[...] = jnp.zeros_like(token)

    hbm = lambda arrs: tuple(pltpu.HBM(a.shape, a.dtype) for a in arrs)
    outs = pl.pallas_call(
        body, name=name,
        out_shape=(pltpu.SemaphoreType.DMA(()),) * ns + hbm(srcs) + hbm(lands) + (jax.ShapeDtypeStruct((8, 128), F32),),
        in_specs=[_HBM] * (2 * na) + [pl.BlockSpec(memory_space=pl.ANY)] * len(extra),
        out_specs=(_SEM,) * ns + (_HBM,) * (2 * na) + (pl.BlockSpec(memory_space=pltpu.VMEM),),
        input_output_aliases={i: ns + i for i in range(2 * na)},
        compiler_params=pltpu.CompilerParams(has_side_effects=_DATAFLOW),
    )(*[pltpu.with_memory_space_constraint(a, pltpu.HBM) for a in list(srcs) + lands], *extra)
    return outs[:ns], outs[ns:ns + na], outs[ns + na:ns + 2 * na], outs[-1]


def send_wait(sems, srcs_thru, lands_thru, name, scatter, after):
    na = len(srcs_thru)
    ns = (2 * N_PEER + 1) * na

    def body(*refs):
        src_refs, land_refs, sm = refs[:na], refs[na:2 * na], refs[2 * na:2 * na + ns]
        _, me = _peer(0)
        for a in range(na):
            pltpu.make_async_copy(src_refs[a].at[me] if scatter else src_refs[a], land_refs[a].at[me],
                                  sm[2 * N_PEER * na + a]).wait()
        for k in range(1, N_DEV):
            peer, pid = _peer(k)
            for a in range(na):
                cp = pltpu.make_async_remote_copy(
                    src_ref=src_refs[a].at[pid] if scatter else src_refs[a], dst_ref=land_refs[a].at[pid],
                    send_sem=sm[2 * (a * N_PEER + k - 1)], recv_sem=sm[2 * (a * N_PEER + k - 1) + 1],
                    device_id=peer, device_id_type=MESH)
                cp.wait_send()
                cp.wait_recv()

    hbm = lambda arrs: tuple(pltpu.HBM(a.shape, a.dtype) for a in arrs)
    outs = pl.pallas_call(
        body, name=name, out_shape=hbm(srcs_thru) + hbm(lands_thru),
        in_specs=[_HBM] * (2 * na) + [_SEM] * ns + [pl.BlockSpec(memory_space=pl.ANY)], out_specs=(_HBM,) * (2 * na),
        input_output_aliases={i: i for i in range(2 * na)},
        compiler_params=pltpu.CompilerParams(has_side_effects=_DATAFLOW),
    )(*srcs_thru, *lands_thru, *sems, after)
    return outs[na:]


def _adamw(w, g, m, v):
    m = ADAM_B1 * m + (1.0 - ADAM_B1) * g
    v = ADAM_B2 * v + (1.0 - ADAM_B2) * (g * g)
    m_hat = m / (1.0 - ADAM_B1 ** ADAM_STEP)
    v_hat = v / (1.0 - ADAM_B2 ** ADAM_STEP)
    return -ADAM_LR * (m_hat / (jnp.sqrt(v_hat) + ADAM_EPS) + ADAM_WD * w), m, v


def adam_sum(w, pieces, m, v, name, layer=0, into=None):
    nl, r, c = w.shape
    tr = r
    for cand in (256, 128, 64, 32, 16, 8):
        if r % cand == 0:
            tr = cand
            break

    def body(w_ref, p_ref, m_ref, v_ref, *rest):
        g_ref, d_ref, nm_ref, nv_ref = rest[-4:]
        g = p_ref[0].astype(F32)
        for s in range(1, N_DEV):
            g = g + p_ref[s].astype(F32)
        g_ref[0] = g
        d_ref[0], nm_ref[0], nv_ref[0] = _adamw(w_ref[0], g, m_ref[0], v_ref[0])

    row = pl.BlockSpec((1, tr, c), lambda i: (layer, i, 0))
    out = jax.ShapeDtypeStruct((nl, r, c), F32)
    extra = [] if into is None else list(into)
    return pl.pallas_call(
        body, name=name, grid=(r // tr,),
        in_specs=[row, pl.BlockSpec((N_DEV, tr, c), lambda i: (0, i, 0)), row, row]
        + [pl.BlockSpec(memory_space=pl.ANY)] * len(extra),
        out_specs=(row,) * 4, out_shape=(out,) * 4,
        input_output_aliases={4 + i: i for i in range(len(extra))},
        compiler_params=_cp("parallel"))(w, pieces, m, v, *extra)


def sum_rows(gathered, name):
    _, r, c = gathered.shape

    def body(p_ref, o_ref):
        g = p_ref[0]
        for s in range(1, N_DEV):
            g = g + p_ref[s]
        o_ref[...] = g

    return pl.pallas_call(body, name=name, out_shape=jax.ShapeDtypeStruct((r, c), F32))(gathered)


def adam_small(w, g, m, v, name):
    def body(w_ref, g_ref, m_ref, v_ref, d_ref, nm_ref, nv_ref):
        d_ref[...], nm_ref[...], nv_ref[...] = _adamw(w_ref[...], g_ref[...], m_ref[...], v_ref[...])

    out = jax.ShapeDtypeStruct(w.shape, F32)
    return pl.pallas_call(body, name=name, out_shape=(out,) * 3)(w, g, m, v)


def _rope_tables(t):
    inv_freq = 10000.0 ** (-jnp.arange(0, HEAD_DIM, 2, dtype=F32) / HEAD_DIM)
    ang = jnp.arange(t, dtype=F32)[:, None] * inv_freq[None, :]
    cos, sin = jnp.cos(ang), jnp.sin(ang)
    return jnp.concatenate([cos, cos], axis=-1), jnp.concatenate([sin, sin], axis=-1)


def _lane_row(vec8):
    return jnp.pad(vec8.reshape(1, DN_HEADS), ((0, 0), (DN_HEADS, 128 - 2 * DN_HEADS)))


def _ffn_bwd(x, norm_g, w_gu, w_d, saved, dy, tag, after=None):
    ft, gu, at = saved
    dgu = ffn_dact(dy, w_d, gu, f"{tag}_d_gate_up", after=after)
    dwd = mm_at(at, dy, f"{tag}_dw_down")
    nj = D_FF // GU_TILE
    dwgu = mm_at(ft, dgu, f"{tag}_dw_gate_up", transposed=True, tn=GU_TILE, row_block=lambda q: (q % 2) * nj + q // 2)
    dx, dg = mm_rms_bwd(dgu, w_gu, x, norm_g, dy, f"{tag}_d_norm", chunks=_GU_CHUNKS)
    return dx, dwgu, dwd, dg


def local_step(x, target, small, weights_of, grads_out, after=None):
    t = x.shape[0]
    cosf, sinf = _rope_tables(t)
    alog_row, dtb_row = _lane_row(small["odd_a_log"]), _lane_row(small["odd_dt_bias"])

    h0, h0t = rms_fwd(x, small["even_norm"], "even_norm", after=after)
    we = weights_of("even", h0)
    small = {**small, **we.get("small", {})}
    proj0 = mm_nt(h0, we["w_in"], "even_in_proj")
    qr, kr = qk_prep_fwd(proj0, small["even_q_gain"], small["even_k_gain"], cosf, sinf, "even_qk_prep")
    y_attn, mix0, mix0_t, lse = swa_fwd(qr, kr, proj0, small["even_sinks"], "even_swa")
    mix0, mix0_t = gconv_fwd(proj0, small["even_conv_w"], mix0, mix0_t, "even_gconv")
    we = {**we, **weights_of("even_out", mix0)}
    x1, f0, f0t = mm_nn_res_norm(mix0, we["w_out"], x, small["ffn_norm0"], "even_out_proj")
    w0 = weights_of("ffn0", x1)
    gu0, a0, a0t = ffn_up(f0, w0["gate_up"], "ffn0_gate_up")
    ffn0 = (f0t, gu0, a0t)
    x2, h1, h1t = mm_nn_res_norm(a0, w0["down"], x1, small["odd_norm"], "ffn0_down")

    wo = weights_of("odd", x2)
    proj1 = mm_nt(h1, wo["w_in"], "odd_in_proj")
    qn, kn, vs, bg, conv1 = gdn_prep_fwd(proj1, small["odd_conv_w"], alog_row, dtb_row, "odd_prep")
    o, sall, tall = gdn_fwd(qn, kn, vs, bg, "odd_delta_rule")
    og, ogt = gdn_out_fwd(o, proj1, small["odd_o_gain"], "odd_gate_norm")
    x3, f1, f1t = mm_nn_res_norm(og, wo["w_out"], x2, small["ffn_norm1"], "odd_out_proj")
    w1 = weights_of("ffn1", x3)
    gu1, a1, a1t = ffn_up(f1, w1["gate_up"], "ffn1_gate_up")
    ffn1 = (f1t, gu1, a1t)
    dy, loss_row = mm_nn_res_loss(a1, w1["down"], x3, target, "ffn1_down_loss")

    gs = {}
    dx3, dwgu, dwd, gs["ffn_norm1"] = _ffn_bwd(x3, small["ffn_norm1"], w1["gate_up"], w1["down"], ffn1, dy, "ffn1")
    tok = grads_out("ffn1", {"gate_up": dwgu, "down": dwd})

    dog = mm_nt(dx3, wo["w_out"], "odd_d_gated", after=tok)
    dwo = mm_at(ogt, dx3, "odd_dw_out")
    do, dz, gs["odd_o_gain"] = gdn_out_bwd(o, proj1, small["odd_o_gain"], dog, "odd_d_gate_norm")
    dqn, dkn, dvs, dbg = gdn_bwd(qn, kn, vs, bg, sall, tall, do, "odd_d_delta_rule")
    dproj1, gs["odd_conv_w"], ddt_row, dal_row = gdn_prep_bwd(
        proj1, conv1, small["odd_conv_w"], alog_row, dtb_row, dqn, dkn, dvs, dbg, dz, "odd_d_prep")
    gs["odd_dt_bias"] = ddt_row[:, DN_HEADS:2 * DN_HEADS]
    gs["odd_a_log"] = dal_row[:, DN_HEADS:2 * DN_HEADS]
    dwi = mm_at(h1t, dproj1, "odd_dw_in", transposed=True)
    dx2, gs["odd_norm"] = mm_rms_bwd(dproj1, wo["w_in"], x2, small["odd_norm"], dx3, "odd_d_norm")
    tok = grads_out("odd", {"w_in": dwi, "w_out": dwo})

    dx1, dwgu, dwd, gs["ffn_norm0"] = _ffn_bwd(x1, small["ffn_norm0"], w0["gate_up"], w0["down"], ffn0, dx2, "ffn0",
                                               after=tok)
    tok = grads_out("ffn0", {"gate_up": dwgu, "down": dwd})

    dmix = mm_nt(dx1, we["w_out"], "even_d_mix", after=tok)
    dwo = mm_at(mix0_t, dx1, "even_dw_out")
    dqr, dkr, dv, gs["even_sinks"] = swa_bwd(qr, kr, proj0, small["even_sinks"], y_attn, lse, dmix, "even_d_swa")
    dqk, gs["even_q_gain"], gs["even_k_gain"] = qk_prep_bwd(
        proj0, small["even_q_gain"], small["even_k_gain"], cosf, sinf, dqr, dkr, "even_d_qk_prep")
    dgb, dgc, dxi, gs["even_conv_w"] = gconv_bwd(proj0, small["even_conv_w"], dmix, "even_d_gconv")
    dproj0 = jnp.concatenate([dqk, dv, dgb, dgc, dxi], axis=-1)
    dwi = mm_at(h0t, dproj0, "even_dw_in", transposed=True)
    tok = grads_out("even", {"w_in": dwi, "w_out": dwo})
    grad_x, gs["even_norm"] = mm_rms_bwd(dproj0, we["w_in"], x, small["even_norm"], dx1, "even_d_norm", after=tok)
    return loss_row, grad_x, gs


_SMALL_ORDER = ("even_norm", "even_q_gain", "even_k_gain", "even_sinks", "odd_a_log", "odd_dt_bias", "odd_o_gain",
                "ffn_norm0", "ffn_norm1", "odd_norm", "even_conv_w", "odd_conv_w")
_SMALL_SIZE = {"even_norm": 1024, "even_q_gain": 64, "even_k_gain": 64, "even_sinks": 8, "odd_a_log": 8,
               "odd_dt_bias": 8, "odd_o_gain": 128, "ffn_norm0": 1024, "ffn_norm1": 1024, "odd_norm": 1024,
               "even_conv_w": 3 * 512, "odd_conv_w": 4 * 3072}
_N_REPL = 9


def _pack_rows(vals):
    flat = jnp.concatenate([v.reshape(-1) for v in vals])
    pad = (-flat.shape[0]) % 1024
    return jnp.pad(flat, (0, pad)).reshape(-1, 128)


def _my_block(full, size, axis):
    me = 4 * lax.axis_index("x") + 2 * lax.axis_index("y") + lax.axis_index("c")
    return lax.dynamic_slice_in_dim(full, me * size, size, axis=axis)


def kernel(x, even_norm, even_w_in, even_q_gain, even_k_gain, even_sinks, even_conv_w, even_w_out, odd_norm, odd_w_in, odd_conv_w, odd_a_log, odd_dt_bias, odd_o_gain, odd_w_out, ffn_norm, ffn_w_gate_up, ffn_w_down, loss_target, m_even_norm, m_even_w_in, m_even_q_gain, m_even_k_gain, m_even_sinks, m_even_conv_w, m_even_w_out, m_odd_norm, m_odd_w_in, m_odd_conv_w, m_odd_a_log, m_odd_dt_bias, m_odd_o_gain, m_odd_w_out, m_ffn_norm, m_ffn_w_gate_up, m_ffn_w_down, v_even_norm, v_even_w_in, v_even_q_gain, v_even_k_gain, v_even_sinks, v_even_conv_w, v_even_w_out, v_odd_norm, v_odd_w_in, v_odd_conv_w, v_odd_a_log, v_odd_dt_bias, v_odd_o_gain, v_odd_w_out, v_ffn_norm, v_ffn_w_gate_up, v_ffn_w_down):
    t = x.shape[1]
    d = D_MODEL

    tr = lambda a: jnp.swapaxes(a, 1, 2)
    shard = {
        "even": {"w_in": tr(even_w_in)[0], "w_out": even_w_out[0]},
        "ffn0": {"gate_up": tr(ffn_w_gate_up)[0], "down": ffn_w_down[0]},
        "odd": {"w_in": tr(odd_w_in)[0], "w_out": odd_w_out[0]},
        "ffn1": {"gate_up": tr(ffn_w_gate_up)[1], "down": ffn_w_down[1]},
    }
    given = {
        ("even", "w_in"): ("even_w_in", even_w_in, m_even_w_in, v_even_w_in, 0),
        ("even", "w_out"): ("even_w_out", even_w_out, m_even_w_out, v_even_w_out, 0),
        ("odd", "w_in"): ("odd_w_in", odd_w_in, m_odd_w_in, v_odd_w_in, 0),
        ("odd", "w_out"): ("odd_w_out", odd_w_out, m_odd_w_out, v_odd_w_out, 0),
        ("ffn0", "gate_up"): ("ffn_w_gate_up", ffn_w_gate_up, m_ffn_w_gate_up, v_ffn_w_gate_up, 0),
        ("ffn1", "gate_up"): ("ffn_w_gate_up", ffn_w_gate_up, m_ffn_w_gate_up, v_ffn_w_gate_up, 1),
        ("ffn0", "down"): ("ffn_w_down", ffn_w_down, m_ffn_w_down, v_ffn_w_down, 0),
        ("ffn1", "down"): ("ffn_w_down", ffn_w_down, m_ffn_w_down, v_ffn_w_down, 1),
    }

    def whole(group, parts):
        col, row = tuple(shard[group])
        w_col = parts[0].reshape(-1, d)
        if group == "odd":
            w_col = jnp.pad(w_col, ((0, ODD_IN_PAD - ODD_IN_W), (0, 0)))
        return {col: w_col, row: parts[1].reshape(-1, d)}

    wire = {g: [a.astype(MXU_DTYPE) for a in shard[g].values()] for g in shard}
    wire["even_out"] = [wire["even"].pop()]
    wire["even"].append(_pack_rows([odd_norm, even_conv_w, odd_conv_w]))
    gathers, tok = {}, None
    for g in ("even", "even_out", "ffn0", "odd", "ffn1"):
        sems, srcs_thru, lands_thru, tok = send_start(wire[g], f"gather_{g}_start", False, tok)
        gathers[g] = (sems, srcs_thru, lands_thru)
    o1 = d // N_DEV
    o2 = o1 + 3 * CONV_CH // N_DEV

    def weights_of(group, after):
        lands = send_wait(*gathers[group], f"gather_{group}_wait", False, after)
        if group == "even_out":
            return {"w_out": lands[0].reshape(-1, d)}
        if group != "even":
            return whole(group, lands)
        sg = lands[1].reshape(N_DEV, -1)
        return {"w_in": lands[0].reshape(-1, d), "small": {
            "odd_norm": sg[:, :o1].reshape(1, d),
            "even_conv_w": sg[:, o1:o2].reshape(N_DEV, 3, CONV_CH // N_DEV).transpose(1, 0, 2).reshape(3, CONV_CH),
            "odd_conv_w": sg[:, o2:o2 + 4 * _QKV_W // N_DEV].reshape(N_DEV, 4, _QKV_W // N_DEV)
            .transpose(1, 0, 2).reshape(4, _QKV_W)}}

    sent = {}

    def grads_out(group, dws):
        col, row = tuple(shard[group])
        n_cols = N_DEV * shard[group][col].shape[0]
        pieces = [dws[col][:n_cols].reshape((N_DEV,) + shard[group][col].shape),
                  dws[row].reshape((N_DEV,) + shard[group][row].shape)]
        sems, srcs_thru, lands_thru, token = send_start(pieces, f"exchange_{group}_start", True, None)
        sent[group] = (sems, srcs_thru, lands_thru, pieces)
        return token

    small = {
        "even_norm": even_norm, "even_q_gain": even_q_gain, "even_k_gain": even_k_gain, "even_sinks": even_sinks,
        "odd_a_log": odd_a_log.reshape(-1), "odd_dt_bias": odd_dt_bias.reshape(-1), "odd_o_gain": odd_o_gain,
        "ffn_norm0": ffn_norm[0:1], "ffn_norm1": ffn_norm[1:2],
    }

    loss_row, grad_x, gs = local_step(x.reshape(t, d), loss_target.reshape(t, d), small, weights_of, grads_out, after=tok)

    rows = _pack_rows([gs[n] for n in _SMALL_ORDER] + [loss_row[:, 0:1]])
    small_sent = send_start([rows], "gather_small_grads_start", False, None)

    res, behind = {}, small_sent[3]
    for g in ("ffn1", "odd", "ffn0", "even"):
        sems, srcs_thru, lands_thru, pieces = sent[g]
        lands = send_wait(sems, srcs_thru, lands_thru, f"exchange_{g}_wait", True, behind)
        for i, (key, pcs) in enumerate(zip(shard[g], lands)):
            name, w_, m_, v_, layer = given[g, key]
            view = tr if i == 0 else (lambda a: a)
            res[name] = adam_sum(view(w_), pcs, view(m_), view(v_), f"adamw_{g}_{key}", layer=layer, into=res.get(name))
        behind = res[name][0]
    for name in ("even_w_in", "odd_w_in", "ffn_w_gate_up"):
        res[name] = tuple(tr(a) for a in res[name])

    (rows_g,) = send_wait(*small_sent[:3], "gather_small_grads_wait", False, behind)
    tot = sum_rows(rows_g, "sum_small_grads").reshape(-1)
    off, sgrad = 0, {}
    for n in _SMALL_ORDER:
        sgrad[n] = tot[off:off + _SMALL_SIZE[n]]
        off += _SMALL_SIZE[n]
    loss = tot[off]

    repl = _SMALL_ORDER[:_N_REPL]
    repl_w = {"even_norm": even_norm, "even_q_gain": even_q_gain, "even_k_gain": even_k_gain, "even_sinks": even_sinks,
              "odd_a_log": odd_a_log, "odd_dt_bias": odd_dt_bias, "odd_o_gain": odd_o_gain,
              "ffn_norm0": ffn_norm[0], "ffn_norm1": ffn_norm[1]}
    repl_m = {"even_norm": m_even_norm, "even_q_gain": m_even_q_gain, "even_k_gain": m_even_k_gain,
              "even_sinks": m_even_sinks, "odd_a_log": m_odd_a_log, "odd_dt_bias": m_odd_dt_bias,
              "odd_o_gain": m_odd_o_gain, "ffn_norm0": m_ffn_norm[0], "ffn_norm1": m_ffn_norm[1]}
    repl_v = {"even_norm": v_even_norm, "even_q_gain": v_even_q_gain, "even_k_gain": v_even_k_gain,
              "even_sinks": v_even_sinks, "odd_a_log": v_odd_a_log, "odd_dt_bias": v_odd_dt_bias,
              "odd_o_gain": v_odd_o_gain, "ffn_norm0": v_ffn_norm[0], "ffn_norm1": v_ffn_norm[1]}
    pk = lambda dct: _pack_rows([dct[n] for n in repl])
    pd_, pm_, pv_ = adam_small(pk(repl_w), pk(sgrad), pk(repl_m), pk(repl_v), "adamw_replicated")
    sres = {}
    off = 0
    for n in repl:
        sz = _SMALL_SIZE[n]
        sres[n] = (sgrad[n], pd_.reshape(-1)[off:off + sz], pm_.reshape(-1)[off:off + sz], pv_.reshape(-1)[off:off + sz])
        off += sz
    g_on = _my_block(sgrad["odd_norm"].reshape(1, d), d // N_DEV, 1)
    g_ec = _my_block(sgrad["even_conv_w"].reshape(3, CONV_CH), CONV_CH // N_DEV, 1)
    g_oc = _my_block(sgrad["odd_conv_w"].reshape(4, _QKV_W), _QKV_W // N_DEV, 1)
    shard_w = _pack_rows([odd_norm, even_conv_w, odd_conv_w])
    sd_, sm_, sv_ = adam_small(shard_w, _pack_rows([g_on, g_ec, g_oc]),
                               _pack_rows([m_odd_norm, m_even_conv_w, m_odd_conv_w]),
                               _pack_rows([v_odd_norm, v_even_conv_w, v_odd_conv_w]), "adamw_sharded_small")
    off = 0
    for n, gfull, like in (("odd_norm", g_on, odd_norm), ("even_conv_w", g_ec, even_conv_w), ("odd_conv_w", g_oc, odd_conv_w)):
        sz = like.size
        sres[n] = (gfull, sd_.reshape(-1)[off:off + sz], sm_.reshape(-1)[off:off + sz], sv_.reshape(-1)[off:off + sz])
        off += sz

    def small_out(name, like, kind):
        if name == "ffn_norm":
            return jnp.stack([sres["ffn_norm0"][kind], sres["ffn_norm1"][kind]]).reshape(like.shape)
        return sres[name][kind].reshape(like.shape)

    order = (("even_norm", even_norm), ("even_w_in", even_w_in), ("even_q_gain", even_q_gain),
             ("even_k_gain", even_k_gain), ("even_sinks", even_sinks), ("even_conv_w", even_conv_w),
             ("even_w_out", even_w_out), ("odd_norm", odd_norm), ("odd_w_in", odd_w_in), ("odd_conv_w", odd_conv_w),
             ("odd_a_log", odd_a_log), ("odd_dt_bias", odd_dt_bias), ("odd_o_gain", odd_o_gain),
             ("odd_w_out", odd_w_out), ("ffn_norm", ffn_norm), ("ffn_w_gate_up", ffn_w_gate_up),
             ("ffn_w_down", ffn_w_down))
    outs = [loss, grad_x.reshape(x.shape)]
    for kind in range(4):
        for name, like in order:
            outs.append(res[name][kind] if name in res else small_out(name, like, kind))
    return tuple(outs)
```

```python
import jax
import jax.numpy as jnp
import numpy as np
from jax import lax
from jax.experimental import pallas as pl
from jax.experimental.pallas import tpu as pltpu

F32 = jnp.float32
MXU_DTYPE = jnp.bfloat16
HI = lax.Precision.HIGH
EPS = 1e-6
N_DEV = 8
D_MODEL = 1024
HEAD_DIM = 64
ATTN_HEADS = 8
KV_HEADS = 2
ATTN_BLOCK = 128
Q_W = 512
KV_W = 128
CONV_CH = 512
EVEN_IN_W = 2304
DN_HEADS = 8
DN_DIM = 128
DN_W = 1024
DN_CHUNK = 64
ODD_IN_W = 4112
ODD_IN_PAD = 4224
D_FF = 2816
NEG = -1e30
VMEM_LIMIT = 56 * 1024 * 1024
ADAM_LR, ADAM_B1, ADAM_B2, ADAM_EPS, ADAM_WD, ADAM_STEP = 0.001, 0.9, 0.999, 1e-08, 0.01, 10
MESH = pl.DeviceIdType.MESH


def _cp(*sem):
    return pltpu.CompilerParams(dimension_semantics=sem, vmem_limit_bytes=VMEM_LIMIT)


def _pick(n, cap):
    best = 128
    for t in range(128, cap + 1, 128):
        if n % t == 0:
            best = t
    return best


def _mx(a, b):
    return jnp.dot(a.astype(MXU_DTYPE), b.astype(MXU_DTYPE), preferred_element_type=F32)


def _mx_nt(a, b):
    return lax.dot_general(a.astype(MXU_DTYPE), b.astype(MXU_DTYPE), (((1,), (1,)), ((), ())),
                           preferred_element_type=F32)


def _mx_tn(a, b):
    return lax.dot_general(a.astype(MXU_DTYPE), b.astype(MXU_DTYPE), (((0,), (0,)), ((), ())),
                           preferred_element_type=F32)


def _hi(a, b):
    return jnp.dot(a, b, precision=HI, preferred_element_type=F32)


def _hi_nt(a, b):
    return lax.dot_general(a, b, (((1,), (1,)), ((), ())), precision=HI, preferred_element_type=F32)


def _hi_tn(a, b):
    return lax.dot_general(a, b, (((0,), (0,)), ((), ())), precision=HI, preferred_element_type=F32)


def _sigmoid(x):
    return 0.5 * jnp.tanh(0.5 * x) + 0.5


def _softplus(x):
    return jnp.maximum(x, 0.0) + jnp.log(1.0 + jnp.exp(-jnp.abs(x)))


def mm_nn_res_norm(a, b, res, g, name, tm=512):
    t, k = a.shape
    d = b.shape[1]
    tm = min(tm, t)

    def body(a_ref, b_ref, res_ref, g_ref, y_ref, h_ref, ht_ref):
        y = res_ref[...] + _mx(a_ref[...], b_ref[...])
        y_ref[...] = y
        h = y * lax.rsqrt(jnp.mean(y * y, axis=-1, keepdims=True) + EPS) * g_ref[...]
        h_ref[...] = h.astype(h_ref.dtype)
        ht_ref[...] = h.T.astype(ht_ref.dtype)

    row = pl.BlockSpec((tm, d), lambda i: (i, 0))
    return pl.pallas_call(
        body, name=name, grid=(t // tm,),
        in_specs=[pl.BlockSpec((tm, k), lambda i: (i, 0)), pl.BlockSpec((k, d), lambda i: (0, 0)), row,
                  pl.BlockSpec((1, d), lambda i: (0, 0))],
        out_specs=(row, row, pl.BlockSpec((d, tm), lambda i: (0, i))),
        out_shape=(jax.ShapeDtypeStruct((t, d), F32), jax.ShapeDtypeStruct((t, d), MXU_DTYPE),
                   jax.ShapeDtypeStruct((d, t), MXU_DTYPE)),
        compiler_params=_cp("parallel"))(a, b, res, g)


def mm_nt(a, b, name, out_dtype=F32, tm=2048, after=None):
    m, k = a.shape
    n, _ = b.shape
    tn = _pick(n, 512 if k > 3000 else 1536)
    tm = min(tm, m)

    def body(a_ref, b_ref, *rest):
        o_ref = rest[-1]
        o_ref[...] = _mx_nt(a_ref[...], b_ref[...]).astype(o_ref.dtype)

    in_specs = [pl.BlockSpec((tm, k), lambda j, i: (i, 0)), pl.BlockSpec((tn, k), lambda j, i: (j, 0))]
    args = [a, b]
    if after is not None:
        in_specs.append(pl.BlockSpec(memory_space=pl.ANY))
        args.append(after)
    return pl.pallas_call(
        body, name=name, grid=(n // tn, m // tm), in_specs=in_specs,
        out_specs=pl.BlockSpec((tm, tn), lambda j, i: (i, j)),
        out_shape=jax.ShapeDtypeStruct((m, n), out_dtype), compiler_params=_cp("parallel", "parallel"))(*args)


def mm_at(at, b, name, tk=2048, transposed=False, tn=None, row_block=None):
    m, kk = at.shape
    _, n = b.shape
    tm, tn, tk = _pick(m, 1408), tn or _pick(n, 2816), min(tk, kk)
    nk = kk // tk

    def body(a_ref, b_ref, o_ref, acc_ref):
        k = pl.program_id(2)
        p = _mx(a_ref[...], b_ref[...])
        acc = jnp.where(k == 0, p, acc_ref[...] + p)
        acc_ref[...] = acc

        @pl.when(k == nk - 1)
        def _():
            o_ref[...] = (acc.T if transposed else acc).astype(o_ref.dtype)

    if transposed:
        rb = row_block or (lambda j: j)
        out_spec = pl.BlockSpec((tn, tm), lambda i, j, k: (rb(j), i))
        out_shape = jax.ShapeDtypeStruct((n, m), MXU_DTYPE)
    else:
        out_spec = pl.BlockSpec((tm, tn), lambda i, j, k: (i, j))
        out_shape = jax.ShapeDtypeStruct((m, n), MXU_DTYPE)
    return pl.pallas_call(
        body, name=name, grid=(m // tm, n // tn, nk),
        in_specs=[pl.BlockSpec((tm, tk), lambda i, j, k: (i, k)), pl.BlockSpec((tk, tn), lambda i, j, k: (k, j))],
        out_specs=out_spec, out_shape=out_shape, scratch_shapes=[pltpu.VMEM((tm, tn), F32)],
        compiler_params=_cp("parallel", "parallel", "arbitrary"))(at, b)


def rms_fwd(x, g, name, tm=512, after=None):
    t, d = x.shape

    def body(x_ref, g_ref, *rest):
        o_ref, ot_ref = rest[-2:]
        xv = x_ref[...]
        r = lax.rsqrt(jnp.mean(xv * xv, axis=-1, keepdims=True) + EPS)
        h = xv * r * g_ref[...]
        o_ref[...] = h.astype(o_ref.dtype)
        ot_ref[...] = h.T.astype(ot_ref.dtype)

    in_specs = [pl.BlockSpec((tm, d), lambda i: (i, 0)), pl.BlockSpec((1, d), lambda i: (0, 0))]
    args = [x, g]
    if after is not None:
        in_specs.append(pl.BlockSpec(memory_space=pl.ANY))
        args.append(after)
    return pl.pallas_call(
        body, name=name, grid=(t // tm,), in_specs=in_specs,
        out_specs=(pl.BlockSpec((tm, d), lambda i: (i, 0)), pl.BlockSpec((d, tm), lambda i: (0, i))),
        out_shape=(jax.ShapeDtypeStruct((t, d), MXU_DTYPE), jax.ShapeDtypeStruct((d, t), MXU_DTYPE)),
        compiler_params=_cp("parallel"))(*args)


def mm_rms_bwd(a, bt, x, g, dres, name, tm=512, after=None, chunks=None):
    t, k = a.shape
    d = bt.shape[1]
    tm = min(tm if k > 3000 else 2 * tm, t)
    chunks = chunks or ((0, 0, k),)

    def body(a_ref, b_ref, x_ref, g_ref, dres_ref, *rest):
        dx_ref, dg_ref = rest[-2:]
        dhv = None
        for ca, cb, size in chunks:
            part = _mx(a_ref[:, ca:ca + size], b_ref[cb:cb + size, :])
            dhv = part if dhv is None else dhv + part
        xv = x_ref[...]
        r = lax.rsqrt(jnp.mean(xv * xv, axis=-1, keepdims=True) + EPS)
        xh = xv * r
        dxh = dhv * g_ref[...]
        dx_ref[...] = dres_ref[...] + r * (dxh - xh * jnp.mean(dxh * xh, axis=-1, keepdims=True))
        part = jnp.sum(dhv * xh, axis=0, keepdims=True)
        dg_ref[...] = jnp.where(pl.program_id(0) == 0, part, dg_ref[...] + part)

    row = pl.BlockSpec((tm, d), lambda i: (i, 0))
    one = pl.BlockSpec((1, d), lambda i: (0, 0))
    in_specs = [pl.BlockSpec((tm, k), lambda i: (i, 0)), pl.BlockSpec((k, d), lambda i: (0, 0)), row, one, row]
    args = [a, bt, x, g, dres]
    if after is not None:
        in_specs.append(pl.BlockSpec(memory_space=pl.ANY))
        args.append(after)
    return pl.pallas_call(
        body, name=name, grid=(t // tm,), in_specs=in_specs, out_specs=(row, one),
        out_shape=(jax.ShapeDtypeStruct((t, d), F32), jax.ShapeDtypeStruct((1, d), F32)),
        compiler_params=_cp("arbitrary"))(*args)


GU_TILE = 1408


def ffn_up(f, wt, name, tm=1024):
    t, d = f.shape
    tm = min(tm, t)
    nj = D_FF // GU_TILE

    def body(f_ref, wg_ref, wu_ref, gu_ref, a_ref, at_ref):
        g = _mx_nt(f_ref[...], wg_ref[...])
        u = _mx_nt(f_ref[...], wu_ref[...])
        sg = _sigmoid(g)
        gs = g * sg
        gu_ref[:, :GU_TILE] = (u * (sg + gs - gs * sg)).astype(gu_ref.dtype)
        gu_ref[:, GU_TILE:] = gs.astype(gu_ref.dtype)
        act = gs * u
        a_ref[...] = act.astype(a_ref.dtype)
        at_ref[...] = act.T.astype(at_ref.dtype)

    return pl.pallas_call(
        body, name=name, grid=(nj, t // tm),
        in_specs=[pl.BlockSpec((tm, d), lambda j, i: (i, 0)), pl.BlockSpec((GU_TILE, d), lambda j, i: (j, 0)),
                  pl.BlockSpec((GU_TILE, d), lambda j, i: (nj + j, 0))],
        out_specs=(pl.BlockSpec((tm, 2 * GU_TILE), lambda j, i: (i, j)), pl.BlockSpec((tm, GU_TILE), lambda j, i: (i, j)),
                   pl.BlockSpec((GU_TILE, tm), lambda j, i: (j, i))),
        out_shape=(jax.ShapeDtypeStruct((t, 2 * D_FF), MXU_DTYPE), jax.ShapeDtypeStruct((t, D_FF), MXU_DTYPE),
                   jax.ShapeDtypeStruct((D_FF, t), MXU_DTYPE)),
        compiler_params=_cp("parallel", "parallel"))(f, wt, wt)


_GU_CHUNKS = tuple((q * GU_TILE, ((q % 2) * (D_FF // GU_TILE) + q // 2) * GU_TILE, GU_TILE)
                   for q in range(2 * D_FF // GU_TILE))


def ffn_dact(dy, w_d, gu, name, tm=1024, after=None):
    t, d = dy.shape
    tm = min(tm, t)

    def body(dy_ref, w_ref, gu_ref, *rest):
        o_ref = rest[-1]
        da = _mx_nt(dy_ref[...], w_ref[...])
        o_ref[:, :GU_TILE] = (da * gu_ref[:, :GU_TILE]).astype(o_ref.dtype)
        o_ref[:, GU_TILE:] = (da * gu_ref[:, GU_TILE:]).astype(o_ref.dtype)

    in_specs = [pl.BlockSpec((tm, d), lambda j, i: (i, 0)), pl.BlockSpec((GU_TILE, d), lambda j, i: (j, 0)),
                pl.BlockSpec((tm, 2 * GU_TILE), lambda j, i: (i, j))]
    args = [dy, w_d, gu]
    if after is not None:
        in_specs.append(pl.BlockSpec(memory_space=pl.ANY))
        args.append(after)
    return pl.pallas_call(
        body, name=name, grid=(D_FF // GU_TILE, t // tm), in_specs=in_specs,
        out_specs=pl.BlockSpec((tm, 2 * GU_TILE), lambda j, i: (i, j)),
        out_shape=jax.ShapeDtypeStruct((t, 2 * D_FF), MXU_DTYPE), compiler_params=_cp("parallel", "parallel"))(*args)


def mm_nn_res_loss(a, b, res, target, name, tm=512):
    t, k = a.shape
    d = b.shape[1]
    tm = min(tm, t)

    def body(a_ref, b_ref, res_ref, t_ref, dy_ref, l_ref):
        e = res_ref[...] + _mx(a_ref[...], b_ref[...]) - t_ref[...]
        dy_ref[...] = e * (1.0 / d)
        part = jnp.zeros((1, 128), F32) + 0.5 * jnp.sum(jnp.mean(e * e, axis=-1, keepdims=True), axis=0, keepdims=True)
        l_ref[...] = jnp.where(pl.program_id(0) == 0, part, l_ref[...] + part)

    row = pl.BlockSpec((tm, d), lambda i: (i, 0))
    return pl.pallas_call(
        body, name=name, grid=(t // tm,),
        in_specs=[pl.BlockSpec((tm, k), lambda i: (i, 0)), pl.BlockSpec((k, d), lambda i: (0, 0)), row, row],
        out_specs=(row, pl.BlockSpec((1, 128), lambda i: (0, 0))),
        out_shape=(jax.ShapeDtypeStruct((t, d), F32), jax.ShapeDtypeStruct((1, 128), F32)),
        compiler_params=_cp("arbitrary"))(a, b, res, target)


QK_W = Q_W + KV_W
_QK_TILE = 256


def _qk_mats():
    idx = np.arange(_QK_TILE)
    half = HEAD_DIM // 2
    same = (idx[:, None] // HEAD_DIM) == (idx[None, :] // HEAD_DIM)
    lo = (idx % HEAD_DIM) < half
    rot = np.where((idx[:, None] == idx[None, :] + half) & lo[None, :], -1.0, 0.0)
    rot = rot + np.where((idx[:, None] == idx[None, :] - half) & ~lo[None, :], 1.0, 0.0)
    return jnp.asarray(same, F32), jnp.asarray(rot, F32)


def _qk_rows(q_gain, k_gain, cosf, sinf):
    gain = jnp.concatenate([q_gain] * ATTN_HEADS + [k_gain] * KV_HEADS, axis=-1)
    return gain, jnp.concatenate([cosf, cosf], axis=-1), jnp.concatenate([sinf, sinf], axis=-1)


def _qk_tiles(a, mat, transposed=False):
    outs = []
    for c0 in range(0, QK_W, _QK_TILE):
        w = min(_QK_TILE, QK_W - c0)
        mt = (mat.T if transposed else mat)[:w, :w].astype(MXU_DTYPE)
        at = a[:, c0:c0 + w]
        hi = at.astype(MXU_DTYPE)
        lo = (at - hi.astype(F32)).astype(MXU_DTYPE)
        outs.append(jnp.dot(hi, mt, preferred_element_type=F32) + jnp.dot(lo, mt, preferred_element_type=F32))
    return jnp.concatenate(outs, axis=-1)


def qk_prep_fwd(proj, q_gain, k_gain, cosf, sinf, name, tm=256):
    t = proj.shape[0]
    gmat, rmat = _qk_mats()
    gain, c2, s2 = _qk_rows(q_gain, k_gain, cosf, sinf)
    rep = QK_W // 128

    def body(p_ref, g_ref, c_ref, s_ref, gm_ref, rm_ref, q_ref, k_ref):
        x = p_ref[...]
        r = lax.rsqrt(_qk_tiles(x * x, gm_ref[...]) * (1.0 / HEAD_DIM) + EPS)
        xn = x * r * g_ref[...]
        c = jnp.concatenate([c_ref[...]] * rep, axis=-1)
        s = jnp.concatenate([s_ref[...]] * rep, axis=-1)
        out = xn * c + _qk_tiles(xn, rm_ref[...]) * s
        q_ref[...] = out[:, :Q_W]
        k_ref[...] = out[:, Q_W:]

    full = pl.BlockSpec((_QK_TILE, _QK_TILE), lambda i: (0, 0))
    tab = pl.BlockSpec((tm, 128), lambda i: (i, 0))
    return pl.pallas_call(
        body, name=name, grid=(t // tm,),
        in_specs=[pl.BlockSpec((tm, QK_W), lambda i: (i, 0)), pl.BlockSpec((1, QK_W), lambda i: (0, 0)), tab, tab,
                  full, full],
        out_specs=(pl.BlockSpec((tm, Q_W), lambda i: (i, 0)), pl.BlockSpec((tm, KV_W), lambda i: (i, 0))),
        out_shape=(jax.ShapeDtypeStruct((t, Q_W), F32), jax.ShapeDtypeStruct((t, KV_W), F32)),
        compiler_params=_cp("parallel"))(proj, gain, c2, s2, gmat, rmat)


def qk_prep_bwd(proj, q_gain, k_gain, cosf, sinf, dq, dk, name, tm=256):
    t = proj.shape[0]
    gmat, rmat = _qk_mats()
    gain, c2, s2 = _qk_rows(q_gain, k_gain, cosf, sinf)
    rep = QK_W // 128
    lanes = np.arange(QK_W)[:, None]
    fold = jnp.asarray(lanes % HEAD_DIM + np.where(lanes >= Q_W, HEAD_DIM, 0) == np.arange(128)[None, :], F32)

    def body(p_ref, g_ref, c_ref, s_ref, gm_ref, rm_ref, f_ref, dq_ref, dk_ref, o_ref, dg_ref):
        x = p_ref[...]
        r = lax.rsqrt(_qk_tiles(x * x, gm_ref[...]) * (1.0 / HEAD_DIM) + EPS)
        xh = x * r
        c = jnp.concatenate([c_ref[...]] * rep, axis=-1)
        s = jnp.concatenate([s_ref[...]] * rep, axis=-1)
        dout = jnp.concatenate([dq_ref[...], dk_ref[...]], axis=-1)
        dxn = dout * c + _qk_tiles(dout * s, rm_ref[...], transposed=True)
        part = _hi(jnp.sum(dxn * xh, axis=0, keepdims=True), f_ref[...])
        dxh = dxn * g_ref[...]
        mean = _qk_tiles(dxh * xh, gm_ref[...]) * (1.0 / HEAD_DIM)
        o_ref[...] = (r * (dxh - xh * mean)).astype(o_ref.dtype)
        dg_ref[...] = jnp.where(pl.program_id(0) == 0, part, dg_ref[...] + part)

    full = pl.BlockSpec((_QK_TILE, _QK_TILE), lambda i: (0, 0))
    tab = pl.BlockSpec((tm, 128), lambda i: (i, 0))
    dqk, dg = pl.pallas_call(
        body, name=name, grid=(t // tm,),
        in_specs=[pl.BlockSpec((tm, QK_W), lambda i: (i, 0)), pl.BlockSpec((1, QK_W), lambda i: (0, 0)), tab, tab,
                  full, full, pl.BlockSpec((QK_W, 128), lambda i: (0, 0)),
                  pl.BlockSpec((tm, Q_W), lambda i: (i, 0)), pl.BlockSpec((tm, KV_W), lambda i: (i, 0))],
        out_specs=(pl.BlockSpec((tm, QK_W), lambda i: (i, 0)), pl.BlockSpec((1, 128), lambda i: (0, 0))),
        out_shape=(jax.ShapeDtypeStruct((t, QK_W), MXU_DTYPE), jax.ShapeDtypeStruct((1, 128), F32)),
        compiler_params=_cp("arbitrary"))(proj, gain, c2, s2, gmat, rmat, fold, dq, dk)
    return dqk, dg[:, :HEAD_DIM], dg[:, HEAD_DIM:]


def _swa_valid(n, grp):
    qi = lax.broadcasted_iota(jnp.int32, (grp * ATTN_BLOCK, 2 * ATTN_BLOCK), 0) & (ATTN_BLOCK - 1)
    kj = lax.broadcasted_iota(jnp.int32, (grp * ATTN_BLOCK, 2 * ATTN_BLOCK), 1)
    diff = qi + ATTN_BLOCK - kj
    return (diff >= 0) & (diff < ATTN_BLOCK) & (n * ATTN_BLOCK - ATTN_BLOCK + kj >= 0)


def _stack_heads(ref, g, grp, rows=slice(None)):
    return jnp.concatenate([ref[rows, (g * grp + j) * HEAD_DIM:(g * grp + j + 1) * HEAD_DIM] for j in range(grp)], axis=0)


def _stack_sinks(s_ref, g, grp):
    return jnp.concatenate([jnp.zeros((ATTN_BLOCK, 1), F32) + s_ref[0:1, g * grp + j:g * grp + j + 1]
                            for j in range(grp)], axis=0)


SWA_STEP = 2


def swa_fwd(q, k, proj, sinks, name):
    t = q.shape[0]
    nb = t // ATTN_BLOCK
    scale = HEAD_DIM ** -0.5
    grp = ATTN_HEADS // KV_HEADS

    rows = SWA_STEP * ATTN_BLOCK

    def body(q_ref, kc_ref, kp_ref, vc_ref, vp_ref, s_ref, y_ref, mix_ref, yt_ref, lse_ref):
        n0 = pl.program_id(0) * SWA_STEP
        kk = jnp.concatenate([kp_ref[...], kc_ref[...]], axis=0).astype(MXU_DTYPE)
        vv = jnp.concatenate([vp_ref[...], vc_ref[...]], axis=0).astype(MXU_DTYPE)
        lane = lax.broadcasted_iota(jnp.int32, (ATTN_BLOCK, ATTN_HEADS), 1)
        units = [(b, g) for b in range(SWA_STEP) for g in range(KV_HEADS)]
        blk = lambda b: slice(b * ATTN_BLOCK, (b + 1) * ATTN_BLOCK)
        keys = lambda b: slice(b * ATTN_BLOCK, (b + 2) * ATTN_BLOCK)
        col = lambda g: slice(g * HEAD_DIM, (g + 1) * HEAD_DIM)
        valid = [_swa_valid(n0 + b, grp) for b in range(SWA_STEP)]
        qg = [_stack_heads(q_ref, g, grp, blk(b)) for b, g in units]
        sink = [_stack_sinks(s_ref, g, grp) for b, g in units]
        sc = [jnp.where(valid[b], _mx_nt(qg[u], kk[keys(b), col(g)]) * scale, NEG) for u, (b, g) in enumerate(units)]
        m = [jnp.maximum(jnp.max(sc_, axis=-1, keepdims=True), sk) for sc_, sk in zip(sc, sink)]
        e = [jnp.exp(sc_ - m_) for sc_, m_ in zip(sc, m)]
        den = [jnp.sum(e_, axis=-1, keepdims=True) + jnp.exp(sk - m_) for e_, sk, m_ in zip(e, sink, m)]
        og = [_mx(e[u] / den[u], vv[keys(b), col(g)]) for u, (b, g) in enumerate(units)]
        lg = [m_ + jnp.log(d_) for m_, d_ in zip(m, den)]
        for b in range(SWA_STEP):
            lse = jnp.zeros((ATTN_BLOCK, ATTN_HEADS), F32)
            outs = []
            for h in range(ATTN_HEADS):
                u = b * KV_HEADS + h // grp
                sub = blk(h % grp)
                outs.append(og[u][sub])
                lse = jnp.where(lane == h, lg[u][sub], lse)
            y = jnp.concatenate(outs, axis=-1)
            y_ref[blk(b), :] = y
            mix_ref[blk(b), :] = y.astype(mix_ref.dtype)
            yt_ref[:, blk(b)] = y.T.astype(yt_ref.dtype)
            lse_ref[blk(b), :] = lse

    cur = lambda n: (n, 0)
    prev = lambda n: (jnp.maximum(n * SWA_STEP - 1, 0), 0)
    vcol = (Q_W + KV_W) // KV_W
    return pl.pallas_call(
        body, name=name, grid=(nb // SWA_STEP,),
        in_specs=[pl.BlockSpec((rows, Q_W), cur), pl.BlockSpec((rows, KV_W), cur),
                  pl.BlockSpec((ATTN_BLOCK, KV_W), prev),
                  pl.BlockSpec((rows, KV_W), lambda n: (n, vcol)),
                  pl.BlockSpec((ATTN_BLOCK, KV_W), lambda n: (jnp.maximum(n * SWA_STEP - 1, 0), vcol)),
                  pl.BlockSpec((1, ATTN_HEADS), lambda n: (0, 0))],
        out_specs=(pl.BlockSpec((rows, Q_W), cur), pl.BlockSpec((rows, Q_W), cur),
                   pl.BlockSpec((Q_W, rows), lambda n: (0, n)), pl.BlockSpec((rows, ATTN_HEADS), cur)),
        out_shape=(jax.ShapeDtypeStruct((t, Q_W), F32), jax.ShapeDtypeStruct((t, Q_W + CONV_CH), MXU_DTYPE),
                   jax.ShapeDtypeStruct((Q_W + CONV_CH, t), MXU_DTYPE), jax.ShapeDtypeStruct((t, ATTN_HEADS), F32)),
        compiler_params=_cp("parallel"))(q, k, k, proj, proj, sinks)


def swa_bwd(q, k, proj, sinks, y, lse, dmix, name):
    t = q.shape[0]
    nb = t // ATTN_BLOCK
    scale = HEAD_DIM ** -0.5
    grp = ATTN_HEADS // KV_HEADS

    def body(q_ref, kc_ref, kp_ref, vc_ref, vp_ref, s_ref, y_ref, lse_ref, dy_ref,
             dq_ref, dk_ref, dv_ref, ds_ref, dkc, dvc):
        n = pl.program_id(0)

        @pl.when(n == 0)
        def _():
            dkc[...] = jnp.zeros_like(dkc)
            dvc[...] = jnp.zeros_like(dvc)
            ds_ref[...] = jnp.zeros_like(ds_ref)

        @pl.when(n < nb)
        def _():
            valid = _swa_valid(n, grp)
            kk = jnp.concatenate([kp_ref[...], kc_ref[...]], axis=0).astype(MXU_DTYPE)
            vv = jnp.concatenate([vp_ref[...], vc_ref[...]], axis=0).astype(MXU_DTYPE)
            lane = lax.broadcasted_iota(jnp.int32, (1, ATTN_HEADS), 1)
            gs = range(KV_HEADS)
            kg = [kk[:, g * HEAD_DIM:(g + 1) * HEAD_DIM] for g in gs]
            vg = [vv[:, g * HEAD_DIM:(g + 1) * HEAD_DIM] for g in gs]
            qg = [_stack_heads(q_ref, g, grp).astype(MXU_DTYPE) for g in gs]
            dog = [_stack_heads(dy_ref, g, grp) for g in gs]
            og = [_stack_heads(y_ref, g, grp) for g in gs]
            lg = [jnp.concatenate([lse_ref[:, g * grp + j:g * grp + j + 1] for j in range(grp)], axis=0) for g in gs]
            sink = [_stack_sinks(s_ref, g, grp) for g in gs]
            sc = [jnp.where(valid, _mx_nt(qg[g], kg[g]) * scale, NEG) for g in gs]
            p = [jnp.exp(sc[g] - lg[g]) for g in gs]
            delta = [jnp.sum(dog[g] * og[g], axis=-1, keepdims=True) for g in gs]
            ds = [p[g] * (_mx_nt(dog[g], vg[g]) - delta[g]) for g in gs]
            dqg = [_mx(ds[g], kg[g]) * scale for g in gs]
            dkf = jnp.concatenate([_mx_tn(ds[g], qg[g]) * scale for g in gs], axis=-1)
            dvf = jnp.concatenate([_mx_tn(p[g], dog[g]) for g in gs], axis=-1)
            dsk = [jnp.exp(sink[g] - lg[g]) * delta[g] for g in gs]
            dsink = jnp.zeros((1, ATTN_HEADS), F32)
            dqs = []
            for h in range(ATTN_HEADS):
                rows = slice((h % grp) * ATTN_BLOCK, (h % grp + 1) * ATTN_BLOCK)
                dqs.append(dqg[h // grp][rows])
                dsink = jnp.where(lane == h, -jnp.sum(dsk[h // grp][rows], axis=0, keepdims=True), dsink)
            dq_ref[...] = jnp.concatenate(dqs, axis=-1)
            dk_ref[...] = dkc[...] + dkf[:ATTN_BLOCK]
            dv_ref[...] = (dvc[...] + dvf[:ATTN_BLOCK]).astype(dv_ref.dtype)
            dkc[...] = dkf[ATTN_BLOCK:]
            dvc[...] = dvf[ATTN_BLOCK:]
            ds_ref[...] += dsink

        @pl.when(n == nb)
        def _():
            dk_ref[...] = dkc[...]
            dv_ref[...] = dvc[...].astype(dv_ref.dtype)

    cur = lambda n: (jnp.minimum(n, nb - 1), 0)
    prev = lambda n: (jnp.clip(n - 1, 0, nb - 1), 0)
    vcol = (Q_W + KV_W) // KV_W
    return pl.pallas_call(
        body, name=name, grid=(nb + 1,),
        in_specs=[pl.BlockSpec((ATTN_BLOCK, Q_W), cur), pl.BlockSpec((ATTN_BLOCK, KV_W), cur),
                  pl.BlockSpec((ATTN_BLOCK, KV_W), prev),
                  pl.BlockSpec((ATTN_BLOCK, KV_W), lambda n: (jnp.minimum(n, nb - 1), vcol)),
                  pl.BlockSpec((ATTN_BLOCK, KV_W), lambda n: (jnp.clip(n - 1, 0, nb - 1), vcol)),
                  pl.BlockSpec((1, ATTN_HEADS), lambda n: (0, 0)),
                  pl.BlockSpec((ATTN_BLOCK, Q_W), cur), pl.BlockSpec((ATTN_BLOCK, ATTN_HEADS), cur),
                  pl.BlockSpec((ATTN_BLOCK, Q_W), cur)],
        out_specs=(pl.BlockSpec((ATTN_BLOCK, Q_W), cur), pl.BlockSpec((ATTN_BLOCK, KV_W), prev),
                   pl.BlockSpec((ATTN_BLOCK, KV_W), prev), pl.BlockSpec((1, ATTN_HEADS), lambda n: (0, 0))),
        out_shape=(jax.ShapeDtypeStruct((t, Q_W), F32), jax.ShapeDtypeStruct((t, KV_W), F32),
                   jax.ShapeDtypeStruct((t, KV_W), MXU_DTYPE), jax.ShapeDtypeStruct((1, ATTN_HEADS), F32)),
        scratch_shapes=[pltpu.VMEM((ATTN_BLOCK, KV_W), F32), pltpu.VMEM((ATTN_BLOCK, KV_W), F32)],
        compiler_params=_cp("arbitrary"))(q, k, k, proj, proj, sinks, y, lse, dmix)


GC_W = 256
_GB0, _GC0, _XI0 = 768 // GC_W, 1280 // GC_W, 1792 // GC_W
HALO = 8


def gconv_fwd(proj, conv_w, mix, mix_t, name, tm=512):
    t = proj.shape[0]
    hb = tm // HALO
    half = Q_W // GC_W

    def body(gb_ref, gc_ref, xi_ref, gch_ref, xih_ref, w_ref, mix_in, mixt_in, y_ref, yt_ref):
        i = pl.program_id(1)
        u = gc_ref[...] * xi_ref[...]
        uh = jnp.where(i == 0, 0.0, gch_ref[...] * xih_ref[...])
        up = jnp.concatenate([uh, u], axis=0)
        cv = w_ref[0:1, :] * up[HALO - 2:HALO - 2 + tm]
        cv = cv + w_ref[1:2, :] * up[HALO - 1:HALO - 1 + tm]
        cv = cv + w_ref[2:3, :] * u
        y = gb_ref[...] * cv
        y_ref[...] = y.astype(y_ref.dtype)
        yt_ref[...] = y.T.astype(yt_ref.dtype)

    def col(c0):
        return pl.BlockSpec((tm, GC_W), lambda cj, i: (i, c0 + cj))

    def halo(c0):
        return pl.BlockSpec((HALO, GC_W), lambda cj, i: (jnp.maximum(i * hb - 1, 0), c0 + cj))

    return pl.pallas_call(
        body, name=name, grid=(CONV_CH // GC_W, t // tm),
        in_specs=[col(_GB0), col(_GC0), col(_XI0), halo(_GC0), halo(_XI0),
                  pl.BlockSpec((3, GC_W), lambda cj, i: (0, cj)),
                  pl.BlockSpec(memory_space=pl.ANY), pl.BlockSpec(memory_space=pl.ANY)],
        out_specs=(pl.BlockSpec((tm, GC_W), lambda cj, i: (i, half + cj)),
                   pl.BlockSpec((GC_W, tm), lambda cj, i: (half + cj, i))),
        out_shape=(jax.ShapeDtypeStruct(mix.shape, mix.dtype), jax.ShapeDtypeStruct(mix_t.shape, mix_t.dtype)),
        input_output_aliases={6: 0, 7: 1},
        compiler_params=_cp("parallel", "parallel"))(proj, proj, proj, proj, proj, conv_w, mix, mix_t)


def gconv_bwd(proj, conv_w, dmix, name, tm=512):
    t = proj.shape[0]
    hb = tm // HALO
    nt = t // tm
    dy0 = Q_W // GC_W

    def body(gb_ref, gc_ref, xi_ref, gch_ref, xih_ref, gbn_ref, dyn_ref, dy_ref, w_ref,
             dgb_ref, dgc_ref, dxi_ref, dw_ref):
        i = pl.program_id(1)
        gc, xi, gb, dy = gc_ref[...], xi_ref[...], gb_ref[...], dy_ref[...]
        u = gc * xi
        uh = jnp.where(i == 0, 0.0, gch_ref[...] * xih_ref[...])
        up = jnp.concatenate([uh, u], axis=0)
        u2 = up[HALO - 2:HALO - 2 + tm]
        u1 = up[HALO - 1:HALO - 1 + tm]
        cv = w_ref[0:1, :] * u2 + w_ref[1:2, :] * u1 + w_ref[2:3, :] * u
        dgb_ref[...] = (dy * cv).astype(dgb_ref.dtype)
        dcv = dy * gb
        dcvn = jnp.where(i == nt - 1, 0.0, dyn_ref[...] * gbn_ref[...])
        dcvp = jnp.concatenate([dcv, dcvn], axis=0)
        du = w_ref[0:1, :] * dcvp[2:2 + tm] + w_ref[1:2, :] * dcvp[1:1 + tm] + w_ref[2:3, :] * dcv
        dgc_ref[...] = (du * xi).astype(dgc_ref.dtype)
        dxi_ref[...] = (du * gc).astype(dxi_ref.dtype)
        dw = jnp.concatenate([jnp.sum(dcv * u2, axis=0, keepdims=True), jnp.sum(dcv * u1, axis=0, keepdims=True),
                              jnp.sum(dcv * u, axis=0, keepdims=True)], axis=0)

        @pl.when(i == 0)
        def _():
            dw_ref[...] = dw

        @pl.when(i > 0)
        def _():
            dw_ref[...] += dw

    def col(c0):
        return pl.BlockSpec((tm, GC_W), lambda cj, i: (i, c0 + cj))

    def halo(c0):
        return pl.BlockSpec((HALO, GC_W), lambda cj, i: (jnp.maximum(i * hb - 1, 0), c0 + cj))

    def nxt(c0):
        return pl.BlockSpec((HALO, GC_W), lambda cj, i: (jnp.minimum((i + 1) * hb, t // HALO - 1), c0 + cj))

    out = pl.BlockSpec((tm, GC_W), lambda cj, i: (i, cj))
    return pl.pallas_call(
        body, name=name, grid=(CONV_CH // GC_W, nt),
        in_specs=[col(_GB0), col(_GC0), col(_XI0), halo(_GC0), halo(_XI0), nxt(_GB0), nxt(dy0), col(dy0),
                  pl.BlockSpec((3, GC_W), lambda cj, i: (0, cj))],
        out_specs=(out, out, out, pl.BlockSpec((3, GC_W), lambda cj, i: (0, cj))),
        out_shape=(jax.ShapeDtypeStruct((t, CONV_CH), MXU_DTYPE),) * 3 + (jax.ShapeDtypeStruct((3, CONV_CH), F32),),
        compiler_params=_cp("parallel", "arbitrary"))(proj, proj, proj, proj, proj, proj, dmix, dmix, conv_w)


_QKV_W = 3 * DN_W
_BA_COL = (4 * DN_W) // 128
_Z_COL = _QKV_W // DN_W


def gdn_prep_fwd(proj, conv_w, alog_row, dtb_row, name, tm=256):
    t = proj.shape[0]
    hb = tm // HALO
    qscale = DN_DIM ** -0.5

    def body(x_ref, xh_ref, w_ref, ba_ref, al_ref, dt_ref, q_ref, k_ref, v_ref, bg_ref, c_ref):
        i = pl.program_id(0)
        for gi in range(3 * DN_HEADS):
            sl = slice(gi * DN_DIM, (gi + 1) * DN_DIM)
            xp = jnp.concatenate([jnp.where(i == 0, 0.0, xh_ref[:, sl]), x_ref[:, sl]], axis=0)
            c = w_ref[0:1, sl] * xp[HALO - 3:HALO - 3 + tm]
            for j in range(1, 4):
                c = c + w_ref[j:j + 1, sl] * xp[HALO - 3 + j:HALO - 3 + j + tm]
            c_ref[:, sl] = c
            s = c * _sigmoid(c)
            osl = slice((gi % DN_HEADS) * DN_DIM, (gi % DN_HEADS + 1) * DN_DIM)
            if gi < DN_HEADS:
                q_ref[:, osl] = s * lax.rsqrt(jnp.sum(s * s, axis=-1, keepdims=True) + EPS) * qscale
            elif gi < 2 * DN_HEADS:
                k_ref[:, osl] = s * lax.rsqrt(jnp.sum(s * s, axis=-1, keepdims=True) + EPS)
            else:
                v_ref[:, osl] = s
        ba = ba_ref[...]
        lane = lax.broadcasted_iota(jnp.int32, ba.shape, 1)
        gval = -jnp.exp(al_ref[...]) * _softplus(ba + dt_ref[...])
        bg_ref[...] = jnp.where(lane < DN_HEADS, _sigmoid(ba), jnp.where(lane < 2 * DN_HEADS, gval, 0.0))

    row = pl.BlockSpec((tm, DN_W), lambda i: (i, 0))
    one = pl.BlockSpec((1, 128), lambda i: (0, 0))
    return pl.pallas_call(
        body, name=name, grid=(t // tm,),
        in_specs=[pl.BlockSpec((tm, _QKV_W), lambda i: (i, 0)),
                  pl.BlockSpec((HALO, _QKV_W), lambda i: (jnp.maximum(i * hb - 1, 0), 0)),
                  pl.BlockSpec((4, _QKV_W), lambda i: (0, 0)),
                  pl.BlockSpec((tm, 128), lambda i: (i, _BA_COL)), one, one],
        out_specs=(row, row, row, pl.BlockSpec((tm, 128), lambda i: (i, 0)), pl.BlockSpec((tm, _QKV_W), lambda i: (i, 0))),
        out_shape=(jax.ShapeDtypeStruct((t, DN_W), F32),) * 3 + (jax.ShapeDtypeStruct((t, 128), F32),
                                                                 jax.ShapeDtypeStruct((t, _QKV_W), F32)),
        compiler_params=_cp("parallel"))(proj, proj, conv_w, proj, alog_row, dtb_row)


def gdn_prep_bwd(proj, conv, conv_w, alog_row, dtb_row, dq, dk, dv, dbg, dz, name, tm=256):
    t = proj.shape[0]
    hb = tm // HALO
    nt = t // tm
    qscale = DN_DIM ** -0.5
    te = tm + HALO

    def body(x_ref, c_ref, cn_ref, w_ref, ba_ref, al_ref, dt_ref, dq_ref, dk_ref, dv_ref,
             dqn_ref, dkn_ref, dvn_ref, dbg_ref, dz_ref, dx_ref, dw_ref, ddt_ref, dal_ref):
        i = pl.program_id(0)
        first = i == 0
        last = i == nt - 1
        dws = []
        for gi in range(3 * DN_HEADS):
            sl = slice(gi * DN_DIM, (gi + 1) * DN_DIM)
            osl = slice((gi % DN_HEADS) * DN_DIM, (gi % DN_HEADS + 1) * DN_DIM)
            c = jnp.concatenate([c_ref[:, sl], cn_ref[:, sl]], axis=0)
            sg = _sigmoid(c)
            s = c * sg
            d_ref, dn_ref = ((dq_ref, dqn_ref), (dk_ref, dkn_ref), (dv_ref, dvn_ref))[gi // DN_HEADS]
            dy = jnp.concatenate([d_ref[:, osl], jnp.where(last, 0.0, dn_ref[:, osl])], axis=0)
            if gi < 2 * DN_HEADS:
                r = lax.rsqrt(jnp.sum(s * s, axis=-1, keepdims=True) + EPS)
                sh = s * r
                ds = r * (dy - sh * jnp.sum(sh * dy, axis=-1, keepdims=True))
                if gi < DN_HEADS:
                    ds = ds * qscale
            else:
                ds = dy
            dc = ds * sg * (1.0 + c * (1.0 - sg))
            dcs = [dc[3 - j:3 - j + tm] for j in range(4)]
            dx = w_ref[0:1, sl] * dcs[0]
            for j in range(1, 4):
                dx = dx + w_ref[j:j + 1, sl] * dcs[j]
            dx_ref[:, sl] = dx.astype(dx_ref.dtype)
            x0 = x_ref[:, sl]
            dws.append(jnp.concatenate([jnp.sum(dcs[j] * x0, axis=0, keepdims=True) for j in range(4)], axis=0))
        dw = jnp.concatenate(dws, axis=-1)
        ba = ba_ref[...]
        dbgv = dbg_ref[...]
        lane = lax.broadcasted_iota(jnp.int32, ba.shape, 1)
        beta = _sigmoid(ba)
        ea = -jnp.exp(al_ref[...])
        zin = ba + dt_ref[...]
        is_b = lane < DN_HEADS
        is_a = (lane >= DN_HEADS) & (lane < 2 * DN_HEADS)
        da = jnp.where(is_a, dbgv * ea * _sigmoid(zin), 0.0)
        dx_ref[:, _QKV_W:_QKV_W + DN_W] = dz_ref[...]
        dx_ref[:, _QKV_W + DN_W:] = jnp.where(is_b, dbgv * beta * (1.0 - beta), da).astype(dx_ref.dtype)
        ddt = jnp.sum(da, axis=0, keepdims=True)
        dal = jnp.sum(jnp.where(is_a, dbgv * ea * _softplus(zin), 0.0), axis=0, keepdims=True)

        @pl.when(first)
        def _():
            dw_ref[...] = dw
            ddt_ref[...] = ddt
            dal_ref[...] = dal

        @pl.when(i > 0)
        def _():
            dw_ref[...] += dw
            ddt_ref[...] += ddt
            dal_ref[...] += dal

    row = pl.BlockSpec((tm, DN_W), lambda i: (i, 0))
    nrow = pl.BlockSpec((HALO, DN_W), lambda i: (jnp.minimum((i + 1) * hb, t // HALO - 1), 0))
    one = pl.BlockSpec((1, 128), lambda i: (0, 0))
    return pl.pallas_call(
        body, name=name, grid=(nt,),
        in_specs=[pl.BlockSpec((tm, _QKV_W), lambda i: (i, 0)), pl.BlockSpec((tm, _QKV_W), lambda i: (i, 0)),
                  pl.BlockSpec((HALO, _QKV_W), lambda i: (jnp.minimum((i + 1) * hb, t // HALO - 1), 0)),
                  pl.BlockSpec((4, _QKV_W), lambda i: (0, 0)),
                  pl.BlockSpec((tm, 128), lambda i: (i, _BA_COL)), one, one,
                  row, row, row, nrow, nrow, nrow, pl.BlockSpec((tm, 128), lambda i: (i, 0)), row],
        out_specs=(pl.BlockSpec((tm, ODD_IN_PAD), lambda i: (i, 0)), pl.BlockSpec((4, _QKV_W), lambda i: (0, 0)), one, one),
        out_shape=(jax.ShapeDtypeStruct((t, ODD_IN_PAD), MXU_DTYPE), jax.ShapeDtypeStruct((4, _QKV_W), F32),
                   jax.ShapeDtypeStruct((1, 128), F32), jax.ShapeDtypeStruct((1, 128), F32)),
        compiler_params=_cp("arbitrary"))(proj, conv, conv, conv_w, proj, alog_row, dtb_row, dq, dk, dv, dq, dk, dv, dbg,
                                          dz)


def _chunk_masks():
    r = lax.broadcasted_iota(jnp.int32, (DN_CHUNK, DN_CHUNK), 0)
    c = lax.broadcasted_iota(jnp.int32, (DN_CHUNK, DN_CHUNK), 1)
    return r >= c, r > c


INV_PACK = 2


def _inv_unit_lower_many(mats):
    n = DN_CHUNK
    wide = INV_PACK * n
    r = lax.broadcasted_iota(jnp.int32, (wide, wide), 0)
    c = lax.broadcasted_iota(jnp.int32, (wide, wide), 1)
    same = (r & -n) == (c & -n)
    eye = jnp.where((r[:n] == (c[:n] & (n - 1))), 1.0, 0.0)

    def blockdiag(row):
        return jnp.where(same, jnp.concatenate([row] * INV_PACK, axis=0), 0.0)

    packs = [jnp.concatenate(mats[g:g + INV_PACK], axis=-1) for g in range(0, len(mats), INV_PACK)]
    xs = [eye - a for a in packs]
    pws = [_hi(a, blockdiag(a)) for a in packs]
    for step in range(5):
        if step < 4:
            both = [_hi(jnp.concatenate([x, pw], axis=0), blockdiag(pw)) for x, pw in zip(xs, pws)]
            xs = [x + b[:n] for x, b in zip(xs, both)]
            pws = [b[n:] for b in both]
        else:
            xs = [x + _hi(x, blockdiag(pw)) for x, pw in zip(xs, pws)]
    return [x[:, j * n:(j + 1) * n] for x in xs for j in range(INV_PACK)]


def _chunk_common(q, k, beta, gc, gcr, lower):
    gam = jnp.exp(jnp.where(lower, gc - gcr, NEG))
    eg = jnp.exp(gc)
    gl = gc[DN_CHUNK - 1:DN_CHUNK, :]
    kdf = jnp.exp(gl - gc)
    kb = k * beta
    bmat = _mx_nt(kb, k)
    qmat = _mx_nt(q, k)
    return gam, eg, jnp.exp(gl), kdf, kb, bmat, qmat


DN_STEP = 4


def gdn_fwd(q, k, v, bg, proj, o_gain, name):
    t = q.shape[0]
    n_chunks = t // DN_CHUNK

    def body(q_ref, k_ref, v_ref, bg_ref, z_ref, g_ref, o_ref, sall_ref, tall_ref, y_ref, yt_ref, s_ref):
        n = pl.program_id(0)

        @pl.when(n == 0)
        def _():
            s_ref[...] = jnp.zeros_like(s_ref)

        lower, strict = _chunk_masks()
        ltri = jnp.where(lower, 1.0, 0.0)
        hs = range(DN_HEADS)
        sl = [slice(h * DN_DIM, (h + 1) * DN_DIM) for h in hs]
        units = [(c, h) for c in range(DN_STEP) for h in hs]
        nu = range(len(units))
        rs = [slice(c * DN_CHUNK, (c + 1) * DN_CHUNK) for c in range(DN_STEP)]
        bgv = [bg_ref[rs[c], :] for c in range(DN_STEP)]
        gcs = [_hi(ltri, b) for b in bgv]
        gcs_t = [g.T for g in gcs]
        qh = [q_ref[rs[c], sl[h]] for c, h in units]
        kh = [k_ref[rs[c], sl[h]] for c, h in units]
        vh = [v_ref[rs[c], sl[h]] for c, h in units]
        beta = [bgv[c][:, h:h + 1] for c, h in units]
        com = [_chunk_common(qh[u], kh[u], beta[u], gcs[c][:, DN_HEADS + h:DN_HEADS + h + 1],
                             gcs_t[c][DN_HEADS + h:DN_HEADS + h + 1, :], lower) for u, (c, h) in enumerate(units)]
        gam, eg, dec, kdf, kb, bmat, qmat = zip(*com)
        tms = _inv_unit_lower_many([jnp.where(strict, bmat[u] * gam[u], 0.0) for u in nu])
        for u, (c, h) in enumerate(units):
            tall_ref[c, h] = tms[u]
        uw = [_hi(tms[u], jnp.concatenate([vh[u] * beta[u], kb[u] * eg[u]], axis=-1)) for u in nu]
        qd = [qh[u] * eg[u] for u in nu]
        pm = [qmat[u] * gam[u] for u in nu]
        kd = [kh[u] * kdf[u] for u in nu]
        st = [s_ref[h] for h in hs]
        for c in range(DN_STEP):
            us = [c * DN_HEADS + h for h in hs]
            for h in hs:
                sall_ref[c, h] = st[h]
            v_new = [uw[us[h]][:, :DN_DIM] - _mx(uw[us[h]][:, DN_DIM:], st[h]) for h in hs]
            o_st = [_mx(qd[us[h]], st[h]) for h in hs]
            o_in = [_mx(pm[us[h]], v_new[h]) for h in hs]
            s_up = [_mx_tn(kd[us[h]], v_new[h]) for h in hs]
            for h in hs:
                ov = o_st[h] + o_in[h]
                o_ref[rs[c], sl[h]] = ov
                zv = z_ref[rs[c], sl[h]]
                y = ov * lax.rsqrt(jnp.mean(ov * ov, axis=-1, keepdims=True) + EPS) * g_ref[...] * (zv * _sigmoid(zv))
                y_ref[rs[c], sl[h]] = y.astype(y_ref.dtype)
                yt_ref[sl[h], rs[c]] = y.T.astype(yt_ref.dtype)
            st = [st[h] * dec[us[h]] + s_up[h] for h in hs]
        for h in hs:
            s_ref[h] = st[h]

    rows = DN_STEP * DN_CHUNK
    row = pl.BlockSpec((rows, DN_W), lambda n: (n, 0))
    return pl.pallas_call(
        body, name=name, grid=(n_chunks // DN_STEP,),
        in_specs=[row, row, row, pl.BlockSpec((rows, 128), lambda n: (n, 0)),
                  pl.BlockSpec((rows, DN_W), lambda n: (n, _Z_COL)), pl.BlockSpec((1, DN_DIM), lambda n: (0, 0))],
        out_specs=(row, pl.BlockSpec((DN_STEP, DN_HEADS, DN_DIM, DN_DIM), lambda n: (n, 0, 0, 0)),
                   pl.BlockSpec((DN_STEP, DN_HEADS, DN_CHUNK, DN_CHUNK), lambda n: (n, 0, 0, 0)),
                   row, pl.BlockSpec((DN_W, rows), lambda n: (0, n))),
        out_shape=(jax.ShapeDtypeStruct((t, DN_W), F32),
                   jax.ShapeDtypeStruct((n_chunks, DN_HEADS, DN_DIM, DN_DIM), F32),
                   jax.ShapeDtypeStruct((n_chunks, DN_HEADS, DN_CHUNK, DN_CHUNK), F32),
                   jax.ShapeDtypeStruct((t, DN_W), MXU_DTYPE), jax.ShapeDtypeStruct((DN_W, t), MXU_DTYPE)),
        scratch_shapes=[pltpu.VMEM((DN_HEADS, DN_DIM, DN_DIM), F32)],
        compiler_params=_cp("arbitrary"))(q, k, v, bg, proj, o_gain)


def gdn_bwd(q, k, v, bg, sall, tall, do, name):
    t = q.shape[0]
    n_chunks = t // DN_CHUNK

    def body(q_ref, k_ref, v_ref, bg_ref, sall_ref, tall_ref, do_ref, dq_ref, dk_ref, dv_ref, dbg_ref, ds_ref):
        n = pl.program_id(0)

        @pl.when(n == 0)
        def _():
            ds_ref[...] = jnp.zeros_like(ds_ref)

        lower, strict = _chunk_masks()
        ltri = jnp.where(lower, 1.0, 0.0)
        bgv = bg_ref[...]
        gcs = _hi(ltri, bgv)
        gcs_t = gcs.T
        lane = lax.broadcasted_iota(jnp.int32, (DN_CHUNK, 128), 1)
        rowi = lax.broadcasted_iota(jnp.int32, (DN_CHUNK, 1), 0)
        hs = range(DN_HEADS)
        each = lambda fn, *ls: [fn(*a) for a in zip(*ls)]
        rsum = lambda a: jnp.sum(a, axis=-1, keepdims=True)
        sl = [slice(h * DN_DIM, (h + 1) * DN_DIM) for h in hs]
        st = [sall_ref[0, h] for h in hs]
        tms = [tall_ref[0, h] for h in hs]
        dsn = [ds_ref[h] for h in hs]
        qh = [q_ref[:, sl[h]] for h in hs]
        kh = [k_ref[:, sl[h]] for h in hs]
        vh = [v_ref[:, sl[h]] for h in hs]
        doh = [do_ref[:, sl[h]] for h in hs]
        beta = [bgv[:, h:h + 1] for h in hs]
        com = [_chunk_common(qh[h], kh[h], beta[h], gcs[:, DN_HEADS + h:DN_HEADS + h + 1],
                             gcs_t[DN_HEADS + h:DN_HEADS + h + 1, :], lower) for h in hs]
        gam, eg, dec, kdf, kb, bmat, qmat = zip(*com)
        rhs_w = each(lambda a, b: a * b, kb, eg)
        uw = each(lambda t_, v_, b_, r_: _hi(t_, jnp.concatenate([v_ * b_, r_], axis=-1)), tms, vh, beta, rhs_w)
        qd = each(lambda a, b: a * b, qh, eg)
        kd = each(lambda a, b: a * b, kh, kdf)
        pmat = each(lambda a, b: a * b, qmat, gam)
        v_new = each(lambda uw_, s_: uw_[:, :DN_DIM] - _mx(uw_[:, DN_DIM:], s_), uw, st)
        dqd = each(_mx_nt, doh, st)
        ds_o = each(_mx_tn, qd, doh)
        dp = each(lambda d_, v_: jnp.where(lower, _mx_nt(d_, v_), 0.0), doh, v_new)
        dvn_o = each(_mx_tn, pmat, doh)
        ddec = each(lambda d_, s_: jnp.sum(rsum(d_ * s_), axis=0, keepdims=True), dsn, st)
        dkd = each(_mx_nt, v_new, dsn)
        dvn = each(lambda a, k_, d_: a + _mx(k_, d_), dvn_o, kd, dsn)
        dw = each(lambda d_, s_: -_mx_nt(d_, s_), dvn, st)
        ds_w = each(lambda uw_, d_: _mx_tn(uw_[:, DN_DIM:], d_), uw, dvn)
        for h in hs:
            ds_ref[h] = ds_o[h] + dec[h] * dsn[h] - ds_w[h]
        dr = each(lambda t_, a, b: _hi_tn(t_, jnp.concatenate([a, b], axis=-1)), tms, dvn, dw)
        da = each(lambda r_, uw_: jnp.where(strict, -_hi_nt(r_, uw_), 0.0), dr, uw)
        dru = [r_[:, :DN_DIM] for r_ in dr]
        drw = [r_[:, DN_DIM:] for r_ in dr]
        db = each(lambda a, b: a * b, da, gam)
        dq_m = each(lambda a, b: a * b, dp, gam)
        e = each(lambda a, bm, p_, qm, g_: (a * bm + p_ * qm) * g_, da, bmat, dp, qmat, gam)
        dkb = each(lambda b_, k_, r_, e_: _mx(b_, k_) + r_ * e_, db, kh, drw, eg)
        dk = each(lambda b_, kb_, m_, q_, d_, f_: _mx_tn(b_, kb_) + _mx_tn(m_, q_) + d_ * f_, db, kb, dq_m, qh, dkd, kdf)
        dq = each(lambda m_, k_, d_, e_: _mx(m_, k_) + d_ * e_, dq_m, kh, dqd, eg)
        tk = each(lambda a, b: rsum(a * b), dkd, kd)
        dbeta_all = jnp.zeros((DN_CHUNK, 128), F32)
        dgc_all = jnp.zeros((DN_CHUNK, 128), F32)
        for h in hs:
            dgc = (jnp.sum(e[h], axis=1, keepdims=True) - jnp.sum(e[h].T, axis=1, keepdims=True)
                   + rsum(dqd[h] * qd[h]) - tk[h] + rsum(drw[h] * rhs_w[h]))
            dgl = jnp.sum(tk[h], axis=0, keepdims=True) + ddec[h] * dec[h]
            dgc = dgc + jnp.where(rowi == DN_CHUNK - 1, dgl, 0.0)
            dbeta = rsum(dru[h] * vh[h]) + rsum(dkb[h] * kh[h])
            dq_ref[:, sl[h]] = dq[h]
            dk_ref[:, sl[h]] = dk[h] + dkb[h] * beta[h]
            dv_ref[:, sl[h]] = dru[h] * beta[h]
            dbeta_all = jnp.where(lane == h, dbeta, dbeta_all)
            dgc_all = jnp.where(lane == DN_HEADS + h, dgc, dgc_all)
        dbg_ref[...] = dbeta_all + _hi_tn(ltri, dgc_all)

    rev = lambda n: (n_chunks - 1 - n, 0)
    row = pl.BlockSpec((DN_CHUNK, DN_W), rev)
    small = pl.BlockSpec((DN_CHUNK, 128), rev)
    return pl.pallas_call(
        body, name=name, grid=(n_chunks,),
        in_specs=[row, row, row, small,
                  pl.BlockSpec((1, DN_HEADS, DN_DIM, DN_DIM), lambda n: (n_chunks - 1 - n, 0, 0, 0)),
                  pl.BlockSpec((1, DN_HEADS, DN_CHUNK, DN_CHUNK), lambda n: (n_chunks - 1 - n, 0, 0, 0)), row],
        out_specs=(row, row, row, small),
        out_shape=(jax.ShapeDtypeStruct((t, DN_W), F32),) * 3 + (jax.ShapeDtypeStruct((t, 128), F32),),
        scratch_shapes=[pltpu.VMEM((DN_HEADS, DN_DIM, DN_DIM), F32)],
        compiler_params=_cp("arbitrary"))(q, k, v, bg, sall, tall, do)


def gdn_out_bwd(o, proj, o_gain, dy, name, tm=256):
    t = o.shape[0]

    def body(o_ref, z_ref, g_ref, dy_ref, do_ref, dz_ref, dg_ref):
        i = pl.program_id(0)
        dg = jnp.zeros((1, DN_DIM), F32)
        for h in range(DN_HEADS):
            sl = slice(h * DN_DIM, (h + 1) * DN_DIM)
            ov, zv, dyv = o_ref[:, sl], z_ref[:, sl], dy_ref[:, sl]
            r = lax.rsqrt(jnp.mean(ov * ov, axis=-1, keepdims=True) + EPS)
            oh = ov * r
            sg = _sigmoid(zv)
            dz_ref[:, sl] = (dyv * oh * g_ref[...] * sg * (1.0 + zv * (1.0 - sg))).astype(dz_ref.dtype)
            don = dyv * (zv * sg)
            dg = dg + jnp.sum(don * oh, axis=0, keepdims=True)
            doh = don * g_ref[...]
            do_ref[:, sl] = r * (doh - oh * jnp.mean(doh * oh, axis=-1, keepdims=True))

        @pl.when(i == 0)
        def _():
            dg_ref[...] = dg

        @pl.when(i > 0)
        def _():
            dg_ref[...] += dg

    row = pl.BlockSpec((tm, DN_W), lambda i: (i, 0))
    one = pl.BlockSpec((1, DN_DIM), lambda i: (0, 0))
    return pl.pallas_call(
        body, name=name, grid=(t // tm,),
        in_specs=[row, pl.BlockSpec((tm, DN_W), lambda i: (i, _Z_COL)), one, row],
        out_specs=(row, row, one),
        out_shape=(jax.ShapeDtypeStruct((t, DN_W), F32), jax.ShapeDtypeStruct((t, DN_W), MXU_DTYPE),
                   jax.ShapeDtypeStruct((1, DN_DIM), F32)),
        compiler_params=_cp("arbitrary"))(o, proj, o_gain, dy)


def _peer(k):
    x, y, c = lax.axis_index("x"), lax.axis_index("y"), lax.axis_index("c")
    px = 1 - x if k & 4 else x
    py = 1 - y if k & 2 else y
    pc = 1 - c if k & 1 else c
    return (px, py, pc), 4 * px + 2 * py + pc


_HBM = pl.BlockSpec(memory_space=pltpu.HBM)
_SEM = pl.BlockSpec(memory_space=pltpu.SEMAPHORE)
_DATAFLOW = pltpu.SideEffectType.DATAFLOW_SIDE_EFFECTING
N_PEER = N_DEV - 1


def send_start(srcs, name, scatter, after):
    na = len(srcs)
    ns = (2 * N_PEER + 1) * na
    lands = [lax.empty((N_DEV,) + (s.shape[1:] if scatter else s.shape), s.dtype) for s in srcs]
    extra = [] if after is None else [after]

    def body(*refs):
        src_refs, land_refs = refs[:na], refs[na:2 * na]
        sems = refs[2 * na + len(extra):2 * na + len(extra) + ns]
        land_out, token = refs[-1 - na:-1], refs[-1]
        _, me = _peer(0)
        for a in range(na):
            pltpu.make_async_copy(src_refs[a].at[me] if scatter else src_refs[a], land_out[a].at[me],
                                  sems[2 * N_PEER * na + a]).start()
        for k in range(1, N_DEV):
            peer, pid = _peer(k)
            for a in range(na):
                pltpu.make_async_remote_copy(
                    src_ref=src_refs[a].at[pid] if scatter else src_refs[a], dst_ref=land_refs[a].at[me],
                    send_sem=sems[2 * (a * N_PEER + k - 1)], recv_sem=sems[2 * (a * N_PEER + k - 1) + 1],
                    device_id=peer, device_id_type=MESH).start()
        token[...] = jnp.zeros_like(token)

    hbm = lambda arrs: tuple(pltpu.HBM(a.shape, a.dtype) for a in arrs)
    outs = pl.pallas_call(
        body, name=name,
        out_shape=(pltpu.SemaphoreType.DMA(()),) * ns + hbm(srcs) + hbm(lands) + (jax.ShapeDtypeStruct((8, 128), F32),),
        in_specs=[_HBM] * (2 * na) + [pl.BlockSpec(memory_space=pl.ANY)] * len(extra),
        out_specs=(_SEM,) * ns + (_HBM,) * (2 * na) + (pl.BlockSpec(memory_space=pltpu.VMEM),),
        input_output_aliases={i: ns + i for i in range(2 * na)},
        compiler_params=pltpu.CompilerParams(has_side_effects=_DATAFLOW),
    )(*[pltpu.with_memory_space_constraint(a, pltpu.HBM) for a in list(srcs) + lands], *extra)
    return outs[:ns], outs[ns:ns + na], outs[ns + na:ns + 2 * na], outs[-1]


def send_wait(sems, srcs_thru, lands_thru, name, scatter, after):
    na = len(srcs_thru)
    ns = (2 * N_PEER + 1) * na

    def body(*refs):
        src_refs, land_refs, sm = refs[:na], refs[na:2 * na], refs[2 * na:2 * na + ns]
        _, me = _peer(0)
        for a in range(na):
            pltpu.make_async_copy(src_refs[a].at[me] if scatter else src_refs[a], land_refs[a].at[me],
                                  sm[2 * N_PEER * na + a]).wait()
        for k in range(1, N_DEV):
            peer, pid = _peer(k)
            for a in range(na):
                cp = pltpu.make_async_remote_copy(
                    src_ref=src_refs[a].at[pid] if scatter else src_refs[a], dst_ref=land_refs[a].at[pid],
                    send_sem=sm[2 * (a * N_PEER + k - 1)], recv_sem=sm[2 * (a * N_PEER + k - 1) + 1],
                    device_id=peer, device_id_type=MESH)
                cp.wait_send()
                cp.wait_recv()

    hbm = lambda arrs: tuple(pltpu.HBM(a.shape, a.dtype) for a in arrs)
    outs = pl.pallas_call(
        body, name=name, out_shape=hbm(srcs_thru) + hbm(lands_thru),
        in_specs=[_HBM] * (2 * na) + [_SEM] * ns + [pl.BlockSpec(memory_space=pl.ANY)], out_specs=(_HBM,) * (2 * na),
        input_output_aliases={i: i for i in range(2 * na)},
        compiler_params=pltpu.CompilerParams(has_side_effects=_DATAFLOW),
    )(*srcs_thru, *lands_thru, *sems, after)
    return outs[na:]


def _adamw(w, g, m, v):
    m = ADAM_B1 * m + (1.0 - ADAM_B1) * g
    v = ADAM_B2 * v + (1.0 - ADAM_B2) * (g * g)
    m_hat = m / (1.0 - ADAM_B1 ** ADAM_STEP)
    v_hat = v / (1.0 - ADAM_B2 ** ADAM_STEP)
    return -ADAM_LR * (m_hat / (jnp.sqrt(v_hat) + ADAM_EPS) + ADAM_WD * w), m, v


def adam_sum(w, pieces, m, v, name, layer=0, into=None):
    nl, r, c = w.shape
    tr = r
    for cand in (256, 128, 64, 32, 16, 8):
        if r % cand == 0:
            tr = cand
            break

    def body(w_ref, p_ref, m_ref, v_ref, *rest):
        g_ref, d_ref, nm_ref, nv_ref = rest[-4:]
        g = p_ref[0].astype(F32)
        for s in range(1, N_DEV):
            g = g + p_ref[s].astype(F32)
        g_ref[0] = g
        d_ref[0], nm_ref[0], nv_ref[0] = _adamw(w_ref[0], g, m_ref[0], v_ref[0])

    row = pl.BlockSpec((1, tr, c), lambda i: (layer, i, 0))
    out = jax.ShapeDtypeStruct((nl, r, c), F32)
    extra = [] if into is None else list(into)
    return pl.pallas_call(
        body, name=name, grid=(r // tr,),
        in_specs=[row, pl.BlockSpec((N_DEV, tr, c), lambda i: (0, i, 0)), row, row]
        + [pl.BlockSpec(memory_space=pl.ANY)] * len(extra),
        out_specs=(row,) * 4, out_shape=(out,) * 4,
        input_output_aliases={4 + i: i for i in range(len(extra))},
        compiler_params=_cp("parallel"))(w, pieces, m, v, *extra)


def sum_rows(gathered, name):
    _, r, c = gathered.shape

    def body(p_ref, o_ref):
        g = p_ref[0]
        for s in range(1, N_DEV):
            g = g + p_ref[s]
        o_ref[...] = g

    return pl.pallas_call(body, name=name, out_shape=jax.ShapeDtypeStruct((r, c), F32))(gathered)


def adam_small(w, g, m, v, name):
    def body(w_ref, g_ref, m_ref, v_ref, d_ref, nm_ref, nv_ref):
        d_ref[...], nm_ref[...], nv_ref[...] = _adamw(w_ref[...], g_ref[...], m_ref[...], v_ref[...])

    out = jax.ShapeDtypeStruct(w.shape, F32)
    return pl.pallas_call(body, name=name, out_shape=(out,) * 3)(w, g, m, v)


def _rope_tables(t):
    inv_freq = 10000.0 ** (-jnp.arange(0, HEAD_DIM, 2, dtype=F32) / HEAD_DIM)
    ang = jnp.arange(t, dtype=F32)[:, None] * inv_freq[None, :]
    cos, sin = jnp.cos(ang), jnp.sin(ang)
    return jnp.concatenate([cos, cos], axis=-1), jnp.concatenate([sin, sin], axis=-1)


def _lane_row(vec8):
    return jnp.pad(vec8.reshape(1, DN_HEADS), ((0, 0), (DN_HEADS, 128 - 2 * DN_HEADS)))


def _ffn_bwd(x, norm_g, w_gu, w_d, saved, dy, tag, after=None):
    ft, gu, at = saved
    dgu = ffn_dact(dy, w_d, gu, f"{tag}_d_gate_up", after=after)
    dwd = mm_at(at, dy, f"{tag}_dw_down")
    nj = D_FF // GU_TILE
    dwgu = mm_at(ft, dgu, f"{tag}_dw_gate_up", transposed=True, tn=GU_TILE, row_block=lambda q: (q % 2) * nj + q // 2)
    dx, dg = mm_rms_bwd(dgu, w_gu, x, norm_g, dy, f"{tag}_d_norm", chunks=_GU_CHUNKS)
    return dx, dwgu, dwd, dg


def local_step(x, target, small, weights_of, grads_out, after=None):
    t = x.shape[0]
    cosf, sinf = _rope_tables(t)
    alog_row, dtb_row = _lane_row(small["odd_a_log"]), _lane_row(small["odd_dt_bias"])

    h0, h0t = rms_fwd(x, small["even_norm"], "even_norm", after=after)
    we = weights_of("even", h0)
    small = {**small, **we.get("small", {})}
    proj0 = mm_nt(h0, we["w_in"], "even_in_proj")
    qr, kr = qk_prep_fwd(proj0, small["even_q_gain"], small["even_k_gain"], cosf, sinf, "even_qk_prep")
    y_attn, mix0, mix0_t, lse = swa_fwd(qr, kr, proj0, small["even_sinks"], "even_swa")
    mix0, mix0_t = gconv_fwd(proj0, small["even_conv_w"], mix0, mix0_t, "even_gconv")
    we = {**we, **weights_of("even_out", mix0)}
    x1, f0, f0t = mm_nn_res_norm(mix0, we["w_out"], x, small["ffn_norm0"], "even_out_proj")
    w0 = weights_of("ffn0", x1)
    gu0, a0, a0t = ffn_up(f0, w0["gate_up"], "ffn0_gate_up")
    ffn0 = (f0t, gu0, a0t)
    x2, h1, h1t = mm_nn_res_norm(a0, w0["down"], x1, small["odd_norm"], "ffn0_down")

    wo = weights_of("odd", x2)
    proj1 = mm_nt(h1, wo["w_in"], "odd_in_proj")
    qn, kn, vs, bg, conv1 = gdn_prep_fwd(proj1, small["odd_conv_w"], alog_row, dtb_row, "odd_prep")
    o, sall, tall, og, ogt = gdn_fwd(qn, kn, vs, bg, proj1, small["odd_o_gain"], "odd_delta_rule")
    x3, f1, f1t = mm_nn_res_norm(og, wo["w_out"], x2, small["ffn_norm1"], "odd_out_proj")
    w1 = weights_of("ffn1", x3)
    gu1, a1, a1t = ffn_up(f1, w1["gate_up"], "ffn1_gate_up")
    ffn1 = (f1t, gu1, a1t)
    dy, loss_row = mm_nn_res_loss(a1, w1["down"], x3, target, "ffn1_down_loss")

    gs = {}
    dx3, dwgu, dwd, gs["ffn_norm1"] = _ffn_bwd(x3, small["ffn_norm1"], w1["gate_up"], w1["down"], ffn1, dy, "ffn1")
    tok = grads_out("ffn1", {"gate_up": dwgu, "down": dwd})

    dog = mm_nt(dx3, wo["w_out"], "odd_d_gated", after=tok)
    dwo = mm_at(ogt, dx3, "odd_dw_out")
    do, dz, gs["odd_o_gain"] = gdn_out_bwd(o, proj1, small["odd_o_gain"], dog, "odd_d_gate_norm")
    dqn, dkn, dvs, dbg = gdn_bwd(qn, kn, vs, bg, sall, tall, do, "odd_d_delta_rule")
    dproj1, gs["odd_conv_w"], ddt_row, dal_row = gdn_prep_bwd(
        proj1, conv1, small["odd_conv_w"], alog_row, dtb_row, dqn, dkn, dvs, dbg, dz, "odd_d_prep")
    gs["odd_dt_bias"] = ddt_row[:, DN_HEADS:2 * DN_HEADS]
    gs["odd_a_log"] = dal_row[:, DN_HEADS:2 * DN_HEADS]
    dwi = mm_at(h1t, dproj1, "odd_dw_in", transposed=True)
    dx2, gs["odd_norm"] = mm_rms_bwd(dproj1, wo["w_in"], x2, small["odd_norm"], dx3, "odd_d_norm")
    tok = grads_out("odd", {"w_in": dwi, "w_out": dwo})

    dx1, dwgu, dwd, gs["ffn_norm0"] = _ffn_bwd(x1, small["ffn_norm0"], w0["gate_up"], w0["down"], ffn0, dx2, "ffn0",
                                               after=tok)
    tok = grads_out("ffn0", {"gate_up": dwgu, "down": dwd})

    dmix = mm_nt(dx1, we["w_out"], "even_d_mix", after=tok)
    dwo = mm_at(mix0_t, dx1, "even_dw_out")
    dqr, dkr, dv, gs["even_sinks"] = swa_bwd(qr, kr, proj0, small["even_sinks"], y_attn, lse, dmix, "even_d_swa")
    dqk, gs["even_q_gain"], gs["even_k_gain"] = qk_prep_bwd(
        proj0, small["even_q_gain"], small["even_k_gain"], cosf, sinf, dqr, dkr, "even_d_qk_prep")
    dgb, dgc, dxi, gs["even_conv_w"] = gconv_bwd(proj0, small["even_conv_w"], dmix, "even_d_gconv")
    dproj0 = jnp.concatenate([dqk, dv, dgb, dgc, dxi], axis=-1)
    dwi = mm_at(h0t, dproj0, "even_dw_in", transposed=True)
    tok = grads_out("even", {"w_in": dwi, "w_out": dwo})
    grad_x, gs["even_norm"] = mm_rms_bwd(dproj0, we["w_in"], x, small["even_norm"], dx1, "even_d_norm", after=tok)
    return loss_row, grad_x, gs


_SMALL_ORDER = ("even_norm", "even_q_gain", "even_k_gain", "even_sinks", "odd_a_log", "odd_dt_bias", "odd_o_gain",
                "ffn_norm0", "ffn_norm1", "odd_norm", "even_conv_w", "odd_conv_w")
_SMALL_SIZE = {"even_norm": 1024, "even_q_gain": 64, "even_k_gain": 64, "even_sinks": 8, "odd_a_log": 8,
               "odd_dt_bias": 8, "odd_o_gain": 128, "ffn_norm0": 1024, "ffn_norm1": 1024, "odd_norm": 1024,
               "even_conv_w": 3 * 512, "odd_conv_w": 4 * 3072}
_N_REPL = 9


def _pack_rows(vals):
    flat = jnp.concatenate([v.reshape(-1) for v in vals])
    pad = (-flat.shape[0]) % 1024
    return jnp.pad(flat, (0, pad)).reshape(-1, 128)


def _my_block(full, size, axis):
    me = 4 * lax.axis_index("x") + 2 * lax.axis_index("y") + lax.axis_index("c")
    return lax.dynamic_slice_in_dim(full, me * size, size, axis=axis)


def kernel(x, even_norm, even_w_in, even_q_gain, even_k_gain, even_sinks, even_conv_w, even_w_out, odd_norm, odd_w_in, odd_conv_w, odd_a_log, odd_dt_bias, odd_o_gain, odd_w_out, ffn_norm, ffn_w_gate_up, ffn_w_down, loss_target, m_even_norm, m_even_w_in, m_even_q_gain, m_even_k_gain, m_even_sinks, m_even_conv_w, m_even_w_out, m_odd_norm, m_odd_w_in, m_odd_conv_w, m_odd_a_log, m_odd_dt_bias, m_odd_o_gain, m_odd_w_out, m_ffn_norm, m_ffn_w_gate_up, m_ffn_w_down, v_even_norm, v_even_w_in, v_even_q_gain, v_even_k_gain, v_even_sinks, v_even_conv_w, v_even_w_out, v_odd_norm, v_odd_w_in, v_odd_conv_w, v_odd_a_log, v_odd_dt_bias, v_odd_o_gain, v_odd_w_out, v_ffn_norm, v_ffn_w_gate_up, v_ffn_w_down):
    t = x.shape[1]
    d = D_MODEL

    tr = lambda a: jnp.swapaxes(a, 1, 2)
    shard = {
        "even": {"w_in": tr(even_w_in)[0], "w_out": even_w_out[0]},
        "ffn0": {"gate_up": tr(ffn_w_gate_up)[0], "down": ffn_w_down[0]},
        "odd": {"w_in": tr(odd_w_in)[0], "w_out": odd_w_out[0]},
        "ffn1": {"gate_up": tr(ffn_w_gate_up)[1], "down": ffn_w_down[1]},
    }
    given = {
        ("even", "w_in"): ("even_w_in", even_w_in, m_even_w_in, v_even_w_in, 0),
        ("even", "w_out"): ("even_w_out", even_w_out, m_even_w_out, v_even_w_out, 0),
        ("odd", "w_in"): ("odd_w_in", odd_w_in, m_odd_w_in, v_odd_w_in, 0),
        ("odd", "w_out"): ("odd_w_out", odd_w_out, m_odd_w_out, v_odd_w_out, 0),
        ("ffn0", "gate_up"): ("ffn_w_gate_up", ffn_w_gate_up, m_ffn_w_gate_up, v_ffn_w_gate_up, 0),
        ("ffn1", "gate_up"): ("ffn_w_gate_up", ffn_w_gate_up, m_ffn_w_gate_up, v_ffn_w_gate_up, 1),
        ("ffn0", "down"): ("ffn_w_down", ffn_w_down, m_ffn_w_down, v_ffn_w_down, 0),
        ("ffn1", "down"): ("ffn_w_down", ffn_w_down, m_ffn_w_down, v_ffn_w_down, 1),
    }

    def whole(group, parts):
        col, row = tuple(shard[group])
        w_col = parts[0].reshape(-1, d)
        if group == "odd":
            w_col = jnp.pad(w_col, ((0, ODD_IN_PAD - ODD_IN_W), (0, 0)))
        return {col: w_col, row: parts[1].reshape(-1, d)}

    wire = {g: [a.astype(MXU_DTYPE) for a in shard[g].values()] for g in shard}
    wire["even_out"] = [wire["even"].pop()]
    wire["even"].append(_pack_rows([odd_norm, even_conv_w, odd_conv_w]))
    gathers, tok = {}, None
    for g in ("even", "even_out", "ffn0", "odd", "ffn1"):
        sems, srcs_thru, lands_thru, tok = send_start(wire[g], f"gather_{g}_start", False, tok)
        gathers[g] = (sems, srcs_thru, lands_thru)
    o1 = d // N_DEV
    o2 = o1 + 3 * CONV_CH // N_DEV

    def weights_of(group, after):
        lands = send_wait(*gathers[group], f"gather_{group}_wait", False, after)
        if group == "even_out":
            return {"w_out": lands[0].reshape(-1, d)}
        if group != "even":
            return whole(group, lands)
        sg = lands[1].reshape(N_DEV, -1)
        return {"w_in": lands[0].reshape(-1, d), "small": {
            "odd_norm": sg[:, :o1].reshape(1, d),
            "even_conv_w": sg[:, o1:o2].reshape(N_DEV, 3, CONV_CH // N_DEV).transpose(1, 0, 2).reshape(3, CONV_CH),
            "odd_conv_w": sg[:, o2:o2 + 4 * _QKV_W // N_DEV].reshape(N_DEV, 4, _QKV_W // N_DEV)
            .transpose(1, 0, 2).reshape(4, _QKV_W)}}

    sent = {}

    def grads_out(group, dws):
        col, row = tuple(shard[group])
        n_cols = N_DEV * shard[group][col].shape[0]
        pieces = [dws[col][:n_cols].reshape((N_DEV,) + shard[group][col].shape),
                  dws[row].reshape((N_DEV,) + shard[group][row].shape)]
        sems, srcs_thru, lands_thru, token = send_start(pieces, f"exchange_{group}_start", True, None)
        sent[group] = (sems, srcs_thru, lands_thru, pieces)
        return token

    small = {
        "even_norm": even_norm, "even_q_gain": even_q_gain, "even_k_gain": even_k_gain, "even_sinks": even_sinks,
        "odd_a_log": odd_a_log.reshape(-1), "odd_dt_bias": odd_dt_bias.reshape(-1), "odd_o_gain": odd_o_gain,
        "ffn_norm0": ffn_norm[0:1], "ffn_norm1": ffn_norm[1:2],
    }

    loss_row, grad_x, gs = local_step(x.reshape(t, d), loss_target.reshape(t, d), small, weights_of, grads_out, after=tok)

    rows = _pack_rows([gs[n] for n in _SMALL_ORDER] + [loss_row[:, 0:1]])
    small_sent = send_start([rows], "gather_small_grads_start", False, None)

    res, behind = {}, small_sent[3]
    for g in ("ffn1", "odd", "ffn0", "even"):
        sems, srcs_thru, lands_thru, pieces = sent[g]
        lands = send_wait(sems, srcs_thru, lands_thru, f"exchange_{g}_wait", True, behind)
        for i, (key, pcs) in enumerate(zip(shard[g], lands)):
            name, w_, m_, v_, layer = given[g, key]
            view = tr if i == 0 else (lambda a: a)
            res[name] = adam_sum(view(w_), pcs, view(m_), view(v_), f"adamw_{g}_{key}", layer=layer, into=res.get(name))
        behind = res[name][0]
    for name in ("even_w_in", "odd_w_in", "ffn_w_gate_up"):
        res[name] = tuple(tr(a) for a in res[name])

    (rows_g,) = send_wait(*small_sent[:3], "gather_small_grads_wait", False, behind)
    tot = sum_rows(rows_g, "sum_small_grads").reshape(-1)
    off, sgrad = 0, {}
    for n in _SMALL_ORDER:
        sgrad[n] = tot[off:off + _SMALL_SIZE[n]]
        off += _SMALL_SIZE[n]
    loss = tot[off]

    repl = _SMALL_ORDER[:_N_REPL]
    repl_w = {"even_norm": even_norm, "even_q_gain": even_q_gain, "even_k_gain": even_k_gain, "even_sinks": even_sinks,
              "odd_a_log": odd_a_log, "odd_dt_bias": odd_dt_bias, "odd_o_gain": odd_o_gain,
              "ffn_norm0": ffn_norm[0], "ffn_norm1": ffn_norm[1]}
    repl_m = {"even_norm": m_even_norm, "even_q_gain": m_even_q_gain, "even_k_gain": m_even_k_gain,
              "even_sinks": m_even_sinks, "odd_a_log": m_odd_a_log, "odd_dt_bias": m_odd_dt_bias,
              "odd_o_gain": m_odd_o_gain, "ffn_norm0": m_ffn_norm[0], "ffn_norm1": m_ffn_norm[1]}
    repl_v = {"even_norm": v_even_norm, "even_q_gain": v_even_q_gain, "even_k_gain": v_even_k_gain,
              "even_sinks": v_even_sinks, "odd_a_log": v_odd_a_log, "odd_dt_bias": v_odd_dt_bias,
              "odd_o_gain": v_odd_o_gain, "ffn_norm0": v_ffn_norm[0], "ffn_norm1": v_ffn_norm[1]}
    pk = lambda dct: _pack_rows([dct[n] for n in repl])
    pd_, pm_, pv_ = adam_small(pk(repl_w), pk(sgrad), pk(repl_m), pk(repl_v), "adamw_replicated")
    sres = {}
    off = 0
    for n in repl:
        sz = _SMALL_SIZE[n]
        sres[n] = (sgrad[n], pd_.reshape(-1)[off:off + sz], pm_.reshape(-1)[off:off + sz], pv_.reshape(-1)[off:off + sz])
        off += sz
    g_on = _my_block(sgrad["odd_norm"].reshape(1, d), d // N_DEV, 1)
    g_ec = _my_block(sgrad["even_conv_w"].reshape(3, CONV_CH), CONV_CH // N_DEV, 1)
    g_oc = _my_block(sgrad["odd_conv_w"].reshape(4, _QKV_W), _QKV_W // N_DEV, 1)
    shard_w = _pack_rows([odd_norm, even_conv_w, odd_conv_w])
    sd_, sm_, sv_ = adam_small(shard_w, _pack_rows([g_on, g_ec, g_oc]),
                               _pack_rows([m_odd_norm, m_even_conv_w, m_odd_conv_w]),
                               _pack_rows([v_odd_norm, v_even_conv_w, v_odd_conv_w]), "adamw_sharded_small")
    off = 0
    for n, gfull, like in (("odd_norm", g_on, odd_norm), ("even_conv_w", g_ec, even_conv_w), ("odd_conv_w", g_oc, odd_conv_w)):
        sz = like.size
        sres[n] = (gfull, sd_.reshape(-1)[off:off + sz], sm_.reshape(-1)[off:off + sz], sv_.reshape(-1)[off:off + sz])
        off += sz

    def small_out(name, like, kind):
        if name == "ffn_norm":
            return jnp.stack([sres["ffn_norm0"][kind], sres["ffn_norm1"][kind]]).reshape(like.shape)
        return sres[name][kind].reshape(like.shape)

    order = (("even_norm", even_norm), ("even_w_in", even_w_in), ("even_q_gain", even_q_gain),
             ("even_k_gain", even_k_gain), ("even_sinks", even_sinks), ("even_conv_w", even_conv_w),
             ("even_w_out", even_w_out), ("odd_norm", odd_norm), ("odd_w_in", odd_w_in), ("odd_conv_w", odd_conv_w),
             ("odd_a_log", odd_a_log), ("odd_dt_bias", odd_dt_bias), ("odd_o_gain", odd_o_gain),
             ("odd_w_out", odd_w_out), ("ffn_norm", ffn_norm), ("ffn_w_gate_up", ffn_w_gate_up),
             ("ffn_w_down", ffn_w_down))
    outs = [loss, grad_x.reshape(x.shape)]
    for kind in range(4):
        for name, like in order:
            outs.append(res[name][kind] if name in res else small_out(name, like, kind))
    return tuple(outs)
```

```python
import jax
import jax.numpy as jnp
import numpy as np
from jax import lax
from jax.experimental import pallas as pl
from jax.experimental.pallas import tpu as pltpu

F32 = jnp.float32
MXU_DTYPE = jnp.bfloat16
HI = lax.Precision.HIGH
EPS = 1e-6
N_DEV = 8
D_MODEL = 1024
HEAD_DIM = 64
ATTN_HEADS = 8
KV_HEADS = 2
ATTN_BLOCK = 128
Q_W = 512
KV_W = 128
CONV_CH = 512
EVEN_IN_W = 2304
DN_HEADS = 8
DN_DIM = 128
DN_W = 1024
DN_CHUNK = 64
ODD_IN_W = 4112
ODD_IN_PAD = 4224
D_FF = 2816
NEG = -1e30
VMEM_LIMIT = 56 * 1024 * 1024
ADAM_LR, ADAM_B1, ADAM_B2, ADAM_EPS, ADAM_WD, ADAM_STEP = 0.001, 0.9, 0.999, 1e-08, 0.01, 10
MESH = pl.DeviceIdType.MESH


def _cp(*sem):
    return pltpu.CompilerParams(dimension_semantics=sem, vmem_limit_bytes=VMEM_LIMIT)


def _pick(n, cap):
    best = 128
    for t in range(128, cap + 1, 128):
        if n % t == 0:
            best = t
    return best


def _mx(a, b):
    return jnp.dot(a.astype(MXU_DTYPE), b.astype(MXU_DTYPE), preferred_element_type=F32)


def _mx_nt(a, b):
    return lax.dot_general(a.astype(MXU_DTYPE), b.astype(MXU_DTYPE), (((1,), (1,)), ((), ())),
                           preferred_element_type=F32)


def _mx_tn(a, b):
    return lax.dot_general(a.astype(MXU_DTYPE), b.astype(MXU_DTYPE), (((0,), (0,)), ((), ())),
                           preferred_element_type=F32)


def _hi(a, b):
    return jnp.dot(a, b, precision=HI, preferred_element_type=F32)


def _hi_nt(a, b):
    return lax.dot_general(a, b, (((1,), (1,)), ((), ())), precision=HI, preferred_element_type=F32)


def _hi_tn(a, b):
    return lax.dot_general(a, b, (((0,), (0,)), ((), ())), precision=HI, preferred_element_type=F32)


def _sigmoid(x):
    return 0.5 * jnp.tanh(0.5 * x) + 0.5


def _softplus(x):
    return jnp.maximum(x, 0.0) + jnp.log(1.0 + jnp.exp(-jnp.abs(x)))


def mm_nn_res_norm(a, b, res, g, name, tm=512):
    t, k = a.shape
    d = b.shape[1]
    tm = min(tm, t)

    def body(a_ref, b_ref, res_ref, g_ref, y_ref, h_ref, ht_ref):
        y = res_ref[...] + _mx(a_ref[...], b_ref[...])
        y_ref[...] = y
        h = y * lax.rsqrt(jnp.mean(y * y, axis=-1, keepdims=True) + EPS) * g_ref[...]
        h_ref[...] = h.astype(h_ref.dtype)
        ht_ref[...] = h.T.astype(ht_ref.dtype)

    row = pl.BlockSpec((tm, d), lambda i: (i, 0))
    return pl.pallas_call(
        body, name=name, grid=(t // tm,),
        in_specs=[pl.BlockSpec((tm, k), lambda i: (i, 0)), pl.BlockSpec((k, d), lambda i: (0, 0)), row,
                  pl.BlockSpec((1, d), lambda i: (0, 0))],
        out_specs=(row, row, pl.BlockSpec((d, tm), lambda i: (0, i))),
        out_shape=(jax.ShapeDtypeStruct((t, d), F32), jax.ShapeDtypeStruct((t, d), MXU_DTYPE),
                   jax.ShapeDtypeStruct((d, t), MXU_DTYPE)),
        compiler_params=_cp("parallel"))(a, b, res, g)


def mm_nt(a, b, name, out_dtype=F32, tm=2048, after=None):
    m, k = a.shape
    n, _ = b.shape
    tn = _pick(n, 512 if k > 3000 else 1536)
    tm = min(tm, m)

    def body(a_ref, b_ref, *rest):
        o_ref = rest[-1]
        o_ref[...] = _mx_nt(a_ref[...], b_ref[...]).astype(o_ref.dtype)

    in_specs = [pl.BlockSpec((tm, k), lambda j, i: (i, 0)), pl.BlockSpec((tn, k), lambda j, i: (j, 0))]
    args = [a, b]
    if after is not None:
        in_specs.append(pl.BlockSpec(memory_space=pl.ANY))
        args.append(after)
    return pl.pallas_call(
        body, name=name, grid=(n // tn, m // tm), in_specs=in_specs,
        out_specs=pl.BlockSpec((tm, tn), lambda j, i: (i, j)),
        out_shape=jax.ShapeDtypeStruct((m, n), out_dtype), compiler_params=_cp("parallel", "parallel"))(*args)


def mm_at(at, b, name, tk=2048, transposed=False, tn=None, row_block=None):
    m, kk = at.shape
    _, n = b.shape
    tm, tn, tk = _pick(m, 1408), tn or _pick(n, 2816), min(tk, kk)
    nk = kk // tk

    def body(a_ref, b_ref, o_ref, acc_ref):
        k = pl.program_id(2)
        p = _mx(a_ref[...], b_ref[...])
        acc = jnp.where(k == 0, p, acc_ref[...] + p)
        acc_ref[...] = acc

        @pl.when(k == nk - 1)
        def _():
            o_ref[...] = (acc.T if transposed else acc).astype(o_ref.dtype)

    if transposed:
        rb = row_block or (lambda j: j)
        out_spec = pl.BlockSpec((tn, tm), lambda i, j, k: (rb(j), i))
        out_shape = jax.ShapeDtypeStruct((n, m), MXU_DTYPE)
    else:
        out_spec = pl.BlockSpec((tm, tn), lambda i, j, k: (i, j))
        out_shape = jax.ShapeDtypeStruct((m, n), MXU_DTYPE)
    return pl.pallas_call(
        body, name=name, grid=(m // tm, n // tn, nk),
        in_specs=[pl.BlockSpec((tm, tk), lambda i, j, k: (i, k)), pl.BlockSpec((tk, tn), lambda i, j, k: (k, j))],
        out_specs=out_spec, out_shape=out_shape, scratch_shapes=[pltpu.VMEM((tm, tn), F32)],
        compiler_params=_cp("parallel", "parallel", "arbitrary"))(at, b)


def rms_fwd(x, g, name, tm=512, after=None):
    t, d = x.shape

    def body(x_ref, g_ref, *rest):
        o_ref, ot_ref = rest[-2:]
        xv = x_ref[...]
        r = lax.rsqrt(jnp.mean(xv * xv, axis=-1, keepdims=True) + EPS)
        h = xv * r * g_ref[...]
        o_ref[...] = h.astype(o_ref.dtype)
        ot_ref[...] = h.T.astype(ot_ref.dtype)

    in_specs = [pl.BlockSpec((tm, d), lambda i: (i, 0)), pl.BlockSpec((1, d), lambda i: (0, 0))]
    args = [x, g]
    if after is not None:
        in_specs.append(pl.BlockSpec(memory_space=pl.ANY))
        args.append(after)
    return pl.pallas_call(
        body, name=name, grid=(t // tm,), in_specs=in_specs,
        out_specs=(pl.BlockSpec((tm, d), lambda i: (i, 0)), pl.BlockSpec((d, tm), lambda i: (0, i))),
        out_shape=(jax.ShapeDtypeStruct((t, d), MXU_DTYPE), jax.ShapeDtypeStruct((d, t), MXU_DTYPE)),
        compiler_params=_cp("parallel"))(*args)


def mm_rms_bwd(a, bt, x, g, dres, name, tm=512, after=None, chunks=None):
    t, k = a.shape
    d = bt.shape[1]
    tm = min(tm if k > 3000 else 2 * tm, t)
    chunks = chunks or ((0, 0, k),)

    def body(a_ref, b_ref, x_ref, g_ref, dres_ref, *rest):
        dx_ref, dg_ref = rest[-2:]
        dhv = None
        for ca, cb, size in chunks:
            part = _mx(a_ref[:, ca:ca + size], b_ref[cb:cb + size, :])
            dhv = part if dhv is None else dhv + part
        xv = x_ref[...]
        r = lax.rsqrt(jnp.mean(xv * xv, axis=-1, keepdims=True) + EPS)
        xh = xv * r
        dxh = dhv * g_ref[...]
        dx_ref[...] = dres_ref[...] + r * (dxh - xh * jnp.mean(dxh * xh, axis=-1, keepdims=True))
        part = jnp.sum(dhv * xh, axis=0, keepdims=True)
        dg_ref[...] = jnp.where(pl.program_id(0) == 0, part, dg_ref[...] + part)

    row = pl.BlockSpec((tm, d), lambda i: (i, 0))
    one = pl.BlockSpec((1, d), lambda i: (0, 0))
    in_specs = [pl.BlockSpec((tm, k), lambda i: (i, 0)), pl.BlockSpec((k, d), lambda i: (0, 0)), row, one, row]
    args = [a, bt, x, g, dres]
    if after is not None:
        in_specs.append(pl.BlockSpec(memory_space=pl.ANY))
        args.append(after)
    return pl.pallas_call(
        body, name=name, grid=(t // tm,), in_specs=in_specs, out_specs=(row, one),
        out_shape=(jax.ShapeDtypeStruct((t, d), F32), jax.ShapeDtypeStruct((1, d), F32)),
        compiler_params=_cp("arbitrary"))(*args)


GU_TILE = 1408


def ffn_up(f, wt, name, tm=1024):
    t, d = f.shape
    tm = min(tm, t)
    nj = D_FF // GU_TILE

    def body(f_ref, wg_ref, wu_ref, gu_ref, a_ref, at_ref):
        g = _mx_nt(f_ref[...], wg_ref[...])
        u = _mx_nt(f_ref[...], wu_ref[...])
        sg = _sigmoid(g)
        gs = g * sg
        gu_ref[:, :GU_TILE] = (u * (sg + gs - gs * sg)).astype(gu_ref.dtype)
        gu_ref[:, GU_TILE:] = gs.astype(gu_ref.dtype)
        act = gs * u
        a_ref[...] = act.astype(a_ref.dtype)
        at_ref[...] = act.T.astype(at_ref.dtype)

    return pl.pallas_call(
        body, name=name, grid=(nj, t // tm),
        in_specs=[pl.BlockSpec((tm, d), lambda j, i: (i, 0)), pl.BlockSpec((GU_TILE, d), lambda j, i: (j, 0)),
                  pl.BlockSpec((GU_TILE, d), lambda j, i: (nj + j, 0))],
        out_specs=(pl.BlockSpec((tm, 2 * GU_TILE), lambda j, i: (i, j)), pl.BlockSpec((tm, GU_TILE), lambda j, i: (i, j)),
                   pl.BlockSpec((GU_TILE, tm), lambda j, i: (j, i))),
        out_shape=(jax.ShapeDtypeStruct((t, 2 * D_FF), MXU_DTYPE), jax.ShapeDtypeStruct((t, D_FF), MXU_DTYPE),
                   jax.ShapeDtypeStruct((D_FF, t), MXU_DTYPE)),
        compiler_params=_cp("parallel", "parallel"))(f, wt, wt)


_GU_CHUNKS = tuple((q * GU_TILE, ((q % 2) * (D_FF // GU_TILE) + q // 2) * GU_TILE, GU_TILE)
                   for q in range(2 * D_FF // GU_TILE))


def ffn_dact(dy, w_d, gu, name, tm=1024, after=None):
    t, d = dy.shape
    tm = min(tm, t)

    def body(dy_ref, w_ref, gu_ref, *rest):
        o_ref = rest[-1]
        da = _mx_nt(dy_ref[...], w_ref[...])
        o_ref[:, :GU_TILE] = (da * gu_ref[:, :GU_TILE]).astype(o_ref.dtype)
        o_ref[:, GU_TILE:] = (da * gu_ref[:, GU_TILE:]).astype(o_ref.dtype)

    in_specs = [pl.BlockSpec((tm, d), lambda j, i: (i, 0)), pl.BlockSpec((GU_TILE, d), lambda j, i: (j, 0)),
                pl.BlockSpec((tm, 2 * GU_TILE), lambda j, i: (i, j))]
    args = [dy, w_d, gu]
    if after is not None:
        in_specs.append(pl.BlockSpec(memory_space=pl.ANY))
        args.append(after)
    return pl.pallas_call(
        body, name=name, grid=(D_FF // GU_TILE, t // tm), in_specs=in_specs,
        out_specs=pl.BlockSpec((tm, 2 * GU_TILE), lambda j, i: (i, j)),
        out_shape=jax.ShapeDtypeStruct((t, 2 * D_FF), MXU_DTYPE), compiler_params=_cp("parallel", "parallel"))(*args)


def mm_nn_res_loss(a, b, res, target, name, tm=512):
    t, k = a.shape
    d = b.shape[1]
    tm = min(tm, t)

    def body(a_ref, b_ref, res_ref, t_ref, dy_ref, l_ref):
        e = res_ref[...] + _mx(a_ref[...], b_ref[...]) - t_ref[...]
        dy_ref[...] = e * (1.0 / d)
        part = jnp.zeros((1, 128), F32) + 0.5 * jnp.sum(jnp.mean(e * e, axis=-1, keepdims=True), axis=0, keepdims=True)
        l_ref[...] = jnp.where(pl.program_id(0) == 0, part, l_ref[...] + part)

    row = pl.BlockSpec((tm, d), lambda i: (i, 0))
    return pl.pallas_call(
        body, name=name, grid=(t // tm,),
        in_specs=[pl.BlockSpec((tm, k), lambda i: (i, 0)), pl.BlockSpec((k, d), lambda i: (0, 0)), row, row],
        out_specs=(row, pl.BlockSpec((1, 128), lambda i: (0, 0))),
        out_shape=(jax.ShapeDtypeStruct((t, d), F32), jax.ShapeDtypeStruct((1, 128), F32)),
        compiler_params=_cp("arbitrary"))(a, b, res, target)


QK_W = Q_W + KV_W
_QK_TILE = 256


def _qk_mats():
    idx = np.arange(_QK_TILE)
    half = HEAD_DIM // 2
    same = (idx[:, None] // HEAD_DIM) == (idx[None, :] // HEAD_DIM)
    lo = (idx % HEAD_DIM) < half
    rot = np.where((idx[:, None] == idx[None, :] + half) & lo[None, :], -1.0, 0.0)
    rot = rot + np.where((idx[:, None] == idx[None, :] - half) & ~lo[None, :], 1.0, 0.0)
    return jnp.asarray(same, F32), jnp.asarray(rot, F32)


def _qk_rows(q_gain, k_gain, cosf, sinf):
    gain = jnp.concatenate([q_gain] * ATTN_HEADS + [k_gain] * KV_HEADS, axis=-1)
    return gain, jnp.concatenate([cosf, cosf], axis=-1), jnp.concatenate([sinf, sinf], axis=-1)


def _qk_tiles(a, mat, transposed=False):
    outs = []
    for c0 in range(0, QK_W, _QK_TILE):
        w = min(_QK_TILE, QK_W - c0)
        mt = (mat.T if transposed else mat)[:w, :w].astype(MXU_DTYPE)
        at = a[:, c0:c0 + w]
        hi = at.astype(MXU_DTYPE)
        lo = (at - hi.astype(F32)).astype(MXU_DTYPE)
        outs.append(jnp.dot(hi, mt, preferred_element_type=F32) + jnp.dot(lo, mt, preferred_element_type=F32))
    return jnp.concatenate(outs, axis=-1)


def qk_prep_fwd(proj, q_gain, k_gain, cosf, sinf, name, tm=256):
    t = proj.shape[0]
    gmat, rmat = _qk_mats()
    gain, c2, s2 = _qk_rows(q_gain, k_gain, cosf, sinf)
    rep = QK_W // 128

    def body(p_ref, g_ref, c_ref, s_ref, gm_ref, rm_ref, q_ref, k_ref):
        x = p_ref[...]
        r = lax.rsqrt(_qk_tiles(x * x, gm_ref[...]) * (1.0 / HEAD_DIM) + EPS)
        xn = x * r * g_ref[...]
        c = jnp.concatenate([c_ref[...]] * rep, axis=-1)
        s = jnp.concatenate([s_ref[...]] * rep, axis=-1)
        out = xn * c + _qk_tiles(xn, rm_ref[...]) * s
        q_ref[...] = out[:, :Q_W]
        k_ref[...] = out[:, Q_W:]

    full = pl.BlockSpec((_QK_TILE, _QK_TILE), lambda i: (0, 0))
    tab = pl.BlockSpec((tm, 128), lambda i: (i, 0))
    return pl.pallas_call(
        body, name=name, grid=(t // tm,),
        in_specs=[pl.BlockSpec((tm, QK_W), lambda i: (i, 0)), pl.BlockSpec((1, QK_W), lambda i: (0, 0)), tab, tab,
                  full, full],
        out_specs=(pl.BlockSpec((tm, Q_W), lambda i: (i, 0)), pl.BlockSpec((tm, KV_W), lambda i: (i, 0))),
        out_shape=(jax.ShapeDtypeStruct((t, Q_W), F32), jax.ShapeDtypeStruct((t, KV_W), F32)),
        compiler_params=_cp("parallel"))(proj, gain, c2, s2, gmat, rmat)


def qk_prep_bwd(proj, q_gain, k_gain, cosf, sinf, dq, dk, name, tm=256):
    t = proj.shape[0]
    gmat, rmat = _qk_mats()
    gain, c2, s2 = _qk_rows(q_gain, k_gain, cosf, sinf)
    rep = QK_W // 128
    lanes = np.arange(QK_W)[:, None]
    fold = jnp.asarray(lanes % HEAD_DIM + np.where(lanes >= Q_W, HEAD_DIM, 0) == np.arange(128)[None, :], F32)

    def body(p_ref, g_ref, c_ref, s_ref, gm_ref, rm_ref, f_ref, dq_ref, dk_ref, o_ref, dg_ref):
        x = p_ref[...]
        r = lax.rsqrt(_qk_tiles(x * x, gm_ref[...]) * (1.0 / HEAD_DIM) + EPS)
        xh = x * r
        c = jnp.concatenate([c_ref[...]] * rep, axis=-1)
        s = jnp.concatenate([s_ref[...]] * rep, axis=-1)
        dout = jnp.concatenate([dq_ref[...], dk_ref[...]], axis=-1)
        dxn = dout * c + _qk_tiles(dout * s, rm_ref[...], transposed=True)
        part = _hi(jnp.sum(dxn * xh, axis=0, keepdims=True), f_ref[...])
        dxh = dxn * g_ref[...]
        mean = _qk_tiles(dxh * xh, gm_ref[...]) * (1.0 / HEAD_DIM)
        o_ref[...] = (r * (dxh - xh * mean)).astype(o_ref.dtype)
        dg_ref[...] = jnp.where(pl.program_id(0) == 0, part, dg_ref[...] + part)

    full = pl.BlockSpec((_QK_TILE, _QK_TILE), lambda i: (0, 0))
    tab = pl.BlockSpec((tm, 128), lambda i: (i, 0))
    dqk, dg = pl.pallas_call(
        body, name=name, grid=(t // tm,),
        in_specs=[pl.BlockSpec((tm, QK_W), lambda i: (i, 0)), pl.BlockSpec((1, QK_W), lambda i: (0, 0)), tab, tab,
                  full, full, pl.BlockSpec((QK_W, 128), lambda i: (0, 0)),
                  pl.BlockSpec((tm, Q_W), lambda i: (i, 0)), pl.BlockSpec((tm, KV_W), lambda i: (i, 0))],
        out_specs=(pl.BlockSpec((tm, QK_W), lambda i: (i, 0)), pl.BlockSpec((1, 128), lambda i: (0, 0))),
        out_shape=(jax.ShapeDtypeStruct((t, QK_W), MXU_DTYPE), jax.ShapeDtypeStruct((1, 128), F32)),
        compiler_params=_cp("arbitrary"))(proj, gain, c2, s2, gmat, rmat, fold, dq, dk)
    return dqk, dg[:, :HEAD_DIM], dg[:, HEAD_DIM:]


def _swa_valid(n, grp):
    qi = lax.broadcasted_iota(jnp.int32, (grp * ATTN_BLOCK, 2 * ATTN_BLOCK), 0) & (ATTN_BLOCK - 1)
    kj = lax.broadcasted_iota(jnp.int32, (grp * ATTN_BLOCK, 2 * ATTN_BLOCK), 1)
    diff = qi + ATTN_BLOCK - kj
    return (diff >= 0) & (diff < ATTN_BLOCK) & (n * ATTN_BLOCK - ATTN_BLOCK + kj >= 0)


def _stack_heads(ref, g, grp, rows=slice(None)):
    return jnp.concatenate([ref[rows, (g * grp + j) * HEAD_DIM:(g * grp + j + 1) * HEAD_DIM] for j in range(grp)], axis=0)


def _stack_sinks(s_ref, g, grp):
    return jnp.concatenate([jnp.zeros((ATTN_BLOCK, 1), F32) + s_ref[0:1, g * grp + j:g * grp + j + 1]
                            for j in range(grp)], axis=0)


SWA_STEP = 2


def swa_fwd(q, k, proj, sinks, name):
    t = q.shape[0]
    nb = t // ATTN_BLOCK
    scale = HEAD_DIM ** -0.5
    grp = ATTN_HEADS // KV_HEADS

    rows = SWA_STEP * ATTN_BLOCK

    def body(q_ref, kc_ref, kp_ref, vc_ref, vp_ref, s_ref, y_ref, mix_ref, yt_ref, lse_ref):
        n0 = pl.program_id(0) * SWA_STEP
        kk = jnp.concatenate([kp_ref[...], kc_ref[...]], axis=0).astype(MXU_DTYPE)
        vv = jnp.concatenate([vp_ref[...], vc_ref[...]], axis=0).astype(MXU_DTYPE)
        lane = lax.broadcasted_iota(jnp.int32, (ATTN_BLOCK, ATTN_HEADS), 1)
        units = [(b, g) for b in range(SWA_STEP) for g in range(KV_HEADS)]
        blk = lambda b: slice(b * ATTN_BLOCK, (b + 1) * ATTN_BLOCK)
        keys = lambda b: slice(b * ATTN_BLOCK, (b + 2) * ATTN_BLOCK)
        col = lambda g: slice(g * HEAD_DIM, (g + 1) * HEAD_DIM)
        valid = [_swa_valid(n0 + b, grp) for b in range(SWA_STEP)]
        qg = [_stack_heads(q_ref, g, grp, blk(b)) for b, g in units]
        sink = [_stack_sinks(s_ref, g, grp) for b, g in units]
        sc = [jnp.where(valid[b], _mx_nt(qg[u], kk[keys(b), col(g)]) * scale, NEG) for u, (b, g) in enumerate(units)]
        m = [jnp.maximum(jnp.max(sc_, axis=-1, keepdims=True), sk) for sc_, sk in zip(sc, sink)]
        e = [jnp.exp(sc_ - m_) for sc_, m_ in zip(sc, m)]
        den = [jnp.sum(e_, axis=-1, keepdims=True) + jnp.exp(sk - m_) for e_, sk, m_ in zip(e, sink, m)]
        og = [_mx(e[u] / den[u], vv[keys(b), col(g)]) for u, (b, g) in enumerate(units)]
        lg = [m_ + jnp.log(d_) for m_, d_ in zip(m, den)]
        for b in range(SWA_STEP):
            lse = jnp.zeros((ATTN_BLOCK, ATTN_HEADS), F32)
            outs = []
            for h in range(ATTN_HEADS):
                u = b * KV_HEADS + h // grp
                sub = blk(h % grp)
                outs.append(og[u][sub])
                lse = jnp.where(lane == h, lg[u][sub], lse)
            y = jnp.concatenate(outs, axis=-1)
            y_ref[blk(b), :] = y
            mix_ref[blk(b), :] = y.astype(mix_ref.dtype)
            yt_ref[:, blk(b)] = y.T.astype(yt_ref.dtype)
            lse_ref[blk(b), :] = lse

    cur = lambda n: (n, 0)
    prev = lambda n: (jnp.maximum(n * SWA_STEP - 1, 0), 0)
    vcol = (Q_W + KV_W) // KV_W
    return pl.pallas_call(
        body, name=name, grid=(nb // SWA_STEP,),
        in_specs=[pl.BlockSpec((rows, Q_W), cur), pl.BlockSpec((rows, KV_W), cur),
                  pl.BlockSpec((ATTN_BLOCK, KV_W), prev),
                  pl.BlockSpec((rows, KV_W), lambda n: (n, vcol)),
                  pl.BlockSpec((ATTN_BLOCK, KV_W), lambda n: (jnp.maximum(n * SWA_STEP - 1, 0), vcol)),
                  pl.BlockSpec((1, ATTN_HEADS), lambda n: (0, 0))],
        out_specs=(pl.BlockSpec((rows, Q_W), cur), pl.BlockSpec((rows, Q_W), cur),
                   pl.BlockSpec((Q_W, rows), lambda n: (0, n)), pl.BlockSpec((rows, ATTN_HEADS), cur)),
        out_shape=(jax.ShapeDtypeStruct((t, Q_W), F32), jax.ShapeDtypeStruct((t, Q_W + CONV_CH), MXU_DTYPE),
                   jax.ShapeDtypeStruct((Q_W + CONV_CH, t), MXU_DTYPE), jax.ShapeDtypeStruct((t, ATTN_HEADS), F32)),
        compiler_params=_cp("parallel"))(q, k, k, proj, proj, sinks)


def swa_bwd(q, k, proj, sinks, y, lse, dmix, name):
    t = q.shape[0]
    nb = t // ATTN_BLOCK
    scale = HEAD_DIM ** -0.5
    grp = ATTN_HEADS // KV_HEADS

    def body(q_ref, kc_ref, kp_ref, vc_ref, vp_ref, s_ref, y_ref, lse_ref, dy_ref,
             dq_ref, dk_ref, dv_ref, ds_ref, dkc, dvc):
        n = pl.program_id(0)

        @pl.when(n == 0)
        def _():
            dkc[...] = jnp.zeros_like(dkc)
            dvc[...] = jnp.zeros_like(dvc)
            ds_ref[...] = jnp.zeros_like(ds_ref)

        @pl.when(n < nb)
        def _():
            valid = _swa_valid(n, grp)
            kk = jnp.concatenate([kp_ref[...], kc_ref[...]], axis=0).astype(MXU_DTYPE)
            vv = jnp.concatenate([vp_ref[...], vc_ref[...]], axis=0).astype(MXU_DTYPE)
            lane = lax.broadcasted_iota(jnp.int32, (1, ATTN_HEADS), 1)
            gs = range(KV_HEADS)
            kg = [kk[:, g * HEAD_DIM:(g + 1) * HEAD_DIM] for g in gs]
            vg = [vv[:, g * HEAD_DIM:(g + 1) * HEAD_DIM] for g in gs]
            qg = [_stack_heads(q_ref, g, grp).astype(MXU_DTYPE) for g in gs]
            dog = [_stack_heads(dy_ref, g, grp) for g in gs]
            og = [_stack_heads(y_ref, g, grp) for g in gs]
            lg = [jnp.concatenate([lse_ref[:, g * grp + j:g * grp + j + 1] for j in range(grp)], axis=0) for g in gs]
            sink = [_stack_sinks(s_ref, g, grp) for g in gs]
            sc = [jnp.where(valid, _mx_nt(qg[g], kg[g]) * scale, NEG) for g in gs]
            p = [jnp.exp(sc[g] - lg[g]) for g in gs]
            delta = [jnp.sum(dog[g] * og[g], axis=-1, keepdims=True) for g in gs]
            ds = [p[g] * (_mx_nt(dog[g], vg[g]) - delta[g]) for g in gs]
            dqg = [_mx(ds[g], kg[g]) * scale for g in gs]
            dkf = jnp.concatenate([_mx_tn(ds[g], qg[g]) * scale for g in gs], axis=-1)
            dvf = jnp.concatenate([_mx_tn(p[g], dog[g]) for g in gs], axis=-1)
            dsk = [jnp.exp(sink[g] - lg[g]) * delta[g] for g in gs]
            dsink = jnp.zeros((1, ATTN_HEADS), F32)
            dqs = []
            for h in range(ATTN_HEADS):
                rows = slice((h % grp) * ATTN_BLOCK, (h % grp + 1) * ATTN_BLOCK)
                dqs.append(dqg[h // grp][rows])
                dsink = jnp.where(lane == h, -jnp.sum(dsk[h // grp][rows], axis=0, keepdims=True), dsink)
            dq_ref[...] = jnp.concatenate(dqs, axis=-1)
            dk_ref[...] = dkc[...] + dkf[:ATTN_BLOCK]
            dv_ref[...] = (dvc[...] + dvf[:ATTN_BLOCK]).astype(dv_ref.dtype)
            dkc[...] = dkf[ATTN_BLOCK:]
            dvc[...] = dvf[ATTN_BLOCK:]
            ds_ref[...] += dsink

        @pl.when(n == nb)
        def _():
            dk_ref[...] = dkc[...]
            dv_ref[...] = dvc[...].astype(dv_ref.dtype)

    cur = lambda n: (jnp.minimum(n, nb - 1), 0)
    prev = lambda n: (jnp.clip(n - 1, 0, nb - 1), 0)
    vcol = (Q_W + KV_W) // KV_W
    return pl.pallas_call(
        body, name=name, grid=(nb + 1,),
        in_specs=[pl.BlockSpec((ATTN_BLOCK, Q_W), cur), pl.BlockSpec((ATTN_BLOCK, KV_W), cur),
                  pl.BlockSpec((ATTN_BLOCK, KV_W), prev),
                  pl.BlockSpec((ATTN_BLOCK, KV_W), lambda n: (jnp.minimum(n, nb - 1), vcol)),
                  pl.BlockSpec((ATTN_BLOCK, KV_W), lambda n: (jnp.clip(n - 1, 0, nb - 1), vcol)),
                  pl.BlockSpec((1, ATTN_HEADS), lambda n: (0, 0)),
                  pl.BlockSpec((ATTN_BLOCK, Q_W), cur), pl.BlockSpec((ATTN_BLOCK, ATTN_HEADS), cur),
                  pl.BlockSpec((ATTN_BLOCK, Q_W), cur)],
        out_specs=(pl.BlockSpec((ATTN_BLOCK, Q_W), cur), pl.BlockSpec((ATTN_BLOCK, KV_W), prev),
                   pl.BlockSpec((ATTN_BLOCK, KV_W), prev), pl.BlockSpec((1, ATTN_HEADS), lambda n: (0, 0))),
        out_shape=(jax.ShapeDtypeStruct((t, Q_W), F32), jax.ShapeDtypeStruct((t, KV_W), F32),
                   jax.ShapeDtypeStruct((t, KV_W), MXU_DTYPE), jax.ShapeDtypeStruct((1, ATTN_HEADS), F32)),
        scratch_shapes=[pltpu.VMEM((ATTN_BLOCK, KV_W), F32), pltpu.VMEM((ATTN_BLOCK, KV_W), F32)],
        compiler_params=_cp("arbitrary"))(q, k, k, proj, proj, sinks, y, lse, dmix)


GC_W = 256
_GB0, _GC0, _XI0 = 768 // GC_W, 1280 // GC_W, 1792 // GC_W
HALO = 8


def gconv_fwd(proj, conv_w, mix, mix_t, name, tm=512):
    t = proj.shape[0]
    hb = tm // HALO
    half = Q_W // GC_W

    def body(gb_ref, gc_ref, xi_ref, gch_ref, xih_ref, w_ref, mix_in, mixt_in, y_ref, yt_ref):
        i = pl.program_id(1)
        u = gc_ref[...] * xi_ref[...]
        uh = jnp.where(i == 0, 0.0, gch_ref[...] * xih_ref[...])
        up = jnp.concatenate([uh, u], axis=0)
        cv = w_ref[0:1, :] * up[HALO - 2:HALO - 2 + tm]
        cv = cv + w_ref[1:2, :] * up[HALO - 1:HALO - 1 + tm]
        cv = cv + w_ref[2:3, :] * u
        y = gb_ref[...] * cv
        y_ref[...] = y.astype(y_ref.dtype)
        yt_ref[...] = y.T.astype(yt_ref.dtype)

    def col(c0):
        return pl.BlockSpec((tm, GC_W), lambda cj, i: (i, c0 + cj))

    def halo(c0):
        return pl.BlockSpec((HALO, GC_W), lambda cj, i: (jnp.maximum(i * hb - 1, 0), c0 + cj))

    return pl.pallas_call(
        body, name=name, grid=(CONV_CH // GC_W, t // tm),
        in_specs=[col(_GB0), col(_GC0), col(_XI0), halo(_GC0), halo(_XI0),
                  pl.BlockSpec((3, GC_W), lambda cj, i: (0, cj)),
                  pl.BlockSpec(memory_space=pl.ANY), pl.BlockSpec(memory_space=pl.ANY)],
        out_specs=(pl.BlockSpec((tm, GC_W), lambda cj, i: (i, half + cj)),
                   pl.BlockSpec((GC_W, tm), lambda cj, i: (half + cj, i))),
        out_shape=(jax.ShapeDtypeStruct(mix.shape, mix.dtype), jax.ShapeDtypeStruct(mix_t.shape, mix_t.dtype)),
        input_output_aliases={6: 0, 7: 1},
        compiler_params=_cp("parallel", "parallel"))(proj, proj, proj, proj, proj, conv_w, mix, mix_t)


def gconv_bwd(proj, conv_w, dmix, name, tm=512):
    t = proj.shape[0]
    hb = tm // HALO
    nt = t // tm
    dy0 = Q_W // GC_W

    def body(gb_ref, gc_ref, xi_ref, gch_ref, xih_ref, gbn_ref, dyn_ref, dy_ref, w_ref,
             dgb_ref, dgc_ref, dxi_ref, dw_ref):
        i = pl.program_id(1)
        gc, xi, gb, dy = gc_ref[...], xi_ref[...], gb_ref[...], dy_ref[...]
        u = gc * xi
        uh = jnp.where(i == 0, 0.0, gch_ref[...] * xih_ref[...])
        up = jnp.concatenate([uh, u], axis=0)
        u2 = up[HALO - 2:HALO - 2 + tm]
        u1 = up[HALO - 1:HALO - 1 + tm]
        cv = w_ref[0:1, :] * u2 + w_ref[1:2, :] * u1 + w_ref[2:3, :] * u
        dgb_ref[...] = (dy * cv).astype(dgb_ref.dtype)
        dcv = dy * gb
        dcvn = jnp.where(i == nt - 1, 0.0, dyn_ref[...] * gbn_ref[...])
        dcvp = jnp.concatenate([dcv, dcvn], axis=0)
        du = w_ref[0:1, :] * dcvp[2:2 + tm] + w_ref[1:2, :] * dcvp[1:1 + tm] + w_ref[2:3, :] * dcv
        dgc_ref[...] = (du * xi).astype(dgc_ref.dtype)
        dxi_ref[...] = (du * gc).astype(dxi_ref.dtype)
        dw = jnp.concatenate([jnp.sum(dcv * u2, axis=0, keepdims=True), jnp.sum(dcv * u1, axis=0, keepdims=True),
                              jnp.sum(dcv * u, axis=0, keepdims=True)], axis=0)

        @pl.when(i == 0)
        def _():
            dw_ref[...] = dw

        @pl.when(i > 0)
        def _():
            dw_ref[...] += dw

    def col(c0):
        return pl.BlockSpec((tm, GC_W), lambda cj, i: (i, c0 + cj))

    def halo(c0):
        return pl.BlockSpec((HALO, GC_W), lambda cj, i: (jnp.maximum(i * hb - 1, 0), c0 + cj))

    def nxt(c0):
        return pl.BlockSpec((HALO, GC_W), lambda cj, i: (jnp.minimum((i + 1) * hb, t // HALO - 1), c0 + cj))

    out = pl.BlockSpec((tm, GC_W), lambda cj, i: (i, cj))
    return pl.pallas_call(
        body, name=name, grid=(CONV_CH // GC_W, nt),
        in_specs=[col(_GB0), col(_GC0), col(_XI0), halo(_GC0), halo(_XI0), nxt(_GB0), nxt(dy0), col(dy0),
                  pl.BlockSpec((3, GC_W), lambda cj, i: (0, cj))],
        out_specs=(out, out, out, pl.BlockSpec((3, GC_W), lambda cj, i: (0, cj))),
        out_shape=(jax.ShapeDtypeStruct((t, CONV_CH), MXU_DTYPE),) * 3 + (jax.ShapeDtypeStruct((3, CONV_CH), F32),),
        compiler_params=_cp("parallel", "arbitrary"))(proj, proj, proj, proj, proj, proj, dmix, dmix, conv_w)


_QKV_W = 3 * DN_W
_BA_COL = (4 * DN_W) // 128
_Z_COL = _QKV_W // DN_W


def gdn_prep_fwd(proj, conv_w, alog_row, dtb_row, name, tm=256):
    t = proj.shape[0]
    hb = tm // HALO
    qscale = DN_DIM ** -0.5

    def body(x_ref, xh_ref, w_ref, ba_ref, al_ref, dt_ref, q_ref, k_ref, v_ref, bg_ref, c_ref):
        i = pl.program_id(0)
        for gi in range(3 * DN_HEADS):
            sl = slice(gi * DN_DIM, (gi + 1) * DN_DIM)
            xp = jnp.concatenate([jnp.where(i == 0, 0.0, xh_ref[:, sl]), x_ref[:, sl]], axis=0)
            c = w_ref[0:1, sl] * xp[HALO - 3:HALO - 3 + tm]
            for j in range(1, 4):
                c = c + w_ref[j:j + 1, sl] * xp[HALO - 3 + j:HALO - 3 + j + tm]
            c_ref[:, sl] = c
            s = c * _sigmoid(c)
            osl = slice((gi % DN_HEADS) * DN_DIM, (gi % DN_HEADS + 1) * DN_DIM)
            if gi < DN_HEADS:
                q_ref[:, osl] = s * lax.rsqrt(jnp.sum(s * s, axis=-1, keepdims=True) + EPS) * qscale
            elif gi < 2 * DN_HEADS:
                k_ref[:, osl] = s * lax.rsqrt(jnp.sum(s * s, axis=-1, keepdims=True) + EPS)
            else:
                v_ref[:, osl] = s
        ba = ba_ref[...]
        lane = lax.broadcasted_iota(jnp.int32, ba.shape, 1)
        gval = -jnp.exp(al_ref[...]) * _softplus(ba + dt_ref[...])
        bg_ref[...] = jnp.where(lane < DN_HEADS, _sigmoid(ba), jnp.where(lane < 2 * DN_HEADS, gval, 0.0))

    row = pl.BlockSpec((tm, DN_W), lambda i: (i, 0))
    one = pl.BlockSpec((1, 128), lambda i: (0, 0))
    return pl.pallas_call(
        body, name=name, grid=(t // tm,),
        in_specs=[pl.BlockSpec((tm, _QKV_W), lambda i: (i, 0)),
                  pl.BlockSpec((HALO, _QKV_W), lambda i: (jnp.maximum(i * hb - 1, 0), 0)),
                  pl.BlockSpec((4, _QKV_W), lambda i: (0, 0)),
                  pl.BlockSpec((tm, 128), lambda i: (i, _BA_COL)), one, one],
        out_specs=(row, row, row, pl.BlockSpec((tm, 128), lambda i: (i, 0)), pl.BlockSpec((tm, _QKV_W), lambda i: (i, 0))),
        out_shape=(jax.ShapeDtypeStruct((t, DN_W), F32),) * 3 + (jax.ShapeDtypeStruct((t, 128), F32),
                                                                 jax.ShapeDtypeStruct((t, _QKV_W), F32)),
        compiler_params=_cp("parallel"))(proj, proj, conv_w, proj, alog_row, dtb_row)


def gdn_prep_bwd(proj, conv, conv_w, alog_row, dtb_row, dq, dk, dv, dbg, dz, name, tm=256):
    t = proj.shape[0]
    hb = tm // HALO
    nt = t // tm
    qscale = DN_DIM ** -0.5
    te = tm + HALO

    def body(x_ref, c_ref, cn_ref, w_ref, ba_ref, al_ref, dt_ref, dq_ref, dk_ref, dv_ref,
             dqn_ref, dkn_ref, dvn_ref, dbg_ref, dz_ref, dx_ref, dw_ref, ddt_ref, dal_ref):
        i = pl.program_id(0)
        first = i == 0
        last = i == nt - 1
        dws = []
        for gi in range(3 * DN_HEADS):
            sl = slice(gi * DN_DIM, (gi + 1) * DN_DIM)
            osl = slice((gi % DN_HEADS) * DN_DIM, (gi % DN_HEADS + 1) * DN_DIM)
            c = jnp.concatenate([c_ref[:, sl], cn_ref[:, sl]], axis=0)
            sg = _sigmoid(c)
            s = c * sg
            d_ref, dn_ref = ((dq_ref, dqn_ref), (dk_ref, dkn_ref), (dv_ref, dvn_ref))[gi // DN_HEADS]
            dy = jnp.concatenate([d_ref[:, osl], jnp.where(last, 0.0, dn_ref[:, osl])], axis=0)
            if gi < 2 * DN_HEADS:
                r = lax.rsqrt(jnp.sum(s * s, axis=-1, keepdims=True) + EPS)
                sh = s * r
                ds = r * (dy - sh * jnp.sum(sh * dy, axis=-1, keepdims=True))
                if gi < DN_HEADS:
                    ds = ds * qscale
            else:
                ds = dy
            dc = ds * sg * (1.0 + c * (1.0 - sg))
            dcs = [dc[3 - j:3 - j + tm] for j in range(4)]
            dx = w_ref[0:1, sl] * dcs[0]
            for j in range(1, 4):
                dx = dx + w_ref[j:j + 1, sl] * dcs[j]
            dx_ref[:, sl] = dx.astype(dx_ref.dtype)
            x0 = x_ref[:, sl]
            dws.append(jnp.concatenate([jnp.sum(dcs[j] * x0, axis=0, keepdims=True) for j in range(4)], axis=0))
        dw = jnp.concatenate(dws, axis=-1)
        ba = ba_ref[...]
        dbgv = dbg_ref[...]
        lane = lax.broadcasted_iota(jnp.int32, ba.shape, 1)
        beta = _sigmoid(ba)
        ea = -jnp.exp(al_ref[...])
        zin = ba + dt_ref[...]
        is_b = lane < DN_HEADS
        is_a = (lane >= DN_HEADS) & (lane < 2 * DN_HEADS)
        da = jnp.where(is_a, dbgv * ea * _sigmoid(zin), 0.0)
        dx_ref[:, _QKV_W:_QKV_W + DN_W] = dz_ref[...]
        dx_ref[:, _QKV_W + DN_W:] = jnp.where(is_b, dbgv * beta * (1.0 - beta), da).astype(dx_ref.dtype)
        ddt = jnp.sum(da, axis=0, keepdims=True)
        dal = jnp.sum(jnp.where(is_a, dbgv * ea * _softplus(zin), 0.0), axis=0, keepdims=True)

        @pl.when(first)
        def _():
            dw_ref[...] = dw
            ddt_ref[...] = ddt
            dal_ref[...] = dal

        @pl.when(i > 0)
        def _():
            dw_ref[...] += dw
            ddt_ref[...] += ddt
            dal_ref[...] += dal

    row = pl.BlockSpec((tm, DN_W), lambda i: (i, 0))
    nrow = pl.BlockSpec((HALO, DN_W), lambda i: (jnp.minimum((i + 1) * hb, t // HALO - 1), 0))
    one = pl.BlockSpec((1, 128), lambda i: (0, 0))
    return pl.pallas_call(
        body, name=name, grid=(nt,),
        in_specs=[pl.BlockSpec((tm, _QKV_W), lambda i: (i, 0)), pl.BlockSpec((tm, _QKV_W), lambda i: (i, 0)),
                  pl.BlockSpec((HALO, _QKV_W), lambda i: (jnp.minimum((i + 1) * hb, t // HALO - 1), 0)),
                  pl.BlockSpec((4, _QKV_W), lambda i: (0, 0)),
                  pl.BlockSpec((tm, 128), lambda i: (i, _BA_COL)), one, one,
                  row, row, row, nrow, nrow, nrow, pl.BlockSpec((tm, 128), lambda i: (i, 0)), row],
        out_specs=(pl.BlockSpec((tm, ODD_IN_PAD), lambda i: (i, 0)), pl.BlockSpec((4, _QKV_W), lambda i: (0, 0)), one, one),
        out_shape=(jax.ShapeDtypeStruct((t, ODD_IN_PAD), MXU_DTYPE), jax.ShapeDtypeStruct((4, _QKV_W), F32),
                   jax.ShapeDtypeStruct((1, 128), F32), jax.ShapeDtypeStruct((1, 128), F32)),
        compiler_params=_cp("arbitrary"))(proj, conv, conv, conv_w, proj, alog_row, dtb_row, dq, dk, dv, dq, dk, dv, dbg,
                                          dz)


def _chunk_masks():
    r = lax.broadcasted_iota(jnp.int32, (DN_CHUNK, DN_CHUNK), 0)
    c = lax.broadcasted_iota(jnp.int32, (DN_CHUNK, DN_CHUNK), 1)
    return r >= c, r > c


INV_PACK = 2


def _inv_unit_lower_many(mats):
    n = DN_CHUNK
    wide = INV_PACK * n
    r = lax.broadcasted_iota(jnp.int32, (wide, wide), 0)
    c = lax.broadcasted_iota(jnp.int32, (wide, wide), 1)
    same = (r & -n) == (c & -n)
    eye = jnp.where((r[:n] == (c[:n] & (n - 1))), 1.0, 0.0)

    def blockdiag(row):
        return jnp.where(same, jnp.concatenate([row] * INV_PACK, axis=0), 0.0)

    packs = [jnp.concatenate(mats[g:g + INV_PACK], axis=-1) for g in range(0, len(mats), INV_PACK)]
    xs = [eye - a for a in packs]
    pws = [_hi(a, blockdiag(a)) for a in packs]
    for step in range(5):
        if step < 4:
            both = [_hi(jnp.concatenate([x, pw], axis=0), blockdiag(pw)) for x, pw in zip(xs, pws)]
            xs = [x + b[:n] for x, b in zip(xs, both)]
            pws = [b[n:] for b in both]
        else:
            xs = [x + _hi(x, blockdiag(pw)) for x, pw in zip(xs, pws)]
    return [x[:, j * n:(j + 1) * n] for x in xs for j in range(INV_PACK)]


def _chunk_common(q, k, beta, gc, gcr, lower):
    gam = jnp.exp(jnp.where(lower, gc - gcr, NEG))
    eg = jnp.exp(gc)
    gl = gc[DN_CHUNK - 1:DN_CHUNK, :]
    kdf = jnp.exp(gl - gc)
    kb = k * beta
    bmat = _mx_nt(kb, k)
    qmat = _mx_nt(q, k)
    return gam, eg, jnp.exp(gl), kdf, kb, bmat, qmat


DN_STEP = 4


def gdn_fwd(q, k, v, bg, proj, o_gain, name):
    t = q.shape[0]
    n_chunks = t // DN_CHUNK

    def body(q_ref, k_ref, v_ref, bg_ref, z_ref, g_ref, o_ref, sall_ref, tall_ref, y_ref, yt_ref, s_ref):
        n = pl.program_id(0)

        @pl.when(n == 0)
        def _():
            s_ref[...] = jnp.zeros_like(s_ref)

        lower, strict = _chunk_masks()
        ltri = jnp.where(lower, 1.0, 0.0)
        hs = range(DN_HEADS)
        sl = [slice(h * DN_DIM, (h + 1) * DN_DIM) for h in hs]
        units = [(c, h) for c in range(DN_STEP) for h in hs]
        nu = range(len(units))
        rs = [slice(c * DN_CHUNK, (c + 1) * DN_CHUNK) for c in range(DN_STEP)]
        bgv = [bg_ref[rs[c], :] for c in range(DN_STEP)]
        gcs = [_hi(ltri, b) for b in bgv]
        gcs_t = [g.T for g in gcs]
        qh = [q_ref[rs[c], sl[h]] for c, h in units]
        kh = [k_ref[rs[c], sl[h]] for c, h in units]
        vh = [v_ref[rs[c], sl[h]] for c, h in units]
        beta = [bgv[c][:, h:h + 1] for c, h in units]
        com = [_chunk_common(qh[u], kh[u], beta[u], gcs[c][:, DN_HEADS + h:DN_HEADS + h + 1],
                             gcs_t[c][DN_HEADS + h:DN_HEADS + h + 1, :], lower) for u, (c, h) in enumerate(units)]
        gam, eg, dec, kdf, kb, bmat, qmat = zip(*com)
        tms = _inv_unit_lower_many([jnp.where(strict, bmat[u] * gam[u], 0.0) for u in nu])
        for u, (c, h) in enumerate(units):
            tall_ref[c, h] = tms[u]
        uw = [_hi(tms[u], jnp.concatenate([vh[u] * beta[u], kb[u] * eg[u]], axis=-1)) for u in nu]
        qd = [qh[u] * eg[u] for u in nu]
        pm = [qmat[u] * gam[u] for u in nu]
        kd = [kh[u] * kdf[u] for u in nu]
        st = [s_ref[h] for h in hs]
        for c in range(DN_STEP):
            us = [c * DN_HEADS + h for h in hs]
            for h in hs:
                sall_ref[c, h] = st[h]
            v_new = [uw[us[h]][:, :DN_DIM] - _mx(uw[us[h]][:, DN_DIM:], st[h]) for h in hs]
            o_st = [_mx(qd[us[h]], st[h]) for h in hs]
            o_in = [_mx(pm[us[h]], v_new[h]) for h in hs]
            s_up = [_mx_tn(kd[us[h]], v_new[h]) for h in hs]
            for h in hs:
                ov = o_st[h] + o_in[h]
                o_ref[rs[c], sl[h]] = ov
                zv = z_ref[rs[c], sl[h]]
                y = ov * lax.rsqrt(jnp.mean(ov * ov, axis=-1, keepdims=True) + EPS) * g_ref[...] * (zv * _sigmoid(zv))
                y_ref[rs[c], sl[h]] = y.astype(y_ref.dtype)
                yt_ref[sl[h], rs[c]] = y.T.astype(yt_ref.dtype)
            st = [st[h] * dec[us[h]] + s_up[h] for h in hs]
        for h in hs:
            s_ref[h] = st[h]

    rows = DN_STEP * DN_CHUNK
    row = pl.BlockSpec((rows, DN_W), lambda n: (n, 0))
    return pl.pallas_call(
        body, name=name, grid=(n_chunks // DN_STEP,),
        in_specs=[row, row, row, pl.BlockSpec((rows, 128), lambda n: (n, 0)),
                  pl.BlockSpec((rows, DN_W), lambda n: (n, _Z_COL)), pl.BlockSpec((1, DN_DIM), lambda n: (0, 0))],
        out_specs=(row, pl.BlockSpec((DN_STEP, DN_HEADS, DN_DIM, DN_DIM), lambda n: (n, 0, 0, 0)),
                   pl.BlockSpec((DN_STEP, DN_HEADS, DN_CHUNK, DN_CHUNK), lambda n: (n, 0, 0, 0)),
                   row, pl.BlockSpec((DN_W, rows), lambda n: (0, n))),
        out_shape=(jax.ShapeDtypeStruct((t, DN_W), F32),
                   jax.ShapeDtypeStruct((n_chunks, DN_HEADS, DN_DIM, DN_DIM), F32),
                   jax.ShapeDtypeStruct((n_chunks, DN_HEADS, DN_CHUNK, DN_CHUNK), F32),
                   jax.ShapeDtypeStruct((t, DN_W), MXU_DTYPE), jax.ShapeDtypeStruct((DN_W, t), MXU_DTYPE)),
        scratch_shapes=[pltpu.VMEM((DN_HEADS, DN_DIM, DN_DIM), F32)],
        compiler_params=_cp("arbitrary"))(q, k, v, bg, proj, o_gain)


def gdn_bwd(q, k, v, bg, sall, tall, do, name):
    t = q.shape[0]
    n_chunks = t // DN_CHUNK

    def body(q_ref, k_ref, v_ref, bg_ref, sall_ref, tall_ref, do_ref, dq_ref, dk_ref, dv_ref, dbg_ref, ds_ref):
        n = pl.program_id(0)

        @pl.when(n == 0)
        def _():
            ds_ref[...] = jnp.zeros_like(ds_ref)

        lower, strict = _chunk_masks()
        ltri = jnp.where(lower, 1.0, 0.0)
        bgv = bg_ref[...]
        gcs = _hi(ltri, bgv)
        gcs_t = gcs.T
        lane = lax.broadcasted_iota(jnp.int32, (DN_CHUNK, 128), 1)
        rowi = lax.broadcasted_iota(jnp.int32, (DN_CHUNK, 1), 0)
        hs = range(DN_HEADS)
        each = lambda fn, *ls: [fn(*a) for a in zip(*ls)]
        rsum = lambda a: jnp.sum(a, axis=-1, keepdims=True)
        sl = [slice(h * DN_DIM, (h + 1) * DN_DIM) for h in hs]
        st = [sall_ref[0, h] for h in hs]
        tms = [tall_ref[0, h] for h in hs]
        dsn = [ds_ref[h] for h in hs]
        qh = [q_ref[:, sl[h]] for h in hs]
        kh = [k_ref[:, sl[h]] for h in hs]
        vh = [v_ref[:, sl[h]] for h in hs]
        doh = [do_ref[:, sl[h]] for h in hs]
        beta = [bgv[:, h:h + 1] for h in hs]
        com = [_chunk_common(qh[h], kh[h], beta[h], gcs[:, DN_HEADS + h:DN_HEADS + h + 1],
                             gcs_t[DN_HEADS + h:DN_HEADS + h + 1, :], lower) for h in hs]
        gam, eg, dec, kdf, kb, bmat, qmat = zip(*com)
        rhs_w = each(lambda a, b: a * b, kb, eg)
        uw = each(lambda t_, v_, b_, r_: _hi(t_, jnp.concatenate([v_ * b_, r_], axis=-1)), tms, vh, beta, rhs_w)
        qd = each(lambda a, b: a * b, qh, eg)
        kd = each(lambda a, b: a * b, kh, kdf)
        pmat = each(lambda a, b: a * b, qmat, gam)
        v_new = each(lambda uw_, s_: uw_[:, :DN_DIM] - _mx(uw_[:, DN_DIM:], s_), uw, st)
        dqd = each(_mx_nt, doh, st)
        ds_o = each(_mx_tn, qd, doh)
        dp = each(lambda d_, v_: jnp.where(lower, _mx_nt(d_, v_), 0.0), doh, v_new)
        dvn_o = each(_mx_tn, pmat, doh)
        ddec = each(lambda d_, s_: jnp.sum(rsum(d_ * s_), axis=0, keepdims=True), dsn, st)
        dkd = each(_mx_nt, v_new, dsn)
        dvn = each(lambda a, k_, d_: a + _mx(k_, d_), dvn_o, kd, dsn)
        dw = each(lambda d_, s_: -_mx_nt(d_, s_), dvn, st)
        ds_w = each(lambda uw_, d_: _mx_tn(uw_[:, DN_DIM:], d_), uw, dvn)
        for h in hs:
            ds_ref[h] = ds_o[h] + dec[h] * dsn[h] - ds_w[h]
        dr = each(lambda t_, a, b: _hi_tn(t_, jnp.concatenate([a, b], axis=-1)), tms, dvn, dw)
        da = each(lambda r_, uw_: jnp.where(strict, -_hi_nt(r_, uw_), 0.0), dr, uw)
        dru = [r_[:, :DN_DIM] for r_ in dr]
        drw = [r_[:, DN_DIM:] for r_ in dr]
        db = each(lambda a, b: a * b, da, gam)
        dq_m = each(lambda a, b: a * b, dp, gam)
        e = each(lambda a, bm, p_, qm, g_: (a * bm + p_ * qm) * g_, da, bmat, dp, qmat, gam)
        dkb = each(lambda b_, k_, r_, e_: _mx(b_, k_) + r_ * e_, db, kh, drw, eg)
        dk = each(lambda b_, kb_, m_, q_, d_, f_: _mx_tn(b_, kb_) + _mx_tn(m_, q_) + d_ * f_, db, kb, dq_m, qh, dkd, kdf)
        dq = each(lambda m_, k_, d_, e_: _mx(m_, k_) + d_ * e_, dq_m, kh, dqd, eg)
        tk = each(lambda a, b: rsum(a * b), dkd, kd)
        dbeta_all = jnp.zeros((DN_CHUNK, 128), F32)
        dgc_all = jnp.zeros((DN_CHUNK, 128), F32)
        for h in hs:
            dgc = (jnp.sum(e[h], axis=1, keepdims=True) - jnp.sum(e[h].T, axis=1, keepdims=True)
                   + rsum(dqd[h] * qd[h]) - tk[h] + rsum(drw[h] * rhs_w[h]))
            dgl = jnp.sum(tk[h], axis=0, keepdims=True) + ddec[h] * dec[h]
            dgc = dgc + jnp.where(rowi == DN_CHUNK - 1, dgl, 0.0)
            dbeta = rsum(dru[h] * vh[h]) + rsum(dkb[h] * kh[h])
            dq_ref[:, sl[h]] = dq[h]
            dk_ref[:, sl[h]] = dk[h] + dkb[h] * beta[h]
            dv_ref[:, sl[h]] = dru[h] * beta[h]
            dbeta_all = jnp.where(lane == h, dbeta, dbeta_all)
            dgc_all = jnp.where(lane == DN_HEADS + h, dgc, dgc_all)
        dbg_ref[...] = dbeta_all + _hi_tn(ltri, dgc_all)

    rev = lambda n: (n_chunks - 1 - n, 0)
    row = pl.BlockSpec((DN_CHUNK, DN_W), rev)
    small = pl.BlockSpec((DN_CHUNK, 128), rev)
    return pl.pallas_call(
        body, name=name, grid=(n_chunks,),
        in_specs=[row, row, row, small,
                  pl.BlockSpec((1, DN_HEADS, DN_DIM, DN_DIM), lambda n: (n_chunks - 1 - n, 0, 0, 0)),
                  pl.BlockSpec((1, DN_HEADS, DN_CHUNK, DN_CHUNK), lambda n: (n_chunks - 1 - n, 0, 0, 0)), row],
        out_specs=(row, row, row, small),
        out_shape=(jax.ShapeDtypeStruct((t, DN_W), F32),) * 3 + (jax.ShapeDtypeStruct((t, 128), F32),),
        scratch_shapes=[pltpu.VMEM((DN_HEADS, DN_DIM, DN_DIM), F32)],
        compiler_params=_cp("arbitrary"))(q, k, v, bg, sall, tall, do)


def gdn_out_bwd(o, proj, o_gain, dx, w_out, name, tm=512, after=None):
    t = o.shape[0]
    tm = min(tm, t)

    def body(o_ref, z_ref, g_ref, dx_ref, w_ref, *rest):
        do_ref, dz_ref, dg_ref = rest[-3:]
        i = pl.program_id(0)
        dy = _mx_nt(dx_ref[...], w_ref[...])
        dg = jnp.zeros((1, DN_DIM), F32)
        for h in range(DN_HEADS):
            sl = slice(h * DN_DIM, (h + 1) * DN_DIM)
            ov, zv, dyv = o_ref[:, sl], z_ref[:, sl], dy[:, sl]
            r = lax.rsqrt(jnp.mean(ov * ov, axis=-1, keepdims=True) + EPS)
            oh = ov * r
            sg = _sigmoid(zv)
            dz_ref[:, sl] = (dyv * oh * g_ref[...] * sg * (1.0 + zv * (1.0 - sg))).astype(dz_ref.dtype)
            don = dyv * (zv * sg)
            dg = dg + jnp.sum(don * oh, axis=0, keepdims=True)
            doh = don * g_ref[...]
            do_ref[:, sl] = r * (doh - oh * jnp.mean(doh * oh, axis=-1, keepdims=True))

        @pl.when(i == 0)
        def _():
            dg_ref[...] = dg

        @pl.when(i > 0)
        def _():
            dg_ref[...] += dg

    row = pl.BlockSpec((tm, DN_W), lambda i: (i, 0))
    one = pl.BlockSpec((1, DN_DIM), lambda i: (0, 0))
    in_specs = [row, pl.BlockSpec((tm, DN_W), lambda i: (i, _Z_COL)), one,
                pl.BlockSpec((tm, dx.shape[1]), lambda i: (i, 0)), pl.BlockSpec(w_out.shape, lambda i: (0, 0))]
    args = [o, proj, o_gain, dx, w_out]
    if after is not None:
        in_specs.append(pl.BlockSpec(memory_space=pl.ANY))
        args.append(after)
    return pl.pallas_call(
        body, name=name, grid=(t // tm,), in_specs=in_specs, out_specs=(row, row, one),
        out_shape=(jax.ShapeDtypeStruct((t, DN_W), F32), jax.ShapeDtypeStruct((t, DN_W), MXU_DTYPE),
                   jax.ShapeDtypeStruct((1, DN_DIM), F32)),
        compiler_params=_cp("arbitrary"))(*args)


def _peer(k):
    x, y, c = lax.axis_index("x"), lax.axis_index("y"), lax.axis_index("c")
    px = 1 - x if k & 4 else x
    py = 1 - y if k & 2 else y
    pc = 1 - c if k & 1 else c
    return (px, py, pc), 4 * px + 2 * py + pc


_HBM = pl.BlockSpec(memory_space=pltpu.HBM)
_SEM = pl.BlockSpec(memory_space=pltpu.SEMAPHORE)
_DATAFLOW = pltpu.SideEffectType.DATAFLOW_SIDE_EFFECTING
N_PEER = N_DEV - 1


def send_start(srcs, name, scatter, after):
    na = len(srcs)
    ns = (2 * N_PEER + 1) * na
    lands = [lax.empty((N_DEV,) + (s.shape[1:] if scatter else s.shape), s.dtype) for s in srcs]
    extra = [] if after is None else [after]

    def body(*refs):
        src_refs, land_refs = refs[:na], refs[na:2 * na]
        sems = refs[2 * na + len(extra):2 * na + len(extra) + ns]
        land_out, token = refs[-1 - na:-1], refs[-1]
        _, me = _peer(0)
        for a in range(na):
            pltpu.make_async_copy(src_refs[a].at[me] if scatter else src_refs[a], land_out[a].at[me],
                                  sems[2 * N_PEER * na + a]).start()
        for k in range(1, N_DEV):
            peer, pid = _peer(k)
            for a in range(na):
                pltpu.make_async_remote_copy(
                    src_ref=src_refs[a].at[pid] if scatter else src_refs[a], dst_ref=land_refs[a].at[me],
                    send_sem=sems[2 * (a * N_PEER + k - 1)], recv_sem=sems[2 * (a * N_PEER + k - 1) + 1],
                    device_id=peer, device_id_type=MESH).start()
        token[...] = jnp.zeros_like(token)

    hbm = lambda arrs: tuple(pltpu.HBM(a.shape, a.dtype) for a in arrs)
    outs = pl.pallas_call(
        body, name=name,
        out_shape=(pltpu.SemaphoreType.DMA(()),) * ns + hbm(srcs) + hbm(lands) + (jax.ShapeDtypeStruct((8, 128), F32),),
        in_specs=[_HBM] * (2 * na) + [pl.BlockSpec(memory_space=pl.ANY)] * len(extra),
        out_specs=(_SEM,) * ns + (_HBM,) * (2 * na) + (pl.BlockSpec(memory_space=pltpu.VMEM),),
        input_output_aliases={i: ns + i for i in range(2 * na)},
        compiler_params=pltpu.CompilerParams(has_side_effects=_DATAFLOW),
    )(*[pltpu.with_memory_space_constraint(a, pltpu.HBM) for a in list(srcs) + lands], *extra)
    return outs[:ns], outs[ns:ns + na], outs[ns + na:ns + 2 * na], outs[-1]


def send_wait(sems, srcs_thru, lands_thru, name, scatter, after):
    na = len(srcs_thru)
    ns = (2 * N_PEER + 1) * na

    def body(*refs):
        src_refs, land_refs, sm = refs[:na], refs[na:2 * na], refs[2 * na:2 * na + ns]
        _, me = _peer(0)
        for a in range(na):
            pltpu.make_async_copy(src_refs[a].at[me] if scatter else src_refs[a], land_refs[a].at[me],
                                  sm[2 * N_PEER * na + a]).wait()
        for k in range(1, N_DEV):
            peer, pid = _peer(k)
            for a in range(na):
                cp = pltpu.make_async_remote_copy(
                    src_ref=src_refs[a].at[pid] if scatter else src_refs[a], dst_ref=land_refs[a].at[pid],
                    send_sem=sm[2 * (a * N_PEER + k - 1)], recv_sem=sm[2 * (a * N_PEER + k - 1) + 1],
                    device_id=peer, device_id_type=MESH)
                cp.wait_send()
                cp.wait_recv()

    hbm = lambda arrs: tuple(pltpu.HBM(a.shape, a.dtype) for a in arrs)
    outs = pl.pallas_call(
        body, name=name, out_shape=hbm(srcs_thru) + hbm(lands_thru),
        in_specs=[_HBM] * (2 * na) + [_SEM] * ns + [pl.BlockSpec(memory_space=pl.ANY)], out_specs=(_HBM,) * (2 * na),
        input_output_aliases={i: i for i in range(2 * na)},
        compiler_params=pltpu.CompilerParams(has_side_effects=_DATAFLOW),
    )(*srcs_thru, *lands_thru, *sems, after)
    return outs[na:]


def _adamw(w, g, m, v):
    m = ADAM_B1 * m + (1.0 - ADAM_B1) * g
    v = ADAM_B2 * v + (1.0 - ADAM_B2) * (g * g)
    m_hat = m / (1.0 - ADAM_B1 ** ADAM_STEP)
    v_hat = v / (1.0 - ADAM_B2 ** ADAM_STEP)
    return -ADAM_LR * (m_hat / (jnp.sqrt(v_hat) + ADAM_EPS) + ADAM_WD * w), m, v


def adam_sum(w, pieces, m, v, name, layer=0, into=None):
    nl, r, c = w.shape
    tr = r
    for cand in (256, 128, 64, 32, 16, 8):
        if r % cand == 0:
            tr = cand
            break

    def body(w_ref, p_ref, m_ref, v_ref, *rest):
        g_ref, d_ref, nm_ref, nv_ref = rest[-4:]
        g = p_ref[0].astype(F32)
        for s in range(1, N_DEV):
            g = g + p_ref[s].astype(F32)
        g_ref[0] = g
        d_ref[0], nm_ref[0], nv_ref[0] = _adamw(w_ref[0], g, m_ref[0], v_ref[0])

    row = pl.BlockSpec((1, tr, c), lambda i: (layer, i, 0))
    out = jax.ShapeDtypeStruct((nl, r, c), F32)
    extra = [] if into is None else list(into)
    return pl.pallas_call(
        body, name=name, grid=(r // tr,),
        in_specs=[row, pl.BlockSpec((N_DEV, tr, c), lambda i: (0, i, 0)), row, row]
        + [pl.BlockSpec(memory_space=pl.ANY)] * len(extra),
        out_specs=(row,) * 4, out_shape=(out,) * 4,
        input_output_aliases={4 + i: i for i in range(len(extra))},
        compiler_params=_cp("parallel"))(w, pieces, m, v, *extra)


def sum_rows(gathered, name):
    _, r, c = gathered.shape

    def body(p_ref, o_ref):
        g = p_ref[0]
        for s in range(1, N_DEV):
            g = g + p_ref[s]
        o_ref[...] = g

    return pl.pallas_call(body, name=name, out_shape=jax.ShapeDtypeStruct((r, c), F32))(gathered)


def adam_small(w, g, m, v, name):
    def body(w_ref, g_ref, m_ref, v_ref, d_ref, nm_ref, nv_ref):
        d_ref[...], nm_ref[...], nv_ref[...] = _adamw(w_ref[...], g_ref[...], m_ref[...], v_ref[...])

    out = jax.ShapeDtypeStruct(w.shape, F32)
    return pl.pallas_call(body, name=name, out_shape=(out,) * 3)(w, g, m, v)


def _rope_tables(t):
    inv_freq = 10000.0 ** (-jnp.arange(0, HEAD_DIM, 2, dtype=F32) / HEAD_DIM)
    ang = jnp.arange(t, dtype=F32)[:, None] * inv_freq[None, :]
    cos, sin = jnp.cos(ang), jnp.sin(ang)
    return jnp.concatenate([cos, cos], axis=-1), jnp.concatenate([sin, sin], axis=-1)


def _lane_row(vec8):
    return jnp.pad(vec8.reshape(1, DN_HEADS), ((0, 0), (DN_HEADS, 128 - 2 * DN_HEADS)))


def _ffn_bwd(x, norm_g, w_gu, w_d, saved, dy, tag, after=None):
    ft, gu, at = saved
    dgu = ffn_dact(dy, w_d, gu, f"{tag}_d_gate_up", after=after)
    dwd = mm_at(at, dy, f"{tag}_dw_down")
    nj = D_FF // GU_TILE
    dwgu = mm_at(ft, dgu, f"{tag}_dw_gate_up", transposed=True, tn=GU_TILE, row_block=lambda q: (q % 2) * nj + q // 2)
    dx, dg = mm_rms_bwd(dgu, w_gu, x, norm_g, dy, f"{tag}_d_norm", chunks=_GU_CHUNKS)
    return dx, dwgu, dwd, dg


def local_step(x, target, small, weights_of, grads_out, after=None):
    t = x.shape[0]
    cosf, sinf = _rope_tables(t)
    alog_row, dtb_row = _lane_row(small["odd_a_log"]), _lane_row(small["odd_dt_bias"])

    h0, h0t = rms_fwd(x, small["even_norm"], "even_norm", after=after)
    we = weights_of("even", h0)
    small = {**small, **we.get("small", {})}
    proj0 = mm_nt(h0, we["w_in"], "even_in_proj")
    qr, kr = qk_prep_fwd(proj0, small["even_q_gain"], small["even_k_gain"], cosf, sinf, "even_qk_prep")
    y_attn, mix0, mix0_t, lse = swa_fwd(qr, kr, proj0, small["even_sinks"], "even_swa")
    mix0, mix0_t = gconv_fwd(proj0, small["even_conv_w"], mix0, mix0_t, "even_gconv")
    we = {**we, **weights_of("even_out", mix0)}
    x1, f0, f0t = mm_nn_res_norm(mix0, we["w_out"], x, small["ffn_norm0"], "even_out_proj")
    w0 = weights_of("ffn0", x1)
    gu0, a0, a0t = ffn_up(f0, w0["gate_up"], "ffn0_gate_up")
    ffn0 = (f0t, gu0, a0t)
    x2, h1, h1t = mm_nn_res_norm(a0, w0["down"], x1, small["odd_norm"], "ffn0_down")

    wo = weights_of("odd", x2)
    proj1 = mm_nt(h1, wo["w_in"], "odd_in_proj")
    qn, kn, vs, bg, conv1 = gdn_prep_fwd(proj1, small["odd_conv_w"], alog_row, dtb_row, "odd_prep")
    o, sall, tall, og, ogt = gdn_fwd(qn, kn, vs, bg, proj1, small["odd_o_gain"], "odd_delta_rule")
    x3, f1, f1t = mm_nn_res_norm(og, wo["w_out"], x2, small["ffn_norm1"], "odd_out_proj")
    w1 = weights_of("ffn1", x3)
    gu1, a1, a1t = ffn_up(f1, w1["gate_up"], "ffn1_gate_up")
    ffn1 = (f1t, gu1, a1t)
    dy, loss_row = mm_nn_res_loss(a1, w1["down"], x3, target, "ffn1_down_loss")

    gs = {}
    dx3, dwgu, dwd, gs["ffn_norm1"] = _ffn_bwd(x3, small["ffn_norm1"], w1["gate_up"], w1["down"], ffn1, dy, "ffn1")
    tok = grads_out("ffn1", {"gate_up": dwgu, "down": dwd})

    do, dz, gs["odd_o_gain"] = gdn_out_bwd(o, proj1, small["odd_o_gain"], dx3, wo["w_out"], "odd_d_gate_norm", after=tok)
    dwo = mm_at(ogt, dx3, "odd_dw_out")
    dqn, dkn, dvs, dbg = gdn_bwd(qn, kn, vs, bg, sall, tall, do, "odd_d_delta_rule")
    dproj1, gs["odd_conv_w"], ddt_row, dal_row = gdn_prep_bwd(
        proj1, conv1, small["odd_conv_w"], alog_row, dtb_row, dqn, dkn, dvs, dbg, dz, "odd_d_prep")
    gs["odd_dt_bias"] = ddt_row[:, DN_HEADS:2 * DN_HEADS]
    gs["odd_a_log"] = dal_row[:, DN_HEADS:2 * DN_HEADS]
    dwi = mm_at(h1t, dproj1, "odd_dw_in", transposed=True)
    dx2, gs["odd_norm"] = mm_rms_bwd(dproj1, wo["w_in"], x2, small["odd_norm"], dx3, "odd_d_norm")
    tok = grads_out("odd", {"w_in": dwi, "w_out": dwo})

    dx1, dwgu, dwd, gs["ffn_norm0"] = _ffn_bwd(x1, small["ffn_norm0"], w0["gate_up"], w0["down"], ffn0, dx2, "ffn0",
                                               after=tok)
    tok = grads_out("ffn0", {"gate_up": dwgu, "down": dwd})

    dmix = mm_nt(dx1, we["w_out"], "even_d_mix", after=tok)
    dwo = mm_at(mix0_t, dx1, "even_dw_out")
    dqr, dkr, dv, gs["even_sinks"] = swa_bwd(qr, kr, proj0, small["even_sinks"], y_attn, lse, dmix, "even_d_swa")
    dqk, gs["even_q_gain"], gs["even_k_gain"] = qk_prep_bwd(
        proj0, small["even_q_gain"], small["even_k_gain"], cosf, sinf, dqr, dkr, "even_d_qk_prep")
    dgb, dgc, dxi, gs["even_conv_w"] = gconv_bwd(proj0, small["even_conv_w"], dmix, "even_d_gconv")
    dproj0 = jnp.concatenate([dqk, dv, dgb, dgc, dxi], axis=-1)
    dwi = mm_at(h0t, dproj0, "even_dw_in", transposed=True)
    tok = grads_out("even", {"w_in": dwi, "w_out": dwo})
    grad_x, gs["even_norm"] = mm_rms_bwd(dproj0, we["w_in"], x, small["even_norm"], dx1, "even_d_norm", after=tok)
    return loss_row, grad_x, gs


_SMALL_ORDER = ("even_norm", "even_q_gain", "even_k_gain", "even_sinks", "odd_a_log", "odd_dt_bias", "odd_o_gain",
                "ffn_norm0", "ffn_norm1", "odd_norm", "even_conv_w", "odd_conv_w")
_SMALL_SIZE = {"even_norm": 1024, "even_q_gain": 64, "even_k_gain": 64, "even_sinks": 8, "odd_a_log": 8,
               "odd_dt_bias": 8, "odd_o_gain": 128, "ffn_norm0": 1024, "ffn_norm1": 1024, "odd_norm": 1024,
               "even_conv_w": 3 * 512, "odd_conv_w": 4 * 3072}
_N_REPL = 9


def _pack_rows(vals):
    flat = jnp.concatenate([v.reshape(-1) for v in vals])
    pad = (-flat.shape[0]) % 1024
    return jnp.pad(flat, (0, pad)).reshape(-1, 128)


def _my_block(full, size, axis):
    me = 4 * lax.axis_index("x") + 2 * lax.axis_index("y") + lax.axis_index("c")
    return lax.dynamic_slice_in_dim(full, me * size, size, axis=axis)


def kernel(x, even_norm, even_w_in, even_q_gain, even_k_gain, even_sinks, even_conv_w, even_w_out, odd_norm, odd_w_in, odd_conv_w, odd_a_log, odd_dt_bias, odd_o_gain, odd_w_out, ffn_norm, ffn_w_gate_up, ffn_w_down, loss_target, m_even_norm, m_even_w_in, m_even_q_gain, m_even_k_gain, m_even_sinks, m_even_conv_w, m_even_w_out, m_odd_norm, m_odd_w_in, m_odd_conv_w, m_odd_a_log, m_odd_dt_bias, m_odd_o_gain, m_odd_w_out, m_ffn_norm, m_ffn_w_gate_up, m_ffn_w_down, v_even_norm, v_even_w_in, v_even_q_gain, v_even_k_gain, v_even_sinks, v_even_conv_w, v_even_w_out, v_odd_norm, v_odd_w_in, v_odd_conv_w, v_odd_a_log, v_odd_dt_bias, v_odd_o_gain, v_odd_w_out, v_ffn_norm, v_ffn_w_gate_up, v_ffn_w_down):
    t = x.shape[1]
    d = D_MODEL

    tr = lambda a: jnp.swapaxes(a, 1, 2)
    shard = {
        "even": {"w_in": tr(even_w_in)[0], "w_out": even_w_out[0]},
        "ffn0": {"gate_up": tr(ffn_w_gate_up)[0], "down": ffn_w_down[0]},
        "odd": {"w_in": tr(odd_w_in)[0], "w_out": odd_w_out[0]},
        "ffn1": {"gate_up": tr(ffn_w_gate_up)[1], "down": ffn_w_down[1]},
    }
    given = {
        ("even", "w_in"): ("even_w_in", even_w_in, m_even_w_in, v_even_w_in, 0),
        ("even", "w_out"): ("even_w_out", even_w_out, m_even_w_out, v_even_w_out, 0),
        ("odd", "w_in"): ("odd_w_in", odd_w_in, m_odd_w_in, v_odd_w_in, 0),
        ("odd", "w_out"): ("odd_w_out", odd_w_out, m_odd_w_out, v_odd_w_out, 0),
        ("ffn0", "gate_up"): ("ffn_w_gate_up", ffn_w_gate_up, m_ffn_w_gate_up, v_ffn_w_gate_up, 0),
        ("ffn1", "gate_up"): ("ffn_w_gate_up", ffn_w_gate_up, m_ffn_w_gate_up, v_ffn_w_gate_up, 1),
        ("ffn0", "down"): ("ffn_w_down", ffn_w_down, m_ffn_w_down, v_ffn_w_down, 0),
        ("ffn1", "down"): ("ffn_w_down", ffn_w_down, m_ffn_w_down, v_ffn_w_down, 1),
    }

    def whole(group, parts):
        col, row = tuple(shard[group])
        w_col = parts[0].reshape(-1, d)
        if group == "odd":
            w_col = jnp.pad(w_col, ((0, ODD_IN_PAD - ODD_IN_W), (0, 0)))
        return {col: w_col, row: parts[1].reshape(-1, d)}

    wire = {g: [a.astype(MXU_DTYPE) for a in shard[g].values()] for g in shard}
    wire["even_out"] = [wire["even"].pop()]
    wire["even"].append(_pack_rows([odd_norm, even_conv_w, odd_conv_w]))
    gathers, tok = {}, None
    for g in ("even", "even_out", "ffn0", "odd", "ffn1"):
        sems, srcs_thru, lands_thru, tok = send_start(wire[g], f"gather_{g}_start", False, tok)
        gathers[g] = (sems, srcs_thru, lands_thru)
    o1 = d // N_DEV
    o2 = o1 + 3 * CONV_CH // N_DEV

    def weights_of(group, after):
        lands = send_wait(*gathers[group], f"gather_{group}_wait", False, after)
        if group == "even_out":
            return {"w_out": lands[0].reshape(-1, d)}
        if group != "even":
            return whole(group, lands)
        sg = lands[1].reshape(N_DEV, -1)
        return {"w_in": lands[0].reshape(-1, d), "small": {
            "odd_norm": sg[:, :o1].reshape(1, d),
            "even_conv_w": sg[:, o1:o2].reshape(N_DEV, 3, CONV_CH // N_DEV).transpose(1, 0, 2).reshape(3, CONV_CH),
            "odd_conv_w": sg[:, o2:o2 + 4 * _QKV_W // N_DEV].reshape(N_DEV, 4, _QKV_W // N_DEV)
            .transpose(1, 0, 2).reshape(4, _QKV_W)}}

    sent = {}

    def grads_out(group, dws):
        col, row = tuple(shard[group])
        n_cols = N_DEV * shard[group][col].shape[0]
        pieces = [dws[col][:n_cols].reshape((N_DEV,) + shard[group][col].shape),
                  dws[row].reshape((N_DEV,) + shard[group][row].shape)]
        sems, srcs_thru, lands_thru, token = send_start(pieces, f"exchange_{group}_start", True, None)
        sent[group] = (sems, srcs_thru, lands_thru, pieces)
        return token

    small = {
        "even_norm": even_norm, "even_q_gain": even_q_gain, "even_k_gain": even_k_gain, "even_sinks": even_sinks,
        "odd_a_log": odd_a_log.reshape(-1), "odd_dt_bias": odd_dt_bias.reshape(-1), "odd_o_gain": odd_o_gain,
        "ffn_norm0": ffn_norm[0:1], "ffn_norm1": ffn_norm[1:2],
    }

    loss_row, grad_x, gs = local_step(x.reshape(t, d), loss_target.reshape(t, d), small, weights_of, grads_out, after=tok)

    rows = _pack_rows([gs[n] for n in _SMALL_ORDER] + [loss_row[:, 0:1]])
    small_sent = send_start([rows], "gather_small_grads_start", False, None)

    res, behind = {}, small_sent[3]
    for g in ("ffn1", "odd", "ffn0", "even"):
        sems, srcs_thru, lands_thru, pieces = sent[g]
        lands = send_wait(sems, srcs_thru, lands_thru, f"exchange_{g}_wait", True, behind)
        for i, (key, pcs) in enumerate(zip(shard[g], lands)):
            name, w_, m_, v_, layer = given[g, key]
            view = tr if i == 0 else (lambda a: a)
            res[name] = adam_sum(view(w_), pcs, view(m_), view(v_), f"adamw_{g}_{key}", layer=layer, into=res.get(name))
        behind = res[name][0]
    for name in ("even_w_in", "odd_w_in", "ffn_w_gate_up"):
        res[name] = tuple(tr(a) for a in res[name])

    (rows_g,) = send_wait(*small_sent[:3], "gather_small_grads_wait", False, behind)
    tot = sum_rows(rows_g, "sum_small_grads").reshape(-1)
    off, sgrad = 0, {}
    for n in _SMALL_ORDER:
        sgrad[n] = tot[off:off + _SMALL_SIZE[n]]
        off += _SMALL_SIZE[n]
    loss = tot[off]

    repl = _SMALL_ORDER[:_N_REPL]
    repl_w = {"even_norm": even_norm, "even_q_gain": even_q_gain, "even_k_gain": even_k_gain, "even_sinks": even_sinks,
              "odd_a_log": odd_a_log, "odd_dt_bias": odd_dt_bias, "odd_o_gain": odd_o_gain,
              "ffn_norm0": ffn_norm[0], "ffn_norm1": ffn_norm[1]}
    repl_m = {"even_norm": m_even_norm, "even_q_gain": m_even_q_gain, "even_k_gain": m_even_k_gain,
              "even_sinks": m_even_sinks, "odd_a_log": m_odd_a_log, "odd_dt_bias": m_odd_dt_bias,
              "odd_o_gain": m_odd_o_gain, "ffn_norm0": m_ffn_norm[0], "ffn_norm1": m_ffn_norm[1]}
    repl_v = {"even_norm": v_even_norm, "even_q_gain": v_even_q_gain, "even_k_gain": v_even_k_gain,
              "even_sinks": v_even_sinks, "odd_a_log": v_odd_a_log, "odd_dt_bias": v_odd_dt_bias,
              "odd_o_gain": v_odd_o_gain, "ffn_norm0": v_ffn_norm[0], "ffn_norm1": v_ffn_norm[1]}
    pk = lambda dct: _pack_rows([dct[n] for n in repl])
    pd_, pm_, pv_ = adam_small(pk(repl_w), pk(sgrad), pk(repl_m), pk(repl_v), "adamw_replicated")
    sres = {}
    off = 0
    for n in repl:
        sz = _SMALL_SIZE[n]
        sres[n] = (sgrad[n], pd_.reshape(-1)[off:off + sz], pm_.reshape(-1)[off:off + sz], pv_.reshape(-1)[off:off + sz])
        off += sz
    g_on = _my_block(sgrad["odd_norm"].reshape(1, d), d // N_DEV, 1)
    g_ec = _my_block(sgrad["even_conv_w"].reshape(3, CONV_CH), CONV_CH // N_DEV, 1)
    g_oc = _my_block(sgrad["odd_conv_w"].reshape(4, _QKV_W), _QKV_W // N_DEV, 1)
    shard_w = _pack_rows([odd_norm, even_conv_w, odd_conv_w])
    sd_, sm_, sv_ = adam_small(shard_w, _pack_rows([g_on, g_ec, g_oc]),
                               _pack_rows([m_odd_norm, m_even_conv_w, m_odd_conv_w]),
                               _pack_rows([v_odd_norm, v_even_conv_w, v_odd_conv_w]), "adamw_sharded_small")
    off = 0
    for n, gfull, like in (("odd_norm", g_on, odd_norm), ("even_conv_w", g_ec, even_conv_w), ("odd_conv_w", g_oc, odd_conv_w)):
        sz = like.size
        sres[n] = (gfull, sd_.reshape(-1)[off:off + sz], sm_.reshape(-1)[off:off + sz], sv_.reshape(-1)[off:off + sz])
        off += sz

    def small_out(name, like, kind):
        if name == "ffn_norm":
            return jnp.stack([sres["ffn_norm0"][kind], sres["ffn_norm1"][kind]]).reshape(like.shape)
        return sres[name][kind].reshape(like.shape)

    order = (("even_norm", even_norm), ("even_w_in", even_w_in), ("even_q_gain", even_q_gain),
             ("even_k_gain", even_k_gain), ("even_sinks", even_sinks), ("even_conv_w", even_conv_w),
             ("even_w_out", even_w_out), ("odd_norm", odd_norm), ("odd_w_in", odd_w_in), ("odd_conv_w", odd_conv_w),
             ("odd_a_log", odd_a_log), ("odd_dt_bias", odd_dt_bias), ("odd_o_gain", odd_o_gain),
             ("odd_w_out", odd_w_out), ("ffn_norm", ffn_norm), ("ffn_w_gate_up", ffn_w_gate_up),
             ("ffn_w_down", ffn_w_down))
    outs = [loss, grad_x.reshape(x.shape)]
    for kind in range(4):
        for name, like in order:
            outs.append(res[name][kind] if name in res else small_out(name, like, kind))
    return tuple(outs)
```

```python
import jax
import jax.numpy as jnp
import numpy as np
from jax import lax
from jax.experimental import pallas as pl
from jax.experimental.pallas import tpu as pltpu

F32 = jnp.float32
MXU_DTYPE = jnp.bfloat16
HI = lax.Precision.HIGH
EPS = 1e-6
N_DEV = 8
D_MODEL = 1024
HEAD_DIM = 64
ATTN_HEADS = 8
KV_HEADS = 2
ATTN_BLOCK = 128
Q_W = 512
KV_W = 128
CONV_CH = 512
EVEN_IN_W = 2304
DN_HEADS = 8
DN_DIM = 128
DN_W = 1024
DN_CHUNK = 64
ODD_IN_W = 4112
ODD_IN_PAD = 4224
D_FF = 2816
NEG = -1e30
VMEM_LIMIT = 56 * 1024 * 1024
ADAM_LR, ADAM_B1, ADAM_B2, ADAM_EPS, ADAM_WD, ADAM_STEP = 0.001, 0.9, 0.999, 1e-08, 0.01, 10
MESH = pl.DeviceIdType.MESH


def _cp(*sem):
    return pltpu.CompilerParams(dimension_semantics=sem, vmem_limit_bytes=VMEM_LIMIT)


def _pick(n, cap):
    best = 128
    for t in range(128, cap + 1, 128):
        if n % t == 0:
            best = t
    return best


def _mx(a, b):
    return jnp.dot(a.astype(MXU_DTYPE), b.astype(MXU_DTYPE), preferred_element_type=F32)


def _mx_nt(a, b):
    return lax.dot_general(a.astype(MXU_DTYPE), b.astype(MXU_DTYPE), (((1,), (1,)), ((), ())),
                           preferred_element_type=F32)


def _mx_tn(a, b):
    return lax.dot_general(a.astype(MXU_DTYPE), b.astype(MXU_DTYPE), (((0,), (0,)), ((), ())),
                           preferred_element_type=F32)


def _hi(a, b):
    return jnp.dot(a, b, precision=HI, preferred_element_type=F32)


def _hi_nt(a, b):
    return lax.dot_general(a, b, (((1,), (1,)), ((), ())), precision=HI, preferred_element_type=F32)


def _hi_tn(a, b):
    return lax.dot_general(a, b, (((0,), (0,)), ((), ())), precision=HI, preferred_element_type=F32)


def _sigmoid(x):
    return 0.5 * jnp.tanh(0.5 * x) + 0.5


def _softplus(x):
    return jnp.maximum(x, 0.0) + jnp.log(1.0 + jnp.exp(-jnp.abs(x)))


def mm_nn_res_norm(a, b, res, g, name, tm=512):
    t, k = a.shape
    d = b.shape[1]
    tm = min(tm, t)

    def body(a_ref, b_ref, res_ref, g_ref, y_ref, h_ref, ht_ref):
        y = res_ref[...] + _mx(a_ref[...], b_ref[...])
        y_ref[...] = y
        h = y * lax.rsqrt(jnp.mean(y * y, axis=-1, keepdims=True) + EPS) * g_ref[...]
        h_ref[...] = h.astype(h_ref.dtype)
        ht_ref[...] = h.T.astype(ht_ref.dtype)

    row = pl.BlockSpec((tm, d), lambda i: (i, 0))
    return pl.pallas_call(
        body, name=name, grid=(t // tm,),
        in_specs=[pl.BlockSpec((tm, k), lambda i: (i, 0)), pl.BlockSpec((k, d), lambda i: (0, 0)), row,
                  pl.BlockSpec((1, d), lambda i: (0, 0))],
        out_specs=(row, row, pl.BlockSpec((d, tm), lambda i: (0, i))),
        out_shape=(jax.ShapeDtypeStruct((t, d), F32), jax.ShapeDtypeStruct((t, d), MXU_DTYPE),
                   jax.ShapeDtypeStruct((d, t), MXU_DTYPE)),
        compiler_params=_cp("parallel"))(a, b, res, g)


def mm_nt(a, b, name, out_dtype=F32, tm=2048, after=None):
    m, k = a.shape
    n, _ = b.shape
    tn = _pick(n, 512 if k > 3000 else 1536)
    tm = min(tm, m)

    def body(a_ref, b_ref, *rest):
        o_ref = rest[-1]
        o_ref[...] = _mx_nt(a_ref[...], b_ref[...]).astype(o_ref.dtype)

    in_specs = [pl.BlockSpec((tm, k), lambda j, i: (i, 0)), pl.BlockSpec((tn, k), lambda j, i: (j, 0))]
    args = [a, b]
    if after is not None:
        in_specs.append(pl.BlockSpec(memory_space=pl.ANY))
        args.append(after)
    return pl.pallas_call(
        body, name=name, grid=(n // tn, m // tm), in_specs=in_specs,
        out_specs=pl.BlockSpec((tm, tn), lambda j, i: (i, j)),
        out_shape=jax.ShapeDtypeStruct((m, n), out_dtype), compiler_params=_cp("parallel", "parallel"))(*args)


def mm_at(at, b, name, tk=2048, transposed=False, tn=None, row_block=None):
    m, kk = at.shape
    _, n = b.shape
    tm, tn, tk = _pick(m, 1408), tn or _pick(n, 2816), min(tk, kk)
    nk = kk // tk

    def body(a_ref, b_ref, o_ref, acc_ref):
        k = pl.program_id(2)
        p = _mx(a_ref[...], b_ref[...])
        acc = jnp.where(k == 0, p, acc_ref[...] + p)
        acc_ref[...] = acc

        @pl.when(k == nk - 1)
        def _():
            o_ref[...] = (acc.T if transposed else acc).astype(o_ref.dtype)

    if transposed:
        rb = row_block or (lambda j: j)
        out_spec = pl.BlockSpec((tn, tm), lambda i, j, k: (rb(j), i))
        out_shape = jax.ShapeDtypeStruct((n, m), MXU_DTYPE)
    else:
        out_spec = pl.BlockSpec((tm, tn), lambda i, j, k: (i, j))
        out_shape = jax.ShapeDtypeStruct((m, n), MXU_DTYPE)
    return pl.pallas_call(
        body, name=name, grid=(m // tm, n // tn, nk),
        in_specs=[pl.BlockSpec((tm, tk), lambda i, j, k: (i, k)), pl.BlockSpec((tk, tn), lambda i, j, k: (k, j))],
        out_specs=out_spec, out_shape=out_shape, scratch_shapes=[pltpu.VMEM((tm, tn), F32)],
        compiler_params=_cp("parallel", "parallel", "arbitrary"))(at, b)


def rms_fwd(x, g, name, tm=512, after=None):
    t, d = x.shape

    def body(x_ref, g_ref, *rest):
        o_ref, ot_ref = rest[-2:]
        xv = x_ref[...]
        r = lax.rsqrt(jnp.mean(xv * xv, axis=-1, keepdims=True) + EPS)
        h = xv * r * g_ref[...]
        o_ref[...] = h.astype(o_ref.dtype)
        ot_ref[...] = h.T.astype(ot_ref.dtype)

    in_specs = [pl.BlockSpec((tm, d), lambda i: (i, 0)), pl.BlockSpec((1, d), lambda i: (0, 0))]
    args = [x, g]
    if after is not None:
        in_specs.append(pl.BlockSpec(memory_space=pl.ANY))
        args.append(after)
    return pl.pallas_call(
        body, name=name, grid=(t // tm,), in_specs=in_specs,
        out_specs=(pl.BlockSpec((tm, d), lambda i: (i, 0)), pl.BlockSpec((d, tm), lambda i: (0, i))),
        out_shape=(jax.ShapeDtypeStruct((t, d), MXU_DTYPE), jax.ShapeDtypeStruct((d, t), MXU_DTYPE)),
        compiler_params=_cp("parallel"))(*args)


def mm_rms_bwd(a, bt, x, g, dres, name, tm=512, after=None, chunks=None):
    t, k = a.shape
    d = bt.shape[1]
    tm = min(tm if k > 3000 else 2 * tm, t)
    chunks = chunks or ((0, 0, k),)

    def body(a_ref, b_ref, x_ref, g_ref, dres_ref, *rest):
        dx_ref, dg_ref = rest[-2:]
        dhv = None
        for ca, cb, size in chunks:
            part = _mx(a_ref[:, ca:ca + size], b_ref[cb:cb + size, :])
            dhv = part if dhv is None else dhv + part
        xv = x_ref[...]
        r = lax.rsqrt(jnp.mean(xv * xv, axis=-1, keepdims=True) + EPS)
        xh = xv * r
        dxh = dhv * g_ref[...]
        dx_ref[...] = dres_ref[...] + r * (dxh - xh * jnp.mean(dxh * xh, axis=-1, keepdims=True))
        part = jnp.sum(dhv * xh, axis=0, keepdims=True)
        dg_ref[...] = jnp.where(pl.program_id(0) == 0, part, dg_ref[...] + part)

    row = pl.BlockSpec((tm, d), lambda i: (i, 0))
    one = pl.BlockSpec((1, d), lambda i: (0, 0))
    in_specs = [pl.BlockSpec((tm, k), lambda i: (i, 0)), pl.BlockSpec((k, d), lambda i: (0, 0)), row, one, row]
    args = [a, bt, x, g, dres]
    if after is not None:
        in_specs.append(pl.BlockSpec(memory_space=pl.ANY))
        args.append(after)
    return pl.pallas_call(
        body, name=name, grid=(t // tm,), in_specs=in_specs, out_specs=(row, one),
        out_shape=(jax.ShapeDtypeStruct((t, d), F32), jax.ShapeDtypeStruct((1, d), F32)),
        compiler_params=_cp("arbitrary"))(*args)


GU_TILE = 1408


def ffn_up(f, wt, name, tm=1024):
    t, d = f.shape
    tm = min(tm, t)
    nj = D_FF // GU_TILE

    def body(f_ref, wg_ref, wu_ref, gu_ref, a_ref, at_ref):
        g = _mx_nt(f_ref[...], wg_ref[...])
        u = _mx_nt(f_ref[...], wu_ref[...])
        sg = _sigmoid(g)
        gs = g * sg
        gu_ref[:, :GU_TILE] = (u * (sg + gs - gs * sg)).astype(gu_ref.dtype)
        gu_ref[:, GU_TILE:] = gs.astype(gu_ref.dtype)
        act = gs * u
        a_ref[...] = act.astype(a_ref.dtype)
        at_ref[...] = act.T.astype(at_ref.dtype)

    return pl.pallas_call(
        body, name=name, grid=(nj, t // tm),
        in_specs=[pl.BlockSpec((tm, d), lambda j, i: (i, 0)), pl.BlockSpec((GU_TILE, d), lambda j, i: (j, 0)),
                  pl.BlockSpec((GU_TILE, d), lambda j, i: (nj + j, 0))],
        out_specs=(pl.BlockSpec((tm, 2 * GU_TILE), lambda j, i: (i, j)), pl.BlockSpec((tm, GU_TILE), lambda j, i: (i, j)),
                   pl.BlockSpec((GU_TILE, tm), lambda j, i: (j, i))),
        out_shape=(jax.ShapeDtypeStruct((t, 2 * D_FF), MXU_DTYPE), jax.ShapeDtypeStruct((t, D_FF), MXU_DTYPE),
                   jax.ShapeDtypeStruct((D_FF, t), MXU_DTYPE)),
        compiler_params=_cp("parallel", "parallel"))(f, wt, wt)


_GU_CHUNKS = tuple((q * GU_TILE, ((q % 2) * (D_FF // GU_TILE) + q // 2) * GU_TILE, GU_TILE)
                   for q in range(2 * D_FF // GU_TILE))


def ffn_dact(dy, w_d, gu, name, tm=1024, after=None):
    t, d = dy.shape
    tm = min(tm, t)

    def body(dy_ref, w_ref, gu_ref, *rest):
        o_ref = rest[-1]
        da = _mx_nt(dy_ref[...], w_ref[...])
        o_ref[:, :GU_TILE] = (da * gu_ref[:, :GU_TILE]).astype(o_ref.dtype)
        o_ref[:, GU_TILE:] = (da * gu_ref[:, GU_TILE:]).astype(o_ref.dtype)

    in_specs = [pl.BlockSpec((tm, d), lambda j, i: (i, 0)), pl.BlockSpec((GU_TILE, d), lambda j, i: (j, 0)),
                pl.BlockSpec((tm, 2 * GU_TILE), lambda j, i: (i, j))]
    args = [dy, w_d, gu]
    if after is not None:
        in_specs.append(pl.BlockSpec(memory_space=pl.ANY))
        args.append(after)
    return pl.pallas_call(
        body, name=name, grid=(D_FF // GU_TILE, t // tm), in_specs=in_specs,
        out_specs=pl.BlockSpec((tm, 2 * GU_TILE), lambda j, i: (i, j)),
        out_shape=jax.ShapeDtypeStruct((t, 2 * D_FF), MXU_DTYPE), compiler_params=_cp("parallel", "parallel"))(*args)


def mm_nn_res_loss(a, b, res, target, name, tm=512):
    t, k = a.shape
    d = b.shape[1]
    tm = min(tm, t)

    def body(a_ref, b_ref, res_ref, t_ref, dy_ref, l_ref):
        e = res_ref[...] + _mx(a_ref[...], b_ref[...]) - t_ref[...]
        dy_ref[...] = e * (1.0 / d)
        part = jnp.zeros((1, 128), F32) + 0.5 * jnp.sum(jnp.mean(e * e, axis=-1, keepdims=True), axis=0, keepdims=True)
        l_ref[...] = jnp.where(pl.program_id(0) == 0, part, l_ref[...] + part)

    row = pl.BlockSpec((tm, d), lambda i: (i, 0))
    return pl.pallas_call(
        body, name=name, grid=(t // tm,),
        in_specs=[pl.BlockSpec((tm, k), lambda i: (i, 0)), pl.BlockSpec((k, d), lambda i: (0, 0)), row, row],
        out_specs=(row, pl.BlockSpec((1, 128), lambda i: (0, 0))),
        out_shape=(jax.ShapeDtypeStruct((t, d), F32), jax.ShapeDtypeStruct((1, 128), F32)),
        compiler_params=_cp("arbitrary"))(a, b, res, target)


QK_W = Q_W + KV_W
_QK_TILE = 256


def _qk_mats():
    idx = np.arange(_QK_TILE)
    half = HEAD_DIM // 2
    same = (idx[:, None] // HEAD_DIM) == (idx[None, :] // HEAD_DIM)
    lo = (idx % HEAD_DIM) < half
    rot = np.where((idx[:, None] == idx[None, :] + half) & lo[None, :], -1.0, 0.0)
    rot = rot + np.where((idx[:, None] == idx[None, :] - half) & ~lo[None, :], 1.0, 0.0)
    return jnp.asarray(same, F32), jnp.asarray(rot, F32)


QK_TM = 256


def _qk_gains(q_gain, k_gain):
    return jnp.concatenate([q_gain] * ATTN_HEADS + [k_gain] * KV_HEADS, axis=-1)


def _rope_tile(ca_ref, sa_ref, cb_ref, sb_ref):
    ca, sa, cb, sb = ca_ref[0], sa_ref[0], cb_ref[...], sb_ref[...]
    c, s = ca * cb - sa * sb, sa * cb + ca * sb
    rep = QK_W // 128
    return jnp.concatenate([c] * rep, axis=-1), jnp.concatenate([s] * rep, axis=-1)


_ROPE_SPECS = [pl.BlockSpec((1, 1, 128), lambda i: (i, 0, 0)), pl.BlockSpec((1, 1, 128), lambda i: (i, 0, 0)),
               pl.BlockSpec((QK_TM, 128), lambda i: (0, 0)), pl.BlockSpec((QK_TM, 128), lambda i: (0, 0))]


def _qk_tiles(a, mat, transposed=False):
    outs = []
    for c0 in range(0, QK_W, _QK_TILE):
        w = min(_QK_TILE, QK_W - c0)
        mt = (mat.T if transposed else mat)[:w, :w].astype(MXU_DTYPE)
        at = a[:, c0:c0 + w]
        hi = at.astype(MXU_DTYPE)
        lo = (at - hi.astype(F32)).astype(MXU_DTYPE)
        outs.append(jnp.dot(hi, mt, preferred_element_type=F32) + jnp.dot(lo, mt, preferred_element_type=F32))
    return jnp.concatenate(outs, axis=-1)


def qk_prep_fwd(proj, q_gain, k_gain, rope, name):
    t = proj.shape[0]
    tm = QK_TM
    gmat, rmat = _qk_mats()
    gain = _qk_gains(q_gain, k_gain)

    def body(p_ref, g_ref, ca_ref, sa_ref, cb_ref, sb_ref, gm_ref, rm_ref, q_ref, k_ref):
        x = p_ref[...]
        r = lax.rsqrt(_qk_tiles(x * x, gm_ref[...]) * (1.0 / HEAD_DIM) + EPS)
        xn = x * r * g_ref[...]
        c, s = _rope_tile(ca_ref, sa_ref, cb_ref, sb_ref)
        out = xn * c + _qk_tiles(xn, rm_ref[...]) * s
        q_ref[...] = out[:, :Q_W]
        k_ref[...] = out[:, Q_W:]

    full = pl.BlockSpec((_QK_TILE, _QK_TILE), lambda i: (0, 0))
    return pl.pallas_call(
        body, name=name, grid=(t // tm,),
        in_specs=[pl.BlockSpec((tm, QK_W), lambda i: (i, 0)), pl.BlockSpec((1, QK_W), lambda i: (0, 0))] + _ROPE_SPECS
        + [full, full],
        out_specs=(pl.BlockSpec((tm, Q_W), lambda i: (i, 0)), pl.BlockSpec((tm, KV_W), lambda i: (i, 0))),
        out_shape=(jax.ShapeDtypeStruct((t, Q_W), F32), jax.ShapeDtypeStruct((t, KV_W), F32)),
        compiler_params=_cp("parallel"))(proj, gain, *rope, gmat, rmat)


def qk_prep_bwd(proj, q_gain, k_gain, rope, dq, dk, name):
    t = proj.shape[0]
    tm = QK_TM
    gmat, rmat = _qk_mats()
    gain = _qk_gains(q_gain, k_gain)
    lanes = np.arange(QK_W)[:, None]
    fold = jnp.asarray(lanes % HEAD_DIM + np.where(lanes >= Q_W, HEAD_DIM, 0) == np.arange(128)[None, :], F32)

    def body(p_ref, g_ref, ca_ref, sa_ref, cb_ref, sb_ref, gm_ref, rm_ref, f_ref, dq_ref, dk_ref, o_ref, dg_ref):
        x = p_ref[...]
        r = lax.rsqrt(_qk_tiles(x * x, gm_ref[...]) * (1.0 / HEAD_DIM) + EPS)
        xh = x * r
        c, s = _rope_tile(ca_ref, sa_ref, cb_ref, sb_ref)
        dout = jnp.concatenate([dq_ref[...], dk_ref[...]], axis=-1)
        dxn = dout * c + _qk_tiles(dout * s, rm_ref[...], transposed=True)
        part = _hi(jnp.sum(dxn * xh, axis=0, keepdims=True), f_ref[...])
        dxh = dxn * g_ref[...]
        mean = _qk_tiles(dxh * xh, gm_ref[...]) * (1.0 / HEAD_DIM)
        o_ref[...] = (r * (dxh - xh * mean)).astype(o_ref.dtype)
        dg_ref[...] = jnp.where(pl.program_id(0) == 0, part, dg_ref[...] + part)

    full = pl.BlockSpec((_QK_TILE, _QK_TILE), lambda i: (0, 0))
    dqk, dg = pl.pallas_call(
        body, name=name, grid=(t // tm,),
        in_specs=[pl.BlockSpec((tm, QK_W), lambda i: (i, 0)), pl.BlockSpec((1, QK_W), lambda i: (0, 0))] + _ROPE_SPECS
        + [full, full, pl.BlockSpec((QK_W, 128), lambda i: (0, 0)),
                  pl.BlockSpec((tm, Q_W), lambda i: (i, 0)), pl.BlockSpec((tm, KV_W), lambda i: (i, 0))],
        out_specs=(pl.BlockSpec((tm, QK_W), lambda i: (i, 0)), pl.BlockSpec((1, 128), lambda i: (0, 0))),
        out_shape=(jax.ShapeDtypeStruct((t, QK_W), MXU_DTYPE), jax.ShapeDtypeStruct((1, 128), F32)),
        compiler_params=_cp("arbitrary"))(proj, gain, *rope, gmat, rmat, fold, dq, dk)
    return dqk, dg[:, :HEAD_DIM], dg[:, HEAD_DIM:]


def _swa_valid(n, grp):
    qi = lax.broadcasted_iota(jnp.int32, (grp * ATTN_BLOCK, 2 * ATTN_BLOCK), 0) & (ATTN_BLOCK - 1)
    kj = lax.broadcasted_iota(jnp.int32, (grp * ATTN_BLOCK, 2 * ATTN_BLOCK), 1)
    diff = qi + ATTN_BLOCK - kj
    return (diff >= 0) & (diff < ATTN_BLOCK) & (n * ATTN_BLOCK - ATTN_BLOCK + kj >= 0)


def _stack_heads(ref, g, grp, rows=slice(None)):
    return jnp.concatenate([ref[rows, (g * grp + j) * HEAD_DIM:(g * grp + j + 1) * HEAD_DIM] for j in range(grp)], axis=0)


def _stack_sinks(s_ref, g, grp):
    return jnp.concatenate([jnp.zeros((ATTN_BLOCK, 1), F32) + s_ref[0:1, g * grp + j:g * grp + j + 1]
                            for j in range(grp)], axis=0)


SWA_STEP = 2


def swa_fwd(q, k, proj, sinks, name):
    t = q.shape[0]
    nb = t // ATTN_BLOCK
    scale = HEAD_DIM ** -0.5
    grp = ATTN_HEADS // KV_HEADS

    rows = SWA_STEP * ATTN_BLOCK

    def body(q_ref, kc_ref, kp_ref, vc_ref, vp_ref, s_ref, y_ref, mix_ref, yt_ref, lse_ref):
        n0 = pl.program_id(0) * SWA_STEP
        kk = jnp.concatenate([kp_ref[...], kc_ref[...]], axis=0).astype(MXU_DTYPE)
        vv = jnp.concatenate([vp_ref[...], vc_ref[...]], axis=0).astype(MXU_DTYPE)
        lane = lax.broadcasted_iota(jnp.int32, (ATTN_BLOCK, ATTN_HEADS), 1)
        units = [(b, g) for b in range(SWA_STEP) for g in range(KV_HEADS)]
        blk = lambda b: slice(b * ATTN_BLOCK, (b + 1) * ATTN_BLOCK)
        keys = lambda b: slice(b * ATTN_BLOCK, (b + 2) * ATTN_BLOCK)
        col = lambda g: slice(g * HEAD_DIM, (g + 1) * HEAD_DIM)
        valid = [_swa_valid(n0 + b, grp) for b in range(SWA_STEP)]
        qg = [_stack_heads(q_ref, g, grp, blk(b)) for b, g in units]
        sink = [_stack_sinks(s_ref, g, grp) for b, g in units]
        sc = [jnp.where(valid[b], _mx_nt(qg[u], kk[keys(b), col(g)]) * scale, NEG) for u, (b, g) in enumerate(units)]
        m = [jnp.maximum(jnp.max(sc_, axis=-1, keepdims=True), sk) for sc_, sk in zip(sc, sink)]
        e = [jnp.exp(sc_ - m_) for sc_, m_ in zip(sc, m)]
        den = [jnp.sum(e_, axis=-1, keepdims=True) + jnp.exp(sk - m_) for e_, sk, m_ in zip(e, sink, m)]
        og = [_mx(e[u] / den[u], vv[keys(b), col(g)]) for u, (b, g) in enumerate(units)]
        lg = [m_ + jnp.log(d_) for m_, d_ in zip(m, den)]
        for b in range(SWA_STEP):
            lse = jnp.zeros((ATTN_BLOCK, ATTN_HEADS), F32)
            outs = []
            for h in range(ATTN_HEADS):
                u = b * KV_HEADS + h // grp
                sub = blk(h % grp)
                outs.append(og[u][sub])
                lse = jnp.where(lane == h, lg[u][sub], lse)
            y = jnp.concatenate(outs, axis=-1)
            y_ref[blk(b), :] = y
            mix_ref[blk(b), :] = y.astype(mix_ref.dtype)
            yt_ref[:, blk(b)] = y.T.astype(yt_ref.dtype)
            lse_ref[blk(b), :] = lse

    cur = lambda n: (n, 0)
    prev = lambda n: (jnp.maximum(n * SWA_STEP - 1, 0), 0)
    vcol = (Q_W + KV_W) // KV_W
    return pl.pallas_call(
        body, name=name, grid=(nb // SWA_STEP,),
        in_specs=[pl.BlockSpec((rows, Q_W), cur), pl.BlockSpec((rows, KV_W), cur),
                  pl.BlockSpec((ATTN_BLOCK, KV_W), prev),
                  pl.BlockSpec((rows, KV_W), lambda n: (n, vcol)),
                  pl.BlockSpec((ATTN_BLOCK, KV_W), lambda n: (jnp.maximum(n * SWA_STEP - 1, 0), vcol)),
                  pl.BlockSpec((1, ATTN_HEADS), lambda n: (0, 0))],
        out_specs=(pl.BlockSpec((rows, Q_W), cur), pl.BlockSpec((rows, Q_W), cur),
                   pl.BlockSpec((Q_W, rows), lambda n: (0, n)), pl.BlockSpec((rows, ATTN_HEADS), cur)),
        out_shape=(jax.ShapeDtypeStruct((t, Q_W), F32), jax.ShapeDtypeStruct((t, Q_W + CONV_CH), MXU_DTYPE),
                   jax.ShapeDtypeStruct((Q_W + CONV_CH, t), MXU_DTYPE), jax.ShapeDtypeStruct((t, ATTN_HEADS), F32)),
        compiler_params=_cp("parallel"))(q, k, k, proj, proj, sinks)


def swa_bwd(q, k, proj, sinks, y, lse, dmix, name):
    t = q.shape[0]
    nb = t // ATTN_BLOCK
    scale = HEAD_DIM ** -0.5
    grp = ATTN_HEADS // KV_HEADS

    def body(q_ref, kc_ref, kp_ref, vc_ref, vp_ref, s_ref, y_ref, lse_ref, dy_ref,
             dq_ref, dk_ref, dv_ref, ds_ref, dkc, dvc):
        n = pl.program_id(0)

        @pl.when(n == 0)
        def _():
            dkc[...] = jnp.zeros_like(dkc)
            dvc[...] = jnp.zeros_like(dvc)
            ds_ref[...] = jnp.zeros_like(ds_ref)

        @pl.when(n < nb)
        def _():
            valid = _swa_valid(n, grp)
            kk = jnp.concatenate([kp_ref[...], kc_ref[...]], axis=0).astype(MXU_DTYPE)
            vv = jnp.concatenate([vp_ref[...], vc_ref[...]], axis=0).astype(MXU_DTYPE)
            lane = lax.broadcasted_iota(jnp.int32, (1, ATTN_HEADS), 1)
            gs = range(KV_HEADS)
            kg = [kk[:, g * HEAD_DIM:(g + 1) * HEAD_DIM] for g in gs]
            vg = [vv[:, g * HEAD_DIM:(g + 1) * HEAD_DIM] for g in gs]
            qg = [_stack_heads(q_ref, g, grp).astype(MXU_DTYPE) for g in gs]
            dog = [_stack_heads(dy_ref, g, grp) for g in gs]
            og = [_stack_heads(y_ref, g, grp) for g in gs]
            lg = [jnp.concatenate([lse_ref[:, g * grp + j:g * grp + j + 1] for j in range(grp)], axis=0) for g in gs]
            sink = [_stack_sinks(s_ref, g, grp) for g in gs]
            sc = [jnp.where(valid, _mx_nt(qg[g], kg[g]) * scale, NEG) for g in gs]
            p = [jnp.exp(sc[g] - lg[g]) for g in gs]
            delta = [jnp.sum(dog[g] * og[g], axis=-1, keepdims=True) for g in gs]
            ds = [p[g] * (_mx_nt(dog[g], vg[g]) - delta[g]) for g in gs]
            dqg = [_mx(ds[g], kg[g]) * scale for g in gs]
            dkf = jnp.concatenate([_mx_tn(ds[g], qg[g]) * scale for g in gs], axis=-1)
            dvf = jnp.concatenate([_mx_tn(p[g], dog[g]) for g in gs], axis=-1)
            dsk = [jnp.exp(sink[g] - lg[g]) * delta[g] for g in gs]
            dsink = jnp.zeros((1, ATTN_HEADS), F32)
            dqs = []
            for h in range(ATTN_HEADS):
                rows = slice((h % grp) * ATTN_BLOCK, (h % grp + 1) * ATTN_BLOCK)
                dqs.append(dqg[h // grp][rows])
                dsink = jnp.where(lane == h, -jnp.sum(dsk[h // grp][rows], axis=0, keepdims=True), dsink)
            dq_ref[...] = jnp.concatenate(dqs, axis=-1)
            dk_ref[...] = dkc[...] + dkf[:ATTN_BLOCK]
            dv_ref[...] = (dvc[...] + dvf[:ATTN_BLOCK]).astype(dv_ref.dtype)
            dkc[...] = dkf[ATTN_BLOCK:]
            dvc[...] = dvf[ATTN_BLOCK:]
            ds_ref[...] += dsink

        @pl.when(n == nb)
        def _():
            dk_ref[...] = dkc[...]
            dv_ref[...] = dvc[...].astype(dv_ref.dtype)

    cur = lambda n: (jnp.minimum(n, nb - 1), 0)
    prev = lambda n: (jnp.clip(n - 1, 0, nb - 1), 0)
    vcol = (Q_W + KV_W) // KV_W
    return pl.pallas_call(
        body, name=name, grid=(nb + 1,),
        in_specs=[pl.BlockSpec((ATTN_BLOCK, Q_W), cur), pl.BlockSpec((ATTN_BLOCK, KV_W), cur),
                  pl.BlockSpec((ATTN_BLOCK, KV_W), prev),
                  pl.BlockSpec((ATTN_BLOCK, KV_W), lambda n: (jnp.minimum(n, nb - 1), vcol)),
                  pl.BlockSpec((ATTN_BLOCK, KV_W), lambda n: (jnp.clip(n - 1, 0, nb - 1), vcol)),
                  pl.BlockSpec((1, ATTN_HEADS), lambda n: (0, 0)),
                  pl.BlockSpec((ATTN_BLOCK, Q_W), cur), pl.BlockSpec((ATTN_BLOCK, ATTN_HEADS), cur),
                  pl.BlockSpec((ATTN_BLOCK, Q_W), cur)],
        out_specs=(pl.BlockSpec((ATTN_BLOCK, Q_W), cur), pl.BlockSpec((ATTN_BLOCK, KV_W), prev),
                   pl.BlockSpec((ATTN_BLOCK, KV_W), prev), pl.BlockSpec((1, ATTN_HEADS), lambda n: (0, 0))),
        out_shape=(jax.ShapeDtypeStruct((t, Q_W), F32), jax.ShapeDtypeStruct((t, KV_W), F32),
                   jax.ShapeDtypeStruct((t, KV_W), MXU_DTYPE), jax.ShapeDtypeStruct((1, ATTN_HEADS), F32)),
        scratch_shapes=[pltpu.VMEM((ATTN_BLOCK, KV_W), F32), pltpu.VMEM((ATTN_BLOCK, KV_W), F32)],
        compiler_params=_cp("arbitrary"))(q, k, k, proj, proj, sinks, y, lse, dmix)


GC_W = 256
_GB0, _GC0, _XI0 = 768 // GC_W, 1280 // GC_W, 1792 // GC_W
HALO = 8


def gconv_fwd(proj, conv_w, mix, mix_t, name, tm=512):
    t = proj.shape[0]
    hb = tm // HALO
    half = Q_W // GC_W

    def body(gb_ref, gc_ref, xi_ref, gch_ref, xih_ref, w_ref, mix_in, mixt_in, y_ref, yt_ref):
        i = pl.program_id(1)
        u = gc_ref[...] * xi_ref[...]
        uh = jnp.where(i == 0, 0.0, gch_ref[...] * xih_ref[...])
        up = jnp.concatenate([uh, u], axis=0)
        cv = w_ref[0:1, :] * up[HALO - 2:HALO - 2 + tm]
        cv = cv + w_ref[1:2, :] * up[HALO - 1:HALO - 1 + tm]
        cv = cv + w_ref[2:3, :] * u
        y = gb_ref[...] * cv
        y_ref[...] = y.astype(y_ref.dtype)
        yt_ref[...] = y.T.astype(yt_ref.dtype)

    def col(c0):
        return pl.BlockSpec((tm, GC_W), lambda cj, i: (i, c0 + cj))

    def halo(c0):
        return pl.BlockSpec((HALO, GC_W), lambda cj, i: (jnp.maximum(i * hb - 1, 0), c0 + cj))

    return pl.pallas_call(
        body, name=name, grid=(CONV_CH // GC_W, t // tm),
        in_specs=[col(_GB0), col(_GC0), col(_XI0), halo(_GC0), halo(_XI0),
                  pl.BlockSpec((3, GC_W), lambda cj, i: (0, cj)),
                  pl.BlockSpec(memory_space=pl.ANY), pl.BlockSpec(memory_space=pl.ANY)],
        out_specs=(pl.BlockSpec((tm, GC_W), lambda cj, i: (i, half + cj)),
                   pl.BlockSpec((GC_W, tm), lambda cj, i: (half + cj, i))),
        out_shape=(jax.ShapeDtypeStruct(mix.shape, mix.dtype), jax.ShapeDtypeStruct(mix_t.shape, mix_t.dtype)),
        input_output_aliases={6: 0, 7: 1},
        compiler_params=_cp("parallel", "parallel"))(proj, proj, proj, proj, proj, conv_w, mix, mix_t)


def gconv_bwd(proj, conv_w, dmix, name, tm=512):
    t = proj.shape[0]
    hb = tm // HALO
    nt = t // tm
    dy0 = Q_W // GC_W

    def body(gb_ref, gc_ref, xi_ref, gch_ref, xih_ref, gbn_ref, dyn_ref, dy_ref, w_ref,
             dgb_ref, dgc_ref, dxi_ref, dw_ref):
        i = pl.program_id(1)
        gc, xi, gb, dy = gc_ref[...], xi_ref[...], gb_ref[...], dy_ref[...]
        u = gc * xi
        uh = jnp.where(i == 0, 0.0, gch_ref[...] * xih_ref[...])
        up = jnp.concatenate([uh, u], axis=0)
        u2 = up[HALO - 2:HALO - 2 + tm]
        u1 = up[HALO - 1:HALO - 1 + tm]
        cv = w_ref[0:1, :] * u2 + w_ref[1:2, :] * u1 + w_ref[2:3, :] * u
        dgb_ref[...] = (dy * cv).astype(dgb_ref.dtype)
        dcv = dy * gb
        dcvn = jnp.where(i == nt - 1, 0.0, dyn_ref[...] * gbn_ref[...])
        dcvp = jnp.concatenate([dcv, dcvn], axis=0)
        du = w_ref[0:1, :] * dcvp[2:2 + tm] + w_ref[1:2, :] * dcvp[1:1 + tm] + w_ref[2:3, :] * dcv
        dgc_ref[...] = (du * xi).astype(dgc_ref.dtype)
        dxi_ref[...] = (du * gc).astype(dxi_ref.dtype)
        dw = jnp.concatenate([jnp.sum(dcv * u2, axis=0, keepdims=True), jnp.sum(dcv * u1, axis=0, keepdims=True),
                              jnp.sum(dcv * u, axis=0, keepdims=True)], axis=0)

        @pl.when(i == 0)
        def _():
            dw_ref[...] = dw

        @pl.when(i > 0)
        def _():
            dw_ref[...] += dw

    def col(c0):
        return pl.BlockSpec((tm, GC_W), lambda cj, i: (i, c0 + cj))

    def halo(c0):
        return pl.BlockSpec((HALO, GC_W), lambda cj, i: (jnp.maximum(i * hb - 1, 0), c0 + cj))

    def nxt(c0):
        return pl.BlockSpec((HALO, GC_W), lambda cj, i: (jnp.minimum((i + 1) * hb, t // HALO - 1), c0 + cj))

    out = pl.BlockSpec((tm, GC_W), lambda cj, i: (i, cj))
    return pl.pallas_call(
        body, name=name, grid=(CONV_CH // GC_W, nt),
        in_specs=[col(_GB0), col(_GC0), col(_XI0), halo(_GC0), halo(_XI0), nxt(_GB0), nxt(dy0), col(dy0),
                  pl.BlockSpec((3, GC_W), lambda cj, i: (0, cj))],
        out_specs=(out, out, out, pl.BlockSpec((3, GC_W), lambda cj, i: (0, cj))),
        out_shape=(jax.ShapeDtypeStruct((t, CONV_CH), MXU_DTYPE),) * 3 + (jax.ShapeDtypeStruct((3, CONV_CH), F32),),
        compiler_params=_cp("parallel", "arbitrary"))(proj, proj, proj, proj, proj, proj, dmix, dmix, conv_w)


_QKV_W = 3 * DN_W
_BA_COL = (4 * DN_W) // 128
_Z_COL = _QKV_W // DN_W


def gdn_prep_fwd(proj, conv_w, alog_row, dtb_row, name, tm=256):
    t = proj.shape[0]
    hb = tm // HALO
    qscale = DN_DIM ** -0.5

    def body(x_ref, xh_ref, w_ref, ba_ref, al_ref, dt_ref, q_ref, k_ref, v_ref, bg_ref, c_ref):
        i = pl.program_id(0)
        for gi in range(3 * DN_HEADS):
            sl = slice(gi * DN_DIM, (gi + 1) * DN_DIM)
            xp = jnp.concatenate([jnp.where(i == 0, 0.0, xh_ref[:, sl]), x_ref[:, sl]], axis=0)
            c = w_ref[0:1, sl] * xp[HALO - 3:HALO - 3 + tm]
            for j in range(1, 4):
                c = c + w_ref[j:j + 1, sl] * xp[HALO - 3 + j:HALO - 3 + j + tm]
            c_ref[:, sl] = c
            s = c * _sigmoid(c)
            osl = slice((gi % DN_HEADS) * DN_DIM, (gi % DN_HEADS + 1) * DN_DIM)
            if gi < DN_HEADS:
                q_ref[:, osl] = s * lax.rsqrt(jnp.sum(s * s, axis=-1, keepdims=True) + EPS) * qscale
            elif gi < 2 * DN_HEADS:
                k_ref[:, osl] = s * lax.rsqrt(jnp.sum(s * s, axis=-1, keepdims=True) + EPS)
            else:
                v_ref[:, osl] = s
        ba = ba_ref[...]
        lane = lax.broadcasted_iota(jnp.int32, ba.shape, 1)
        gval = -jnp.exp(al_ref[...]) * _softplus(ba + dt_ref[...])
        bg_ref[...] = jnp.where(lane < DN_HEADS, _sigmoid(ba), jnp.where(lane < 2 * DN_HEADS, gval, 0.0))

    row = pl.BlockSpec((tm, DN_W), lambda i: (i, 0))
    one = pl.BlockSpec((1, 128), lambda i: (0, 0))
    return pl.pallas_call(
        body, name=name, grid=(t // tm,),
        in_specs=[pl.BlockSpec((tm, _QKV_W), lambda i: (i, 0)),
                  pl.BlockSpec((HALO, _QKV_W), lambda i: (jnp.maximum(i * hb - 1, 0), 0)),
                  pl.BlockSpec((4, _QKV_W), lambda i: (0, 0)),
                  pl.BlockSpec((tm, 128), lambda i: (i, _BA_COL)), one, one],
        out_specs=(row, row, row, pl.BlockSpec((tm, 128), lambda i: (i, 0)), pl.BlockSpec((tm, _QKV_W), lambda i: (i, 0))),
        out_shape=(jax.ShapeDtypeStruct((t, DN_W), F32),) * 3 + (jax.ShapeDtypeStruct((t, 128), F32),
                                                                 jax.ShapeDtypeStruct((t, _QKV_W), F32)),
        compiler_params=_cp("parallel"))(proj, proj, conv_w, proj, alog_row, dtb_row)


def gdn_prep_bwd(proj, conv, conv_w, alog_row, dtb_row, dq, dk, dv, dbg, dz, name, tm=256):
    t = proj.shape[0]
    hb = tm // HALO
    nt = t // tm
    qscale = DN_DIM ** -0.5
    te = tm + HALO

    def body(x_ref, c_ref, cn_ref, w_ref, ba_ref, al_ref, dt_ref, dq_ref, dk_ref, dv_ref,
             dqn_ref, dkn_ref, dvn_ref, dbg_ref, dz_ref, dx_ref, dw_ref, ddt_ref, dal_ref):
        i = pl.program_id(0)
        first = i == 0
        last = i == nt - 1
        dws = []
        for gi in range(3 * DN_HEADS):
            sl = slice(gi * DN_DIM, (gi + 1) * DN_DIM)
            osl = slice((gi % DN_HEADS) * DN_DIM, (gi % DN_HEADS + 1) * DN_DIM)
            c = jnp.concatenate([c_ref[:, sl], cn_ref[:, sl]], axis=0)
            sg = _sigmoid(c)
            s = c * sg
            d_ref, dn_ref = ((dq_ref, dqn_ref), (dk_ref, dkn_ref), (dv_ref, dvn_ref))[gi // DN_HEADS]
            dy = jnp.concatenate([d_ref[:, osl], jnp.where(last, 0.0, dn_ref[:, osl])], axis=0)
            if gi < 2 * DN_HEADS:
                r = lax.rsqrt(jnp.sum(s * s, axis=-1, keepdims=True) + EPS)
                sh = s * r
                ds = r * (dy - sh * jnp.sum(sh * dy, axis=-1, keepdims=True))
                if gi < DN_HEADS:
                    ds = ds * qscale
            else:
                ds = dy
            dc = ds * sg * (1.0 + c * (1.0 - sg))
            dcs = [dc[3 - j:3 - j + tm] for j in range(4)]
            dx = w_ref[0:1, sl] * dcs[0]
            for j in range(1, 4):
                dx = dx + w_ref[j:j + 1, sl] * dcs[j]
            dx_ref[:, sl] = dx.astype(dx_ref.dtype)
            x0 = x_ref[:, sl]
            dws.append(jnp.concatenate([jnp.sum(dcs[j] * x0, axis=0, keepdims=True) for j in range(4)], axis=0))
        dw = jnp.concatenate(dws, axis=-1)
        ba = ba_ref[...]
        dbgv = dbg_ref[...]
        lane = lax.broadcasted_iota(jnp.int32, ba.shape, 1)
        beta = _sigmoid(ba)
        ea = -jnp.exp(al_ref[...])
        zin = ba + dt_ref[...]
        is_b = lane < DN_HEADS
        is_a = (lane >= DN_HEADS) & (lane < 2 * DN_HEADS)
        da = jnp.where(is_a, dbgv * ea * _sigmoid(zin), 0.0)
        dx_ref[:, _QKV_W:_QKV_W + DN_W] = dz_ref[...]
        dx_ref[:, _QKV_W + DN_W:] = jnp.where(is_b, dbgv * beta * (1.0 - beta), da).astype(dx_ref.dtype)
        ddt = jnp.sum(da, axis=0, keepdims=True)
        dal = jnp.sum(jnp.where(is_a, dbgv * ea * _softplus(zin), 0.0), axis=0, keepdims=True)

        @pl.when(first)
        def _():
            dw_ref[...] = dw
            ddt_ref[...] = ddt
            dal_ref[...] = dal

        @pl.when(i > 0)
        def _():
            dw_ref[...] += dw
            ddt_ref[...] += ddt
            dal_ref[...] += dal

    row = pl.BlockSpec((tm, DN_W), lambda i: (i, 0))
    nrow = pl.BlockSpec((HALO, DN_W), lambda i: (jnp.minimum((i + 1) * hb, t // HALO - 1), 0))
    one = pl.BlockSpec((1, 128), lambda i: (0, 0))
    return pl.pallas_call(
        body, name=name, grid=(nt,),
        in_specs=[pl.BlockSpec((tm, _QKV_W), lambda i: (i, 0)), pl.BlockSpec((tm, _QKV_W), lambda i: (i, 0)),
                  pl.BlockSpec((HALO, _QKV_W), lambda i: (jnp.minimum((i + 1) * hb, t // HALO - 1), 0)),
                  pl.BlockSpec((4, _QKV_W), lambda i: (0, 0)),
                  pl.BlockSpec((tm, 128), lambda i: (i, _BA_COL)), one, one,
                  row, row, row, nrow, nrow, nrow, pl.BlockSpec((tm, 128), lambda i: (i, 0)), row],
        out_specs=(pl.BlockSpec((tm, ODD_IN_PAD), lambda i: (i, 0)), pl.BlockSpec((4, _QKV_W), lambda i: (0, 0)), one, one),
        out_shape=(jax.ShapeDtypeStruct((t, ODD_IN_PAD), MXU_DTYPE), jax.ShapeDtypeStruct((4, _QKV_W), F32),
                   jax.ShapeDtypeStruct((1, 128), F32), jax.ShapeDtypeStruct((1, 128), F32)),
        compiler_params=_cp("arbitrary"))(proj, conv, conv, conv_w, proj, alog_row, dtb_row, dq, dk, dv, dq, dk, dv, dbg,
                                          dz)


def _chunk_masks():
    r = lax.broadcasted_iota(jnp.int32, (DN_CHUNK, DN_CHUNK), 0)
    c = lax.broadcasted_iota(jnp.int32, (DN_CHUNK, DN_CHUNK), 1)
    return r >= c, r > c


INV_PACK = 2


def _inv_unit_lower_many(mats):
    n = DN_CHUNK
    wide = INV_PACK * n
    r = lax.broadcasted_iota(jnp.int32, (wide, wide), 0)
    c = lax.broadcasted_iota(jnp.int32, (wide, wide), 1)
    same = (r & -n) == (c & -n)
    eye = jnp.where((r[:n] == (c[:n] & (n - 1))), 1.0, 0.0)

    def blockdiag(row):
        return jnp.where(same, jnp.concatenate([row] * INV_PACK, axis=0), 0.0)

    packs = [jnp.concatenate(mats[g:g + INV_PACK], axis=-1) for g in range(0, len(mats), INV_PACK)]
    xs = [eye - a for a in packs]
    pws = [_hi(a, blockdiag(a)) for a in packs]
    for step in range(5):
        if step < 4:
            both = [_hi(jnp.concatenate([x, pw], axis=0), blockdiag(pw)) for x, pw in zip(xs, pws)]
            xs = [x + b[:n] for x, b in zip(xs, both)]
            pws = [b[n:] for b in both]
        else:
            xs = [x + _hi(x, blockdiag(pw)) for x, pw in zip(xs, pws)]
    return [x[:, j * n:(j + 1) * n] for x in xs for j in range(INV_PACK)]


def _chunk_common(q, k, beta, gc, gcr, lower):
    gam = jnp.exp(jnp.where(lower, gc - gcr, NEG))
    eg = jnp.exp(gc)
    gl = gc[DN_CHUNK - 1:DN_CHUNK, :]
    kdf = jnp.exp(gl - gc)
    kb = k * beta
    bmat = _mx_nt(kb, k)
    qmat = _mx_nt(q, k)
    return gam, eg, jnp.exp(gl), kdf, kb, bmat, qmat


DN_STEP = 4


def gdn_fwd(q, k, v, bg, proj, o_gain, name):
    t = q.shape[0]
    n_chunks = t // DN_CHUNK

    def body(q_ref, k_ref, v_ref, bg_ref, z_ref, g_ref, o_ref, sall_ref, tall_ref, y_ref, yt_ref, s_ref):
        n = pl.program_id(0)

        @pl.when(n == 0)
        def _():
            s_ref[...] = jnp.zeros_like(s_ref)

        lower, strict = _chunk_masks()
        ltri = jnp.where(lower, 1.0, 0.0)
        hs = range(DN_HEADS)
        sl = [slice(h * DN_DIM, (h + 1) * DN_DIM) for h in hs]
        units = [(c, h) for c in range(DN_STEP) for h in hs]
        nu = range(len(units))
        rs = [slice(c * DN_CHUNK, (c + 1) * DN_CHUNK) for c in range(DN_STEP)]
        bgv = [bg_ref[rs[c], :] for c in range(DN_STEP)]
        gcs = [_hi(ltri, b) for b in bgv]
        gcs_t = [g.T for g in gcs]
        qh = [q_ref[rs[c], sl[h]] for c, h in units]
        kh = [k_ref[rs[c], sl[h]] for c, h in units]
        vh = [v_ref[rs[c], sl[h]] for c, h in units]
        beta = [bgv[c][:, h:h + 1] for c, h in units]
        com = [_chunk_common(qh[u], kh[u], beta[u], gcs[c][:, DN_HEADS + h:DN_HEADS + h + 1],
                             gcs_t[c][DN_HEADS + h:DN_HEADS + h + 1, :], lower) for u, (c, h) in enumerate(units)]
        gam, eg, dec, kdf, kb, bmat, qmat = zip(*com)
        tms = _inv_unit_lower_many([jnp.where(strict, bmat[u] * gam[u], 0.0) for u in nu])
        for u, (c, h) in enumerate(units):
            tall_ref[c, h] = tms[u]
        uw = [_hi(tms[u], jnp.concatenate([vh[u] * beta[u], kb[u] * eg[u]], axis=-1)) for u in nu]
        qd = [qh[u] * eg[u] for u in nu]
        pm = [qmat[u] * gam[u] for u in nu]
        kd = [kh[u] * kdf[u] for u in nu]
        st = [s_ref[h] for h in hs]
        for c in range(DN_STEP):
            us = [c * DN_HEADS + h for h in hs]
            for h in hs:
                sall_ref[c, h] = st[h]
            v_new = [uw[us[h]][:, :DN_DIM] - _mx(uw[us[h]][:, DN_DIM:], st[h]) for h in hs]
            o_st = [_mx(qd[us[h]], st[h]) for h in hs]
            o_in = [_mx(pm[us[h]], v_new[h]) for h in hs]
            s_up = [_mx_tn(kd[us[h]], v_new[h]) for h in hs]
            for h in hs:
                ov = o_st[h] + o_in[h]
                o_ref[rs[c], sl[h]] = ov
                zv = z_ref[rs[c], sl[h]]
                y = ov * lax.rsqrt(jnp.mean(ov * ov, axis=-1, keepdims=True) + EPS) * g_ref[...] * (zv * _sigmoid(zv))
                y_ref[rs[c], sl[h]] = y.astype(y_ref.dtype)
                yt_ref[sl[h], rs[c]] = y.T.astype(yt_ref.dtype)
            st = [st[h] * dec[us[h]] + s_up[h] for h in hs]
        for h in hs:
            s_ref[h] = st[h]

    rows = DN_STEP * DN_CHUNK
    row = pl.BlockSpec((rows, DN_W), lambda n: (n, 0))
    return pl.pallas_call(
        body, name=name, grid=(n_chunks // DN_STEP,),
        in_specs=[row, row, row, pl.BlockSpec((rows, 128), lambda n: (n, 0)),
                  pl.BlockSpec((rows, DN_W), lambda n: (n, _Z_COL)), pl.BlockSpec((1, DN_DIM), lambda n: (0, 0))],
        out_specs=(row, pl.BlockSpec((DN_STEP, DN_HEADS, DN_DIM, DN_DIM), lambda n: (n, 0, 0, 0)),
                   pl.BlockSpec((DN_STEP, DN_HEADS, DN_CHUNK, DN_CHUNK), lambda n: (n, 0, 0, 0)),
                   row, pl.BlockSpec((DN_W, rows), lambda n: (0, n))),
        out_shape=(jax.ShapeDtypeStruct((t, DN_W), F32),
                   jax.ShapeDtypeStruct((n_chunks, DN_HEADS, DN_DIM, DN_DIM), F32),
                   jax.ShapeDtypeStruct((n_chunks, DN_HEADS, DN_CHUNK, DN_CHUNK), F32),
                   jax.ShapeDtypeStruct((t, DN_W), MXU_DTYPE), jax.ShapeDtypeStruct((DN_W, t), MXU_DTYPE)),
        scratch_shapes=[pltpu.VMEM((DN_HEADS, DN_DIM, DN_DIM), F32)],
        compiler_params=_cp("arbitrary"))(q, k, v, bg, proj, o_gain)


def gdn_bwd(q, k, v, bg, sall, tall, do, name):
    t = q.shape[0]
    n_chunks = t // DN_CHUNK

    def body(q_ref, k_ref, v_ref, bg_ref, sall_ref, tall_ref, do_ref, dq_ref, dk_ref, dv_ref, dbg_ref, ds_ref):
        n = pl.program_id(0)

        @pl.when(n == 0)
        def _():
            ds_ref[...] = jnp.zeros_like(ds_ref)

        lower, strict = _chunk_masks()
        ltri = jnp.where(lower, 1.0, 0.0)
        bgv = bg_ref[...]
        gcs = _hi(ltri, bgv)
        gcs_t = gcs.T
        lane = lax.broadcasted_iota(jnp.int32, (DN_CHUNK, 128), 1)
        rowi = lax.broadcasted_iota(jnp.int32, (DN_CHUNK, 1), 0)
        hs = range(DN_HEADS)
        each = lambda fn, *ls: [fn(*a) for a in zip(*ls)]
        rsum = lambda a: jnp.sum(a, axis=-1, keepdims=True)
        sl = [slice(h * DN_DIM, (h + 1) * DN_DIM) for h in hs]
        st = [sall_ref[0, h] for h in hs]
        tms = [tall_ref[0, h] for h in hs]
        dsn = [ds_ref[h] for h in hs]
        qh = [q_ref[:, sl[h]] for h in hs]
        kh = [k_ref[:, sl[h]] for h in hs]
        vh = [v_ref[:, sl[h]] for h in hs]
        doh = [do_ref[:, sl[h]] for h in hs]
        beta = [bgv[:, h:h + 1] for h in hs]
        com = [_chunk_common(qh[h], kh[h], beta[h], gcs[:, DN_HEADS + h:DN_HEADS + h + 1],
                             gcs_t[DN_HEADS + h:DN_HEADS + h + 1, :], lower) for h in hs]
        gam, eg, dec, kdf, kb, bmat, qmat = zip(*com)
        rhs_w = each(lambda a, b: a * b, kb, eg)
        uw = each(lambda t_, v_, b_, r_: _hi(t_, jnp.concatenate([v_ * b_, r_], axis=-1)), tms, vh, beta, rhs_w)
        qd = each(lambda a, b: a * b, qh, eg)
        kd = each(lambda a, b: a * b, kh, kdf)
        pmat = each(lambda a, b: a * b, qmat, gam)
        v_new = each(lambda uw_, s_: uw_[:, :DN_DIM] - _mx(uw_[:, DN_DIM:], s_), uw, st)
        dqd = each(_mx_nt, doh, st)
        ds_o = each(_mx_tn, qd, doh)
        dp = each(lambda d_, v_: jnp.where(lower, _mx_nt(d_, v_), 0.0), doh, v_new)
        dvn_o = each(_mx_tn, pmat, doh)
        ddec = each(lambda d_, s_: jnp.sum(rsum(d_ * s_), axis=0, keepdims=True), dsn, st)
        dkd = each(_mx_nt, v_new, dsn)
        dvn = each(lambda a, k_, d_: a + _mx(k_, d_), dvn_o, kd, dsn)
        dw = each(lambda d_, s_: -_mx_nt(d_, s_), dvn, st)
        ds_w = each(lambda uw_, d_: _mx_tn(uw_[:, DN_DIM:], d_), uw, dvn)
        for h in hs:
            ds_ref[h] = ds_o[h] + dec[h] * dsn[h] - ds_w[h]
        dr = each(lambda t_, a, b: _hi_tn(t_, jnp.concatenate([a, b], axis=-1)), tms, dvn, dw)
        da = each(lambda r_, uw_: jnp.where(strict, -_hi_nt(r_, uw_), 0.0), dr, uw)
        dru = [r_[:, :DN_DIM] for r_ in dr]
        drw = [r_[:, DN_DIM:] for r_ in dr]
        db = each(lambda a, b: a * b, da, gam)
        dq_m = each(lambda a, b: a * b, dp, gam)
        e = each(lambda a, bm, p_, qm, g_: (a * bm + p_ * qm) * g_, da, bmat, dp, qmat, gam)
        dkb = each(lambda b_, k_, r_, e_: _mx(b_, k_) + r_ * e_, db, kh, drw, eg)
        dk = each(lambda b_, kb_, m_, q_, d_, f_: _mx_tn(b_, kb_) + _mx_tn(m_, q_) + d_ * f_, db, kb, dq_m, qh, dkd, kdf)
        dq = each(lambda m_, k_, d_, e_: _mx(m_, k_) + d_ * e_, dq_m, kh, dqd, eg)
        tk = each(lambda a, b: rsum(a * b), dkd, kd)
        dbeta_all = jnp.zeros((DN_CHUNK, 128), F32)
        dgc_all = jnp.zeros((DN_CHUNK, 128), F32)
        for h in hs:
            dgc = (jnp.sum(e[h], axis=1, keepdims=True) - jnp.sum(e[h].T, axis=1, keepdims=True)
                   + rsum(dqd[h] * qd[h]) - tk[h] + rsum(drw[h] * rhs_w[h]))
            dgl = jnp.sum(tk[h], axis=0, keepdims=True) + ddec[h] * dec[h]
            dgc = dgc + jnp.where(rowi == DN_CHUNK - 1, dgl, 0.0)
            dbeta = rsum(dru[h] * vh[h]) + rsum(dkb[h] * kh[h])
            dq_ref[:, sl[h]] = dq[h]
            dk_ref[:, sl[h]] = dk[h] + dkb[h] * beta[h]
            dv_ref[:, sl[h]] = dru[h] * beta[h]
            dbeta_all = jnp.where(lane == h, dbeta, dbeta_all)
            dgc_all = jnp.where(lane == DN_HEADS + h, dgc, dgc_all)
        dbg_ref[...] = dbeta_all + _hi_tn(ltri, dgc_all)

    rev = lambda n: (n_chunks - 1 - n, 0)
    row = pl.BlockSpec((DN_CHUNK, DN_W), rev)
    small = pl.BlockSpec((DN_CHUNK, 128), rev)
    return pl.pallas_call(
        body, name=name, grid=(n_chunks,),
        in_specs=[row, row, row, small,
                  pl.BlockSpec((1, DN_HEADS, DN_DIM, DN_DIM), lambda n: (n_chunks - 1 - n, 0, 0, 0)),
                  pl.BlockSpec((1, DN_HEADS, DN_CHUNK, DN_CHUNK), lambda n: (n_chunks - 1 - n, 0, 0, 0)), row],
        out_specs=(row, row, row, small),
        out_shape=(jax.ShapeDtypeStruct((t, DN_W), F32),) * 3 + (jax.ShapeDtypeStruct((t, 128), F32),),
        scratch_shapes=[pltpu.VMEM((DN_HEADS, DN_DIM, DN_DIM), F32)],
        compiler_params=_cp("arbitrary"))(q, k, v, bg, sall, tall, do)


def gdn_out_bwd(o, proj, o_gain, dx, w_out, name, tm=512, after=None):
    t = o.shape[0]
    tm = min(tm, t)

    def body(o_ref, z_ref, g_ref, dx_ref, w_ref, *rest):
        do_ref, dz_ref, dg_ref = rest[-3:]
        i = pl.program_id(0)
        dy = _mx_nt(dx_ref[...], w_ref[...])
        dg = jnp.zeros((1, DN_DIM), F32)
        for h in range(DN_HEADS):
            sl = slice(h * DN_DIM, (h + 1) * DN_DIM)
            ov, zv, dyv = o_ref[:, sl], z_ref[:, sl], dy[:, sl]
            r = lax.rsqrt(jnp.mean(ov * ov, axis=-1, keepdims=True) + EPS)
            oh = ov * r
            sg = _sigmoid(zv)
            dz_ref[:, sl] = (dyv * oh * g_ref[...] * sg * (1.0 + zv * (1.0 - sg))).astype(dz_ref.dtype)
            don = dyv * (zv * sg)
            dg = dg + jnp.sum(don * oh, axis=0, keepdims=True)
            doh = don * g_ref[...]
            do_ref[:, sl] = r * (doh - oh * jnp.mean(doh * oh, axis=-1, keepdims=True))

        @pl.when(i == 0)
        def _():
            dg_ref[...] = dg

        @pl.when(i > 0)
        def _():
            dg_ref[...] += dg

    row = pl.BlockSpec((tm, DN_W), lambda i: (i, 0))
    one = pl.BlockSpec((1, DN_DIM), lambda i: (0, 0))
    in_specs = [row, pl.BlockSpec((tm, DN_W), lambda i: (i, _Z_COL)), one,
                pl.BlockSpec((tm, dx.shape[1]), lambda i: (i, 0)), pl.BlockSpec(w_out.shape, lambda i: (0, 0))]
    args = [o, proj, o_gain, dx, w_out]
    if after is not None:
        in_specs.append(pl.BlockSpec(memory_space=pl.ANY))
        args.append(after)
    return pl.pallas_call(
        body, name=name, grid=(t // tm,), in_specs=in_specs, out_specs=(row, row, one),
        out_shape=(jax.ShapeDtypeStruct((t, DN_W), F32), jax.ShapeDtypeStruct((t, DN_W), MXU_DTYPE),
                   jax.ShapeDtypeStruct((1, DN_DIM), F32)),
        compiler_params=_cp("arbitrary"))(*args)


def _peer(k):
    x, y, c = lax.axis_index("x"), lax.axis_index("y"), lax.axis_index("c")
    px = 1 - x if k & 4 else x
    py = 1 - y if k & 2 else y
    pc = 1 - c if k & 1 else c
    return (px, py, pc), 4 * px + 2 * py + pc


_HBM = pl.BlockSpec(memory_space=pltpu.HBM)
_SEM = pl.BlockSpec(memory_space=pltpu.SEMAPHORE)
_DATAFLOW = pltpu.SideEffectType.DATAFLOW_SIDE_EFFECTING
N_PEER = N_DEV - 1


def send_start(srcs, name, scatter, after):
    na = len(srcs)
    ns = (2 * N_PEER + 1) * na
    lands = [lax.empty((N_DEV,) + (s.shape[1:] if scatter else s.shape), s.dtype) for s in srcs]
    extra = [] if after is None else [after]

    def body(*refs):
        src_refs, land_refs = refs[:na], refs[na:2 * na]
        sems = refs[2 * na + len(extra):2 * na + len(extra) + ns]
        land_out, token = refs[-1 - na:-1], refs[-1]
        _, me = _peer(0)
        for a in range(na):
            pltpu.make_async_copy(src_refs[a].at[me] if scatter else src_refs[a], land_out[a].at[me],
                                  sems[2 * N_PEER * na + a]).start()
        for k in range(1, N_DEV):
            peer, pid = _peer(k)
            for a in range(na):
                pltpu.make_async_remote_copy(
                    src_ref=src_refs[a].at[pid] if scatter else src_refs[a], dst_ref=land_refs[a].at[me],
                    send_sem=sems[2 * (a * N_PEER + k - 1)], recv_sem=sems[2 * (a * N_PEER + k - 1) + 1],
                    device_id=peer, device_id_type=MESH).start()
        token[...] = jnp.zeros_like(token)

    hbm = lambda arrs: tuple(pltpu.HBM(a.shape, a.dtype) for a in arrs)
    outs = pl.pallas_call(
        body, name=name,
        out_shape=(pltpu.SemaphoreType.DMA(()),) * ns + hbm(srcs) + hbm(lands) + (jax.ShapeDtypeStruct((8, 128), F32),),
        in_specs=[_HBM] * (2 * na) + [pl.BlockSpec(memory_space=pl.ANY)] * len(extra),
        out_specs=(_SEM,) * ns + (_HBM,) * (2 * na) + (pl.BlockSpec(memory_space=pltpu.VMEM),),
        input_output_aliases={i: ns + i for i in range(2 * na)},
        compiler_params=pltpu.CompilerParams(has_side_effects=_DATAFLOW),
    )(*[pltpu.with_memory_space_constraint(a, pltpu.HBM) for a in list(srcs) + lands], *extra)
    return outs[:ns], outs[ns:ns + na], outs[ns + na:ns + 2 * na], outs[-1]


def send_wait(sems, srcs_thru, lands_thru, name, scatter, after):
    na = len(srcs_thru)
    ns = (2 * N_PEER + 1) * na

    def body(*refs):
        src_refs, land_refs, sm = refs[:na], refs[na:2 * na], refs[2 * na:2 * na + ns]
        _, me = _peer(0)
        for a in range(na):
            pltpu.make_async_copy(src_refs[a].at[me] if scatter else src_refs[a], land_refs[a].at[me],
                                  sm[2 * N_PEER * na + a]).wait()
        for k in range(1, N_DEV):
            peer, pid = _peer(k)
            for a in range(na):
                cp = pltpu.make_async_remote_copy(
                    src_ref=src_refs[a].at[pid] if scatter else src_refs[a], dst_ref=land_refs[a].at[pid],
                    send_sem=sm[2 * (a * N_PEER + k - 1)], recv_sem=sm[2 * (a * N_PEER + k - 1) + 1],
                    device_id=peer, device_id_type=MESH)
                cp.wait_send()
                cp.wait_recv()

    hbm = lambda arrs: tuple(pltpu.HBM(a.shape, a.dtype) for a in arrs)
    outs = pl.pallas_call(
        body, name=name, out_shape=hbm(srcs_thru) + hbm(lands_thru),
        in_specs=[_HBM] * (2 * na) + [_SEM] * ns + [pl.BlockSpec(memory_space=pl.ANY)], out_specs=(_HBM,) * (2 * na),
        input_output_aliases={i: i for i in range(2 * na)},
        compiler_params=pltpu.CompilerParams(has_side_effects=_DATAFLOW),
    )(*srcs_thru, *lands_thru, *sems, after)
    return outs[na:]


def _adamw(w, g, m, v):
    m = ADAM_B1 * m + (1.0 - ADAM_B1) * g
    v = ADAM_B2 * v + (1.0 - ADAM_B2) * (g * g)
    m_hat = m / (1.0 - ADAM_B1 ** ADAM_STEP)
    v_hat = v / (1.0 - ADAM_B2 ** ADAM_STEP)
    return -ADAM_LR * (m_hat / (jnp.sqrt(v_hat) + ADAM_EPS) + ADAM_WD * w), m, v


def adam_sum(w, pieces, m, v, name, layer=0, into=None):
    nl, r, c = w.shape
    tr = r
    for cand in (256, 128, 64, 32, 16, 8):
        if r % cand == 0:
            tr = cand
            break

    def body(w_ref, p_ref, m_ref, v_ref, *rest):
        g_ref, d_ref, nm_ref, nv_ref = rest[-4:]
        g = p_ref[0].astype(F32)
        for s in range(1, N_DEV):
            g = g + p_ref[s].astype(F32)
        g_ref[0] = g
        d_ref[0], nm_ref[0], nv_ref[0] = _adamw(w_ref[0], g, m_ref[0], v_ref[0])

    row = pl.BlockSpec((1, tr, c), lambda i: (layer, i, 0))
    out = jax.ShapeDtypeStruct((nl, r, c), F32)
    extra = [] if into is None else list(into)
    return pl.pallas_call(
        body, name=name, grid=(r // tr,),
        in_specs=[row, pl.BlockSpec((N_DEV, tr, c), lambda i: (0, i, 0)), row, row]
        + [pl.BlockSpec(memory_space=pl.ANY)] * len(extra),
        out_specs=(row,) * 4, out_shape=(out,) * 4,
        input_output_aliases={4 + i: i for i in range(len(extra))},
        compiler_params=_cp("parallel"))(w, pieces, m, v, *extra)


def sum_rows(gathered, name):
    _, r, c = gathered.shape

    def body(p_ref, o_ref):
        g = p_ref[0]
        for s in range(1, N_DEV):
            g = g + p_ref[s]
        o_ref[...] = g

    return pl.pallas_call(body, name=name, out_shape=jax.ShapeDtypeStruct((r, c), F32))(gathered)


def adam_small(w, g, m, v, name):
    def body(w_ref, g_ref, m_ref, v_ref, d_ref, nm_ref, nv_ref):
        d_ref[...], nm_ref[...], nv_ref[...] = _adamw(w_ref[...], g_ref[...], m_ref[...], v_ref[...])

    out = jax.ShapeDtypeStruct(w.shape, F32)
    return pl.pallas_call(body, name=name, out_shape=(out,) * 3)(w, g, m, v)


def _rope_tables(t):
    inv_freq = 10000.0 ** (-jnp.arange(0, HEAD_DIM, 2, dtype=F32) / HEAD_DIM)
    lanes = lambda a: jnp.concatenate([a] * (128 // a.shape[-1]), axis=-1)
    base = (jnp.arange(t // QK_TM, dtype=F32) * QK_TM)[:, None] * inv_freq[None, :]
    offs = jnp.arange(QK_TM, dtype=F32)[:, None] * inv_freq[None, :]
    return (lanes(jnp.cos(base))[:, None, :], lanes(jnp.sin(base))[:, None, :], lanes(jnp.cos(offs)), lanes(jnp.sin(offs)))


def _lane_row(vec8):
    return jnp.pad(vec8.reshape(1, DN_HEADS), ((0, 0), (DN_HEADS, 128 - 2 * DN_HEADS)))


def _ffn_bwd(x, norm_g, w_gu, w_d, saved, dy, tag, after=None):
    ft, gu, at = saved
    dgu = ffn_dact(dy, w_d, gu, f"{tag}_d_gate_up", after=after)
    dwd = mm_at(at, dy, f"{tag}_dw_down")
    nj = D_FF // GU_TILE
    dwgu = mm_at(ft, dgu, f"{tag}_dw_gate_up", transposed=True, tn=GU_TILE, row_block=lambda q: (q % 2) * nj + q // 2)
    dx, dg = mm_rms_bwd(dgu, w_gu, x, norm_g, dy, f"{tag}_d_norm", chunks=_GU_CHUNKS)
    return dx, dwgu, dwd, dg


def local_step(x, target, small, weights_of, grads_out, after=None):
    t = x.shape[0]
    rope = _rope_tables(t)
    alog_row, dtb_row = _lane_row(small["odd_a_log"]), _lane_row(small["odd_dt_bias"])

    h0, h0t = rms_fwd(x, small["even_norm"], "even_norm", after=after)
    we = weights_of("even", h0)
    small = {**small, **we.get("small", {})}
    proj0 = mm_nt(h0, we["w_in"], "even_in_proj")
    qr, kr = qk_prep_fwd(proj0, small["even_q_gain"], small["even_k_gain"], rope, "even_qk_prep")
    y_attn, mix0, mix0_t, lse = swa_fwd(qr, kr, proj0, small["even_sinks"], "even_swa")
    mix0, mix0_t = gconv_fwd(proj0, small["even_conv_w"], mix0, mix0_t, "even_gconv")
    we = {**we, **weights_of("even_out", mix0)}
    x1, f0, f0t = mm_nn_res_norm(mix0, we["w_out"], x, small["ffn_norm0"], "even_out_proj")
    w0 = weights_of("ffn0", x1)
    gu0, a0, a0t = ffn_up(f0, w0["gate_up"], "ffn0_gate_up")
    ffn0 = (f0t, gu0, a0t)
    x2, h1, h1t = mm_nn_res_norm(a0, w0["down"], x1, small["odd_norm"], "ffn0_down")

    wo = weights_of("odd", x2)
    proj1 = mm_nt(h1, wo["w_in"], "odd_in_proj")
    qn, kn, vs, bg, conv1 = gdn_prep_fwd(proj1, small["odd_conv_w"], alog_row, dtb_row, "odd_prep")
    o, sall, tall, og, ogt = gdn_fwd(qn, kn, vs, bg, proj1, small["odd_o_gain"], "odd_delta_rule")
    x3, f1, f1t = mm_nn_res_norm(og, wo["w_out"], x2, small["ffn_norm1"], "odd_out_proj")
    w1 = weights_of("ffn1", x3)
    gu1, a1, a1t = ffn_up(f1, w1["gate_up"], "ffn1_gate_up")
    ffn1 = (f1t, gu1, a1t)
    dy, loss_row = mm_nn_res_loss(a1, w1["down"], x3, target, "ffn1_down_loss")

    gs = {}
    dx3, dwgu, dwd, gs["ffn_norm1"] = _ffn_bwd(x3, small["ffn_norm1"], w1["gate_up"], w1["down"], ffn1, dy, "ffn1")
    tok = grads_out("ffn1", {"gate_up": dwgu, "down": dwd})

    do, dz, gs["odd_o_gain"] = gdn_out_bwd(o, proj1, small["odd_o_gain"], dx3, wo["w_out"], "odd_d_gate_norm", after=tok)
    dwo = mm_at(ogt, dx3, "odd_dw_out")
    dqn, dkn, dvs, dbg = gdn_bwd(qn, kn, vs, bg, sall, tall, do, "odd_d_delta_rule")
    dproj1, gs["odd_conv_w"], ddt_row, dal_row = gdn_prep_bwd(
        proj1, conv1, small["odd_conv_w"], alog_row, dtb_row, dqn, dkn, dvs, dbg, dz, "odd_d_prep")
    gs["odd_dt_bias"] = ddt_row[:, DN_HEADS:2 * DN_HEADS]
    gs["odd_a_log"] = dal_row[:, DN_HEADS:2 * DN_HEADS]
    dwi = mm_at(h1t, dproj1, "odd_dw_in", transposed=True)
    dx2, gs["odd_norm"] = mm_rms_bwd(dproj1, wo["w_in"], x2, small["odd_norm"], dx3, "odd_d_norm")
    tok = grads_out("odd", {"w_in": dwi, "w_out": dwo})

    dx1, dwgu, dwd, gs["ffn_norm0"] = _ffn_bwd(x1, small["ffn_norm0"], w0["gate_up"], w0["down"], ffn0, dx2, "ffn0",
                                               after=tok)
    tok = grads_out("ffn0", {"gate_up": dwgu, "down": dwd})

    dmix = mm_nt(dx1, we["w_out"], "even_d_mix", after=tok)
    dwo = mm_at(mix0_t, dx1, "even_dw_out")
    dqr, dkr, dv, gs["even_sinks"] = swa_bwd(qr, kr, proj0, small["even_sinks"], y_attn, lse, dmix, "even_d_swa")
    dqk, gs["even_q_gain"], gs["even_k_gain"] = qk_prep_bwd(
        proj0, small["even_q_gain"], small["even_k_gain"], rope, dqr, dkr, "even_d_qk_prep")
    dgb, dgc, dxi, gs["even_conv_w"] = gconv_bwd(proj0, small["even_conv_w"], dmix, "even_d_gconv")
    dproj0 = jnp.concatenate([dqk, dv, dgb, dgc, dxi], axis=-1)
    dwi = mm_at(h0t, dproj0, "even_dw_in", transposed=True)
    tok = grads_out("even", {"w_in": dwi, "w_out": dwo})
    grad_x, gs["even_norm"] = mm_rms_bwd(dproj0, we["w_in"], x, small["even_norm"], dx1, "even_d_norm", after=tok)
    return loss_row, grad_x, gs


_SMALL_ORDER = ("even_norm", "even_q_gain", "even_k_gain", "even_sinks", "odd_a_log", "odd_dt_bias", "odd_o_gain",
                "ffn_norm0", "ffn_norm1", "odd_norm", "even_conv_w", "odd_conv_w")
_SMALL_SIZE = {"even_norm": 1024, "even_q_gain": 64, "even_k_gain": 64, "even_sinks": 8, "odd_a_log": 8,
               "odd_dt_bias": 8, "odd_o_gain": 128, "ffn_norm0": 1024, "ffn_norm1": 1024, "odd_norm": 1024,
               "even_conv_w": 3 * 512, "odd_conv_w": 4 * 3072}
_N_REPL = 9


def _pack_rows(vals):
    flat = jnp.concatenate([v.reshape(-1) for v in vals])
    pad = (-flat.shape[0]) % 1024
    return jnp.pad(flat, (0, pad)).reshape(-1, 128)


def _my_block(full, size, axis):
    me = 4 * lax.axis_index("x") + 2 * lax.axis_index("y") + lax.axis_index("c")
    return lax.dynamic_slice_in_dim(full, me * size, size, axis=axis)


def kernel(x, even_norm, even_w_in, even_q_gain, even_k_gain, even_sinks, even_conv_w, even_w_out, odd_norm, odd_w_in, odd_conv_w, odd_a_log, odd_dt_bias, odd_o_gain, odd_w_out, ffn_norm, ffn_w_gate_up, ffn_w_down, loss_target, m_even_norm, m_even_w_in, m_even_q_gain, m_even_k_gain, m_even_sinks, m_even_conv_w, m_even_w_out, m_odd_norm, m_odd_w_in, m_odd_conv_w, m_odd_a_log, m_odd_dt_bias, m_odd_o_gain, m_odd_w_out, m_ffn_norm, m_ffn_w_gate_up, m_ffn_w_down, v_even_norm, v_even_w_in, v_even_q_gain, v_even_k_gain, v_even_sinks, v_even_conv_w, v_even_w_out, v_odd_norm, v_odd_w_in, v_odd_conv_w, v_odd_a_log, v_odd_dt_bias, v_odd_o_gain, v_odd_w_out, v_ffn_norm, v_ffn_w_gate_up, v_ffn_w_down):
    t = x.shape[1]
    d = D_MODEL

    tr = lambda a: jnp.swapaxes(a, 1, 2)
    shard = {
        "even": {"w_in": tr(even_w_in)[0], "w_out": even_w_out[0]},
        "ffn0": {"gate_up": tr(ffn_w_gate_up)[0], "down": ffn_w_down[0]},
        "odd": {"w_in": tr(odd_w_in)[0], "w_out": odd_w_out[0]},
        "ffn1": {"gate_up": tr(ffn_w_gate_up)[1], "down": ffn_w_down[1]},
    }
    given = {
        ("even", "w_in"): ("even_w_in", even_w_in, m_even_w_in, v_even_w_in, 0),
        ("even", "w_out"): ("even_w_out", even_w_out, m_even_w_out, v_even_w_out, 0),
        ("odd", "w_in"): ("odd_w_in", odd_w_in, m_odd_w_in, v_odd_w_in, 0),
        ("odd", "w_out"): ("odd_w_out", odd_w_out, m_odd_w_out, v_odd_w_out, 0),
        ("ffn0", "gate_up"): ("ffn_w_gate_up", ffn_w_gate_up, m_ffn_w_gate_up, v_ffn_w_gate_up, 0),
        ("ffn1", "gate_up"): ("ffn_w_gate_up", ffn_w_gate_up, m_ffn_w_gate_up, v_ffn_w_gate_up, 1),
        ("ffn0", "down"): ("ffn_w_down", ffn_w_down, m_ffn_w_down, v_ffn_w_down, 0),
        ("ffn1", "down"): ("ffn_w_down", ffn_w_down, m_ffn_w_down, v_ffn_w_down, 1),
    }

    def whole(group, parts):
        col, row = tuple(shard[group])
        w_col = parts[0].reshape(-1, d)
        if group == "odd":
            w_col = jnp.pad(w_col, ((0, ODD_IN_PAD - ODD_IN_W), (0, 0)))
        return {col: w_col, row: parts[1].reshape(-1, d)}

    wire = {g: [a.astype(MXU_DTYPE) for a in shard[g].values()] for g in shard}
    wire["even_out"] = [wire["even"].pop()]
    wire["even"].append(_pack_rows([odd_norm, even_conv_w, odd_conv_w]))
    gathers, tok = {}, None
    for g in ("even", "even_out", "ffn0", "odd", "ffn1"):
        sems, srcs_thru, lands_thru, tok = send_start(wire[g], f"gather_{g}_start", False, tok)
        gathers[g] = (sems, srcs_thru, lands_thru)
    o1 = d // N_DEV
    o2 = o1 + 3 * CONV_CH // N_DEV

    def weights_of(group, after):
        lands = send_wait(*gathers[group], f"gather_{group}_wait", False, after)
        if group == "even_out":
            return {"w_out": lands[0].reshape(-1, d)}
        if group != "even":
            return whole(group, lands)
        sg = lands[1].reshape(N_DEV, -1)
        return {"w_in": lands[0].reshape(-1, d), "small": {
            "odd_norm": sg[:, :o1].reshape(1, d),
            "even_conv_w": sg[:, o1:o2].reshape(N_DEV, 3, CONV_CH // N_DEV).transpose(1, 0, 2).reshape(3, CONV_CH),
            "odd_conv_w": sg[:, o2:o2 + 4 * _QKV_W // N_DEV].reshape(N_DEV, 4, _QKV_W // N_DEV)
            .transpose(1, 0, 2).reshape(4, _QKV_W)}}

    sent = {}

    def grads_out(group, dws):
        col, row = tuple(shard[group])
        n_cols = N_DEV * shard[group][col].shape[0]
        pieces = [dws[col][:n_cols].reshape((N_DEV,) + shard[group][col].shape),
                  dws[row].reshape((N_DEV,) + shard[group][row].shape)]
        sems, srcs_thru, lands_thru, token = send_start(pieces, f"exchange_{group}_start", True, None)
        sent[group] = (sems, srcs_thru, lands_thru, pieces)
        return token

    small = {
        "even_norm": even_norm, "even_q_gain": even_q_gain, "even_k_gain": even_k_gain, "even_sinks": even_sinks,
        "odd_a_log": odd_a_log.reshape(-1), "odd_dt_bias": odd_dt_bias.reshape(-1), "odd_o_gain": odd_o_gain,
        "ffn_norm0": ffn_norm[0:1], "ffn_norm1": ffn_norm[1:2],
    }

    loss_row, grad_x, gs = local_step(x.reshape(t, d), loss_target.reshape(t, d), small, weights_of, grads_out, after=tok)

    rows = _pack_rows([gs[n] for n in _SMALL_ORDER] + [loss_row[:, 0:1]])
    small_sent = send_start([rows], "gather_small_grads_start", False, None)

    res, behind = {}, small_sent[3]
    for g in ("ffn1", "odd", "ffn0", "even"):
        sems, srcs_thru, lands_thru, pieces = sent[g]
        lands = send_wait(sems, srcs_thru, lands_thru, f"exchange_{g}_wait", True, behind)
        for i, (key, pcs) in enumerate(zip(shard[g], lands)):
            name, w_, m_, v_, layer = given[g, key]
            view = tr if i == 0 else (lambda a: a)
            res[name] = adam_sum(view(w_), pcs, view(m_), view(v_), f"adamw_{g}_{key}", layer=layer, into=res.get(name))
        behind = res[name][0]
    for name in ("even_w_in", "odd_w_in", "ffn_w_gate_up"):
        res[name] = tuple(tr(a) for a in res[name])

    (rows_g,) = send_wait(*small_sent[:3], "gather_small_grads_wait", False, behind)
    tot = sum_rows(rows_g, "sum_small_grads").reshape(-1)
    off, sgrad = 0, {}
    for n in _SMALL_ORDER:
        sgrad[n] = tot[off:off + _SMALL_SIZE[n]]
        off += _SMALL_SIZE[n]
    loss = tot[off]

    repl = _SMALL_ORDER[:_N_REPL]
    repl_w = {"even_norm": even_norm, "even_q_gain": even_q_gain, "even_k_gain": even_k_gain, "even_sinks": even_sinks,
              "odd_a_log": odd_a_log, "odd_dt_bias": odd_dt_bias, "odd_o_gain": odd_o_gain,
              "ffn_norm0": ffn_norm[0], "ffn_norm1": ffn_norm[1]}
    repl_m = {"even_norm": m_even_norm, "even_q_gain": m_even_q_gain, "even_k_gain": m_even_k_gain,
              "even_sinks": m_even_sinks, "odd_a_log": m_odd_a_log, "odd_dt_bias": m_odd_dt_bias,
              "odd_o_gain": m_odd_o_gain, "ffn_norm0": m_ffn_norm[0], "ffn_norm1": m_ffn_norm[1]}
    repl_v = {"even_norm": v_even_norm, "even_q_gain": v_even_q_gain, "even_k_gain": v_even_k_gain,
              "even_sinks": v_even_sinks, "odd_a_log": v_odd_a_log, "odd_dt_bias": v_odd_dt_bias,
              "odd_o_gain": v_odd_o_gain, "ffn_norm0": v_ffn_norm[0], "ffn_norm1": v_ffn_norm[1]}
    pk = lambda dct: _pack_rows([dct[n] for n in repl])
    pd_, pm_, pv_ = adam_small(pk(repl_w), pk(sgrad), pk(repl_m), pk(repl_v), "adamw_replicated")
    sres = {}
    off = 0
    for n in repl:
        sz = _SMALL_SIZE[n]
        sres[n] = (sgrad[n], pd_.reshape(-1)[off:off + sz], pm_.reshape(-1)[off:off + sz], pv_.reshape(-1)[off:off + sz])
        off += sz
    g_on = _my_block(sgrad["odd_norm"].reshape(1, d), d // N_DEV, 1)
    g_ec = _my_block(sgrad["even_conv_w"].reshape(3, CONV_CH), CONV_CH // N_DEV, 1)
    g_oc = _my_block(sgrad["odd_conv_w"].reshape(4, _QKV_W), _QKV_W // N_DEV, 1)
    shard_w = _pack_rows([odd_norm, even_conv_w, odd_conv_w])
    sd_, sm_, sv_ = adam_small(shard_w, _pack_rows([g_on, g_ec, g_oc]),
                               _pack_rows([m_odd_norm, m_even_conv_w, m_odd_conv_w]),
                               _pack_rows([v_odd_norm, v_even_conv_w, v_odd_conv_w]), "adamw_sharded_small")
    off = 0
    for n, gfull, like in (("odd_norm", g_on, odd_norm), ("even_conv_w", g_ec, even_conv_w), ("odd_conv_w", g_oc, odd_conv_w)):
        sz = like.size
        sres[n] = (gfull, sd_.reshape(-1)[off:off + sz], sm_.reshape(-1)[off:off + sz], sv_.reshape(-1)[off:off + sz])
        off += sz

    def small_out(name, like, kind):
        if name == "ffn_norm":
            return jnp.stack([sres["ffn_norm0"][kind], sres["ffn_norm1"][kind]]).reshape(like.shape)
        return sres[name][kind].reshape(like.shape)

    order = (("even_norm", even_norm), ("even_w_in", even_w_in), ("even_q_gain", even_q_gain),
             ("even_k_gain", even_k_gain), ("even_sinks", even_sinks), ("even_conv_w", even_conv_w),
             ("even_w_out", even_w_out), ("odd_norm", odd_norm), ("odd_w_in", odd_w_in), ("odd_conv_w", odd_conv_w),
             ("odd_a_log", odd_a_log), ("odd_dt_bias", odd_dt_bias), ("odd_o_gain", odd_o_gain),
             ("odd_w_out", odd_w_out), ("ffn_norm", ffn_norm), ("ffn_w_gate_up", ffn_w_gate_up),
             ("ffn_w_down", ffn_w_down))
    outs = [loss, grad_x.reshape(x.shape)]
    for kind in range(4):
        for name, like in order:
            outs.append(res[name][kind] if name in res else small_out(name, like, kind))
    return tuple(outs)
```

```python
import jax
import jax.numpy as jnp
import numpy as np
from jax import lax
from jax.experimental import pallas as pl
from jax.experimental.pallas import tpu as pltpu

F32 = jnp.float32
MXU_DTYPE = jnp.bfloat16
HI = lax.Precision.HIGH
EPS = 1e-6
N_DEV = 8
D_MODEL = 1024
HEAD_DIM = 64
ATTN_HEADS = 8
KV_HEADS = 2
ATTN_BLOCK = 128
Q_W = 512
KV_W = 128
CONV_CH = 512
EVEN_IN_W = 2304
DN_HEADS = 8
DN_DIM = 128
DN_W = 1024
DN_CHUNK = 64
ODD_IN_W = 4112
ODD_IN_PAD = 4224
D_FF = 2816
NEG = -1e30
VMEM_LIMIT = 56 * 1024 * 1024
ADAM_LR, ADAM_B1, ADAM_B2, ADAM_EPS, ADAM_WD, ADAM_STEP = 0.001, 0.9, 0.999, 1e-08, 0.01, 10
MESH = pl.DeviceIdType.MESH


def _cp(*sem):
    return pltpu.CompilerParams(dimension_semantics=sem, vmem_limit_bytes=VMEM_LIMIT)


def _pick(n, cap):
    best = 128
    for t in range(128, cap + 1, 128):
        if n % t == 0:
            best = t
    return best


def _mx(a, b):
    return jnp.dot(a.astype(MXU_DTYPE), b.astype(MXU_DTYPE), preferred_element_type=F32)


def _mx_nt(a, b):
    return lax.dot_general(a.astype(MXU_DTYPE), b.astype(MXU_DTYPE), (((1,), (1,)), ((), ())),
                           preferred_element_type=F32)


def _mx_tn(a, b):
    return lax.dot_general(a.astype(MXU_DTYPE), b.astype(MXU_DTYPE), (((0,), (0,)), ((), ())),
                           preferred_element_type=F32)


def _hi(a, b):
    return jnp.dot(a, b, precision=HI, preferred_element_type=F32)


def _hi_nt(a, b):
    return lax.dot_general(a, b, (((1,), (1,)), ((), ())), precision=HI, preferred_element_type=F32)


def _hi_tn(a, b):
    return lax.dot_general(a, b, (((0,), (0,)), ((), ())), precision=HI, preferred_element_type=F32)


def _sigmoid(x):
    return 0.5 * jnp.tanh(0.5 * x) + 0.5


def _softplus(x):
    return jnp.maximum(x, 0.0) + jnp.log(1.0 + jnp.exp(-jnp.abs(x)))


def mm_nn_res_norm(a, b, res, g, name, tm=512):
    t, k = a.shape
    d = b.shape[1]
    tm = min(tm, t)

    def body(a_ref, b_ref, res_ref, g_ref, y_ref, h_ref, ht_ref):
        y = res_ref[...] + _mx(a_ref[...], b_ref[...])
        y_ref[...] = y
        h = y * lax.rsqrt(jnp.mean(y * y, axis=-1, keepdims=True) + EPS) * g_ref[...]
        h_ref[...] = h.astype(h_ref.dtype)
        ht_ref[...] = h.T.astype(ht_ref.dtype)

    row = pl.BlockSpec((tm, d), lambda i: (i, 0))
    return pl.pallas_call(
        body, name=name, grid=(t // tm,),
        in_specs=[pl.BlockSpec((tm, k), lambda i: (i, 0)), pl.BlockSpec((k, d), lambda i: (0, 0)), row,
                  pl.BlockSpec((1, d), lambda i: (0, 0))],
        out_specs=(row, row, pl.BlockSpec((d, tm), lambda i: (0, i))),
        out_shape=(jax.ShapeDtypeStruct((t, d), F32), jax.ShapeDtypeStruct((t, d), MXU_DTYPE),
                   jax.ShapeDtypeStruct((d, t), MXU_DTYPE)),
        compiler_params=_cp("parallel"))(a, b, res, g)


def mm_nt(a, b, name, out_dtype=F32, tm=2048, after=None):
    m, k = a.shape
    n, _ = b.shape
    tn = _pick(n, 512 if k > 3000 else 1536)
    tm = min(tm, m)

    def body(a_ref, b_ref, *rest):
        o_ref = rest[-1]
        o_ref[...] = _mx_nt(a_ref[...], b_ref[...]).astype(o_ref.dtype)

    in_specs = [pl.BlockSpec((tm, k), lambda j, i: (i, 0)), pl.BlockSpec((tn, k), lambda j, i: (j, 0))]
    args = [a, b]
    if after is not None:
        in_specs.append(pl.BlockSpec(memory_space=pl.ANY))
        args.append(after)
    return pl.pallas_call(
        body, name=name, grid=(n // tn, m // tm), in_specs=in_specs,
        out_specs=pl.BlockSpec((tm, tn), lambda j, i: (i, j)),
        out_shape=jax.ShapeDtypeStruct((m, n), out_dtype), compiler_params=_cp("parallel", "parallel"))(*args)


def mm_at(at, b, name, tk=2048, transposed=False, tn=None, row_block=None, chunks=None):
    m, kk = at.shape
    _, n = b.shape
    tm, tn, tk = _pick(m, 1408), tn or _pick(n, 2816), min(tk, kk)
    nk = kk // tk
    assert chunks is None or (transposed and tn == n)

    def body(a_ref, b_ref, o_ref, acc_ref):
        k = pl.program_id(2)
        p = _mx(a_ref[...], b_ref[...])
        acc = jnp.where(k == 0, p, acc_ref[...] + p)
        acc_ref[...] = acc

        @pl.when(k == nk - 1)
        def _():
            res = (acc.T if transposed else acc).astype(o_ref.dtype)
            if chunks is None:
                o_ref[...] = res
            else:
                for cb, co, size in chunks:
                    o_ref[co:co + size, :] = res[cb:cb + size]

    if transposed:
        rb = row_block or (lambda j: j)
        out_spec = pl.BlockSpec((tn, tm), lambda i, j, k: (rb(j), i))
        out_shape = jax.ShapeDtypeStruct((n, m), MXU_DTYPE)
    else:
        out_spec = pl.BlockSpec((tm, tn), lambda i, j, k: (i, j))
        out_shape = jax.ShapeDtypeStruct((m, n), MXU_DTYPE)
    return pl.pallas_call(
        body, name=name, grid=(m // tm, n // tn, nk),
        in_specs=[pl.BlockSpec((tm, tk), lambda i, j, k: (i, k)), pl.BlockSpec((tk, tn), lambda i, j, k: (k, j))],
        out_specs=out_spec, out_shape=out_shape, scratch_shapes=[pltpu.VMEM((tm, tn), F32)],
        compiler_params=_cp("parallel", "parallel", "arbitrary"))(at, b)


def rms_fwd(x, g, name, tm=512, after=None):
    t, d = x.shape

    def body(x_ref, g_ref, *rest):
        o_ref, ot_ref = rest[-2:]
        xv = x_ref[...]
        r = lax.rsqrt(jnp.mean(xv * xv, axis=-1, keepdims=True) + EPS)
        h = xv * r * g_ref[...]
        o_ref[...] = h.astype(o_ref.dtype)
        ot_ref[...] = h.T.astype(ot_ref.dtype)

    in_specs = [pl.BlockSpec((tm, d), lambda i: (i, 0)), pl.BlockSpec((1, d), lambda i: (0, 0))]
    args = [x, g]
    if after is not None:
        in_specs.append(pl.BlockSpec(memory_space=pl.ANY))
        args.append(after)
    return pl.pallas_call(
        body, name=name, grid=(t // tm,), in_specs=in_specs,
        out_specs=(pl.BlockSpec((tm, d), lambda i: (i, 0)), pl.BlockSpec((d, tm), lambda i: (0, i))),
        out_shape=(jax.ShapeDtypeStruct((t, d), MXU_DTYPE), jax.ShapeDtypeStruct((d, t), MXU_DTYPE)),
        compiler_params=_cp("parallel"))(*args)


def mm_rms_bwd(a, bt, x, g, dres, name, tm=512, after=None, chunks=None):
    t, k = a.shape
    d = bt.shape[1]
    tm = min(tm if k > 3000 else 2 * tm, t)
    chunks = chunks or ((0, 0, k),)

    def body(a_ref, b_ref, x_ref, g_ref, dres_ref, *rest):
        dx_ref, dg_ref = rest[-2:]
        dhv = None
        for ca, cb, size in chunks:
            part = _mx(a_ref[:, ca:ca + size], b_ref[cb:cb + size, :])
            dhv = part if dhv is None else dhv + part
        xv = x_ref[...]
        r = lax.rsqrt(jnp.mean(xv * xv, axis=-1, keepdims=True) + EPS)
        xh = xv * r
        dxh = dhv * g_ref[...]
        dx_ref[...] = dres_ref[...] + r * (dxh - xh * jnp.mean(dxh * xh, axis=-1, keepdims=True))
        part = jnp.sum(dhv * xh, axis=0, keepdims=True)
        dg_ref[...] = jnp.where(pl.program_id(0) == 0, part, dg_ref[...] + part)

    row = pl.BlockSpec((tm, d), lambda i: (i, 0))
    one = pl.BlockSpec((1, d), lambda i: (0, 0))
    in_specs = [pl.BlockSpec((tm, k), lambda i: (i, 0)), pl.BlockSpec((k, d), lambda i: (0, 0)), row, one, row]
    args = [a, bt, x, g, dres]
    if after is not None:
        in_specs.append(pl.BlockSpec(memory_space=pl.ANY))
        args.append(after)
    return pl.pallas_call(
        body, name=name, grid=(t // tm,), in_specs=in_specs, out_specs=(row, one),
        out_shape=(jax.ShapeDtypeStruct((t, d), F32), jax.ShapeDtypeStruct((1, d), F32)),
        compiler_params=_cp("arbitrary"))(*args)


GU_TILE = 1408


def ffn_up(f, wt, name, tm=1024):
    t, d = f.shape
    tm = min(tm, t)
    nj = D_FF // GU_TILE

    def body(f_ref, wg_ref, wu_ref, gu_ref, a_ref, at_ref):
        g = _mx_nt(f_ref[...], wg_ref[...])
        u = _mx_nt(f_ref[...], wu_ref[...])
        sg = _sigmoid(g)
        gs = g * sg
        gu_ref[:, :GU_TILE] = (u * (sg + gs - gs * sg)).astype(gu_ref.dtype)
        gu_ref[:, GU_TILE:] = gs.astype(gu_ref.dtype)
        act = gs * u
        a_ref[...] = act.astype(a_ref.dtype)
        at_ref[...] = act.T.astype(at_ref.dtype)

    return pl.pallas_call(
        body, name=name, grid=(nj, t // tm),
        in_specs=[pl.BlockSpec((tm, d), lambda j, i: (i, 0)), pl.BlockSpec((GU_TILE, d), lambda j, i: (j, 0)),
                  pl.BlockSpec((GU_TILE, d), lambda j, i: (nj + j, 0))],
        out_specs=(pl.BlockSpec((tm, 2 * GU_TILE), lambda j, i: (i, j)), pl.BlockSpec((tm, GU_TILE), lambda j, i: (i, j)),
                   pl.BlockSpec((GU_TILE, tm), lambda j, i: (j, i))),
        out_shape=(jax.ShapeDtypeStruct((t, 2 * D_FF), MXU_DTYPE), jax.ShapeDtypeStruct((t, D_FF), MXU_DTYPE),
                   jax.ShapeDtypeStruct((D_FF, t), MXU_DTYPE)),
        compiler_params=_cp("parallel", "parallel"))(f, wt, wt)


_GU_CHUNKS = tuple((q * GU_TILE, ((q % 2) * (D_FF // GU_TILE) + q // 2) * GU_TILE, GU_TILE)
                   for q in range(2 * D_FF // GU_TILE))


def ffn_dact(dy, w_d, gu, name, tm=1024, after=None):
    t, d = dy.shape
    tm = min(tm, t)

    def body(dy_ref, w_ref, gu_ref, *rest):
        o_ref = rest[-1]
        da = _mx_nt(dy_ref[...], w_ref[...])
        o_ref[:, :GU_TILE] = (da * gu_ref[:, :GU_TILE]).astype(o_ref.dtype)
        o_ref[:, GU_TILE:] = (da * gu_ref[:, GU_TILE:]).astype(o_ref.dtype)

    in_specs = [pl.BlockSpec((tm, d), lambda j, i: (i, 0)), pl.BlockSpec((GU_TILE, d), lambda j, i: (j, 0)),
                pl.BlockSpec((tm, 2 * GU_TILE), lambda j, i: (i, j))]
    args = [dy, w_d, gu]
    if after is not None:
        in_specs.append(pl.BlockSpec(memory_space=pl.ANY))
        args.append(after)
    return pl.pallas_call(
        body, name=name, grid=(D_FF // GU_TILE, t // tm), in_specs=in_specs,
        out_specs=pl.BlockSpec((tm, 2 * GU_TILE), lambda j, i: (i, j)),
        out_shape=jax.ShapeDtypeStruct((t, 2 * D_FF), MXU_DTYPE), compiler_params=_cp("parallel", "parallel"))(*args)


def mm_nn_res_loss(a, b, res, target, name, tm=512):
    t, k = a.shape
    d = b.shape[1]
    tm = min(tm, t)

    def body(a_ref, b_ref, res_ref, t_ref, dy_ref, l_ref):
        e = res_ref[...] + _mx(a_ref[...], b_ref[...]) - t_ref[...]
        dy_ref[...] = e * (1.0 / d)
        part = jnp.zeros((1, 128), F32) + 0.5 * jnp.sum(jnp.mean(e * e, axis=-1, keepdims=True), axis=0, keepdims=True)
        l_ref[...] = jnp.where(pl.program_id(0) == 0, part, l_ref[...] + part)

    row = pl.BlockSpec((tm, d), lambda i: (i, 0))
    return pl.pallas_call(
        body, name=name, grid=(t // tm,),
        in_specs=[pl.BlockSpec((tm, k), lambda i: (i, 0)), pl.BlockSpec((k, d), lambda i: (0, 0)), row, row],
        out_specs=(row, pl.BlockSpec((1, 128), lambda i: (0, 0))),
        out_shape=(jax.ShapeDtypeStruct((t, d), F32), jax.ShapeDtypeStruct((1, 128), F32)),
        compiler_params=_cp("arbitrary"))(a, b, res, target)


QK_W = Q_W + KV_W
_QK_TILE = 256


def _qk_mats():
    idx = np.arange(_QK_TILE)
    half = HEAD_DIM // 2
    same = (idx[:, None] // HEAD_DIM) == (idx[None, :] // HEAD_DIM)
    lo = (idx % HEAD_DIM) < half
    rot = np.where((idx[:, None] == idx[None, :] + half) & lo[None, :], -1.0, 0.0)
    rot = rot + np.where((idx[:, None] == idx[None, :] - half) & ~lo[None, :], 1.0, 0.0)
    return jnp.asarray(same, F32), jnp.asarray(rot, F32)


QK_TM = 256


def _qk_gains(q_gain, k_gain):
    return jnp.concatenate([q_gain] * ATTN_HEADS + [k_gain] * KV_HEADS, axis=-1)


def _rope_tile(ca_ref, sa_ref, cb_ref, sb_ref):
    ca, sa, cb, sb = ca_ref[0], sa_ref[0], cb_ref[...], sb_ref[...]
    c, s = ca * cb - sa * sb, sa * cb + ca * sb
    rep = QK_W // 128
    return jnp.concatenate([c] * rep, axis=-1), jnp.concatenate([s] * rep, axis=-1)


_ROPE_SPECS = [pl.BlockSpec((1, 1, 128), lambda i: (i, 0, 0)), pl.BlockSpec((1, 1, 128), lambda i: (i, 0, 0)),
               pl.BlockSpec((QK_TM, 128), lambda i: (0, 0)), pl.BlockSpec((QK_TM, 128), lambda i: (0, 0))]


def _qk_tiles(a, mat, transposed=False):
    outs = []
    for c0 in range(0, QK_W, _QK_TILE):
        w = min(_QK_TILE, QK_W - c0)
        mt = (mat.T if transposed else mat)[:w, :w].astype(MXU_DTYPE)
        at = a[:, c0:c0 + w]
        hi = at.astype(MXU_DTYPE)
        lo = (at - hi.astype(F32)).astype(MXU_DTYPE)
        outs.append(jnp.dot(hi, mt, preferred_element_type=F32) + jnp.dot(lo, mt, preferred_element_type=F32))
    return jnp.concatenate(outs, axis=-1)


def qk_prep_fwd(proj, q_gain, k_gain, rope, name):
    t = proj.shape[0]
    tm = QK_TM
    gmat, rmat = _qk_mats()
    gain = _qk_gains(q_gain, k_gain)

    def body(p_ref, g_ref, ca_ref, sa_ref, cb_ref, sb_ref, gm_ref, rm_ref, q_ref, k_ref):
        x = p_ref[...]
        r = lax.rsqrt(_qk_tiles(x * x, gm_ref[...]) * (1.0 / HEAD_DIM) + EPS)
        xn = x * r * g_ref[...]
        c, s = _rope_tile(ca_ref, sa_ref, cb_ref, sb_ref)
        out = xn * c + _qk_tiles(xn, rm_ref[...]) * s
        q_ref[...] = out[:, :Q_W]
        k_ref[...] = out[:, Q_W:]

    full = pl.BlockSpec((_QK_TILE, _QK_TILE), lambda i: (0, 0))
    return pl.pallas_call(
        body, name=name, grid=(t // tm,),
        in_specs=[pl.BlockSpec((tm, QK_W), lambda i: (i, 0)), pl.BlockSpec((1, QK_W), lambda i: (0, 0))] + _ROPE_SPECS
        + [full, full],
        out_specs=(pl.BlockSpec((tm, Q_W), lambda i: (i, 0)), pl.BlockSpec((tm, KV_W), lambda i: (i, 0))),
        out_shape=(jax.ShapeDtypeStruct((t, Q_W), F32), jax.ShapeDtypeStruct((t, KV_W), F32)),
        compiler_params=_cp("parallel"))(proj, gain, *rope, gmat, rmat)


def qk_prep_bwd(proj, q_gain, k_gain, rope, dq, dk, into, name):
    t = proj.shape[0]
    tm = QK_TM
    gmat, rmat = _qk_mats()
    gain = _qk_gains(q_gain, k_gain)
    lanes = np.arange(QK_W)[:, None]
    fold = jnp.asarray(lanes % HEAD_DIM + np.where(lanes >= Q_W, HEAD_DIM, 0) == np.arange(128)[None, :], F32)

    def body(p_ref, g_ref, ca_ref, sa_ref, cb_ref, sb_ref, gm_ref, rm_ref, f_ref, dq_ref, dk_ref, into_ref,
             o_ref, dg_ref):
        x = p_ref[...]
        r = lax.rsqrt(_qk_tiles(x * x, gm_ref[...]) * (1.0 / HEAD_DIM) + EPS)
        xh = x * r
        c, s = _rope_tile(ca_ref, sa_ref, cb_ref, sb_ref)
        dout = jnp.concatenate([dq_ref[...], dk_ref[...]], axis=-1)
        dxn = dout * c + _qk_tiles(dout * s, rm_ref[...], transposed=True)
        part = _hi(jnp.sum(dxn * xh, axis=0, keepdims=True), f_ref[...])
        dxh = dxn * g_ref[...]
        mean = _qk_tiles(dxh * xh, gm_ref[...]) * (1.0 / HEAD_DIM)
        o_ref[...] = (r * (dxh - xh * mean)).astype(o_ref.dtype)
        dg_ref[...] = jnp.where(pl.program_id(0) == 0, part, dg_ref[...] + part)

    full = pl.BlockSpec((_QK_TILE, _QK_TILE), lambda i: (0, 0))
    dqk, dg = pl.pallas_call(
        body, name=name, grid=(t // tm,),
        in_specs=[pl.BlockSpec((tm, QK_W), lambda i: (i, 0)), pl.BlockSpec((1, QK_W), lambda i: (0, 0))] + _ROPE_SPECS
        + [full, full, pl.BlockSpec((QK_W, 128), lambda i: (0, 0)),
                  pl.BlockSpec((tm, Q_W), lambda i: (i, 0)), pl.BlockSpec((tm, KV_W), lambda i: (i, 0)),
                  pl.BlockSpec(memory_space=pl.ANY)],
        out_specs=(pl.BlockSpec((tm, QK_W), lambda i: (i, 0)), pl.BlockSpec((1, 128), lambda i: (0, 0))),
        out_shape=(jax.ShapeDtypeStruct(into.shape, into.dtype), jax.ShapeDtypeStruct((1, 128), F32)),
        input_output_aliases={len(rope) + 7: 0},
        compiler_params=_cp("arbitrary"))(proj, gain, *rope, gmat, rmat, fold, dq, dk, into)
    return dqk, dg[:, :HEAD_DIM], dg[:, HEAD_DIM:]


def _swa_valid(n, grp):
    qi = lax.broadcasted_iota(jnp.int32, (grp * ATTN_BLOCK, 2 * ATTN_BLOCK), 0) & (ATTN_BLOCK - 1)
    kj = lax.broadcasted_iota(jnp.int32, (grp * ATTN_BLOCK, 2 * ATTN_BLOCK), 1)
    diff = qi + ATTN_BLOCK - kj
    return (diff >= 0) & (diff < ATTN_BLOCK) & (n * ATTN_BLOCK - ATTN_BLOCK + kj >= 0)


def _stack_heads(ref, g, grp, rows=slice(None)):
    return jnp.concatenate([ref[rows, (g * grp + j) * HEAD_DIM:(g * grp + j + 1) * HEAD_DIM] for j in range(grp)], axis=0)


def _stack_sinks(s_ref, g, grp):
    return jnp.concatenate([jnp.zeros((ATTN_BLOCK, 1), F32) + s_ref[0:1, g * grp + j:g * grp + j + 1]
                            for j in range(grp)], axis=0)


SWA_STEP = 2


def swa_fwd(q, k, proj, sinks, name):
    t = q.shape[0]
    nb = t // ATTN_BLOCK
    scale = HEAD_DIM ** -0.5
    grp = ATTN_HEADS // KV_HEADS

    rows = SWA_STEP * ATTN_BLOCK

    def body(q_ref, kc_ref, kp_ref, vc_ref, vp_ref, s_ref, y_ref, mix_ref, yt_ref, lse_ref):
        n0 = pl.program_id(0) * SWA_STEP
        kk = jnp.concatenate([kp_ref[...], kc_ref[...]], axis=0).astype(MXU_DTYPE)
        vv = jnp.concatenate([vp_ref[...], vc_ref[...]], axis=0).astype(MXU_DTYPE)
        lane = lax.broadcasted_iota(jnp.int32, (ATTN_BLOCK, ATTN_HEADS), 1)
        units = [(b, g) for b in range(SWA_STEP) for g in range(KV_HEADS)]
        blk = lambda b: slice(b * ATTN_BLOCK, (b + 1) * ATTN_BLOCK)
        keys = lambda b: slice(b * ATTN_BLOCK, (b + 2) * ATTN_BLOCK)
        col = lambda g: slice(g * HEAD_DIM, (g + 1) * HEAD_DIM)
        valid = [_swa_valid(n0 + b, grp) for b in range(SWA_STEP)]
        qg = [_stack_heads(q_ref, g, grp, blk(b)) for b, g in units]
        sink = [_stack_sinks(s_ref, g, grp) for b, g in units]
        sc = [jnp.where(valid[b], _mx_nt(qg[u], kk[keys(b), col(g)]) * scale, NEG) for u, (b, g) in enumerate(units)]
        m = [jnp.maximum(jnp.max(sc_, axis=-1, keepdims=True), sk) for sc_, sk in zip(sc, sink)]
        e = [jnp.exp(sc_ - m_) for sc_, m_ in zip(sc, m)]
        den = [jnp.sum(e_, axis=-1, keepdims=True) + jnp.exp(sk - m_) for e_, sk, m_ in zip(e, sink, m)]
        og = [_mx(e[u] / den[u], vv[keys(b), col(g)]) for u, (b, g) in enumerate(units)]
        lg = [m_ + jnp.log(d_) for m_, d_ in zip(m, den)]
        for b in range(SWA_STEP):
            lse = jnp.zeros((ATTN_BLOCK, ATTN_HEADS), F32)
            outs = []
            for h in range(ATTN_HEADS):
                u = b * KV_HEADS + h // grp
                sub = blk(h % grp)
                outs.append(og[u][sub])
                lse = jnp.where(lane == h, lg[u][sub], lse)
            y = jnp.concatenate(outs, axis=-1)
            y_ref[blk(b), :] = y
            mix_ref[blk(b), :] = y.astype(mix_ref.dtype)
            yt_ref[:, blk(b)] = y.T.astype(yt_ref.dtype)
            lse_ref[blk(b), :] = lse

    cur = lambda n: (n, 0)
    prev = lambda n: (jnp.maximum(n * SWA_STEP - 1, 0), 0)
    vcol = (Q_W + KV_W) // KV_W
    return pl.pallas_call(
        body, name=name, grid=(nb // SWA_STEP,),
        in_specs=[pl.BlockSpec((rows, Q_W), cur), pl.BlockSpec((rows, KV_W), cur),
                  pl.BlockSpec((ATTN_BLOCK, KV_W), prev),
                  pl.BlockSpec((rows, KV_W), lambda n: (n, vcol)),
                  pl.BlockSpec((ATTN_BLOCK, KV_W), lambda n: (jnp.maximum(n * SWA_STEP - 1, 0), vcol)),
                  pl.BlockSpec((1, ATTN_HEADS), lambda n: (0, 0))],
        out_specs=(pl.BlockSpec((rows, Q_W), cur), pl.BlockSpec((rows, Q_W), cur),
                   pl.BlockSpec((Q_W, rows), lambda n: (0, n)), pl.BlockSpec((rows, ATTN_HEADS), cur)),
        out_shape=(jax.ShapeDtypeStruct((t, Q_W), F32), jax.ShapeDtypeStruct((t, Q_W + CONV_CH), MXU_DTYPE),
                   jax.ShapeDtypeStruct((Q_W + CONV_CH, t), MXU_DTYPE), jax.ShapeDtypeStruct((t, ATTN_HEADS), F32)),
        compiler_params=_cp("parallel"))(q, k, k, proj, proj, sinks)


def swa_bwd(q, k, proj, sinks, y, lse, dmix, into, name):
    t = q.shape[0]
    nb = t // ATTN_BLOCK
    scale = HEAD_DIM ** -0.5
    grp = ATTN_HEADS // KV_HEADS

    def body(q_ref, kc_ref, kp_ref, vc_ref, vp_ref, s_ref, y_ref, lse_ref, dy_ref, into_ref,
             dq_ref, dk_ref, dv_ref, ds_ref, dkc, dvc):
        n = pl.program_id(0)

        @pl.when(n == 0)
        def _():
            dkc[...] = jnp.zeros_like(dkc)
            dvc[...] = jnp.zeros_like(dvc)
            ds_ref[...] = jnp.zeros_like(ds_ref)

        @pl.when(n < nb)
        def _():
            valid = _swa_valid(n, grp)
            kk = jnp.concatenate([kp_ref[...], kc_ref[...]], axis=0).astype(MXU_DTYPE)
            vv = jnp.concatenate([vp_ref[...], vc_ref[...]], axis=0).astype(MXU_DTYPE)
            lane = lax.broadcasted_iota(jnp.int32, (1, ATTN_HEADS), 1)
            gs = range(KV_HEADS)
            kg = [kk[:, g * HEAD_DIM:(g + 1) * HEAD_DIM] for g in gs]
            vg = [vv[:, g * HEAD_DIM:(g + 1) * HEAD_DIM] for g in gs]
            qg = [_stack_heads(q_ref, g, grp).astype(MXU_DTYPE) for g in gs]
            dog = [_stack_heads(dy_ref, g, grp) for g in gs]
            og = [_stack_heads(y_ref, g, grp) for g in gs]
            lg = [jnp.concatenate([lse_ref[:, g * grp + j:g * grp + j + 1] for j in range(grp)], axis=0) for g in gs]
            sink = [_stack_sinks(s_ref, g, grp) for g in gs]
            sc = [jnp.where(valid, _mx_nt(qg[g], kg[g]) * scale, NEG) for g in gs]
            p = [jnp.exp(sc[g] - lg[g]) for g in gs]
            delta = [jnp.sum(dog[g] * og[g], axis=-1, keepdims=True) for g in gs]
            ds = [p[g] * (_mx_nt(dog[g], vg[g]) - delta[g]) for g in gs]
            dqg = [_mx(ds[g], kg[g]) * scale for g in gs]
            dkf = jnp.concatenate([_mx_tn(ds[g], qg[g]) * scale for g in gs], axis=-1)
            dvf = jnp.concatenate([_mx_tn(p[g], dog[g]) for g in gs], axis=-1)
            dsk = [jnp.exp(sink[g] - lg[g]) * delta[g] for g in gs]
            dsink = jnp.zeros((1, ATTN_HEADS), F32)
            dqs = []
            for h in range(ATTN_HEADS):
                rows = slice((h % grp) * ATTN_BLOCK, (h % grp + 1) * ATTN_BLOCK)
                dqs.append(dqg[h // grp][rows])
                dsink = jnp.where(lane == h, -jnp.sum(dsk[h // grp][rows], axis=0, keepdims=True), dsink)
            dq_ref[...] = jnp.concatenate(dqs, axis=-1)
            dk_ref[...] = dkc[...] + dkf[:ATTN_BLOCK]
            dv_ref[...] = (dvc[...] + dvf[:ATTN_BLOCK]).astype(dv_ref.dtype)
            dkc[...] = dkf[ATTN_BLOCK:]
            dvc[...] = dvf[ATTN_BLOCK:]
            ds_ref[...] += dsink

        @pl.when(n == nb)
        def _():
            dk_ref[...] = dkc[...]
            dv_ref[...] = dvc[...].astype(dv_ref.dtype)

    cur = lambda n: (jnp.minimum(n, nb - 1), 0)
    prev = lambda n: (jnp.clip(n - 1, 0, nb - 1), 0)
    vcol = (Q_W + KV_W) // KV_W
    return pl.pallas_call(
        body, name=name, grid=(nb + 1,),
        in_specs=[pl.BlockSpec((ATTN_BLOCK, Q_W), cur), pl.BlockSpec((ATTN_BLOCK, KV_W), cur),
                  pl.BlockSpec((ATTN_BLOCK, KV_W), prev),
                  pl.BlockSpec((ATTN_BLOCK, KV_W), lambda n: (jnp.minimum(n, nb - 1), vcol)),
                  pl.BlockSpec((ATTN_BLOCK, KV_W), lambda n: (jnp.clip(n - 1, 0, nb - 1), vcol)),
                  pl.BlockSpec((1, ATTN_HEADS), lambda n: (0, 0)),
                  pl.BlockSpec((ATTN_BLOCK, Q_W), cur), pl.BlockSpec((ATTN_BLOCK, ATTN_HEADS), cur),
                  pl.BlockSpec((ATTN_BLOCK, Q_W), cur), pl.BlockSpec(memory_space=pl.ANY)],
        out_specs=(pl.BlockSpec((ATTN_BLOCK, Q_W), cur), pl.BlockSpec((ATTN_BLOCK, KV_W), prev),
                   pl.BlockSpec((ATTN_BLOCK, KV_W), lambda n: (jnp.clip(n - 1, 0, nb - 1), vcol)),
                   pl.BlockSpec((1, ATTN_HEADS), lambda n: (0, 0))),
        out_shape=(jax.ShapeDtypeStruct((t, Q_W), F32), jax.ShapeDtypeStruct((t, KV_W), F32),
                   jax.ShapeDtypeStruct(into.shape, into.dtype), jax.ShapeDtypeStruct((1, ATTN_HEADS), F32)),
        scratch_shapes=[pltpu.VMEM((ATTN_BLOCK, KV_W), F32), pltpu.VMEM((ATTN_BLOCK, KV_W), F32)],
        input_output_aliases={9: 2},
        compiler_params=_cp("arbitrary"))(q, k, k, proj, proj, sinks, y, lse, dmix, into)


GC_W = 256
_GB0, _GC0, _XI0 = 768 // GC_W, 1280 // GC_W, 1792 // GC_W
HALO = 8


def gconv_fwd(proj, conv_w, mix, mix_t, name, tm=512):
    t = proj.shape[0]
    hb = tm // HALO
    half = Q_W // GC_W

    def body(gb_ref, gc_ref, xi_ref, gch_ref, xih_ref, w_ref, mix_in, mixt_in, y_ref, yt_ref):
        i = pl.program_id(1)
        u = gc_ref[...] * xi_ref[...]
        uh = jnp.where(i == 0, 0.0, gch_ref[...] * xih_ref[...])
        up = jnp.concatenate([uh, u], axis=0)
        cv = w_ref[0:1, :] * up[HALO - 2:HALO - 2 + tm]
        cv = cv + w_ref[1:2, :] * up[HALO - 1:HALO - 1 + tm]
        cv = cv + w_ref[2:3, :] * u
        y = gb_ref[...] * cv
        y_ref[...] = y.astype(y_ref.dtype)
        yt_ref[...] = y.T.astype(yt_ref.dtype)

    def col(c0):
        return pl.BlockSpec((tm, GC_W), lambda cj, i: (i, c0 + cj))

    def halo(c0):
        return pl.BlockSpec((HALO, GC_W), lambda cj, i: (jnp.maximum(i * hb - 1, 0), c0 + cj))

    return pl.pallas_call(
        body, name=name, grid=(CONV_CH // GC_W, t // tm),
        in_specs=[col(_GB0), col(_GC0), col(_XI0), halo(_GC0), halo(_XI0),
                  pl.BlockSpec((3, GC_W), lambda cj, i: (0, cj)),
                  pl.BlockSpec(memory_space=pl.ANY), pl.BlockSpec(memory_space=pl.ANY)],
        out_specs=(pl.BlockSpec((tm, GC_W), lambda cj, i: (i, half + cj)),
                   pl.BlockSpec((GC_W, tm), lambda cj, i: (half + cj, i))),
        out_shape=(jax.ShapeDtypeStruct(mix.shape, mix.dtype), jax.ShapeDtypeStruct(mix_t.shape, mix_t.dtype)),
        input_output_aliases={6: 0, 7: 1},
        compiler_params=_cp("parallel", "parallel"))(proj, proj, proj, proj, proj, conv_w, mix, mix_t)


def gconv_bwd(proj, conv_w, dmix, into, name, tm=512):
    t = proj.shape[0]
    hb = tm // HALO
    nt = t // tm
    dy0 = Q_W // GC_W

    def body(gb_ref, gc_ref, xi_ref, gch_ref, xih_ref, gbn_ref, dyn_ref, dy_ref, w_ref, into_ref, o_ref, dw_ref):
        dgb_ref, dgc_ref, dxi_ref = (o_ref.at[:, j * GC_W:(j + 1) * GC_W] for j in range(3))
        i = pl.program_id(1)
        gc, xi, gb, dy = gc_ref[...], xi_ref[...], gb_ref[...], dy_ref[...]
        u = gc * xi
        uh = jnp.where(i == 0, 0.0, gch_ref[...] * xih_ref[...])
        up = jnp.concatenate([uh, u], axis=0)
        u2 = up[HALO - 2:HALO - 2 + tm]
        u1 = up[HALO - 1:HALO - 1 + tm]
        cv = w_ref[0:1, :] * u2 + w_ref[1:2, :] * u1 + w_ref[2:3, :] * u
        dgb_ref[...] = (dy * cv).astype(dgb_ref.dtype)
        dcv = dy * gb
        dcvn = jnp.where(i == nt - 1, 0.0, dyn_ref[...] * gbn_ref[...])
        dcvp = jnp.concatenate([dcv, dcvn], axis=0)
        du = w_ref[0:1, :] * dcvp[2:2 + tm] + w_ref[1:2, :] * dcvp[1:1 + tm] + w_ref[2:3, :] * dcv
        dgc_ref[...] = (du * xi).astype(dgc_ref.dtype)
        dxi_ref[...] = (du * gc).astype(dxi_ref.dtype)
        dw = jnp.concatenate([jnp.sum(dcv * u2, axis=0, keepdims=True), jnp.sum(dcv * u1, axis=0, keepdims=True),
                              jnp.sum(dcv * u, axis=0, keepdims=True)], axis=0)

        @pl.when(i == 0)
        def _():
            dw_ref[...] = dw

        @pl.when(i > 0)
        def _():
            dw_ref[...] += dw

    def col(c0):
        return pl.BlockSpec((tm, GC_W), lambda cj, i: (i, c0 + cj))

    def halo(c0):
        return pl.BlockSpec((HALO, GC_W), lambda cj, i: (jnp.maximum(i * hb - 1, 0), c0 + cj))

    def nxt(c0):
        return pl.BlockSpec((HALO, GC_W), lambda cj, i: (jnp.minimum((i + 1) * hb, t // HALO - 1), c0 + cj))

    return pl.pallas_call(
        body, name=name, grid=(CONV_CH // GC_W, nt),
        in_specs=[col(_GB0), col(_GC0), col(_XI0), halo(_GC0), halo(_XI0), nxt(_GB0), nxt(dy0), col(dy0),
                  pl.BlockSpec((3, GC_W), lambda cj, i: (0, cj)), pl.BlockSpec(memory_space=pl.ANY)],
        out_specs=(pl.BlockSpec((tm, 3 * GC_W), lambda cj, i: (i, 1 + cj)),
                   pl.BlockSpec((3, GC_W), lambda cj, i: (0, cj))),
        out_shape=(jax.ShapeDtypeStruct(into.shape, into.dtype), jax.ShapeDtypeStruct((3, CONV_CH), F32)),
        input_output_aliases={9: 0},
        compiler_params=_cp("parallel", "arbitrary"))(proj, proj, proj, proj, proj, proj, dmix, dmix, conv_w, into)


_EVEN_D_CHUNKS = ((0, 0, 768),) + tuple(
    (768 + (3 * j + part) * GC_W, 768 + part * CONV_CH + j * GC_W, GC_W)
    for j in range(CONV_CH // GC_W) for part in range(3))


_QKV_W = 3 * DN_W
_BA_COL = (4 * DN_W) // 128
_Z_COL = _QKV_W // DN_W


def gdn_prep_fwd(proj, conv_w, alog_row, dtb_row, name, tm=256):
    t = proj.shape[0]
    hb = tm // HALO
    qscale = DN_DIM ** -0.5

    def body(x_ref, xh_ref, w_ref, ba_ref, al_ref, dt_ref, q_ref, k_ref, v_ref, bg_ref, c_ref):
        i = pl.program_id(0)
        for gi in range(3 * DN_HEADS):
            sl = slice(gi * DN_DIM, (gi + 1) * DN_DIM)
            xp = jnp.concatenate([jnp.where(i == 0, 0.0, xh_ref[:, sl]), x_ref[:, sl]], axis=0)
            c = w_ref[0:1, sl] * xp[HALO - 3:HALO - 3 + tm]
            for j in range(1, 4):
                c = c + w_ref[j:j + 1, sl] * xp[HALO - 3 + j:HALO - 3 + j + tm]
            c_ref[:, sl] = c
            s = c * _sigmoid(c)
            osl = slice((gi % DN_HEADS) * DN_DIM, (gi % DN_HEADS + 1) * DN_DIM)
            if gi < DN_HEADS:
                q_ref[:, osl] = s * lax.rsqrt(jnp.sum(s * s, axis=-1, keepdims=True) + EPS) * qscale
            elif gi < 2 * DN_HEADS:
                k_ref[:, osl] = s * lax.rsqrt(jnp.sum(s * s, axis=-1, keepdims=True) + EPS)
            else:
                v_ref[:, osl] = s
        ba = ba_ref[...]
        lane = lax.broadcasted_iota(jnp.int32, ba.shape, 1)
        gval = -jnp.exp(al_ref[...]) * _softplus(ba + dt_ref[...])
        bg_ref[...] = jnp.where(lane < DN_HEADS, _sigmoid(ba), jnp.where(lane < 2 * DN_HEADS, gval, 0.0))

    row = pl.BlockSpec((tm, DN_W), lambda i: (i, 0))
    one = pl.BlockSpec((1, 128), lambda i: (0, 0))
    return pl.pallas_call(
        body, name=name, grid=(t // tm,),
        in_specs=[pl.BlockSpec((tm, _QKV_W), lambda i: (i, 0)),
                  pl.BlockSpec((HALO, _QKV_W), lambda i: (jnp.maximum(i * hb - 1, 0), 0)),
                  pl.BlockSpec((4, _QKV_W), lambda i: (0, 0)),
                  pl.BlockSpec((tm, 128), lambda i: (i, _BA_COL)), one, one],
        out_specs=(row, row, row, pl.BlockSpec((tm, 128), lambda i: (i, 0)), pl.BlockSpec((tm, _QKV_W), lambda i: (i, 0))),
        out_shape=(jax.ShapeDtypeStruct((t, DN_W), F32),) * 3 + (jax.ShapeDtypeStruct((t, 128), F32),
                                                                 jax.ShapeDtypeStruct((t, _QKV_W), F32)),
        compiler_params=_cp("parallel"))(proj, proj, conv_w, proj, alog_row, dtb_row)


def gdn_prep_bwd(proj, conv, conv_w, alog_row, dtb_row, dq, dk, dv, dbg, dz, name, tm=256):
    t = proj.shape[0]
    hb = tm // HALO
    nt = t // tm
    qscale = DN_DIM ** -0.5
    te = tm + HALO

    def body(x_ref, c_ref, cn_ref, w_ref, ba_ref, al_ref, dt_ref, dq_ref, dk_ref, dv_ref,
             dqn_ref, dkn_ref, dvn_ref, dbg_ref, dz_ref, dx_ref, dw_ref, ddt_ref, dal_ref):
        i = pl.program_id(0)
        first = i == 0
        last = i == nt - 1
        dws = []
        for gi in range(3 * DN_HEADS):
            sl = slice(gi * DN_DIM, (gi + 1) * DN_DIM)
            osl = slice((gi % DN_HEADS) * DN_DIM, (gi % DN_HEADS + 1) * DN_DIM)
            c = jnp.concatenate([c_ref[:, sl], cn_ref[:, sl]], axis=0)
            sg = _sigmoid(c)
            s = c * sg
            d_ref, dn_ref = ((dq_ref, dqn_ref), (dk_ref, dkn_ref), (dv_ref, dvn_ref))[gi // DN_HEADS]
            dy = jnp.concatenate([d_ref[:, osl], jnp.where(last, 0.0, dn_ref[:, osl])], axis=0)
            if gi < 2 * DN_HEADS:
                r = lax.rsqrt(jnp.sum(s * s, axis=-1, keepdims=True) + EPS)
                sh = s * r
                ds = r * (dy - sh * jnp.sum(sh * dy, axis=-1, keepdims=True))
                if gi < DN_HEADS:
                    ds = ds * qscale
            else:
                ds = dy
            dc = ds * sg * (1.0 + c * (1.0 - sg))
            dcs = [dc[3 - j:3 - j + tm] for j in range(4)]
            dx = w_ref[0:1, sl] * dcs[0]
            for j in range(1, 4):
                dx = dx + w_ref[j:j + 1, sl] * dcs[j]
            dx_ref[:, sl] = dx.astype(dx_ref.dtype)
            x0 = x_ref[:, sl]
            dws.append(jnp.concatenate([jnp.sum(dcs[j] * x0, axis=0, keepdims=True) for j in range(4)], axis=0))
        dw = jnp.concatenate(dws, axis=-1)
        ba = ba_ref[...]
        dbgv = dbg_ref[...]
        lane = lax.broadcasted_iota(jnp.int32, ba.shape, 1)
        beta = _sigmoid(ba)
        ea = -jnp.exp(al_ref[...])
        zin = ba + dt_ref[...]
        is_b = lane < DN_HEADS
        is_a = (lane >= DN_HEADS) & (lane < 2 * DN_HEADS)
        da = jnp.where(is_a, dbgv * ea * _sigmoid(zin), 0.0)
        dx_ref[:, _QKV_W:_QKV_W + DN_W] = dz_ref[...]
        dx_ref[:, _QKV_W + DN_W:] = jnp.where(is_b, dbgv * beta * (1.0 - beta), da).astype(dx_ref.dtype)
        ddt = jnp.sum(da, axis=0, keepdims=True)
        dal = jnp.sum(jnp.where(is_a, dbgv * ea * _softplus(zin), 0.0), axis=0, keepdims=True)

        @pl.when(first)
        def _():
            dw_ref[...] = dw
            ddt_ref[...] = ddt
            dal_ref[...] = dal

        @pl.when(i > 0)
        def _():
            dw_ref[...] += dw
            ddt_ref[...] += ddt
            dal_ref[...] += dal

    row = pl.BlockSpec((tm, DN_W), lambda i: (i, 0))
    nrow = pl.BlockSpec((HALO, DN_W), lambda i: (jnp.minimum((i + 1) * hb, t // HALO - 1), 0))
    one = pl.BlockSpec((1, 128), lambda i: (0, 0))
    return pl.pallas_call(
        body, name=name, grid=(nt,),
        in_specs=[pl.BlockSpec((tm, _QKV_W), lambda i: (i, 0)), pl.BlockSpec((tm, _QKV_W), lambda i: (i, 0)),
                  pl.BlockSpec((HALO, _QKV_W), lambda i: (jnp.minimum((i + 1) * hb, t // HALO - 1), 0)),
                  pl.BlockSpec((4, _QKV_W), lambda i: (0, 0)),
                  pl.BlockSpec((tm, 128), lambda i: (i, _BA_COL)), one, one,
                  row, row, row, nrow, nrow, nrow, pl.BlockSpec((tm, 128), lambda i: (i, 0)), row],
        out_specs=(pl.BlockSpec((tm, ODD_IN_PAD), lambda i: (i, 0)), pl.BlockSpec((4, _QKV_W), lambda i: (0, 0)), one, one),
        out_shape=(jax.ShapeDtypeStruct((t, ODD_IN_PAD), MXU_DTYPE), jax.ShapeDtypeStruct((4, _QKV_W), F32),
                   jax.ShapeDtypeStruct((1, 128), F32), jax.ShapeDtypeStruct((1, 128), F32)),
        compiler_params=_cp("arbitrary"))(proj, conv, conv, conv_w, proj, alog_row, dtb_row, dq, dk, dv, dq, dk, dv, dbg,
                                          dz)


def _chunk_masks():
    r = lax.broadcasted_iota(jnp.int32, (DN_CHUNK, DN_CHUNK), 0)
    c = lax.broadcasted_iota(jnp.int32, (DN_CHUNK, DN_CHUNK), 1)
    return r >= c, r > c


INV_PACK = 2


def _inv_unit_lower_many(mats):
    n = DN_CHUNK
    wide = INV_PACK * n
    r = lax.broadcasted_iota(jnp.int32, (wide, wide), 0)
    c = lax.broadcasted_iota(jnp.int32, (wide, wide), 1)
    same = (r & -n) == (c & -n)
    eye = jnp.where((r[:n] == (c[:n] & (n - 1))), 1.0, 0.0)

    def blockdiag(row):
        return jnp.where(same, jnp.concatenate([row] * INV_PACK, axis=0), 0.0)

    packs = [jnp.concatenate(mats[g:g + INV_PACK], axis=-1) for g in range(0, len(mats), INV_PACK)]
    xs = [eye - a for a in packs]
    pws = [_hi(a, blockdiag(a)) for a in packs]
    for step in range(5):
        if step < 4:
            both = [_hi(jnp.concatenate([x, pw], axis=0), blockdiag(pw)) for x, pw in zip(xs, pws)]
            xs = [x + b[:n] for x, b in zip(xs, both)]
            pws = [b[n:] for b in both]
        else:
            xs = [x + _hi(x, blockdiag(pw)) for x, pw in zip(xs, pws)]
    return [x[:, j * n:(j + 1) * n] for x in xs for j in range(INV_PACK)]


def _chunk_common(q, k, beta, gc, gcr, lower):
    gam = jnp.exp(jnp.where(lower, gc - gcr, NEG))
    eg = jnp.exp(gc)
    gl = gc[DN_CHUNK - 1:DN_CHUNK, :]
    kdf = jnp.exp(gl - gc)
    kb = k * beta
    bmat = _mx_nt(kb, k)
    qmat = _mx_nt(q, k)
    return gam, eg, jnp.exp(gl), kdf, kb, bmat, qmat


DN_STEP = 4


def gdn_fwd(q, k, v, bg, proj, o_gain, name):
    t = q.shape[0]
    n_chunks = t // DN_CHUNK

    def body(q_ref, k_ref, v_ref, bg_ref, z_ref, g_ref, o_ref, sall_ref, tall_ref, y_ref, yt_ref, s_ref):
        n = pl.program_id(0)

        @pl.when(n == 0)
        def _():
            s_ref[...] = jnp.zeros_like(s_ref)

        lower, strict = _chunk_masks()
        ltri = jnp.where(lower, 1.0, 0.0)
        hs = range(DN_HEADS)
        sl = [slice(h * DN_DIM, (h + 1) * DN_DIM) for h in hs]
        units = [(c, h) for c in range(DN_STEP) for h in hs]
        nu = range(len(units))
        rs = [slice(c * DN_CHUNK, (c + 1) * DN_CHUNK) for c in range(DN_STEP)]
        bgv = [bg_ref[rs[c], :] for c in range(DN_STEP)]
        gcs = [_hi(ltri, b) for b in bgv]
        gcs_t = [g.T for g in gcs]
        qh = [q_ref[rs[c], sl[h]] for c, h in units]
        kh = [k_ref[rs[c], sl[h]] for c, h in units]
        vh = [v_ref[rs[c], sl[h]] for c, h in units]
        beta = [bgv[c][:, h:h + 1] for c, h in units]
        com = [_chunk_common(qh[u], kh[u], beta[u], gcs[c][:, DN_HEADS + h:DN_HEADS + h + 1],
                             gcs_t[c][DN_HEADS + h:DN_HEADS + h + 1, :], lower) for u, (c, h) in enumerate(units)]
        gam, eg, dec, kdf, kb, bmat, qmat = zip(*com)
        tms = _inv_unit_lower_many([jnp.where(strict, bmat[u] * gam[u], 0.0) for u in nu])
        for u, (c, h) in enumerate(units):
            tall_ref[c, h] = tms[u]
        uw = [_hi(tms[u], jnp.concatenate([vh[u] * beta[u], kb[u] * eg[u]], axis=-1)) for u in nu]
        qd = [qh[u] * eg[u] for u in nu]
        pm = [qmat[u] * gam[u] for u in nu]
        kd = [kh[u] * kdf[u] for u in nu]
        st = [s_ref[h] for h in hs]
        for c in range(DN_STEP):
            us = [c * DN_HEADS + h for h in hs]
            for h in hs:
                sall_ref[c, h] = st[h]
            v_new = [uw[us[h]][:, :DN_DIM] - _mx(uw[us[h]][:, DN_DIM:], st[h]) for h in hs]
            o_st = [_mx(qd[us[h]], st[h]) for h in hs]
            o_in = [_mx(pm[us[h]], v_new[h]) for h in hs]
            s_up = [_mx_tn(kd[us[h]], v_new[h]) for h in hs]
            for h in hs:
                ov = o_st[h] + o_in[h]
                o_ref[rs[c], sl[h]] = ov
                zv = z_ref[rs[c], sl[h]]
                y = ov * lax.rsqrt(jnp.mean(ov * ov, axis=-1, keepdims=True) + EPS) * g_ref[...] * (zv * _sigmoid(zv))
                y_ref[rs[c], sl[h]] = y.astype(y_ref.dtype)
                yt_ref[sl[h], rs[c]] = y.T.astype(yt_ref.dtype)
            st = [st[h] * dec[us[h]] + s_up[h] for h in hs]
        for h in hs:
            s_ref[h] = st[h]

    rows = DN_STEP * DN_CHUNK
    row = pl.BlockSpec((rows, DN_W), lambda n: (n, 0))
    return pl.pallas_call(
        body, name=name, grid=(n_chunks // DN_STEP,),
        in_specs=[row, row, row, pl.BlockSpec((rows, 128), lambda n: (n, 0)),
                  pl.BlockSpec((rows, DN_W), lambda n: (n, _Z_COL)), pl.BlockSpec((1, DN_DIM), lambda n: (0, 0))],
        out_specs=(row, pl.BlockSpec((DN_STEP, DN_HEADS, DN_DIM, DN_DIM), lambda n: (n, 0, 0, 0)),
                   pl.BlockSpec((DN_STEP, DN_HEADS, DN_CHUNK, DN_CHUNK), lambda n: (n, 0, 0, 0)),
                   row, pl.BlockSpec((DN_W, rows), lambda n: (0, n))),
        out_shape=(jax.ShapeDtypeStruct((t, DN_W), F32),
                   jax.ShapeDtypeStruct((n_chunks, DN_HEADS, DN_DIM, DN_DIM), F32),
                   jax.ShapeDtypeStruct((n_chunks, DN_HEADS, DN_CHUNK, DN_CHUNK), F32),
                   jax.ShapeDtypeStruct((t, DN_W), MXU_DTYPE), jax.ShapeDtypeStruct((DN_W, t), MXU_DTYPE)),
        scratch_shapes=[pltpu.VMEM((DN_HEADS, DN_DIM, DN_DIM), F32)],
        compiler_params=_cp("arbitrary"))(q, k, v, bg, proj, o_gain)


def gdn_bwd(q, k, v, bg, sall, tall, do, name):
    t = q.shape[0]
    n_chunks = t // DN_CHUNK

    def body(q_ref, k_ref, v_ref, bg_ref, sall_ref, tall_ref, do_ref, dq_ref, dk_ref, dv_ref, dbg_ref, ds_ref):
        n = pl.program_id(0)

        @pl.when(n == 0)
        def _():
            ds_ref[...] = jnp.zeros_like(ds_ref)

        lower, strict = _chunk_masks()
        ltri = jnp.where(lower, 1.0, 0.0)
        bgv = bg_ref[...]
        gcs = _hi(ltri, bgv)
        gcs_t = gcs.T
        lane = lax.broadcasted_iota(jnp.int32, (DN_CHUNK, 128), 1)
        rowi = lax.broadcasted_iota(jnp.int32, (DN_CHUNK, 1), 0)
        hs = range(DN_HEADS)
        each = lambda fn, *ls: [fn(*a) for a in zip(*ls)]
        rsum = lambda a: jnp.sum(a, axis=-1, keepdims=True)
        sl = [slice(h * DN_DIM, (h + 1) * DN_DIM) for h in hs]
        st = [sall_ref[0, h] for h in hs]
        tms = [tall_ref[0, h] for h in hs]
        dsn = [ds_ref[h] for h in hs]
        qh = [q_ref[:, sl[h]] for h in hs]
        kh = [k_ref[:, sl[h]] for h in hs]
        vh = [v_ref[:, sl[h]] for h in hs]
        doh = [do_ref[:, sl[h]] for h in hs]
        beta = [bgv[:, h:h + 1] for h in hs]
        com = [_chunk_common(qh[h], kh[h], beta[h], gcs[:, DN_HEADS + h:DN_HEADS + h + 1],
                             gcs_t[DN_HEADS + h:DN_HEADS + h + 1, :], lower) for h in hs]
        gam, eg, dec, kdf, kb, bmat, qmat = zip(*com)
        rhs_w = each(lambda a, b: a * b, kb, eg)
        uw = each(lambda t_, v_, b_, r_: _hi(t_, jnp.concatenate([v_ * b_, r_], axis=-1)), tms, vh, beta, rhs_w)
        qd = each(lambda a, b: a * b, qh, eg)
        kd = each(lambda a, b: a * b, kh, kdf)
        pmat = each(lambda a, b: a * b, qmat, gam)
        v_new = each(lambda uw_, s_: uw_[:, :DN_DIM] - _mx(uw_[:, DN_DIM:], s_), uw, st)
        dqd = each(_mx_nt, doh, st)
        ds_o = each(_mx_tn, qd, doh)
        dp = each(lambda d_, v_: jnp.where(lower, _mx_nt(d_, v_), 0.0), doh, v_new)
        dvn_o = each(_mx_tn, pmat, doh)
        ddec = each(lambda d_, s_: jnp.sum(rsum(d_ * s_), axis=0, keepdims=True), dsn, st)
        dkd = each(_mx_nt, v_new, dsn)
        dvn = each(lambda a, k_, d_: a + _mx(k_, d_), dvn_o, kd, dsn)
        dw = each(lambda d_, s_: -_mx_nt(d_, s_), dvn, st)
        ds_w = each(lambda uw_, d_: _mx_tn(uw_[:, DN_DIM:], d_), uw, dvn)
        for h in hs:
            ds_ref[h] = ds_o[h] + dec[h] * dsn[h] - ds_w[h]
        dr = each(lambda t_, a, b: _hi_tn(t_, jnp.concatenate([a, b], axis=-1)), tms, dvn, dw)
        da = each(lambda r_, uw_: jnp.where(strict, -_hi_nt(r_, uw_), 0.0), dr, uw)
        dru = [r_[:, :DN_DIM] for r_ in dr]
        drw = [r_[:, DN_DIM:] for r_ in dr]
        db = each(lambda a, b: a * b, da, gam)
        dq_m = each(lambda a, b: a * b, dp, gam)
        e = each(lambda a, bm, p_, qm, g_: (a * bm + p_ * qm) * g_, da, bmat, dp, qmat, gam)
        dkb = each(lambda b_, k_, r_, e_: _mx(b_, k_) + r_ * e_, db, kh, drw, eg)
        dk = each(lambda b_, kb_, m_, q_, d_, f_: _mx_tn(b_, kb_) + _mx_tn(m_, q_) + d_ * f_, db, kb, dq_m, qh, dkd, kdf)
        dq = each(lambda m_, k_, d_, e_: _mx(m_, k_) + d_ * e_, dq_m, kh, dqd, eg)
        tk = each(lambda a, b: rsum(a * b), dkd, kd)
        dbeta_all = jnp.zeros((DN_CHUNK, 128), F32)
        dgc_all = jnp.zeros((DN_CHUNK, 128), F32)
        for h in hs:
            dgc = (jnp.sum(e[h], axis=1, keepdims=True) - jnp.sum(e[h].T, axis=1, keepdims=True)
                   + rsum(dqd[h] * qd[h]) - tk[h] + rsum(drw[h] * rhs_w[h]))
            dgl = jnp.sum(tk[h], axis=0, keepdims=True) + ddec[h] * dec[h]
            dgc = dgc + jnp.where(rowi == DN_CHUNK - 1, dgl, 0.0)
            dbeta = rsum(dru[h] * vh[h]) + rsum(dkb[h] * kh[h])
            dq_ref[:, sl[h]] = dq[h]
            dk_ref[:, sl[h]] = dk[h] + dkb[h] * beta[h]
            dv_ref[:, sl[h]] = dru[h] * beta[h]
            dbeta_all = jnp.where(lane == h, dbeta, dbeta_all)
            dgc_all = jnp.where(lane == DN_HEADS + h, dgc, dgc_all)
        dbg_ref[...] = dbeta_all + _hi_tn(ltri, dgc_all)

    rev = lambda n: (n_chunks - 1 - n, 0)
    row = pl.BlockSpec((DN_CHUNK, DN_W), rev)
    small = pl.BlockSpec((DN_CHUNK, 128), rev)
    return pl.pallas_call(
        body, name=name, grid=(n_chunks,),
        in_specs=[row, row, row, small,
                  pl.BlockSpec((1, DN_HEADS, DN_DIM, DN_DIM), lambda n: (n_chunks - 1 - n, 0, 0, 0)),
                  pl.BlockSpec((1, DN_HEADS, DN_CHUNK, DN_CHUNK), lambda n: (n_chunks - 1 - n, 0, 0, 0)), row],
        out_specs=(row, row, row, small),
        out_shape=(jax.ShapeDtypeStruct((t, DN_W), F32),) * 3 + (jax.ShapeDtypeStruct((t, 128), F32),),
        scratch_shapes=[pltpu.VMEM((DN_HEADS, DN_DIM, DN_DIM), F32)],
        compiler_params=_cp("arbitrary"))(q, k, v, bg, sall, tall, do)


def gdn_out_bwd(o, proj, o_gain, dx, w_out, name, tm=512, after=None):
    t = o.shape[0]
    tm = min(tm, t)

    def body(o_ref, z_ref, g_ref, dx_ref, w_ref, *rest):
        do_ref, dz_ref, dg_ref = rest[-3:]
        i = pl.program_id(0)
        dy = _mx_nt(dx_ref[...], w_ref[...])
        dg = jnp.zeros((1, DN_DIM), F32)
        for h in range(DN_HEADS):
            sl = slice(h * DN_DIM, (h + 1) * DN_DIM)
            ov, zv, dyv = o_ref[:, sl], z_ref[:, sl], dy[:, sl]
            r = lax.rsqrt(jnp.mean(ov * ov, axis=-1, keepdims=True) + EPS)
            oh = ov * r
            sg = _sigmoid(zv)
            dz_ref[:, sl] = (dyv * oh * g_ref[...] * sg * (1.0 + zv * (1.0 - sg))).astype(dz_ref.dtype)
            don = dyv * (zv * sg)
            dg = dg + jnp.sum(don * oh, axis=0, keepdims=True)
            doh = don * g_ref[...]
            do_ref[:, sl] = r * (doh - oh * jnp.mean(doh * oh, axis=-1, keepdims=True))

        @pl.when(i == 0)
        def _():
            dg_ref[...] = dg

        @pl.when(i > 0)
        def _():
            dg_ref[...] += dg

    row = pl.BlockSpec((tm, DN_W), lambda i: (i, 0))
    one = pl.BlockSpec((1, DN_DIM), lambda i: (0, 0))
    in_specs = [row, pl.BlockSpec((tm, DN_W), lambda i: (i, _Z_COL)), one,
                pl.BlockSpec((tm, dx.shape[1]), lambda i: (i, 0)), pl.BlockSpec(w_out.shape, lambda i: (0, 0))]
    args = [o, proj, o_gain, dx, w_out]
    if after is not None:
        in_specs.append(pl.BlockSpec(memory_space=pl.ANY))
        args.append(after)
    return pl.pallas_call(
        body, name=name, grid=(t // tm,), in_specs=in_specs, out_specs=(row, row, one),
        out_shape=(jax.ShapeDtypeStruct((t, DN_W), F32), jax.ShapeDtypeStruct((t, DN_W), MXU_DTYPE),
                   jax.ShapeDtypeStruct((1, DN_DIM), F32)),
        compiler_params=_cp("arbitrary"))(*args)


def _peer(k):
    x, y, c = lax.axis_index("x"), lax.axis_index("y"), lax.axis_index("c")
    px = 1 - x if k & 4 else x
    py = 1 - y if k & 2 else y
    pc = 1 - c if k & 1 else c
    return (px, py, pc), 4 * px + 2 * py + pc


_HBM = pl.BlockSpec(memory_space=pltpu.HBM)
_SEM = pl.BlockSpec(memory_space=pltpu.SEMAPHORE)
_DATAFLOW = pltpu.SideEffectType.DATAFLOW_SIDE_EFFECTING
N_PEER = N_DEV - 1


def send_start(srcs, name, scatter, after):
    na = len(srcs)
    ns = (2 * N_PEER + 1) * na
    lands = [lax.empty((N_DEV,) + (s.shape[1:] if scatter else s.shape), s.dtype) for s in srcs]
    extra = [] if after is None else [after]

    def body(*refs):
        src_refs, land_refs = refs[:na], refs[na:2 * na]
        sems = refs[2 * na + len(extra):2 * na + len(extra) + ns]
        land_out, token = refs[-1 - na:-1], refs[-1]
        _, me = _peer(0)
        for a in range(na):
            pltpu.make_async_copy(src_refs[a].at[me] if scatter else src_refs[a], land_out[a].at[me],
                                  sems[2 * N_PEER * na + a]).start()
        for k in range(1, N_DEV):
            peer, pid = _peer(k)
            for a in range(na):
                pltpu.make_async_remote_copy(
                    src_ref=src_refs[a].at[pid] if scatter else src_refs[a], dst_ref=land_refs[a].at[me],
                    send_sem=sems[2 * (a * N_PEER + k - 1)], recv_sem=sems[2 * (a * N_PEER + k - 1) + 1],
                    device_id=peer, device_id_type=MESH).start()
        token[...] = jnp.zeros_like(token)

    hbm = lambda arrs: tuple(pltpu.HBM(a.shape, a.dtype) for a in arrs)
    outs = pl.pallas_call(
        body, name=name,
        out_shape=(pltpu.SemaphoreType.DMA(()),) * ns + hbm(srcs) + hbm(lands) + (jax.ShapeDtypeStruct((8, 128), F32),),
        in_specs=[_HBM] * (2 * na) + [pl.BlockSpec(memory_space=pl.ANY)] * len(extra),
        out_specs=(_SEM,) * ns + (_HBM,) * (2 * na) + (pl.BlockSpec(memory_space=pltpu.VMEM),),
        input_output_aliases={i: ns + i for i in range(2 * na)},
        compiler_params=pltpu.CompilerParams(has_side_effects=_DATAFLOW),
    )(*[pltpu.with_memory_space_constraint(a, pltpu.HBM) for a in list(srcs) + lands], *extra)
    return outs[:ns], outs[ns:ns + na], outs[ns + na:ns + 2 * na], outs[-1]


def send_wait(sems, srcs_thru, lands_thru, name, scatter, after):
    na = len(srcs_thru)
    ns = (2 * N_PEER + 1) * na

    def body(*refs):
        src_refs, land_refs, sm = refs[:na], refs[na:2 * na], refs[2 * na:2 * na + ns]
        _, me = _peer(0)
        for a in range(na):
            pltpu.make_async_copy(src_refs[a].at[me] if scatter else src_refs[a], land_refs[a].at[me],
                                  sm[2 * N_PEER * na + a]).wait()
        for k in range(1, N_DEV):
            peer, pid = _peer(k)
            for a in range(na):
                cp = pltpu.make_async_remote_copy(
                    src_ref=src_refs[a].at[pid] if scatter else src_refs[a], dst_ref=land_refs[a].at[pid],
                    send_sem=sm[2 * (a * N_PEER + k - 1)], recv_sem=sm[2 * (a * N_PEER + k - 1) + 1],
                    device_id=peer, device_id_type=MESH)
                cp.wait_send()
                cp.wait_recv()

    hbm = lambda arrs: tuple(pltpu.HBM(a.shape, a.dtype) for a in arrs)
    outs = pl.pallas_call(
        body, name=name, out_shape=hbm(srcs_thru) + hbm(lands_thru),
        in_specs=[_HBM] * (2 * na) + [_SEM] * ns + [pl.BlockSpec(memory_space=pl.ANY)], out_specs=(_HBM,) * (2 * na),
        input_output_aliases={i: i for i in range(2 * na)},
        compiler_params=pltpu.CompilerParams(has_side_effects=_DATAFLOW),
    )(*srcs_thru, *lands_thru, *sems, after)
    return outs[na:]


def _adamw(w, g, m, v):
    m = ADAM_B1 * m + (1.0 - ADAM_B1) * g
    v = ADAM_B2 * v + (1.0 - ADAM_B2) * (g * g)
    m_hat = m / (1.0 - ADAM_B1 ** ADAM_STEP)
    v_hat = v / (1.0 - ADAM_B2 ** ADAM_STEP)
    return -ADAM_LR * (m_hat / (jnp.sqrt(v_hat) + ADAM_EPS) + ADAM_WD * w), m, v


def adam_sum(w, pieces, m, v, name, layer=0, into=None):
    nl, r, c = w.shape
    tr = r
    for cand in (256, 128, 64, 32, 16, 8):
        if r % cand == 0:
            tr = cand
            break

    def body(w_ref, p_ref, m_ref, v_ref, *rest):
        g_ref, d_ref, nm_ref, nv_ref = rest[-4:]
        g = p_ref[0].astype(F32)
        for s in range(1, N_DEV):
            g = g + p_ref[s].astype(F32)
        g_ref[0] = g
        d_ref[0], nm_ref[0], nv_ref[0] = _adamw(w_ref[0], g, m_ref[0], v_ref[0])

    row = pl.BlockSpec((1, tr, c), lambda i: (layer, i, 0))
    out = jax.ShapeDtypeStruct((nl, r, c), F32)
    extra = [] if into is None else list(into)
    return pl.pallas_call(
        body, name=name, grid=(r // tr,),
        in_specs=[row, pl.BlockSpec((N_DEV, tr, c), lambda i: (0, i, 0)), row, row]
        + [pl.BlockSpec(memory_space=pl.ANY)] * len(extra),
        out_specs=(row,) * 4, out_shape=(out,) * 4,
        input_output_aliases={4 + i: i for i in range(len(extra))},
        compiler_params=_cp("parallel"))(w, pieces, m, v, *extra)


def sum_rows(gathered, name):
    _, r, c = gathered.shape

    def body(p_ref, o_ref):
        g = p_ref[0]
        for s in range(1, N_DEV):
            g = g + p_ref[s]
        o_ref[...] = g

    return pl.pallas_call(body, name=name, out_shape=jax.ShapeDtypeStruct((r, c), F32))(gathered)


def adam_small(w, g, m, v, name):
    def body(w_ref, g_ref, m_ref, v_ref, d_ref, nm_ref, nv_ref):
        d_ref[...], nm_ref[...], nv_ref[...] = _adamw(w_ref[...], g_ref[...], m_ref[...], v_ref[...])

    out = jax.ShapeDtypeStruct(w.shape, F32)
    return pl.pallas_call(body, name=name, out_shape=(out,) * 3)(w, g, m, v)


def _rope_tables(t):
    inv_freq = 10000.0 ** (-jnp.arange(0, HEAD_DIM, 2, dtype=F32) / HEAD_DIM)
    lanes = lambda a: jnp.concatenate([a] * (128 // a.shape[-1]), axis=-1)
    base = (jnp.arange(t // QK_TM, dtype=F32) * QK_TM)[:, None] * inv_freq[None, :]
    offs = jnp.arange(QK_TM, dtype=F32)[:, None] * inv_freq[None, :]
    return (lanes(jnp.cos(base))[:, None, :], lanes(jnp.sin(base))[:, None, :], lanes(jnp.cos(offs)), lanes(jnp.sin(offs)))


def _lane_row(vec8):
    return jnp.pad(vec8.reshape(1, DN_HEADS), ((0, 0), (DN_HEADS, 128 - 2 * DN_HEADS)))


def _ffn_bwd(x, norm_g, w_gu, w_d, saved, dy, tag, after=None):
    ft, gu, at = saved
    dgu = ffn_dact(dy, w_d, gu, f"{tag}_d_gate_up", after=after)
    dwd = mm_at(at, dy, f"{tag}_dw_down")
    nj = D_FF // GU_TILE
    dwgu = mm_at(ft, dgu, f"{tag}_dw_gate_up", transposed=True, tn=GU_TILE, row_block=lambda q: (q % 2) * nj + q // 2)
    dx, dg = mm_rms_bwd(dgu, w_gu, x, norm_g, dy, f"{tag}_d_norm", chunks=_GU_CHUNKS)
    return dx, dwgu, dwd, dg


def local_step(x, target, small, weights_of, grads_out, after=None):
    t = x.shape[0]
    rope = _rope_tables(t)
    alog_row, dtb_row = _lane_row(small["odd_a_log"]), _lane_row(small["odd_dt_bias"])

    h0, h0t = rms_fwd(x, small["even_norm"], "even_norm", after=after)
    we = weights_of("even", h0)
    small = {**small, **we.get("small", {})}
    proj0 = mm_nt(h0, we["w_in"], "even_in_proj")
    qr, kr = qk_prep_fwd(proj0, small["even_q_gain"], small["even_k_gain"], rope, "even_qk_prep")
    y_attn, mix0, mix0_t, lse = swa_fwd(qr, kr, proj0, small["even_sinks"], "even_swa")
    mix0, mix0_t = gconv_fwd(proj0, small["even_conv_w"], mix0, mix0_t, "even_gconv")
    we = {**we, **weights_of("even_out", mix0)}
    x1, f0, f0t = mm_nn_res_norm(mix0, we["w_out"], x, small["ffn_norm0"], "even_out_proj")
    w0 = weights_of("ffn0", x1)
    gu0, a0, a0t = ffn_up(f0, w0["gate_up"], "ffn0_gate_up")
    ffn0 = (f0t, gu0, a0t)
    x2, h1, h1t = mm_nn_res_norm(a0, w0["down"], x1, small["odd_norm"], "ffn0_down")

    wo = weights_of("odd", x2)
    proj1 = mm_nt(h1, wo["w_in"], "odd_in_proj")
    qn, kn, vs, bg, conv1 = gdn_prep_fwd(proj1, small["odd_conv_w"], alog_row, dtb_row, "odd_prep")
    o, sall, tall, og, ogt = gdn_fwd(qn, kn, vs, bg, proj1, small["odd_o_gain"], "odd_delta_rule")
    x3, f1, f1t = mm_nn_res_norm(og, wo["w_out"], x2, small["ffn_norm1"], "odd_out_proj")
    w1 = weights_of("ffn1", x3)
    gu1, a1, a1t = ffn_up(f1, w1["gate_up"], "ffn1_gate_up")
    ffn1 = (f1t, gu1, a1t)
    dy, loss_row = mm_nn_res_loss(a1, w1["down"], x3, target, "ffn1_down_loss")

    gs = {}
    dx3, dwgu, dwd, gs["ffn_norm1"] = _ffn_bwd(x3, small["ffn_norm1"], w1["gate_up"], w1["down"], ffn1, dy, "ffn1")
    tok = grads_out("ffn1", {"gate_up": dwgu, "down": dwd})

    do, dz, gs["odd_o_gain"] = gdn_out_bwd(o, proj1, small["odd_o_gain"], dx3, wo["w_out"], "odd_d_gate_norm", after=tok)
    dwo = mm_at(ogt, dx3, "odd_dw_out")
    dqn, dkn, dvs, dbg = gdn_bwd(qn, kn, vs, bg, sall, tall, do, "odd_d_delta_rule")
    dproj1, gs["odd_conv_w"], ddt_row, dal_row = gdn_prep_bwd(
        proj1, conv1, small["odd_conv_w"], alog_row, dtb_row, dqn, dkn, dvs, dbg, dz, "odd_d_prep")
    gs["odd_dt_bias"] = ddt_row[:, DN_HEADS:2 * DN_HEADS]
    gs["odd_a_log"] = dal_row[:, DN_HEADS:2 * DN_HEADS]
    dwi = mm_at(h1t, dproj1, "odd_dw_in", transposed=True)
    dx2, gs["odd_norm"] = mm_rms_bwd(dproj1, wo["w_in"], x2, small["odd_norm"], dx3, "odd_d_norm")
    tok = grads_out("odd", {"w_in": dwi, "w_out": dwo})

    dx1, dwgu, dwd, gs["ffn_norm0"] = _ffn_bwd(x1, small["ffn_norm0"], w0["gate_up"], w0["down"], ffn0, dx2, "ffn0",
                                               after=tok)
    tok = grads_out("ffn0", {"gate_up": dwgu, "down": dwd})

    dmix = mm_nt(dx1, we["w_out"], "even_d_mix", after=tok)
    dwo = mm_at(mix0_t, dx1, "even_dw_out")
    dproj0 = lax.empty((t, EVEN_IN_W), MXU_DTYPE)
    dqr, dkr, dproj0, gs["even_sinks"] = swa_bwd(
        qr, kr, proj0, small["even_sinks"], y_attn, lse, dmix, dproj0, "even_d_swa")
    dproj0, gs["even_q_gain"], gs["even_k_gain"] = qk_prep_bwd(
        proj0, small["even_q_gain"], small["even_k_gain"], rope, dqr, dkr, dproj0, "even_d_qk_prep")
    dproj0, gs["even_conv_w"] = gconv_bwd(proj0, small["even_conv_w"], dmix, dproj0, "even_d_gconv")
    dwi = mm_at(h0t, dproj0, "even_dw_in", transposed=True, chunks=_EVEN_D_CHUNKS)
    tok = grads_out("even", {"w_in": dwi, "w_out": dwo})
    grad_x, gs["even_norm"] = mm_rms_bwd(dproj0, we["w_in"], x, small["even_norm"], dx1, "even_d_norm", after=tok,
                                         chunks=_EVEN_D_CHUNKS)
    return loss_row, grad_x, gs


_SMALL_ORDER = ("even_norm", "even_q_gain", "even_k_gain", "even_sinks", "odd_a_log", "odd_dt_bias", "odd_o_gain",
                "ffn_norm0", "ffn_norm1", "odd_norm", "even_conv_w", "odd_conv_w")
_SMALL_SIZE = {"even_norm": 1024, "even_q_gain": 64, "even_k_gain": 64, "even_sinks": 8, "odd_a_log": 8,
               "odd_dt_bias": 8, "odd_o_gain": 128, "ffn_norm0": 1024, "ffn_norm1": 1024, "odd_norm": 1024,
               "even_conv_w": 3 * 512, "odd_conv_w": 4 * 3072}
_N_REPL = 9


def _pack_rows(vals):
    flat = jnp.concatenate([v.reshape(-1) for v in vals])
    pad = (-flat.shape[0]) % 1024
    return jnp.pad(flat, (0, pad)).reshape(-1, 128)


def _my_block(full, size, axis):
    me = 4 * lax.axis_index("x") + 2 * lax.axis_index("y") + lax.axis_index("c")
    return lax.dynamic_slice_in_dim(full, me * size, size, axis=axis)


def kernel(x, even_norm, even_w_in, even_q_gain, even_k_gain, even_sinks, even_conv_w, even_w_out, odd_norm, odd_w_in, odd_conv_w, odd_a_log, odd_dt_bias, odd_o_gain, odd_w_out, ffn_norm, ffn_w_gate_up, ffn_w_down, loss_target, m_even_norm, m_even_w_in, m_even_q_gain, m_even_k_gain, m_even_sinks, m_even_conv_w, m_even_w_out, m_odd_norm, m_odd_w_in, m_odd_conv_w, m_odd_a_log, m_odd_dt_bias, m_odd_o_gain, m_odd_w_out, m_ffn_norm, m_ffn_w_gate_up, m_ffn_w_down, v_even_norm, v_even_w_in, v_even_q_gain, v_even_k_gain, v_even_sinks, v_even_conv_w, v_even_w_out, v_odd_norm, v_odd_w_in, v_odd_conv_w, v_odd_a_log, v_odd_dt_bias, v_odd_o_gain, v_odd_w_out, v_ffn_norm, v_ffn_w_gate_up, v_ffn_w_down):
    t = x.shape[1]
    d = D_MODEL

    tr = lambda a: jnp.swapaxes(a, 1, 2)
    shard = {
        "even": {"w_in": tr(even_w_in)[0], "w_out": even_w_out[0]},
        "ffn0": {"gate_up": tr(ffn_w_gate_up)[0], "down": ffn_w_down[0]},
        "odd": {"w_in": tr(odd_w_in)[0], "w_out": odd_w_out[0]},
        "ffn1": {"gate_up": tr(ffn_w_gate_up)[1], "down": ffn_w_down[1]},
    }
    given = {
        ("even", "w_in"): ("even_w_in", even_w_in, m_even_w_in, v_even_w_in, 0),
        ("even", "w_out"): ("even_w_out", even_w_out, m_even_w_out, v_even_w_out, 0),
        ("odd", "w_in"): ("odd_w_in", odd_w_in, m_odd_w_in, v_odd_w_in, 0),
        ("odd", "w_out"): ("odd_w_out", odd_w_out, m_odd_w_out, v_odd_w_out, 0),
        ("ffn0", "gate_up"): ("ffn_w_gate_up", ffn_w_gate_up, m_ffn_w_gate_up, v_ffn_w_gate_up, 0),
        ("ffn1", "gate_up"): ("ffn_w_gate_up", ffn_w_gate_up, m_ffn_w_gate_up, v_ffn_w_gate_up, 1),
        ("ffn0", "down"): ("ffn_w_down", ffn_w_down, m_ffn_w_down, v_ffn_w_down, 0),
        ("ffn1", "down"): ("ffn_w_down", ffn_w_down, m_ffn_w_down, v_ffn_w_down, 1),
    }

    def whole(group, parts):
        col, row = tuple(shard[group])
        w_col = parts[0].reshape(-1, d)
        if group == "odd":
            w_col = jnp.pad(w_col, ((0, ODD_IN_PAD - ODD_IN_W), (0, 0)))
        return {col: w_col, row: parts[1].reshape(-1, d)}

    wire = {g: [a.astype(MXU_DTYPE) for a in shard[g].values()] for g in shard}
    wire["even_out"] = [wire["even"].pop()]
    wire["even"].append(_pack_rows([odd_norm, even_conv_w, odd_conv_w]))
    gathers, tok = {}, None
    for g in ("even", "even_out", "ffn0", "odd", "ffn1"):
        sems, srcs_thru, lands_thru, tok = send_start(wire[g], f"gather_{g}_start", False, tok)
        gathers[g] = (sems, srcs_thru, lands_thru)
    o1 = d // N_DEV
    o2 = o1 + 3 * CONV_CH // N_DEV

    def weights_of(group, after):
        lands = send_wait(*gathers[group], f"gather_{group}_wait", False, after)
        if group == "even_out":
            return {"w_out": lands[0].reshape(-1, d)}
        if group != "even":
            return whole(group, lands)
        sg = lands[1].reshape(N_DEV, -1)
        return {"w_in": lands[0].reshape(-1, d), "small": {
            "odd_norm": sg[:, :o1].reshape(1, d),
            "even_conv_w": sg[:, o1:o2].reshape(N_DEV, 3, CONV_CH // N_DEV).transpose(1, 0, 2).reshape(3, CONV_CH),
            "odd_conv_w": sg[:, o2:o2 + 4 * _QKV_W // N_DEV].reshape(N_DEV, 4, _QKV_W // N_DEV)
            .transpose(1, 0, 2).reshape(4, _QKV_W)}}

    sent = {}

    def grads_out(group, dws):
        col, row = tuple(shard[group])
        c = shard[group][col].shape[0]
        pieces = [jnp.stack([dws[col][j * c:(j + 1) * c] for j in range(N_DEV)]) if group == "odd"
                  else dws[col].reshape((N_DEV,) + shard[group][col].shape),
                  dws[row].reshape((N_DEV,) + shard[group][row].shape)]
        sems, srcs_thru, lands_thru, token = send_start(pieces, f"exchange_{group}_start", True, None)
        sent[group] = (sems, srcs_thru, lands_thru, pieces)
        return token

    small = {
        "even_norm": even_norm, "even_q_gain": even_q_gain, "even_k_gain": even_k_gain, "even_sinks": even_sinks,
        "odd_a_log": odd_a_log.reshape(-1), "odd_dt_bias": odd_dt_bias.reshape(-1), "odd_o_gain": odd_o_gain,
        "ffn_norm0": ffn_norm[0:1], "ffn_norm1": ffn_norm[1:2],
    }

    loss_row, grad_x, gs = local_step(x.reshape(t, d), loss_target.reshape(t, d), small, weights_of, grads_out, after=tok)

    rows = _pack_rows([gs[n] for n in _SMALL_ORDER] + [loss_row[:, 0:1]])
    small_sent = send_start([rows], "gather_small_grads_start", False, None)

    res, behind = {}, small_sent[3]
    for g in ("ffn1", "odd", "ffn0", "even"):
        sems, srcs_thru, lands_thru, pieces = sent[g]
        lands = send_wait(sems, srcs_thru, lands_thru, f"exchange_{g}_wait", True, behind)
        for i, (key, pcs) in enumerate(zip(shard[g], lands)):
            name, w_, m_, v_, layer = given[g, key]
            view = tr if i == 0 else (lambda a: a)
            res[name] = adam_sum(view(w_), pcs, view(m_), view(v_), f"adamw_{g}_{key}", layer=layer, into=res.get(name))
        behind = res[name][0]
    for name in ("even_w_in", "odd_w_in", "ffn_w_gate_up"):
        res[name] = tuple(tr(a) for a in res[name])

    (rows_g,) = send_wait(*small_sent[:3], "gather_small_grads_wait", False, behind)
    tot = sum_rows(rows_g, "sum_small_grads").reshape(-1)
    off, sgrad = 0, {}
    for n in _SMALL_ORDER:
        sgrad[n] = tot[off:off + _SMALL_SIZE[n]]
        off += _SMALL_SIZE[n]
    loss = tot[off]

    repl = _SMALL_ORDER[:_N_REPL]
    repl_w = {"even_norm": even_norm, "even_q_gain": even_q_gain, "even_k_gain": even_k_gain, "even_sinks": even_sinks,
              "odd_a_log": odd_a_log, "odd_dt_bias": odd_dt_bias, "odd_o_gain": odd_o_gain,
              "ffn_norm0": ffn_norm[0], "ffn_norm1": ffn_norm[1]}
    repl_m = {"even_norm": m_even_norm, "even_q_gain": m_even_q_gain, "even_k_gain": m_even_k_gain,
              "even_sinks": m_even_sinks, "odd_a_log": m_odd_a_log, "odd_dt_bias": m_odd_dt_bias,
              "odd_o_gain": m_odd_o_gain, "ffn_norm0": m_ffn_norm[0], "ffn_norm1": m_ffn_norm[1]}
    repl_v = {"even_norm": v_even_norm, "even_q_gain": v_even_q_gain, "even_k_gain": v_even_k_gain,
              "even_sinks": v_even_sinks, "odd_a_log": v_odd_a_log, "odd_dt_bias": v_odd_dt_bias,
              "odd_o_gain": v_odd_o_gain, "ffn_norm0": v_ffn_norm[0], "ffn_norm1": v_ffn_norm[1]}
    pk = lambda dct: _pack_rows([dct[n] for n in repl])
    pd_, pm_, pv_ = adam_small(pk(repl_w), pk(sgrad), pk(repl_m), pk(repl_v), "adamw_replicated")
    sres = {}
    off = 0
    for n in repl:
        sz = _SMALL_SIZE[n]
        sres[n] = (sgrad[n], pd_.reshape(-1)[off:off + sz], pm_.reshape(-1)[off:off + sz], pv_.reshape(-1)[off:off + sz])
        off += sz
    g_on = _my_block(sgrad["odd_norm"].reshape(1, d), d // N_DEV, 1)
    g_ec = _my_block(sgrad["even_conv_w"].reshape(3, CONV_CH), CONV_CH // N_DEV, 1)
    g_oc = _my_block(sgrad["odd_conv_w"].reshape(4, _QKV_W), _QKV_W // N_DEV, 1)
    shard_w = _pack_rows([odd_norm, even_conv_w, odd_conv_w])
    sd_, sm_, sv_ = adam_small(shard_w, _pack_rows([g_on, g_ec, g_oc]),
                               _pack_rows([m_odd_norm, m_even_conv_w, m_odd_conv_w]),
                               _pack_rows([v_odd_norm, v_even_conv_w, v_odd_conv_w]), "adamw_sharded_small")
    off = 0
    for n, gfull, like in (("odd_norm", g_on, odd_norm), ("even_conv_w", g_ec, even_conv_w), ("odd_conv_w", g_oc, odd_conv_w)):
        sz = like.size
        sres[n] = (gfull, sd_.reshape(-1)[off:off + sz], sm_.reshape(-1)[off:off + sz], sv_.reshape(-1)[off:off + sz])
        off += sz

    def small_out(name, like, kind):
        if name == "ffn_norm":
            return jnp.stack([sres["ffn_norm0"][kind], sres["ffn_norm1"][kind]]).reshape(like.shape)
        return sres[name][kind].reshape(like.shape)

    order = (("even_norm", even_norm), ("even_w_in", even_w_in), ("even_q_gain", even_q_gain),
             ("even_k_gain", even_k_gain), ("even_sinks", even_sinks), ("even_conv_w", even_conv_w),
             ("even_w_out", even_w_out), ("odd_norm", odd_norm), ("odd_w_in", odd_w_in), ("odd_conv_w", odd_conv_w),
             ("odd_a_log", odd_a_log), ("odd_dt_bias", odd_dt_bias), ("odd_o_gain", odd_o_gain),
             ("odd_w_out", odd_w_out), ("ffn_norm", ffn_norm), ("ffn_w_gate_up", ffn_w_gate_up),
             ("ffn_w_down", ffn_w_down))
    outs = [loss, grad_x.reshape(x.shape)]
    for kind in range(4):
        for name, like in order:
            outs.append(res[name][kind] if name in res else small_out(name, like, kind))
    return tuple(outs)
```

```python
import jax
import jax.numpy as jnp
import numpy as np
from jax import lax
from jax.experimental import pallas as pl
from jax.experimental.pallas import tpu as pltpu

F32 = jnp.float32
MXU_DTYPE = jnp.bfloat16
HI = lax.Precision.HIGH
EPS = 1e-6
N_DEV = 8
D_MODEL = 1024
HEAD_DIM = 64
ATTN_HEADS = 8
KV_HEADS = 2
ATTN_BLOCK = 128
Q_W = 512
KV_W = 128
CONV_CH = 512
EVEN_IN_W = 2304
DN_HEADS = 8
DN_DIM = 128
DN_W = 1024
DN_CHUNK = 64
ODD_IN_W = 4112
ODD_IN_PAD = 4224
D_FF = 2816
NEG = -1e30
VMEM_LIMIT = 56 * 1024 * 1024
ADAM_LR, ADAM_B1, ADAM_B2, ADAM_EPS, ADAM_WD, ADAM_STEP = 0.001, 0.9, 0.999, 1e-08, 0.01, 10
MESH = pl.DeviceIdType.MESH


def _cp(*sem):
    return pltpu.CompilerParams(dimension_semantics=sem, vmem_limit_bytes=VMEM_LIMIT)


def _pick(n, cap):
    best = 128
    for t in range(128, cap + 1, 128):
        if n % t == 0:
            best = t
    return best


def _mx(a, b):
    return jnp.dot(a.astype(MXU_DTYPE), b.astype(MXU_DTYPE), preferred_element_type=F32)


def _mx_nt(a, b):
    return lax.dot_general(a.astype(MXU_DTYPE), b.astype(MXU_DTYPE), (((1,), (1,)), ((), ())),
                           preferred_element_type=F32)


def _mx_tn(a, b):
    return lax.dot_general(a.astype(MXU_DTYPE), b.astype(MXU_DTYPE), (((0,), (0,)), ((), ())),
                           preferred_element_type=F32)


def _hi(a, b):
    return jnp.dot(a, b, precision=HI, preferred_element_type=F32)


def _hi_nt(a, b):
    return lax.dot_general(a, b, (((1,), (1,)), ((), ())), precision=HI, preferred_element_type=F32)


def _hi_tn(a, b):
    return lax.dot_general(a, b, (((0,), (0,)), ((), ())), precision=HI, preferred_element_type=F32)


def _sigmoid(x):
    return 0.5 * jnp.tanh(0.5 * x) + 0.5


def _softplus(x):
    return jnp.maximum(x, 0.0) + jnp.log(1.0 + jnp.exp(-jnp.abs(x)))


def mm_nn_res_norm(a, b, res, g, name, tm=512):
    t, k = a.shape
    d = b.shape[1]
    tm = min(tm, t)

    def body(a_ref, b_ref, res_ref, g_ref, y_ref, h_ref, ht_ref):
        y = res_ref[...] + _mx(a_ref[...], b_ref[...])
        y_ref[...] = y
        h = y * lax.rsqrt(jnp.mean(y * y, axis=-1, keepdims=True) + EPS) * g_ref[...]
        h_ref[...] = h.astype(h_ref.dtype)
        ht_ref[...] = h.T.astype(ht_ref.dtype)

    row = pl.BlockSpec((tm, d), lambda i: (i, 0))
    return pl.pallas_call(
        body, name=name, grid=(t // tm,),
        in_specs=[pl.BlockSpec((tm, k), lambda i: (i, 0)), pl.BlockSpec((k, d), lambda i: (0, 0)), row,
                  pl.BlockSpec((1, d), lambda i: (0, 0))],
        out_specs=(row, row, pl.BlockSpec((d, tm), lambda i: (0, i))),
        out_shape=(jax.ShapeDtypeStruct((t, d), F32), jax.ShapeDtypeStruct((t, d), MXU_DTYPE),
                   jax.ShapeDtypeStruct((d, t), MXU_DTYPE)),
        compiler_params=_cp("parallel"))(a, b, res, g)


def mm_nt(a, b, name, out_dtype=F32, tm=2048, after=None):
    m, k = a.shape
    n, _ = b.shape
    tn = _pick(n, 512 if k > 3000 else 1536)
    tm = min(tm, m)

    def body(a_ref, b_ref, *rest):
        o_ref = rest[-1]
        o_ref[...] = _mx_nt(a_ref[...], b_ref[...]).astype(o_ref.dtype)

    in_specs = [pl.BlockSpec((tm, k), lambda j, i: (i, 0)), pl.BlockSpec((tn, k), lambda j, i: (j, 0))]
    args = [a, b]
    if after is not None:
        in_specs.append(pl.BlockSpec(memory_space=pl.ANY))
        args.append(after)
    return pl.pallas_call(
        body, name=name, grid=(n // tn, m // tm), in_specs=in_specs,
        out_specs=pl.BlockSpec((tm, tn), lambda j, i: (i, j)),
        out_shape=jax.ShapeDtypeStruct((m, n), out_dtype), compiler_params=_cp("parallel", "parallel"))(*args)


def mm_at(at, b, name, tk=2048, transposed=False, tn=None, row_block=None, chunks=None):
    m, kk = at.shape
    _, n = b.shape
    tm, tn, tk = _pick(m, 1408), tn or _pick(n, 2816), min(tk, kk)
    nk = kk // tk
    assert chunks is None or (transposed and tn == n)

    def body(a_ref, b_ref, o_ref, acc_ref):
        k = pl.program_id(2)
        p = _mx(a_ref[...], b_ref[...])
        acc = jnp.where(k == 0, p, acc_ref[...] + p)
        acc_ref[...] = acc

        @pl.when(k == nk - 1)
        def _():
            res = (acc.T if transposed else acc).astype(o_ref.dtype)
            if chunks is None:
                o_ref[...] = res
            else:
                for cb, co, size in chunks:
                    o_ref[co:co + size, :] = res[cb:cb + size]

    if transposed:
        rb = row_block or (lambda j: j)
        out_spec = pl.BlockSpec((tn, tm), lambda i, j, k: (rb(j), i))
        out_shape = jax.ShapeDtypeStruct((n, m), MXU_DTYPE)
    else:
        out_spec = pl.BlockSpec((tm, tn), lambda i, j, k: (i, j))
        out_shape = jax.ShapeDtypeStruct((m, n), MXU_DTYPE)
    return pl.pallas_call(
        body, name=name, grid=(m // tm, n // tn, nk),
        in_specs=[pl.BlockSpec((tm, tk), lambda i, j, k: (i, k)), pl.BlockSpec((tk, tn), lambda i, j, k: (k, j))],
        out_specs=out_spec, out_shape=out_shape, scratch_shapes=[pltpu.VMEM((tm, tn), F32)],
        compiler_params=_cp("parallel", "parallel", "arbitrary"))(at, b)


def rms_fwd(x, g, name, tm=512, after=None):
    t, d = x.shape

    def body(x_ref, g_ref, *rest):
        o_ref, ot_ref = rest[-2:]
        xv = x_ref[...]
        r = lax.rsqrt(jnp.mean(xv * xv, axis=-1, keepdims=True) + EPS)
        h = xv * r * g_ref[...]
        o_ref[...] = h.astype(o_ref.dtype)
        ot_ref[...] = h.T.astype(ot_ref.dtype)

    in_specs = [pl.BlockSpec((tm, d), lambda i: (i, 0)), pl.BlockSpec((1, d), lambda i: (0, 0))]
    args = [x, g]
    if after is not None:
        in_specs.append(pl.BlockSpec(memory_space=pl.ANY))
        args.append(after)
    return pl.pallas_call(
        body, name=name, grid=(t // tm,), in_specs=in_specs,
        out_specs=(pl.BlockSpec((tm, d), lambda i: (i, 0)), pl.BlockSpec((d, tm), lambda i: (0, i))),
        out_shape=(jax.ShapeDtypeStruct((t, d), MXU_DTYPE), jax.ShapeDtypeStruct((d, t), MXU_DTYPE)),
        compiler_params=_cp("parallel"))(*args)


def mm_rms_bwd(a, bt, x, g, dres, name, tm=512, after=None, chunks=None):
    t, k = a.shape
    d = bt.shape[1]
    tm = min(tm if k > 3000 else 2 * tm, t)
    chunks = chunks or ((0, 0, k),)

    def body(a_ref, b_ref, x_ref, g_ref, dres_ref, *rest):
        dx_ref, dg_ref = rest[-2:]
        dhv = None
        for ca, cb, size in chunks:
            part = _mx(a_ref[:, ca:ca + size], b_ref[cb:cb + size, :])
            dhv = part if dhv is None else dhv + part
        xv = x_ref[...]
        r = lax.rsqrt(jnp.mean(xv * xv, axis=-1, keepdims=True) + EPS)
        xh = xv * r
        dxh = dhv * g_ref[...]
        dx_ref[...] = dres_ref[...] + r * (dxh - xh * jnp.mean(dxh * xh, axis=-1, keepdims=True))
        part = jnp.sum(dhv * xh, axis=0, keepdims=True)
        dg_ref[...] = jnp.where(pl.program_id(0) == 0, part, dg_ref[...] + part)

    row = pl.BlockSpec((tm, d), lambda i: (i, 0))
    one = pl.BlockSpec((1, d), lambda i: (0, 0))
    in_specs = [pl.BlockSpec((tm, k), lambda i: (i, 0)), pl.BlockSpec((k, d), lambda i: (0, 0)), row, one, row]
    args = [a, bt, x, g, dres]
    if after is not None:
        in_specs.append(pl.BlockSpec(memory_space=pl.ANY))
        args.append(after)
    return pl.pallas_call(
        body, name=name, grid=(t // tm,), in_specs=in_specs, out_specs=(row, one),
        out_shape=(jax.ShapeDtypeStruct((t, d), F32), jax.ShapeDtypeStruct((1, d), F32)),
        compiler_params=_cp("arbitrary"))(*args)


GU_TILE = 1408


def ffn_up(f, wt, name, tm=1024):
    t, d = f.shape
    tm = min(tm, t)
    nj = D_FF // GU_TILE

    def body(f_ref, wg_ref, wu_ref, gu_ref, a_ref, at_ref):
        g = _mx_nt(f_ref[...], wg_ref[...])
        u = _mx_nt(f_ref[...], wu_ref[...])
        sg = _sigmoid(g)
        gs = g * sg
        gu_ref[:, :GU_TILE] = (u * (sg + gs - gs * sg)).astype(gu_ref.dtype)
        gu_ref[:, GU_TILE:] = gs.astype(gu_ref.dtype)
        act = gs * u
        a_ref[...] = act.astype(a_ref.dtype)
        at_ref[...] = act.T.astype(at_ref.dtype)

    return pl.pallas_call(
        body, name=name, grid=(nj, t // tm),
        in_specs=[pl.BlockSpec((tm, d), lambda j, i: (i, 0)), pl.BlockSpec((GU_TILE, d), lambda j, i: (j, 0)),
                  pl.BlockSpec((GU_TILE, d), lambda j, i: (nj + j, 0))],
        out_specs=(pl.BlockSpec((tm, 2 * GU_TILE), lambda j, i: (i, j)), pl.BlockSpec((tm, GU_TILE), lambda j, i: (i, j)),
                   pl.BlockSpec((GU_TILE, tm), lambda j, i: (j, i))),
        out_shape=(jax.ShapeDtypeStruct((t, 2 * D_FF), MXU_DTYPE), jax.ShapeDtypeStruct((t, D_FF), MXU_DTYPE),
                   jax.ShapeDtypeStruct((D_FF, t), MXU_DTYPE)),
        compiler_params=_cp("parallel", "parallel"))(f, wt, wt)


_GU_CHUNKS = tuple((q * GU_TILE, ((q % 2) * (D_FF // GU_TILE) + q // 2) * GU_TILE, GU_TILE)
                   for q in range(2 * D_FF // GU_TILE))


def ffn_dact(dy, w_d, gu, name, tm=1024, after=None):
    t, d = dy.shape
    tm = min(tm, t)

    def body(dy_ref, w_ref, gu_ref, *rest):
        o_ref = rest[-1]
        da = _mx_nt(dy_ref[...], w_ref[...])
        o_ref[:, :GU_TILE] = (da * gu_ref[:, :GU_TILE]).astype(o_ref.dtype)
        o_ref[:, GU_TILE:] = (da * gu_ref[:, GU_TILE:]).astype(o_ref.dtype)

    in_specs = [pl.BlockSpec((tm, d), lambda j, i: (i, 0)), pl.BlockSpec((GU_TILE, d), lambda j, i: (j, 0)),
                pl.BlockSpec((tm, 2 * GU_TILE), lambda j, i: (i, j))]
    args = [dy, w_d, gu]
    if after is not None:
        in_specs.append(pl.BlockSpec(memory_space=pl.ANY))
        args.append(after)
    return pl.pallas_call(
        body, name=name, grid=(D_FF // GU_TILE, t // tm), in_specs=in_specs,
        out_specs=pl.BlockSpec((tm, 2 * GU_TILE), lambda j, i: (i, j)),
        out_shape=jax.ShapeDtypeStruct((t, 2 * D_FF), MXU_DTYPE), compiler_params=_cp("parallel", "parallel"))(*args)


def mm_nn_res_loss(a, b, res, target, name, tm=512):
    t, k = a.shape
    d = b.shape[1]
    tm = min(tm, t)

    def body(a_ref, b_ref, res_ref, t_ref, dy_ref, l_ref):
        e = res_ref[...] + _mx(a_ref[...], b_ref[...]) - t_ref[...]
        dy_ref[...] = e * (1.0 / d)
        part = jnp.zeros((1, 128), F32) + 0.5 * jnp.sum(jnp.mean(e * e, axis=-1, keepdims=True), axis=0, keepdims=True)
        l_ref[...] = jnp.where(pl.program_id(0) == 0, part, l_ref[...] + part)

    row = pl.BlockSpec((tm, d), lambda i: (i, 0))
    return pl.pallas_call(
        body, name=name, grid=(t // tm,),
        in_specs=[pl.BlockSpec((tm, k), lambda i: (i, 0)), pl.BlockSpec((k, d), lambda i: (0, 0)), row, row],
        out_specs=(row, pl.BlockSpec((1, 128), lambda i: (0, 0))),
        out_shape=(jax.ShapeDtypeStruct((t, d), F32), jax.ShapeDtypeStruct((1, 128), F32)),
        compiler_params=_cp("arbitrary"))(a, b, res, target)


QK_W = Q_W + KV_W
_QK_TILE = 256


def _qk_mats():
    idx = np.arange(_QK_TILE)
    half = HEAD_DIM // 2
    same = (idx[:, None] // HEAD_DIM) == (idx[None, :] // HEAD_DIM)
    lo = (idx % HEAD_DIM) < half
    rot = np.where((idx[:, None] == idx[None, :] + half) & lo[None, :], -1.0, 0.0)
    rot = rot + np.where((idx[:, None] == idx[None, :] - half) & ~lo[None, :], 1.0, 0.0)
    return jnp.asarray(same, F32), jnp.asarray(rot, F32)


QK_TM = 256


def _qk_gains(q_gain, k_gain):
    return jnp.concatenate([q_gain] * ATTN_HEADS + [k_gain] * KV_HEADS, axis=-1)


def _rope_tile(ca_ref, sa_ref, cb_ref, sb_ref):
    ca, sa, cb, sb = ca_ref[0], sa_ref[0], cb_ref[...], sb_ref[...]
    c, s = ca * cb - sa * sb, sa * cb + ca * sb
    rep = QK_W // 128
    return jnp.concatenate([c] * rep, axis=-1), jnp.concatenate([s] * rep, axis=-1)


_ROPE_SPECS = [pl.BlockSpec((1, 1, 128), lambda i: (i, 0, 0)), pl.BlockSpec((1, 1, 128), lambda i: (i, 0, 0)),
               pl.BlockSpec((QK_TM, 128), lambda i: (0, 0)), pl.BlockSpec((QK_TM, 128), lambda i: (0, 0))]


def _qk_tiles(a, mat, transposed=False):
    outs = []
    for c0 in range(0, QK_W, _QK_TILE):
        w = min(_QK_TILE, QK_W - c0)
        mt = (mat.T if transposed else mat)[:w, :w].astype(MXU_DTYPE)
        at = a[:, c0:c0 + w]
        hi = at.astype(MXU_DTYPE)
        lo = (at - hi.astype(F32)).astype(MXU_DTYPE)
        outs.append(jnp.dot(hi, mt, preferred_element_type=F32) + jnp.dot(lo, mt, preferred_element_type=F32))
    return jnp.concatenate(outs, axis=-1)


def qk_prep_fwd(proj, q_gain, k_gain, rope, name):
    t = proj.shape[0]
    tm = QK_TM
    gmat, rmat = _qk_mats()
    gain = _qk_gains(q_gain, k_gain)

    def body(p_ref, g_ref, ca_ref, sa_ref, cb_ref, sb_ref, gm_ref, rm_ref, q_ref, k_ref):
        x = p_ref[...]
        r = lax.rsqrt(_qk_tiles(x * x, gm_ref[...]) * (1.0 / HEAD_DIM) + EPS)
        xn = x * r * g_ref[...]
        c, s = _rope_tile(ca_ref, sa_ref, cb_ref, sb_ref)
        out = xn * c + _qk_tiles(xn, rm_ref[...]) * s
        q_ref[...] = out[:, :Q_W]
        k_ref[...] = out[:, Q_W:]

    full = pl.BlockSpec((_QK_TILE, _QK_TILE), lambda i: (0, 0))
    return pl.pallas_call(
        body, name=name, grid=(t // tm,),
        in_specs=[pl.BlockSpec((tm, QK_W), lambda i: (i, 0)), pl.BlockSpec((1, QK_W), lambda i: (0, 0))] + _ROPE_SPECS
        + [full, full],
        out_specs=(pl.BlockSpec((tm, Q_W), lambda i: (i, 0)), pl.BlockSpec((tm, KV_W), lambda i: (i, 0))),
        out_shape=(jax.ShapeDtypeStruct((t, Q_W), F32), jax.ShapeDtypeStruct((t, KV_W), F32)),
        compiler_params=_cp("parallel"))(proj, gain, *rope, gmat, rmat)


def qk_prep_bwd(proj, q_gain, k_gain, rope, dq, dk, into, name):
    t = proj.shape[0]
    tm = QK_TM
    gmat, rmat = _qk_mats()
    gain = _qk_gains(q_gain, k_gain)
    lanes = np.arange(QK_W)[:, None]
    fold = jnp.asarray(lanes % HEAD_DIM + np.where(lanes >= Q_W, HEAD_DIM, 0) == np.arange(128)[None, :], F32)

    def body(p_ref, g_ref, ca_ref, sa_ref, cb_ref, sb_ref, gm_ref, rm_ref, f_ref, dq_ref, dk_ref, into_ref,
             o_ref, dg_ref):
        x = p_ref[...]
        r = lax.rsqrt(_qk_tiles(x * x, gm_ref[...]) * (1.0 / HEAD_DIM) + EPS)
        xh = x * r
        c, s = _rope_tile(ca_ref, sa_ref, cb_ref, sb_ref)
        dout = jnp.concatenate([dq_ref[...], dk_ref[...]], axis=-1)
        dxn = dout * c + _qk_tiles(dout * s, rm_ref[...], transposed=True)
        part = _hi(jnp.sum(dxn * xh, axis=0, keepdims=True), f_ref[...])
        dxh = dxn * g_ref[...]
        mean = _qk_tiles(dxh * xh, gm_ref[...]) * (1.0 / HEAD_DIM)
        o_ref[...] = (r * (dxh - xh * mean)).astype(o_ref.dtype)
        dg_ref[...] = jnp.where(pl.program_id(0) == 0, part, dg_ref[...] + part)

    full = pl.BlockSpec((_QK_TILE, _QK_TILE), lambda i: (0, 0))
    dqk, dg = pl.pallas_call(
        body, name=name, grid=(t // tm,),
        in_specs=[pl.BlockSpec((tm, QK_W), lambda i: (i, 0)), pl.BlockSpec((1, QK_W), lambda i: (0, 0))] + _ROPE_SPECS
        + [full, full, pl.BlockSpec((QK_W, 128), lambda i: (0, 0)),
                  pl.BlockSpec((tm, Q_W), lambda i: (i, 0)), pl.BlockSpec((tm, KV_W), lambda i: (i, 0)),
                  pl.BlockSpec(memory_space=pl.ANY)],
        out_specs=(pl.BlockSpec((tm, QK_W), lambda i: (i, 0)), pl.BlockSpec((1, 128), lambda i: (0, 0))),
        out_shape=(jax.ShapeDtypeStruct(into.shape, into.dtype), jax.ShapeDtypeStruct((1, 128), F32)),
        input_output_aliases={len(rope) + 7: 0},
        compiler_params=_cp("arbitrary"))(proj, gain, *rope, gmat, rmat, fold, dq, dk, into)
    return dqk, dg[:, :HEAD_DIM], dg[:, HEAD_DIM:]


def _swa_valid(n, grp):
    qi = lax.broadcasted_iota(jnp.int32, (grp * ATTN_BLOCK, 2 * ATTN_BLOCK), 0) & (ATTN_BLOCK - 1)
    kj = lax.broadcasted_iota(jnp.int32, (grp * ATTN_BLOCK, 2 * ATTN_BLOCK), 1)
    diff = qi + ATTN_BLOCK - kj
    return (diff >= 0) & (diff < ATTN_BLOCK) & (n * ATTN_BLOCK - ATTN_BLOCK + kj >= 0)


def _stack_heads(ref, g, grp, rows=slice(None)):
    return jnp.concatenate([ref[rows, (g * grp + j) * HEAD_DIM:(g * grp + j + 1) * HEAD_DIM] for j in range(grp)], axis=0)


def _stack_sinks(s_ref, g, grp):
    return jnp.concatenate([jnp.zeros((ATTN_BLOCK, 1), F32) + s_ref[0:1, g * grp + j:g * grp + j + 1]
                            for j in range(grp)], axis=0)


SWA_STEP = 2


def swa_fwd(q, k, proj, sinks, name):
    t = q.shape[0]
    nb = t // ATTN_BLOCK
    scale = HEAD_DIM ** -0.5
    grp = ATTN_HEADS // KV_HEADS

    rows = SWA_STEP * ATTN_BLOCK

    def body(q_ref, kc_ref, kp_ref, vc_ref, vp_ref, s_ref, y_ref, mix_ref, yt_ref, lse_ref):
        n0 = pl.program_id(0) * SWA_STEP
        kk = jnp.concatenate([kp_ref[...], kc_ref[...]], axis=0).astype(MXU_DTYPE)
        vv = jnp.concatenate([vp_ref[...], vc_ref[...]], axis=0).astype(MXU_DTYPE)
        lane = lax.broadcasted_iota(jnp.int32, (ATTN_BLOCK, ATTN_HEADS), 1)
        units = [(b, g) for b in range(SWA_STEP) for g in range(KV_HEADS)]
        blk = lambda b: slice(b * ATTN_BLOCK, (b + 1) * ATTN_BLOCK)
        keys = lambda b: slice(b * ATTN_BLOCK, (b + 2) * ATTN_BLOCK)
        col = lambda g: slice(g * HEAD_DIM, (g + 1) * HEAD_DIM)
        valid = [_swa_valid(n0 + b, grp) for b in range(SWA_STEP)]
        qg = [_stack_heads(q_ref, g, grp, blk(b)) for b, g in units]
        sink = [_stack_sinks(s_ref, g, grp) for b, g in units]
        sc = [jnp.where(valid[b], _mx_nt(qg[u], kk[keys(b), col(g)]) * scale, NEG) for u, (b, g) in enumerate(units)]
        m = [jnp.maximum(jnp.max(sc_, axis=-1, keepdims=True), sk) for sc_, sk in zip(sc, sink)]
        e = [jnp.exp(sc_ - m_) for sc_, m_ in zip(sc, m)]
        den = [jnp.sum(e_, axis=-1, keepdims=True) + jnp.exp(sk - m_) for e_, sk, m_ in zip(e, sink, m)]
        og = [_mx(e[u] / den[u], vv[keys(b), col(g)]) for u, (b, g) in enumerate(units)]
        lg = [m_ + jnp.log(d_) for m_, d_ in zip(m, den)]
        for b in range(SWA_STEP):
            lse = jnp.zeros((ATTN_BLOCK, ATTN_HEADS), F32)
            outs = []
            for h in range(ATTN_HEADS):
                u = b * KV_HEADS + h // grp
                sub = blk(h % grp)
                outs.append(og[u][sub])
                lse = jnp.where(lane == h, lg[u][sub], lse)
            y = jnp.concatenate(outs, axis=-1)
            y_ref[blk(b), :] = y
            mix_ref[blk(b), :] = y.astype(mix_ref.dtype)
            yt_ref[:, blk(b)] = y.T.astype(yt_ref.dtype)
            lse_ref[blk(b), :] = lse

    cur = lambda n: (n, 0)
    prev = lambda n: (jnp.maximum(n * SWA_STEP - 1, 0), 0)
    vcol = (Q_W + KV_W) // KV_W
    return pl.pallas_call(
        body, name=name, grid=(nb // SWA_STEP,),
        in_specs=[pl.BlockSpec((rows, Q_W), cur), pl.BlockSpec((rows, KV_W), cur),
                  pl.BlockSpec((ATTN_BLOCK, KV_W), prev),
                  pl.BlockSpec((rows, KV_W), lambda n: (n, vcol)),
                  pl.BlockSpec((ATTN_BLOCK, KV_W), lambda n: (jnp.maximum(n * SWA_STEP - 1, 0), vcol)),
                  pl.BlockSpec((1, ATTN_HEADS), lambda n: (0, 0))],
        out_specs=(pl.BlockSpec((rows, Q_W), cur), pl.BlockSpec((rows, Q_W), cur),
                   pl.BlockSpec((Q_W, rows), lambda n: (0, n)), pl.BlockSpec((rows, ATTN_HEADS), cur)),
        out_shape=(jax.ShapeDtypeStruct((t, Q_W), F32), jax.ShapeDtypeStruct((t, Q_W + CONV_CH), MXU_DTYPE),
                   jax.ShapeDtypeStruct((Q_W + CONV_CH, t), MXU_DTYPE), jax.ShapeDtypeStruct((t, ATTN_HEADS), F32)),
        compiler_params=_cp("parallel"))(q, k, k, proj, proj, sinks)


def swa_bwd(q, k, proj, sinks, y, lse, dmix, into, name):
    t = q.shape[0]
    nb = t // ATTN_BLOCK
    scale = HEAD_DIM ** -0.5
    grp = ATTN_HEADS // KV_HEADS

    def body(q_ref, kc_ref, kp_ref, vc_ref, vp_ref, s_ref, y_ref, lse_ref, dy_ref, into_ref,
             dq_ref, dk_ref, dv_ref, ds_ref, dkc, dvc):
        n = pl.program_id(0)

        @pl.when(n == 0)
        def _():
            dkc[...] = jnp.zeros_like(dkc)
            dvc[...] = jnp.zeros_like(dvc)
            ds_ref[...] = jnp.zeros_like(ds_ref)

        @pl.when(n < nb)
        def _():
            valid = _swa_valid(n, grp)
            kk = jnp.concatenate([kp_ref[...], kc_ref[...]], axis=0).astype(MXU_DTYPE)
            vv = jnp.concatenate([vp_ref[...], vc_ref[...]], axis=0).astype(MXU_DTYPE)
            lane = lax.broadcasted_iota(jnp.int32, (1, ATTN_HEADS), 1)
            gs = range(KV_HEADS)
            kg = [kk[:, g * HEAD_DIM:(g + 1) * HEAD_DIM] for g in gs]
            vg = [vv[:, g * HEAD_DIM:(g + 1) * HEAD_DIM] for g in gs]
            qg = [_stack_heads(q_ref, g, grp).astype(MXU_DTYPE) for g in gs]
            dog = [_stack_heads(dy_ref, g, grp) for g in gs]
            og = [_stack_heads(y_ref, g, grp) for g in gs]
            lg = [jnp.concatenate([lse_ref[:, g * grp + j:g * grp + j + 1] for j in range(grp)], axis=0) for g in gs]
            sink = [_stack_sinks(s_ref, g, grp) for g in gs]
            sc = [jnp.where(valid, _mx_nt(qg[g], kg[g]) * scale, NEG) for g in gs]
            p = [jnp.exp(sc[g] - lg[g]) for g in gs]
            delta = [jnp.sum(dog[g] * og[g], axis=-1, keepdims=True) for g in gs]
            ds = [p[g] * (_mx_nt(dog[g], vg[g]) - delta[g]) for g in gs]
            dqg = [_mx(ds[g], kg[g]) * scale for g in gs]
            dkf = jnp.concatenate([_mx_tn(ds[g], qg[g]) * scale for g in gs], axis=-1)
            dvf = jnp.concatenate([_mx_tn(p[g], dog[g]) for g in gs], axis=-1)
            dsk = [jnp.exp(sink[g] - lg[g]) * delta[g] for g in gs]
            dsink = jnp.zeros((1, ATTN_HEADS), F32)
            dqs = []
            for h in range(ATTN_HEADS):
                rows = slice((h % grp) * ATTN_BLOCK, (h % grp + 1) * ATTN_BLOCK)
                dqs.append(dqg[h // grp][rows])
                dsink = jnp.where(lane == h, -jnp.sum(dsk[h // grp][rows], axis=0, keepdims=True), dsink)
            dq_ref[...] = jnp.concatenate(dqs, axis=-1)
            dk_ref[...] = dkc[...] + dkf[:ATTN_BLOCK]
            dv_ref[...] = (dvc[...] + dvf[:ATTN_BLOCK]).astype(dv_ref.dtype)
            dkc[...] = dkf[ATTN_BLOCK:]
            dvc[...] = dvf[ATTN_BLOCK:]
            ds_ref[...] += dsink

        @pl.when(n == nb)
        def _():
            dk_ref[...] = dkc[...]
            dv_ref[...] = dvc[...].astype(dv_ref.dtype)

    cur = lambda n: (jnp.minimum(n, nb - 1), 0)
    prev = lambda n: (jnp.clip(n - 1, 0, nb - 1), 0)
    vcol = (Q_W + KV_W) // KV_W
    return pl.pallas_call(
        body, name=name, grid=(nb + 1,),
        in_specs=[pl.BlockSpec((ATTN_BLOCK, Q_W), cur), pl.BlockSpec((ATTN_BLOCK, KV_W), cur),
                  pl.BlockSpec((ATTN_BLOCK, KV_W), prev),
                  pl.BlockSpec((ATTN_BLOCK, KV_W), lambda n: (jnp.minimum(n, nb - 1), vcol)),
                  pl.BlockSpec((ATTN_BLOCK, KV_W), lambda n: (jnp.clip(n - 1, 0, nb - 1), vcol)),
                  pl.BlockSpec((1, ATTN_HEADS), lambda n: (0, 0)),
                  pl.BlockSpec((ATTN_BLOCK, Q_W), cur), pl.BlockSpec((ATTN_BLOCK, ATTN_HEADS), cur),
                  pl.BlockSpec((ATTN_BLOCK, Q_W), cur), pl.BlockSpec(memory_space=pl.ANY)],
        out_specs=(pl.BlockSpec((ATTN_BLOCK, Q_W), cur), pl.BlockSpec((ATTN_BLOCK, KV_W), prev),
                   pl.BlockSpec((ATTN_BLOCK, KV_W), lambda n: (jnp.clip(n - 1, 0, nb - 1), vcol)),
                   pl.BlockSpec((1, ATTN_HEADS), lambda n: (0, 0))),
        out_shape=(jax.ShapeDtypeStruct((t, Q_W), F32), jax.ShapeDtypeStruct((t, KV_W), F32),
                   jax.ShapeDtypeStruct(into.shape, into.dtype), jax.ShapeDtypeStruct((1, ATTN_HEADS), F32)),
        scratch_shapes=[pltpu.VMEM((ATTN_BLOCK, KV_W), F32), pltpu.VMEM((ATTN_BLOCK, KV_W), F32)],
        input_output_aliases={9: 2},
        compiler_params=_cp("arbitrary"))(q, k, k, proj, proj, sinks, y, lse, dmix, into)


GC_W = 256
_GB0, _GC0, _XI0 = 768 // GC_W, 1280 // GC_W, 1792 // GC_W
HALO = 8


def gconv_fwd(proj, conv_w, mix, mix_t, name, tm=512):
    t = proj.shape[0]
    hb = tm // HALO
    half = Q_W // GC_W

    def body(gb_ref, gc_ref, xi_ref, gch_ref, xih_ref, w_ref, mix_in, mixt_in, y_ref, yt_ref):
        i = pl.program_id(1)
        u = gc_ref[...] * xi_ref[...]
        uh = jnp.where(i == 0, 0.0, gch_ref[...] * xih_ref[...])
        up = jnp.concatenate([uh, u], axis=0)
        cv = w_ref[0:1, :] * up[HALO - 2:HALO - 2 + tm]
        cv = cv + w_ref[1:2, :] * up[HALO - 1:HALO - 1 + tm]
        cv = cv + w_ref[2:3, :] * u
        y = gb_ref[...] * cv
        y_ref[...] = y.astype(y_ref.dtype)
        yt_ref[...] = y.T.astype(yt_ref.dtype)

    def col(c0):
        return pl.BlockSpec((tm, GC_W), lambda cj, i: (i, c0 + cj))

    def halo(c0):
        return pl.BlockSpec((HALO, GC_W), lambda cj, i: (jnp.maximum(i * hb - 1, 0), c0 + cj))

    return pl.pallas_call(
        body, name=name, grid=(CONV_CH // GC_W, t // tm),
        in_specs=[col(_GB0), col(_GC0), col(_XI0), halo(_GC0), halo(_XI0),
                  pl.BlockSpec((3, GC_W), lambda cj, i: (0, cj)),
                  pl.BlockSpec(memory_space=pl.ANY), pl.BlockSpec(memory_space=pl.ANY)],
        out_specs=(pl.BlockSpec((tm, GC_W), lambda cj, i: (i, half + cj)),
                   pl.BlockSpec((GC_W, tm), lambda cj, i: (half + cj, i))),
        out_shape=(jax.ShapeDtypeStruct(mix.shape, mix.dtype), jax.ShapeDtypeStruct(mix_t.shape, mix_t.dtype)),
        input_output_aliases={6: 0, 7: 1},
        compiler_params=_cp("parallel", "parallel"))(proj, proj, proj, proj, proj, conv_w, mix, mix_t)


def gconv_bwd(proj, conv_w, dmix, into, name, tm=512):
    t = proj.shape[0]
    hb = tm // HALO
    nt = t // tm
    dy0 = Q_W // GC_W

    def body(gb_ref, gc_ref, xi_ref, gch_ref, xih_ref, gbn_ref, dyn_ref, dy_ref, w_ref, into_ref, o_ref, dw_ref):
        dgb_ref, dgc_ref, dxi_ref = (o_ref.at[:, j * GC_W:(j + 1) * GC_W] for j in range(3))
        i = pl.program_id(1)
        gc, xi, gb, dy = gc_ref[...], xi_ref[...], gb_ref[...], dy_ref[...]
        u = gc * xi
        uh = jnp.where(i == 0, 0.0, gch_ref[...] * xih_ref[...])
        up = jnp.concatenate([uh, u], axis=0)
        u2 = up[HALO - 2:HALO - 2 + tm]
        u1 = up[HALO - 1:HALO - 1 + tm]
        cv = w_ref[0:1, :] * u2 + w_ref[1:2, :] * u1 + w_ref[2:3, :] * u
        dgb_ref[...] = (dy * cv).astype(dgb_ref.dtype)
        dcv = dy * gb
        dcvn = jnp.where(i == nt - 1, 0.0, dyn_ref[...] * gbn_ref[...])
        dcvp = jnp.concatenate([dcv, dcvn], axis=0)
        du = w_ref[0:1, :] * dcvp[2:2 + tm] + w_ref[1:2, :] * dcvp[1:1 + tm] + w_ref[2:3, :] * dcv
        dgc_ref[...] = (du * xi).astype(dgc_ref.dtype)
        dxi_ref[...] = (du * gc).astype(dxi_ref.dtype)
        dw = jnp.concatenate([jnp.sum(dcv * u2, axis=0, keepdims=True), jnp.sum(dcv * u1, axis=0, keepdims=True),
                              jnp.sum(dcv * u, axis=0, keepdims=True)], axis=0)

        @pl.when(i == 0)
        def _():
            dw_ref[...] = dw

        @pl.when(i > 0)
        def _():
            dw_ref[...] += dw

    def col(c0):
        return pl.BlockSpec((tm, GC_W), lambda cj, i: (i, c0 + cj))

    def halo(c0):
        return pl.BlockSpec((HALO, GC_W), lambda cj, i: (jnp.maximum(i * hb - 1, 0), c0 + cj))

    def nxt(c0):
        return pl.BlockSpec((HALO, GC_W), lambda cj, i: (jnp.minimum((i + 1) * hb, t // HALO - 1), c0 + cj))

    return pl.pallas_call(
        body, name=name, grid=(CONV_CH // GC_W, nt),
        in_specs=[col(_GB0), col(_GC0), col(_XI0), halo(_GC0), halo(_XI0), nxt(_GB0), nxt(dy0), col(dy0),
                  pl.BlockSpec((3, GC_W), lambda cj, i: (0, cj)), pl.BlockSpec(memory_space=pl.ANY)],
        out_specs=(pl.BlockSpec((tm, 3 * GC_W), lambda cj, i: (i, 1 + cj)),
                   pl.BlockSpec((3, GC_W), lambda cj, i: (0, cj))),
        out_shape=(jax.ShapeDtypeStruct(into.shape, into.dtype), jax.ShapeDtypeStruct((3, CONV_CH), F32)),
        input_output_aliases={9: 0},
        compiler_params=_cp("parallel", "arbitrary"))(proj, proj, proj, proj, proj, proj, dmix, dmix, conv_w, into)


_EVEN_D_CHUNKS = ((0, 0, 768),) + tuple(
    (768 + (3 * j + part) * GC_W, 768 + part * CONV_CH + j * GC_W, GC_W)
    for j in range(CONV_CH // GC_W) for part in range(3))


_QKV_W = 3 * DN_W
_BA_COL = (4 * DN_W) // 128
_Z_COL = _QKV_W // DN_W


def gdn_prep_fwd(proj, conv_w, alog_row, dtb_row, name, tm=256):
    t = proj.shape[0]
    hb = tm // HALO
    qscale = DN_DIM ** -0.5

    def body(x_ref, xh_ref, w_ref, ba_ref, al_ref, dt_ref, q_ref, k_ref, v_ref, bg_ref, c_ref):
        i = pl.program_id(0)
        for gi in range(3 * DN_HEADS):
            sl = slice(gi * DN_DIM, (gi + 1) * DN_DIM)
            xp = jnp.concatenate([jnp.where(i == 0, 0.0, xh_ref[:, sl]), x_ref[:, sl]], axis=0)
            c = w_ref[0:1, sl] * xp[HALO - 3:HALO - 3 + tm]
            for j in range(1, 4):
                c = c + w_ref[j:j + 1, sl] * xp[HALO - 3 + j:HALO - 3 + j + tm]
            c_ref[:, sl] = c
            s = c * _sigmoid(c)
            osl = slice((gi % DN_HEADS) * DN_DIM, (gi % DN_HEADS + 1) * DN_DIM)
            if gi < DN_HEADS:
                q_ref[:, osl] = s * lax.rsqrt(jnp.sum(s * s, axis=-1, keepdims=True) + EPS) * qscale
            elif gi < 2 * DN_HEADS:
                k_ref[:, osl] = s * lax.rsqrt(jnp.sum(s * s, axis=-1, keepdims=True) + EPS)
            else:
                v_ref[:, osl] = s
        ba = ba_ref[...]
        lane = lax.broadcasted_iota(jnp.int32, ba.shape, 1)
        gval = -jnp.exp(al_ref[...]) * _softplus(ba + dt_ref[...])
        bg_ref[...] = jnp.where(lane < DN_HEADS, _sigmoid(ba), jnp.where(lane < 2 * DN_HEADS, gval, 0.0))

    row = pl.BlockSpec((tm, DN_W), lambda i: (i, 0))
    one = pl.BlockSpec((1, 128), lambda i: (0, 0))
    return pl.pallas_call(
        body, name=name, grid=(t // tm,),
        in_specs=[pl.BlockSpec((tm, _QKV_W), lambda i: (i, 0)),
                  pl.BlockSpec((HALO, _QKV_W), lambda i: (jnp.maximum(i * hb - 1, 0), 0)),
                  pl.BlockSpec((4, _QKV_W), lambda i: (0, 0)),
                  pl.BlockSpec((tm, 128), lambda i: (i, _BA_COL)), one, one],
        out_specs=(row, row, row, pl.BlockSpec((tm, 128), lambda i: (i, 0)), pl.BlockSpec((tm, _QKV_W), lambda i: (i, 0))),
        out_shape=(jax.ShapeDtypeStruct((t, DN_W), F32),) * 3 + (jax.ShapeDtypeStruct((t, 128), F32),
                                                                 jax.ShapeDtypeStruct((t, _QKV_W), F32)),
        compiler_params=_cp("parallel"))(proj, proj, conv_w, proj, alog_row, dtb_row)


def gdn_prep_bwd(proj, conv, conv_w, alog_row, dtb_row, dq, dk, dv, dbg, dz, name, tm=256):
    t = proj.shape[0]
    hb = tm // HALO
    nt = t // tm
    qscale = DN_DIM ** -0.5
    te = tm + HALO

    def body(x_ref, c_ref, cn_ref, w_ref, ba_ref, al_ref, dt_ref, dq_ref, dk_ref, dv_ref,
             dqn_ref, dkn_ref, dvn_ref, dbg_ref, dz_ref, dx_ref, dw_ref, ddt_ref, dal_ref):
        i = pl.program_id(0)
        first = i == 0
        last = i == nt - 1
        dws = []
        for gi in range(3 * DN_HEADS):
            sl = slice(gi * DN_DIM, (gi + 1) * DN_DIM)
            osl = slice((gi % DN_HEADS) * DN_DIM, (gi % DN_HEADS + 1) * DN_DIM)
            c = jnp.concatenate([c_ref[:, sl], cn_ref[:, sl]], axis=0)
            sg = _sigmoid(c)
            s = c * sg
            d_ref, dn_ref = ((dq_ref, dqn_ref), (dk_ref, dkn_ref), (dv_ref, dvn_ref))[gi // DN_HEADS]
            dy = jnp.concatenate([d_ref[:, osl], jnp.where(last, 0.0, dn_ref[:, osl])], axis=0)
            if gi < 2 * DN_HEADS:
                r = lax.rsqrt(jnp.sum(s * s, axis=-1, keepdims=True) + EPS)
                sh = s * r
                ds = r * (dy - sh * jnp.sum(sh * dy, axis=-1, keepdims=True))
                if gi < DN_HEADS:
                    ds = ds * qscale
            else:
                ds = dy
            dc = ds * sg * (1.0 + c * (1.0 - sg))
            dcs = [dc[3 - j:3 - j + tm] for j in range(4)]
            dx = w_ref[0:1, sl] * dcs[0]
            for j in range(1, 4):
                dx = dx + w_ref[j:j + 1, sl] * dcs[j]
            dx_ref[:, sl] = dx.astype(dx_ref.dtype)
            x0 = x_ref[:, sl]
            dws.append(jnp.concatenate([jnp.sum(dcs[j] * x0, axis=0, keepdims=True) for j in range(4)], axis=0))
        dw = jnp.concatenate(dws, axis=-1)
        ba = ba_ref[...]
        dbgv = dbg_ref[...]
        lane = lax.broadcasted_iota(jnp.int32, ba.shape, 1)
        beta = _sigmoid(ba)
        ea = -jnp.exp(al_ref[...])
        zin = ba + dt_ref[...]
        is_b = lane < DN_HEADS
        is_a = (lane >= DN_HEADS) & (lane < 2 * DN_HEADS)
        da = jnp.where(is_a, dbgv * ea * _sigmoid(zin), 0.0)
        dx_ref[:, _QKV_W:_QKV_W + DN_W] = dz_ref[...]
        dx_ref[:, _QKV_W + DN_W:] = jnp.where(is_b, dbgv * beta * (1.0 - beta), da).astype(dx_ref.dtype)
        ddt = jnp.sum(da, axis=0, keepdims=True)
        dal = jnp.sum(jnp.where(is_a, dbgv * ea * _softplus(zin), 0.0), axis=0, keepdims=True)

        @pl.when(first)
        def _():
            dw_ref[...] = dw
            ddt_ref[...] = ddt
            dal_ref[...] = dal

        @pl.when(i > 0)
        def _():
            dw_ref[...] += dw
            ddt_ref[...] += ddt
            dal_ref[...] += dal

    row = pl.BlockSpec((tm, DN_W), lambda i: (i, 0))
    nrow = pl.BlockSpec((HALO, DN_W), lambda i: (jnp.minimum((i + 1) * hb, t // HALO - 1), 0))
    one = pl.BlockSpec((1, 128), lambda i: (0, 0))
    return pl.pallas_call(
        body, name=name, grid=(nt,),
        in_specs=[pl.BlockSpec((tm, _QKV_W), lambda i: (i, 0)), pl.BlockSpec((tm, _QKV_W), lambda i: (i, 0)),
                  pl.BlockSpec((HALO, _QKV_W), lambda i: (jnp.minimum((i + 1) * hb, t // HALO - 1), 0)),
                  pl.BlockSpec((4, _QKV_W), lambda i: (0, 0)),
                  pl.BlockSpec((tm, 128), lambda i: (i, _BA_COL)), one, one,
                  row, row, row, nrow, nrow, nrow, pl.BlockSpec((tm, 128), lambda i: (i, 0)), row],
        out_specs=(pl.BlockSpec((tm, ODD_IN_PAD), lambda i: (i, 0)), pl.BlockSpec((4, _QKV_W), lambda i: (0, 0)), one, one),
        out_shape=(jax.ShapeDtypeStruct((t, ODD_IN_PAD), MXU_DTYPE), jax.ShapeDtypeStruct((4, _QKV_W), F32),
                   jax.ShapeDtypeStruct((1, 128), F32), jax.ShapeDtypeStruct((1, 128), F32)),
        compiler_params=_cp("arbitrary"))(proj, conv, conv, conv_w, proj, alog_row, dtb_row, dq, dk, dv, dq, dk, dv, dbg,
                                          dz)


def _chunk_masks():
    r = lax.broadcasted_iota(jnp.int32, (DN_CHUNK, DN_CHUNK), 0)
    c = lax.broadcasted_iota(jnp.int32, (DN_CHUNK, DN_CHUNK), 1)
    return r >= c, r > c


INV_PACK = 2


def _inv_unit_lower_many(mats):
    n = DN_CHUNK
    wide = INV_PACK * n
    r = lax.broadcasted_iota(jnp.int32, (wide, wide), 0)
    c = lax.broadcasted_iota(jnp.int32, (wide, wide), 1)
    same = (r & -n) == (c & -n)
    eye = jnp.where((r[:n] == (c[:n] & (n - 1))), 1.0, 0.0)

    def blockdiag(row):
        return jnp.where(same, jnp.concatenate([row] * INV_PACK, axis=0), 0.0)

    packs = [jnp.concatenate(mats[g:g + INV_PACK], axis=-1) for g in range(0, len(mats), INV_PACK)]
    xs = [eye - a for a in packs]
    pws = [_hi(a, blockdiag(a)) for a in packs]
    for step in range(5):
        if step < 4:
            both = [_hi(jnp.concatenate([x, pw], axis=0), blockdiag(pw)) for x, pw in zip(xs, pws)]
            xs = [x + b[:n] for x, b in zip(xs, both)]
            pws = [b[n:] for b in both]
        else:
            xs = [x + _hi(x, blockdiag(pw)) for x, pw in zip(xs, pws)]
    return [x[:, j * n:(j + 1) * n] for x in xs for j in range(INV_PACK)]


def _chunk_common(q, k, beta, gc, gcr, lower):
    gam = jnp.exp(jnp.where(lower, gc - gcr, NEG))
    eg = jnp.exp(gc)
    gl = gc[DN_CHUNK - 1:DN_CHUNK, :]
    kdf = jnp.exp(gl - gc)
    kb = k * beta
    bmat = _mx_nt(kb, k)
    qmat = _mx_nt(q, k)
    return gam, eg, jnp.exp(gl), kdf, kb, bmat, qmat


DN_STEP = 4


def gdn_fwd(q, k, v, bg, proj, o_gain, name):
    t = q.shape[0]
    n_chunks = t // DN_CHUNK

    def body(q_ref, k_ref, v_ref, bg_ref, z_ref, g_ref, o_ref, sall_ref, tall_ref, y_ref, yt_ref, s_ref):
        n = pl.program_id(0)

        @pl.when(n == 0)
        def _():
            s_ref[...] = jnp.zeros_like(s_ref)

        lower, strict = _chunk_masks()
        ltri = jnp.where(lower, 1.0, 0.0)
        hs = range(DN_HEADS)
        sl = [slice(h * DN_DIM, (h + 1) * DN_DIM) for h in hs]
        units = [(c, h) for c in range(DN_STEP) for h in hs]
        nu = range(len(units))
        rs = [slice(c * DN_CHUNK, (c + 1) * DN_CHUNK) for c in range(DN_STEP)]
        bgv = [bg_ref[rs[c], :] for c in range(DN_STEP)]
        gcs = [_hi(ltri, b) for b in bgv]
        gcs_t = [g.T for g in gcs]
        qh = [q_ref[rs[c], sl[h]] for c, h in units]
        kh = [k_ref[rs[c], sl[h]] for c, h in units]
        vh = [v_ref[rs[c], sl[h]] for c, h in units]
        beta = [bgv[c][:, h:h + 1] for c, h in units]
        com = [_chunk_common(qh[u], kh[u], beta[u], gcs[c][:, DN_HEADS + h:DN_HEADS + h + 1],
                             gcs_t[c][DN_HEADS + h:DN_HEADS + h + 1, :], lower) for u, (c, h) in enumerate(units)]
        gam, eg, dec, kdf, kb, bmat, qmat = zip(*com)
        tms = _inv_unit_lower_many([jnp.where(strict, bmat[u] * gam[u], 0.0) for u in nu])
        for u, (c, h) in enumerate(units):
            tall_ref[c, h] = tms[u]
        uw = [_hi(tms[u], jnp.concatenate([vh[u] * beta[u], kb[u] * eg[u]], axis=-1)) for u in nu]
        qd = [qh[u] * eg[u] for u in nu]
        pm = [qmat[u] * gam[u] for u in nu]
        kd = [kh[u] * kdf[u] for u in nu]
        st = [s_ref[h] for h in hs]
        for c in range(DN_STEP):
            us = [c * DN_HEADS + h for h in hs]
            for h in hs:
                sall_ref[c, h] = st[h]
            v_new = [uw[us[h]][:, :DN_DIM] - _mx(uw[us[h]][:, DN_DIM:], st[h]) for h in hs]
            o_st = [_mx(qd[us[h]], st[h]) for h in hs]
            o_in = [_mx(pm[us[h]], v_new[h]) for h in hs]
            s_up = [_mx_tn(kd[us[h]], v_new[h]) for h in hs]
            for h in hs:
                ov = o_st[h] + o_in[h]
                o_ref[rs[c], sl[h]] = ov
                zv = z_ref[rs[c], sl[h]]
                y = ov * lax.rsqrt(jnp.mean(ov * ov, axis=-1, keepdims=True) + EPS) * g_ref[...] * (zv * _sigmoid(zv))
                y_ref[rs[c], sl[h]] = y.astype(y_ref.dtype)
                yt_ref[sl[h], rs[c]] = y.T.astype(yt_ref.dtype)
            st = [st[h] * dec[us[h]] + s_up[h] for h in hs]
        for h in hs:
            s_ref[h] = st[h]

    rows = DN_STEP * DN_CHUNK
    row = pl.BlockSpec((rows, DN_W), lambda n: (n, 0))
    return pl.pallas_call(
        body, name=name, grid=(n_chunks // DN_STEP,),
        in_specs=[row, row, row, pl.BlockSpec((rows, 128), lambda n: (n, 0)),
                  pl.BlockSpec((rows, DN_W), lambda n: (n, _Z_COL)), pl.BlockSpec((1, DN_DIM), lambda n: (0, 0))],
        out_specs=(row, pl.BlockSpec((DN_STEP, DN_HEADS, DN_DIM, DN_DIM), lambda n: (n, 0, 0, 0)),
                   pl.BlockSpec((DN_STEP, DN_HEADS, DN_CHUNK, DN_CHUNK), lambda n: (n, 0, 0, 0)),
                   row, pl.BlockSpec((DN_W, rows), lambda n: (0, n))),
        out_shape=(jax.ShapeDtypeStruct((t, DN_W), F32),
                   jax.ShapeDtypeStruct((n_chunks, DN_HEADS, DN_DIM, DN_DIM), F32),
                   jax.ShapeDtypeStruct((n_chunks, DN_HEADS, DN_CHUNK, DN_CHUNK), F32),
                   jax.ShapeDtypeStruct((t, DN_W), MXU_DTYPE), jax.ShapeDtypeStruct((DN_W, t), MXU_DTYPE)),
        scratch_shapes=[pltpu.VMEM((DN_HEADS, DN_DIM, DN_DIM), F32)],
        compiler_params=_cp("arbitrary"))(q, k, v, bg, proj, o_gain)


def gdn_bwd(q, k, v, bg, sall, tall, do, name):
    t = q.shape[0]
    n_chunks = t // DN_CHUNK

    def body(q_ref, k_ref, v_ref, bg_ref, sall_ref, tall_ref, do_ref, dq_ref, dk_ref, dv_ref, dbg_ref, ds_ref):
        n = pl.program_id(0)

        @pl.when(n == 0)
        def _():
            ds_ref[...] = jnp.zeros_like(ds_ref)

        lower, strict = _chunk_masks()
        ltri = jnp.where(lower, 1.0, 0.0)
        bgv = bg_ref[...]
        gcs = _hi(ltri, bgv)
        gcs_t = gcs.T
        lane = lax.broadcasted_iota(jnp.int32, (DN_CHUNK, 128), 1)
        rowi = lax.broadcasted_iota(jnp.int32, (DN_CHUNK, 1), 0)
        hs = range(DN_HEADS)
        each = lambda fn, *ls: [fn(*a) for a in zip(*ls)]
        rsum = lambda a: jnp.sum(a, axis=-1, keepdims=True)
        sl = [slice(h * DN_DIM, (h + 1) * DN_DIM) for h in hs]
        st = [sall_ref[0, h] for h in hs]
        tms = [tall_ref[0, h] for h in hs]
        dsn = [ds_ref[h] for h in hs]
        qh = [q_ref[:, sl[h]] for h in hs]
        kh = [k_ref[:, sl[h]] for h in hs]
        vh = [v_ref[:, sl[h]] for h in hs]
        doh = [do_ref[:, sl[h]] for h in hs]
        beta = [bgv[:, h:h + 1] for h in hs]
        com = [_chunk_common(qh[h], kh[h], beta[h], gcs[:, DN_HEADS + h:DN_HEADS + h + 1],
                             gcs_t[DN_HEADS + h:DN_HEADS + h + 1, :], lower) for h in hs]
        gam, eg, dec, kdf, kb, bmat, qmat = zip(*com)
        rhs_w = each(lambda a, b: a * b, kb, eg)
        uw = each(lambda t_, v_, b_, r_: _hi(t_, jnp.concatenate([v_ * b_, r_], axis=-1)), tms, vh, beta, rhs_w)
        qd = each(lambda a, b: a * b, qh, eg)
        kd = each(lambda a, b: a * b, kh, kdf)
        pmat = each(lambda a, b: a * b, qmat, gam)
        v_new = each(lambda uw_, s_: uw_[:, :DN_DIM] - _mx(uw_[:, DN_DIM:], s_), uw, st)
        dqd = each(_mx_nt, doh, st)
        ds_o = each(_mx_tn, qd, doh)
        dp = each(lambda d_, v_: jnp.where(lower, _mx_nt(d_, v_), 0.0), doh, v_new)
        dvn_o = each(_mx_tn, pmat, doh)
        ddec = each(lambda d_, s_: jnp.sum(rsum(d_ * s_), axis=0, keepdims=True), dsn, st)
        dkd = each(_mx_nt, v_new, dsn)
        dvn = each(lambda a, k_, d_: a + _mx(k_, d_), dvn_o, kd, dsn)
        dw = each(lambda d_, s_: -_mx_nt(d_, s_), dvn, st)
        ds_w = each(lambda uw_, d_: _mx_tn(uw_[:, DN_DIM:], d_), uw, dvn)
        for h in hs:
            ds_ref[h] = ds_o[h] + dec[h] * dsn[h] - ds_w[h]
        dr = each(lambda t_, a, b: _hi_tn(t_, jnp.concatenate([a, b], axis=-1)), tms, dvn, dw)
        da = each(lambda r_, uw_: jnp.where(strict, -_hi_nt(r_, uw_), 0.0), dr, uw)
        dru = [r_[:, :DN_DIM] for r_ in dr]
        drw = [r_[:, DN_DIM:] for r_ in dr]
        db = each(lambda a, b: a * b, da, gam)
        dq_m = each(lambda a, b: a * b, dp, gam)
        e = each(lambda a, bm, p_, qm, g_: (a * bm + p_ * qm) * g_, da, bmat, dp, qmat, gam)
        dkb = each(lambda b_, k_, r_, e_: _mx(b_, k_) + r_ * e_, db, kh, drw, eg)
        dk = each(lambda b_, kb_, m_, q_, d_, f_: _mx_tn(b_, kb_) + _mx_tn(m_, q_) + d_ * f_, db, kb, dq_m, qh, dkd, kdf)
        dq = each(lambda m_, k_, d_, e_: _mx(m_, k_) + d_ * e_, dq_m, kh, dqd, eg)
        tk = each(lambda a, b: rsum(a * b), dkd, kd)
        dbeta_all = jnp.zeros((DN_CHUNK, 128), F32)
        dgc_all = jnp.zeros((DN_CHUNK, 128), F32)
        for h in hs:
            dgc = (jnp.sum(e[h], axis=1, keepdims=True) - jnp.sum(e[h].T, axis=1, keepdims=True)
                   + rsum(dqd[h] * qd[h]) - tk[h] + rsum(drw[h] * rhs_w[h]))
            dgl = jnp.sum(tk[h], axis=0, keepdims=True) + ddec[h] * dec[h]
            dgc = dgc + jnp.where(rowi == DN_CHUNK - 1, dgl, 0.0)
            dbeta = rsum(dru[h] * vh[h]) + rsum(dkb[h] * kh[h])
            dq_ref[:, sl[h]] = dq[h]
            dk_ref[:, sl[h]] = dk[h] + dkb[h] * beta[h]
            dv_ref[:, sl[h]] = dru[h] * beta[h]
            dbeta_all = jnp.where(lane == h, dbeta, dbeta_all)
            dgc_all = jnp.where(lane == DN_HEADS + h, dgc, dgc_all)
        dbg_ref[...] = dbeta_all + _hi_tn(ltri, dgc_all)

    rev = lambda n: (n_chunks - 1 - n, 0)
    row = pl.BlockSpec((DN_CHUNK, DN_W), rev)
    small = pl.BlockSpec((DN_CHUNK, 128), rev)
    return pl.pallas_call(
        body, name=name, grid=(n_chunks,),
        in_specs=[row, row, row, small,
                  pl.BlockSpec((1, DN_HEADS, DN_DIM, DN_DIM), lambda n: (n_chunks - 1 - n, 0, 0, 0)),
                  pl.BlockSpec((1, DN_HEADS, DN_CHUNK, DN_CHUNK), lambda n: (n_chunks - 1 - n, 0, 0, 0)), row],
        out_specs=(row, row, row, small),
        out_shape=(jax.ShapeDtypeStruct((t, DN_W), F32),) * 3 + (jax.ShapeDtypeStruct((t, 128), F32),),
        scratch_shapes=[pltpu.VMEM((DN_HEADS, DN_DIM, DN_DIM), F32)],
        compiler_params=_cp("arbitrary"))(q, k, v, bg, sall, tall, do)


def gdn_out_bwd(o, proj, o_gain, dx, w_out, name, tm=512, after=None):
    t = o.shape[0]
    tm = min(tm, t)

    def body(o_ref, z_ref, g_ref, dx_ref, w_ref, *rest):
        do_ref, dz_ref, dg_ref = rest[-3:]
        i = pl.program_id(0)
        dy = _mx_nt(dx_ref[...], w_ref[...])
        dg = jnp.zeros((1, DN_DIM), F32)
        for h in range(DN_HEADS):
            sl = slice(h * DN_DIM, (h + 1) * DN_DIM)
            ov, zv, dyv = o_ref[:, sl], z_ref[:, sl], dy[:, sl]
            r = lax.rsqrt(jnp.mean(ov * ov, axis=-1, keepdims=True) + EPS)
            oh = ov * r
            sg = _sigmoid(zv)
            dz_ref[:, sl] = (dyv * oh * g_ref[...] * sg * (1.0 + zv * (1.0 - sg))).astype(dz_ref.dtype)
            don = dyv * (zv * sg)
            dg = dg + jnp.sum(don * oh, axis=0, keepdims=True)
            doh = don * g_ref[...]
            do_ref[:, sl] = r * (doh - oh * jnp.mean(doh * oh, axis=-1, keepdims=True))

        @pl.when(i == 0)
        def _():
            dg_ref[...] = dg

        @pl.when(i > 0)
        def _():
            dg_ref[...] += dg

    row = pl.BlockSpec((tm, DN_W), lambda i: (i, 0))
    one = pl.BlockSpec((1, DN_DIM), lambda i: (0, 0))
    in_specs = [row, pl.BlockSpec((tm, DN_W), lambda i: (i, _Z_COL)), one,
                pl.BlockSpec((tm, dx.shape[1]), lambda i: (i, 0)), pl.BlockSpec(w_out.shape, lambda i: (0, 0))]
    args = [o, proj, o_gain, dx, w_out]
    if after is not None:
        in_specs.append(pl.BlockSpec(memory_space=pl.ANY))
        args.append(after)
    return pl.pallas_call(
        body, name=name, grid=(t // tm,), in_specs=in_specs, out_specs=(row, row, one),
        out_shape=(jax.ShapeDtypeStruct((t, DN_W), F32), jax.ShapeDtypeStruct((t, DN_W), MXU_DTYPE),
                   jax.ShapeDtypeStruct((1, DN_DIM), F32)),
        compiler_params=_cp("arbitrary"))(*args)


def _peer(k):
    x, y, c = lax.axis_index("x"), lax.axis_index("y"), lax.axis_index("c")
    px = 1 - x if k & 4 else x
    py = 1 - y if k & 2 else y
    pc = 1 - c if k & 1 else c
    return (px, py, pc), 4 * px + 2 * py + pc


_HBM = pl.BlockSpec(memory_space=pltpu.HBM)
_SEM = pl.BlockSpec(memory_space=pltpu.SEMAPHORE)
_DATAFLOW = pltpu.SideEffectType.DATAFLOW_SIDE_EFFECTING
N_PEER = N_DEV - 1


def join_blocks(parts, rows, name, tc=256):
    n, r, c = parts.shape

    def body(p_ref, o_ref):
        for j in range(n):
            o_ref[j * r:(j + 1) * r, :] = p_ref[j]
        if rows > n * r:
            o_ref[n * r:, :] = jnp.zeros((rows - n * r, tc), o_ref.dtype)

    return pl.pallas_call(
        body, name=name, grid=(c // tc,), in_specs=[pl.BlockSpec((n, r, tc), lambda i: (0, 0, i))],
        out_specs=pl.BlockSpec((rows, tc), lambda i: (0, i)), out_shape=jax.ShapeDtypeStruct((rows, c), parts.dtype),
        compiler_params=_cp("parallel"))(parts)


def split_blocks(whole, n, r, name, tc=256):
    rows, c = whole.shape

    def body(w_ref, o_ref):
        for j in range(n):
            o_ref[j] = w_ref[j * r:(j + 1) * r, :]

    return pl.pallas_call(
        body, name=name, grid=(c // tc,), in_specs=[pl.BlockSpec((rows, tc), lambda i: (0, i))],
        out_specs=pl.BlockSpec((n, r, tc), lambda i: (0, 0, i)), out_shape=jax.ShapeDtypeStruct((n, r, c), whole.dtype),
        compiler_params=_cp("parallel"))(whole)


def send_start(srcs, name, scatter, after):
    na = len(srcs)
    ns = (2 * N_PEER + 1) * na
    lands = [lax.empty((N_DEV,) + (s.shape[1:] if scatter else s.shape), s.dtype) for s in srcs]
    extra = [] if after is None else [after]

    def body(*refs):
        src_refs, land_refs = refs[:na], refs[na:2 * na]
        sems = refs[2 * na + len(extra):2 * na + len(extra) + ns]
        land_out, token = refs[-1 - na:-1], refs[-1]
        _, me = _peer(0)
        for a in range(na):
            pltpu.make_async_copy(src_refs[a].at[me] if scatter else src_refs[a], land_out[a].at[me],
                                  sems[2 * N_PEER * na + a]).start()
        for k in range(1, N_DEV):
            peer, pid = _peer(k)
            for a in range(na):
                pltpu.make_async_remote_copy(
                    src_ref=src_refs[a].at[pid] if scatter else src_refs[a], dst_ref=land_refs[a].at[me],
                    send_sem=sems[2 * (a * N_PEER + k - 1)], recv_sem=sems[2 * (a * N_PEER + k - 1) + 1],
                    device_id=peer, device_id_type=MESH).start()
        token[...] = jnp.zeros_like(token)

    hbm = lambda arrs: tuple(pltpu.HBM(a.shape, a.dtype) for a in arrs)
    outs = pl.pallas_call(
        body, name=name,
        out_shape=(pltpu.SemaphoreType.DMA(()),) * ns + hbm(srcs) + hbm(lands) + (jax.ShapeDtypeStruct((8, 128), F32),),
        in_specs=[_HBM] * (2 * na) + [pl.BlockSpec(memory_space=pl.ANY)] * len(extra),
        out_specs=(_SEM,) * ns + (_HBM,) * (2 * na) + (pl.BlockSpec(memory_space=pltpu.VMEM),),
        input_output_aliases={i: ns + i for i in range(2 * na)},
        compiler_params=pltpu.CompilerParams(has_side_effects=_DATAFLOW),
    )(*[pltpu.with_memory_space_constraint(a, pltpu.HBM) for a in list(srcs) + lands], *extra)
    return outs[:ns], outs[ns:ns + na], outs[ns + na:ns + 2 * na], outs[-1]


def send_wait(sems, srcs_thru, lands_thru, name, scatter, after):
    na = len(srcs_thru)
    ns = (2 * N_PEER + 1) * na

    def body(*refs):
        src_refs, land_refs, sm = refs[:na], refs[na:2 * na], refs[2 * na:2 * na + ns]
        _, me = _peer(0)
        for a in range(na):
            pltpu.make_async_copy(src_refs[a].at[me] if scatter else src_refs[a], land_refs[a].at[me],
                                  sm[2 * N_PEER * na + a]).wait()
        for k in range(1, N_DEV):
            peer, pid = _peer(k)
            for a in range(na):
                cp = pltpu.make_async_remote_copy(
                    src_ref=src_refs[a].at[pid] if scatter else src_refs[a], dst_ref=land_refs[a].at[pid],
                    send_sem=sm[2 * (a * N_PEER + k - 1)], recv_sem=sm[2 * (a * N_PEER + k - 1) + 1],
                    device_id=peer, device_id_type=MESH)
                cp.wait_send()
                cp.wait_recv()

    hbm = lambda arrs: tuple(pltpu.HBM(a.shape, a.dtype) for a in arrs)
    outs = pl.pallas_call(
        body, name=name, out_shape=hbm(srcs_thru) + hbm(lands_thru),
        in_specs=[_HBM] * (2 * na) + [_SEM] * ns + [pl.BlockSpec(memory_space=pl.ANY)], out_specs=(_HBM,) * (2 * na),
        input_output_aliases={i: i for i in range(2 * na)},
        compiler_params=pltpu.CompilerParams(has_side_effects=_DATAFLOW),
    )(*srcs_thru, *lands_thru, *sems, after)
    return outs[na:]


def _adamw(w, g, m, v):
    m = ADAM_B1 * m + (1.0 - ADAM_B1) * g
    v = ADAM_B2 * v + (1.0 - ADAM_B2) * (g * g)
    m_hat = m / (1.0 - ADAM_B1 ** ADAM_STEP)
    v_hat = v / (1.0 - ADAM_B2 ** ADAM_STEP)
    return -ADAM_LR * (m_hat / (jnp.sqrt(v_hat) + ADAM_EPS) + ADAM_WD * w), m, v


def adam_sum(w, pieces, m, v, name, layer=0, into=None):
    nl, r, c = w.shape
    tr = r
    for cand in (256, 128, 64, 32, 16, 8):
        if r % cand == 0:
            tr = cand
            break

    def body(w_ref, p_ref, m_ref, v_ref, *rest):
        g_ref, d_ref, nm_ref, nv_ref = rest[-4:]
        g = p_ref[0].astype(F32)
        for s in range(1, N_DEV):
            g = g + p_ref[s].astype(F32)
        g_ref[0] = g
        d_ref[0], nm_ref[0], nv_ref[0] = _adamw(w_ref[0], g, m_ref[0], v_ref[0])

    row = pl.BlockSpec((1, tr, c), lambda i: (layer, i, 0))
    out = jax.ShapeDtypeStruct((nl, r, c), F32)
    extra = [] if into is None else list(into)
    return pl.pallas_call(
        body, name=name, grid=(r // tr,),
        in_specs=[row, pl.BlockSpec((N_DEV, tr, c), lambda i: (0, i, 0)), row, row]
        + [pl.BlockSpec(memory_space=pl.ANY)] * len(extra),
        out_specs=(row,) * 4, out_shape=(out,) * 4,
        input_output_aliases={4 + i: i for i in range(len(extra))},
        compiler_params=_cp("parallel"))(w, pieces, m, v, *extra)


def sum_rows(gathered, name):
    _, r, c = gathered.shape

    def body(p_ref, o_ref):
        g = p_ref[0]
        for s in range(1, N_DEV):
            g = g + p_ref[s]
        o_ref[...] = g

    return pl.pallas_call(body, name=name, out_shape=jax.ShapeDtypeStruct((r, c), F32))(gathered)


def adam_small(w, g, m, v, name):
    def body(w_ref, g_ref, m_ref, v_ref, d_ref, nm_ref, nv_ref):
        d_ref[...], nm_ref[...], nv_ref[...] = _adamw(w_ref[...], g_ref[...], m_ref[...], v_ref[...])

    out = jax.ShapeDtypeStruct(w.shape, F32)
    return pl.pallas_call(body, name=name, out_shape=(out,) * 3)(w, g, m, v)


def _rope_tables(t):
    inv_freq = 10000.0 ** (-jnp.arange(0, HEAD_DIM, 2, dtype=F32) / HEAD_DIM)
    lanes = lambda a: jnp.concatenate([a] * (128 // a.shape[-1]), axis=-1)
    base = (jnp.arange(t // QK_TM, dtype=F32) * QK_TM)[:, None] * inv_freq[None, :]
    offs = jnp.arange(QK_TM, dtype=F32)[:, None] * inv_freq[None, :]
    return (lanes(jnp.cos(base))[:, None, :], lanes(jnp.sin(base))[:, None, :], lanes(jnp.cos(offs)), lanes(jnp.sin(offs)))


def _lane_row(vec8):
    return jnp.pad(vec8.reshape(1, DN_HEADS), ((0, 0), (DN_HEADS, 128 - 2 * DN_HEADS)))


def _ffn_bwd(x, norm_g, w_gu, w_d, saved, dy, tag, after=None):
    ft, gu, at = saved
    dgu = ffn_dact(dy, w_d, gu, f"{tag}_d_gate_up", after=after)
    dwd = mm_at(at, dy, f"{tag}_dw_down")
    nj = D_FF // GU_TILE
    dwgu = mm_at(ft, dgu, f"{tag}_dw_gate_up", transposed=True, tn=GU_TILE, row_block=lambda q: (q % 2) * nj + q // 2)
    dx, dg = mm_rms_bwd(dgu, w_gu, x, norm_g, dy, f"{tag}_d_norm", chunks=_GU_CHUNKS)
    return dx, dwgu, dwd, dg


def local_step(x, target, small, weights_of, grads_out, after=None):
    t = x.shape[0]
    rope = _rope_tables(t)
    alog_row, dtb_row = _lane_row(small["odd_a_log"]), _lane_row(small["odd_dt_bias"])

    h0, h0t = rms_fwd(x, small["even_norm"], "even_norm", after=after)
    we = weights_of("even", h0)
    small = {**small, **we.get("small", {})}
    proj0 = mm_nt(h0, we["w_in"], "even_in_proj")
    qr, kr = qk_prep_fwd(proj0, small["even_q_gain"], small["even_k_gain"], rope, "even_qk_prep")
    y_attn, mix0, mix0_t, lse = swa_fwd(qr, kr, proj0, small["even_sinks"], "even_swa")
    mix0, mix0_t = gconv_fwd(proj0, small["even_conv_w"], mix0, mix0_t, "even_gconv")
    we = {**we, **weights_of("even_out", mix0)}
    x1, f0, f0t = mm_nn_res_norm(mix0, we["w_out"], x, small["ffn_norm0"], "even_out_proj")
    w0 = weights_of("ffn0", x1)
    gu0, a0, a0t = ffn_up(f0, w0["gate_up"], "ffn0_gate_up")
    ffn0 = (f0t, gu0, a0t)
    x2, h1, h1t = mm_nn_res_norm(a0, w0["down"], x1, small["odd_norm"], "ffn0_down")

    wo = weights_of("odd", x2)
    proj1 = mm_nt(h1, wo["w_in"], "odd_in_proj")
    qn, kn, vs, bg, conv1 = gdn_prep_fwd(proj1, small["odd_conv_w"], alog_row, dtb_row, "odd_prep")
    o, sall, tall, og, ogt = gdn_fwd(qn, kn, vs, bg, proj1, small["odd_o_gain"], "odd_delta_rule")
    x3, f1, f1t = mm_nn_res_norm(og, wo["w_out"], x2, small["ffn_norm1"], "odd_out_proj")
    w1 = weights_of("ffn1", x3)
    gu1, a1, a1t = ffn_up(f1, w1["gate_up"], "ffn1_gate_up")
    ffn1 = (f1t, gu1, a1t)
    dy, loss_row = mm_nn_res_loss(a1, w1["down"], x3, target, "ffn1_down_loss")

    gs = {}
    dx3, dwgu, dwd, gs["ffn_norm1"] = _ffn_bwd(x3, small["ffn_norm1"], w1["gate_up"], w1["down"], ffn1, dy, "ffn1")
    tok = grads_out("ffn1", {"gate_up": dwgu, "down": dwd})

    do, dz, gs["odd_o_gain"] = gdn_out_bwd(o, proj1, small["odd_o_gain"], dx3, wo["w_out"], "odd_d_gate_norm", after=tok)
    dwo = mm_at(ogt, dx3, "odd_dw_out")
    dqn, dkn, dvs, dbg = gdn_bwd(qn, kn, vs, bg, sall, tall, do, "odd_d_delta_rule")
    dproj1, gs["odd_conv_w"], ddt_row, dal_row = gdn_prep_bwd(
        proj1, conv1, small["odd_conv_w"], alog_row, dtb_row, dqn, dkn, dvs, dbg, dz, "odd_d_prep")
    gs["odd_dt_bias"] = ddt_row[:, DN_HEADS:2 * DN_HEADS]
    gs["odd_a_log"] = dal_row[:, DN_HEADS:2 * DN_HEADS]
    dwi = mm_at(h1t, dproj1, "odd_dw_in", transposed=True)
    dx2, gs["odd_norm"] = mm_rms_bwd(dproj1, wo["w_in"], x2, small["odd_norm"], dx3, "odd_d_norm")
    tok = grads_out("odd", {"w_in": dwi, "w_out": dwo})

    dx1, dwgu, dwd, gs["ffn_norm0"] = _ffn_bwd(x1, small["ffn_norm0"], w0["gate_up"], w0["down"], ffn0, dx2, "ffn0",
                                               after=tok)
    tok = grads_out("ffn0", {"gate_up": dwgu, "down": dwd})

    dmix = mm_nt(dx1, we["w_out"], "even_d_mix", after=tok)
    dwo = mm_at(mix0_t, dx1, "even_dw_out")
    dproj0 = lax.empty((t, EVEN_IN_W), MXU_DTYPE)
    dqr, dkr, dproj0, gs["even_sinks"] = swa_bwd(
        qr, kr, proj0, small["even_sinks"], y_attn, lse, dmix, dproj0, "even_d_swa")
    dproj0, gs["even_q_gain"], gs["even_k_gain"] = qk_prep_bwd(
        proj0, small["even_q_gain"], small["even_k_gain"], rope, dqr, dkr, dproj0, "even_d_qk_prep")
    dproj0, gs["even_conv_w"] = gconv_bwd(proj0, small["even_conv_w"], dmix, dproj0, "even_d_gconv")
    dwi = mm_at(h0t, dproj0, "even_dw_in", transposed=True, chunks=_EVEN_D_CHUNKS)
    tok = grads_out("even", {"w_in": dwi, "w_out": dwo})
    grad_x, gs["even_norm"] = mm_rms_bwd(dproj0, we["w_in"], x, small["even_norm"], dx1, "even_d_norm", after=tok,
                                         chunks=_EVEN_D_CHUNKS)
    return loss_row, grad_x, gs


_SMALL_ORDER = ("even_norm", "even_q_gain", "even_k_gain", "even_sinks", "odd_a_log", "odd_dt_bias", "odd_o_gain",
                "ffn_norm0", "ffn_norm1", "odd_norm", "even_conv_w", "odd_conv_w")
_SMALL_SIZE = {"even_norm": 1024, "even_q_gain": 64, "even_k_gain": 64, "even_sinks": 8, "odd_a_log": 8,
               "odd_dt_bias": 8, "odd_o_gain": 128, "ffn_norm0": 1024, "ffn_norm1": 1024, "odd_norm": 1024,
               "even_conv_w": 3 * 512, "odd_conv_w": 4 * 3072}
_N_REPL = 9


def _pack_rows(vals):
    flat = jnp.concatenate([v.reshape(-1) for v in vals])
    pad = (-flat.shape[0]) % 1024
    return jnp.pad(flat, (0, pad)).reshape(-1, 128)


def _my_block(full, size, axis):
    me = 4 * lax.axis_index("x") + 2 * lax.axis_index("y") + lax.axis_index("c")
    return lax.dynamic_slice_in_dim(full, me * size, size, axis=axis)


def kernel(x, even_norm, even_w_in, even_q_gain, even_k_gain, even_sinks, even_conv_w, even_w_out, odd_norm, odd_w_in, odd_conv_w, odd_a_log, odd_dt_bias, odd_o_gain, odd_w_out, ffn_norm, ffn_w_gate_up, ffn_w_down, loss_target, m_even_norm, m_even_w_in, m_even_q_gain, m_even_k_gain, m_even_sinks, m_even_conv_w, m_even_w_out, m_odd_norm, m_odd_w_in, m_odd_conv_w, m_odd_a_log, m_odd_dt_bias, m_odd_o_gain, m_odd_w_out, m_ffn_norm, m_ffn_w_gate_up, m_ffn_w_down, v_even_norm, v_even_w_in, v_even_q_gain, v_even_k_gain, v_even_sinks, v_even_conv_w, v_even_w_out, v_odd_norm, v_odd_w_in, v_odd_conv_w, v_odd_a_log, v_odd_dt_bias, v_odd_o_gain, v_odd_w_out, v_ffn_norm, v_ffn_w_gate_up, v_ffn_w_down):
    t = x.shape[1]
    d = D_MODEL

    tr = lambda a: jnp.swapaxes(a, 1, 2)
    shard = {
        "even": {"w_in": tr(even_w_in)[0], "w_out": even_w_out[0]},
        "ffn0": {"gate_up": tr(ffn_w_gate_up)[0], "down": ffn_w_down[0]},
        "odd": {"w_in": tr(odd_w_in)[0], "w_out": odd_w_out[0]},
        "ffn1": {"gate_up": tr(ffn_w_gate_up)[1], "down": ffn_w_down[1]},
    }
    given = {
        ("even", "w_in"): ("even_w_in", even_w_in, m_even_w_in, v_even_w_in, 0),
        ("even", "w_out"): ("even_w_out", even_w_out, m_even_w_out, v_even_w_out, 0),
        ("odd", "w_in"): ("odd_w_in", odd_w_in, m_odd_w_in, v_odd_w_in, 0),
        ("odd", "w_out"): ("odd_w_out", odd_w_out, m_odd_w_out, v_odd_w_out, 0),
        ("ffn0", "gate_up"): ("ffn_w_gate_up", ffn_w_gate_up, m_ffn_w_gate_up, v_ffn_w_gate_up, 0),
        ("ffn1", "gate_up"): ("ffn_w_gate_up", ffn_w_gate_up, m_ffn_w_gate_up, v_ffn_w_gate_up, 1),
        ("ffn0", "down"): ("ffn_w_down", ffn_w_down, m_ffn_w_down, v_ffn_w_down, 0),
        ("ffn1", "down"): ("ffn_w_down", ffn_w_down, m_ffn_w_down, v_ffn_w_down, 1),
    }

    def whole(group, parts):
        col, row = tuple(shard[group])
        if group == "odd":
            w_col = join_blocks(parts[0], ODD_IN_PAD, "odd_w_in_join")
        else:
            w_col = parts[0].reshape(-1, d)
        return {col: w_col, row: parts[1].reshape(-1, d)}

    wire = {g: [a.astype(MXU_DTYPE) for a in shard[g].values()] for g in shard}
    wire["even_out"] = [wire["even"].pop()]
    wire["even"].append(_pack_rows([odd_norm, even_conv_w, odd_conv_w]))
    gathers, tok = {}, None
    for g in ("even", "even_out", "ffn0", "odd", "ffn1"):
        sems, srcs_thru, lands_thru, tok = send_start(wire[g], f"gather_{g}_start", False, tok)
        gathers[g] = (sems, srcs_thru, lands_thru)
    o1 = d // N_DEV
    o2 = o1 + 3 * CONV_CH // N_DEV

    def weights_of(group, after):
        lands = send_wait(*gathers[group], f"gather_{group}_wait", False, after)
        if group == "even_out":
            return {"w_out": lands[0].reshape(-1, d)}
        if group != "even":
            return whole(group, lands)
        sg = lands[1].reshape(N_DEV, -1)
        return {"w_in": lands[0].reshape(-1, d), "small": {
            "odd_norm": sg[:, :o1].reshape(1, d),
            "even_conv_w": sg[:, o1:o2].reshape(N_DEV, 3, CONV_CH // N_DEV).transpose(1, 0, 2).reshape(3, CONV_CH),
            "odd_conv_w": sg[:, o2:o2 + 4 * _QKV_W // N_DEV].reshape(N_DEV, 4, _QKV_W // N_DEV)
            .transpose(1, 0, 2).reshape(4, _QKV_W)}}

    sent = {}

    def grads_out(group, dws):
        col, row = tuple(shard[group])
        c = shard[group][col].shape[0]
        pieces = [split_blocks(dws[col], N_DEV, c, "odd_dw_in_split") if group == "odd"
                  else dws[col].reshape((N_DEV,) + shard[group][col].shape),
                  dws[row].reshape((N_DEV,) + shard[group][row].shape)]
        sems, srcs_thru, lands_thru, token = send_start(pieces, f"exchange_{group}_start", True, None)
        sent[group] = (sems, srcs_thru, lands_thru, pieces)
        return token

    small = {
        "even_norm": even_norm, "even_q_gain": even_q_gain, "even_k_gain": even_k_gain, "even_sinks": even_sinks,
        "odd_a_log": odd_a_log.reshape(-1), "odd_dt_bias": odd_dt_bias.reshape(-1), "odd_o_gain": odd_o_gain,
        "ffn_norm0": ffn_norm[0:1], "ffn_norm1": ffn_norm[1:2],
    }

    loss_row, grad_x, gs = local_step(x.reshape(t, d), loss_target.reshape(t, d), small, weights_of, grads_out, after=tok)

    rows = _pack_rows([gs[n] for n in _SMALL_ORDER] + [loss_row[:, 0:1]])
    small_sent = send_start([rows], "gather_small_grads_start", False, None)

    res, behind = {}, small_sent[3]
    for g in ("ffn1", "odd", "ffn0", "even"):
        sems, srcs_thru, lands_thru, pieces = sent[g]
        lands = send_wait(sems, srcs_thru, lands_thru, f"exchange_{g}_wait", True, behind)
        for i, (key, pcs) in enumerate(zip(shard[g], lands)):
            name, w_, m_, v_, layer = given[g, key]
            view = tr if i == 0 else (lambda a: a)
            res[name] = adam_sum(view(w_), pcs, view(m_), view(v_), f"adamw_{g}_{key}", layer=layer, into=res.get(name))
        behind = res[name][0]
    for name in ("even_w_in", "odd_w_in", "ffn_w_gate_up"):
        res[name] = tuple(tr(a) for a in res[name])

    (rows_g,) = send_wait(*small_sent[:3], "gather_small_grads_wait", False, behind)
    tot = sum_rows(rows_g, "sum_small_grads").reshape(-1)
    off, sgrad = 0, {}
    for n in _SMALL_ORDER:
        sgrad[n] = tot[off:off + _SMALL_SIZE[n]]
        off += _SMALL_SIZE[n]
    loss = tot[off]

    repl = _SMALL_ORDER[:_N_REPL]
    repl_w = {"even_norm": even_norm, "even_q_gain": even_q_gain, "even_k_gain": even_k_gain, "even_sinks": even_sinks,
              "odd_a_log": odd_a_log, "odd_dt_bias": odd_dt_bias, "odd_o_gain": odd_o_gain,
              "ffn_norm0": ffn_norm[0], "ffn_norm1": ffn_norm[1]}
    repl_m = {"even_norm": m_even_norm, "even_q_gain": m_even_q_gain, "even_k_gain": m_even_k_gain,
              "even_sinks": m_even_sinks, "odd_a_log": m_odd_a_log, "odd_dt_bias": m_odd_dt_bias,
              "odd_o_gain": m_odd_o_gain, "ffn_norm0": m_ffn_norm[0], "ffn_norm1": m_ffn_norm[1]}
    repl_v = {"even_norm": v_even_norm, "even_q_gain": v_even_q_gain, "even_k_gain": v_even_k_gain,
              "even_sinks": v_even_sinks, "odd_a_log": v_odd_a_log, "odd_dt_bias": v_odd_dt_bias,
              "odd_o_gain": v_odd_o_gain, "ffn_norm0": v_ffn_norm[0], "ffn_norm1": v_ffn_norm[1]}
    pk = lambda dct: _pack_rows([dct[n] for n in repl])
    pd_, pm_, pv_ = adam_small(pk(repl_w), pk(sgrad), pk(repl_m), pk(repl_v), "adamw_replicated")
    sres = {}
    off = 0
    for n in repl:
        sz = _SMALL_SIZE[n]
        sres[n] = (sgrad[n], pd_.reshape(-1)[off:off + sz], pm_.reshape(-1)[off:off + sz], pv_.reshape(-1)[off:off + sz])
        off += sz
    g_on = _my_block(sgrad["odd_norm"].reshape(1, d), d // N_DEV, 1)
    g_ec = _my_block(sgrad["even_conv_w"].reshape(3, CONV_CH), CONV_CH // N_DEV, 1)
    g_oc = _my_block(sgrad["odd_conv_w"].reshape(4, _QKV_W), _QKV_W // N_DEV, 1)
    shard_w = _pack_rows([odd_norm, even_conv_w, odd_conv_w])
    sd_, sm_, sv_ = adam_small(shard_w, _pack_rows([g_on, g_ec, g_oc]),
                               _pack_rows([m_odd_norm, m_even_conv_w, m_odd_conv_w]),
                               _pack_rows([v_odd_norm, v_even_conv_w, v_odd_conv_w]), "adamw_sharded_small")
    off = 0
    for n, gfull, like in (("odd_norm", g_on, odd_norm), ("even_conv_w", g_ec, even_conv_w), ("odd_conv_w", g_oc, odd_conv_w)):
        sz = like.size
        sres[n] = (gfull, sd_.reshape(-1)[off:off + sz], sm_.reshape(-1)[off:off + sz], sv_.reshape(-1)[off:off + sz])
        off += sz

    def small_out(name, like, kind):
        if name == "ffn_norm":
            return jnp.stack([sres["ffn_norm0"][kind], sres["ffn_norm1"][kind]]).reshape(like.shape)
        return sres[name][kind].reshape(like.shape)

    order = (("even_norm", even_norm), ("even_w_in", even_w_in), ("even_q_gain", even_q_gain),
             ("even_k_gain", even_k_gain), ("even_sinks", even_sinks), ("even_conv_w", even_conv_w),
             ("even_w_out", even_w_out), ("odd_norm", odd_norm), ("odd_w_in", odd_w_in), ("odd_conv_w", odd_conv_w),
             ("odd_a_log", odd_a_log), ("odd_dt_bias", odd_dt_bias), ("odd_o_gain", odd_o_gain),
             ("odd_w_out", odd_w_out), ("ffn_norm", ffn_norm), ("ffn_w_gate_up", ffn_w_gate_up),
             ("ffn_w_down", ffn_w_down))
    outs = [loss, grad_x.reshape(x.shape)]
    for kind in range(4):
        for name, like in order:
            outs.append(res[name][kind] if name in res else small_out(name, like, kind))
    return tuple(outs)
```

```python
import jax
import jax.numpy as jnp
import numpy as np
from jax import lax
from jax.experimental import pallas as pl
from jax.experimental.pallas import tpu as pltpu

F32 = jnp.float32
MXU_DTYPE = jnp.bfloat16
HI = lax.Precision.HIGH
EPS = 1e-6
N_DEV = 8
D_MODEL = 1024
HEAD_DIM = 64
ATTN_HEADS = 8
KV_HEADS = 2
ATTN_BLOCK = 128
Q_W = 512
KV_W = 128
CONV_CH = 512
EVEN_IN_W = 2304
DN_HEADS = 8
DN_DIM = 128
DN_W = 1024
DN_CHUNK = 64
ODD_IN_W = 4112
ODD_IN_PAD = 4224
D_FF = 2816
NEG = -1e30
VMEM_LIMIT = 56 * 1024 * 1024
ADAM_LR, ADAM_B1, ADAM_B2, ADAM_EPS, ADAM_WD, ADAM_STEP = 0.001, 0.9, 0.999, 1e-08, 0.01, 10
MESH = pl.DeviceIdType.MESH


def _cp(*sem):
    return pltpu.CompilerParams(dimension_semantics=sem, vmem_limit_bytes=VMEM_LIMIT)


def _pick(n, cap):
    best = 128
    for t in range(128, cap + 1, 128):
        if n % t == 0:
            best = t
    return best


def _mx(a, b):
    return jnp.dot(a.astype(MXU_DTYPE), b.astype(MXU_DTYPE), preferred_element_type=F32)


def _mx_nt(a, b):
    return lax.dot_general(a.astype(MXU_DTYPE), b.astype(MXU_DTYPE), (((1,), (1,)), ((), ())),
                           preferred_element_type=F32)


def _mx_tn(a, b):
    return lax.dot_general(a.astype(MXU_DTYPE), b.astype(MXU_DTYPE), (((0,), (0,)), ((), ())),
                           preferred_element_type=F32)


def _hi(a, b):
    return jnp.dot(a, b, precision=HI, preferred_element_type=F32)


def _hi_nt(a, b):
    return lax.dot_general(a, b, (((1,), (1,)), ((), ())), precision=HI, preferred_element_type=F32)


def _hi_tn(a, b):
    return lax.dot_general(a, b, (((0,), (0,)), ((), ())), precision=HI, preferred_element_type=F32)


def _sigmoid(x):
    return 0.5 * jnp.tanh(0.5 * x) + 0.5


def _softplus(x):
    return jnp.maximum(x, 0.0) + jnp.log(1.0 + jnp.exp(-jnp.abs(x)))


def mm_nn_res_norm(a, b, res, g, name, tm=512):
    t, k = a.shape
    d = b.shape[1]
    tm = min(tm, t)

    def body(a_ref, b_ref, res_ref, g_ref, y_ref, h_ref, ht_ref):
        y = res_ref[...] + _mx(a_ref[...], b_ref[...])
        y_ref[...] = y
        h = y * lax.rsqrt(jnp.mean(y * y, axis=-1, keepdims=True) + EPS) * g_ref[...]
        h_ref[...] = h.astype(h_ref.dtype)
        ht_ref[...] = h.T.astype(ht_ref.dtype)

    row = pl.BlockSpec((tm, d), lambda i: (i, 0))
    return pl.pallas_call(
        body, name=name, grid=(t // tm,),
        in_specs=[pl.BlockSpec((tm, k), lambda i: (i, 0)), pl.BlockSpec((k, d), lambda i: (0, 0)), row,
                  pl.BlockSpec((1, d), lambda i: (0, 0))],
        out_specs=(row, row, pl.BlockSpec((d, tm), lambda i: (0, i))),
        out_shape=(jax.ShapeDtypeStruct((t, d), F32), jax.ShapeDtypeStruct((t, d), MXU_DTYPE),
                   jax.ShapeDtypeStruct((d, t), MXU_DTYPE)),
        compiler_params=_cp("parallel"))(a, b, res, g)


def mm_nt(a, b, name, out_dtype=F32, tm=2048, after=None):
    m, k = a.shape
    n, _ = b.shape
    tn = _pick(n, 512 if k > 3000 else 1536)
    tm = min(tm, m)

    def body(a_ref, b_ref, *rest):
        o_ref = rest[-1]
        o_ref[...] = _mx_nt(a_ref[...], b_ref[...]).astype(o_ref.dtype)

    in_specs = [pl.BlockSpec((tm, k), lambda j, i: (i, 0)), pl.BlockSpec((tn, k), lambda j, i: (j, 0))]
    args = [a, b]
    if after is not None:
        in_specs.append(pl.BlockSpec(memory_space=pl.ANY))
        args.append(after)
    return pl.pallas_call(
        body, name=name, grid=(n // tn, m // tm), in_specs=in_specs,
        out_specs=pl.BlockSpec((tm, tn), lambda j, i: (i, j)),
        out_shape=jax.ShapeDtypeStruct((m, n), out_dtype), compiler_params=_cp("parallel", "parallel"))(*args)


def mm_at(at, b, name, tk=2048, transposed=False, tn=None, row_block=None, chunks=None):
    m, kk = at.shape
    _, n = b.shape
    tm, tn, tk = _pick(m, 1408), tn or _pick(n, 2816), min(tk, kk)
    nk = kk // tk
    assert chunks is None or (transposed and tn == n)

    def body(a_ref, b_ref, o_ref, acc_ref):
        k = pl.program_id(2)
        p = _mx(a_ref[...], b_ref[...])
        acc = jnp.where(k == 0, p, acc_ref[...] + p)
        acc_ref[...] = acc

        @pl.when(k == nk - 1)
        def _():
            res = (acc.T if transposed else acc).astype(o_ref.dtype)
            if chunks is None:
                o_ref[...] = res
            else:
                for cb, co, size in chunks:
                    o_ref[co:co + size, :] = res[cb:cb + size]

    if transposed:
        rb = row_block or (lambda j: j)
        out_spec = pl.BlockSpec((tn, tm), lambda i, j, k: (rb(j), i))
        out_shape = jax.ShapeDtypeStruct((n, m), MXU_DTYPE)
    else:
        out_spec = pl.BlockSpec((tm, tn), lambda i, j, k: (i, j))
        out_shape = jax.ShapeDtypeStruct((m, n), MXU_DTYPE)
    return pl.pallas_call(
        body, name=name, grid=(m // tm, n // tn, nk),
        in_specs=[pl.BlockSpec((tm, tk), lambda i, j, k: (i, k)), pl.BlockSpec((tk, tn), lambda i, j, k: (k, j))],
        out_specs=out_spec, out_shape=out_shape, scratch_shapes=[pltpu.VMEM((tm, tn), F32)],
        compiler_params=_cp("parallel", "parallel", "arbitrary"))(at, b)


def rms_fwd(x, g, name, tm=512, after=None):
    t, d = x.shape

    def body(x_ref, g_ref, *rest):
        o_ref, ot_ref = rest[-2:]
        xv = x_ref[...]
        r = lax.rsqrt(jnp.mean(xv * xv, axis=-1, keepdims=True) + EPS)
        h = xv * r * g_ref[...]
        o_ref[...] = h.astype(o_ref.dtype)
        ot_ref[...] = h.T.astype(ot_ref.dtype)

    in_specs = [pl.BlockSpec((tm, d), lambda i: (i, 0)), pl.BlockSpec((1, d), lambda i: (0, 0))]
    args = [x, g]
    if after is not None:
        in_specs.append(pl.BlockSpec(memory_space=pl.ANY))
        args.append(after)
    return pl.pallas_call(
        body, name=name, grid=(t // tm,), in_specs=in_specs,
        out_specs=(pl.BlockSpec((tm, d), lambda i: (i, 0)), pl.BlockSpec((d, tm), lambda i: (0, i))),
        out_shape=(jax.ShapeDtypeStruct((t, d), MXU_DTYPE), jax.ShapeDtypeStruct((d, t), MXU_DTYPE)),
        compiler_params=_cp("parallel"))(*args)


def mm_rms_bwd(a, bt, x, g, dres, name, tm=512, after=None, chunks=None):
    t, k = a.shape
    d = bt.shape[1]
    tm = min(tm if k > 3000 else 2 * tm, t)
    chunks = chunks or ((0, 0, k),)

    def body(a_ref, b_ref, x_ref, g_ref, dres_ref, *rest):
        dx_ref, dg_ref = rest[-2:]
        dhv = None
        for ca, cb, size in chunks:
            part = _mx(a_ref[:, ca:ca + size], b_ref[cb:cb + size, :])
            dhv = part if dhv is None else dhv + part
        xv = x_ref[...]
        r = lax.rsqrt(jnp.mean(xv * xv, axis=-1, keepdims=True) + EPS)
        xh = xv * r
        dxh = dhv * g_ref[...]
        dx_ref[...] = dres_ref[...] + r * (dxh - xh * jnp.mean(dxh * xh, axis=-1, keepdims=True))
        part = jnp.sum(dhv * xh, axis=0, keepdims=True)
        dg_ref[...] = jnp.where(pl.program_id(0) == 0, part, dg_ref[...] + part)

    row = pl.BlockSpec((tm, d), lambda i: (i, 0))
    one = pl.BlockSpec((1, d), lambda i: (0, 0))
    in_specs = [pl.BlockSpec((tm, k), lambda i: (i, 0)), pl.BlockSpec((k, d), lambda i: (0, 0)), row, one, row]
    args = [a, bt, x, g, dres]
    if after is not None:
        in_specs.append(pl.BlockSpec(memory_space=pl.ANY))
        args.append(after)
    return pl.pallas_call(
        body, name=name, grid=(t // tm,), in_specs=in_specs, out_specs=(row, one),
        out_shape=(jax.ShapeDtypeStruct((t, d), F32), jax.ShapeDtypeStruct((1, d), F32)),
        compiler_params=_cp("arbitrary"))(*args)


GU_TILE = 1408


def ffn_up(f, wt, name, tm=1024):
    t, d = f.shape
    tm = min(tm, t)
    nj = D_FF // GU_TILE

    def body(f_ref, wg_ref, wu_ref, gu_ref, a_ref, at_ref):
        g = _mx_nt(f_ref[...], wg_ref[...])
        u = _mx_nt(f_ref[...], wu_ref[...])
        sg = _sigmoid(g)
        gs = g * sg
        gu_ref[:, :GU_TILE] = (u * (sg + gs - gs * sg)).astype(gu_ref.dtype)
        gu_ref[:, GU_TILE:] = gs.astype(gu_ref.dtype)
        act = gs * u
        a_ref[...] = act.astype(a_ref.dtype)
        at_ref[...] = act.T.astype(at_ref.dtype)

    return pl.pallas_call(
        body, name=name, grid=(nj, t // tm),
        in_specs=[pl.BlockSpec((tm, d), lambda j, i: (i, 0)), pl.BlockSpec((GU_TILE, d), lambda j, i: (j, 0)),
                  pl.BlockSpec((GU_TILE, d), lambda j, i: (nj + j, 0))],
        out_specs=(pl.BlockSpec((tm, 2 * GU_TILE), lambda j, i: (i, j)), pl.BlockSpec((tm, GU_TILE), lambda j, i: (i, j)),
                   pl.BlockSpec((GU_TILE, tm), lambda j, i: (j, i))),
        out_shape=(jax.ShapeDtypeStruct((t, 2 * D_FF), MXU_DTYPE), jax.ShapeDtypeStruct((t, D_FF), MXU_DTYPE),
                   jax.ShapeDtypeStruct((D_FF, t), MXU_DTYPE)),
        compiler_params=_cp("parallel", "parallel"))(f, wt, wt)


_GU_CHUNKS = tuple((q * GU_TILE, ((q % 2) * (D_FF // GU_TILE) + q // 2) * GU_TILE, GU_TILE)
                   for q in range(2 * D_FF // GU_TILE))


def ffn_dact(dy, w_d, gu, name, tm=1024, after=None):
    t, d = dy.shape
    tm = min(tm, t)

    def body(dy_ref, w_ref, gu_ref, *rest):
        o_ref = rest[-1]
        da = _mx_nt(dy_ref[...], w_ref[...])
        o_ref[:, :GU_TILE] = (da * gu_ref[:, :GU_TILE]).astype(o_ref.dtype)
        o_ref[:, GU_TILE:] = (da * gu_ref[:, GU_TILE:]).astype(o_ref.dtype)

    in_specs = [pl.BlockSpec((tm, d), lambda j, i: (i, 0)), pl.BlockSpec((GU_TILE, d), lambda j, i: (j, 0)),
                pl.BlockSpec((tm, 2 * GU_TILE), lambda j, i: (i, j))]
    args = [dy, w_d, gu]
    if after is not None:
        in_specs.append(pl.BlockSpec(memory_space=pl.ANY))
        args.append(after)
    return pl.pallas_call(
        body, name=name, grid=(D_FF // GU_TILE, t // tm), in_specs=in_specs,
        out_specs=pl.BlockSpec((tm, 2 * GU_TILE), lambda j, i: (i, j)),
        out_shape=jax.ShapeDtypeStruct((t, 2 * D_FF), MXU_DTYPE), compiler_params=_cp("parallel", "parallel"))(*args)


def mm_nn_res_loss(a, b, res, target, name, tm=512):
    t, k = a.shape
    d = b.shape[1]
    tm = min(tm, t)

    def body(a_ref, b_ref, res_ref, t_ref, dy_ref, l_ref):
        e = res_ref[...] + _mx(a_ref[...], b_ref[...]) - t_ref[...]
        dy_ref[...] = e * (1.0 / d)
        part = jnp.zeros((1, 128), F32) + 0.5 * jnp.sum(jnp.mean(e * e, axis=-1, keepdims=True), axis=0, keepdims=True)
        l_ref[...] = jnp.where(pl.program_id(0) == 0, part, l_ref[...] + part)

    row = pl.BlockSpec((tm, d), lambda i: (i, 0))
    return pl.pallas_call(
        body, name=name, grid=(t // tm,),
        in_specs=[pl.BlockSpec((tm, k), lambda i: (i, 0)), pl.BlockSpec((k, d), lambda i: (0, 0)), row, row],
        out_specs=(row, pl.BlockSpec((1, 128), lambda i: (0, 0))),
        out_shape=(jax.ShapeDtypeStruct((t, d), F32), jax.ShapeDtypeStruct((1, 128), F32)),
        compiler_params=_cp("arbitrary"))(a, b, res, target)


QK_W = Q_W + KV_W
_QK_TILE = 256


def _qk_mats():
    idx = np.arange(_QK_TILE)
    half = HEAD_DIM // 2
    same = (idx[:, None] // HEAD_DIM) == (idx[None, :] // HEAD_DIM)
    lo = (idx % HEAD_DIM) < half
    rot = np.where((idx[:, None] == idx[None, :] + half) & lo[None, :], -1.0, 0.0)
    rot = rot + np.where((idx[:, None] == idx[None, :] - half) & ~lo[None, :], 1.0, 0.0)
    return jnp.asarray(same, F32), jnp.asarray(rot, F32)


QK_TM = 256


def _qk_gains(q_gain, k_gain):
    return jnp.concatenate([q_gain] * ATTN_HEADS + [k_gain] * KV_HEADS, axis=-1)


def _rope_tile(ca_ref, sa_ref, cb_ref, sb_ref):
    ca, sa, cb, sb = ca_ref[0], sa_ref[0], cb_ref[...], sb_ref[...]
    c, s = ca * cb - sa * sb, sa * cb + ca * sb
    rep = QK_W // 128
    return jnp.concatenate([c] * rep, axis=-1), jnp.concatenate([s] * rep, axis=-1)


_ROPE_SPECS = [pl.BlockSpec((1, 1, 128), lambda i: (i, 0, 0)), pl.BlockSpec((1, 1, 128), lambda i: (i, 0, 0)),
               pl.BlockSpec((QK_TM, 128), lambda i: (0, 0)), pl.BlockSpec((QK_TM, 128), lambda i: (0, 0))]


def _qk_tiles(a, mat, transposed=False):
    outs = []
    for c0 in range(0, QK_W, _QK_TILE):
        w = min(_QK_TILE, QK_W - c0)
        mt = (mat.T if transposed else mat)[:w, :w].astype(MXU_DTYPE)
        at = a[:, c0:c0 + w]
        hi = at.astype(MXU_DTYPE)
        lo = (at - hi.astype(F32)).astype(MXU_DTYPE)
        outs.append(jnp.dot(hi, mt, preferred_element_type=F32) + jnp.dot(lo, mt, preferred_element_type=F32))
    return jnp.concatenate(outs, axis=-1)


def qk_prep_fwd(proj, q_gain, k_gain, rope, name):
    t = proj.shape[0]
    tm = QK_TM
    gmat, rmat = _qk_mats()
    gain = _qk_gains(q_gain, k_gain)

    def body(p_ref, g_ref, ca_ref, sa_ref, cb_ref, sb_ref, gm_ref, rm_ref, q_ref, k_ref):
        x = p_ref[...]
        r = lax.rsqrt(_qk_tiles(x * x, gm_ref[...]) * (1.0 / HEAD_DIM) + EPS)
        xn = x * r * g_ref[...]
        c, s = _rope_tile(ca_ref, sa_ref, cb_ref, sb_ref)
        out = xn * c + _qk_tiles(xn, rm_ref[...]) * s
        q_ref[...] = out[:, :Q_W]
        k_ref[...] = out[:, Q_W:]

    full = pl.BlockSpec((_QK_TILE, _QK_TILE), lambda i: (0, 0))
    return pl.pallas_call(
        body, name=name, grid=(t // tm,),
        in_specs=[pl.BlockSpec((tm, QK_W), lambda i: (i, 0)), pl.BlockSpec((1, QK_W), lambda i: (0, 0))] + _ROPE_SPECS
        + [full, full],
        out_specs=(pl.BlockSpec((tm, Q_W), lambda i: (i, 0)), pl.BlockSpec((tm, KV_W), lambda i: (i, 0))),
        out_shape=(jax.ShapeDtypeStruct((t, Q_W), F32), jax.ShapeDtypeStruct((t, KV_W), F32)),
        compiler_params=_cp("parallel"))(proj, gain, *rope, gmat, rmat)


def qk_prep_bwd(proj, q_gain, k_gain, rope, dq, dk, into, name):
    t = proj.shape[0]
    tm = QK_TM
    gmat, rmat = _qk_mats()
    gain = _qk_gains(q_gain, k_gain)
    lanes = np.arange(QK_W)[:, None]
    fold = jnp.asarray(lanes % HEAD_DIM + np.where(lanes >= Q_W, HEAD_DIM, 0) == np.arange(128)[None, :], F32)

    def body(p_ref, g_ref, ca_ref, sa_ref, cb_ref, sb_ref, gm_ref, rm_ref, f_ref, dq_ref, dk_ref, into_ref,
             o_ref, dg_ref):
        x = p_ref[...]
        r = lax.rsqrt(_qk_tiles(x * x, gm_ref[...]) * (1.0 / HEAD_DIM) + EPS)
        xh = x * r
        c, s = _rope_tile(ca_ref, sa_ref, cb_ref, sb_ref)
        dout = jnp.concatenate([dq_ref[...], dk_ref[...]], axis=-1)
        dxn = dout * c + _qk_tiles(dout * s, rm_ref[...], transposed=True)
        part = _hi(jnp.sum(dxn * xh, axis=0, keepdims=True), f_ref[...])
        dxh = dxn * g_ref[...]
        mean = _qk_tiles(dxh * xh, gm_ref[...]) * (1.0 / HEAD_DIM)
        o_ref[...] = (r * (dxh - xh * mean)).astype(o_ref.dtype)
        dg_ref[...] = jnp.where(pl.program_id(0) == 0, part, dg_ref[...] + part)

    full = pl.BlockSpec((_QK_TILE, _QK_TILE), lambda i: (0, 0))
    dqk, dg = pl.pallas_call(
        body, name=name, grid=(t // tm,),
        in_specs=[pl.BlockSpec((tm, QK_W), lambda i: (i, 0)), pl.BlockSpec((1, QK_W), lambda i: (0, 0))] + _ROPE_SPECS
        + [full, full, pl.BlockSpec((QK_W, 128), lambda i: (0, 0)),
                  pl.BlockSpec((tm, Q_W), lambda i: (i, 0)), pl.BlockSpec((tm, KV_W), lambda i: (i, 0)),
                  pl.BlockSpec(memory_space=pl.ANY)],
        out_specs=(pl.BlockSpec((tm, QK_W), lambda i: (i, 0)), pl.BlockSpec((1, 128), lambda i: (0, 0))),
        out_shape=(jax.ShapeDtypeStruct(into.shape, into.dtype), jax.ShapeDtypeStruct((1, 128), F32)),
        input_output_aliases={len(rope) + 7: 0},
        compiler_params=_cp("arbitrary"))(proj, gain, *rope, gmat, rmat, fold, dq, dk, into)
    return dqk, dg[:, :HEAD_DIM], dg[:, HEAD_DIM:]


def _swa_valid(n, grp):
    qi = lax.broadcasted_iota(jnp.int32, (grp * ATTN_BLOCK, 2 * ATTN_BLOCK), 0) & (ATTN_BLOCK - 1)
    kj = lax.broadcasted_iota(jnp.int32, (grp * ATTN_BLOCK, 2 * ATTN_BLOCK), 1)
    diff = qi + ATTN_BLOCK - kj
    return (diff >= 0) & (diff < ATTN_BLOCK) & (n * ATTN_BLOCK - ATTN_BLOCK + kj >= 0)


def _stack_heads(ref, g, grp, rows=slice(None)):
    return jnp.concatenate([ref[rows, (g * grp + j) * HEAD_DIM:(g * grp + j + 1) * HEAD_DIM] for j in range(grp)], axis=0)


def _stack_sinks(s_ref, g, grp):
    return jnp.concatenate([jnp.zeros((ATTN_BLOCK, 1), F32) + s_ref[0:1, g * grp + j:g * grp + j + 1]
                            for j in range(grp)], axis=0)


SWA_STEP = 2


def swa_fwd(q, k, proj, sinks, name):
    t = q.shape[0]
    nb = t // ATTN_BLOCK
    scale = HEAD_DIM ** -0.5
    grp = ATTN_HEADS // KV_HEADS

    rows = SWA_STEP * ATTN_BLOCK

    def body(q_ref, kc_ref, kp_ref, vc_ref, vp_ref, s_ref, y_ref, mix_ref, yt_ref, lse_ref):
        n0 = pl.program_id(0) * SWA_STEP
        kk = jnp.concatenate([kp_ref[...], kc_ref[...]], axis=0).astype(MXU_DTYPE)
        vv = jnp.concatenate([vp_ref[...], vc_ref[...]], axis=0).astype(MXU_DTYPE)
        lane = lax.broadcasted_iota(jnp.int32, (ATTN_BLOCK, ATTN_HEADS), 1)
        units = [(b, g) for b in range(SWA_STEP) for g in range(KV_HEADS)]
        blk = lambda b: slice(b * ATTN_BLOCK, (b + 1) * ATTN_BLOCK)
        keys = lambda b: slice(b * ATTN_BLOCK, (b + 2) * ATTN_BLOCK)
        col = lambda g: slice(g * HEAD_DIM, (g + 1) * HEAD_DIM)
        valid = [_swa_valid(n0 + b, grp) for b in range(SWA_STEP)]
        qg = [_stack_heads(q_ref, g, grp, blk(b)) for b, g in units]
        sink = [_stack_sinks(s_ref, g, grp) for b, g in units]
        sc = [jnp.where(valid[b], _mx_nt(qg[u], kk[keys(b), col(g)]) * scale, NEG) for u, (b, g) in enumerate(units)]
        m = [jnp.maximum(jnp.max(sc_, axis=-1, keepdims=True), sk) for sc_, sk in zip(sc, sink)]
        e = [jnp.exp(sc_ - m_) for sc_, m_ in zip(sc, m)]
        den = [jnp.sum(e_, axis=-1, keepdims=True) + jnp.exp(sk - m_) for e_, sk, m_ in zip(e, sink, m)]
        og = [_mx(e[u] / den[u], vv[keys(b), col(g)]) for u, (b, g) in enumerate(units)]
        lg = [m_ + jnp.log(d_) for m_, d_ in zip(m, den)]
        for b in range(SWA_STEP):
            lse = jnp.zeros((ATTN_BLOCK, ATTN_HEADS), F32)
            outs = []
            for h in range(ATTN_HEADS):
                u = b * KV_HEADS + h // grp
                sub = blk(h % grp)
                outs.append(og[u][sub])
                lse = jnp.where(lane == h, lg[u][sub], lse)
            y = jnp.concatenate(outs, axis=-1)
            y_ref[blk(b), :] = y
            mix_ref[blk(b), :] = y.astype(mix_ref.dtype)
            yt_ref[:, blk(b)] = y.T.astype(yt_ref.dtype)
            lse_ref[blk(b), :] = lse

    cur = lambda n: (n, 0)
    prev = lambda n: (jnp.maximum(n * SWA_STEP - 1, 0), 0)
    vcol = (Q_W + KV_W) // KV_W
    return pl.pallas_call(
        body, name=name, grid=(nb // SWA_STEP,),
        in_specs=[pl.BlockSpec((rows, Q_W), cur), pl.BlockSpec((rows, KV_W), cur),
                  pl.BlockSpec((ATTN_BLOCK, KV_W), prev),
                  pl.BlockSpec((rows, KV_W), lambda n: (n, vcol)),
                  pl.BlockSpec((ATTN_BLOCK, KV_W), lambda n: (jnp.maximum(n * SWA_STEP - 1, 0), vcol)),
                  pl.BlockSpec((1, ATTN_HEADS), lambda n: (0, 0))],
        out_specs=(pl.BlockSpec((rows, Q_W), cur), pl.BlockSpec((rows, Q_W), cur),
                   pl.BlockSpec((Q_W, rows), lambda n: (0, n)), pl.BlockSpec((rows, ATTN_HEADS), cur)),
        out_shape=(jax.ShapeDtypeStruct((t, Q_W), F32), jax.ShapeDtypeStruct((t, Q_W + CONV_CH), MXU_DTYPE),
                   jax.ShapeDtypeStruct((Q_W + CONV_CH, t), MXU_DTYPE), jax.ShapeDtypeStruct((t, ATTN_HEADS), F32)),
        compiler_params=_cp("parallel"))(q, k, k, proj, proj, sinks)


def swa_bwd(q, k, proj, sinks, y, lse, dmix, into, name):
    t = q.shape[0]
    nb = t // ATTN_BLOCK
    scale = HEAD_DIM ** -0.5
    grp = ATTN_HEADS // KV_HEADS

    def body(q_ref, kc_ref, kp_ref, vc_ref, vp_ref, s_ref, y_ref, lse_ref, dy_ref, into_ref,
             dq_ref, dk_ref, dv_ref, ds_ref, dkc, dvc):
        n = pl.program_id(0)

        @pl.when(n == 0)
        def _():
            dkc[...] = jnp.zeros_like(dkc)
            dvc[...] = jnp.zeros_like(dvc)
            ds_ref[...] = jnp.zeros_like(ds_ref)

        @pl.when(n < nb)
        def _():
            valid = _swa_valid(n, grp)
            kk = jnp.concatenate([kp_ref[...], kc_ref[...]], axis=0).astype(MXU_DTYPE)
            vv = jnp.concatenate([vp_ref[...], vc_ref[...]], axis=0).astype(MXU_DTYPE)
            lane = lax.broadcasted_iota(jnp.int32, (1, ATTN_HEADS), 1)
            gs = range(KV_HEADS)
            kg = [kk[:, g * HEAD_DIM:(g + 1) * HEAD_DIM] for g in gs]
            vg = [vv[:, g * HEAD_DIM:(g + 1) * HEAD_DIM] for g in gs]
            qg = [_stack_heads(q_ref, g, grp).astype(MXU_DTYPE) for g in gs]
            dog = [_stack_heads(dy_ref, g, grp) for g in gs]
            og = [_stack_heads(y_ref, g, grp) for g in gs]
            lg = [jnp.concatenate([lse_ref[:, g * grp + j:g * grp + j + 1] for j in range(grp)], axis=0) for g in gs]
            sink = [_stack_sinks(s_ref, g, grp) for g in gs]
            sc = [jnp.where(valid, _mx_nt(qg[g], kg[g]) * scale, NEG) for g in gs]
            p = [jnp.exp(sc[g] - lg[g]) for g in gs]
            delta = [jnp.sum(dog[g] * og[g], axis=-1, keepdims=True) for g in gs]
            ds = [p[g] * (_mx_nt(dog[g], vg[g]) - delta[g]) for g in gs]
            dqg = [_mx(ds[g], kg[g]) * scale for g in gs]
            dkf = jnp.concatenate([_mx_tn(ds[g], qg[g]) * scale for g in gs], axis=-1)
            dvf = jnp.concatenate([_mx_tn(p[g], dog[g]) for g in gs], axis=-1)
            dsk = [jnp.exp(sink[g] - lg[g]) * delta[g] for g in gs]
            dsink = jnp.zeros((1, ATTN_HEADS), F32)
            dqs = []
            for h in range(ATTN_HEADS):
                rows = slice((h % grp) * ATTN_BLOCK, (h % grp + 1) * ATTN_BLOCK)
                dqs.append(dqg[h // grp][rows])
                dsink = jnp.where(lane == h, -jnp.sum(dsk[h // grp][rows], axis=0, keepdims=True), dsink)
            dq_ref[...] = jnp.concatenate(dqs, axis=-1)
            dk_ref[...] = dkc[...] + dkf[:ATTN_BLOCK]
            dv_ref[...] = (dvc[...] + dvf[:ATTN_BLOCK]).astype(dv_ref.dtype)
            dkc[...] = dkf[ATTN_BLOCK:]
            dvc[...] = dvf[ATTN_BLOCK:]
            ds_ref[...] += dsink

        @pl.when(n == nb)
        def _():
            dk_ref[...] = dkc[...]
            dv_ref[...] = dvc[...].astype(dv_ref.dtype)

    cur = lambda n: (jnp.minimum(n, nb - 1), 0)
    prev = lambda n: (jnp.clip(n - 1, 0, nb - 1), 0)
    vcol = (Q_W + KV_W) // KV_W
    return pl.pallas_call(
        body, name=name, grid=(nb + 1,),
        in_specs=[pl.BlockSpec((ATTN_BLOCK, Q_W), cur), pl.BlockSpec((ATTN_BLOCK, KV_W), cur),
                  pl.BlockSpec((ATTN_BLOCK, KV_W), prev),
                  pl.BlockSpec((ATTN_BLOCK, KV_W), lambda n: (jnp.minimum(n, nb - 1), vcol)),
                  pl.BlockSpec((ATTN_BLOCK, KV_W), lambda n: (jnp.clip(n - 1, 0, nb - 1), vcol)),
                  pl.BlockSpec((1, ATTN_HEADS), lambda n: (0, 0)),
                  pl.BlockSpec((ATTN_BLOCK, Q_W), cur), pl.BlockSpec((ATTN_BLOCK, ATTN_HEADS), cur),
                  pl.BlockSpec((ATTN_BLOCK, Q_W), cur), pl.BlockSpec(memory_space=pl.ANY)],
        out_specs=(pl.BlockSpec((ATTN_BLOCK, Q_W), cur), pl.BlockSpec((ATTN_BLOCK, KV_W), prev),
                   pl.BlockSpec((ATTN_BLOCK, KV_W), lambda n: (jnp.clip(n - 1, 0, nb - 1), vcol)),
                   pl.BlockSpec((1, ATTN_HEADS), lambda n: (0, 0))),
        out_shape=(jax.ShapeDtypeStruct((t, Q_W), F32), jax.ShapeDtypeStruct((t, KV_W), F32),
                   jax.ShapeDtypeStruct(into.shape, into.dtype), jax.ShapeDtypeStruct((1, ATTN_HEADS), F32)),
        scratch_shapes=[pltpu.VMEM((ATTN_BLOCK, KV_W), F32), pltpu.VMEM((ATTN_BLOCK, KV_W), F32)],
        input_output_aliases={9: 2},
        compiler_params=_cp("arbitrary"))(q, k, k, proj, proj, sinks, y, lse, dmix, into)


GC_W = 256
_GB0, _GC0, _XI0 = 768 // GC_W, 1280 // GC_W, 1792 // GC_W
HALO = 8


def gconv_fwd(proj, conv_w, mix, mix_t, name, tm=512):
    t = proj.shape[0]
    hb = tm // HALO
    half = Q_W // GC_W

    def body(gb_ref, gc_ref, xi_ref, gch_ref, xih_ref, w_ref, mix_in, mixt_in, y_ref, yt_ref):
        i = pl.program_id(1)
        u = gc_ref[...] * xi_ref[...]
        uh = jnp.where(i == 0, 0.0, gch_ref[...] * xih_ref[...])
        up = jnp.concatenate([uh, u], axis=0)
        cv = w_ref[0:1, :] * up[HALO - 2:HALO - 2 + tm]
        cv = cv + w_ref[1:2, :] * up[HALO - 1:HALO - 1 + tm]
        cv = cv + w_ref[2:3, :] * u
        y = gb_ref[...] * cv
        y_ref[...] = y.astype(y_ref.dtype)
        yt_ref[...] = y.T.astype(yt_ref.dtype)

    def col(c0):
        return pl.BlockSpec((tm, GC_W), lambda cj, i: (i, c0 + cj))

    def halo(c0):
        return pl.BlockSpec((HALO, GC_W), lambda cj, i: (jnp.maximum(i * hb - 1, 0), c0 + cj))

    return pl.pallas_call(
        body, name=name, grid=(CONV_CH // GC_W, t // tm),
        in_specs=[col(_GB0), col(_GC0), col(_XI0), halo(_GC0), halo(_XI0),
                  pl.BlockSpec((3, GC_W), lambda cj, i: (0, cj)),
                  pl.BlockSpec(memory_space=pl.ANY), pl.BlockSpec(memory_space=pl.ANY)],
        out_specs=(pl.BlockSpec((tm, GC_W), lambda cj, i: (i, half + cj)),
                   pl.BlockSpec((GC_W, tm), lambda cj, i: (half + cj, i))),
        out_shape=(jax.ShapeDtypeStruct(mix.shape, mix.dtype), jax.ShapeDtypeStruct(mix_t.shape, mix_t.dtype)),
        input_output_aliases={6: 0, 7: 1},
        compiler_params=_cp("parallel", "parallel"))(proj, proj, proj, proj, proj, conv_w, mix, mix_t)


def gconv_bwd(proj, conv_w, dmix, into, name, tm=512):
    t = proj.shape[0]
    hb = tm // HALO
    nt = t // tm
    dy0 = Q_W // GC_W

    def body(gb_ref, gc_ref, xi_ref, gch_ref, xih_ref, gbn_ref, dyn_ref, dy_ref, w_ref, into_ref, o_ref, dw_ref):
        dgb_ref, dgc_ref, dxi_ref = (o_ref.at[:, j * GC_W:(j + 1) * GC_W] for j in range(3))
        i = pl.program_id(1)
        gc, xi, gb, dy = gc_ref[...], xi_ref[...], gb_ref[...], dy_ref[...]
        u = gc * xi
        uh = jnp.where(i == 0, 0.0, gch_ref[...] * xih_ref[...])
        up = jnp.concatenate([uh, u], axis=0)
        u2 = up[HALO - 2:HALO - 2 + tm]
        u1 = up[HALO - 1:HALO - 1 + tm]
        cv = w_ref[0:1, :] * u2 + w_ref[1:2, :] * u1 + w_ref[2:3, :] * u
        dgb_ref[...] = (dy * cv).astype(dgb_ref.dtype)
        dcv = dy * gb
        dcvn = jnp.where(i == nt - 1, 0.0, dyn_ref[...] * gbn_ref[...])
        dcvp = jnp.concatenate([dcv, dcvn], axis=0)
        du = w_ref[0:1, :] * dcvp[2:2 + tm] + w_ref[1:2, :] * dcvp[1:1 + tm] + w_ref[2:3, :] * dcv
        dgc_ref[...] = (du * xi).astype(dgc_ref.dtype)
        dxi_ref[...] = (du * gc).astype(dxi_ref.dtype)
        dw = jnp.concatenate([jnp.sum(dcv * u2, axis=0, keepdims=True), jnp.sum(dcv * u1, axis=0, keepdims=True),
                              jnp.sum(dcv * u, axis=0, keepdims=True)], axis=0)

        @pl.when(i == 0)
        def _():
            dw_ref[...] = dw

        @pl.when(i > 0)
        def _():
            dw_ref[...] += dw

    def col(c0):
        return pl.BlockSpec((tm, GC_W), lambda cj, i: (i, c0 + cj))

    def halo(c0):
        return pl.BlockSpec((HALO, GC_W), lambda cj, i: (jnp.maximum(i * hb - 1, 0), c0 + cj))

    def nxt(c0):
        return pl.BlockSpec((HALO, GC_W), lambda cj, i: (jnp.minimum((i + 1) * hb, t // HALO - 1), c0 + cj))

    return pl.pallas_call(
        body, name=name, grid=(CONV_CH // GC_W, nt),
        in_specs=[col(_GB0), col(_GC0), col(_XI0), halo(_GC0), halo(_XI0), nxt(_GB0), nxt(dy0), col(dy0),
                  pl.BlockSpec((3, GC_W), lambda cj, i: (0, cj)), pl.BlockSpec(memory_space=pl.ANY)],
        out_specs=(pl.BlockSpec((tm, 3 * GC_W), lambda cj, i: (i, 1 + cj)),
                   pl.BlockSpec((3, GC_W), lambda cj, i: (0, cj))),
        out_shape=(jax.ShapeDtypeStruct(into.shape, into.dtype), jax.ShapeDtypeStruct((3, CONV_CH), F32)),
        input_output_aliases={9: 0},
        compiler_params=_cp("parallel", "arbitrary"))(proj, proj, proj, proj, proj, proj, dmix, dmix, conv_w, into)


_EVEN_D_CHUNKS = ((0, 0, 768),) + tuple(
    (768 + (3 * j + part) * GC_W, 768 + part * CONV_CH + j * GC_W, GC_W)
    for j in range(CONV_CH // GC_W) for part in range(3))


_QKV_W = 3 * DN_W
_BA_COL = (4 * DN_W) // 128
_Z_COL = _QKV_W // DN_W


def gdn_prep_fwd(proj, conv_w, alog_row, dtb_row, name, tm=256):
    t = proj.shape[0]
    hb = tm // HALO
    qscale = DN_DIM ** -0.5

    def body(x_ref, xh_ref, w_ref, ba_ref, al_ref, dt_ref, q_ref, k_ref, v_ref, bg_ref, c_ref):
        i = pl.program_id(0)
        for gi in range(3 * DN_HEADS):
            sl = slice(gi * DN_DIM, (gi + 1) * DN_DIM)
            xp = jnp.concatenate([jnp.where(i == 0, 0.0, xh_ref[:, sl]), x_ref[:, sl]], axis=0)
            c = w_ref[0:1, sl] * xp[HALO - 3:HALO - 3 + tm]
            for j in range(1, 4):
                c = c + w_ref[j:j + 1, sl] * xp[HALO - 3 + j:HALO - 3 + j + tm]
            c_ref[:, sl] = c
            s = c * _sigmoid(c)
            osl = slice((gi % DN_HEADS) * DN_DIM, (gi % DN_HEADS + 1) * DN_DIM)
            if gi < DN_HEADS:
                q_ref[:, osl] = s * lax.rsqrt(jnp.sum(s * s, axis=-1, keepdims=True) + EPS) * qscale
            elif gi < 2 * DN_HEADS:
                k_ref[:, osl] = s * lax.rsqrt(jnp.sum(s * s, axis=-1, keepdims=True) + EPS)
            else:
                v_ref[:, osl] = s
        ba = ba_ref[...]
        lane = lax.broadcasted_iota(jnp.int32, ba.shape, 1)
        gval = -jnp.exp(al_ref[...]) * _softplus(ba + dt_ref[...])
        bg_ref[...] = jnp.where(lane < DN_HEADS, _sigmoid(ba), jnp.where(lane < 2 * DN_HEADS, gval, 0.0))

    row = pl.BlockSpec((tm, DN_W), lambda i: (i, 0))
    one = pl.BlockSpec((1, 128), lambda i: (0, 0))
    return pl.pallas_call(
        body, name=name, grid=(t // tm,),
        in_specs=[pl.BlockSpec((tm, _QKV_W), lambda i: (i, 0)),
                  pl.BlockSpec((HALO, _QKV_W), lambda i: (jnp.maximum(i * hb - 1, 0), 0)),
                  pl.BlockSpec((4, _QKV_W), lambda i: (0, 0)),
                  pl.BlockSpec((tm, 128), lambda i: (i, _BA_COL)), one, one],
        out_specs=(row, row, row, pl.BlockSpec((tm, 128), lambda i: (i, 0)), pl.BlockSpec((tm, _QKV_W), lambda i: (i, 0))),
        out_shape=(jax.ShapeDtypeStruct((t, DN_W), F32),) * 3 + (jax.ShapeDtypeStruct((t, 128), F32),
                                                                 jax.ShapeDtypeStruct((t, _QKV_W), F32)),
        compiler_params=_cp("parallel"))(proj, proj, conv_w, proj, alog_row, dtb_row)


def gdn_prep_bwd(proj, conv, conv_w, alog_row, dtb_row, dq, dk, dv, dbg, dz, name, tm=256):
    t = proj.shape[0]
    hb = tm // HALO
    nt = t // tm
    qscale = DN_DIM ** -0.5
    te = tm + HALO

    def body(x_ref, c_ref, cn_ref, w_ref, ba_ref, al_ref, dt_ref, dq_ref, dk_ref, dv_ref,
             dqn_ref, dkn_ref, dvn_ref, dbg_ref, dz_ref, dx_ref, dw_ref, ddt_ref, dal_ref):
        i = pl.program_id(0)
        first = i == 0
        last = i == nt - 1
        dws = []
        for gi in range(3 * DN_HEADS):
            sl = slice(gi * DN_DIM, (gi + 1) * DN_DIM)
            osl = slice((gi % DN_HEADS) * DN_DIM, (gi % DN_HEADS + 1) * DN_DIM)
            c = jnp.concatenate([c_ref[:, sl], cn_ref[:, sl]], axis=0)
            sg = _sigmoid(c)
            s = c * sg
            d_ref, dn_ref = ((dq_ref, dqn_ref), (dk_ref, dkn_ref), (dv_ref, dvn_ref))[gi // DN_HEADS]
            dy = jnp.concatenate([d_ref[:, osl], jnp.where(last, 0.0, dn_ref[:, osl])], axis=0)
            if gi < 2 * DN_HEADS:
                r = lax.rsqrt(jnp.sum(s * s, axis=-1, keepdims=True) + EPS)
                sh = s * r
                ds = r * (dy - sh * jnp.sum(sh * dy, axis=-1, keepdims=True))
                if gi < DN_HEADS:
                    ds = ds * qscale
            else:
                ds = dy
            dc = ds * sg * (1.0 + c * (1.0 - sg))
            dcs = [dc[3 - j:3 - j + tm] for j in range(4)]
            dx = w_ref[0:1, sl] * dcs[0]
            for j in range(1, 4):
                dx = dx + w_ref[j:j + 1, sl] * dcs[j]
            dx_ref[:, sl] = dx.astype(dx_ref.dtype)
            x0 = x_ref[:, sl]
            dws.append(jnp.concatenate([jnp.sum(dcs[j] * x0, axis=0, keepdims=True) for j in range(4)], axis=0))
        dw = jnp.concatenate(dws, axis=-1)
        ba = ba_ref[...]
        dbgv = dbg_ref[...]
        lane = lax.broadcasted_iota(jnp.int32, ba.shape, 1)
        beta = _sigmoid(ba)
        ea = -jnp.exp(al_ref[...])
        zin = ba + dt_ref[...]
        is_b = lane < DN_HEADS
        is_a = (lane >= DN_HEADS) & (lane < 2 * DN_HEADS)
        da = jnp.where(is_a, dbgv * ea * _sigmoid(zin), 0.0)
        dx_ref[:, _QKV_W:_QKV_W + DN_W] = dz_ref[...]
        dx_ref[:, _QKV_W + DN_W:] = jnp.where(is_b, dbgv * beta * (1.0 - beta), da).astype(dx_ref.dtype)
        ddt = jnp.sum(da, axis=0, keepdims=True)
        dal = jnp.sum(jnp.where(is_a, dbgv * ea * _softplus(zin), 0.0), axis=0, keepdims=True)

        @pl.when(first)
        def _():
            dw_ref[...] = dw
            ddt_ref[...] = ddt
            dal_ref[...] = dal

        @pl.when(i > 0)
        def _():
            dw_ref[...] += dw
            ddt_ref[...] += ddt
            dal_ref[...] += dal

    row = pl.BlockSpec((tm, DN_W), lambda i: (i, 0))
    nrow = pl.BlockSpec((HALO, DN_W), lambda i: (jnp.minimum((i + 1) * hb, t // HALO - 1), 0))
    one = pl.BlockSpec((1, 128), lambda i: (0, 0))
    return pl.pallas_call(
        body, name=name, grid=(nt,),
        in_specs=[pl.BlockSpec((tm, _QKV_W), lambda i: (i, 0)), pl.BlockSpec((tm, _QKV_W), lambda i: (i, 0)),
                  pl.BlockSpec((HALO, _QKV_W), lambda i: (jnp.minimum((i + 1) * hb, t // HALO - 1), 0)),
                  pl.BlockSpec((4, _QKV_W), lambda i: (0, 0)),
                  pl.BlockSpec((tm, 128), lambda i: (i, _BA_COL)), one, one,
                  row, row, row, nrow, nrow, nrow, pl.BlockSpec((tm, 128), lambda i: (i, 0)), row],
        out_specs=(pl.BlockSpec((tm, ODD_IN_PAD), lambda i: (i, 0)), pl.BlockSpec((4, _QKV_W), lambda i: (0, 0)), one, one),
        out_shape=(jax.ShapeDtypeStruct((t, ODD_IN_PAD), MXU_DTYPE), jax.ShapeDtypeStruct((4, _QKV_W), F32),
                   jax.ShapeDtypeStruct((1, 128), F32), jax.ShapeDtypeStruct((1, 128), F32)),
        compiler_params=_cp("arbitrary"))(proj, conv, conv, conv_w, proj, alog_row, dtb_row, dq, dk, dv, dq, dk, dv, dbg,
                                          dz)


def _chunk_masks():
    r = lax.broadcasted_iota(jnp.int32, (DN_CHUNK, DN_CHUNK), 0)
    c = lax.broadcasted_iota(jnp.int32, (DN_CHUNK, DN_CHUNK), 1)
    return r >= c, r > c


INV_PACK = 2


def _inv_unit_lower_many(mats):
    n = DN_CHUNK
    wide = INV_PACK * n
    r = lax.broadcasted_iota(jnp.int32, (wide, wide), 0)
    c = lax.broadcasted_iota(jnp.int32, (wide, wide), 1)
    same = (r & -n) == (c & -n)
    eye = jnp.where((r[:n] == (c[:n] & (n - 1))), 1.0, 0.0)

    def blockdiag(row):
        return jnp.where(same, jnp.concatenate([row] * INV_PACK, axis=0), 0.0)

    packs = [jnp.concatenate(mats[g:g + INV_PACK], axis=-1) for g in range(0, len(mats), INV_PACK)]
    xs = [eye - a for a in packs]
    pws = [_hi(a, blockdiag(a)) for a in packs]
    for step in range(5):
        if step < 4:
            both = [_hi(jnp.concatenate([x, pw], axis=0), blockdiag(pw)) for x, pw in zip(xs, pws)]
            xs = [x + b[:n] for x, b in zip(xs, both)]
            pws = [b[n:] for b in both]
        else:
            xs = [x + _hi(x, blockdiag(pw)) for x, pw in zip(xs, pws)]
    return [x[:, j * n:(j + 1) * n] for x in xs for j in range(INV_PACK)]


def _chunk_common(q, k, beta, gc, gcr, lower):
    gam = jnp.exp(jnp.where(lower, gc - gcr, NEG))
    eg = jnp.exp(gc)
    gl = gc[DN_CHUNK - 1:DN_CHUNK, :]
    kdf = jnp.exp(gl - gc)
    kb = k * beta
    bmat = _mx_nt(kb, k)
    qmat = _mx_nt(q, k)
    return gam, eg, jnp.exp(gl), kdf, kb, bmat, qmat


DN_STEP = 4


def gdn_fwd(q, k, v, bg, proj, o_gain, name):
    t = q.shape[0]
    n_chunks = t // DN_CHUNK

    def body(q_ref, k_ref, v_ref, bg_ref, z_ref, g_ref, o_ref, sall_ref, tall_ref, y_ref, yt_ref, s_ref):
        n = pl.program_id(0)

        @pl.when(n == 0)
        def _():
            s_ref[...] = jnp.zeros_like(s_ref)

        lower, strict = _chunk_masks()
        ltri = jnp.where(lower, 1.0, 0.0)
        hs = range(DN_HEADS)
        sl = [slice(h * DN_DIM, (h + 1) * DN_DIM) for h in hs]
        units = [(c, h) for c in range(DN_STEP) for h in hs]
        nu = range(len(units))
        rs = [slice(c * DN_CHUNK, (c + 1) * DN_CHUNK) for c in range(DN_STEP)]
        bgv = [bg_ref[rs[c], :] for c in range(DN_STEP)]
        gcs = [_hi(ltri, b) for b in bgv]
        gcs_t = [g.T for g in gcs]
        qh = [q_ref[rs[c], sl[h]] for c, h in units]
        kh = [k_ref[rs[c], sl[h]] for c, h in units]
        vh = [v_ref[rs[c], sl[h]] for c, h in units]
        beta = [bgv[c][:, h:h + 1] for c, h in units]
        com = [_chunk_common(qh[u], kh[u], beta[u], gcs[c][:, DN_HEADS + h:DN_HEADS + h + 1],
                             gcs_t[c][DN_HEADS + h:DN_HEADS + h + 1, :], lower) for u, (c, h) in enumerate(units)]
        gam, eg, dec, kdf, kb, bmat, qmat = zip(*com)
        tms = _inv_unit_lower_many([jnp.where(strict, bmat[u] * gam[u], 0.0) for u in nu])
        for u, (c, h) in enumerate(units):
            tall_ref[c, h] = tms[u]
        uw = [_hi(tms[u], jnp.concatenate([vh[u] * beta[u], kb[u] * eg[u]], axis=-1)) for u in nu]
        qd = [qh[u] * eg[u] for u in nu]
        pm = [qmat[u] * gam[u] for u in nu]
        kd = [kh[u] * kdf[u] for u in nu]
        st = [s_ref[h] for h in hs]
        for c in range(DN_STEP):
            us = [c * DN_HEADS + h for h in hs]
            for h in hs:
                sall_ref[c, h] = st[h]
            v_new = [uw[us[h]][:, :DN_DIM] - _mx(uw[us[h]][:, DN_DIM:], st[h]) for h in hs]
            o_st = [_mx(qd[us[h]], st[h]) for h in hs]
            o_in = [_mx(pm[us[h]], v_new[h]) for h in hs]
            s_up = [_mx_tn(kd[us[h]], v_new[h]) for h in hs]
            for h in hs:
                ov = o_st[h] + o_in[h]
                o_ref[rs[c], sl[h]] = ov
                zv = z_ref[rs[c], sl[h]]
                y = ov * lax.rsqrt(jnp.mean(ov * ov, axis=-1, keepdims=True) + EPS) * g_ref[...] * (zv * _sigmoid(zv))
                y_ref[rs[c], sl[h]] = y.astype(y_ref.dtype)
                yt_ref[sl[h], rs[c]] = y.T.astype(yt_ref.dtype)
            st = [st[h] * dec[us[h]] + s_up[h] for h in hs]
        for h in hs:
            s_ref[h] = st[h]

    rows = DN_STEP * DN_CHUNK
    row = pl.BlockSpec((rows, DN_W), lambda n: (n, 0))
    return pl.pallas_call(
        body, name=name, grid=(n_chunks // DN_STEP,),
        in_specs=[row, row, row, pl.BlockSpec((rows, 128), lambda n: (n, 0)),
                  pl.BlockSpec((rows, DN_W), lambda n: (n, _Z_COL)), pl.BlockSpec((1, DN_DIM), lambda n: (0, 0))],
        out_specs=(row, pl.BlockSpec((DN_STEP, DN_HEADS, DN_DIM, DN_DIM), lambda n: (n, 0, 0, 0)),
                   pl.BlockSpec((DN_STEP, DN_HEADS, DN_CHUNK, DN_CHUNK), lambda n: (n, 0, 0, 0)),
                   row, pl.BlockSpec((DN_W, rows), lambda n: (0, n))),
        out_shape=(jax.ShapeDtypeStruct((t, DN_W), F32),
                   jax.ShapeDtypeStruct((n_chunks, DN_HEADS, DN_DIM, DN_DIM), F32),
                   jax.ShapeDtypeStruct((n_chunks, DN_HEADS, DN_CHUNK, DN_CHUNK), F32),
                   jax.ShapeDtypeStruct((t, DN_W), MXU_DTYPE), jax.ShapeDtypeStruct((DN_W, t), MXU_DTYPE)),
        scratch_shapes=[pltpu.VMEM((DN_HEADS, DN_DIM, DN_DIM), F32)],
        compiler_params=_cp("arbitrary"))(q, k, v, bg, proj, o_gain)


def gdn_bwd(q, k, v, bg, sall, tall, do, name):
    t = q.shape[0]
    n_chunks = t // DN_CHUNK

    def body(q_ref, k_ref, v_ref, bg_ref, sall_ref, tall_ref, do_ref, dq_ref, dk_ref, dv_ref, dbg_ref, ds_ref):
        n = pl.program_id(0)

        @pl.when(n == 0)
        def _():
            ds_ref[...] = jnp.zeros_like(ds_ref)

        lower, strict = _chunk_masks()
        ltri = jnp.where(lower, 1.0, 0.0)
        bgv = bg_ref[...]
        gcs = _hi(ltri, bgv)
        gcs_t = gcs.T
        lane = lax.broadcasted_iota(jnp.int32, (DN_CHUNK, 128), 1)
        rowi = lax.broadcasted_iota(jnp.int32, (DN_CHUNK, 1), 0)
        hs = range(DN_HEADS)
        each = lambda fn, *ls: [fn(*a) for a in zip(*ls)]
        rsum = lambda a: jnp.sum(a, axis=-1, keepdims=True)
        sl = [slice(h * DN_DIM, (h + 1) * DN_DIM) for h in hs]
        st = [sall_ref[0, h] for h in hs]
        tms = [tall_ref[0, h] for h in hs]
        dsn = [ds_ref[h] for h in hs]
        qh = [q_ref[:, sl[h]] for h in hs]
        kh = [k_ref[:, sl[h]] for h in hs]
        vh = [v_ref[:, sl[h]] for h in hs]
        doh = [do_ref[:, sl[h]] for h in hs]
        beta = [bgv[:, h:h + 1] for h in hs]
        com = [_chunk_common(qh[h], kh[h], beta[h], gcs[:, DN_HEADS + h:DN_HEADS + h + 1],
                             gcs_t[DN_HEADS + h:DN_HEADS + h + 1, :], lower) for h in hs]
        gam, eg, dec, kdf, kb, bmat, qmat = zip(*com)
        rhs_w = each(lambda a, b: a * b, kb, eg)
        uw = each(lambda t_, v_, b_, r_: _hi(t_, jnp.concatenate([v_ * b_, r_], axis=-1)), tms, vh, beta, rhs_w)
        qd = each(lambda a, b: a * b, qh, eg)
        kd = each(lambda a, b: a * b, kh, kdf)
        pmat = each(lambda a, b: a * b, qmat, gam)
        v_new = each(lambda uw_, s_: uw_[:, :DN_DIM] - _mx(uw_[:, DN_DIM:], s_), uw, st)
        dqd = each(_mx_nt, doh, st)
        ds_o = each(_mx_tn, qd, doh)
        dp = each(lambda d_, v_: jnp.where(lower, _mx_nt(d_, v_), 0.0), doh, v_new)
        dvn_o = each(_mx_tn, pmat, doh)
        ddec = each(lambda d_, s_: jnp.sum(rsum(d_ * s_), axis=0, keepdims=True), dsn, st)
        dkd = each(_mx_nt, v_new, dsn)
        dvn = each(lambda a, k_, d_: a + _mx(k_, d_), dvn_o, kd, dsn)
        dw = each(lambda d_, s_: -_mx_nt(d_, s_), dvn, st)
        ds_w = each(lambda uw_, d_: _mx_tn(uw_[:, DN_DIM:], d_), uw, dvn)
        for h in hs:
            ds_ref[h] = ds_o[h] + dec[h] * dsn[h] - ds_w[h]
        dr = each(lambda t_, a, b: _hi_tn(t_, jnp.concatenate([a, b], axis=-1)), tms, dvn, dw)
        da = each(lambda r_, uw_: jnp.where(strict, -_hi_nt(r_, uw_), 0.0), dr, uw)
        dru = [r_[:, :DN_DIM] for r_ in dr]
        drw = [r_[:, DN_DIM:] for r_ in dr]
        db = each(lambda a, b: a * b, da, gam)
        dq_m = each(lambda a, b: a * b, dp, gam)
        e = each(lambda a, bm, p_, qm, g_: (a * bm + p_ * qm) * g_, da, bmat, dp, qmat, gam)
        dkb = each(lambda b_, k_, r_, e_: _mx(b_, k_) + r_ * e_, db, kh, drw, eg)
        dk = each(lambda b_, kb_, m_, q_, d_, f_: _mx_tn(b_, kb_) + _mx_tn(m_, q_) + d_ * f_, db, kb, dq_m, qh, dkd, kdf)
        dq = each(lambda m_, k_, d_, e_: _mx(m_, k_) + d_ * e_, dq_m, kh, dqd, eg)
        tk = each(lambda a, b: rsum(a * b), dkd, kd)
        dbeta_all = jnp.zeros((DN_CHUNK, 128), F32)
        dgc_all = jnp.zeros((DN_CHUNK, 128), F32)
        for h in hs:
            dgc = (jnp.sum(e[h], axis=1, keepdims=True) - jnp.sum(e[h].T, axis=1, keepdims=True)
                   + rsum(dqd[h] * qd[h]) - tk[h] + rsum(drw[h] * rhs_w[h]))
            dgl = jnp.sum(tk[h], axis=0, keepdims=True) + ddec[h] * dec[h]
            dgc = dgc + jnp.where(rowi == DN_CHUNK - 1, dgl, 0.0)
            dbeta = rsum(dru[h] * vh[h]) + rsum(dkb[h] * kh[h])
            dq_ref[:, sl[h]] = dq[h]
            dk_ref[:, sl[h]] = dk[h] + dkb[h] * beta[h]
            dv_ref[:, sl[h]] = dru[h] * beta[h]
            dbeta_all = jnp.where(lane == h, dbeta, dbeta_all)
            dgc_all = jnp.where(lane == DN_HEADS + h, dgc, dgc_all)
        dbg_ref[...] = dbeta_all + _hi_tn(ltri, dgc_all)

    rev = lambda n: (n_chunks - 1 - n, 0)
    row = pl.BlockSpec((DN_CHUNK, DN_W), rev)
    small = pl.BlockSpec((DN_CHUNK, 128), rev)
    return pl.pallas_call(
        body, name=name, grid=(n_chunks,),
        in_specs=[row, row, row, small,
                  pl.BlockSpec((1, DN_HEADS, DN_DIM, DN_DIM), lambda n: (n_chunks - 1 - n, 0, 0, 0)),
                  pl.BlockSpec((1, DN_HEADS, DN_CHUNK, DN_CHUNK), lambda n: (n_chunks - 1 - n, 0, 0, 0)), row],
        out_specs=(row, row, row, small),
        out_shape=(jax.ShapeDtypeStruct((t, DN_W), F32),) * 3 + (jax.ShapeDtypeStruct((t, 128), F32),),
        scratch_shapes=[pltpu.VMEM((DN_HEADS, DN_DIM, DN_DIM), F32)],
        compiler_params=_cp("arbitrary"))(q, k, v, bg, sall, tall, do)


def gdn_out_bwd(o, proj, o_gain, dx, w_out, name, tm=512, after=None):
    t = o.shape[0]
    tm = min(tm, t)

    def body(o_ref, z_ref, g_ref, dx_ref, w_ref, *rest):
        do_ref, dz_ref, dg_ref = rest[-3:]
        i = pl.program_id(0)
        dy = _mx_nt(dx_ref[...], w_ref[...])
        dg = jnp.zeros((1, DN_DIM), F32)
        for h in range(DN_HEADS):
            sl = slice(h * DN_DIM, (h + 1) * DN_DIM)
            ov, zv, dyv = o_ref[:, sl], z_ref[:, sl], dy[:, sl]
            r = lax.rsqrt(jnp.mean(ov * ov, axis=-1, keepdims=True) + EPS)
            oh = ov * r
            sg = _sigmoid(zv)
            dz_ref[:, sl] = (dyv * oh * g_ref[...] * sg * (1.0 + zv * (1.0 - sg))).astype(dz_ref.dtype)
            don = dyv * (zv * sg)
            dg = dg + jnp.sum(don * oh, axis=0, keepdims=True)
            doh = don * g_ref[...]
            do_ref[:, sl] = r * (doh - oh * jnp.mean(doh * oh, axis=-1, keepdims=True))

        @pl.when(i == 0)
        def _():
            dg_ref[...] = dg

        @pl.when(i > 0)
        def _():
            dg_ref[...] += dg

    row = pl.BlockSpec((tm, DN_W), lambda i: (i, 0))
    one = pl.BlockSpec((1, DN_DIM), lambda i: (0, 0))
    in_specs = [row, pl.BlockSpec((tm, DN_W), lambda i: (i, _Z_COL)), one,
                pl.BlockSpec((tm, dx.shape[1]), lambda i: (i, 0)), pl.BlockSpec(w_out.shape, lambda i: (0, 0))]
    args = [o, proj, o_gain, dx, w_out]
    if after is not None:
        in_specs.append(pl.BlockSpec(memory_space=pl.ANY))
        args.append(after)
    return pl.pallas_call(
        body, name=name, grid=(t // tm,), in_specs=in_specs, out_specs=(row, row, one),
        out_shape=(jax.ShapeDtypeStruct((t, DN_W), F32), jax.ShapeDtypeStruct((t, DN_W), MXU_DTYPE),
                   jax.ShapeDtypeStruct((1, DN_DIM), F32)),
        compiler_params=_cp("arbitrary"))(*args)


def _peer(k):
    x, y, c = lax.axis_index("x"), lax.axis_index("y"), lax.axis_index("c")
    px = 1 - x if k & 4 else x
    py = 1 - y if k & 2 else y
    pc = 1 - c if k & 1 else c
    return (px, py, pc), 4 * px + 2 * py + pc


_HBM = pl.BlockSpec(memory_space=pltpu.HBM)
_SEM = pl.BlockSpec(memory_space=pltpu.SEMAPHORE)
_DATAFLOW = pltpu.SideEffectType.DATAFLOW_SIDE_EFFECTING
N_PEER = N_DEV - 1


def join_blocks(parts, rows, name, tc=256):
    n, r, c = parts.shape

    def body(p_ref, o_ref):
        for j in range(n):
            o_ref[j * r:(j + 1) * r, :] = p_ref[j]
        if rows > n * r:
            o_ref[n * r:, :] = jnp.zeros((rows - n * r, tc), o_ref.dtype)

    return pl.pallas_call(
        body, name=name, grid=(c // tc,), in_specs=[pl.BlockSpec((n, r, tc), lambda i: (0, 0, i))],
        out_specs=pl.BlockSpec((rows, tc), lambda i: (0, i)), out_shape=jax.ShapeDtypeStruct((rows, c), parts.dtype),
        compiler_params=_cp("parallel"))(parts)


def split_blocks(whole, n, r, name, tc=256):
    rows, c = whole.shape

    def body(w_ref, o_ref):
        for j in range(n):
            o_ref[j] = w_ref[j * r:(j + 1) * r, :]

    return pl.pallas_call(
        body, name=name, grid=(c // tc,), in_specs=[pl.BlockSpec((rows, tc), lambda i: (0, i))],
        out_specs=pl.BlockSpec((n, r, tc), lambda i: (0, 0, i)), out_shape=jax.ShapeDtypeStruct((n, r, c), whole.dtype),
        compiler_params=_cp("parallel"))(whole)


def send_start(srcs, name, scatter, after):
    na = len(srcs)
    ns = (2 * N_PEER + 1) * na
    lands = [lax.empty((N_DEV,) + (s.shape[1:] if scatter else s.shape), s.dtype) for s in srcs]
    extra = [] if after is None else [after]

    def body(*refs):
        src_refs, land_refs = refs[:na], refs[na:2 * na]
        sems = refs[2 * na + len(extra):2 * na + len(extra) + ns]
        land_out, token = refs[-1 - na:-1], refs[-1]
        _, me = _peer(0)
        for a in range(na):
            pltpu.make_async_copy(src_refs[a].at[me] if scatter else src_refs[a], land_out[a].at[me],
                                  sems[2 * N_PEER * na + a]).start()
        peers = [_peer(k) for k in range(1, N_DEV)]
        for a in range(na):
            for k, (peer, pid) in enumerate(peers):
                pltpu.make_async_remote_copy(
                    src_ref=src_refs[a].at[pid] if scatter else src_refs[a], dst_ref=land_refs[a].at[me],
                    send_sem=sems[2 * (a * N_PEER + k)], recv_sem=sems[2 * (a * N_PEER + k) + 1],
                    device_id=peer, device_id_type=MESH).start()
        token[...] = jnp.zeros_like(token)

    hbm = lambda arrs: tuple(pltpu.HBM(a.shape, a.dtype) for a in arrs)
    outs = pl.pallas_call(
        body, name=name,
        out_shape=(pltpu.SemaphoreType.DMA(()),) * ns + hbm(srcs) + hbm(lands) + (jax.ShapeDtypeStruct((8, 128), F32),),
        in_specs=[_HBM] * (2 * na) + [pl.BlockSpec(memory_space=pl.ANY)] * len(extra),
        out_specs=(_SEM,) * ns + (_HBM,) * (2 * na) + (pl.BlockSpec(memory_space=pltpu.VMEM),),
        input_output_aliases={i: ns + i for i in range(2 * na)},
        compiler_params=pltpu.CompilerParams(has_side_effects=_DATAFLOW),
    )(*[pltpu.with_memory_space_constraint(a, pltpu.HBM) for a in list(srcs) + lands], *extra)
    return outs[:ns], outs[ns:ns + na], outs[ns + na:ns + 2 * na], outs[-1]


def send_wait(sems, srcs_thru, lands_thru, name, scatter, after):
    na = len(srcs_thru)
    ns = (2 * N_PEER + 1) * na

    def body(*refs):
        src_refs, land_refs, sm = refs[:na], refs[na:2 * na], refs[2 * na:2 * na + ns]
        _, me = _peer(0)
        for a in range(na):
            pltpu.make_async_copy(src_refs[a].at[me] if scatter else src_refs[a], land_refs[a].at[me],
                                  sm[2 * N_PEER * na + a]).wait()
        for k in range(1, N_DEV):
            peer, pid = _peer(k)
            for a in range(na):
                cp = pltpu.make_async_remote_copy(
                    src_ref=src_refs[a].at[pid] if scatter else src_refs[a], dst_ref=land_refs[a].at[pid],
                    send_sem=sm[2 * (a * N_PEER + k - 1)], recv_sem=sm[2 * (a * N_PEER + k - 1) + 1],
                    device_id=peer, device_id_type=MESH)
                cp.wait_send()
                cp.wait_recv()

    hbm = lambda arrs: tuple(pltpu.HBM(a.shape, a.dtype) for a in arrs)
    outs = pl.pallas_call(
        body, name=name, out_shape=hbm(srcs_thru) + hbm(lands_thru),
        in_specs=[_HBM] * (2 * na) + [_SEM] * ns + [pl.BlockSpec(memory_space=pl.ANY)], out_specs=(_HBM,) * (2 * na),
        input_output_aliases={i: i for i in range(2 * na)},
        compiler_params=pltpu.CompilerParams(has_side_effects=_DATAFLOW),
    )(*srcs_thru, *lands_thru, *sems, after)
    return outs[na:]


def _adamw(w, g, m, v):
    m = ADAM_B1 * m + (1.0 - ADAM_B1) * g
    v = ADAM_B2 * v + (1.0 - ADAM_B2) * (g * g)
    m_hat = m / (1.0 - ADAM_B1 ** ADAM_STEP)
    v_hat = v / (1.0 - ADAM_B2 ** ADAM_STEP)
    return -ADAM_LR * (m_hat / (jnp.sqrt(v_hat) + ADAM_EPS) + ADAM_WD * w), m, v


def adam_sum(w, pieces, m, v, name, layer=0, into=None):
    nl, r, c = w.shape
    tc = _pick(c, 256)

    def body(w_ref, p_ref, m_ref, v_ref, *rest):
        g_ref, d_ref, nm_ref, nv_ref = rest[-4:]
        g = p_ref[0].astype(F32)
        for s in range(1, N_DEV):
            g = g + p_ref[s].astype(F32)
        g_ref[0] = g
        d_ref[0], nm_ref[0], nv_ref[0] = _adamw(w_ref[0], g, m_ref[0], v_ref[0])

    row = pl.BlockSpec((1, r, tc), lambda i: (layer, 0, i))
    out = jax.ShapeDtypeStruct((nl, r, c), F32)
    extra = [] if into is None else list(into)
    return pl.pallas_call(
        body, name=name, grid=(c // tc,),
        in_specs=[row, pl.BlockSpec((N_DEV, r, tc), lambda i: (0, 0, i)), row, row]
        + [pl.BlockSpec(memory_space=pl.ANY)] * len(extra),
        out_specs=(row,) * 4, out_shape=(out,) * 4,
        input_output_aliases={4 + i: i for i in range(len(extra))},
        compiler_params=_cp("parallel"))(w, pieces, m, v, *extra)


def sum_rows(gathered, name):
    _, r, c = gathered.shape

    def body(p_ref, o_ref):
        g = p_ref[0]
        for s in range(1, N_DEV):
            g = g + p_ref[s]
        o_ref[...] = g

    return pl.pallas_call(body, name=name, out_shape=jax.ShapeDtypeStruct((r, c), F32))(gathered)


def adam_small(w, g, m, v, name):
    def body(w_ref, g_ref, m_ref, v_ref, d_ref, nm_ref, nv_ref):
        d_ref[...], nm_ref[...], nv_ref[...] = _adamw(w_ref[...], g_ref[...], m_ref[...], v_ref[...])

    out = jax.ShapeDtypeStruct(w.shape, F32)
    return pl.pallas_call(body, name=name, out_shape=(out,) * 3)(w, g, m, v)


def _rope_tables(t):
    inv_freq = 10000.0 ** (-jnp.arange(0, HEAD_DIM, 2, dtype=F32) / HEAD_DIM)
    lanes = lambda a: jnp.concatenate([a] * (128 // a.shape[-1]), axis=-1)
    base = (jnp.arange(t // QK_TM, dtype=F32) * QK_TM)[:, None] * inv_freq[None, :]
    offs = jnp.arange(QK_TM, dtype=F32)[:, None] * inv_freq[None, :]
    return (lanes(jnp.cos(base))[:, None, :], lanes(jnp.sin(base))[:, None, :], lanes(jnp.cos(offs)), lanes(jnp.sin(offs)))


def _lane_row(vec8):
    return jnp.pad(vec8.reshape(1, DN_HEADS), ((0, 0), (DN_HEADS, 128 - 2 * DN_HEADS)))


def _ffn_bwd(x, norm_g, w_gu, w_d, saved, dy, tag, after=None):
    ft, gu, at = saved
    dgu = ffn_dact(dy, w_d, gu, f"{tag}_d_gate_up", after=after)
    dwd = mm_at(at, dy, f"{tag}_dw_down")
    nj = D_FF // GU_TILE
    dwgu = mm_at(ft, dgu, f"{tag}_dw_gate_up", transposed=True, tn=GU_TILE, row_block=lambda q: (q % 2) * nj + q // 2)
    dx, dg = mm_rms_bwd(dgu, w_gu, x, norm_g, dy, f"{tag}_d_norm", chunks=_GU_CHUNKS)
    return dx, dwgu, dwd, dg


def local_step(x, target, small, weights_of, grads_out, after=None):
    t = x.shape[0]
    rope = _rope_tables(t)
    alog_row, dtb_row = _lane_row(small["odd_a_log"]), _lane_row(small["odd_dt_bias"])

    h0, h0t = rms_fwd(x, small["even_norm"], "even_norm", after=after)
    we = weights_of("even", h0)
    small = {**small, **we.get("small", {})}
    proj0 = mm_nt(h0, we["w_in"], "even_in_proj")
    qr, kr = qk_prep_fwd(proj0, small["even_q_gain"], small["even_k_gain"], rope, "even_qk_prep")
    y_attn, mix0, mix0_t, lse = swa_fwd(qr, kr, proj0, small["even_sinks"], "even_swa")
    mix0, mix0_t = gconv_fwd(proj0, small["even_conv_w"], mix0, mix0_t, "even_gconv")
    we = {**we, **weights_of("even_out", mix0)}
    x1, f0, f0t = mm_nn_res_norm(mix0, we["w_out"], x, small["ffn_norm0"], "even_out_proj")
    w0 = weights_of("ffn0", x1)
    gu0, a0, a0t = ffn_up(f0, w0["gate_up"], "ffn0_gate_up")
    ffn0 = (f0t, gu0, a0t)
    x2, h1, h1t = mm_nn_res_norm(a0, w0["down"], x1, small["odd_norm"], "ffn0_down")

    wo = weights_of("odd", x2)
    proj1 = mm_nt(h1, wo["w_in"], "odd_in_proj")
    qn, kn, vs, bg, conv1 = gdn_prep_fwd(proj1, small["odd_conv_w"], alog_row, dtb_row, "odd_prep")
    o, sall, tall, og, ogt = gdn_fwd(qn, kn, vs, bg, proj1, small["odd_o_gain"], "odd_delta_rule")
    x3, f1, f1t = mm_nn_res_norm(og, wo["w_out"], x2, small["ffn_norm1"], "odd_out_proj")
    w1 = weights_of("ffn1", x3)
    gu1, a1, a1t = ffn_up(f1, w1["gate_up"], "ffn1_gate_up")
    ffn1 = (f1t, gu1, a1t)
    dy, loss_row = mm_nn_res_loss(a1, w1["down"], x3, target, "ffn1_down_loss")

    gs = {}
    dx3, dwgu, dwd, gs["ffn_norm1"] = _ffn_bwd(x3, small["ffn_norm1"], w1["gate_up"], w1["down"], ffn1, dy, "ffn1")
    tok = grads_out("ffn1", {"gate_up": dwgu, "down": dwd})

    do, dz, gs["odd_o_gain"] = gdn_out_bwd(o, proj1, small["odd_o_gain"], dx3, wo["w_out"], "odd_d_gate_norm", after=tok)
    dwo = mm_at(ogt, dx3, "odd_dw_out")
    dqn, dkn, dvs, dbg = gdn_bwd(qn, kn, vs, bg, sall, tall, do, "odd_d_delta_rule")
    dproj1, gs["odd_conv_w"], ddt_row, dal_row = gdn_prep_bwd(
        proj1, conv1, small["odd_conv_w"], alog_row, dtb_row, dqn, dkn, dvs, dbg, dz, "odd_d_prep")
    gs["odd_dt_bias"] = ddt_row[:, DN_HEADS:2 * DN_HEADS]
    gs["odd_a_log"] = dal_row[:, DN_HEADS:2 * DN_HEADS]
    dwi = mm_at(h1t, dproj1, "odd_dw_in", transposed=True)
    dx2, gs["odd_norm"] = mm_rms_bwd(dproj1, wo["w_in"], x2, small["odd_norm"], dx3, "odd_d_norm")
    tok = grads_out("odd", {"w_in": dwi, "w_out": dwo})

    dx1, dwgu, dwd, gs["ffn_norm0"] = _ffn_bwd(x1, small["ffn_norm0"], w0["gate_up"], w0["down"], ffn0, dx2, "ffn0",
                                               after=tok)
    tok = grads_out("ffn0", {"gate_up": dwgu, "down": dwd})

    dmix = mm_nt(dx1, we["w_out"], "even_d_mix", after=tok)
    dwo = mm_at(mix0_t, dx1, "even_dw_out")
    dproj0 = lax.empty((t, EVEN_IN_W), MXU_DTYPE)
    dqr, dkr, dproj0, gs["even_sinks"] = swa_bwd(
        qr, kr, proj0, small["even_sinks"], y_attn, lse, dmix, dproj0, "even_d_swa")
    dproj0, gs["even_q_gain"], gs["even_k_gain"] = qk_prep_bwd(
        proj0, small["even_q_gain"], small["even_k_gain"], rope, dqr, dkr, dproj0, "even_d_qk_prep")
    dproj0, gs["even_conv_w"] = gconv_bwd(proj0, small["even_conv_w"], dmix, dproj0, "even_d_gconv")
    dwi = mm_at(h0t, dproj0, "even_dw_in", transposed=True, chunks=_EVEN_D_CHUNKS)
    tok = grads_out("even", {"w_in": dwi, "w_out": dwo})
    grad_x, gs["even_norm"] = mm_rms_bwd(dproj0, we["w_in"], x, small["even_norm"], dx1, "even_d_norm", after=tok,
                                         chunks=_EVEN_D_CHUNKS)
    return loss_row, grad_x, gs


_SMALL_ORDER = ("even_norm", "even_q_gain", "even_k_gain", "even_sinks", "odd_a_log", "odd_dt_bias", "odd_o_gain",
                "ffn_norm0", "ffn_norm1", "odd_norm", "even_conv_w", "odd_conv_w")
_SMALL_SIZE = {"even_norm": 1024, "even_q_gain": 64, "even_k_gain": 64, "even_sinks": 8, "odd_a_log": 8,
               "odd_dt_bias": 8, "odd_o_gain": 128, "ffn_norm0": 1024, "ffn_norm1": 1024, "odd_norm": 1024,
               "even_conv_w": 3 * 512, "odd_conv_w": 4 * 3072}
_N_REPL = 9


def _pack_rows(vals):
    flat = jnp.concatenate([v.reshape(-1) for v in vals])
    pad = (-flat.shape[0]) % 1024
    return jnp.pad(flat, (0, pad)).reshape(-1, 128)


def _my_block(full, size, axis):
    me = 4 * lax.axis_index("x") + 2 * lax.axis_index("y") + lax.axis_index("c")
    return lax.dynamic_slice_in_dim(full, me * size, size, axis=axis)


def kernel(x, even_norm, even_w_in, even_q_gain, even_k_gain, even_sinks, even_conv_w, even_w_out, odd_norm, odd_w_in, odd_conv_w, odd_a_log, odd_dt_bias, odd_o_gain, odd_w_out, ffn_norm, ffn_w_gate_up, ffn_w_down, loss_target, m_even_norm, m_even_w_in, m_even_q_gain, m_even_k_gain, m_even_sinks, m_even_conv_w, m_even_w_out, m_odd_norm, m_odd_w_in, m_odd_conv_w, m_odd_a_log, m_odd_dt_bias, m_odd_o_gain, m_odd_w_out, m_ffn_norm, m_ffn_w_gate_up, m_ffn_w_down, v_even_norm, v_even_w_in, v_even_q_gain, v_even_k_gain, v_even_sinks, v_even_conv_w, v_even_w_out, v_odd_norm, v_odd_w_in, v_odd_conv_w, v_odd_a_log, v_odd_dt_bias, v_odd_o_gain, v_odd_w_out, v_ffn_norm, v_ffn_w_gate_up, v_ffn_w_down):
    t = x.shape[1]
    d = D_MODEL

    tr = lambda a: jnp.swapaxes(a, 1, 2)
    shard = {
        "even": {"w_in": tr(even_w_in)[0], "w_out": even_w_out[0]},
        "ffn0": {"gate_up": tr(ffn_w_gate_up)[0], "down": ffn_w_down[0]},
        "odd": {"w_in": tr(odd_w_in)[0], "w_out": odd_w_out[0]},
        "ffn1": {"gate_up": tr(ffn_w_gate_up)[1], "down": ffn_w_down[1]},
    }
    given = {
        ("even", "w_in"): ("even_w_in", even_w_in, m_even_w_in, v_even_w_in, 0),
        ("even", "w_out"): ("even_w_out", even_w_out, m_even_w_out, v_even_w_out, 0),
        ("odd", "w_in"): ("odd_w_in", odd_w_in, m_odd_w_in, v_odd_w_in, 0),
        ("odd", "w_out"): ("odd_w_out", odd_w_out, m_odd_w_out, v_odd_w_out, 0),
        ("ffn0", "gate_up"): ("ffn_w_gate_up", ffn_w_gate_up, m_ffn_w_gate_up, v_ffn_w_gate_up, 0),
        ("ffn1", "gate_up"): ("ffn_w_gate_up", ffn_w_gate_up, m_ffn_w_gate_up, v_ffn_w_gate_up, 1),
        ("ffn0", "down"): ("ffn_w_down", ffn_w_down, m_ffn_w_down, v_ffn_w_down, 0),
        ("ffn1", "down"): ("ffn_w_down", ffn_w_down, m_ffn_w_down, v_ffn_w_down, 1),
    }

    def whole(group, parts):
        col, row = tuple(shard[group])
        if group == "odd":
            w_col = join_blocks(parts[0], ODD_IN_PAD, "odd_w_in_join")
        else:
            w_col = parts[0].reshape(-1, d)
        return {col: w_col, row: parts[1].reshape(-1, d)}

    wire = {g: [a.astype(MXU_DTYPE) for a in shard[g].values()] for g in shard}
    wire["even_out"] = [wire["even"].pop()]
    wire["even"].append(_pack_rows([odd_norm, even_conv_w, odd_conv_w]))
    order = ("even", "even_out", "ffn0", "odd", "ffn1")
    sems, srcs_thru, lands_thru, tok = send_start([a for g in order for a in wire[g]], "gather_start", False, None)
    gathers, a0, na = {}, 0, sum(len(wire[g]) for g in order)
    for g in order:
        a1 = a0 + len(wire[g])
        gathers[g] = (sems[2 * N_PEER * a0:2 * N_PEER * a1] + sems[2 * N_PEER * na + a0:2 * N_PEER * na + a1],
                      srcs_thru[a0:a1], lands_thru[a0:a1])
        a0 = a1
    o1 = d // N_DEV
    o2 = o1 + 3 * CONV_CH // N_DEV

    def weights_of(group, after):
        lands = send_wait(*gathers[group], f"gather_{group}_wait", False, after)
        if group == "even_out":
            return {"w_out": lands[0].reshape(-1, d)}
        if group != "even":
            return whole(group, lands)
        sg = lands[1].reshape(N_DEV, -1)
        return {"w_in": lands[0].reshape(-1, d), "small": {
            "odd_norm": sg[:, :o1].reshape(1, d),
            "even_conv_w": sg[:, o1:o2].reshape(N_DEV, 3, CONV_CH // N_DEV).transpose(1, 0, 2).reshape(3, CONV_CH),
            "odd_conv_w": sg[:, o2:o2 + 4 * _QKV_W // N_DEV].reshape(N_DEV, 4, _QKV_W // N_DEV)
            .transpose(1, 0, 2).reshape(4, _QKV_W)}}

    sent = {}

    def grads_out(group, dws):
        col, row = tuple(shard[group])
        c = shard[group][col].shape[0]
        pieces = [split_blocks(dws[col], N_DEV, c, "odd_dw_in_split") if group == "odd"
                  else dws[col].reshape((N_DEV,) + shard[group][col].shape),
                  dws[row].reshape((N_DEV,) + shard[group][row].shape)]
        sems, srcs_thru, lands_thru, token = send_start(pieces, f"exchange_{group}_start", True, None)
        sent[group] = (sems, srcs_thru, lands_thru, pieces)
        return token

    small = {
        "even_norm": even_norm, "even_q_gain": even_q_gain, "even_k_gain": even_k_gain, "even_sinks": even_sinks,
        "odd_a_log": odd_a_log.reshape(-1), "odd_dt_bias": odd_dt_bias.reshape(-1), "odd_o_gain": odd_o_gain,
        "ffn_norm0": ffn_norm[0:1], "ffn_norm1": ffn_norm[1:2],
    }

    loss_row, grad_x, gs = local_step(x.reshape(t, d), loss_target.reshape(t, d), small, weights_of, grads_out, after=tok)

    rows = _pack_rows([gs[n] for n in _SMALL_ORDER] + [loss_row[:, 0:1]])
    small_sent = send_start([rows], "gather_small_grads_start", False, None)

    res, behind = {}, small_sent[3]
    for g in ("ffn1", "odd", "ffn0", "even"):
        sems, srcs_thru, lands_thru, pieces = sent[g]
        lands = send_wait(sems, srcs_thru, lands_thru, f"exchange_{g}_wait", True, behind)
        for i, (key, pcs) in enumerate(zip(shard[g], lands)):
            name, w_, m_, v_, layer = given[g, key]
            view = tr if i == 0 else (lambda a: a)
            res[name] = adam_sum(view(w_), pcs, view(m_), view(v_), f"adamw_{g}_{key}", layer=layer, into=res.get(name))
        behind = res[name][0]
    for name in ("even_w_in", "odd_w_in", "ffn_w_gate_up"):
        res[name] = tuple(tr(a) for a in res[name])

    (rows_g,) = send_wait(*small_sent[:3], "gather_small_grads_wait", False, behind)
    tot = sum_rows(rows_g, "sum_small_grads").reshape(-1)
    off, sgrad = 0, {}
    for n in _SMALL_ORDER:
        sgrad[n] = tot[off:off + _SMALL_SIZE[n]]
        off += _SMALL_SIZE[n]
    loss = tot[off]

    repl = _SMALL_ORDER[:_N_REPL]
    repl_w = {"even_norm": even_norm, "even_q_gain": even_q_gain, "even_k_gain": even_k_gain, "even_sinks": even_sinks,
              "odd_a_log": odd_a_log, "odd_dt_bias": odd_dt_bias, "odd_o_gain": odd_o_gain,
              "ffn_norm0": ffn_norm[0], "ffn_norm1": ffn_norm[1]}
    repl_m = {"even_norm": m_even_norm, "even_q_gain": m_even_q_gain, "even_k_gain": m_even_k_gain,
              "even_sinks": m_even_sinks, "odd_a_log": m_odd_a_log, "odd_dt_bias": m_odd_dt_bias,
              "odd_o_gain": m_odd_o_gain, "ffn_norm0": m_ffn_norm[0], "ffn_norm1": m_ffn_norm[1]}
    repl_v = {"even_norm": v_even_norm, "even_q_gain": v_even_q_gain, "even_k_gain": v_even_k_gain,
              "even_sinks": v_even_sinks, "odd_a_log": v_odd_a_log, "odd_dt_bias": v_odd_dt_bias,
              "odd_o_gain": v_odd_o_gain, "ffn_norm0": v_ffn_norm[0], "ffn_norm1": v_ffn_norm[1]}
    pk = lambda dct: _pack_rows([dct[n] for n in repl])
    pd_, pm_, pv_ = adam_small(pk(repl_w), pk(sgrad), pk(repl_m), pk(repl_v), "adamw_replicated")
    sres = {}
    off = 0
    for n in repl:
        sz = _SMALL_SIZE[n]
        sres[n] = (sgrad[n], pd_.reshape(-1)[off:off + sz], pm_.reshape(-1)[off:off + sz], pv_.reshape(-1)[off:off + sz])
        off += sz
    g_on = _my_block(sgrad["odd_norm"].reshape(1, d), d // N_DEV, 1)
    g_ec = _my_block(sgrad["even_conv_w"].reshape(3, CONV_CH), CONV_CH // N_DEV, 1)
    g_oc = _my_block(sgrad["odd_conv_w"].reshape(4, _QKV_W), _QKV_W // N_DEV, 1)
    shard_w = _pack_rows([odd_norm, even_conv_w, odd_conv_w])
    sd_, sm_, sv_ = adam_small(shard_w, _pack_rows([g_on, g_ec, g_oc]),
                               _pack_rows([m_odd_norm, m_even_conv_w, m_odd_conv_w]),
                               _pack_rows([v_odd_norm, v_even_conv_w, v_odd_conv_w]), "adamw_sharded_small")
    off = 0
    for n, gfull, like in (("odd_norm", g_on, odd_norm), ("even_conv_w", g_ec, even_conv_w), ("odd_conv_w", g_oc, odd_conv_w)):
        sz = like.size
        sres[n] = (gfull, sd_.reshape(-1)[off:off + sz], sm_.reshape(-1)[off:off + sz], sv_.reshape(-1)[off:off + sz])
        off += sz

    def small_out(name, like, kind):
        if name == "ffn_norm":
            return jnp.stack([sres["ffn_norm0"][kind], sres["ffn_norm1"][kind]]).reshape(like.shape)
        return sres[name][kind].reshape(like.shape)

    order = (("even_norm", even_norm), ("even_w_in", even_w_in), ("even_q_gain", even_q_gain),
             ("even_k_gain", even_k_gain), ("even_sinks", even_sinks), ("even_conv_w", even_conv_w),
             ("even_w_out", even_w_out), ("odd_norm", odd_norm), ("odd_w_in", odd_w_in), ("odd_conv_w", odd_conv_w),
             ("odd_a_log", odd_a_log), ("odd_dt_bias", odd_dt_bias), ("odd_o_gain", odd_o_gain),
             ("odd_w_out", odd_w_out), ("ffn_norm", ffn_norm), ("ffn_w_gate_up", ffn_w_gate_up),
             ("ffn_w_down", ffn_w_down))
    outs = [loss, grad_x.reshape(x.shape)]
    for kind in range(4):
        for name, like in order:
            outs.append(res[name][kind] if name in res else small_out(name, like, kind))
    return tuple(outs)
```

```python
import jax
import jax.numpy as jnp
import numpy as np
from jax import lax
from jax.experimental import pallas as pl
from jax.experimental.pallas import tpu as pltpu

F32 = jnp.float32
MXU_DTYPE = jnp.bfloat16
HI = lax.Precision.HIGH
EPS = 1e-6
N_DEV = 8
D_MODEL = 1024
HEAD_DIM = 64
ATTN_HEADS = 8
KV_HEADS = 2
ATTN_BLOCK = 128
Q_W = 512
KV_W = 128
CONV_CH = 512
EVEN_IN_W = 2304
DN_HEADS = 8
DN_DIM = 128
DN_W = 1024
DN_CHUNK = 64
ODD_IN_W = 4112
ODD_IN_PAD = 4224
D_FF = 2816
NEG = -1e30
VMEM_LIMIT = 56 * 1024 * 1024
ADAM_LR, ADAM_B1, ADAM_B2, ADAM_EPS, ADAM_WD, ADAM_STEP = 0.001, 0.9, 0.999, 1e-08, 0.01, 10
MESH = pl.DeviceIdType.MESH


def _cp(*sem):
    return pltpu.CompilerParams(dimension_semantics=sem, vmem_limit_bytes=VMEM_LIMIT)


def _pick(n, cap):
    best = 128
    for t in range(128, cap + 1, 128):
        if n % t == 0:
            best = t
    return best


def _mx(a, b):
    return jnp.dot(a.astype(MXU_DTYPE), b.astype(MXU_DTYPE), preferred_element_type=F32)


def _mx_nt(a, b):
    return lax.dot_general(a.astype(MXU_DTYPE), b.astype(MXU_DTYPE), (((1,), (1,)), ((), ())),
                           preferred_element_type=F32)


def _mx_tn(a, b):
    return lax.dot_general(a.astype(MXU_DTYPE), b.astype(MXU_DTYPE), (((0,), (0,)), ((), ())),
                           preferred_element_type=F32)


def _hi(a, b):
    return jnp.dot(a, b, precision=HI, preferred_element_type=F32)


def _hi_nt(a, b):
    return lax.dot_general(a, b, (((1,), (1,)), ((), ())), precision=HI, preferred_element_type=F32)


def _hi_tn(a, b):
    return lax.dot_general(a, b, (((0,), (0,)), ((), ())), precision=HI, preferred_element_type=F32)


def _sigmoid(x):
    return 0.5 * jnp.tanh(0.5 * x) + 0.5


def _softplus(x):
    return jnp.maximum(x, 0.0) + jnp.log(1.0 + jnp.exp(-jnp.abs(x)))


def mm_nn_res_norm(a, b, res, g, name, tm=512):
    t, k = a.shape
    d = b.shape[1]
    tm = min(tm, t)

    def body(a_ref, b_ref, res_ref, g_ref, y_ref, h_ref, ht_ref):
        y = res_ref[...] + _mx(a_ref[...], b_ref[...])
        y_ref[...] = y
        h = y * lax.rsqrt(jnp.mean(y * y, axis=-1, keepdims=True) + EPS) * g_ref[...]
        h_ref[...] = h.astype(h_ref.dtype)
        ht_ref[...] = h.T.astype(ht_ref.dtype)

    row = pl.BlockSpec((tm, d), lambda i: (i, 0))
    return pl.pallas_call(
        body, name=name, grid=(t // tm,),
        in_specs=[pl.BlockSpec((tm, k), lambda i: (i, 0)), pl.BlockSpec((k, d), lambda i: (0, 0)), row,
                  pl.BlockSpec((1, d), lambda i: (0, 0))],
        out_specs=(row, row, pl.BlockSpec((d, tm), lambda i: (0, i))),
        out_shape=(jax.ShapeDtypeStruct((t, d), F32), jax.ShapeDtypeStruct((t, d), MXU_DTYPE),
                   jax.ShapeDtypeStruct((d, t), MXU_DTYPE)),
        compiler_params=_cp("parallel"))(a, b, res, g)


def mm_nt(a, b, name, out_dtype=F32, tm=2048, after=None):
    m, k = a.shape
    n, _ = b.shape
    tn = _pick(n, 512 if k > 3000 else 1536)
    tm = min(tm, m)

    def body(a_ref, b_ref, *rest):
        o_ref = rest[-1]
        o_ref[...] = _mx_nt(a_ref[...], b_ref[...]).astype(o_ref.dtype)

    in_specs = [pl.BlockSpec((tm, k), lambda j, i: (i, 0)), pl.BlockSpec((tn, k), lambda j, i: (j, 0))]
    args = [a, b]
    if after is not None:
        in_specs.append(pl.BlockSpec(memory_space=pl.ANY))
        args.append(after)
    return pl.pallas_call(
        body, name=name, grid=(n // tn, m // tm), in_specs=in_specs,
        out_specs=pl.BlockSpec((tm, tn), lambda j, i: (i, j)),
        out_shape=jax.ShapeDtypeStruct((m, n), out_dtype), compiler_params=_cp("parallel", "parallel"))(*args)


def mm_at(at, b, name, tk=2048, transposed=False, tn=None, row_block=None, chunks=None):
    m, kk = at.shape
    _, n = b.shape
    tm, tn, tk = _pick(m, 1408), tn or _pick(n, 2816), min(tk, kk)
    nk = kk // tk
    assert chunks is None or (transposed and tn == n)

    def body(a_ref, b_ref, o_ref, acc_ref):
        k = pl.program_id(2)
        p = _mx(a_ref[...], b_ref[...])
        acc = jnp.where(k == 0, p, acc_ref[...] + p)
        acc_ref[...] = acc

        @pl.when(k == nk - 1)
        def _():
            res = (acc.T if transposed else acc).astype(o_ref.dtype)
            if chunks is None:
                o_ref[...] = res
            else:
                for cb, co, size in chunks:
                    o_ref[co:co + size, :] = res[cb:cb + size]

    if transposed:
        rb = row_block or (lambda j: j)
        out_spec = pl.BlockSpec((tn, tm), lambda i, j, k: (rb(j), i))
        out_shape = jax.ShapeDtypeStruct((n, m), MXU_DTYPE)
    else:
        out_spec = pl.BlockSpec((tm, tn), lambda i, j, k: (i, j))
        out_shape = jax.ShapeDtypeStruct((m, n), MXU_DTYPE)
    return pl.pallas_call(
        body, name=name, grid=(m // tm, n // tn, nk),
        in_specs=[pl.BlockSpec((tm, tk), lambda i, j, k: (i, k)), pl.BlockSpec((tk, tn), lambda i, j, k: (k, j))],
        out_specs=out_spec, out_shape=out_shape, scratch_shapes=[pltpu.VMEM((tm, tn), F32)],
        compiler_params=_cp("parallel", "parallel", "arbitrary"))(at, b)


def rms_fwd(x, g, name, tm=512, after=None):
    t, d = x.shape

    def body(x_ref, g_ref, *rest):
        o_ref, ot_ref = rest[-2:]
        xv = x_ref[...]
        r = lax.rsqrt(jnp.mean(xv * xv, axis=-1, keepdims=True) + EPS)
        h = xv * r * g_ref[...]
        o_ref[...] = h.astype(o_ref.dtype)
        ot_ref[...] = h.T.astype(ot_ref.dtype)

    in_specs = [pl.BlockSpec((tm, d), lambda i: (i, 0)), pl.BlockSpec((1, d), lambda i: (0, 0))]
    args = [x, g]
    if after is not None:
        in_specs.append(pl.BlockSpec(memory_space=pl.ANY))
        args.append(after)
    return pl.pallas_call(
        body, name=name, grid=(t // tm,), in_specs=in_specs,
        out_specs=(pl.BlockSpec((tm, d), lambda i: (i, 0)), pl.BlockSpec((d, tm), lambda i: (0, i))),
        out_shape=(jax.ShapeDtypeStruct((t, d), MXU_DTYPE), jax.ShapeDtypeStruct((d, t), MXU_DTYPE)),
        compiler_params=_cp("parallel"))(*args)


def mm_rms_bwd(a, bt, x, g, dres, name, tm=512, after=None, chunks=None):
    t, k = a.shape
    d = bt.shape[1]
    tm = min(tm if k > 3000 else 2 * tm, t)
    chunks = chunks or ((0, 0, k),)

    def body(a_ref, b_ref, x_ref, g_ref, dres_ref, *rest):
        dx_ref, dg_ref = rest[-2:]
        dhv = None
        for ca, cb, size in chunks:
            part = _mx(a_ref[:, ca:ca + size], b_ref[cb:cb + size, :])
            dhv = part if dhv is None else dhv + part
        xv = x_ref[...]
        r = lax.rsqrt(jnp.mean(xv * xv, axis=-1, keepdims=True) + EPS)
        xh = xv * r
        dxh = dhv * g_ref[...]
        dx_ref[...] = dres_ref[...] + r * (dxh - xh * jnp.mean(dxh * xh, axis=-1, keepdims=True))
        part = jnp.sum(dhv * xh, axis=0, keepdims=True)
        dg_ref[...] = jnp.where(pl.program_id(0) == 0, part, dg_ref[...] + part)

    row = pl.BlockSpec((tm, d), lambda i: (i, 0))
    one = pl.BlockSpec((1, d), lambda i: (0, 0))
    in_specs = [pl.BlockSpec((tm, k), lambda i: (i, 0)), pl.BlockSpec((k, d), lambda i: (0, 0)), row, one, row]
    args = [a, bt, x, g, dres]
    if after is not None:
        in_specs.append(pl.BlockSpec(memory_space=pl.ANY))
        args.append(after)
    return pl.pallas_call(
        body, name=name, grid=(t // tm,), in_specs=in_specs, out_specs=(row, one),
        out_shape=(jax.ShapeDtypeStruct((t, d), F32), jax.ShapeDtypeStruct((1, d), F32)),
        compiler_params=_cp("arbitrary"))(*args)


GU_TILE = 1408


def ffn_up(f, wt, name, tm=1024):
    t, d = f.shape
    tm = min(tm, t)
    nj = D_FF // GU_TILE

    def body(f_ref, wg_ref, wu_ref, gu_ref, a_ref, at_ref):
        g = _mx_nt(f_ref[...], wg_ref[...])
        u = _mx_nt(f_ref[...], wu_ref[...])
        sg = _sigmoid(g)
        gs = g * sg
        gu_ref[:, :GU_TILE] = (u * (sg + gs - gs * sg)).astype(gu_ref.dtype)
        gu_ref[:, GU_TILE:] = gs.astype(gu_ref.dtype)
        act = gs * u
        a_ref[...] = act.astype(a_ref.dtype)
        at_ref[...] = act.T.astype(at_ref.dtype)

    return pl.pallas_call(
        body, name=name, grid=(nj, t // tm),
        in_specs=[pl.BlockSpec((tm, d), lambda j, i: (i, 0)), pl.BlockSpec((GU_TILE, d), lambda j, i: (j, 0)),
                  pl.BlockSpec((GU_TILE, d), lambda j, i: (nj + j, 0))],
        out_specs=(pl.BlockSpec((tm, 2 * GU_TILE), lambda j, i: (i, j)), pl.BlockSpec((tm, GU_TILE), lambda j, i: (i, j)),
                   pl.BlockSpec((GU_TILE, tm), lambda j, i: (j, i))),
        out_shape=(jax.ShapeDtypeStruct((t, 2 * D_FF), MXU_DTYPE), jax.ShapeDtypeStruct((t, D_FF), MXU_DTYPE),
                   jax.ShapeDtypeStruct((D_FF, t), MXU_DTYPE)),
        compiler_params=_cp("parallel", "parallel"))(f, wt, wt)


_GU_CHUNKS = tuple((q * GU_TILE, ((q % 2) * (D_FF // GU_TILE) + q // 2) * GU_TILE, GU_TILE)
                   for q in range(2 * D_FF // GU_TILE))


def ffn_dact(dy, w_d, gu, name, tm=1024, after=None):
    t, d = dy.shape
    tm = min(tm, t)

    def body(dy_ref, w_ref, gu_ref, *rest):
        o_ref = rest[-1]
        da = _mx_nt(dy_ref[...], w_ref[...])
        o_ref[:, :GU_TILE] = (da * gu_ref[:, :GU_TILE]).astype(o_ref.dtype)
        o_ref[:, GU_TILE:] = (da * gu_ref[:, GU_TILE:]).astype(o_ref.dtype)

    in_specs = [pl.BlockSpec((tm, d), lambda j, i: (i, 0)), pl.BlockSpec((GU_TILE, d), lambda j, i: (j, 0)),
                pl.BlockSpec((tm, 2 * GU_TILE), lambda j, i: (i, j))]
    args = [dy, w_d, gu]
    if after is not None:
        in_specs.append(pl.BlockSpec(memory_space=pl.ANY))
        args.append(after)
    return pl.pallas_call(
        body, name=name, grid=(D_FF // GU_TILE, t // tm), in_specs=in_specs,
        out_specs=pl.BlockSpec((tm, 2 * GU_TILE), lambda j, i: (i, j)),
        out_shape=jax.ShapeDtypeStruct((t, 2 * D_FF), MXU_DTYPE), compiler_params=_cp("parallel", "parallel"))(*args)


def mm_nn_res_loss(a, b, res, target, name, tm=512):
    t, k = a.shape
    d = b.shape[1]
    tm = min(tm, t)

    def body(a_ref, b_ref, res_ref, t_ref, dy_ref, l_ref):
        e = res_ref[...] + _mx(a_ref[...], b_ref[...]) - t_ref[...]
        dy_ref[...] = e * (1.0 / d)
        part = jnp.zeros((1, 128), F32) + 0.5 * jnp.sum(jnp.mean(e * e, axis=-1, keepdims=True), axis=0, keepdims=True)
        l_ref[...] = jnp.where(pl.program_id(0) == 0, part, l_ref[...] + part)

    row = pl.BlockSpec((tm, d), lambda i: (i, 0))
    return pl.pallas_call(
        body, name=name, grid=(t // tm,),
        in_specs=[pl.BlockSpec((tm, k), lambda i: (i, 0)), pl.BlockSpec((k, d), lambda i: (0, 0)), row, row],
        out_specs=(row, pl.BlockSpec((1, 128), lambda i: (0, 0))),
        out_shape=(jax.ShapeDtypeStruct((t, d), F32), jax.ShapeDtypeStruct((1, 128), F32)),
        compiler_params=_cp("arbitrary"))(a, b, res, target)


QK_W = Q_W + KV_W
_QK_TILE = 256


def _qk_mats():
    idx = np.arange(_QK_TILE)
    half = HEAD_DIM // 2
    same = (idx[:, None] // HEAD_DIM) == (idx[None, :] // HEAD_DIM)
    lo = (idx % HEAD_DIM) < half
    rot = np.where((idx[:, None] == idx[None, :] + half) & lo[None, :], -1.0, 0.0)
    rot = rot + np.where((idx[:, None] == idx[None, :] - half) & ~lo[None, :], 1.0, 0.0)
    return jnp.asarray(same, F32), jnp.asarray(rot, F32)


QK_TM = 256


def _qk_gains(q_gain, k_gain):
    return jnp.concatenate([q_gain] * ATTN_HEADS + [k_gain] * KV_HEADS, axis=-1)


def _rope_tile(ca_ref, sa_ref, cb_ref, sb_ref):
    ca, sa, cb, sb = ca_ref[0], sa_ref[0], cb_ref[...], sb_ref[...]
    c, s = ca * cb - sa * sb, sa * cb + ca * sb
    rep = QK_W // 128
    return jnp.concatenate([c] * rep, axis=-1), jnp.concatenate([s] * rep, axis=-1)


_ROPE_SPECS = [pl.BlockSpec((1, 1, 128), lambda i: (i, 0, 0)), pl.BlockSpec((1, 1, 128), lambda i: (i, 0, 0)),
               pl.BlockSpec((QK_TM, 128), lambda i: (0, 0)), pl.BlockSpec((QK_TM, 128), lambda i: (0, 0))]


def _qk_tiles(a, mat, transposed=False):
    outs = []
    for c0 in range(0, QK_W, _QK_TILE):
        w = min(_QK_TILE, QK_W - c0)
        mt = (mat.T if transposed else mat)[:w, :w].astype(MXU_DTYPE)
        at = a[:, c0:c0 + w]
        hi = at.astype(MXU_DTYPE)
        lo = (at - hi.astype(F32)).astype(MXU_DTYPE)
        outs.append(jnp.dot(hi, mt, preferred_element_type=F32) + jnp.dot(lo, mt, preferred_element_type=F32))
    return jnp.concatenate(outs, axis=-1)


def qk_prep_fwd(proj, q_gain, k_gain, rope, name):
    t = proj.shape[0]
    tm = QK_TM
    gmat, rmat = _qk_mats()
    gain = _qk_gains(q_gain, k_gain)

    def body(p_ref, g_ref, ca_ref, sa_ref, cb_ref, sb_ref, gm_ref, rm_ref, q_ref, k_ref):
        x = p_ref[...]
        r = lax.rsqrt(_qk_tiles(x * x, gm_ref[...]) * (1.0 / HEAD_DIM) + EPS)
        xn = x * r * g_ref[...]
        c, s = _rope_tile(ca_ref, sa_ref, cb_ref, sb_ref)
        out = xn * c + _qk_tiles(xn, rm_ref[...]) * s
        q_ref[...] = out[:, :Q_W]
        k_ref[...] = out[:, Q_W:]

    full = pl.BlockSpec((_QK_TILE, _QK_TILE), lambda i: (0, 0))
    return pl.pallas_call(
        body, name=name, grid=(t // tm,),
        in_specs=[pl.BlockSpec((tm, QK_W), lambda i: (i, 0)), pl.BlockSpec((1, QK_W), lambda i: (0, 0))] + _ROPE_SPECS
        + [full, full],
        out_specs=(pl.BlockSpec((tm, Q_W), lambda i: (i, 0)), pl.BlockSpec((tm, KV_W), lambda i: (i, 0))),
        out_shape=(jax.ShapeDtypeStruct((t, Q_W), F32), jax.ShapeDtypeStruct((t, KV_W), F32)),
        compiler_params=_cp("parallel"))(proj, gain, *rope, gmat, rmat)


def qk_prep_bwd(proj, q_gain, k_gain, rope, dq, dk, into, name):
    t = proj.shape[0]
    tm = QK_TM
    gmat, rmat = _qk_mats()
    gain = _qk_gains(q_gain, k_gain)
    lanes = np.arange(QK_W)[:, None]
    fold = jnp.asarray(lanes % HEAD_DIM + np.where(lanes >= Q_W, HEAD_DIM, 0) == np.arange(128)[None, :], F32)

    def body(p_ref, g_ref, ca_ref, sa_ref, cb_ref, sb_ref, gm_ref, rm_ref, f_ref, dq_ref, dk_ref, into_ref,
             o_ref, dg_ref):
        x = p_ref[...]
        r = lax.rsqrt(_qk_tiles(x * x, gm_ref[...]) * (1.0 / HEAD_DIM) + EPS)
        xh = x * r
        c, s = _rope_tile(ca_ref, sa_ref, cb_ref, sb_ref)
        dout = jnp.concatenate([dq_ref[...], dk_ref[...]], axis=-1)
        dxn = dout * c + _qk_tiles(dout * s, rm_ref[...], transposed=True)
        part = _hi(jnp.sum(dxn * xh, axis=0, keepdims=True), f_ref[...])
        dxh = dxn * g_ref[...]
        mean = _qk_tiles(dxh * xh, gm_ref[...]) * (1.0 / HEAD_DIM)
        o_ref[...] = (r * (dxh - xh * mean)).astype(o_ref.dtype)
        dg_ref[...] = jnp.where(pl.program_id(0) == 0, part, dg_ref[...] + part)

    full = pl.BlockSpec((_QK_TILE, _QK_TILE), lambda i: (0, 0))
    dqk, dg = pl.pallas_call(
        body, name=name, grid=(t // tm,),
        in_specs=[pl.BlockSpec((tm, QK_W), lambda i: (i, 0)), pl.BlockSpec((1, QK_W), lambda i: (0, 0))] + _ROPE_SPECS
        + [full, full, pl.BlockSpec((QK_W, 128), lambda i: (0, 0)),
                  pl.BlockSpec((tm, Q_W), lambda i: (i, 0)), pl.BlockSpec((tm, KV_W), lambda i: (i, 0)),
                  pl.BlockSpec(memory_space=pl.ANY)],
        out_specs=(pl.BlockSpec((tm, QK_W), lambda i: (i, 0)), pl.BlockSpec((1, 128), lambda i: (0, 0))),
        out_shape=(jax.ShapeDtypeStruct(into.shape, into.dtype), jax.ShapeDtypeStruct((1, 128), F32)),
        input_output_aliases={len(rope) + 7: 0},
        compiler_params=_cp("arbitrary"))(proj, gain, *rope, gmat, rmat, fold, dq, dk, into)
    return dqk, dg[:, :HEAD_DIM], dg[:, HEAD_DIM:]


def _swa_valid(n, grp):
    qi = lax.broadcasted_iota(jnp.int32, (grp * ATTN_BLOCK, 2 * ATTN_BLOCK), 0) & (ATTN_BLOCK - 1)
    kj = lax.broadcasted_iota(jnp.int32, (grp * ATTN_BLOCK, 2 * ATTN_BLOCK), 1)
    diff = qi + ATTN_BLOCK - kj
    return (diff >= 0) & (diff < ATTN_BLOCK) & (n * ATTN_BLOCK - ATTN_BLOCK + kj >= 0)


def _stack_heads(ref, g, grp, rows=slice(None)):
    return jnp.concatenate([ref[rows, (g * grp + j) * HEAD_DIM:(g * grp + j + 1) * HEAD_DIM] for j in range(grp)], axis=0)


def _stack_sinks(s_ref, g, grp):
    return jnp.concatenate([jnp.zeros((ATTN_BLOCK, 1), F32) + s_ref[0:1, g * grp + j:g * grp + j + 1]
                            for j in range(grp)], axis=0)


SWA_STEP = 2


def swa_fwd(q, k, proj, sinks, name):
    t = q.shape[0]
    nb = t // ATTN_BLOCK
    scale = HEAD_DIM ** -0.5
    grp = ATTN_HEADS // KV_HEADS

    rows = SWA_STEP * ATTN_BLOCK

    def body(q_ref, kc_ref, kp_ref, vc_ref, vp_ref, s_ref, y_ref, mix_ref, yt_ref, lse_ref):
        n0 = pl.program_id(0) * SWA_STEP
        kk = jnp.concatenate([kp_ref[...], kc_ref[...]], axis=0).astype(MXU_DTYPE)
        vv = jnp.concatenate([vp_ref[...], vc_ref[...]], axis=0).astype(MXU_DTYPE)
        lane = lax.broadcasted_iota(jnp.int32, (ATTN_BLOCK, ATTN_HEADS), 1)
        units = [(b, g) for b in range(SWA_STEP) for g in range(KV_HEADS)]
        blk = lambda b: slice(b * ATTN_BLOCK, (b + 1) * ATTN_BLOCK)
        keys = lambda b: slice(b * ATTN_BLOCK, (b + 2) * ATTN_BLOCK)
        col = lambda g: slice(g * HEAD_DIM, (g + 1) * HEAD_DIM)
        valid = [_swa_valid(n0 + b, grp) for b in range(SWA_STEP)]
        qg = [_stack_heads(q_ref, g, grp, blk(b)) for b, g in units]
        sink = [_stack_sinks(s_ref, g, grp) for b, g in units]
        sc = [jnp.where(valid[b], _mx_nt(qg[u], kk[keys(b), col(g)]) * scale, NEG) for u, (b, g) in enumerate(units)]
        m = [jnp.maximum(jnp.max(sc_, axis=-1, keepdims=True), sk) for sc_, sk in zip(sc, sink)]
        e = [jnp.exp(sc_ - m_) for sc_, m_ in zip(sc, m)]
        den = [jnp.sum(e_, axis=-1, keepdims=True) + jnp.exp(sk - m_) for e_, sk, m_ in zip(e, sink, m)]
        og = [_mx(e[u] / den[u], vv[keys(b), col(g)]) for u, (b, g) in enumerate(units)]
        lg = [m_ + jnp.log(d_) for m_, d_ in zip(m, den)]
        for b in range(SWA_STEP):
            lse = jnp.zeros((ATTN_BLOCK, ATTN_HEADS), F32)
            outs = []
            for h in range(ATTN_HEADS):
                u = b * KV_HEADS + h // grp
                sub = blk(h % grp)
                outs.append(og[u][sub])
                lse = jnp.where(lane == h, lg[u][sub], lse)
            y = jnp.concatenate(outs, axis=-1)
            y_ref[blk(b), :] = y
            mix_ref[blk(b), :] = y.astype(mix_ref.dtype)
            yt_ref[:, blk(b)] = y.T.astype(yt_ref.dtype)
            lse_ref[blk(b), :] = lse

    cur = lambda n: (n, 0)
    prev = lambda n: (jnp.maximum(n * SWA_STEP - 1, 0), 0)
    vcol = (Q_W + KV_W) // KV_W
    return pl.pallas_call(
        body, name=name, grid=(nb // SWA_STEP,),
        in_specs=[pl.BlockSpec((rows, Q_W), cur), pl.BlockSpec((rows, KV_W), cur),
                  pl.BlockSpec((ATTN_BLOCK, KV_W), prev),
                  pl.BlockSpec((rows, KV_W), lambda n: (n, vcol)),
                  pl.BlockSpec((ATTN_BLOCK, KV_W), lambda n: (jnp.maximum(n * SWA_STEP - 1, 0), vcol)),
                  pl.BlockSpec((1, ATTN_HEADS), lambda n: (0, 0))],
        out_specs=(pl.BlockSpec((rows, Q_W), cur), pl.BlockSpec((rows, Q_W), cur),
                   pl.BlockSpec((Q_W, rows), lambda n: (0, n)), pl.BlockSpec((rows, ATTN_HEADS), cur)),
        out_shape=(jax.ShapeDtypeStruct((t, Q_W), F32), jax.ShapeDtypeStruct((t, Q_W + CONV_CH), MXU_DTYPE),
                   jax.ShapeDtypeStruct((Q_W + CONV_CH, t), MXU_DTYPE), jax.ShapeDtypeStruct((t, ATTN_HEADS), F32)),
        compiler_params=_cp("parallel"))(q, k, k, proj, proj, sinks)


def swa_bwd(q, k, proj, sinks, y, lse, dmix, into, name):
    t = q.shape[0]
    nb = t // ATTN_BLOCK
    scale = HEAD_DIM ** -0.5
    grp = ATTN_HEADS // KV_HEADS

    def body(q_ref, kc_ref, kp_ref, vc_ref, vp_ref, s_ref, y_ref, lse_ref, dy_ref, into_ref,
             dq_ref, dk_ref, dv_ref, ds_ref, dkc, dvc):
        n = pl.program_id(0)

        @pl.when(n == 0)
        def _():
            dkc[...] = jnp.zeros_like(dkc)
            dvc[...] = jnp.zeros_like(dvc)
            ds_ref[...] = jnp.zeros_like(ds_ref)

        @pl.when(n < nb)
        def _():
            valid = _swa_valid(n, grp)
            kk = jnp.concatenate([kp_ref[...], kc_ref[...]], axis=0).astype(MXU_DTYPE)
            vv = jnp.concatenate([vp_ref[...], vc_ref[...]], axis=0).astype(MXU_DTYPE)
            lane = lax.broadcasted_iota(jnp.int32, (1, ATTN_HEADS), 1)
            gs = range(KV_HEADS)
            kg = [kk[:, g * HEAD_DIM:(g + 1) * HEAD_DIM] for g in gs]
            vg = [vv[:, g * HEAD_DIM:(g + 1) * HEAD_DIM] for g in gs]
            qg = [_stack_heads(q_ref, g, grp).astype(MXU_DTYPE) for g in gs]
            dog = [_stack_heads(dy_ref, g, grp) for g in gs]
            og = [_stack_heads(y_ref, g, grp) for g in gs]
            lg = [jnp.concatenate([lse_ref[:, g * grp + j:g * grp + j + 1] for j in range(grp)], axis=0) for g in gs]
            sink = [_stack_sinks(s_ref, g, grp) for g in gs]
            sc = [jnp.where(valid, _mx_nt(qg[g], kg[g]) * scale, NEG) for g in gs]
            p = [jnp.exp(sc[g] - lg[g]) for g in gs]
            delta = [jnp.sum(dog[g] * og[g], axis=-1, keepdims=True) for g in gs]
            ds = [p[g] * (_mx_nt(dog[g], vg[g]) - delta[g]) for g in gs]
            dqg = [_mx(ds[g], kg[g]) * scale for g in gs]
            dkf = jnp.concatenate([_mx_tn(ds[g], qg[g]) * scale for g in gs], axis=-1)
            dvf = jnp.concatenate([_mx_tn(p[g], dog[g]) for g in gs], axis=-1)
            dsk = [jnp.exp(sink[g] - lg[g]) * delta[g] for g in gs]
            dsink = jnp.zeros((1, ATTN_HEADS), F32)
            dqs = []
            for h in range(ATTN_HEADS):
                rows = slice((h % grp) * ATTN_BLOCK, (h % grp + 1) * ATTN_BLOCK)
                dqs.append(dqg[h // grp][rows])
                dsink = jnp.where(lane == h, -jnp.sum(dsk[h // grp][rows], axis=0, keepdims=True), dsink)
            dq_ref[...] = jnp.concatenate(dqs, axis=-1)
            dk_ref[...] = dkc[...] + dkf[:ATTN_BLOCK]
            dv_ref[...] = (dvc[...] + dvf[:ATTN_BLOCK]).astype(dv_ref.dtype)
            dkc[...] = dkf[ATTN_BLOCK:]
            dvc[...] = dvf[ATTN_BLOCK:]
            ds_ref[...] += dsink

        @pl.when(n == nb)
        def _():
            dk_ref[...] = dkc[...]
            dv_ref[...] = dvc[...].astype(dv_ref.dtype)

    cur = lambda n: (jnp.minimum(n, nb - 1), 0)
    prev = lambda n: (jnp.clip(n - 1, 0, nb - 1), 0)
    vcol = (Q_W + KV_W) // KV_W
    return pl.pallas_call(
        body, name=name, grid=(nb + 1,),
        in_specs=[pl.BlockSpec((ATTN_BLOCK, Q_W), cur), pl.BlockSpec((ATTN_BLOCK, KV_W), cur),
                  pl.BlockSpec((ATTN_BLOCK, KV_W), prev),
                  pl.BlockSpec((ATTN_BLOCK, KV_W), lambda n: (jnp.minimum(n, nb - 1), vcol)),
                  pl.BlockSpec((ATTN_BLOCK, KV_W), lambda n: (jnp.clip(n - 1, 0, nb - 1), vcol)),
                  pl.BlockSpec((1, ATTN_HEADS), lambda n: (0, 0)),
                  pl.BlockSpec((ATTN_BLOCK, Q_W), cur), pl.BlockSpec((ATTN_BLOCK, ATTN_HEADS), cur),
                  pl.BlockSpec((ATTN_BLOCK, Q_W), cur), pl.BlockSpec(memory_space=pl.ANY)],
        out_specs=(pl.BlockSpec((ATTN_BLOCK, Q_W), cur), pl.BlockSpec((ATTN_BLOCK, KV_W), prev),
                   pl.BlockSpec((ATTN_BLOCK, KV_W), lambda n: (jnp.clip(n - 1, 0, nb - 1), vcol)),
                   pl.BlockSpec((1, ATTN_HEADS), lambda n: (0, 0))),
        out_shape=(jax.ShapeDtypeStruct((t, Q_W), F32), jax.ShapeDtypeStruct((t, KV_W), F32),
                   jax.ShapeDtypeStruct(into.shape, into.dtype), jax.ShapeDtypeStruct((1, ATTN_HEADS), F32)),
        scratch_shapes=[pltpu.VMEM((ATTN_BLOCK, KV_W), F32), pltpu.VMEM((ATTN_BLOCK, KV_W), F32)],
        input_output_aliases={9: 2},
        compiler_params=_cp("arbitrary"))(q, k, k, proj, proj, sinks, y, lse, dmix, into)


GC_W = 256
_GB0, _GC0, _XI0 = 768 // GC_W, 1280 // GC_W, 1792 // GC_W
HALO = 8


def gconv_fwd(proj, conv_w, mix, mix_t, name, tm=512):
    t = proj.shape[0]
    hb = tm // HALO
    half = Q_W // GC_W

    def body(gb_ref, gc_ref, xi_ref, gch_ref, xih_ref, w_ref, mix_in, mixt_in, y_ref, yt_ref):
        i = pl.program_id(1)
        u = gc_ref[...] * xi_ref[...]
        uh = jnp.where(i == 0, 0.0, gch_ref[...] * xih_ref[...])
        up = jnp.concatenate([uh, u], axis=0)
        cv = w_ref[0:1, :] * up[HALO - 2:HALO - 2 + tm]
        cv = cv + w_ref[1:2, :] * up[HALO - 1:HALO - 1 + tm]
        cv = cv + w_ref[2:3, :] * u
        y = gb_ref[...] * cv
        y_ref[...] = y.astype(y_ref.dtype)
        yt_ref[...] = y.T.astype(yt_ref.dtype)

    def col(c0):
        return pl.BlockSpec((tm, GC_W), lambda cj, i: (i, c0 + cj))

    def halo(c0):
        return pl.BlockSpec((HALO, GC_W), lambda cj, i: (jnp.maximum(i * hb - 1, 0), c0 + cj))

    return pl.pallas_call(
        body, name=name, grid=(CONV_CH // GC_W, t // tm),
        in_specs=[col(_GB0), col(_GC0), col(_XI0), halo(_GC0), halo(_XI0),
                  pl.BlockSpec((3, GC_W), lambda cj, i: (0, cj)),
                  pl.BlockSpec(memory_space=pl.ANY), pl.BlockSpec(memory_space=pl.ANY)],
        out_specs=(pl.BlockSpec((tm, GC_W), lambda cj, i: (i, half + cj)),
                   pl.BlockSpec((GC_W, tm), lambda cj, i: (half + cj, i))),
        out_shape=(jax.ShapeDtypeStruct(mix.shape, mix.dtype), jax.ShapeDtypeStruct(mix_t.shape, mix_t.dtype)),
        input_output_aliases={6: 0, 7: 1},
        compiler_params=_cp("parallel", "parallel"))(proj, proj, proj, proj, proj, conv_w, mix, mix_t)


def gconv_bwd(proj, conv_w, dmix, into, name, tm=512):
    t = proj.shape[0]
    hb = tm // HALO
    nt = t // tm
    dy0 = Q_W // GC_W

    def body(gb_ref, gc_ref, xi_ref, gch_ref, xih_ref, gbn_ref, dyn_ref, dy_ref, w_ref, into_ref, o_ref, dw_ref):
        dgb_ref, dgc_ref, dxi_ref = (o_ref.at[:, j * GC_W:(j + 1) * GC_W] for j in range(3))
        i = pl.program_id(1)
        gc, xi, gb, dy = gc_ref[...], xi_ref[...], gb_ref[...], dy_ref[...]
        u = gc * xi
        uh = jnp.where(i == 0, 0.0, gch_ref[...] * xih_ref[...])
        up = jnp.concatenate([uh, u], axis=0)
        u2 = up[HALO - 2:HALO - 2 + tm]
        u1 = up[HALO - 1:HALO - 1 + tm]
        cv = w_ref[0:1, :] * u2 + w_ref[1:2, :] * u1 + w_ref[2:3, :] * u
        dgb_ref[...] = (dy * cv).astype(dgb_ref.dtype)
        dcv = dy * gb
        dcvn = jnp.where(i == nt - 1, 0.0, dyn_ref[...] * gbn_ref[...])
        dcvp = jnp.concatenate([dcv, dcvn], axis=0)
        du = w_ref[0:1, :] * dcvp[2:2 + tm] + w_ref[1:2, :] * dcvp[1:1 + tm] + w_ref[2:3, :] * dcv
        dgc_ref[...] = (du * xi).astype(dgc_ref.dtype)
        dxi_ref[...] = (du * gc).astype(dxi_ref.dtype)
        dw = jnp.concatenate([jnp.sum(dcv * u2, axis=0, keepdims=True), jnp.sum(dcv * u1, axis=0, keepdims=True),
                              jnp.sum(dcv * u, axis=0, keepdims=True)], axis=0)

        @pl.when(i == 0)
        def _():
            dw_ref[...] = dw

        @pl.when(i > 0)
        def _():
            dw_ref[...] += dw

    def col(c0):
        return pl.BlockSpec((tm, GC_W), lambda cj, i: (i, c0 + cj))

    def halo(c0):
        return pl.BlockSpec((HALO, GC_W), lambda cj, i: (jnp.maximum(i * hb - 1, 0), c0 + cj))

    def nxt(c0):
        return pl.BlockSpec((HALO, GC_W), lambda cj, i: (jnp.minimum((i + 1) * hb, t // HALO - 1), c0 + cj))

    return pl.pallas_call(
        body, name=name, grid=(CONV_CH // GC_W, nt),
        in_specs=[col(_GB0), col(_GC0), col(_XI0), halo(_GC0), halo(_XI0), nxt(_GB0), nxt(dy0), col(dy0),
                  pl.BlockSpec((3, GC_W), lambda cj, i: (0, cj)), pl.BlockSpec(memory_space=pl.ANY)],
        out_specs=(pl.BlockSpec((tm, 3 * GC_W), lambda cj, i: (i, 1 + cj)),
                   pl.BlockSpec((3, GC_W), lambda cj, i: (0, cj))),
        out_shape=(jax.ShapeDtypeStruct(into.shape, into.dtype), jax.ShapeDtypeStruct((3, CONV_CH), F32)),
        input_output_aliases={9: 0},
        compiler_params=_cp("parallel", "arbitrary"))(proj, proj, proj, proj, proj, proj, dmix, dmix, conv_w, into)


_EVEN_D_CHUNKS = ((0, 0, 768),) + tuple(
    (768 + (3 * j + part) * GC_W, 768 + part * CONV_CH + j * GC_W, GC_W)
    for j in range(CONV_CH // GC_W) for part in range(3))


_QKV_W = 3 * DN_W
_BA_COL = (4 * DN_W) // 128
_Z_COL = _QKV_W // DN_W


def gdn_prep_fwd(proj, conv_w, alog_row, dtb_row, name, tm=256):
    t = proj.shape[0]
    hb = tm // HALO
    qscale = DN_DIM ** -0.5

    def body(x_ref, xh_ref, w_ref, ba_ref, al_ref, dt_ref, q_ref, k_ref, v_ref, bg_ref, c_ref):
        i = pl.program_id(0)
        for gi in range(3 * DN_HEADS):
            sl = slice(gi * DN_DIM, (gi + 1) * DN_DIM)
            xp = jnp.concatenate([jnp.where(i == 0, 0.0, xh_ref[:, sl]), x_ref[:, sl]], axis=0)
            c = w_ref[0:1, sl] * xp[HALO - 3:HALO - 3 + tm]
            for j in range(1, 4):
                c = c + w_ref[j:j + 1, sl] * xp[HALO - 3 + j:HALO - 3 + j + tm]
            c_ref[:, sl] = c
            s = c * _sigmoid(c)
            osl = slice((gi % DN_HEADS) * DN_DIM, (gi % DN_HEADS + 1) * DN_DIM)
            if gi < DN_HEADS:
                q_ref[:, osl] = s * lax.rsqrt(jnp.sum(s * s, axis=-1, keepdims=True) + EPS) * qscale
            elif gi < 2 * DN_HEADS:
                k_ref[:, osl] = s * lax.rsqrt(jnp.sum(s * s, axis=-1, keepdims=True) + EPS)
            else:
                v_ref[:, osl] = s
        ba = ba_ref[...]
        lane = lax.broadcasted_iota(jnp.int32, ba.shape, 1)
        gval = -jnp.exp(al_ref[...]) * _softplus(ba + dt_ref[...])
        bg_ref[...] = jnp.where(lane < DN_HEADS, _sigmoid(ba), jnp.where(lane < 2 * DN_HEADS, gval, 0.0))

    row = pl.BlockSpec((tm, DN_W), lambda i: (i, 0))
    one = pl.BlockSpec((1, 128), lambda i: (0, 0))
    return pl.pallas_call(
        body, name=name, grid=(t // tm,),
        in_specs=[pl.BlockSpec((tm, _QKV_W), lambda i: (i, 0)),
                  pl.BlockSpec((HALO, _QKV_W), lambda i: (jnp.maximum(i * hb - 1, 0), 0)),
                  pl.BlockSpec((4, _QKV_W), lambda i: (0, 0)),
                  pl.BlockSpec((tm, 128), lambda i: (i, _BA_COL)), one, one],
        out_specs=(row, row, row, pl.BlockSpec((tm, 128), lambda i: (i, 0)), pl.BlockSpec((tm, _QKV_W), lambda i: (i, 0))),
        out_shape=(jax.ShapeDtypeStruct((t, DN_W), F32),) * 3 + (jax.ShapeDtypeStruct((t, 128), F32),
                                                                 jax.ShapeDtypeStruct((t, _QKV_W), F32)),
        compiler_params=_cp("parallel"))(proj, proj, conv_w, proj, alog_row, dtb_row)


def gdn_prep_bwd(proj, conv, conv_w, alog_row, dtb_row, dq, dk, dv, dbg, dz, name, tm=256):
    t = proj.shape[0]
    hb = tm // HALO
    nt = t // tm
    qscale = DN_DIM ** -0.5
    te = tm + HALO

    def body(x_ref, c_ref, cn_ref, w_ref, ba_ref, al_ref, dt_ref, dq_ref, dk_ref, dv_ref,
             dqn_ref, dkn_ref, dvn_ref, dbg_ref, dz_ref, dx_ref, dw_ref, ddt_ref, dal_ref):
        i = pl.program_id(0)
        first = i == 0
        last = i == nt - 1
        dws = []
        for gi in range(3 * DN_HEADS):
            sl = slice(gi * DN_DIM, (gi + 1) * DN_DIM)
            osl = slice((gi % DN_HEADS) * DN_DIM, (gi % DN_HEADS + 1) * DN_DIM)
            c = jnp.concatenate([c_ref[:, sl], cn_ref[:, sl]], axis=0)
            sg = _sigmoid(c)
            s = c * sg
            d_ref, dn_ref = ((dq_ref, dqn_ref), (dk_ref, dkn_ref), (dv_ref, dvn_ref))[gi // DN_HEADS]
            dy = jnp.concatenate([d_ref[:, osl], jnp.where(last, 0.0, dn_ref[:, osl])], axis=0)
            if gi < 2 * DN_HEADS:
                r = lax.rsqrt(jnp.sum(s * s, axis=-1, keepdims=True) + EPS)
                sh = s * r
                ds = r * (dy - sh * jnp.sum(sh * dy, axis=-1, keepdims=True))
                if gi < DN_HEADS:
                    ds = ds * qscale
            else:
                ds = dy
            dc = ds * sg * (1.0 + c * (1.0 - sg))
            dcs = [dc[3 - j:3 - j + tm] for j in range(4)]
            dx = w_ref[0:1, sl] * dcs[0]
            for j in range(1, 4):
                dx = dx + w_ref[j:j + 1, sl] * dcs[j]
            dx_ref[:, sl] = dx.astype(dx_ref.dtype)
            x0 = x_ref[:, sl]
            dws.append(jnp.concatenate([jnp.sum(dcs[j] * x0, axis=0, keepdims=True) for j in range(4)], axis=0))
        dw = jnp.concatenate(dws, axis=-1)
        ba = ba_ref[...]
        dbgv = dbg_ref[...]
        lane = lax.broadcasted_iota(jnp.int32, ba.shape, 1)
        beta = _sigmoid(ba)
        ea = -jnp.exp(al_ref[...])
        zin = ba + dt_ref[...]
        is_b = lane < DN_HEADS
        is_a = (lane >= DN_HEADS) & (lane < 2 * DN_HEADS)
        da = jnp.where(is_a, dbgv * ea * _sigmoid(zin), 0.0)
        dx_ref[:, _QKV_W:_QKV_W + DN_W] = dz_ref[...]
        dx_ref[:, _QKV_W + DN_W:] = jnp.where(is_b, dbgv * beta * (1.0 - beta), da).astype(dx_ref.dtype)
        ddt = jnp.sum(da, axis=0, keepdims=True)
        dal = jnp.sum(jnp.where(is_a, dbgv * ea * _softplus(zin), 0.0), axis=0, keepdims=True)

        @pl.when(first)
        def _():
            dw_ref[...] = dw
            ddt_ref[...] = ddt
            dal_ref[...] = dal

        @pl.when(i > 0)
        def _():
            dw_ref[...] += dw
            ddt_ref[...] += ddt
            dal_ref[...] += dal

    row = pl.BlockSpec((tm, DN_W), lambda i: (i, 0))
    nrow = pl.BlockSpec((HALO, DN_W), lambda i: (jnp.minimum((i + 1) * hb, t // HALO - 1), 0))
    one = pl.BlockSpec((1, 128), lambda i: (0, 0))
    return pl.pallas_call(
        body, name=name, grid=(nt,),
        in_specs=[pl.BlockSpec((tm, _QKV_W), lambda i: (i, 0)), pl.BlockSpec((tm, _QKV_W), lambda i: (i, 0)),
                  pl.BlockSpec((HALO, _QKV_W), lambda i: (jnp.minimum((i + 1) * hb, t // HALO - 1), 0)),
                  pl.BlockSpec((4, _QKV_W), lambda i: (0, 0)),
                  pl.BlockSpec((tm, 128), lambda i: (i, _BA_COL)), one, one,
                  row, row, row, nrow, nrow, nrow, pl.BlockSpec((tm, 128), lambda i: (i, 0)), row],
        out_specs=(pl.BlockSpec((tm, ODD_IN_PAD), lambda i: (i, 0)), pl.BlockSpec((4, _QKV_W), lambda i: (0, 0)), one, one),
        out_shape=(jax.ShapeDtypeStruct((t, ODD_IN_PAD), MXU_DTYPE), jax.ShapeDtypeStruct((4, _QKV_W), F32),
                   jax.ShapeDtypeStruct((1, 128), F32), jax.ShapeDtypeStruct((1, 128), F32)),
        compiler_params=_cp("arbitrary"))(proj, conv, conv, conv_w, proj, alog_row, dtb_row, dq, dk, dv, dq, dk, dv, dbg,
                                          dz)


def _chunk_masks():
    r = lax.broadcasted_iota(jnp.int32, (DN_CHUNK, DN_CHUNK), 0)
    c = lax.broadcasted_iota(jnp.int32, (DN_CHUNK, DN_CHUNK), 1)
    return r >= c, r > c


INV_PACK = 2


def _inv_unit_lower_many(mats):
    n = DN_CHUNK
    wide = INV_PACK * n
    r = lax.broadcasted_iota(jnp.int32, (wide, wide), 0)
    c = lax.broadcasted_iota(jnp.int32, (wide, wide), 1)
    same = (r & -n) == (c & -n)
    eye = jnp.where((r[:n] == (c[:n] & (n - 1))), 1.0, 0.0)

    def blockdiag(row):
        return jnp.where(same, jnp.concatenate([row] * INV_PACK, axis=0), 0.0)

    packs = [jnp.concatenate(mats[g:g + INV_PACK], axis=-1) for g in range(0, len(mats), INV_PACK)]
    xs = [eye - a for a in packs]
    pws = [_hi(a, blockdiag(a)) for a in packs]
    for step in range(5):
        if step < 4:
            both = [_hi(jnp.concatenate([x, pw], axis=0), blockdiag(pw)) for x, pw in zip(xs, pws)]
            xs = [x + b[:n] for x, b in zip(xs, both)]
            pws = [b[n:] for b in both]
        else:
            xs = [x + _hi(x, blockdiag(pw)) for x, pw in zip(xs, pws)]
    return [x[:, j * n:(j + 1) * n] for x in xs for j in range(INV_PACK)]


def _chunk_common(q, k, beta, gc, gcr, lower):
    gam = jnp.exp(jnp.where(lower, gc - gcr, NEG))
    eg = jnp.exp(gc)
    gl = gc[DN_CHUNK - 1:DN_CHUNK, :]
    kdf = jnp.exp(gl - gc)
    kb = k * beta
    bmat = _mx_nt(kb, k)
    qmat = _mx_nt(q, k)
    return gam, eg, jnp.exp(gl), kdf, kb, bmat, qmat


DN_STEP = 4


def gdn_fwd(q, k, v, bg, proj, o_gain, name):
    t = q.shape[0]
    n_chunks = t // DN_CHUNK

    def body(q_ref, k_ref, v_ref, bg_ref, z_ref, g_ref, o_ref, sall_ref, tall_ref, y_ref, yt_ref, s_ref):
        n = pl.program_id(0)

        @pl.when(n == 0)
        def _():
            s_ref[...] = jnp.zeros_like(s_ref)

        lower, strict = _chunk_masks()
        ltri = jnp.where(lower, 1.0, 0.0)
        hs = range(DN_HEADS)
        sl = [slice(h * DN_DIM, (h + 1) * DN_DIM) for h in hs]
        units = [(c, h) for c in range(DN_STEP) for h in hs]
        nu = range(len(units))
        rs = [slice(c * DN_CHUNK, (c + 1) * DN_CHUNK) for c in range(DN_STEP)]
        bgv = [bg_ref[rs[c], :] for c in range(DN_STEP)]
        gcs = [_hi(ltri, b) for b in bgv]
        gcs_t = [g.T for g in gcs]
        qh = [q_ref[rs[c], sl[h]] for c, h in units]
        kh = [k_ref[rs[c], sl[h]] for c, h in units]
        vh = [v_ref[rs[c], sl[h]] for c, h in units]
        beta = [bgv[c][:, h:h + 1] for c, h in units]
        com = [_chunk_common(qh[u], kh[u], beta[u], gcs[c][:, DN_HEADS + h:DN_HEADS + h + 1],
                             gcs_t[c][DN_HEADS + h:DN_HEADS + h + 1, :], lower) for u, (c, h) in enumerate(units)]
        gam, eg, dec, kdf, kb, bmat, qmat = zip(*com)
        tms = _inv_unit_lower_many([jnp.where(strict, bmat[u] * gam[u], 0.0) for u in nu])
        for u, (c, h) in enumerate(units):
            tall_ref[c, h] = tms[u]
        uw = [_hi(tms[u], jnp.concatenate([vh[u] * beta[u], kb[u] * eg[u]], axis=-1)) for u in nu]
        qd = [qh[u] * eg[u] for u in nu]
        pm = [qmat[u] * gam[u] for u in nu]
        kd = [kh[u] * kdf[u] for u in nu]
        st = [s_ref[h] for h in hs]
        for c in range(DN_STEP):
            us = [c * DN_HEADS + h for h in hs]
            for h in hs:
                sall_ref[c, h] = st[h]
            v_new = [uw[us[h]][:, :DN_DIM] - _mx(uw[us[h]][:, DN_DIM:], st[h]) for h in hs]
            o_st = [_mx(qd[us[h]], st[h]) for h in hs]
            o_in = [_mx(pm[us[h]], v_new[h]) for h in hs]
            s_up = [_mx_tn(kd[us[h]], v_new[h]) for h in hs]
            for h in hs:
                ov = o_st[h] + o_in[h]
                o_ref[rs[c], sl[h]] = ov
                zv = z_ref[rs[c], sl[h]]
                y = ov * lax.rsqrt(jnp.mean(ov * ov, axis=-1, keepdims=True) + EPS) * g_ref[...] * (zv * _sigmoid(zv))
                y_ref[rs[c], sl[h]] = y.astype(y_ref.dtype)
                yt_ref[sl[h], rs[c]] = y.T.astype(yt_ref.dtype)
            st = [st[h] * dec[us[h]] + s_up[h] for h in hs]
        for h in hs:
            s_ref[h] = st[h]

    rows = DN_STEP * DN_CHUNK
    row = pl.BlockSpec((rows, DN_W), lambda n: (n, 0))
    return pl.pallas_call(
        body, name=name, grid=(n_chunks // DN_STEP,),
        in_specs=[row, row, row, pl.BlockSpec((rows, 128), lambda n: (n, 0)),
                  pl.BlockSpec((rows, DN_W), lambda n: (n, _Z_COL)), pl.BlockSpec((1, DN_DIM), lambda n: (0, 0))],
        out_specs=(row, pl.BlockSpec((DN_STEP, DN_HEADS, DN_DIM, DN_DIM), lambda n: (n, 0, 0, 0)),
                   pl.BlockSpec((DN_STEP, DN_HEADS, DN_CHUNK, DN_CHUNK), lambda n: (n, 0, 0, 0)),
                   row, pl.BlockSpec((DN_W, rows), lambda n: (0, n))),
        out_shape=(jax.ShapeDtypeStruct((t, DN_W), F32),
                   jax.ShapeDtypeStruct((n_chunks, DN_HEADS, DN_DIM, DN_DIM), F32),
                   jax.ShapeDtypeStruct((n_chunks, DN_HEADS, DN_CHUNK, DN_CHUNK), F32),
                   jax.ShapeDtypeStruct((t, DN_W), MXU_DTYPE), jax.ShapeDtypeStruct((DN_W, t), MXU_DTYPE)),
        scratch_shapes=[pltpu.VMEM((DN_HEADS, DN_DIM, DN_DIM), F32)],
        compiler_params=_cp("arbitrary"))(q, k, v, bg, proj, o_gain)


def gdn_bwd(q, k, v, bg, sall, tall, do, name):
    t = q.shape[0]
    n_chunks = t // DN_CHUNK

    def body(q_ref, k_ref, v_ref, bg_ref, sall_ref, tall_ref, do_ref, dq_ref, dk_ref, dv_ref, dbg_ref, ds_ref):
        n = pl.program_id(0)

        @pl.when(n == 0)
        def _():
            ds_ref[...] = jnp.zeros_like(ds_ref)

        lower, strict = _chunk_masks()
        ltri = jnp.where(lower, 1.0, 0.0)
        bgv = bg_ref[...]
        gcs = _hi(ltri, bgv)
        gcs_t = gcs.T
        lane = lax.broadcasted_iota(jnp.int32, (DN_CHUNK, 128), 1)
        rowi = lax.broadcasted_iota(jnp.int32, (DN_CHUNK, 1), 0)
        hs = range(DN_HEADS)
        each = lambda fn, *ls: [fn(*a) for a in zip(*ls)]
        rsum = lambda a: jnp.sum(a, axis=-1, keepdims=True)
        sl = [slice(h * DN_DIM, (h + 1) * DN_DIM) for h in hs]
        st = [sall_ref[0, h] for h in hs]
        tms = [tall_ref[0, h] for h in hs]
        dsn = [ds_ref[h] for h in hs]
        qh = [q_ref[:, sl[h]] for h in hs]
        kh = [k_ref[:, sl[h]] for h in hs]
        vh = [v_ref[:, sl[h]] for h in hs]
        doh = [do_ref[:, sl[h]] for h in hs]
        beta = [bgv[:, h:h + 1] for h in hs]
        com = [_chunk_common(qh[h], kh[h], beta[h], gcs[:, DN_HEADS + h:DN_HEADS + h + 1],
                             gcs_t[DN_HEADS + h:DN_HEADS + h + 1, :], lower) for h in hs]
        gam, eg, dec, kdf, kb, bmat, qmat = zip(*com)
        rhs_w = each(lambda a, b: a * b, kb, eg)
        uw = each(lambda t_, v_, b_, r_: _hi(t_, jnp.concatenate([v_ * b_, r_], axis=-1)), tms, vh, beta, rhs_w)
        qd = each(lambda a, b: a * b, qh, eg)
        kd = each(lambda a, b: a * b, kh, kdf)
        pmat = each(lambda a, b: a * b, qmat, gam)
        v_new = each(lambda uw_, s_: uw_[:, :DN_DIM] - _mx(uw_[:, DN_DIM:], s_), uw, st)
        dqd = each(_mx_nt, doh, st)
        ds_o = each(_mx_tn, qd, doh)
        dp = each(lambda d_, v_: jnp.where(lower, _mx_nt(d_, v_), 0.0), doh, v_new)
        dvn_o = each(_mx_tn, pmat, doh)
        ddec = each(lambda d_, s_: jnp.sum(rsum(d_ * s_), axis=0, keepdims=True), dsn, st)
        dkd = each(_mx_nt, v_new, dsn)
        dvn = each(lambda a, k_, d_: a + _mx(k_, d_), dvn_o, kd, dsn)
        dw = each(lambda d_, s_: -_mx_nt(d_, s_), dvn, st)
        ds_w = each(lambda uw_, d_: _mx_tn(uw_[:, DN_DIM:], d_), uw, dvn)
        for h in hs:
            ds_ref[h] = ds_o[h] + dec[h] * dsn[h] - ds_w[h]
        dr = each(lambda t_, a, b: _hi_tn(t_, jnp.concatenate([a, b], axis=-1)), tms, dvn, dw)
        da = each(lambda r_, uw_: jnp.where(strict, -_hi_nt(r_, uw_), 0.0), dr, uw)
        dru = [r_[:, :DN_DIM] for r_ in dr]
        drw = [r_[:, DN_DIM:] for r_ in dr]
        db = each(lambda a, b: a * b, da, gam)
        dq_m = each(lambda a, b: a * b, dp, gam)
        e = each(lambda a, bm, p_, qm, g_: (a * bm + p_ * qm) * g_, da, bmat, dp, qmat, gam)
        dkb = each(lambda b_, k_, r_, e_: _mx(b_, k_) + r_ * e_, db, kh, drw, eg)
        dk = each(lambda b_, kb_, m_, q_, d_, f_: _mx_tn(b_, kb_) + _mx_tn(m_, q_) + d_ * f_, db, kb, dq_m, qh, dkd, kdf)
        dq = each(lambda m_, k_, d_, e_: _mx(m_, k_) + d_ * e_, dq_m, kh, dqd, eg)
        tk = each(lambda a, b: rsum(a * b), dkd, kd)
        dbeta_all = jnp.zeros((DN_CHUNK, 128), F32)
        dgc_all = jnp.zeros((DN_CHUNK, 128), F32)
        for h in hs:
            dgc = (jnp.sum(e[h], axis=1, keepdims=True) - jnp.sum(e[h].T, axis=1, keepdims=True)
                   + rsum(dqd[h] * qd[h]) - tk[h] + rsum(drw[h] * rhs_w[h]))
            dgl = jnp.sum(tk[h], axis=0, keepdims=True) + ddec[h] * dec[h]
            dgc = dgc + jnp.where(rowi == DN_CHUNK - 1, dgl, 0.0)
            dbeta = rsum(dru[h] * vh[h]) + rsum(dkb[h] * kh[h])
            dq_ref[:, sl[h]] = dq[h]
            dk_ref[:, sl[h]] = dk[h] + dkb[h] * beta[h]
            dv_ref[:, sl[h]] = dru[h] * beta[h]
            dbeta_all = jnp.where(lane == h, dbeta, dbeta_all)
            dgc_all = jnp.where(lane == DN_HEADS + h, dgc, dgc_all)
        dbg_ref[...] = dbeta_all + _hi_tn(ltri, dgc_all)

    rev = lambda n: (n_chunks - 1 - n, 0)
    row = pl.BlockSpec((DN_CHUNK, DN_W), rev)
    small = pl.BlockSpec((DN_CHUNK, 128), rev)
    return pl.pallas_call(
        body, name=name, grid=(n_chunks,),
        in_specs=[row, row, row, small,
                  pl.BlockSpec((1, DN_HEADS, DN_DIM, DN_DIM), lambda n: (n_chunks - 1 - n, 0, 0, 0)),
                  pl.BlockSpec((1, DN_HEADS, DN_CHUNK, DN_CHUNK), lambda n: (n_chunks - 1 - n, 0, 0, 0)), row],
        out_specs=(row, row, row, small),
        out_shape=(jax.ShapeDtypeStruct((t, DN_W), F32),) * 3 + (jax.ShapeDtypeStruct((t, 128), F32),),
        scratch_shapes=[pltpu.VMEM((DN_HEADS, DN_DIM, DN_DIM), F32)],
        compiler_params=_cp("arbitrary"))(q, k, v, bg, sall, tall, do)


def gdn_out_bwd(o, proj, o_gain, dx, w_out, name, tm=512, after=None):
    t = o.shape[0]
    tm = min(tm, t)

    def body(o_ref, z_ref, g_ref, dx_ref, w_ref, *rest):
        do_ref, dz_ref, dg_ref = rest[-3:]
        i = pl.program_id(0)
        dy = _mx_nt(dx_ref[...], w_ref[...])
        dg = jnp.zeros((1, DN_DIM), F32)
        for h in range(DN_HEADS):
            sl = slice(h * DN_DIM, (h + 1) * DN_DIM)
            ov, zv, dyv = o_ref[:, sl], z_ref[:, sl], dy[:, sl]
            r = lax.rsqrt(jnp.mean(ov * ov, axis=-1, keepdims=True) + EPS)
            oh = ov * r
            sg = _sigmoid(zv)
            dz_ref[:, sl] = (dyv * oh * g_ref[...] * sg * (1.0 + zv * (1.0 - sg))).astype(dz_ref.dtype)
            don = dyv * (zv * sg)
            dg = dg + jnp.sum(don * oh, axis=0, keepdims=True)
            doh = don * g_ref[...]
            do_ref[:, sl] = r * (doh - oh * jnp.mean(doh * oh, axis=-1, keepdims=True))

        @pl.when(i == 0)
        def _():
            dg_ref[...] = dg

        @pl.when(i > 0)
        def _():
            dg_ref[...] += dg

    row = pl.BlockSpec((tm, DN_W), lambda i: (i, 0))
    one = pl.BlockSpec((1, DN_DIM), lambda i: (0, 0))
    in_specs = [row, pl.BlockSpec((tm, DN_W), lambda i: (i, _Z_COL)), one,
                pl.BlockSpec((tm, dx.shape[1]), lambda i: (i, 0)), pl.BlockSpec(w_out.shape, lambda i: (0, 0))]
    args = [o, proj, o_gain, dx, w_out]
    if after is not None:
        in_specs.append(pl.BlockSpec(memory_space=pl.ANY))
        args.append(after)
    return pl.pallas_call(
        body, name=name, grid=(t // tm,), in_specs=in_specs, out_specs=(row, row, one),
        out_shape=(jax.ShapeDtypeStruct((t, DN_W), F32), jax.ShapeDtypeStruct((t, DN_W), MXU_DTYPE),
                   jax.ShapeDtypeStruct((1, DN_DIM), F32)),
        compiler_params=_cp("arbitrary"))(*args)


def _peer(k):
    x, y, c = lax.axis_index("x"), lax.axis_index("y"), lax.axis_index("c")
    px = 1 - x if k & 4 else x
    py = 1 - y if k & 2 else y
    pc = 1 - c if k & 1 else c
    return (px, py, pc), 4 * px + 2 * py + pc


_HBM = pl.BlockSpec(memory_space=pltpu.HBM)
_SEM = pl.BlockSpec(memory_space=pltpu.SEMAPHORE)
_DATAFLOW = pltpu.SideEffectType.DATAFLOW_SIDE_EFFECTING
N_PEER = N_DEV - 1


def join_blocks(parts, rows, name, tc=256):
    n, r, c = parts.shape

    def body(p_ref, o_ref):
        for j in range(n):
            o_ref[j * r:(j + 1) * r, :] = p_ref[j]
        if rows > n * r:
            o_ref[n * r:, :] = jnp.zeros((rows - n * r, tc), o_ref.dtype)

    return pl.pallas_call(
        body, name=name, grid=(c // tc,), in_specs=[pl.BlockSpec((n, r, tc), lambda i: (0, 0, i))],
        out_specs=pl.BlockSpec((rows, tc), lambda i: (0, i)), out_shape=jax.ShapeDtypeStruct((rows, c), parts.dtype),
        compiler_params=_cp("parallel"))(parts)


def split_blocks(whole, n, r, name, tc=256):
    rows, c = whole.shape

    def body(w_ref, o_ref):
        for j in range(n):
            o_ref[j] = w_ref[j * r:(j + 1) * r, :]

    return pl.pallas_call(
        body, name=name, grid=(c // tc,), in_specs=[pl.BlockSpec((rows, tc), lambda i: (0, i))],
        out_specs=pl.BlockSpec((n, r, tc), lambda i: (0, 0, i)), out_shape=jax.ShapeDtypeStruct((n, r, c), whole.dtype),
        compiler_params=_cp("parallel"))(whole)


def send_start(srcs, name, scatter, after):
    na = len(srcs)
    ns = (2 * N_PEER + 1) * na
    lands = [lax.empty((N_DEV,) + (s.shape[1:] if scatter else s.shape), s.dtype) for s in srcs]
    extra = [] if after is None else [after]

    def body(*refs):
        src_refs, land_refs = refs[:na], refs[na:2 * na]
        sems = refs[2 * na + len(extra):2 * na + len(extra) + ns]
        land_out, token = refs[-1 - na:-1], refs[-1]
        _, me = _peer(0)
        for a in range(na):
            pltpu.make_async_copy(src_refs[a].at[me] if scatter else src_refs[a], land_out[a].at[me],
                                  sems[2 * N_PEER * na + a]).start()
        peers = [_peer(k) for k in range(1, N_DEV)]
        for a in range(na):
            for k, (peer, pid) in enumerate(peers):
                pltpu.make_async_remote_copy(
                    src_ref=src_refs[a].at[pid] if scatter else src_refs[a], dst_ref=land_refs[a].at[me],
                    send_sem=sems[2 * (a * N_PEER + k)], recv_sem=sems[2 * (a * N_PEER + k) + 1],
                    device_id=peer, device_id_type=MESH).start()
        token[...] = jnp.zeros_like(token)

    hbm = lambda arrs: tuple(pltpu.HBM(a.shape, a.dtype) for a in arrs)
    outs = pl.pallas_call(
        body, name=name,
        out_shape=(pltpu.SemaphoreType.DMA(()),) * ns + hbm(srcs) + hbm(lands) + (jax.ShapeDtypeStruct((8, 128), F32),),
        in_specs=[_HBM] * (2 * na) + [pl.BlockSpec(memory_space=pl.ANY)] * len(extra),
        out_specs=(_SEM,) * ns + (_HBM,) * (2 * na) + (pl.BlockSpec(memory_space=pltpu.VMEM),),
        input_output_aliases={i: ns + i for i in range(2 * na)},
        compiler_params=pltpu.CompilerParams(has_side_effects=_DATAFLOW),
    )(*[pltpu.with_memory_space_constraint(a, pltpu.HBM) for a in list(srcs) + lands], *extra)
    return outs[:ns], outs[ns:ns + na], outs[ns + na:ns + 2 * na], outs[-1]


def send_wait(sems, srcs_thru, lands_thru, name, scatter, after):
    na = len(srcs_thru)
    ns = (2 * N_PEER + 1) * na

    def body(*refs):
        src_refs, land_refs, sm = refs[:na], refs[na:2 * na], refs[2 * na:2 * na + ns]
        _, me = _peer(0)
        for a in range(na):
            pltpu.make_async_copy(src_refs[a].at[me] if scatter else src_refs[a], land_refs[a].at[me],
                                  sm[2 * N_PEER * na + a]).wait()
        for k in range(1, N_DEV):
            peer, pid = _peer(k)
            for a in range(na):
                cp = pltpu.make_async_remote_copy(
                    src_ref=src_refs[a].at[pid] if scatter else src_refs[a], dst_ref=land_refs[a].at[pid],
                    send_sem=sm[2 * (a * N_PEER + k - 1)], recv_sem=sm[2 * (a * N_PEER + k - 1) + 1],
                    device_id=peer, device_id_type=MESH)
                cp.wait_send()
                cp.wait_recv()

    hbm = lambda arrs: tuple(pltpu.HBM(a.shape, a.dtype) for a in arrs)
    outs = pl.pallas_call(
        body, name=name, out_shape=hbm(srcs_thru) + hbm(lands_thru),
        in_specs=[_HBM] * (2 * na) + [_SEM] * ns + [pl.BlockSpec(memory_space=pl.ANY)], out_specs=(_HBM,) * (2 * na),
        input_output_aliases={i: i for i in range(2 * na)},
        compiler_params=pltpu.CompilerParams(has_side_effects=_DATAFLOW),
    )(*srcs_thru, *lands_thru, *sems, after)
    return outs[na:]


def _adamw(w, g, m, v):
    m = ADAM_B1 * m + (1.0 - ADAM_B1) * g
    v = ADAM_B2 * v + (1.0 - ADAM_B2) * (g * g)
    m_hat = m / (1.0 - ADAM_B1 ** ADAM_STEP)
    v_hat = v / (1.0 - ADAM_B2 ** ADAM_STEP)
    return -ADAM_LR * (m_hat / (jnp.sqrt(v_hat) + ADAM_EPS) + ADAM_WD * w), m, v


def adam_sum(w, pieces, m, v, name, layer=0, into=None):
    nl, r, c = w.shape
    tc = _pick(c, 256)

    def body(w_ref, p_ref, m_ref, v_ref, *rest):
        g_ref, d_ref, nm_ref, nv_ref = rest[-4:]
        g = p_ref[0].astype(F32)
        for s in range(1, N_DEV):
            g = g + p_ref[s].astype(F32)
        g_ref[0] = g
        d_ref[0], nm_ref[0], nv_ref[0] = _adamw(w_ref[0], g, m_ref[0], v_ref[0])

    row = pl.BlockSpec((1, r, tc), lambda i: (layer, 0, i))
    out = jax.ShapeDtypeStruct((nl, r, c), F32)
    extra = [] if into is None else list(into)
    return pl.pallas_call(
        body, name=name, grid=(c // tc,),
        in_specs=[row, pl.BlockSpec((N_DEV, r, tc), lambda i: (0, 0, i)), row, row]
        + [pl.BlockSpec(memory_space=pl.ANY)] * len(extra),
        out_specs=(row,) * 4, out_shape=(out,) * 4,
        input_output_aliases={4 + i: i for i in range(len(extra))},
        compiler_params=_cp("parallel"))(w, pieces, m, v, *extra)


def sum_rows(gathered, name):
    _, r, c = gathered.shape

    def body(p_ref, o_ref):
        g = p_ref[0]
        for s in range(1, N_DEV):
            g = g + p_ref[s]
        o_ref[...] = g

    return pl.pallas_call(body, name=name, out_shape=jax.ShapeDtypeStruct((r, c), F32))(gathered)


def adam_small(w, g, m, v, name):
    def body(w_ref, g_ref, m_ref, v_ref, d_ref, nm_ref, nv_ref):
        d_ref[...], nm_ref[...], nv_ref[...] = _adamw(w_ref[...], g_ref[...], m_ref[...], v_ref[...])

    out = jax.ShapeDtypeStruct(w.shape, F32)
    return pl.pallas_call(body, name=name, out_shape=(out,) * 3)(w, g, m, v)


def _rope_tables(t):
    inv_freq = 10000.0 ** (-jnp.arange(0, HEAD_DIM, 2, dtype=F32) / HEAD_DIM)
    lanes = lambda a: jnp.concatenate([a] * (128 // a.shape[-1]), axis=-1)
    base = (jnp.arange(t // QK_TM, dtype=F32) * QK_TM)[:, None] * inv_freq[None, :]
    offs = jnp.arange(QK_TM, dtype=F32)[:, None] * inv_freq[None, :]
    return (lanes(jnp.cos(base))[:, None, :], lanes(jnp.sin(base))[:, None, :], lanes(jnp.cos(offs)), lanes(jnp.sin(offs)))


def _lane_row(vec8):
    return jnp.pad(vec8.reshape(1, DN_HEADS), ((0, 0), (DN_HEADS, 128 - 2 * DN_HEADS)))


def _ffn_bwd(x, norm_g, w_gu, w_d, saved, dy, tag, after=None):
    ft, gu, at = saved
    dgu = ffn_dact(dy, w_d, gu, f"{tag}_d_gate_up", after=after)
    dwd = mm_at(at, dy, f"{tag}_dw_down")
    nj = D_FF // GU_TILE
    dwgu = mm_at(ft, dgu, f"{tag}_dw_gate_up", transposed=True, tn=GU_TILE, row_block=lambda q: (q % 2) * nj + q // 2)
    dx, dg = mm_rms_bwd(dgu, w_gu, x, norm_g, dy, f"{tag}_d_norm", chunks=_GU_CHUNKS)
    return dx, dwgu, dwd, dg


def local_step(x, target, small, weights_of, grads_out, after=None):
    t = x.shape[0]
    rope = _rope_tables(t)
    alog_row, dtb_row = _lane_row(small["odd_a_log"]), _lane_row(small["odd_dt_bias"])

    h0, h0t = rms_fwd(x, small["even_norm"], "even_norm", after=after)
    we = weights_of("even", h0)
    small = {**small, **we.get("small", {})}
    proj0 = mm_nt(h0, we["w_in"], "even_in_proj", after=we.get("after"))
    qr, kr = qk_prep_fwd(proj0, small["even_q_gain"], small["even_k_gain"], rope, "even_qk_prep")
    y_attn, mix0, mix0_t, lse = swa_fwd(qr, kr, proj0, small["even_sinks"], "even_swa")
    mix0, mix0_t = gconv_fwd(proj0, small["even_conv_w"], mix0, mix0_t, "even_gconv")
    we = {**we, **weights_of("even_out", mix0)}
    x1, f0, f0t = mm_nn_res_norm(mix0, we["w_out"], x, small["ffn_norm0"], "even_out_proj")
    w0 = weights_of("ffn0", x1)
    gu0, a0, a0t = ffn_up(f0, w0["gate_up"], "ffn0_gate_up")
    ffn0 = (f0t, gu0, a0t)
    x2, h1, h1t = mm_nn_res_norm(a0, w0["down"], x1, small["odd_norm"], "ffn0_down")

    wo = weights_of("odd", x2)
    proj1 = mm_nt(h1, wo["w_in"], "odd_in_proj")
    qn, kn, vs, bg, conv1 = gdn_prep_fwd(proj1, small["odd_conv_w"], alog_row, dtb_row, "odd_prep")
    o, sall, tall, og, ogt = gdn_fwd(qn, kn, vs, bg, proj1, small["odd_o_gain"], "odd_delta_rule")
    x3, f1, f1t = mm_nn_res_norm(og, wo["w_out"], x2, small["ffn_norm1"], "odd_out_proj")
    w1 = weights_of("ffn1", x3)
    gu1, a1, a1t = ffn_up(f1, w1["gate_up"], "ffn1_gate_up")
    ffn1 = (f1t, gu1, a1t)
    dy, loss_row = mm_nn_res_loss(a1, w1["down"], x3, target, "ffn1_down_loss")

    gs = {}
    dx3, dwgu, dwd, gs["ffn_norm1"] = _ffn_bwd(x3, small["ffn_norm1"], w1["gate_up"], w1["down"], ffn1, dy, "ffn1")
    tok = grads_out("ffn1", {"gate_up": dwgu, "down": dwd})

    do, dz, gs["odd_o_gain"] = gdn_out_bwd(o, proj1, small["odd_o_gain"], dx3, wo["w_out"], "odd_d_gate_norm", after=tok)
    dwo = mm_at(ogt, dx3, "odd_dw_out")
    dqn, dkn, dvs, dbg = gdn_bwd(qn, kn, vs, bg, sall, tall, do, "odd_d_delta_rule")
    dproj1, gs["odd_conv_w"], ddt_row, dal_row = gdn_prep_bwd(
        proj1, conv1, small["odd_conv_w"], alog_row, dtb_row, dqn, dkn, dvs, dbg, dz, "odd_d_prep")
    gs["odd_dt_bias"] = ddt_row[:, DN_HEADS:2 * DN_HEADS]
    gs["odd_a_log"] = dal_row[:, DN_HEADS:2 * DN_HEADS]
    dwi = mm_at(h1t, dproj1, "odd_dw_in", transposed=True)
    dx2, gs["odd_norm"] = mm_rms_bwd(dproj1, wo["w_in"], x2, small["odd_norm"], dx3, "odd_d_norm")
    tok = grads_out("odd", {"w_in": dwi, "w_out": dwo})

    dx1, dwgu, dwd, gs["ffn_norm0"] = _ffn_bwd(x1, small["ffn_norm0"], w0["gate_up"], w0["down"], ffn0, dx2, "ffn0",
                                               after=tok)
    tok = grads_out("ffn0", {"gate_up": dwgu, "down": dwd})

    dmix = mm_nt(dx1, we["w_out"], "even_d_mix", after=tok)
    dwo = mm_at(mix0_t, dx1, "even_dw_out")
    dproj0 = lax.empty((t, EVEN_IN_W), MXU_DTYPE)
    dqr, dkr, dproj0, gs["even_sinks"] = swa_bwd(
        qr, kr, proj0, small["even_sinks"], y_attn, lse, dmix, dproj0, "even_d_swa")
    dproj0, gs["even_q_gain"], gs["even_k_gain"] = qk_prep_bwd(
        proj0, small["even_q_gain"], small["even_k_gain"], rope, dqr, dkr, dproj0, "even_d_qk_prep")
    dproj0, gs["even_conv_w"] = gconv_bwd(proj0, small["even_conv_w"], dmix, dproj0, "even_d_gconv")
    dwi = mm_at(h0t, dproj0, "even_dw_in", transposed=True, chunks=_EVEN_D_CHUNKS)
    tok = grads_out("even", {"w_in": dwi, "w_out": dwo})
    grad_x, gs["even_norm"] = mm_rms_bwd(dproj0, we["w_in"], x, small["even_norm"], dx1, "even_d_norm", after=tok,
                                         chunks=_EVEN_D_CHUNKS)
    return loss_row, grad_x, gs


_SMALL_ORDER = ("even_norm", "even_q_gain", "even_k_gain", "even_sinks", "odd_a_log", "odd_dt_bias", "odd_o_gain",
                "ffn_norm0", "ffn_norm1", "odd_norm", "even_conv_w", "odd_conv_w")
_SMALL_SIZE = {"even_norm": 1024, "even_q_gain": 64, "even_k_gain": 64, "even_sinks": 8, "odd_a_log": 8,
               "odd_dt_bias": 8, "odd_o_gain": 128, "ffn_norm0": 1024, "ffn_norm1": 1024, "odd_norm": 1024,
               "even_conv_w": 3 * 512, "odd_conv_w": 4 * 3072}
_N_REPL = 9


def _pack_rows(vals):
    flat = jnp.concatenate([v.reshape(-1) for v in vals])
    pad = (-flat.shape[0]) % 1024
    return jnp.pad(flat, (0, pad)).reshape(-1, 128)


def _my_block(full, size, axis):
    me = 4 * lax.axis_index("x") + 2 * lax.axis_index("y") + lax.axis_index("c")
    return lax.dynamic_slice_in_dim(full, me * size, size, axis=axis)


def kernel(x, even_norm, even_w_in, even_q_gain, even_k_gain, even_sinks, even_conv_w, even_w_out, odd_norm, odd_w_in, odd_conv_w, odd_a_log, odd_dt_bias, odd_o_gain, odd_w_out, ffn_norm, ffn_w_gate_up, ffn_w_down, loss_target, m_even_norm, m_even_w_in, m_even_q_gain, m_even_k_gain, m_even_sinks, m_even_conv_w, m_even_w_out, m_odd_norm, m_odd_w_in, m_odd_conv_w, m_odd_a_log, m_odd_dt_bias, m_odd_o_gain, m_odd_w_out, m_ffn_norm, m_ffn_w_gate_up, m_ffn_w_down, v_even_norm, v_even_w_in, v_even_q_gain, v_even_k_gain, v_even_sinks, v_even_conv_w, v_even_w_out, v_odd_norm, v_odd_w_in, v_odd_conv_w, v_odd_a_log, v_odd_dt_bias, v_odd_o_gain, v_odd_w_out, v_ffn_norm, v_ffn_w_gate_up, v_ffn_w_down):
    t = x.shape[1]
    d = D_MODEL

    tr = lambda a: jnp.swapaxes(a, 1, 2)
    shard = {
        "even": {"w_in": tr(even_w_in)[0], "w_out": even_w_out[0]},
        "ffn0": {"gate_up": tr(ffn_w_gate_up)[0], "down": ffn_w_down[0]},
        "odd": {"w_in": tr(odd_w_in)[0], "w_out": odd_w_out[0]},
        "ffn1": {"gate_up": tr(ffn_w_gate_up)[1], "down": ffn_w_down[1]},
    }
    given = {
        ("even", "w_in"): ("even_w_in", even_w_in, m_even_w_in, v_even_w_in, 0),
        ("even", "w_out"): ("even_w_out", even_w_out, m_even_w_out, v_even_w_out, 0),
        ("odd", "w_in"): ("odd_w_in", odd_w_in, m_odd_w_in, v_odd_w_in, 0),
        ("odd", "w_out"): ("odd_w_out", odd_w_out, m_odd_w_out, v_odd_w_out, 0),
        ("ffn0", "gate_up"): ("ffn_w_gate_up", ffn_w_gate_up, m_ffn_w_gate_up, v_ffn_w_gate_up, 0),
        ("ffn1", "gate_up"): ("ffn_w_gate_up", ffn_w_gate_up, m_ffn_w_gate_up, v_ffn_w_gate_up, 1),
        ("ffn0", "down"): ("ffn_w_down", ffn_w_down, m_ffn_w_down, v_ffn_w_down, 0),
        ("ffn1", "down"): ("ffn_w_down", ffn_w_down, m_ffn_w_down, v_ffn_w_down, 1),
    }

    def whole(group, parts):
        col, row = tuple(shard[group])
        if group == "odd":
            w_col = join_blocks(parts[0], ODD_IN_PAD, "odd_w_in_join")
        else:
            w_col = parts[0].reshape(-1, d)
        return {col: w_col, row: parts[1].reshape(-1, d)}

    wire = {g: [a.astype(MXU_DTYPE) for a in shard[g].values()] for g in shard}
    wire["even_out"] = [wire["even"].pop()]
    wire["even"].append(_pack_rows([odd_norm, even_conv_w, odd_conv_w]))
    sems, srcs_thru, lands_thru, tok = send_start(wire["even"], "gather_even_start", False, None)
    gathers = {"even": (sems, srcs_thru, lands_thru)}

    def start_rest(after):
        order = ("even_out", "ffn0", "odd", "ffn1")
        sems, srcs_thru, lands_thru, token = send_start([a for g in order for a in wire[g]], "gather_rest_start", False,
                                                        after)
        a0, na = 0, sum(len(wire[g]) for g in order)
        for g in order:
            a1 = a0 + len(wire[g])
            gathers[g] = (sems[2 * N_PEER * a0:2 * N_PEER * a1] + sems[2 * N_PEER * na + a0:2 * N_PEER * na + a1],
                          srcs_thru[a0:a1], lands_thru[a0:a1])
            a0 = a1
        return token

    o1 = d // N_DEV
    o2 = o1 + 3 * CONV_CH // N_DEV

    def weights_of(group, after):
        lands = send_wait(*gathers[group], f"gather_{group}_wait", False, after)
        if group == "even_out":
            return {"w_out": lands[0].reshape(-1, d)}
        if group != "even":
            return whole(group, lands)
        sg = lands[1].reshape(N_DEV, -1)
        return {"w_in": lands[0].reshape(-1, d), "after": start_rest(lands[0]), "small": {
            "odd_norm": sg[:, :o1].reshape(1, d),
            "even_conv_w": sg[:, o1:o2].reshape(N_DEV, 3, CONV_CH // N_DEV).transpose(1, 0, 2).reshape(3, CONV_CH),
            "odd_conv_w": sg[:, o2:o2 + 4 * _QKV_W // N_DEV].reshape(N_DEV, 4, _QKV_W // N_DEV)
            .transpose(1, 0, 2).reshape(4, _QKV_W)}}

    sent = {}

    def grads_out(group, dws):
        col, row = tuple(shard[group])
        c = shard[group][col].shape[0]
        pieces = [split_blocks(dws[col], N_DEV, c, "odd_dw_in_split") if group == "odd"
                  else dws[col].reshape((N_DEV,) + shard[group][col].shape),
                  dws[row].reshape((N_DEV,) + shard[group][row].shape)]
        sems, srcs_thru, lands_thru, token = send_start(pieces, f"exchange_{group}_start", True, None)
        sent[group] = (sems, srcs_thru, lands_thru, pieces)
        return token

    small = {
        "even_norm": even_norm, "even_q_gain": even_q_gain, "even_k_gain": even_k_gain, "even_sinks": even_sinks,
        "odd_a_log": odd_a_log.reshape(-1), "odd_dt_bias": odd_dt_bias.reshape(-1), "odd_o_gain": odd_o_gain,
        "ffn_norm0": ffn_norm[0:1], "ffn_norm1": ffn_norm[1:2],
    }

    loss_row, grad_x, gs = local_step(x.reshape(t, d), loss_target.reshape(t, d), small, weights_of, grads_out, after=tok)

    rows = _pack_rows([gs[n] for n in _SMALL_ORDER] + [loss_row[:, 0:1]])
    small_sent = send_start([rows], "gather_small_grads_start", False, None)

    res, behind = {}, small_sent[3]
    for g in ("ffn1", "odd", "ffn0", "even"):
        sems, srcs_thru, lands_thru, pieces = sent[g]
        lands = send_wait(sems, srcs_thru, lands_thru, f"exchange_{g}_wait", True, behind)
        for i, (key, pcs) in enumerate(zip(shard[g], lands)):
            name, w_, m_, v_, layer = given[g, key]
            view = tr if i == 0 else (lambda a: a)
            res[name] = adam_sum(view(w_), pcs, view(m_), view(v_), f"adamw_{g}_{key}", layer=layer, into=res.get(name))
        behind = res[name][0]
    for name in ("even_w_in", "odd_w_in", "ffn_w_gate_up"):
        res[name] = tuple(tr(a) for a in res[name])

    (rows_g,) = send_wait(*small_sent[:3], "gather_small_grads_wait", False, behind)
    tot = sum_rows(rows_g, "sum_small_grads").reshape(-1)
    off, sgrad = 0, {}
    for n in _SMALL_ORDER:
        sgrad[n] = tot[off:off + _SMALL_SIZE[n]]
        off += _SMALL_SIZE[n]
    loss = tot[off]

    repl = _SMALL_ORDER[:_N_REPL]
    repl_w = {"even_norm": even_norm, "even_q_gain": even_q_gain, "even_k_gain": even_k_gain, "even_sinks": even_sinks,
              "odd_a_log": odd_a_log, "odd_dt_bias": odd_dt_bias, "odd_o_gain": odd_o_gain,
              "ffn_norm0": ffn_norm[0], "ffn_norm1": ffn_norm[1]}
    repl_m = {"even_norm": m_even_norm, "even_q_gain": m_even_q_gain, "even_k_gain": m_even_k_gain,
              "even_sinks": m_even_sinks, "odd_a_log": m_odd_a_log, "odd_dt_bias": m_odd_dt_bias,
              "odd_o_gain": m_odd_o_gain, "ffn_norm0": m_ffn_norm[0], "ffn_norm1": m_ffn_norm[1]}
    repl_v = {"even_norm": v_even_norm, "even_q_gain": v_even_q_gain, "even_k_gain": v_even_k_gain,
              "even_sinks": v_even_sinks, "odd_a_log": v_odd_a_log, "odd_dt_bias": v_odd_dt_bias,
              "odd_o_gain": v_odd_o_gain, "ffn_norm0": v_ffn_norm[0], "ffn_norm1": v_ffn_norm[1]}
    pk = lambda dct: _pack_rows([dct[n] for n in repl])
    pd_, pm_, pv_ = adam_small(pk(repl_w), pk(sgrad), pk(repl_m), pk(repl_v), "adamw_replicated")
    sres = {}
    off = 0
    for n in repl:
        sz = _SMALL_SIZE[n]
        sres[n] = (sgrad[n], pd_.reshape(-1)[off:off + sz], pm_.reshape(-1)[off:off + sz], pv_.reshape(-1)[off:off + sz])
        off += sz
    g_on = _my_block(sgrad["odd_norm"].reshape(1, d), d // N_DEV, 1)
    g_ec = _my_block(sgrad["even_conv_w"].reshape(3, CONV_CH), CONV_CH // N_DEV, 1)
    g_oc = _my_block(sgrad["odd_conv_w"].reshape(4, _QKV_W), _QKV_W // N_DEV, 1)
    shard_w = _pack_rows([odd_norm, even_conv_w, odd_conv_w])
    sd_, sm_, sv_ = adam_small(shard_w, _pack_rows([g_on, g_ec, g_oc]),
                               _pack_rows([m_odd_norm, m_even_conv_w, m_odd_conv_w]),
                               _pack_rows([v_odd_norm, v_even_conv_w, v_odd_conv_w]), "adamw_sharded_small")
    off = 0
    for n, gfull, like in (("odd_norm", g_on, odd_norm), ("even_conv_w", g_ec, even_conv_w), ("odd_conv_w", g_oc, odd_conv_w)):
        sz = like.size
        sres[n] = (gfull, sd_.reshape(-1)[off:off + sz], sm_.reshape(-1)[off:off + sz], sv_.reshape(-1)[off:off + sz])
        off += sz

    def small_out(name, like, kind):
        if name == "ffn_norm":
            return jnp.stack([sres["ffn_norm0"][kind], sres["ffn_norm1"][kind]]).reshape(like.shape)
        return sres[name][kind].reshape(like.shape)

    order = (("even_norm", even_norm), ("even_w_in", even_w_in), ("even_q_gain", even_q_gain),
             ("even_k_gain", even_k_gain), ("even_sinks", even_sinks), ("even_conv_w", even_conv_w),
             ("even_w_out", even_w_out), ("odd_norm", odd_norm), ("odd_w_in", odd_w_in), ("odd_conv_w", odd_conv_w),
             ("odd_a_log", odd_a_log), ("odd_dt_bias", odd_dt_bias), ("odd_o_gain", odd_o_gain),
             ("odd_w_out", odd_w_out), ("ffn_norm", ffn_norm), ("ffn_w_gate_up", ffn_w_gate_up),
             ("ffn_w_down", ffn_w_down))
    outs = [loss, grad_x.reshape(x.shape)]
    for kind in range(4):
        for name, like in order:
            outs.append(res[name][kind] if name in res else small_out(name, like, kind))
    return tuple(outs)
```

```python
import jax
import jax.numpy as jnp
import numpy as np
from jax import lax
from jax.experimental import pallas as pl
from jax.experimental.pallas import tpu as pltpu

F32 = jnp.float32
MXU_DTYPE = jnp.bfloat16
HI = lax.Precision.HIGH
EPS = 1e-6
N_DEV = 8
D_MODEL = 1024
HEAD_DIM = 64
ATTN_HEADS = 8
KV_HEADS = 2
ATTN_BLOCK = 128
Q_W = 512
KV_W = 128
CONV_CH = 512
EVEN_IN_W = 2304
DN_HEADS = 8
DN_DIM = 128
DN_W = 1024
DN_CHUNK = 64
ODD_IN_W = 4112
ODD_IN_PAD = 4224
D_FF = 2816
NEG = -1e30
VMEM_LIMIT = 56 * 1024 * 1024
ADAM_LR, ADAM_B1, ADAM_B2, ADAM_EPS, ADAM_WD, ADAM_STEP = 0.001, 0.9, 0.999, 1e-08, 0.01, 10
MESH = pl.DeviceIdType.MESH


def _cp(*sem):
    return pltpu.CompilerParams(dimension_semantics=sem, vmem_limit_bytes=VMEM_LIMIT)


def _pick(n, cap):
    best = 128
    for t in range(128, cap + 1, 128):
        if n % t == 0:
            best = t
    return best


def _mx(a, b):
    return jnp.dot(a.astype(MXU_DTYPE), b.astype(MXU_DTYPE), preferred_element_type=F32)


def _mx_nt(a, b):
    return lax.dot_general(a.astype(MXU_DTYPE), b.astype(MXU_DTYPE), (((1,), (1,)), ((), ())),
                           preferred_element_type=F32)


def _mx_tn(a, b):
    return lax.dot_general(a.astype(MXU_DTYPE), b.astype(MXU_DTYPE), (((0,), (0,)), ((), ())),
                           preferred_element_type=F32)


def _hi(a, b):
    return jnp.dot(a, b, precision=HI, preferred_element_type=F32)


def _hi_nt(a, b):
    return lax.dot_general(a, b, (((1,), (1,)), ((), ())), precision=HI, preferred_element_type=F32)


def _hi_tn(a, b):
    return lax.dot_general(a, b, (((0,), (0,)), ((), ())), precision=HI, preferred_element_type=F32)


def _sigmoid(x):
    return 0.5 * jnp.tanh(0.5 * x) + 0.5


def _softplus(x):
    return jnp.maximum(x, 0.0) + jnp.log(1.0 + jnp.exp(-jnp.abs(x)))


def mm_nn_res_norm(a, b, res, g, name, tm=512):
    t, k = a.shape
    d = b.shape[1]
    tm = min(tm, t)

    def body(a_ref, b_ref, res_ref, g_ref, y_ref, h_ref, ht_ref):
        y = res_ref[...] + _mx(a_ref[...], b_ref[...])
        y_ref[...] = y
        h = y * lax.rsqrt(jnp.mean(y * y, axis=-1, keepdims=True) + EPS) * g_ref[...]
        h_ref[...] = h.astype(h_ref.dtype)
        ht_ref[...] = h.T.astype(ht_ref.dtype)

    row = pl.BlockSpec((tm, d), lambda i: (i, 0))
    return pl.pallas_call(
        body, name=name, grid=(t // tm,),
        in_specs=[pl.BlockSpec((tm, k), lambda i: (i, 0)), pl.BlockSpec((k, d), lambda i: (0, 0)), row,
                  pl.BlockSpec((1, d), lambda i: (0, 0))],
        out_specs=(row, row, pl.BlockSpec((d, tm), lambda i: (0, i))),
        out_shape=(jax.ShapeDtypeStruct((t, d), F32), jax.ShapeDtypeStruct((t, d), MXU_DTYPE),
                   jax.ShapeDtypeStruct((d, t), MXU_DTYPE)),
        compiler_params=_cp("parallel"))(a, b, res, g)


def mm_nt(a, b, name, out_dtype=F32, tm=2048, after=None):
    m, k = a.shape
    n, _ = b.shape
    tn = _pick(n, 512 if k > 3000 else 1536)
    tm = min(tm, m)

    def body(a_ref, b_ref, *rest):
        o_ref = rest[-1]
        o_ref[...] = _mx_nt(a_ref[...], b_ref[...]).astype(o_ref.dtype)

    in_specs = [pl.BlockSpec((tm, k), lambda j, i: (i, 0)), pl.BlockSpec((tn, k), lambda j, i: (j, 0))]
    args = [a, b]
    if after is not None:
        in_specs.append(pl.BlockSpec(memory_space=pl.ANY))
        args.append(after)
    return pl.pallas_call(
        body, name=name, grid=(n // tn, m // tm), in_specs=in_specs,
        out_specs=pl.BlockSpec((tm, tn), lambda j, i: (i, j)),
        out_shape=jax.ShapeDtypeStruct((m, n), out_dtype), compiler_params=_cp("parallel", "parallel"))(*args)


def mm_at(at, b, name, tk=2048, transposed=False, tn=None, row_block=None, chunks=None):
    m, kk = at.shape
    _, n = b.shape
    tm, tn, tk = _pick(m, 1408), tn or _pick(n, 2816), min(tk, kk)
    nk = kk // tk
    assert chunks is None or (transposed and tn == n)

    def body(a_ref, b_ref, o_ref, acc_ref):
        k = pl.program_id(2)
        p = _mx(a_ref[...], b_ref[...])
        acc = jnp.where(k == 0, p, acc_ref[...] + p)
        acc_ref[...] = acc

        @pl.when(k == nk - 1)
        def _():
            res = (acc.T if transposed else acc).astype(o_ref.dtype)
            if chunks is None:
                o_ref[...] = res
            else:
                for cb, co, size in chunks:
                    o_ref[co:co + size, :] = res[cb:cb + size]

    if transposed:
        rb = row_block or (lambda j: j)
        out_spec = pl.BlockSpec((tn, tm), lambda i, j, k: (rb(j), i))
        out_shape = jax.ShapeDtypeStruct((n, m), MXU_DTYPE)
    else:
        out_spec = pl.BlockSpec((tm, tn), lambda i, j, k: (i, j))
        out_shape = jax.ShapeDtypeStruct((m, n), MXU_DTYPE)
    return pl.pallas_call(
        body, name=name, grid=(m // tm, n // tn, nk),
        in_specs=[pl.BlockSpec((tm, tk), lambda i, j, k: (i, k)), pl.BlockSpec((tk, tn), lambda i, j, k: (k, j))],
        out_specs=out_spec, out_shape=out_shape, scratch_shapes=[pltpu.VMEM((tm, tn), F32)],
        compiler_params=_cp("parallel", "parallel", "arbitrary"))(at, b)


def rms_fwd(x, g, name, tm=512, after=None):
    t, d = x.shape

    def body(x_ref, g_ref, *rest):
        o_ref, ot_ref = rest[-2:]
        xv = x_ref[...]
        r = lax.rsqrt(jnp.mean(xv * xv, axis=-1, keepdims=True) + EPS)
        h = xv * r * g_ref[...]
        o_ref[...] = h.astype(o_ref.dtype)
        ot_ref[...] = h.T.astype(ot_ref.dtype)

    in_specs = [pl.BlockSpec((tm, d), lambda i: (i, 0)), pl.BlockSpec((1, d), lambda i: (0, 0))]
    args = [x, g]
    if after is not None:
        in_specs.append(pl.BlockSpec(memory_space=pl.ANY))
        args.append(after)
    return pl.pallas_call(
        body, name=name, grid=(t // tm,), in_specs=in_specs,
        out_specs=(pl.BlockSpec((tm, d), lambda i: (i, 0)), pl.BlockSpec((d, tm), lambda i: (0, i))),
        out_shape=(jax.ShapeDtypeStruct((t, d), MXU_DTYPE), jax.ShapeDtypeStruct((d, t), MXU_DTYPE)),
        compiler_params=_cp("parallel"))(*args)


def mm_rms_bwd(a, bt, x, g, dres, name, tm=512, after=None, chunks=None):
    t, k = a.shape
    d = bt.shape[1]
    tm = min(tm if k > 3000 else 2 * tm, t)
    chunks = chunks or ((0, 0, k),)

    def body(a_ref, b_ref, x_ref, g_ref, dres_ref, *rest):
        dx_ref, dg_ref = rest[-2:]
        dhv = None
        for ca, cb, size in chunks:
            part = _mx(a_ref[:, ca:ca + size], b_ref[cb:cb + size, :])
            dhv = part if dhv is None else dhv + part
        xv = x_ref[...]
        r = lax.rsqrt(jnp.mean(xv * xv, axis=-1, keepdims=True) + EPS)
        xh = xv * r
        dxh = dhv * g_ref[...]
        dx_ref[...] = dres_ref[...] + r * (dxh - xh * jnp.mean(dxh * xh, axis=-1, keepdims=True))
        part = jnp.sum(dhv * xh, axis=0, keepdims=True)
        dg_ref[...] = jnp.where(pl.program_id(0) == 0, part, dg_ref[...] + part)

    row = pl.BlockSpec((tm, d), lambda i: (i, 0))
    one = pl.BlockSpec((1, d), lambda i: (0, 0))
    in_specs = [pl.BlockSpec((tm, k), lambda i: (i, 0)), pl.BlockSpec((k, d), lambda i: (0, 0)), row, one, row]
    args = [a, bt, x, g, dres]
    if after is not None:
        in_specs.append(pl.BlockSpec(memory_space=pl.ANY))
        args.append(after)
    return pl.pallas_call(
        body, name=name, grid=(t // tm,), in_specs=in_specs, out_specs=(row, one),
        out_shape=(jax.ShapeDtypeStruct((t, d), F32), jax.ShapeDtypeStruct((1, d), F32)),
        compiler_params=_cp("arbitrary"))(*args)


GU_TILE = 1408


def ffn_up(f, wt, name, tm=1024):
    t, d = f.shape
    tm = min(tm, t)
    nj = D_FF // GU_TILE

    def body(f_ref, wg_ref, wu_ref, gu_ref, a_ref, at_ref):
        g = _mx_nt(f_ref[...], wg_ref[...])
        u = _mx_nt(f_ref[...], wu_ref[...])
        sg = _sigmoid(g)
        gs = g * sg
        gu_ref[:, :GU_TILE] = (u * (sg + gs - gs * sg)).astype(gu_ref.dtype)
        gu_ref[:, GU_TILE:] = gs.astype(gu_ref.dtype)
        act = gs * u
        a_ref[...] = act.astype(a_ref.dtype)
        at_ref[...] = act.T.astype(at_ref.dtype)

    return pl.pallas_call(
        body, name=name, grid=(nj, t // tm),
        in_specs=[pl.BlockSpec((tm, d), lambda j, i: (i, 0)), pl.BlockSpec((GU_TILE, d), lambda j, i: (j, 0)),
                  pl.BlockSpec((GU_TILE, d), lambda j, i: (nj + j, 0))],
        out_specs=(pl.BlockSpec((tm, 2 * GU_TILE), lambda j, i: (i, j)), pl.BlockSpec((tm, GU_TILE), lambda j, i: (i, j)),
                   pl.BlockSpec((GU_TILE, tm), lambda j, i: (j, i))),
        out_shape=(jax.ShapeDtypeStruct((t, 2 * D_FF), MXU_DTYPE), jax.ShapeDtypeStruct((t, D_FF), MXU_DTYPE),
                   jax.ShapeDtypeStruct((D_FF, t), MXU_DTYPE)),
        compiler_params=_cp("parallel", "parallel"))(f, wt, wt)


_GU_CHUNKS = tuple((q * GU_TILE, ((q % 2) * (D_FF // GU_TILE) + q // 2) * GU_TILE, GU_TILE)
                   for q in range(2 * D_FF // GU_TILE))


def ffn_dact(dy, w_d, gu, name, tm=1024, after=None):
    t, d = dy.shape
    tm = min(tm, t)

    def body(dy_ref, w_ref, gu_ref, *rest):
        o_ref = rest[-1]
        da = _mx_nt(dy_ref[...], w_ref[...])
        o_ref[:, :GU_TILE] = (da * gu_ref[:, :GU_TILE]).astype(o_ref.dtype)
        o_ref[:, GU_TILE:] = (da * gu_ref[:, GU_TILE:]).astype(o_ref.dtype)

    in_specs = [pl.BlockSpec((tm, d), lambda j, i: (i, 0)), pl.BlockSpec((GU_TILE, d), lambda j, i: (j, 0)),
                pl.BlockSpec((tm, 2 * GU_TILE), lambda j, i: (i, j))]
    args = [dy, w_d, gu]
    if after is not None:
        in_specs.append(pl.BlockSpec(memory_space=pl.ANY))
        args.append(after)
    return pl.pallas_call(
        body, name=name, grid=(D_FF // GU_TILE, t // tm), in_specs=in_specs,
        out_specs=pl.BlockSpec((tm, 2 * GU_TILE), lambda j, i: (i, j)),
        out_shape=jax.ShapeDtypeStruct((t, 2 * D_FF), MXU_DTYPE), compiler_params=_cp("parallel", "parallel"))(*args)


def mm_nn_res_loss(a, b, res, target, name, tm=512):
    t, k = a.shape
    d = b.shape[1]
    tm = min(tm, t)

    def body(a_ref, b_ref, res_ref, t_ref, dy_ref, l_ref):
        e = res_ref[...] + _mx(a_ref[...], b_ref[...]) - t_ref[...]
        dy_ref[...] = e * (1.0 / d)
        part = jnp.zeros((1, 128), F32) + 0.5 * jnp.sum(jnp.mean(e * e, axis=-1, keepdims=True), axis=0, keepdims=True)
        l_ref[...] = jnp.where(pl.program_id(0) == 0, part, l_ref[...] + part)

    row = pl.BlockSpec((tm, d), lambda i: (i, 0))
    return pl.pallas_call(
        body, name=name, grid=(t // tm,),
        in_specs=[pl.BlockSpec((tm, k), lambda i: (i, 0)), pl.BlockSpec((k, d), lambda i: (0, 0)), row, row],
        out_specs=(row, pl.BlockSpec((1, 128), lambda i: (0, 0))),
        out_shape=(jax.ShapeDtypeStruct((t, d), F32), jax.ShapeDtypeStruct((1, 128), F32)),
        compiler_params=_cp("arbitrary"))(a, b, res, target)


QK_W = Q_W + KV_W
_QK_TILE = 256


def _qk_mats():
    idx = np.arange(_QK_TILE)
    half = HEAD_DIM // 2
    same = (idx[:, None] // HEAD_DIM) == (idx[None, :] // HEAD_DIM)
    lo = (idx % HEAD_DIM) < half
    rot = np.where((idx[:, None] == idx[None, :] + half) & lo[None, :], -1.0, 0.0)
    rot = rot + np.where((idx[:, None] == idx[None, :] - half) & ~lo[None, :], 1.0, 0.0)
    return jnp.asarray(same, F32), jnp.asarray(rot, F32)


QK_TM = 256


def _qk_gains(q_gain, k_gain):
    return jnp.concatenate([q_gain] * ATTN_HEADS + [k_gain] * KV_HEADS, axis=-1)


def _rope_tile(ca_ref, sa_ref, cb_ref, sb_ref):
    ca, sa, cb, sb = ca_ref[0], sa_ref[0], cb_ref[...], sb_ref[...]
    c, s = ca * cb - sa * sb, sa * cb + ca * sb
    rep = QK_W // 128
    return jnp.concatenate([c] * rep, axis=-1), jnp.concatenate([s] * rep, axis=-1)


_ROPE_SPECS = [pl.BlockSpec((1, 1, 128), lambda i: (i, 0, 0)), pl.BlockSpec((1, 1, 128), lambda i: (i, 0, 0)),
               pl.BlockSpec((QK_TM, 128), lambda i: (0, 0)), pl.BlockSpec((QK_TM, 128), lambda i: (0, 0))]


def _qk_tiles(a, mat, transposed=False):
    outs = []
    for c0 in range(0, QK_W, _QK_TILE):
        w = min(_QK_TILE, QK_W - c0)
        mt = (mat.T if transposed else mat)[:w, :w].astype(MXU_DTYPE)
        at = a[:, c0:c0 + w]
        hi = at.astype(MXU_DTYPE)
        lo = (at - hi.astype(F32)).astype(MXU_DTYPE)
        outs.append(jnp.dot(hi, mt, preferred_element_type=F32) + jnp.dot(lo, mt, preferred_element_type=F32))
    return jnp.concatenate(outs, axis=-1)


def qk_prep_fwd(proj, q_gain, k_gain, rope, name):
    t = proj.shape[0]
    tm = QK_TM
    gmat, rmat = _qk_mats()
    gain = _qk_gains(q_gain, k_gain)

    def body(p_ref, g_ref, ca_ref, sa_ref, cb_ref, sb_ref, gm_ref, rm_ref, q_ref, k_ref):
        x = p_ref[...]
        r = lax.rsqrt(_qk_tiles(x * x, gm_ref[...]) * (1.0 / HEAD_DIM) + EPS)
        xn = x * r * g_ref[...]
        c, s = _rope_tile(ca_ref, sa_ref, cb_ref, sb_ref)
        out = xn * c + _qk_tiles(xn, rm_ref[...]) * s
        q_ref[...] = out[:, :Q_W]
        k_ref[...] = out[:, Q_W:]

    full = pl.BlockSpec((_QK_TILE, _QK_TILE), lambda i: (0, 0))
    return pl.pallas_call(
        body, name=name, grid=(t // tm,),
        in_specs=[pl.BlockSpec((tm, QK_W), lambda i: (i, 0)), pl.BlockSpec((1, QK_W), lambda i: (0, 0))] + _ROPE_SPECS
        + [full, full],
        out_specs=(pl.BlockSpec((tm, Q_W), lambda i: (i, 0)), pl.BlockSpec((tm, KV_W), lambda i: (i, 0))),
        out_shape=(jax.ShapeDtypeStruct((t, Q_W), F32), jax.ShapeDtypeStruct((t, KV_W), F32)),
        compiler_params=_cp("parallel"))(proj, gain, *rope, gmat, rmat)


def qk_prep_bwd(proj, q_gain, k_gain, rope, dq, dk, into, name):
    t = proj.shape[0]
    tm = QK_TM
    gmat, rmat = _qk_mats()
    gain = _qk_gains(q_gain, k_gain)
    lanes = np.arange(QK_W)[:, None]
    fold = jnp.asarray(lanes % HEAD_DIM + np.where(lanes >= Q_W, HEAD_DIM, 0) == np.arange(128)[None, :], F32)

    def body(p_ref, g_ref, ca_ref, sa_ref, cb_ref, sb_ref, gm_ref, rm_ref, f_ref, dq_ref, dk_ref, into_ref,
             o_ref, dg_ref):
        x = p_ref[...]
        r = lax.rsqrt(_qk_tiles(x * x, gm_ref[...]) * (1.0 / HEAD_DIM) + EPS)
        xh = x * r
        c, s = _rope_tile(ca_ref, sa_ref, cb_ref, sb_ref)
        dout = jnp.concatenate([dq_ref[...], dk_ref[...]], axis=-1)
        dxn = dout * c + _qk_tiles(dout * s, rm_ref[...], transposed=True)
        part = _hi(jnp.sum(dxn * xh, axis=0, keepdims=True), f_ref[...])
        dxh = dxn * g_ref[...]
        mean = _qk_tiles(dxh * xh, gm_ref[...]) * (1.0 / HEAD_DIM)
        o_ref[...] = (r * (dxh - xh * mean)).astype(o_ref.dtype)
        dg_ref[...] = jnp.where(pl.program_id(0) == 0, part, dg_ref[...] + part)

    full = pl.BlockSpec((_QK_TILE, _QK_TILE), lambda i: (0, 0))
    dqk, dg = pl.pallas_call(
        body, name=name, grid=(t // tm,),
        in_specs=[pl.BlockSpec((tm, QK_W), lambda i: (i, 0)), pl.BlockSpec((1, QK_W), lambda i: (0, 0))] + _ROPE_SPECS
        + [full, full, pl.BlockSpec((QK_W, 128), lambda i: (0, 0)),
                  pl.BlockSpec((tm, Q_W), lambda i: (i, 0)), pl.BlockSpec((tm, KV_W), lambda i: (i, 0)),
                  pl.BlockSpec(memory_space=pl.ANY)],
        out_specs=(pl.BlockSpec((tm, QK_W), lambda i: (i, 0)), pl.BlockSpec((1, 128), lambda i: (0, 0))),
        out_shape=(jax.ShapeDtypeStruct(into.shape, into.dtype), jax.ShapeDtypeStruct((1, 128), F32)),
        input_output_aliases={len(rope) + 7: 0},
        compiler_params=_cp("arbitrary"))(proj, gain, *rope, gmat, rmat, fold, dq, dk, into)
    return dqk, dg[:, :HEAD_DIM], dg[:, HEAD_DIM:]


def _swa_valid(n, grp):
    qi = lax.broadcasted_iota(jnp.int32, (grp * ATTN_BLOCK, 2 * ATTN_BLOCK), 0) & (ATTN_BLOCK - 1)
    kj = lax.broadcasted_iota(jnp.int32, (grp * ATTN_BLOCK, 2 * ATTN_BLOCK), 1)
    diff = qi + ATTN_BLOCK - kj
    return (diff >= 0) & (diff < ATTN_BLOCK) & (n * ATTN_BLOCK - ATTN_BLOCK + kj >= 0)


def _stack_heads(ref, g, grp, rows=slice(None)):
    return jnp.concatenate([ref[rows, (g * grp + j) * HEAD_DIM:(g * grp + j + 1) * HEAD_DIM] for j in range(grp)], axis=0)


def _stack_sinks(s_ref, g, grp):
    return jnp.concatenate([jnp.zeros((ATTN_BLOCK, 1), F32) + s_ref[0:1, g * grp + j:g * grp + j + 1]
                            for j in range(grp)], axis=0)


SWA_STEP = 2


def swa_fwd(q, k, proj, sinks, name):
    t = q.shape[0]
    nb = t // ATTN_BLOCK
    scale = HEAD_DIM ** -0.5
    grp = ATTN_HEADS // KV_HEADS

    rows = SWA_STEP * ATTN_BLOCK

    def body(q_ref, kc_ref, kp_ref, vc_ref, vp_ref, s_ref, y_ref, mix_ref, yt_ref, lse_ref):
        n0 = pl.program_id(0) * SWA_STEP
        kk = jnp.concatenate([kp_ref[...], kc_ref[...]], axis=0).astype(MXU_DTYPE)
        vv = jnp.concatenate([vp_ref[...], vc_ref[...]], axis=0).astype(MXU_DTYPE)
        lane = lax.broadcasted_iota(jnp.int32, (ATTN_BLOCK, ATTN_HEADS), 1)
        units = [(b, g) for b in range(SWA_STEP) for g in range(KV_HEADS)]
        blk = lambda b: slice(b * ATTN_BLOCK, (b + 1) * ATTN_BLOCK)
        keys = lambda b: slice(b * ATTN_BLOCK, (b + 2) * ATTN_BLOCK)
        col = lambda g: slice(g * HEAD_DIM, (g + 1) * HEAD_DIM)
        valid = [_swa_valid(n0 + b, grp) for b in range(SWA_STEP)]
        qg = [_stack_heads(q_ref, g, grp, blk(b)) for b, g in units]
        sink = [_stack_sinks(s_ref, g, grp) for b, g in units]
        sc = [jnp.where(valid[b], _mx_nt(qg[u], kk[keys(b), col(g)]) * scale, NEG) for u, (b, g) in enumerate(units)]
        m = [jnp.maximum(jnp.max(sc_, axis=-1, keepdims=True), sk) for sc_, sk in zip(sc, sink)]
        e = [jnp.exp(sc_ - m_) for sc_, m_ in zip(sc, m)]
        den = [jnp.sum(e_, axis=-1, keepdims=True) + jnp.exp(sk - m_) for e_, sk, m_ in zip(e, sink, m)]
        og = [_mx(e[u] / den[u], vv[keys(b), col(g)]) for u, (b, g) in enumerate(units)]
        lg = [m_ + jnp.log(d_) for m_, d_ in zip(m, den)]
        for b in range(SWA_STEP):
            lse = jnp.zeros((ATTN_BLOCK, ATTN_HEADS), F32)
            outs = []
            for h in range(ATTN_HEADS):
                u = b * KV_HEADS + h // grp
                sub = blk(h % grp)
                outs.append(og[u][sub])
                lse = jnp.where(lane == h, lg[u][sub], lse)
            y = jnp.concatenate(outs, axis=-1)
            y_ref[blk(b), :] = y
            mix_ref[blk(b), :] = y.astype(mix_ref.dtype)
            yt_ref[:, blk(b)] = y.T.astype(yt_ref.dtype)
            lse_ref[blk(b), :] = lse

    cur = lambda n: (n, 0)
    prev = lambda n: (jnp.maximum(n * SWA_STEP - 1, 0), 0)
    vcol = (Q_W + KV_W) // KV_W
    return pl.pallas_call(
        body, name=name, grid=(nb // SWA_STEP,),
        in_specs=[pl.BlockSpec((rows, Q_W), cur), pl.BlockSpec((rows, KV_W), cur),
                  pl.BlockSpec((ATTN_BLOCK, KV_W), prev),
                  pl.BlockSpec((rows, KV_W), lambda n: (n, vcol)),
                  pl.BlockSpec((ATTN_BLOCK, KV_W), lambda n: (jnp.maximum(n * SWA_STEP - 1, 0), vcol)),
                  pl.BlockSpec((1, ATTN_HEADS), lambda n: (0, 0))],
        out_specs=(pl.BlockSpec((rows, Q_W), cur), pl.BlockSpec((rows, Q_W), cur),
                   pl.BlockSpec((Q_W, rows), lambda n: (0, n)), pl.BlockSpec((rows, ATTN_HEADS), cur)),
        out_shape=(jax.ShapeDtypeStruct((t, Q_W), F32), jax.ShapeDtypeStruct((t, Q_W + CONV_CH), MXU_DTYPE),
                   jax.ShapeDtypeStruct((Q_W + CONV_CH, t), MXU_DTYPE), jax.ShapeDtypeStruct((t, ATTN_HEADS), F32)),
        compiler_params=_cp("parallel"))(q, k, k, proj, proj, sinks)


def swa_bwd(q, k, proj, sinks, y, lse, dmix, into, name):
    t = q.shape[0]
    nb = t // ATTN_BLOCK
    scale = HEAD_DIM ** -0.5
    grp = ATTN_HEADS // KV_HEADS

    def body(q_ref, kc_ref, kp_ref, vc_ref, vp_ref, s_ref, y_ref, lse_ref, dy_ref, into_ref,
             dq_ref, dk_ref, dv_ref, ds_ref, dkc, dvc):
        n = pl.program_id(0)

        @pl.when(n == 0)
        def _():
            dkc[...] = jnp.zeros_like(dkc)
            dvc[...] = jnp.zeros_like(dvc)
            ds_ref[...] = jnp.zeros_like(ds_ref)

        @pl.when(n < nb)
        def _():
            valid = _swa_valid(n, grp)
            kk = jnp.concatenate([kp_ref[...], kc_ref[...]], axis=0).astype(MXU_DTYPE)
            vv = jnp.concatenate([vp_ref[...], vc_ref[...]], axis=0).astype(MXU_DTYPE)
            lane = lax.broadcasted_iota(jnp.int32, (1, ATTN_HEADS), 1)
            gs = range(KV_HEADS)
            kg = [kk[:, g * HEAD_DIM:(g + 1) * HEAD_DIM] for g in gs]
            vg = [vv[:, g * HEAD_DIM:(g + 1) * HEAD_DIM] for g in gs]
            qg = [_stack_heads(q_ref, g, grp).astype(MXU_DTYPE) for g in gs]
            dog = [_stack_heads(dy_ref, g, grp) for g in gs]
            og = [_stack_heads(y_ref, g, grp) for g in gs]
            lg = [jnp.concatenate([lse_ref[:, g * grp + j:g * grp + j + 1] for j in range(grp)], axis=0) for g in gs]
            sink = [_stack_sinks(s_ref, g, grp) for g in gs]
            sc = [jnp.where(valid, _mx_nt(qg[g], kg[g]) * scale, NEG) for g in gs]
            p = [jnp.exp(sc[g] - lg[g]) for g in gs]
            delta = [jnp.sum(dog[g] * og[g], axis=-1, keepdims=True) for g in gs]
            ds = [p[g] * (_mx_nt(dog[g], vg[g]) - delta[g]) for g in gs]
            dqg = [_mx(ds[g], kg[g]) * scale for g in gs]
            dkf = jnp.concatenate([_mx_tn(ds[g], qg[g]) * scale for g in gs], axis=-1)
            dvf = jnp.concatenate([_mx_tn(p[g], dog[g]) for g in gs], axis=-1)
            dsk = [jnp.exp(sink[g] - lg[g]) * delta[g] for g in gs]
            dsink = jnp.zeros((1, ATTN_HEADS), F32)
            dqs = []
            for h in range(ATTN_HEADS):
                rows = slice((h % grp) * ATTN_BLOCK, (h % grp + 1) * ATTN_BLOCK)
                dqs.append(dqg[h // grp][rows])
                dsink = jnp.where(lane == h, -jnp.sum(dsk[h // grp][rows], axis=0, keepdims=True), dsink)
            dq_ref[...] = jnp.concatenate(dqs, axis=-1)
            dk_ref[...] = dkc[...] + dkf[:ATTN_BLOCK]
            dv_ref[...] = (dvc[...] + dvf[:ATTN_BLOCK]).astype(dv_ref.dtype)
            dkc[...] = dkf[ATTN_BLOCK:]
            dvc[...] = dvf[ATTN_BLOCK:]
            ds_ref[...] += dsink

        @pl.when(n == nb)
        def _():
            dk_ref[...] = dkc[...]
            dv_ref[...] = dvc[...].astype(dv_ref.dtype)

    cur = lambda n: (jnp.minimum(n, nb - 1), 0)
    prev = lambda n: (jnp.clip(n - 1, 0, nb - 1), 0)
    vcol = (Q_W + KV_W) // KV_W
    return pl.pallas_call(
        body, name=name, grid=(nb + 1,),
        in_specs=[pl.BlockSpec((ATTN_BLOCK, Q_W), cur), pl.BlockSpec((ATTN_BLOCK, KV_W), cur),
                  pl.BlockSpec((ATTN_BLOCK, KV_W), prev),
                  pl.BlockSpec((ATTN_BLOCK, KV_W), lambda n: (jnp.minimum(n, nb - 1), vcol)),
                  pl.BlockSpec((ATTN_BLOCK, KV_W), lambda n: (jnp.clip(n - 1, 0, nb - 1), vcol)),
                  pl.BlockSpec((1, ATTN_HEADS), lambda n: (0, 0)),
                  pl.BlockSpec((ATTN_BLOCK, Q_W), cur), pl.BlockSpec((ATTN_BLOCK, ATTN_HEADS), cur),
                  pl.BlockSpec((ATTN_BLOCK, Q_W), cur), pl.BlockSpec(memory_space=pl.ANY)],
        out_specs=(pl.BlockSpec((ATTN_BLOCK, Q_W), cur), pl.BlockSpec((ATTN_BLOCK, KV_W), prev),
                   pl.BlockSpec((ATTN_BLOCK, KV_W), lambda n: (jnp.clip(n - 1, 0, nb - 1), vcol)),
                   pl.BlockSpec((1, ATTN_HEADS), lambda n: (0, 0))),
        out_shape=(jax.ShapeDtypeStruct((t, Q_W), F32), jax.ShapeDtypeStruct((t, KV_W), F32),
                   jax.ShapeDtypeStruct(into.shape, into.dtype), jax.ShapeDtypeStruct((1, ATTN_HEADS), F32)),
        scratch_shapes=[pltpu.VMEM((ATTN_BLOCK, KV_W), F32), pltpu.VMEM((ATTN_BLOCK, KV_W), F32)],
        input_output_aliases={9: 2},
        compiler_params=_cp("arbitrary"))(q, k, k, proj, proj, sinks, y, lse, dmix, into)


GC_W = 256
_GB0, _GC0, _XI0 = 768 // GC_W, 1280 // GC_W, 1792 // GC_W
HALO = 8


def gconv_fwd(proj, conv_w, mix, mix_t, name, tm=512):
    t = proj.shape[0]
    hb = tm // HALO
    half = Q_W // GC_W

    def body(gb_ref, gc_ref, xi_ref, gch_ref, xih_ref, w_ref, mix_in, mixt_in, y_ref, yt_ref):
        i = pl.program_id(1)
        u = gc_ref[...] * xi_ref[...]
        uh = jnp.where(i == 0, 0.0, gch_ref[...] * xih_ref[...])
        up = jnp.concatenate([uh, u], axis=0)
        cv = w_ref[0:1, :] * up[HALO - 2:HALO - 2 + tm]
        cv = cv + w_ref[1:2, :] * up[HALO - 1:HALO - 1 + tm]
        cv = cv + w_ref[2:3, :] * u
        y = gb_ref[...] * cv
        y_ref[...] = y.astype(y_ref.dtype)
        yt_ref[...] = y.T.astype(yt_ref.dtype)

    def col(c0):
        return pl.BlockSpec((tm, GC_W), lambda cj, i: (i, c0 + cj))

    def halo(c0):
        return pl.BlockSpec((HALO, GC_W), lambda cj, i: (jnp.maximum(i * hb - 1, 0), c0 + cj))

    return pl.pallas_call(
        body, name=name, grid=(CONV_CH // GC_W, t // tm),
        in_specs=[col(_GB0), col(_GC0), col(_XI0), halo(_GC0), halo(_XI0),
                  pl.BlockSpec((3, GC_W), lambda cj, i: (0, cj)),
                  pl.BlockSpec(memory_space=pl.ANY), pl.BlockSpec(memory_space=pl.ANY)],
        out_specs=(pl.BlockSpec((tm, GC_W), lambda cj, i: (i, half + cj)),
                   pl.BlockSpec((GC_W, tm), lambda cj, i: (half + cj, i))),
        out_shape=(jax.ShapeDtypeStruct(mix.shape, mix.dtype), jax.ShapeDtypeStruct(mix_t.shape, mix_t.dtype)),
        input_output_aliases={6: 0, 7: 1},
        compiler_params=_cp("parallel", "parallel"))(proj, proj, proj, proj, proj, conv_w, mix, mix_t)


def gconv_bwd(proj, conv_w, dmix, into, name, tm=512):
    t = proj.shape[0]
    hb = tm // HALO
    nt = t // tm
    dy0 = Q_W // GC_W

    def body(gb_ref, gc_ref, xi_ref, gch_ref, xih_ref, gbn_ref, dyn_ref, dy_ref, w_ref, into_ref, o_ref, dw_ref):
        dgb_ref, dgc_ref, dxi_ref = (o_ref.at[:, j * GC_W:(j + 1) * GC_W] for j in range(3))
        i = pl.program_id(1)
        gc, xi, gb, dy = gc_ref[...], xi_ref[...], gb_ref[...], dy_ref[...]
        u = gc * xi
        uh = jnp.where(i == 0, 0.0, gch_ref[...] * xih_ref[...])
        up = jnp.concatenate([uh, u], axis=0)
        u2 = up[HALO - 2:HALO - 2 + tm]
        u1 = up[HALO - 1:HALO - 1 + tm]
        cv = w_ref[0:1, :] * u2 + w_ref[1:2, :] * u1 + w_ref[2:3, :] * u
        dgb_ref[...] = (dy * cv).astype(dgb_ref.dtype)
        dcv = dy * gb
        dcvn = jnp.where(i == nt - 1, 0.0, dyn_ref[...] * gbn_ref[...])
        dcvp = jnp.concatenate([dcv, dcvn], axis=0)
        du = w_ref[0:1, :] * dcvp[2:2 + tm] + w_ref[1:2, :] * dcvp[1:1 + tm] + w_ref[2:3, :] * dcv
        dgc_ref[...] = (du * xi).astype(dgc_ref.dtype)
        dxi_ref[...] = (du * gc).astype(dxi_ref.dtype)
        dw = jnp.concatenate([jnp.sum(dcv * u2, axis=0, keepdims=True), jnp.sum(dcv * u1, axis=0, keepdims=True),
                              jnp.sum(dcv * u, axis=0, keepdims=True)], axis=0)

        @pl.when(i == 0)
        def _():
            dw_ref[...] = dw

        @pl.when(i > 0)
        def _():
            dw_ref[...] += dw

    def col(c0):
        return pl.BlockSpec((tm, GC_W), lambda cj, i: (i, c0 + cj))

    def halo(c0):
        return pl.BlockSpec((HALO, GC_W), lambda cj, i: (jnp.maximum(i * hb - 1, 0), c0 + cj))

    def nxt(c0):
        return pl.BlockSpec((HALO, GC_W), lambda cj, i: (jnp.minimum((i + 1) * hb, t // HALO - 1), c0 + cj))

    return pl.pallas_call(
        body, name=name, grid=(CONV_CH // GC_W, nt),
        in_specs=[col(_GB0), col(_GC0), col(_XI0), halo(_GC0), halo(_XI0), nxt(_GB0), nxt(dy0), col(dy0),
                  pl.BlockSpec((3, GC_W), lambda cj, i: (0, cj)), pl.BlockSpec(memory_space=pl.ANY)],
        out_specs=(pl.BlockSpec((tm, 3 * GC_W), lambda cj, i: (i, 1 + cj)),
                   pl.BlockSpec((3, GC_W), lambda cj, i: (0, cj))),
        out_shape=(jax.ShapeDtypeStruct(into.shape, into.dtype), jax.ShapeDtypeStruct((3, CONV_CH), F32)),
        input_output_aliases={9: 0},
        compiler_params=_cp("parallel", "arbitrary"))(proj, proj, proj, proj, proj, proj, dmix, dmix, conv_w, into)


_EVEN_D_CHUNKS = ((0, 0, 768),) + tuple(
    (768 + (3 * j + part) * GC_W, 768 + part * CONV_CH + j * GC_W, GC_W)
    for j in range(CONV_CH // GC_W) for part in range(3))


_QKV_W = 3 * DN_W
_BA_COL = (4 * DN_W) // 128
_Z_COL = _QKV_W // DN_W


def gdn_prep_fwd(proj, conv_w, alog_row, dtb_row, name, tm=256):
    t = proj.shape[0]
    hb = tm // HALO
    qscale = DN_DIM ** -0.5

    def body(x_ref, xh_ref, w_ref, ba_ref, al_ref, dt_ref, q_ref, k_ref, v_ref, bg_ref, c_ref):
        i = pl.program_id(0)
        for gi in range(3 * DN_HEADS):
            sl = slice(gi * DN_DIM, (gi + 1) * DN_DIM)
            xp = jnp.concatenate([jnp.where(i == 0, 0.0, xh_ref[:, sl]), x_ref[:, sl]], axis=0)
            c = w_ref[0:1, sl] * xp[HALO - 3:HALO - 3 + tm]
            for j in range(1, 4):
                c = c + w_ref[j:j + 1, sl] * xp[HALO - 3 + j:HALO - 3 + j + tm]
            c_ref[:, sl] = c
            s = c * _sigmoid(c)
            osl = slice((gi % DN_HEADS) * DN_DIM, (gi % DN_HEADS + 1) * DN_DIM)
            if gi < DN_HEADS:
                q_ref[:, osl] = s * lax.rsqrt(jnp.sum(s * s, axis=-1, keepdims=True) + EPS) * qscale
            elif gi < 2 * DN_HEADS:
                k_ref[:, osl] = s * lax.rsqrt(jnp.sum(s * s, axis=-1, keepdims=True) + EPS)
            else:
                v_ref[:, osl] = s
        ba = ba_ref[...]
        lane = lax.broadcasted_iota(jnp.int32, ba.shape, 1)
        gval = -jnp.exp(al_ref[...]) * _softplus(ba + dt_ref[...])
        bg_ref[...] = jnp.where(lane < DN_HEADS, _sigmoid(ba), jnp.where(lane < 2 * DN_HEADS, gval, 0.0))

    row = pl.BlockSpec((tm, DN_W), lambda i: (i, 0))
    one = pl.BlockSpec((1, 128), lambda i: (0, 0))
    return pl.pallas_call(
        body, name=name, grid=(t // tm,),
        in_specs=[pl.BlockSpec((tm, _QKV_W), lambda i: (i, 0)),
                  pl.BlockSpec((HALO, _QKV_W), lambda i: (jnp.maximum(i * hb - 1, 0), 0)),
                  pl.BlockSpec((4, _QKV_W), lambda i: (0, 0)),
                  pl.BlockSpec((tm, 128), lambda i: (i, _BA_COL)), one, one],
        out_specs=(row, row, row, pl.BlockSpec((tm, 128), lambda i: (i, 0)), pl.BlockSpec((tm, _QKV_W), lambda i: (i, 0))),
        out_shape=(jax.ShapeDtypeStruct((t, DN_W), F32),) * 3 + (jax.ShapeDtypeStruct((t, 128), F32),
                                                                 jax.ShapeDtypeStruct((t, _QKV_W), F32)),
        compiler_params=_cp("parallel"))(proj, proj, conv_w, proj, alog_row, dtb_row)


def gdn_prep_bwd(proj, conv, conv_w, alog_row, dtb_row, dq, dk, dv, dbg, dz, name, tm=256):
    t = proj.shape[0]
    hb = tm // HALO
    nt = t // tm
    qscale = DN_DIM ** -0.5
    te = tm + HALO

    def body(x_ref, c_ref, cn_ref, w_ref, ba_ref, al_ref, dt_ref, dq_ref, dk_ref, dv_ref,
             dqn_ref, dkn_ref, dvn_ref, dbg_ref, dz_ref, dx_ref, dw_ref, ddt_ref, dal_ref):
        i = pl.program_id(0)
        first = i == 0
        last = i == nt - 1
        dws = []
        for gi in range(3 * DN_HEADS):
            sl = slice(gi * DN_DIM, (gi + 1) * DN_DIM)
            osl = slice((gi % DN_HEADS) * DN_DIM, (gi % DN_HEADS + 1) * DN_DIM)
            c = jnp.concatenate([c_ref[:, sl], cn_ref[:, sl]], axis=0)
            sg = _sigmoid(c)
            s = c * sg
            d_ref, dn_ref = ((dq_ref, dqn_ref), (dk_ref, dkn_ref), (dv_ref, dvn_ref))[gi // DN_HEADS]
            dy = jnp.concatenate([d_ref[:, osl], jnp.where(last, 0.0, dn_ref[:, osl])], axis=0)
            if gi < 2 * DN_HEADS:
                r = lax.rsqrt(jnp.sum(s * s, axis=-1, keepdims=True) + EPS)
                sh = s * r
                ds = r * (dy - sh * jnp.sum(sh * dy, axis=-1, keepdims=True))
                if gi < DN_HEADS:
                    ds = ds * qscale
            else:
                ds = dy
            dc = ds * sg * (1.0 + c * (1.0 - sg))
            dcs = [dc[3 - j:3 - j + tm] for j in range(4)]
            dx = w_ref[0:1, sl] * dcs[0]
            for j in range(1, 4):
                dx = dx + w_ref[j:j + 1, sl] * dcs[j]
            dx_ref[:, sl] = dx.astype(dx_ref.dtype)
            x0 = x_ref[:, sl]
            dws.append(jnp.concatenate([jnp.sum(dcs[j] * x0, axis=0, keepdims=True) for j in range(4)], axis=0))
        dw = jnp.concatenate(dws, axis=-1)
        ba = ba_ref[...]
        dbgv = dbg_ref[...]
        lane = lax.broadcasted_iota(jnp.int32, ba.shape, 1)
        beta = _sigmoid(ba)
        ea = -jnp.exp(al_ref[...])
        zin = ba + dt_ref[...]
        is_b = lane < DN_HEADS
        is_a = (lane >= DN_HEADS) & (lane < 2 * DN_HEADS)
        da = jnp.where(is_a, dbgv * ea * _sigmoid(zin), 0.0)
        dx_ref[:, _QKV_W:_QKV_W + DN_W] = dz_ref[...]
        dx_ref[:, _QKV_W + DN_W:] = jnp.where(is_b, dbgv * beta * (1.0 - beta), da).astype(dx_ref.dtype)
        ddt = jnp.sum(da, axis=0, keepdims=True)
        dal = jnp.sum(jnp.where(is_a, dbgv * ea * _softplus(zin), 0.0), axis=0, keepdims=True)

        @pl.when(first)
        def _():
            dw_ref[...] = dw
            ddt_ref[...] = ddt
            dal_ref[...] = dal

        @pl.when(i > 0)
        def _():
            dw_ref[...] += dw
            ddt_ref[...] += ddt
            dal_ref[...] += dal

    row = pl.BlockSpec((tm, DN_W), lambda i: (i, 0))
    nrow = pl.BlockSpec((HALO, DN_W), lambda i: (jnp.minimum((i + 1) * hb, t // HALO - 1), 0))
    one = pl.BlockSpec((1, 128), lambda i: (0, 0))
    return pl.pallas_call(
        body, name=name, grid=(nt,),
        in_specs=[pl.BlockSpec((tm, _QKV_W), lambda i: (i, 0)), pl.BlockSpec((tm, _QKV_W), lambda i: (i, 0)),
                  pl.BlockSpec((HALO, _QKV_W), lambda i: (jnp.minimum((i + 1) * hb, t // HALO - 1), 0)),
                  pl.BlockSpec((4, _QKV_W), lambda i: (0, 0)),
                  pl.BlockSpec((tm, 128), lambda i: (i, _BA_COL)), one, one,
                  row, row, row, nrow, nrow, nrow, pl.BlockSpec((tm, 128), lambda i: (i, 0)), row],
        out_specs=(pl.BlockSpec((tm, ODD_IN_PAD), lambda i: (i, 0)), pl.BlockSpec((4, _QKV_W), lambda i: (0, 0)), one, one),
        out_shape=(jax.ShapeDtypeStruct((t, ODD_IN_PAD), MXU_DTYPE), jax.ShapeDtypeStruct((4, _QKV_W), F32),
                   jax.ShapeDtypeStruct((1, 128), F32), jax.ShapeDtypeStruct((1, 128), F32)),
        compiler_params=_cp("arbitrary"))(proj, conv, conv, conv_w, proj, alog_row, dtb_row, dq, dk, dv, dq, dk, dv, dbg,
                                          dz)


def _chunk_masks():
    r = lax.broadcasted_iota(jnp.int32, (DN_CHUNK, DN_CHUNK), 0)
    c = lax.broadcasted_iota(jnp.int32, (DN_CHUNK, DN_CHUNK), 1)
    return r >= c, r > c


INV_PACK = 2


def _inv_unit_lower_many(mats):
    n = DN_CHUNK
    wide = INV_PACK * n
    r = lax.broadcasted_iota(jnp.int32, (wide, wide), 0)
    c = lax.broadcasted_iota(jnp.int32, (wide, wide), 1)
    same = (r & -n) == (c & -n)
    eye = jnp.where((r[:n] == (c[:n] & (n - 1))), 1.0, 0.0)

    def blockdiag(row):
        return jnp.where(same, jnp.concatenate([row] * INV_PACK, axis=0), 0.0)

    packs = [jnp.concatenate(mats[g:g + INV_PACK], axis=-1) for g in range(0, len(mats), INV_PACK)]
    xs = [eye - a for a in packs]
    pws = [_hi(a, blockdiag(a)) for a in packs]
    for step in range(5):
        if step < 4:
            both = [_hi(jnp.concatenate([x, pw], axis=0), blockdiag(pw)) for x, pw in zip(xs, pws)]
            xs = [x + b[:n] for x, b in zip(xs, both)]
            pws = [b[n:] for b in both]
        else:
            xs = [x + _hi(x, blockdiag(pw)) for x, pw in zip(xs, pws)]
    return [x[:, j * n:(j + 1) * n] for x in xs for j in range(INV_PACK)]


def _chunk_common(q, k, beta, gc, gcr, lower):
    gam = jnp.exp(jnp.where(lower, gc - gcr, NEG))
    eg = jnp.exp(gc)
    gl = gc[DN_CHUNK - 1:DN_CHUNK, :]
    kdf = jnp.exp(gl - gc)
    kb = k * beta
    bmat = _mx_nt(kb, k)
    qmat = _mx_nt(q, k)
    return gam, eg, jnp.exp(gl), kdf, kb, bmat, qmat


DN_STEP = 4


def gdn_fwd(q, k, v, bg, proj, o_gain, name):
    t = q.shape[0]
    n_chunks = t // DN_CHUNK

    def body(q_ref, k_ref, v_ref, bg_ref, z_ref, g_ref, o_ref, sall_ref, tall_ref, y_ref, yt_ref, s_ref):
        n = pl.program_id(0)

        @pl.when(n == 0)
        def _():
            s_ref[...] = jnp.zeros_like(s_ref)

        lower, strict = _chunk_masks()
        ltri = jnp.where(lower, 1.0, 0.0)
        hs = range(DN_HEADS)
        sl = [slice(h * DN_DIM, (h + 1) * DN_DIM) for h in hs]
        units = [(c, h) for c in range(DN_STEP) for h in hs]
        nu = range(len(units))
        rs = [slice(c * DN_CHUNK, (c + 1) * DN_CHUNK) for c in range(DN_STEP)]
        bgv = [bg_ref[rs[c], :] for c in range(DN_STEP)]
        gcs = [_hi(ltri, b) for b in bgv]
        gcs_t = [g.T for g in gcs]
        qh = [q_ref[rs[c], sl[h]] for c, h in units]
        kh = [k_ref[rs[c], sl[h]] for c, h in units]
        vh = [v_ref[rs[c], sl[h]] for c, h in units]
        beta = [bgv[c][:, h:h + 1] for c, h in units]
        com = [_chunk_common(qh[u], kh[u], beta[u], gcs[c][:, DN_HEADS + h:DN_HEADS + h + 1],
                             gcs_t[c][DN_HEADS + h:DN_HEADS + h + 1, :], lower) for u, (c, h) in enumerate(units)]
        gam, eg, dec, kdf, kb, bmat, qmat = zip(*com)
        tms = _inv_unit_lower_many([jnp.where(strict, bmat[u] * gam[u], 0.0) for u in nu])
        for u, (c, h) in enumerate(units):
            tall_ref[c, h] = tms[u]
        uw = [_hi(tms[u], jnp.concatenate([vh[u] * beta[u], kb[u] * eg[u]], axis=-1)) for u in nu]
        qd = [qh[u] * eg[u] for u in nu]
        pm = [qmat[u] * gam[u] for u in nu]
        kd = [kh[u] * kdf[u] for u in nu]
        st = [s_ref[h] for h in hs]
        for c in range(DN_STEP):
            us = [c * DN_HEADS + h for h in hs]
            for h in hs:
                sall_ref[c, h] = st[h]
            v_new = [uw[us[h]][:, :DN_DIM] - _mx(uw[us[h]][:, DN_DIM:], st[h]) for h in hs]
            o_st = [_mx(qd[us[h]], st[h]) for h in hs]
            o_in = [_mx(pm[us[h]], v_new[h]) for h in hs]
            s_up = [_mx_tn(kd[us[h]], v_new[h]) for h in hs]
            for h in hs:
                ov = o_st[h] + o_in[h]
                o_ref[rs[c], sl[h]] = ov
                zv = z_ref[rs[c], sl[h]]
                y = ov * lax.rsqrt(jnp.mean(ov * ov, axis=-1, keepdims=True) + EPS) * g_ref[...] * (zv * _sigmoid(zv))
                y_ref[rs[c], sl[h]] = y.astype(y_ref.dtype)
                yt_ref[sl[h], rs[c]] = y.T.astype(yt_ref.dtype)
            st = [st[h] * dec[us[h]] + s_up[h] for h in hs]
        for h in hs:
            s_ref[h] = st[h]

    rows = DN_STEP * DN_CHUNK
    row = pl.BlockSpec((rows, DN_W), lambda n: (n, 0))
    return pl.pallas_call(
        body, name=name, grid=(n_chunks // DN_STEP,),
        in_specs=[row, row, row, pl.BlockSpec((rows, 128), lambda n: (n, 0)),
                  pl.BlockSpec((rows, DN_W), lambda n: (n, _Z_COL)), pl.BlockSpec((1, DN_DIM), lambda n: (0, 0))],
        out_specs=(row, pl.BlockSpec((DN_STEP, DN_HEADS, DN_DIM, DN_DIM), lambda n: (n, 0, 0, 0)),
                   pl.BlockSpec((DN_STEP, DN_HEADS, DN_CHUNK, DN_CHUNK), lambda n: (n, 0, 0, 0)),
                   row, pl.BlockSpec((DN_W, rows), lambda n: (0, n))),
        out_shape=(jax.ShapeDtypeStruct((t, DN_W), F32),
                   jax.ShapeDtypeStruct((n_chunks, DN_HEADS, DN_DIM, DN_DIM), F32),
                   jax.ShapeDtypeStruct((n_chunks, DN_HEADS, DN_CHUNK, DN_CHUNK), F32),
                   jax.ShapeDtypeStruct((t, DN_W), MXU_DTYPE), jax.ShapeDtypeStruct((DN_W, t), MXU_DTYPE)),
        scratch_shapes=[pltpu.VMEM((DN_HEADS, DN_DIM, DN_DIM), F32)],
        compiler_params=_cp("arbitrary"))(q, k, v, bg, proj, o_gain)


def gdn_bwd(q, k, v, bg, sall, tall, do, name):
    t = q.shape[0]
    n_chunks = t // DN_CHUNK

    def body(q_ref, k_ref, v_ref, bg_ref, sall_ref, tall_ref, do_ref, dq_ref, dk_ref, dv_ref, dbg_ref, ds_ref):
        n = pl.program_id(0)

        @pl.when(n == 0)
        def _():
            ds_ref[...] = jnp.zeros_like(ds_ref)

        lower, strict = _chunk_masks()
        ltri = jnp.where(lower, 1.0, 0.0)
        bgv = bg_ref[...]
        gcs = _hi(ltri, bgv)
        gcs_t = gcs.T
        lane = lax.broadcasted_iota(jnp.int32, (DN_CHUNK, 128), 1)
        rowi = lax.broadcasted_iota(jnp.int32, (DN_CHUNK, 1), 0)
        hs = range(DN_HEADS)
        each = lambda fn, *ls: [fn(*a) for a in zip(*ls)]
        rsum = lambda a: jnp.sum(a, axis=-1, keepdims=True)
        sl = [slice(h * DN_DIM, (h + 1) * DN_DIM) for h in hs]
        st = [sall_ref[0, h] for h in hs]
        tms = [tall_ref[0, h] for h in hs]
        dsn = [ds_ref[h] for h in hs]
        qh = [q_ref[:, sl[h]] for h in hs]
        kh = [k_ref[:, sl[h]] for h in hs]
        vh = [v_ref[:, sl[h]] for h in hs]
        doh = [do_ref[:, sl[h]] for h in hs]
        beta = [bgv[:, h:h + 1] for h in hs]
        com = [_chunk_common(qh[h], kh[h], beta[h], gcs[:, DN_HEADS + h:DN_HEADS + h + 1],
                             gcs_t[DN_HEADS + h:DN_HEADS + h + 1, :], lower) for h in hs]
        gam, eg, dec, kdf, kb, bmat, qmat = zip(*com)
        rhs_w = each(lambda a, b: a * b, kb, eg)
        uw = each(lambda t_, v_, b_, r_: _hi(t_, jnp.concatenate([v_ * b_, r_], axis=-1)), tms, vh, beta, rhs_w)
        qd = each(lambda a, b: a * b, qh, eg)
        kd = each(lambda a, b: a * b, kh, kdf)
        pmat = each(lambda a, b: a * b, qmat, gam)
        v_new = each(lambda uw_, s_: uw_[:, :DN_DIM] - _mx(uw_[:, DN_DIM:], s_), uw, st)
        dqd = each(_mx_nt, doh, st)
        ds_o = each(_mx_tn, qd, doh)
        dp = each(lambda d_, v_: jnp.where(lower, _mx_nt(d_, v_), 0.0), doh, v_new)
        dvn_o = each(_mx_tn, pmat, doh)
        ddec = each(lambda d_, s_: jnp.sum(rsum(d_ * s_), axis=0, keepdims=True), dsn, st)
        dkd = each(_mx_nt, v_new, dsn)
        dvn = each(lambda a, k_, d_: a + _mx(k_, d_), dvn_o, kd, dsn)
        dw = each(lambda d_, s_: -_mx_nt(d_, s_), dvn, st)
        ds_w = each(lambda uw_, d_: _mx_tn(uw_[:, DN_DIM:], d_), uw, dvn)
        for h in hs:
            ds_ref[h] = ds_o[h] + dec[h] * dsn[h] - ds_w[h]
        dr = each(lambda t_, a, b: _hi_tn(t_, jnp.concatenate([a, b], axis=-1)), tms, dvn, dw)
        da = each(lambda r_, uw_: jnp.where(strict, -_hi_nt(r_, uw_), 0.0), dr, uw)
        dru = [r_[:, :DN_DIM] for r_ in dr]
        drw = [r_[:, DN_DIM:] for r_ in dr]
        db = each(lambda a, b: a * b, da, gam)
        dq_m = each(lambda a, b: a * b, dp, gam)
        e = each(lambda a, bm, p_, qm, g_: (a * bm + p_ * qm) * g_, da, bmat, dp, qmat, gam)
        dkb = each(lambda b_, k_, r_, e_: _mx(b_, k_) + r_ * e_, db, kh, drw, eg)
        dk = each(lambda b_, kb_, m_, q_, d_, f_: _mx_tn(b_, kb_) + _mx_tn(m_, q_) + d_ * f_, db, kb, dq_m, qh, dkd, kdf)
        dq = each(lambda m_, k_, d_, e_: _mx(m_, k_) + d_ * e_, dq_m, kh, dqd, eg)
        tk = each(lambda a, b: rsum(a * b), dkd, kd)
        dbeta_all = jnp.zeros((DN_CHUNK, 128), F32)
        dgc_all = jnp.zeros((DN_CHUNK, 128), F32)
        for h in hs:
            dgc = (jnp.sum(e[h], axis=1, keepdims=True) - jnp.sum(e[h].T, axis=1, keepdims=True)
                   + rsum(dqd[h] * qd[h]) - tk[h] + rsum(drw[h] * rhs_w[h]))
            dgl = jnp.sum(tk[h], axis=0, keepdims=True) + ddec[h] * dec[h]
            dgc = dgc + jnp.where(rowi == DN_CHUNK - 1, dgl, 0.0)
            dbeta = rsum(dru[h] * vh[h]) + rsum(dkb[h] * kh[h])
            dq_ref[:, sl[h]] = dq[h]
            dk_ref[:, sl[h]] = dk[h] + dkb[h] * beta[h]
            dv_ref[:, sl[h]] = dru[h] * beta[h]
            dbeta_all = jnp.where(lane == h, dbeta, dbeta_all)
            dgc_all = jnp.where(lane == DN_HEADS + h, dgc, dgc_all)
        dbg_ref[...] = dbeta_all + _hi_tn(ltri, dgc_all)

    rev = lambda n: (n_chunks - 1 - n, 0)
    row = pl.BlockSpec((DN_CHUNK, DN_W), rev)
    small = pl.BlockSpec((DN_CHUNK, 128), rev)
    return pl.pallas_call(
        body, name=name, grid=(n_chunks,),
        in_specs=[row, row, row, small,
                  pl.BlockSpec((1, DN_HEADS, DN_DIM, DN_DIM), lambda n: (n_chunks - 1 - n, 0, 0, 0)),
                  pl.BlockSpec((1, DN_HEADS, DN_CHUNK, DN_CHUNK), lambda n: (n_chunks - 1 - n, 0, 0, 0)), row],
        out_specs=(row, row, row, small),
        out_shape=(jax.ShapeDtypeStruct((t, DN_W), F32),) * 3 + (jax.ShapeDtypeStruct((t, 128), F32),),
        scratch_shapes=[pltpu.VMEM((DN_HEADS, DN_DIM, DN_DIM), F32)],
        compiler_params=_cp("arbitrary"))(q, k, v, bg, sall, tall, do)


def gdn_out_bwd(o, proj, o_gain, dx, w_out, name, tm=512, after=None):
    t = o.shape[0]
    tm = min(tm, t)

    def body(o_ref, z_ref, g_ref, dx_ref, w_ref, *rest):
        do_ref, dz_ref, dg_ref = rest[-3:]
        i = pl.program_id(0)
        dy = _mx_nt(dx_ref[...], w_ref[...])
        dg = jnp.zeros((1, DN_DIM), F32)
        for h in range(DN_HEADS):
            sl = slice(h * DN_DIM, (h + 1) * DN_DIM)
            ov, zv, dyv = o_ref[:, sl], z_ref[:, sl], dy[:, sl]
            r = lax.rsqrt(jnp.mean(ov * ov, axis=-1, keepdims=True) + EPS)
            oh = ov * r
            sg = _sigmoid(zv)
            dz_ref[:, sl] = (dyv * oh * g_ref[...] * sg * (1.0 + zv * (1.0 - sg))).astype(dz_ref.dtype)
            don = dyv * (zv * sg)
            dg = dg + jnp.sum(don * oh, axis=0, keepdims=True)
            doh = don * g_ref[...]
            do_ref[:, sl] = r * (doh - oh * jnp.mean(doh * oh, axis=-1, keepdims=True))

        @pl.when(i == 0)
        def _():
            dg_ref[...] = dg

        @pl.when(i > 0)
        def _():
            dg_ref[...] += dg

    row = pl.BlockSpec((tm, DN_W), lambda i: (i, 0))
    one = pl.BlockSpec((1, DN_DIM), lambda i: (0, 0))
    in_specs = [row, pl.BlockSpec((tm, DN_W), lambda i: (i, _Z_COL)), one,
                pl.BlockSpec((tm, dx.shape[1]), lambda i: (i, 0)), pl.BlockSpec(w_out.shape, lambda i: (0, 0))]
    args = [o, proj, o_gain, dx, w_out]
    if after is not None:
        in_specs.append(pl.BlockSpec(memory_space=pl.ANY))
        args.append(after)
    return pl.pallas_call(
        body, name=name, grid=(t // tm,), in_specs=in_specs, out_specs=(row, row, one),
        out_shape=(jax.ShapeDtypeStruct((t, DN_W), F32), jax.ShapeDtypeStruct((t, DN_W), MXU_DTYPE),
                   jax.ShapeDtypeStruct((1, DN_DIM), F32)),
        compiler_params=_cp("arbitrary"))(*args)


def _peer(k):
    x, y, c = lax.axis_index("x"), lax.axis_index("y"), lax.axis_index("c")
    px = 1 - x if k & 4 else x
    py = 1 - y if k & 2 else y
    pc = 1 - c if k & 1 else c
    return (px, py, pc), 4 * px + 2 * py + pc


_HBM = pl.BlockSpec(memory_space=pltpu.HBM)
_SEM = pl.BlockSpec(memory_space=pltpu.SEMAPHORE)
_DATAFLOW = pltpu.SideEffectType.DATAFLOW_SIDE_EFFECTING
N_PEER = N_DEV - 1


def join_blocks(parts, rows, name, tc=256):
    n, r, c = parts.shape

    def body(p_ref, o_ref):
        for j in range(n):
            o_ref[j * r:(j + 1) * r, :] = p_ref[j]
        if rows > n * r:
            o_ref[n * r:, :] = jnp.zeros((rows - n * r, tc), o_ref.dtype)

    return pl.pallas_call(
        body, name=name, grid=(c // tc,), in_specs=[pl.BlockSpec((n, r, tc), lambda i: (0, 0, i))],
        out_specs=pl.BlockSpec((rows, tc), lambda i: (0, i)), out_shape=jax.ShapeDtypeStruct((rows, c), parts.dtype),
        compiler_params=_cp("parallel"))(parts)


def split_blocks(whole, n, r, name, tc=256):
    rows, c = whole.shape

    def body(w_ref, o_ref):
        for j in range(n):
            o_ref[j] = w_ref[j * r:(j + 1) * r, :]

    return pl.pallas_call(
        body, name=name, grid=(c // tc,), in_specs=[pl.BlockSpec((rows, tc), lambda i: (0, i))],
        out_specs=pl.BlockSpec((n, r, tc), lambda i: (0, 0, i)), out_shape=jax.ShapeDtypeStruct((n, r, c), whole.dtype),
        compiler_params=_cp("parallel"))(whole)


_ALL_PEERS = tuple(range(1, N_DEV))
_OTHER_CHIPS = (2, 4, 6)
_SIBLING = 1


def send_start(srcs, name, scatter, after, masks=_ALL_PEERS):
    na, n_peer = len(srcs), len(masks)
    ns = (2 * n_peer + 1) * na
    lands = [lax.empty((N_DEV,) + (s.shape[1:] if scatter else s.shape), s.dtype) for s in srcs]
    extra = [] if after is None else [after]

    def body(*refs):
        src_refs, land_refs = refs[:na], refs[na:2 * na]
        sems = refs[2 * na + len(extra):2 * na + len(extra) + ns]
        land_out, token = refs[-1 - na:-1], refs[-1]
        _, me = _peer(0)
        for a in range(na):
            pltpu.make_async_copy(src_refs[a].at[me] if scatter else src_refs[a], land_out[a].at[me],
                                  sems[2 * n_peer * na + a]).start()
        peers = [_peer(k) for k in masks]
        for a in range(na):
            for k, (peer, pid) in enumerate(peers):
                pltpu.make_async_remote_copy(
                    src_ref=src_refs[a].at[pid] if scatter else src_refs[a], dst_ref=land_refs[a].at[me],
                    send_sem=sems[2 * (a * n_peer + k)], recv_sem=sems[2 * (a * n_peer + k) + 1],
                    device_id=peer, device_id_type=MESH).start()
        token[...] = jnp.zeros_like(token)

    hbm = lambda arrs: tuple(pltpu.HBM(a.shape, a.dtype) for a in arrs)
    outs = pl.pallas_call(
        body, name=name,
        out_shape=(pltpu.SemaphoreType.DMA(()),) * ns + hbm(srcs) + hbm(lands) + (jax.ShapeDtypeStruct((8, 128), F32),),
        in_specs=[_HBM] * (2 * na) + [pl.BlockSpec(memory_space=pl.ANY)] * len(extra),
        out_specs=(_SEM,) * ns + (_HBM,) * (2 * na) + (pl.BlockSpec(memory_space=pltpu.VMEM),),
        input_output_aliases={i: ns + i for i in range(2 * na)},
        compiler_params=pltpu.CompilerParams(has_side_effects=_DATAFLOW),
    )(*[pltpu.with_memory_space_constraint(a, pltpu.HBM) for a in list(srcs) + lands], *extra)
    return outs[:ns], outs[ns:ns + na], outs[ns + na:ns + 2 * na], outs[-1]


def send_wait(sems, srcs_thru, lands_thru, name, scatter, after, masks=_ALL_PEERS):
    na, n_peer = len(srcs_thru), len(masks)
    ns = (2 * n_peer + 1) * na

    def body(*refs):
        src_refs, land_refs, sm = refs[:na], refs[na:2 * na], refs[2 * na:2 * na + ns]
        _, me = _peer(0)
        for a in range(na):
            pltpu.make_async_copy(src_refs[a].at[me] if scatter else src_refs[a], land_refs[a].at[me],
                                  sm[2 * n_peer * na + a]).wait()
        for k, mask in enumerate(masks):
            peer, pid = _peer(mask)
            for a in range(na):
                cp = pltpu.make_async_remote_copy(
                    src_ref=src_refs[a].at[pid] if scatter else src_refs[a], dst_ref=land_refs[a].at[pid],
                    send_sem=sm[2 * (a * n_peer + k)], recv_sem=sm[2 * (a * n_peer + k) + 1],
                    device_id=peer, device_id_type=MESH)
                cp.wait_send()
                cp.wait_recv()

    hbm = lambda arrs: tuple(pltpu.HBM(a.shape, a.dtype) for a in arrs)
    outs = pl.pallas_call(
        body, name=name, out_shape=hbm(srcs_thru) + hbm(lands_thru),
        in_specs=[_HBM] * (2 * na) + [_SEM] * ns + [pl.BlockSpec(memory_space=pl.ANY)], out_specs=(_HBM,) * (2 * na),
        input_output_aliases={i: i for i in range(2 * na)},
        compiler_params=pltpu.CompilerParams(has_side_effects=_DATAFLOW),
    )(*srcs_thru, *lands_thru, *sems, after)
    return outs[na:]


def relay_blocks(lands, name, masks=_OTHER_CHIPS):
    na, n_slot = len(lands), 1 + len(masks)

    def body(*refs):
        land_refs, rs = refs[:na], refs[2 * na:]
        sibling, _ = _peer(_SIBLING)
        copies = [pltpu.make_async_remote_copy(
            src_ref=land_refs[a].at[_peer(mask)[1]], dst_ref=land_refs[a].at[_peer(mask)[1]],
            send_sem=rs[2 * (a * n_slot + k)], recv_sem=rs[2 * (a * n_slot + k) + 1],
            device_id=sibling, device_id_type=MESH) for a in range(na) for k, mask in enumerate((0,) + tuple(masks))]
        for cp in copies:
            cp.start()
        for cp in copies:
            cp.wait_send()
            cp.wait_recv()

    return pl.pallas_call(
        body, name=name, out_shape=tuple(pltpu.HBM(a.shape, a.dtype) for a in lands),
        in_specs=[_HBM] * na, out_specs=(_HBM,) * na, input_output_aliases={i: i for i in range(na)},
        scratch_shapes=[pltpu.SemaphoreType.DMA(())] * (2 * na * n_slot),
        compiler_params=pltpu.CompilerParams(has_side_effects=_DATAFLOW),
    )(*lands)


def _adamw(w, g, m, v):
    m = ADAM_B1 * m + (1.0 - ADAM_B1) * g
    v = ADAM_B2 * v + (1.0 - ADAM_B2) * (g * g)
    m_hat = m / (1.0 - ADAM_B1 ** ADAM_STEP)
    v_hat = v / (1.0 - ADAM_B2 ** ADAM_STEP)
    return -ADAM_LR * (m_hat / (jnp.sqrt(v_hat) + ADAM_EPS) + ADAM_WD * w), m, v


def adam_sum(w, pieces, m, v, name, layer=0, into=None):
    nl, r, c = w.shape
    tc = _pick(c, 256)

    def body(w_ref, p_ref, m_ref, v_ref, *rest):
        g_ref, d_ref, nm_ref, nv_ref = rest[-4:]
        g = p_ref[0].astype(F32)
        for s in range(1, N_DEV):
            g = g + p_ref[s].astype(F32)
        g_ref[0] = g
        d_ref[0], nm_ref[0], nv_ref[0] = _adamw(w_ref[0], g, m_ref[0], v_ref[0])

    row = pl.BlockSpec((1, r, tc), lambda i: (layer, 0, i))
    out = jax.ShapeDtypeStruct((nl, r, c), F32)
    extra = [] if into is None else list(into)
    return pl.pallas_call(
        body, name=name, grid=(c // tc,),
        in_specs=[row, pl.BlockSpec((N_DEV, r, tc), lambda i: (0, 0, i)), row, row]
        + [pl.BlockSpec(memory_space=pl.ANY)] * len(extra),
        out_specs=(row,) * 4, out_shape=(out,) * 4,
        input_output_aliases={4 + i: i for i in range(len(extra))},
        compiler_params=_cp("parallel"))(w, pieces, m, v, *extra)


def sum_rows(gathered, name):
    _, r, c = gathered.shape

    def body(p_ref, o_ref):
        g = p_ref[0]
        for s in range(1, N_DEV):
            g = g + p_ref[s]
        o_ref[...] = g

    return pl.pallas_call(body, name=name, out_shape=jax.ShapeDtypeStruct((r, c), F32))(gathered)


def adam_small(w, g, m, v, name):
    def body(w_ref, g_ref, m_ref, v_ref, d_ref, nm_ref, nv_ref):
        d_ref[...], nm_ref[...], nv_ref[...] = _adamw(w_ref[...], g_ref[...], m_ref[...], v_ref[...])

    out = jax.ShapeDtypeStruct(w.shape, F32)
    return pl.pallas_call(body, name=name, out_shape=(out,) * 3)(w, g, m, v)


def _rope_tables(t):
    inv_freq = 10000.0 ** (-jnp.arange(0, HEAD_DIM, 2, dtype=F32) / HEAD_DIM)
    lanes = lambda a: jnp.concatenate([a] * (128 // a.shape[-1]), axis=-1)
    base = (jnp.arange(t // QK_TM, dtype=F32) * QK_TM)[:, None] * inv_freq[None, :]
    offs = jnp.arange(QK_TM, dtype=F32)[:, None] * inv_freq[None, :]
    return (lanes(jnp.cos(base))[:, None, :], lanes(jnp.sin(base))[:, None, :], lanes(jnp.cos(offs)), lanes(jnp.sin(offs)))


def _lane_row(vec8):
    return jnp.pad(vec8.reshape(1, DN_HEADS), ((0, 0), (DN_HEADS, 128 - 2 * DN_HEADS)))


def _ffn_bwd(x, norm_g, w_gu, w_d, saved, dy, tag, after=None):
    ft, gu, at = saved
    dgu = ffn_dact(dy, w_d, gu, f"{tag}_d_gate_up", after=after)
    dwd = mm_at(at, dy, f"{tag}_dw_down")
    nj = D_FF // GU_TILE
    dwgu = mm_at(ft, dgu, f"{tag}_dw_gate_up", transposed=True, tn=GU_TILE, row_block=lambda q: (q % 2) * nj + q // 2)
    dx, dg = mm_rms_bwd(dgu, w_gu, x, norm_g, dy, f"{tag}_d_norm", chunks=_GU_CHUNKS)
    return dx, dwgu, dwd, dg


def local_step(x, target, small, weights_of, grads_out, after=None):
    t = x.shape[0]
    rope = _rope_tables(t)
    alog_row, dtb_row = _lane_row(small["odd_a_log"]), _lane_row(small["odd_dt_bias"])

    h0, h0t = rms_fwd(x, small["even_norm"], "even_norm", after=after)
    we = weights_of("even", h0)
    small = {**small, **we.get("small", {})}
    proj0 = mm_nt(h0, we["w_in"], "even_in_proj", after=we.get("after"))
    qr, kr = qk_prep_fwd(proj0, small["even_q_gain"], small["even_k_gain"], rope, "even_qk_prep")
    y_attn, mix0, mix0_t, lse = swa_fwd(qr, kr, proj0, small["even_sinks"], "even_swa")
    mix0, mix0_t = gconv_fwd(proj0, small["even_conv_w"], mix0, mix0_t, "even_gconv")
    we = {**we, **weights_of("even_out", mix0)}
    x1, f0, f0t = mm_nn_res_norm(mix0, we["w_out"], x, small["ffn_norm0"], "even_out_proj")
    w0 = weights_of("ffn0", x1)
    gu0, a0, a0t = ffn_up(f0, w0["gate_up"], "ffn0_gate_up")
    ffn0 = (f0t, gu0, a0t)
    x2, h1, h1t = mm_nn_res_norm(a0, w0["down"], x1, small["odd_norm"], "ffn0_down")

    wo = weights_of("odd", x2)
    proj1 = mm_nt(h1, wo["w_in"], "odd_in_proj")
    qn, kn, vs, bg, conv1 = gdn_prep_fwd(proj1, small["odd_conv_w"], alog_row, dtb_row, "odd_prep")
    o, sall, tall, og, ogt = gdn_fwd(qn, kn, vs, bg, proj1, small["odd_o_gain"], "odd_delta_rule")
    x3, f1, f1t = mm_nn_res_norm(og, wo["w_out"], x2, small["ffn_norm1"], "odd_out_proj")
    w1 = weights_of("ffn1", x3)
    gu1, a1, a1t = ffn_up(f1, w1["gate_up"], "ffn1_gate_up")
    ffn1 = (f1t, gu1, a1t)
    dy, loss_row = mm_nn_res_loss(a1, w1["down"], x3, target, "ffn1_down_loss")

    gs = {}
    dx3, dwgu, dwd, gs["ffn_norm1"] = _ffn_bwd(x3, small["ffn_norm1"], w1["gate_up"], w1["down"], ffn1, dy, "ffn1")
    tok = grads_out("ffn1", {"gate_up": dwgu, "down": dwd})

    do, dz, gs["odd_o_gain"] = gdn_out_bwd(o, proj1, small["odd_o_gain"], dx3, wo["w_out"], "odd_d_gate_norm", after=tok)
    dwo = mm_at(ogt, dx3, "odd_dw_out")
    dqn, dkn, dvs, dbg = gdn_bwd(qn, kn, vs, bg, sall, tall, do, "odd_d_delta_rule")
    dproj1, gs["odd_conv_w"], ddt_row, dal_row = gdn_prep_bwd(
        proj1, conv1, small["odd_conv_w"], alog_row, dtb_row, dqn, dkn, dvs, dbg, dz, "odd_d_prep")
    gs["odd_dt_bias"] = ddt_row[:, DN_HEADS:2 * DN_HEADS]
    gs["odd_a_log"] = dal_row[:, DN_HEADS:2 * DN_HEADS]
    dwi = mm_at(h1t, dproj1, "odd_dw_in", transposed=True)
    dx2, gs["odd_norm"] = mm_rms_bwd(dproj1, wo["w_in"], x2, small["odd_norm"], dx3, "odd_d_norm")
    tok = grads_out("odd", {"w_in": dwi, "w_out": dwo})

    dx1, dwgu, dwd, gs["ffn_norm0"] = _ffn_bwd(x1, small["ffn_norm0"], w0["gate_up"], w0["down"], ffn0, dx2, "ffn0",
                                               after=tok)
    tok = grads_out("ffn0", {"gate_up": dwgu, "down": dwd})

    dmix = mm_nt(dx1, we["w_out"], "even_d_mix", after=tok)
    dwo = mm_at(mix0_t, dx1, "even_dw_out")
    dproj0 = lax.empty((t, EVEN_IN_W), MXU_DTYPE)
    dqr, dkr, dproj0, gs["even_sinks"] = swa_bwd(
        qr, kr, proj0, small["even_sinks"], y_attn, lse, dmix, dproj0, "even_d_swa")
    dproj0, gs["even_q_gain"], gs["even_k_gain"] = qk_prep_bwd(
        proj0, small["even_q_gain"], small["even_k_gain"], rope, dqr, dkr, dproj0, "even_d_qk_prep")
    dproj0, gs["even_conv_w"] = gconv_bwd(proj0, small["even_conv_w"], dmix, dproj0, "even_d_gconv")
    dwi = mm_at(h0t, dproj0, "even_dw_in", transposed=True, chunks=_EVEN_D_CHUNKS)
    tok = grads_out("even", {"w_in": dwi, "w_out": dwo})
    grad_x, gs["even_norm"] = mm_rms_bwd(dproj0, we["w_in"], x, small["even_norm"], dx1, "even_d_norm", after=tok,
                                         chunks=_EVEN_D_CHUNKS)
    return loss_row, grad_x, gs


_SMALL_ORDER = ("even_norm", "even_q_gain", "even_k_gain", "even_sinks", "odd_a_log", "odd_dt_bias", "odd_o_gain",
                "ffn_norm0", "ffn_norm1", "odd_norm", "even_conv_w", "odd_conv_w")
_SMALL_SIZE = {"even_norm": 1024, "even_q_gain": 64, "even_k_gain": 64, "even_sinks": 8, "odd_a_log": 8,
               "odd_dt_bias": 8, "odd_o_gain": 128, "ffn_norm0": 1024, "ffn_norm1": 1024, "odd_norm": 1024,
               "even_conv_w": 3 * 512, "odd_conv_w": 4 * 3072}
_N_REPL = 9


def _pack_rows(vals):
    flat = jnp.concatenate([v.reshape(-1) for v in vals])
    pad = (-flat.shape[0]) % 1024
    return jnp.pad(flat, (0, pad)).reshape(-1, 128)


def _my_block(full, size, axis):
    me = 4 * lax.axis_index("x") + 2 * lax.axis_index("y") + lax.axis_index("c")
    return lax.dynamic_slice_in_dim(full, me * size, size, axis=axis)


def kernel(x, even_norm, even_w_in, even_q_gain, even_k_gain, even_sinks, even_conv_w, even_w_out, odd_norm, odd_w_in, odd_conv_w, odd_a_log, odd_dt_bias, odd_o_gain, odd_w_out, ffn_norm, ffn_w_gate_up, ffn_w_down, loss_target, m_even_norm, m_even_w_in, m_even_q_gain, m_even_k_gain, m_even_sinks, m_even_conv_w, m_even_w_out, m_odd_norm, m_odd_w_in, m_odd_conv_w, m_odd_a_log, m_odd_dt_bias, m_odd_o_gain, m_odd_w_out, m_ffn_norm, m_ffn_w_gate_up, m_ffn_w_down, v_even_norm, v_even_w_in, v_even_q_gain, v_even_k_gain, v_even_sinks, v_even_conv_w, v_even_w_out, v_odd_norm, v_odd_w_in, v_odd_conv_w, v_odd_a_log, v_odd_dt_bias, v_odd_o_gain, v_odd_w_out, v_ffn_norm, v_ffn_w_gate_up, v_ffn_w_down):
    t = x.shape[1]
    d = D_MODEL

    tr = lambda a: jnp.swapaxes(a, 1, 2)
    shard = {
        "even": {"w_in": tr(even_w_in)[0], "w_out": even_w_out[0]},
        "ffn0": {"gate_up": tr(ffn_w_gate_up)[0], "down": ffn_w_down[0]},
        "odd": {"w_in": tr(odd_w_in)[0], "w_out": odd_w_out[0]},
        "ffn1": {"gate_up": tr(ffn_w_gate_up)[1], "down": ffn_w_down[1]},
    }
    given = {
        ("even", "w_in"): ("even_w_in", even_w_in, m_even_w_in, v_even_w_in, 0),
        ("even", "w_out"): ("even_w_out", even_w_out, m_even_w_out, v_even_w_out, 0),
        ("odd", "w_in"): ("odd_w_in", odd_w_in, m_odd_w_in, v_odd_w_in, 0),
        ("odd", "w_out"): ("odd_w_out", odd_w_out, m_odd_w_out, v_odd_w_out, 0),
        ("ffn0", "gate_up"): ("ffn_w_gate_up", ffn_w_gate_up, m_ffn_w_gate_up, v_ffn_w_gate_up, 0),
        ("ffn1", "gate_up"): ("ffn_w_gate_up", ffn_w_gate_up, m_ffn_w_gate_up, v_ffn_w_gate_up, 1),
        ("ffn0", "down"): ("ffn_w_down", ffn_w_down, m_ffn_w_down, v_ffn_w_down, 0),
        ("ffn1", "down"): ("ffn_w_down", ffn_w_down, m_ffn_w_down, v_ffn_w_down, 1),
    }

    def whole(group, parts):
        col, row = tuple(shard[group])
        if group == "odd":
            w_col = join_blocks(parts[0], ODD_IN_PAD, "odd_w_in_join")
        else:
            w_col = parts[0].reshape(-1, d)
        return {col: w_col, row: parts[1].reshape(-1, d)}

    wire = {g: [a.astype(MXU_DTYPE) for a in shard[g].values()] for g in shard}
    wire["even_out"] = [wire["even"].pop()]
    wire["even"].append(_pack_rows([odd_norm, even_conv_w, odd_conv_w]))
    sems, srcs_thru, lands_thru, tok = send_start(wire["even"], "gather_even_start", False, None, masks=_OTHER_CHIPS)
    gathers = {"even": (sems, srcs_thru, lands_thru)}

    def start_rest(after):
        order = ("even_out", "ffn0", "odd", "ffn1")
        sems, srcs_thru, lands_thru, token = send_start([a for g in order for a in wire[g]], "gather_rest_start", False,
                                                        after)
        a0, na = 0, sum(len(wire[g]) for g in order)
        for g in order:
            a1 = a0 + len(wire[g])
            gathers[g] = (sems[2 * N_PEER * a0:2 * N_PEER * a1] + sems[2 * N_PEER * na + a0:2 * N_PEER * na + a1],
                          srcs_thru[a0:a1], lands_thru[a0:a1])
            a0 = a1
        return token

    o1 = d // N_DEV
    o2 = o1 + 3 * CONV_CH // N_DEV

    def weights_of(group, after):
        first = group == "even"
        lands = send_wait(*gathers[group], f"gather_{group}_wait", False, after,
                          masks=_OTHER_CHIPS if first else _ALL_PEERS)
        if first:
            lands = relay_blocks(lands, "gather_even_relay")
        if group == "even_out":
            return {"w_out": lands[0].reshape(-1, d)}
        if group != "even":
            return whole(group, lands)
        sg = lands[1].reshape(N_DEV, -1)
        return {"w_in": lands[0].reshape(-1, d), "after": start_rest(lands[0]), "small": {
            "odd_norm": sg[:, :o1].reshape(1, d),
            "even_conv_w": sg[:, o1:o2].reshape(N_DEV, 3, CONV_CH // N_DEV).transpose(1, 0, 2).reshape(3, CONV_CH),
            "odd_conv_w": sg[:, o2:o2 + 4 * _QKV_W // N_DEV].reshape(N_DEV, 4, _QKV_W // N_DEV)
            .transpose(1, 0, 2).reshape(4, _QKV_W)}}

    sent = {}

    def grads_out(group, dws):
        col, row = tuple(shard[group])
        c = shard[group][col].shape[0]
        pieces = [split_blocks(dws[col], N_DEV, c, "odd_dw_in_split") if group == "odd"
                  else dws[col].reshape((N_DEV,) + shard[group][col].shape),
                  dws[row].reshape((N_DEV,) + shard[group][row].shape)]
        sems, srcs_thru, lands_thru, token = send_start(pieces, f"exchange_{group}_start", True, None)
        sent[group] = (sems, srcs_thru, lands_thru, pieces)
        return token

    small = {
        "even_norm": even_norm, "even_q_gain": even_q_gain, "even_k_gain": even_k_gain, "even_sinks": even_sinks,
        "odd_a_log": odd_a_log.reshape(-1), "odd_dt_bias": odd_dt_bias.reshape(-1), "odd_o_gain": odd_o_gain,
        "ffn_norm0": ffn_norm[0:1], "ffn_norm1": ffn_norm[1:2],
    }

    loss_row, grad_x, gs = local_step(x.reshape(t, d), loss_target.reshape(t, d), small, weights_of, grads_out, after=tok)

    rows = _pack_rows([gs[n] for n in _SMALL_ORDER] + [loss_row[:, 0:1]])
    small_sent = send_start([rows], "gather_small_grads_start", False, None)

    res, behind = {}, small_sent[3]
    for g in ("ffn1", "odd", "ffn0", "even"):
        sems, srcs_thru, lands_thru, pieces = sent[g]
        lands = send_wait(sems, srcs_thru, lands_thru, f"exchange_{g}_wait", True, behind)
        for i, (key, pcs) in enumerate(zip(shard[g], lands)):
            name, w_, m_, v_, layer = given[g, key]
            view = tr if i == 0 else (lambda a: a)
            res[name] = adam_sum(view(w_), pcs, view(m_), view(v_), f"adamw_{g}_{key}", layer=layer, into=res.get(name))
        behind = res[name][0]
    for name in ("even_w_in", "odd_w_in", "ffn_w_gate_up"):
        res[name] = tuple(tr(a) for a in res[name])

    (rows_g,) = send_wait(*small_sent[:3], "gather_small_grads_wait", False, behind)
    tot = sum_rows(rows_g, "sum_small_grads").reshape(-1)
    off, sgrad = 0, {}
    for n in _SMALL_ORDER:
        sgrad[n] = tot[off:off + _SMALL_SIZE[n]]
        off += _SMALL_SIZE[n]
    loss = tot[off]

    repl = _SMALL_ORDER[:_N_REPL]
    repl_w = {"even_norm": even_norm, "even_q_gain": even_q_gain, "even_k_gain": even_k_gain, "even_sinks": even_sinks,
              "odd_a_log": odd_a_log, "odd_dt_bias": odd_dt_bias, "odd_o_gain": odd_o_gain,
              "ffn_norm0": ffn_norm[0], "ffn_norm1": ffn_norm[1]}
    repl_m = {"even_norm": m_even_norm, "even_q_gain": m_even_q_gain, "even_k_gain": m_even_k_gain,
              "even_sinks": m_even_sinks, "odd_a_log": m_odd_a_log, "odd_dt_bias": m_odd_dt_bias,
              "odd_o_gain": m_odd_o_gain, "ffn_norm0": m_ffn_norm[0], "ffn_norm1": m_ffn_norm[1]}
    repl_v = {"even_norm": v_even_norm, "even_q_gain": v_even_q_gain, "even_k_gain": v_even_k_gain,
              "even_sinks": v_even_sinks, "odd_a_log": v_odd_a_log, "odd_dt_bias": v_odd_dt_bias,
              "odd_o_gain": v_odd_o_gain, "ffn_norm0": v_ffn_norm[0], "ffn_norm1": v_ffn_norm[1]}
    pk = lambda dct: _pack_rows([dct[n] for n in repl])
    pd_, pm_, pv_ = adam_small(pk(repl_w), pk(sgrad), pk(repl_m), pk(repl_v), "adamw_replicated")
    sres = {}
    off = 0
    for n in repl:
        sz = _SMALL_SIZE[n]
        sres[n] = (sgrad[n], pd_.reshape(-1)[off:off + sz], pm_.reshape(-1)[off:off + sz], pv_.reshape(-1)[off:off + sz])
        off += sz
    g_on = _my_block(sgrad["odd_norm"].reshape(1, d), d // N_DEV, 1)
    g_ec = _my_block(sgrad["even_conv_w"].reshape(3, CONV_CH), CONV_CH // N_DEV, 1)
    g_oc = _my_block(sgrad["odd_conv_w"].reshape(4, _QKV_W), _QKV_W // N_DEV, 1)
    shard_w = _pack_rows([odd_norm, even_conv_w, odd_conv_w])
    sd_, sm_, sv_ = adam_small(shard_w, _pack_rows([g_on, g_ec, g_oc]),
                               _pack_rows([m_odd_norm, m_even_conv_w, m_odd_conv_w]),
                               _pack_rows([v_odd_norm, v_even_conv_w, v_odd_conv_w]), "adamw_sharded_small")
    off = 0
    for n, gfull, like in (("odd_norm", g_on, odd_norm), ("even_conv_w", g_ec, even_conv_w), ("odd_conv_w", g_oc, odd_conv_w)):
        sz = like.size
        sres[n] = (gfull, sd_.reshape(-1)[off:off + sz], sm_.reshape(-1)[off:off + sz], sv_.reshape(-1)[off:off + sz])
        off += sz

    def small_out(name, like, kind):
        if name == "ffn_norm":
            return jnp.stack([sres["ffn_norm0"][kind], sres["ffn_norm1"][kind]]).reshape(like.shape)
        return sres[name][kind].reshape(like.shape)

    order = (("even_norm", even_norm), ("even_w_in", even_w_in), ("even_q_gain", even_q_gain),
             ("even_k_gain", even_k_gain), ("even_sinks", even_sinks), ("even_conv_w", even_conv_w),
             ("even_w_out", even_w_out), ("odd_norm", odd_norm), ("odd_w_in", odd_w_in), ("odd_conv_w", odd_conv_w),
             ("odd_a_log", odd_a_log), ("odd_dt_bias", odd_dt_bias), ("odd_o_gain", odd_o_gain),
             ("odd_w_out", odd_w_out), ("ffn_norm", ffn_norm), ("ffn_w_gate_up", ffn_w_gate_up),
             ("ffn_w_down", ffn_w_down))
    outs = [loss, grad_x.reshape(x.shape)]
    for kind in range(4):
        for name, like in order:
            outs.append(res[name][kind] if name in res else small_out(name, like, kind))
    return tuple(outs)
```

```python
import jax
import jax.numpy as jnp
import numpy as np
from jax import lax
from jax.experimental import pallas as pl
from jax.experimental.pallas import tpu as pltpu

F32 = jnp.float32
MXU_DTYPE = jnp.bfloat16
HI = lax.Precision.HIGH
EPS = 1e-6
N_DEV = 8
D_MODEL = 1024
HEAD_DIM = 64
ATTN_HEADS = 8
KV_HEADS = 2
ATTN_BLOCK = 128
Q_W = 512
KV_W = 128
CONV_CH = 512
EVEN_IN_W = 2304
DN_HEADS = 8
DN_DIM = 128
DN_W = 1024
DN_CHUNK = 64
ODD_IN_W = 4112
ODD_IN_PAD = 4224
D_FF = 2816
NEG = -1e30
VMEM_LIMIT = 56 * 1024 * 1024
ADAM_LR, ADAM_B1, ADAM_B2, ADAM_EPS, ADAM_WD, ADAM_STEP = 0.001, 0.9, 0.999, 1e-08, 0.01, 10
MESH = pl.DeviceIdType.MESH


def _cp(*sem):
    return pltpu.CompilerParams(dimension_semantics=sem, vmem_limit_bytes=VMEM_LIMIT)


def _pick(n, cap):
    best = 128
    for t in range(128, cap + 1, 128):
        if n % t == 0:
            best = t
    return best


def _mx(a, b):
    return jnp.dot(a.astype(MXU_DTYPE), b.astype(MXU_DTYPE), preferred_element_type=F32)


def _mx_nt(a, b):
    return lax.dot_general(a.astype(MXU_DTYPE), b.astype(MXU_DTYPE), (((1,), (1,)), ((), ())),
                           preferred_element_type=F32)


def _mx_tn(a, b):
    return lax.dot_general(a.astype(MXU_DTYPE), b.astype(MXU_DTYPE), (((0,), (0,)), ((), ())),
                           preferred_element_type=F32)


def _hi(a, b):
    return jnp.dot(a, b, precision=HI, preferred_element_type=F32)


def _hi_nt(a, b):
    return lax.dot_general(a, b, (((1,), (1,)), ((), ())), precision=HI, preferred_element_type=F32)


def _hi_tn(a, b):
    return lax.dot_general(a, b, (((0,), (0,)), ((), ())), precision=HI, preferred_element_type=F32)


def _sigmoid(x):
    return 0.5 * jnp.tanh(0.5 * x) + 0.5


def _softplus(x):
    return jnp.maximum(x, 0.0) + jnp.log(1.0 + jnp.exp(-jnp.abs(x)))


def mm_nn_res_norm(a, b, res, g, name, tm=512):
    t, k = a.shape
    d = b.shape[1]
    tm = min(tm, t)

    def body(a_ref, b_ref, res_ref, g_ref, y_ref, h_ref, ht_ref):
        y = res_ref[...] + _mx(a_ref[...], b_ref[...])
        y_ref[...] = y
        h = y * lax.rsqrt(jnp.mean(y * y, axis=-1, keepdims=True) + EPS) * g_ref[...]
        h_ref[...] = h.astype(h_ref.dtype)
        ht_ref[...] = h.T.astype(ht_ref.dtype)

    row = pl.BlockSpec((tm, d), lambda i: (i, 0))
    return pl.pallas_call(
        body, name=name, grid=(t // tm,),
        in_specs=[pl.BlockSpec((tm, k), lambda i: (i, 0)), pl.BlockSpec((k, d), lambda i: (0, 0)), row,
                  pl.BlockSpec((1, d), lambda i: (0, 0))],
        out_specs=(row, row, pl.BlockSpec((d, tm), lambda i: (0, i))),
        out_shape=(jax.ShapeDtypeStruct((t, d), F32), jax.ShapeDtypeStruct((t, d), MXU_DTYPE),
                   jax.ShapeDtypeStruct((d, t), MXU_DTYPE)),
        compiler_params=_cp("parallel"))(a, b, res, g)


def mm_nt(a, b, name, out_dtype=F32, tm=2048, after=None):
    m, k = a.shape
    n, _ = b.shape
    tn = _pick(n, 512 if k > 3000 else 1536)
    tm = min(tm, m)

    def body(a_ref, b_ref, *rest):
        o_ref = rest[-1]
        o_ref[...] = _mx_nt(a_ref[...], b_ref[...]).astype(o_ref.dtype)

    in_specs = [pl.BlockSpec((tm, k), lambda j, i: (i, 0)), pl.BlockSpec((tn, k), lambda j, i: (j, 0))]
    args = [a, b]
    if after is not None:
        in_specs.append(pl.BlockSpec(memory_space=pl.ANY))
        args.append(after)
    return pl.pallas_call(
        body, name=name, grid=(n // tn, m // tm), in_specs=in_specs,
        out_specs=pl.BlockSpec((tm, tn), lambda j, i: (i, j)),
        out_shape=jax.ShapeDtypeStruct((m, n), out_dtype), compiler_params=_cp("parallel", "parallel"))(*args)


def mm_at(at, b, name, tk=2048, transposed=False, tn=None, row_block=None, chunks=None):
    m, kk = at.shape
    _, n = b.shape
    tm, tn, tk = _pick(m, 1408), tn or _pick(n, 2816), min(tk, kk)
    nk = kk // tk
    assert chunks is None or (transposed and tn == n)

    def body(a_ref, b_ref, o_ref, acc_ref):
        k = pl.program_id(2)
        p = _mx(a_ref[...], b_ref[...])
        acc = jnp.where(k == 0, p, acc_ref[...] + p)
        acc_ref[...] = acc

        @pl.when(k == nk - 1)
        def _():
            res = (acc.T if transposed else acc).astype(o_ref.dtype)
            if chunks is None:
                o_ref[...] = res
            else:
                for cb, co, size in chunks:
                    o_ref[co:co + size, :] = res[cb:cb + size]

    if transposed:
        rb = row_block or (lambda j: j)
        out_spec = pl.BlockSpec((tn, tm), lambda i, j, k: (rb(j), i))
        out_shape = jax.ShapeDtypeStruct((n, m), MXU_DTYPE)
    else:
        out_spec = pl.BlockSpec((tm, tn), lambda i, j, k: (i, j))
        out_shape = jax.ShapeDtypeStruct((m, n), MXU_DTYPE)
    return pl.pallas_call(
        body, name=name, grid=(m // tm, n // tn, nk),
        in_specs=[pl.BlockSpec((tm, tk), lambda i, j, k: (i, k)), pl.BlockSpec((tk, tn), lambda i, j, k: (k, j))],
        out_specs=out_spec, out_shape=out_shape, scratch_shapes=[pltpu.VMEM((tm, tn), F32)],
        compiler_params=_cp("parallel", "parallel", "arbitrary"))(at, b)


def rms_fwd(x, g, name, tm=512, after=None):
    t, d = x.shape

    def body(x_ref, g_ref, *rest):
        o_ref, ot_ref = rest[-2:]
        xv = x_ref[...]
        r = lax.rsqrt(jnp.mean(xv * xv, axis=-1, keepdims=True) + EPS)
        h = xv * r * g_ref[...]
        o_ref[...] = h.astype(o_ref.dtype)
        ot_ref[...] = h.T.astype(ot_ref.dtype)

    in_specs = [pl.BlockSpec((tm, d), lambda i: (i, 0)), pl.BlockSpec((1, d), lambda i: (0, 0))]
    args = [x, g]
    if after is not None:
        in_specs.append(pl.BlockSpec(memory_space=pl.ANY))
        args.append(after)
    return pl.pallas_call(
        body, name=name, grid=(t // tm,), in_specs=in_specs,
        out_specs=(pl.BlockSpec((tm, d), lambda i: (i, 0)), pl.BlockSpec((d, tm), lambda i: (0, i))),
        out_shape=(jax.ShapeDtypeStruct((t, d), MXU_DTYPE), jax.ShapeDtypeStruct((d, t), MXU_DTYPE)),
        compiler_params=_cp("parallel"))(*args)


def mm_rms_bwd(a, bt, x, g, dres, name, tm=512, after=None, chunks=None):
    t, k = a.shape
    d = bt.shape[1]
    tm = min(tm if k > 3000 else 2 * tm, t)
    chunks = chunks or ((0, 0, k),)

    def body(a_ref, b_ref, x_ref, g_ref, dres_ref, *rest):
        dx_ref, dg_ref = rest[-2:]
        dhv = None
        for ca, cb, size in chunks:
            part = _mx(a_ref[:, ca:ca + size], b_ref[cb:cb + size, :])
            dhv = part if dhv is None else dhv + part
        xv = x_ref[...]
        r = lax.rsqrt(jnp.mean(xv * xv, axis=-1, keepdims=True) + EPS)
        xh = xv * r
        dxh = dhv * g_ref[...]
        dx_ref[...] = dres_ref[...] + r * (dxh - xh * jnp.mean(dxh * xh, axis=-1, keepdims=True))
        part = jnp.sum(dhv * xh, axis=0, keepdims=True)
        dg_ref[...] = jnp.where(pl.program_id(0) == 0, part, dg_ref[...] + part)

    row = pl.BlockSpec((tm, d), lambda i: (i, 0))
    one = pl.BlockSpec((1, d), lambda i: (0, 0))
    in_specs = [pl.BlockSpec((tm, k), lambda i: (i, 0)), pl.BlockSpec((k, d), lambda i: (0, 0)), row, one, row]
    args = [a, bt, x, g, dres]
    if after is not None:
        in_specs.append(pl.BlockSpec(memory_space=pl.ANY))
        args.append(after)
    return pl.pallas_call(
        body, name=name, grid=(t // tm,), in_specs=in_specs, out_specs=(row, one),
        out_shape=(jax.ShapeDtypeStruct((t, d), F32), jax.ShapeDtypeStruct((1, d), F32)),
        compiler_params=_cp("arbitrary"))(*args)


GU_TILE = 1408


def ffn_up(f, wt, name, tm=1024):
    t, d = f.shape
    tm = min(tm, t)
    nj = D_FF // GU_TILE

    def body(f_ref, wg_ref, wu_ref, gu_ref, a_ref, at_ref):
        g = _mx_nt(f_ref[...], wg_ref[...])
        u = _mx_nt(f_ref[...], wu_ref[...])
        sg = _sigmoid(g)
        gs = g * sg
        gu_ref[:, :GU_TILE] = (u * (sg + gs - gs * sg)).astype(gu_ref.dtype)
        gu_ref[:, GU_TILE:] = gs.astype(gu_ref.dtype)
        act = gs * u
        a_ref[...] = act.astype(a_ref.dtype)
        at_ref[...] = act.T.astype(at_ref.dtype)

    return pl.pallas_call(
        body, name=name, grid=(nj, t // tm),
        in_specs=[pl.BlockSpec((tm, d), lambda j, i: (i, 0)), pl.BlockSpec((GU_TILE, d), lambda j, i: (j, 0)),
                  pl.BlockSpec((GU_TILE, d), lambda j, i: (nj + j, 0))],
        out_specs=(pl.BlockSpec((tm, 2 * GU_TILE), lambda j, i: (i, j)), pl.BlockSpec((tm, GU_TILE), lambda j, i: (i, j)),
                   pl.BlockSpec((GU_TILE, tm), lambda j, i: (j, i))),
        out_shape=(jax.ShapeDtypeStruct((t, 2 * D_FF), MXU_DTYPE), jax.ShapeDtypeStruct((t, D_FF), MXU_DTYPE),
                   jax.ShapeDtypeStruct((D_FF, t), MXU_DTYPE)),
        compiler_params=_cp("parallel", "parallel"))(f, wt, wt)


_GU_CHUNKS = tuple((q * GU_TILE, ((q % 2) * (D_FF // GU_TILE) + q // 2) * GU_TILE, GU_TILE)
                   for q in range(2 * D_FF // GU_TILE))


def ffn_dact(dy, w_d, gu, name, tm=1024, after=None):
    t, d = dy.shape
    tm = min(tm, t)

    def body(dy_ref, w_ref, gu_ref, *rest):
        o_ref = rest[-1]
        da = _mx_nt(dy_ref[...], w_ref[...])
        o_ref[:, :GU_TILE] = (da * gu_ref[:, :GU_TILE]).astype(o_ref.dtype)
        o_ref[:, GU_TILE:] = (da * gu_ref[:, GU_TILE:]).astype(o_ref.dtype)

    in_specs = [pl.BlockSpec((tm, d), lambda j, i: (i, 0)), pl.BlockSpec((GU_TILE, d), lambda j, i: (j, 0)),
                pl.BlockSpec((tm, 2 * GU_TILE), lambda j, i: (i, j))]
    args = [dy, w_d, gu]
    if after is not None:
        in_specs.append(pl.BlockSpec(memory_space=pl.ANY))
        args.append(after)
    return pl.pallas_call(
        body, name=name, grid=(D_FF // GU_TILE, t // tm), in_specs=in_specs,
        out_specs=pl.BlockSpec((tm, 2 * GU_TILE), lambda j, i: (i, j)),
        out_shape=jax.ShapeDtypeStruct((t, 2 * D_FF), MXU_DTYPE), compiler_params=_cp("parallel", "parallel"))(*args)


def mm_nn_res_loss(a, b, res, target, name, tm=512):
    t, k = a.shape
    d = b.shape[1]
    tm = min(tm, t)

    def body(a_ref, b_ref, res_ref, t_ref, dy_ref, l_ref):
        e = res_ref[...] + _mx(a_ref[...], b_ref[...]) - t_ref[...]
        dy_ref[...] = e * (1.0 / d)
        part = jnp.zeros((1, 128), F32) + 0.5 * jnp.sum(jnp.mean(e * e, axis=-1, keepdims=True), axis=0, keepdims=True)
        l_ref[...] = jnp.where(pl.program_id(0) == 0, part, l_ref[...] + part)

    row = pl.BlockSpec((tm, d), lambda i: (i, 0))
    return pl.pallas_call(
        body, name=name, grid=(t // tm,),
        in_specs=[pl.BlockSpec((tm, k), lambda i: (i, 0)), pl.BlockSpec((k, d), lambda i: (0, 0)), row, row],
        out_specs=(row, pl.BlockSpec((1, 128), lambda i: (0, 0))),
        out_shape=(jax.ShapeDtypeStruct((t, d), F32), jax.ShapeDtypeStruct((1, 128), F32)),
        compiler_params=_cp("arbitrary"))(a, b, res, target)


QK_W = Q_W + KV_W
_QK_TILE = 256


def _qk_mats():
    idx = np.arange(_QK_TILE)
    half = HEAD_DIM // 2
    same = (idx[:, None] // HEAD_DIM) == (idx[None, :] // HEAD_DIM)
    lo = (idx % HEAD_DIM) < half
    rot = np.where((idx[:, None] == idx[None, :] + half) & lo[None, :], -1.0, 0.0)
    rot = rot + np.where((idx[:, None] == idx[None, :] - half) & ~lo[None, :], 1.0, 0.0)
    return jnp.asarray(same, F32), jnp.asarray(rot, F32)


QK_TM = 256


def _qk_gains(q_gain, k_gain):
    return jnp.concatenate([q_gain] * ATTN_HEADS + [k_gain] * KV_HEADS, axis=-1)


def _rope_tile(ca_ref, sa_ref, cb_ref, sb_ref):
    ca, sa, cb, sb = ca_ref[0], sa_ref[0], cb_ref[...], sb_ref[...]
    c, s = ca * cb - sa * sb, sa * cb + ca * sb
    rep = QK_W // 128
    return jnp.concatenate([c] * rep, axis=-1), jnp.concatenate([s] * rep, axis=-1)


_ROPE_SPECS = [pl.BlockSpec((1, 1, 128), lambda i: (i, 0, 0)), pl.BlockSpec((1, 1, 128), lambda i: (i, 0, 0)),
               pl.BlockSpec((QK_TM, 128), lambda i: (0, 0)), pl.BlockSpec((QK_TM, 128), lambda i: (0, 0))]


def _qk_tiles(a, mat, transposed=False):
    outs = []
    for c0 in range(0, QK_W, _QK_TILE):
        w = min(_QK_TILE, QK_W - c0)
        mt = (mat.T if transposed else mat)[:w, :w].astype(MXU_DTYPE)
        at = a[:, c0:c0 + w]
        hi = at.astype(MXU_DTYPE)
        lo = (at - hi.astype(F32)).astype(MXU_DTYPE)
        outs.append(jnp.dot(hi, mt, preferred_element_type=F32) + jnp.dot(lo, mt, preferred_element_type=F32))
    return jnp.concatenate(outs, axis=-1)


def qk_prep_fwd(proj, q_gain, k_gain, rope, name):
    t = proj.shape[0]
    tm = QK_TM
    gmat, rmat = _qk_mats()
    gain = _qk_gains(q_gain, k_gain)

    def body(p_ref, g_ref, ca_ref, sa_ref, cb_ref, sb_ref, gm_ref, rm_ref, q_ref, k_ref):
        x = p_ref[...]
        r = lax.rsqrt(_qk_tiles(x * x, gm_ref[...]) * (1.0 / HEAD_DIM) + EPS)
        xn = x * r * g_ref[...]
        c, s = _rope_tile(ca_ref, sa_ref, cb_ref, sb_ref)
        out = xn * c + _qk_tiles(xn, rm_ref[...]) * s
        q_ref[...] = out[:, :Q_W]
        k_ref[...] = out[:, Q_W:]

    full = pl.BlockSpec((_QK_TILE, _QK_TILE), lambda i: (0, 0))
    return pl.pallas_call(
        body, name=name, grid=(t // tm,),
        in_specs=[pl.BlockSpec((tm, QK_W), lambda i: (i, 0)), pl.BlockSpec((1, QK_W), lambda i: (0, 0))] + _ROPE_SPECS
        + [full, full],
        out_specs=(pl.BlockSpec((tm, Q_W), lambda i: (i, 0)), pl.BlockSpec((tm, KV_W), lambda i: (i, 0))),
        out_shape=(jax.ShapeDtypeStruct((t, Q_W), F32), jax.ShapeDtypeStruct((t, KV_W), F32)),
        compiler_params=_cp("parallel"))(proj, gain, *rope, gmat, rmat)


def qk_prep_bwd(proj, q_gain, k_gain, rope, dq, dk, into, name):
    t = proj.shape[0]
    tm = QK_TM
    gmat, rmat = _qk_mats()
    gain = _qk_gains(q_gain, k_gain)
    lanes = np.arange(QK_W)[:, None]
    fold = jnp.asarray(lanes % HEAD_DIM + np.where(lanes >= Q_W, HEAD_DIM, 0) == np.arange(128)[None, :], F32)

    def body(p_ref, g_ref, ca_ref, sa_ref, cb_ref, sb_ref, gm_ref, rm_ref, f_ref, dq_ref, dk_ref, into_ref,
             o_ref, dg_ref):
        x = p_ref[...]
        r = lax.rsqrt(_qk_tiles(x * x, gm_ref[...]) * (1.0 / HEAD_DIM) + EPS)
        xh = x * r
        c, s = _rope_tile(ca_ref, sa_ref, cb_ref, sb_ref)
        dout = jnp.concatenate([dq_ref[...], dk_ref[...]], axis=-1)
        dxn = dout * c + _qk_tiles(dout * s, rm_ref[...], transposed=True)
        part = _hi(jnp.sum(dxn * xh, axis=0, keepdims=True), f_ref[...])
        dxh = dxn * g_ref[...]
        mean = _qk_tiles(dxh * xh, gm_ref[...]) * (1.0 / HEAD_DIM)
        o_ref[...] = (r * (dxh - xh * mean)).astype(o_ref.dtype)
        dg_ref[...] = jnp.where(pl.program_id(0) == 0, part, dg_ref[...] + part)

    full = pl.BlockSpec((_QK_TILE, _QK_TILE), lambda i: (0, 0))
    dqk, dg = pl.pallas_call(
        body, name=name, grid=(t // tm,),
        in_specs=[pl.BlockSpec((tm, QK_W), lambda i: (i, 0)), pl.BlockSpec((1, QK_W), lambda i: (0, 0))] + _ROPE_SPECS
        + [full, full, pl.BlockSpec((QK_W, 128), lambda i: (0, 0)),
                  pl.BlockSpec((tm, Q_W), lambda i: (i, 0)), pl.BlockSpec((tm, KV_W), lambda i: (i, 0)),
                  pl.BlockSpec(memory_space=pl.ANY)],
        out_specs=(pl.BlockSpec((tm, QK_W), lambda i: (i, 0)), pl.BlockSpec((1, 128), lambda i: (0, 0))),
        out_shape=(jax.ShapeDtypeStruct(into.shape, into.dtype), jax.ShapeDtypeStruct((1, 128), F32)),
        input_output_aliases={len(rope) + 7: 0},
        compiler_params=_cp("arbitrary"))(proj, gain, *rope, gmat, rmat, fold, dq, dk, into)
    return dqk, dg[:, :HEAD_DIM], dg[:, HEAD_DIM:]


def _swa_valid(n, grp):
    qi = lax.broadcasted_iota(jnp.int32, (grp * ATTN_BLOCK, 2 * ATTN_BLOCK), 0) & (ATTN_BLOCK - 1)
    kj = lax.broadcasted_iota(jnp.int32, (grp * ATTN_BLOCK, 2 * ATTN_BLOCK), 1)
    diff = qi + ATTN_BLOCK - kj
    return (diff >= 0) & (diff < ATTN_BLOCK) & (n * ATTN_BLOCK - ATTN_BLOCK + kj >= 0)


def _stack_heads(ref, g, grp, rows=slice(None)):
    return jnp.concatenate([ref[rows, (g * grp + j) * HEAD_DIM:(g * grp + j + 1) * HEAD_DIM] for j in range(grp)], axis=0)


def _stack_sinks(s_ref, g, grp):
    return jnp.concatenate([jnp.zeros((ATTN_BLOCK, 1), F32) + s_ref[0:1, g * grp + j:g * grp + j + 1]
                            for j in range(grp)], axis=0)


SWA_STEP = 2


def swa_fwd(q, k, proj, sinks, name):
    t = q.shape[0]
    nb = t // ATTN_BLOCK
    scale = HEAD_DIM ** -0.5
    grp = ATTN_HEADS // KV_HEADS

    rows = SWA_STEP * ATTN_BLOCK

    def body(q_ref, kc_ref, kp_ref, vc_ref, vp_ref, s_ref, y_ref, mix_ref, yt_ref, lse_ref):
        n0 = pl.program_id(0) * SWA_STEP
        kk = jnp.concatenate([kp_ref[...], kc_ref[...]], axis=0).astype(MXU_DTYPE)
        vv = jnp.concatenate([vp_ref[...], vc_ref[...]], axis=0).astype(MXU_DTYPE)
        lane = lax.broadcasted_iota(jnp.int32, (ATTN_BLOCK, ATTN_HEADS), 1)
        units = [(b, g) for b in range(SWA_STEP) for g in range(KV_HEADS)]
        blk = lambda b: slice(b * ATTN_BLOCK, (b + 1) * ATTN_BLOCK)
        keys = lambda b: slice(b * ATTN_BLOCK, (b + 2) * ATTN_BLOCK)
        col = lambda g: slice(g * HEAD_DIM, (g + 1) * HEAD_DIM)
        valid = [_swa_valid(n0 + b, grp) for b in range(SWA_STEP)]
        qg = [_stack_heads(q_ref, g, grp, blk(b)) for b, g in units]
        sink = [_stack_sinks(s_ref, g, grp) for b, g in units]
        sc = [jnp.where(valid[b], _mx_nt(qg[u], kk[keys(b), col(g)]) * scale, NEG) for u, (b, g) in enumerate(units)]
        m = [jnp.maximum(jnp.max(sc_, axis=-1, keepdims=True), sk) for sc_, sk in zip(sc, sink)]
        e = [jnp.exp(sc_ - m_) for sc_, m_ in zip(sc, m)]
        den = [jnp.sum(e_, axis=-1, keepdims=True) + jnp.exp(sk - m_) for e_, sk, m_ in zip(e, sink, m)]
        og = [_mx(e[u] / den[u], vv[keys(b), col(g)]) for u, (b, g) in enumerate(units)]
        lg = [m_ + jnp.log(d_) for m_, d_ in zip(m, den)]
        for b in range(SWA_STEP):
            lse = jnp.zeros((ATTN_BLOCK, ATTN_HEADS), F32)
            outs = []
            for h in range(ATTN_HEADS):
                u = b * KV_HEADS + h // grp
                sub = blk(h % grp)
                outs.append(og[u][sub])
                lse = jnp.where(lane == h, lg[u][sub], lse)
            y = jnp.concatenate(outs, axis=-1)
            y_ref[blk(b), :] = y
            mix_ref[blk(b), :] = y.astype(mix_ref.dtype)
            yt_ref[:, blk(b)] = y.T.astype(yt_ref.dtype)
            lse_ref[blk(b), :] = lse

    cur = lambda n: (n, 0)
    prev = lambda n: (jnp.maximum(n * SWA_STEP - 1, 0), 0)
    vcol = (Q_W + KV_W) // KV_W
    return pl.pallas_call(
        body, name=name, grid=(nb // SWA_STEP,),
        in_specs=[pl.BlockSpec((rows, Q_W), cur), pl.BlockSpec((rows, KV_W), cur),
                  pl.BlockSpec((ATTN_BLOCK, KV_W), prev),
                  pl.BlockSpec((rows, KV_W), lambda n: (n, vcol)),
                  pl.BlockSpec((ATTN_BLOCK, KV_W), lambda n: (jnp.maximum(n * SWA_STEP - 1, 0), vcol)),
                  pl.BlockSpec((1, ATTN_HEADS), lambda n: (0, 0))],
        out_specs=(pl.BlockSpec((rows, Q_W), cur), pl.BlockSpec((rows, Q_W), cur),
                   pl.BlockSpec((Q_W, rows), lambda n: (0, n)), pl.BlockSpec((rows, ATTN_HEADS), cur)),
        out_shape=(jax.ShapeDtypeStruct((t, Q_W), F32), jax.ShapeDtypeStruct((t, Q_W + CONV_CH), MXU_DTYPE),
                   jax.ShapeDtypeStruct((Q_W + CONV_CH, t), MXU_DTYPE), jax.ShapeDtypeStruct((t, ATTN_HEADS), F32)),
        compiler_params=_cp("parallel"))(q, k, k, proj, proj, sinks)


def swa_bwd(q, k, proj, sinks, y, lse, dmix, into, name):
    t = q.shape[0]
    nb = t // ATTN_BLOCK
    scale = HEAD_DIM ** -0.5
    grp = ATTN_HEADS // KV_HEADS

    def body(q_ref, kc_ref, kp_ref, vc_ref, vp_ref, s_ref, y_ref, lse_ref, dy_ref, into_ref,
             dq_ref, dk_ref, dv_ref, ds_ref, dkc, dvc):
        n = pl.program_id(0)

        @pl.when(n == 0)
        def _():
            dkc[...] = jnp.zeros_like(dkc)
            dvc[...] = jnp.zeros_like(dvc)
            ds_ref[...] = jnp.zeros_like(ds_ref)

        @pl.when(n < nb)
        def _():
            valid = _swa_valid(n, grp)
            kk = jnp.concatenate([kp_ref[...], kc_ref[...]], axis=0).astype(MXU_DTYPE)
            vv = jnp.concatenate([vp_ref[...], vc_ref[...]], axis=0).astype(MXU_DTYPE)
            lane = lax.broadcasted_iota(jnp.int32, (1, ATTN_HEADS), 1)
            gs = range(KV_HEADS)
            kg = [kk[:, g * HEAD_DIM:(g + 1) * HEAD_DIM] for g in gs]
            vg = [vv[:, g * HEAD_DIM:(g + 1) * HEAD_DIM] for g in gs]
            qg = [_stack_heads(q_ref, g, grp).astype(MXU_DTYPE) for g in gs]
            dog = [_stack_heads(dy_ref, g, grp) for g in gs]
            og = [_stack_heads(y_ref, g, grp) for g in gs]
            lg = [jnp.concatenate([lse_ref[:, g * grp + j:g * grp + j + 1] for j in range(grp)], axis=0) for g in gs]
            sink = [_stack_sinks(s_ref, g, grp) for g in gs]
            sc = [jnp.where(valid, _mx_nt(qg[g], kg[g]) * scale, NEG) for g in gs]
            p = [jnp.exp(sc[g] - lg[g]) for g in gs]
            delta = [jnp.sum(dog[g] * og[g], axis=-1, keepdims=True) for g in gs]
            ds = [p[g] * (_mx_nt(dog[g], vg[g]) - delta[g]) for g in gs]
            dqg = [_mx(ds[g], kg[g]) * scale for g in gs]
            dkf = jnp.concatenate([_mx_tn(ds[g], qg[g]) * scale for g in gs], axis=-1)
            dvf = jnp.concatenate([_mx_tn(p[g], dog[g]) for g in gs], axis=-1)
            dsk = [jnp.exp(sink[g] - lg[g]) * delta[g] for g in gs]
            dsink = jnp.zeros((1, ATTN_HEADS), F32)
            dqs = []
            for h in range(ATTN_HEADS):
                rows = slice((h % grp) * ATTN_BLOCK, (h % grp + 1) * ATTN_BLOCK)
                dqs.append(dqg[h // grp][rows])
                dsink = jnp.where(lane == h, -jnp.sum(dsk[h // grp][rows], axis=0, keepdims=True), dsink)
            dq_ref[...] = jnp.concatenate(dqs, axis=-1)
            dk_ref[...] = dkc[...] + dkf[:ATTN_BLOCK]
            dv_ref[...] = (dvc[...] + dvf[:ATTN_BLOCK]).astype(dv_ref.dtype)
            dkc[...] = dkf[ATTN_BLOCK:]
            dvc[...] = dvf[ATTN_BLOCK:]
            ds_ref[...] += dsink

        @pl.when(n == nb)
        def _():
            dk_ref[...] = dkc[...]
            dv_ref[...] = dvc[...].astype(dv_ref.dtype)

    cur = lambda n: (jnp.minimum(n, nb - 1), 0)
    prev = lambda n: (jnp.clip(n - 1, 0, nb - 1), 0)
    vcol = (Q_W + KV_W) // KV_W
    return pl.pallas_call(
        body, name=name, grid=(nb + 1,),
        in_specs=[pl.BlockSpec((ATTN_BLOCK, Q_W), cur), pl.BlockSpec((ATTN_BLOCK, KV_W), cur),
                  pl.BlockSpec((ATTN_BLOCK, KV_W), prev),
                  pl.BlockSpec((ATTN_BLOCK, KV_W), lambda n: (jnp.minimum(n, nb - 1), vcol)),
                  pl.BlockSpec((ATTN_BLOCK, KV_W), lambda n: (jnp.clip(n - 1, 0, nb - 1), vcol)),
                  pl.BlockSpec((1, ATTN_HEADS), lambda n: (0, 0)),
                  pl.BlockSpec((ATTN_BLOCK, Q_W), cur), pl.BlockSpec((ATTN_BLOCK, ATTN_HEADS), cur),
                  pl.BlockSpec((ATTN_BLOCK, Q_W), cur), pl.BlockSpec(memory_space=pl.ANY)],
        out_specs=(pl.BlockSpec((ATTN_BLOCK, Q_W), cur), pl.BlockSpec((ATTN_BLOCK, KV_W), prev),
                   pl.BlockSpec((ATTN_BLOCK, KV_W), lambda n: (jnp.clip(n - 1, 0, nb - 1), vcol)),
                   pl.BlockSpec((1, ATTN_HEADS), lambda n: (0, 0))),
        out_shape=(jax.ShapeDtypeStruct((t, Q_W), F32), jax.ShapeDtypeStruct((t, KV_W), F32),
                   jax.ShapeDtypeStruct(into.shape, into.dtype), jax.ShapeDtypeStruct((1, ATTN_HEADS), F32)),
        scratch_shapes=[pltpu.VMEM((ATTN_BLOCK, KV_W), F32), pltpu.VMEM((ATTN_BLOCK, KV_W), F32)],
        input_output_aliases={9: 2},
        compiler_params=_cp("arbitrary"))(q, k, k, proj, proj, sinks, y, lse, dmix, into)


GC_W = 256
_GB0, _GC0, _XI0 = 768 // GC_W, 1280 // GC_W, 1792 // GC_W
HALO = 8


def gconv_fwd(proj, conv_w, mix, mix_t, name, tm=512):
    t = proj.shape[0]
    hb = tm // HALO
    half = Q_W // GC_W

    def body(gb_ref, gc_ref, xi_ref, gch_ref, xih_ref, w_ref, mix_in, mixt_in, y_ref, yt_ref):
        i = pl.program_id(1)
        u = gc_ref[...] * xi_ref[...]
        uh = jnp.where(i == 0, 0.0, gch_ref[...] * xih_ref[...])
        up = jnp.concatenate([uh, u], axis=0)
        cv = w_ref[0:1, :] * up[HALO - 2:HALO - 2 + tm]
        cv = cv + w_ref[1:2, :] * up[HALO - 1:HALO - 1 + tm]
        cv = cv + w_ref[2:3, :] * u
        y = gb_ref[...] * cv
        y_ref[...] = y.astype(y_ref.dtype)
        yt_ref[...] = y.T.astype(yt_ref.dtype)

    def col(c0):
        return pl.BlockSpec((tm, GC_W), lambda cj, i: (i, c0 + cj))

    def halo(c0):
        return pl.BlockSpec((HALO, GC_W), lambda cj, i: (jnp.maximum(i * hb - 1, 0), c0 + cj))

    return pl.pallas_call(
        body, name=name, grid=(CONV_CH // GC_W, t // tm),
        in_specs=[col(_GB0), col(_GC0), col(_XI0), halo(_GC0), halo(_XI0),
                  pl.BlockSpec((3, GC_W), lambda cj, i: (0, cj)),
                  pl.BlockSpec(memory_space=pl.ANY), pl.BlockSpec(memory_space=pl.ANY)],
        out_specs=(pl.BlockSpec((tm, GC_W), lambda cj, i: (i, half + cj)),
                   pl.BlockSpec((GC_W, tm), lambda cj, i: (half + cj, i))),
        out_shape=(jax.ShapeDtypeStruct(mix.shape, mix.dtype), jax.ShapeDtypeStruct(mix_t.shape, mix_t.dtype)),
        input_output_aliases={6: 0, 7: 1},
        compiler_params=_cp("parallel", "parallel"))(proj, proj, proj, proj, proj, conv_w, mix, mix_t)


def gconv_bwd(proj, conv_w, dmix, into, name, tm=512):
    t = proj.shape[0]
    hb = tm // HALO
    nt = t // tm
    dy0 = Q_W // GC_W

    def body(gb_ref, gc_ref, xi_ref, gch_ref, xih_ref, gbn_ref, dyn_ref, dy_ref, w_ref, into_ref, o_ref, dw_ref):
        dgb_ref, dgc_ref, dxi_ref = (o_ref.at[:, j * GC_W:(j + 1) * GC_W] for j in range(3))
        i = pl.program_id(1)
        gc, xi, gb, dy = gc_ref[...], xi_ref[...], gb_ref[...], dy_ref[...]
        u = gc * xi
        uh = jnp.where(i == 0, 0.0, gch_ref[...] * xih_ref[...])
        up = jnp.concatenate([uh, u], axis=0)
        u2 = up[HALO - 2:HALO - 2 + tm]
        u1 = up[HALO - 1:HALO - 1 + tm]
        cv = w_ref[0:1, :] * u2 + w_ref[1:2, :] * u1 + w_ref[2:3, :] * u
        dgb_ref[...] = (dy * cv).astype(dgb_ref.dtype)
        dcv = dy * gb
        dcvn = jnp.where(i == nt - 1, 0.0, dyn_ref[...] * gbn_ref[...])
        dcvp = jnp.concatenate([dcv, dcvn], axis=0)
        du = w_ref[0:1, :] * dcvp[2:2 + tm] + w_ref[1:2, :] * dcvp[1:1 + tm] + w_ref[2:3, :] * dcv
        dgc_ref[...] = (du * xi).astype(dgc_ref.dtype)
        dxi_ref[...] = (du * gc).astype(dxi_ref.dtype)
        dw = jnp.concatenate([jnp.sum(dcv * u2, axis=0, keepdims=True), jnp.sum(dcv * u1, axis=0, keepdims=True),
                              jnp.sum(dcv * u, axis=0, keepdims=True)], axis=0)

        @pl.when(i == 0)
        def _():
            dw_ref[...] = dw

        @pl.when(i > 0)
        def _():
            dw_ref[...] += dw

    def col(c0):
        return pl.BlockSpec((tm, GC_W), lambda cj, i: (i, c0 + cj))

    def halo(c0):
        return pl.BlockSpec((HALO, GC_W), lambda cj, i: (jnp.maximum(i * hb - 1, 0), c0 + cj))

    def nxt(c0):
        return pl.BlockSpec((HALO, GC_W), lambda cj, i: (jnp.minimum((i + 1) * hb, t // HALO - 1), c0 + cj))

    return pl.pallas_call(
        body, name=name, grid=(CONV_CH // GC_W, nt),
        in_specs=[col(_GB0), col(_GC0), col(_XI0), halo(_GC0), halo(_XI0), nxt(_GB0), nxt(dy0), col(dy0),
                  pl.BlockSpec((3, GC_W), lambda cj, i: (0, cj)), pl.BlockSpec(memory_space=pl.ANY)],
        out_specs=(pl.BlockSpec((tm, 3 * GC_W), lambda cj, i: (i, 1 + cj)),
                   pl.BlockSpec((3, GC_W), lambda cj, i: (0, cj))),
        out_shape=(jax.ShapeDtypeStruct(into.shape, into.dtype), jax.ShapeDtypeStruct((3, CONV_CH), F32)),
        input_output_aliases={9: 0},
        compiler_params=_cp("parallel", "arbitrary"))(proj, proj, proj, proj, proj, proj, dmix, dmix, conv_w, into)


_EVEN_D_CHUNKS = ((0, 0, 768),) + tuple(
    (768 + (3 * j + part) * GC_W, 768 + part * CONV_CH + j * GC_W, GC_W)
    for j in range(CONV_CH // GC_W) for part in range(3))


_QKV_W = 3 * DN_W
_BA_COL = (4 * DN_W) // 128
_Z_COL = _QKV_W // DN_W


def gdn_prep_fwd(proj, conv_w, alog_row, dtb_row, name, tm=256):
    t = proj.shape[0]
    hb = tm // HALO
    qscale = DN_DIM ** -0.5

    def body(x_ref, xh_ref, w_ref, ba_ref, al_ref, dt_ref, q_ref, k_ref, v_ref, bg_ref, c_ref):
        i = pl.program_id(0)
        for gi in range(3 * DN_HEADS):
            sl = slice(gi * DN_DIM, (gi + 1) * DN_DIM)
            xp = jnp.concatenate([jnp.where(i == 0, 0.0, xh_ref[:, sl]), x_ref[:, sl]], axis=0)
            c = w_ref[0:1, sl] * xp[HALO - 3:HALO - 3 + tm]
            for j in range(1, 4):
                c = c + w_ref[j:j + 1, sl] * xp[HALO - 3 + j:HALO - 3 + j + tm]
            c_ref[:, sl] = c
            s = c * _sigmoid(c)
            osl = slice((gi % DN_HEADS) * DN_DIM, (gi % DN_HEADS + 1) * DN_DIM)
            if gi < DN_HEADS:
                q_ref[:, osl] = s * lax.rsqrt(jnp.sum(s * s, axis=-1, keepdims=True) + EPS) * qscale
            elif gi < 2 * DN_HEADS:
                k_ref[:, osl] = s * lax.rsqrt(jnp.sum(s * s, axis=-1, keepdims=True) + EPS)
            else:
                v_ref[:, osl] = s
        ba = ba_ref[...]
        lane = lax.broadcasted_iota(jnp.int32, ba.shape, 1)
        gval = -jnp.exp(al_ref[...]) * _softplus(ba + dt_ref[...])
        bg_ref[...] = jnp.where(lane < DN_HEADS, _sigmoid(ba), jnp.where(lane < 2 * DN_HEADS, gval, 0.0))

    row = pl.BlockSpec((tm, DN_W), lambda i: (i, 0))
    one = pl.BlockSpec((1, 128), lambda i: (0, 0))
    return pl.pallas_call(
        body, name=name, grid=(t // tm,),
        in_specs=[pl.BlockSpec((tm, _QKV_W), lambda i: (i, 0)),
                  pl.BlockSpec((HALO, _QKV_W), lambda i: (jnp.maximum(i * hb - 1, 0), 0)),
                  pl.BlockSpec((4, _QKV_W), lambda i: (0, 0)),
                  pl.BlockSpec((tm, 128), lambda i: (i, _BA_COL)), one, one],
        out_specs=(row, row, row, pl.BlockSpec((tm, 128), lambda i: (i, 0)), pl.BlockSpec((tm, _QKV_W), lambda i: (i, 0))),
        out_shape=(jax.ShapeDtypeStruct((t, DN_W), F32),) * 3 + (jax.ShapeDtypeStruct((t, 128), F32),
                                                                 jax.ShapeDtypeStruct((t, _QKV_W), F32)),
        compiler_params=_cp("parallel"))(proj, proj, conv_w, proj, alog_row, dtb_row)


def gdn_prep_bwd(proj, conv, conv_w, alog_row, dtb_row, dq, dk, dv, dbg, dz, name, tm=256):
    t = proj.shape[0]
    hb = tm // HALO
    nt = t // tm
    qscale = DN_DIM ** -0.5
    te = tm + HALO

    def body(x_ref, c_ref, cn_ref, w_ref, ba_ref, al_ref, dt_ref, dq_ref, dk_ref, dv_ref,
             dqn_ref, dkn_ref, dvn_ref, dbg_ref, dz_ref, dx_ref, dw_ref, ddt_ref, dal_ref):
        i = pl.program_id(0)
        first = i == 0
        last = i == nt - 1
        dws = []
        for gi in range(3 * DN_HEADS):
            sl = slice(gi * DN_DIM, (gi + 1) * DN_DIM)
            osl = slice((gi % DN_HEADS) * DN_DIM, (gi % DN_HEADS + 1) * DN_DIM)
            c = jnp.concatenate([c_ref[:, sl], cn_ref[:, sl]], axis=0)
            sg = _sigmoid(c)
            s = c * sg
            d_ref, dn_ref = ((dq_ref, dqn_ref), (dk_ref, dkn_ref), (dv_ref, dvn_ref))[gi // DN_HEADS]
            dy = jnp.concatenate([d_ref[:, osl], jnp.where(last, 0.0, dn_ref[:, osl])], axis=0)
            if gi < 2 * DN_HEADS:
                r = lax.rsqrt(jnp.sum(s * s, axis=-1, keepdims=True) + EPS)
                sh = s * r
                ds = r * (dy - sh * jnp.sum(sh * dy, axis=-1, keepdims=True))
                if gi < DN_HEADS:
                    ds = ds * qscale
            else:
                ds = dy
            dc = ds * sg * (1.0 + c * (1.0 - sg))
            dcs = [dc[3 - j:3 - j + tm] for j in range(4)]
            dx = w_ref[0:1, sl] * dcs[0]
            for j in range(1, 4):
                dx = dx + w_ref[j:j + 1, sl] * dcs[j]
            dx_ref[:, sl] = dx.astype(dx_ref.dtype)
            x0 = x_ref[:, sl]
            dws.append(jnp.concatenate([jnp.sum(dcs[j] * x0, axis=0, keepdims=True) for j in range(4)], axis=0))
        dw = jnp.concatenate(dws, axis=-1)
        ba = ba_ref[...]
        dbgv = dbg_ref[...]
        lane = lax.broadcasted_iota(jnp.int32, ba.shape, 1)
        beta = _sigmoid(ba)
        ea = -jnp.exp(al_ref[...])
        zin = ba + dt_ref[...]
        is_b = lane < DN_HEADS
        is_a = (lane >= DN_HEADS) & (lane < 2 * DN_HEADS)
        da = jnp.where(is_a, dbgv * ea * _sigmoid(zin), 0.0)
        dx_ref[:, _QKV_W:_QKV_W + DN_W] = dz_ref[...]
        dx_ref[:, _QKV_W + DN_W:] = jnp.where(is_b, dbgv * beta * (1.0 - beta), da).astype(dx_ref.dtype)
        ddt = jnp.sum(da, axis=0, keepdims=True)
        dal = jnp.sum(jnp.where(is_a, dbgv * ea * _softplus(zin), 0.0), axis=0, keepdims=True)

        @pl.when(first)
        def _():
            dw_ref[...] = dw
            ddt_ref[...] = ddt
            dal_ref[...] = dal

        @pl.when(i > 0)
        def _():
            dw_ref[...] += dw
            ddt_ref[...] += ddt
            dal_ref[...] += dal

    row = pl.BlockSpec((tm, DN_W), lambda i: (i, 0))
    nrow = pl.BlockSpec((HALO, DN_W), lambda i: (jnp.minimum((i + 1) * hb, t // HALO - 1), 0))
    one = pl.BlockSpec((1, 128), lambda i: (0, 0))
    return pl.pallas_call(
        body, name=name, grid=(nt,),
        in_specs=[pl.BlockSpec((tm, _QKV_W), lambda i: (i, 0)), pl.BlockSpec((tm, _QKV_W), lambda i: (i, 0)),
                  pl.BlockSpec((HALO, _QKV_W), lambda i: (jnp.minimum((i + 1) * hb, t // HALO - 1), 0)),
                  pl.BlockSpec((4, _QKV_W), lambda i: (0, 0)),
                  pl.BlockSpec((tm, 128), lambda i: (i, _BA_COL)), one, one,
                  row, row, row, nrow, nrow, nrow, pl.BlockSpec((tm, 128), lambda i: (i, 0)), row],
        out_specs=(pl.BlockSpec((tm, ODD_IN_PAD), lambda i: (i, 0)), pl.BlockSpec((4, _QKV_W), lambda i: (0, 0)), one, one),
        out_shape=(jax.ShapeDtypeStruct((t, ODD_IN_PAD), MXU_DTYPE), jax.ShapeDtypeStruct((4, _QKV_W), F32),
                   jax.ShapeDtypeStruct((1, 128), F32), jax.ShapeDtypeStruct((1, 128), F32)),
        compiler_params=_cp("arbitrary"))(proj, conv, conv, conv_w, proj, alog_row, dtb_row, dq, dk, dv, dq, dk, dv, dbg,
                                          dz)


def _chunk_masks():
    r = lax.broadcasted_iota(jnp.int32, (DN_CHUNK, DN_CHUNK), 0)
    c = lax.broadcasted_iota(jnp.int32, (DN_CHUNK, DN_CHUNK), 1)
    return r >= c, r > c


INV_PACK = 2


def _inv_unit_lower_many(mats):
    n = DN_CHUNK
    wide = INV_PACK * n
    r = lax.broadcasted_iota(jnp.int32, (wide, wide), 0)
    c = lax.broadcasted_iota(jnp.int32, (wide, wide), 1)
    same = (r & -n) == (c & -n)
    eye = jnp.where((r[:n] == (c[:n] & (n - 1))), 1.0, 0.0)

    def blockdiag(row):
        return jnp.where(same, jnp.concatenate([row] * INV_PACK, axis=0), 0.0)

    packs = [jnp.concatenate(mats[g:g + INV_PACK], axis=-1) for g in range(0, len(mats), INV_PACK)]
    xs = [eye - a for a in packs]
    pws = [_hi(a, blockdiag(a)) for a in packs]
    for step in range(5):
        if step < 4:
            both = [_hi(jnp.concatenate([x, pw], axis=0), blockdiag(pw)) for x, pw in zip(xs, pws)]
            xs = [x + b[:n] for x, b in zip(xs, both)]
            pws = [b[n:] for b in both]
        else:
            xs = [x + _hi(x, blockdiag(pw)) for x, pw in zip(xs, pws)]
    return [x[:, j * n:(j + 1) * n] for x in xs for j in range(INV_PACK)]


def _chunk_common(q, k, beta, gc, gcr, lower):
    gam = jnp.exp(jnp.where(lower, gc - gcr, NEG))
    eg = jnp.exp(gc)
    gl = gc[DN_CHUNK - 1:DN_CHUNK, :]
    kdf = jnp.exp(gl - gc)
    kb = k * beta
    bmat = _mx_nt(kb, k)
    qmat = _mx_nt(q, k)
    return gam, eg, jnp.exp(gl), kdf, kb, bmat, qmat


DN_STEP = 4


def gdn_fwd(q, k, v, bg, proj, o_gain, name):
    t = q.shape[0]
    n_chunks = t // DN_CHUNK

    def body(q_ref, k_ref, v_ref, bg_ref, z_ref, g_ref, o_ref, sall_ref, tall_ref, y_ref, yt_ref, s_ref):
        n = pl.program_id(0)

        @pl.when(n == 0)
        def _():
            s_ref[...] = jnp.zeros_like(s_ref)

        lower, strict = _chunk_masks()
        ltri = jnp.where(lower, 1.0, 0.0)
        hs = range(DN_HEADS)
        sl = [slice(h * DN_DIM, (h + 1) * DN_DIM) for h in hs]
        units = [(c, h) for c in range(DN_STEP) for h in hs]
        nu = range(len(units))
        rs = [slice(c * DN_CHUNK, (c + 1) * DN_CHUNK) for c in range(DN_STEP)]
        bgv = [bg_ref[rs[c], :] for c in range(DN_STEP)]
        gcs = [_hi(ltri, b) for b in bgv]
        gcs_t = [g.T for g in gcs]
        qh = [q_ref[rs[c], sl[h]] for c, h in units]
        kh = [k_ref[rs[c], sl[h]] for c, h in units]
        vh = [v_ref[rs[c], sl[h]] for c, h in units]
        beta = [bgv[c][:, h:h + 1] for c, h in units]
        com = [_chunk_common(qh[u], kh[u], beta[u], gcs[c][:, DN_HEADS + h:DN_HEADS + h + 1],
                             gcs_t[c][DN_HEADS + h:DN_HEADS + h + 1, :], lower) for u, (c, h) in enumerate(units)]
        gam, eg, dec, kdf, kb, bmat, qmat = zip(*com)
        tms = _inv_unit_lower_many([jnp.where(strict, bmat[u] * gam[u], 0.0) for u in nu])
        for u, (c, h) in enumerate(units):
            tall_ref[c, h] = tms[u]
        uw = [_hi(tms[u], jnp.concatenate([vh[u] * beta[u], kb[u] * eg[u]], axis=-1)) for u in nu]
        qd = [qh[u] * eg[u] for u in nu]
        pm = [qmat[u] * gam[u] for u in nu]
        kd = [kh[u] * kdf[u] for u in nu]
        st = [s_ref[h] for h in hs]
        for c in range(DN_STEP):
            us = [c * DN_HEADS + h for h in hs]
            for h in hs:
                sall_ref[c, h] = st[h]
            v_new = [uw[us[h]][:, :DN_DIM] - _mx(uw[us[h]][:, DN_DIM:], st[h]) for h in hs]
            o_st = [_mx(qd[us[h]], st[h]) for h in hs]
            o_in = [_mx(pm[us[h]], v_new[h]) for h in hs]
            s_up = [_mx_tn(kd[us[h]], v_new[h]) for h in hs]
            for h in hs:
                ov = o_st[h] + o_in[h]
                o_ref[rs[c], sl[h]] = ov
                zv = z_ref[rs[c], sl[h]]
                y = ov * lax.rsqrt(jnp.mean(ov * ov, axis=-1, keepdims=True) + EPS) * g_ref[...] * (zv * _sigmoid(zv))
                y_ref[rs[c], sl[h]] = y.astype(y_ref.dtype)
                yt_ref[sl[h], rs[c]] = y.T.astype(yt_ref.dtype)
            st = [st[h] * dec[us[h]] + s_up[h] for h in hs]
        for h in hs:
            s_ref[h] = st[h]

    rows = DN_STEP * DN_CHUNK
    row = pl.BlockSpec((rows, DN_W), lambda n: (n, 0))
    return pl.pallas_call(
        body, name=name, grid=(n_chunks // DN_STEP,),
        in_specs=[row, row, row, pl.BlockSpec((rows, 128), lambda n: (n, 0)),
                  pl.BlockSpec((rows, DN_W), lambda n: (n, _Z_COL)), pl.BlockSpec((1, DN_DIM), lambda n: (0, 0))],
        out_specs=(row, pl.BlockSpec((DN_STEP, DN_HEADS, DN_DIM, DN_DIM), lambda n: (n, 0, 0, 0)),
                   pl.BlockSpec((DN_STEP, DN_HEADS, DN_CHUNK, DN_CHUNK), lambda n: (n, 0, 0, 0)),
                   row, pl.BlockSpec((DN_W, rows), lambda n: (0, n))),
        out_shape=(jax.ShapeDtypeStruct((t, DN_W), F32),
                   jax.ShapeDtypeStruct((n_chunks, DN_HEADS, DN_DIM, DN_DIM), F32),
                   jax.ShapeDtypeStruct((n_chunks, DN_HEADS, DN_CHUNK, DN_CHUNK), F32),
                   jax.ShapeDtypeStruct((t, DN_W), MXU_DTYPE), jax.ShapeDtypeStruct((DN_W, t), MXU_DTYPE)),
        scratch_shapes=[pltpu.VMEM((DN_HEADS, DN_DIM, DN_DIM), F32)],
        compiler_params=_cp("arbitrary"))(q, k, v, bg, proj, o_gain)


def gdn_bwd(q, k, v, bg, sall, tall, do, name):
    t = q.shape[0]
    n_chunks = t // DN_CHUNK

    def body(q_ref, k_ref, v_ref, bg_ref, sall_ref, tall_ref, do_ref, dq_ref, dk_ref, dv_ref, dbg_ref, ds_ref):
        n = pl.program_id(0)

        @pl.when(n == 0)
        def _():
            ds_ref[...] = jnp.zeros_like(ds_ref)

        lower, strict = _chunk_masks()
        ltri = jnp.where(lower, 1.0, 0.0)
        bgv = bg_ref[...]
        gcs = _hi(ltri, bgv)
        gcs_t = gcs.T
        lane = lax.broadcasted_iota(jnp.int32, (DN_CHUNK, 128), 1)
        rowi = lax.broadcasted_iota(jnp.int32, (DN_CHUNK, 1), 0)
        hs = range(DN_HEADS)
        each = lambda fn, *ls: [fn(*a) for a in zip(*ls)]
        rsum = lambda a: jnp.sum(a, axis=-1, keepdims=True)
        sl = [slice(h * DN_DIM, (h + 1) * DN_DIM) for h in hs]
        st = [sall_ref[0, h] for h in hs]
        tms = [tall_ref[0, h] for h in hs]
        dsn = [ds_ref[h] for h in hs]
        qh = [q_ref[:, sl[h]] for h in hs]
        kh = [k_ref[:, sl[h]] for h in hs]
        vh = [v_ref[:, sl[h]] for h in hs]
        doh = [do_ref[:, sl[h]] for h in hs]
        beta = [bgv[:, h:h + 1] for h in hs]
        com = [_chunk_common(qh[h], kh[h], beta[h], gcs[:, DN_HEADS + h:DN_HEADS + h + 1],
                             gcs_t[DN_HEADS + h:DN_HEADS + h + 1, :], lower) for h in hs]
        gam, eg, dec, kdf, kb, bmat, qmat = zip(*com)
        rhs_w = each(lambda a, b: a * b, kb, eg)
        uw = each(lambda t_, v_, b_, r_: _hi(t_, jnp.concatenate([v_ * b_, r_], axis=-1)), tms, vh, beta, rhs_w)
        qd = each(lambda a, b: a * b, qh, eg)
        kd = each(lambda a, b: a * b, kh, kdf)
        pmat = each(lambda a, b: a * b, qmat, gam)
        v_new = each(lambda uw_, s_: uw_[:, :DN_DIM] - _mx(uw_[:, DN_DIM:], s_), uw, st)
        dqd = each(_mx_nt, doh, st)
        ds_o = each(_mx_tn, qd, doh)
        dp = each(lambda d_, v_: jnp.where(lower, _mx_nt(d_, v_), 0.0), doh, v_new)
        dvn_o = each(_mx_tn, pmat, doh)
        ddec = each(lambda d_, s_: jnp.sum(rsum(d_ * s_), axis=0, keepdims=True), dsn, st)
        dkd = each(_mx_nt, v_new, dsn)
        dvn = each(lambda a, k_, d_: a + _mx(k_, d_), dvn_o, kd, dsn)
        dw = each(lambda d_, s_: -_mx_nt(d_, s_), dvn, st)
        ds_w = each(lambda uw_, d_: _mx_tn(uw_[:, DN_DIM:], d_), uw, dvn)
        for h in hs:
            ds_ref[h] = ds_o[h] + dec[h] * dsn[h] - ds_w[h]
        dr = each(lambda t_, a, b: _hi_tn(t_, jnp.concatenate([a, b], axis=-1)), tms, dvn, dw)
        da = each(lambda r_, uw_: jnp.where(strict, -_hi_nt(r_, uw_), 0.0), dr, uw)
        dru = [r_[:, :DN_DIM] for r_ in dr]
        drw = [r_[:, DN_DIM:] for r_ in dr]
        db = each(lambda a, b: a * b, da, gam)
        dq_m = each(lambda a, b: a * b, dp, gam)
        e = each(lambda a, bm, p_, qm, g_: (a * bm + p_ * qm) * g_, da, bmat, dp, qmat, gam)
        dkb = each(lambda b_, k_, r_, e_: _mx(b_, k_) + r_ * e_, db, kh, drw, eg)
        dk = each(lambda b_, kb_, m_, q_, d_, f_: _mx_tn(b_, kb_) + _mx_tn(m_, q_) + d_ * f_, db, kb, dq_m, qh, dkd, kdf)
        dq = each(lambda m_, k_, d_, e_: _mx(m_, k_) + d_ * e_, dq_m, kh, dqd, eg)
        tk = each(lambda a, b: rsum(a * b), dkd, kd)
        dbeta_all = jnp.zeros((DN_CHUNK, 128), F32)
        dgc_all = jnp.zeros((DN_CHUNK, 128), F32)
        for h in hs:
            dgc = (jnp.sum(e[h], axis=1, keepdims=True) - jnp.sum(e[h].T, axis=1, keepdims=True)
                   + rsum(dqd[h] * qd[h]) - tk[h] + rsum(drw[h] * rhs_w[h]))
            dgl = jnp.sum(tk[h], axis=0, keepdims=True) + ddec[h] * dec[h]
            dgc = dgc + jnp.where(rowi == DN_CHUNK - 1, dgl, 0.0)
            dbeta = rsum(dru[h] * vh[h]) + rsum(dkb[h] * kh[h])
            dq_ref[:, sl[h]] = dq[h]
            dk_ref[:, sl[h]] = dk[h] + dkb[h] * beta[h]
            dv_ref[:, sl[h]] = dru[h] * beta[h]
            dbeta_all = jnp.where(lane == h, dbeta, dbeta_all)
            dgc_all = jnp.where(lane == DN_HEADS + h, dgc, dgc_all)
        dbg_ref[...] = dbeta_all + _hi_tn(ltri, dgc_all)

    rev = lambda n: (n_chunks - 1 - n, 0)
    row = pl.BlockSpec((DN_CHUNK, DN_W), rev)
    small = pl.BlockSpec((DN_CHUNK, 128), rev)
    return pl.pallas_call(
        body, name=name, grid=(n_chunks,),
        in_specs=[row, row, row, small,
                  pl.BlockSpec((1, DN_HEADS, DN_DIM, DN_DIM), lambda n: (n_chunks - 1 - n, 0, 0, 0)),
                  pl.BlockSpec((1, DN_HEADS, DN_CHUNK, DN_CHUNK), lambda n: (n_chunks - 1 - n, 0, 0, 0)), row],
        out_specs=(row, row, row, small),
        out_shape=(jax.ShapeDtypeStruct((t, DN_W), F32),) * 3 + (jax.ShapeDtypeStruct((t, 128), F32),),
        scratch_shapes=[pltpu.VMEM((DN_HEADS, DN_DIM, DN_DIM), F32)],
        compiler_params=_cp("arbitrary"))(q, k, v, bg, sall, tall, do)


def gdn_out_bwd(o, proj, o_gain, dx, w_out, name, tm=512, after=None):
    t = o.shape[0]
    tm = min(tm, t)

    def body(o_ref, z_ref, g_ref, dx_ref, w_ref, *rest):
        do_ref, dz_ref, dg_ref = rest[-3:]
        i = pl.program_id(0)
        dy = _mx_nt(dx_ref[...], w_ref[...])
        dg = jnp.zeros((1, DN_DIM), F32)
        for h in range(DN_HEADS):
            sl = slice(h * DN_DIM, (h + 1) * DN_DIM)
            ov, zv, dyv = o_ref[:, sl], z_ref[:, sl], dy[:, sl]
            r = lax.rsqrt(jnp.mean(ov * ov, axis=-1, keepdims=True) + EPS)
            oh = ov * r
            sg = _sigmoid(zv)
            dz_ref[:, sl] = (dyv * oh * g_ref[...] * sg * (1.0 + zv * (1.0 - sg))).astype(dz_ref.dtype)
            don = dyv * (zv * sg)
            dg = dg + jnp.sum(don * oh, axis=0, keepdims=True)
            doh = don * g_ref[...]
            do_ref[:, sl] = r * (doh - oh * jnp.mean(doh * oh, axis=-1, keepdims=True))

        @pl.when(i == 0)
        def _():
            dg_ref[...] = dg

        @pl.when(i > 0)
        def _():
            dg_ref[...] += dg

    row = pl.BlockSpec((tm, DN_W), lambda i: (i, 0))
    one = pl.BlockSpec((1, DN_DIM), lambda i: (0, 0))
    in_specs = [row, pl.BlockSpec((tm, DN_W), lambda i: (i, _Z_COL)), one,
                pl.BlockSpec((tm, dx.shape[1]), lambda i: (i, 0)), pl.BlockSpec(w_out.shape, lambda i: (0, 0))]
    args = [o, proj, o_gain, dx, w_out]
    if after is not None:
        in_specs.append(pl.BlockSpec(memory_space=pl.ANY))
        args.append(after)
    return pl.pallas_call(
        body, name=name, grid=(t // tm,), in_specs=in_specs, out_specs=(row, row, one),
        out_shape=(jax.ShapeDtypeStruct((t, DN_W), F32), jax.ShapeDtypeStruct((t, DN_W), MXU_DTYPE),
                   jax.ShapeDtypeStruct((1, DN_DIM), F32)),
        compiler_params=_cp("arbitrary"))(*args)


def _peer(k):
    x, y, c = lax.axis_index("x"), lax.axis_index("y"), lax.axis_index("c")
    px = 1 - x if k & 4 else x
    py = 1 - y if k & 2 else y
    pc = 1 - c if k & 1 else c
    return (px, py, pc), 4 * px + 2 * py + pc


_HBM = pl.BlockSpec(memory_space=pltpu.HBM)
_SEM = pl.BlockSpec(memory_space=pltpu.SEMAPHORE)
_DATAFLOW = pltpu.SideEffectType.DATAFLOW_SIDE_EFFECTING
N_PEER = N_DEV - 1


def join_blocks(parts, rows, name, tc=256):
    n, r, c = parts.shape

    def body(p_ref, o_ref):
        for j in range(n):
            o_ref[j * r:(j + 1) * r, :] = p_ref[j]
        if rows > n * r:
            o_ref[n * r:, :] = jnp.zeros((rows - n * r, tc), o_ref.dtype)

    return pl.pallas_call(
        body, name=name, grid=(c // tc,), in_specs=[pl.BlockSpec((n, r, tc), lambda i: (0, 0, i))],
        out_specs=pl.BlockSpec((rows, tc), lambda i: (0, i)), out_shape=jax.ShapeDtypeStruct((rows, c), parts.dtype),
        compiler_params=_cp("parallel"))(parts)


def split_blocks(whole, n, r, name, tc=256):
    rows, c = whole.shape

    def body(w_ref, o_ref):
        for j in range(n):
            o_ref[j] = w_ref[j * r:(j + 1) * r, :]

    return pl.pallas_call(
        body, name=name, grid=(c // tc,), in_specs=[pl.BlockSpec((rows, tc), lambda i: (0, i))],
        out_specs=pl.BlockSpec((n, r, tc), lambda i: (0, 0, i)), out_shape=jax.ShapeDtypeStruct((n, r, c), whole.dtype),
        compiler_params=_cp("parallel"))(whole)


_ALL_PEERS = tuple(range(1, N_DEV))
_OTHER_CHIPS = (2, 4, 6)
_SIBLING = 1


def send_start(srcs, name, scatter, after, masks=_ALL_PEERS):
    na, n_peer = len(srcs), len(masks)
    ns = (2 * n_peer + 1) * na
    lands = [lax.empty((N_DEV,) + (s.shape[1:] if scatter else s.shape), s.dtype) for s in srcs]
    extra = [] if after is None else [after]

    def body(*refs):
        src_refs, land_refs = refs[:na], refs[na:2 * na]
        sems = refs[2 * na + len(extra):2 * na + len(extra) + ns]
        land_out, token = refs[-1 - na:-1], refs[-1]
        _, me = _peer(0)
        for a in range(na):
            pltpu.make_async_copy(src_refs[a].at[me] if scatter else src_refs[a], land_out[a].at[me],
                                  sems[2 * n_peer * na + a]).start()
        peers = [_peer(k) for k in masks]
        for a in range(na):
            for k, (peer, pid) in enumerate(peers):
                pltpu.make_async_remote_copy(
                    src_ref=src_refs[a].at[pid] if scatter else src_refs[a], dst_ref=land_refs[a].at[me],
                    send_sem=sems[2 * (a * n_peer + k)], recv_sem=sems[2 * (a * n_peer + k) + 1],
                    device_id=peer, device_id_type=MESH).start()
        token[...] = jnp.zeros_like(token)

    hbm = lambda arrs: tuple(pltpu.HBM(a.shape, a.dtype) for a in arrs)
    outs = pl.pallas_call(
        body, name=name,
        out_shape=(pltpu.SemaphoreType.DMA(()),) * ns + hbm(srcs) + hbm(lands) + (jax.ShapeDtypeStruct((8, 128), F32),),
        in_specs=[_HBM] * (2 * na) + [pl.BlockSpec(memory_space=pl.ANY)] * len(extra),
        out_specs=(_SEM,) * ns + (_HBM,) * (2 * na) + (pl.BlockSpec(memory_space=pltpu.VMEM),),
        input_output_aliases={i: ns + i for i in range(2 * na)},
        compiler_params=pltpu.CompilerParams(has_side_effects=_DATAFLOW),
    )(*[pltpu.with_memory_space_constraint(a, pltpu.HBM) for a in list(srcs) + lands], *extra)
    return outs[:ns], outs[ns:ns + na], outs[ns + na:ns + 2 * na], outs[-1]


def send_wait(sems, srcs_thru, lands_thru, name, scatter, after, masks=_ALL_PEERS):
    na, n_peer = len(srcs_thru), len(masks)
    ns = (2 * n_peer + 1) * na

    def body(*refs):
        src_refs, land_refs, sm = refs[:na], refs[na:2 * na], refs[2 * na:2 * na + ns]
        _, me = _peer(0)
        for a in range(na):
            pltpu.make_async_copy(src_refs[a].at[me] if scatter else src_refs[a], land_refs[a].at[me],
                                  sm[2 * n_peer * na + a]).wait()
        for k, mask in enumerate(masks):
            peer, pid = _peer(mask)
            for a in range(na):
                cp = pltpu.make_async_remote_copy(
                    src_ref=src_refs[a].at[pid] if scatter else src_refs[a], dst_ref=land_refs[a].at[pid],
                    send_sem=sm[2 * (a * n_peer + k)], recv_sem=sm[2 * (a * n_peer + k) + 1],
                    device_id=peer, device_id_type=MESH)
                cp.wait_send()
                cp.wait_recv()

    hbm = lambda arrs: tuple(pltpu.HBM(a.shape, a.dtype) for a in arrs)
    outs = pl.pallas_call(
        body, name=name, out_shape=hbm(srcs_thru) + hbm(lands_thru),
        in_specs=[_HBM] * (2 * na) + [_SEM] * ns + [pl.BlockSpec(memory_space=pl.ANY)], out_specs=(_HBM,) * (2 * na),
        input_output_aliases={i: i for i in range(2 * na)},
        compiler_params=pltpu.CompilerParams(has_side_effects=_DATAFLOW),
    )(*srcs_thru, *lands_thru, *sems, after)
    return outs[na:]


def relay_blocks(lands, name, masks=_OTHER_CHIPS):
    na, n_slot = len(lands), 1 + len(masks)

    def body(*refs):
        land_refs, rs = refs[:na], refs[2 * na:]
        sibling, _ = _peer(_SIBLING)
        copies = [pltpu.make_async_remote_copy(
            src_ref=land_refs[a].at[_peer(mask)[1]], dst_ref=land_refs[a].at[_peer(mask)[1]],
            send_sem=rs[2 * (a * n_slot + k)], recv_sem=rs[2 * (a * n_slot + k) + 1],
            device_id=sibling, device_id_type=MESH) for a in range(na) for k, mask in enumerate((0,) + tuple(masks))]
        for cp in copies:
            cp.start()
        for cp in copies:
            cp.wait_send()
            cp.wait_recv()

    return pl.pallas_call(
        body, name=name, out_shape=tuple(pltpu.HBM(a.shape, a.dtype) for a in lands),
        in_specs=[_HBM] * na, out_specs=(_HBM,) * na, input_output_aliases={i: i for i in range(na)},
        scratch_shapes=[pltpu.SemaphoreType.DMA(())] * (2 * na * n_slot),
        compiler_params=pltpu.CompilerParams(has_side_effects=_DATAFLOW),
    )(*lands)


def _adamw(w, g, m, v):
    m = ADAM_B1 * m + (1.0 - ADAM_B1) * g
    v = ADAM_B2 * v + (1.0 - ADAM_B2) * (g * g)
    m_hat = m / (1.0 - ADAM_B1 ** ADAM_STEP)
    v_hat = v / (1.0 - ADAM_B2 ** ADAM_STEP)
    return -ADAM_LR * (m_hat / (jnp.sqrt(v_hat) + ADAM_EPS) + ADAM_WD * w), m, v


def adam_sum(w, pieces, m, v, name, layer=0, into=None):
    nl, r, c = w.shape
    tc = _pick(c, 256)

    def body(w_ref, p_ref, m_ref, v_ref, *rest):
        g_ref, d_ref, nm_ref, nv_ref = rest[-4:]
        g = p_ref[0].astype(F32)
        for s in range(1, N_DEV):
            g = g + p_ref[s].astype(F32)
        g_ref[0] = g
        d_ref[0], nm_ref[0], nv_ref[0] = _adamw(w_ref[0], g, m_ref[0], v_ref[0])

    row = pl.BlockSpec((1, r, tc), lambda i: (layer, 0, i))
    out = jax.ShapeDtypeStruct((nl, r, c), F32)
    extra = [] if into is None else list(into)
    return pl.pallas_call(
        body, name=name, grid=(c // tc,),
        in_specs=[row, pl.BlockSpec((N_DEV, r, tc), lambda i: (0, 0, i)), row, row]
        + [pl.BlockSpec(memory_space=pl.ANY)] * len(extra),
        out_specs=(row,) * 4, out_shape=(out,) * 4,
        input_output_aliases={4 + i: i for i in range(len(extra))},
        compiler_params=_cp("parallel"))(w, pieces, m, v, *extra)


def adam_sum_linear(w, pieces, m, v, name):
    _, r, c = pieces.shape
    nq = c // 128
    assert w.shape == (r * nq, 128)

    def body(w_ref, p_ref, m_ref, v_ref, g_ref, d_ref, nm_ref, nv_ref):
        for q in range(nq):
            rows = pl.ds(q, r, stride=nq)
            lanes = slice(q * 128, (q + 1) * 128)
            g = p_ref[0, :, lanes].astype(F32)
            for s in range(1, N_DEV):
                g = g + p_ref[s, :, lanes].astype(F32)
            g_ref[rows, :] = g
            d_ref[rows, :], nm_ref[rows, :], nv_ref[rows, :] = _adamw(w_ref[rows, :], g, m_ref[rows, :], v_ref[rows, :])

    return pl.pallas_call(body, name=name, out_shape=(jax.ShapeDtypeStruct(w.shape, F32),) * 4,
                          compiler_params=pltpu.CompilerParams(vmem_limit_bytes=VMEM_LIMIT))(w, pieces, m, v)


def sum_rows(gathered, name):
    _, r, c = gathered.shape

    def body(p_ref, o_ref):
        g = p_ref[0]
        for s in range(1, N_DEV):
            g = g + p_ref[s]
        o_ref[...] = g

    return pl.pallas_call(body, name=name, out_shape=jax.ShapeDtypeStruct((r, c), F32))(gathered)


def adam_small(w, g, m, v, name):
    def body(w_ref, g_ref, m_ref, v_ref, d_ref, nm_ref, nv_ref):
        d_ref[...], nm_ref[...], nv_ref[...] = _adamw(w_ref[...], g_ref[...], m_ref[...], v_ref[...])

    out = jax.ShapeDtypeStruct(w.shape, F32)
    return pl.pallas_call(body, name=name, out_shape=(out,) * 3)(w, g, m, v)


def _rope_tables(t):
    inv_freq = 10000.0 ** (-jnp.arange(0, HEAD_DIM, 2, dtype=F32) / HEAD_DIM)
    lanes = lambda a: jnp.concatenate([a] * (128 // a.shape[-1]), axis=-1)
    base = (jnp.arange(t // QK_TM, dtype=F32) * QK_TM)[:, None] * inv_freq[None, :]
    offs = jnp.arange(QK_TM, dtype=F32)[:, None] * inv_freq[None, :]
    return (lanes(jnp.cos(base))[:, None, :], lanes(jnp.sin(base))[:, None, :], lanes(jnp.cos(offs)), lanes(jnp.sin(offs)))


def _lane_row(vec8):
    return jnp.pad(vec8.reshape(1, DN_HEADS), ((0, 0), (DN_HEADS, 128 - 2 * DN_HEADS)))


def _ffn_bwd(x, norm_g, w_gu, w_d, saved, dy, tag, after=None):
    ft, gu, at = saved
    dgu = ffn_dact(dy, w_d, gu, f"{tag}_d_gate_up", after=after)
    dwd = mm_at(at, dy, f"{tag}_dw_down")
    nj = D_FF // GU_TILE
    dwgu = mm_at(ft, dgu, f"{tag}_dw_gate_up", transposed=True, tn=GU_TILE, row_block=lambda q: (q % 2) * nj + q // 2)
    dx, dg = mm_rms_bwd(dgu, w_gu, x, norm_g, dy, f"{tag}_d_norm", chunks=_GU_CHUNKS)
    return dx, dwgu, dwd, dg


def local_step(x, target, small, weights_of, grads_out, after=None):
    t = x.shape[0]
    rope = _rope_tables(t)
    alog_row, dtb_row = _lane_row(small["odd_a_log"]), _lane_row(small["odd_dt_bias"])

    h0, h0t = rms_fwd(x, small["even_norm"], "even_norm", after=after)
    we = weights_of("even", h0)
    small = {**small, **we.get("small", {})}
    proj0 = mm_nt(h0, we["w_in"], "even_in_proj", after=we.get("after"))
    qr, kr = qk_prep_fwd(proj0, small["even_q_gain"], small["even_k_gain"], rope, "even_qk_prep")
    y_attn, mix0, mix0_t, lse = swa_fwd(qr, kr, proj0, small["even_sinks"], "even_swa")
    mix0, mix0_t = gconv_fwd(proj0, small["even_conv_w"], mix0, mix0_t, "even_gconv")
    we = {**we, **weights_of("even_out", mix0)}
    x1, f0, f0t = mm_nn_res_norm(mix0, we["w_out"], x, small["ffn_norm0"], "even_out_proj")
    w0 = weights_of("ffn0", x1)
    gu0, a0, a0t = ffn_up(f0, w0["gate_up"], "ffn0_gate_up")
    ffn0 = (f0t, gu0, a0t)
    x2, h1, h1t = mm_nn_res_norm(a0, w0["down"], x1, small["odd_norm"], "ffn0_down")

    wo = weights_of("odd", x2)
    proj1 = mm_nt(h1, wo["w_in"], "odd_in_proj")
    qn, kn, vs, bg, conv1 = gdn_prep_fwd(proj1, small["odd_conv_w"], alog_row, dtb_row, "odd_prep")
    o, sall, tall, og, ogt = gdn_fwd(qn, kn, vs, bg, proj1, small["odd_o_gain"], "odd_delta_rule")
    x3, f1, f1t = mm_nn_res_norm(og, wo["w_out"], x2, small["ffn_norm1"], "odd_out_proj")
    w1 = weights_of("ffn1", x3)
    gu1, a1, a1t = ffn_up(f1, w1["gate_up"], "ffn1_gate_up")
    ffn1 = (f1t, gu1, a1t)
    dy, loss_row = mm_nn_res_loss(a1, w1["down"], x3, target, "ffn1_down_loss")

    gs = {}
    dx3, dwgu, dwd, gs["ffn_norm1"] = _ffn_bwd(x3, small["ffn_norm1"], w1["gate_up"], w1["down"], ffn1, dy, "ffn1")
    tok = grads_out("ffn1", {"gate_up": dwgu, "down": dwd})

    do, dz, gs["odd_o_gain"] = gdn_out_bwd(o, proj1, small["odd_o_gain"], dx3, wo["w_out"], "odd_d_gate_norm", after=tok)
    dwo = mm_at(ogt, dx3, "odd_dw_out")
    dqn, dkn, dvs, dbg = gdn_bwd(qn, kn, vs, bg, sall, tall, do, "odd_d_delta_rule")
    dproj1, gs["odd_conv_w"], ddt_row, dal_row = gdn_prep_bwd(
        proj1, conv1, small["odd_conv_w"], alog_row, dtb_row, dqn, dkn, dvs, dbg, dz, "odd_d_prep")
    gs["odd_dt_bias"] = ddt_row[:, DN_HEADS:2 * DN_HEADS]
    gs["odd_a_log"] = dal_row[:, DN_HEADS:2 * DN_HEADS]
    dwi = mm_at(h1t, dproj1, "odd_dw_in", transposed=True)
    dx2, gs["odd_norm"] = mm_rms_bwd(dproj1, wo["w_in"], x2, small["odd_norm"], dx3, "odd_d_norm")
    tok = grads_out("odd", {"w_in": dwi, "w_out": dwo})

    dx1, dwgu, dwd, gs["ffn_norm0"] = _ffn_bwd(x1, small["ffn_norm0"], w0["gate_up"], w0["down"], ffn0, dx2, "ffn0",
                                               after=tok)
    tok = grads_out("ffn0", {"gate_up": dwgu, "down": dwd})

    dmix = mm_nt(dx1, we["w_out"], "even_d_mix", after=tok)
    dwo = mm_at(mix0_t, dx1, "even_dw_out")
    dproj0 = lax.empty((t, EVEN_IN_W), MXU_DTYPE)
    dqr, dkr, dproj0, gs["even_sinks"] = swa_bwd(
        qr, kr, proj0, small["even_sinks"], y_attn, lse, dmix, dproj0, "even_d_swa")
    dproj0, gs["even_q_gain"], gs["even_k_gain"] = qk_prep_bwd(
        proj0, small["even_q_gain"], small["even_k_gain"], rope, dqr, dkr, dproj0, "even_d_qk_prep")
    dproj0, gs["even_conv_w"] = gconv_bwd(proj0, small["even_conv_w"], dmix, dproj0, "even_d_gconv")
    dwi = mm_at(h0t, dproj0, "even_dw_in", transposed=True, chunks=_EVEN_D_CHUNKS)
    tok = grads_out("even", {"w_in": dwi, "w_out": dwo})
    grad_x, gs["even_norm"] = mm_rms_bwd(dproj0, we["w_in"], x, small["even_norm"], dx1, "even_d_norm", after=tok,
                                         chunks=_EVEN_D_CHUNKS)
    return loss_row, grad_x, gs


_SMALL_ORDER = ("even_norm", "even_q_gain", "even_k_gain", "even_sinks", "odd_a_log", "odd_dt_bias", "odd_o_gain",
                "ffn_norm0", "ffn_norm1", "odd_norm", "even_conv_w", "odd_conv_w")
_SMALL_SIZE = {"even_norm": 1024, "even_q_gain": 64, "even_k_gain": 64, "even_sinks": 8, "odd_a_log": 8,
               "odd_dt_bias": 8, "odd_o_gain": 128, "ffn_norm0": 1024, "ffn_norm1": 1024, "odd_norm": 1024,
               "even_conv_w": 3 * 512, "odd_conv_w": 4 * 3072}
_N_REPL = 9


def _pack_rows(vals):
    flat = jnp.concatenate([v.reshape(-1) for v in vals])
    pad = (-flat.shape[0]) % 1024
    return jnp.pad(flat, (0, pad)).reshape(-1, 128)


def _my_block(full, size, axis):
    me = 4 * lax.axis_index("x") + 2 * lax.axis_index("y") + lax.axis_index("c")
    return lax.dynamic_slice_in_dim(full, me * size, size, axis=axis)


def kernel(x, even_norm, even_w_in, even_q_gain, even_k_gain, even_sinks, even_conv_w, even_w_out, odd_norm, odd_w_in, odd_conv_w, odd_a_log, odd_dt_bias, odd_o_gain, odd_w_out, ffn_norm, ffn_w_gate_up, ffn_w_down, loss_target, m_even_norm, m_even_w_in, m_even_q_gain, m_even_k_gain, m_even_sinks, m_even_conv_w, m_even_w_out, m_odd_norm, m_odd_w_in, m_odd_conv_w, m_odd_a_log, m_odd_dt_bias, m_odd_o_gain, m_odd_w_out, m_ffn_norm, m_ffn_w_gate_up, m_ffn_w_down, v_even_norm, v_even_w_in, v_even_q_gain, v_even_k_gain, v_even_sinks, v_even_conv_w, v_even_w_out, v_odd_norm, v_odd_w_in, v_odd_conv_w, v_odd_a_log, v_odd_dt_bias, v_odd_o_gain, v_odd_w_out, v_ffn_norm, v_ffn_w_gate_up, v_ffn_w_down):
    t = x.shape[1]
    d = D_MODEL

    tr = lambda a: jnp.swapaxes(a, 1, 2)
    shard = {
        "even": {"w_in": tr(even_w_in)[0], "w_out": even_w_out[0]},
        "ffn0": {"gate_up": tr(ffn_w_gate_up)[0], "down": ffn_w_down[0]},
        "odd": {"w_in": tr(odd_w_in)[0], "w_out": odd_w_out[0]},
        "ffn1": {"gate_up": tr(ffn_w_gate_up)[1], "down": ffn_w_down[1]},
    }
    given = {
        ("even", "w_in"): ("even_w_in", even_w_in, m_even_w_in, v_even_w_in, 0),
        ("even", "w_out"): ("even_w_out", even_w_out, m_even_w_out, v_even_w_out, 0),
        ("odd", "w_in"): ("odd_w_in", odd_w_in, m_odd_w_in, v_odd_w_in, 0),
        ("odd", "w_out"): ("odd_w_out", odd_w_out, m_odd_w_out, v_odd_w_out, 0),
        ("ffn0", "gate_up"): ("ffn_w_gate_up", ffn_w_gate_up, m_ffn_w_gate_up, v_ffn_w_gate_up, 0),
        ("ffn1", "gate_up"): ("ffn_w_gate_up", ffn_w_gate_up, m_ffn_w_gate_up, v_ffn_w_gate_up, 1),
        ("ffn0", "down"): ("ffn_w_down", ffn_w_down, m_ffn_w_down, v_ffn_w_down, 0),
        ("ffn1", "down"): ("ffn_w_down", ffn_w_down, m_ffn_w_down, v_ffn_w_down, 1),
    }

    def whole(group, parts):
        col, row = tuple(shard[group])
        if group == "odd":
            w_col = join_blocks(parts[0], ODD_IN_PAD, "odd_w_in_join")
        else:
            w_col = parts[0].reshape(-1, d)
        return {col: w_col, row: parts[1].reshape(-1, d)}

    wire = {g: [a.astype(MXU_DTYPE) for a in shard[g].values()] for g in shard}
    wire["even_out"] = [wire["even"].pop()]
    wire["even"].append(_pack_rows([odd_norm, even_conv_w, odd_conv_w]))
    sems, srcs_thru, lands_thru, tok = send_start(wire["even"], "gather_even_start", False, None, masks=_OTHER_CHIPS)
    gathers = {"even": (sems, srcs_thru, lands_thru)}

    def start_rest(after):
        order = ("even_out", "ffn0", "odd", "ffn1")
        sems, srcs_thru, lands_thru, token = send_start([a for g in order for a in wire[g]], "gather_rest_start", False,
                                                        after)
        a0, na = 0, sum(len(wire[g]) for g in order)
        for g in order:
            a1 = a0 + len(wire[g])
            gathers[g] = (sems[2 * N_PEER * a0:2 * N_PEER * a1] + sems[2 * N_PEER * na + a0:2 * N_PEER * na + a1],
                          srcs_thru[a0:a1], lands_thru[a0:a1])
            a0 = a1
        return token

    o1 = d // N_DEV
    o2 = o1 + 3 * CONV_CH // N_DEV

    def weights_of(group, after):
        first = group == "even"
        lands = send_wait(*gathers[group], f"gather_{group}_wait", False, after,
                          masks=_OTHER_CHIPS if first else _ALL_PEERS)
        if first:
            lands = relay_blocks(lands, "gather_even_relay")
        if group == "even_out":
            return {"w_out": lands[0].reshape(-1, d)}
        if group != "even":
            return whole(group, lands)
        sg = lands[1].reshape(N_DEV, -1)
        return {"w_in": lands[0].reshape(-1, d), "after": start_rest(lands[0]), "small": {
            "odd_norm": sg[:, :o1].reshape(1, d),
            "even_conv_w": sg[:, o1:o2].reshape(N_DEV, 3, CONV_CH // N_DEV).transpose(1, 0, 2).reshape(3, CONV_CH),
            "odd_conv_w": sg[:, o2:o2 + 4 * _QKV_W // N_DEV].reshape(N_DEV, 4, _QKV_W // N_DEV)
            .transpose(1, 0, 2).reshape(4, _QKV_W)}}

    sent = {}

    def grads_out(group, dws):
        col, row = tuple(shard[group])
        c = shard[group][col].shape[0]
        pieces = [split_blocks(dws[col], N_DEV, c, "odd_dw_in_split") if group == "odd"
                  else dws[col].reshape((N_DEV,) + shard[group][col].shape),
                  dws[row].reshape((N_DEV,) + shard[group][row].shape)]
        sems, srcs_thru, lands_thru, token = send_start(pieces, f"exchange_{group}_start", True, None)
        sent[group] = (sems, srcs_thru, lands_thru, pieces)
        return token

    small = {
        "even_norm": even_norm, "even_q_gain": even_q_gain, "even_k_gain": even_k_gain, "even_sinks": even_sinks,
        "odd_a_log": odd_a_log.reshape(-1), "odd_dt_bias": odd_dt_bias.reshape(-1), "odd_o_gain": odd_o_gain,
        "ffn_norm0": ffn_norm[0:1], "ffn_norm1": ffn_norm[1:2],
    }

    loss_row, grad_x, gs = local_step(x.reshape(t, d), loss_target.reshape(t, d), small, weights_of, grads_out, after=tok)

    rows = _pack_rows([gs[n] for n in _SMALL_ORDER] + [loss_row[:, 0:1]])
    small_sent = send_start([rows], "gather_small_grads_start", False, None)

    res, behind = {}, small_sent[3]
    for g in ("ffn1", "odd", "ffn0", "even"):
        sems, srcs_thru, lands_thru, pieces = sent[g]
        lands = send_wait(sems, srcs_thru, lands_thru, f"exchange_{g}_wait", True, behind)
        for i, (key, pcs) in enumerate(zip(shard[g], lands)):
            name, w_, m_, v_, layer = given[g, key]
            view = tr if i == 0 else (lambda a: a)
            if pcs.shape[1] % 8:
                lin = lambda a: tr(a).reshape(-1, 128)
                res[name] = tuple(a.reshape((1,) + pcs.shape[1:]) for a in adam_sum_linear(
                    lin(w_), pcs, lin(m_), lin(v_), f"adamw_{g}_{key}"))
                continue
            res[name] = adam_sum(view(w_), pcs, view(m_), view(v_), f"adamw_{g}_{key}", layer=layer, into=res.get(name))
        behind = res[name][0]
    for name in ("even_w_in", "odd_w_in", "ffn_w_gate_up"):
        res[name] = tuple(tr(a) for a in res[name])

    (rows_g,) = send_wait(*small_sent[:3], "gather_small_grads_wait", False, behind)
    tot = sum_rows(rows_g, "sum_small_grads").reshape(-1)
    off, sgrad = 0, {}
    for n in _SMALL_ORDER:
        sgrad[n] = tot[off:off + _SMALL_SIZE[n]]
        off += _SMALL_SIZE[n]
    loss = tot[off]

    repl = _SMALL_ORDER[:_N_REPL]
    repl_w = {"even_norm": even_norm, "even_q_gain": even_q_gain, "even_k_gain": even_k_gain, "even_sinks": even_sinks,
              "odd_a_log": odd_a_log, "odd_dt_bias": odd_dt_bias, "odd_o_gain": odd_o_gain,
              "ffn_norm0": ffn_norm[0], "ffn_norm1": ffn_norm[1]}
    repl_m = {"even_norm": m_even_norm, "even_q_gain": m_even_q_gain, "even_k_gain": m_even_k_gain,
              "even_sinks": m_even_sinks, "odd_a_log": m_odd_a_log, "odd_dt_bias": m_odd_dt_bias,
              "odd_o_gain": m_odd_o_gain, "ffn_norm0": m_ffn_norm[0], "ffn_norm1": m_ffn_norm[1]}
    repl_v = {"even_norm": v_even_norm, "even_q_gain": v_even_q_gain, "even_k_gain": v_even_k_gain,
              "even_sinks": v_even_sinks, "odd_a_log": v_odd_a_log, "odd_dt_bias": v_odd_dt_bias,
              "odd_o_gain": v_odd_o_gain, "ffn_norm0": v_ffn_norm[0], "ffn_norm1": v_ffn_norm[1]}
    pk = lambda dct: _pack_rows([dct[n] for n in repl])
    pd_, pm_, pv_ = adam_small(pk(repl_w), pk(sgrad), pk(repl_m), pk(repl_v), "adamw_replicated")
    sres = {}
    off = 0
    for n in repl:
        sz = _SMALL_SIZE[n]
        sres[n] = (sgrad[n], pd_.reshape(-1)[off:off + sz], pm_.reshape(-1)[off:off + sz], pv_.reshape(-1)[off:off + sz])
        off += sz
    g_on = _my_block(sgrad["odd_norm"].reshape(1, d), d // N_DEV, 1)
    g_ec = _my_block(sgrad["even_conv_w"].reshape(3, CONV_CH), CONV_CH // N_DEV, 1)
    g_oc = _my_block(sgrad["odd_conv_w"].reshape(4, _QKV_W), _QKV_W // N_DEV, 1)
    shard_w = _pack_rows([odd_norm, even_conv_w, odd_conv_w])
    sd_, sm_, sv_ = adam_small(shard_w, _pack_rows([g_on, g_ec, g_oc]),
                               _pack_rows([m_odd_norm, m_even_conv_w, m_odd_conv_w]),
                               _pack_rows([v_odd_norm, v_even_conv_w, v_odd_conv_w]), "adamw_sharded_small")
    off = 0
    for n, gfull, like in (("odd_norm", g_on, odd_norm), ("even_conv_w", g_ec, even_conv_w), ("odd_conv_w", g_oc, odd_conv_w)):
        sz = like.size
        sres[n] = (gfull, sd_.reshape(-1)[off:off + sz], sm_.reshape(-1)[off:off + sz], sv_.reshape(-1)[off:off + sz])
        off += sz

    def small_out(name, like, kind):
        if name == "ffn_norm":
            return jnp.stack([sres["ffn_norm0"][kind], sres["ffn_norm1"][kind]]).reshape(like.shape)
        return sres[name][kind].reshape(like.shape)

    order = (("even_norm", even_norm), ("even_w_in", even_w_in), ("even_q_gain", even_q_gain),
             ("even_k_gain", even_k_gain), ("even_sinks", even_sinks), ("even_conv_w", even_conv_w),
             ("even_w_out", even_w_out), ("odd_norm", odd_norm), ("odd_w_in", odd_w_in), ("odd_conv_w", odd_conv_w),
             ("odd_a_log", odd_a_log), ("odd_dt_bias", odd_dt_bias), ("odd_o_gain", odd_o_gain),
             ("odd_w_out", odd_w_out), ("ffn_norm", ffn_norm), ("ffn_w_gate_up", ffn_w_gate_up),
             ("ffn_w_down", ffn_w_down))
    outs = [loss, grad_x.reshape(x.shape)]
    for kind in range(4):
        for name, like in order:
            outs.append(res[name][kind] if name in res else small_out(name, like, kind))
    return tuple(outs)
```

```python
import jax
import jax.numpy as jnp
import numpy as np
from jax import lax
from jax.experimental import pallas as pl
from jax.experimental.pallas import tpu as pltpu

F32 = jnp.float32
MXU_DTYPE = jnp.bfloat16
HI = lax.Precision.HIGH
EPS = 1e-6
N_DEV = 8
D_MODEL = 1024
HEAD_DIM = 64
ATTN_HEADS = 8
KV_HEADS = 2
ATTN_BLOCK = 128
Q_W = 512
KV_W = 128
CONV_CH = 512
EVEN_IN_W = 2304
DN_HEADS = 8
DN_DIM = 128
DN_W = 1024
DN_CHUNK = 64
ODD_IN_W = 4112
ODD_IN_PAD = 4224
D_FF = 2816
NEG = -1e30
VMEM_LIMIT = 56 * 1024 * 1024
ADAM_LR, ADAM_B1, ADAM_B2, ADAM_EPS, ADAM_WD, ADAM_STEP = 0.001, 0.9, 0.999, 1e-08, 0.01, 10
MESH = pl.DeviceIdType.MESH


def _cp(*sem):
    return pltpu.CompilerParams(dimension_semantics=sem, vmem_limit_bytes=VMEM_LIMIT)


def _pick(n, cap):
    best = 128
    for t in range(128, cap + 1, 128):
        if n % t == 0:
            best = t
    return best


def _mx(a, b):
    return jnp.dot(a.astype(MXU_DTYPE), b.astype(MXU_DTYPE), preferred_element_type=F32)


def _mx_nt(a, b):
    return lax.dot_general(a.astype(MXU_DTYPE), b.astype(MXU_DTYPE), (((1,), (1,)), ((), ())),
                           preferred_element_type=F32)


def _mx_tn(a, b):
    return lax.dot_general(a.astype(MXU_DTYPE), b.astype(MXU_DTYPE), (((0,), (0,)), ((), ())),
                           preferred_element_type=F32)


def _hi(a, b):
    return jnp.dot(a, b, precision=HI, preferred_element_type=F32)


def _hi_nt(a, b):
    return lax.dot_general(a, b, (((1,), (1,)), ((), ())), precision=HI, preferred_element_type=F32)


def _hi_tn(a, b):
    return lax.dot_general(a, b, (((0,), (0,)), ((), ())), precision=HI, preferred_element_type=F32)


def _sigmoid(x):
    return 0.5 * jnp.tanh(0.5 * x) + 0.5


def _softplus(x):
    return jnp.maximum(x, 0.0) + jnp.log(1.0 + jnp.exp(-jnp.abs(x)))


def mm_nn_res_norm(a, b, res, g, name, tm=512):
    t, k = a.shape
    d = b.shape[1]
    tm = min(tm, t)

    def body(a_ref, b_ref, res_ref, g_ref, y_ref, h_ref, ht_ref):
        y = res_ref[...] + _mx(a_ref[...], b_ref[...])
        y_ref[...] = y
        h = y * lax.rsqrt(jnp.mean(y * y, axis=-1, keepdims=True) + EPS) * g_ref[...]
        h_ref[...] = h.astype(h_ref.dtype)
        ht_ref[...] = h.T.astype(ht_ref.dtype)

    row = pl.BlockSpec((tm, d), lambda i: (i, 0))
    return pl.pallas_call(
        body, name=name, grid=(t // tm,),
        in_specs=[pl.BlockSpec((tm, k), lambda i: (i, 0)), pl.BlockSpec((k, d), lambda i: (0, 0)), row,
                  pl.BlockSpec((1, d), lambda i: (0, 0))],
        out_specs=(row, row, pl.BlockSpec((d, tm), lambda i: (0, i))),
        out_shape=(jax.ShapeDtypeStruct((t, d), F32), jax.ShapeDtypeStruct((t, d), MXU_DTYPE),
                   jax.ShapeDtypeStruct((d, t), MXU_DTYPE)),
        compiler_params=_cp("parallel"))(a, b, res, g)


def mm_nt(a, b, name, out_dtype=F32, tm=2048, after=None):
    m, k = a.shape
    n, _ = b.shape
    tn = _pick(n, 512 if k > 3000 else 1536)
    tm = min(tm, m)

    def body(a_ref, b_ref, *rest):
        o_ref = rest[-1]
        o_ref[...] = _mx_nt(a_ref[...], b_ref[...]).astype(o_ref.dtype)

    in_specs = [pl.BlockSpec((tm, k), lambda j, i: (i, 0)), pl.BlockSpec((tn, k), lambda j, i: (j, 0))]
    args = [a, b]
    if after is not None:
        in_specs.append(pl.BlockSpec(memory_space=pl.ANY))
        args.append(after)
    return pl.pallas_call(
        body, name=name, grid=(n // tn, m // tm), in_specs=in_specs,
        out_specs=pl.BlockSpec((tm, tn), lambda j, i: (i, j)),
        out_shape=jax.ShapeDtypeStruct((m, n), out_dtype), compiler_params=_cp("parallel", "parallel"))(*args)


def mm_at(at, b, name, tk=2048, transposed=False, tn=None, row_block=None, chunks=None):
    m, kk = at.shape
    _, n = b.shape
    tm, tn, tk = _pick(m, 1408), tn or _pick(n, 2816), min(tk, kk)
    nk = kk // tk
    assert chunks is None or (transposed and tn == n)

    def body(a_ref, b_ref, o_ref, acc_ref):
        k = pl.program_id(2)
        p = _mx(a_ref[...], b_ref[...])
        acc = jnp.where(k == 0, p, acc_ref[...] + p)
        acc_ref[...] = acc

        @pl.when(k == nk - 1)
        def _():
            res = (acc.T if transposed else acc).astype(o_ref.dtype)
            if chunks is None:
                o_ref[...] = res
            else:
                for cb, co, size in chunks:
                    o_ref[co:co + size, :] = res[cb:cb + size]

    if transposed:
        rb = row_block or (lambda j: j)
        out_spec = pl.BlockSpec((tn, tm), lambda i, j, k: (rb(j), i))
        out_shape = jax.ShapeDtypeStruct((n, m), MXU_DTYPE)
    else:
        out_spec = pl.BlockSpec((tm, tn), lambda i, j, k: (i, j))
        out_shape = jax.ShapeDtypeStruct((m, n), MXU_DTYPE)
    return pl.pallas_call(
        body, name=name, grid=(m // tm, n // tn, nk),
        in_specs=[pl.BlockSpec((tm, tk), lambda i, j, k: (i, k)), pl.BlockSpec((tk, tn), lambda i, j, k: (k, j))],
        out_specs=out_spec, out_shape=out_shape, scratch_shapes=[pltpu.VMEM((tm, tn), F32)],
        compiler_params=_cp("parallel", "parallel", "arbitrary"))(at, b)


def rms_fwd(x, g, name, tm=512, after=None):
    t, d = x.shape

    def body(x_ref, g_ref, *rest):
        o_ref, ot_ref = rest[-2:]
        xv = x_ref[...]
        r = lax.rsqrt(jnp.mean(xv * xv, axis=-1, keepdims=True) + EPS)
        h = xv * r * g_ref[...]
        o_ref[...] = h.astype(o_ref.dtype)
        ot_ref[...] = h.T.astype(ot_ref.dtype)

    in_specs = [pl.BlockSpec((tm, d), lambda i: (i, 0)), pl.BlockSpec((1, d), lambda i: (0, 0))]
    args = [x, g]
    if after is not None:
        in_specs.append(pl.BlockSpec(memory_space=pl.ANY))
        args.append(after)
    return pl.pallas_call(
        body, name=name, grid=(t // tm,), in_specs=in_specs,
        out_specs=(pl.BlockSpec((tm, d), lambda i: (i, 0)), pl.BlockSpec((d, tm), lambda i: (0, i))),
        out_shape=(jax.ShapeDtypeStruct((t, d), MXU_DTYPE), jax.ShapeDtypeStruct((d, t), MXU_DTYPE)),
        compiler_params=_cp("parallel"))(*args)


def mm_rms_bwd(a, bt, x, g, dres, name, tm=512, after=None, chunks=None):
    t, k = a.shape
    d = bt.shape[1]
    tm = min(tm if k > 3000 else 2 * tm, t)
    chunks = chunks or ((0, 0, k),)

    def body(a_ref, b_ref, x_ref, g_ref, dres_ref, *rest):
        dx_ref, dg_ref = rest[-2:]
        dhv = None
        for ca, cb, size in chunks:
            part = _mx(a_ref[:, ca:ca + size], b_ref[cb:cb + size, :])
            dhv = part if dhv is None else dhv + part
        xv = x_ref[...]
        r = lax.rsqrt(jnp.mean(xv * xv, axis=-1, keepdims=True) + EPS)
        xh = xv * r
        dxh = dhv * g_ref[...]
        dx_ref[...] = dres_ref[...] + r * (dxh - xh * jnp.mean(dxh * xh, axis=-1, keepdims=True))
        part = jnp.sum(dhv * xh, axis=0, keepdims=True)
        dg_ref[...] = jnp.where(pl.program_id(0) == 0, part, dg_ref[...] + part)

    row = pl.BlockSpec((tm, d), lambda i: (i, 0))
    one = pl.BlockSpec((1, d), lambda i: (0, 0))
    in_specs = [pl.BlockSpec((tm, k), lambda i: (i, 0)), pl.BlockSpec((k, d), lambda i: (0, 0)), row, one, row]
    args = [a, bt, x, g, dres]
    if after is not None:
        in_specs.append(pl.BlockSpec(memory_space=pl.ANY))
        args.append(after)
    return pl.pallas_call(
        body, name=name, grid=(t // tm,), in_specs=in_specs, out_specs=(row, one),
        out_shape=(jax.ShapeDtypeStruct((t, d), F32), jax.ShapeDtypeStruct((1, d), F32)),
        compiler_params=_cp("arbitrary"))(*args)


GU_TILE = 1408


def ffn_up(f, wt, name, tm=1024):
    t, d = f.shape
    tm = min(tm, t)
    nj = D_FF // GU_TILE

    def body(f_ref, wg_ref, wu_ref, gu_ref, a_ref, at_ref):
        g = _mx_nt(f_ref[...], wg_ref[...])
        u = _mx_nt(f_ref[...], wu_ref[...])
        sg = _sigmoid(g)
        gs = g * sg
        gu_ref[:, :GU_TILE] = (u * (sg + gs - gs * sg)).astype(gu_ref.dtype)
        gu_ref[:, GU_TILE:] = gs.astype(gu_ref.dtype)
        act = gs * u
        a_ref[...] = act.astype(a_ref.dtype)
        at_ref[...] = act.T.astype(at_ref.dtype)

    return pl.pallas_call(
        body, name=name, grid=(nj, t // tm),
        in_specs=[pl.BlockSpec((tm, d), lambda j, i: (i, 0)), pl.BlockSpec((GU_TILE, d), lambda j, i: (j, 0)),
                  pl.BlockSpec((GU_TILE, d), lambda j, i: (nj + j, 0))],
        out_specs=(pl.BlockSpec((tm, 2 * GU_TILE), lambda j, i: (i, j)), pl.BlockSpec((tm, GU_TILE), lambda j, i: (i, j)),
                   pl.BlockSpec((GU_TILE, tm), lambda j, i: (j, i))),
        out_shape=(jax.ShapeDtypeStruct((t, 2 * D_FF), MXU_DTYPE), jax.ShapeDtypeStruct((t, D_FF), MXU_DTYPE),
                   jax.ShapeDtypeStruct((D_FF, t), MXU_DTYPE)),
        compiler_params=_cp("parallel", "parallel"))(f, wt, wt)


_GU_CHUNKS = tuple((q * GU_TILE, ((q % 2) * (D_FF // GU_TILE) + q // 2) * GU_TILE, GU_TILE)
                   for q in range(2 * D_FF // GU_TILE))


def ffn_dact(dy, w_d, gu, name, tm=1024, after=None):
    t, d = dy.shape
    tm = min(tm, t)

    def body(dy_ref, w_ref, gu_ref, *rest):
        o_ref = rest[-1]
        da = _mx_nt(dy_ref[...], w_ref[...])
        o_ref[:, :GU_TILE] = (da * gu_ref[:, :GU_TILE]).astype(o_ref.dtype)
        o_ref[:, GU_TILE:] = (da * gu_ref[:, GU_TILE:]).astype(o_ref.dtype)

    in_specs = [pl.BlockSpec((tm, d), lambda j, i: (i, 0)), pl.BlockSpec((GU_TILE, d), lambda j, i: (j, 0)),
                pl.BlockSpec((tm, 2 * GU_TILE), lambda j, i: (i, j))]
    args = [dy, w_d, gu]
    if after is not None:
        in_specs.append(pl.BlockSpec(memory_space=pl.ANY))
        args.append(after)
    return pl.pallas_call(
        body, name=name, grid=(D_FF // GU_TILE, t // tm), in_specs=in_specs,
        out_specs=pl.BlockSpec((tm, 2 * GU_TILE), lambda j, i: (i, j)),
        out_shape=jax.ShapeDtypeStruct((t, 2 * D_FF), MXU_DTYPE), compiler_params=_cp("parallel", "parallel"))(*args)


def mm_nn_res_loss(a, b, res, target, name, tm=512):
    t, k = a.shape
    d = b.shape[1]
    tm = min(tm, t)

    def body(a_ref, b_ref, res_ref, t_ref, dy_ref, l_ref):
        e = res_ref[...] + _mx(a_ref[...], b_ref[...]) - t_ref[...]
        dy_ref[...] = e * (1.0 / d)
        part = jnp.zeros((1, 128), F32) + 0.5 * jnp.sum(jnp.mean(e * e, axis=-1, keepdims=True), axis=0, keepdims=True)
        l_ref[...] = jnp.where(pl.program_id(0) == 0, part, l_ref[...] + part)

    row = pl.BlockSpec((tm, d), lambda i: (i, 0))
    return pl.pallas_call(
        body, name=name, grid=(t // tm,),
        in_specs=[pl.BlockSpec((tm, k), lambda i: (i, 0)), pl.BlockSpec((k, d), lambda i: (0, 0)), row, row],
        out_specs=(row, pl.BlockSpec((1, 128), lambda i: (0, 0))),
        out_shape=(jax.ShapeDtypeStruct((t, d), F32), jax.ShapeDtypeStruct((1, 128), F32)),
        compiler_params=_cp("arbitrary"))(a, b, res, target)


QK_W = Q_W + KV_W
_QK_TILE = 256


def _qk_mats():
    idx = np.arange(_QK_TILE)
    half = HEAD_DIM // 2
    same = (idx[:, None] // HEAD_DIM) == (idx[None, :] // HEAD_DIM)
    lo = (idx % HEAD_DIM) < half
    rot = np.where((idx[:, None] == idx[None, :] + half) & lo[None, :], -1.0, 0.0)
    rot = rot + np.where((idx[:, None] == idx[None, :] - half) & ~lo[None, :], 1.0, 0.0)
    return jnp.asarray(same, F32), jnp.asarray(rot, F32)


QK_TM = 256


def _qk_gains(q_gain, k_gain):
    return jnp.concatenate([q_gain] * ATTN_HEADS + [k_gain] * KV_HEADS, axis=-1)


def _rope_tile(ca_ref, sa_ref, cb_ref, sb_ref):
    ca, sa, cb, sb = ca_ref[0], sa_ref[0], cb_ref[...], sb_ref[...]
    c, s = ca * cb - sa * sb, sa * cb + ca * sb
    rep = QK_W // 128
    return jnp.concatenate([c] * rep, axis=-1), jnp.concatenate([s] * rep, axis=-1)


_ROPE_SPECS = [pl.BlockSpec((1, 1, 128), lambda i: (i, 0, 0)), pl.BlockSpec((1, 1, 128), lambda i: (i, 0, 0)),
               pl.BlockSpec((QK_TM, 128), lambda i: (0, 0)), pl.BlockSpec((QK_TM, 128), lambda i: (0, 0))]


def _qk_tiles(a, mat, transposed=False):
    outs = []
    for c0 in range(0, QK_W, _QK_TILE):
        w = min(_QK_TILE, QK_W - c0)
        mt = (mat.T if transposed else mat)[:w, :w].astype(MXU_DTYPE)
        at = a[:, c0:c0 + w]
        hi = at.astype(MXU_DTYPE)
        lo = (at - hi.astype(F32)).astype(MXU_DTYPE)
        outs.append(jnp.dot(hi, mt, preferred_element_type=F32) + jnp.dot(lo, mt, preferred_element_type=F32))
    return jnp.concatenate(outs, axis=-1)


def qk_prep_fwd(proj, q_gain, k_gain, rope, name):
    t = proj.shape[0]
    tm = QK_TM
    gmat, rmat = _qk_mats()
    gain = _qk_gains(q_gain, k_gain)

    def body(p_ref, g_ref, ca_ref, sa_ref, cb_ref, sb_ref, gm_ref, rm_ref, q_ref, k_ref):
        x = p_ref[...]
        r = lax.rsqrt(_qk_tiles(x * x, gm_ref[...]) * (1.0 / HEAD_DIM) + EPS)
        xn = x * r * g_ref[...]
        c, s = _rope_tile(ca_ref, sa_ref, cb_ref, sb_ref)
        out = xn * c + _qk_tiles(xn, rm_ref[...]) * s
        q_ref[...] = out[:, :Q_W]
        k_ref[...] = out[:, Q_W:]

    full = pl.BlockSpec((_QK_TILE, _QK_TILE), lambda i: (0, 0))
    return pl.pallas_call(
        body, name=name, grid=(t // tm,),
        in_specs=[pl.BlockSpec((tm, QK_W), lambda i: (i, 0)), pl.BlockSpec((1, QK_W), lambda i: (0, 0))] + _ROPE_SPECS
        + [full, full],
        out_specs=(pl.BlockSpec((tm, Q_W), lambda i: (i, 0)), pl.BlockSpec((tm, KV_W), lambda i: (i, 0))),
        out_shape=(jax.ShapeDtypeStruct((t, Q_W), F32), jax.ShapeDtypeStruct((t, KV_W), F32)),
        compiler_params=_cp("parallel"))(proj, gain, *rope, gmat, rmat)


def qk_prep_bwd(proj, q_gain, k_gain, rope, dq, dk, into, name):
    t = proj.shape[0]
    tm = QK_TM
    gmat, rmat = _qk_mats()
    gain = _qk_gains(q_gain, k_gain)
    lanes = np.arange(QK_W)[:, None]
    fold = jnp.asarray(lanes % HEAD_DIM + np.where(lanes >= Q_W, HEAD_DIM, 0) == np.arange(128)[None, :], F32)

    def body(p_ref, g_ref, ca_ref, sa_ref, cb_ref, sb_ref, gm_ref, rm_ref, f_ref, dq_ref, dk_ref, into_ref,
             o_ref, dg_ref):
        x = p_ref[...]
        r = lax.rsqrt(_qk_tiles(x * x, gm_ref[...]) * (1.0 / HEAD_DIM) + EPS)
        xh = x * r
        c, s = _rope_tile(ca_ref, sa_ref, cb_ref, sb_ref)
        dout = jnp.concatenate([dq_ref[...], dk_ref[...]], axis=-1)
        dxn = dout * c + _qk_tiles(dout * s, rm_ref[...], transposed=True)
        part = _hi(jnp.sum(dxn * xh, axis=0, keepdims=True), f_ref[...])
        dxh = dxn * g_ref[...]
        mean = _qk_tiles(dxh * xh, gm_ref[...]) * (1.0 / HEAD_DIM)
        o_ref[...] = (r * (dxh - xh * mean)).astype(o_ref.dtype)
        dg_ref[...] = jnp.where(pl.program_id(0) == 0, part, dg_ref[...] + part)

    full = pl.BlockSpec((_QK_TILE, _QK_TILE), lambda i: (0, 0))
    dqk, dg = pl.pallas_call(
        body, name=name, grid=(t // tm,),
        in_specs=[pl.BlockSpec((tm, QK_W), lambda i: (i, 0)), pl.BlockSpec((1, QK_W), lambda i: (0, 0))] + _ROPE_SPECS
        + [full, full, pl.BlockSpec((QK_W, 128), lambda i: (0, 0)),
                  pl.BlockSpec((tm, Q_W), lambda i: (i, 0)), pl.BlockSpec((tm, KV_W), lambda i: (i, 0)),
                  pl.BlockSpec(memory_space=pl.ANY)],
        out_specs=(pl.BlockSpec((tm, QK_W), lambda i: (i, 0)), pl.BlockSpec((1, 128), lambda i: (0, 0))),
        out_shape=(jax.ShapeDtypeStruct(into.shape, into.dtype), jax.ShapeDtypeStruct((1, 128), F32)),
        input_output_aliases={len(rope) + 7: 0},
        compiler_params=_cp("arbitrary"))(proj, gain, *rope, gmat, rmat, fold, dq, dk, into)
    return dqk, dg[:, :HEAD_DIM], dg[:, HEAD_DIM:]


def _swa_valid(n, grp):
    qi = lax.broadcasted_iota(jnp.int32, (grp * ATTN_BLOCK, 2 * ATTN_BLOCK), 0) & (ATTN_BLOCK - 1)
    kj = lax.broadcasted_iota(jnp.int32, (grp * ATTN_BLOCK, 2 * ATTN_BLOCK), 1)
    diff = qi + ATTN_BLOCK - kj
    return (diff >= 0) & (diff < ATTN_BLOCK) & (n * ATTN_BLOCK - ATTN_BLOCK + kj >= 0)


def _stack_heads(ref, g, grp, rows=slice(None)):
    return jnp.concatenate([ref[rows, (g * grp + j) * HEAD_DIM:(g * grp + j + 1) * HEAD_DIM] for j in range(grp)], axis=0)


def _stack_sinks(s_ref, g, grp):
    return jnp.concatenate([jnp.zeros((ATTN_BLOCK, 1), F32) + s_ref[0:1, g * grp + j:g * grp + j + 1]
                            for j in range(grp)], axis=0)


SWA_STEP = 2


def swa_fwd(q, k, proj, sinks, name):
    t = q.shape[0]
    nb = t // ATTN_BLOCK
    scale = HEAD_DIM ** -0.5
    grp = ATTN_HEADS // KV_HEADS

    rows = SWA_STEP * ATTN_BLOCK

    def body(q_ref, kc_ref, kp_ref, vc_ref, vp_ref, s_ref, y_ref, mix_ref, yt_ref, lse_ref):
        n0 = pl.program_id(0) * SWA_STEP
        kk = jnp.concatenate([kp_ref[...], kc_ref[...]], axis=0).astype(MXU_DTYPE)
        vv = jnp.concatenate([vp_ref[...], vc_ref[...]], axis=0).astype(MXU_DTYPE)
        lane = lax.broadcasted_iota(jnp.int32, (ATTN_BLOCK, ATTN_HEADS), 1)
        units = [(b, g) for b in range(SWA_STEP) for g in range(KV_HEADS)]
        blk = lambda b: slice(b * ATTN_BLOCK, (b + 1) * ATTN_BLOCK)
        keys = lambda b: slice(b * ATTN_BLOCK, (b + 2) * ATTN_BLOCK)
        col = lambda g: slice(g * HEAD_DIM, (g + 1) * HEAD_DIM)
        valid = [_swa_valid(n0 + b, grp) for b in range(SWA_STEP)]
        qg = [_stack_heads(q_ref, g, grp, blk(b)) for b, g in units]
        sink = [_stack_sinks(s_ref, g, grp) for b, g in units]
        sc = [jnp.where(valid[b], _mx_nt(qg[u], kk[keys(b), col(g)]) * scale, NEG) for u, (b, g) in enumerate(units)]
        m = [jnp.maximum(jnp.max(sc_, axis=-1, keepdims=True), sk) for sc_, sk in zip(sc, sink)]
        e = [jnp.exp(sc_ - m_) for sc_, m_ in zip(sc, m)]
        den = [jnp.sum(e_, axis=-1, keepdims=True) + jnp.exp(sk - m_) for e_, sk, m_ in zip(e, sink, m)]
        og = [_mx(e[u] / den[u], vv[keys(b), col(g)]) for u, (b, g) in enumerate(units)]
        lg = [m_ + jnp.log(d_) for m_, d_ in zip(m, den)]
        for b in range(SWA_STEP):
            lse = jnp.zeros((ATTN_BLOCK, ATTN_HEADS), F32)
            outs = []
            for h in range(ATTN_HEADS):
                u = b * KV_HEADS + h // grp
                sub = blk(h % grp)
                outs.append(og[u][sub])
                lse = jnp.where(lane == h, lg[u][sub], lse)
            y = jnp.concatenate(outs, axis=-1)
            y_ref[blk(b), :] = y
            mix_ref[blk(b), :] = y.astype(mix_ref.dtype)
            yt_ref[:, blk(b)] = y.T.astype(yt_ref.dtype)
            lse_ref[blk(b), :] = lse

    cur = lambda n: (n, 0)
    prev = lambda n: (jnp.maximum(n * SWA_STEP - 1, 0), 0)
    vcol = (Q_W + KV_W) // KV_W
    return pl.pallas_call(
        body, name=name, grid=(nb // SWA_STEP,),
        in_specs=[pl.BlockSpec((rows, Q_W), cur), pl.BlockSpec((rows, KV_W), cur),
                  pl.BlockSpec((ATTN_BLOCK, KV_W), prev),
                  pl.BlockSpec((rows, KV_W), lambda n: (n, vcol)),
                  pl.BlockSpec((ATTN_BLOCK, KV_W), lambda n: (jnp.maximum(n * SWA_STEP - 1, 0), vcol)),
                  pl.BlockSpec((1, ATTN_HEADS), lambda n: (0, 0))],
        out_specs=(pl.BlockSpec((rows, Q_W), cur), pl.BlockSpec((rows, Q_W), cur),
                   pl.BlockSpec((Q_W, rows), lambda n: (0, n)), pl.BlockSpec((rows, ATTN_HEADS), cur)),
        out_shape=(jax.ShapeDtypeStruct((t, Q_W), F32), jax.ShapeDtypeStruct((t, Q_W + CONV_CH), MXU_DTYPE),
                   jax.ShapeDtypeStruct((Q_W + CONV_CH, t), MXU_DTYPE), jax.ShapeDtypeStruct((t, ATTN_HEADS), F32)),
        compiler_params=_cp("parallel"))(q, k, k, proj, proj, sinks)


def swa_bwd(q, k, proj, sinks, y, lse, dmix, into, name):
    t = q.shape[0]
    nb = t // ATTN_BLOCK
    scale = HEAD_DIM ** -0.5
    grp = ATTN_HEADS // KV_HEADS

    def body(q_ref, kc_ref, kp_ref, vc_ref, vp_ref, s_ref, y_ref, lse_ref, dy_ref, into_ref,
             dq_ref, dk_ref, dv_ref, ds_ref, dkc, dvc):
        n = pl.program_id(0)

        @pl.when(n == 0)
        def _():
            dkc[...] = jnp.zeros_like(dkc)
            dvc[...] = jnp.zeros_like(dvc)
            ds_ref[...] = jnp.zeros_like(ds_ref)

        @pl.when(n < nb)
        def _():
            valid = _swa_valid(n, grp)
            kk = jnp.concatenate([kp_ref[...], kc_ref[...]], axis=0).astype(MXU_DTYPE)
            vv = jnp.concatenate([vp_ref[...], vc_ref[...]], axis=0).astype(MXU_DTYPE)
            lane = lax.broadcasted_iota(jnp.int32, (1, ATTN_HEADS), 1)
            gs = range(KV_HEADS)
            kg = [kk[:, g * HEAD_DIM:(g + 1) * HEAD_DIM] for g in gs]
            vg = [vv[:, g * HEAD_DIM:(g + 1) * HEAD_DIM] for g in gs]
            qg = [_stack_heads(q_ref, g, grp).astype(MXU_DTYPE) for g in gs]
            dog = [_stack_heads(dy_ref, g, grp) for g in gs]
            og = [_stack_heads(y_ref, g, grp) for g in gs]
            lg = [jnp.concatenate([lse_ref[:, g * grp + j:g * grp + j + 1] for j in range(grp)], axis=0) for g in gs]
            sink = [_stack_sinks(s_ref, g, grp) for g in gs]
            sc = [jnp.where(valid, _mx_nt(qg[g], kg[g]) * scale, NEG) for g in gs]
            p = [jnp.exp(sc[g] - lg[g]) for g in gs]
            delta = [jnp.sum(dog[g] * og[g], axis=-1, keepdims=True) for g in gs]
            ds = [p[g] * (_mx_nt(dog[g], vg[g]) - delta[g]) for g in gs]
            dqg = [_mx(ds[g], kg[g]) * scale for g in gs]
            dkf = jnp.concatenate([_mx_tn(ds[g], qg[g]) * scale for g in gs], axis=-1)
            dvf = jnp.concatenate([_mx_tn(p[g], dog[g]) for g in gs], axis=-1)
            dsk = [jnp.exp(sink[g] - lg[g]) * delta[g] for g in gs]
            dsink = jnp.zeros((1, ATTN_HEADS), F32)
            dqs = []
            for h in range(ATTN_HEADS):
                rows = slice((h % grp) * ATTN_BLOCK, (h % grp + 1) * ATTN_BLOCK)
                dqs.append(dqg[h // grp][rows])
                dsink = jnp.where(lane == h, -jnp.sum(dsk[h // grp][rows], axis=0, keepdims=True), dsink)
            dq_ref[...] = jnp.concatenate(dqs, axis=-1)
            dk_ref[...] = dkc[...] + dkf[:ATTN_BLOCK]
            dv_ref[...] = (dvc[...] + dvf[:ATTN_BLOCK]).astype(dv_ref.dtype)
            dkc[...] = dkf[ATTN_BLOCK:]
            dvc[...] = dvf[ATTN_BLOCK:]
            ds_ref[...] += dsink

        @pl.when(n == nb)
        def _():
            dk_ref[...] = dkc[...]
            dv_ref[...] = dvc[...].astype(dv_ref.dtype)

    cur = lambda n: (jnp.minimum(n, nb - 1), 0)
    prev = lambda n: (jnp.clip(n - 1, 0, nb - 1), 0)
    vcol = (Q_W + KV_W) // KV_W
    return pl.pallas_call(
        body, name=name, grid=(nb + 1,),
        in_specs=[pl.BlockSpec((ATTN_BLOCK, Q_W), cur), pl.BlockSpec((ATTN_BLOCK, KV_W), cur),
                  pl.BlockSpec((ATTN_BLOCK, KV_W), prev),
                  pl.BlockSpec((ATTN_BLOCK, KV_W), lambda n: (jnp.minimum(n, nb - 1), vcol)),
                  pl.BlockSpec((ATTN_BLOCK, KV_W), lambda n: (jnp.clip(n - 1, 0, nb - 1), vcol)),
                  pl.BlockSpec((1, ATTN_HEADS), lambda n: (0, 0)),
                  pl.BlockSpec((ATTN_BLOCK, Q_W), cur), pl.BlockSpec((ATTN_BLOCK, ATTN_HEADS), cur),
                  pl.BlockSpec((ATTN_BLOCK, Q_W), cur), pl.BlockSpec(memory_space=pl.ANY)],
        out_specs=(pl.BlockSpec((ATTN_BLOCK, Q_W), cur), pl.BlockSpec((ATTN_BLOCK, KV_W), prev),
                   pl.BlockSpec((ATTN_BLOCK, KV_W), lambda n: (jnp.clip(n - 1, 0, nb - 1), vcol)),
                   pl.BlockSpec((1, ATTN_HEADS), lambda n: (0, 0))),
        out_shape=(jax.ShapeDtypeStruct((t, Q_W), F32), jax.ShapeDtypeStruct((t, KV_W), F32),
                   jax.ShapeDtypeStruct(into.shape, into.dtype), jax.ShapeDtypeStruct((1, ATTN_HEADS), F32)),
        scratch_shapes=[pltpu.VMEM((ATTN_BLOCK, KV_W), F32), pltpu.VMEM((ATTN_BLOCK, KV_W), F32)],
        input_output_aliases={9: 2},
        compiler_params=_cp("arbitrary"))(q, k, k, proj, proj, sinks, y, lse, dmix, into)


GC_W = 256
_GB0, _GC0, _XI0 = 768 // GC_W, 1280 // GC_W, 1792 // GC_W
HALO = 8


def gconv_fwd(proj, conv_w, mix, mix_t, name, tm=512):
    t = proj.shape[0]
    hb = tm // HALO
    half = Q_W // GC_W

    def body(gb_ref, gc_ref, xi_ref, gch_ref, xih_ref, w_ref, mix_in, mixt_in, y_ref, yt_ref):
        i = pl.program_id(1)
        u = gc_ref[...] * xi_ref[...]
        uh = jnp.where(i == 0, 0.0, gch_ref[...] * xih_ref[...])
        up = jnp.concatenate([uh, u], axis=0)
        cv = w_ref[0:1, :] * up[HALO - 2:HALO - 2 + tm]
        cv = cv + w_ref[1:2, :] * up[HALO - 1:HALO - 1 + tm]
        cv = cv + w_ref[2:3, :] * u
        y = gb_ref[...] * cv
        y_ref[...] = y.astype(y_ref.dtype)
        yt_ref[...] = y.T.astype(yt_ref.dtype)

    def col(c0):
        return pl.BlockSpec((tm, GC_W), lambda cj, i: (i, c0 + cj))

    def halo(c0):
        return pl.BlockSpec((HALO, GC_W), lambda cj, i: (jnp.maximum(i * hb - 1, 0), c0 + cj))

    return pl.pallas_call(
        body, name=name, grid=(CONV_CH // GC_W, t // tm),
        in_specs=[col(_GB0), col(_GC0), col(_XI0), halo(_GC0), halo(_XI0),
                  pl.BlockSpec((3, GC_W), lambda cj, i: (0, cj)),
                  pl.BlockSpec(memory_space=pl.ANY), pl.BlockSpec(memory_space=pl.ANY)],
        out_specs=(pl.BlockSpec((tm, GC_W), lambda cj, i: (i, half + cj)),
                   pl.BlockSpec((GC_W, tm), lambda cj, i: (half + cj, i))),
        out_shape=(jax.ShapeDtypeStruct(mix.shape, mix.dtype), jax.ShapeDtypeStruct(mix_t.shape, mix_t.dtype)),
        input_output_aliases={6: 0, 7: 1},
        compiler_params=_cp("parallel", "parallel"))(proj, proj, proj, proj, proj, conv_w, mix, mix_t)


def gconv_bwd(proj, conv_w, dmix, into, name, tm=512):
    t = proj.shape[0]
    hb = tm // HALO
    nt = t // tm
    dy0 = Q_W // GC_W

    def body(gb_ref, gc_ref, xi_ref, gch_ref, xih_ref, gbn_ref, dyn_ref, dy_ref, w_ref, into_ref, o_ref, dw_ref):
        dgb_ref, dgc_ref, dxi_ref = (o_ref.at[:, j * GC_W:(j + 1) * GC_W] for j in range(3))
        i = pl.program_id(1)
        gc, xi, gb, dy = gc_ref[...], xi_ref[...], gb_ref[...], dy_ref[...]
        u = gc * xi
        uh = jnp.where(i == 0, 0.0, gch_ref[...] * xih_ref[...])
        up = jnp.concatenate([uh, u], axis=0)
        u2 = up[HALO - 2:HALO - 2 + tm]
        u1 = up[HALO - 1:HALO - 1 + tm]
        cv = w_ref[0:1, :] * u2 + w_ref[1:2, :] * u1 + w_ref[2:3, :] * u
        dgb_ref[...] = (dy * cv).astype(dgb_ref.dtype)
        dcv = dy * gb
        dcvn = jnp.where(i == nt - 1, 0.0, dyn_ref[...] * gbn_ref[...])
        dcvp = jnp.concatenate([dcv, dcvn], axis=0)
        du = w_ref[0:1, :] * dcvp[2:2 + tm] + w_ref[1:2, :] * dcvp[1:1 + tm] + w_ref[2:3, :] * dcv
        dgc_ref[...] = (du * xi).astype(dgc_ref.dtype)
        dxi_ref[...] = (du * gc).astype(dxi_ref.dtype)
        dw = jnp.concatenate([jnp.sum(dcv * u2, axis=0, keepdims=True), jnp.sum(dcv * u1, axis=0, keepdims=True),
                              jnp.sum(dcv * u, axis=0, keepdims=True)], axis=0)

        @pl.when(i == 0)
        def _():
            dw_ref[...] = dw

        @pl.when(i > 0)
        def _():
            dw_ref[...] += dw

    def col(c0):
        return pl.BlockSpec((tm, GC_W), lambda cj, i: (i, c0 + cj))

    def halo(c0):
        return pl.BlockSpec((HALO, GC_W), lambda cj, i: (jnp.maximum(i * hb - 1, 0), c0 + cj))

    def nxt(c0):
        return pl.BlockSpec((HALO, GC_W), lambda cj, i: (jnp.minimum((i + 1) * hb, t // HALO - 1), c0 + cj))

    return pl.pallas_call(
        body, name=name, grid=(CONV_CH // GC_W, nt),
        in_specs=[col(_GB0), col(_GC0), col(_XI0), halo(_GC0), halo(_XI0), nxt(_GB0), nxt(dy0), col(dy0),
                  pl.BlockSpec((3, GC_W), lambda cj, i: (0, cj)), pl.BlockSpec(memory_space=pl.ANY)],
        out_specs=(pl.BlockSpec((tm, 3 * GC_W), lambda cj, i: (i, 1 + cj)),
                   pl.BlockSpec((3, GC_W), lambda cj, i: (0, cj))),
        out_shape=(jax.ShapeDtypeStruct(into.shape, into.dtype), jax.ShapeDtypeStruct((3, CONV_CH), F32)),
        input_output_aliases={9: 0},
        compiler_params=_cp("parallel", "arbitrary"))(proj, proj, proj, proj, proj, proj, dmix, dmix, conv_w, into)


_EVEN_D_CHUNKS = ((0, 0, 768),) + tuple(
    (768 + (3 * j + part) * GC_W, 768 + part * CONV_CH + j * GC_W, GC_W)
    for j in range(CONV_CH // GC_W) for part in range(3))


_QKV_W = 3 * DN_W
_BA_COL = (4 * DN_W) // 128
_Z_COL = _QKV_W // DN_W


def gdn_prep_fwd(proj, conv_w, alog_row, dtb_row, name, tm=256):
    t = proj.shape[0]
    hb = tm // HALO
    nt = t // tm
    qscale = DN_DIM ** -0.5
    ring = 3

    def body(x_hbm, xh_ref, w_ref, ba_ref, al_ref, dt_ref, q_ref, k_ref, v_ref, bg_ref, c_ref, xbuf, sem):
        i = pl.program_id(0)

        def fetch(step):
            slot = step % ring
            return pltpu.make_async_copy(x_hbm.at[pl.ds(step * tm, tm), pl.ds(0, _QKV_W)], xbuf.at[slot], sem.at[slot])

        @pl.when(i == 0)
        def _():
            for s in range(ring - 1):
                fetch(s).start()

        @pl.when(i + ring - 1 < nt)
        def _():
            fetch(i + ring - 1).start()

        fetch(i).wait()
        x_ref = xbuf.at[i % ring]
        for gi in range(3 * DN_HEADS):
            sl = slice(gi * DN_DIM, (gi + 1) * DN_DIM)
            xp = jnp.concatenate([jnp.where(i == 0, 0.0, xh_ref[:, sl]), x_ref[:, sl]], axis=0)
            c = w_ref[0:1, sl] * xp[HALO - 3:HALO - 3 + tm]
            for j in range(1, 4):
                c = c + w_ref[j:j + 1, sl] * xp[HALO - 3 + j:HALO - 3 + j + tm]
            c_ref[:, sl] = c
            s = c * _sigmoid(c)
            osl = slice((gi % DN_HEADS) * DN_DIM, (gi % DN_HEADS + 1) * DN_DIM)
            if gi < DN_HEADS:
                q_ref[:, osl] = s * lax.rsqrt(jnp.sum(s * s, axis=-1, keepdims=True) + EPS) * qscale
            elif gi < 2 * DN_HEADS:
                k_ref[:, osl] = s * lax.rsqrt(jnp.sum(s * s, axis=-1, keepdims=True) + EPS)
            else:
                v_ref[:, osl] = s
        ba = ba_ref[...]
        lane = lax.broadcasted_iota(jnp.int32, ba.shape, 1)
        gval = -jnp.exp(al_ref[...]) * _softplus(ba + dt_ref[...])
        bg_ref[...] = jnp.where(lane < DN_HEADS, _sigmoid(ba), jnp.where(lane < 2 * DN_HEADS, gval, 0.0))

    row = pl.BlockSpec((tm, DN_W), lambda i: (i, 0))
    one = pl.BlockSpec((1, 128), lambda i: (0, 0))
    return pl.pallas_call(
        body, name=name, grid=(nt,),
        in_specs=[pl.BlockSpec(memory_space=pl.ANY),
                  pl.BlockSpec((HALO, _QKV_W), lambda i: (jnp.maximum(i * hb - 1, 0), 0)),
                  pl.BlockSpec((4, _QKV_W), lambda i: (0, 0)),
                  pl.BlockSpec((tm, 128), lambda i: (i, _BA_COL)), one, one],
        out_specs=(row, row, row, pl.BlockSpec((tm, 128), lambda i: (i, 0)), pl.BlockSpec((tm, _QKV_W), lambda i: (i, 0))),
        out_shape=(jax.ShapeDtypeStruct((t, DN_W), F32),) * 3 + (jax.ShapeDtypeStruct((t, 128), F32),
                                                                 jax.ShapeDtypeStruct((t, _QKV_W), F32)),
        scratch_shapes=[pltpu.VMEM((ring, tm, _QKV_W), F32), pltpu.SemaphoreType.DMA((ring,))],
        compiler_params=_cp("arbitrary"))(proj, proj, conv_w, proj, alog_row, dtb_row)


def gdn_prep_bwd(proj, conv, conv_w, alog_row, dtb_row, dq, dk, dv, dbg, dz, name, tm=256):
    t = proj.shape[0]
    hb = tm // HALO
    nt = t // tm
    qscale = DN_DIM ** -0.5
    te = tm + HALO

    def body(x_ref, c_ref, cn_ref, w_ref, ba_ref, al_ref, dt_ref, dq_ref, dk_ref, dv_ref,
             dqn_ref, dkn_ref, dvn_ref, dbg_ref, dz_ref, dx_ref, dw_ref, ddt_ref, dal_ref):
        i = pl.program_id(0)
        first = i == 0
        last = i == nt - 1
        dws = []
        for gi in range(3 * DN_HEADS):
            sl = slice(gi * DN_DIM, (gi + 1) * DN_DIM)
            osl = slice((gi % DN_HEADS) * DN_DIM, (gi % DN_HEADS + 1) * DN_DIM)
            c = jnp.concatenate([c_ref[:, sl], cn_ref[:, sl]], axis=0)
            sg = _sigmoid(c)
            s = c * sg
            d_ref, dn_ref = ((dq_ref, dqn_ref), (dk_ref, dkn_ref), (dv_ref, dvn_ref))[gi // DN_HEADS]
            dy = jnp.concatenate([d_ref[:, osl], jnp.where(last, 0.0, dn_ref[:, osl])], axis=0)
            if gi < 2 * DN_HEADS:
                r = lax.rsqrt(jnp.sum(s * s, axis=-1, keepdims=True) + EPS)
                sh = s * r
                ds = r * (dy - sh * jnp.sum(sh * dy, axis=-1, keepdims=True))
                if gi < DN_HEADS:
                    ds = ds * qscale
            else:
                ds = dy
            dc = ds * sg * (1.0 + c * (1.0 - sg))
            dcs = [dc[3 - j:3 - j + tm] for j in range(4)]
            dx = w_ref[0:1, sl] * dcs[0]
            for j in range(1, 4):
                dx = dx + w_ref[j:j + 1, sl] * dcs[j]
            dx_ref[:, sl] = dx.astype(dx_ref.dtype)
            x0 = x_ref[:, sl]
            dws.append(jnp.concatenate([jnp.sum(dcs[j] * x0, axis=0, keepdims=True) for j in range(4)], axis=0))
        dw = jnp.concatenate(dws, axis=-1)
        ba = ba_ref[...]
        dbgv = dbg_ref[...]
        lane = lax.broadcasted_iota(jnp.int32, ba.shape, 1)
        beta = _sigmoid(ba)
        ea = -jnp.exp(al_ref[...])
        zin = ba + dt_ref[...]
        is_b = lane < DN_HEADS
        is_a = (lane >= DN_HEADS) & (lane < 2 * DN_HEADS)
        da = jnp.where(is_a, dbgv * ea * _sigmoid(zin), 0.0)
        dx_ref[:, _QKV_W:_QKV_W + DN_W] = dz_ref[...]
        dx_ref[:, _QKV_W + DN_W:] = jnp.where(is_b, dbgv * beta * (1.0 - beta), da).astype(dx_ref.dtype)
        ddt = jnp.sum(da, axis=0, keepdims=True)
        dal = jnp.sum(jnp.where(is_a, dbgv * ea * _softplus(zin), 0.0), axis=0, keepdims=True)

        @pl.when(first)
        def _():
            dw_ref[...] = dw
            ddt_ref[...] = ddt
            dal_ref[...] = dal

        @pl.when(i > 0)
        def _():
            dw_ref[...] += dw
            ddt_ref[...] += ddt
            dal_ref[...] += dal

    row = pl.BlockSpec((tm, DN_W), lambda i: (i, 0))
    nrow = pl.BlockSpec((HALO, DN_W), lambda i: (jnp.minimum((i + 1) * hb, t // HALO - 1), 0))
    one = pl.BlockSpec((1, 128), lambda i: (0, 0))
    return pl.pallas_call(
        body, name=name, grid=(nt,),
        in_specs=[pl.BlockSpec((tm, _QKV_W), lambda i: (i, 0)), pl.BlockSpec((tm, _QKV_W), lambda i: (i, 0)),
                  pl.BlockSpec((HALO, _QKV_W), lambda i: (jnp.minimum((i + 1) * hb, t // HALO - 1), 0)),
                  pl.BlockSpec((4, _QKV_W), lambda i: (0, 0)),
                  pl.BlockSpec((tm, 128), lambda i: (i, _BA_COL)), one, one,
                  row, row, row, nrow, nrow, nrow, pl.BlockSpec((tm, 128), lambda i: (i, 0)), row],
        out_specs=(pl.BlockSpec((tm, ODD_IN_PAD), lambda i: (i, 0)), pl.BlockSpec((4, _QKV_W), lambda i: (0, 0)), one, one),
        out_shape=(jax.ShapeDtypeStruct((t, ODD_IN_PAD), MXU_DTYPE), jax.ShapeDtypeStruct((4, _QKV_W), F32),
                   jax.ShapeDtypeStruct((1, 128), F32), jax.ShapeDtypeStruct((1, 128), F32)),
        compiler_params=_cp("arbitrary"))(proj, conv, conv, conv_w, proj, alog_row, dtb_row, dq, dk, dv, dq, dk, dv, dbg,
                                          dz)


def _chunk_masks():
    r = lax.broadcasted_iota(jnp.int32, (DN_CHUNK, DN_CHUNK), 0)
    c = lax.broadcasted_iota(jnp.int32, (DN_CHUNK, DN_CHUNK), 1)
    return r >= c, r > c


INV_PACK = 2


def _inv_unit_lower_many(mats):
    n = DN_CHUNK
    wide = INV_PACK * n
    r = lax.broadcasted_iota(jnp.int32, (wide, wide), 0)
    c = lax.broadcasted_iota(jnp.int32, (wide, wide), 1)
    same = (r & -n) == (c & -n)
    eye = jnp.where((r[:n] == (c[:n] & (n - 1))), 1.0, 0.0)

    def blockdiag(row):
        return jnp.where(same, jnp.concatenate([row] * INV_PACK, axis=0), 0.0)

    packs = [jnp.concatenate(mats[g:g + INV_PACK], axis=-1) for g in range(0, len(mats), INV_PACK)]
    xs = [eye - a for a in packs]
    pws = [_hi(a, blockdiag(a)) for a in packs]
    for step in range(5):
        if step < 4:
            both = [_hi(jnp.concatenate([x, pw], axis=0), blockdiag(pw)) for x, pw in zip(xs, pws)]
            xs = [x + b[:n] for x, b in zip(xs, both)]
            pws = [b[n:] for b in both]
        else:
            xs = [x + _hi(x, blockdiag(pw)) for x, pw in zip(xs, pws)]
    return [x[:, j * n:(j + 1) * n] for x in xs for j in range(INV_PACK)]


def _chunk_common(q, k, beta, gc, gcr, lower):
    gam = jnp.exp(jnp.where(lower, gc - gcr, NEG))
    eg = jnp.exp(gc)
    gl = gc[DN_CHUNK - 1:DN_CHUNK, :]
    kdf = jnp.exp(gl - gc)
    kb = k * beta
    bmat = _mx_nt(kb, k)
    qmat = _mx_nt(q, k)
    return gam, eg, jnp.exp(gl), kdf, kb, bmat, qmat


DN_STEP = 4


def gdn_fwd(q, k, v, bg, proj, o_gain, name):
    t = q.shape[0]
    n_chunks = t // DN_CHUNK

    def body(q_ref, k_ref, v_ref, bg_ref, z_ref, g_ref, o_ref, sall_ref, tall_ref, y_ref, yt_ref, s_ref):
        n = pl.program_id(0)

        @pl.when(n == 0)
        def _():
            s_ref[...] = jnp.zeros_like(s_ref)

        lower, strict = _chunk_masks()
        ltri = jnp.where(lower, 1.0, 0.0)
        hs = range(DN_HEADS)
        sl = [slice(h * DN_DIM, (h + 1) * DN_DIM) for h in hs]
        units = [(c, h) for c in range(DN_STEP) for h in hs]
        nu = range(len(units))
        rs = [slice(c * DN_CHUNK, (c + 1) * DN_CHUNK) for c in range(DN_STEP)]
        bgv = [bg_ref[rs[c], :] for c in range(DN_STEP)]
        gcs = [_hi(ltri, b) for b in bgv]
        gcs_t = [g.T for g in gcs]
        qh = [q_ref[rs[c], sl[h]] for c, h in units]
        kh = [k_ref[rs[c], sl[h]] for c, h in units]
        vh = [v_ref[rs[c], sl[h]] for c, h in units]
        beta = [bgv[c][:, h:h + 1] for c, h in units]
        com = [_chunk_common(qh[u], kh[u], beta[u], gcs[c][:, DN_HEADS + h:DN_HEADS + h + 1],
                             gcs_t[c][DN_HEADS + h:DN_HEADS + h + 1, :], lower) for u, (c, h) in enumerate(units)]
        gam, eg, dec, kdf, kb, bmat, qmat = zip(*com)
        tms = _inv_unit_lower_many([jnp.where(strict, bmat[u] * gam[u], 0.0) for u in nu])
        for u, (c, h) in enumerate(units):
            tall_ref[c, h] = tms[u]
        uw = [_hi(tms[u], jnp.concatenate([vh[u] * beta[u], kb[u] * eg[u]], axis=-1)) for u in nu]
        qd = [qh[u] * eg[u] for u in nu]
        pm = [qmat[u] * gam[u] for u in nu]
        kd = [kh[u] * kdf[u] for u in nu]
        st = [s_ref[h] for h in hs]
        for c in range(DN_STEP):
            us = [c * DN_HEADS + h for h in hs]
            for h in hs:
                sall_ref[c, h] = st[h]
            v_new = [uw[us[h]][:, :DN_DIM] - _mx(uw[us[h]][:, DN_DIM:], st[h]) for h in hs]
            o_st = [_mx(qd[us[h]], st[h]) for h in hs]
            o_in = [_mx(pm[us[h]], v_new[h]) for h in hs]
            s_up = [_mx_tn(kd[us[h]], v_new[h]) for h in hs]
            for h in hs:
                ov = o_st[h] + o_in[h]
                o_ref[rs[c], sl[h]] = ov
                zv = z_ref[rs[c], sl[h]]
                y = ov * lax.rsqrt(jnp.mean(ov * ov, axis=-1, keepdims=True) + EPS) * g_ref[...] * (zv * _sigmoid(zv))
                y_ref[rs[c], sl[h]] = y.astype(y_ref.dtype)
                yt_ref[sl[h], rs[c]] = y.T.astype(yt_ref.dtype)
            st = [st[h] * dec[us[h]] + s_up[h] for h in hs]
        for h in hs:
            s_ref[h] = st[h]

    rows = DN_STEP * DN_CHUNK
    row = pl.BlockSpec((rows, DN_W), lambda n: (n, 0))
    return pl.pallas_call(
        body, name=name, grid=(n_chunks // DN_STEP,),
        in_specs=[row, row, row, pl.BlockSpec((rows, 128), lambda n: (n, 0)),
                  pl.BlockSpec((rows, DN_W), lambda n: (n, _Z_COL)), pl.BlockSpec((1, DN_DIM), lambda n: (0, 0))],
        out_specs=(row, pl.BlockSpec((DN_STEP, DN_HEADS, DN_DIM, DN_DIM), lambda n: (n, 0, 0, 0)),
                   pl.BlockSpec((DN_STEP, DN_HEADS, DN_CHUNK, DN_CHUNK), lambda n: (n, 0, 0, 0)),
                   row, pl.BlockSpec((DN_W, rows), lambda n: (0, n))),
        out_shape=(jax.ShapeDtypeStruct((t, DN_W), F32),
                   jax.ShapeDtypeStruct((n_chunks, DN_HEADS, DN_DIM, DN_DIM), F32),
                   jax.ShapeDtypeStruct((n_chunks, DN_HEADS, DN_CHUNK, DN_CHUNK), F32),
                   jax.ShapeDtypeStruct((t, DN_W), MXU_DTYPE), jax.ShapeDtypeStruct((DN_W, t), MXU_DTYPE)),
        scratch_shapes=[pltpu.VMEM((DN_HEADS, DN_DIM, DN_DIM), F32)],
        compiler_params=_cp("arbitrary"))(q, k, v, bg, proj, o_gain)


def gdn_bwd(q, k, v, bg, sall, tall, do, name):
    t = q.shape[0]
    n_chunks = t // DN_CHUNK

    def body(q_ref, k_ref, v_ref, bg_ref, sall_ref, tall_ref, do_ref, dq_ref, dk_ref, dv_ref, dbg_ref, ds_ref):
        n = pl.program_id(0)

        @pl.when(n == 0)
        def _():
            ds_ref[...] = jnp.zeros_like(ds_ref)

        lower, strict = _chunk_masks()
        ltri = jnp.where(lower, 1.0, 0.0)
        bgv = bg_ref[...]
        gcs = _hi(ltri, bgv)
        gcs_t = gcs.T
        lane = lax.broadcasted_iota(jnp.int32, (DN_CHUNK, 128), 1)
        rowi = lax.broadcasted_iota(jnp.int32, (DN_CHUNK, 1), 0)
        hs = range(DN_HEADS)
        each = lambda fn, *ls: [fn(*a) for a in zip(*ls)]
        rsum = lambda a: jnp.sum(a, axis=-1, keepdims=True)
        sl = [slice(h * DN_DIM, (h + 1) * DN_DIM) for h in hs]
        st = [sall_ref[0, h] for h in hs]
        tms = [tall_ref[0, h] for h in hs]
        dsn = [ds_ref[h] for h in hs]
        qh = [q_ref[:, sl[h]] for h in hs]
        kh = [k_ref[:, sl[h]] for h in hs]
        vh = [v_ref[:, sl[h]] for h in hs]
        doh = [do_ref[:, sl[h]] for h in hs]
        beta = [bgv[:, h:h + 1] for h in hs]
        com = [_chunk_common(qh[h], kh[h], beta[h], gcs[:, DN_HEADS + h:DN_HEADS + h + 1],
                             gcs_t[DN_HEADS + h:DN_HEADS + h + 1, :], lower) for h in hs]
        gam, eg, dec, kdf, kb, bmat, qmat = zip(*com)
        rhs_w = each(lambda a, b: a * b, kb, eg)
        uw = each(lambda t_, v_, b_, r_: _hi(t_, jnp.concatenate([v_ * b_, r_], axis=-1)), tms, vh, beta, rhs_w)
        qd = each(lambda a, b: a * b, qh, eg)
        kd = each(lambda a, b: a * b, kh, kdf)
        pmat = each(lambda a, b: a * b, qmat, gam)
        v_new = each(lambda uw_, s_: uw_[:, :DN_DIM] - _mx(uw_[:, DN_DIM:], s_), uw, st)
        dqd = each(_mx_nt, doh, st)
        ds_o = each(_mx_tn, qd, doh)
        dp = each(lambda d_, v_: jnp.where(lower, _mx_nt(d_, v_), 0.0), doh, v_new)
        dvn_o = each(_mx_tn, pmat, doh)
        ddec = each(lambda d_, s_: jnp.sum(rsum(d_ * s_), axis=0, keepdims=True), dsn, st)
        dkd = each(_mx_nt, v_new, dsn)
        dvn = each(lambda a, k_, d_: a + _mx(k_, d_), dvn_o, kd, dsn)
        dw = each(lambda d_, s_: -_mx_nt(d_, s_), dvn, st)
        ds_w = each(lambda uw_, d_: _mx_tn(uw_[:, DN_DIM:], d_), uw, dvn)
        for h in hs:
            ds_ref[h] = ds_o[h] + dec[h] * dsn[h] - ds_w[h]
        dr = each(lambda t_, a, b: _hi_tn(t_, jnp.concatenate([a, b], axis=-1)), tms, dvn, dw)
        da = each(lambda r_, uw_: jnp.where(strict, -_hi_nt(r_, uw_), 0.0), dr, uw)
        dru = [r_[:, :DN_DIM] for r_ in dr]
        drw = [r_[:, DN_DIM:] for r_ in dr]
        db = each(lambda a, b: a * b, da, gam)
        dq_m = each(lambda a, b: a * b, dp, gam)
        e = each(lambda a, bm, p_, qm, g_: (a * bm + p_ * qm) * g_, da, bmat, dp, qmat, gam)
        dkb = each(lambda b_, k_, r_, e_: _mx(b_, k_) + r_ * e_, db, kh, drw, eg)
        dk = each(lambda b_, kb_, m_, q_, d_, f_: _mx_tn(b_, kb_) + _mx_tn(m_, q_) + d_ * f_, db, kb, dq_m, qh, dkd, kdf)
        dq = each(lambda m_, k_, d_, e_: _mx(m_, k_) + d_ * e_, dq_m, kh, dqd, eg)
        tk = each(lambda a, b: rsum(a * b), dkd, kd)
        dbeta_all = jnp.zeros((DN_CHUNK, 128), F32)
        dgc_all = jnp.zeros((DN_CHUNK, 128), F32)
        for h in hs:
            dgc = (jnp.sum(e[h], axis=1, keepdims=True) - jnp.sum(e[h].T, axis=1, keepdims=True)
                   + rsum(dqd[h] * qd[h]) - tk[h] + rsum(drw[h] * rhs_w[h]))
            dgl = jnp.sum(tk[h], axis=0, keepdims=True) + ddec[h] * dec[h]
            dgc = dgc + jnp.where(rowi == DN_CHUNK - 1, dgl, 0.0)
            dbeta = rsum(dru[h] * vh[h]) + rsum(dkb[h] * kh[h])
            dq_ref[:, sl[h]] = dq[h]
            dk_ref[:, sl[h]] = dk[h] + dkb[h] * beta[h]
            dv_ref[:, sl[h]] = dru[h] * beta[h]
            dbeta_all = jnp.where(lane == h, dbeta, dbeta_all)
            dgc_all = jnp.where(lane == DN_HEADS + h, dgc, dgc_all)
        dbg_ref[...] = dbeta_all + _hi_tn(ltri, dgc_all)

    rev = lambda n: (n_chunks - 1 - n, 0)
    row = pl.BlockSpec((DN_CHUNK, DN_W), rev)
    small = pl.BlockSpec((DN_CHUNK, 128), rev)
    return pl.pallas_call(
        body, name=name, grid=(n_chunks,),
        in_specs=[row, row, row, small,
                  pl.BlockSpec((1, DN_HEADS, DN_DIM, DN_DIM), lambda n: (n_chunks - 1 - n, 0, 0, 0)),
                  pl.BlockSpec((1, DN_HEADS, DN_CHUNK, DN_CHUNK), lambda n: (n_chunks - 1 - n, 0, 0, 0)), row],
        out_specs=(row, row, row, small),
        out_shape=(jax.ShapeDtypeStruct((t, DN_W), F32),) * 3 + (jax.ShapeDtypeStruct((t, 128), F32),),
        scratch_shapes=[pltpu.VMEM((DN_HEADS, DN_DIM, DN_DIM), F32)],
        compiler_params=_cp("arbitrary"))(q, k, v, bg, sall, tall, do)


def gdn_out_bwd(o, proj, o_gain, dx, w_out, name, tm=512, after=None):
    t = o.shape[0]
    tm = min(tm, t)

    def body(o_ref, z_ref, g_ref, dx_ref, w_ref, *rest):
        do_ref, dz_ref, dg_ref = rest[-3:]
        i = pl.program_id(0)
        dy = _mx_nt(dx_ref[...], w_ref[...])
        dg = jnp.zeros((1, DN_DIM), F32)
        for h in range(DN_HEADS):
            sl = slice(h * DN_DIM, (h + 1) * DN_DIM)
            ov, zv, dyv = o_ref[:, sl], z_ref[:, sl], dy[:, sl]
            r = lax.rsqrt(jnp.mean(ov * ov, axis=-1, keepdims=True) + EPS)
            oh = ov * r
            sg = _sigmoid(zv)
            dz_ref[:, sl] = (dyv * oh * g_ref[...] * sg * (1.0 + zv * (1.0 - sg))).astype(dz_ref.dtype)
            don = dyv * (zv * sg)
            dg = dg + jnp.sum(don * oh, axis=0, keepdims=True)
            doh = don * g_ref[...]
            do_ref[:, sl] = r * (doh - oh * jnp.mean(doh * oh, axis=-1, keepdims=True))

        @pl.when(i == 0)
        def _():
            dg_ref[...] = dg

        @pl.when(i > 0)
        def _():
            dg_ref[...] += dg

    row = pl.BlockSpec((tm, DN_W), lambda i: (i, 0))
    one = pl.BlockSpec((1, DN_DIM), lambda i: (0, 0))
    in_specs = [row, pl.BlockSpec((tm, DN_W), lambda i: (i, _Z_COL)), one,
                pl.BlockSpec((tm, dx.shape[1]), lambda i: (i, 0)), pl.BlockSpec(w_out.shape, lambda i: (0, 0))]
    args = [o, proj, o_gain, dx, w_out]
    if after is not None:
        in_specs.append(pl.BlockSpec(memory_space=pl.ANY))
        args.append(after)
    return pl.pallas_call(
        body, name=name, grid=(t // tm,), in_specs=in_specs, out_specs=(row, row, one),
        out_shape=(jax.ShapeDtypeStruct((t, DN_W), F32), jax.ShapeDtypeStruct((t, DN_W), MXU_DTYPE),
                   jax.ShapeDtypeStruct((1, DN_DIM), F32)),
        compiler_params=_cp("arbitrary"))(*args)


def _peer(k):
    x, y, c = lax.axis_index("x"), lax.axis_index("y"), lax.axis_index("c")
    px = 1 - x if k & 4 else x
    py = 1 - y if k & 2 else y
    pc = 1 - c if k & 1 else c
    return (px, py, pc), 4 * px + 2 * py + pc


_HBM = pl.BlockSpec(memory_space=pltpu.HBM)
_SEM = pl.BlockSpec(memory_space=pltpu.SEMAPHORE)
_DATAFLOW = pltpu.SideEffectType.DATAFLOW_SIDE_EFFECTING
N_PEER = N_DEV - 1


def join_blocks(parts, rows, name, tc=256):
    n, r, c = parts.shape

    def body(p_ref, o_ref):
        for j in range(n):
            o_ref[j * r:(j + 1) * r, :] = p_ref[j]
        if rows > n * r:
            o_ref[n * r:, :] = jnp.zeros((rows - n * r, tc), o_ref.dtype)

    return pl.pallas_call(
        body, name=name, grid=(c // tc,), in_specs=[pl.BlockSpec((n, r, tc), lambda i: (0, 0, i))],
        out_specs=pl.BlockSpec((rows, tc), lambda i: (0, i)), out_shape=jax.ShapeDtypeStruct((rows, c), parts.dtype),
        compiler_params=_cp("parallel"))(parts)


def split_blocks(whole, n, r, name, tc=256):
    rows, c = whole.shape

    def body(w_ref, o_ref):
        for j in range(n):
            o_ref[j] = w_ref[j * r:(j + 1) * r, :]

    return pl.pallas_call(
        body, name=name, grid=(c // tc,), in_specs=[pl.BlockSpec((rows, tc), lambda i: (0, i))],
        out_specs=pl.BlockSpec((n, r, tc), lambda i: (0, 0, i)), out_shape=jax.ShapeDtypeStruct((n, r, c), whole.dtype),
        compiler_params=_cp("parallel"))(whole)


_ALL_PEERS = tuple(range(1, N_DEV))
_OTHER_CHIPS = (2, 4, 6)
_SIBLING = 1


def send_start(srcs, name, scatter, after, masks=_ALL_PEERS):
    na, n_peer = len(srcs), len(masks)
    ns = (2 * n_peer + 1) * na
    lands = [lax.empty((N_DEV,) + (s.shape[1:] if scatter else s.shape), s.dtype) for s in srcs]
    extra = [] if after is None else [after]

    def body(*refs):
        src_refs, land_refs = refs[:na], refs[na:2 * na]
        sems = refs[2 * na + len(extra):2 * na + len(extra) + ns]
        land_out, token = refs[-1 - na:-1], refs[-1]
        _, me = _peer(0)
        for a in range(na):
            pltpu.make_async_copy(src_refs[a].at[me] if scatter else src_refs[a], land_out[a].at[me],
                                  sems[2 * n_peer * na + a]).start()
        peers = [_peer(k) for k in masks]
        for a in range(na):
            for k, (peer, pid) in enumerate(peers):
                pltpu.make_async_remote_copy(
                    src_ref=src_refs[a].at[pid] if scatter else src_refs[a], dst_ref=land_refs[a].at[me],
                    send_sem=sems[2 * (a * n_peer + k)], recv_sem=sems[2 * (a * n_peer + k) + 1],
                    device_id=peer, device_id_type=MESH).start()
        token[...] = jnp.zeros_like(token)

    hbm = lambda arrs: tuple(pltpu.HBM(a.shape, a.dtype) for a in arrs)
    outs = pl.pallas_call(
        body, name=name,
        out_shape=(pltpu.SemaphoreType.DMA(()),) * ns + hbm(srcs) + hbm(lands) + (jax.ShapeDtypeStruct((8, 128), F32),),
        in_specs=[_HBM] * (2 * na) + [pl.BlockSpec(memory_space=pl.ANY)] * len(extra),
        out_specs=(_SEM,) * ns + (_HBM,) * (2 * na) + (pl.BlockSpec(memory_space=pltpu.VMEM),),
        input_output_aliases={i: ns + i for i in range(2 * na)},
        compiler_params=pltpu.CompilerParams(has_side_effects=_DATAFLOW),
    )(*[pltpu.with_memory_space_constraint(a, pltpu.HBM) for a in list(srcs) + lands], *extra)
    return outs[:ns], outs[ns:ns + na], outs[ns + na:ns + 2 * na], outs[-1]


def send_wait(sems, srcs_thru, lands_thru, name, scatter, after, masks=_ALL_PEERS):
    na, n_peer = len(srcs_thru), len(masks)
    ns = (2 * n_peer + 1) * na

    def body(*refs):
        src_refs, land_refs, sm = refs[:na], refs[na:2 * na], refs[2 * na:2 * na + ns]
        _, me = _peer(0)
        for a in range(na):
            pltpu.make_async_copy(src_refs[a].at[me] if scatter else src_refs[a], land_refs[a].at[me],
                                  sm[2 * n_peer * na + a]).wait()
        for k, mask in enumerate(masks):
            peer, pid = _peer(mask)
            for a in range(na):
                cp = pltpu.make_async_remote_copy(
                    src_ref=src_refs[a].at[pid] if scatter else src_refs[a], dst_ref=land_refs[a].at[pid],
                    send_sem=sm[2 * (a * n_peer + k)], recv_sem=sm[2 * (a * n_peer + k) + 1],
                    device_id=peer, device_id_type=MESH)
                cp.wait_send()
                cp.wait_recv()

    hbm = lambda arrs: tuple(pltpu.HBM(a.shape, a.dtype) for a in arrs)
    outs = pl.pallas_call(
        body, name=name, out_shape=hbm(srcs_thru) + hbm(lands_thru),
        in_specs=[_HBM] * (2 * na) + [_SEM] * ns + [pl.BlockSpec(memory_space=pl.ANY)], out_specs=(_HBM,) * (2 * na),
        input_output_aliases={i: i for i in range(2 * na)},
        compiler_params=pltpu.CompilerParams(has_side_effects=_DATAFLOW),
    )(*srcs_thru, *lands_thru, *sems, after)
    return outs[na:]


def relay_blocks(lands, name, masks=_OTHER_CHIPS):
    na, n_slot = len(lands), 1 + len(masks)

    def body(*refs):
        land_refs, rs = refs[:na], refs[2 * na:]
        sibling, _ = _peer(_SIBLING)
        copies = [pltpu.make_async_remote_copy(
            src_ref=land_refs[a].at[_peer(mask)[1]], dst_ref=land_refs[a].at[_peer(mask)[1]],
            send_sem=rs[2 * (a * n_slot + k)], recv_sem=rs[2 * (a * n_slot + k) + 1],
            device_id=sibling, device_id_type=MESH) for a in range(na) for k, mask in enumerate((0,) + tuple(masks))]
        for cp in copies:
            cp.start()
        for cp in copies:
            cp.wait_send()
            cp.wait_recv()

    return pl.pallas_call(
        body, name=name, out_shape=tuple(pltpu.HBM(a.shape, a.dtype) for a in lands),
        in_specs=[_HBM] * na, out_specs=(_HBM,) * na, input_output_aliases={i: i for i in range(na)},
        scratch_shapes=[pltpu.SemaphoreType.DMA(())] * (2 * na * n_slot),
        compiler_params=pltpu.CompilerParams(has_side_effects=_DATAFLOW),
    )(*lands)


def _adamw(w, g, m, v):
    m = ADAM_B1 * m + (1.0 - ADAM_B1) * g
    v = ADAM_B2 * v + (1.0 - ADAM_B2) * (g * g)
    m_hat = m / (1.0 - ADAM_B1 ** ADAM_STEP)
    v_hat = v / (1.0 - ADAM_B2 ** ADAM_STEP)
    return -ADAM_LR * (m_hat / (jnp.sqrt(v_hat) + ADAM_EPS) + ADAM_WD * w), m, v


def adam_sum(w, pieces, m, v, name, layer=0, into=None):
    nl, r, c = w.shape
    tc = _pick(c, 256)

    def body(w_ref, p_ref, m_ref, v_ref, *rest):
        g_ref, d_ref, nm_ref, nv_ref = rest[-4:]
        g = p_ref[0].astype(F32)
        for s in range(1, N_DEV):
            g = g + p_ref[s].astype(F32)
        g_ref[0] = g
        d_ref[0], nm_ref[0], nv_ref[0] = _adamw(w_ref[0], g, m_ref[0], v_ref[0])

    row = pl.BlockSpec((1, r, tc), lambda i: (layer, 0, i))
    out = jax.ShapeDtypeStruct((nl, r, c), F32)
    extra = [] if into is None else list(into)
    return pl.pallas_call(
        body, name=name, grid=(c // tc,),
        in_specs=[row, pl.BlockSpec((N_DEV, r, tc), lambda i: (0, 0, i)), row, row]
        + [pl.BlockSpec(memory_space=pl.ANY)] * len(extra),
        out_specs=(row,) * 4, out_shape=(out,) * 4,
        input_output_aliases={4 + i: i for i in range(len(extra))},
        compiler_params=_cp("parallel"))(w, pieces, m, v, *extra)


def adam_sum_linear(w, pieces, m, v, name):
    _, r, c = pieces.shape
    nq = c // 128
    assert w.shape == (r * nq, 128)

    def body(w_ref, p_ref, m_ref, v_ref, g_ref, d_ref, nm_ref, nv_ref):
        for q in range(nq):
            rows = pl.ds(q, r, stride=nq)
            lanes = slice(q * 128, (q + 1) * 128)
            g = p_ref[0, :, lanes].astype(F32)
            for s in range(1, N_DEV):
                g = g + p_ref[s, :, lanes].astype(F32)
            g_ref[rows, :] = g
            d_ref[rows, :], nm_ref[rows, :], nv_ref[rows, :] = _adamw(w_ref[rows, :], g, m_ref[rows, :], v_ref[rows, :])

    return pl.pallas_call(body, name=name, out_shape=(jax.ShapeDtypeStruct(w.shape, F32),) * 4,
                          compiler_params=pltpu.CompilerParams(vmem_limit_bytes=VMEM_LIMIT))(w, pieces, m, v)


def sum_rows(gathered, name):
    _, r, c = gathered.shape

    def body(p_ref, o_ref):
        g = p_ref[0]
        for s in range(1, N_DEV):
            g = g + p_ref[s]
        o_ref[...] = g

    return pl.pallas_call(body, name=name, out_shape=jax.ShapeDtypeStruct((r, c), F32))(gathered)


def adam_small(w, g, m, v, name):
    def body(w_ref, g_ref, m_ref, v_ref, d_ref, nm_ref, nv_ref):
        d_ref[...], nm_ref[...], nv_ref[...] = _adamw(w_ref[...], g_ref[...], m_ref[...], v_ref[...])

    out = jax.ShapeDtypeStruct(w.shape, F32)
    return pl.pallas_call(body, name=name, out_shape=(out,) * 3)(w, g, m, v)


def _rope_tables(t):
    inv_freq = 10000.0 ** (-jnp.arange(0, HEAD_DIM, 2, dtype=F32) / HEAD_DIM)
    lanes = lambda a: jnp.concatenate([a] * (128 // a.shape[-1]), axis=-1)
    base = (jnp.arange(t // QK_TM, dtype=F32) * QK_TM)[:, None] * inv_freq[None, :]
    offs = jnp.arange(QK_TM, dtype=F32)[:, None] * inv_freq[None, :]
    return (lanes(jnp.cos(base))[:, None, :], lanes(jnp.sin(base))[:, None, :], lanes(jnp.cos(offs)), lanes(jnp.sin(offs)))


def _lane_row(vec8):
    return jnp.pad(vec8.reshape(1, DN_HEADS), ((0, 0), (DN_HEADS, 128 - 2 * DN_HEADS)))


def _ffn_bwd(x, norm_g, w_gu, w_d, saved, dy, tag, after=None):
    ft, gu, at = saved
    dgu = ffn_dact(dy, w_d, gu, f"{tag}_d_gate_up", after=after)
    dwd = mm_at(at, dy, f"{tag}_dw_down")
    nj = D_FF // GU_TILE
    dwgu = mm_at(ft, dgu, f"{tag}_dw_gate_up", transposed=True, tn=GU_TILE, row_block=lambda q: (q % 2) * nj + q // 2)
    dx, dg = mm_rms_bwd(dgu, w_gu, x, norm_g, dy, f"{tag}_d_norm", chunks=_GU_CHUNKS)
    return dx, dwgu, dwd, dg


def local_step(x, target, small, weights_of, grads_out, after=None):
    t = x.shape[0]
    rope = _rope_tables(t)
    alog_row, dtb_row = _lane_row(small["odd_a_log"]), _lane_row(small["odd_dt_bias"])

    h0, h0t = rms_fwd(x, small["even_norm"], "even_norm", after=after)
    we = weights_of("even", h0)
    small = {**small, **we.get("small", {})}
    proj0 = mm_nt(h0, we["w_in"], "even_in_proj", after=we.get("after"))
    qr, kr = qk_prep_fwd(proj0, small["even_q_gain"], small["even_k_gain"], rope, "even_qk_prep")
    y_attn, mix0, mix0_t, lse = swa_fwd(qr, kr, proj0, small["even_sinks"], "even_swa")
    mix0, mix0_t = gconv_fwd(proj0, small["even_conv_w"], mix0, mix0_t, "even_gconv")
    we = {**we, **weights_of("even_out", mix0)}
    x1, f0, f0t = mm_nn_res_norm(mix0, we["w_out"], x, small["ffn_norm0"], "even_out_proj")
    w0 = weights_of("ffn0", x1)
    gu0, a0, a0t = ffn_up(f0, w0["gate_up"], "ffn0_gate_up")
    ffn0 = (f0t, gu0, a0t)
    x2, h1, h1t = mm_nn_res_norm(a0, w0["down"], x1, small["odd_norm"], "ffn0_down")

    wo = weights_of("odd", x2)
    proj1 = mm_nt(h1, wo["w_in"], "odd_in_proj")
    qn, kn, vs, bg, conv1 = gdn_prep_fwd(proj1, small["odd_conv_w"], alog_row, dtb_row, "odd_prep")
    o, sall, tall, og, ogt = gdn_fwd(qn, kn, vs, bg, proj1, small["odd_o_gain"], "odd_delta_rule")
    x3, f1, f1t = mm_nn_res_norm(og, wo["w_out"], x2, small["ffn_norm1"], "odd_out_proj")
    w1 = weights_of("ffn1", x3)
    gu1, a1, a1t = ffn_up(f1, w1["gate_up"], "ffn1_gate_up")
    ffn1 = (f1t, gu1, a1t)
    dy, loss_row = mm_nn_res_loss(a1, w1["down"], x3, target, "ffn1_down_loss")

    gs = {}
    dx3, dwgu, dwd, gs["ffn_norm1"] = _ffn_bwd(x3, small["ffn_norm1"], w1["gate_up"], w1["down"], ffn1, dy, "ffn1")
    tok = grads_out("ffn1", {"gate_up": dwgu, "down": dwd})

    do, dz, gs["odd_o_gain"] = gdn_out_bwd(o, proj1, small["odd_o_gain"], dx3, wo["w_out"], "odd_d_gate_norm", after=tok)
    dwo = mm_at(ogt, dx3, "odd_dw_out")
    dqn, dkn, dvs, dbg = gdn_bwd(qn, kn, vs, bg, sall, tall, do, "odd_d_delta_rule")
    dproj1, gs["odd_conv_w"], ddt_row, dal_row = gdn_prep_bwd(
        proj1, conv1, small["odd_conv_w"], alog_row, dtb_row, dqn, dkn, dvs, dbg, dz, "odd_d_prep")
    gs["odd_dt_bias"] = ddt_row[:, DN_HEADS:2 * DN_HEADS]
    gs["odd_a_log"] = dal_row[:, DN_HEADS:2 * DN_HEADS]
    dwi = mm_at(h1t, dproj1, "odd_dw_in", transposed=True)
    dx2, gs["odd_norm"] = mm_rms_bwd(dproj1, wo["w_in"], x2, small["odd_norm"], dx3, "odd_d_norm")
    tok = grads_out("odd", {"w_in": dwi, "w_out": dwo})

    dx1, dwgu, dwd, gs["ffn_norm0"] = _ffn_bwd(x1, small["ffn_norm0"], w0["gate_up"], w0["down"], ffn0, dx2, "ffn0",
                                               after=tok)
    tok = grads_out("ffn0", {"gate_up": dwgu, "down": dwd})

    dmix = mm_nt(dx1, we["w_out"], "even_d_mix", after=tok)
    dwo = mm_at(mix0_t, dx1, "even_dw_out")
    dproj0 = lax.empty((t, EVEN_IN_W), MXU_DTYPE)
    dqr, dkr, dproj0, gs["even_sinks"] = swa_bwd(
        qr, kr, proj0, small["even_sinks"], y_attn, lse, dmix, dproj0, "even_d_swa")
    dproj0, gs["even_q_gain"], gs["even_k_gain"] = qk_prep_bwd(
        proj0, small["even_q_gain"], small["even_k_gain"], rope, dqr, dkr, dproj0, "even_d_qk_prep")
    dproj0, gs["even_conv_w"] = gconv_bwd(proj0, small["even_conv_w"], dmix, dproj0, "even_d_gconv")
    dwi = mm_at(h0t, dproj0, "even_dw_in", transposed=True, chunks=_EVEN_D_CHUNKS)
    tok = grads_out("even", {"w_in": dwi, "w_out": dwo})
    grad_x, gs["even_norm"] = mm_rms_bwd(dproj0, we["w_in"], x, small["even_norm"], dx1, "even_d_norm", after=tok,
                                         chunks=_EVEN_D_CHUNKS)
    return loss_row, grad_x, gs


_SMALL_ORDER = ("even_norm", "even_q_gain", "even_k_gain", "even_sinks", "odd_a_log", "odd_dt_bias", "odd_o_gain",
                "ffn_norm0", "ffn_norm1", "odd_norm", "even_conv_w", "odd_conv_w")
_SMALL_SIZE = {"even_norm": 1024, "even_q_gain": 64, "even_k_gain": 64, "even_sinks": 8, "odd_a_log": 8,
               "odd_dt_bias": 8, "odd_o_gain": 128, "ffn_norm0": 1024, "ffn_norm1": 1024, "odd_norm": 1024,
               "even_conv_w": 3 * 512, "odd_conv_w": 4 * 3072}
_N_REPL = 9


def _pack_rows(vals):
    flat = jnp.concatenate([v.reshape(-1) for v in vals])
    pad = (-flat.shape[0]) % 1024
    return jnp.pad(flat, (0, pad)).reshape(-1, 128)


def _my_block(full, size, axis):
    me = 4 * lax.axis_index("x") + 2 * lax.axis_index("y") + lax.axis_index("c")
    return lax.dynamic_slice_in_dim(full, me * size, size, axis=axis)


def kernel(x, even_norm, even_w_in, even_q_gain, even_k_gain, even_sinks, even_conv_w, even_w_out, odd_norm, odd_w_in, odd_conv_w, odd_a_log, odd_dt_bias, odd_o_gain, odd_w_out, ffn_norm, ffn_w_gate_up, ffn_w_down, loss_target, m_even_norm, m_even_w_in, m_even_q_gain, m_even_k_gain, m_even_sinks, m_even_conv_w, m_even_w_out, m_odd_norm, m_odd_w_in, m_odd_conv_w, m_odd_a_log, m_odd_dt_bias, m_odd_o_gain, m_odd_w_out, m_ffn_norm, m_ffn_w_gate_up, m_ffn_w_down, v_even_norm, v_even_w_in, v_even_q_gain, v_even_k_gain, v_even_sinks, v_even_conv_w, v_even_w_out, v_odd_norm, v_odd_w_in, v_odd_conv_w, v_odd_a_log, v_odd_dt_bias, v_odd_o_gain, v_odd_w_out, v_ffn_norm, v_ffn_w_gate_up, v_ffn_w_down):
    t = x.shape[1]
    d = D_MODEL

    tr = lambda a: jnp.swapaxes(a, 1, 2)
    shard = {
        "even": {"w_in": tr(even_w_in)[0], "w_out": even_w_out[0]},
        "ffn0": {"gate_up": tr(ffn_w_gate_up)[0], "down": ffn_w_down[0]},
        "odd": {"w_in": tr(odd_w_in)[0], "w_out": odd_w_out[0]},
        "ffn1": {"gate_up": tr(ffn_w_gate_up)[1], "down": ffn_w_down[1]},
    }
    given = {
        ("even", "w_in"): ("even_w_in", even_w_in, m_even_w_in, v_even_w_in, 0),
        ("even", "w_out"): ("even_w_out", even_w_out, m_even_w_out, v_even_w_out, 0),
        ("odd", "w_in"): ("odd_w_in", odd_w_in, m_odd_w_in, v_odd_w_in, 0),
        ("odd", "w_out"): ("odd_w_out", odd_w_out, m_odd_w_out, v_odd_w_out, 0),
        ("ffn0", "gate_up"): ("ffn_w_gate_up", ffn_w_gate_up, m_ffn_w_gate_up, v_ffn_w_gate_up, 0),
        ("ffn1", "gate_up"): ("ffn_w_gate_up", ffn_w_gate_up, m_ffn_w_gate_up, v_ffn_w_gate_up, 1),
        ("ffn0", "down"): ("ffn_w_down", ffn_w_down, m_ffn_w_down, v_ffn_w_down, 0),
        ("ffn1", "down"): ("ffn_w_down", ffn_w_down, m_ffn_w_down, v_ffn_w_down, 1),
    }

    def whole(group, parts):
        col, row = tuple(shard[group])
        if group == "odd":
            w_col = join_blocks(parts[0], ODD_IN_PAD, "odd_w_in_join")
        else:
            w_col = parts[0].reshape(-1, d)
        return {col: w_col, row: parts[1].reshape(-1, d)}

    wire = {g: [a.astype(MXU_DTYPE) for a in shard[g].values()] for g in shard}
    wire["even_out"] = [wire["even"].pop()]
    wire["even"].append(_pack_rows([odd_norm, even_conv_w, odd_conv_w]))
    sems, srcs_thru, lands_thru, tok = send_start(wire["even"], "gather_even_start", False, None, masks=_OTHER_CHIPS)
    gathers = {"even": (sems, srcs_thru, lands_thru)}

    def start_rest(after):
        order = ("even_out", "ffn0", "odd", "ffn1")
        sems, srcs_thru, lands_thru, token = send_start([a for g in order for a in wire[g]], "gather_rest_start", False,
                                                        after)
        a0, na = 0, sum(len(wire[g]) for g in order)
        for g in order:
            a1 = a0 + len(wire[g])
            gathers[g] = (sems[2 * N_PEER * a0:2 * N_PEER * a1] + sems[2 * N_PEER * na + a0:2 * N_PEER * na + a1],
                          srcs_thru[a0:a1], lands_thru[a0:a1])
            a0 = a1
        return token

    o1 = d // N_DEV
    o2 = o1 + 3 * CONV_CH // N_DEV

    def weights_of(group, after):
        first = group == "even"
        lands = send_wait(*gathers[group], f"gather_{group}_wait", False, after,
                          masks=_OTHER_CHIPS if first else _ALL_PEERS)
        if first:
            lands = relay_blocks(lands, "gather_even_relay")
        if group == "even_out":
            return {"w_out": lands[0].reshape(-1, d)}
        if group != "even":
            return whole(group, lands)
        sg = lands[1].reshape(N_DEV, -1)
        return {"w_in": lands[0].reshape(-1, d), "after": start_rest(lands[0]), "small": {
            "odd_norm": sg[:, :o1].reshape(1, d),
            "even_conv_w": sg[:, o1:o2].reshape(N_DEV, 3, CONV_CH // N_DEV).transpose(1, 0, 2).reshape(3, CONV_CH),
            "odd_conv_w": sg[:, o2:o2 + 4 * _QKV_W // N_DEV].reshape(N_DEV, 4, _QKV_W // N_DEV)
            .transpose(1, 0, 2).reshape(4, _QKV_W)}}

    sent = {}

    def grads_out(group, dws):
        col, row = tuple(shard[group])
        c = shard[group][col].shape[0]
        pieces = [split_blocks(dws[col], N_DEV, c, "odd_dw_in_split") if group == "odd"
                  else dws[col].reshape((N_DEV,) + shard[group][col].shape),
                  dws[row].reshape((N_DEV,) + shard[group][row].shape)]
        sems, srcs_thru, lands_thru, token = send_start(pieces, f"exchange_{group}_start", True, None)
        sent[group] = (sems, srcs_thru, lands_thru, pieces)
        return token

    small = {
        "even_norm": even_norm, "even_q_gain": even_q_gain, "even_k_gain": even_k_gain, "even_sinks": even_sinks,
        "odd_a_log": odd_a_log.reshape(-1), "odd_dt_bias": odd_dt_bias.reshape(-1), "odd_o_gain": odd_o_gain,
        "ffn_norm0": ffn_norm[0:1], "ffn_norm1": ffn_norm[1:2],
    }

    loss_row, grad_x, gs = local_step(x.reshape(t, d), loss_target.reshape(t, d), small, weights_of, grads_out, after=tok)

    rows = _pack_rows([gs[n] for n in _SMALL_ORDER] + [loss_row[:, 0:1]])
    small_sent = send_start([rows], "gather_small_grads_start", False, None)

    res, behind = {}, small_sent[3]
    for g in ("ffn1", "odd", "ffn0", "even"):
        sems, srcs_thru, lands_thru, pieces = sent[g]
        lands = send_wait(sems, srcs_thru, lands_thru, f"exchange_{g}_wait", True, behind)
        for i, (key, pcs) in enumerate(zip(shard[g], lands)):
            name, w_, m_, v_, layer = given[g, key]
            view = tr if i == 0 else (lambda a: a)
            if pcs.shape[1] % 8:
                lin = lambda a: tr(a).reshape(-1, 128)
                res[name] = tuple(a.reshape((1,) + pcs.shape[1:]) for a in adam_sum_linear(
                    lin(w_), pcs, lin(m_), lin(v_), f"adamw_{g}_{key}"))
                continue
            res[name] = adam_sum(view(w_), pcs, view(m_), view(v_), f"adamw_{g}_{key}", layer=layer, into=res.get(name))
        behind = res[name][0]
    for name in ("even_w_in", "odd_w_in", "ffn_w_gate_up"):
        res[name] = tuple(tr(a) for a in res[name])

    (rows_g,) = send_wait(*small_sent[:3], "gather_small_grads_wait", False, behind)
    tot = sum_rows(rows_g, "sum_small_grads").reshape(-1)
    off, sgrad = 0, {}
    for n in _SMALL_ORDER:
        sgrad[n] = tot[off:off + _SMALL_SIZE[n]]
        off += _SMALL_SIZE[n]
    loss = tot[off]

    repl = _SMALL_ORDER[:_N_REPL]
    repl_w = {"even_norm": even_norm, "even_q_gain": even_q_gain, "even_k_gain": even_k_gain, "even_sinks": even_sinks,
              "odd_a_log": odd_a_log, "odd_dt_bias": odd_dt_bias, "odd_o_gain": odd_o_gain,
              "ffn_norm0": ffn_norm[0], "ffn_norm1": ffn_norm[1]}
    repl_m = {"even_norm": m_even_norm, "even_q_gain": m_even_q_gain, "even_k_gain": m_even_k_gain,
              "even_sinks": m_even_sinks, "odd_a_log": m_odd_a_log, "odd_dt_bias": m_odd_dt_bias,
              "odd_o_gain": m_odd_o_gain, "ffn_norm0": m_ffn_norm[0], "ffn_norm1": m_ffn_norm[1]}
    repl_v = {"even_norm": v_even_norm, "even_q_gain": v_even_q_gain, "even_k_gain": v_even_k_gain,
              "even_sinks": v_even_sinks, "odd_a_log": v_odd_a_log, "odd_dt_bias": v_odd_dt_bias,
              "odd_o_gain": v_odd_o_gain, "ffn_norm0": v_ffn_norm[0], "ffn_norm1": v_ffn_norm[1]}
    pk = lambda dct: _pack_rows([dct[n] for n in repl])
    pd_, pm_, pv_ = adam_small(pk(repl_w), pk(sgrad), pk(repl_m), pk(repl_v), "adamw_replicated")
    sres = {}
    off = 0
    for n in repl:
        sz = _SMALL_SIZE[n]
        sres[n] = (sgrad[n], pd_.reshape(-1)[off:off + sz], pm_.reshape(-1)[off:off + sz], pv_.reshape(-1)[off:off + sz])
        off += sz
    g_on = _my_block(sgrad["odd_norm"].reshape(1, d), d // N_DEV, 1)
    g_ec = _my_block(sgrad["even_conv_w"].reshape(3, CONV_CH), CONV_CH // N_DEV, 1)
    g_oc = _my_block(sgrad["odd_conv_w"].reshape(4, _QKV_W), _QKV_W // N_DEV, 1)
    shard_w = _pack_rows([odd_norm, even_conv_w, odd_conv_w])
    sd_, sm_, sv_ = adam_small(shard_w, _pack_rows([g_on, g_ec, g_oc]),
                               _pack_rows([m_odd_norm, m_even_conv_w, m_odd_conv_w]),
                               _pack_rows([v_odd_norm, v_even_conv_w, v_odd_conv_w]), "adamw_sharded_small")
    off = 0
    for n, gfull, like in (("odd_norm", g_on, odd_norm), ("even_conv_w", g_ec, even_conv_w), ("odd_conv_w", g_oc, odd_conv_w)):
        sz = like.size
        sres[n] = (gfull, sd_.reshape(-1)[off:off + sz], sm_.reshape(-1)[off:off + sz], sv_.reshape(-1)[off:off + sz])
        off += sz

    def small_out(name, like, kind):
        if name == "ffn_norm":
            return jnp.stack([sres["ffn_norm0"][kind], sres["ffn_norm1"][kind]]).reshape(like.shape)
        return sres[name][kind].reshape(like.shape)

    order = (("even_norm", even_norm), ("even_w_in", even_w_in), ("even_q_gain", even_q_gain),
             ("even_k_gain", even_k_gain), ("even_sinks", even_sinks), ("even_conv_w", even_conv_w),
             ("even_w_out", even_w_out), ("odd_norm", odd_norm), ("odd_w_in", odd_w_in), ("odd_conv_w", odd_conv_w),
             ("odd_a_log", odd_a_log), ("odd_dt_bias", odd_dt_bias), ("odd_o_gain", odd_o_gain),
             ("odd_w_out", odd_w_out), ("ffn_norm", ffn_norm), ("ffn_w_gate_up", ffn_w_gate_up),
             ("ffn_w_down", ffn_w_down))
    outs = [loss, grad_x.reshape(x.shape)]
    for kind in range(4):
        for name, like in order:
            outs.append(res[name][kind] if name in res else small_out(name, like, kind))
    return tuple(outs)
```

```python
import jax
import jax.numpy as jnp
import numpy as np
from jax import lax
from jax.experimental import pallas as pl
from jax.experimental.pallas import tpu as pltpu

F32 = jnp.float32
MXU_DTYPE = jnp.bfloat16
HI = lax.Precision.HIGH
EPS = 1e-6
N_DEV = 8
D_MODEL = 1024
HEAD_DIM = 64
ATTN_HEADS = 8
KV_HEADS = 2
ATTN_BLOCK = 128
Q_W = 512
KV_W = 128
CONV_CH = 512
EVEN_IN_W = 2304
DN_HEADS = 8
DN_DIM = 128
DN_W = 1024
DN_CHUNK = 64
ODD_IN_W = 4112
ODD_IN_PAD = 4224
D_FF = 2816
NEG = -1e30
VMEM_LIMIT = 56 * 1024 * 1024
ADAM_LR, ADAM_B1, ADAM_B2, ADAM_EPS, ADAM_WD, ADAM_STEP = 0.001, 0.9, 0.999, 1e-08, 0.01, 10
MESH = pl.DeviceIdType.MESH


def _cp(*sem):
    return pltpu.CompilerParams(dimension_semantics=sem, vmem_limit_bytes=VMEM_LIMIT)


def _pick(n, cap):
    best = 128
    for t in range(128, cap + 1, 128):
        if n % t == 0:
            best = t
    return best


def _mx(a, b):
    return jnp.dot(a.astype(MXU_DTYPE), b.astype(MXU_DTYPE), preferred_element_type=F32)


def _mx_nt(a, b):
    return lax.dot_general(a.astype(MXU_DTYPE), b.astype(MXU_DTYPE), (((1,), (1,)), ((), ())),
                           preferred_element_type=F32)


def _mx_tn(a, b):
    return lax.dot_general(a.astype(MXU_DTYPE), b.astype(MXU_DTYPE), (((0,), (0,)), ((), ())),
                           preferred_element_type=F32)


def _hi(a, b):
    return jnp.dot(a, b, precision=HI, preferred_element_type=F32)


def _hi_nt(a, b):
    return lax.dot_general(a, b, (((1,), (1,)), ((), ())), precision=HI, preferred_element_type=F32)


def _hi_tn(a, b):
    return lax.dot_general(a, b, (((0,), (0,)), ((), ())), precision=HI, preferred_element_type=F32)


def _sigmoid(x):
    return 0.5 * jnp.tanh(0.5 * x) + 0.5


def _softplus(x):
    return jnp.maximum(x, 0.0) + jnp.log(1.0 + jnp.exp(-jnp.abs(x)))


def mm_nn_res_norm(a, b, res, g, name, tm=512):
    t, k = a.shape
    d = b.shape[1]
    tm = min(tm, t)

    def body(a_ref, b_ref, res_ref, g_ref, y_ref, h_ref, ht_ref):
        y = res_ref[...] + _mx(a_ref[...], b_ref[...])
        y_ref[...] = y
        h = y * lax.rsqrt(jnp.mean(y * y, axis=-1, keepdims=True) + EPS) * g_ref[...]
        h_ref[...] = h.astype(h_ref.dtype)
        ht_ref[...] = h.T.astype(ht_ref.dtype)

    row = pl.BlockSpec((tm, d), lambda i: (i, 0))
    return pl.pallas_call(
        body, name=name, grid=(t // tm,),
        in_specs=[pl.BlockSpec((tm, k), lambda i: (i, 0)), pl.BlockSpec((k, d), lambda i: (0, 0)), row,
                  pl.BlockSpec((1, d), lambda i: (0, 0))],
        out_specs=(row, row, pl.BlockSpec((d, tm), lambda i: (0, i))),
        out_shape=(jax.ShapeDtypeStruct((t, d), F32), jax.ShapeDtypeStruct((t, d), MXU_DTYPE),
                   jax.ShapeDtypeStruct((d, t), MXU_DTYPE)),
        compiler_params=_cp("parallel"))(a, b, res, g)


def mm_nt(a, b, name, out_dtype=F32, tm=2048, after=None):
    m, k = a.shape
    n, _ = b.shape
    tn = _pick(n, 512 if k > 3000 else 1536)
    tm = min(tm, m)

    def body(a_ref, b_ref, *rest):
        o_ref = rest[-1]
        o_ref[...] = _mx_nt(a_ref[...], b_ref[...]).astype(o_ref.dtype)

    in_specs = [pl.BlockSpec((tm, k), lambda j, i: (i, 0)), pl.BlockSpec((tn, k), lambda j, i: (j, 0))]
    args = [a, b]
    if after is not None:
        in_specs.append(pl.BlockSpec(memory_space=pl.ANY))
        args.append(after)
    return pl.pallas_call(
        body, name=name, grid=(n // tn, m // tm), in_specs=in_specs,
        out_specs=pl.BlockSpec((tm, tn), lambda j, i: (i, j)),
        out_shape=jax.ShapeDtypeStruct((m, n), out_dtype), compiler_params=_cp("parallel", "parallel"))(*args)


def mm_at(at, b, name, tk=2048, transposed=False, tn=None, row_block=None, chunks=None):
    m, kk = at.shape
    _, n = b.shape
    tm, tn, tk = _pick(m, 1408), tn or _pick(n, 2816), min(tk, kk)
    nk = kk // tk
    assert chunks is None or (transposed and tn == n)

    def body(a_ref, b_ref, o_ref, acc_ref):
        k = pl.program_id(2)
        p = _mx(a_ref[...], b_ref[...])
        acc = jnp.where(k == 0, p, acc_ref[...] + p)
        acc_ref[...] = acc

        @pl.when(k == nk - 1)
        def _():
            res = (acc.T if transposed else acc).astype(o_ref.dtype)
            if chunks is None:
                o_ref[...] = res
            else:
                for cb, co, size in chunks:
                    o_ref[co:co + size, :] = res[cb:cb + size]

    if transposed:
        rb = row_block or (lambda j: j)
        out_spec = pl.BlockSpec((tn, tm), lambda i, j, k: (rb(j), i))
        out_shape = jax.ShapeDtypeStruct((n, m), MXU_DTYPE)
    else:
        out_spec = pl.BlockSpec((tm, tn), lambda i, j, k: (i, j))
        out_shape = jax.ShapeDtypeStruct((m, n), MXU_DTYPE)
    return pl.pallas_call(
        body, name=name, grid=(m // tm, n // tn, nk),
        in_specs=[pl.BlockSpec((tm, tk), lambda i, j, k: (i, k)), pl.BlockSpec((tk, tn), lambda i, j, k: (k, j))],
        out_specs=out_spec, out_shape=out_shape, scratch_shapes=[pltpu.VMEM((tm, tn), F32)],
        compiler_params=_cp("parallel", "parallel", "arbitrary"))(at, b)


def rms_fwd(x, g, name, tm=512, after=None):
    t, d = x.shape

    def body(x_ref, g_ref, *rest):
        o_ref, ot_ref = rest[-2:]
        xv = x_ref[...]
        r = lax.rsqrt(jnp.mean(xv * xv, axis=-1, keepdims=True) + EPS)
        h = xv * r * g_ref[...]
        o_ref[...] = h.astype(o_ref.dtype)
        ot_ref[...] = h.T.astype(ot_ref.dtype)

    in_specs = [pl.BlockSpec((tm, d), lambda i: (i, 0)), pl.BlockSpec((1, d), lambda i: (0, 0))]
    args = [x, g]
    if after is not None:
        in_specs.append(pl.BlockSpec(memory_space=pl.ANY))
        args.append(after)
    return pl.pallas_call(
        body, name=name, grid=(t // tm,), in_specs=in_specs,
        out_specs=(pl.BlockSpec((tm, d), lambda i: (i, 0)), pl.BlockSpec((d, tm), lambda i: (0, i))),
        out_shape=(jax.ShapeDtypeStruct((t, d), MXU_DTYPE), jax.ShapeDtypeStruct((d, t), MXU_DTYPE)),
        compiler_params=_cp("parallel"))(*args)


def mm_rms_bwd(a, bt, x, g, dres, name, tm=512, after=None, chunks=None):
    t, k = a.shape
    d = bt.shape[1]
    tm = min(tm if k > 3000 else 2 * tm, t)
    chunks = chunks or ((0, 0, k),)

    def body(a_ref, b_ref, x_ref, g_ref, dres_ref, *rest):
        dx_ref, dg_ref = rest[-2:]
        dhv = None
        for ca, cb, size in chunks:
            part = _mx(a_ref[:, ca:ca + size], b_ref[cb:cb + size, :])
            dhv = part if dhv is None else dhv + part
        xv = x_ref[...]
        r = lax.rsqrt(jnp.mean(xv * xv, axis=-1, keepdims=True) + EPS)
        xh = xv * r
        dxh = dhv * g_ref[...]
        dx_ref[...] = dres_ref[...] + r * (dxh - xh * jnp.mean(dxh * xh, axis=-1, keepdims=True))
        part = jnp.sum(dhv * xh, axis=0, keepdims=True)
        dg_ref[...] = jnp.where(pl.program_id(0) == 0, part, dg_ref[...] + part)

    row = pl.BlockSpec((tm, d), lambda i: (i, 0))
    one = pl.BlockSpec((1, d), lambda i: (0, 0))
    in_specs = [pl.BlockSpec((tm, k), lambda i: (i, 0)), pl.BlockSpec((k, d), lambda i: (0, 0)), row, one, row]
    args = [a, bt, x, g, dres]
    if after is not None:
        in_specs.append(pl.BlockSpec(memory_space=pl.ANY))
        args.append(after)
    return pl.pallas_call(
        body, name=name, grid=(t // tm,), in_specs=in_specs, out_specs=(row, one),
        out_shape=(jax.ShapeDtypeStruct((t, d), F32), jax.ShapeDtypeStruct((1, d), F32)),
        compiler_params=_cp("arbitrary"))(*args)


GU_TILE = 1408


def ffn_up(f, wt, name, tm=1024):
    t, d = f.shape
    tm = min(tm, t)
    nj = D_FF // GU_TILE

    def body(f_ref, wg_ref, wu_ref, gu_ref, a_ref, at_ref):
        g = _mx_nt(f_ref[...], wg_ref[...])
        u = _mx_nt(f_ref[...], wu_ref[...])
        sg = _sigmoid(g)
        gs = g * sg
        gu_ref[:, :GU_TILE] = (u * (sg + gs - gs * sg)).astype(gu_ref.dtype)
        gu_ref[:, GU_TILE:] = gs.astype(gu_ref.dtype)
        act = gs * u
        a_ref[...] = act.astype(a_ref.dtype)
        at_ref[...] = act.T.astype(at_ref.dtype)

    return pl.pallas_call(
        body, name=name, grid=(nj, t // tm),
        in_specs=[pl.BlockSpec((tm, d), lambda j, i: (i, 0)), pl.BlockSpec((GU_TILE, d), lambda j, i: (j, 0)),
                  pl.BlockSpec((GU_TILE, d), lambda j, i: (nj + j, 0))],
        out_specs=(pl.BlockSpec((tm, 2 * GU_TILE), lambda j, i: (i, j)), pl.BlockSpec((tm, GU_TILE), lambda j, i: (i, j)),
                   pl.BlockSpec((GU_TILE, tm), lambda j, i: (j, i))),
        out_shape=(jax.ShapeDtypeStruct((t, 2 * D_FF), MXU_DTYPE), jax.ShapeDtypeStruct((t, D_FF), MXU_DTYPE),
                   jax.ShapeDtypeStruct((D_FF, t), MXU_DTYPE)),
        compiler_params=_cp("parallel", "parallel"))(f, wt, wt)


_GU_CHUNKS = tuple((q * GU_TILE, ((q % 2) * (D_FF // GU_TILE) + q // 2) * GU_TILE, GU_TILE)
                   for q in range(2 * D_FF // GU_TILE))


def ffn_dact(dy, w_d, gu, name, tm=1024, after=None):
    t, d = dy.shape
    tm = min(tm, t)

    def body(dy_ref, w_ref, gu_ref, *rest):
        o_ref = rest[-1]
        da = _mx_nt(dy_ref[...], w_ref[...])
        o_ref[:, :GU_TILE] = (da * gu_ref[:, :GU_TILE]).astype(o_ref.dtype)
        o_ref[:, GU_TILE:] = (da * gu_ref[:, GU_TILE:]).astype(o_ref.dtype)

    in_specs = [pl.BlockSpec((tm, d), lambda j, i: (i, 0)), pl.BlockSpec((GU_TILE, d), lambda j, i: (j, 0)),
                pl.BlockSpec((tm, 2 * GU_TILE), lambda j, i: (i, j))]
    args = [dy, w_d, gu]
    if after is not None:
        in_specs.append(pl.BlockSpec(memory_space=pl.ANY))
        args.append(after)
    return pl.pallas_call(
        body, name=name, grid=(D_FF // GU_TILE, t // tm), in_specs=in_specs,
        out_specs=pl.BlockSpec((tm, 2 * GU_TILE), lambda j, i: (i, j)),
        out_shape=jax.ShapeDtypeStruct((t, 2 * D_FF), MXU_DTYPE), compiler_params=_cp("parallel", "parallel"))(*args)


def mm_nn_res_loss(a, b, res, target, name, tm=512):
    t, k = a.shape
    d = b.shape[1]
    tm = min(tm, t)

    def body(a_ref, b_ref, res_ref, t_ref, dy_ref, l_ref):
        e = res_ref[...] + _mx(a_ref[...], b_ref[...]) - t_ref[...]
        dy_ref[...] = e * (1.0 / d)
        part = jnp.zeros((1, 128), F32) + 0.5 * jnp.sum(jnp.mean(e * e, axis=-1, keepdims=True), axis=0, keepdims=True)
        l_ref[...] = jnp.where(pl.program_id(0) == 0, part, l_ref[...] + part)

    row = pl.BlockSpec((tm, d), lambda i: (i, 0))
    return pl.pallas_call(
        body, name=name, grid=(t // tm,),
        in_specs=[pl.BlockSpec((tm, k), lambda i: (i, 0)), pl.BlockSpec((k, d), lambda i: (0, 0)), row, row],
        out_specs=(row, pl.BlockSpec((1, 128), lambda i: (0, 0))),
        out_shape=(jax.ShapeDtypeStruct((t, d), F32), jax.ShapeDtypeStruct((1, 128), F32)),
        compiler_params=_cp("arbitrary"))(a, b, res, target)


QK_W = Q_W + KV_W
_QK_TILE = 256


def _qk_mats():
    idx = np.arange(_QK_TILE)
    half = HEAD_DIM // 2
    same = (idx[:, None] // HEAD_DIM) == (idx[None, :] // HEAD_DIM)
    lo = (idx % HEAD_DIM) < half
    rot = np.where((idx[:, None] == idx[None, :] + half) & lo[None, :], -1.0, 0.0)
    rot = rot + np.where((idx[:, None] == idx[None, :] - half) & ~lo[None, :], 1.0, 0.0)
    return jnp.asarray(same, F32), jnp.asarray(rot, F32)


QK_TM = 256


def _qk_gains(q_gain, k_gain):
    return jnp.concatenate([q_gain] * ATTN_HEADS + [k_gain] * KV_HEADS, axis=-1)


def _rope_tile(ca_ref, sa_ref, cb_ref, sb_ref):
    ca, sa, cb, sb = ca_ref[0], sa_ref[0], cb_ref[...], sb_ref[...]
    c, s = ca * cb - sa * sb, sa * cb + ca * sb
    rep = QK_W // 128
    return jnp.concatenate([c] * rep, axis=-1), jnp.concatenate([s] * rep, axis=-1)


_ROPE_SPECS = [pl.BlockSpec((1, 1, 128), lambda i: (i, 0, 0)), pl.BlockSpec((1, 1, 128), lambda i: (i, 0, 0)),
               pl.BlockSpec((QK_TM, 128), lambda i: (0, 0)), pl.BlockSpec((QK_TM, 128), lambda i: (0, 0))]


def _qk_tiles(a, mat, transposed=False):
    outs = []
    for c0 in range(0, QK_W, _QK_TILE):
        w = min(_QK_TILE, QK_W - c0)
        mt = (mat.T if transposed else mat)[:w, :w].astype(MXU_DTYPE)
        at = a[:, c0:c0 + w]
        hi = at.astype(MXU_DTYPE)
        lo = (at - hi.astype(F32)).astype(MXU_DTYPE)
        outs.append(jnp.dot(hi, mt, preferred_element_type=F32) + jnp.dot(lo, mt, preferred_element_type=F32))
    return jnp.concatenate(outs, axis=-1)


def qk_prep_fwd(proj, q_gain, k_gain, rope, name):
    t = proj.shape[0]
    tm = QK_TM
    gmat, rmat = _qk_mats()
    gain = _qk_gains(q_gain, k_gain)

    def body(p_ref, g_ref, ca_ref, sa_ref, cb_ref, sb_ref, gm_ref, rm_ref, q_ref, k_ref):
        x = p_ref[...]
        r = lax.rsqrt(_qk_tiles(x * x, gm_ref[...]) * (1.0 / HEAD_DIM) + EPS)
        xn = x * r * g_ref[...]
        c, s = _rope_tile(ca_ref, sa_ref, cb_ref, sb_ref)
        out = xn * c + _qk_tiles(xn, rm_ref[...]) * s
        q_ref[...] = out[:, :Q_W]
        k_ref[...] = out[:, Q_W:]

    full = pl.BlockSpec((_QK_TILE, _QK_TILE), lambda i: (0, 0))
    return pl.pallas_call(
        body, name=name, grid=(t // tm,),
        in_specs=[pl.BlockSpec((tm, QK_W), lambda i: (i, 0)), pl.BlockSpec((1, QK_W), lambda i: (0, 0))] + _ROPE_SPECS
        + [full, full],
        out_specs=(pl.BlockSpec((tm, Q_W), lambda i: (i, 0)), pl.BlockSpec((tm, KV_W), lambda i: (i, 0))),
        out_shape=(jax.ShapeDtypeStruct((t, Q_W), F32), jax.ShapeDtypeStruct((t, KV_W), F32)),
        compiler_params=_cp("parallel"))(proj, gain, *rope, gmat, rmat)


def qk_prep_bwd(proj, q_gain, k_gain, rope, dq, dk, into, name):
    t = proj.shape[0]
    tm = QK_TM
    gmat, rmat = _qk_mats()
    gain = _qk_gains(q_gain, k_gain)
    lanes = np.arange(QK_W)[:, None]
    fold = jnp.asarray(lanes % HEAD_DIM + np.where(lanes >= Q_W, HEAD_DIM, 0) == np.arange(128)[None, :], F32)

    def body(p_ref, g_ref, ca_ref, sa_ref, cb_ref, sb_ref, gm_ref, rm_ref, f_ref, dq_ref, dk_ref, into_ref,
             o_ref, dg_ref):
        x = p_ref[...]
        r = lax.rsqrt(_qk_tiles(x * x, gm_ref[...]) * (1.0 / HEAD_DIM) + EPS)
        xh = x * r
        c, s = _rope_tile(ca_ref, sa_ref, cb_ref, sb_ref)
        dout = jnp.concatenate([dq_ref[...], dk_ref[...]], axis=-1)
        dxn = dout * c + _qk_tiles(dout * s, rm_ref[...], transposed=True)
        part = _hi(jnp.sum(dxn * xh, axis=0, keepdims=True), f_ref[...])
        dxh = dxn * g_ref[...]
        mean = _qk_tiles(dxh * xh, gm_ref[...]) * (1.0 / HEAD_DIM)
        o_ref[...] = (r * (dxh - xh * mean)).astype(o_ref.dtype)
        dg_ref[...] = jnp.where(pl.program_id(0) == 0, part, dg_ref[...] + part)

    full = pl.BlockSpec((_QK_TILE, _QK_TILE), lambda i: (0, 0))
    dqk, dg = pl.pallas_call(
        body, name=name, grid=(t // tm,),
        in_specs=[pl.BlockSpec((tm, QK_W), lambda i: (i, 0)), pl.BlockSpec((1, QK_W), lambda i: (0, 0))] + _ROPE_SPECS
        + [full, full, pl.BlockSpec((QK_W, 128), lambda i: (0, 0)),
                  pl.BlockSpec((tm, Q_W), lambda i: (i, 0)), pl.BlockSpec((tm, KV_W), lambda i: (i, 0)),
                  pl.BlockSpec(memory_space=pl.ANY)],
        out_specs=(pl.BlockSpec((tm, QK_W), lambda i: (i, 0)), pl.BlockSpec((1, 128), lambda i: (0, 0))),
        out_shape=(jax.ShapeDtypeStruct(into.shape, into.dtype), jax.ShapeDtypeStruct((1, 128), F32)),
        input_output_aliases={len(rope) + 7: 0},
        compiler_params=_cp("arbitrary"))(proj, gain, *rope, gmat, rmat, fold, dq, dk, into)
    return dqk, dg[:, :HEAD_DIM], dg[:, HEAD_DIM:]


def _swa_valid(n, grp):
    qi = lax.broadcasted_iota(jnp.int32, (grp * ATTN_BLOCK, 2 * ATTN_BLOCK), 0) & (ATTN_BLOCK - 1)
    kj = lax.broadcasted_iota(jnp.int32, (grp * ATTN_BLOCK, 2 * ATTN_BLOCK), 1)
    diff = qi + ATTN_BLOCK - kj
    return (diff >= 0) & (diff < ATTN_BLOCK) & (n * ATTN_BLOCK - ATTN_BLOCK + kj >= 0)


def _stack_heads(ref, g, grp, rows=slice(None)):
    return jnp.concatenate([ref[rows, (g * grp + j) * HEAD_DIM:(g * grp + j + 1) * HEAD_DIM] for j in range(grp)], axis=0)


def _stack_sinks(s_ref, g, grp):
    return jnp.concatenate([jnp.zeros((ATTN_BLOCK, 1), F32) + s_ref[0:1, g * grp + j:g * grp + j + 1]
                            for j in range(grp)], axis=0)


SWA_STEP = 2


def swa_fwd(q, k, proj, sinks, name):
    t = q.shape[0]
    nb = t // ATTN_BLOCK
    scale = HEAD_DIM ** -0.5
    grp = ATTN_HEADS // KV_HEADS

    rows = SWA_STEP * ATTN_BLOCK

    def body(q_ref, kc_ref, kp_ref, vc_ref, vp_ref, s_ref, y_ref, mix_ref, yt_ref, lse_ref):
        n0 = pl.program_id(0) * SWA_STEP
        kk = jnp.concatenate([kp_ref[...], kc_ref[...]], axis=0).astype(MXU_DTYPE)
        vv = jnp.concatenate([vp_ref[...], vc_ref[...]], axis=0).astype(MXU_DTYPE)
        lane = lax.broadcasted_iota(jnp.int32, (ATTN_BLOCK, ATTN_HEADS), 1)
        units = [(b, g) for b in range(SWA_STEP) for g in range(KV_HEADS)]
        blk = lambda b: slice(b * ATTN_BLOCK, (b + 1) * ATTN_BLOCK)
        keys = lambda b: slice(b * ATTN_BLOCK, (b + 2) * ATTN_BLOCK)
        col = lambda g: slice(g * HEAD_DIM, (g + 1) * HEAD_DIM)
        valid = [_swa_valid(n0 + b, grp) for b in range(SWA_STEP)]
        qg = [_stack_heads(q_ref, g, grp, blk(b)) for b, g in units]
        sink = [_stack_sinks(s_ref, g, grp) for b, g in units]
        sc = [jnp.where(valid[b], _mx_nt(qg[u], kk[keys(b), col(g)]) * scale, NEG) for u, (b, g) in enumerate(units)]
        m = [jnp.maximum(jnp.max(sc_, axis=-1, keepdims=True), sk) for sc_, sk in zip(sc, sink)]
        e = [jnp.exp(sc_ - m_) for sc_, m_ in zip(sc, m)]
        den = [jnp.sum(e_, axis=-1, keepdims=True) + jnp.exp(sk - m_) for e_, sk, m_ in zip(e, sink, m)]
        og = [_mx(e[u] / den[u], vv[keys(b), col(g)]) for u, (b, g) in enumerate(units)]
        lg = [m_ + jnp.log(d_) for m_, d_ in zip(m, den)]
        for b in range(SWA_STEP):
            lse = jnp.zeros((ATTN_BLOCK, ATTN_HEADS), F32)
            outs = []
            for h in range(ATTN_HEADS):
                u = b * KV_HEADS + h // grp
                sub = blk(h % grp)
                outs.append(og[u][sub])
                lse = jnp.where(lane == h, lg[u][sub], lse)
            y = jnp.concatenate(outs, axis=-1)
            y_ref[blk(b), :] = y
            mix_ref[blk(b), :] = y.astype(mix_ref.dtype)
            yt_ref[:, blk(b)] = y.T.astype(yt_ref.dtype)
            lse_ref[blk(b), :] = lse

    cur = lambda n: (n, 0)
    prev = lambda n: (jnp.maximum(n * SWA_STEP - 1, 0), 0)
    vcol = (Q_W + KV_W) // KV_W
    return pl.pallas_call(
        body, name=name, grid=(nb // SWA_STEP,),
        in_specs=[pl.BlockSpec((rows, Q_W), cur), pl.BlockSpec((rows, KV_W), cur),
                  pl.BlockSpec((ATTN_BLOCK, KV_W), prev),
                  pl.BlockSpec((rows, KV_W), lambda n: (n, vcol)),
                  pl.BlockSpec((ATTN_BLOCK, KV_W), lambda n: (jnp.maximum(n * SWA_STEP - 1, 0), vcol)),
                  pl.BlockSpec((1, ATTN_HEADS), lambda n: (0, 0))],
        out_specs=(pl.BlockSpec((rows, Q_W), cur), pl.BlockSpec((rows, Q_W), cur),
                   pl.BlockSpec((Q_W, rows), lambda n: (0, n)), pl.BlockSpec((rows, ATTN_HEADS), cur)),
        out_shape=(jax.ShapeDtypeStruct((t, Q_W), F32), jax.ShapeDtypeStruct((t, Q_W + CONV_CH), MXU_DTYPE),
                   jax.ShapeDtypeStruct((Q_W + CONV_CH, t), MXU_DTYPE), jax.ShapeDtypeStruct((t, ATTN_HEADS), F32)),
        compiler_params=_cp("parallel"))(q, k, k, proj, proj, sinks)


def swa_bwd(q, k, proj, sinks, y, lse, dmix, into, name):
    t = q.shape[0]
    nb = t // ATTN_BLOCK
    scale = HEAD_DIM ** -0.5
    grp = ATTN_HEADS // KV_HEADS

    def body(q_ref, kc_ref, kp_ref, vc_ref, vp_ref, s_ref, y_ref, lse_ref, dy_ref, into_ref,
             dq_ref, dk_ref, dv_ref, ds_ref, dkc, dvc):
        n = pl.program_id(0)

        @pl.when(n == 0)
        def _():
            dkc[...] = jnp.zeros_like(dkc)
            dvc[...] = jnp.zeros_like(dvc)
            ds_ref[...] = jnp.zeros_like(ds_ref)

        @pl.when(n < nb)
        def _():
            valid = _swa_valid(n, grp)
            kk = jnp.concatenate([kp_ref[...], kc_ref[...]], axis=0).astype(MXU_DTYPE)
            vv = jnp.concatenate([vp_ref[...], vc_ref[...]], axis=0).astype(MXU_DTYPE)
            lane = lax.broadcasted_iota(jnp.int32, (1, ATTN_HEADS), 1)
            gs = range(KV_HEADS)
            kg = [kk[:, g * HEAD_DIM:(g + 1) * HEAD_DIM] for g in gs]
            vg = [vv[:, g * HEAD_DIM:(g + 1) * HEAD_DIM] for g in gs]
            qg = [_stack_heads(q_ref, g, grp).astype(MXU_DTYPE) for g in gs]
            dog = [_stack_heads(dy_ref, g, grp) for g in gs]
            og = [_stack_heads(y_ref, g, grp) for g in gs]
            lg = [jnp.concatenate([lse_ref[:, g * grp + j:g * grp + j + 1] for j in range(grp)], axis=0) for g in gs]
            sink = [_stack_sinks(s_ref, g, grp) for g in gs]
            sc = [jnp.where(valid, _mx_nt(qg[g], kg[g]) * scale, NEG) for g in gs]
            p = [jnp.exp(sc[g] - lg[g]) for g in gs]
            delta = [jnp.sum(dog[g] * og[g], axis=-1, keepdims=True) for g in gs]
            ds = [p[g] * (_mx_nt(dog[g], vg[g]) - delta[g]) for g in gs]
            dqg = [_mx(ds[g], kg[g]) * scale for g in gs]
            dkf = jnp.concatenate([_mx_tn(ds[g], qg[g]) * scale for g in gs], axis=-1)
            dvf = jnp.concatenate([_mx_tn(p[g], dog[g]) for g in gs], axis=-1)
            dsk = [jnp.exp(sink[g] - lg[g]) * delta[g] for g in gs]
            dsink = jnp.zeros((1, ATTN_HEADS), F32)
            dqs = []
            for h in range(ATTN_HEADS):
                rows = slice((h % grp) * ATTN_BLOCK, (h % grp + 1) * ATTN_BLOCK)
                dqs.append(dqg[h // grp][rows])
                dsink = jnp.where(lane == h, -jnp.sum(dsk[h // grp][rows], axis=0, keepdims=True), dsink)
            dq_ref[...] = jnp.concatenate(dqs, axis=-1)
            dk_ref[...] = dkc[...] + dkf[:ATTN_BLOCK]
            dv_ref[...] = (dvc[...] + dvf[:ATTN_BLOCK]).astype(dv_ref.dtype)
            dkc[...] = dkf[ATTN_BLOCK:]
            dvc[...] = dvf[ATTN_BLOCK:]
            ds_ref[...] += dsink

        @pl.when(n == nb)
        def _():
            dk_ref[...] = dkc[...]
            dv_ref[...] = dvc[...].astype(dv_ref.dtype)

    cur = lambda n: (jnp.minimum(n, nb - 1), 0)
    prev = lambda n: (jnp.clip(n - 1, 0, nb - 1), 0)
    vcol = (Q_W + KV_W) // KV_W
    return pl.pallas_call(
        body, name=name, grid=(nb + 1,),
        in_specs=[pl.BlockSpec((ATTN_BLOCK, Q_W), cur), pl.BlockSpec((ATTN_BLOCK, KV_W), cur),
                  pl.BlockSpec((ATTN_BLOCK, KV_W), prev),
                  pl.BlockSpec((ATTN_BLOCK, KV_W), lambda n: (jnp.minimum(n, nb - 1), vcol)),
                  pl.BlockSpec((ATTN_BLOCK, KV_W), lambda n: (jnp.clip(n - 1, 0, nb - 1), vcol)),
                  pl.BlockSpec((1, ATTN_HEADS), lambda n: (0, 0)),
                  pl.BlockSpec((ATTN_BLOCK, Q_W), cur), pl.BlockSpec((ATTN_BLOCK, ATTN_HEADS), cur),
                  pl.BlockSpec((ATTN_BLOCK, Q_W), cur), pl.BlockSpec(memory_space=pl.ANY)],
        out_specs=(pl.BlockSpec((ATTN_BLOCK, Q_W), cur), pl.BlockSpec((ATTN_BLOCK, KV_W), prev),
                   pl.BlockSpec((ATTN_BLOCK, KV_W), lambda n: (jnp.clip(n - 1, 0, nb - 1), vcol)),
                   pl.BlockSpec((1, ATTN_HEADS), lambda n: (0, 0))),
        out_shape=(jax.ShapeDtypeStruct((t, Q_W), F32), jax.ShapeDtypeStruct((t, KV_W), F32),
                   jax.ShapeDtypeStruct(into.shape, into.dtype), jax.ShapeDtypeStruct((1, ATTN_HEADS), F32)),
        scratch_shapes=[pltpu.VMEM((ATTN_BLOCK, KV_W), F32), pltpu.VMEM((ATTN_BLOCK, KV_W), F32)],
        input_output_aliases={9: 2},
        compiler_params=_cp("arbitrary"))(q, k, k, proj, proj, sinks, y, lse, dmix, into)


GC_W = 256
_GB0, _GC0, _XI0 = 768 // GC_W, 1280 // GC_W, 1792 // GC_W
HALO = 8


def gconv_fwd(proj, conv_w, mix, mix_t, name, tm=1024):
    t = proj.shape[0]
    hb = tm // HALO
    half = Q_W // GC_W

    def body(gb_ref, gc_ref, xi_ref, gch_ref, xih_ref, w_ref, mix_in, mixt_in, y_ref, yt_ref):
        i = pl.program_id(1)
        u = gc_ref[...] * xi_ref[...]
        uh = jnp.where(i == 0, 0.0, gch_ref[...] * xih_ref[...])
        up = jnp.concatenate([uh, u], axis=0)
        cv = w_ref[0:1, :] * up[HALO - 2:HALO - 2 + tm]
        cv = cv + w_ref[1:2, :] * up[HALO - 1:HALO - 1 + tm]
        cv = cv + w_ref[2:3, :] * u
        y = gb_ref[...] * cv
        y_ref[...] = y.astype(y_ref.dtype)
        yt_ref[...] = y.T.astype(yt_ref.dtype)

    def col(c0):
        return pl.BlockSpec((tm, GC_W), lambda cj, i: (i, c0 + cj))

    def halo(c0):
        return pl.BlockSpec((HALO, GC_W), lambda cj, i: (jnp.maximum(i * hb - 1, 0), c0 + cj))

    return pl.pallas_call(
        body, name=name, grid=(CONV_CH // GC_W, t // tm),
        in_specs=[col(_GB0), col(_GC0), col(_XI0), halo(_GC0), halo(_XI0),
                  pl.BlockSpec((3, GC_W), lambda cj, i: (0, cj)),
                  pl.BlockSpec(memory_space=pl.ANY), pl.BlockSpec(memory_space=pl.ANY)],
        out_specs=(pl.BlockSpec((tm, GC_W), lambda cj, i: (i, half + cj)),
                   pl.BlockSpec((GC_W, tm), lambda cj, i: (half + cj, i))),
        out_shape=(jax.ShapeDtypeStruct(mix.shape, mix.dtype), jax.ShapeDtypeStruct(mix_t.shape, mix_t.dtype)),
        input_output_aliases={6: 0, 7: 1},
        compiler_params=_cp("parallel", "parallel"))(proj, proj, proj, proj, proj, conv_w, mix, mix_t)


def gconv_bwd(proj, conv_w, dmix, into, name, tm=1024):
    t = proj.shape[0]
    hb = tm // HALO
    nt = t // tm
    dy0 = Q_W // GC_W

    def body(gb_ref, gc_ref, xi_ref, gch_ref, xih_ref, gbn_ref, dyn_ref, dy_ref, w_ref, into_ref, o_ref, dw_ref):
        dgb_ref, dgc_ref, dxi_ref = (o_ref.at[:, j * GC_W:(j + 1) * GC_W] for j in range(3))
        i = pl.program_id(1)
        gc, xi, gb, dy = gc_ref[...], xi_ref[...], gb_ref[...], dy_ref[...]
        u = gc * xi
        uh = jnp.where(i == 0, 0.0, gch_ref[...] * xih_ref[...])
        up = jnp.concatenate([uh, u], axis=0)
        u2 = up[HALO - 2:HALO - 2 + tm]
        u1 = up[HALO - 1:HALO - 1 + tm]
        cv = w_ref[0:1, :] * u2 + w_ref[1:2, :] * u1 + w_ref[2:3, :] * u
        dgb_ref[...] = (dy * cv).astype(dgb_ref.dtype)
        dcv = dy * gb
        dcvn = jnp.where(i == nt - 1, 0.0, dyn_ref[...] * gbn_ref[...])
        dcvp = jnp.concatenate([dcv, dcvn], axis=0)
        du = w_ref[0:1, :] * dcvp[2:2 + tm] + w_ref[1:2, :] * dcvp[1:1 + tm] + w_ref[2:3, :] * dcv
        dgc_ref[...] = (du * xi).astype(dgc_ref.dtype)
        dxi_ref[...] = (du * gc).astype(dxi_ref.dtype)
        dw = jnp.concatenate([jnp.sum(dcv * u2, axis=0, keepdims=True), jnp.sum(dcv * u1, axis=0, keepdims=True),
                              jnp.sum(dcv * u, axis=0, keepdims=True)], axis=0)

        @pl.when(i == 0)
        def _():
            dw_ref[...] = dw

        @pl.when(i > 0)
        def _():
            dw_ref[...] += dw

    def col(c0):
        return pl.BlockSpec((tm, GC_W), lambda cj, i: (i, c0 + cj))

    def halo(c0):
        return pl.BlockSpec((HALO, GC_W), lambda cj, i: (jnp.maximum(i * hb - 1, 0), c0 + cj))

    def nxt(c0):
        return pl.BlockSpec((HALO, GC_W), lambda cj, i: (jnp.minimum((i + 1) * hb, t // HALO - 1), c0 + cj))

    return pl.pallas_call(
        body, name=name, grid=(CONV_CH // GC_W, nt),
        in_specs=[col(_GB0), col(_GC0), col(_XI0), halo(_GC0), halo(_XI0), nxt(_GB0), nxt(dy0), col(dy0),
                  pl.BlockSpec((3, GC_W), lambda cj, i: (0, cj)), pl.BlockSpec(memory_space=pl.ANY)],
        out_specs=(pl.BlockSpec((tm, 3 * GC_W), lambda cj, i: (i, 1 + cj)),
                   pl.BlockSpec((3, GC_W), lambda cj, i: (0, cj))),
        out_shape=(jax.ShapeDtypeStruct(into.shape, into.dtype), jax.ShapeDtypeStruct((3, CONV_CH), F32)),
        input_output_aliases={9: 0},
        compiler_params=_cp("parallel", "arbitrary"))(proj, proj, proj, proj, proj, proj, dmix, dmix, conv_w, into)


_EVEN_D_CHUNKS = ((0, 0, 768),) + tuple(
    (768 + (3 * j + part) * GC_W, 768 + part * CONV_CH + j * GC_W, GC_W)
    for j in range(CONV_CH // GC_W) for part in range(3))


_QKV_W = 3 * DN_W
_BA_COL = (4 * DN_W) // 128
_Z_COL = _QKV_W // DN_W


def gdn_prep_fwd(proj, conv_w, alog_row, dtb_row, name, tm=256):
    t = proj.shape[0]
    hb = tm // HALO
    nt = t // tm
    qscale = DN_DIM ** -0.5
    ring = 3

    def body(x_hbm, xh_ref, w_ref, ba_ref, al_ref, dt_ref, q_ref, k_ref, v_ref, bg_ref, c_ref, xbuf, sem):
        i = pl.program_id(0)

        def fetch(step):
            slot = step % ring
            return pltpu.make_async_copy(x_hbm.at[pl.ds(step * tm, tm), pl.ds(0, _QKV_W)], xbuf.at[slot], sem.at[slot])

        @pl.when(i == 0)
        def _():
            for s in range(ring - 1):
                fetch(s).start()

        @pl.when(i + ring - 1 < nt)
        def _():
            fetch(i + ring - 1).start()

        fetch(i).wait()
        x_ref = xbuf.at[i % ring]
        for gi in range(3 * DN_HEADS):
            sl = slice(gi * DN_DIM, (gi + 1) * DN_DIM)
            xp = jnp.concatenate([jnp.where(i == 0, 0.0, xh_ref[:, sl]), x_ref[:, sl]], axis=0)
            c = w_ref[0:1, sl] * xp[HALO - 3:HALO - 3 + tm]
            for j in range(1, 4):
                c = c + w_ref[j:j + 1, sl] * xp[HALO - 3 + j:HALO - 3 + j + tm]
            c_ref[:, sl] = c
            s = c * _sigmoid(c)
            osl = slice((gi % DN_HEADS) * DN_DIM, (gi % DN_HEADS + 1) * DN_DIM)
            if gi < DN_HEADS:
                q_ref[:, osl] = s * lax.rsqrt(jnp.sum(s * s, axis=-1, keepdims=True) + EPS) * qscale
            elif gi < 2 * DN_HEADS:
                k_ref[:, osl] = s * lax.rsqrt(jnp.sum(s * s, axis=-1, keepdims=True) + EPS)
            else:
                v_ref[:, osl] = s
        ba = ba_ref[...]
        lane = lax.broadcasted_iota(jnp.int32, ba.shape, 1)
        gval = -jnp.exp(al_ref[...]) * _softplus(ba + dt_ref[...])
        bg_ref[...] = jnp.where(lane < DN_HEADS, _sigmoid(ba), jnp.where(lane < 2 * DN_HEADS, gval, 0.0))

    row = pl.BlockSpec((tm, DN_W), lambda i: (i, 0))
    one = pl.BlockSpec((1, 128), lambda i: (0, 0))
    return pl.pallas_call(
        body, name=name, grid=(nt,),
        in_specs=[pl.BlockSpec(memory_space=pl.ANY),
                  pl.BlockSpec((HALO, _QKV_W), lambda i: (jnp.maximum(i * hb - 1, 0), 0)),
                  pl.BlockSpec((4, _QKV_W), lambda i: (0, 0)),
                  pl.BlockSpec((tm, 128), lambda i: (i, _BA_COL)), one, one],
        out_specs=(row, row, row, pl.BlockSpec((tm, 128), lambda i: (i, 0)), pl.BlockSpec((tm, _QKV_W), lambda i: (i, 0))),
        out_shape=(jax.ShapeDtypeStruct((t, DN_W), F32),) * 3 + (jax.ShapeDtypeStruct((t, 128), F32),
                                                                 jax.ShapeDtypeStruct((t, _QKV_W), F32)),
        scratch_shapes=[pltpu.VMEM((ring, tm, _QKV_W), F32), pltpu.SemaphoreType.DMA((ring,))],
        compiler_params=_cp("arbitrary"))(proj, proj, conv_w, proj, alog_row, dtb_row)


def gdn_prep_bwd(proj, conv, conv_w, alog_row, dtb_row, dq, dk, dv, dbg, dz, name, tm=256):
    t = proj.shape[0]
    hb = tm // HALO
    nt = t // tm
    qscale = DN_DIM ** -0.5
    te = tm + HALO

    def body(x_ref, c_ref, cn_ref, w_ref, ba_ref, al_ref, dt_ref, dq_ref, dk_ref, dv_ref,
             dqn_ref, dkn_ref, dvn_ref, dbg_ref, dz_ref, dx_ref, dw_ref, ddt_ref, dal_ref):
        i = pl.program_id(0)
        first = i == 0
        last = i == nt - 1
        dws = []
        for gi in range(3 * DN_HEADS):
            sl = slice(gi * DN_DIM, (gi + 1) * DN_DIM)
            osl = slice((gi % DN_HEADS) * DN_DIM, (gi % DN_HEADS + 1) * DN_DIM)
            c = jnp.concatenate([c_ref[:, sl], cn_ref[:, sl]], axis=0)
            sg = _sigmoid(c)
            s = c * sg
            d_ref, dn_ref = ((dq_ref, dqn_ref), (dk_ref, dkn_ref), (dv_ref, dvn_ref))[gi // DN_HEADS]
            dy = jnp.concatenate([d_ref[:, osl], jnp.where(last, 0.0, dn_ref[:, osl])], axis=0)
            if gi < 2 * DN_HEADS:
                r = lax.rsqrt(jnp.sum(s * s, axis=-1, keepdims=True) + EPS)
                sh = s * r
                ds = r * (dy - sh * jnp.sum(sh * dy, axis=-1, keepdims=True))
                if gi < DN_HEADS:
                    ds = ds * qscale
            else:
                ds = dy
            dc = ds * sg * (1.0 + c * (1.0 - sg))
            dcs = [dc[3 - j:3 - j + tm] for j in range(4)]
            dx = w_ref[0:1, sl] * dcs[0]
            for j in range(1, 4):
                dx = dx + w_ref[j:j + 1, sl] * dcs[j]
            dx_ref[:, sl] = dx.astype(dx_ref.dtype)
            x0 = x_ref[:, sl]
            dws.append(jnp.concatenate([jnp.sum(dcs[j] * x0, axis=0, keepdims=True) for j in range(4)], axis=0))
        dw = jnp.concatenate(dws, axis=-1)
        ba = ba_ref[...]
        dbgv = dbg_ref[...]
        lane = lax.broadcasted_iota(jnp.int32, ba.shape, 1)
        beta = _sigmoid(ba)
        ea = -jnp.exp(al_ref[...])
        zin = ba + dt_ref[...]
        is_b = lane < DN_HEADS
        is_a = (lane >= DN_HEADS) & (lane < 2 * DN_HEADS)
        da = jnp.where(is_a, dbgv * ea * _sigmoid(zin), 0.0)
        dx_ref[:, _QKV_W:_QKV_W + DN_W] = dz_ref[...]
        dx_ref[:, _QKV_W + DN_W:] = jnp.where(is_b, dbgv * beta * (1.0 - beta), da).astype(dx_ref.dtype)
        ddt = jnp.sum(da, axis=0, keepdims=True)
        dal = jnp.sum(jnp.where(is_a, dbgv * ea * _softplus(zin), 0.0), axis=0, keepdims=True)

        @pl.when(first)
        def _():
            dw_ref[...] = dw
            ddt_ref[...] = ddt
            dal_ref[...] = dal

        @pl.when(i > 0)
        def _():
            dw_ref[...] += dw
            ddt_ref[...] += ddt
            dal_ref[...] += dal

    row = pl.BlockSpec((tm, DN_W), lambda i: (i, 0))
    nrow = pl.BlockSpec((HALO, DN_W), lambda i: (jnp.minimum((i + 1) * hb, t // HALO - 1), 0))
    one = pl.BlockSpec((1, 128), lambda i: (0, 0))
    return pl.pallas_call(
        body, name=name, grid=(nt,),
        in_specs=[pl.BlockSpec((tm, _QKV_W), lambda i: (i, 0)), pl.BlockSpec((tm, _QKV_W), lambda i: (i, 0)),
                  pl.BlockSpec((HALO, _QKV_W), lambda i: (jnp.minimum((i + 1) * hb, t // HALO - 1), 0)),
                  pl.BlockSpec((4, _QKV_W), lambda i: (0, 0)),
                  pl.BlockSpec((tm, 128), lambda i: (i, _BA_COL)), one, one,
                  row, row, row, nrow, nrow, nrow, pl.BlockSpec((tm, 128), lambda i: (i, 0)), row],
        out_specs=(pl.BlockSpec((tm, ODD_IN_PAD), lambda i: (i, 0)), pl.BlockSpec((4, _QKV_W), lambda i: (0, 0)), one, one),
        out_shape=(jax.ShapeDtypeStruct((t, ODD_IN_PAD), MXU_DTYPE), jax.ShapeDtypeStruct((4, _QKV_W), F32),
                   jax.ShapeDtypeStruct((1, 128), F32), jax.ShapeDtypeStruct((1, 128), F32)),
        compiler_params=_cp("arbitrary"))(proj, conv, conv, conv_w, proj, alog_row, dtb_row, dq, dk, dv, dq, dk, dv, dbg,
                                          dz)


def _chunk_masks():
    r = lax.broadcasted_iota(jnp.int32, (DN_CHUNK, DN_CHUNK), 0)
    c = lax.broadcasted_iota(jnp.int32, (DN_CHUNK, DN_CHUNK), 1)
    return r >= c, r > c


INV_PACK = 2


def _inv_unit_lower_many(mats):
    n = DN_CHUNK
    wide = INV_PACK * n
    r = lax.broadcasted_iota(jnp.int32, (wide, wide), 0)
    c = lax.broadcasted_iota(jnp.int32, (wide, wide), 1)
    same = (r & -n) == (c & -n)
    eye = jnp.where((r[:n] == (c[:n] & (n - 1))), 1.0, 0.0)

    def blockdiag(row):
        return jnp.where(same, jnp.concatenate([row] * INV_PACK, axis=0), 0.0)

    packs = [jnp.concatenate(mats[g:g + INV_PACK], axis=-1) for g in range(0, len(mats), INV_PACK)]
    xs = [eye - a for a in packs]
    pws = [_hi(a, blockdiag(a)) for a in packs]
    for step in range(5):
        if step < 4:
            both = [_hi(jnp.concatenate([x, pw], axis=0), blockdiag(pw)) for x, pw in zip(xs, pws)]
            xs = [x + b[:n] for x, b in zip(xs, both)]
            pws = [b[n:] for b in both]
        else:
            xs = [x + _hi(x, blockdiag(pw)) for x, pw in zip(xs, pws)]
    return [x[:, j * n:(j + 1) * n] for x in xs for j in range(INV_PACK)]


def _chunk_common(q, k, beta, gc, gcr, lower):
    gam = jnp.exp(jnp.where(lower, gc - gcr, NEG))
    eg = jnp.exp(gc)
    gl = gc[DN_CHUNK - 1:DN_CHUNK, :]
    kdf = jnp.exp(gl - gc)
    kb = k * beta
    bmat = _mx_nt(kb, k)
    qmat = _mx_nt(q, k)
    return gam, eg, jnp.exp(gl), kdf, kb, bmat, qmat


DN_STEP = 4


def gdn_fwd(q, k, v, bg, proj, o_gain, name):
    t = q.shape[0]
    n_chunks = t // DN_CHUNK

    def body(q_ref, k_ref, v_ref, bg_ref, z_ref, g_ref, o_ref, sall_ref, tall_ref, y_ref, yt_ref, s_ref):
        n = pl.program_id(0)

        @pl.when(n == 0)
        def _():
            s_ref[...] = jnp.zeros_like(s_ref)

        lower, strict = _chunk_masks()
        ltri = jnp.where(lower, 1.0, 0.0)
        hs = range(DN_HEADS)
        sl = [slice(h * DN_DIM, (h + 1) * DN_DIM) for h in hs]
        units = [(c, h) for c in range(DN_STEP) for h in hs]
        nu = range(len(units))
        rs = [slice(c * DN_CHUNK, (c + 1) * DN_CHUNK) for c in range(DN_STEP)]
        bgv = [bg_ref[rs[c], :] for c in range(DN_STEP)]
        gcs = [_hi(ltri, b) for b in bgv]
        gcs_t = [g.T for g in gcs]
        qh = [q_ref[rs[c], sl[h]] for c, h in units]
        kh = [k_ref[rs[c], sl[h]] for c, h in units]
        vh = [v_ref[rs[c], sl[h]] for c, h in units]
        beta = [bgv[c][:, h:h + 1] for c, h in units]
        com = [_chunk_common(qh[u], kh[u], beta[u], gcs[c][:, DN_HEADS + h:DN_HEADS + h + 1],
                             gcs_t[c][DN_HEADS + h:DN_HEADS + h + 1, :], lower) for u, (c, h) in enumerate(units)]
        gam, eg, dec, kdf, kb, bmat, qmat = zip(*com)
        tms = _inv_unit_lower_many([jnp.where(strict, bmat[u] * gam[u], 0.0) for u in nu])
        for u, (c, h) in enumerate(units):
            tall_ref[c, h] = tms[u]
        uw = [_hi(tms[u], jnp.concatenate([vh[u] * beta[u], kb[u] * eg[u]], axis=-1)) for u in nu]
        qd = [qh[u] * eg[u] for u in nu]
        pm = [qmat[u] * gam[u] for u in nu]
        kd = [kh[u] * kdf[u] for u in nu]
        st = [s_ref[h] for h in hs]
        for c in range(DN_STEP):
            us = [c * DN_HEADS + h for h in hs]
            for h in hs:
                sall_ref[c, h] = st[h]
            v_new = [uw[us[h]][:, :DN_DIM] - _mx(uw[us[h]][:, DN_DIM:], st[h]) for h in hs]
            o_st = [_mx(qd[us[h]], st[h]) for h in hs]
            o_in = [_mx(pm[us[h]], v_new[h]) for h in hs]
            s_up = [_mx_tn(kd[us[h]], v_new[h]) for h in hs]
            for h in hs:
                ov = o_st[h] + o_in[h]
                o_ref[rs[c], sl[h]] = ov
                zv = z_ref[rs[c], sl[h]]
                y = ov * lax.rsqrt(jnp.mean(ov * ov, axis=-1, keepdims=True) + EPS) * g_ref[...] * (zv * _sigmoid(zv))
                y_ref[rs[c], sl[h]] = y.astype(y_ref.dtype)
                yt_ref[sl[h], rs[c]] = y.T.astype(yt_ref.dtype)
            st = [st[h] * dec[us[h]] + s_up[h] for h in hs]
        for h in hs:
            s_ref[h] = st[h]

    rows = DN_STEP * DN_CHUNK
    row = pl.BlockSpec((rows, DN_W), lambda n: (n, 0))
    return pl.pallas_call(
        body, name=name, grid=(n_chunks // DN_STEP,),
        in_specs=[row, row, row, pl.BlockSpec((rows, 128), lambda n: (n, 0)),
                  pl.BlockSpec((rows, DN_W), lambda n: (n, _Z_COL)), pl.BlockSpec((1, DN_DIM), lambda n: (0, 0))],
        out_specs=(row, pl.BlockSpec((DN_STEP, DN_HEADS, DN_DIM, DN_DIM), lambda n: (n, 0, 0, 0)),
                   pl.BlockSpec((DN_STEP, DN_HEADS, DN_CHUNK, DN_CHUNK), lambda n: (n, 0, 0, 0)),
                   row, pl.BlockSpec((DN_W, rows), lambda n: (0, n))),
        out_shape=(jax.ShapeDtypeStruct((t, DN_W), F32),
                   jax.ShapeDtypeStruct((n_chunks, DN_HEADS, DN_DIM, DN_DIM), F32),
                   jax.ShapeDtypeStruct((n_chunks, DN_HEADS, DN_CHUNK, DN_CHUNK), F32),
                   jax.ShapeDtypeStruct((t, DN_W), MXU_DTYPE), jax.ShapeDtypeStruct((DN_W, t), MXU_DTYPE)),
        scratch_shapes=[pltpu.VMEM((DN_HEADS, DN_DIM, DN_DIM), F32)],
        compiler_params=_cp("arbitrary"))(q, k, v, bg, proj, o_gain)


def gdn_bwd(q, k, v, bg, sall, tall, do, name):
    t = q.shape[0]
    n_chunks = t // DN_CHUNK

    def body(q_ref, k_ref, v_ref, bg_ref, sall_ref, tall_ref, do_ref, dq_ref, dk_ref, dv_ref, dbg_ref, ds_ref):
        n = pl.program_id(0)

        @pl.when(n == 0)
        def _():
            ds_ref[...] = jnp.zeros_like(ds_ref)

        lower, strict = _chunk_masks()
        ltri = jnp.where(lower, 1.0, 0.0)
        bgv = bg_ref[...]
        gcs = _hi(ltri, bgv)
        gcs_t = gcs.T
        lane = lax.broadcasted_iota(jnp.int32, (DN_CHUNK, 128), 1)
        rowi = lax.broadcasted_iota(jnp.int32, (DN_CHUNK, 1), 0)
        hs = range(DN_HEADS)
        each = lambda fn, *ls: [fn(*a) for a in zip(*ls)]
        rsum = lambda a: jnp.sum(a, axis=-1, keepdims=True)
        sl = [slice(h * DN_DIM, (h + 1) * DN_DIM) for h in hs]
        st = [sall_ref[0, h] for h in hs]
        tms = [tall_ref[0, h] for h in hs]
        dsn = [ds_ref[h] for h in hs]
        qh = [q_ref[:, sl[h]] for h in hs]
        kh = [k_ref[:, sl[h]] for h in hs]
        vh = [v_ref[:, sl[h]] for h in hs]
        doh = [do_ref[:, sl[h]] for h in hs]
        beta = [bgv[:, h:h + 1] for h in hs]
        com = [_chunk_common(qh[h], kh[h], beta[h], gcs[:, DN_HEADS + h:DN_HEADS + h + 1],
                             gcs_t[DN_HEADS + h:DN_HEADS + h + 1, :], lower) for h in hs]
        gam, eg, dec, kdf, kb, bmat, qmat = zip(*com)
        rhs_w = each(lambda a, b: a * b, kb, eg)
        uw = each(lambda t_, v_, b_, r_: _hi(t_, jnp.concatenate([v_ * b_, r_], axis=-1)), tms, vh, beta, rhs_w)
        qd = each(lambda a, b: a * b, qh, eg)
        kd = each(lambda a, b: a * b, kh, kdf)
        pmat = each(lambda a, b: a * b, qmat, gam)
        v_new = each(lambda uw_, s_: uw_[:, :DN_DIM] - _mx(uw_[:, DN_DIM:], s_), uw, st)
        dqd = each(_mx_nt, doh, st)
        ds_o = each(_mx_tn, qd, doh)
        dp = each(lambda d_, v_: jnp.where(lower, _mx_nt(d_, v_), 0.0), doh, v_new)
        dvn_o = each(_mx_tn, pmat, doh)
        ddec = each(lambda d_, s_: jnp.sum(rsum(d_ * s_), axis=0, keepdims=True), dsn, st)
        dkd = each(_mx_nt, v_new, dsn)
        dvn = each(lambda a, k_, d_: a + _mx(k_, d_), dvn_o, kd, dsn)
        dw = each(lambda d_, s_: -_mx_nt(d_, s_), dvn, st)
        ds_w = each(lambda uw_, d_: _mx_tn(uw_[:, DN_DIM:], d_), uw, dvn)
        for h in hs:
            ds_ref[h] = ds_o[h] + dec[h] * dsn[h] - ds_w[h]
        dr = each(lambda t_, a, b: _hi_tn(t_, jnp.concatenate([a, b], axis=-1)), tms, dvn, dw)
        da = each(lambda r_, uw_: jnp.where(strict, -_hi_nt(r_, uw_), 0.0), dr, uw)
        dru = [r_[:, :DN_DIM] for r_ in dr]
        drw = [r_[:, DN_DIM:] for r_ in dr]
        db = each(lambda a, b: a * b, da, gam)
        dq_m = each(lambda a, b: a * b, dp, gam)
        e = each(lambda a, bm, p_, qm, g_: (a * bm + p_ * qm) * g_, da, bmat, dp, qmat, gam)
        dkb = each(lambda b_, k_, r_, e_: _mx(b_, k_) + r_ * e_, db, kh, drw, eg)
        dk = each(lambda b_, kb_, m_, q_, d_, f_: _mx_tn(b_, kb_) + _mx_tn(m_, q_) + d_ * f_, db, kb, dq_m, qh, dkd, kdf)
        dq = each(lambda m_, k_, d_, e_: _mx(m_, k_) + d_ * e_, dq_m, kh, dqd, eg)
        tk = each(lambda a, b: rsum(a * b), dkd, kd)
        dbeta_all = jnp.zeros((DN_CHUNK, 128), F32)
        dgc_all = jnp.zeros((DN_CHUNK, 128), F32)
        for h in hs:
            dgc = (jnp.sum(e[h], axis=1, keepdims=True) - jnp.sum(e[h].T, axis=1, keepdims=True)
                   + rsum(dqd[h] * qd[h]) - tk[h] + rsum(drw[h] * rhs_w[h]))
            dgl = jnp.sum(tk[h], axis=0, keepdims=True) + ddec[h] * dec[h]
            dgc = dgc + jnp.where(rowi == DN_CHUNK - 1, dgl, 0.0)
            dbeta = rsum(dru[h] * vh[h]) + rsum(dkb[h] * kh[h])
            dq_ref[:, sl[h]] = dq[h]
            dk_ref[:, sl[h]] = dk[h] + dkb[h] * beta[h]
            dv_ref[:, sl[h]] = dru[h] * beta[h]
            dbeta_all = jnp.where(lane == h, dbeta, dbeta_all)
            dgc_all = jnp.where(lane == DN_HEADS + h, dgc, dgc_all)
        dbg_ref[...] = dbeta_all + _hi_tn(ltri, dgc_all)

    rev = lambda n: (n_chunks - 1 - n, 0)
    row = pl.BlockSpec((DN_CHUNK, DN_W), rev)
    small = pl.BlockSpec((DN_CHUNK, 128), rev)
    return pl.pallas_call(
        body, name=name, grid=(n_chunks,),
        in_specs=[row, row, row, small,
                  pl.BlockSpec((1, DN_HEADS, DN_DIM, DN_DIM), lambda n: (n_chunks - 1 - n, 0, 0, 0)),
                  pl.BlockSpec((1, DN_HEADS, DN_CHUNK, DN_CHUNK), lambda n: (n_chunks - 1 - n, 0, 0, 0)), row],
        out_specs=(row, row, row, small),
        out_shape=(jax.ShapeDtypeStruct((t, DN_W), F32),) * 3 + (jax.ShapeDtypeStruct((t, 128), F32),),
        scratch_shapes=[pltpu.VMEM((DN_HEADS, DN_DIM, DN_DIM), F32)],
        compiler_params=_cp("arbitrary"))(q, k, v, bg, sall, tall, do)


def gdn_out_bwd(o, proj, o_gain, dx, w_out, name, tm=512, after=None):
    t = o.shape[0]
    tm = min(tm, t)

    def body(o_ref, z_ref, g_ref, dx_ref, w_ref, *rest):
        do_ref, dz_ref, dg_ref = rest[-3:]
        i = pl.program_id(0)
        dy = _mx_nt(dx_ref[...], w_ref[...])
        dg = jnp.zeros((1, DN_DIM), F32)
        for h in range(DN_HEADS):
            sl = slice(h * DN_DIM, (h + 1) * DN_DIM)
            ov, zv, dyv = o_ref[:, sl], z_ref[:, sl], dy[:, sl]
            r = lax.rsqrt(jnp.mean(ov * ov, axis=-1, keepdims=True) + EPS)
            oh = ov * r
            sg = _sigmoid(zv)
            dz_ref[:, sl] = (dyv * oh * g_ref[...] * sg * (1.0 + zv * (1.0 - sg))).astype(dz_ref.dtype)
            don = dyv * (zv * sg)
            dg = dg + jnp.sum(don * oh, axis=0, keepdims=True)
            doh = don * g_ref[...]
            do_ref[:, sl] = r * (doh - oh * jnp.mean(doh * oh, axis=-1, keepdims=True))

        @pl.when(i == 0)
        def _():
            dg_ref[...] = dg

        @pl.when(i > 0)
        def _():
            dg_ref[...] += dg

    row = pl.BlockSpec((tm, DN_W), lambda i: (i, 0))
    one = pl.BlockSpec((1, DN_DIM), lambda i: (0, 0))
    in_specs = [row, pl.BlockSpec((tm, DN_W), lambda i: (i, _Z_COL)), one,
                pl.BlockSpec((tm, dx.shape[1]), lambda i: (i, 0)), pl.BlockSpec(w_out.shape, lambda i: (0, 0))]
    args = [o, proj, o_gain, dx, w_out]
    if after is not None:
        in_specs.append(pl.BlockSpec(memory_space=pl.ANY))
        args.append(after)
    return pl.pallas_call(
        body, name=name, grid=(t // tm,), in_specs=in_specs, out_specs=(row, row, one),
        out_shape=(jax.ShapeDtypeStruct((t, DN_W), F32), jax.ShapeDtypeStruct((t, DN_W), MXU_DTYPE),
                   jax.ShapeDtypeStruct((1, DN_DIM), F32)),
        compiler_params=_cp("arbitrary"))(*args)


def _peer(k):
    x, y, c = lax.axis_index("x"), lax.axis_index("y"), lax.axis_index("c")
    px = 1 - x if k & 4 else x
    py = 1 - y if k & 2 else y
    pc = 1 - c if k & 1 else c
    return (px, py, pc), 4 * px + 2 * py + pc


_HBM = pl.BlockSpec(memory_space=pltpu.HBM)
_SEM = pl.BlockSpec(memory_space=pltpu.SEMAPHORE)
_DATAFLOW = pltpu.SideEffectType.DATAFLOW_SIDE_EFFECTING
N_PEER = N_DEV - 1


def join_blocks(parts, rows, name, tc=256):
    n, r, c = parts.shape

    def body(p_ref, o_ref):
        for j in range(n):
            o_ref[j * r:(j + 1) * r, :] = p_ref[j]
        if rows > n * r:
            o_ref[n * r:, :] = jnp.zeros((rows - n * r, tc), o_ref.dtype)

    return pl.pallas_call(
        body, name=name, grid=(c // tc,), in_specs=[pl.BlockSpec((n, r, tc), lambda i: (0, 0, i))],
        out_specs=pl.BlockSpec((rows, tc), lambda i: (0, i)), out_shape=jax.ShapeDtypeStruct((rows, c), parts.dtype),
        compiler_params=_cp("parallel"))(parts)


def split_blocks(whole, n, r, name, tc=256):
    rows, c = whole.shape

    def body(w_ref, o_ref):
        for j in range(n):
            o_ref[j] = w_ref[j * r:(j + 1) * r, :]

    return pl.pallas_call(
        body, name=name, grid=(c // tc,), in_specs=[pl.BlockSpec((rows, tc), lambda i: (0, i))],
        out_specs=pl.BlockSpec((n, r, tc), lambda i: (0, 0, i)), out_shape=jax.ShapeDtypeStruct((n, r, c), whole.dtype),
        compiler_params=_cp("parallel"))(whole)


_ALL_PEERS = tuple(range(1, N_DEV))
_OTHER_CHIPS = (2, 4, 6)
_SIBLING = 1


def send_start(srcs, name, scatter, after, masks=_ALL_PEERS):
    na, n_peer = len(srcs), len(masks)
    ns = (2 * n_peer + 1) * na
    lands = [lax.empty((N_DEV,) + (s.shape[1:] if scatter else s.shape), s.dtype) for s in srcs]
    extra = [] if after is None else [after]

    def body(*refs):
        src_refs, land_refs = refs[:na], refs[na:2 * na]
        sems = refs[2 * na + len(extra):2 * na + len(extra) + ns]
        land_out, token = refs[-1 - na:-1], refs[-1]
        _, me = _peer(0)
        for a in range(na):
            pltpu.make_async_copy(src_refs[a].at[me] if scatter else src_refs[a], land_out[a].at[me],
                                  sems[2 * n_peer * na + a]).start()
        peers = [_peer(k) for k in masks]
        for a in range(na):
            for k, (peer, pid) in enumerate(peers):
                pltpu.make_async_remote_copy(
                    src_ref=src_refs[a].at[pid] if scatter else src_refs[a], dst_ref=land_refs[a].at[me],
                    send_sem=sems[2 * (a * n_peer + k)], recv_sem=sems[2 * (a * n_peer + k) + 1],
                    device_id=peer, device_id_type=MESH).start()
        token[...] = jnp.zeros_like(token)

    hbm = lambda arrs: tuple(pltpu.HBM(a.shape, a.dtype) for a in arrs)
    outs = pl.pallas_call(
        body, name=name,
        out_shape=(pltpu.SemaphoreType.DMA(()),) * ns + hbm(srcs) + hbm(lands) + (jax.ShapeDtypeStruct((8, 128), F32),),
        in_specs=[_HBM] * (2 * na) + [pl.BlockSpec(memory_space=pl.ANY)] * len(extra),
        out_specs=(_SEM,) * ns + (_HBM,) * (2 * na) + (pl.BlockSpec(memory_space=pltpu.VMEM),),
        input_output_aliases={i: ns + i for i in range(2 * na)},
        compiler_params=pltpu.CompilerParams(has_side_effects=_DATAFLOW),
    )(*[pltpu.with_memory_space_constraint(a, pltpu.HBM) for a in list(srcs) + lands], *extra)
    return outs[:ns], outs[ns:ns + na], outs[ns + na:ns + 2 * na], outs[-1]


def send_wait(sems, srcs_thru, lands_thru, name, scatter, after, masks=_ALL_PEERS):
    na, n_peer = len(srcs_thru), len(masks)
    ns = (2 * n_peer + 1) * na

    def body(*refs):
        src_refs, land_refs, sm = refs[:na], refs[na:2 * na], refs[2 * na:2 * na + ns]
        _, me = _peer(0)
        for a in range(na):
            pltpu.make_async_copy(src_refs[a].at[me] if scatter else src_refs[a], land_refs[a].at[me],
                                  sm[2 * n_peer * na + a]).wait()
        for k, mask in enumerate(masks):
            peer, pid = _peer(mask)
            for a in range(na):
                cp = pltpu.make_async_remote_copy(
                    src_ref=src_refs[a].at[pid] if scatter else src_refs[a], dst_ref=land_refs[a].at[pid],
                    send_sem=sm[2 * (a * n_peer + k)], recv_sem=sm[2 * (a * n_peer + k) + 1],
                    device_id=peer, device_id_type=MESH)
                cp.wait_send()
                cp.wait_recv()

    hbm = lambda arrs: tuple(pltpu.HBM(a.shape, a.dtype) for a in arrs)
    outs = pl.pallas_call(
        body, name=name, out_shape=hbm(srcs_thru) + hbm(lands_thru),
        in_specs=[_HBM] * (2 * na) + [_SEM] * ns + [pl.BlockSpec(memory_space=pl.ANY)], out_specs=(_HBM,) * (2 * na),
        input_output_aliases={i: i for i in range(2 * na)},
        compiler_params=pltpu.CompilerParams(has_side_effects=_DATAFLOW),
    )(*srcs_thru, *lands_thru, *sems, after)
    return outs[na:]


def relay_blocks(lands, name, masks=_OTHER_CHIPS):
    na, n_slot = len(lands), 1 + len(masks)

    def body(*refs):
        land_refs, rs = refs[:na], refs[2 * na:]
        sibling, _ = _peer(_SIBLING)
        copies = [pltpu.make_async_remote_copy(
            src_ref=land_refs[a].at[_peer(mask)[1]], dst_ref=land_refs[a].at[_peer(mask)[1]],
            send_sem=rs[2 * (a * n_slot + k)], recv_sem=rs[2 * (a * n_slot + k) + 1],
            device_id=sibling, device_id_type=MESH) for a in range(na) for k, mask in enumerate((0,) + tuple(masks))]
        for cp in copies:
            cp.start()
        for cp in copies:
            cp.wait_send()
            cp.wait_recv()

    return pl.pallas_call(
        body, name=name, out_shape=tuple(pltpu.HBM(a.shape, a.dtype) for a in lands),
        in_specs=[_HBM] * na, out_specs=(_HBM,) * na, input_output_aliases={i: i for i in range(na)},
        scratch_shapes=[pltpu.SemaphoreType.DMA(())] * (2 * na * n_slot),
        compiler_params=pltpu.CompilerParams(has_side_effects=_DATAFLOW),
    )(*lands)


def _adamw(w, g, m, v):
    m = ADAM_B1 * m + (1.0 - ADAM_B1) * g
    v = ADAM_B2 * v + (1.0 - ADAM_B2) * (g * g)
    m_hat = m / (1.0 - ADAM_B1 ** ADAM_STEP)
    v_hat = v / (1.0 - ADAM_B2 ** ADAM_STEP)
    return -ADAM_LR * (m_hat / (jnp.sqrt(v_hat) + ADAM_EPS) + ADAM_WD * w), m, v


def adam_sum(w, pieces, m, v, name, layer=0, into=None):
    nl, r, c = w.shape
    tc = _pick(c, 256)

    def body(w_ref, p_ref, m_ref, v_ref, *rest):
        g_ref, d_ref, nm_ref, nv_ref = rest[-4:]
        g = p_ref[0].astype(F32)
        for s in range(1, N_DEV):
            g = g + p_ref[s].astype(F32)
        g_ref[0] = g
        d_ref[0], nm_ref[0], nv_ref[0] = _adamw(w_ref[0], g, m_ref[0], v_ref[0])

    row = pl.BlockSpec((1, r, tc), lambda i: (layer, 0, i))
    out = jax.ShapeDtypeStruct((nl, r, c), F32)
    extra = [] if into is None else list(into)
    return pl.pallas_call(
        body, name=name, grid=(c // tc,),
        in_specs=[row, pl.BlockSpec((N_DEV, r, tc), lambda i: (0, 0, i)), row, row]
        + [pl.BlockSpec(memory_space=pl.ANY)] * len(extra),
        out_specs=(row,) * 4, out_shape=(out,) * 4,
        input_output_aliases={4 + i: i for i in range(len(extra))},
        compiler_params=_cp("parallel"))(w, pieces, m, v, *extra)


def adam_sum_linear(w, pieces, m, v, name):
    _, r, c = pieces.shape
    nq = c // 128
    assert w.shape == (r * nq, 128)

    def body(w_ref, p_ref, m_ref, v_ref, g_ref, d_ref, nm_ref, nv_ref):
        for q in range(nq):
            rows = pl.ds(q, r, stride=nq)
            lanes = slice(q * 128, (q + 1) * 128)
            g = p_ref[0, :, lanes].astype(F32)
            for s in range(1, N_DEV):
                g = g + p_ref[s, :, lanes].astype(F32)
            g_ref[rows, :] = g
            d_ref[rows, :], nm_ref[rows, :], nv_ref[rows, :] = _adamw(w_ref[rows, :], g, m_ref[rows, :], v_ref[rows, :])

    return pl.pallas_call(body, name=name, out_shape=(jax.ShapeDtypeStruct(w.shape, F32),) * 4,
                          compiler_params=pltpu.CompilerParams(vmem_limit_bytes=VMEM_LIMIT))(w, pieces, m, v)


def sum_rows(gathered, name):
    _, r, c = gathered.shape

    def body(p_ref, o_ref):
        g = p_ref[0]
        for s in range(1, N_DEV):
            g = g + p_ref[s]
        o_ref[...] = g

    return pl.pallas_call(body, name=name, out_shape=jax.ShapeDtypeStruct((r, c), F32))(gathered)


def adam_small(w, g, m, v, name):
    def body(w_ref, g_ref, m_ref, v_ref, d_ref, nm_ref, nv_ref):
        d_ref[...], nm_ref[...], nv_ref[...] = _adamw(w_ref[...], g_ref[...], m_ref[...], v_ref[...])

    out = jax.ShapeDtypeStruct(w.shape, F32)
    return pl.pallas_call(body, name=name, out_shape=(out,) * 3)(w, g, m, v)


def _rope_tables(t):
    inv_freq = 10000.0 ** (-jnp.arange(0, HEAD_DIM, 2, dtype=F32) / HEAD_DIM)
    lanes = lambda a: jnp.concatenate([a] * (128 // a.shape[-1]), axis=-1)
    base = (jnp.arange(t // QK_TM, dtype=F32) * QK_TM)[:, None] * inv_freq[None, :]
    offs = jnp.arange(QK_TM, dtype=F32)[:, None] * inv_freq[None, :]
    return (lanes(jnp.cos(base))[:, None, :], lanes(jnp.sin(base))[:, None, :], lanes(jnp.cos(offs)), lanes(jnp.sin(offs)))


def _lane_row(vec8):
    return jnp.pad(vec8.reshape(1, DN_HEADS), ((0, 0), (DN_HEADS, 128 - 2 * DN_HEADS)))


def _ffn_bwd(x, norm_g, w_gu, w_d, saved, dy, tag, after=None):
    ft, gu, at = saved
    dgu = ffn_dact(dy, w_d, gu, f"{tag}_d_gate_up", after=after)
    dwd = mm_at(at, dy, f"{tag}_dw_down")
    nj = D_FF // GU_TILE
    dwgu = mm_at(ft, dgu, f"{tag}_dw_gate_up", transposed=True, tn=GU_TILE, row_block=lambda q: (q % 2) * nj + q // 2)
    dx, dg = mm_rms_bwd(dgu, w_gu, x, norm_g, dy, f"{tag}_d_norm", chunks=_GU_CHUNKS)
    return dx, dwgu, dwd, dg


def local_step(x, target, small, weights_of, grads_out, after=None):
    t = x.shape[0]
    rope = _rope_tables(t)
    alog_row, dtb_row = _lane_row(small["odd_a_log"]), _lane_row(small["odd_dt_bias"])

    h0, h0t = rms_fwd(x, small["even_norm"], "even_norm", after=after)
    we = weights_of("even", h0)
    small = {**small, **we.get("small", {})}
    proj0 = mm_nt(h0, we["w_in"], "even_in_proj", after=we.get("after"))
    qr, kr = qk_prep_fwd(proj0, small["even_q_gain"], small["even_k_gain"], rope, "even_qk_prep")
    y_attn, mix0, mix0_t, lse = swa_fwd(qr, kr, proj0, small["even_sinks"], "even_swa")
    mix0, mix0_t = gconv_fwd(proj0, small["even_conv_w"], mix0, mix0_t, "even_gconv")
    we = {**we, **weights_of("even_out", mix0)}
    x1, f0, f0t = mm_nn_res_norm(mix0, we["w_out"], x, small["ffn_norm0"], "even_out_proj")
    w0 = weights_of("ffn0", x1)
    gu0, a0, a0t = ffn_up(f0, w0["gate_up"], "ffn0_gate_up")
    ffn0 = (f0t, gu0, a0t)
    x2, h1, h1t = mm_nn_res_norm(a0, w0["down"], x1, small["odd_norm"], "ffn0_down")

    wo = weights_of("odd", x2)
    proj1 = mm_nt(h1, wo["w_in"], "odd_in_proj")
    qn, kn, vs, bg, conv1 = gdn_prep_fwd(proj1, small["odd_conv_w"], alog_row, dtb_row, "odd_prep")
    o, sall, tall, og, ogt = gdn_fwd(qn, kn, vs, bg, proj1, small["odd_o_gain"], "odd_delta_rule")
    x3, f1, f1t = mm_nn_res_norm(og, wo["w_out"], x2, small["ffn_norm1"], "odd_out_proj")
    w1 = weights_of("ffn1", x3)
    gu1, a1, a1t = ffn_up(f1, w1["gate_up"], "ffn1_gate_up")
    ffn1 = (f1t, gu1, a1t)
    dy, loss_row = mm_nn_res_loss(a1, w1["down"], x3, target, "ffn1_down_loss")

    gs = {}
    dx3, dwgu, dwd, gs["ffn_norm1"] = _ffn_bwd(x3, small["ffn_norm1"], w1["gate_up"], w1["down"], ffn1, dy, "ffn1")
    tok = grads_out("ffn1", {"gate_up": dwgu, "down": dwd})

    do, dz, gs["odd_o_gain"] = gdn_out_bwd(o, proj1, small["odd_o_gain"], dx3, wo["w_out"], "odd_d_gate_norm", after=tok)
    dwo = mm_at(ogt, dx3, "odd_dw_out")
    dqn, dkn, dvs, dbg = gdn_bwd(qn, kn, vs, bg, sall, tall, do, "odd_d_delta_rule")
    dproj1, gs["odd_conv_w"], ddt_row, dal_row = gdn_prep_bwd(
        proj1, conv1, small["odd_conv_w"], alog_row, dtb_row, dqn, dkn, dvs, dbg, dz, "odd_d_prep")
    gs["odd_dt_bias"] = ddt_row[:, DN_HEADS:2 * DN_HEADS]
    gs["odd_a_log"] = dal_row[:, DN_HEADS:2 * DN_HEADS]
    dwi = mm_at(h1t, dproj1, "odd_dw_in", transposed=True)
    dx2, gs["odd_norm"] = mm_rms_bwd(dproj1, wo["w_in"], x2, small["odd_norm"], dx3, "odd_d_norm")
    tok = grads_out("odd", {"w_in": dwi, "w_out": dwo})

    dx1, dwgu, dwd, gs["ffn_norm0"] = _ffn_bwd(x1, small["ffn_norm0"], w0["gate_up"], w0["down"], ffn0, dx2, "ffn0",
                                               after=tok)
    tok = grads_out("ffn0", {"gate_up": dwgu, "down": dwd})

    dmix = mm_nt(dx1, we["w_out"], "even_d_mix", after=tok)
    dwo = mm_at(mix0_t, dx1, "even_dw_out")
    dproj0 = lax.empty((t, EVEN_IN_W), MXU_DTYPE)
    dqr, dkr, dproj0, gs["even_sinks"] = swa_bwd(
        qr, kr, proj0, small["even_sinks"], y_attn, lse, dmix, dproj0, "even_d_swa")
    dproj0, gs["even_q_gain"], gs["even_k_gain"] = qk_prep_bwd(
        proj0, small["even_q_gain"], small["even_k_gain"], rope, dqr, dkr, dproj0, "even_d_qk_prep")
    dproj0, gs["even_conv_w"] = gconv_bwd(proj0, small["even_conv_w"], dmix, dproj0, "even_d_gconv")
    dwi = mm_at(h0t, dproj0, "even_dw_in", transposed=True, chunks=_EVEN_D_CHUNKS)
    tok = grads_out("even", {"w_in": dwi, "w_out": dwo})
    grad_x, gs["even_norm"] = mm_rms_bwd(dproj0, we["w_in"], x, small["even_norm"], dx1, "even_d_norm", after=tok,
                                         chunks=_EVEN_D_CHUNKS)
    return loss_row, grad_x, gs


_SMALL_ORDER = ("even_norm", "even_q_gain", "even_k_gain", "even_sinks", "odd_a_log", "odd_dt_bias", "odd_o_gain",
                "ffn_norm0", "ffn_norm1", "odd_norm", "even_conv_w", "odd_conv_w")
_SMALL_SIZE = {"even_norm": 1024, "even_q_gain": 64, "even_k_gain": 64, "even_sinks": 8, "odd_a_log": 8,
               "odd_dt_bias": 8, "odd_o_gain": 128, "ffn_norm0": 1024, "ffn_norm1": 1024, "odd_norm": 1024,
               "even_conv_w": 3 * 512, "odd_conv_w": 4 * 3072}
_N_REPL = 9


def _pack_rows(vals):
    flat = jnp.concatenate([v.reshape(-1) for v in vals])
    pad = (-flat.shape[0]) % 1024
    return jnp.pad(flat, (0, pad)).reshape(-1, 128)


def _my_block(full, size, axis):
    me = 4 * lax.axis_index("x") + 2 * lax.axis_index("y") + lax.axis_index("c")
    return lax.dynamic_slice_in_dim(full, me * size, size, axis=axis)


def kernel(x, even_norm, even_w_in, even_q_gain, even_k_gain, even_sinks, even_conv_w, even_w_out, odd_norm, odd_w_in, odd_conv_w, odd_a_log, odd_dt_bias, odd_o_gain, odd_w_out, ffn_norm, ffn_w_gate_up, ffn_w_down, loss_target, m_even_norm, m_even_w_in, m_even_q_gain, m_even_k_gain, m_even_sinks, m_even_conv_w, m_even_w_out, m_odd_norm, m_odd_w_in, m_odd_conv_w, m_odd_a_log, m_odd_dt_bias, m_odd_o_gain, m_odd_w_out, m_ffn_norm, m_ffn_w_gate_up, m_ffn_w_down, v_even_norm, v_even_w_in, v_even_q_gain, v_even_k_gain, v_even_sinks, v_even_conv_w, v_even_w_out, v_odd_norm, v_odd_w_in, v_odd_conv_w, v_odd_a_log, v_odd_dt_bias, v_odd_o_gain, v_odd_w_out, v_ffn_norm, v_ffn_w_gate_up, v_ffn_w_down):
    t = x.shape[1]
    d = D_MODEL

    tr = lambda a: jnp.swapaxes(a, 1, 2)
    shard = {
        "even": {"w_in": tr(even_w_in)[0], "w_out": even_w_out[0]},
        "ffn0": {"gate_up": tr(ffn_w_gate_up)[0], "down": ffn_w_down[0]},
        "odd": {"w_in": tr(odd_w_in)[0], "w_out": odd_w_out[0]},
        "ffn1": {"gate_up": tr(ffn_w_gate_up)[1], "down": ffn_w_down[1]},
    }
    given = {
        ("even", "w_in"): ("even_w_in", even_w_in, m_even_w_in, v_even_w_in, 0),
        ("even", "w_out"): ("even_w_out", even_w_out, m_even_w_out, v_even_w_out, 0),
        ("odd", "w_in"): ("odd_w_in", odd_w_in, m_odd_w_in, v_odd_w_in, 0),
        ("odd", "w_out"): ("odd_w_out", odd_w_out, m_odd_w_out, v_odd_w_out, 0),
        ("ffn0", "gate_up"): ("ffn_w_gate_up", ffn_w_gate_up, m_ffn_w_gate_up, v_ffn_w_gate_up, 0),
        ("ffn1", "gate_up"): ("ffn_w_gate_up", ffn_w_gate_up, m_ffn_w_gate_up, v_ffn_w_gate_up, 1),
        ("ffn0", "down"): ("ffn_w_down", ffn_w_down, m_ffn_w_down, v_ffn_w_down, 0),
        ("ffn1", "down"): ("ffn_w_down", ffn_w_down, m_ffn_w_down, v_ffn_w_down, 1),
    }

    def whole(group, parts):
        col, row = tuple(shard[group])
        if group == "odd":
            w_col = join_blocks(parts[0], ODD_IN_PAD, "odd_w_in_join")
        else:
            w_col = parts[0].reshape(-1, d)
        return {col: w_col, row: parts[1].reshape(-1, d)}

    wire = {g: [a.astype(MXU_DTYPE) for a in shard[g].values()] for g in shard}
    wire["even_out"] = [wire["even"].pop()]
    wire["even"].append(_pack_rows([odd_norm, even_conv_w, odd_conv_w]))
    sems, srcs_thru, lands_thru, tok = send_start(wire["even"], "gather_even_start", False, None, masks=_OTHER_CHIPS)
    gathers = {"even": (sems, srcs_thru, lands_thru)}

    def start_rest(after):
        order = ("even_out", "ffn0", "odd", "ffn1")
        sems, srcs_thru, lands_thru, token = send_start([a for g in order for a in wire[g]], "gather_rest_start", False,
                                                        after)
        a0, na = 0, sum(len(wire[g]) for g in order)
        for g in order:
            a1 = a0 + len(wire[g])
            gathers[g] = (sems[2 * N_PEER * a0:2 * N_PEER * a1] + sems[2 * N_PEER * na + a0:2 * N_PEER * na + a1],
                          srcs_thru[a0:a1], lands_thru[a0:a1])
            a0 = a1
        return token

    o1 = d // N_DEV
    o2 = o1 + 3 * CONV_CH // N_DEV

    def weights_of(group, after):
        first = group == "even"
        lands = send_wait(*gathers[group], f"gather_{group}_wait", False, after,
                          masks=_OTHER_CHIPS if first else _ALL_PEERS)
        if first:
            lands = relay_blocks(lands, "gather_even_relay")
        if group == "even_out":
            return {"w_out": lands[0].reshape(-1, d)}
        if group != "even":
            return whole(group, lands)
        sg = lands[1].reshape(N_DEV, -1)
        return {"w_in": lands[0].reshape(-1, d), "after": start_rest(lands[0]), "small": {
            "odd_norm": sg[:, :o1].reshape(1, d),
            "even_conv_w": sg[:, o1:o2].reshape(N_DEV, 3, CONV_CH // N_DEV).transpose(1, 0, 2).reshape(3, CONV_CH),
            "odd_conv_w": sg[:, o2:o2 + 4 * _QKV_W // N_DEV].reshape(N_DEV, 4, _QKV_W // N_DEV)
            .transpose(1, 0, 2).reshape(4, _QKV_W)}}

    sent = {}

    def grads_out(group, dws):
        col, row = tuple(shard[group])
        c = shard[group][col].shape[0]
        pieces = [split_blocks(dws[col], N_DEV, c, "odd_dw_in_split") if group == "odd"
                  else dws[col].reshape((N_DEV,) + shard[group][col].shape),
                  dws[row].reshape((N_DEV,) + shard[group][row].shape)]
        sems, srcs_thru, lands_thru, token = send_start(pieces, f"exchange_{group}_start", True, None)
        sent[group] = (sems, srcs_thru, lands_thru, pieces)
        return token

    small = {
        "even_norm": even_norm, "even_q_gain": even_q_gain, "even_k_gain": even_k_gain, "even_sinks": even_sinks,
        "odd_a_log": odd_a_log.reshape(-1), "odd_dt_bias": odd_dt_bias.reshape(-1), "odd_o_gain": odd_o_gain,
        "ffn_norm0": ffn_norm[0:1], "ffn_norm1": ffn_norm[1:2],
    }

    loss_row, grad_x, gs = local_step(x.reshape(t, d), loss_target.reshape(t, d), small, weights_of, grads_out, after=tok)

    rows = _pack_rows([gs[n] for n in _SMALL_ORDER] + [loss_row[:, 0:1]])
    small_sent = send_start([rows], "gather_small_grads_start", False, None)

    res, behind = {}, small_sent[3]
    for g in ("ffn1", "odd", "ffn0", "even"):
        sems, srcs_thru, lands_thru, pieces = sent[g]
        lands = send_wait(sems, srcs_thru, lands_thru, f"exchange_{g}_wait", True, behind)
        for i, (key, pcs) in enumerate(zip(shard[g], lands)):
            name, w_, m_, v_, layer = given[g, key]
            view = tr if i == 0 else (lambda a: a)
            if pcs.shape[1] % 8:
                lin = lambda a: tr(a).reshape(-1, 128)
                res[name] = tuple(a.reshape((1,) + pcs.shape[1:]) for a in adam_sum_linear(
                    lin(w_), pcs, lin(m_), lin(v_), f"adamw_{g}_{key}"))
                continue
            res[name] = adam_sum(view(w_), pcs, view(m_), view(v_), f"adamw_{g}_{key}", layer=layer, into=res.get(name))
        behind = res[name][0]
    for name in ("even_w_in", "odd_w_in", "ffn_w_gate_up"):
        res[name] = tuple(tr(a) for a in res[name])

    (rows_g,) = send_wait(*small_sent[:3], "gather_small_grads_wait", False, behind)
    tot = sum_rows(rows_g, "sum_small_grads").reshape(-1)
    off, sgrad = 0, {}
    for n in _SMALL_ORDER:
        sgrad[n] = tot[off:off + _SMALL_SIZE[n]]
        off += _SMALL_SIZE[n]
    loss = tot[off]

    repl = _SMALL_ORDER[:_N_REPL]
    repl_w = {"even_norm": even_norm, "even_q_gain": even_q_gain, "even_k_gain": even_k_gain, "even_sinks": even_sinks,
              "odd_a_log": odd_a_log, "odd_dt_bias": odd_dt_bias, "odd_o_gain": odd_o_gain,
              "ffn_norm0": ffn_norm[0], "ffn_norm1": ffn_norm[1]}
    repl_m = {"even_norm": m_even_norm, "even_q_gain": m_even_q_gain, "even_k_gain": m_even_k_gain,
              "even_sinks": m_even_sinks, "odd_a_log": m_odd_a_log, "odd_dt_bias": m_odd_dt_bias,
              "odd_o_gain": m_odd_o_gain, "ffn_norm0": m_ffn_norm[0], "ffn_norm1": m_ffn_norm[1]}
    repl_v = {"even_norm": v_even_norm, "even_q_gain": v_even_q_gain, "even_k_gain": v_even_k_gain,
              "even_sinks": v_even_sinks, "odd_a_log": v_odd_a_log, "odd_dt_bias": v_odd_dt_bias,
              "odd_o_gain": v_odd_o_gain, "ffn_norm0": v_ffn_norm[0], "ffn_norm1": v_ffn_norm[1]}
    pk = lambda dct: _pack_rows([dct[n] for n in repl])
    pd_, pm_, pv_ = adam_small(pk(repl_w), pk(sgrad), pk(repl_m), pk(repl_v), "adamw_replicated")
    sres = {}
    off = 0
    for n in repl:
        sz = _SMALL_SIZE[n]
        sres[n] = (sgrad[n], pd_.reshape(-1)[off:off + sz], pm_.reshape(-1)[off:off + sz], pv_.reshape(-1)[off:off + sz])
        off += sz
    g_on = _my_block(sgrad["odd_norm"].reshape(1, d), d // N_DEV, 1)
    g_ec = _my_block(sgrad["even_conv_w"].reshape(3, CONV_CH), CONV_CH // N_DEV, 1)
    g_oc = _my_block(sgrad["odd_conv_w"].reshape(4, _QKV_W), _QKV_W // N_DEV, 1)
    shard_w = _pack_rows([odd_norm, even_conv_w, odd_conv_w])
    sd_, sm_, sv_ = adam_small(shard_w, _pack_rows([g_on, g_ec, g_oc]),
                               _pack_rows([m_odd_norm, m_even_conv_w, m_odd_conv_w]),
                               _pack_rows([v_odd_norm, v_even_conv_w, v_odd_conv_w]), "adamw_sharded_small")
    off = 0
    for n, gfull, like in (("odd_norm", g_on, odd_norm), ("even_conv_w", g_ec, even_conv_w), ("odd_conv_w", g_oc, odd_conv_w)):
        sz = like.size
        sres[n] = (gfull, sd_.reshape(-1)[off:off + sz], sm_.reshape(-1)[off:off + sz], sv_.reshape(-1)[off:off + sz])
        off += sz

    def small_out(name, like, kind):
        if name == "ffn_norm":
            return jnp.stack([sres["ffn_norm0"][kind], sres["ffn_norm1"][kind]]).reshape(like.shape)
        return sres[name][kind].reshape(like.shape)

    order = (("even_norm", even_norm), ("even_w_in", even_w_in), ("even_q_gain", even_q_gain),
             ("even_k_gain", even_k_gain), ("even_sinks", even_sinks), ("even_conv_w", even_conv_w),
             ("even_w_out", even_w_out), ("odd_norm", odd_norm), ("odd_w_in", odd_w_in), ("odd_conv_w", odd_conv_w),
             ("odd_a_log", odd_a_log), ("odd_dt_bias", odd_dt_bias), ("odd_o_gain", odd_o_gain),
             ("odd_w_out", odd_w_out), ("ffn_norm", ffn_norm), ("ffn_w_gate_up", ffn_w_gate_up),
             ("ffn_w_down", ffn_w_down))
    outs = [loss, grad_x.reshape(x.shape)]
    for kind in range(4):
        for name, like in order:
            outs.append(res[name][kind] if name in res else small_out(name, like, kind))
    return tuple(outs)
```
